```python
import math
import jax, jax.numpy as jnp
from jax import lax
import numpy as np

D_MODEL = 1024
BATCH = 8
SEQ = 4096
DEPTH = 2

N_META = 16
BLOCK = 128
PAD_FRONT = BLOCK - N_META
HEAD_DIM = 64
FOX_HEADS = 8
SWA_Q_HEADS = 8
SWA_KV_HEADS = 2
WINDOW = 128
N_BUCKETS = 32
MAX_DISTANCE = 128
D_FF = 2816
EPS = 1e-6
NEG = -1e30
FORGET_BIAS_INIT = 2.0
FOX_W = FOX_HEADS * HEAD_DIM
SWA_QW = SWA_Q_HEADS * HEAD_DIM
SWA_KVW = SWA_KV_HEADS * HEAD_DIM
D_IN = 3 * FOX_W + FOX_HEADS + SWA_QW + 2 * SWA_KVW + 2 * D_MODEL

kernel_name = "hybrid_fox_swa_sink_macaron_meta"


def rms_norm(x, g):
    xf = x.astype(jnp.float32)
    ms = jnp.mean(xf * xf, axis=-1, keepdims=True)
    return (xf * lax.rsqrt(ms + EPS) * g.astype(jnp.float32)).astype(x.dtype)


def swiglu(h, w_in, w_out):
    gu = h @ w_in
    gate, up = jnp.split(gu, 2, axis=-1)
    return (jax.nn.silu(gate) * up) @ w_out


def t5_bucket(dist):
    n = jnp.maximum(dist, 0)
    max_exact = N_BUCKETS // 2
    nf = jnp.maximum(n, 1).astype(jnp.float32)
    large = max_exact + (jnp.log(nf / max_exact) / math.log(MAX_DISTANCE / max_exact)
                         * (N_BUCKETS - max_exact)).astype(jnp.int32)
    large = jnp.minimum(large, N_BUCKETS - 1)
    return jnp.where(n < max_exact, n, large)


def fox_attention(q, k, v, log_f):
    b, p, h, dh = q.shape
    nb = p // BLOCK
    c_k = jnp.cumsum(log_f, axis=1).transpose(0, 2, 1)
    k_pos = jnp.arange(p)
    k_valid = k_pos >= PAD_FRONT
    qb = q.reshape(b, nb, BLOCK, h, dh).transpose(1, 0, 2, 3, 4)
    cqb = c_k.reshape(b, h, nb, BLOCK).transpose(2, 0, 1, 3)
    scale = dh ** -0.5

    def one_block(args):
        qi, cqi, i = args
        q_pos = i * BLOCK + jnp.arange(BLOCK)
        s = jnp.einsum('bqhd,bkhd->bhqk', qi, k, preferred_element_type=jnp.float32) * scale
        s = s + cqi[..., :, None] - c_k[:, :, None, :]
        mask = (k_pos[None, :] <= q_pos[:, None]) & k_valid[None, :]
        s = jnp.where(mask[None, None], s, NEG)
        pr = jax.nn.softmax(s, axis=-1)
        return jnp.einsum('bhqk,bkhd->bqhd', pr.astype(v.dtype), v)

    out = lax.map(one_block, (qb, cqb, jnp.arange(nb)))
    return out.transpose(1, 0, 2, 3, 4).reshape(b, p, h * dh)


def swa_attention(q, k, v, sinks, rel_bias_table):
    b, p, hq, dh = q.shape
    kv = k.shape[2]
    r = hq // kv
    nb = p // BLOCK
    scale = dh ** -0.5
    qb = q.reshape(b, nb, BLOCK, kv, r, dh)

    def band(a):
        a_ext = jnp.concatenate([jnp.zeros_like(a[:, :BLOCK]), a], axis=1)
        a_ext = a_ext.reshape(b, nb + 1, BLOCK, kv, dh)
        return jnp.concatenate([a_ext[:, :-1], a_ext[:, 1:]], axis=2)

    kb, vb = band(k), band(v)
    km, vm = k[:, PAD_FRONT:BLOCK], v[:, PAD_FRONT:BLOCK]
    q_pos = jnp.arange(nb)[:, None] * BLOCK + jnp.arange(BLOCK)[None, :]
    band_pos = (jnp.arange(nb)[:, None] - 1) * BLOCK + jnp.arange(2 * BLOCK)[None, :]
    meta_pos = PAD_FRONT + jnp.arange(N_META)
    d_band = q_pos[:, :, None] - band_pos[:, None, :]
    m_band = (d_band >= 0) & (d_band < WINDOW) & (band_pos[:, None, :] >= PAD_FRONT)
    d_meta = q_pos[:, :, None] - meta_pos[None, None, :]
    m_meta = d_meta >= WINDOW
    table = rel_bias_table.astype(jnp.float32)
    bias_band = table[t5_bucket(d_band)].transpose(0, 3, 1, 2).reshape(nb, kv, r, BLOCK, 2 * BLOCK)
    bias_meta = table[t5_bucket(d_meta)].transpose(0, 3, 1, 2).reshape(nb, kv, r, BLOCK, N_META)

    s_band = jnp.einsum('bnqgrd,bnkgd->bngrqk', qb, kb, preferred_element_type=jnp.float32) * scale + bias_band[None]
    s_band = jnp.where(m_band[None, :, None, None], s_band, NEG)
    s_meta = jnp.einsum('bnqgrd,bmgd->bngrqm', qb, km, preferred_element_type=jnp.float32) * scale + bias_meta[None]
    s_meta = jnp.where(m_meta[None, :, None, None], s_meta, NEG)
    s = jnp.concatenate([s_band, s_meta], axis=-1)
    sink = sinks.astype(jnp.float32).reshape(kv, r)[None, None, :, :, None, None]
    mx = jnp.maximum(jnp.max(s, axis=-1, keepdims=True), sink)
    pr = jnp.exp(s - mx)
    pr = pr / (jnp.sum(pr, axis=-1, keepdims=True) + jnp.exp(sink - mx))
    pr = pr.astype(v.dtype)
    out = (jnp.einsum('bngrqk,bnkgd->bnqgrd', pr[..., :2 * BLOCK], vb)
           + jnp.einsum('bngrqm,bmgd->bnqgrd', pr[..., 2 * BLOCK:], vm))
    return out.reshape(b, p, hq * dh)


def token_mixer(h, rel_bias_table, w_in, forget_bias, fox_q_norm, fox_k_norm,
                swa_q_norm, swa_k_norm, swa_sinks, w_branch_fox, w_branch_swa, w_out):
    b, p, _ = h.shape
    proj = h @ w_in
    widths = [FOX_W, FOX_W, FOX_W, FOX_HEADS, SWA_QW, SWA_KVW, SWA_KVW, D_MODEL]
    idx = list(np.cumsum(widths))
    qa, ka, va, fa, qb, kb, vb, ga, gb = jnp.split(proj, idx, axis=-1)
    qa = rms_norm(qa.reshape(b, p, FOX_HEADS, HEAD_DIM), fox_q_norm)
    ka = rms_norm(ka.reshape(b, p, FOX_HEADS, HEAD_DIM), fox_k_norm)
    va = va.reshape(b, p, FOX_HEADS, HEAD_DIM)
    log_f = jax.nn.log_sigmoid(fa.astype(jnp.float32) + forget_bias.astype(jnp.float32))
    qb = rms_norm(qb.reshape(b, p, SWA_Q_HEADS, HEAD_DIM), swa_q_norm)
    kb = rms_norm(kb.reshape(b, p, SWA_KV_HEADS, HEAD_DIM), swa_k_norm)
    vb = vb.reshape(b, p, SWA_KV_HEADS, HEAD_DIM)
    o_fox = fox_attention(qa, ka, va, log_f)
    o_swa = swa_attention(qb, kb, vb, swa_sinks, rel_bias_table)
    y = jax.nn.sigmoid(ga) * (o_fox @ w_branch_fox) + jax.nn.sigmoid(gb) * (o_swa @ w_branch_swa)
    return y @ w_out


def _fwd_setup_inputs(seed: int = 0) -> dict:
    key = jax.random.key(seed)
    ks = jax.random.split(key, 24)
    f32 = jnp.float32

    def nrm(k, shape, scale):
        return jax.random.normal(k, shape, f32) * scale

    def gain(k, shape):
        return 1.0 + 0.1 * jax.random.normal(k, shape, f32)

    return {
        "x": nrm(ks[0], (BATCH, SEQ, D_MODEL), 1.0),
        "meta_tokens": nrm(ks[1], (N_META, D_MODEL), 1.0),
        "rel_bias_table": nrm(ks[2], (N_BUCKETS, SWA_Q_HEADS), 0.5),
        "ffn1_norm": gain(ks[3], (DEPTH, D_MODEL)),
        "ffn1_w_in": nrm(ks[4], (DEPTH, D_MODEL, 2 * D_FF), D_MODEL ** -0.5),
        "ffn1_w_out": nrm(ks[5], (DEPTH, D_FF, D_MODEL), D_FF ** -0.5),
        "mix_norm": gain(ks[6], (DEPTH, D_MODEL)),
        "w_in": nrm(ks[7], (DEPTH, D_MODEL, D_IN), D_MODEL ** -0.5),
        "forget_bias": FORGET_BIAS_INIT + 0.1 * jax.random.normal(ks[8], (DEPTH, FOX_HEADS), f32),
        "fox_q_norm": gain(ks[9], (DEPTH, HEAD_DIM)),
        "fox_k_norm": gain(ks[10], (DEPTH, HEAD_DIM)),
        "swa_q_norm": gain(ks[11], (DEPTH, HEAD_DIM)),
        "swa_k_norm": gain(ks[12], (DEPTH, HEAD_DIM)),
        "swa_sinks": nrm(ks[13], (DEPTH, SWA_Q_HEADS), 0.5),
        "w_branch_fox": nrm(ks[14], (DEPTH, FOX_W, D_MODEL), FOX_W ** -0.5),
        "w_branch_swa": nrm(ks[15], (DEPTH, SWA_QW, D_MODEL), SWA_QW ** -0.5),
        "w_out": nrm(ks[16], (DEPTH, D_MODEL, D_MODEL), D_MODEL ** -0.5),
        "ffn2_norm": gain(ks[17], (DEPTH, D_MODEL)),
        "ffn2_w_in": nrm(ks[18], (DEPTH, D_MODEL, 2 * D_FF), D_MODEL ** -0.5),
        "ffn2_w_out": nrm(ks[19], (DEPTH, D_FF, D_MODEL), D_FF ** -0.5),
    }


def _fwd_reference(x, meta_tokens, rel_bias_table, ffn1_norm, ffn1_w_in, ffn1_w_out, mix_norm,
              w_in, forget_bias, fox_q_norm, fox_k_norm, swa_q_norm, swa_k_norm, swa_sinks,
              w_branch_fox, w_branch_swa, w_out, ffn2_norm, ffn2_w_in, ffn2_w_out):
    b = x.shape[0]
    pad = jnp.zeros((b, PAD_FRONT, D_MODEL), x.dtype)
    meta = jnp.broadcast_to(meta_tokens.astype(x.dtype)[None], (b, N_META, D_MODEL))
    h = jnp.concatenate([pad, meta, x], axis=1)
    for l in range(DEPTH):
        h = h + 0.5 * swiglu(rms_norm(h, ffn1_norm[l]), ffn1_w_in[l], ffn1_w_out[l])
        h = h + token_mixer(rms_norm(h, mix_norm[l]), rel_bias_table, w_in[l], forget_bias[l],
                            fox_q_norm[l], fox_k_norm[l], swa_q_norm[l], swa_k_norm[l],
                            swa_sinks[l], w_branch_fox[l], w_branch_swa[l], w_out[l])
        h = h + 0.5 * swiglu(rms_norm(h, ffn2_norm[l]), ffn2_w_in[l], ffn2_w_out[l])
    return h[:, BLOCK:]


import jax as _jax
import jax.numpy as _jnp

TWIN_FORMAT = 'train_step'
FWD_PARAMS = ['x', 'meta_tokens', 'rel_bias_table', 'ffn1_norm', 'ffn1_w_in', 'ffn1_w_out', 'mix_norm', 'w_in', 'forget_bias', 'fox_q_norm', 'fox_k_norm', 'swa_q_norm', 'swa_k_norm', 'swa_sinks', 'w_branch_fox', 'w_branch_swa', 'w_out', 'ffn2_norm', 'ffn2_w_in', 'ffn2_w_out']
TWIN_WEIGHTS = ['meta_tokens', 'rel_bias_table', 'ffn1_norm', 'ffn1_w_in', 'ffn1_w_out', 'mix_norm', 'w_in', 'forget_bias', 'fox_q_norm', 'fox_k_norm', 'swa_q_norm', 'swa_k_norm', 'swa_sinks', 'w_branch_fox', 'w_branch_swa', 'w_out', 'ffn2_norm', 'ffn2_w_in', 'ffn2_w_out']
TWIN_DIFF_INPUT = 'x'
TWIN_INPUTS = ['x', 'meta_tokens', 'rel_bias_table', 'ffn1_norm', 'ffn1_w_in', 'ffn1_w_out', 'mix_norm', 'w_in', 'forget_bias', 'fox_q_norm', 'fox_k_norm', 'swa_q_norm', 'swa_k_norm', 'swa_sinks', 'w_branch_fox', 'w_branch_swa', 'w_out', 'ffn2_norm', 'ffn2_w_in', 'ffn2_w_out', 'loss_target', 'm_meta_tokens', 'm_rel_bias_table', 'm_ffn1_norm', 'm_ffn1_w_in', 'm_ffn1_w_out', 'm_mix_norm', 'm_w_in', 'm_forget_bias', 'm_fox_q_norm', 'm_fox_k_norm', 'm_swa_q_norm', 'm_swa_k_norm', 'm_swa_sinks', 'm_w_branch_fox', 'm_w_branch_swa', 'm_w_out', 'm_ffn2_norm', 'm_ffn2_w_in', 'm_ffn2_w_out', 'v_meta_tokens', 'v_rel_bias_table', 'v_ffn1_norm', 'v_ffn1_w_in', 'v_ffn1_w_out', 'v_mix_norm', 'v_w_in', 'v_forget_bias', 'v_fox_q_norm', 'v_fox_k_norm', 'v_swa_q_norm', 'v_swa_k_norm', 'v_swa_sinks', 'v_w_branch_fox', 'v_w_branch_swa', 'v_w_out', 'v_ffn2_norm', 'v_ffn2_w_in', 'v_ffn2_w_out']
TWIN_OUTPUTS = ['loss', 'grad_x', 'grad_meta_tokens', 'grad_rel_bias_table', 'grad_ffn1_norm', 'grad_ffn1_w_in', 'grad_ffn1_w_out', 'grad_mix_norm', 'grad_w_in', 'grad_forget_bias', 'grad_fox_q_norm', 'grad_fox_k_norm', 'grad_swa_q_norm', 'grad_swa_k_norm', 'grad_swa_sinks', 'grad_w_branch_fox', 'grad_w_branch_swa', 'grad_w_out', 'grad_ffn2_norm', 'grad_ffn2_w_in', 'grad_ffn2_w_out', 'delta_meta_tokens', 'delta_rel_bias_table', 'delta_ffn1_norm', 'delta_ffn1_w_in', 'delta_ffn1_w_out', 'delta_mix_norm', 'delta_w_in', 'delta_forget_bias', 'delta_fox_q_norm', 'delta_fox_k_norm', 'delta_swa_q_norm', 'delta_swa_k_norm', 'delta_swa_sinks', 'delta_w_branch_fox', 'delta_w_branch_swa', 'delta_w_out', 'delta_ffn2_norm', 'delta_ffn2_w_in', 'delta_ffn2_w_out', 'new_m_meta_tokens', 'new_m_rel_bias_table', 'new_m_ffn1_norm', 'new_m_ffn1_w_in', 'new_m_ffn1_w_out', 'new_m_mix_norm', 'new_m_w_in', 'new_m_forget_bias', 'new_m_fox_q_norm', 'new_m_fox_k_norm', 'new_m_swa_q_norm', 'new_m_swa_k_norm', 'new_m_swa_sinks', 'new_m_w_branch_fox', 'new_m_w_branch_swa', 'new_m_w_out', 'new_m_ffn2_norm', 'new_m_ffn2_w_in', 'new_m_ffn2_w_out', 'new_v_meta_tokens', 'new_v_rel_bias_table', 'new_v_ffn1_norm', 'new_v_ffn1_w_in', 'new_v_ffn1_w_out', 'new_v_mix_norm', 'new_v_w_in', 'new_v_forget_bias', 'new_v_fox_q_norm', 'new_v_fox_k_norm', 'new_v_swa_q_norm', 'new_v_swa_k_norm', 'new_v_swa_sinks', 'new_v_w_branch_fox', 'new_v_w_branch_swa', 'new_v_w_out', 'new_v_ffn2_norm', 'new_v_ffn2_w_in', 'new_v_ffn2_w_out']
TWIN_LEAF_KINDS = {'loss': 'loss', 'grad_x': 'grad_x', 'grad_meta_tokens': 'grad_w', 'grad_rel_bias_table': 'grad_w', 'grad_ffn1_norm': 'grad_w', 'grad_ffn1_w_in': 'grad_w', 'grad_ffn1_w_out': 'grad_w', 'grad_mix_norm': 'grad_w', 'grad_w_in': 'grad_w', 'grad_forget_bias': 'grad_w', 'grad_fox_q_norm': 'grad_w', 'grad_fox_k_norm': 'grad_w', 'grad_swa_q_norm': 'grad_w', 'grad_swa_k_norm': 'grad_w', 'grad_swa_sinks': 'grad_w', 'grad_w_branch_fox': 'grad_w', 'grad_w_branch_swa': 'grad_w', 'grad_w_out': 'grad_w', 'grad_ffn2_norm': 'grad_w', 'grad_ffn2_w_in': 'grad_w', 'grad_ffn2_w_out': 'grad_w', 'delta_meta_tokens': 'delta_w', 'delta_rel_bias_table': 'delta_w', 'delta_ffn1_norm': 'delta_w', 'delta_ffn1_w_in': 'delta_w', 'delta_ffn1_w_out': 'delta_w', 'delta_mix_norm': 'delta_w', 'delta_w_in': 'delta_w', 'delta_forget_bias': 'delta_w', 'delta_fox_q_norm': 'delta_w', 'delta_fox_k_norm': 'delta_w', 'delta_swa_q_norm': 'delta_w', 'delta_swa_k_norm': 'delta_w', 'delta_swa_sinks': 'delta_w', 'delta_w_branch_fox': 'delta_w', 'delta_w_branch_swa': 'delta_w', 'delta_w_out': 'delta_w', 'delta_ffn2_norm': 'delta_w', 'delta_ffn2_w_in': 'delta_w', 'delta_ffn2_w_out': 'delta_w', 'new_m_meta_tokens': 'new_m', 'new_m_rel_bias_table': 'new_m', 'new_m_ffn1_norm': 'new_m', 'new_m_ffn1_w_in': 'new_m', 'new_m_ffn1_w_out': 'new_m', 'new_m_mix_norm': 'new_m', 'new_m_w_in': 'new_m', 'new_m_forget_bias': 'new_m', 'new_m_fox_q_norm': 'new_m', 'new_m_fox_k_norm': 'new_m', 'new_m_swa_q_norm': 'new_m', 'new_m_swa_k_norm': 'new_m', 'new_m_swa_sinks': 'new_m', 'new_m_w_branch_fox': 'new_m', 'new_m_w_branch_swa': 'new_m', 'new_m_w_out': 'new_m', 'new_m_ffn2_norm': 'new_m', 'new_m_ffn2_w_in': 'new_m', 'new_m_ffn2_w_out': 'new_m', 'new_v_meta_tokens': 'new_v', 'new_v_rel_bias_table': 'new_v', 'new_v_ffn1_norm': 'new_v', 'new_v_ffn1_w_in': 'new_v', 'new_v_ffn1_w_out': 'new_v', 'new_v_mix_norm': 'new_v', 'new_v_w_in': 'new_v', 'new_v_forget_bias': 'new_v', 'new_v_fox_q_norm': 'new_v', 'new_v_fox_k_norm': 'new_v', 'new_v_swa_q_norm': 'new_v', 'new_v_swa_k_norm': 'new_v', 'new_v_swa_sinks': 'new_v', 'new_v_w_branch_fox': 'new_v', 'new_v_w_branch_swa': 'new_v', 'new_v_w_out': 'new_v', 'new_v_ffn2_norm': 'new_v', 'new_v_ffn2_w_in': 'new_v', 'new_v_ffn2_w_out': 'new_v'}


def _forward(args):
    return _fwd_reference(*[args[k] for k in FWD_PARAMS])


def _output_shape():
    def fwd():
        inp = _fwd_setup_inputs(0)
        return _fwd_reference(*[inp[k] for k in FWD_PARAMS])
    out = _jax.eval_shape(fwd)
    return out.shape, out.dtype

N_MICROBATCH = 1
ADAM_LR = 0.001
ADAM_B1 = 0.9
ADAM_B2 = 0.999
ADAM_EPS = 1e-08
ADAM_WD = 0.01
ADAM_STEP = 10
PER_EXAMPLE_BATCH_AXIS = {'x': 0, 'loss_target': 0}
SHARED_INPUTS = []
_WEIGHT_DTYPES = {'meta_tokens': _jnp.float32, 'rel_bias_table': _jnp.float32, 'ffn1_norm': _jnp.float32, 'ffn1_w_in': _jnp.float32, 'ffn1_w_out': _jnp.float32, 'mix_norm': _jnp.float32, 'w_in': _jnp.float32, 'forget_bias': _jnp.float32, 'fox_q_norm': _jnp.float32, 'fox_k_norm': _jnp.float32, 'swa_q_norm': _jnp.float32, 'swa_k_norm': _jnp.float32, 'swa_sinks': _jnp.float32, 'w_branch_fox': _jnp.float32, 'w_branch_swa': _jnp.float32, 'w_out': _jnp.float32, 'ffn2_norm': _jnp.float32, 'ffn2_w_in': _jnp.float32, 'ffn2_w_out': _jnp.float32}
MOMENT_SCALE = {'meta_tokens': 2.655929e-02, 'rel_bias_table': 1.200326e+00, 'ffn1_norm': 6.236753e+00, 'ffn1_w_in': 8.569380e-02, 'ffn1_w_out': 1.494275e-01, 'mix_norm': 2.373683e+00, 'w_in': 1.081207e-01, 'forget_bias': 6.639441e+01, 'fox_q_norm': 9.647685e+00, 'fox_k_norm': 9.720255e+00, 'swa_q_norm': 2.629588e+00, 'swa_k_norm': 2.584952e+00, 'swa_sinks': 1.930326e-01, 'w_branch_fox': 1.349266e-01, 'w_branch_swa': 6.026414e-02, 'w_out': 1.371211e-01, 'ffn2_norm': 6.334948e+00, 'ffn2_w_in': 7.770290e-02, 'ffn2_w_out': 1.384032e-01}


def _to_microbatches(a, axis):
    t = _jnp.moveaxis(a, axis, 0)
    t = t.reshape((N_MICROBATCH, t.shape[0] // N_MICROBATCH) + t.shape[1:])
    return _jnp.moveaxis(t, 1, axis + 1)


def setup_inputs(seed: int = 0) -> dict:
    inp = _fwd_setup_inputs(seed)
    key = _jax.random.fold_in(_jax.random.key(seed), 7919)
    shape, _ = _output_shape()
    out = dict(inp)
    out["loss_target"] = _jax.random.normal(_jax.random.fold_in(key, 0), shape, _jnp.float32)
    for i, name in enumerate(TWIN_WEIGHTS):
        w = inp[name].astype(_jnp.float32)
        if MOMENT_SCALE is None:
            s = _jnp.sqrt(_jnp.mean(_jnp.square(w)) + 1e-30)
        else:
            s = MOMENT_SCALE[name]
        km, kv = _jax.random.split(_jax.random.fold_in(key, i + 1))
        out[name] = w
        out["m_" + name] = s * _jax.random.normal(km, w.shape, _jnp.float32)
        out["v_" + name] = (s * s) * _jax.random.uniform(kv, w.shape, _jnp.float32, 0.5, 1.5)
    if N_MICROBATCH > 1:
        for name, axis in PER_EXAMPLE_BATCH_AXIS.items():
            out[name] = _to_microbatches(out[name], axis)
    return {'x': out['x'], 'meta_tokens': out['meta_tokens'], 'rel_bias_table': out['rel_bias_table'], 'ffn1_norm': out['ffn1_norm'], 'ffn1_w_in': out['ffn1_w_in'], 'ffn1_w_out': out['ffn1_w_out'], 'mix_norm': out['mix_norm'], 'w_in': out['w_in'], 'forget_bias': out['forget_bias'], 'fox_q_norm': out['fox_q_norm'], 'fox_k_norm': out['fox_k_norm'], 'swa_q_norm': out['swa_q_norm'], 'swa_k_norm': out['swa_k_norm'], 'swa_sinks': out['swa_sinks'], 'w_branch_fox': out['w_branch_fox'], 'w_branch_swa': out['w_branch_swa'], 'w_out': out['w_out'], 'ffn2_norm': out['ffn2_norm'], 'ffn2_w_in': out['ffn2_w_in'], 'ffn2_w_out': out['ffn2_w_out'], 'loss_target': out['loss_target'], 'm_meta_tokens': out['m_meta_tokens'], 'm_rel_bias_table': out['m_rel_bias_table'], 'm_ffn1_norm': out['m_ffn1_norm'], 'm_ffn1_w_in': out['m_ffn1_w_in'], 'm_ffn1_w_out': out['m_ffn1_w_out'], 'm_mix_norm': out['m_mix_norm'], 'm_w_in': out['m_w_in'], 'm_forget_bias': out['m_forget_bias'], 'm_fox_q_norm': out['m_fox_q_norm'], 'm_fox_k_norm': out['m_fox_k_norm'], 'm_swa_q_norm': out['m_swa_q_norm'], 'm_swa_k_norm': out['m_swa_k_norm'], 'm_swa_sinks': out['m_swa_sinks'], 'm_w_branch_fox': out['m_w_branch_fox'], 'm_w_branch_swa': out['m_w_branch_swa'], 'm_w_out': out['m_w_out'], 'm_ffn2_norm': out['m_ffn2_norm'], 'm_ffn2_w_in': out['m_ffn2_w_in'], 'm_ffn2_w_out': out['m_ffn2_w_out'], 'v_meta_tokens': out['v_meta_tokens'], 'v_rel_bias_table': out['v_rel_bias_table'], 'v_ffn1_norm': out['v_ffn1_norm'], 'v_ffn1_w_in': out['v_ffn1_w_in'], 'v_ffn1_w_out': out['v_ffn1_w_out'], 'v_mix_norm': out['v_mix_norm'], 'v_w_in': out['v_w_in'], 'v_forget_bias': out['v_forget_bias'], 'v_fox_q_norm': out['v_fox_q_norm'], 'v_fox_k_norm': out['v_fox_k_norm'], 'v_swa_q_norm': out['v_swa_q_norm'], 'v_swa_k_norm': out['v_swa_k_norm'], 'v_swa_sinks': out['v_swa_sinks'], 'v_w_branch_fox': out['v_w_branch_fox'], 'v_w_branch_swa': out['v_w_branch_swa'], 'v_w_out': out['v_w_out'], 'v_ffn2_norm': out['v_ffn2_norm'], 'v_ffn2_w_in': out['v_ffn2_w_in'], 'v_ffn2_w_out': out['v_ffn2_w_out']}


def _loss(weights, diff, rest, loss_target):
    with _jax.named_scope("forward"):
        args = {**rest, TWIN_DIFF_INPUT: diff, **{k: w.astype(_WEIGHT_DTYPES[k]) for k, w in weights.items()}}
        y = _forward(args)
    with _jax.named_scope("loss_head"):
        err = _jnp.square(y.astype(_jnp.float32) - loss_target)
        return 0.5 * _jnp.sum(_jnp.mean(err, axis=-1)) if err.ndim else 0.5 * err


def _adamw(w, g, m, v):
    m = ADAM_B1 * m + (1.0 - ADAM_B1) * g
    v = ADAM_B2 * v + (1.0 - ADAM_B2) * _jnp.square(g)
    m_hat = m / (1.0 - ADAM_B1 ** ADAM_STEP)
    v_hat = v / (1.0 - ADAM_B2 ** ADAM_STEP)
    delta = -ADAM_LR * (m_hat / (_jnp.sqrt(v_hat) + ADAM_EPS) + ADAM_WD * w)
    return delta, m, v


def reference(x, meta_tokens, rel_bias_table, ffn1_norm, ffn1_w_in, ffn1_w_out, mix_norm, w_in, forget_bias, fox_q_norm, fox_k_norm, swa_q_norm, swa_k_norm, swa_sinks, w_branch_fox, w_branch_swa, w_out, ffn2_norm, ffn2_w_in, ffn2_w_out, loss_target, m_meta_tokens, m_rel_bias_table, m_ffn1_norm, m_ffn1_w_in, m_ffn1_w_out, m_mix_norm, m_w_in, m_forget_bias, m_fox_q_norm, m_fox_k_norm, m_swa_q_norm, m_swa_k_norm, m_swa_sinks, m_w_branch_fox, m_w_branch_swa, m_w_out, m_ffn2_norm, m_ffn2_w_in, m_ffn2_w_out, v_meta_tokens, v_rel_bias_table, v_ffn1_norm, v_ffn1_w_in, v_ffn1_w_out, v_mix_norm, v_w_in, v_forget_bias, v_fox_q_norm, v_fox_k_norm, v_swa_q_norm, v_swa_k_norm, v_swa_sinks, v_w_branch_fox, v_w_branch_swa, v_w_out, v_ffn2_norm, v_ffn2_w_in, v_ffn2_w_out):
    given = dict(x=x, meta_tokens=meta_tokens, rel_bias_table=rel_bias_table, ffn1_norm=ffn1_norm, ffn1_w_in=ffn1_w_in, ffn1_w_out=ffn1_w_out, mix_norm=mix_norm, w_in=w_in, forget_bias=forget_bias, fox_q_norm=fox_q_norm, fox_k_norm=fox_k_norm, swa_q_norm=swa_q_norm, swa_k_norm=swa_k_norm, swa_sinks=swa_sinks, w_branch_fox=w_branch_fox, w_branch_swa=w_branch_swa, w_out=w_out, ffn2_norm=ffn2_norm, ffn2_w_in=ffn2_w_in, ffn2_w_out=ffn2_w_out, loss_target=loss_target, m_meta_tokens=m_meta_tokens, m_rel_bias_table=m_rel_bias_table, m_ffn1_norm=m_ffn1_norm, m_ffn1_w_in=m_ffn1_w_in, m_ffn1_w_out=m_ffn1_w_out, m_mix_norm=m_mix_norm, m_w_in=m_w_in, m_forget_bias=m_forget_bias, m_fox_q_norm=m_fox_q_norm, m_fox_k_norm=m_fox_k_norm, m_swa_q_norm=m_swa_q_norm, m_swa_k_norm=m_swa_k_norm, m_swa_sinks=m_swa_sinks, m_w_branch_fox=m_w_branch_fox, m_w_branch_swa=m_w_branch_swa, m_w_out=m_w_out, m_ffn2_norm=m_ffn2_norm, m_ffn2_w_in=m_ffn2_w_in, m_ffn2_w_out=m_ffn2_w_out, v_meta_tokens=v_meta_tokens, v_rel_bias_table=v_rel_bias_table, v_ffn1_norm=v_ffn1_norm, v_ffn1_w_in=v_ffn1_w_in, v_ffn1_w_out=v_ffn1_w_out, v_mix_norm=v_mix_norm, v_w_in=v_w_in, v_forget_bias=v_forget_bias, v_fox_q_norm=v_fox_q_norm, v_fox_k_norm=v_fox_k_norm, v_swa_q_norm=v_swa_q_norm, v_swa_k_norm=v_swa_k_norm, v_swa_sinks=v_swa_sinks, v_w_branch_fox=v_w_branch_fox, v_w_branch_swa=v_w_branch_swa, v_w_out=v_w_out, v_ffn2_norm=v_ffn2_norm, v_ffn2_w_in=v_ffn2_w_in, v_ffn2_w_out=v_ffn2_w_out)
    weights = {n: given[n] for n in TWIN_WEIGHTS}
    shared = {n: given[n] for n in SHARED_INPUTS}
    per_example = {n: given[n] for n in ['x']}
    grad_fn = _jax.value_and_grad(_loss, argnums=(0, 1))

    def one_microbatch(ex, loss_target):
        ex = dict(ex)
        diff = ex.pop(TWIN_DIFF_INPUT)
        return grad_fn(weights, diff, {**shared, **ex}, loss_target)

    if N_MICROBATCH == 1:
        loss, (grad_w, grad_x) = one_microbatch(per_example, given["loss_target"])
    else:
        def body(carry, xs):
            loss_sum, grad_sum = carry
            l_k, (gw_k, gx_k) = one_microbatch(xs[0], xs[1])
            with _jax.named_scope("update"):
                return (loss_sum + l_k, _jax.tree.map(_jnp.add, grad_sum, gw_k)), gx_k

        init = (_jnp.zeros((), _jnp.float32), _jax.tree.map(_jnp.zeros_like, weights))
        (loss, grad_w), grad_x = _jax.lax.scan(body, init, (per_example, given["loss_target"]))
    with _jax.named_scope("update"):
        delta_w, new_m, new_v = {}, {}, {}
        for n in TWIN_WEIGHTS:
            delta_w[n], new_m[n], new_v[n] = _adamw(weights[n], grad_w[n], given["m_" + n], given["v_" + n])
    return (loss, grad_x, *[grad_w[n] for n in TWIN_WEIGHTS], *[delta_w[n] for n in TWIN_WEIGHTS],
            *[new_m[n] for n in TWIN_WEIGHTS], *[new_v[n] for n in TWIN_WEIGHTS])
```

```python
import math

import numpy as np
import jax
import jax.numpy as jnp
from jax import lax
from jax.experimental import pallas as pl
from jax.experimental.pallas import tpu as pltpu

D = 1024
F = 2816
HD = 64
NPAIR = 4
N_META = 16
BLK = 128
PAD = BLK - N_META
EPS = 1e-6
NEG = -1e30
N_BUCKETS = 32
GA, GB, QA, KA, VA, QB, KB, VB, FA, DP = 0, 1024, 2048, 2560, 3072, 3584, 4096, 4224, 4352, 4480
D_IN = 4360
CDT = jnp.bfloat16
F32 = jnp.float32
VMEM_LIMIT = 48 * 1024 * 1024
MESH_ID = pl.DeviceIdType.MESH

ADAM_LR, ADAM_B1, ADAM_B2, ADAM_EPS, ADAM_WD, ADAM_STEP = 0.001, 0.9, 0.999, 1e-08, 0.01, 10

SHARD_ITEMS = (
    ("ffn1_w_in", (1024, 1408), "col"),
    ("ffn1_w_out", (704, 1024), "row"),
    ("w_in", (1024, 1090), "col"),
    ("w_branch_fox", (512, 256), "col"),
    ("w_branch_swa", (512, 256), "col"),
    ("w_out", (256, 1024), "row"),
    ("ffn2_w_in", (1024, 1408), "col"),
    ("ffn2_w_out", (704, 1024), "row"),
)
RAW_ROWS = 2 * sum(r * c for _, (r, c), _ in SHARD_ITEMS) // 128
PACK_TILE = 5840
HALF_ROWS = 8 * PACK_TILE
SHARD_ROWS = 2 * HALF_ROWS
SMALL_ROWS = 192
META_ROWS = 32


def _row_tile(t):
    return 384 if t % 384 == 0 else 128


def _dot(a, b):
    return jnp.dot(a, b, preferred_element_type=F32)


def _dot_nt(a, b):
    return lax.dot_general(a, b, (((1,), (1,)), ((), ())), preferred_element_type=F32)


def _dot_hi(a, b):
    return jnp.dot(a, b, preferred_element_type=F32, precision=lax.Precision.HIGHEST)


def _sigmoid(x):
    return 1.0 / (1.0 + jnp.exp(-x))


def _iota(shape, dim):
    return lax.broadcasted_iota(jnp.int32, shape, dim)


def _params(sem=None):
    return pltpu.CompilerParams(dimension_semantics=sem, vmem_limit_bytes=VMEM_LIMIT)


def _sds(shape, dtype):
    return jax.ShapeDtypeStruct(shape, dtype)


def _rms_fwd(h, g, name):
    t = h.shape[0]
    tm = _row_tile(t)

    def body(h_ref, g_ref, a_ref, at_ref):
        x = h_ref[...]
        ms = jnp.mean(x * x, axis=-1, keepdims=True)
        a = x * lax.rsqrt(ms + EPS) * g_ref[...]
        a_ref[...] = a.astype(CDT)
        at_ref[...] = a.T.astype(CDT)

    return pl.pallas_call(
        body, name=name, grid=(t // tm,),
        in_specs=[pl.BlockSpec((tm, D), lambda i: (i, 0)), pl.BlockSpec((1, D), lambda i: (0, 0))],
        out_specs=[pl.BlockSpec((tm, D), lambda i: (i, 0)), pl.BlockSpec((D, tm), lambda i: (0, i))],
        out_shape=[_sds((t, D), CDT), _sds((D, t), CDT)],
        compiler_params=_params(("parallel",)),
    )(h, g)


def _rms_bwd(da, h, g, dres, name):
    t = h.shape[0]
    tm = _row_tile(t)

    def body(da_ref, h_ref, g_ref, dr_ref, dh_ref, dhb_ref, dg_ref):
        i = pl.program_id(0)
        x = h_ref[...]
        da_ = da_ref[...]
        r = lax.rsqrt(jnp.mean(x * x, axis=-1, keepdims=True) + EPS)
        xh = x * r
        day = da_ * g_ref[...]
        dx = r * (day - xh * jnp.mean(day * xh, axis=-1, keepdims=True))
        dh = dr_ref[...] + dx
        dh_ref[...] = dh
        dhb_ref[...] = dh.astype(CDT)

        @pl.when(i == 0)
        def _():
            dg_ref[...] = jnp.zeros(dg_ref.shape, F32)

        dg_ref[0:1, :] += jnp.sum(da_ * xh, axis=0, keepdims=True)

    row = pl.BlockSpec((tm, D), lambda i: (i, 0))
    return pl.pallas_call(
        body, name=name, grid=(t // tm,),
        in_specs=[row, row, pl.BlockSpec((1, D), lambda i: (0, 0)), row],
        out_specs=[row, row, pl.BlockSpec((8, D), lambda i: (0, 0))],
        out_shape=[_sds((t, D), F32), _sds((t, D), CDT), _sds((8, D), F32)],
        compiler_params=_params(("arbitrary",)),
    )(da, h, g, dres)


def _ffn_in(a, w_in, name):
    t = a.shape[0]
    tm = _row_tile(t)
    tn = 256
    nj = F // tn

    def body(a_ref, wg_ref, wu_ref, gu_ref, s_ref, st_ref):
        a_ = a_ref[...]
        g = _dot(a_, wg_ref[...])
        u = _dot(a_, wu_ref[...])
        s = g * _sigmoid(g) * u
        gu_ref[0] = g.astype(CDT)
        gu_ref[1] = u.astype(CDT)
        s_ref[...] = s.astype(CDT)
        st_ref[...] = s.T.astype(CDT)

    return pl.pallas_call(
        body, name=name, grid=(t // tm, nj),
        in_specs=[pl.BlockSpec((tm, D), lambda i, j: (i, 0)),
                  pl.BlockSpec((D, tn), lambda i, j: (0, j)),
                  pl.BlockSpec((D, tn), lambda i, j: (0, j + nj))],
        out_specs=[pl.BlockSpec((2, tm, tn), lambda i, j: (0, i, j)),
                   pl.BlockSpec((tm, tn), lambda i, j: (i, j)),
                   pl.BlockSpec((tn, tm), lambda i, j: (j, i))],
        out_shape=[_sds((2, t, F), CDT), _sds((t, F), CDT), _sds((F, t), CDT)],
        compiler_params=_params(("parallel", "parallel")),
    )(a, w_in, w_in)


def _mm_res(a, b, res, scale, name):
    t, k = a.shape
    n = b.shape[1]
    tm = _row_tile(t)
    tn = 512

    def body(a_ref, b_ref, r_ref, o_ref):
        o_ref[...] = r_ref[...] + scale * _dot(a_ref[...], b_ref[...])

    return pl.pallas_call(
        body, name=name, grid=(t // tm, n // tn),
        in_specs=[pl.BlockSpec((tm, k), lambda i, j: (i, 0)),
                  pl.BlockSpec((k, tn), lambda i, j: (0, j)),
                  pl.BlockSpec((tm, tn), lambda i, j: (i, j))],
        out_specs=pl.BlockSpec((tm, tn), lambda i, j: (i, j)),
        out_shape=_sds((t, n), F32),
        compiler_params=_params(("parallel", "parallel")),
    )(a, b, res)


def _mm(a, b, out_dtype, tm, tn, name, scale=1.0):
    m, k = a.shape
    if b.ndim == 3:
        nh = b.shape[2] // tn
        n = 2 * b.shape[2]
        b_spec = pl.BlockSpec((None, k, tn), lambda i, j: (j // nh, 0, j % nh))
    else:
        n = b.shape[1]
        b_spec = pl.BlockSpec((k, tn), lambda i, j: (0, j))

    def body(a_ref, b_ref, o_ref):
        o_ref[...] = (scale * _dot(a_ref[...], b_ref[...])).astype(out_dtype)

    return pl.pallas_call(
        body, name=name, grid=(m // tm, n // tn),
        in_specs=[pl.BlockSpec((tm, k), lambda i, j: (i, 0)), b_spec],
        out_specs=pl.BlockSpec((tm, tn), lambda i, j: (i, j)),
        out_shape=_sds((m, n), out_dtype),
        compiler_params=_params(("parallel", "parallel")),
    )(a, b)


def _mm_nt(a, b, name):
    m, n = a.shape
    k = b.shape[0]
    tm = _row_tile(m)
    tk = 512

    def body(a_ref, b_ref, o_ref):
        o_ref[...] = _dot_nt(a_ref[...], b_ref[...])

    return pl.pallas_call(
        body, name=name, grid=(m // tm, k // tk),
        in_specs=[pl.BlockSpec((tm, n), lambda i, j: (i, 0)), pl.BlockSpec((tk, n), lambda i, j: (j, 0))],
        out_specs=pl.BlockSpec((tm, tk), lambda i, j: (i, j)),
        out_shape=_sds((m, k), F32),
        compiler_params=_params(("parallel", "parallel")),
    )(a, b)


def _ffn_bwd_mid(dhb, w_out, gu, name):
    t = dhb.shape[0]
    tm = _row_tile(t)
    tn = 256

    def body(dh_ref, w_ref, gu_ref, o_ref):
        ds = 0.5 * _dot_nt(dh_ref[...], w_ref[...])
        g = gu_ref[0].astype(F32)
        u = gu_ref[1].astype(F32)
        sg = _sigmoid(g)
        o_ref[0] = (ds * u * (sg * (1.0 + g * (1.0 - sg)))).astype(CDT)
        o_ref[1] = (ds * (g * sg)).astype(CDT)

    return pl.pallas_call(
        body, name=name, grid=(t // tm, F // tn),
        in_specs=[pl.BlockSpec((tm, D), lambda i, j: (i, 0)),
                  pl.BlockSpec((tn, D), lambda i, j: (j, 0)),
                  pl.BlockSpec((2, tm, tn), lambda i, j: (0, i, j))],
        out_specs=pl.BlockSpec((2, tm, tn), lambda i, j: (0, i, j)),
        out_shape=_sds((2, t, F), CDT),
        compiler_params=_params(("parallel", "parallel")),
    )(dhb, w_out, gu)


def _ffn_bwd_in(dgu, w_in, name):
    t = dgu.shape[1]
    tm = _row_tile(t)
    tk = 512

    def body(dg_ref, wg_ref, wu_ref, o_ref):
        o_ref[...] = _dot_nt(dg_ref[0], wg_ref[...]) + _dot_nt(dg_ref[1], wu_ref[...])

    return pl.pallas_call(
        body, name=name, grid=(t // tm, D // tk),
        in_specs=[pl.BlockSpec((2, tm, F), lambda i, j: (0, i, 0)),
                  pl.BlockSpec((tk, F), lambda i, j: (j, 0)),
                  pl.BlockSpec((tk, F), lambda i, j: (j, 1))],
        out_specs=pl.BlockSpec((tm, tk), lambda i, j: (i, j)),
        out_shape=_sds((t, D), F32),
        compiler_params=_params(("parallel", "parallel")),
    )(dgu, w_in, w_in)


def _loss(h, target, name):
    t = h.shape[0]

    def body(h_ref, t_ref, dh_ref, dhb_ref, l_ref):
        i = pl.program_id(0)

        @pl.when(i == 0)
        def _():
            l_ref[...] = jnp.zeros(l_ref.shape, F32)
            dh_ref[...] = jnp.zeros(dh_ref.shape, F32)
            dhb_ref[...] = jnp.zeros(dhb_ref.shape, CDT)

        @pl.when(i > 0)
        def _():
            err = h_ref[...] - t_ref[...]
            l_ref[...] += (0.5 / D) * jnp.sum(err * err)
            d = err * (1.0 / D)
            dh_ref[...] = d
            dhb_ref[...] = d.astype(CDT)

    row = pl.BlockSpec((BLK, D), lambda i: (i, 0))
    return pl.pallas_call(
        body, name=name, grid=(t // BLK,),
        in_specs=[row, pl.BlockSpec((BLK, D), lambda i: (jnp.maximum(i - 1, 0), 0))],
        out_specs=[row, row, pl.BlockSpec((8, 128), lambda i: (0, 0))],
        out_shape=[_sds((t, D), F32), _sds((t, D), CDT), _sds((8, 128), F32)],
        compiler_params=_params(("arbitrary",)),
    )(h, target)


def _block_diag():
    return (_iota((128, 128), 0) // HD == _iota((128, 128), 1) // HD).astype(F32)


def _dup_halves(x, lo):
    sw = pltpu.roll(x, 64, 1)
    return jnp.where(lo, x, sw), jnp.where(lo, sw, x)


def _qknorm_fwd(proj, gfq, gfk, gsq, gsk, fb, name):
    t = proj.shape[0]
    tm = _row_tile(t)

    def body(qa, ka, va, qb, kb, vb, fa, gfq_r, gfk_r, gsq_r, gsk_r, fb_r,
             qf_o, kf_o, vf_o, qs_o, kse_o, vse_o, c_o, ct_o, carry):
        i = pl.program_id(0)
        bd = _block_diag()
        lane = _iota((1, 128), 1)
        lo = lane < HD

        def hnorm(x, g):
            ms = _dot_hi(x * x, bd) * (1.0 / HD)
            return x * lax.rsqrt(ms + EPS) * g

        for ch in range(4):
            sl = slice(128 * ch, 128 * (ch + 1))
            qf_o[:, sl] = (hnorm(qa[:, sl], gfq_r[:, sl]) * 0.125).astype(CDT)
            kf_o[:, sl] = hnorm(ka[:, sl], gfk_r[:, sl]).astype(CDT)
            qs_o[:, sl] = (hnorm(qb[:, sl], gsq_r[:, sl]) * 0.125).astype(CDT)
        vf_o[...] = va[...].astype(CDT)
        k0, k1 = _dup_halves(hnorm(kb[...], gsk_r[...]), lo)
        kse_o[0] = k0.astype(CDT)
        kse_o[1] = k1.astype(CDT)
        v0, v1 = _dup_halves(vb[...], lo)
        vse_o[0] = v0.astype(CDT)
        vse_o[1] = v1.astype(CDT)

        z = fa[...] + fb_r[...]
        lf = jnp.minimum(z, 0.0) - jnp.log(1.0 + jnp.exp(-jnp.abs(z)))
        lf = jnp.where(lane < 8, lf, 0.0)
        ltri = (_iota((tm, tm), 1) <= _iota((tm, tm), 0)).astype(F32)

        @pl.when(i == 0)
        def _():
            carry[...] = jnp.zeros(carry.shape, F32)

        c = _dot_hi(ltri, lf) + carry[0:1, :]
        carry[0:1, :] = c[tm - 1:tm, :]
        c_o[...] = c
        ct_o[...] = c.T[0:8, :]

    def col(width, off):
        return pl.BlockSpec((tm, width), lambda i: (i, off // width))

    def vec(width):
        return pl.BlockSpec((1, width), lambda i: (0, 0))

    return pl.pallas_call(
        body, name=name, grid=(t // tm,),
        in_specs=[col(512, QA), col(512, KA), col(512, VA), col(512, QB), col(128, KB), col(128, VB), col(128, FA),
                  vec(512), vec(512), vec(512), vec(128), vec(128)],
        out_specs=[pl.BlockSpec((tm, 512), lambda i: (i, 0))] * 4
        + [pl.BlockSpec((2, tm, 128), lambda i: (0, i, 0))] * 2
        + [pl.BlockSpec((tm, 128), lambda i: (i, 0)), pl.BlockSpec((8, tm), lambda i: (0, i))],
        out_shape=[_sds((t, 512), CDT)] * 4 + [_sds((2, t, 128), CDT)] * 2 + [_sds((t, 128), F32), _sds((8, t), F32)],
        scratch_shapes=[pltpu.VMEM((8, 128), F32)],
        compiler_params=_params(("arbitrary",)),
    )(proj, proj, proj, proj, proj, proj, proj, gfq, gfk, gsq, gsk, fb)


def _qknorm_bwd(proj, dqf, dkf, dvf, dqs, dkse, dvse, dcq, dck, dga, dgb, gfq, gfk, gsq, gsk, fb, name):
    t = proj.shape[0]
    tm = _row_tile(t)
    nt = t // tm

    def body(qa, ka, qb, kb, fa, dqf_r, dkf_r, dvf_r, dqs_r, dkse_r, dvse_r, dcq_r, dck_r, dga_r, dgb_r,
             gfq_r, gfk_r, gsq_r, gsk_r, fb_r, dp_o, dgn_o, carry, acc):
        i = pl.program_id(0)
        bd = _block_diag()
        lane = _iota((1, 128), 1)
        lo = lane < HD

        @pl.when(i == 0)
        def _():
            carry[...] = jnp.zeros(carry.shape, F32)
            acc[...] = jnp.zeros(acc.shape, F32)

        def hnorm_bwd(x, g, dy):
            r = lax.rsqrt(_dot_hi(x * x, bd) * (1.0 / HD) + EPS)
            xh = x * r
            day = dy * g
            dx = r * (day - xh * (_dot_hi(day * xh, bd) * (1.0 / HD)))
            return dx, jnp.sum(dy * xh, axis=0, keepdims=True)

        for ch in range(4):
            sl = slice(128 * ch, 128 * (ch + 1))
            dx, dg = hnorm_bwd(qa[:, sl], gfq_r[:, sl], dqf_r[:, sl] * 0.125)
            dp_o[:, QA + 128 * ch:QA + 128 * (ch + 1)] = dx.astype(CDT)
            acc[0:1, sl] += dg
            dx, dg = hnorm_bwd(ka[:, sl], gfk_r[:, sl], dkf_r[:, sl])
            dp_o[:, KA + 128 * ch:KA + 128 * (ch + 1)] = dx.astype(CDT)
            acc[1:2, sl] += dg
            dx, dg = hnorm_bwd(qb[:, sl], gsq_r[:, sl], dqs_r[:, sl] * 0.125)
            dp_o[:, QB + 128 * ch:QB + 128 * (ch + 1)] = dx.astype(CDT)
            acc[2:3, sl] += dg
        dp_o[:, VA:VA + 512] = dvf_r[...].astype(CDT)
        dp_o[:, GA:GA + D] = dga_r[...]
        dp_o[:, GB:GB + D] = dgb_r[...]

        def fold(x):
            e0 = x[0]
            e1 = x[1]
            return jnp.where(lo, e0 + pltpu.roll(e0, 64, 1), e1 + pltpu.roll(e1, 64, 1))

        dx, dg = hnorm_bwd(kb[...], gsk_r[...], fold(dkse_r))
        dp_o[:, KB:KB + 128] = dx.astype(CDT)
        acc[3:4, 0:128] += dg
        dp_o[:, VB:VB + 128] = fold(dvse_r).astype(CDT)

        rr = _iota((512, 128), 0)
        hh = _iota((512, 128), 1)
        sel = ((rr == (hh >> 1) * 128 + (hh & 1) * HD) & (hh < 8)).astype(F32)
        dcs = _dot_hi(dcq_r[...] - dck_r[...], sel)
        utri = (_iota((tm, tm), 1) >= _iota((tm, tm), 0)).astype(F32)
        dlf = _dot_hi(utri, dcs) + carry[0:1, :]
        carry[0:1, :] = dlf[0:1, :]
        z = fa[...] + fb_r[...]
        dfa = jnp.where(lane < 8, dlf * _sigmoid(-z), 0.0)
        dp_o[:, FA:FA + 128] = dfa.astype(CDT)
        acc[4:5, 0:128] += jnp.sum(dfa, axis=0, keepdims=True)

        @pl.when(i == nt - 1)
        def _():
            foldm = ((_iota((512, 128), 0) & (HD - 1)) == _iota((512, 128), 1)).astype(F32)
            dgn_o[...] = _dot_hi(acc[...], foldm)

    def col(width, off):
        return pl.BlockSpec((tm, width), lambda i: (nt - 1 - i, off // width))

    def rows(width):
        return pl.BlockSpec((tm, width), lambda i: (nt - 1 - i, 0))

    def vec(width):
        return pl.BlockSpec((1, width), lambda i: (0, 0))

    pair = pl.BlockSpec((2, tm, 128), lambda i: (0, nt - 1 - i, 0))
    return pl.pallas_call(
        body, name=name, grid=(nt,),
        in_specs=[col(512, QA), col(512, KA), col(512, QB), col(128, KB), col(128, FA),
                  rows(512), rows(512), rows(512), rows(512), pair, pair, rows(512), rows(512), rows(D), rows(D),
                  vec(512), vec(512), vec(512), vec(128), vec(128)],
        out_specs=[rows(DP), pl.BlockSpec((8, 128), lambda i: (0, 0))],
        out_shape=[_sds((t, DP), CDT), _sds((8, 128), F32)],
        scratch_shapes=[pltpu.VMEM((8, 128), F32), pltpu.VMEM((8, 512), F32)],
        compiler_params=_params(("arbitrary",)),
    )(proj, proj, proj, proj, proj, dqf, dkf, dvf, dqs, dkse, dvse, dcq, dck, dga, dgb, gfq, gfk, gsq, gsk, fb)


def _gate_fwd(ofox, oswa, wbf, wbs, proj, name):
    t = ofox.shape[0]
    tm = _row_tile(t)
    tn = 512

    def body(of_r, os_r, wf_r, ws_r, ga_r, gb_r, y_o, yt_o, pf_o, ps_o, oft_o, ost_o):
        j = pl.program_id(1)
        pf = _dot(of_r[...], wf_r[...])
        ps = _dot(os_r[...], ws_r[...])
        y = _sigmoid(ga_r[...]) * pf + _sigmoid(gb_r[...]) * ps
        y_o[...] = y.astype(CDT)
        yt_o[...] = y.T.astype(CDT)
        pf_o[...] = pf.astype(CDT)
        ps_o[...] = ps.astype(CDT)

        @pl.when(j == 0)
        def _():
            oft_o[...] = of_r[...].astype(F32).T.astype(CDT)
            ost_o[...] = os_r[...].astype(F32).T.astype(CDT)

    tile = pl.BlockSpec((tm, tn), lambda i, j: (i, j))
    return pl.pallas_call(
        body, name=name, grid=(t // tm, D // tn),
        in_specs=[pl.BlockSpec((tm, 512), lambda i, j: (i, 0)), pl.BlockSpec((tm, 512), lambda i, j: (i, 0)),
                  pl.BlockSpec((512, tn), lambda i, j: (0, j)), pl.BlockSpec((512, tn), lambda i, j: (0, j)),
                  pl.BlockSpec((tm, tn), lambda i, j: (i, GA // tn + j)),
                  pl.BlockSpec((tm, tn), lambda i, j: (i, GB // tn + j))],
        out_specs=[tile, pl.BlockSpec((tn, tm), lambda i, j: (j, i)), tile, tile,
                   pl.BlockSpec((512, tm), lambda i, j: (0, i)), pl.BlockSpec((512, tm), lambda i, j: (0, i))],
        out_shape=[_sds((t, D), CDT), _sds((D, t), CDT), _sds((t, D), CDT), _sds((t, D), CDT),
                   _sds((512, t), CDT), _sds((512, t), CDT)],
        compiler_params=_params(("parallel", "arbitrary")),
    )(ofox, oswa, wbf, wbs, proj, proj)


def _gate_bwd(dy, pf, ps, proj, name):
    t = dy.shape[0]
    tm = _row_tile(t)
    tn = 512

    def body(dy_r, pf_r, ps_r, ga_r, gb_r, dpf_o, dps_o, dga_o, dgb_o):
        dy_ = dy_r[...]
        sa = _sigmoid(ga_r[...])
        sb = _sigmoid(gb_r[...])
        dpf_o[...] = (dy_ * sa).astype(CDT)
        dps_o[...] = (dy_ * sb).astype(CDT)
        dga_o[...] = (dy_ * pf_r[...].astype(F32) * (sa * (1.0 - sa))).astype(CDT)
        dgb_o[...] = (dy_ * ps_r[...].astype(F32) * (sb * (1.0 - sb))).astype(CDT)

    tile = pl.BlockSpec((tm, tn), lambda i, j: (i, j))
    return pl.pallas_call(
        body, name=name, grid=(t // tm, D // tn),
        in_specs=[tile, tile, tile,
                  pl.BlockSpec((tm, tn), lambda i, j: (i, GA // tn + j)),
                  pl.BlockSpec((tm, tn), lambda i, j: (i, GB // tn + j))],
        out_specs=[tile] * 4,
        out_shape=[_sds((t, D), CDT)] * 4,
        compiler_params=_params(("parallel", "parallel")),
    )(dy, pf, ps, proj, proj)


def _tri_steps(n, by_key):
    if by_key:
        pairs = [(i, j) for j in range(n) for i in range(j, n)]
    else:
        pairs = [(i, j) for i in range(n) for j in range(i + 1)]
    return (np.array([p[0] for p in pairs], np.int32), np.array([p[1] for p in pairs], np.int32))


def _head_col(blk, lane, h):
    return jnp.sum(jnp.where(lane == h, blk, 0.0), axis=1, keepdims=True)


def _head_row(blk, sub, h):
    return jnp.sum(jnp.where(sub == h, blk, 0.0), axis=0, keepdims=True)


def _fox_fwd(qf, kf, vf, c, ct, name):
    t = qf.shape[0]
    ta = _row_tile(t)
    qi, kj = _tri_steps(t // ta, by_key=False)

    def body(qi_r, kj_r, q_r, k_r, v_r, c_r, ct_r, o_o, lse_o, m_sc, l_sc, acc_sc):
        p = pl.program_id(0)
        n = pl.program_id(1)
        i = qi_r[n]
        j = kj_r[n]
        lane = _iota((1, 128), 1)
        lo = lane < HD

        @pl.when(j == 0)
        def _():
            m_sc[...] = jnp.full(m_sc.shape, NEG, F32)
            l_sc[...] = jnp.zeros(l_sc.shape, F32)
            acc_sc[...] = jnp.zeros(acc_sc.shape, F32)

        q = q_r[...]
        k = k_r[...]
        v = v_r[...]
        rows = i * ta + _iota((ta, 1), 0)
        cols = j * ta + _iota((1, ta), 1)
        mask = (cols <= rows) & (cols >= PAD)
        sub = _iota((8, 1), 0)
        alphas, pvs = [], []
        for e in (0, 1):
            sel = lo if e == 0 else jnp.logical_not(lo)
            s = _dot_nt(jnp.where(sel, q, 0), k)
            s = s + _head_col(c_r[...], lane, 2 * p + e) - _head_row(ct_r[...], sub, 2 * p + e)
            s = jnp.where(mask, s, NEG)
            m_prev = m_sc[e][:, 0:1]
            m_new = jnp.maximum(m_prev, jnp.max(s, axis=1, keepdims=True))
            alpha = jnp.exp(m_prev - m_new)
            pe = jnp.exp(s - m_new)
            l_new = alpha * l_sc[e][:, 0:1] + jnp.sum(pe, axis=1, keepdims=True)
            m_sc[e] = jnp.broadcast_to(m_new, (ta, 128))
            l_sc[e] = jnp.broadcast_to(l_new, (ta, 128))
            alphas.append(alpha)
            pvs.append(_dot(pe.astype(CDT), v))
        acc_sc[...] = acc_sc[...] * jnp.where(lo, alphas[0], alphas[1]) + jnp.where(lo, pvs[0], pvs[1])

        @pl.when(j == i)
        def _():
            l = jnp.where(lo, l_sc[0], l_sc[1])
            o_o[...] = (acc_sc[...] / l).astype(CDT)
            lse_o[...] = jnp.where(lo, m_sc[0], m_sc[1]) + jnp.log(l)

    qblk = pl.BlockSpec((ta, 128), lambda p, n, qi_r, kj_r: (qi_r[n], p))
    kblk = pl.BlockSpec((ta, 128), lambda p, n, qi_r, kj_r: (kj_r[n], p))
    grid_spec = pltpu.PrefetchScalarGridSpec(
        num_scalar_prefetch=2, grid=(NPAIR, len(qi)),
        in_specs=[qblk, kblk, kblk,
                  pl.BlockSpec((ta, 128), lambda p, n, qi_r, kj_r: (qi_r[n], 0)),
                  pl.BlockSpec((8, ta), lambda p, n, qi_r, kj_r: (0, kj_r[n]))],
        out_specs=[qblk, qblk],
        scratch_shapes=[pltpu.VMEM((2, ta, 128), F32), pltpu.VMEM((2, ta, 128), F32), pltpu.VMEM((ta, 128), F32)],
    )
    return pl.pallas_call(
        body, name=name, grid_spec=grid_spec,
        out_shape=[_sds((t, 512), CDT), _sds((t, 512), F32)],
        compiler_params=_params(("parallel", "arbitrary")),
    )(jnp.asarray(qi), jnp.asarray(kj), qf, kf, vf, c, ct)


def _fox_bwd(qf, kf, vf, c, ct, o, lse, do, name):
    t = qf.shape[0]
    ta = _row_tile(t)
    nq = t // ta
    qi, kj = _tri_steps(nq, by_key=True)

    def body(qi_r, kj_r, q_r, k_r, v_r, c_r, ct_r, o_r, lse_r, do_r,
             dq_o, dcq_o, dk_o, dv_o, dck_o, dk_sc, dv_sc, dck_sc):
        p = pl.program_id(0)
        n = pl.program_id(1)
        i = qi_r[n]
        j = kj_r[n]
        lane = _iota((1, 128), 1)
        lo = lane < HD

        @pl.when(n == 0)
        def _():
            dq_o[...] = jnp.zeros(dq_o.shape, F32)
            dcq_o[...] = jnp.zeros(dcq_o.shape, F32)

        @pl.when(i == j)
        def _():
            dk_sc[...] = jnp.zeros(dk_sc.shape, F32)
            dv_sc[...] = jnp.zeros(dv_sc.shape, F32)
            dck_sc[...] = jnp.zeros(dck_sc.shape, F32)

        q = q_r[...]
        k = k_r[...]
        v = v_r[...]
        do_ = do_r[...]
        dd = do_ * o_r[...].astype(F32)
        lse = lse_r[...]
        rows = i * ta + _iota((ta, 1), 0)
        cols = j * ta + _iota((1, ta), 1)
        mask = (cols <= rows) & (cols >= PAD)
        sub = _iota((8, 1), 0)
        dq_add = jnp.zeros((ta, 128), F32)
        dk_add = jnp.zeros((ta, 128), F32)
        dv_add = jnp.zeros((ta, 128), F32)
        rsum, csum = [], []
        for e in (0, 1):
            sel = lo if e == 0 else jnp.logical_not(lo)
            qe = jnp.where(sel, q, 0)
            doe = jnp.where(sel, do_, 0.0).astype(CDT)
            s = _dot_nt(qe, k)
            s = s + _head_col(c_r[...], lane, 2 * p + e) - _head_row(ct_r[...], sub, 2 * p + e)
            s = jnp.where(mask, s, NEG)
            pr = jnp.exp(s - lse[:, HD * e:HD * e + 1])
            dv_add = dv_add + _dot(pr.T.astype(CDT), doe)
            dp = _dot_nt(doe, v)
            delta = jnp.sum(jnp.where(sel, dd, 0.0), axis=1, keepdims=True)
            ds = pr * (dp - delta)
            dq_add = dq_add + _dot(ds.astype(CDT), jnp.where(sel, k, 0))
            dst = ds.T
            dk_add = dk_add + _dot(dst.astype(CDT), qe)
            rsum.append(jnp.sum(ds, axis=1, keepdims=True))
            csum.append(jnp.sum(dst, axis=1, keepdims=True))
        rs = pl.ds(pl.multiple_of(i * ta, ta), ta)
        dq_o[rs, :] += dq_add
        dcq_o[rs, :] += jnp.where(lo, rsum[0], rsum[1])
        dk_sc[...] += dk_add
        dv_sc[...] += dv_add
        dck_sc[...] += jnp.where(lo, csum[0], csum[1])

        @pl.when(i == nq - 1)
        def _():
            dk_o[...] = dk_sc[...]
            dv_o[...] = dv_sc[...]
            dck_o[...] = dck_sc[...]

    qblk = pl.BlockSpec((ta, 128), lambda p, n, qi_r, kj_r: (qi_r[n], p))
    kblk = pl.BlockSpec((ta, 128), lambda p, n, qi_r, kj_r: (kj_r[n], p))
    whole = pl.BlockSpec((t, 128), lambda p, n, qi_r, kj_r: (0, p))
    grid_spec = pltpu.PrefetchScalarGridSpec(
        num_scalar_prefetch=2, grid=(NPAIR, len(qi)),
        in_specs=[qblk, kblk, kblk,
                  pl.BlockSpec((ta, 128), lambda p, n, qi_r, kj_r: (qi_r[n], 0)),
                  pl.BlockSpec((8, ta), lambda p, n, qi_r, kj_r: (0, kj_r[n])),
                  qblk, qblk, qblk],
        out_specs=[whole, whole, kblk, kblk, kblk],
        scratch_shapes=[pltpu.VMEM((ta, 128), F32)] * 3,
    )
    return pl.pallas_call(
        body, name=name, grid_spec=grid_spec,
        out_shape=[_sds((t, 512), F32)] * 5,
        compiler_params=_params(("parallel", "arbitrary")),
    )(jnp.asarray(qi), jnp.asarray(kj), qf, kf, vf, c, ct, o, lse, do)


def _bucket_table():
    r = np.arange(BLK)[:, None]
    c = np.arange(3 * BLK)[None, :]
    d = np.where(c < BLK, r + BLK - c, r - (c - BLK))
    n = np.maximum(d, 0)
    max_exact = N_BUCKETS // 2
    nf = np.maximum(n, 1).astype(np.float32)
    large = max_exact + (np.log(nf / max_exact) / math.log(BLK / max_exact) * (N_BUCKETS - max_exact)).astype(np.int32)
    large = np.minimum(large, N_BUCKETS - 1)
    b = np.where(n < max_exact, n, large)
    return np.where(c < 2 * BLK, b, N_BUCKETS - 1).astype(np.int32)


def _bias_fwd(table, name):
    bucket = jnp.asarray(_bucket_table())

    def body(tab_r, b_r, o_o):
        h = pl.program_id(0)
        b = b_r[...]
        acc = jnp.zeros(b.shape, F32)
        for k in range(N_BUCKETS):
            acc = jnp.where(b == k, tab_r[k, h], acc)
        o_o[...] = acc

    return pl.pallas_call(
        body, name=name, grid=(8,),
        in_specs=[pl.BlockSpec(memory_space=pltpu.SMEM), pl.BlockSpec((BLK, 3 * BLK), lambda h: (0, 0))],
        out_specs=pl.BlockSpec((None, BLK, 3 * BLK), lambda h: (h, 0, 0)),
        out_shape=_sds((8, BLK, 3 * BLK), F32),
        compiler_params=_params(("parallel",)),
    )(table, bucket)


def _bias_bwd(dbias, name):
    bucket = jnp.asarray(_bucket_table())

    def body(d_r, b_r, o_o):
        h = pl.program_id(0)
        b = b_r[...]
        d = d_r[...]
        lane = _iota((1, 128), 1)
        row = jnp.zeros((1, 128), F32)
        for k in range(N_BUCKETS):
            row = jnp.where(lane == k, jnp.sum(jnp.where(b == k, d, 0.0)), row)
        o_o[pl.ds(h, 1), :] = row

    return pl.pallas_call(
        body, name=name, grid=(8,),
        in_specs=[pl.BlockSpec((None, BLK, 3 * BLK), lambda h: (h, 0, 0)), pl.BlockSpec((BLK, 3 * BLK), lambda h: (0, 0))],
        out_specs=pl.BlockSpec((8, 128), lambda h: (0, 0)),
        out_shape=_sds((8, 128), F32),
        compiler_params=_params(("arbitrary",)),
    )(dbias, bucket)


def _swa_valid(i):
    r = _iota((BLK, 1), 0)
    c = _iota((1, 3 * BLK), 1)
    prev = (c < BLK) & (c > r) & (i >= 1) & ((i - 1) * BLK + c >= PAD)
    cc = c - BLK
    cur = (c >= BLK) & (c < 2 * BLK) & (cc <= r) & (i * BLK + cc >= PAD)
    cm = c - 2 * BLK
    meta = (c >= 2 * BLK) & (cm >= PAD) & (i * BLK + r - cm >= BLK)
    return prev | cur | meta


def _swa_kv_specs():
    def at(f):
        return pl.BlockSpec((None, BLK, 128), lambda p, i: (p // 2, f(i), 0))
    return [at(lambda i: jnp.maximum(i - 1, 0)), at(lambda i: i), at(lambda i: 0)]


def _swa_fwd(qs, kse, vse, bias, sinks, name):
    t = qs.shape[0]

    def body(sink_r, q_r, kp_r, kc_r, km_r, vp_r, vc_r, vm_r, b_r, o_o, lse_o):
        p = pl.program_id(0)
        i = pl.program_id(1)
        lo = _iota((1, 128), 1) < HD
        q = q_r[...]
        k3 = jnp.concatenate([kp_r[...], kc_r[...], km_r[...]], axis=0)
        v3 = jnp.concatenate([vp_r[...], vc_r[...], vm_r[...]], axis=0)
        valid = _swa_valid(i)
        outs, lses = [], []
        for e in (0, 1):
            sel = lo if e == 0 else jnp.logical_not(lo)
            s = _dot_nt(jnp.where(sel, q, 0), k3) + b_r[e]
            s = jnp.where(valid, s, NEG)
            sink = sink_r[2 * p + e]
            mx = jnp.maximum(jnp.max(s, axis=1, keepdims=True), sink)
            pe = jnp.exp(s - mx)
            den = jnp.sum(pe, axis=1, keepdims=True) + jnp.exp(sink - mx)
            outs.append(_dot(pe.astype(CDT), v3) / den)
            lses.append(mx + jnp.log(den))
        o_o[...] = jnp.where(lo, outs[0], outs[1]).astype(CDT)
        lse_o[...] = jnp.where(lo, lses[0], lses[1])

    qblk = pl.BlockSpec((BLK, 128), lambda p, i: (i, p))
    return pl.pallas_call(
        body, name=name, grid=(NPAIR, t // BLK),
        in_specs=[pl.BlockSpec(memory_space=pltpu.SMEM), qblk] + _swa_kv_specs() + _swa_kv_specs()
        + [pl.BlockSpec((2, BLK, 3 * BLK), lambda p, i: (p, 0, 0))],
        out_specs=[qblk, qblk],
        out_shape=[_sds((t, 512), CDT), _sds((t, 512), F32)],
        compiler_params=_params(("parallel", "parallel")),
    )(sinks, qs, kse, kse, kse, vse, vse, vse, bias)


def _swa_bwd(qs, kse, vse, bias, sinks, o, lse, do, name):
    t = qs.shape[0]

    def body(sink_r, q_r, kp_r, kc_r, km_r, vp_r, vc_r, vm_r, b_r, o_r, lse_r, do_r,
             dq_o, dk_o, dv_o, db_o, dsk_o):
        p = pl.program_id(0)
        i = pl.program_id(1)
        lo = _iota((1, 128), 1) < HD

        @pl.when((i == 0) & (p % 2 == 0))
        def _():
            dk_o[...] = jnp.zeros(dk_o.shape, F32)
            dv_o[...] = jnp.zeros(dv_o.shape, F32)

        @pl.when(i == 0)
        def _():
            db_o[...] = jnp.zeros(db_o.shape, F32)
            dsk_o[...] = jnp.zeros(dsk_o.shape, F32)

        q = q_r[...]
        do_ = do_r[...]
        dd = do_ * o_r[...].astype(F32)
        lse = lse_r[...]
        k3 = jnp.concatenate([kp_r[...], kc_r[...], km_r[...]], axis=0)
        v3 = jnp.concatenate([vp_r[...], vc_r[...], vm_r[...]], axis=0)
        valid = _swa_valid(i)
        dq = jnp.zeros((BLK, 128), F32)
        dk3 = jnp.zeros((3 * BLK, 128), F32)
        dv3 = jnp.zeros((3 * BLK, 128), F32)
        dsink = []
        for e in (0, 1):
            sel = lo if e == 0 else jnp.logical_not(lo)
            qe = jnp.where(sel, q, 0)
            doe = jnp.where(sel, do_, 0.0).astype(CDT)
            lse_e = lse[:, HD * e:HD * e + 1]
            s = _dot_nt(qe, k3) + b_r[e]
            s = jnp.where(valid, s, NEG)
            pr = jnp.exp(s - lse_e)
            delta = jnp.sum(jnp.where(sel, dd, 0.0), axis=1, keepdims=True)
            ds = pr * (_dot_nt(doe, v3) - delta)
            db_o[e] += ds
            dsink.append(-jnp.sum(jnp.exp(sink_r[2 * p + e] - lse_e) * delta, axis=0, keepdims=True))
            dq = dq + _dot(ds.astype(CDT), jnp.where(sel, k3, 0))
            dk3 = dk3 + _dot(ds.T.astype(CDT), qe)
            dv3 = dv3 + _dot(pr.T.astype(CDT), doe)
        dq_o[...] = dq
        prev = pl.ds(pl.multiple_of(jnp.maximum(i - 1, 0) * BLK, BLK), BLK)
        cur = pl.ds(pl.multiple_of(i * BLK, BLK), BLK)
        dk_o[prev, :] += dk3[0:BLK]
        dk_o[cur, :] += dk3[BLK:2 * BLK]
        dk_o[0:BLK, :] += dk3[2 * BLK:]
        dv_o[prev, :] += dv3[0:BLK]
        dv_o[cur, :] += dv3[BLK:2 * BLK]
        dv_o[0:BLK, :] += dv3[2 * BLK:]
        dsk_o[0:1, :] += jnp.where(lo, dsink[0], dsink[1])

    qblk = pl.BlockSpec((BLK, 128), lambda p, i: (i, p))
    kvacc = pl.BlockSpec((None, t, 128), lambda p, i: (p // 2, 0, 0))
    bblk = pl.BlockSpec((2, BLK, 3 * BLK), lambda p, i: (p, 0, 0))
    return pl.pallas_call(
        body, name=name, grid=(NPAIR, t // BLK),
        in_specs=[pl.BlockSpec(memory_space=pltpu.SMEM), qblk] + _swa_kv_specs() + _swa_kv_specs()
        + [bblk, qblk, qblk, qblk],
        out_specs=[qblk, kvacc, kvacc, bblk, pl.BlockSpec((None, 8, 128), lambda p, i: (p, 0, 0))],
        out_shape=[_sds((t, 512), F32), _sds((2, t, 128), F32), _sds((2, t, 128), F32),
                   _sds((8, BLK, 3 * BLK), F32), _sds((NPAIR, 8, 128), F32)],
        compiler_params=_params(("arbitrary", "arbitrary")),
    )(sinks, qs, kse, kse, kse, vse, vse, vse, bias, o, lse, do)


def _adamw(w, g, m, v, name):
    r, c = w.shape
    tr = 128 if r % 128 == 0 else r

    def body(w_r, g_r, m_r, v_r, d_o, m_o, v_o):
        g_ = g_r[...]
        m_ = ADAM_B1 * m_r[...] + (1.0 - ADAM_B1) * g_
        v_ = ADAM_B2 * v_r[...] + (1.0 - ADAM_B2) * jnp.square(g_)
        m_hat = m_ / (1.0 - ADAM_B1 ** ADAM_STEP)
        v_hat = v_ / (1.0 - ADAM_B2 ** ADAM_STEP)
        d_o[...] = -ADAM_LR * (m_hat / (jnp.sqrt(v_hat) + ADAM_EPS) + ADAM_WD * w_r[...])
        m_o[...] = m_
        v_o[...] = v_

    blk = pl.BlockSpec((tr, c), lambda i: (i, 0))
    return pl.pallas_call(
        body, name=name, grid=(r // tr,),
        in_specs=[blk] * 4, out_specs=[blk] * 3, out_shape=[_sds((r, c), F32)] * 3,
        compiler_params=_params(("parallel",)),
    )(w, g, m, v)


def _pair_add(own, got, half_idx, name):
    tr = PACK_TILE
    nb = HALF_ROWS // tr

    def body(c_r, a_r, b_r, o_o):
        o_o[...] = (a_r[...].astype(F32) + b_r[...].astype(F32)).astype(CDT)

    grid_spec = pltpu.PrefetchScalarGridSpec(
        num_scalar_prefetch=1, grid=(4, nb),
        in_specs=[pl.BlockSpec((None, tr, 128), lambda s, i, c_r: (s, c_r[0] * nb + i, 0)),
                  pl.BlockSpec((None, tr, 128), lambda s, i, c_r: (s, i, 0))],
        out_specs=pl.BlockSpec((None, tr, 128), lambda s, i, c_r: (s, i, 0)),
    )
    return pl.pallas_call(
        body, name=name, grid_spec=grid_spec, out_shape=_sds((4, HALF_ROWS, 128), CDT),
        compiler_params=_params(("parallel", "parallel")),
    )(half_idx, own, got)


def _sum4(ps, got, shard_idx, name):
    tr = PACK_TILE

    def body(s_r, a_r, b_r, o_o):
        o_o[...] = ((a_r[...].astype(F32) + b_r[0].astype(F32)) + b_r[1].astype(F32)) + b_r[2].astype(F32)

    grid_spec = pltpu.PrefetchScalarGridSpec(
        num_scalar_prefetch=1, grid=(HALF_ROWS // tr,),
        in_specs=[pl.BlockSpec((None, tr, 128), lambda i, s_r: (s_r[0], i, 0)),
                  pl.BlockSpec((3, tr, 128), lambda i, s_r: (0, i, 0))],
        out_specs=pl.BlockSpec((tr, 128), lambda i, s_r: (i, 0)),
    )
    return pl.pallas_call(
        body, name=name, grid_spec=grid_spec, out_shape=_sds((HALF_ROWS, 128), F32),
        compiler_params=_params(("parallel",)),
    )(shard_idx, ps, got)


def _sum8(slots, name):
    def body(a_r, o_o):
        acc = a_r[0]
        for k in range(1, 8):
            acc = acc + a_r[k]
        o_o[...] = acc

    return pl.pallas_call(
        body, name=name, out_shape=_sds((SMALL_ROWS, 128), F32),
        in_specs=[pl.BlockSpec(memory_space=pltpu.VMEM)], out_specs=pl.BlockSpec(memory_space=pltpu.VMEM),
        compiler_params=_params(),
    )(slots)


def _place():
    x, y, c = lax.axis_index("x"), lax.axis_index("y"), lax.axis_index("c")
    chips = [(1 - x, y), (x, 1 - y), (1 - x, 1 - y)]
    return x, y, c, chips


def _remote(src, dst, send_sems, recv_sems, k, to):
    return pltpu.make_async_remote_copy(src_ref=src, dst_ref=dst, send_sem=send_sems.at[k], recv_sem=recv_sems.at[k],
                                        device_id=to, device_id_type=MESH_ID)


ANY = pl.BlockSpec(memory_space=pl.ANY)


def _gather_weights(wflat, mflat, name):
    def body(w_r, m_r, wall_o, mall_o, send_sems, recv_sems, loc_sems):
        x, y, c, chips = _place()
        s = 2 * x + y
        sib = (x, y, 1 - c)
        mine = pl.ds(pl.multiple_of(c * HALF_ROWS, 16), HALF_ROWS)
        other = pl.ds(pl.multiple_of((1 - c) * HALF_ROWS, 16), HALF_ROWS)
        loc = [pltpu.make_async_copy(w_r, wall_o.at[s], loc_sems.at[0]),
               pltpu.make_async_copy(m_r, mall_o.at[s], loc_sems.at[1])]
        for cp in loc:
            cp.start()
        sent = []
        for j, (cx, cy) in enumerate(chips):
            sent.append(_remote(w_r.at[mine], wall_o.at[s, mine], send_sems, recv_sems, j, (cx, cy, c)))
            sent.append(_remote(m_r, mall_o.at[s], send_sems, recv_sems, 6 + j, (cx, cy, c)))
        for cp in sent:
            cp.start()
        for j, (cx, cy) in enumerate(chips):
            sj = 2 * cx + cy
            _remote(w_r.at[mine], wall_o.at[sj, mine], send_sems, recv_sems, j, sib).wait_recv()
            fwd = _remote(wall_o.at[sj, mine], wall_o.at[sj, mine], send_sems, recv_sems, 3 + j, sib)
            fwd.start()
            sent.append(fwd)
        for j, (cx, cy) in enumerate(chips):
            sj = 2 * cx + cy
            _remote(w_r.at[other], wall_o.at[sj, other], send_sems, recv_sems, 3 + j, sib).wait_recv()
            _remote(m_r, mall_o.at[sj], send_sems, recv_sems, 6 + j, sib).wait_recv()
        for cp in sent:
            cp.wait_send()
        for cp in loc:
            cp.wait()

    return pl.pallas_call(
        body, name=name,
        out_shape=[_sds((4, SHARD_ROWS, 128), CDT), _sds((4, META_ROWS, 128), F32)],
        in_specs=[ANY, ANY], out_specs=[ANY, ANY],
        scratch_shapes=[pltpu.SemaphoreType.DMA((9,)), pltpu.SemaphoreType.DMA((9,)), pltpu.SemaphoreType.DMA((2,))],
    )(wflat, mflat)


def _swap_halves(gfl, gsm, name):
    def body(g_r, s_r, got_o, slots_o, send_sems, recv_sems, loc_sem):
        x, y, c, _ = _place()
        me = 4 * x + 2 * y + c
        theirs = pl.ds(pl.multiple_of((1 - c) * HALF_ROWS, 16), HALF_ROWS)
        loc = pltpu.make_async_copy(s_r, slots_o.at[me], loc_sem.at[0])
        loc.start()
        sent = [_remote(g_r.at[:, theirs, :], got_o, send_sems, recv_sems, 0, (x, y, 1 - c))]
        for k in range(1, 8):
            px, py, pc = x ^ (k >> 2), y ^ ((k >> 1) & 1), c ^ (k & 1)
            sent.append(_remote(s_r, slots_o.at[me], send_sems, recv_sems, k, (px, py, pc)))
        for cp in sent:
            cp.start()
        _remote(g_r.at[:, theirs, :], got_o, send_sems, recv_sems, 0, (x, y, 1 - c)).wait_recv()
        for k in range(1, 8):
            px, py, pc = x ^ (k >> 2), y ^ ((k >> 1) & 1), c ^ (k & 1)
            _remote(s_r, slots_o.at[4 * px + 2 * py + pc], send_sems, recv_sems, k, (px, py, pc)).wait_recv()
        for cp in sent:
            cp.wait_send()
        loc.wait()

    return pl.pallas_call(
        body, name=name,
        out_shape=[_sds((4, HALF_ROWS, 128), CDT), _sds((8, SMALL_ROWS, 128), F32)],
        in_specs=[ANY, ANY], out_specs=[ANY, ANY],
        scratch_shapes=[pltpu.SemaphoreType.DMA((8,)), pltpu.SemaphoreType.DMA((8,)), pltpu.SemaphoreType.DMA((1,))],
    )(gfl, gsm)


def _scatter_shards(ps, name):
    def body(p_r, got_o, send_sems, recv_sems):
        x, y, c, chips = _place()
        sent = [_remote(p_r.at[2 * cx + cy], got_o.at[j], send_sems, recv_sems, j, (cx, cy, c))
                for j, (cx, cy) in enumerate(chips)]
        for cp in sent:
            cp.start()
        for j in range(3):
            _remote(p_r.at[0], got_o.at[j], send_sems, recv_sems, j, (x, y, c)).wait_recv()
        for cp in sent:
            cp.wait_send()

    return pl.pallas_call(
        body, name=name, out_shape=_sds((3, HALF_ROWS, 128), CDT),
        in_specs=[ANY], out_specs=ANY,
        scratch_shapes=[pltpu.SemaphoreType.DMA((3,)), pltpu.SemaphoreType.DMA((3,))],
    )(ps)


def _join_halves(red_half, name):
    def body(h_r, red_o, send_sem, recv_sem, loc_sem):
        x, y, c, _ = _place()
        mine = pl.ds(pl.multiple_of(c * HALF_ROWS, 8), HALF_ROWS)
        other = pl.ds(pl.multiple_of((1 - c) * HALF_ROWS, 8), HALF_ROWS)
        loc = pltpu.make_async_copy(h_r, red_o.at[mine], loc_sem.at[0])
        loc.start()
        out = _remote(h_r, red_o.at[mine], send_sem, recv_sem, 0, (x, y, 1 - c))
        out.start()
        _remote(h_r, red_o.at[other], send_sem, recv_sem, 0, (x, y, 1 - c)).wait_recv()
        out.wait_send()
        loc.wait()

    return pl.pallas_call(
        body, name=name, out_shape=_sds((SHARD_ROWS, 128), F32),
        in_specs=[ANY], out_specs=ANY,
        scratch_shapes=[pltpu.SemaphoreType.DMA((1,)), pltpu.SemaphoreType.DMA((1,)), pltpu.SemaphoreType.DMA((1,))],
    )(red_half)


def _pack_shard(local):
    parts = [local[nm][l].reshape(-1) for l in range(2) for nm, _, _ in SHARD_ITEMS]
    parts.append(jnp.zeros(((SHARD_ROWS - RAW_ROWS) * 128,), parts[0].dtype))
    return jnp.concatenate(parts).reshape(SHARD_ROWS, 128)


def _unpack_full(wall):
    flat = wall.reshape(4, SHARD_ROWS * 128)
    off = 0
    layers = []
    for l in range(2):
        ws = {}
        for nm, (r, c), kind in SHARD_ITEMS:
            piece = flat[:, off:off + r * c].reshape(4, r, c)
            off += r * c
            if kind == "row":
                ws[nm] = piece.reshape(4 * r, c)
            else:
                ws[nm] = jnp.concatenate([piece[s] for s in range(4)], axis=1)
        layers.append(ws)
    return layers


def _mix_cols(w):
    return jnp.concatenate([w[:, 2312:4360], w[:, 0:1536], w[:, 1544:2312], w[:, 1536:1544],
                            jnp.zeros((w.shape[0], DP - D_IN), w.dtype)], axis=1)


def _unmix_cols(w):
    return jnp.concatenate([w[:, QA:QA + 1536], w[:, FA:FA + 8], w[:, QB:QB + 768], w[:, GA:GA + 2048]], axis=1)


def _pack_grads(grads):
    shards = []
    for s in range(4):
        parts = []
        for l in range(2):
            for nm, (r, c), kind in SHARD_ITEMS:
                g = grads[l][nm]
                parts.append((g[s * r:(s + 1) * r] if kind == "row" else g[:, s * c:(s + 1) * c]).reshape(-1))
        parts.append(jnp.zeros(((SHARD_ROWS - RAW_ROWS) * 128,), parts[0].dtype))
        shards.append(jnp.concatenate(parts).reshape(SHARD_ROWS, 128))
    return jnp.stack(shards)


def _unpack_shard(red):
    flat = red.reshape(-1)
    off = 0
    out = {nm: [] for nm, _, _ in SHARD_ITEMS}
    for l in range(2):
        for nm, (r, c), _ in SHARD_ITEMS:
            out[nm].append(flat[off:off + r * c].reshape(r, c))
            off += r * c
    return {nm: jnp.stack(v) for nm, v in out.items()}


def _rows128(a, rows):
    flat = a.reshape(-1)
    return jnp.pad(flat, (0, rows * 128 - flat.shape[0])).reshape(rows, 128)


SMALL_ITEMS = (("rel_bias_table", 2), ("ffn1_norm", 16), ("mix_norm", 16), ("ffn2_norm", 16), ("forget_bias", 1),
               ("fox_q_norm", 1), ("fox_k_norm", 1), ("swa_q_norm", 1), ("swa_k_norm", 1), ("swa_sinks", 1))
SMALL_ADAM_ROWS = 96


def _layer_fwd(h, lw, l):
    sv = {"h0": h}
    a, sv["a1t"] = _rms_fwd(h, lw["ffn1_norm"], f"rms_fwd_a{l}")
    sv["gu1"], s, sv["s1t"] = _ffn_in(a, lw["ffn1_w_in"], f"ffn_in_a{l}")
    h = _mm_res(s, lw["ffn1_w_out"], h, 0.5, f"ffn_out_a{l}")
    sv["h1"] = h
    a, sv["amt"] = _rms_fwd(h, lw["mix_norm"], f"rms_fwd_m{l}")
    proj = _mm(a, lw["w_mix"], F32, _row_tile(h.shape[0]), 640, f"proj{l}")
    sv["proj"] = proj
    qf, kf, vf, qs, kse, vse, c, ct = _qknorm_fwd(proj, lw["gfq"], lw["gfk"], lw["gsq"], lw["gsk"], lw["fb"],
                                                   f"qknorm_fwd{l}")
    ofox, lse_f = _fox_fwd(qf, kf, vf, c, ct, f"fox_fwd{l}")
    oswa, lse_s = _swa_fwd(qs, kse, vse, lw["bias"], lw["sinks"], f"swa_fwd{l}")
    sv.update(qf=qf, kf=kf, vf=vf, qs=qs, kse=kse, vse=vse, c=c, ct=ct, ofox=ofox, oswa=oswa, lse_f=lse_f, lse_s=lse_s)
    y, sv["yt"], sv["pf"], sv["ps"], sv["oft"], sv["ost"] = _gate_fwd(ofox, oswa, lw["w_branch_fox"], lw["w_branch_swa"],
                                                                     proj, f"gate_fwd{l}")
    h = _mm_res(y, lw["w_out"], h, 1.0, f"mix_out{l}")
    sv["h2"] = h
    a, sv["a2t"] = _rms_fwd(h, lw["ffn2_norm"], f"rms_fwd_b{l}")
    sv["gu2"], s, sv["s2t"] = _ffn_in(a, lw["ffn2_w_in"], f"ffn_in_b{l}")
    h = _mm_res(s, lw["ffn2_w_out"], h, 0.5, f"ffn_out_b{l}")
    return h, sv


def _ffn_bwd(dh, dhb, h_in, at, gu, st, norm, w_in, w_out, tag):
    dgu = _ffn_bwd_mid(dhb, w_out, gu, f"ffn_bwd_mid_{tag}")
    d_w_out = _mm(st, dhb, CDT, 256, 512, f"dw_ffn_out_{tag}", scale=0.5)
    da = _ffn_bwd_in(dgu, w_in, f"ffn_bwd_in_{tag}")
    d_w_in = _mm(at, dgu, CDT, 512, 256, f"dw_ffn_in_{tag}")
    dh, dhb, dg = _rms_bwd(da, h_in, norm, dh, f"rms_bwd_{tag}")
    return dh, dhb, d_w_out, d_w_in, dg


def _layer_bwd(dh, dhb, sv, lw, l):
    g = {}
    dh, dhb, g["ffn2_w_out"], g["ffn2_w_in"], g["ffn2_norm"] = _ffn_bwd(
        dh, dhb, sv["h2"], sv["a2t"], sv["gu2"], sv["s2t"], lw["ffn2_norm"], lw["ffn2_w_in"], lw["ffn2_w_out"], f"b{l}")
    dy = _mm_nt(dhb, lw["w_out"], f"d_y{l}")
    g["w_out"] = _mm(sv["yt"], dhb, CDT, 512, 512, f"dw_out{l}")
    dpf, dps, dga, dgb = _gate_bwd(dy, sv["pf"], sv["ps"], sv["proj"], f"gate_bwd{l}")
    do_f = _mm_nt(dpf, lw["w_branch_fox"], f"d_ofox{l}")
    do_s = _mm_nt(dps, lw["w_branch_swa"], f"d_oswa{l}")
    g["w_branch_fox"] = _mm(sv["oft"], dpf, CDT, 512, 512, f"dw_bfox{l}")
    g["w_branch_swa"] = _mm(sv["ost"], dps, CDT, 512, 512, f"dw_bswa{l}")
    dqf, dcq, dkf, dvf, dck = _fox_bwd(sv["qf"], sv["kf"], sv["vf"], sv["c"], sv["ct"], sv["ofox"], sv["lse_f"], do_f,
                                       f"fox_bwd{l}")
    dqs, dkse, dvse, dbias, dsk = _swa_bwd(sv["qs"], sv["kse"], sv["vse"], lw["bias"], lw["sinks"], sv["oswa"],
                                           sv["lse_s"], do_s, f"swa_bwd{l}")
    dproj, dgn = _qknorm_bwd(sv["proj"], dqf, dkf, dvf, dqs, dkse, dvse, dcq, dck, dga, dgb,
                             lw["gfq"], lw["gfk"], lw["gsq"], lw["gsk"], lw["fb"], f"qknorm_bwd{l}")
    dam = _mm_nt(dproj, lw["w_mix"], f"d_am{l}")
    g["w_mix"] = _mm(sv["amt"], dproj, CDT, 512, 640, f"dw_mix{l}")
    dh, dhb, g["mix_norm"] = _rms_bwd(dam, sv["h1"], lw["mix_norm"], dh, f"rms_bwd_m{l}")
    g["dbias"], g["dsk"], g["dgn"] = dbias, dsk, dgn
    dh, dhb, g["ffn1_w_out"], g["ffn1_w_in"], g["ffn1_norm"] = _ffn_bwd(
        dh, dhb, sv["h0"], sv["a1t"], sv["gu1"], sv["s1t"], lw["ffn1_norm"], lw["ffn1_w_in"], lw["ffn1_w_out"], f"a{l}")
    return dh, dhb, g


def kernel(x, meta_tokens, rel_bias_table, ffn1_norm, ffn1_w_in, ffn1_w_out, mix_norm, w_in, forget_bias, fox_q_norm, fox_k_norm, swa_q_norm, swa_k_norm, swa_sinks, w_branch_fox, w_branch_swa, w_out, ffn2_norm, ffn2_w_in, ffn2_w_out, loss_target, m_meta_tokens, m_rel_bias_table, m_ffn1_norm, m_ffn1_w_in, m_ffn1_w_out, m_mix_norm, m_w_in, m_forget_bias, m_fox_q_norm, m_fox_k_norm, m_swa_q_norm, m_swa_k_norm, m_swa_sinks, m_w_branch_fox, m_w_branch_swa, m_w_out, m_ffn2_norm, m_ffn2_w_in, m_ffn2_w_out, v_meta_tokens, v_rel_bias_table, v_ffn1_norm, v_ffn1_w_in, v_ffn1_w_out, v_mix_norm, v_w_in, v_forget_bias, v_fox_q_norm, v_fox_k_norm, v_swa_q_norm, v_swa_k_norm, v_swa_sinks, v_w_branch_fox, v_w_branch_swa, v_w_out, v_ffn2_norm, v_ffn2_w_in, v_ffn2_w_out):
    names = ["meta_tokens", "rel_bias_table", "ffn1_norm", "ffn1_w_in", "ffn1_w_out", "mix_norm", "w_in", "forget_bias",
             "fox_q_norm", "fox_k_norm", "swa_q_norm", "swa_k_norm", "swa_sinks", "w_branch_fox", "w_branch_swa", "w_out",
             "ffn2_norm", "ffn2_w_in", "ffn2_w_out"]
    w = dict(zip(names, [meta_tokens, rel_bias_table, ffn1_norm, ffn1_w_in, ffn1_w_out, mix_norm, w_in, forget_bias,
                         fox_q_norm, fox_k_norm, swa_q_norm, swa_k_norm, swa_sinks, w_branch_fox, w_branch_swa, w_out,
                         ffn2_norm, ffn2_w_in, ffn2_w_out]))
    m = dict(zip(names, [m_meta_tokens, m_rel_bias_table, m_ffn1_norm, m_ffn1_w_in, m_ffn1_w_out, m_mix_norm, m_w_in,
                         m_forget_bias, m_fox_q_norm, m_fox_k_norm, m_swa_q_norm, m_swa_k_norm, m_swa_sinks,
                         m_w_branch_fox, m_w_branch_swa, m_w_out, m_ffn2_norm, m_ffn2_w_in, m_ffn2_w_out]))
    v = dict(zip(names, [v_meta_tokens, v_rel_bias_table, v_ffn1_norm, v_ffn1_w_in, v_ffn1_w_out, v_mix_norm, v_w_in,
                         v_forget_bias, v_fox_q_norm, v_fox_k_norm, v_swa_q_norm, v_swa_k_norm, v_swa_sinks,
                         v_w_branch_fox, v_w_branch_swa, v_w_out, v_ffn2_norm, v_ffn2_w_in, v_ffn2_w_out]))
    xi, yi, ci = lax.axis_index("x"), lax.axis_index("y"), lax.axis_index("c")
    shard = 2 * xi + yi
    seq = x.shape[1]
    t = seq + BLK

    wflat = _pack_shard({nm: w[nm].astype(CDT) for nm, _, _ in SHARD_ITEMS})
    wall, mall = _gather_weights(wflat, meta_tokens.reshape(META_ROWS, 128), "gather_weights")
    full = _unpack_full(wall)
    meta_full = jnp.concatenate([mall[s].reshape(N_META, 256) for s in range(4)], axis=1)
    bias = _bias_fwd(rel_bias_table, "bias_fwd")
    lws = []
    for l in range(2):
        lw = dict(full[l])
        lw["w_mix"] = _mix_cols(lw.pop("w_in"))
        for nm in ("ffn1_norm", "mix_norm", "ffn2_norm"):
            lw[nm] = w[nm][l].reshape(1, D)
        lw["gfq"] = jnp.tile(fox_q_norm[l], 8).reshape(1, 512)
        lw["gfk"] = jnp.tile(fox_k_norm[l], 8).reshape(1, 512)
        lw["gsq"] = jnp.tile(swa_q_norm[l], 8).reshape(1, 512)
        lw["gsk"] = jnp.tile(swa_k_norm[l], 2).reshape(1, 128)
        lw["fb"] = jnp.pad(forget_bias[l], (0, 120)).reshape(1, 128)
        lw["sinks"] = swa_sinks[l]
        lw["bias"] = bias
        lws.append(lw)

    h = jnp.concatenate([jnp.zeros((PAD, D), F32), meta_full, x[0]], axis=0)
    saved = []
    for l in range(2):
        h, sv = _layer_fwd(h, lws[l], l)
        saved.append(sv)
    dh, dhb, lacc = _loss(h, loss_target[0], "loss")
    loss = lax.psum(lacc[0, 0], ("x", "y", "c"))

    grads = [None, None]
    for l in (1, 0):
        dh, dhb, grads[l] = _layer_bwd(dh, dhb, saved[l], lws[l], l)
    grad_x = dh[BLK:].reshape(1, seq, D)
    for l in range(2):
        grads[l]["w_in"] = _unmix_cols(grads[l].pop("w_mix"))
    dtab = _bias_bwd(grads[0]["dbias"] + grads[1]["dbias"], "bias_bwd")

    small = [dh[PAD:BLK].reshape(128, 128), _rows128(dtab[:, :N_BUCKETS].T, 2)]
    for nm in ("ffn1_norm", "mix_norm", "ffn2_norm"):
        small.append(jnp.stack([grads[l][nm][0] for l in range(2)]).reshape(16, 128))
    small.append(_rows128(jnp.stack([grads[l]["dgn"][4, :8] for l in range(2)]), 1))
    for row in range(4):
        small.append(jnp.stack([grads[l]["dgn"][row, :HD] for l in range(2)]).reshape(1, 128))
    dsk = [grads[l]["dsk"][:, 0, :] for l in range(2)]
    small.append(_rows128(jnp.stack([jnp.stack([d[:, 0], d[:, HD]], axis=1).reshape(8) for d in dsk]), 1))
    gsm = jnp.concatenate(small, axis=0)
    gsm = jnp.pad(gsm, ((0, SMALL_ROWS - gsm.shape[0]), (0, 0)))

    gfl = _pack_grads(grads)
    got, slots = _swap_halves(gfl, gsm, "swap_halves")
    ps = _pair_add(gfl, got, ci.reshape(1).astype(jnp.int32), "pair_add")
    got3 = _scatter_shards(ps, "scatter_shards")
    red_half = _sum4(ps, got3, shard.reshape(1).astype(jnp.int32), "sum4")
    red = _join_halves(red_half, "join_halves")
    gs = _sum8(slots, "sum8")
    big = _unpack_shard(red)

    g_out = dict(big)
    g_out["meta_tokens"] = lax.dynamic_slice(gs[0:128].reshape(N_META, D), (0, shard * 256), (N_META, 256))
    off = 128
    for nm, rows in SMALL_ITEMS:
        n = w[nm].size
        g_out[nm] = gs[off:off + rows].reshape(-1)[:n].reshape(w[nm].shape)
        off += rows

    delta, new_m, new_v = {}, {}, {}
    for nm, (r, c), _ in SHARD_ITEMS:
        two = lambda a: a.reshape(2 * r, c)
        d_, m_, v_ = _adamw(two(w[nm]), two(g_out[nm]), two(m[nm]), two(v[nm]), f"adamw_{nm}")
        delta[nm], new_m[nm], new_v[nm] = (a.reshape(2, r, c) for a in (d_, m_, v_))
    small_names = ["meta_tokens"] + [nm for nm, _ in SMALL_ITEMS]
    small_rows = [META_ROWS] + [rows for _, rows in SMALL_ITEMS]

    def pack_small(src):
        buf = jnp.concatenate([_rows128(src[nm], rows) for nm, rows in zip(small_names, small_rows)], axis=0)
        return jnp.pad(buf, ((0, SMALL_ADAM_ROWS - buf.shape[0]), (0, 0)))

    d_, m_, v_ = _adamw(pack_small(w), pack_small(g_out), pack_small(m), pack_small(v), "adamw_small")
    off = 0
    for nm, rows in zip(small_names, small_rows):
        n = w[nm].size
        for dst, src in ((delta, d_), (new_m, m_), (new_v, v_)):
            dst[nm] = src[off:off + rows].reshape(-1)[:n].reshape(w[nm].shape)
        off += rows

    return (loss, grad_x, *[g_out[n] for n in names], *[delta[n] for n in names],
            *[new_m[n] for n in names], *[new_v[n] for n in names])
```

```python
import math

import numpy as np
import jax
import jax.numpy as jnp
from jax import lax
from jax.experimental import pallas as pl
from jax.experimental.pallas import tpu as pltpu

D = 1024
F = 2816
HD = 64
NPAIR = 4
N_META = 16
BLK = 128
PAD = BLK - N_META
EPS = 1e-6
NEG = -1e30
N_BUCKETS = 32
GA, GB, QA, KA, VA, QB, KB, VB, FA, DP = 0, 1024, 2048, 2560, 3072, 3584, 4096, 4224, 4352, 4480
D_IN = 4360
CDT = jnp.bfloat16
F32 = jnp.float32
VMEM_LIMIT = 48 * 1024 * 1024
MESH_ID = pl.DeviceIdType.MESH

ADAM_LR, ADAM_B1, ADAM_B2, ADAM_EPS, ADAM_WD, ADAM_STEP = 0.001, 0.9, 0.999, 1e-08, 0.01, 10

SHARD_ITEMS = (
    ("ffn1_w_in", (1024, 1408), "col"),
    ("ffn1_w_out", (704, 1024), "row"),
    ("w_in", (1024, 1090), "col"),
    ("w_branch_fox", (512, 256), "col"),
    ("w_branch_swa", (512, 256), "col"),
    ("w_out", (256, 1024), "row"),
    ("ffn2_w_in", (1024, 1408), "col"),
    ("ffn2_w_out", (704, 1024), "row"),
)
LAYER_RAW = sum(r * c for _, (r, c), _ in SHARD_ITEMS) // 128
PACK_TILE = 5840
HALF_ROWS = 4 * PACK_TILE
LAYER_ROWS = 2 * HALF_ROWS
SMALL_ROWS = 192
META_ROWS = 32


def _row_tile(t):
    return 384 if t % 384 == 0 else 128


def _dot(a, b):
    return jnp.dot(a, b, preferred_element_type=F32)


def _dot_nt(a, b):
    return lax.dot_general(a, b, (((1,), (1,)), ((), ())), preferred_element_type=F32)


def _dot_hi(a, b):
    return jnp.dot(a, b, preferred_element_type=F32, precision=lax.Precision.HIGHEST)


def _sigmoid(x):
    return 1.0 / (1.0 + jnp.exp(-x))


def _iota(shape, dim):
    return lax.broadcasted_iota(jnp.int32, shape, dim)


def _params(sem=None):
    return pltpu.CompilerParams(dimension_semantics=sem, vmem_limit_bytes=VMEM_LIMIT)


def _sds(shape, dtype):
    return jax.ShapeDtypeStruct(shape, dtype)


def _rms_fwd(h, g, name):
    t = h.shape[0]
    tm = _row_tile(t)

    def body(h_ref, g_ref, a_ref, at_ref):
        x = h_ref[...]
        ms = jnp.mean(x * x, axis=-1, keepdims=True)
        a = x * lax.rsqrt(ms + EPS) * g_ref[...]
        a_ref[...] = a.astype(CDT)
        at_ref[...] = a.T.astype(CDT)

    return pl.pallas_call(
        body, name=name, grid=(t // tm,),
        in_specs=[pl.BlockSpec((tm, D), lambda i: (i, 0)), pl.BlockSpec((1, D), lambda i: (0, 0))],
        out_specs=[pl.BlockSpec((tm, D), lambda i: (i, 0)), pl.BlockSpec((D, tm), lambda i: (0, i))],
        out_shape=[_sds((t, D), CDT), _sds((D, t), CDT)],
        compiler_params=_params(("parallel",)),
    )(h, g)


def _rms_bwd(da, h, g, dres, name):
    t = h.shape[0]
    tm = _row_tile(t)

    def body(da_ref, h_ref, g_ref, dr_ref, dh_ref, dhb_ref, dg_ref):
        i = pl.program_id(0)
        x = h_ref[...]
        da_ = da_ref[...]
        r = lax.rsqrt(jnp.mean(x * x, axis=-1, keepdims=True) + EPS)
        xh = x * r
        day = da_ * g_ref[...]
        dx = r * (day - xh * jnp.mean(day * xh, axis=-1, keepdims=True))
        dh = dr_ref[...] + dx
        dh_ref[...] = dh
        dhb_ref[...] = dh.astype(CDT)

        @pl.when(i == 0)
        def _():
            dg_ref[...] = jnp.zeros(dg_ref.shape, F32)

        dg_ref[0:1, :] += jnp.sum(da_ * xh, axis=0, keepdims=True)

    row = pl.BlockSpec((tm, D), lambda i: (i, 0))
    return pl.pallas_call(
        body, name=name, grid=(t // tm,),
        in_specs=[row, row, pl.BlockSpec((1, D), lambda i: (0, 0)), row],
        out_specs=[row, row, pl.BlockSpec((8, D), lambda i: (0, 0))],
        out_shape=[_sds((t, D), F32), _sds((t, D), CDT), _sds((8, D), F32)],
        compiler_params=_params(("arbitrary",)),
    )(da, h, g, dres)


def _ffn_in(a, w_in, name):
    t = a.shape[0]
    tm = _row_tile(t)
    tn = 256
    nj = F // tn

    def body(a_ref, wg_ref, wu_ref, gu_ref, s_ref, st_ref):
        a_ = a_ref[...]
        g = _dot(a_, wg_ref[...])
        u = _dot(a_, wu_ref[...])
        s = g * _sigmoid(g) * u
        gu_ref[0] = g.astype(CDT)
        gu_ref[1] = u.astype(CDT)
        s_ref[...] = s.astype(CDT)
        st_ref[...] = s.T.astype(CDT)

    return pl.pallas_call(
        body, name=name, grid=(t // tm, nj),
        in_specs=[pl.BlockSpec((tm, D), lambda i, j: (i, 0)),
                  pl.BlockSpec((D, tn), lambda i, j: (0, j)),
                  pl.BlockSpec((D, tn), lambda i, j: (0, j + nj))],
        out_specs=[pl.BlockSpec((2, tm, tn), lambda i, j: (0, i, j)),
                   pl.BlockSpec((tm, tn), lambda i, j: (i, j)),
                   pl.BlockSpec((tn, tm), lambda i, j: (j, i))],
        out_shape=[_sds((2, t, F), CDT), _sds((t, F), CDT), _sds((F, t), CDT)],
        compiler_params=_params(("parallel", "parallel")),
    )(a, w_in, w_in)


def _mm_res(a, b, res, scale, name):
    t, k = a.shape
    n = b.shape[1]
    tm = _row_tile(t)
    tn = 512

    def body(a_ref, b_ref, r_ref, o_ref):
        o_ref[...] = r_ref[...] + scale * _dot(a_ref[...], b_ref[...])

    return pl.pallas_call(
        body, name=name, grid=(t // tm, n // tn),
        in_specs=[pl.BlockSpec((tm, k), lambda i, j: (i, 0)),
                  pl.BlockSpec((k, tn), lambda i, j: (0, j)),
                  pl.BlockSpec((tm, tn), lambda i, j: (i, j))],
        out_specs=pl.BlockSpec((tm, tn), lambda i, j: (i, j)),
        out_shape=_sds((t, n), F32),
        compiler_params=_params(("parallel", "parallel")),
    )(a, b, res)


def _mm(a, b, out_dtype, tm, tn, name, scale=1.0):
    m, k = a.shape
    if b.ndim == 3:
        nh = b.shape[2] // tn
        n = 2 * b.shape[2]
        b_spec = pl.BlockSpec((None, k, tn), lambda i, j: (j // nh, 0, j % nh))
    else:
        n = b.shape[1]
        b_spec = pl.BlockSpec((k, tn), lambda i, j: (0, j))

    def body(a_ref, b_ref, o_ref):
        o_ref[...] = (scale * _dot(a_ref[...], b_ref[...])).astype(out_dtype)

    return pl.pallas_call(
        body, name=name, grid=(m // tm, n // tn),
        in_specs=[pl.BlockSpec((tm, k), lambda i, j: (i, 0)), b_spec],
        out_specs=pl.BlockSpec((tm, tn), lambda i, j: (i, j)),
        out_shape=_sds((m, n), out_dtype),
        compiler_params=_params(("parallel", "parallel")),
    )(a, b)


def _mm_nt(a, b, name):
    m, n = a.shape
    k = b.shape[0]
    tm = _row_tile(m)
    tk = 512

    def body(a_ref, b_ref, o_ref):
        o_ref[...] = _dot_nt(a_ref[...], b_ref[...])

    return pl.pallas_call(
        body, name=name, grid=(m // tm, k // tk),
        in_specs=[pl.BlockSpec((tm, n), lambda i, j: (i, 0)), pl.BlockSpec((tk, n), lambda i, j: (j, 0))],
        out_specs=pl.BlockSpec((tm, tk), lambda i, j: (i, j)),
        out_shape=_sds((m, k), F32),
        compiler_params=_params(("parallel", "parallel")),
    )(a, b)


def _ffn_bwd_mid(dhb, w_out, gu, name):
    t = dhb.shape[0]
    tm = _row_tile(t)
    tn = 256

    def body(dh_ref, w_ref, gu_ref, o_ref):
        ds = 0.5 * _dot_nt(dh_ref[...], w_ref[...])
        g = gu_ref[0].astype(F32)
        u = gu_ref[1].astype(F32)
        sg = _sigmoid(g)
        o_ref[0] = (ds * u * (sg * (1.0 + g * (1.0 - sg)))).astype(CDT)
        o_ref[1] = (ds * (g * sg)).astype(CDT)

    return pl.pallas_call(
        body, name=name, grid=(t // tm, F // tn),
        in_specs=[pl.BlockSpec((tm, D), lambda i, j: (i, 0)),
                  pl.BlockSpec((tn, D), lambda i, j: (j, 0)),
                  pl.BlockSpec((2, tm, tn), lambda i, j: (0, i, j))],
        out_specs=pl.BlockSpec((2, tm, tn), lambda i, j: (0, i, j)),
        out_shape=_sds((2, t, F), CDT),
        compiler_params=_params(("parallel", "parallel")),
    )(dhb, w_out, gu)


def _ffn_bwd_in(dgu, w_in, name):
    t = dgu.shape[1]
    tm = _row_tile(t)
    tk = 512

    def body(dg_ref, wg_ref, wu_ref, o_ref):
        o_ref[...] = _dot_nt(dg_ref[0], wg_ref[...]) + _dot_nt(dg_ref[1], wu_ref[...])

    return pl.pallas_call(
        body, name=name, grid=(t // tm, D // tk),
        in_specs=[pl.BlockSpec((2, tm, F), lambda i, j: (0, i, 0)),
                  pl.BlockSpec((tk, F), lambda i, j: (j, 0)),
                  pl.BlockSpec((tk, F), lambda i, j: (j, 1))],
        out_specs=pl.BlockSpec((tm, tk), lambda i, j: (i, j)),
        out_shape=_sds((t, D), F32),
        compiler_params=_params(("parallel", "parallel")),
    )(dgu, w_in, w_in)


def _loss(h, target, name):
    t = h.shape[0]

    def body(h_ref, t_ref, dh_ref, dhb_ref, l_ref):
        i = pl.program_id(0)

        @pl.when(i == 0)
        def _():
            l_ref[...] = jnp.zeros(l_ref.shape, F32)
            dh_ref[...] = jnp.zeros(dh_ref.shape, F32)
            dhb_ref[...] = jnp.zeros(dhb_ref.shape, CDT)

        @pl.when(i > 0)
        def _():
            err = h_ref[...] - t_ref[...]
            l_ref[...] += (0.5 / D) * jnp.sum(err * err)
            d = err * (1.0 / D)
            dh_ref[...] = d
            dhb_ref[...] = d.astype(CDT)

    row = pl.BlockSpec((BLK, D), lambda i: (i, 0))
    return pl.pallas_call(
        body, name=name, grid=(t // BLK,),
        in_specs=[row, pl.BlockSpec((BLK, D), lambda i: (jnp.maximum(i - 1, 0), 0))],
        out_specs=[row, row, pl.BlockSpec((8, 128), lambda i: (0, 0))],
        out_shape=[_sds((t, D), F32), _sds((t, D), CDT), _sds((8, 128), F32)],
        compiler_params=_params(("arbitrary",)),
    )(h, target)


def _block_diag():
    return (_iota((128, 128), 0) // HD == _iota((128, 128), 1) // HD).astype(F32)


def _dup_halves(x, lo):
    sw = pltpu.roll(x, 64, 1)
    return jnp.where(lo, x, sw), jnp.where(lo, sw, x)


def _qknorm_fwd(proj, gfq, gfk, gsq, gsk, fb, name):
    t = proj.shape[0]
    tm = _row_tile(t)

    def body(qa, ka, va, qb, kb, vb, fa, gfq_r, gfk_r, gsq_r, gsk_r, fb_r,
             qf_o, kf_o, vf_o, qs_o, kse_o, vse_o, c_o, ct_o, carry):
        i = pl.program_id(0)
        bd = _block_diag()
        lane = _iota((1, 128), 1)
        lo = lane < HD

        def hnorm(x, g):
            ms = _dot_hi(x * x, bd) * (1.0 / HD)
            return x * lax.rsqrt(ms + EPS) * g

        for ch in range(4):
            sl = slice(128 * ch, 128 * (ch + 1))
            qf_o[:, sl] = (hnorm(qa[:, sl], gfq_r[:, sl]) * 0.125).astype(CDT)
            kf_o[:, sl] = hnorm(ka[:, sl], gfk_r[:, sl]).astype(CDT)
            qs_o[:, sl] = (hnorm(qb[:, sl], gsq_r[:, sl]) * 0.125).astype(CDT)
        vf_o[...] = va[...].astype(CDT)
        k0, k1 = _dup_halves(hnorm(kb[...], gsk_r[...]), lo)
        kse_o[0] = k0.astype(CDT)
        kse_o[1] = k1.astype(CDT)
        v0, v1 = _dup_halves(vb[...], lo)
        vse_o[0] = v0.astype(CDT)
        vse_o[1] = v1.astype(CDT)

        z = fa[...] + fb_r[...]
        lf = jnp.minimum(z, 0.0) - jnp.log(1.0 + jnp.exp(-jnp.abs(z)))
        lf = jnp.where(lane < 8, lf, 0.0)
        ltri = (_iota((tm, tm), 1) <= _iota((tm, tm), 0)).astype(F32)

        @pl.when(i == 0)
        def _():
            carry[...] = jnp.zeros(carry.shape, F32)

        c = _dot_hi(ltri, lf) + carry[0:1, :]
        carry[0:1, :] = c[tm - 1:tm, :]
        c_o[...] = c
        ct_o[...] = c.T[0:8, :]

    def col(width, off):
        return pl.BlockSpec((tm, width), lambda i: (i, off // width))

    def vec(width):
        return pl.BlockSpec((1, width), lambda i: (0, 0))

    return pl.pallas_call(
        body, name=name, grid=(t // tm,),
        in_specs=[col(512, QA), col(512, KA), col(512, VA), col(512, QB), col(128, KB), col(128, VB), col(128, FA),
                  vec(512), vec(512), vec(512), vec(128), vec(128)],
        out_specs=[pl.BlockSpec((tm, 512), lambda i: (i, 0))] * 4
        + [pl.BlockSpec((2, tm, 128), lambda i: (0, i, 0))] * 2
        + [pl.BlockSpec((tm, 128), lambda i: (i, 0)), pl.BlockSpec((8, tm), lambda i: (0, i))],
        out_shape=[_sds((t, 512), CDT)] * 4 + [_sds((2, t, 128), CDT)] * 2 + [_sds((t, 128), F32), _sds((8, t), F32)],
        scratch_shapes=[pltpu.VMEM((8, 128), F32)],
        compiler_params=_params(("arbitrary",)),
    )(proj, proj, proj, proj, proj, proj, proj, gfq, gfk, gsq, gsk, fb)


def _qknorm_bwd(proj, dqf, dkf, dvf, dqs, dkse, dvse, dcq, dck, dga, dgb, gfq, gfk, gsq, gsk, fb, name):
    t = proj.shape[0]
    tm = _row_tile(t)
    nt = t // tm

    def body(qa, ka, qb, kb, fa, dqf_r, dkf_r, dvf_r, dqs_r, dkse_r, dvse_r, dcq_r, dck_r, dga_r, dgb_r,
             gfq_r, gfk_r, gsq_r, gsk_r, fb_r, dp_o, dgn_o, carry, acc):
        i = pl.program_id(0)
        bd = _block_diag()
        lane = _iota((1, 128), 1)
        lo = lane < HD

        @pl.when(i == 0)
        def _():
            carry[...] = jnp.zeros(carry.shape, F32)
            acc[...] = jnp.zeros(acc.shape, F32)

        def hnorm_bwd(x, g, dy):
            r = lax.rsqrt(_dot_hi(x * x, bd) * (1.0 / HD) + EPS)
            xh = x * r
            day = dy * g
            dx = r * (day - xh * (_dot_hi(day * xh, bd) * (1.0 / HD)))
            return dx, jnp.sum(dy * xh, axis=0, keepdims=True)

        for ch in range(4):
            sl = slice(128 * ch, 128 * (ch + 1))
            dx, dg = hnorm_bwd(qa[:, sl], gfq_r[:, sl], dqf_r[:, sl] * 0.125)
            dp_o[:, QA + 128 * ch:QA + 128 * (ch + 1)] = dx.astype(CDT)
            acc[0:1, sl] += dg
            dx, dg = hnorm_bwd(ka[:, sl], gfk_r[:, sl], dkf_r[:, sl])
            dp_o[:, KA + 128 * ch:KA + 128 * (ch + 1)] = dx.astype(CDT)
            acc[1:2, sl] += dg
            dx, dg = hnorm_bwd(qb[:, sl], gsq_r[:, sl], dqs_r[:, sl] * 0.125)
            dp_o[:, QB + 128 * ch:QB + 128 * (ch + 1)] = dx.astype(CDT)
            acc[2:3, sl] += dg
        dp_o[:, VA:VA + 512] = dvf_r[...].astype(CDT)
        dp_o[:, GA:GA + D] = dga_r[...]
        dp_o[:, GB:GB + D] = dgb_r[...]

        def fold(x):
            e0 = x[0]
            e1 = x[1]
            return jnp.where(lo, e0 + pltpu.roll(e0, 64, 1), e1 + pltpu.roll(e1, 64, 1))

        dx, dg = hnorm_bwd(kb[...], gsk_r[...], fold(dkse_r))
        dp_o[:, KB:KB + 128] = dx.astype(CDT)
        acc[3:4, 0:128] += dg
        dp_o[:, VB:VB + 128] = fold(dvse_r).astype(CDT)

        rr = _iota((512, 128), 0)
        hh = _iota((512, 128), 1)
        sel = ((rr == (hh >> 1) * 128 + (hh & 1) * HD) & (hh < 8)).astype(F32)
        dcs = _dot_hi(dcq_r[...] - dck_r[...], sel)
        utri = (_iota((tm, tm), 1) >= _iota((tm, tm), 0)).astype(F32)
        dlf = _dot_hi(utri, dcs) + carry[0:1, :]
        carry[0:1, :] = dlf[0:1, :]
        z = fa[...] + fb_r[...]
        dfa = jnp.where(lane < 8, dlf * _sigmoid(-z), 0.0)
        dp_o[:, FA:FA + 128] = dfa.astype(CDT)
        acc[4:5, 0:128] += jnp.sum(dfa, axis=0, keepdims=True)

        @pl.when(i == nt - 1)
        def _():
            foldm = ((_iota((512, 128), 0) & (HD - 1)) == _iota((512, 128), 1)).astype(F32)
            dgn_o[...] = _dot_hi(acc[...], foldm)

    def col(width, off):
        return pl.BlockSpec((tm, width), lambda i: (nt - 1 - i, off // width))

    def rows(width):
        return pl.BlockSpec((tm, width), lambda i: (nt - 1 - i, 0))

    def vec(width):
        return pl.BlockSpec((1, width), lambda i: (0, 0))

    pair = pl.BlockSpec((2, tm, 128), lambda i: (0, nt - 1 - i, 0))
    return pl.pallas_call(
        body, name=name, grid=(nt,),
        in_specs=[col(512, QA), col(512, KA), col(512, QB), col(128, KB), col(128, FA),
                  rows(512), rows(512), rows(512), rows(512), pair, pair, rows(512), rows(512), rows(D), rows(D),
                  vec(512), vec(512), vec(512), vec(128), vec(128)],
        out_specs=[rows(DP), pl.BlockSpec((8, 128), lambda i: (0, 0))],
        out_shape=[_sds((t, DP), CDT), _sds((8, 128), F32)],
        scratch_shapes=[pltpu.VMEM((8, 128), F32), pltpu.VMEM((8, 512), F32)],
        compiler_params=_params(("arbitrary",)),
    )(proj, proj, proj, proj, proj, dqf, dkf, dvf, dqs, dkse, dvse, dcq, dck, dga, dgb, gfq, gfk, gsq, gsk, fb)


def _gate_fwd(ofox, oswa, wbf, wbs, proj, name):
    t = ofox.shape[0]
    tm = _row_tile(t)
    tn = 512

    def body(of_r, os_r, wf_r, ws_r, ga_r, gb_r, y_o, yt_o, pf_o, ps_o, oft_o, ost_o):
        j = pl.program_id(1)
        pf = _dot(of_r[...], wf_r[...])
        ps = _dot(os_r[...], ws_r[...])
        y = _sigmoid(ga_r[...]) * pf + _sigmoid(gb_r[...]) * ps
        y_o[...] = y.astype(CDT)
        yt_o[...] = y.T.astype(CDT)
        pf_o[...] = pf.astype(CDT)
        ps_o[...] = ps.astype(CDT)

        @pl.when(j == 0)
        def _():
            oft_o[...] = of_r[...].astype(F32).T.astype(CDT)
            ost_o[...] = os_r[...].astype(F32).T.astype(CDT)

    tile = pl.BlockSpec((tm, tn), lambda i, j: (i, j))
    return pl.pallas_call(
        body, name=name, grid=(t // tm, D // tn),
        in_specs=[pl.BlockSpec((tm, 512), lambda i, j: (i, 0)), pl.BlockSpec((tm, 512), lambda i, j: (i, 0)),
                  pl.BlockSpec((512, tn), lambda i, j: (0, j)), pl.BlockSpec((512, tn), lambda i, j: (0, j)),
                  pl.BlockSpec((tm, tn), lambda i, j: (i, GA // tn + j)),
                  pl.BlockSpec((tm, tn), lambda i, j: (i, GB // tn + j))],
        out_specs=[tile, pl.BlockSpec((tn, tm), lambda i, j: (j, i)), tile, tile,
                   pl.BlockSpec((512, tm), lambda i, j: (0, i)), pl.BlockSpec((512, tm), lambda i, j: (0, i))],
        out_shape=[_sds((t, D), CDT), _sds((D, t), CDT), _sds((t, D), CDT), _sds((t, D), CDT),
                   _sds((512, t), CDT), _sds((512, t), CDT)],
        compiler_params=_params(("parallel", "arbitrary")),
    )(ofox, oswa, wbf, wbs, proj, proj)


def _gate_bwd(dy, pf, ps, proj, name):
    t = dy.shape[0]
    tm = _row_tile(t)
    tn = 512

    def body(dy_r, pf_r, ps_r, ga_r, gb_r, dpf_o, dps_o, dga_o, dgb_o):
        dy_ = dy_r[...]
        sa = _sigmoid(ga_r[...])
        sb = _sigmoid(gb_r[...])
        dpf_o[...] = (dy_ * sa).astype(CDT)
        dps_o[...] = (dy_ * sb).astype(CDT)
        dga_o[...] = (dy_ * pf_r[...].astype(F32) * (sa * (1.0 - sa))).astype(CDT)
        dgb_o[...] = (dy_ * ps_r[...].astype(F32) * (sb * (1.0 - sb))).astype(CDT)

    tile = pl.BlockSpec((tm, tn), lambda i, j: (i, j))
    return pl.pallas_call(
        body, name=name, grid=(t // tm, D // tn),
        in_specs=[tile, tile, tile,
                  pl.BlockSpec((tm, tn), lambda i, j: (i, GA // tn + j)),
                  pl.BlockSpec((tm, tn), lambda i, j: (i, GB // tn + j))],
        out_specs=[tile] * 4,
        out_shape=[_sds((t, D), CDT)] * 4,
        compiler_params=_params(("parallel", "parallel")),
    )(dy, pf, ps, proj, proj)


def _tri_steps(n, by_key):
    if by_key:
        pairs = [(i, j) for j in range(n) for i in range(j, n)]
    else:
        pairs = [(i, j) for i in range(n) for j in range(i + 1)]
    return (np.array([p[0] for p in pairs], np.int32), np.array([p[1] for p in pairs], np.int32))


def _head_col(blk, lane, h):
    return jnp.sum(jnp.where(lane == h, blk, 0.0), axis=1, keepdims=True)


def _head_row(blk, sub, h):
    return jnp.sum(jnp.where(sub == h, blk, 0.0), axis=0, keepdims=True)


def _fox_fwd(qf, kf, vf, c, ct, name):
    t = qf.shape[0]
    ta = _row_tile(t)
    qi, kj = _tri_steps(t // ta, by_key=False)

    def body(qi_r, kj_r, q_r, k_r, v_r, c_r, ct_r, o_o, lse_o, m_sc, l_sc, acc_sc):
        p = pl.program_id(0)
        n = pl.program_id(1)
        i = qi_r[n]
        j = kj_r[n]
        lane = _iota((1, 128), 1)
        lo = lane < HD

        @pl.when(j == 0)
        def _():
            m_sc[...] = jnp.full(m_sc.shape, NEG, F32)
            l_sc[...] = jnp.zeros(l_sc.shape, F32)
            acc_sc[...] = jnp.zeros(acc_sc.shape, F32)

        q = q_r[...]
        k = k_r[...]
        v = v_r[...]
        rows = i * ta + _iota((ta, 1), 0)
        cols = j * ta + _iota((1, ta), 1)
        mask = (cols <= rows) & (cols >= PAD)
        sub = _iota((8, 1), 0)
        alphas, pvs = [], []
        for e in (0, 1):
            sel = lo if e == 0 else jnp.logical_not(lo)
            s = _dot_nt(jnp.where(sel, q, 0), k)
            s = s + _head_col(c_r[...], lane, 2 * p + e) - _head_row(ct_r[...], sub, 2 * p + e)
            s = jnp.where(mask, s, NEG)
            m_prev = m_sc[e][:, 0:1]
            m_new = jnp.maximum(m_prev, jnp.max(s, axis=1, keepdims=True))
            alpha = jnp.exp(m_prev - m_new)
            pe = jnp.exp(s - m_new)
            l_new = alpha * l_sc[e][:, 0:1] + jnp.sum(pe, axis=1, keepdims=True)
            m_sc[e] = jnp.broadcast_to(m_new, (ta, 128))
            l_sc[e] = jnp.broadcast_to(l_new, (ta, 128))
            alphas.append(alpha)
            pvs.append(_dot(pe.astype(CDT), v))
        acc_sc[...] = acc_sc[...] * jnp.where(lo, alphas[0], alphas[1]) + jnp.where(lo, pvs[0], pvs[1])

        @pl.when(j == i)
        def _():
            l = jnp.where(lo, l_sc[0], l_sc[1])
            o_o[...] = (acc_sc[...] / l).astype(CDT)
            lse_o[...] = jnp.where(lo, m_sc[0], m_sc[1]) + jnp.log(l)

    qblk = pl.BlockSpec((ta, 128), lambda p, n, qi_r, kj_r: (qi_r[n], p))
    kblk = pl.BlockSpec((ta, 128), lambda p, n, qi_r, kj_r: (kj_r[n], p))
    grid_spec = pltpu.PrefetchScalarGridSpec(
        num_scalar_prefetch=2, grid=(NPAIR, len(qi)),
        in_specs=[qblk, kblk, kblk,
                  pl.BlockSpec((ta, 128), lambda p, n, qi_r, kj_r: (qi_r[n], 0)),
                  pl.BlockSpec((8, ta), lambda p, n, qi_r, kj_r: (0, kj_r[n]))],
        out_specs=[qblk, qblk],
        scratch_shapes=[pltpu.VMEM((2, ta, 128), F32), pltpu.VMEM((2, ta, 128), F32), pltpu.VMEM((ta, 128), F32)],
    )
    return pl.pallas_call(
        body, name=name, grid_spec=grid_spec,
        out_shape=[_sds((t, 512), CDT), _sds((t, 512), F32)],
        compiler_params=_params(("parallel", "arbitrary")),
    )(jnp.asarray(qi), jnp.asarray(kj), qf, kf, vf, c, ct)


def _fox_bwd(qf, kf, vf, c, ct, o, lse, do, name):
    t = qf.shape[0]
    ta = _row_tile(t)
    nq = t // ta
    qi, kj = _tri_steps(nq, by_key=True)

    def body(qi_r, kj_r, q_r, k_r, v_r, c_r, ct_r, o_r, lse_r, do_r,
             dq_o, dcq_o, dk_o, dv_o, dck_o, dk_sc, dv_sc, dck_sc):
        p = pl.program_id(0)
        n = pl.program_id(1)
        i = qi_r[n]
        j = kj_r[n]
        lane = _iota((1, 128), 1)
        lo = lane < HD

        @pl.when(n == 0)
        def _():
            dq_o[...] = jnp.zeros(dq_o.shape, F32)
            dcq_o[...] = jnp.zeros(dcq_o.shape, F32)

        @pl.when(i == j)
        def _():
            dk_sc[...] = jnp.zeros(dk_sc.shape, F32)
            dv_sc[...] = jnp.zeros(dv_sc.shape, F32)
            dck_sc[...] = jnp.zeros(dck_sc.shape, F32)

        q = q_r[...]
        k = k_r[...]
        v = v_r[...]
        do_ = do_r[...]
        dd = do_ * o_r[...].astype(F32)
        lse = lse_r[...]
        rows = i * ta + _iota((ta, 1), 0)
        cols = j * ta + _iota((1, ta), 1)
        mask = (cols <= rows) & (cols >= PAD)
        sub = _iota((8, 1), 0)
        dq_add = jnp.zeros((ta, 128), F32)
        dk_add = jnp.zeros((ta, 128), F32)
        dv_add = jnp.zeros((ta, 128), F32)
        rsum, csum = [], []
        for e in (0, 1):
            sel = lo if e == 0 else jnp.logical_not(lo)
            qe = jnp.where(sel, q, 0)
            doe = jnp.where(sel, do_, 0.0).astype(CDT)
            s = _dot_nt(qe, k)
            s = s + _head_col(c_r[...], lane, 2 * p + e) - _head_row(ct_r[...], sub, 2 * p + e)
            s = jnp.where(mask, s, NEG)
            pr = jnp.exp(s - lse[:, HD * e:HD * e + 1])
            dv_add = dv_add + _dot(pr.T.astype(CDT), doe)
            dp = _dot_nt(doe, v)
            delta = jnp.sum(jnp.where(sel, dd, 0.0), axis=1, keepdims=True)
            ds = pr * (dp - delta)
            dq_add = dq_add + _dot(ds.astype(CDT), jnp.where(sel, k, 0))
            dst = ds.T
            dk_add = dk_add + _dot(dst.astype(CDT), qe)
            rsum.append(jnp.sum(ds, axis=1, keepdims=True))
            csum.append(jnp.sum(dst, axis=1, keepdims=True))
        rs = pl.ds(pl.multiple_of(i * ta, ta), ta)
        dq_o[rs, :] += dq_add
        dcq_o[rs, :] += jnp.where(lo, rsum[0], rsum[1])
        dk_sc[...] += dk_add
        dv_sc[...] += dv_add
        dck_sc[...] += jnp.where(lo, csum[0], csum[1])

        @pl.when(i == nq - 1)
        def _():
            dk_o[...] = dk_sc[...]
            dv_o[...] = dv_sc[...]
            dck_o[...] = dck_sc[...]

    qblk = pl.BlockSpec((ta, 128), lambda p, n, qi_r, kj_r: (qi_r[n], p))
    kblk = pl.BlockSpec((ta, 128), lambda p, n, qi_r, kj_r: (kj_r[n], p))
    whole = pl.BlockSpec((t, 128), lambda p, n, qi_r, kj_r: (0, p))
    grid_spec = pltpu.PrefetchScalarGridSpec(
        num_scalar_prefetch=2, grid=(NPAIR, len(qi)),
        in_specs=[qblk, kblk, kblk,
                  pl.BlockSpec((ta, 128), lambda p, n, qi_r, kj_r: (qi_r[n], 0)),
                  pl.BlockSpec((8, ta), lambda p, n, qi_r, kj_r: (0, kj_r[n])),
                  qblk, qblk, qblk],
        out_specs=[whole, whole, kblk, kblk, kblk],
        scratch_shapes=[pltpu.VMEM((ta, 128), F32)] * 3,
    )
    return pl.pallas_call(
        body, name=name, grid_spec=grid_spec,
        out_shape=[_sds((t, 512), F32)] * 5,
        compiler_params=_params(("parallel", "arbitrary")),
    )(jnp.asarray(qi), jnp.asarray(kj), qf, kf, vf, c, ct, o, lse, do)


def _bucket_table():
    r = np.arange(BLK)[:, None]
    c = np.arange(3 * BLK)[None, :]
    d = np.where(c < BLK, r + BLK - c, r - (c - BLK))
    n = np.maximum(d, 0)
    max_exact = N_BUCKETS // 2
    nf = np.maximum(n, 1).astype(np.float32)
    large = max_exact + (np.log(nf / max_exact) / math.log(BLK / max_exact) * (N_BUCKETS - max_exact)).astype(np.int32)
    large = np.minimum(large, N_BUCKETS - 1)
    b = np.where(n < max_exact, n, large)
    return np.where(c < 2 * BLK, b, N_BUCKETS - 1).astype(np.int32)


def _bias_fwd(table, name):
    bucket = jnp.asarray(_bucket_table())

    def body(tab_r, b_r, o_o):
        h = pl.program_id(0)
        b = b_r[...]
        acc = jnp.zeros(b.shape, F32)
        for k in range(N_BUCKETS):
            acc = jnp.where(b == k, tab_r[k, h], acc)
        o_o[...] = acc

    return pl.pallas_call(
        body, name=name, grid=(8,),
        in_specs=[pl.BlockSpec(memory_space=pltpu.SMEM), pl.BlockSpec((BLK, 3 * BLK), lambda h: (0, 0))],
        out_specs=pl.BlockSpec((None, BLK, 3 * BLK), lambda h: (h, 0, 0)),
        out_shape=_sds((8, BLK, 3 * BLK), F32),
        compiler_params=_params(("parallel",)),
    )(table, bucket)


def _bias_bwd(dbias, name):
    bucket = jnp.asarray(_bucket_table())

    def body(d_r, b_r, o_o):
        h = pl.program_id(0)
        b = b_r[...]
        d = d_r[...]
        lane = _iota((1, 128), 1)
        row = jnp.zeros((1, 128), F32)
        for k in range(N_BUCKETS):
            row = jnp.where(lane == k, jnp.sum(jnp.where(b == k, d, 0.0)), row)
        o_o[pl.ds(h, 1), :] = row

    return pl.pallas_call(
        body, name=name, grid=(8,),
        in_specs=[pl.BlockSpec((None, BLK, 3 * BLK), lambda h: (h, 0, 0)), pl.BlockSpec((BLK, 3 * BLK), lambda h: (0, 0))],
        out_specs=pl.BlockSpec((8, 128), lambda h: (0, 0)),
        out_shape=_sds((8, 128), F32),
        compiler_params=_params(("arbitrary",)),
    )(dbias, bucket)


def _swa_valid(i):
    r = _iota((BLK, 1), 0)
    c = _iota((1, 3 * BLK), 1)
    prev = (c < BLK) & (c > r) & (i >= 1) & ((i - 1) * BLK + c >= PAD)
    cc = c - BLK
    cur = (c >= BLK) & (c < 2 * BLK) & (cc <= r) & (i * BLK + cc >= PAD)
    cm = c - 2 * BLK
    meta = (c >= 2 * BLK) & (cm >= PAD) & (i * BLK + r - cm >= BLK)
    return prev | cur | meta


def _swa_kv_specs():
    def at(f):
        return pl.BlockSpec((None, BLK, 128), lambda p, i: (p // 2, f(i), 0))
    return [at(lambda i: jnp.maximum(i - 1, 0)), at(lambda i: i), at(lambda i: 0)]


def _swa_fwd(qs, kse, vse, bias, sinks, name):
    t = qs.shape[0]

    def body(sink_r, q_r, kp_r, kc_r, km_r, vp_r, vc_r, vm_r, b_r, o_o, lse_o):
        p = pl.program_id(0)
        i = pl.program_id(1)
        lo = _iota((1, 128), 1) < HD
        q = q_r[...]
        k3 = jnp.concatenate([kp_r[...], kc_r[...], km_r[...]], axis=0)
        v3 = jnp.concatenate([vp_r[...], vc_r[...], vm_r[...]], axis=0)
        valid = _swa_valid(i)
        outs, lses = [], []
        for e in (0, 1):
            sel = lo if e == 0 else jnp.logical_not(lo)
            s = _dot_nt(jnp.where(sel, q, 0), k3) + b_r[e]
            s = jnp.where(valid, s, NEG)
            sink = sink_r[2 * p + e]
            mx = jnp.maximum(jnp.max(s, axis=1, keepdims=True), sink)
            pe = jnp.exp(s - mx)
            den = jnp.sum(pe, axis=1, keepdims=True) + jnp.exp(sink - mx)
            outs.append(_dot(pe.astype(CDT), v3) / den)
            lses.append(mx + jnp.log(den))
        o_o[...] = jnp.where(lo, outs[0], outs[1]).astype(CDT)
        lse_o[...] = jnp.where(lo, lses[0], lses[1])

    qblk = pl.BlockSpec((BLK, 128), lambda p, i: (i, p))
    return pl.pallas_call(
        body, name=name, grid=(NPAIR, t // BLK),
        in_specs=[pl.BlockSpec(memory_space=pltpu.SMEM), qblk] + _swa_kv_specs() + _swa_kv_specs()
        + [pl.BlockSpec((2, BLK, 3 * BLK), lambda p, i: (p, 0, 0))],
        out_specs=[qblk, qblk],
        out_shape=[_sds((t, 512), CDT), _sds((t, 512), F32)],
        compiler_params=_params(("parallel", "parallel")),
    )(sinks, qs, kse, kse, kse, vse, vse, vse, bias)


def _swa_bwd(qs, kse, vse, bias, sinks, o, lse, do, name):
    t = qs.shape[0]

    def body(sink_r, q_r, kp_r, kc_r, km_r, vp_r, vc_r, vm_r, b_r, o_r, lse_r, do_r,
             dq_o, dk_o, dv_o, db_o, dsk_o):
        p = pl.program_id(0)
        i = pl.program_id(1)
        lo = _iota((1, 128), 1) < HD

        @pl.when((i == 0) & (p % 2 == 0))
        def _():
            dk_o[...] = jnp.zeros(dk_o.shape, F32)
            dv_o[...] = jnp.zeros(dv_o.shape, F32)

        @pl.when(i == 0)
        def _():
            db_o[...] = jnp.zeros(db_o.shape, F32)
            dsk_o[...] = jnp.zeros(dsk_o.shape, F32)

        q = q_r[...]
        do_ = do_r[...]
        dd = do_ * o_r[...].astype(F32)
        lse = lse_r[...]
        k3 = jnp.concatenate([kp_r[...], kc_r[...], km_r[...]], axis=0)
        v3 = jnp.concatenate([vp_r[...], vc_r[...], vm_r[...]], axis=0)
        valid = _swa_valid(i)
        dq = jnp.zeros((BLK, 128), F32)
        dk3 = jnp.zeros((3 * BLK, 128), F32)
        dv3 = jnp.zeros((3 * BLK, 128), F32)
        dsink = []
        for e in (0, 1):
            sel = lo if e == 0 else jnp.logical_not(lo)
            qe = jnp.where(sel, q, 0)
            doe = jnp.where(sel, do_, 0.0).astype(CDT)
            lse_e = lse[:, HD * e:HD * e + 1]
            s = _dot_nt(qe, k3) + b_r[e]
            s = jnp.where(valid, s, NEG)
            pr = jnp.exp(s - lse_e)
            delta = jnp.sum(jnp.where(sel, dd, 0.0), axis=1, keepdims=True)
            ds = pr * (_dot_nt(doe, v3) - delta)
            db_o[e] += ds
            dsink.append(-jnp.sum(jnp.exp(sink_r[2 * p + e] - lse_e) * delta, axis=0, keepdims=True))
            dq = dq + _dot(ds.astype(CDT), jnp.where(sel, k3, 0))
            dk3 = dk3 + _dot(ds.T.astype(CDT), qe)
            dv3 = dv3 + _dot(pr.T.astype(CDT), doe)
        dq_o[...] = dq
        prev = pl.ds(pl.multiple_of(jnp.maximum(i - 1, 0) * BLK, BLK), BLK)
        cur = pl.ds(pl.multiple_of(i * BLK, BLK), BLK)
        dk_o[prev, :] += dk3[0:BLK]
        dk_o[cur, :] += dk3[BLK:2 * BLK]
        dk_o[0:BLK, :] += dk3[2 * BLK:]
        dv_o[prev, :] += dv3[0:BLK]
        dv_o[cur, :] += dv3[BLK:2 * BLK]
        dv_o[0:BLK, :] += dv3[2 * BLK:]
        dsk_o[0:1, :] += jnp.where(lo, dsink[0], dsink[1])

    qblk = pl.BlockSpec((BLK, 128), lambda p, i: (i, p))
    kvacc = pl.BlockSpec((None, t, 128), lambda p, i: (p // 2, 0, 0))
    bblk = pl.BlockSpec((2, BLK, 3 * BLK), lambda p, i: (p, 0, 0))
    return pl.pallas_call(
        body, name=name, grid=(NPAIR, t // BLK),
        in_specs=[pl.BlockSpec(memory_space=pltpu.SMEM), qblk] + _swa_kv_specs() + _swa_kv_specs()
        + [bblk, qblk, qblk, qblk],
        out_specs=[qblk, kvacc, kvacc, bblk, pl.BlockSpec((None, 8, 128), lambda p, i: (p, 0, 0))],
        out_shape=[_sds((t, 512), F32), _sds((2, t, 128), F32), _sds((2, t, 128), F32),
                   _sds((8, BLK, 3 * BLK), F32), _sds((NPAIR, 8, 128), F32)],
        compiler_params=_params(("arbitrary", "arbitrary")),
    )(sinks, qs, kse, kse, kse, vse, vse, vse, bias, o, lse, do)


def _adamw(w, g, m, v, name):
    r, c = w.shape
    tr = 128 if r % 128 == 0 else r

    def body(w_r, g_r, m_r, v_r, d_o, m_o, v_o):
        g_ = g_r[...]
        m_ = ADAM_B1 * m_r[...] + (1.0 - ADAM_B1) * g_
        v_ = ADAM_B2 * v_r[...] + (1.0 - ADAM_B2) * jnp.square(g_)
        m_hat = m_ / (1.0 - ADAM_B1 ** ADAM_STEP)
        v_hat = v_ / (1.0 - ADAM_B2 ** ADAM_STEP)
        d_o[...] = -ADAM_LR * (m_hat / (jnp.sqrt(v_hat) + ADAM_EPS) + ADAM_WD * w_r[...])
        m_o[...] = m_
        v_o[...] = v_

    blk = pl.BlockSpec((tr, c), lambda i: (i, 0))
    return pl.pallas_call(
        body, name=name, grid=(r // tr,),
        in_specs=[blk] * 4, out_specs=[blk] * 3, out_shape=[_sds((r, c), F32)] * 3,
        compiler_params=_params(("parallel",)),
    )(w, g, m, v)


def _pair_add(own, got, half_idx, name):
    tr = PACK_TILE
    nb = HALF_ROWS // tr
    nl = own.shape[1]

    def body(c_r, a_r, b_r, o_o):
        o_o[...] = (a_r[...].astype(F32) + b_r[...].astype(F32)).astype(CDT)

    grid_spec = pltpu.PrefetchScalarGridSpec(
        num_scalar_prefetch=1, grid=(4, nl, nb),
        in_specs=[pl.BlockSpec((None, None, tr, 128), lambda s, l, i, c_r: (s, l, c_r[0] * nb + i, 0)),
                  pl.BlockSpec((None, None, tr, 128), lambda s, l, i, c_r: (s, l, i, 0))],
        out_specs=pl.BlockSpec((None, None, tr, 128), lambda s, l, i, c_r: (s, l, i, 0)),
    )
    return pl.pallas_call(
        body, name=name, grid_spec=grid_spec, out_shape=_sds((4, nl, HALF_ROWS, 128), CDT),
        compiler_params=_params(("parallel", "parallel", "parallel")),
    )(half_idx, own, got)


def _sum4(ps, got, shard_idx, name):
    tr = PACK_TILE
    nl = ps.shape[1]

    def body(s_r, a_r, b_r, o_o):
        o_o[...] = ((a_r[...].astype(F32) + b_r[0].astype(F32)) + b_r[1].astype(F32)) + b_r[2].astype(F32)

    grid_spec = pltpu.PrefetchScalarGridSpec(
        num_scalar_prefetch=1, grid=(nl, HALF_ROWS // tr),
        in_specs=[pl.BlockSpec((None, None, tr, 128), lambda l, i, s_r: (s_r[0], l, i, 0)),
                  pl.BlockSpec((3, None, tr, 128), lambda l, i, s_r: (0, l, i, 0))],
        out_specs=pl.BlockSpec((None, tr, 128), lambda l, i, s_r: (l, i, 0)),
    )
    return pl.pallas_call(
        body, name=name, grid_spec=grid_spec, out_shape=_sds((nl, HALF_ROWS, 128), F32),
        compiler_params=_params(("parallel", "parallel")),
    )(shard_idx, ps, got)


def _sum8(slots, name):
    def body(a_r, o_o):
        acc = a_r[0]
        for k in range(1, 8):
            acc = acc + a_r[k]
        o_o[...] = acc

    return pl.pallas_call(
        body, name=name, out_shape=_sds((SMALL_ROWS, 128), F32),
        in_specs=[pl.BlockSpec(memory_space=pltpu.VMEM)], out_specs=pl.BlockSpec(memory_space=pltpu.VMEM),
        compiler_params=_params(),
    )(slots)


def _place():
    x, y, c = lax.axis_index("x"), lax.axis_index("y"), lax.axis_index("c")
    chips = [(1 - x, y), (x, 1 - y), (1 - x, 1 - y)]
    return x, y, c, chips


def _remote(src, dst, send_sems, recv_sems, k, to):
    return pltpu.make_async_remote_copy(src_ref=src, dst_ref=dst, send_sem=send_sems.at[k], recv_sem=recv_sems.at[k],
                                        device_id=to, device_id_type=MESH_ID)


ANY = pl.BlockSpec(memory_space=pl.ANY)


def _gather_weights(wflat, mflat, name):
    nl = wflat.shape[0]

    def body(w_r, m_r, wall_o, mall_o, send_sems, recv_sems):
        x, y, c, chips = _place()
        s = 2 * x + y
        sib = (x, y, 1 - c)
        mine = pl.ds(pl.multiple_of(c * HALF_ROWS, 16), HALF_ROWS)
        other = pl.ds(pl.multiple_of((1 - c) * HALF_ROWS, 16), HALF_ROWS)
        sent = []
        for j, (cx, cy) in enumerate(chips):
            sent.append(_remote(w_r.at[:, mine], wall_o.at[s, :, mine], send_sems, recv_sems, j, (cx, cy, c)))
            sent.append(_remote(m_r, mall_o.at[s], send_sems, recv_sems, 6 + j, (cx, cy, c)))
        for cp in sent:
            cp.start()
        for j, (cx, cy) in enumerate(chips):
            sj = 2 * cx + cy
            _remote(w_r.at[:, mine], wall_o.at[sj, :, mine], send_sems, recv_sems, j, sib).wait_recv()
            fwd = _remote(wall_o.at[sj, :, mine], wall_o.at[sj, :, mine], send_sems, recv_sems, 3 + j, sib)
            fwd.start()
            sent.append(fwd)
        for j, (cx, cy) in enumerate(chips):
            sj = 2 * cx + cy
            _remote(w_r.at[:, other], wall_o.at[sj, :, other], send_sems, recv_sems, 3 + j, sib).wait_recv()
            _remote(m_r, mall_o.at[sj], send_sems, recv_sems, 6 + j, sib).wait_recv()
        for cp in sent:
            cp.wait_send()

    return pl.pallas_call(
        body, name=name,
        out_shape=[_sds((4, nl, LAYER_ROWS, 128), CDT), _sds((4, META_ROWS, 128), F32)],
        in_specs=[ANY, ANY], out_specs=[ANY, ANY],
        scratch_shapes=[pltpu.SemaphoreType.DMA((9,)), pltpu.SemaphoreType.DMA((9,))],
    )(wflat, mflat)


def _swap_halves(gfl, gsm, name):
    def body(g_r, s_r, got_o, slots_o, send_sems, recv_sems, loc_sem):
        x, y, c, _ = _place()
        me = 4 * x + 2 * y + c
        theirs = pl.ds(pl.multiple_of((1 - c) * HALF_ROWS, 16), HALF_ROWS)
        loc = pltpu.make_async_copy(s_r, slots_o.at[me], loc_sem.at[0])
        loc.start()
        sent = [_remote(g_r.at[:, :, theirs, :], got_o, send_sems, recv_sems, 0, (x, y, 1 - c))]
        for k in range(1, 8):
            px, py, pc = x ^ (k >> 2), y ^ ((k >> 1) & 1), c ^ (k & 1)
            sent.append(_remote(s_r, slots_o.at[me], send_sems, recv_sems, k, (px, py, pc)))
        for cp in sent:
            cp.start()
        _remote(g_r.at[:, :, theirs, :], got_o, send_sems, recv_sems, 0, (x, y, 1 - c)).wait_recv()
        for k in range(1, 8):
            px, py, pc = x ^ (k >> 2), y ^ ((k >> 1) & 1), c ^ (k & 1)
            _remote(s_r, slots_o.at[4 * px + 2 * py + pc], send_sems, recv_sems, k, (px, py, pc)).wait_recv()
        for cp in sent:
            cp.wait_send()
        loc.wait()

    return pl.pallas_call(
        body, name=name,
        out_shape=[_sds((4, gfl.shape[1], HALF_ROWS, 128), CDT), _sds((8, SMALL_ROWS, 128), F32)],
        in_specs=[ANY, ANY], out_specs=[ANY, ANY],
        scratch_shapes=[pltpu.SemaphoreType.DMA((8,)), pltpu.SemaphoreType.DMA((8,)), pltpu.SemaphoreType.DMA((1,))],
    )(gfl, gsm)


def _scatter_shards(ps, name):
    def body(p_r, got_o, send_sems, recv_sems):
        x, y, c, chips = _place()
        sent = [_remote(p_r.at[2 * cx + cy], got_o.at[j], send_sems, recv_sems, j, (cx, cy, c))
                for j, (cx, cy) in enumerate(chips)]
        for cp in sent:
            cp.start()
        for j in range(3):
            _remote(p_r.at[0], got_o.at[j], send_sems, recv_sems, j, (x, y, c)).wait_recv()
        for cp in sent:
            cp.wait_send()

    return pl.pallas_call(
        body, name=name, out_shape=_sds((3, ps.shape[1], HALF_ROWS, 128), CDT),
        in_specs=[ANY], out_specs=ANY,
        scratch_shapes=[pltpu.SemaphoreType.DMA((3,)), pltpu.SemaphoreType.DMA((3,))],
    )(ps)


def _join_halves(red_half, name):
    def body(h_r, got_o, send_sem, recv_sem):
        x, y, c, _ = _place()
        out = _remote(h_r, got_o, send_sem, recv_sem, 0, (x, y, 1 - c))
        out.start()
        out.wait()

    return pl.pallas_call(
        body, name=name, out_shape=_sds(red_half.shape, F32),
        in_specs=[ANY], out_specs=ANY,
        scratch_shapes=[pltpu.SemaphoreType.DMA((1,)), pltpu.SemaphoreType.DMA((1,))],
    )(red_half)


def _pack_shard(local):
    pad = [jnp.zeros((LAYER_ROWS - LAYER_RAW, 128), local[SHARD_ITEMS[0][0]].dtype)]
    return jnp.stack([jnp.concatenate([local[nm][l].reshape(-1, 128) for nm, _, _ in SHARD_ITEMS] + pad)
                      for l in range(2)])


def _unpack_full(wall):
    layers = []
    for l in range(2):
        ws = {}
        off = 0
        for nm, (r, c), kind in SHARD_ITEMS:
            n = r * c // 128
            piece = wall[:, l, off:off + n, :].reshape(4, r, c)
            off += n
            if kind == "row":
                ws[nm] = piece.reshape(4 * r, c)
            else:
                ws[nm] = jnp.concatenate([piece[s] for s in range(4)], axis=1)
        layers.append(ws)
    return layers


def _mix_cols(w):
    return jnp.concatenate([w[:, 2312:4360], w[:, 0:1536], w[:, 1544:2312], w[:, 1536:1544],
                            jnp.zeros((w.shape[0], DP - D_IN), w.dtype)], axis=1)


def _unmix_cols(w):
    return jnp.concatenate([w[:, QA:QA + 1536], w[:, FA:FA + 8], w[:, QB:QB + 768], w[:, GA:GA + 2048]], axis=1)


def _pack_grads(grads):
    pad = [jnp.zeros((LAYER_ROWS - LAYER_RAW, 128), CDT)]
    shards = []
    for s in range(4):
        layers = []
        for l in range(2):
            parts = []
            for nm, (r, c), kind in SHARD_ITEMS:
                g = grads[l][nm]
                parts.append((g[s * r:(s + 1) * r] if kind == "row" else g[:, s * c:(s + 1) * c]).reshape(-1, 128))
            layers.append(jnp.concatenate(parts + pad))
        shards.append(jnp.stack(layers))
    return jnp.stack(shards)


def _unpack_shard(red):
    out = {}
    off = 0
    for nm, (r, c), _ in SHARD_ITEMS:
        n = r * c // 128
        out[nm] = red[:, off:off + n, :].reshape(2, r, c)
        off += n
    return out


def _rows128(a, rows):
    flat = a.reshape(-1)
    return jnp.pad(flat, (0, rows * 128 - flat.shape[0])).reshape(rows, 128)


SMALL_ITEMS = (("rel_bias_table", 2), ("ffn1_norm", 16), ("mix_norm", 16), ("ffn2_norm", 16), ("forget_bias", 1),
               ("fox_q_norm", 1), ("fox_k_norm", 1), ("swa_q_norm", 1), ("swa_k_norm", 1), ("swa_sinks", 1))
SMALL_ADAM_ROWS = 96


def _layer_fwd(h, lw, l):
    sv = {"h0": h}
    a, sv["a1t"] = _rms_fwd(h, lw["ffn1_norm"], f"rms_fwd_a{l}")
    sv["gu1"], s, sv["s1t"] = _ffn_in(a, lw["ffn1_w_in"], f"ffn_in_a{l}")
    h = _mm_res(s, lw["ffn1_w_out"], h, 0.5, f"ffn_out_a{l}")
    sv["h1"] = h
    a, sv["amt"] = _rms_fwd(h, lw["mix_norm"], f"rms_fwd_m{l}")
    proj = _mm(a, lw["w_mix"], F32, _row_tile(h.shape[0]), 640, f"proj{l}")
    sv["proj"] = proj
    qf, kf, vf, qs, kse, vse, c, ct = _qknorm_fwd(proj, lw["gfq"], lw["gfk"], lw["gsq"], lw["gsk"], lw["fb"],
                                                   f"qknorm_fwd{l}")
    ofox, lse_f = _fox_fwd(qf, kf, vf, c, ct, f"fox_fwd{l}")
    oswa, lse_s = _swa_fwd(qs, kse, vse, lw["bias"], lw["sinks"], f"swa_fwd{l}")
    sv.update(qf=qf, kf=kf, vf=vf, qs=qs, kse=kse, vse=vse, c=c, ct=ct, ofox=ofox, oswa=oswa, lse_f=lse_f, lse_s=lse_s)
    y, sv["yt"], sv["pf"], sv["ps"], sv["oft"], sv["ost"] = _gate_fwd(ofox, oswa, lw["w_branch_fox"], lw["w_branch_swa"],
                                                                     proj, f"gate_fwd{l}")
    h = _mm_res(y, lw["w_out"], h, 1.0, f"mix_out{l}")
    sv["h2"] = h
    a, sv["a2t"] = _rms_fwd(h, lw["ffn2_norm"], f"rms_fwd_b{l}")
    sv["gu2"], s, sv["s2t"] = _ffn_in(a, lw["ffn2_w_in"], f"ffn_in_b{l}")
    h = _mm_res(s, lw["ffn2_w_out"], h, 0.5, f"ffn_out_b{l}")
    return h, sv


def _ffn_bwd(dh, dhb, h_in, at, gu, st, norm, w_in, w_out, tag):
    dgu = _ffn_bwd_mid(dhb, w_out, gu, f"ffn_bwd_mid_{tag}")
    d_w_out = _mm(st, dhb, CDT, 256, 512, f"dw_ffn_out_{tag}", scale=0.5)
    da = _ffn_bwd_in(dgu, w_in, f"ffn_bwd_in_{tag}")
    d_w_in = _mm(at, dgu, CDT, 512, 256, f"dw_ffn_in_{tag}")
    dh, dhb, dg = _rms_bwd(da, h_in, norm, dh, f"rms_bwd_{tag}")
    return dh, dhb, d_w_out, d_w_in, dg


def _layer_bwd(dh, dhb, sv, lw, l):
    g = {}
    dh, dhb, g["ffn2_w_out"], g["ffn2_w_in"], g["ffn2_norm"] = _ffn_bwd(
        dh, dhb, sv["h2"], sv["a2t"], sv["gu2"], sv["s2t"], lw["ffn2_norm"], lw["ffn2_w_in"], lw["ffn2_w_out"], f"b{l}")
    dy = _mm_nt(dhb, lw["w_out"], f"d_y{l}")
    g["w_out"] = _mm(sv["yt"], dhb, CDT, 512, 512, f"dw_out{l}")
    dpf, dps, dga, dgb = _gate_bwd(dy, sv["pf"], sv["ps"], sv["proj"], f"gate_bwd{l}")
    do_f = _mm_nt(dpf, lw["w_branch_fox"], f"d_ofox{l}")
    do_s = _mm_nt(dps, lw["w_branch_swa"], f"d_oswa{l}")
    g["w_branch_fox"] = _mm(sv["oft"], dpf, CDT, 512, 512, f"dw_bfox{l}")
    g["w_branch_swa"] = _mm(sv["ost"], dps, CDT, 512, 512, f"dw_bswa{l}")
    dqf, dcq, dkf, dvf, dck = _fox_bwd(sv["qf"], sv["kf"], sv["vf"], sv["c"], sv["ct"], sv["ofox"], sv["lse_f"], do_f,
                                       f"fox_bwd{l}")
    dqs, dkse, dvse, dbias, dsk = _swa_bwd(sv["qs"], sv["kse"], sv["vse"], lw["bias"], lw["sinks"], sv["oswa"],
                                           sv["lse_s"], do_s, f"swa_bwd{l}")
    dproj, dgn = _qknorm_bwd(sv["proj"], dqf, dkf, dvf, dqs, dkse, dvse, dcq, dck, dga, dgb,
                             lw["gfq"], lw["gfk"], lw["gsq"], lw["gsk"], lw["fb"], f"qknorm_bwd{l}")
    dam = _mm_nt(dproj, lw["w_mix"], f"d_am{l}")
    g["w_mix"] = _mm(sv["amt"], dproj, CDT, 512, 640, f"dw_mix{l}")
    dh, dhb, g["mix_norm"] = _rms_bwd(dam, sv["h1"], lw["mix_norm"], dh, f"rms_bwd_m{l}")
    g["dbias"], g["dsk"], g["dgn"] = dbias, dsk, dgn
    dh, dhb, g["ffn1_w_out"], g["ffn1_w_in"], g["ffn1_norm"] = _ffn_bwd(
        dh, dhb, sv["h0"], sv["a1t"], sv["gu1"], sv["s1t"], lw["ffn1_norm"], lw["ffn1_w_in"], lw["ffn1_w_out"], f"a{l}")
    return dh, dhb, g


def kernel(x, meta_tokens, rel_bias_table, ffn1_norm, ffn1_w_in, ffn1_w_out, mix_norm, w_in, forget_bias, fox_q_norm, fox_k_norm, swa_q_norm, swa_k_norm, swa_sinks, w_branch_fox, w_branch_swa, w_out, ffn2_norm, ffn2_w_in, ffn2_w_out, loss_target, m_meta_tokens, m_rel_bias_table, m_ffn1_norm, m_ffn1_w_in, m_ffn1_w_out, m_mix_norm, m_w_in, m_forget_bias, m_fox_q_norm, m_fox_k_norm, m_swa_q_norm, m_swa_k_norm, m_swa_sinks, m_w_branch_fox, m_w_branch_swa, m_w_out, m_ffn2_norm, m_ffn2_w_in, m_ffn2_w_out, v_meta_tokens, v_rel_bias_table, v_ffn1_norm, v_ffn1_w_in, v_ffn1_w_out, v_mix_norm, v_w_in, v_forget_bias, v_fox_q_norm, v_fox_k_norm, v_swa_q_norm, v_swa_k_norm, v_swa_sinks, v_w_branch_fox, v_w_branch_swa, v_w_out, v_ffn2_norm, v_ffn2_w_in, v_ffn2_w_out):
    names = ["meta_tokens", "rel_bias_table", "ffn1_norm", "ffn1_w_in", "ffn1_w_out", "mix_norm", "w_in", "forget_bias",
             "fox_q_norm", "fox_k_norm", "swa_q_norm", "swa_k_norm", "swa_sinks", "w_branch_fox", "w_branch_swa", "w_out",
             "ffn2_norm", "ffn2_w_in", "ffn2_w_out"]
    w = dict(zip(names, [meta_tokens, rel_bias_table, ffn1_norm, ffn1_w_in, ffn1_w_out, mix_norm, w_in, forget_bias,
                         fox_q_norm, fox_k_norm, swa_q_norm, swa_k_norm, swa_sinks, w_branch_fox, w_branch_swa, w_out,
                         ffn2_norm, ffn2_w_in, ffn2_w_out]))
    m = dict(zip(names, [m_meta_tokens, m_rel_bias_table, m_ffn1_norm, m_ffn1_w_in, m_ffn1_w_out, m_mix_norm, m_w_in,
                         m_forget_bias, m_fox_q_norm, m_fox_k_norm, m_swa_q_norm, m_swa_k_norm, m_swa_sinks,
                         m_w_branch_fox, m_w_branch_swa, m_w_out, m_ffn2_norm, m_ffn2_w_in, m_ffn2_w_out]))
    v = dict(zip(names, [v_meta_tokens, v_rel_bias_table, v_ffn1_norm, v_ffn1_w_in, v_ffn1_w_out, v_mix_norm, v_w_in,
                         v_forget_bias, v_fox_q_norm, v_fox_k_norm, v_swa_q_norm, v_swa_k_norm, v_swa_sinks,
                         v_w_branch_fox, v_w_branch_swa, v_w_out, v_ffn2_norm, v_ffn2_w_in, v_ffn2_w_out]))
    xi, yi, ci = lax.axis_index("x"), lax.axis_index("y"), lax.axis_index("c")
    shard = 2 * xi + yi
    seq = x.shape[1]
    t = seq + BLK

    wflat = _pack_shard({nm: w[nm].astype(CDT) for nm, _, _ in SHARD_ITEMS})
    mflat = meta_tokens.reshape(META_ROWS, 128)
    wall, mall = _gather_weights(wflat, mflat, "gather_weights")
    wall = lax.dynamic_update_slice(wall, wflat[None], (shard, 0, 0, 0))
    mall = lax.dynamic_update_slice(mall, mflat[None], (shard, 0, 0))
    full = _unpack_full(wall)
    meta_full = jnp.concatenate([mall[s].reshape(N_META, 256) for s in range(4)], axis=1)
    bias = _bias_fwd(rel_bias_table, "bias_fwd")
    lws = []
    for l in range(2):
        lw = dict(full[l])
        lw["w_mix"] = _mix_cols(lw.pop("w_in"))
        for nm in ("ffn1_norm", "mix_norm", "ffn2_norm"):
            lw[nm] = w[nm][l].reshape(1, D)
        lw["gfq"] = jnp.tile(fox_q_norm[l], 8).reshape(1, 512)
        lw["gfk"] = jnp.tile(fox_k_norm[l], 8).reshape(1, 512)
        lw["gsq"] = jnp.tile(swa_q_norm[l], 8).reshape(1, 512)
        lw["gsk"] = jnp.tile(swa_k_norm[l], 2).reshape(1, 128)
        lw["fb"] = jnp.pad(forget_bias[l], (0, 120)).reshape(1, 128)
        lw["sinks"] = swa_sinks[l]
        lw["bias"] = bias
        lws.append(lw)

    h = jnp.concatenate([jnp.zeros((PAD, D), F32), meta_full, x[0]], axis=0)
    saved = []
    for l in range(2):
        h, sv = _layer_fwd(h, lws[l], l)
        saved.append(sv)
    dh, dhb, lacc = _loss(h, loss_target[0], "loss")
    loss = lax.psum(lacc[0, 0], ("x", "y", "c"))

    grads = [None, None]
    for l in (1, 0):
        dh, dhb, grads[l] = _layer_bwd(dh, dhb, saved[l], lws[l], l)
    grad_x = dh[BLK:].reshape(1, seq, D)
    for l in range(2):
        grads[l]["w_in"] = _unmix_cols(grads[l].pop("w_mix"))
    dtab = _bias_bwd(grads[0]["dbias"] + grads[1]["dbias"], "bias_bwd")

    small = [dh[PAD:BLK].reshape(128, 128), _rows128(dtab[:, :N_BUCKETS].T, 2)]
    for nm in ("ffn1_norm", "mix_norm", "ffn2_norm"):
        small.append(jnp.stack([grads[l][nm][0] for l in range(2)]).reshape(16, 128))
    small.append(_rows128(jnp.stack([grads[l]["dgn"][4, :8] for l in range(2)]), 1))
    for row in range(4):
        small.append(jnp.stack([grads[l]["dgn"][row, :HD] for l in range(2)]).reshape(1, 128))
    dsk = [grads[l]["dsk"][:, 0, :] for l in range(2)]
    small.append(_rows128(jnp.stack([jnp.stack([d[:, 0], d[:, HD]], axis=1).reshape(8) for d in dsk]), 1))
    gsm = jnp.concatenate(small, axis=0)
    gsm = jnp.pad(gsm, ((0, SMALL_ROWS - gsm.shape[0]), (0, 0)))

    gfl = _pack_grads(grads)
    got, slots = _swap_halves(gfl, gsm, "swap_halves")
    ps = _pair_add(gfl, got, ci.reshape(1).astype(jnp.int32), "pair_add")
    got3 = _scatter_shards(ps, "scatter_shards")
    red_half = _sum4(ps, got3, shard.reshape(1).astype(jnp.int32), "sum4")
    sib_half = _join_halves(red_half, "join_halves")
    south = ci == 0
    red = jnp.concatenate([jnp.where(south, red_half, sib_half), jnp.where(south, sib_half, red_half)], axis=1)
    gs = _sum8(slots, "sum8")
    big = _unpack_shard(red)

    g_out = dict(big)
    g_out["meta_tokens"] = lax.dynamic_slice(gs[0:128].reshape(N_META, D), (0, shard * 256), (N_META, 256))
    off = 128
    for nm, rows in SMALL_ITEMS:
        n = w[nm].size
        g_out[nm] = gs[off:off + rows].reshape(-1)[:n].reshape(w[nm].shape)
        off += rows

    delta, new_m, new_v = {}, {}, {}
    for nm, (r, c), _ in SHARD_ITEMS:
        two = lambda a: a.reshape(2 * r, c)
        d_, m_, v_ = _adamw(two(w[nm]), two(g_out[nm]), two(m[nm]), two(v[nm]), f"adamw_{nm}")
        delta[nm], new_m[nm], new_v[nm] = (a.reshape(2, r, c) for a in (d_, m_, v_))
    small_names = ["meta_tokens"] + [nm for nm, _ in SMALL_ITEMS]
    small_rows = [META_ROWS] + [rows for _, rows in SMALL_ITEMS]

    def pack_small(src):
        buf = jnp.concatenate([_rows128(src[nm], rows) for nm, rows in zip(small_names, small_rows)], axis=0)
        return jnp.pad(buf, ((0, SMALL_ADAM_ROWS - buf.shape[0]), (0, 0)))

    d_, m_, v_ = _adamw(pack_small(w), pack_small(g_out), pack_small(m), pack_small(v), "adamw_small")
    off = 0
    for nm, rows in zip(small_names, small_rows):
        n = w[nm].size
        for dst, src in ((delta, d_), (new_m, m_), (new_v, v_)):
            dst[nm] = src[off:off + rows].reshape(-1)[:n].reshape(w[nm].shape)
        off += rows

    return (loss, grad_x, *[g_out[n] for n in names], *[delta[n] for n in names],
            *[new_m[n] for n in names], *[new_v[n] for n in names])
```

```python
import math

import numpy as np
import jax
import jax.numpy as jnp
from jax import lax
from jax.experimental import pallas as pl
from jax.experimental.pallas import tpu as pltpu

D = 1024
F = 2816
FT = F // 2
HD = 64
NPAIR = 4
N_META = 16
BLK = 128
PAD = BLK - N_META
EPS = 1e-6
NEG = -1e30
N_BUCKETS = 32
GA, GB, QA, KA, VA, QB, KB, VB, FA, DP = 0, 1024, 2048, 2560, 3072, 3584, 4096, 4224, 4352, 4480
D_IN = 4360
CDT = jnp.bfloat16
F32 = jnp.float32
VMEM_LIMIT = 48 * 1024 * 1024
MESH_ID = pl.DeviceIdType.MESH

ADAM_LR, ADAM_B1, ADAM_B2, ADAM_EPS, ADAM_WD, ADAM_STEP = 0.001, 0.9, 0.999, 1e-08, 0.01, 10

SHARD_ITEMS = (
    ("ffn1_w_in", (1024, 1408), "col"),
    ("ffn1_w_out", (704, 1024), "row"),
    ("w_in", (1024, 1090), "col"),
    ("w_branch_fox", (512, 256), "col"),
    ("w_branch_swa", (512, 256), "col"),
    ("w_out", (256, 1024), "row"),
    ("ffn2_w_in", (1024, 1408), "col"),
    ("ffn2_w_out", (704, 1024), "row"),
)
LAYER_RAW = sum(r * c for _, (r, c), _ in SHARD_ITEMS) // 128
PACK_TILE = 5840
HALF_ROWS = 4 * PACK_TILE
LAYER_ROWS = 2 * HALF_ROWS
SMALL_ROWS = 192
META_ROWS = 32


def _row_tile(t):
    return 384 if t % 384 == 0 else 128


def _dot(a, b):
    return jnp.dot(a, b, preferred_element_type=F32)


def _dot_nt(a, b):
    return lax.dot_general(a, b, (((1,), (1,)), ((), ())), preferred_element_type=F32)


def _dot_hi(a, b):
    return jnp.dot(a, b, preferred_element_type=F32, precision=lax.Precision.HIGHEST)


def _sigmoid(x):
    return 1.0 / (1.0 + jnp.exp(-x))


def _iota(shape, dim):
    return lax.broadcasted_iota(jnp.int32, shape, dim)


def _params(sem=None):
    return pltpu.CompilerParams(dimension_semantics=sem, vmem_limit_bytes=VMEM_LIMIT)


def _sds(shape, dtype):
    return jax.ShapeDtypeStruct(shape, dtype)


def _rms_fwd(h, g, name):
    t = h.shape[0]
    tm = _row_tile(t)

    def body(h_ref, g_ref, a_ref, at_ref):
        x = h_ref[...]
        ms = jnp.mean(x * x, axis=-1, keepdims=True)
        a = x * lax.rsqrt(ms + EPS) * g_ref[...]
        a_ref[...] = a.astype(CDT)
        at_ref[...] = a.T.astype(CDT)

    return pl.pallas_call(
        body, name=name, grid=(t // tm,),
        in_specs=[pl.BlockSpec((tm, D), lambda i: (i, 0)), pl.BlockSpec((1, D), lambda i: (0, 0))],
        out_specs=[pl.BlockSpec((tm, D), lambda i: (i, 0)), pl.BlockSpec((D, tm), lambda i: (0, i))],
        out_shape=[_sds((t, D), CDT), _sds((D, t), CDT)],
        compiler_params=_params(("parallel",)),
    )(h, g)


def _rms_bwd(da, h, g, dres, name):
    t = h.shape[0]
    tm = _row_tile(t)

    def body(da_ref, h_ref, g_ref, dr_ref, dh_ref, dhb_ref, dg_ref):
        i = pl.program_id(0)
        x = h_ref[...]
        da_ = da_ref[...]
        r = lax.rsqrt(jnp.mean(x * x, axis=-1, keepdims=True) + EPS)
        xh = x * r
        day = da_ * g_ref[...]
        dx = r * (day - xh * jnp.mean(day * xh, axis=-1, keepdims=True))
        dh = dr_ref[...] + dx
        dh_ref[...] = dh
        dhb_ref[...] = dh.astype(CDT)

        @pl.when(i == 0)
        def _():
            dg_ref[...] = jnp.zeros(dg_ref.shape, F32)

        dg_ref[0:1, :] += jnp.sum(da_ * xh, axis=0, keepdims=True)

    row = pl.BlockSpec((tm, D), lambda i: (i, 0))
    return pl.pallas_call(
        body, name=name, grid=(t // tm,),
        in_specs=[row, row, pl.BlockSpec((1, D), lambda i: (0, 0)), row],
        out_specs=[row, row, pl.BlockSpec((8, D), lambda i: (0, 0))],
        out_shape=[_sds((t, D), F32), _sds((t, D), CDT), _sds((8, D), F32)],
        compiler_params=_params(("arbitrary",)),
    )(da, h, g, dres)


def _ffn_in(a, w_in, name):
    t = a.shape[0]
    tm = _row_tile(t)
    tn = FT
    nj = F // tn

    def body(a_ref, wg_ref, wu_ref, gu_ref, s_ref, st_ref):
        a_ = a_ref[...]
        g = _dot(a_, wg_ref[...])
        u = _dot(a_, wu_ref[...])
        s = g * _sigmoid(g) * u
        gu_ref[0] = g.astype(CDT)
        gu_ref[1] = u.astype(CDT)
        s_ref[...] = s.astype(CDT)
        st_ref[...] = s.T.astype(CDT)

    return pl.pallas_call(
        body, name=name, grid=(nj, t // tm),
        in_specs=[pl.BlockSpec((tm, D), lambda j, i: (i, 0)),
                  pl.BlockSpec((D, tn), lambda j, i: (0, j)),
                  pl.BlockSpec((D, tn), lambda j, i: (0, j + nj))],
        out_specs=[pl.BlockSpec((2, tm, tn), lambda j, i: (0, i, j)),
                   pl.BlockSpec((tm, tn), lambda j, i: (i, j)),
                   pl.BlockSpec((tn, tm), lambda j, i: (j, i))],
        out_shape=[_sds((2, t, F), CDT), _sds((t, F), CDT), _sds((F, t), CDT)],
        compiler_params=_params(("parallel", "parallel")),
    )(a, w_in, w_in)


def _mm_res(a, b, res, scale, name):
    t, k = a.shape
    n = b.shape[1]
    tm = _row_tile(t)
    tn = 512

    def body(a_ref, b_ref, r_ref, o_ref):
        o_ref[...] = r_ref[...] + scale * _dot(a_ref[...], b_ref[...])

    return pl.pallas_call(
        body, name=name, grid=(t // tm, n // tn),
        in_specs=[pl.BlockSpec((tm, k), lambda i, j: (i, 0)),
                  pl.BlockSpec((k, tn), lambda i, j: (0, j)),
                  pl.BlockSpec((tm, tn), lambda i, j: (i, j))],
        out_specs=pl.BlockSpec((tm, tn), lambda i, j: (i, j)),
        out_shape=_sds((t, n), F32),
        compiler_params=_params(("parallel", "parallel")),
    )(a, b, res)


def _mm(a, b, out_dtype, tm, tn, name, scale=1.0, b_resident=False):
    m, k = a.shape
    order = (lambda j, i: (i, j)) if b_resident else (lambda i, j: (i, j))
    if b.ndim == 3:
        nh = b.shape[2] // tn
        n = 2 * b.shape[2]
        b_spec = pl.BlockSpec((None, k, tn), lambda *g: (order(*g)[1] // nh, 0, order(*g)[1] % nh))
    else:
        n = b.shape[1]
        b_spec = pl.BlockSpec((k, tn), lambda *g: (0, order(*g)[1]))

    def body(a_ref, b_ref, o_ref):
        o_ref[...] = (scale * _dot(a_ref[...], b_ref[...])).astype(out_dtype)

    return pl.pallas_call(
        body, name=name, grid=(n // tn, m // tm) if b_resident else (m // tm, n // tn),
        in_specs=[pl.BlockSpec((tm, k), lambda *g: (order(*g)[0], 0)), b_spec],
        out_specs=pl.BlockSpec((tm, tn), lambda *g: order(*g)),
        out_shape=_sds((m, n), out_dtype),
        compiler_params=_params(("parallel", "parallel")),
    )(a, b)


def _mm_nt(a, b, name):
    m, n = a.shape
    k = b.shape[0]
    tm = _row_tile(m)
    tk = 512

    def body(a_ref, b_ref, o_ref):
        o_ref[...] = _dot_nt(a_ref[...], b_ref[...])

    return pl.pallas_call(
        body, name=name, grid=(m // tm, k // tk),
        in_specs=[pl.BlockSpec((tm, n), lambda i, j: (i, 0)), pl.BlockSpec((tk, n), lambda i, j: (j, 0))],
        out_specs=pl.BlockSpec((tm, tk), lambda i, j: (i, j)),
        out_shape=_sds((m, k), F32),
        compiler_params=_params(("parallel", "parallel")),
    )(a, b)


def _ffn_bwd_mid(dhb, w_out, gu, name):
    t = dhb.shape[0]
    tm = _row_tile(t)
    tn = FT

    def body(dh_ref, w_ref, gu_ref, o_ref):
        ds = 0.5 * _dot_nt(dh_ref[...], w_ref[...])
        g = gu_ref[0].astype(F32)
        u = gu_ref[1].astype(F32)
        sg = _sigmoid(g)
        o_ref[0] = (ds * u * (sg * (1.0 + g * (1.0 - sg)))).astype(CDT)
        o_ref[1] = (ds * (g * sg)).astype(CDT)

    return pl.pallas_call(
        body, name=name, grid=(F // tn, t // tm),
        in_specs=[pl.BlockSpec((tm, D), lambda j, i: (i, 0)),
                  pl.BlockSpec((tn, D), lambda j, i: (j, 0)),
                  pl.BlockSpec((2, tm, tn), lambda j, i: (0, i, j))],
        out_specs=pl.BlockSpec((2, tm, tn), lambda j, i: (0, i, j)),
        out_shape=_sds((2, t, F), CDT),
        compiler_params=_params(("parallel", "parallel")),
    )(dhb, w_out, gu)


def _ffn_bwd_in(dgu, w_in, name):
    t = dgu.shape[1]
    tm = _row_tile(t)
    tk = 512

    def body(dg_ref, wg_ref, wu_ref, o_ref):
        o_ref[...] = _dot_nt(dg_ref[0], wg_ref[...]) + _dot_nt(dg_ref[1], wu_ref[...])

    return pl.pallas_call(
        body, name=name, grid=(t // tm, D // tk),
        in_specs=[pl.BlockSpec((2, tm, F), lambda i, j: (0, i, 0)),
                  pl.BlockSpec((tk, F), lambda i, j: (j, 0)),
                  pl.BlockSpec((tk, F), lambda i, j: (j, 1))],
        out_specs=pl.BlockSpec((tm, tk), lambda i, j: (i, j)),
        out_shape=_sds((t, D), F32),
        compiler_params=_params(("parallel", "parallel")),
    )(dgu, w_in, w_in)


def _loss(h, target, name):
    t = h.shape[0]

    def body(h_ref, t_ref, dh_ref, dhb_ref, l_ref):
        i = pl.program_id(0)

        @pl.when(i == 0)
        def _():
            l_ref[...] = jnp.zeros(l_ref.shape, F32)
            dh_ref[...] = jnp.zeros(dh_ref.shape, F32)
            dhb_ref[...] = jnp.zeros(dhb_ref.shape, CDT)

        @pl.when(i > 0)
        def _():
            err = h_ref[...] - t_ref[...]
            l_ref[...] += (0.5 / D) * jnp.sum(err * err)
            d = err * (1.0 / D)
            dh_ref[...] = d
            dhb_ref[...] = d.astype(CDT)

    row = pl.BlockSpec((BLK, D), lambda i: (i, 0))
    return pl.pallas_call(
        body, name=name, grid=(t // BLK,),
        in_specs=[row, pl.BlockSpec((BLK, D), lambda i: (jnp.maximum(i - 1, 0), 0))],
        out_specs=[row, row, pl.BlockSpec((8, 128), lambda i: (0, 0))],
        out_shape=[_sds((t, D), F32), _sds((t, D), CDT), _sds((8, 128), F32)],
        compiler_params=_params(("arbitrary",)),
    )(h, target)


def _block_diag():
    return (_iota((128, 128), 0) // HD == _iota((128, 128), 1) // HD).astype(F32)


def _dup_halves(x, lo):
    sw = pltpu.roll(x, 64, 1)
    return jnp.where(lo, x, sw), jnp.where(lo, sw, x)


def _qknorm_fwd(proj, gfq, gfk, gsq, gsk, fb, name):
    t = proj.shape[0]
    tm = _row_tile(t)

    def body(qa, ka, va, qb, kb, vb, fa, gfq_r, gfk_r, gsq_r, gsk_r, fb_r,
             qf_o, kf_o, vf_o, qs_o, kse_o, vse_o, c_o, ct_o, carry):
        i = pl.program_id(0)
        bd = _block_diag()
        lane = _iota((1, 128), 1)
        lo = lane < HD

        def hnorm(x, g):
            ms = _dot_hi(x * x, bd) * (1.0 / HD)
            return x * lax.rsqrt(ms + EPS) * g

        for ch in range(4):
            sl = slice(128 * ch, 128 * (ch + 1))
            qf_o[:, sl] = (hnorm(qa[:, sl], gfq_r[:, sl]) * 0.125).astype(CDT)
            kf_o[:, sl] = hnorm(ka[:, sl], gfk_r[:, sl]).astype(CDT)
            qs_o[:, sl] = (hnorm(qb[:, sl], gsq_r[:, sl]) * 0.125).astype(CDT)
        vf_o[...] = va[...].astype(CDT)
        k0, k1 = _dup_halves(hnorm(kb[...], gsk_r[...]), lo)
        kse_o[0] = k0.astype(CDT)
        kse_o[1] = k1.astype(CDT)
        v0, v1 = _dup_halves(vb[...], lo)
        vse_o[0] = v0.astype(CDT)
        vse_o[1] = v1.astype(CDT)

        z = fa[...] + fb_r[...]
        lf = jnp.minimum(z, 0.0) - jnp.log(1.0 + jnp.exp(-jnp.abs(z)))
        lf = jnp.where(lane < 8, lf, 0.0)
        ltri = (_iota((tm, tm), 1) <= _iota((tm, tm), 0)).astype(F32)

        @pl.when(i == 0)
        def _():
            carry[...] = jnp.zeros(carry.shape, F32)

        c = _dot_hi(ltri, lf) + carry[0:1, :]
        carry[0:1, :] = c[tm - 1:tm, :]
        c_o[...] = c
        ct_o[...] = c.T[0:8, :]

    def col(width, off):
        return pl.BlockSpec((tm, width), lambda i: (i, off // width))

    def vec(width):
        return pl.BlockSpec((1, width), lambda i: (0, 0))

    return pl.pallas_call(
        body, name=name, grid=(t // tm,),
        in_specs=[col(512, QA), col(512, KA), col(512, VA), col(512, QB), col(128, KB), col(128, VB), col(128, FA),
                  vec(512), vec(512), vec(512), vec(128), vec(128)],
        out_specs=[pl.BlockSpec((tm, 512), lambda i: (i, 0))] * 4
        + [pl.BlockSpec((2, tm, 128), lambda i: (0, i, 0))] * 2
        + [pl.BlockSpec((tm, 128), lambda i: (i, 0)), pl.BlockSpec((8, tm), lambda i: (0, i))],
        out_shape=[_sds((t, 512), CDT)] * 4 + [_sds((2, t, 128), CDT)] * 2 + [_sds((t, 128), F32), _sds((8, t), F32)],
        scratch_shapes=[pltpu.VMEM((8, 128), F32)],
        compiler_params=_params(("arbitrary",)),
    )(proj, proj, proj, proj, proj, proj, proj, gfq, gfk, gsq, gsk, fb)


def _qknorm_bwd(proj, dqf, dkf, dvf, dqs, dkse, dvse, dcq, dck, dga, dgb, gfq, gfk, gsq, gsk, fb, name):
    t = proj.shape[0]
    tm = _row_tile(t)
    nt = t // tm

    def body(qa, ka, qb, kb, fa, dqf_r, dkf_r, dvf_r, dqs_r, dkse_r, dvse_r, dcq_r, dck_r, dga_r, dgb_r,
             gfq_r, gfk_r, gsq_r, gsk_r, fb_r, dp_o, dgn_o, carry, acc):
        i = pl.program_id(0)
        bd = _block_diag()
        lane = _iota((1, 128), 1)
        lo = lane < HD

        @pl.when(i == 0)
        def _():
            carry[...] = jnp.zeros(carry.shape, F32)
            acc[...] = jnp.zeros(acc.shape, F32)

        def hnorm_bwd(x, g, dy):
            r = lax.rsqrt(_dot_hi(x * x, bd) * (1.0 / HD) + EPS)
            xh = x * r
            day = dy * g
            dx = r * (day - xh * (_dot_hi(day * xh, bd) * (1.0 / HD)))
            return dx, jnp.sum(dy * xh, axis=0, keepdims=True)

        for ch in range(4):
            sl = slice(128 * ch, 128 * (ch + 1))
            dx, dg = hnorm_bwd(qa[:, sl], gfq_r[:, sl], dqf_r[:, sl] * 0.125)
            dp_o[:, QA + 128 * ch:QA + 128 * (ch + 1)] = dx.astype(CDT)
            acc[0:1, sl] += dg
            dx, dg = hnorm_bwd(ka[:, sl], gfk_r[:, sl], dkf_r[:, sl])
            dp_o[:, KA + 128 * ch:KA + 128 * (ch + 1)] = dx.astype(CDT)
            acc[1:2, sl] += dg
            dx, dg = hnorm_bwd(qb[:, sl], gsq_r[:, sl], dqs_r[:, sl] * 0.125)
            dp_o[:, QB + 128 * ch:QB + 128 * (ch + 1)] = dx.astype(CDT)
            acc[2:3, sl] += dg
        dp_o[:, VA:VA + 512] = dvf_r[...].astype(CDT)
        dp_o[:, GA:GA + D] = dga_r[...]
        dp_o[:, GB:GB + D] = dgb_r[...]

        def fold(x):
            e0 = x[0]
            e1 = x[1]
            return jnp.where(lo, e0 + pltpu.roll(e0, 64, 1), e1 + pltpu.roll(e1, 64, 1))

        dx, dg = hnorm_bwd(kb[...], gsk_r[...], fold(dkse_r))
        dp_o[:, KB:KB + 128] = dx.astype(CDT)
        acc[3:4, 0:128] += dg
        dp_o[:, VB:VB + 128] = fold(dvse_r).astype(CDT)

        rr = _iota((512, 128), 0)
        hh = _iota((512, 128), 1)
        sel = ((rr == (hh >> 1) * 128 + (hh & 1) * HD) & (hh < 8)).astype(F32)
        dcs = _dot_hi(dcq_r[...] - dck_r[...], sel)
        utri = (_iota((tm, tm), 1) >= _iota((tm, tm), 0)).astype(F32)
        dlf = _dot_hi(utri, dcs) + carry[0:1, :]
        carry[0:1, :] = dlf[0:1, :]
        z = fa[...] + fb_r[...]
        dfa = jnp.where(lane < 8, dlf * _sigmoid(-z), 0.0)
        dp_o[:, FA:FA + 128] = dfa.astype(CDT)
        acc[4:5, 0:128] += jnp.sum(dfa, axis=0, keepdims=True)

        @pl.when(i == nt - 1)
        def _():
            foldm = ((_iota((512, 128), 0) & (HD - 1)) == _iota((512, 128), 1)).astype(F32)
            dgn_o[...] = _dot_hi(acc[...], foldm)

    def col(width, off):
        return pl.BlockSpec((tm, width), lambda i: (nt - 1 - i, off // width))

    def rows(width):
        return pl.BlockSpec((tm, width), lambda i: (nt - 1 - i, 0))

    def vec(width):
        return pl.BlockSpec((1, width), lambda i: (0, 0))

    pair = pl.BlockSpec((2, tm, 128), lambda i: (0, nt - 1 - i, 0))
    return pl.pallas_call(
        body, name=name, grid=(nt,),
        in_specs=[col(512, QA), col(512, KA), col(512, QB), col(128, KB), col(128, FA),
                  rows(512), rows(512), rows(512), rows(512), pair, pair, rows(512), rows(512), rows(D), rows(D),
                  vec(512), vec(512), vec(512), vec(128), vec(128)],
        out_specs=[rows(DP), pl.BlockSpec((8, 128), lambda i: (0, 0))],
        out_shape=[_sds((t, DP), CDT), _sds((8, 128), F32)],
        scratch_shapes=[pltpu.VMEM((8, 128), F32), pltpu.VMEM((8, 512), F32)],
        compiler_params=_params(("arbitrary",)),
    )(proj, proj, proj, proj, proj, dqf, dkf, dvf, dqs, dkse, dvse, dcq, dck, dga, dgb, gfq, gfk, gsq, gsk, fb)


def _gate_fwd(ofox, oswa, wbf, wbs, proj, name):
    t = ofox.shape[0]
    tm = _row_tile(t)
    tn = 512

    def body(of_r, os_r, wf_r, ws_r, ga_r, gb_r, y_o, yt_o, pf_o, ps_o, oft_o, ost_o):
        j = pl.program_id(1)
        pf = _dot(of_r[...], wf_r[...])
        ps = _dot(os_r[...], ws_r[...])
        y = _sigmoid(ga_r[...]) * pf + _sigmoid(gb_r[...]) * ps
        y_o[...] = y.astype(CDT)
        yt_o[...] = y.T.astype(CDT)
        pf_o[...] = pf.astype(CDT)
        ps_o[...] = ps.astype(CDT)

        @pl.when(j == 0)
        def _():
            oft_o[...] = of_r[...].astype(F32).T.astype(CDT)
            ost_o[...] = os_r[...].astype(F32).T.astype(CDT)

    tile = pl.BlockSpec((tm, tn), lambda i, j: (i, j))
    return pl.pallas_call(
        body, name=name, grid=(t // tm, D // tn),
        in_specs=[pl.BlockSpec((tm, 512), lambda i, j: (i, 0)), pl.BlockSpec((tm, 512), lambda i, j: (i, 0)),
                  pl.BlockSpec((512, tn), lambda i, j: (0, j)), pl.BlockSpec((512, tn), lambda i, j: (0, j)),
                  pl.BlockSpec((tm, tn), lambda i, j: (i, GA // tn + j)),
                  pl.BlockSpec((tm, tn), lambda i, j: (i, GB // tn + j))],
        out_specs=[tile, pl.BlockSpec((tn, tm), lambda i, j: (j, i)), tile, tile,
                   pl.BlockSpec((512, tm), lambda i, j: (0, i)), pl.BlockSpec((512, tm), lambda i, j: (0, i))],
        out_shape=[_sds((t, D), CDT), _sds((D, t), CDT), _sds((t, D), CDT), _sds((t, D), CDT),
                   _sds((512, t), CDT), _sds((512, t), CDT)],
        compiler_params=_params(("parallel", "arbitrary")),
    )(ofox, oswa, wbf, wbs, proj, proj)


def _gate_bwd(dy, pf, ps, proj, name):
    t = dy.shape[0]
    tm = _row_tile(t)
    tn = 512

    def body(dy_r, pf_r, ps_r, ga_r, gb_r, dpf_o, dps_o, dga_o, dgb_o):
        dy_ = dy_r[...]
        sa = _sigmoid(ga_r[...])
        sb = _sigmoid(gb_r[...])
        dpf_o[...] = (dy_ * sa).astype(CDT)
        dps_o[...] = (dy_ * sb).astype(CDT)
        dga_o[...] = (dy_ * pf_r[...].astype(F32) * (sa * (1.0 - sa))).astype(CDT)
        dgb_o[...] = (dy_ * ps_r[...].astype(F32) * (sb * (1.0 - sb))).astype(CDT)

    tile = pl.BlockSpec((tm, tn), lambda i, j: (i, j))
    return pl.pallas_call(
        body, name=name, grid=(t // tm, D // tn),
        in_specs=[tile, tile, tile,
                  pl.BlockSpec((tm, tn), lambda i, j: (i, GA // tn + j)),
                  pl.BlockSpec((tm, tn), lambda i, j: (i, GB // tn + j))],
        out_specs=[tile] * 4,
        out_shape=[_sds((t, D), CDT)] * 4,
        compiler_params=_params(("parallel", "parallel")),
    )(dy, pf, ps, proj, proj)


def _tri_steps(n, by_key):
    if by_key:
        pairs = [(i, j) for j in range(n) for i in range(j, n)]
    else:
        pairs = [(i, j) for i in range(n) for j in range(i + 1)]
    return (np.array([p[0] for p in pairs], np.int32), np.array([p[1] for p in pairs], np.int32))


def _head_col(blk, lane, h):
    return jnp.sum(jnp.where(lane == h, blk, 0.0), axis=1, keepdims=True)


def _head_row(blk, sub, h):
    return jnp.sum(jnp.where(sub == h, blk, 0.0), axis=0, keepdims=True)


def _fox_fwd(qf, kf, vf, c, ct, name):
    t = qf.shape[0]
    ta = _row_tile(t)
    qi, kj = _tri_steps(t // ta, by_key=False)

    def body(qi_r, kj_r, q_r, k_r, v_r, c_r, ct_r, o_o, lse_o, m_sc, l_sc, acc_sc):
        p = pl.program_id(0)
        n = pl.program_id(1)
        i = qi_r[n]
        j = kj_r[n]
        lane = _iota((1, 128), 1)
        lo = lane < HD

        @pl.when(j == 0)
        def _():
            m_sc[...] = jnp.full(m_sc.shape, NEG, F32)
            l_sc[...] = jnp.zeros(l_sc.shape, F32)
            acc_sc[...] = jnp.zeros(acc_sc.shape, F32)

        def step(masked):
            q = q_r[...]
            k = k_r[...]
            v = v_r[...]
            if masked:
                rows = i * ta + _iota((ta, 1), 0)
                cols = j * ta + _iota((1, ta), 1)
                mask = (cols <= rows) & (cols >= PAD)
            sub = _iota((8, 1), 0)
            alphas, pvs = [], []
            for e in (0, 1):
                sel = lo if e == 0 else jnp.logical_not(lo)
                s = _dot_nt(jnp.where(sel, q, 0), k)
                s = s + _head_col(c_r[...], lane, 2 * p + e) - _head_row(ct_r[...], sub, 2 * p + e)
                if masked:
                    s = jnp.where(mask, s, NEG)
                m_prev = m_sc[e][:, 0:1]
                m_new = jnp.maximum(m_prev, jnp.max(s, axis=1, keepdims=True))
                alpha = jnp.exp(m_prev - m_new)
                pe = jnp.exp(s - m_new)
                l_new = alpha * l_sc[e][:, 0:1] + jnp.sum(pe, axis=1, keepdims=True)
                m_sc[e] = jnp.broadcast_to(m_new, (ta, 128))
                l_sc[e] = jnp.broadcast_to(l_new, (ta, 128))
                alphas.append(alpha)
                pvs.append(_dot(pe.astype(CDT), v))
            acc_sc[...] = acc_sc[...] * jnp.where(lo, alphas[0], alphas[1]) + jnp.where(lo, pvs[0], pvs[1])

        edge = (j == i) | (j == 0)

        @pl.when(edge)
        def _():
            step(True)

        @pl.when(jnp.logical_not(edge))
        def _():
            step(False)

        @pl.when(j == i)
        def _():
            l = jnp.where(lo, l_sc[0], l_sc[1])
            o_o[...] = (acc_sc[...] / l).astype(CDT)
            lse_o[...] = jnp.where(lo, m_sc[0], m_sc[1]) + jnp.log(l)

    qblk = pl.BlockSpec((ta, 128), lambda p, n, qi_r, kj_r: (qi_r[n], p))
    kblk = pl.BlockSpec((ta, 128), lambda p, n, qi_r, kj_r: (kj_r[n], p))
    grid_spec = pltpu.PrefetchScalarGridSpec(
        num_scalar_prefetch=2, grid=(NPAIR, len(qi)),
        in_specs=[qblk, kblk, kblk,
                  pl.BlockSpec((ta, 128), lambda p, n, qi_r, kj_r: (qi_r[n], 0)),
                  pl.BlockSpec((8, ta), lambda p, n, qi_r, kj_r: (0, kj_r[n]))],
        out_specs=[qblk, qblk],
        scratch_shapes=[pltpu.VMEM((2, ta, 128), F32), pltpu.VMEM((2, ta, 128), F32), pltpu.VMEM((ta, 128), F32)],
    )
    return pl.pallas_call(
        body, name=name, grid_spec=grid_spec,
        out_shape=[_sds((t, 512), CDT), _sds((t, 512), F32)],
        compiler_params=_params(("parallel", "arbitrary")),
    )(jnp.asarray(qi), jnp.asarray(kj), qf, kf, vf, c, ct)


def _fox_bwd(qf, kf, vf, c, ct, o, lse, do, name):
    t = qf.shape[0]
    ta = _row_tile(t)
    nq = t // ta
    qi, kj = _tri_steps(nq, by_key=True)

    def body(qi_r, kj_r, q_r, k_r, v_r, c_r, ct_r, o_r, lse_r, do_r,
             dq_o, dcq_o, dk_o, dv_o, dck_o, dk_sc, dv_sc, dck_sc):
        p = pl.program_id(0)
        n = pl.program_id(1)
        i = qi_r[n]
        j = kj_r[n]
        lane = _iota((1, 128), 1)
        lo = lane < HD

        @pl.when(n == 0)
        def _():
            dq_o[...] = jnp.zeros(dq_o.shape, F32)
            dcq_o[...] = jnp.zeros(dcq_o.shape, F32)

        @pl.when(i == j)
        def _():
            dk_sc[...] = jnp.zeros(dk_sc.shape, F32)
            dv_sc[...] = jnp.zeros(dv_sc.shape, F32)
            dck_sc[...] = jnp.zeros(dck_sc.shape, F32)

        def step(masked):
            q = q_r[...]
            k = k_r[...]
            v = v_r[...]
            do_ = do_r[...]
            dd = do_ * o_r[...].astype(F32)
            lse = lse_r[...]
            if masked:
                rows = i * ta + _iota((ta, 1), 0)
                cols = j * ta + _iota((1, ta), 1)
                mask = (cols <= rows) & (cols >= PAD)
            sub = _iota((8, 1), 0)
            dq_add = jnp.zeros((ta, 128), F32)
            dk_add = jnp.zeros((ta, 128), F32)
            dv_add = jnp.zeros((ta, 128), F32)
            rsum, csum = [], []
            for e in (0, 1):
                sel = lo if e == 0 else jnp.logical_not(lo)
                qe = jnp.where(sel, q, 0)
                doe = jnp.where(sel, do_, 0.0).astype(CDT)
                s = _dot_nt(qe, k)
                s = s + _head_col(c_r[...], lane, 2 * p + e) - _head_row(ct_r[...], sub, 2 * p + e)
                if masked:
                    s = jnp.where(mask, s, NEG)
                pr = jnp.exp(s - lse[:, HD * e:HD * e + 1])
                dv_add = dv_add + _dot(pr.T.astype(CDT), doe)
                dp = _dot_nt(doe, v)
                delta = jnp.sum(jnp.where(sel, dd, 0.0), axis=1, keepdims=True)
                ds = pr * (dp - delta)
                dq_add = dq_add + _dot(ds.astype(CDT), jnp.where(sel, k, 0))
                dst = ds.T
                dk_add = dk_add + _dot(dst.astype(CDT), qe)
                rsum.append(jnp.sum(ds, axis=1, keepdims=True))
                csum.append(jnp.sum(dst, axis=1, keepdims=True))
            rs = pl.ds(pl.multiple_of(i * ta, ta), ta)
            dq_o[rs, :] += dq_add
            dcq_o[rs, :] += jnp.where(lo, rsum[0], rsum[1])
            dk_sc[...] += dk_add
            dv_sc[...] += dv_add
            dck_sc[...] += jnp.where(lo, csum[0], csum[1])

        edge = (j == i) | (j == 0)

        @pl.when(edge)
        def _():
            step(True)

        @pl.when(jnp.logical_not(edge))
        def _():
            step(False)

        @pl.when(i == nq - 1)
        def _():
            dk_o[...] = dk_sc[...]
            dv_o[...] = dv_sc[...]
            dck_o[...] = dck_sc[...]

    qblk = pl.BlockSpec((ta, 128), lambda p, n, qi_r, kj_r: (qi_r[n], p))
    kblk = pl.BlockSpec((ta, 128), lambda p, n, qi_r, kj_r: (kj_r[n], p))
    whole = pl.BlockSpec((t, 128), lambda p, n, qi_r, kj_r: (0, p))
    grid_spec = pltpu.PrefetchScalarGridSpec(
        num_scalar_prefetch=2, grid=(NPAIR, len(qi)),
        in_specs=[qblk, kblk, kblk,
                  pl.BlockSpec((ta, 128), lambda p, n, qi_r, kj_r: (qi_r[n], 0)),
                  pl.BlockSpec((8, ta), lambda p, n, qi_r, kj_r: (0, kj_r[n])),
                  qblk, qblk, qblk],
        out_specs=[whole, whole, kblk, kblk, kblk],
        scratch_shapes=[pltpu.VMEM((ta, 128), F32)] * 3,
    )
    return pl.pallas_call(
        body, name=name, grid_spec=grid_spec,
        out_shape=[_sds((t, 512), F32)] * 5,
        compiler_params=_params(("parallel", "arbitrary")),
    )(jnp.asarray(qi), jnp.asarray(kj), qf, kf, vf, c, ct, o, lse, do)


def _bucket_table():
    r = np.arange(BLK)[:, None]
    c = np.arange(3 * BLK)[None, :]
    d = np.where(c < BLK, r + BLK - c, r - (c - BLK))
    n = np.maximum(d, 0)
    max_exact = N_BUCKETS // 2
    nf = np.maximum(n, 1).astype(np.float32)
    large = max_exact + (np.log(nf / max_exact) / math.log(BLK / max_exact) * (N_BUCKETS - max_exact)).astype(np.int32)
    large = np.minimum(large, N_BUCKETS - 1)
    b = np.where(n < max_exact, n, large)
    return np.where(c < 2 * BLK, b, N_BUCKETS - 1).astype(np.int32)


def _bias_fwd(table, name):
    bucket = jnp.asarray(_bucket_table())

    def body(tab_r, b_r, o_o):
        h = pl.program_id(0)
        b = b_r[...]
        acc = jnp.zeros(b.shape, F32)
        for k in range(N_BUCKETS):
            acc = jnp.where(b == k, tab_r[k, h], acc)
        o_o[...] = acc

    return pl.pallas_call(
        body, name=name, grid=(8,),
        in_specs=[pl.BlockSpec(memory_space=pltpu.SMEM), pl.BlockSpec((BLK, 3 * BLK), lambda h: (0, 0))],
        out_specs=pl.BlockSpec((None, BLK, 3 * BLK), lambda h: (h, 0, 0)),
        out_shape=_sds((8, BLK, 3 * BLK), F32),
        compiler_params=_params(("parallel",)),
    )(table, bucket)


def _bias_bwd(dbias, name):
    bucket = jnp.asarray(_bucket_table())

    def body(d_r, b_r, o_o):
        h = pl.program_id(0)
        b = b_r[...]
        d = d_r[...]
        lane = _iota((1, 128), 1)
        row = jnp.zeros((1, 128), F32)
        for k in range(N_BUCKETS):
            row = jnp.where(lane == k, jnp.sum(jnp.where(b == k, d, 0.0)), row)
        o_o[pl.ds(h, 1), :] = row

    return pl.pallas_call(
        body, name=name, grid=(8,),
        in_specs=[pl.BlockSpec((None, BLK, 3 * BLK), lambda h: (h, 0, 0)), pl.BlockSpec((BLK, 3 * BLK), lambda h: (0, 0))],
        out_specs=pl.BlockSpec((8, 128), lambda h: (0, 0)),
        out_shape=_sds((8, 128), F32),
        compiler_params=_params(("arbitrary",)),
    )(dbias, bucket)


def _swa_valid(i):
    r = _iota((BLK, 1), 0)
    c = _iota((1, 3 * BLK), 1)
    prev = (c < BLK) & (c > r) & (i >= 1) & ((i - 1) * BLK + c >= PAD)
    cc = c - BLK
    cur = (c >= BLK) & (c < 2 * BLK) & (cc <= r) & (i * BLK + cc >= PAD)
    cm = c - 2 * BLK
    meta = (c >= 2 * BLK) & (cm >= PAD) & (i * BLK + r - cm >= BLK)
    return prev | cur | meta


def _swa_kv_specs():
    def at(f):
        return pl.BlockSpec((None, BLK, 128), lambda p, i: (p // 2, f(i), 0))
    return [at(lambda i: jnp.maximum(i - 1, 0)), at(lambda i: i), at(lambda i: 0)]


def _swa_fwd(qs, kse, vse, bias, sinks, name):
    t = qs.shape[0]

    def body(sink_r, q_r, kp_r, kc_r, km_r, vp_r, vc_r, vm_r, b_r, o_o, lse_o):
        p = pl.program_id(0)
        i = pl.program_id(1)
        lo = _iota((1, 128), 1) < HD
        q = q_r[...]
        k3 = jnp.concatenate([kp_r[...], kc_r[...], km_r[...]], axis=0)
        v3 = jnp.concatenate([vp_r[...], vc_r[...], vm_r[...]], axis=0)
        valid = _swa_valid(i)
        outs, lses = [], []
        for e in (0, 1):
            sel = lo if e == 0 else jnp.logical_not(lo)
            s = _dot_nt(jnp.where(sel, q, 0), k3) + b_r[e]
            s = jnp.where(valid, s, NEG)
            sink = sink_r[2 * p + e]
            mx = jnp.maximum(jnp.max(s, axis=1, keepdims=True), sink)
            pe = jnp.exp(s - mx)
            den = jnp.sum(pe, axis=1, keepdims=True) + jnp.exp(sink - mx)
            outs.append(_dot(pe.astype(CDT), v3) / den)
            lses.append(mx + jnp.log(den))
        o_o[...] = jnp.where(lo, outs[0], outs[1]).astype(CDT)
        lse_o[...] = jnp.where(lo, lses[0], lses[1])

    qblk = pl.BlockSpec((BLK, 128), lambda p, i: (i, p))
    return pl.pallas_call(
        body, name=name, grid=(NPAIR, t // BLK),
        in_specs=[pl.BlockSpec(memory_space=pltpu.SMEM), qblk] + _swa_kv_specs() + _swa_kv_specs()
        + [pl.BlockSpec((2, BLK, 3 * BLK), lambda p, i: (p, 0, 0))],
        out_specs=[qblk, qblk],
        out_shape=[_sds((t, 512), CDT), _sds((t, 512), F32)],
        compiler_params=_params(("parallel", "parallel")),
    )(sinks, qs, kse, kse, kse, vse, vse, vse, bias)


def _swa_bwd(qs, kse, vse, bias, sinks, o, lse, do, name):
    t = qs.shape[0]

    def body(sink_r, q_r, kp_r, kc_r, km_r, vp_r, vc_r, vm_r, b_r, o_r, lse_r, do_r,
             dq_o, dk_o, dv_o, db_o, dsk_o):
        p = pl.program_id(0)
        i = pl.program_id(1)
        lo = _iota((1, 128), 1) < HD

        @pl.when((i == 0) & (p % 2 == 0))
        def _():
            dk_o[...] = jnp.zeros(dk_o.shape, F32)
            dv_o[...] = jnp.zeros(dv_o.shape, F32)

        @pl.when(i == 0)
        def _():
            db_o[...] = jnp.zeros(db_o.shape, F32)
            dsk_o[...] = jnp.zeros(dsk_o.shape, F32)

        q = q_r[...]
        do_ = do_r[...]
        dd = do_ * o_r[...].astype(F32)
        lse = lse_r[...]
        k3 = jnp.concatenate([kp_r[...], kc_r[...], km_r[...]], axis=0)
        v3 = jnp.concatenate([vp_r[...], vc_r[...], vm_r[...]], axis=0)
        valid = _swa_valid(i)
        dq = jnp.zeros((BLK, 128), F32)
        dk3 = jnp.zeros((3 * BLK, 128), F32)
        dv3 = jnp.zeros((3 * BLK, 128), F32)
        dsink = []
        for e in (0, 1):
            sel = lo if e == 0 else jnp.logical_not(lo)
            qe = jnp.where(sel, q, 0)
            doe = jnp.where(sel, do_, 0.0).astype(CDT)
            lse_e = lse[:, HD * e:HD * e + 1]
            s = _dot_nt(qe, k3) + b_r[e]
            s = jnp.where(valid, s, NEG)
            pr = jnp.exp(s - lse_e)
            delta = jnp.sum(jnp.where(sel, dd, 0.0), axis=1, keepdims=True)
            ds = pr * (_dot_nt(doe, v3) - delta)
            db_o[e] += ds
            dsink.append(-jnp.sum(jnp.exp(sink_r[2 * p + e] - lse_e) * delta, axis=0, keepdims=True))
            dq = dq + _dot(ds.astype(CDT), jnp.where(sel, k3, 0))
            dk3 = dk3 + _dot(ds.T.astype(CDT), qe)
            dv3 = dv3 + _dot(pr.T.astype(CDT), doe)
        dq_o[...] = dq
        prev = pl.ds(pl.multiple_of(jnp.maximum(i - 1, 0) * BLK, BLK), BLK)
        cur = pl.ds(pl.multiple_of(i * BLK, BLK), BLK)
        dk_o[prev, :] += dk3[0:BLK]
        dk_o[cur, :] += dk3[BLK:2 * BLK]
        dk_o[0:BLK, :] += dk3[2 * BLK:]
        dv_o[prev, :] += dv3[0:BLK]
        dv_o[cur, :] += dv3[BLK:2 * BLK]
        dv_o[0:BLK, :] += dv3[2 * BLK:]
        dsk_o[0:1, :] += jnp.where(lo, dsink[0], dsink[1])

    qblk = pl.BlockSpec((BLK, 128), lambda p, i: (i, p))
    kvacc = pl.BlockSpec((None, t, 128), lambda p, i: (p // 2, 0, 0))
    bblk = pl.BlockSpec((2, BLK, 3 * BLK), lambda p, i: (p, 0, 0))
    return pl.pallas_call(
        body, name=name, grid=(NPAIR, t // BLK),
        in_specs=[pl.BlockSpec(memory_space=pltpu.SMEM), qblk] + _swa_kv_specs() + _swa_kv_specs()
        + [bblk, qblk, qblk, qblk],
        out_specs=[qblk, kvacc, kvacc, bblk, pl.BlockSpec((None, 8, 128), lambda p, i: (p, 0, 0))],
        out_shape=[_sds((t, 512), F32), _sds((2, t, 128), F32), _sds((2, t, 128), F32),
                   _sds((8, BLK, 3 * BLK), F32), _sds((NPAIR, 8, 128), F32)],
        compiler_params=_params(("arbitrary", "arbitrary")),
    )(sinks, qs, kse, kse, kse, vse, vse, vse, bias, o, lse, do)


def _adamw(w, g, m, v, name):
    r, c = w.shape
    tr = 128 if r % 128 == 0 else r

    def body(w_r, g_r, m_r, v_r, d_o, m_o, v_o):
        g_ = g_r[...]
        m_ = ADAM_B1 * m_r[...] + (1.0 - ADAM_B1) * g_
        v_ = ADAM_B2 * v_r[...] + (1.0 - ADAM_B2) * jnp.square(g_)
        m_hat = m_ / (1.0 - ADAM_B1 ** ADAM_STEP)
        v_hat = v_ / (1.0 - ADAM_B2 ** ADAM_STEP)
        d_o[...] = -ADAM_LR * (m_hat / (jnp.sqrt(v_hat) + ADAM_EPS) + ADAM_WD * w_r[...])
        m_o[...] = m_
        v_o[...] = v_

    blk = pl.BlockSpec((tr, c), lambda i: (i, 0))
    return pl.pallas_call(
        body, name=name, grid=(r // tr,),
        in_specs=[blk] * 4, out_specs=[blk] * 3, out_shape=[_sds((r, c), F32)] * 3,
        compiler_params=_params(("parallel",)),
    )(w, g, m, v)


def _pair_add(own, got, half_idx, name):
    tr = PACK_TILE
    nb = HALF_ROWS // tr
    nl = own.shape[1]

    def body(c_r, a_r, b_r, o_o):
        o_o[...] = (a_r[...].astype(F32) + b_r[...].astype(F32)).astype(CDT)

    grid_spec = pltpu.PrefetchScalarGridSpec(
        num_scalar_prefetch=1, grid=(4, nl, nb),
        in_specs=[pl.BlockSpec((None, None, tr, 128), lambda s, l, i, c_r: (s, l, c_r[0] * nb + i, 0)),
                  pl.BlockSpec((None, None, tr, 128), lambda s, l, i, c_r: (s, l, i, 0))],
        out_specs=pl.BlockSpec((None, None, tr, 128), lambda s, l, i, c_r: (s, l, i, 0)),
    )
    return pl.pallas_call(
        body, name=name, grid_spec=grid_spec, out_shape=_sds((4, nl, HALF_ROWS, 128), CDT),
        compiler_params=_params(("parallel", "parallel", "parallel")),
    )(half_idx, own, got)


def _sum4(ps, got, shard_idx, name):
    tr = PACK_TILE
    nl = ps.shape[1]

    def body(s_r, a_r, b_r, o_o):
        o_o[...] = ((a_r[...].astype(F32) + b_r[0].astype(F32)) + b_r[1].astype(F32)) + b_r[2].astype(F32)

    grid_spec = pltpu.PrefetchScalarGridSpec(
        num_scalar_prefetch=1, grid=(nl, HALF_ROWS // tr),
        in_specs=[pl.BlockSpec((None, None, tr, 128), lambda l, i, s_r: (s_r[0], l, i, 0)),
                  pl.BlockSpec((3, None, tr, 128), lambda l, i, s_r: (0, l, i, 0))],
        out_specs=pl.BlockSpec((None, tr, 128), lambda l, i, s_r: (l, i, 0)),
    )
    return pl.pallas_call(
        body, name=name, grid_spec=grid_spec, out_shape=_sds((nl, HALF_ROWS, 128), F32),
        compiler_params=_params(("parallel", "parallel")),
    )(shard_idx, ps, got)


def _sum8(slots, name):
    def body(a_r, o_o):
        acc = a_r[0]
        for k in range(1, 8):
            acc = acc + a_r[k]
        o_o[...] = acc

    return pl.pallas_call(
        body, name=name, out_shape=_sds((SMALL_ROWS, 128), F32),
        in_specs=[pl.BlockSpec(memory_space=pltpu.VMEM)], out_specs=pl.BlockSpec(memory_space=pltpu.VMEM),
        compiler_params=_params(),
    )(slots)


def _place():
    x, y, c = lax.axis_index("x"), lax.axis_index("y"), lax.axis_index("c")
    chips = [(1 - x, y), (x, 1 - y), (1 - x, 1 - y)]
    return x, y, c, chips


def _remote(src, dst, send_sems, recv_sems, k, to):
    return pltpu.make_async_remote_copy(src_ref=src, dst_ref=dst, send_sem=send_sems.at[k], recv_sem=recv_sems.at[k],
                                        device_id=to, device_id_type=MESH_ID)


ANY = pl.BlockSpec(memory_space=pl.ANY)


def _gather_weights(wflat, mflat, name):
    nl = wflat.shape[0]

    def body(w_r, m_r, wall_o, mall_o, send_sems, recv_sems):
        x, y, c, chips = _place()
        s = 2 * x + y
        sib = (x, y, 1 - c)
        mine = pl.ds(pl.multiple_of(c * HALF_ROWS, 16), HALF_ROWS)
        other = pl.ds(pl.multiple_of((1 - c) * HALF_ROWS, 16), HALF_ROWS)
        sent = []
        for j, (cx, cy) in enumerate(chips):
            sent.append(_remote(w_r.at[:, mine], wall_o.at[s, :, mine], send_sems, recv_sems, j, (cx, cy, c)))
            sent.append(_remote(m_r, mall_o.at[s], send_sems, recv_sems, 6 + j, (cx, cy, c)))
        for cp in sent:
            cp.start()
        for j, (cx, cy) in enumerate(chips):
            sj = 2 * cx + cy
            _remote(w_r.at[:, mine], wall_o.at[sj, :, mine], send_sems, recv_sems, j, sib).wait_recv()
            fwd = _remote(wall_o.at[sj, :, mine], wall_o.at[sj, :, mine], send_sems, recv_sems, 3 + j, sib)
            fwd.start()
            sent.append(fwd)
        for j, (cx, cy) in enumerate(chips):
            sj = 2 * cx + cy
            _remote(w_r.at[:, other], wall_o.at[sj, :, other], send_sems, recv_sems, 3 + j, sib).wait_recv()
            _remote(m_r, mall_o.at[sj], send_sems, recv_sems, 6 + j, sib).wait_recv()
        for cp in sent:
            cp.wait_send()

    return pl.pallas_call(
        body, name=name,
        out_shape=[_sds((4, nl, LAYER_ROWS, 128), CDT), _sds((4, META_ROWS, 128), F32)],
        in_specs=[ANY, ANY], out_specs=[ANY, ANY],
        scratch_shapes=[pltpu.SemaphoreType.DMA((9,)), pltpu.SemaphoreType.DMA((9,))],
    )(wflat, mflat)


def _swap_halves(gfl, gsm, name):
    def body(g_r, s_r, got_o, slots_o, send_sems, recv_sems, loc_sem):
        x, y, c, _ = _place()
        me = 4 * x + 2 * y + c
        theirs = pl.ds(pl.multiple_of((1 - c) * HALF_ROWS, 16), HALF_ROWS)
        loc = pltpu.make_async_copy(s_r, slots_o.at[me], loc_sem.at[0])
        loc.start()
        sent = [_remote(g_r.at[:, :, theirs, :], got_o, send_sems, recv_sems, 0, (x, y, 1 - c))]
        for k in range(1, 8):
            px, py, pc = x ^ (k >> 2), y ^ ((k >> 1) & 1), c ^ (k & 1)
            sent.append(_remote(s_r, slots_o.at[me], send_sems, recv_sems, k, (px, py, pc)))
        for cp in sent:
            cp.start()
        _remote(g_r.at[:, :, theirs, :], got_o, send_sems, recv_sems, 0, (x, y, 1 - c)).wait_recv()
        for k in range(1, 8):
            px, py, pc = x ^ (k >> 2), y ^ ((k >> 1) & 1), c ^ (k & 1)
            _remote(s_r, slots_o.at[4 * px + 2 * py + pc], send_sems, recv_sems, k, (px, py, pc)).wait_recv()
        for cp in sent:
            cp.wait_send()
        loc.wait()

    return pl.pallas_call(
        body, name=name,
        out_shape=[_sds((4, gfl.shape[1], HALF_ROWS, 128), CDT), _sds((8, SMALL_ROWS, 128), F32)],
        in_specs=[ANY, ANY], out_specs=[ANY, ANY],
        scratch_shapes=[pltpu.SemaphoreType.DMA((8,)), pltpu.SemaphoreType.DMA((8,)), pltpu.SemaphoreType.DMA((1,))],
    )(gfl, gsm)


def _scatter_shards(ps, name):
    def body(p_r, got_o, send_sems, recv_sems):
        x, y, c, chips = _place()
        sent = [_remote(p_r.at[2 * cx + cy], got_o.at[j], send_sems, recv_sems, j, (cx, cy, c))
                for j, (cx, cy) in enumerate(chips)]
        for cp in sent:
            cp.start()
        for j in range(3):
            _remote(p_r.at[0], got_o.at[j], send_sems, recv_sems, j, (x, y, c)).wait_recv()
        for cp in sent:
            cp.wait_send()

    return pl.pallas_call(
        body, name=name, out_shape=_sds((3, ps.shape[1], HALF_ROWS, 128), CDT),
        in_specs=[ANY], out_specs=ANY,
        scratch_shapes=[pltpu.SemaphoreType.DMA((3,)), pltpu.SemaphoreType.DMA((3,))],
    )(ps)


def _join_halves(red_half, name):
    def body(h_r, got_o, send_sem, recv_sem):
        x, y, c, _ = _place()
        out = _remote(h_r, got_o, send_sem, recv_sem, 0, (x, y, 1 - c))
        out.start()
        out.wait()

    return pl.pallas_call(
        body, name=name, out_shape=_sds(red_half.shape, F32),
        in_specs=[ANY], out_specs=ANY,
        scratch_shapes=[pltpu.SemaphoreType.DMA((1,)), pltpu.SemaphoreType.DMA((1,))],
    )(red_half)


def _pack_shard(local):
    pad = [jnp.zeros((LAYER_ROWS - LAYER_RAW, 128), local[SHARD_ITEMS[0][0]].dtype)]
    return jnp.stack([jnp.concatenate([local[nm][l].reshape(-1, 128) for nm, _, _ in SHARD_ITEMS] + pad)
                      for l in range(2)])


def _unpack_full(wall):
    layers = []
    for l in range(2):
        ws = {}
        off = 0
        for nm, (r, c), kind in SHARD_ITEMS:
            n = r * c // 128
            piece = wall[:, l, off:off + n, :].reshape(4, r, c)
            off += n
            if kind == "row":
                ws[nm] = piece.reshape(4 * r, c)
            else:
                ws[nm] = jnp.concatenate([piece[s] for s in range(4)], axis=1)
        layers.append(ws)
    return layers


def _mix_cols(w):
    return jnp.concatenate([w[:, 2312:4360], w[:, 0:1536], w[:, 1544:2312], w[:, 1536:1544],
                            jnp.zeros((w.shape[0], DP - D_IN), w.dtype)], axis=1)


def _unmix_cols(w):
    return jnp.concatenate([w[:, QA:QA + 1536], w[:, FA:FA + 8], w[:, QB:QB + 768], w[:, GA:GA + 2048]], axis=1)


def _pack_grads(grads):
    pad = [jnp.zeros((LAYER_ROWS - LAYER_RAW, 128), CDT)]
    shards = []
    for s in range(4):
        layers = []
        for l in range(2):
            parts = []
            for nm, (r, c), kind in SHARD_ITEMS:
                g = grads[l][nm]
                parts.append((g[s * r:(s + 1) * r] if kind == "row" else g[:, s * c:(s + 1) * c]).reshape(-1, 128))
            layers.append(jnp.concatenate(parts + pad))
        shards.append(jnp.stack(layers))
    return jnp.stack(shards)


def _unpack_shard(red):
    out = {}
    off = 0
    for nm, (r, c), _ in SHARD_ITEMS:
        n = r * c // 128
        out[nm] = red[:, off:off + n, :].reshape(2, r, c)
        off += n
    return out


def _rows128(a, rows):
    flat = a.reshape(-1)
    return jnp.pad(flat, (0, rows * 128 - flat.shape[0])).reshape(rows, 128)


SMALL_ITEMS = (("rel_bias_table", 2), ("ffn1_norm", 16), ("mix_norm", 16), ("ffn2_norm", 16), ("forget_bias", 1),
               ("fox_q_norm", 1), ("fox_k_norm", 1), ("swa_q_norm", 1), ("swa_k_norm", 1), ("swa_sinks", 1))
SMALL_ADAM_ROWS = 96


def _layer_fwd(h, lw, l):
    sv = {"h0": h}
    a, sv["a1t"] = _rms_fwd(h, lw["ffn1_norm"], f"rms_fwd_a{l}")
    sv["gu1"], s, sv["s1t"] = _ffn_in(a, lw["ffn1_w_in"], f"ffn_in_a{l}")
    h = _mm_res(s, lw["ffn1_w_out"], h, 0.5, f"ffn_out_a{l}")
    sv["h1"] = h
    a, sv["amt"] = _rms_fwd(h, lw["mix_norm"], f"rms_fwd_m{l}")
    proj = _mm(a, lw["w_mix"], F32, _row_tile(h.shape[0]), 640, f"proj{l}", b_resident=True)
    sv["proj"] = proj
    qf, kf, vf, qs, kse, vse, c, ct = _qknorm_fwd(proj, lw["gfq"], lw["gfk"], lw["gsq"], lw["gsk"], lw["fb"],
                                                   f"qknorm_fwd{l}")
    ofox, lse_f = _fox_fwd(qf, kf, vf, c, ct, f"fox_fwd{l}")
    oswa, lse_s = _swa_fwd(qs, kse, vse, lw["bias"], lw["sinks"], f"swa_fwd{l}")
    sv.update(qf=qf, kf=kf, vf=vf, qs=qs, kse=kse, vse=vse, c=c, ct=ct, ofox=ofox, oswa=oswa, lse_f=lse_f, lse_s=lse_s)
    y, sv["yt"], sv["pf"], sv["ps"], sv["oft"], sv["ost"] = _gate_fwd(ofox, oswa, lw["w_branch_fox"], lw["w_branch_swa"],
                                                                     proj, f"gate_fwd{l}")
    h = _mm_res(y, lw["w_out"], h, 1.0, f"mix_out{l}")
    sv["h2"] = h
    a, sv["a2t"] = _rms_fwd(h, lw["ffn2_norm"], f"rms_fwd_b{l}")
    sv["gu2"], s, sv["s2t"] = _ffn_in(a, lw["ffn2_w_in"], f"ffn_in_b{l}")
    h = _mm_res(s, lw["ffn2_w_out"], h, 0.5, f"ffn_out_b{l}")
    return h, sv


def _ffn_bwd(dh, dhb, h_in, at, gu, st, norm, w_in, w_out, tag):
    dgu = _ffn_bwd_mid(dhb, w_out, gu, f"ffn_bwd_mid_{tag}")
    d_w_out = _mm(st, dhb, CDT, 256, 512, f"dw_ffn_out_{tag}", scale=0.5)
    da = _ffn_bwd_in(dgu, w_in, f"ffn_bwd_in_{tag}")
    d_w_in = _mm(at, dgu, CDT, 512, 256, f"dw_ffn_in_{tag}")
    dh, dhb, dg = _rms_bwd(da, h_in, norm, dh, f"rms_bwd_{tag}")
    return dh, dhb, d_w_out, d_w_in, dg


def _layer_bwd(dh, dhb, sv, lw, l):
    g = {}
    dh, dhb, g["ffn2_w_out"], g["ffn2_w_in"], g["ffn2_norm"] = _ffn_bwd(
        dh, dhb, sv["h2"], sv["a2t"], sv["gu2"], sv["s2t"], lw["ffn2_norm"], lw["ffn2_w_in"], lw["ffn2_w_out"], f"b{l}")
    dy = _mm_nt(dhb, lw["w_out"], f"d_y{l}")
    g["w_out"] = _mm(sv["yt"], dhb, CDT, 512, 512, f"dw_out{l}")
    dpf, dps, dga, dgb = _gate_bwd(dy, sv["pf"], sv["ps"], sv["proj"], f"gate_bwd{l}")
    do_f = _mm_nt(dpf, lw["w_branch_fox"], f"d_ofox{l}")
    do_s = _mm_nt(dps, lw["w_branch_swa"], f"d_oswa{l}")
    g["w_branch_fox"] = _mm(sv["oft"], dpf, CDT, 512, 512, f"dw_bfox{l}")
    g["w_branch_swa"] = _mm(sv["ost"], dps, CDT, 512, 512, f"dw_bswa{l}")
    dqf, dcq, dkf, dvf, dck = _fox_bwd(sv["qf"], sv["kf"], sv["vf"], sv["c"], sv["ct"], sv["ofox"], sv["lse_f"], do_f,
                                       f"fox_bwd{l}")
    dqs, dkse, dvse, dbias, dsk = _swa_bwd(sv["qs"], sv["kse"], sv["vse"], lw["bias"], lw["sinks"], sv["oswa"],
                                           sv["lse_s"], do_s, f"swa_bwd{l}")
    dproj, dgn = _qknorm_bwd(sv["proj"], dqf, dkf, dvf, dqs, dkse, dvse, dcq, dck, dga, dgb,
                             lw["gfq"], lw["gfk"], lw["gsq"], lw["gsk"], lw["fb"], f"qknorm_bwd{l}")
    dam = _mm_nt(dproj, lw["w_mix"], f"d_am{l}")
    g["w_mix"] = _mm(sv["amt"], dproj, CDT, 512, 640, f"dw_mix{l}")
    dh, dhb, g["mix_norm"] = _rms_bwd(dam, sv["h1"], lw["mix_norm"], dh, f"rms_bwd_m{l}")
    g["dbias"], g["dsk"], g["dgn"] = dbias, dsk, dgn
    dh, dhb, g["ffn1_w_out"], g["ffn1_w_in"], g["ffn1_norm"] = _ffn_bwd(
        dh, dhb, sv["h0"], sv["a1t"], sv["gu1"], sv["s1t"], lw["ffn1_norm"], lw["ffn1_w_in"], lw["ffn1_w_out"], f"a{l}")
    return dh, dhb, g


def kernel(x, meta_tokens, rel_bias_table, ffn1_norm, ffn1_w_in, ffn1_w_out, mix_norm, w_in, forget_bias, fox_q_norm, fox_k_norm, swa_q_norm, swa_k_norm, swa_sinks, w_branch_fox, w_branch_swa, w_out, ffn2_norm, ffn2_w_in, ffn2_w_out, loss_target, m_meta_tokens, m_rel_bias_table, m_ffn1_norm, m_ffn1_w_in, m_ffn1_w_out, m_mix_norm, m_w_in, m_forget_bias, m_fox_q_norm, m_fox_k_norm, m_swa_q_norm, m_swa_k_norm, m_swa_sinks, m_w_branch_fox, m_w_branch_swa, m_w_out, m_ffn2_norm, m_ffn2_w_in, m_ffn2_w_out, v_meta_tokens, v_rel_bias_table, v_ffn1_norm, v_ffn1_w_in, v_ffn1_w_out, v_mix_norm, v_w_in, v_forget_bias, v_fox_q_norm, v_fox_k_norm, v_swa_q_norm, v_swa_k_norm, v_swa_sinks, v_w_branch_fox, v_w_branch_swa, v_w_out, v_ffn2_norm, v_ffn2_w_in, v_ffn2_w_out):
    names = ["meta_tokens", "rel_bias_table", "ffn1_norm", "ffn1_w_in", "ffn1_w_out", "mix_norm", "w_in", "forget_bias",
             "fox_q_norm", "fox_k_norm", "swa_q_norm", "swa_k_norm", "swa_sinks", "w_branch_fox", "w_branch_swa", "w_out",
             "ffn2_norm", "ffn2_w_in", "ffn2_w_out"]
    w = dict(zip(names, [meta_tokens, rel_bias_table, ffn1_norm, ffn1_w_in, ffn1_w_out, mix_norm, w_in, forget_bias,
                         fox_q_norm, fox_k_norm, swa_q_norm, swa_k_norm, swa_sinks, w_branch_fox, w_branch_swa, w_out,
                         ffn2_norm, ffn2_w_in, ffn2_w_out]))
    m = dict(zip(names, [m_meta_tokens, m_rel_bias_table, m_ffn1_norm, m_ffn1_w_in, m_ffn1_w_out, m_mix_norm, m_w_in,
                         m_forget_bias, m_fox_q_norm, m_fox_k_norm, m_swa_q_norm, m_swa_k_norm, m_swa_sinks,
                         m_w_branch_fox, m_w_branch_swa, m_w_out, m_ffn2_norm, m_ffn2_w_in, m_ffn2_w_out]))
    v = dict(zip(names, [v_meta_tokens, v_rel_bias_table, v_ffn1_norm, v_ffn1_w_in, v_ffn1_w_out, v_mix_norm, v_w_in,
                         v_forget_bias, v_fox_q_norm, v_fox_k_norm, v_swa_q_norm, v_swa_k_norm, v_swa_sinks,
                         v_w_branch_fox, v_w_branch_swa, v_w_out, v_ffn2_norm, v_ffn2_w_in, v_ffn2_w_out]))
    xi, yi, ci = lax.axis_index("x"), lax.axis_index("y"), lax.axis_index("c")
    shard = 2 * xi + yi
    seq = x.shape[1]
    t = seq + BLK

    wflat = _pack_shard({nm: w[nm].astype(CDT) for nm, _, _ in SHARD_ITEMS})
    mflat = meta_tokens.reshape(META_ROWS, 128)
    wall, mall = _gather_weights(wflat, mflat, "gather_weights")
    wall = lax.dynamic_update_slice(wall, wflat[None], (shard, 0, 0, 0))
    mall = lax.dynamic_update_slice(mall, mflat[None], (shard, 0, 0))
    full = _unpack_full(wall)
    meta_full = jnp.concatenate([mall[s].reshape(N_META, 256) for s in range(4)], axis=1)
    bias = _bias_fwd(rel_bias_table, "bias_fwd")
    lws = []
    for l in range(2):
        lw = dict(full[l])
        lw["w_mix"] = _mix_cols(lw.pop("w_in"))
        for nm in ("ffn1_norm", "mix_norm", "ffn2_norm"):
            lw[nm] = w[nm][l].reshape(1, D)
        lw["gfq"] = jnp.tile(fox_q_norm[l], 8).reshape(1, 512)
        lw["gfk"] = jnp.tile(fox_k_norm[l], 8).reshape(1, 512)
        lw["gsq"] = jnp.tile(swa_q_norm[l], 8).reshape(1, 512)
        lw["gsk"] = jnp.tile(swa_k_norm[l], 2).reshape(1, 128)
        lw["fb"] = jnp.pad(forget_bias[l], (0, 120)).reshape(1, 128)
        lw["sinks"] = swa_sinks[l]
        lw["bias"] = bias
        lws.append(lw)

    h = jnp.concatenate([jnp.zeros((PAD, D), F32), meta_full, x[0]], axis=0)
    saved = []
    for l in range(2):
        h, sv = _layer_fwd(h, lws[l], l)
        saved.append(sv)
    dh, dhb, lacc = _loss(h, loss_target[0], "loss")
    loss = lax.psum(lacc[0, 0], ("x", "y", "c"))

    grads = [None, None]
    for l in (1, 0):
        dh, dhb, grads[l] = _layer_bwd(dh, dhb, saved[l], lws[l], l)
    grad_x = dh[BLK:].reshape(1, seq, D)
    for l in range(2):
        grads[l]["w_in"] = _unmix_cols(grads[l].pop("w_mix"))
    dtab = _bias_bwd(grads[0]["dbias"] + grads[1]["dbias"], "bias_bwd")

    small = [dh[PAD:BLK].reshape(128, 128), _rows128(dtab[:, :N_BUCKETS].T, 2)]
    for nm in ("ffn1_norm", "mix_norm", "ffn2_norm"):
        small.append(jnp.stack([grads[l][nm][0] for l in range(2)]).reshape(16, 128))
    small.append(_rows128(jnp.stack([grads[l]["dgn"][4, :8] for l in range(2)]), 1))
    for row in range(4):
        small.append(jnp.stack([grads[l]["dgn"][row, :HD] for l in range(2)]).reshape(1, 128))
    dsk = [grads[l]["dsk"][:, 0, :] for l in range(2)]
    small.append(_rows128(jnp.stack([jnp.stack([d[:, 0], d[:, HD]], axis=1).reshape(8) for d in dsk]), 1))
    gsm = jnp.concatenate(small, axis=0)
    gsm = jnp.pad(gsm, ((0, SMALL_ROWS - gsm.shape[0]), (0, 0)))

    gfl = _pack_grads(grads)
    got, slots = _swap_halves(gfl, gsm, "swap_halves")
    ps = _pair_add(gfl, got, ci.reshape(1).astype(jnp.int32), "pair_add")
    got3 = _scatter_shards(ps, "scatter_shards")
    red_half = _sum4(ps, got3, shard.reshape(1).astype(jnp.int32), "sum4")
    sib_half = _join_halves(red_half, "join_halves")
    south = ci == 0
    red = jnp.concatenate([jnp.where(south, red_half, sib_half), jnp.where(south, sib_half, red_half)], axis=1)
    gs = _sum8(slots, "sum8")
    big = _unpack_shard(red)

    g_out = dict(big)
    g_out["meta_tokens"] = lax.dynamic_slice(gs[0:128].reshape(N_META, D), (0, shard * 256), (N_META, 256))
    off = 128
    for nm, rows in SMALL_ITEMS:
        n = w[nm].size
        g_out[nm] = gs[off:off + rows].reshape(-1)[:n].reshape(w[nm].shape)
        off += rows

    delta, new_m, new_v = {}, {}, {}
    for nm, (r, c), _ in SHARD_ITEMS:
        two = lambda a: a.reshape(2 * r, c)
        d_, m_, v_ = _adamw(two(w[nm]), two(g_out[nm]), two(m[nm]), two(v[nm]), f"adamw_{nm}")
        delta[nm], new_m[nm], new_v[nm] = (a.reshape(2, r, c) for a in (d_, m_, v_))
    small_names = ["meta_tokens"] + [nm for nm, _ in SMALL_ITEMS]
    small_rows = [META_ROWS] + [rows for _, rows in SMALL_ITEMS]

    def pack_small(src):
        buf = jnp.concatenate([_rows128(src[nm], rows) for nm, rows in zip(small_names, small_rows)], axis=0)
        return jnp.pad(buf, ((0, SMALL_ADAM_ROWS - buf.shape[0]), (0, 0)))

    d_, m_, v_ = _adamw(pack_small(w), pack_small(g_out), pack_small(m), pack_small(v), "adamw_small")
    off = 0
    for nm, rows in zip(small_names, small_rows):
        n = w[nm].size
        for dst, src in ((delta, d_), (new_m, m_), (new_v, v_)):
            dst[nm] = src[off:off + rows].reshape(-1)[:n].reshape(w[nm].shape)
        off += rows

    return (loss, grad_x, *[g_out[n] for n in names], *[delta[n] for n in names],
            *[new_m[n] for n in names], *[new_v[n] for n in names])
```

```python
import math

import numpy as np
import jax
import jax.numpy as jnp
from jax import lax
from jax.experimental import pallas as pl
from jax.experimental.pallas import tpu as pltpu

D = 1024
F = 2816
FT = F // 2
HD = 64
NPAIR = 4
N_META = 16
BLK = 128
PAD = BLK - N_META
EPS = 1e-6
NEG = -1e30
N_BUCKETS = 32
GA, GB, QA, KA, VA, QB, KB, VB, FA, DP = 0, 1024, 2048, 2560, 3072, 3584, 4096, 4224, 4352, 4480
D_IN = 4360
CDT = jnp.bfloat16
F32 = jnp.float32
VMEM_LIMIT = 48 * 1024 * 1024
MESH_ID = pl.DeviceIdType.MESH

ADAM_LR, ADAM_B1, ADAM_B2, ADAM_EPS, ADAM_WD, ADAM_STEP = 0.001, 0.9, 0.999, 1e-08, 0.01, 10

SHARD_ITEMS = (
    ("ffn1_w_in", (1024, 1408), "col"),
    ("ffn1_w_out", (704, 1024), "row"),
    ("w_in", (1024, 1090), "col"),
    ("w_branch_fox", (512, 256), "col"),
    ("w_branch_swa", (512, 256), "col"),
    ("w_out", (256, 1024), "row"),
    ("ffn2_w_in", (1024, 1408), "col"),
    ("ffn2_w_out", (704, 1024), "row"),
)
LAYER_RAW = sum(r * c for _, (r, c), _ in SHARD_ITEMS) // 128
PACK_TILE = 5840
HALF_ROWS = 4 * PACK_TILE
LAYER_ROWS = 2 * HALF_ROWS
SMALL_ROWS = 192
META_ROWS = 32


def _row_tile(t):
    return 384 if t % 384 == 0 else 128


def _dot(a, b):
    return jnp.dot(a, b, preferred_element_type=F32)


def _dot_nt(a, b):
    return lax.dot_general(a, b, (((1,), (1,)), ((), ())), preferred_element_type=F32)


def _dot_hi(a, b):
    return jnp.dot(a, b, preferred_element_type=F32, precision=lax.Precision.HIGHEST)


def _sigmoid(x):
    return 1.0 / (1.0 + jnp.exp(-x))


def _iota(shape, dim):
    return lax.broadcasted_iota(jnp.int32, shape, dim)


def _params(sem=None):
    return pltpu.CompilerParams(dimension_semantics=sem, vmem_limit_bytes=VMEM_LIMIT)


def _sds(shape, dtype):
    return jax.ShapeDtypeStruct(shape, dtype)


def _rms_fwd(h, g, name):
    t = h.shape[0]
    tm = _row_tile(t)

    def body(h_ref, g_ref, a_ref, at_ref):
        x = h_ref[...]
        ms = jnp.mean(x * x, axis=-1, keepdims=True)
        a = x * lax.rsqrt(ms + EPS) * g_ref[...]
        a_ref[...] = a.astype(CDT)
        at_ref[...] = a.T.astype(CDT)

    return pl.pallas_call(
        body, name=name, grid=(t // tm,),
        in_specs=[pl.BlockSpec((tm, D), lambda i: (i, 0)), pl.BlockSpec((1, D), lambda i: (0, 0))],
        out_specs=[pl.BlockSpec((tm, D), lambda i: (i, 0)), pl.BlockSpec((D, tm), lambda i: (0, i))],
        out_shape=[_sds((t, D), CDT), _sds((D, t), CDT)],
        compiler_params=_params(("parallel",)),
    )(h, g)


def _rms_bwd(da, h, g, dres, name):
    t = h.shape[0]
    tm = _row_tile(t)

    def body(da_ref, h_ref, g_ref, dr_ref, dh_ref, dhb_ref, dg_ref):
        i = pl.program_id(0)
        x = h_ref[...]
        da_ = da_ref[...]
        r = lax.rsqrt(jnp.mean(x * x, axis=-1, keepdims=True) + EPS)
        xh = x * r
        day = da_ * g_ref[...]
        dx = r * (day - xh * jnp.mean(day * xh, axis=-1, keepdims=True))
        dh = dr_ref[...] + dx
        dh_ref[...] = dh
        dhb_ref[...] = dh.astype(CDT)

        @pl.when(i == 0)
        def _():
            dg_ref[...] = jnp.zeros(dg_ref.shape, F32)

        dg_ref[0:1, :] += jnp.sum(da_ * xh, axis=0, keepdims=True)

    row = pl.BlockSpec((tm, D), lambda i: (i, 0))
    return pl.pallas_call(
        body, name=name, grid=(t // tm,),
        in_specs=[row, row, pl.BlockSpec((1, D), lambda i: (0, 0)), row],
        out_specs=[row, row, pl.BlockSpec((8, D), lambda i: (0, 0))],
        out_shape=[_sds((t, D), F32), _sds((t, D), CDT), _sds((8, D), F32)],
        compiler_params=_params(("arbitrary",)),
    )(da, h, g, dres)


def _ffn_in(a, w_in, name):
    t = a.shape[0]
    tm = _row_tile(t)
    tn = FT
    nj = F // tn

    def body(a_ref, wg_ref, wu_ref, gu_ref, s_ref, st_ref):
        a_ = a_ref[...]
        g = _dot(a_, wg_ref[...])
        u = _dot(a_, wu_ref[...])
        s = g * _sigmoid(g) * u
        gu_ref[0] = g.astype(CDT)
        gu_ref[1] = u.astype(CDT)
        s_ref[...] = s.astype(CDT)
        st_ref[...] = s.T.astype(CDT)

    return pl.pallas_call(
        body, name=name, grid=(nj, t // tm),
        in_specs=[pl.BlockSpec((tm, D), lambda j, i: (i, 0)),
                  pl.BlockSpec((D, tn), lambda j, i: (0, j)),
                  pl.BlockSpec((D, tn), lambda j, i: (0, j + nj))],
        out_specs=[pl.BlockSpec((2, tm, tn), lambda j, i: (0, i, j)),
                   pl.BlockSpec((tm, tn), lambda j, i: (i, j)),
                   pl.BlockSpec((tn, tm), lambda j, i: (j, i))],
        out_shape=[_sds((2, t, F), CDT), _sds((t, F), CDT), _sds((F, t), CDT)],
        compiler_params=_params(("parallel", "parallel")),
    )(a, w_in, w_in)


def _mm_res(a, b, res, scale, name):
    t, k = a.shape
    n = b.shape[1]
    tm = _row_tile(t)
    tn = 512

    def body(a_ref, b_ref, r_ref, o_ref):
        o_ref[...] = r_ref[...] + scale * _dot(a_ref[...], b_ref[...])

    return pl.pallas_call(
        body, name=name, grid=(t // tm, n // tn),
        in_specs=[pl.BlockSpec((tm, k), lambda i, j: (i, 0)),
                  pl.BlockSpec((k, tn), lambda i, j: (0, j)),
                  pl.BlockSpec((tm, tn), lambda i, j: (i, j))],
        out_specs=pl.BlockSpec((tm, tn), lambda i, j: (i, j)),
        out_shape=_sds((t, n), F32),
        compiler_params=_params(("parallel", "parallel")),
    )(a, b, res)


def _mm(a, b, out_dtype, tm, tn, name, scale=1.0, b_resident=False):
    m, k = a.shape
    order = (lambda j, i: (i, j)) if b_resident else (lambda i, j: (i, j))
    if b.ndim == 3:
        nh = b.shape[2] // tn
        n = 2 * b.shape[2]
        b_spec = pl.BlockSpec((None, k, tn), lambda *g: (order(*g)[1] // nh, 0, order(*g)[1] % nh))
    else:
        n = b.shape[1]
        b_spec = pl.BlockSpec((k, tn), lambda *g: (0, order(*g)[1]))

    def body(a_ref, b_ref, o_ref):
        o_ref[...] = (scale * _dot(a_ref[...], b_ref[...])).astype(out_dtype)

    return pl.pallas_call(
        body, name=name, grid=(n // tn, m // tm) if b_resident else (m // tm, n // tn),
        in_specs=[pl.BlockSpec((tm, k), lambda *g: (order(*g)[0], 0)), b_spec],
        out_specs=pl.BlockSpec((tm, tn), lambda *g: order(*g)),
        out_shape=_sds((m, n), out_dtype),
        compiler_params=_params(("parallel", "parallel")),
    )(a, b)


def _mm_nt(a, b, name):
    m, n = a.shape
    k = b.shape[0]
    tm = _row_tile(m)
    tk = 512

    def body(a_ref, b_ref, o_ref):
        o_ref[...] = _dot_nt(a_ref[...], b_ref[...])

    return pl.pallas_call(
        body, name=name, grid=(m // tm, k // tk),
        in_specs=[pl.BlockSpec((tm, n), lambda i, j: (i, 0)), pl.BlockSpec((tk, n), lambda i, j: (j, 0))],
        out_specs=pl.BlockSpec((tm, tk), lambda i, j: (i, j)),
        out_shape=_sds((m, k), F32),
        compiler_params=_params(("parallel", "parallel")),
    )(a, b)


def _ffn_bwd_mid(dhb, w_out, gu, name):
    t = dhb.shape[0]
    tm = _row_tile(t)
    tn = FT

    def body(dh_ref, w_ref, gu_ref, o_ref):
        ds = 0.5 * _dot_nt(dh_ref[...], w_ref[...])
        g = gu_ref[0].astype(F32)
        u = gu_ref[1].astype(F32)
        sg = _sigmoid(g)
        o_ref[0] = (ds * u * (sg * (1.0 + g * (1.0 - sg)))).astype(CDT)
        o_ref[1] = (ds * (g * sg)).astype(CDT)

    return pl.pallas_call(
        body, name=name, grid=(F // tn, t // tm),
        in_specs=[pl.BlockSpec((tm, D), lambda j, i: (i, 0)),
                  pl.BlockSpec((tn, D), lambda j, i: (j, 0)),
                  pl.BlockSpec((2, tm, tn), lambda j, i: (0, i, j))],
        out_specs=pl.BlockSpec((2, tm, tn), lambda j, i: (0, i, j)),
        out_shape=_sds((2, t, F), CDT),
        compiler_params=_params(("parallel", "parallel")),
    )(dhb, w_out, gu)


def _ffn_bwd_in(dgu, w_in, name):
    t = dgu.shape[1]
    tm = _row_tile(t)
    tk = 512

    def body(dg_ref, wg_ref, wu_ref, o_ref):
        o_ref[...] = _dot_nt(dg_ref[0], wg_ref[...]) + _dot_nt(dg_ref[1], wu_ref[...])

    return pl.pallas_call(
        body, name=name, grid=(t // tm, D // tk),
        in_specs=[pl.BlockSpec((2, tm, F), lambda i, j: (0, i, 0)),
                  pl.BlockSpec((tk, F), lambda i, j: (j, 0)),
                  pl.BlockSpec((tk, F), lambda i, j: (j, 1))],
        out_specs=pl.BlockSpec((tm, tk), lambda i, j: (i, j)),
        out_shape=_sds((t, D), F32),
        compiler_params=_params(("parallel", "parallel")),
    )(dgu, w_in, w_in)


def _loss(h, target, name):
    t = h.shape[0]

    def body(h_ref, t_ref, dh_ref, dhb_ref, l_ref):
        i = pl.program_id(0)

        @pl.when(i == 0)
        def _():
            l_ref[...] = jnp.zeros(l_ref.shape, F32)
            dh_ref[...] = jnp.zeros(dh_ref.shape, F32)
            dhb_ref[...] = jnp.zeros(dhb_ref.shape, CDT)

        @pl.when(i > 0)
        def _():
            err = h_ref[...] - t_ref[...]
            l_ref[...] += (0.5 / D) * jnp.sum(err * err)
            d = err * (1.0 / D)
            dh_ref[...] = d
            dhb_ref[...] = d.astype(CDT)

    row = pl.BlockSpec((BLK, D), lambda i: (i, 0))
    return pl.pallas_call(
        body, name=name, grid=(t // BLK,),
        in_specs=[row, pl.BlockSpec((BLK, D), lambda i: (jnp.maximum(i - 1, 0), 0))],
        out_specs=[row, row, pl.BlockSpec((8, 128), lambda i: (0, 0))],
        out_shape=[_sds((t, D), F32), _sds((t, D), CDT), _sds((8, 128), F32)],
        compiler_params=_params(("arbitrary",)),
    )(h, target)


def _block_diag():
    return (_iota((128, 128), 0) // HD == _iota((128, 128), 1) // HD).astype(F32)


def _dup_halves(x, lo):
    sw = pltpu.roll(x, 64, 1)
    return jnp.where(lo, x, sw), jnp.where(lo, sw, x)


def _qknorm_fwd(proj, gfq, gfk, gsq, gsk, fb, name):
    t = proj.shape[0]
    tm = _row_tile(t)

    def body(qa, ka, va, qb, kb, vb, fa, gfq_r, gfk_r, gsq_r, gsk_r, fb_r,
             qf_o, kf_o, vf_o, qs_o, kse_o, vse_o, c_o, ct_o, carry):
        i = pl.program_id(0)
        bd = _block_diag()
        lane = _iota((1, 128), 1)
        lo = lane < HD

        def hnorm(x, g):
            ms = _dot_hi(x * x, bd) * (1.0 / HD)
            return x * lax.rsqrt(ms + EPS) * g

        for ch in range(4):
            sl = slice(128 * ch, 128 * (ch + 1))
            qf_o[:, sl] = (hnorm(qa[:, sl], gfq_r[:, sl]) * 0.125).astype(CDT)
            kf_o[:, sl] = hnorm(ka[:, sl], gfk_r[:, sl]).astype(CDT)
            qs_o[:, sl] = (hnorm(qb[:, sl], gsq_r[:, sl]) * 0.125).astype(CDT)
        vf_o[...] = va[...].astype(CDT)
        k0, k1 = _dup_halves(hnorm(kb[...], gsk_r[...]), lo)
        kse_o[0] = k0.astype(CDT)
        kse_o[1] = k1.astype(CDT)
        v0, v1 = _dup_halves(vb[...], lo)
        vse_o[0] = v0.astype(CDT)
        vse_o[1] = v1.astype(CDT)

        z = fa[...] + fb_r[...]
        lf = jnp.minimum(z, 0.0) - jnp.log(1.0 + jnp.exp(-jnp.abs(z)))
        lf = jnp.where(lane < 8, lf, 0.0)
        ltri = (_iota((tm, tm), 1) <= _iota((tm, tm), 0)).astype(F32)

        @pl.when(i == 0)
        def _():
            carry[...] = jnp.zeros(carry.shape, F32)

        c = _dot_hi(ltri, lf) + carry[0:1, :]
        carry[0:1, :] = c[tm - 1:tm, :]
        c_o[...] = c
        ct_o[...] = c.T[0:8, :]

    def col(width, off):
        return pl.BlockSpec((tm, width), lambda i: (i, off // width))

    def vec(width):
        return pl.BlockSpec((1, width), lambda i: (0, 0))

    return pl.pallas_call(
        body, name=name, grid=(t // tm,),
        in_specs=[col(512, QA), col(512, KA), col(512, VA), col(512, QB), col(128, KB), col(128, VB), col(128, FA),
                  vec(512), vec(512), vec(512), vec(128), vec(128)],
        out_specs=[pl.BlockSpec((tm, 512), lambda i: (i, 0))] * 4
        + [pl.BlockSpec((2, tm, 128), lambda i: (0, i, 0))] * 2
        + [pl.BlockSpec((tm, 128), lambda i: (i, 0)), pl.BlockSpec((8, tm), lambda i: (0, i))],
        out_shape=[_sds((t, 512), CDT)] * 4 + [_sds((2, t, 128), CDT)] * 2 + [_sds((t, 128), F32), _sds((8, t), F32)],
        scratch_shapes=[pltpu.VMEM((8, 128), F32)],
        compiler_params=_params(("arbitrary",)),
    )(proj, proj, proj, proj, proj, proj, proj, gfq, gfk, gsq, gsk, fb)


def _qknorm_bwd(proj, dqf, dkf, dvf, dqs, dkse, dvse, dcq, dck, dga, dgb, gfq, gfk, gsq, gsk, fb, name):
    t = proj.shape[0]
    tm = _row_tile(t)
    nt = t // tm

    def body(qa, ka, qb, kb, fa, dqf_r, dkf_r, dvf_r, dqs_r, dkse_r, dvse_r, dcq_r, dck_r, dga_r, dgb_r,
             gfq_r, gfk_r, gsq_r, gsk_r, fb_r, dp_o, dgn_o, carry, acc):
        i = pl.program_id(0)
        bd = _block_diag()
        lane = _iota((1, 128), 1)
        lo = lane < HD

        @pl.when(i == 0)
        def _():
            carry[...] = jnp.zeros(carry.shape, F32)
            acc[...] = jnp.zeros(acc.shape, F32)

        def hnorm_bwd(x, g, dy):
            r = lax.rsqrt(_dot_hi(x * x, bd) * (1.0 / HD) + EPS)
            xh = x * r
            day = dy * g
            dx = r * (day - xh * (_dot_hi(day * xh, bd) * (1.0 / HD)))
            return dx, jnp.sum(dy * xh, axis=0, keepdims=True)

        for ch in range(4):
            sl = slice(128 * ch, 128 * (ch + 1))
            dx, dg = hnorm_bwd(qa[:, sl], gfq_r[:, sl], dqf_r[:, sl] * 0.125)
            dp_o[:, QA + 128 * ch:QA + 128 * (ch + 1)] = dx.astype(CDT)
            acc[0:1, sl] += dg
            dx, dg = hnorm_bwd(ka[:, sl], gfk_r[:, sl], dkf_r[:, sl])
            dp_o[:, KA + 128 * ch:KA + 128 * (ch + 1)] = dx.astype(CDT)
            acc[1:2, sl] += dg
            dx, dg = hnorm_bwd(qb[:, sl], gsq_r[:, sl], dqs_r[:, sl] * 0.125)
            dp_o[:, QB + 128 * ch:QB + 128 * (ch + 1)] = dx.astype(CDT)
            acc[2:3, sl] += dg
        dp_o[:, VA:VA + 512] = dvf_r[...].astype(CDT)
        dp_o[:, GA:GA + D] = dga_r[...]
        dp_o[:, GB:GB + D] = dgb_r[...]

        def fold(x):
            e0 = x[0]
            e1 = x[1]
            return jnp.where(lo, e0 + pltpu.roll(e0, 64, 1), e1 + pltpu.roll(e1, 64, 1))

        dx, dg = hnorm_bwd(kb[...], gsk_r[...], fold(dkse_r))
        dp_o[:, KB:KB + 128] = dx.astype(CDT)
        acc[3:4, 0:128] += dg
        dp_o[:, VB:VB + 128] = fold(dvse_r).astype(CDT)

        rr = _iota((512, 128), 0)
        hh = _iota((512, 128), 1)
        sel = ((rr == (hh >> 1) * 128 + (hh & 1) * HD) & (hh < 8)).astype(F32)
        dcs = _dot_hi(dcq_r[...] - dck_r[...], sel)
        utri = (_iota((tm, tm), 1) >= _iota((tm, tm), 0)).astype(F32)
        dlf = _dot_hi(utri, dcs) + carry[0:1, :]
        carry[0:1, :] = dlf[0:1, :]
        z = fa[...] + fb_r[...]
        dfa = jnp.where(lane < 8, dlf * _sigmoid(-z), 0.0)
        dp_o[:, FA:FA + 128] = dfa.astype(CDT)
        acc[4:5, 0:128] += jnp.sum(dfa, axis=0, keepdims=True)

        @pl.when(i == nt - 1)
        def _():
            foldm = ((_iota((512, 128), 0) & (HD - 1)) == _iota((512, 128), 1)).astype(F32)
            dgn_o[...] = _dot_hi(acc[...], foldm)

    def col(width, off):
        return pl.BlockSpec((tm, width), lambda i: (nt - 1 - i, off // width))

    def rows(width):
        return pl.BlockSpec((tm, width), lambda i: (nt - 1 - i, 0))

    def vec(width):
        return pl.BlockSpec((1, width), lambda i: (0, 0))

    pair = pl.BlockSpec((2, tm, 128), lambda i: (0, nt - 1 - i, 0))
    return pl.pallas_call(
        body, name=name, grid=(nt,),
        in_specs=[col(512, QA), col(512, KA), col(512, QB), col(128, KB), col(128, FA),
                  rows(512), rows(512), rows(512), rows(512), pair, pair, rows(512), rows(512), rows(D), rows(D),
                  vec(512), vec(512), vec(512), vec(128), vec(128)],
        out_specs=[rows(DP), pl.BlockSpec((8, 128), lambda i: (0, 0))],
        out_shape=[_sds((t, DP), CDT), _sds((8, 128), F32)],
        scratch_shapes=[pltpu.VMEM((8, 128), F32), pltpu.VMEM((8, 512), F32)],
        compiler_params=_params(("arbitrary",)),
    )(proj, proj, proj, proj, proj, dqf, dkf, dvf, dqs, dkse, dvse, dcq, dck, dga, dgb, gfq, gfk, gsq, gsk, fb)


def _gate_fwd(ofox, oswa, wbf, wbs, proj, name):
    t = ofox.shape[0]
    tm = _row_tile(t)
    tn = 512

    def body(of_r, os_r, wf_r, ws_r, ga_r, gb_r, y_o, yt_o, pf_o, ps_o, oft_o, ost_o):
        j = pl.program_id(1)
        pf = _dot(of_r[...], wf_r[...])
        ps = _dot(os_r[...], ws_r[...])
        y = _sigmoid(ga_r[...]) * pf + _sigmoid(gb_r[...]) * ps
        y_o[...] = y.astype(CDT)
        yt_o[...] = y.T.astype(CDT)
        pf_o[...] = pf.astype(CDT)
        ps_o[...] = ps.astype(CDT)

        @pl.when(j == 0)
        def _():
            oft_o[...] = of_r[...].astype(F32).T.astype(CDT)
            ost_o[...] = os_r[...].astype(F32).T.astype(CDT)

    tile = pl.BlockSpec((tm, tn), lambda i, j: (i, j))
    return pl.pallas_call(
        body, name=name, grid=(t // tm, D // tn),
        in_specs=[pl.BlockSpec((tm, 512), lambda i, j: (i, 0)), pl.BlockSpec((tm, 512), lambda i, j: (i, 0)),
                  pl.BlockSpec((512, tn), lambda i, j: (0, j)), pl.BlockSpec((512, tn), lambda i, j: (0, j)),
                  pl.BlockSpec((tm, tn), lambda i, j: (i, GA // tn + j)),
                  pl.BlockSpec((tm, tn), lambda i, j: (i, GB // tn + j))],
        out_specs=[tile, pl.BlockSpec((tn, tm), lambda i, j: (j, i)), tile, tile,
                   pl.BlockSpec((512, tm), lambda i, j: (0, i)), pl.BlockSpec((512, tm), lambda i, j: (0, i))],
        out_shape=[_sds((t, D), CDT), _sds((D, t), CDT), _sds((t, D), CDT), _sds((t, D), CDT),
                   _sds((512, t), CDT), _sds((512, t), CDT)],
        compiler_params=_params(("parallel", "arbitrary")),
    )(ofox, oswa, wbf, wbs, proj, proj)


def _gate_bwd(dy, pf, ps, proj, name):
    t = dy.shape[0]
    tm = _row_tile(t)
    tn = 512

    def body(dy_r, pf_r, ps_r, ga_r, gb_r, dpf_o, dps_o, dga_o, dgb_o):
        dy_ = dy_r[...]
        sa = _sigmoid(ga_r[...])
        sb = _sigmoid(gb_r[...])
        dpf_o[...] = (dy_ * sa).astype(CDT)
        dps_o[...] = (dy_ * sb).astype(CDT)
        dga_o[...] = (dy_ * pf_r[...].astype(F32) * (sa * (1.0 - sa))).astype(CDT)
        dgb_o[...] = (dy_ * ps_r[...].astype(F32) * (sb * (1.0 - sb))).astype(CDT)

    tile = pl.BlockSpec((tm, tn), lambda i, j: (i, j))
    return pl.pallas_call(
        body, name=name, grid=(t // tm, D // tn),
        in_specs=[tile, tile, tile,
                  pl.BlockSpec((tm, tn), lambda i, j: (i, GA // tn + j)),
                  pl.BlockSpec((tm, tn), lambda i, j: (i, GB // tn + j))],
        out_specs=[tile] * 4,
        out_shape=[_sds((t, D), CDT)] * 4,
        compiler_params=_params(("parallel", "parallel")),
    )(dy, pf, ps, proj, proj)


def _tri_steps(n, by_key):
    if by_key:
        pairs = [(i, j) for j in range(n) for i in range(j, n)]
    else:
        pairs = [(i, j) for i in range(n) for j in range(i + 1)]
    return (np.array([p[0] for p in pairs], np.int32), np.array([p[1] for p in pairs], np.int32))


def _head_col(blk, lane, h):
    return jnp.sum(jnp.where(lane == h, blk, 0.0), axis=1, keepdims=True)


def _head_row(blk, sub, h):
    return jnp.sum(jnp.where(sub == h, blk, 0.0), axis=0, keepdims=True)


def _ride_specs(ride):
    if ride is None:
        return [], [], [], [], []
    kind, src, out = ride
    return [src], [ANY], [out], [ANY], [pltpu.SemaphoreType.DMA((3,)), pltpu.SemaphoreType.DMA((3,))]


def _ride_start(kind, src_r, dst_o, send_sems, recv_sems):
    for cp in _ici_copies(kind, src_r, dst_o, send_sems, recv_sems)[0]:
        cp.start()


def _ride_wait(kind, src_r, dst_o, send_sems, recv_sems):
    sends, recvs = _ici_copies(kind, src_r, dst_o, send_sems, recv_sems)
    for cp in recvs:
        cp.wait_recv()
    for cp in sends:
        cp.wait_send()


def _fox_fwd(qf, kf, vf, c, ct, name, ride=None):
    t = qf.shape[0]
    ta = _row_tile(t)
    qi, kj = _tri_steps(t // ta, by_key=False)
    nsteps = len(qi)
    ride_in, ride_in_specs, ride_out, ride_out_specs, ride_sems = _ride_specs(ride)

    def body(qi_r, kj_r, q_r, k_r, v_r, c_r, ct_r, *rest):
        if ride is None:
            o_o, lse_o, m_sc, l_sc, acc_sc = rest
        else:
            src_r, o_o, lse_o, dst_o, m_sc, l_sc, acc_sc, send_sems, recv_sems = rest
        p = pl.program_id(0)
        n = pl.program_id(1)
        i = qi_r[n]
        j = kj_r[n]
        lane = _iota((1, 128), 1)
        lo = lane < HD

        if ride is not None:
            @pl.when((p == 0) & (n == 0))
            def _():
                _ride_start(ride[0], src_r, dst_o, send_sems, recv_sems)

        @pl.when(j == 0)
        def _():
            m_sc[...] = jnp.full(m_sc.shape, NEG, F32)
            l_sc[...] = jnp.zeros(l_sc.shape, F32)
            acc_sc[...] = jnp.zeros(acc_sc.shape, F32)

        def step(masked):
            q = q_r[...]
            k = k_r[...]
            v = v_r[...]
            if masked:
                rows = i * ta + _iota((ta, 1), 0)
                cols = j * ta + _iota((1, ta), 1)
                mask = (cols <= rows) & (cols >= PAD)
            sub = _iota((8, 1), 0)
            alphas, pvs = [], []
            for e in (0, 1):
                sel = lo if e == 0 else jnp.logical_not(lo)
                s = _dot_nt(jnp.where(sel, q, 0), k)
                s = s + _head_col(c_r[...], lane, 2 * p + e) - _head_row(ct_r[...], sub, 2 * p + e)
                if masked:
                    s = jnp.where(mask, s, NEG)
                m_prev = m_sc[e][:, 0:1]
                m_new = jnp.maximum(m_prev, jnp.max(s, axis=1, keepdims=True))
                alpha = jnp.exp(m_prev - m_new)
                pe = jnp.exp(s - m_new)
                l_new = alpha * l_sc[e][:, 0:1] + jnp.sum(pe, axis=1, keepdims=True)
                m_sc[e] = jnp.broadcast_to(m_new, (ta, 128))
                l_sc[e] = jnp.broadcast_to(l_new, (ta, 128))
                alphas.append(alpha)
                pvs.append(_dot(pe.astype(CDT), v))
            acc_sc[...] = acc_sc[...] * jnp.where(lo, alphas[0], alphas[1]) + jnp.where(lo, pvs[0], pvs[1])

        edge = (j == i) | (j == 0)

        @pl.when(edge)
        def _():
            step(True)

        @pl.when(jnp.logical_not(edge))
        def _():
            step(False)

        @pl.when(j == i)
        def _():
            l = jnp.where(lo, l_sc[0], l_sc[1])
            o_o[...] = (acc_sc[...] / l).astype(CDT)
            lse_o[...] = jnp.where(lo, m_sc[0], m_sc[1]) + jnp.log(l)

        if ride is not None:
            @pl.when((p == NPAIR - 1) & (n == nsteps - 1))
            def _():
                _ride_wait(ride[0], src_r, dst_o, send_sems, recv_sems)

    qblk = pl.BlockSpec((ta, 128), lambda p, n, qi_r, kj_r: (qi_r[n], p))
    kblk = pl.BlockSpec((ta, 128), lambda p, n, qi_r, kj_r: (kj_r[n], p))
    grid_spec = pltpu.PrefetchScalarGridSpec(
        num_scalar_prefetch=2, grid=(NPAIR, nsteps),
        in_specs=[qblk, kblk, kblk,
                  pl.BlockSpec((ta, 128), lambda p, n, qi_r, kj_r: (qi_r[n], 0)),
                  pl.BlockSpec((8, ta), lambda p, n, qi_r, kj_r: (0, kj_r[n]))] + ride_in_specs,
        out_specs=[qblk, qblk] + ride_out_specs,
        scratch_shapes=[pltpu.VMEM((2, ta, 128), F32), pltpu.VMEM((2, ta, 128), F32), pltpu.VMEM((ta, 128), F32)]
        + ride_sems,
    )
    return pl.pallas_call(
        body, name=name, grid_spec=grid_spec,
        out_shape=[_sds((t, 512), CDT), _sds((t, 512), F32)] + ride_out,
        compiler_params=_params(("arbitrary", "arbitrary")),
    )(jnp.asarray(qi), jnp.asarray(kj), qf, kf, vf, c, ct, *ride_in)


def _fox_bwd(qf, kf, vf, c, ct, o, lse, do, name, ride=None):
    t = qf.shape[0]
    ta = _row_tile(t)
    nq = t // ta
    qi, kj = _tri_steps(nq, by_key=True)
    nsteps = len(qi)
    ride_in, ride_in_specs, ride_out, ride_out_specs, ride_sems = _ride_specs(ride)

    def body(qi_r, kj_r, q_r, k_r, v_r, c_r, ct_r, o_r, lse_r, do_r, *rest):
        if ride is None:
            dq_o, dcq_o, dk_o, dv_o, dck_o, dk_sc, dv_sc, dck_sc = rest
        else:
            src_r, dq_o, dcq_o, dk_o, dv_o, dck_o, dst_o, dk_sc, dv_sc, dck_sc, send_sems, recv_sems = rest
        p = pl.program_id(0)
        n = pl.program_id(1)
        i = qi_r[n]
        j = kj_r[n]
        lane = _iota((1, 128), 1)
        lo = lane < HD

        if ride is not None:
            @pl.when((p == 0) & (n == 0))
            def _():
                _ride_start(ride[0], src_r, dst_o, send_sems, recv_sems)

        @pl.when(n == 0)
        def _():
            dq_o[...] = jnp.zeros(dq_o.shape, F32)
            dcq_o[...] = jnp.zeros(dcq_o.shape, F32)

        @pl.when(i == j)
        def _():
            dk_sc[...] = jnp.zeros(dk_sc.shape, F32)
            dv_sc[...] = jnp.zeros(dv_sc.shape, F32)
            dck_sc[...] = jnp.zeros(dck_sc.shape, F32)

        def step(masked):
            q = q_r[...]
            k = k_r[...]
            v = v_r[...]
            do_ = do_r[...]
            dd = do_ * o_r[...].astype(F32)
            lse = lse_r[...]
            if masked:
                rows = i * ta + _iota((ta, 1), 0)
                cols = j * ta + _iota((1, ta), 1)
                mask = (cols <= rows) & (cols >= PAD)
            sub = _iota((8, 1), 0)
            dq_add = jnp.zeros((ta, 128), F32)
            dk_add = jnp.zeros((ta, 128), F32)
            dv_add = jnp.zeros((ta, 128), F32)
            rsum, csum = [], []
            for e in (0, 1):
                sel = lo if e == 0 else jnp.logical_not(lo)
                qe = jnp.where(sel, q, 0)
                doe = jnp.where(sel, do_, 0.0).astype(CDT)
                s = _dot_nt(qe, k)
                s = s + _head_col(c_r[...], lane, 2 * p + e) - _head_row(ct_r[...], sub, 2 * p + e)
                if masked:
                    s = jnp.where(mask, s, NEG)
                pr = jnp.exp(s - lse[:, HD * e:HD * e + 1])
                dv_add = dv_add + _dot(pr.T.astype(CDT), doe)
                dp = _dot_nt(doe, v)
                delta = jnp.sum(jnp.where(sel, dd, 0.0), axis=1, keepdims=True)
                ds = pr * (dp - delta)
                dq_add = dq_add + _dot(ds.astype(CDT), jnp.where(sel, k, 0))
                dst = ds.T
                dk_add = dk_add + _dot(dst.astype(CDT), qe)
                rsum.append(jnp.sum(ds, axis=1, keepdims=True))
                csum.append(jnp.sum(dst, axis=1, keepdims=True))
            rs = pl.ds(pl.multiple_of(i * ta, ta), ta)
            dq_o[rs, :] += dq_add
            dcq_o[rs, :] += jnp.where(lo, rsum[0], rsum[1])
            dk_sc[...] += dk_add
            dv_sc[...] += dv_add
            dck_sc[...] += jnp.where(lo, csum[0], csum[1])

        edge = (j == i) | (j == 0)

        @pl.when(edge)
        def _():
            step(True)

        @pl.when(jnp.logical_not(edge))
        def _():
            step(False)

        @pl.when(i == nq - 1)
        def _():
            dk_o[...] = dk_sc[...]
            dv_o[...] = dv_sc[...]
            dck_o[...] = dck_sc[...]

        if ride is not None:
            @pl.when((p == NPAIR - 1) & (n == nsteps - 1))
            def _():
                _ride_wait(ride[0], src_r, dst_o, send_sems, recv_sems)

    qblk = pl.BlockSpec((ta, 128), lambda p, n, qi_r, kj_r: (qi_r[n], p))
    kblk = pl.BlockSpec((ta, 128), lambda p, n, qi_r, kj_r: (kj_r[n], p))
    whole = pl.BlockSpec((t, 128), lambda p, n, qi_r, kj_r: (0, p))
    grid_spec = pltpu.PrefetchScalarGridSpec(
        num_scalar_prefetch=2, grid=(NPAIR, nsteps),
        in_specs=[qblk, kblk, kblk,
                  pl.BlockSpec((ta, 128), lambda p, n, qi_r, kj_r: (qi_r[n], 0)),
                  pl.BlockSpec((8, ta), lambda p, n, qi_r, kj_r: (0, kj_r[n])),
                  qblk, qblk, qblk] + ride_in_specs,
        out_specs=[whole, whole, kblk, kblk, kblk] + ride_out_specs,
        scratch_shapes=[pltpu.VMEM((ta, 128), F32)] * 3 + ride_sems,
    )
    return pl.pallas_call(
        body, name=name, grid_spec=grid_spec,
        out_shape=[_sds((t, 512), F32)] * 5 + ride_out,
        compiler_params=_params(("arbitrary", "arbitrary")),
    )(jnp.asarray(qi), jnp.asarray(kj), qf, kf, vf, c, ct, o, lse, do, *ride_in)


def _bucket_table():
    r = np.arange(BLK)[:, None]
    c = np.arange(3 * BLK)[None, :]
    d = np.where(c < BLK, r + BLK - c, r - (c - BLK))
    n = np.maximum(d, 0)
    max_exact = N_BUCKETS // 2
    nf = np.maximum(n, 1).astype(np.float32)
    large = max_exact + (np.log(nf / max_exact) / math.log(BLK / max_exact) * (N_BUCKETS - max_exact)).astype(np.int32)
    large = np.minimum(large, N_BUCKETS - 1)
    b = np.where(n < max_exact, n, large)
    return np.where(c < 2 * BLK, b, N_BUCKETS - 1).astype(np.int32)


def _bias_fwd(table, name):
    bucket = jnp.asarray(_bucket_table())

    def body(tab_r, b_r, o_o):
        h = pl.program_id(0)
        b = b_r[...]
        acc = jnp.zeros(b.shape, F32)
        for k in range(N_BUCKETS):
            acc = jnp.where(b == k, tab_r[k, h], acc)
        o_o[...] = acc

    return pl.pallas_call(
        body, name=name, grid=(8,),
        in_specs=[pl.BlockSpec(memory_space=pltpu.SMEM), pl.BlockSpec((BLK, 3 * BLK), lambda h: (0, 0))],
        out_specs=pl.BlockSpec((None, BLK, 3 * BLK), lambda h: (h, 0, 0)),
        out_shape=_sds((8, BLK, 3 * BLK), F32),
        compiler_params=_params(("parallel",)),
    )(table, bucket)


def _bias_bwd(dbias, name):
    bucket = jnp.asarray(_bucket_table())

    def body(d_r, b_r, o_o):
        h = pl.program_id(0)
        b = b_r[...]
        d = d_r[...]
        lane = _iota((1, 128), 1)
        row = jnp.zeros((1, 128), F32)
        for k in range(N_BUCKETS):
            row = jnp.where(lane == k, jnp.sum(jnp.where(b == k, d, 0.0)), row)
        o_o[pl.ds(h, 1), :] = row

    return pl.pallas_call(
        body, name=name, grid=(8,),
        in_specs=[pl.BlockSpec((None, BLK, 3 * BLK), lambda h: (h, 0, 0)), pl.BlockSpec((BLK, 3 * BLK), lambda h: (0, 0))],
        out_specs=pl.BlockSpec((8, 128), lambda h: (0, 0)),
        out_shape=_sds((8, 128), F32),
        compiler_params=_params(("arbitrary",)),
    )(dbias, bucket)


def _swa_valid(i):
    r = _iota((BLK, 1), 0)
    c = _iota((1, 3 * BLK), 1)
    prev = (c < BLK) & (c > r) & (i >= 1) & ((i - 1) * BLK + c >= PAD)
    cc = c - BLK
    cur = (c >= BLK) & (c < 2 * BLK) & (cc <= r) & (i * BLK + cc >= PAD)
    cm = c - 2 * BLK
    meta = (c >= 2 * BLK) & (cm >= PAD) & (i * BLK + r - cm >= BLK)
    return prev | cur | meta


def _swa_kv_specs():
    def at(f):
        return pl.BlockSpec((None, BLK, 128), lambda p, i: (p // 2, f(i), 0))
    return [at(lambda i: jnp.maximum(i - 1, 0)), at(lambda i: i), at(lambda i: 0)]


def _swa_fwd(qs, kse, vse, bias, sinks, name):
    t = qs.shape[0]

    def body(sink_r, q_r, kp_r, kc_r, km_r, vp_r, vc_r, vm_r, b_r, o_o, lse_o):
        p = pl.program_id(0)
        i = pl.program_id(1)
        lo = _iota((1, 128), 1) < HD
        q = q_r[...]
        k3 = jnp.concatenate([kp_r[...], kc_r[...], km_r[...]], axis=0)
        v3 = jnp.concatenate([vp_r[...], vc_r[...], vm_r[...]], axis=0)
        valid = _swa_valid(i)
        outs, lses = [], []
        for e in (0, 1):
            sel = lo if e == 0 else jnp.logical_not(lo)
            s = _dot_nt(jnp.where(sel, q, 0), k3) + b_r[e]
            s = jnp.where(valid, s, NEG)
            sink = sink_r[2 * p + e]
            mx = jnp.maximum(jnp.max(s, axis=1, keepdims=True), sink)
            pe = jnp.exp(s - mx)
            den = jnp.sum(pe, axis=1, keepdims=True) + jnp.exp(sink - mx)
            outs.append(_dot(pe.astype(CDT), v3) / den)
            lses.append(mx + jnp.log(den))
        o_o[...] = jnp.where(lo, outs[0], outs[1]).astype(CDT)
        lse_o[...] = jnp.where(lo, lses[0], lses[1])

    qblk = pl.BlockSpec((BLK, 128), lambda p, i: (i, p))
    return pl.pallas_call(
        body, name=name, grid=(NPAIR, t // BLK),
        in_specs=[pl.BlockSpec(memory_space=pltpu.SMEM), qblk] + _swa_kv_specs() + _swa_kv_specs()
        + [pl.BlockSpec((2, BLK, 3 * BLK), lambda p, i: (p, 0, 0))],
        out_specs=[qblk, qblk],
        out_shape=[_sds((t, 512), CDT), _sds((t, 512), F32)],
        compiler_params=_params(("parallel", "parallel")),
    )(sinks, qs, kse, kse, kse, vse, vse, vse, bias)


def _swa_bwd(qs, kse, vse, bias, sinks, o, lse, do, name):
    t = qs.shape[0]

    def body(sink_r, q_r, kp_r, kc_r, km_r, vp_r, vc_r, vm_r, b_r, o_r, lse_r, do_r,
             dq_o, dk_o, dv_o, db_o, dsk_o):
        p = pl.program_id(0)
        i = pl.program_id(1)
        lo = _iota((1, 128), 1) < HD

        @pl.when((i == 0) & (p % 2 == 0))
        def _():
            dk_o[...] = jnp.zeros(dk_o.shape, F32)
            dv_o[...] = jnp.zeros(dv_o.shape, F32)

        @pl.when(i == 0)
        def _():
            db_o[...] = jnp.zeros(db_o.shape, F32)
            dsk_o[...] = jnp.zeros(dsk_o.shape, F32)

        q = q_r[...]
        do_ = do_r[...]
        dd = do_ * o_r[...].astype(F32)
        lse = lse_r[...]
        k3 = jnp.concatenate([kp_r[...], kc_r[...], km_r[...]], axis=0)
        v3 = jnp.concatenate([vp_r[...], vc_r[...], vm_r[...]], axis=0)
        valid = _swa_valid(i)
        dq = jnp.zeros((BLK, 128), F32)
        dk3 = jnp.zeros((3 * BLK, 128), F32)
        dv3 = jnp.zeros((3 * BLK, 128), F32)
        dsink = []
        for e in (0, 1):
            sel = lo if e == 0 else jnp.logical_not(lo)
            qe = jnp.where(sel, q, 0)
            doe = jnp.where(sel, do_, 0.0).astype(CDT)
            lse_e = lse[:, HD * e:HD * e + 1]
            s = _dot_nt(qe, k3) + b_r[e]
            s = jnp.where(valid, s, NEG)
            pr = jnp.exp(s - lse_e)
            delta = jnp.sum(jnp.where(sel, dd, 0.0), axis=1, keepdims=True)
            ds = pr * (_dot_nt(doe, v3) - delta)
            db_o[e] += ds
            dsink.append(-jnp.sum(jnp.exp(sink_r[2 * p + e] - lse_e) * delta, axis=0, keepdims=True))
            dq = dq + _dot(ds.astype(CDT), jnp.where(sel, k3, 0))
            dk3 = dk3 + _dot(ds.T.astype(CDT), qe)
            dv3 = dv3 + _dot(pr.T.astype(CDT), doe)
        dq_o[...] = dq
        prev = pl.ds(pl.multiple_of(jnp.maximum(i - 1, 0) * BLK, BLK), BLK)
        cur = pl.ds(pl.multiple_of(i * BLK, BLK), BLK)
        dk_o[prev, :] += dk3[0:BLK]
        dk_o[cur, :] += dk3[BLK:2 * BLK]
        dk_o[0:BLK, :] += dk3[2 * BLK:]
        dv_o[prev, :] += dv3[0:BLK]
        dv_o[cur, :] += dv3[BLK:2 * BLK]
        dv_o[0:BLK, :] += dv3[2 * BLK:]
        dsk_o[0:1, :] += jnp.where(lo, dsink[0], dsink[1])

    qblk = pl.BlockSpec((BLK, 128), lambda p, i: (i, p))
    kvacc = pl.BlockSpec((None, t, 128), lambda p, i: (p // 2, 0, 0))
    bblk = pl.BlockSpec((2, BLK, 3 * BLK), lambda p, i: (p, 0, 0))
    return pl.pallas_call(
        body, name=name, grid=(NPAIR, t // BLK),
        in_specs=[pl.BlockSpec(memory_space=pltpu.SMEM), qblk] + _swa_kv_specs() + _swa_kv_specs()
        + [bblk, qblk, qblk, qblk],
        out_specs=[qblk, kvacc, kvacc, bblk, pl.BlockSpec((None, 8, 128), lambda p, i: (p, 0, 0))],
        out_shape=[_sds((t, 512), F32), _sds((2, t, 128), F32), _sds((2, t, 128), F32),
                   _sds((8, BLK, 3 * BLK), F32), _sds((NPAIR, 8, 128), F32)],
        compiler_params=_params(("arbitrary", "arbitrary")),
    )(sinks, qs, kse, kse, kse, vse, vse, vse, bias, o, lse, do)


def _adamw(w, g, m, v, name):
    r, c = w.shape
    tr = 128 if r % 128 == 0 else r

    def body(w_r, g_r, m_r, v_r, d_o, m_o, v_o):
        g_ = g_r[...]
        m_ = ADAM_B1 * m_r[...] + (1.0 - ADAM_B1) * g_
        v_ = ADAM_B2 * v_r[...] + (1.0 - ADAM_B2) * jnp.square(g_)
        m_hat = m_ / (1.0 - ADAM_B1 ** ADAM_STEP)
        v_hat = v_ / (1.0 - ADAM_B2 ** ADAM_STEP)
        d_o[...] = -ADAM_LR * (m_hat / (jnp.sqrt(v_hat) + ADAM_EPS) + ADAM_WD * w_r[...])
        m_o[...] = m_
        v_o[...] = v_

    blk = pl.BlockSpec((tr, c), lambda i: (i, 0))
    return pl.pallas_call(
        body, name=name, grid=(r // tr,),
        in_specs=[blk] * 4, out_specs=[blk] * 3, out_shape=[_sds((r, c), F32)] * 3,
        compiler_params=_params(("parallel",)),
    )(w, g, m, v)


def _pair_add(own, got, half_idx, name):
    tr = PACK_TILE
    nb = HALF_ROWS // tr
    nl = own.shape[1]

    def body(c_r, a_r, b_r, o_o):
        o_o[...] = (a_r[...].astype(F32) + b_r[...].astype(F32)).astype(CDT)

    grid_spec = pltpu.PrefetchScalarGridSpec(
        num_scalar_prefetch=1, grid=(4, nl, nb),
        in_specs=[pl.BlockSpec((None, None, tr, 128), lambda s, l, i, c_r: (s, l, c_r[0] * nb + i, 0)),
                  pl.BlockSpec((None, None, tr, 128), lambda s, l, i, c_r: (s, l, i, 0))],
        out_specs=pl.BlockSpec((None, None, tr, 128), lambda s, l, i, c_r: (s, l, i, 0)),
    )
    return pl.pallas_call(
        body, name=name, grid_spec=grid_spec, out_shape=_sds((4, nl, HALF_ROWS, 128), CDT),
        compiler_params=_params(("parallel", "parallel", "parallel")),
    )(half_idx, own, got)


def _sum4(ps, got, shard_idx, name):
    tr = PACK_TILE
    nl = ps.shape[1]

    def body(s_r, a_r, b_r, o_o):
        o_o[...] = ((a_r[...].astype(F32) + b_r[0].astype(F32)) + b_r[1].astype(F32)) + b_r[2].astype(F32)

    grid_spec = pltpu.PrefetchScalarGridSpec(
        num_scalar_prefetch=1, grid=(nl, HALF_ROWS // tr),
        in_specs=[pl.BlockSpec((None, None, tr, 128), lambda l, i, s_r: (s_r[0], l, i, 0)),
                  pl.BlockSpec((3, None, tr, 128), lambda l, i, s_r: (0, l, i, 0))],
        out_specs=pl.BlockSpec((None, tr, 128), lambda l, i, s_r: (l, i, 0)),
    )
    return pl.pallas_call(
        body, name=name, grid_spec=grid_spec, out_shape=_sds((nl, HALF_ROWS, 128), F32),
        compiler_params=_params(("parallel", "parallel")),
    )(shard_idx, ps, got)


def _sum8(slots, name):
    def body(a_r, o_o):
        acc = a_r[0]
        for k in range(1, 8):
            acc = acc + a_r[k]
        o_o[...] = acc

    return pl.pallas_call(
        body, name=name, out_shape=_sds((SMALL_ROWS, 128), F32),
        in_specs=[pl.BlockSpec(memory_space=pltpu.VMEM)], out_specs=pl.BlockSpec(memory_space=pltpu.VMEM),
        compiler_params=_params(),
    )(slots)


def _place():
    x, y, c = lax.axis_index("x"), lax.axis_index("y"), lax.axis_index("c")
    chips = [(1 - x, y), (x, 1 - y), (1 - x, 1 - y)]
    return x, y, c, chips


def _remote(src, dst, send_sems, recv_sems, k, to):
    return pltpu.make_async_remote_copy(src_ref=src, dst_ref=dst, send_sem=send_sems.at[k], recv_sem=recv_sems.at[k],
                                        device_id=to, device_id_type=MESH_ID)


ANY = pl.BlockSpec(memory_space=pl.ANY)


def _ici_copies(kind, src_r, dst_r, send_sems, recv_sems):
    x, y, c, chips = _place()
    s = 2 * x + y
    mine = pl.ds(pl.multiple_of(c * HALF_ROWS, 16), HALF_ROWS)
    sends, recvs = [], []
    for j, (cx, cy) in enumerate(chips):
        sj = 2 * cx + cy
        if kind == "gather":
            sends.append(_remote(src_r.at[:, mine], dst_r.at[s, :, mine], send_sems, recv_sems, j, (cx, cy, c)))
            recvs.append(_remote(src_r.at[:, mine], dst_r.at[sj, :, mine], send_sems, recv_sems, j, (cx, cy, c)))
        else:
            sends.append(_remote(src_r.at[sj], dst_r.at[j], send_sems, recv_sems, j, (cx, cy, c)))
            recvs.append(sends[-1])
    return sends, recvs


def _forward_halves(wall, name):
    def body(w_r, w_o, send_sems, recv_sems):
        x, y, c, chips = _place()
        mine = pl.ds(pl.multiple_of(c * HALF_ROWS, 16), HALF_ROWS)
        other = pl.ds(pl.multiple_of((1 - c) * HALF_ROWS, 16), HALF_ROWS)
        sent = []
        for j, (cx, cy) in enumerate(chips):
            sj = 2 * cx + cy
            sent.append(_remote(w_r.at[sj, :, mine], w_o.at[sj, :, mine], send_sems, recv_sems, j, (x, y, 1 - c)))
        for cp in sent:
            cp.start()
        for j, (cx, cy) in enumerate(chips):
            sj = 2 * cx + cy
            _remote(w_r.at[sj, :, other], w_o.at[sj, :, other], send_sems, recv_sems, j, (x, y, 1 - c)).wait_recv()
        for cp in sent:
            cp.wait_send()

    return pl.pallas_call(
        body, name=name, out_shape=_sds(wall.shape, wall.dtype),
        in_specs=[ANY], out_specs=ANY, input_output_aliases={0: 0},
        scratch_shapes=[pltpu.SemaphoreType.DMA((3,)), pltpu.SemaphoreType.DMA((3,))],
    )(wall)


def _gather_weights(wflat, mflat, name):
    nl = wflat.shape[0]

    def body(w_r, m_r, wall_o, mall_o, send_sems, recv_sems):
        x, y, c, chips = _place()
        s = 2 * x + y
        sib = (x, y, 1 - c)
        mine = pl.ds(pl.multiple_of(c * HALF_ROWS, 16), HALF_ROWS)
        other = pl.ds(pl.multiple_of((1 - c) * HALF_ROWS, 16), HALF_ROWS)
        sent = []
        for j, (cx, cy) in enumerate(chips):
            sent.append(_remote(w_r.at[:, mine], wall_o.at[s, :, mine], send_sems, recv_sems, j, (cx, cy, c)))
            sent.append(_remote(m_r, mall_o.at[s], send_sems, recv_sems, 6 + j, (cx, cy, c)))
        for cp in sent:
            cp.start()
        for j, (cx, cy) in enumerate(chips):
            sj = 2 * cx + cy
            _remote(w_r.at[:, mine], wall_o.at[sj, :, mine], send_sems, recv_sems, j, sib).wait_recv()
            fwd = _remote(wall_o.at[sj, :, mine], wall_o.at[sj, :, mine], send_sems, recv_sems, 3 + j, sib)
            fwd.start()
            sent.append(fwd)
        for j, (cx, cy) in enumerate(chips):
            sj = 2 * cx + cy
            _remote(w_r.at[:, other], wall_o.at[sj, :, other], send_sems, recv_sems, 3 + j, sib).wait_recv()
            _remote(m_r, mall_o.at[sj], send_sems, recv_sems, 6 + j, sib).wait_recv()
        for cp in sent:
            cp.wait_send()

    return pl.pallas_call(
        body, name=name,
        out_shape=[_sds((4, nl, LAYER_ROWS, 128), CDT), _sds((4, META_ROWS, 128), F32)],
        in_specs=[ANY, ANY], out_specs=[ANY, ANY],
        scratch_shapes=[pltpu.SemaphoreType.DMA((9,)), pltpu.SemaphoreType.DMA((9,))],
    )(wflat, mflat)


def _swap_only(gfl, name):
    def body(g_r, got_o, send_sem, recv_sem):
        x, y, c, _ = _place()
        theirs = pl.ds(pl.multiple_of((1 - c) * HALF_ROWS, 16), HALF_ROWS)
        cp = _remote(g_r.at[:, :, theirs, :], got_o, send_sem, recv_sem, 0, (x, y, 1 - c))
        cp.start()
        cp.wait()

    return pl.pallas_call(
        body, name=name, out_shape=_sds((4, gfl.shape[1], HALF_ROWS, 128), CDT),
        in_specs=[ANY], out_specs=ANY,
        scratch_shapes=[pltpu.SemaphoreType.DMA((1,)), pltpu.SemaphoreType.DMA((1,))],
    )(gfl)


def _swap_halves(gfl, gsm, name):
    def body(g_r, s_r, got_o, slots_o, send_sems, recv_sems, loc_sem):
        x, y, c, _ = _place()
        me = 4 * x + 2 * y + c
        theirs = pl.ds(pl.multiple_of((1 - c) * HALF_ROWS, 16), HALF_ROWS)
        loc = pltpu.make_async_copy(s_r, slots_o.at[me], loc_sem.at[0])
        loc.start()
        sent = [_remote(g_r.at[:, :, theirs, :], got_o, send_sems, recv_sems, 0, (x, y, 1 - c))]
        for k in range(1, 8):
            px, py, pc = x ^ (k >> 2), y ^ ((k >> 1) & 1), c ^ (k & 1)
            sent.append(_remote(s_r, slots_o.at[me], send_sems, recv_sems, k, (px, py, pc)))
        for cp in sent:
            cp.start()
        _remote(g_r.at[:, :, theirs, :], got_o, send_sems, recv_sems, 0, (x, y, 1 - c)).wait_recv()
        for k in range(1, 8):
            px, py, pc = x ^ (k >> 2), y ^ ((k >> 1) & 1), c ^ (k & 1)
            _remote(s_r, slots_o.at[4 * px + 2 * py + pc], send_sems, recv_sems, k, (px, py, pc)).wait_recv()
        for cp in sent:
            cp.wait_send()
        loc.wait()

    return pl.pallas_call(
        body, name=name,
        out_shape=[_sds((4, gfl.shape[1], HALF_ROWS, 128), CDT), _sds((8, SMALL_ROWS, 128), F32)],
        in_specs=[ANY, ANY], out_specs=[ANY, ANY],
        scratch_shapes=[pltpu.SemaphoreType.DMA((8,)), pltpu.SemaphoreType.DMA((8,)), pltpu.SemaphoreType.DMA((1,))],
    )(gfl, gsm)


def _scatter_shards(ps, name):
    def body(p_r, got_o, send_sems, recv_sems):
        x, y, c, chips = _place()
        sent = [_remote(p_r.at[2 * cx + cy], got_o.at[j], send_sems, recv_sems, j, (cx, cy, c))
                for j, (cx, cy) in enumerate(chips)]
        for cp in sent:
            cp.start()
        for j in range(3):
            _remote(p_r.at[0], got_o.at[j], send_sems, recv_sems, j, (x, y, c)).wait_recv()
        for cp in sent:
            cp.wait_send()

    return pl.pallas_call(
        body, name=name, out_shape=_sds((3, ps.shape[1], HALF_ROWS, 128), CDT),
        in_specs=[ANY], out_specs=ANY,
        scratch_shapes=[pltpu.SemaphoreType.DMA((3,)), pltpu.SemaphoreType.DMA((3,))],
    )(ps)


def _join_halves(red_half, name):
    def body(h_r, got_o, send_sem, recv_sem):
        x, y, c, _ = _place()
        out = _remote(h_r, got_o, send_sem, recv_sem, 0, (x, y, 1 - c))
        out.start()
        out.wait()

    return pl.pallas_call(
        body, name=name, out_shape=_sds(red_half.shape, F32),
        in_specs=[ANY], out_specs=ANY,
        scratch_shapes=[pltpu.SemaphoreType.DMA((1,)), pltpu.SemaphoreType.DMA((1,))],
    )(red_half)


def _pack_shard(local):
    pad = [jnp.zeros((LAYER_ROWS - LAYER_RAW, 128), local[SHARD_ITEMS[0][0]].dtype)]
    return jnp.stack([jnp.concatenate([local[nm][l].reshape(-1, 128) for nm, _, _ in SHARD_ITEMS] + pad)
                      for l in range(2)])


def _unpack_layer(wall):
    ws = {}
    off = 0
    for nm, (r, c), kind in SHARD_ITEMS:
        n = r * c // 128
        piece = wall[:, off:off + n, :].reshape(4, r, c)
        off += n
        if kind == "row":
            ws[nm] = piece.reshape(4 * r, c)
        else:
            ws[nm] = jnp.concatenate([piece[s] for s in range(4)], axis=1)
    return ws


def _mix_cols(w):
    return jnp.concatenate([w[:, 2312:4360], w[:, 0:1536], w[:, 1544:2312], w[:, 1536:1544],
                            jnp.zeros((w.shape[0], DP - D_IN), w.dtype)], axis=1)


def _unmix_cols(w):
    return jnp.concatenate([w[:, QA:QA + 1536], w[:, FA:FA + 8], w[:, QB:QB + 768], w[:, GA:GA + 2048]], axis=1)


def _pack_grads(grads):
    pad = [jnp.zeros((LAYER_ROWS - LAYER_RAW, 128), CDT)]
    shards = []
    for s in range(4):
        parts = []
        for nm, (r, c), kind in SHARD_ITEMS:
            g = grads[nm]
            parts.append((g[s * r:(s + 1) * r] if kind == "row" else g[:, s * c:(s + 1) * c]).reshape(-1, 128))
        shards.append(jnp.concatenate(parts + pad)[None])
    return jnp.stack(shards)


def _unpack_shard(red):
    out = {}
    off = 0
    for nm, (r, c), _ in SHARD_ITEMS:
        n = r * c // 128
        out[nm] = red[:, off:off + n, :].reshape(2, r, c)
        off += n
    return out


def _rows128(a, rows):
    flat = a.reshape(-1)
    return jnp.pad(flat, (0, rows * 128 - flat.shape[0])).reshape(rows, 128)


SMALL_ITEMS = (("rel_bias_table", 2), ("ffn1_norm", 16), ("mix_norm", 16), ("ffn2_norm", 16), ("forget_bias", 1),
               ("fox_q_norm", 1), ("fox_k_norm", 1), ("swa_q_norm", 1), ("swa_k_norm", 1), ("swa_sinks", 1))
SMALL_ADAM_ROWS = 96


def _layer_fwd(h, lw, l, ride=None):
    sv = {"h0": h}
    a, sv["a1t"] = _rms_fwd(h, lw["ffn1_norm"], f"rms_fwd_a{l}")
    sv["gu1"], s, sv["s1t"] = _ffn_in(a, lw["ffn1_w_in"], f"ffn_in_a{l}")
    h = _mm_res(s, lw["ffn1_w_out"], h, 0.5, f"ffn_out_a{l}")
    sv["h1"] = h
    a, sv["amt"] = _rms_fwd(h, lw["mix_norm"], f"rms_fwd_m{l}")
    proj = _mm(a, lw["w_mix"], F32, _row_tile(h.shape[0]), 640, f"proj{l}", b_resident=True)
    sv["proj"] = proj
    qf, kf, vf, qs, kse, vse, c, ct = _qknorm_fwd(proj, lw["gfq"], lw["gfk"], lw["gsq"], lw["gsk"], lw["fb"],
                                                   f"qknorm_fwd{l}")
    ofox, lse_f, *rode = _fox_fwd(qf, kf, vf, c, ct, f"fox_fwd{l}", ride)
    oswa, lse_s = _swa_fwd(qs, kse, vse, lw["bias"], lw["sinks"], f"swa_fwd{l}")
    sv.update(qf=qf, kf=kf, vf=vf, qs=qs, kse=kse, vse=vse, c=c, ct=ct, ofox=ofox, oswa=oswa, lse_f=lse_f, lse_s=lse_s)
    y, sv["yt"], sv["pf"], sv["ps"], sv["oft"], sv["ost"] = _gate_fwd(ofox, oswa, lw["w_branch_fox"], lw["w_branch_swa"],
                                                                     proj, f"gate_fwd{l}")
    h = _mm_res(y, lw["w_out"], h, 1.0, f"mix_out{l}")
    sv["h2"] = h
    a, sv["a2t"] = _rms_fwd(h, lw["ffn2_norm"], f"rms_fwd_b{l}")
    sv["gu2"], s, sv["s2t"] = _ffn_in(a, lw["ffn2_w_in"], f"ffn_in_b{l}")
    h = _mm_res(s, lw["ffn2_w_out"], h, 0.5, f"ffn_out_b{l}")
    return h, sv, (rode[0] if rode else None)


def _ffn_bwd(dh, dhb, h_in, at, gu, st, norm, w_in, w_out, tag):
    dgu = _ffn_bwd_mid(dhb, w_out, gu, f"ffn_bwd_mid_{tag}")
    d_w_out = _mm(st, dhb, CDT, 256, 512, f"dw_ffn_out_{tag}", scale=0.5)
    da = _ffn_bwd_in(dgu, w_in, f"ffn_bwd_in_{tag}")
    d_w_in = _mm(at, dgu, CDT, 512, 256, f"dw_ffn_in_{tag}")
    dh, dhb, dg = _rms_bwd(da, h_in, norm, dh, f"rms_bwd_{tag}")
    return dh, dhb, d_w_out, d_w_in, dg


def _layer_bwd(dh, dhb, sv, lw, l, ride=None):
    g = {}
    dh, dhb, g["ffn2_w_out"], g["ffn2_w_in"], g["ffn2_norm"] = _ffn_bwd(
        dh, dhb, sv["h2"], sv["a2t"], sv["gu2"], sv["s2t"], lw["ffn2_norm"], lw["ffn2_w_in"], lw["ffn2_w_out"], f"b{l}")
    dy = _mm_nt(dhb, lw["w_out"], f"d_y{l}")
    g["w_out"] = _mm(sv["yt"], dhb, CDT, 512, 512, f"dw_out{l}")
    dpf, dps, dga, dgb = _gate_bwd(dy, sv["pf"], sv["ps"], sv["proj"], f"gate_bwd{l}")
    do_f = _mm_nt(dpf, lw["w_branch_fox"], f"d_ofox{l}")
    do_s = _mm_nt(dps, lw["w_branch_swa"], f"d_oswa{l}")
    g["w_branch_fox"] = _mm(sv["oft"], dpf, CDT, 512, 512, f"dw_bfox{l}")
    g["w_branch_swa"] = _mm(sv["ost"], dps, CDT, 512, 512, f"dw_bswa{l}")
    dqf, dcq, dkf, dvf, dck, *rode = _fox_bwd(sv["qf"], sv["kf"], sv["vf"], sv["c"], sv["ct"], sv["ofox"], sv["lse_f"],
                                              do_f, f"fox_bwd{l}", ride)
    g["rode"] = rode[0] if rode else None
    dqs, dkse, dvse, dbias, dsk = _swa_bwd(sv["qs"], sv["kse"], sv["vse"], lw["bias"], lw["sinks"], sv["oswa"],
                                           sv["lse_s"], do_s, f"swa_bwd{l}")
    dproj, dgn = _qknorm_bwd(sv["proj"], dqf, dkf, dvf, dqs, dkse, dvse, dcq, dck, dga, dgb,
                             lw["gfq"], lw["gfk"], lw["gsq"], lw["gsk"], lw["fb"], f"qknorm_bwd{l}")
    dam = _mm_nt(dproj, lw["w_mix"], f"d_am{l}")
    g["w_mix"] = _mm(sv["amt"], dproj, CDT, 512, 640, f"dw_mix{l}")
    dh, dhb, g["mix_norm"] = _rms_bwd(dam, sv["h1"], lw["mix_norm"], dh, f"rms_bwd_m{l}")
    g["dbias"], g["dsk"], g["dgn"] = dbias, dsk, dgn
    dh, dhb, g["ffn1_w_out"], g["ffn1_w_in"], g["ffn1_norm"] = _ffn_bwd(
        dh, dhb, sv["h0"], sv["a1t"], sv["gu1"], sv["s1t"], lw["ffn1_norm"], lw["ffn1_w_in"], lw["ffn1_w_out"], f"a{l}")
    return dh, dhb, g


def kernel(x, meta_tokens, rel_bias_table, ffn1_norm, ffn1_w_in, ffn1_w_out, mix_norm, w_in, forget_bias, fox_q_norm, fox_k_norm, swa_q_norm, swa_k_norm, swa_sinks, w_branch_fox, w_branch_swa, w_out, ffn2_norm, ffn2_w_in, ffn2_w_out, loss_target, m_meta_tokens, m_rel_bias_table, m_ffn1_norm, m_ffn1_w_in, m_ffn1_w_out, m_mix_norm, m_w_in, m_forget_bias, m_fox_q_norm, m_fox_k_norm, m_swa_q_norm, m_swa_k_norm, m_swa_sinks, m_w_branch_fox, m_w_branch_swa, m_w_out, m_ffn2_norm, m_ffn2_w_in, m_ffn2_w_out, v_meta_tokens, v_rel_bias_table, v_ffn1_norm, v_ffn1_w_in, v_ffn1_w_out, v_mix_norm, v_w_in, v_forget_bias, v_fox_q_norm, v_fox_k_norm, v_swa_q_norm, v_swa_k_norm, v_swa_sinks, v_w_branch_fox, v_w_branch_swa, v_w_out, v_ffn2_norm, v_ffn2_w_in, v_ffn2_w_out):
    names = ["meta_tokens", "rel_bias_table", "ffn1_norm", "ffn1_w_in", "ffn1_w_out", "mix_norm", "w_in", "forget_bias",
             "fox_q_norm", "fox_k_norm", "swa_q_norm", "swa_k_norm", "swa_sinks", "w_branch_fox", "w_branch_swa", "w_out",
             "ffn2_norm", "ffn2_w_in", "ffn2_w_out"]
    w = dict(zip(names, [meta_tokens, rel_bias_table, ffn1_norm, ffn1_w_in, ffn1_w_out, mix_norm, w_in, forget_bias,
                         fox_q_norm, fox_k_norm, swa_q_norm, swa_k_norm, swa_sinks, w_branch_fox, w_branch_swa, w_out,
                         ffn2_norm, ffn2_w_in, ffn2_w_out]))
    m = dict(zip(names, [m_meta_tokens, m_rel_bias_table, m_ffn1_norm, m_ffn1_w_in, m_ffn1_w_out, m_mix_norm, m_w_in,
                         m_forget_bias, m_fox_q_norm, m_fox_k_norm, m_swa_q_norm, m_swa_k_norm, m_swa_sinks,
                         m_w_branch_fox, m_w_branch_swa, m_w_out, m_ffn2_norm, m_ffn2_w_in, m_ffn2_w_out]))
    v = dict(zip(names, [v_meta_tokens, v_rel_bias_table, v_ffn1_norm, v_ffn1_w_in, v_ffn1_w_out, v_mix_norm, v_w_in,
                         v_forget_bias, v_fox_q_norm, v_fox_k_norm, v_swa_q_norm, v_swa_k_norm, v_swa_sinks,
                         v_w_branch_fox, v_w_branch_swa, v_w_out, v_ffn2_norm, v_ffn2_w_in, v_ffn2_w_out]))
    xi, yi, ci = lax.axis_index("x"), lax.axis_index("y"), lax.axis_index("c")
    shard = 2 * xi + yi
    seq = x.shape[1]
    t = seq + BLK

    wflat = _pack_shard({nm: w[nm].astype(CDT) for nm, _, _ in SHARD_ITEMS})
    mflat = meta_tokens.reshape(META_ROWS, 128)
    wall0, mall = _gather_weights(wflat[0:1], mflat, "gather_weights")
    mall = lax.dynamic_update_slice(mall, mflat[None], (shard, 0, 0))
    meta_full = jnp.concatenate([mall[s].reshape(N_META, 256) for s in range(4)], axis=1)
    bias = _bias_fwd(rel_bias_table, "bias_fwd")

    def layer_weights(wall, l):
        wall = lax.dynamic_update_slice(wall, wflat[l:l + 1][None], (shard, 0, 0, 0))
        lw = _unpack_layer(wall[:, 0])
        lw["w_mix"] = _mix_cols(lw.pop("w_in"))
        for nm in ("ffn1_norm", "mix_norm", "ffn2_norm"):
            lw[nm] = w[nm][l].reshape(1, D)
        lw["gfq"] = jnp.tile(fox_q_norm[l], 8).reshape(1, 512)
        lw["gfk"] = jnp.tile(fox_k_norm[l], 8).reshape(1, 512)
        lw["gsq"] = jnp.tile(swa_q_norm[l], 8).reshape(1, 512)
        lw["gsk"] = jnp.tile(swa_k_norm[l], 2).reshape(1, 128)
        lw["fb"] = jnp.pad(forget_bias[l], (0, 120)).reshape(1, 128)
        lw["sinks"] = swa_sinks[l]
        lw["bias"] = bias
        return lw

    h = jnp.concatenate([jnp.zeros((PAD, D), F32), meta_full, x[0]], axis=0)
    lws = [layer_weights(wall0, 0)]
    h, sv0, wall1 = _layer_fwd(h, lws[0], 0, ("gather", wflat[1:2], _sds((4, 1, LAYER_ROWS, 128), CDT)))
    lws.append(layer_weights(_forward_halves(wall1, "forward_halves"), 1))
    h, sv1, _ = _layer_fwd(h, lws[1], 1)
    saved = [sv0, sv1]
    dh, dhb, lacc = _loss(h, loss_target[0], "loss")
    loss = lax.psum(lacc[0, 0], ("x", "y", "c"))

    half_idx = ci.reshape(1).astype(jnp.int32)
    shard_idx = shard.reshape(1).astype(jnp.int32)
    grads = [None, None]
    dh, dhb, grads[1] = _layer_bwd(dh, dhb, saved[1], lws[1], 1)
    grads[1]["w_in"] = _unmix_cols(grads[1].pop("w_mix"))
    gfl1 = _pack_grads(grads[1])
    ps1 = _pair_add(gfl1, _swap_only(gfl1, "swap_halves1"), half_idx, "pair_add1")
    dh, dhb, grads[0] = _layer_bwd(dh, dhb, saved[0], lws[0], 0, ("scatter", ps1, _sds((3, 1, HALF_ROWS, 128), CDT)))
    grads[0]["w_in"] = _unmix_cols(grads[0].pop("w_mix"))
    grad_x = dh[BLK:].reshape(1, seq, D)
    dtab = _bias_bwd(grads[0]["dbias"] + grads[1]["dbias"], "bias_bwd")

    small = [dh[PAD:BLK].reshape(128, 128), _rows128(dtab[:, :N_BUCKETS].T, 2)]
    for nm in ("ffn1_norm", "mix_norm", "ffn2_norm"):
        small.append(jnp.stack([grads[l][nm][0] for l in range(2)]).reshape(16, 128))
    small.append(_rows128(jnp.stack([grads[l]["dgn"][4, :8] for l in range(2)]), 1))
    for row in range(4):
        small.append(jnp.stack([grads[l]["dgn"][row, :HD] for l in range(2)]).reshape(1, 128))
    dsk = [grads[l]["dsk"][:, 0, :] for l in range(2)]
    small.append(_rows128(jnp.stack([jnp.stack([d[:, 0], d[:, HD]], axis=1).reshape(8) for d in dsk]), 1))
    gsm = jnp.concatenate(small, axis=0)
    gsm = jnp.pad(gsm, ((0, SMALL_ROWS - gsm.shape[0]), (0, 0)))

    gfl0 = _pack_grads(grads[0])
    got0, slots = _swap_halves(gfl0, gsm, "swap_halves0")
    ps0 = _pair_add(gfl0, got0, half_idx, "pair_add0")
    red_half = jnp.concatenate([_sum4(ps0, _scatter_shards(ps0, "scatter_shards"), shard_idx, "sum4_0"),
                                _sum4(ps1, grads[0]["rode"], shard_idx, "sum4_1")], axis=0)
    sib_half = _join_halves(red_half, "join_halves")
    south = ci == 0
    red = jnp.concatenate([jnp.where(south, red_half, sib_half), jnp.where(south, sib_half, red_half)], axis=1)
    gs = _sum8(slots, "sum8")
    big = _unpack_shard(red)

    g_out = dict(big)
    g_out["meta_tokens"] = lax.dynamic_slice(gs[0:128].reshape(N_META, D), (0, shard * 256), (N_META, 256))
    off = 128
    for nm, rows in SMALL_ITEMS:
        n = w[nm].size
        g_out[nm] = gs[off:off + rows].reshape(-1)[:n].reshape(w[nm].shape)
        off += rows

    delta, new_m, new_v = {}, {}, {}
    for nm, (r, c), _ in SHARD_ITEMS:
        two = lambda a: a.reshape(2 * r, c)
        d_, m_, v_ = _adamw(two(w[nm]), two(g_out[nm]), two(m[nm]), two(v[nm]), f"adamw_{nm}")
        delta[nm], new_m[nm], new_v[nm] = (a.reshape(2, r, c) for a in (d_, m_, v_))
    small_names = ["meta_tokens"] + [nm for nm, _ in SMALL_ITEMS]
    small_rows = [META_ROWS] + [rows for _, rows in SMALL_ITEMS]

    def pack_small(src):
        buf = jnp.concatenate([_rows128(src[nm], rows) for nm, rows in zip(small_names, small_rows)], axis=0)
        return jnp.pad(buf, ((0, SMALL_ADAM_ROWS - buf.shape[0]), (0, 0)))

    d_, m_, v_ = _adamw(pack_small(w), pack_small(g_out), pack_small(m), pack_small(v), "adamw_small")
    off = 0
    for nm, rows in zip(small_names, small_rows):
        n = w[nm].size
        for dst, src in ((delta, d_), (new_m, m_), (new_v, v_)):
            dst[nm] = src[off:off + rows].reshape(-1)[:n].reshape(w[nm].shape)
        off += rows

    return (loss, grad_x, *[g_out[n] for n in names], *[delta[n] for n in names],
            *[new_m[n] for n in names], *[new_v[n] for n in names])
```

```python
import math

import numpy as np
import jax
import jax.numpy as jnp
from jax import lax
from jax.experimental import pallas as pl
from jax.experimental.pallas import tpu as pltpu

D = 1024
F = 2816
FT = F // 2
HD = 64
NPAIR = 4
N_META = 16
BLK = 128
PAD = BLK - N_META
EPS = 1e-6
NEG = -1e30
N_BUCKETS = 32
GA, GB, QA, KA, VA, QB, KB, VB, FA, DP = 0, 1024, 2048, 2560, 3072, 3584, 4096, 4224, 4352, 4480
D_IN = 4360
CDT = jnp.bfloat16
F32 = jnp.float32
VMEM_LIMIT = 48 * 1024 * 1024
MESH_ID = pl.DeviceIdType.MESH

ADAM_LR, ADAM_B1, ADAM_B2, ADAM_EPS, ADAM_WD, ADAM_STEP = 0.001, 0.9, 0.999, 1e-08, 0.01, 10

SHARD_ITEMS = (
    ("ffn1_w_in", (1024, 1408), "col"),
    ("ffn1_w_out", (704, 1024), "row"),
    ("w_in", (1024, 1090), "col"),
    ("w_branch_fox", (512, 256), "col"),
    ("w_branch_swa", (512, 256), "col"),
    ("w_out", (256, 1024), "row"),
    ("ffn2_w_in", (1024, 1408), "col"),
    ("ffn2_w_out", (704, 1024), "row"),
)
LAYER_RAW = sum(r * c for _, (r, c), _ in SHARD_ITEMS) // 128
PACK_TILE = 5840
HALF_ROWS = 4 * PACK_TILE
LAYER_ROWS = 2 * HALF_ROWS
SMALL_ROWS = 192
META_ROWS = 32


def _row_tile(t):
    return 384 if t % 384 == 0 else 128


def _dot(a, b):
    return jnp.dot(a, b, preferred_element_type=F32)


def _dot_nt(a, b):
    return lax.dot_general(a, b, (((1,), (1,)), ((), ())), preferred_element_type=F32)


def _dot_hi(a, b):
    return jnp.dot(a, b, preferred_element_type=F32, precision=lax.Precision.HIGHEST)


def _sigmoid(x):
    return 1.0 / (1.0 + jnp.exp(-x))


def _iota(shape, dim):
    return lax.broadcasted_iota(jnp.int32, shape, dim)


def _params(sem=None):
    return pltpu.CompilerParams(dimension_semantics=sem, vmem_limit_bytes=VMEM_LIMIT)


def _sds(shape, dtype):
    return jax.ShapeDtypeStruct(shape, dtype)


def _rms_fwd(h, g, name):
    t = h.shape[0]
    tm = _row_tile(t)

    def body(h_ref, g_ref, a_ref, at_ref):
        x = h_ref[...]
        ms = jnp.mean(x * x, axis=-1, keepdims=True)
        a = x * lax.rsqrt(ms + EPS) * g_ref[...]
        a_ref[...] = a.astype(CDT)
        at_ref[...] = a.T.astype(CDT)

    return pl.pallas_call(
        body, name=name, grid=(t // tm,),
        in_specs=[pl.BlockSpec((tm, D), lambda i: (i, 0)), pl.BlockSpec((1, D), lambda i: (0, 0))],
        out_specs=[pl.BlockSpec((tm, D), lambda i: (i, 0)), pl.BlockSpec((D, tm), lambda i: (0, i))],
        out_shape=[_sds((t, D), CDT), _sds((D, t), CDT)],
        compiler_params=_params(("parallel",)),
    )(h, g)


def _rms_bwd(da, h, g, dres, name):
    t = h.shape[0]
    tm = _row_tile(t)

    def body(da_ref, h_ref, g_ref, dr_ref, dh_ref, dhb_ref, dg_ref):
        i = pl.program_id(0)
        x = h_ref[...]
        da_ = da_ref[...]
        r = lax.rsqrt(jnp.mean(x * x, axis=-1, keepdims=True) + EPS)
        xh = x * r
        day = da_ * g_ref[...]
        dx = r * (day - xh * jnp.mean(day * xh, axis=-1, keepdims=True))
        dh = dr_ref[...] + dx
        dh_ref[...] = dh
        dhb_ref[...] = dh.astype(CDT)

        @pl.when(i == 0)
        def _():
            dg_ref[...] = jnp.zeros(dg_ref.shape, F32)

        dg_ref[0:1, :] += jnp.sum(da_ * xh, axis=0, keepdims=True)

    row = pl.BlockSpec((tm, D), lambda i: (i, 0))
    return pl.pallas_call(
        body, name=name, grid=(t // tm,),
        in_specs=[row, row, pl.BlockSpec((1, D), lambda i: (0, 0)), row],
        out_specs=[row, row, pl.BlockSpec((8, D), lambda i: (0, 0))],
        out_shape=[_sds((t, D), F32), _sds((t, D), CDT), _sds((8, D), F32)],
        compiler_params=_params(("arbitrary",)),
    )(da, h, g, dres)


def _ffn_in(a, w_in, name):
    t = a.shape[0]
    tm = _row_tile(t)
    tn = FT
    nj = F // tn

    def body(a_ref, wg_ref, wu_ref, gu_ref, s_ref, st_ref):
        a_ = a_ref[...]
        g = _dot(a_, wg_ref[...])
        u = _dot(a_, wu_ref[...])
        s = g * _sigmoid(g) * u
        gu_ref[0] = g.astype(CDT)
        gu_ref[1] = u.astype(CDT)
        s_ref[...] = s.astype(CDT)
        st_ref[...] = s.T.astype(CDT)

    return pl.pallas_call(
        body, name=name, grid=(nj, t // tm),
        in_specs=[pl.BlockSpec((tm, D), lambda j, i: (i, 0)),
                  pl.BlockSpec((D, tn), lambda j, i: (0, j)),
                  pl.BlockSpec((D, tn), lambda j, i: (0, j + nj))],
        out_specs=[pl.BlockSpec((2, tm, tn), lambda j, i: (0, i, j)),
                   pl.BlockSpec((tm, tn), lambda j, i: (i, j)),
                   pl.BlockSpec((tn, tm), lambda j, i: (j, i))],
        out_shape=[_sds((2, t, F), CDT), _sds((t, F), CDT), _sds((F, t), CDT)],
        compiler_params=_params(("parallel", "parallel")),
    )(a, w_in, w_in)


def _mm_res(a, b, res, scale, name):
    t, k = a.shape
    n = b.shape[1]
    tm = _row_tile(t)
    tn = 512

    def body(a_ref, b_ref, r_ref, o_ref):
        o_ref[...] = r_ref[...] + scale * _dot(a_ref[...], b_ref[...])

    return pl.pallas_call(
        body, name=name, grid=(t // tm, n // tn),
        in_specs=[pl.BlockSpec((tm, k), lambda i, j: (i, 0)),
                  pl.BlockSpec((k, tn), lambda i, j: (0, j)),
                  pl.BlockSpec((tm, tn), lambda i, j: (i, j))],
        out_specs=pl.BlockSpec((tm, tn), lambda i, j: (i, j)),
        out_shape=_sds((t, n), F32),
        compiler_params=_params(("parallel", "parallel")),
    )(a, b, res)


def _mm(a, b, out_dtype, tm, tn, name, scale=1.0, b_resident=False):
    m, k = a.shape
    order = (lambda j, i: (i, j)) if b_resident else (lambda i, j: (i, j))
    if b.ndim == 3:
        nh = b.shape[2] // tn
        n = 2 * b.shape[2]
        b_spec = pl.BlockSpec((None, k, tn), lambda *g: (order(*g)[1] // nh, 0, order(*g)[1] % nh))
    else:
        n = b.shape[1]
        b_spec = pl.BlockSpec((k, tn), lambda *g: (0, order(*g)[1]))

    def body(a_ref, b_ref, o_ref):
        o_ref[...] = (scale * _dot(a_ref[...], b_ref[...])).astype(out_dtype)

    return pl.pallas_call(
        body, name=name, grid=(n // tn, m // tm) if b_resident else (m // tm, n // tn),
        in_specs=[pl.BlockSpec((tm, k), lambda *g: (order(*g)[0], 0)), b_spec],
        out_specs=pl.BlockSpec((tm, tn), lambda *g: order(*g)),
        out_shape=_sds((m, n), out_dtype),
        compiler_params=_params(("parallel", "parallel")),
    )(a, b)


def _mm_nt(a, b, name):
    m, n = a.shape
    k = b.shape[0]
    tm = _row_tile(m)
    tk = 512

    def body(a_ref, b_ref, o_ref):
        o_ref[...] = _dot_nt(a_ref[...], b_ref[...])

    return pl.pallas_call(
        body, name=name, grid=(m // tm, k // tk),
        in_specs=[pl.BlockSpec((tm, n), lambda i, j: (i, 0)), pl.BlockSpec((tk, n), lambda i, j: (j, 0))],
        out_specs=pl.BlockSpec((tm, tk), lambda i, j: (i, j)),
        out_shape=_sds((m, k), F32),
        compiler_params=_params(("parallel", "parallel")),
    )(a, b)


def _ffn_bwd_mid(dhb, w_out, gu, name):
    t = dhb.shape[0]
    tm = _row_tile(t)
    tn = FT

    def body(dh_ref, w_ref, gu_ref, o_ref):
        ds = 0.5 * _dot_nt(dh_ref[...], w_ref[...])
        g = gu_ref[0].astype(F32)
        u = gu_ref[1].astype(F32)
        sg = _sigmoid(g)
        o_ref[0] = (ds * u * (sg * (1.0 + g * (1.0 - sg)))).astype(CDT)
        o_ref[1] = (ds * (g * sg)).astype(CDT)

    return pl.pallas_call(
        body, name=name, grid=(F // tn, t // tm),
        in_specs=[pl.BlockSpec((tm, D), lambda j, i: (i, 0)),
                  pl.BlockSpec((tn, D), lambda j, i: (j, 0)),
                  pl.BlockSpec((2, tm, tn), lambda j, i: (0, i, j))],
        out_specs=pl.BlockSpec((2, tm, tn), lambda j, i: (0, i, j)),
        out_shape=_sds((2, t, F), CDT),
        compiler_params=_params(("parallel", "parallel")),
    )(dhb, w_out, gu)


def _ffn_bwd_in(dgu, w_in, name):
    t = dgu.shape[1]
    tm = _row_tile(t)
    tk = 512

    def body(dg_ref, wg_ref, wu_ref, o_ref):
        o_ref[...] = _dot_nt(dg_ref[0], wg_ref[...]) + _dot_nt(dg_ref[1], wu_ref[...])

    return pl.pallas_call(
        body, name=name, grid=(t // tm, D // tk),
        in_specs=[pl.BlockSpec((2, tm, F), lambda i, j: (0, i, 0)),
                  pl.BlockSpec((tk, F), lambda i, j: (j, 0)),
                  pl.BlockSpec((tk, F), lambda i, j: (j, 1))],
        out_specs=pl.BlockSpec((tm, tk), lambda i, j: (i, j)),
        out_shape=_sds((t, D), F32),
        compiler_params=_params(("parallel", "parallel")),
    )(dgu, w_in, w_in)


def _loss(h, target, name):
    t = h.shape[0]

    def body(h_ref, t_ref, dh_ref, dhb_ref, l_ref):
        i = pl.program_id(0)

        @pl.when(i == 0)
        def _():
            l_ref[...] = jnp.zeros(l_ref.shape, F32)
            dh_ref[...] = jnp.zeros(dh_ref.shape, F32)
            dhb_ref[...] = jnp.zeros(dhb_ref.shape, CDT)

        @pl.when(i > 0)
        def _():
            err = h_ref[...] - t_ref[...]
            l_ref[...] += (0.5 / D) * jnp.sum(err * err)
            d = err * (1.0 / D)
            dh_ref[...] = d
            dhb_ref[...] = d.astype(CDT)

    row = pl.BlockSpec((BLK, D), lambda i: (i, 0))
    return pl.pallas_call(
        body, name=name, grid=(t // BLK,),
        in_specs=[row, pl.BlockSpec((BLK, D), lambda i: (jnp.maximum(i - 1, 0), 0))],
        out_specs=[row, row, pl.BlockSpec((8, 128), lambda i: (0, 0))],
        out_shape=[_sds((t, D), F32), _sds((t, D), CDT), _sds((8, 128), F32)],
        compiler_params=_params(("arbitrary",)),
    )(h, target)


def _block_diag():
    return (_iota((128, 128), 0) // HD == _iota((128, 128), 1) // HD).astype(F32)


def _dup_halves(x, lo):
    sw = pltpu.roll(x, 64, 1)
    return jnp.where(lo, x, sw), jnp.where(lo, sw, x)


def _qknorm_fwd(proj, gfq, gfk, gsq, gsk, fb, name):
    t = proj.shape[0]
    tm = _row_tile(t)

    def body(qa, ka, va, qb, kb, vb, fa, gfq_r, gfk_r, gsq_r, gsk_r, fb_r,
             qf_o, kf_o, vf_o, qs_o, kse_o, vse_o, c_o, ct_o, carry):
        i = pl.program_id(0)
        bd = _block_diag()
        lane = _iota((1, 128), 1)
        lo = lane < HD

        def hnorm(x, g):
            ms = _dot_hi(x * x, bd) * (1.0 / HD)
            return x * lax.rsqrt(ms + EPS) * g

        for ch in range(4):
            sl = slice(128 * ch, 128 * (ch + 1))
            qf_o[:, sl] = (hnorm(qa[:, sl], gfq_r[:, sl]) * 0.125).astype(CDT)
            kf_o[:, sl] = hnorm(ka[:, sl], gfk_r[:, sl]).astype(CDT)
            qs_o[:, sl] = (hnorm(qb[:, sl], gsq_r[:, sl]) * 0.125).astype(CDT)
        vf_o[...] = va[...].astype(CDT)
        k0, k1 = _dup_halves(hnorm(kb[...], gsk_r[...]), lo)
        kse_o[0] = k0.astype(CDT)
        kse_o[1] = k1.astype(CDT)
        v0, v1 = _dup_halves(vb[...], lo)
        vse_o[0] = v0.astype(CDT)
        vse_o[1] = v1.astype(CDT)

        z = fa[...] + fb_r[...]
        lf = jnp.minimum(z, 0.0) - jnp.log(1.0 + jnp.exp(-jnp.abs(z)))
        lf = jnp.where(lane < 8, lf, 0.0)
        ltri = (_iota((tm, tm), 1) <= _iota((tm, tm), 0)).astype(F32)

        @pl.when(i == 0)
        def _():
            carry[...] = jnp.zeros(carry.shape, F32)

        c = _dot_hi(ltri, lf) + carry[0:1, :]
        carry[0:1, :] = c[tm - 1:tm, :]
        c_o[...] = c
        ct_o[...] = c.T[0:8, :]

    def col(width, off):
        return pl.BlockSpec((tm, width), lambda i: (i, off // width))

    def vec(width):
        return pl.BlockSpec((1, width), lambda i: (0, 0))

    return pl.pallas_call(
        body, name=name, grid=(t // tm,),
        in_specs=[col(512, QA), col(512, KA), col(512, VA), col(512, QB), col(128, KB), col(128, VB), col(128, FA),
                  vec(512), vec(512), vec(512), vec(128), vec(128)],
        out_specs=[pl.BlockSpec((tm, 512), lambda i: (i, 0))] * 4
        + [pl.BlockSpec((2, tm, 128), lambda i: (0, i, 0))] * 2
        + [pl.BlockSpec((tm, 128), lambda i: (i, 0)), pl.BlockSpec((8, tm), lambda i: (0, i))],
        out_shape=[_sds((t, 512), CDT)] * 4 + [_sds((2, t, 128), CDT)] * 2 + [_sds((t, 128), F32), _sds((8, t), F32)],
        scratch_shapes=[pltpu.VMEM((8, 128), F32)],
        compiler_params=_params(("arbitrary",)),
    )(proj, proj, proj, proj, proj, proj, proj, gfq, gfk, gsq, gsk, fb)


def _qknorm_bwd(proj, dqf, dkf, dvf, dqs, dkse, dvse, dcq, dck, dga, dgb, gfq, gfk, gsq, gsk, fb, name):
    t = proj.shape[0]
    tm = _row_tile(t)
    nt = t // tm

    def body(qa, ka, qb, kb, fa, dqf_r, dkf_r, dvf_r, dqs_r, dkse_r, dvse_r, dcq_r, dck_r, dga_r, dgb_r,
             gfq_r, gfk_r, gsq_r, gsk_r, fb_r, dp_o, dgn_o, carry, acc):
        i = pl.program_id(0)
        bd = _block_diag()
        lane = _iota((1, 128), 1)
        lo = lane < HD

        @pl.when(i == 0)
        def _():
            carry[...] = jnp.zeros(carry.shape, F32)
            acc[...] = jnp.zeros(acc.shape, F32)

        def hnorm_bwd(x, g, dy):
            r = lax.rsqrt(_dot_hi(x * x, bd) * (1.0 / HD) + EPS)
            xh = x * r
            day = dy * g
            dx = r * (day - xh * (_dot_hi(day * xh, bd) * (1.0 / HD)))
            return dx, jnp.sum(dy * xh, axis=0, keepdims=True)

        for ch in range(4):
            sl = slice(128 * ch, 128 * (ch + 1))
            dx, dg = hnorm_bwd(qa[:, sl], gfq_r[:, sl], dqf_r[:, sl] * 0.125)
            dp_o[:, QA + 128 * ch:QA + 128 * (ch + 1)] = dx.astype(CDT)
            acc[0:1, sl] += dg
            dx, dg = hnorm_bwd(ka[:, sl], gfk_r[:, sl], dkf_r[:, sl])
            dp_o[:, KA + 128 * ch:KA + 128 * (ch + 1)] = dx.astype(CDT)
            acc[1:2, sl] += dg
            dx, dg = hnorm_bwd(qb[:, sl], gsq_r[:, sl], dqs_r[:, sl] * 0.125)
            dp_o[:, QB + 128 * ch:QB + 128 * (ch + 1)] = dx.astype(CDT)
            acc[2:3, sl] += dg
        dp_o[:, VA:VA + 512] = dvf_r[...].astype(CDT)
        dp_o[:, GA:GA + D] = dga_r[...]
        dp_o[:, GB:GB + D] = dgb_r[...]

        def fold(x):
            e0 = x[0]
            e1 = x[1]
            return jnp.where(lo, e0 + pltpu.roll(e0, 64, 1), e1 + pltpu.roll(e1, 64, 1))

        dx, dg = hnorm_bwd(kb[...], gsk_r[...], fold(dkse_r))
        dp_o[:, KB:KB + 128] = dx.astype(CDT)
        acc[3:4, 0:128] += dg
        dp_o[:, VB:VB + 128] = fold(dvse_r).astype(CDT)

        rr = _iota((512, 128), 0)
        hh = _iota((512, 128), 1)
        sel = ((rr == (hh >> 1) * 128 + (hh & 1) * HD) & (hh < 8)).astype(F32)
        dcs = _dot_hi(dcq_r[...] - dck_r[...], sel)
        utri = (_iota((tm, tm), 1) >= _iota((tm, tm), 0)).astype(F32)
        dlf = _dot_hi(utri, dcs) + carry[0:1, :]
        carry[0:1, :] = dlf[0:1, :]
        z = fa[...] + fb_r[...]
        dfa = jnp.where(lane < 8, dlf * _sigmoid(-z), 0.0)
        dp_o[:, FA:FA + 128] = dfa.astype(CDT)
        acc[4:5, 0:128] += jnp.sum(dfa, axis=0, keepdims=True)

        @pl.when(i == nt - 1)
        def _():
            foldm = ((_iota((512, 128), 0) & (HD - 1)) == _iota((512, 128), 1)).astype(F32)
            dgn_o[...] = _dot_hi(acc[...], foldm)

    def col(width, off):
        return pl.BlockSpec((tm, width), lambda i: (nt - 1 - i, off // width))

    def rows(width):
        return pl.BlockSpec((tm, width), lambda i: (nt - 1 - i, 0))

    def vec(width):
        return pl.BlockSpec((1, width), lambda i: (0, 0))

    pair = pl.BlockSpec((2, tm, 128), lambda i: (0, nt - 1 - i, 0))
    return pl.pallas_call(
        body, name=name, grid=(nt,),
        in_specs=[col(512, QA), col(512, KA), col(512, QB), col(128, KB), col(128, FA),
                  rows(512), rows(512), rows(512), rows(512), pair, pair, rows(512), rows(512), rows(D), rows(D),
                  vec(512), vec(512), vec(512), vec(128), vec(128)],
        out_specs=[rows(DP), pl.BlockSpec((8, 128), lambda i: (0, 0))],
        out_shape=[_sds((t, DP), CDT), _sds((8, 128), F32)],
        scratch_shapes=[pltpu.VMEM((8, 128), F32), pltpu.VMEM((8, 512), F32)],
        compiler_params=_params(("arbitrary",)),
    )(proj, proj, proj, proj, proj, dqf, dkf, dvf, dqs, dkse, dvse, dcq, dck, dga, dgb, gfq, gfk, gsq, gsk, fb)


def _gate_fwd(ofox, oswa, wbf, wbs, proj, name):
    t = ofox.shape[0]
    tm = _row_tile(t)
    tn = 512

    def body(of_r, os_r, wf_r, ws_r, ga_r, gb_r, y_o, yt_o, pf_o, ps_o, oft_o, ost_o):
        j = pl.program_id(1)
        pf = _dot(of_r[...], wf_r[...])
        ps = _dot(os_r[...], ws_r[...])
        y = _sigmoid(ga_r[...]) * pf + _sigmoid(gb_r[...]) * ps
        y_o[...] = y.astype(CDT)
        yt_o[...] = y.T.astype(CDT)
        pf_o[...] = pf.astype(CDT)
        ps_o[...] = ps.astype(CDT)

        @pl.when(j == 0)
        def _():
            oft_o[...] = of_r[...].astype(F32).T.astype(CDT)
            ost_o[...] = os_r[...].astype(F32).T.astype(CDT)

    tile = pl.BlockSpec((tm, tn), lambda i, j: (i, j))
    return pl.pallas_call(
        body, name=name, grid=(t // tm, D // tn),
        in_specs=[pl.BlockSpec((tm, 512), lambda i, j: (i, 0)), pl.BlockSpec((tm, 512), lambda i, j: (i, 0)),
                  pl.BlockSpec((512, tn), lambda i, j: (0, j)), pl.BlockSpec((512, tn), lambda i, j: (0, j)),
                  pl.BlockSpec((tm, tn), lambda i, j: (i, GA // tn + j)),
                  pl.BlockSpec((tm, tn), lambda i, j: (i, GB // tn + j))],
        out_specs=[tile, pl.BlockSpec((tn, tm), lambda i, j: (j, i)), tile, tile,
                   pl.BlockSpec((512, tm), lambda i, j: (0, i)), pl.BlockSpec((512, tm), lambda i, j: (0, i))],
        out_shape=[_sds((t, D), CDT), _sds((D, t), CDT), _sds((t, D), CDT), _sds((t, D), CDT),
                   _sds((512, t), CDT), _sds((512, t), CDT)],
        compiler_params=_params(("parallel", "arbitrary")),
    )(ofox, oswa, wbf, wbs, proj, proj)


def _gate_bwd(dy, pf, ps, proj, name):
    t = dy.shape[0]
    tm = _row_tile(t)
    tn = 512

    def body(dy_r, pf_r, ps_r, ga_r, gb_r, dpf_o, dps_o, dga_o, dgb_o):
        dy_ = dy_r[...]
        sa = _sigmoid(ga_r[...])
        sb = _sigmoid(gb_r[...])
        dpf_o[...] = (dy_ * sa).astype(CDT)
        dps_o[...] = (dy_ * sb).astype(CDT)
        dga_o[...] = (dy_ * pf_r[...].astype(F32) * (sa * (1.0 - sa))).astype(CDT)
        dgb_o[...] = (dy_ * ps_r[...].astype(F32) * (sb * (1.0 - sb))).astype(CDT)

    tile = pl.BlockSpec((tm, tn), lambda i, j: (i, j))
    return pl.pallas_call(
        body, name=name, grid=(t // tm, D // tn),
        in_specs=[tile, tile, tile,
                  pl.BlockSpec((tm, tn), lambda i, j: (i, GA // tn + j)),
                  pl.BlockSpec((tm, tn), lambda i, j: (i, GB // tn + j))],
        out_specs=[tile] * 4,
        out_shape=[_sds((t, D), CDT)] * 4,
        compiler_params=_params(("parallel", "parallel")),
    )(dy, pf, ps, proj, proj)


def _tri_steps(n, by_key):
    if by_key:
        pairs = [(i, j) for j in range(n) for i in range(j, n)]
    else:
        pairs = [(i, j) for i in range(n) for j in range(i + 1)]
    return (np.array([p[0] for p in pairs], np.int32), np.array([p[1] for p in pairs], np.int32))


def _head_col(blk, lane, h):
    return jnp.sum(jnp.where(lane == h, blk, 0.0), axis=1, keepdims=True)


def _head_row(blk, sub, h):
    return jnp.sum(jnp.where(sub == h, blk, 0.0), axis=0, keepdims=True)


def _ride_specs(ride):
    if ride is None:
        return [], [], [], [], []
    kind, srcs, outs, layer = ride
    return list(srcs), [ANY] * len(srcs), list(outs), [ANY] * len(outs), _dma_sems(3 * len(srcs))


def _ride_start(ride, srcs, dsts, send_sems, recv_sems):
    for cp in _ici_copies(ride[0], srcs, dsts, send_sems, recv_sems, ride[3])[0]:
        cp.start()


def _ride_wait(ride, srcs, dsts, send_sems, recv_sems):
    sends, recvs = _ici_copies(ride[0], srcs, dsts, send_sems, recv_sems, ride[3])
    for cp in recvs:
        cp.wait_recv()
    for cp in sends:
        cp.wait_send()


def _fox_fwd(qf, kf, vf, c, ct, name, ride=None):
    t = qf.shape[0]
    ta = _row_tile(t)
    qi, kj = _tri_steps(t // ta, by_key=False)
    nsteps = len(qi)
    ride_in, ride_in_specs, ride_out, ride_out_specs, ride_sems = _ride_specs(ride)

    def body(qi_r, kj_r, q_r, k_r, v_r, c_r, ct_r, *rest):
        nr = len(ride_in)
        src_r, (o_o, lse_o), dst_o = rest[:nr], rest[nr:nr + 2], rest[nr + 2:2 * nr + 2]
        m_sc, l_sc, acc_sc, *sems = rest[2 * nr + 2:]
        p = pl.program_id(0)
        n = pl.program_id(1)
        i = qi_r[n]
        j = kj_r[n]
        lane = _iota((1, 128), 1)
        lo = lane < HD

        if ride is not None:
            @pl.when((p == 0) & (n == 0))
            def _():
                _ride_start(ride, src_r, dst_o, *sems)

        @pl.when(j == 0)
        def _():
            m_sc[...] = jnp.full(m_sc.shape, NEG, F32)
            l_sc[...] = jnp.zeros(l_sc.shape, F32)
            acc_sc[...] = jnp.zeros(acc_sc.shape, F32)

        def step(masked):
            q = q_r[...]
            k = k_r[...]
            v = v_r[...]
            if masked:
                rows = i * ta + _iota((ta, 1), 0)
                cols = j * ta + _iota((1, ta), 1)
                mask = (cols <= rows) & (cols >= PAD)
            sub = _iota((8, 1), 0)
            alphas, pvs = [], []
            for e in (0, 1):
                sel = lo if e == 0 else jnp.logical_not(lo)
                s = _dot_nt(jnp.where(sel, q, 0), k)
                s = s + _head_col(c_r[...], lane, 2 * p + e) - _head_row(ct_r[...], sub, 2 * p + e)
                if masked:
                    s = jnp.where(mask, s, NEG)
                m_prev = m_sc[e][:, 0:1]
                m_new = jnp.maximum(m_prev, jnp.max(s, axis=1, keepdims=True))
                alpha = jnp.exp(m_prev - m_new)
                pe = jnp.exp(s - m_new)
                l_new = alpha * l_sc[e][:, 0:1] + jnp.sum(pe, axis=1, keepdims=True)
                m_sc[e] = jnp.broadcast_to(m_new, (ta, 128))
                l_sc[e] = jnp.broadcast_to(l_new, (ta, 128))
                alphas.append(alpha)
                pvs.append(_dot(pe.astype(CDT), v))
            acc_sc[...] = acc_sc[...] * jnp.where(lo, alphas[0], alphas[1]) + jnp.where(lo, pvs[0], pvs[1])

        edge = (j == i) | (j == 0)

        @pl.when(edge)
        def _():
            step(True)

        @pl.when(jnp.logical_not(edge))
        def _():
            step(False)

        @pl.when(j == i)
        def _():
            l = jnp.where(lo, l_sc[0], l_sc[1])
            o_o[...] = (acc_sc[...] / l).astype(CDT)
            lse_o[...] = jnp.where(lo, m_sc[0], m_sc[1]) + jnp.log(l)

        if ride is not None:
            @pl.when((p == NPAIR - 1) & (n == nsteps - 1))
            def _():
                _ride_wait(ride, src_r, dst_o, *sems)

    qblk = pl.BlockSpec((ta, 128), lambda p, n, qi_r, kj_r: (qi_r[n], p))
    kblk = pl.BlockSpec((ta, 128), lambda p, n, qi_r, kj_r: (kj_r[n], p))
    grid_spec = pltpu.PrefetchScalarGridSpec(
        num_scalar_prefetch=2, grid=(NPAIR, nsteps),
        in_specs=[qblk, kblk, kblk,
                  pl.BlockSpec((ta, 128), lambda p, n, qi_r, kj_r: (qi_r[n], 0)),
                  pl.BlockSpec((8, ta), lambda p, n, qi_r, kj_r: (0, kj_r[n]))] + ride_in_specs,
        out_specs=[qblk, qblk] + ride_out_specs,
        scratch_shapes=[pltpu.VMEM((2, ta, 128), F32), pltpu.VMEM((2, ta, 128), F32), pltpu.VMEM((ta, 128), F32)]
        + ride_sems,
    )
    return pl.pallas_call(
        body, name=name, grid_spec=grid_spec,
        out_shape=[_sds((t, 512), CDT), _sds((t, 512), F32)] + ride_out,
        compiler_params=_params(("arbitrary", "arbitrary")),
    )(jnp.asarray(qi), jnp.asarray(kj), qf, kf, vf, c, ct, *ride_in)


def _fox_bwd(qf, kf, vf, c, ct, o, lse, do, name, ride=None):
    t = qf.shape[0]
    ta = _row_tile(t)
    nq = t // ta
    qi, kj = _tri_steps(nq, by_key=True)
    nsteps = len(qi)
    ride_in, ride_in_specs, ride_out, ride_out_specs, ride_sems = _ride_specs(ride)

    def body(qi_r, kj_r, q_r, k_r, v_r, c_r, ct_r, o_r, lse_r, do_r, *rest):
        nr = len(ride_in)
        src_r, (dq_o, dcq_o, dk_o, dv_o, dck_o), dst_o = rest[:nr], rest[nr:nr + 5], rest[nr + 5:2 * nr + 5]
        dk_sc, dv_sc, dck_sc, *sems = rest[2 * nr + 5:]
        p = pl.program_id(0)
        n = pl.program_id(1)
        i = qi_r[n]
        j = kj_r[n]
        lane = _iota((1, 128), 1)
        lo = lane < HD

        if ride is not None:
            @pl.when((p == 0) & (n == 0))
            def _():
                _ride_start(ride, src_r, dst_o, *sems)

        @pl.when(n == 0)
        def _():
            dq_o[...] = jnp.zeros(dq_o.shape, F32)
            dcq_o[...] = jnp.zeros(dcq_o.shape, F32)

        @pl.when(i == j)
        def _():
            dk_sc[...] = jnp.zeros(dk_sc.shape, F32)
            dv_sc[...] = jnp.zeros(dv_sc.shape, F32)
            dck_sc[...] = jnp.zeros(dck_sc.shape, F32)

        def step(masked):
            q = q_r[...]
            k = k_r[...]
            v = v_r[...]
            do_ = do_r[...]
            dd = do_ * o_r[...].astype(F32)
            lse = lse_r[...]
            if masked:
                rows = i * ta + _iota((ta, 1), 0)
                cols = j * ta + _iota((1, ta), 1)
                mask = (cols <= rows) & (cols >= PAD)
            sub = _iota((8, 1), 0)
            dq_add = jnp.zeros((ta, 128), F32)
            dk_add = jnp.zeros((ta, 128), F32)
            dv_add = jnp.zeros((ta, 128), F32)
            rsum, csum = [], []
            for e in (0, 1):
                sel = lo if e == 0 else jnp.logical_not(lo)
                qe = jnp.where(sel, q, 0)
                doe = jnp.where(sel, do_, 0.0).astype(CDT)
                s = _dot_nt(qe, k)
                s = s + _head_col(c_r[...], lane, 2 * p + e) - _head_row(ct_r[...], sub, 2 * p + e)
                if masked:
                    s = jnp.where(mask, s, NEG)
                pr = jnp.exp(s - lse[:, HD * e:HD * e + 1])
                dv_add = dv_add + _dot(pr.T.astype(CDT), doe)
                dp = _dot_nt(doe, v)
                delta = jnp.sum(jnp.where(sel, dd, 0.0), axis=1, keepdims=True)
                ds = pr * (dp - delta)
                dq_add = dq_add + _dot(ds.astype(CDT), jnp.where(sel, k, 0))
                dst = ds.T
                dk_add = dk_add + _dot(dst.astype(CDT), qe)
                rsum.append(jnp.sum(ds, axis=1, keepdims=True))
                csum.append(jnp.sum(dst, axis=1, keepdims=True))
            rs = pl.ds(pl.multiple_of(i * ta, ta), ta)
            dq_o[rs, :] += dq_add
            dcq_o[rs, :] += jnp.where(lo, rsum[0], rsum[1])
            dk_sc[...] += dk_add
            dv_sc[...] += dv_add
            dck_sc[...] += jnp.where(lo, csum[0], csum[1])

        edge = (j == i) | (j == 0)

        @pl.when(edge)
        def _():
            step(True)

        @pl.when(jnp.logical_not(edge))
        def _():
            step(False)

        @pl.when(i == nq - 1)
        def _():
            dk_o[...] = dk_sc[...]
            dv_o[...] = dv_sc[...]
            dck_o[...] = dck_sc[...]

        if ride is not None:
            @pl.when((p == NPAIR - 1) & (n == nsteps - 1))
            def _():
                _ride_wait(ride, src_r, dst_o, *sems)

    qblk = pl.BlockSpec((ta, 128), lambda p, n, qi_r, kj_r: (qi_r[n], p))
    kblk = pl.BlockSpec((ta, 128), lambda p, n, qi_r, kj_r: (kj_r[n], p))
    whole = pl.BlockSpec((t, 128), lambda p, n, qi_r, kj_r: (0, p))
    grid_spec = pltpu.PrefetchScalarGridSpec(
        num_scalar_prefetch=2, grid=(NPAIR, nsteps),
        in_specs=[qblk, kblk, kblk,
                  pl.BlockSpec((ta, 128), lambda p, n, qi_r, kj_r: (qi_r[n], 0)),
                  pl.BlockSpec((8, ta), lambda p, n, qi_r, kj_r: (0, kj_r[n])),
                  qblk, qblk, qblk] + ride_in_specs,
        out_specs=[whole, whole, kblk, kblk, kblk] + ride_out_specs,
        scratch_shapes=[pltpu.VMEM((ta, 128), F32)] * 3 + ride_sems,
    )
    return pl.pallas_call(
        body, name=name, grid_spec=grid_spec,
        out_shape=[_sds((t, 512), F32)] * 5 + ride_out,
        compiler_params=_params(("arbitrary", "arbitrary")),
    )(jnp.asarray(qi), jnp.asarray(kj), qf, kf, vf, c, ct, o, lse, do, *ride_in)


def _bucket_table():
    r = np.arange(BLK)[:, None]
    c = np.arange(3 * BLK)[None, :]
    d = np.where(c < BLK, r + BLK - c, r - (c - BLK))
    n = np.maximum(d, 0)
    max_exact = N_BUCKETS // 2
    nf = np.maximum(n, 1).astype(np.float32)
    large = max_exact + (np.log(nf / max_exact) / math.log(BLK / max_exact) * (N_BUCKETS - max_exact)).astype(np.int32)
    large = np.minimum(large, N_BUCKETS - 1)
    b = np.where(n < max_exact, n, large)
    return np.where(c < 2 * BLK, b, N_BUCKETS - 1).astype(np.int32)


def _bias_fwd(table, name):
    bucket = jnp.asarray(_bucket_table())

    def body(tab_r, b_r, o_o):
        h = pl.program_id(0)
        b = b_r[...]
        acc = jnp.zeros(b.shape, F32)
        for k in range(N_BUCKETS):
            acc = jnp.where(b == k, tab_r[k, h], acc)
        o_o[...] = acc

    return pl.pallas_call(
        body, name=name, grid=(8,),
        in_specs=[pl.BlockSpec(memory_space=pltpu.SMEM), pl.BlockSpec((BLK, 3 * BLK), lambda h: (0, 0))],
        out_specs=pl.BlockSpec((None, BLK, 3 * BLK), lambda h: (h, 0, 0)),
        out_shape=_sds((8, BLK, 3 * BLK), F32),
        compiler_params=_params(("parallel",)),
    )(table, bucket)


def _bias_bwd(dbias, name):
    bucket = jnp.asarray(_bucket_table())

    def body(d_r, b_r, o_o):
        h = pl.program_id(0)
        b = b_r[...]
        d = d_r[...]
        lane = _iota((1, 128), 1)
        row = jnp.zeros((1, 128), F32)
        for k in range(N_BUCKETS):
            row = jnp.where(lane == k, jnp.sum(jnp.where(b == k, d, 0.0)), row)
        o_o[pl.ds(h, 1), :] = row

    return pl.pallas_call(
        body, name=name, grid=(8,),
        in_specs=[pl.BlockSpec((None, BLK, 3 * BLK), lambda h: (h, 0, 0)), pl.BlockSpec((BLK, 3 * BLK), lambda h: (0, 0))],
        out_specs=pl.BlockSpec((8, 128), lambda h: (0, 0)),
        out_shape=_sds((8, 128), F32),
        compiler_params=_params(("arbitrary",)),
    )(dbias, bucket)


def _swa_valid(i):
    r = _iota((BLK, 1), 0)
    c = _iota((1, 3 * BLK), 1)
    prev = (c < BLK) & (c > r) & (i >= 1) & ((i - 1) * BLK + c >= PAD)
    cc = c - BLK
    cur = (c >= BLK) & (c < 2 * BLK) & (cc <= r) & (i * BLK + cc >= PAD)
    cm = c - 2 * BLK
    meta = (c >= 2 * BLK) & (cm >= PAD) & (i * BLK + r - cm >= BLK)
    return prev | cur | meta


def _swa_kv_specs():
    def at(f):
        return pl.BlockSpec((None, BLK, 128), lambda p, i: (p // 2, f(i), 0))
    return [at(lambda i: jnp.maximum(i - 1, 0)), at(lambda i: i), at(lambda i: 0)]


def _swa_fwd(qs, kse, vse, bias, sinks, name):
    t = qs.shape[0]

    def body(sink_r, q_r, kp_r, kc_r, km_r, vp_r, vc_r, vm_r, b_r, o_o, lse_o):
        p = pl.program_id(0)
        i = pl.program_id(1)
        lo = _iota((1, 128), 1) < HD
        q = q_r[...]
        k3 = jnp.concatenate([kp_r[...], kc_r[...], km_r[...]], axis=0)
        v3 = jnp.concatenate([vp_r[...], vc_r[...], vm_r[...]], axis=0)
        valid = _swa_valid(i)
        outs, lses = [], []
        for e in (0, 1):
            sel = lo if e == 0 else jnp.logical_not(lo)
            s = _dot_nt(jnp.where(sel, q, 0), k3) + b_r[e]
            s = jnp.where(valid, s, NEG)
            sink = sink_r[2 * p + e]
            mx = jnp.maximum(jnp.max(s, axis=1, keepdims=True), sink)
            pe = jnp.exp(s - mx)
            den = jnp.sum(pe, axis=1, keepdims=True) + jnp.exp(sink - mx)
            outs.append(_dot(pe.astype(CDT), v3) / den)
            lses.append(mx + jnp.log(den))
        o_o[...] = jnp.where(lo, outs[0], outs[1]).astype(CDT)
        lse_o[...] = jnp.where(lo, lses[0], lses[1])

    qblk = pl.BlockSpec((BLK, 128), lambda p, i: (i, p))
    return pl.pallas_call(
        body, name=name, grid=(NPAIR, t // BLK),
        in_specs=[pl.BlockSpec(memory_space=pltpu.SMEM), qblk] + _swa_kv_specs() + _swa_kv_specs()
        + [pl.BlockSpec((2, BLK, 3 * BLK), lambda p, i: (p, 0, 0))],
        out_specs=[qblk, qblk],
        out_shape=[_sds((t, 512), CDT), _sds((t, 512), F32)],
        compiler_params=_params(("parallel", "parallel")),
    )(sinks, qs, kse, kse, kse, vse, vse, vse, bias)


def _swa_bwd(qs, kse, vse, bias, sinks, o, lse, do, name):
    t = qs.shape[0]

    def body(sink_r, q_r, kp_r, kc_r, km_r, vp_r, vc_r, vm_r, b_r, o_r, lse_r, do_r,
             dq_o, dk_o, dv_o, db_o, dsk_o):
        p = pl.program_id(0)
        i = pl.program_id(1)
        lo = _iota((1, 128), 1) < HD

        @pl.when((i == 0) & (p % 2 == 0))
        def _():
            dk_o[...] = jnp.zeros(dk_o.shape, F32)
            dv_o[...] = jnp.zeros(dv_o.shape, F32)

        @pl.when(i == 0)
        def _():
            db_o[...] = jnp.zeros(db_o.shape, F32)
            dsk_o[...] = jnp.zeros(dsk_o.shape, F32)

        q = q_r[...]
        do_ = do_r[...]
        dd = do_ * o_r[...].astype(F32)
        lse = lse_r[...]
        k3 = jnp.concatenate([kp_r[...], kc_r[...], km_r[...]], axis=0)
        v3 = jnp.concatenate([vp_r[...], vc_r[...], vm_r[...]], axis=0)
        valid = _swa_valid(i)
        dq = jnp.zeros((BLK, 128), F32)
        dk3 = jnp.zeros((3 * BLK, 128), F32)
        dv3 = jnp.zeros((3 * BLK, 128), F32)
        dsink = []
        for e in (0, 1):
            sel = lo if e == 0 else jnp.logical_not(lo)
            qe = jnp.where(sel, q, 0)
            doe = jnp.where(sel, do_, 0.0).astype(CDT)
            lse_e = lse[:, HD * e:HD * e + 1]
            s = _dot_nt(qe, k3) + b_r[e]
            s = jnp.where(valid, s, NEG)
            pr = jnp.exp(s - lse_e)
            delta = jnp.sum(jnp.where(sel, dd, 0.0), axis=1, keepdims=True)
            ds = pr * (_dot_nt(doe, v3) - delta)
            db_o[e] += ds
            dsink.append(-jnp.sum(jnp.exp(sink_r[2 * p + e] - lse_e) * delta, axis=0, keepdims=True))
            dq = dq + _dot(ds.astype(CDT), jnp.where(sel, k3, 0))
            dk3 = dk3 + _dot(ds.T.astype(CDT), qe)
            dv3 = dv3 + _dot(pr.T.astype(CDT), doe)
        dq_o[...] = dq
        prev = pl.ds(pl.multiple_of(jnp.maximum(i - 1, 0) * BLK, BLK), BLK)
        cur = pl.ds(pl.multiple_of(i * BLK, BLK), BLK)
        dk_o[prev, :] += dk3[0:BLK]
        dk_o[cur, :] += dk3[BLK:2 * BLK]
        dk_o[0:BLK, :] += dk3[2 * BLK:]
        dv_o[prev, :] += dv3[0:BLK]
        dv_o[cur, :] += dv3[BLK:2 * BLK]
        dv_o[0:BLK, :] += dv3[2 * BLK:]
        dsk_o[0:1, :] += jnp.where(lo, dsink[0], dsink[1])

    qblk = pl.BlockSpec((BLK, 128), lambda p, i: (i, p))
    kvacc = pl.BlockSpec((None, t, 128), lambda p, i: (p // 2, 0, 0))
    bblk = pl.BlockSpec((2, BLK, 3 * BLK), lambda p, i: (p, 0, 0))
    return pl.pallas_call(
        body, name=name, grid=(NPAIR, t // BLK),
        in_specs=[pl.BlockSpec(memory_space=pltpu.SMEM), qblk] + _swa_kv_specs() + _swa_kv_specs()
        + [bblk, qblk, qblk, qblk],
        out_specs=[qblk, kvacc, kvacc, bblk, pl.BlockSpec((None, 8, 128), lambda p, i: (p, 0, 0))],
        out_shape=[_sds((t, 512), F32), _sds((2, t, 128), F32), _sds((2, t, 128), F32),
                   _sds((8, BLK, 3 * BLK), F32), _sds((NPAIR, 8, 128), F32)],
        compiler_params=_params(("arbitrary", "arbitrary")),
    )(sinks, qs, kse, kse, kse, vse, vse, vse, bias, o, lse, do)


def _adamw(w, g, m, v, name):
    r, c = w.shape
    tr = 128 if r % 128 == 0 else r

    def body(w_r, g_r, m_r, v_r, d_o, m_o, v_o):
        g_ = g_r[...]
        m_ = ADAM_B1 * m_r[...] + (1.0 - ADAM_B1) * g_
        v_ = ADAM_B2 * v_r[...] + (1.0 - ADAM_B2) * jnp.square(g_)
        m_hat = m_ / (1.0 - ADAM_B1 ** ADAM_STEP)
        v_hat = v_ / (1.0 - ADAM_B2 ** ADAM_STEP)
        d_o[...] = -ADAM_LR * (m_hat / (jnp.sqrt(v_hat) + ADAM_EPS) + ADAM_WD * w_r[...])
        m_o[...] = m_
        v_o[...] = v_

    blk = pl.BlockSpec((tr, c), lambda i: (i, 0))
    return pl.pallas_call(
        body, name=name, grid=(r // tr,),
        in_specs=[blk] * 4, out_specs=[blk] * 3, out_shape=[_sds((r, c), F32)] * 3,
        compiler_params=_params(("parallel",)),
    )(w, g, m, v)


def _pair_add(own, got, half_idx, name):
    tr = PACK_TILE
    nb = HALF_ROWS // tr
    nl = own.shape[1]

    def body(c_r, a_r, b_r, o_o):
        o_o[...] = (a_r[...].astype(F32) + b_r[...].astype(F32)).astype(CDT)

    grid_spec = pltpu.PrefetchScalarGridSpec(
        num_scalar_prefetch=1, grid=(4, nl, nb),
        in_specs=[pl.BlockSpec((None, None, tr, 128), lambda s, l, i, c_r: (s, l, c_r[0] * nb + i, 0)),
                  pl.BlockSpec((None, None, tr, 128), lambda s, l, i, c_r: (s, l, i, 0))],
        out_specs=pl.BlockSpec((None, None, tr, 128), lambda s, l, i, c_r: (s, l, i, 0)),
    )
    return pl.pallas_call(
        body, name=name, grid_spec=grid_spec, out_shape=_sds((4, nl, HALF_ROWS, 128), CDT),
        compiler_params=_params(("parallel", "parallel", "parallel")),
    )(half_idx, own, got)


def _sum4(ps, got, shard_idx, name):
    tr = PACK_TILE
    nl = ps.shape[1]

    def body(s_r, a_r, b_r, o_o):
        o_o[...] = ((a_r[...].astype(F32) + b_r[0].astype(F32)) + b_r[1].astype(F32)) + b_r[2].astype(F32)

    grid_spec = pltpu.PrefetchScalarGridSpec(
        num_scalar_prefetch=1, grid=(nl, HALF_ROWS // tr),
        in_specs=[pl.BlockSpec((None, None, tr, 128), lambda l, i, s_r: (s_r[0], l, i, 0)),
                  pl.BlockSpec((3, None, tr, 128), lambda l, i, s_r: (0, l, i, 0))],
        out_specs=pl.BlockSpec((None, tr, 128), lambda l, i, s_r: (l, i, 0)),
    )
    return pl.pallas_call(
        body, name=name, grid_spec=grid_spec, out_shape=_sds((nl, HALF_ROWS, 128), F32),
        compiler_params=_params(("parallel", "parallel")),
    )(shard_idx, ps, got)


def _sum8(slots, name):
    def body(a_r, o_o):
        acc = a_r[0]
        for k in range(1, 8):
            acc = acc + a_r[k]
        o_o[...] = acc

    return pl.pallas_call(
        body, name=name, out_shape=_sds((SMALL_ROWS, 128), F32),
        in_specs=[pl.BlockSpec(memory_space=pltpu.VMEM)], out_specs=pl.BlockSpec(memory_space=pltpu.VMEM),
        compiler_params=_params(),
    )(slots)


def _place():
    x, y, c = lax.axis_index("x"), lax.axis_index("y"), lax.axis_index("c")
    chips = [(1 - x, y), (x, 1 - y), (1 - x, 1 - y)]
    return x, y, c, chips


def _remote(src, dst, send_sems, recv_sems, k, to):
    return pltpu.make_async_remote_copy(src_ref=src, dst_ref=dst, send_sem=send_sems.at[k], recv_sem=recv_sems.at[k],
                                        device_id=to, device_id_type=MESH_ID)


ANY = pl.BlockSpec(memory_space=pl.ANY)


def _ici_copies(kind, src_r, dst_r, send_sems, recv_sems):
    x, y, c, chips = _place()
    s = 2 * x + y
    mine = pl.ds(pl.multiple_of(c * HALF_ROWS, 16), HALF_ROWS)
    sends, recvs = [], []
    for j, (cx, cy) in enumerate(chips):
        sj = 2 * cx + cy
        if kind == "gather":
            sends.append(_remote(src_r.at[:, mine], dst_r.at[s, :, mine], send_sems, recv_sems, j, (cx, cy, c)))
            recvs.append(_remote(src_r.at[:, mine], dst_r.at[sj, :, mine], send_sems, recv_sems, j, (cx, cy, c)))
        else:
            sends.append(_remote(src_r.at[sj], dst_r.at[j], send_sems, recv_sems, j, (cx, cy, c)))
            recvs.append(sends[-1])
    return sends, recvs


def _forward_halves(wall, name):
    def body(w_r, w_o, send_sems, recv_sems):
        x, y, c, chips = _place()
        mine = pl.ds(pl.multiple_of(c * HALF_ROWS, 16), HALF_ROWS)
        other = pl.ds(pl.multiple_of((1 - c) * HALF_ROWS, 16), HALF_ROWS)
        sent = []
        for j, (cx, cy) in enumerate(chips):
            sj = 2 * cx + cy
            sent.append(_remote(w_r.at[sj, :, mine], w_o.at[sj, :, mine], send_sems, recv_sems, j, (x, y, 1 - c)))
        for cp in sent:
            cp.start()
        for j, (cx, cy) in enumerate(chips):
            sj = 2 * cx + cy
            _remote(w_r.at[sj, :, other], w_o.at[sj, :, other], send_sems, recv_sems, j, (x, y, 1 - c)).wait_recv()
        for cp in sent:
            cp.wait_send()

    return pl.pallas_call(
        body, name=name, out_shape=_sds(wall.shape, wall.dtype),
        in_specs=[ANY], out_specs=ANY, input_output_aliases={0: 0},
        scratch_shapes=[pltpu.SemaphoreType.DMA((3,)), pltpu.SemaphoreType.DMA((3,))],
    )(wall)


def _gather_weights(wflat, mflat, name):
    nl = wflat.shape[0]

    def body(w_r, m_r, wall_o, mall_o, send_sems, recv_sems):
        x, y, c, chips = _place()
        s = 2 * x + y
        sib = (x, y, 1 - c)
        mine = pl.ds(pl.multiple_of(c * HALF_ROWS, 16), HALF_ROWS)
        other = pl.ds(pl.multiple_of((1 - c) * HALF_ROWS, 16), HALF_ROWS)
        sent = []
        for j, (cx, cy) in enumerate(chips):
            sent.append(_remote(w_r.at[:, mine], wall_o.at[s, :, mine], send_sems, recv_sems, j, (cx, cy, c)))
            sent.append(_remote(m_r, mall_o.at[s], send_sems, recv_sems, 6 + j, (cx, cy, c)))
        for cp in sent:
            cp.start()
        for j, (cx, cy) in enumerate(chips):
            sj = 2 * cx + cy
            _remote(w_r.at[:, mine], wall_o.at[sj, :, mine], send_sems, recv_sems, j, sib).wait_recv()
            fwd = _remote(wall_o.at[sj, :, mine], wall_o.at[sj, :, mine], send_sems, recv_sems, 3 + j, sib)
            fwd.start()
            sent.append(fwd)
        for j, (cx, cy) in enumerate(chips):
            sj = 2 * cx + cy
            _remote(w_r.at[:, other], wall_o.at[sj, :, other], send_sems, recv_sems, 3 + j, sib).wait_recv()
            _remote(m_r, mall_o.at[sj], send_sems, recv_sems, 6 + j, sib).wait_recv()
        for cp in sent:
            cp.wait_send()

    return pl.pallas_call(
        body, name=name,
        out_shape=[_sds((4, nl, LAYER_ROWS, 128), CDT), _sds((4, META_ROWS, 128), F32)],
        in_specs=[ANY, ANY], out_specs=[ANY, ANY],
        scratch_shapes=[pltpu.SemaphoreType.DMA((9,)), pltpu.SemaphoreType.DMA((9,))],
    )(wflat, mflat)


def _swap_only(gfl, name):
    def body(g_r, got_o, send_sem, recv_sem):
        x, y, c, _ = _place()
        theirs = pl.ds(pl.multiple_of((1 - c) * HALF_ROWS, 16), HALF_ROWS)
        cp = _remote(g_r.at[:, :, theirs, :], got_o, send_sem, recv_sem, 0, (x, y, 1 - c))
        cp.start()
        cp.wait()

    return pl.pallas_call(
        body, name=name, out_shape=_sds((4, gfl.shape[1], HALF_ROWS, 128), CDT),
        in_specs=[ANY], out_specs=ANY,
        scratch_shapes=[pltpu.SemaphoreType.DMA((1,)), pltpu.SemaphoreType.DMA((1,))],
    )(gfl)


def _swap_halves(gfl, gsm, name):
    def body(g_r, s_r, got_o, slots_o, send_sems, recv_sems, loc_sem):
        x, y, c, _ = _place()
        me = 4 * x + 2 * y + c
        theirs = pl.ds(pl.multiple_of((1 - c) * HALF_ROWS, 16), HALF_ROWS)
        loc = pltpu.make_async_copy(s_r, slots_o.at[me], loc_sem.at[0])
        loc.start()
        sent = [_remote(g_r.at[:, :, theirs, :], got_o, send_sems, recv_sems, 0, (x, y, 1 - c))]
        for k in range(1, 8):
            px, py, pc = x ^ (k >> 2), y ^ ((k >> 1) & 1), c ^ (k & 1)
            sent.append(_remote(s_r, slots_o.at[me], send_sems, recv_sems, k, (px, py, pc)))
        for cp in sent:
            cp.start()
        _remote(g_r.at[:, :, theirs, :], got_o, send_sems, recv_sems, 0, (x, y, 1 - c)).wait_recv()
        for k in range(1, 8):
            px, py, pc = x ^ (k >> 2), y ^ ((k >> 1) & 1), c ^ (k & 1)
            _remote(s_r, slots_o.at[4 * px + 2 * py + pc], send_sems, recv_sems, k, (px, py, pc)).wait_recv()
        for cp in sent:
            cp.wait_send()
        loc.wait()

    return pl.pallas_call(
        body, name=name,
        out_shape=[_sds((4, gfl.shape[1], HALF_ROWS, 128), CDT), _sds((8, SMALL_ROWS, 128), F32)],
        in_specs=[ANY, ANY], out_specs=[ANY, ANY],
        scratch_shapes=[pltpu.SemaphoreType.DMA((8,)), pltpu.SemaphoreType.DMA((8,)), pltpu.SemaphoreType.DMA((1,))],
    )(gfl, gsm)


def _scatter_shards(ps, name):
    def body(p_r, got_o, send_sems, recv_sems):
        x, y, c, chips = _place()
        sent = [_remote(p_r.at[2 * cx + cy], got_o.at[j], send_sems, recv_sems, j, (cx, cy, c))
                for j, (cx, cy) in enumerate(chips)]
        for cp in sent:
            cp.start()
        for j in range(3):
            _remote(p_r.at[0], got_o.at[j], send_sems, recv_sems, j, (x, y, c)).wait_recv()
        for cp in sent:
            cp.wait_send()

    return pl.pallas_call(
        body, name=name, out_shape=_sds((3, ps.shape[1], HALF_ROWS, 128), CDT),
        in_specs=[ANY], out_specs=ANY,
        scratch_shapes=[pltpu.SemaphoreType.DMA((3,)), pltpu.SemaphoreType.DMA((3,))],
    )(ps)


def _join_halves(red_half, name):
    def body(h_r, got_o, send_sem, recv_sem):
        x, y, c, _ = _place()
        out = _remote(h_r, got_o, send_sem, recv_sem, 0, (x, y, 1 - c))
        out.start()
        out.wait()

    return pl.pallas_call(
        body, name=name, out_shape=_sds(red_half.shape, F32),
        in_specs=[ANY], out_specs=ANY,
        scratch_shapes=[pltpu.SemaphoreType.DMA((1,)), pltpu.SemaphoreType.DMA((1,))],
    )(red_half)


def _pack_shard(local):
    pad = [jnp.zeros((LAYER_ROWS - LAYER_RAW, 128), local[SHARD_ITEMS[0][0]].dtype)]
    return jnp.stack([jnp.concatenate([local[nm][l].reshape(-1, 128) for nm, _, _ in SHARD_ITEMS] + pad)
                      for l in range(2)])


def _unpack_layer(wall):
    ws = {}
    off = 0
    for nm, (r, c), kind in SHARD_ITEMS:
        n = r * c // 128
        piece = wall[:, off:off + n, :].reshape(4, r, c)
        off += n
        if kind == "row":
            ws[nm] = piece.reshape(4 * r, c)
        else:
            ws[nm] = jnp.concatenate([piece[s] for s in range(4)], axis=1)
    return ws


def _mix_cols(w):
    return jnp.concatenate([w[:, 2312:4360], w[:, 0:1536], w[:, 1544:2312], w[:, 1536:1544],
                            jnp.zeros((w.shape[0], DP - D_IN), w.dtype)], axis=1)


def _unmix_cols(w):
    return jnp.concatenate([w[:, QA:QA + 1536], w[:, FA:FA + 8], w[:, QB:QB + 768], w[:, GA:GA + 2048]], axis=1)


def _pack_grads(grads):
    pad = [jnp.zeros((LAYER_ROWS - LAYER_RAW, 128), CDT)]
    shards = []
    for s in range(4):
        parts = []
        for nm, (r, c), kind in SHARD_ITEMS:
            g = grads[nm]
            parts.append((g[s * r:(s + 1) * r] if kind == "row" else g[:, s * c:(s + 1) * c]).reshape(-1, 128))
        shards.append(jnp.concatenate(parts + pad)[None])
    return jnp.stack(shards)


def _unpack_shard(red):
    out = {}
    off = 0
    for nm, (r, c), _ in SHARD_ITEMS:
        n = r * c // 128
        out[nm] = red[:, off:off + n, :].reshape(2, r, c)
        off += n
    return out


def _rows128(a, rows):
    flat = a.reshape(-1)
    return jnp.pad(flat, (0, rows * 128 - flat.shape[0])).reshape(rows, 128)


GRAD_FORM = {"ffn1_w_in": "col", "ffn2_w_in": "col", "w_branch_fox": "col", "w_branch_swa": "col",
             "ffn1_w_out": "3d", "ffn2_w_out": "3d", "w_out": "3d", "w_in": "3d"}
SUM_TILE = {1024: 128, 704: 176, 512: 128, 256: 128}
NT = len(SHARD_ITEMS)


def _half_rows(c, r):
    return pl.ds(pl.multiple_of(c * (r // 2), 16), r // 2)


def _ici_copies(kind, srcs, dsts, send_sems, recv_sems, layer):
    x, y, c, chips = _place()
    s = 2 * x + y
    sends, recvs = [], []
    for t, ((nm, (r, cc), _), src, dst) in enumerate(zip(SHARD_ITEMS, srcs, dsts)):
        for j, (cx, cy) in enumerate(chips):
            sj = 2 * cx + cy
            k = 3 * t + j
            to = (cx, cy, c)
            if kind == "gather":
                hs = _half_rows(c, r)
                sends.append(_remote(src.at[layer, hs], dst.at[s, hs], send_sems, recv_sems, k, to))
                recvs.append(_remote(src.at[layer, hs], dst.at[sj, hs], send_sems, recv_sems, k, to))
            else:
                if GRAD_FORM[nm] == "col":
                    piece = src.at[:, pl.ds(pl.multiple_of(sj * cc, 128), cc)]
                else:
                    piece = src.at[sj]
                sends.append(_remote(piece, dst.at[j], send_sems, recv_sems, k, to))
                recvs.append(sends[-1])
    return sends, recvs


def _slab_shapes():
    return [_sds((4, r, c), CDT) for _, (r, c), _ in SHARD_ITEMS]


def _dma_sems(n):
    return [pltpu.SemaphoreType.DMA((n,)), pltpu.SemaphoreType.DMA((n,))]


def _forward_sends(dsts, send_sems, recv_sems):
    x, y, c, chips = _place()
    sends, recvs = [], []
    for t, ((nm, (r, cc), _), dst) in enumerate(zip(SHARD_ITEMS, dsts)):
        for j, (cx, cy) in enumerate(chips):
            sj = 2 * cx + cy
            hs, ho = _half_rows(c, r), _half_rows(1 - c, r)
            sends.append(_remote(dst.at[sj, hs], dst.at[sj, hs], send_sems, recv_sems, 3 * t + j, (x, y, 1 - c)))
            recvs.append(_remote(dst.at[sj, ho], dst.at[sj, ho], send_sems, recv_sems, 3 * t + j, (x, y, 1 - c)))
    return sends, recvs


def _gather_layer(wb, mflat, layer, name):
    def body(*refs):
        srcs, m_r, dsts, mall_o = refs[:NT], refs[NT], refs[NT + 1:2 * NT + 1], refs[2 * NT + 1]
        send_sems, recv_sems, fsend, frecv, msend, mrecv = refs[2 * NT + 2:]
        x, y, c, chips = _place()
        s = 2 * x + y
        sends, recvs = _ici_copies("gather", srcs, dsts, send_sems, recv_sems, layer)
        metas = [_remote(m_r, mall_o.at[s], msend, mrecv, j, (cx, cy, c)) for j, (cx, cy) in enumerate(chips)]
        for cp in sends + metas:
            cp.start()
        fwds, frecvs = _forward_sends(dsts, fsend, frecv)
        for got, fwd in zip(recvs, fwds):
            got.wait_recv()
            fwd.start()
        for got in frecvs:
            got.wait_recv()
        for j, (cx, cy) in enumerate(chips):
            _remote(m_r, mall_o.at[2 * cx + cy], msend, mrecv, j, (cx, cy, c)).wait_recv()
        for cp in sends + metas + fwds:
            cp.wait_send()

    return pl.pallas_call(
        body, name=name, out_shape=_slab_shapes() + [_sds((4, META_ROWS, 128), F32)],
        in_specs=[ANY] * (NT + 1), out_specs=[ANY] * (NT + 1),
        scratch_shapes=_dma_sems(3 * NT) + _dma_sems(3 * NT) + _dma_sems(3),
    )(*wb, mflat)


def _forward_layer(slabs, name):
    def body(*refs):
        ins, outs, send_sems, recv_sems = refs[:NT], refs[NT:2 * NT], refs[2 * NT], refs[2 * NT + 1]
        sends, recvs = _forward_sends(outs, send_sems, recv_sems)
        for cp in sends:
            cp.start()
        for cp in recvs:
            cp.wait_recv()
        for cp in sends:
            cp.wait_send()

    return pl.pallas_call(
        body, name=name, out_shape=_slab_shapes(), in_specs=[ANY] * NT, out_specs=[ANY] * NT,
        input_output_aliases={t: t for t in range(NT)}, scratch_shapes=_dma_sems(3 * NT),
    )(*slabs)


def _half_shape(nm, r, c):
    return (r // 2, 4 * c) if GRAD_FORM[nm] == "col" else (4, r // 2, c)


def _swap_layer(gs, gsm, name):
    small = gsm is not None

    def body(*refs):
        g_rs = refs[:NT]
        pos = NT
        if small:
            s_r = refs[pos]
            pos += 1
        got_os = refs[pos:pos + NT]
        pos += NT
        if small:
            slots_o = refs[pos]
            pos += 1
        send_sems, recv_sems = refs[pos], refs[pos + 1]
        x, y, c, _ = _place()
        sib = (x, y, 1 - c)
        sent = []
        for t, ((nm, (r, cc), _), g_r, got_o) in enumerate(zip(SHARD_ITEMS, g_rs, got_os)):
            ho = _half_rows(1 - c, r)
            src = g_r.at[ho, :] if GRAD_FORM[nm] == "col" else g_r.at[:, ho, :]
            sent.append(_remote(src, got_o, send_sems, recv_sems, t, sib))
        if small:
            ssend, srecv, loc_sem = refs[pos + 2], refs[pos + 3], refs[pos + 4]
            me = 4 * x + 2 * y + c
            loc = pltpu.make_async_copy(s_r, slots_o.at[me], loc_sem.at[0])
            loc.start()
            peers = [(x ^ (k >> 2), y ^ ((k >> 1) & 1), c ^ (k & 1)) for k in range(1, 8)]
            for k, peer in enumerate(peers):
                sent.append(_remote(s_r, slots_o.at[me], ssend, srecv, k, peer))
        for cp in sent:
            cp.start()
        for cp in sent[:NT]:
            cp.wait_recv()
        if small:
            for k, (px, py, pc) in enumerate(peers):
                _remote(s_r, slots_o.at[4 * px + 2 * py + pc], ssend, srecv, k, (px, py, pc)).wait_recv()
        for cp in sent:
            cp.wait_send()
        if small:
            loc.wait()

    outs = [_sds(_half_shape(nm, r, c), CDT) for nm, (r, c), _ in SHARD_ITEMS]
    ops = list(gs)
    sems = _dma_sems(NT)
    if small:
        outs.append(_sds((8, SMALL_ROWS, 128), F32))
        ops.append(gsm)
        sems = sems + _dma_sems(7) + [pltpu.SemaphoreType.DMA((1,))]
    res = pl.pallas_call(
        body, name=name, out_shape=outs, in_specs=[ANY] * len(ops), out_specs=[ANY] * len(outs), scratch_shapes=sems,
    )(*ops)
    return (res[:NT], res[NT]) if small else (res, None)


def _pair_add_t(own, got, half_idx, nm, r, name):
    tr = SUM_TILE[r]
    nb = (r // 2) // tr
    if GRAD_FORM[nm] == "col":
        blk = (tr, own.shape[1])
        own_spec = pl.BlockSpec(blk, lambda i, c_r: (c_r[0] * nb + i, 0))
        half_spec = pl.BlockSpec(blk, lambda i, c_r: (i, 0))
    else:
        blk = (4, tr, own.shape[2])
        own_spec = pl.BlockSpec(blk, lambda i, c_r: (0, c_r[0] * nb + i, 0))
        half_spec = pl.BlockSpec(blk, lambda i, c_r: (0, i, 0))

    def body(c_r, a_r, b_r, o_o):
        o_o[...] = (a_r[...].astype(F32) + b_r[...].astype(F32)).astype(CDT)

    grid_spec = pltpu.PrefetchScalarGridSpec(num_scalar_prefetch=1, grid=(nb,), in_specs=[own_spec, half_spec],
                                             out_specs=half_spec)
    return pl.pallas_call(body, name=name, grid_spec=grid_spec, out_shape=_sds(got.shape, CDT),
                          compiler_params=_params(("parallel",)))(half_idx, own, got)


def _sum4_t(ps, got3, buf, idx, layer, nm, r, name):
    tr = SUM_TILE[r]
    nb = (r // 2) // tr
    c = got3.shape[2]
    if GRAD_FORM[nm] == "col":
        ps_spec = pl.BlockSpec((tr, c), lambda i, x_r: (i, x_r[0]))
    else:
        ps_spec = pl.BlockSpec((None, tr, c), lambda i, x_r: (x_r[0], i, 0))

    def body(x_r, a_r, b_r, buf_r, o_o):
        o_o[...] = ((a_r[...].astype(F32) + b_r[0].astype(F32)) + b_r[1].astype(F32)) + b_r[2].astype(F32)

    grid_spec = pltpu.PrefetchScalarGridSpec(
        num_scalar_prefetch=1, grid=(nb,),
        in_specs=[ps_spec, pl.BlockSpec((3, tr, c), lambda i, x_r: (0, i, 0)), ANY],
        out_specs=pl.BlockSpec((None, tr, c), lambda i, x_r: (layer, x_r[1] * nb + i, 0)),
    )
    return pl.pallas_call(body, name=name, grid_spec=grid_spec, out_shape=_sds(buf.shape, F32),
                          input_output_aliases={3: 0}, compiler_params=_params(("parallel",)))(idx, ps, got3, buf)


def _scatter_layer(ps, name):
    def body(*refs):
        srcs, dsts, send_sems, recv_sems = refs[:NT], refs[NT:2 * NT], refs[2 * NT], refs[2 * NT + 1]
        sends, recvs = _ici_copies("scatter", srcs, dsts, send_sems, recv_sems, None)
        for cp in sends:
            cp.start()
        for cp in recvs:
            cp.wait_recv()
        for cp in sends:
            cp.wait_send()

    return pl.pallas_call(
        body, name=name, out_shape=_got3_shapes(), in_specs=[ANY] * NT, out_specs=[ANY] * NT,
        scratch_shapes=_dma_sems(3 * NT),
    )(*ps)


def _got3_shapes():
    return [_sds((3, r // 2, c), CDT) for _, (r, c), _ in SHARD_ITEMS]


def _join_layer(bufs, name):
    def body(*refs):
        ins, outs, send_sems, recv_sems = refs[:NT], refs[NT:2 * NT], refs[2 * NT], refs[2 * NT + 1]
        x, y, c, _ = _place()
        sent = []
        for t, ((nm, (r, cc), _), b_o) in enumerate(zip(SHARD_ITEMS, outs)):
            hs = _half_rows(c, r)
            sent.append(_remote(b_o.at[:, hs, :], b_o.at[:, hs, :], send_sems, recv_sems, t, (x, y, 1 - c)))
        for cp in sent:
            cp.start()
        for t, ((nm, (r, cc), _), b_o) in enumerate(zip(SHARD_ITEMS, outs)):
            ho = _half_rows(1 - c, r)
            _remote(b_o.at[:, ho, :], b_o.at[:, ho, :], send_sems, recv_sems, t, (x, y, 1 - c)).wait_recv()
        for cp in sent:
            cp.wait_send()

    return pl.pallas_call(
        body, name=name, out_shape=[_sds(b.shape, F32) for b in bufs], in_specs=[ANY] * NT, out_specs=[ANY] * NT,
        input_output_aliases={t: t for t in range(NT)}, scratch_shapes=_dma_sems(NT),
    )(*bufs)


def _adamw3(w, g, m, v, name):
    nl, r, c = w.shape
    tr = SUM_TILE.get(r, r)

    def body(w_r, g_r, m_r, v_r, d_o, m_o, v_o):
        g_ = g_r[...]
        m_ = ADAM_B1 * m_r[...] + (1.0 - ADAM_B1) * g_
        v_ = ADAM_B2 * v_r[...] + (1.0 - ADAM_B2) * jnp.square(g_)
        m_hat = m_ / (1.0 - ADAM_B1 ** ADAM_STEP)
        v_hat = v_ / (1.0 - ADAM_B2 ** ADAM_STEP)
        d_o[...] = -ADAM_LR * (m_hat / (jnp.sqrt(v_hat) + ADAM_EPS) + ADAM_WD * w_r[...])
        m_o[...] = m_
        v_o[...] = v_

    blk = pl.BlockSpec((None, tr, c), lambda l, i: (l, i, 0))
    return pl.pallas_call(
        body, name=name, grid=(nl, r // tr),
        in_specs=[blk] * 4, out_specs=[blk] * 3, out_shape=[_sds((nl, r, c), F32)] * 3,
        compiler_params=_params(("parallel", "parallel")),
    )(w, g, m, v)


def _full_weights(slabs, wb, layer, shard):
    ws = {}
    for (nm, (r, c), kind), slab in zip(SHARD_ITEMS, slabs):
        slab = lax.dynamic_update_slice(slab, wb[nm][layer][None], (shard, 0, 0))
        ws[nm] = slab.reshape(4 * r, c) if kind == "row" else jnp.concatenate([slab[s] for s in range(4)], axis=1)
    return ws


def _exchange_forms(g):
    out = []
    for nm, (r, c), _ in SHARD_ITEMS:
        a = g[nm]
        if nm == "w_in":
            a = a.reshape(D, 4, c).transpose(1, 0, 2)
        elif GRAD_FORM[nm] == "3d":
            a = a.reshape(4, r, c)
        out.append(a)
    return out


SMALL_ITEMS = (("rel_bias_table", 2), ("ffn1_norm", 16), ("mix_norm", 16), ("ffn2_norm", 16), ("forget_bias", 1),
               ("fox_q_norm", 1), ("fox_k_norm", 1), ("swa_q_norm", 1), ("swa_k_norm", 1), ("swa_sinks", 1))
SMALL_ADAM_ROWS = 96


def _layer_fwd(h, lw, l, ride=None):
    sv = {"h0": h}
    a, sv["a1t"] = _rms_fwd(h, lw["ffn1_norm"], f"rms_fwd_a{l}")
    sv["gu1"], s, sv["s1t"] = _ffn_in(a, lw["ffn1_w_in"], f"ffn_in_a{l}")
    h = _mm_res(s, lw["ffn1_w_out"], h, 0.5, f"ffn_out_a{l}")
    sv["h1"] = h
    a, sv["amt"] = _rms_fwd(h, lw["mix_norm"], f"rms_fwd_m{l}")
    proj = _mm(a, lw["w_mix"], F32, _row_tile(h.shape[0]), 640, f"proj{l}", b_resident=True)
    sv["proj"] = proj
    qf, kf, vf, qs, kse, vse, c, ct = _qknorm_fwd(proj, lw["gfq"], lw["gfk"], lw["gsq"], lw["gsk"], lw["fb"],
                                                   f"qknorm_fwd{l}")
    ofox, lse_f, *rode = _fox_fwd(qf, kf, vf, c, ct, f"fox_fwd{l}", ride)
    oswa, lse_s = _swa_fwd(qs, kse, vse, lw["bias"], lw["sinks"], f"swa_fwd{l}")
    sv.update(qf=qf, kf=kf, vf=vf, qs=qs, kse=kse, vse=vse, c=c, ct=ct, ofox=ofox, oswa=oswa, lse_f=lse_f, lse_s=lse_s)
    y, sv["yt"], sv["pf"], sv["ps"], sv["oft"], sv["ost"] = _gate_fwd(ofox, oswa, lw["w_branch_fox"], lw["w_branch_swa"],
                                                                     proj, f"gate_fwd{l}")
    h = _mm_res(y, lw["w_out"], h, 1.0, f"mix_out{l}")
    sv["h2"] = h
    a, sv["a2t"] = _rms_fwd(h, lw["ffn2_norm"], f"rms_fwd_b{l}")
    sv["gu2"], s, sv["s2t"] = _ffn_in(a, lw["ffn2_w_in"], f"ffn_in_b{l}")
    h = _mm_res(s, lw["ffn2_w_out"], h, 0.5, f"ffn_out_b{l}")
    return h, sv, rode


def _ffn_bwd(dh, dhb, h_in, at, gu, st, norm, w_in, w_out, tag):
    dgu = _ffn_bwd_mid(dhb, w_out, gu, f"ffn_bwd_mid_{tag}")
    d_w_out = _mm(st, dhb, CDT, 256, 512, f"dw_ffn_out_{tag}", scale=0.5)
    da = _ffn_bwd_in(dgu, w_in, f"ffn_bwd_in_{tag}")
    d_w_in = _mm(at, dgu, CDT, 512, 256, f"dw_ffn_in_{tag}")
    dh, dhb, dg = _rms_bwd(da, h_in, norm, dh, f"rms_bwd_{tag}")
    return dh, dhb, d_w_out, d_w_in, dg


def _layer_bwd(dh, dhb, sv, lw, l, ride=None):
    g = {}
    dh, dhb, g["ffn2_w_out"], g["ffn2_w_in"], g["ffn2_norm"] = _ffn_bwd(
        dh, dhb, sv["h2"], sv["a2t"], sv["gu2"], sv["s2t"], lw["ffn2_norm"], lw["ffn2_w_in"], lw["ffn2_w_out"], f"b{l}")
    dy = _mm_nt(dhb, lw["w_out"], f"d_y{l}")
    g["w_out"] = _mm(sv["yt"], dhb, CDT, 512, 512, f"dw_out{l}")
    dpf, dps, dga, dgb = _gate_bwd(dy, sv["pf"], sv["ps"], sv["proj"], f"gate_bwd{l}")
    do_f = _mm_nt(dpf, lw["w_branch_fox"], f"d_ofox{l}")
    do_s = _mm_nt(dps, lw["w_branch_swa"], f"d_oswa{l}")
    g["w_branch_fox"] = _mm(sv["oft"], dpf, CDT, 512, 512, f"dw_bfox{l}")
    g["w_branch_swa"] = _mm(sv["ost"], dps, CDT, 512, 512, f"dw_bswa{l}")
    dqf, dcq, dkf, dvf, dck, *rode = _fox_bwd(sv["qf"], sv["kf"], sv["vf"], sv["c"], sv["ct"], sv["ofox"], sv["lse_f"],
                                              do_f, f"fox_bwd{l}", ride)
    g["rode"] = rode
    dqs, dkse, dvse, dbias, dsk = _swa_bwd(sv["qs"], sv["kse"], sv["vse"], lw["bias"], lw["sinks"], sv["oswa"],
                                           sv["lse_s"], do_s, f"swa_bwd{l}")
    dproj, dgn = _qknorm_bwd(sv["proj"], dqf, dkf, dvf, dqs, dkse, dvse, dcq, dck, dga, dgb,
                             lw["gfq"], lw["gfk"], lw["gsq"], lw["gsk"], lw["fb"], f"qknorm_bwd{l}")
    dam = _mm_nt(dproj, lw["w_mix"], f"d_am{l}")
    g["w_mix"] = _mm(sv["amt"], dproj, CDT, 512, 640, f"dw_mix{l}")
    dh, dhb, g["mix_norm"] = _rms_bwd(dam, sv["h1"], lw["mix_norm"], dh, f"rms_bwd_m{l}")
    g["dbias"], g["dsk"], g["dgn"] = dbias, dsk, dgn
    dh, dhb, g["ffn1_w_out"], g["ffn1_w_in"], g["ffn1_norm"] = _ffn_bwd(
        dh, dhb, sv["h0"], sv["a1t"], sv["gu1"], sv["s1t"], lw["ffn1_norm"], lw["ffn1_w_in"], lw["ffn1_w_out"], f"a{l}")
    return dh, dhb, g


def kernel(x, meta_tokens, rel_bias_table, ffn1_norm, ffn1_w_in, ffn1_w_out, mix_norm, w_in, forget_bias, fox_q_norm, fox_k_norm, swa_q_norm, swa_k_norm, swa_sinks, w_branch_fox, w_branch_swa, w_out, ffn2_norm, ffn2_w_in, ffn2_w_out, loss_target, m_meta_tokens, m_rel_bias_table, m_ffn1_norm, m_ffn1_w_in, m_ffn1_w_out, m_mix_norm, m_w_in, m_forget_bias, m_fox_q_norm, m_fox_k_norm, m_swa_q_norm, m_swa_k_norm, m_swa_sinks, m_w_branch_fox, m_w_branch_swa, m_w_out, m_ffn2_norm, m_ffn2_w_in, m_ffn2_w_out, v_meta_tokens, v_rel_bias_table, v_ffn1_norm, v_ffn1_w_in, v_ffn1_w_out, v_mix_norm, v_w_in, v_forget_bias, v_fox_q_norm, v_fox_k_norm, v_swa_q_norm, v_swa_k_norm, v_swa_sinks, v_w_branch_fox, v_w_branch_swa, v_w_out, v_ffn2_norm, v_ffn2_w_in, v_ffn2_w_out):
    names = ["meta_tokens", "rel_bias_table", "ffn1_norm", "ffn1_w_in", "ffn1_w_out", "mix_norm", "w_in", "forget_bias",
             "fox_q_norm", "fox_k_norm", "swa_q_norm", "swa_k_norm", "swa_sinks", "w_branch_fox", "w_branch_swa", "w_out",
             "ffn2_norm", "ffn2_w_in", "ffn2_w_out"]
    w = dict(zip(names, [meta_tokens, rel_bias_table, ffn1_norm, ffn1_w_in, ffn1_w_out, mix_norm, w_in, forget_bias,
                         fox_q_norm, fox_k_norm, swa_q_norm, swa_k_norm, swa_sinks, w_branch_fox, w_branch_swa, w_out,
                         ffn2_norm, ffn2_w_in, ffn2_w_out]))
    m = dict(zip(names, [m_meta_tokens, m_rel_bias_table, m_ffn1_norm, m_ffn1_w_in, m_ffn1_w_out, m_mix_norm, m_w_in,
                         m_forget_bias, m_fox_q_norm, m_fox_k_norm, m_swa_q_norm, m_swa_k_norm, m_swa_sinks,
                         m_w_branch_fox, m_w_branch_swa, m_w_out, m_ffn2_norm, m_ffn2_w_in, m_ffn2_w_out]))
    v = dict(zip(names, [v_meta_tokens, v_rel_bias_table, v_ffn1_norm, v_ffn1_w_in, v_ffn1_w_out, v_mix_norm, v_w_in,
                         v_forget_bias, v_fox_q_norm, v_fox_k_norm, v_swa_q_norm, v_swa_k_norm, v_swa_sinks,
                         v_w_branch_fox, v_w_branch_swa, v_w_out, v_ffn2_norm, v_ffn2_w_in, v_ffn2_w_out]))
    xi, yi, ci = lax.axis_index("x"), lax.axis_index("y"), lax.axis_index("c")
    shard = 2 * xi + yi
    seq = x.shape[1]
    t = seq + BLK

    wb = {nm: w[nm].astype(CDT) for nm, _, _ in SHARD_ITEMS}
    wb_list = [wb[nm] for nm, _, _ in SHARD_ITEMS]
    mflat = meta_tokens.reshape(META_ROWS, 128)
    *slabs0, mall = _gather_layer(wb_list, mflat, 0, "gather_weights")
    mall = lax.dynamic_update_slice(mall, mflat[None], (shard, 0, 0))
    meta_full = jnp.concatenate([mall[s].reshape(N_META, 256) for s in range(4)], axis=1)
    bias = _bias_fwd(rel_bias_table, "bias_fwd")

    def layer_weights(slabs, l):
        lw = _full_weights(slabs, wb, l, shard)
        lw["w_mix"] = _mix_cols(lw.pop("w_in"))
        for nm in ("ffn1_norm", "mix_norm", "ffn2_norm"):
            lw[nm] = w[nm][l].reshape(1, D)
        lw["gfq"] = jnp.tile(fox_q_norm[l], 8).reshape(1, 512)
        lw["gfk"] = jnp.tile(fox_k_norm[l], 8).reshape(1, 512)
        lw["gsq"] = jnp.tile(swa_q_norm[l], 8).reshape(1, 512)
        lw["gsk"] = jnp.tile(swa_k_norm[l], 2).reshape(1, 128)
        lw["fb"] = jnp.pad(forget_bias[l], (0, 120)).reshape(1, 128)
        lw["sinks"] = swa_sinks[l]
        lw["bias"] = bias
        return lw

    h = jnp.concatenate([jnp.zeros((PAD, D), F32), meta_full, x[0]], axis=0)
    lws = [layer_weights(slabs0, 0)]
    h, sv0, slabs1 = _layer_fwd(h, lws[0], 0, ("gather", wb_list, _slab_shapes(), 1))
    lws.append(layer_weights(_forward_layer(slabs1, "forward_halves"), 1))
    h, sv1, _ = _layer_fwd(h, lws[1], 1)
    saved = [sv0, sv1]
    dh, dhb, lacc = _loss(h, loss_target[0], "loss")
    loss = lax.psum(lacc[0, 0], ("x", "y", "c"))

    half_idx = ci.reshape(1).astype(jnp.int32)
    place_idx = jnp.stack([shard, ci]).astype(jnp.int32)

    def pair_sums(g, gsm, tag):
        g["w_in"] = _unmix_cols(g.pop("w_mix"))
        forms = _exchange_forms(g)
        got, slots = _swap_layer(forms, gsm, f"swap_halves{tag}")
        return [_pair_add_t(a, b, half_idx, nm, r, f"pair_add{tag}_{nm}")
                for a, b, (nm, (r, c), _) in zip(forms, got, SHARD_ITEMS)], slots

    grads = [None, None]
    dh, dhb, grads[1] = _layer_bwd(dh, dhb, saved[1], lws[1], 1)
    ps1, _ = pair_sums(grads[1], None, 1)
    dh, dhb, grads[0] = _layer_bwd(dh, dhb, saved[0], lws[0], 0, ("scatter", ps1, _got3_shapes(), None))
    grad_x = dh[BLK:].reshape(1, seq, D)
    dtab = _bias_bwd(grads[0]["dbias"] + grads[1]["dbias"], "bias_bwd")

    small = [dh[PAD:BLK].reshape(128, 128), _rows128(dtab[:, :N_BUCKETS].T, 2)]
    for nm in ("ffn1_norm", "mix_norm", "ffn2_norm"):
        small.append(jnp.stack([grads[l][nm][0] for l in range(2)]).reshape(16, 128))
    small.append(_rows128(jnp.stack([grads[l]["dgn"][4, :8] for l in range(2)]), 1))
    for row in range(4):
        small.append(jnp.stack([grads[l]["dgn"][row, :HD] for l in range(2)]).reshape(1, 128))
    dsk = [grads[l]["dsk"][:, 0, :] for l in range(2)]
    small.append(_rows128(jnp.stack([jnp.stack([d[:, 0], d[:, HD]], axis=1).reshape(8) for d in dsk]), 1))
    gsm = jnp.concatenate(small, axis=0)
    gsm = jnp.pad(gsm, ((0, SMALL_ROWS - gsm.shape[0]), (0, 0)))

    ps0, slots = pair_sums(grads[0], gsm, 0)
    got3 = [_scatter_layer(ps0, "scatter_shards"), grads[0]["rode"]]
    bufs = []
    for t, (nm, (r, c), _) in enumerate(SHARD_ITEMS):
        buf = lax.empty((2, r, c), F32)
        for l, ps in ((1, ps1), (0, ps0)):
            buf = _sum4_t(ps[t], got3[l][t], buf, place_idx, l, nm, r, f"sum4_{l}_{nm}")
        bufs.append(buf)
    bufs = _join_layer(bufs, "join_halves")
    gs = _sum8(slots, "sum8")

    g_out = {nm: buf for (nm, _, _), buf in zip(SHARD_ITEMS, bufs)}
    g_out["meta_tokens"] = lax.dynamic_slice(gs[0:128].reshape(N_META, D), (0, shard * 256), (N_META, 256))
    off = 128
    for nm, rows in SMALL_ITEMS:
        n = w[nm].size
        g_out[nm] = gs[off:off + rows].reshape(-1)[:n].reshape(w[nm].shape)
        off += rows

    delta, new_m, new_v = {}, {}, {}
    for nm, _, _ in SHARD_ITEMS:
        delta[nm], new_m[nm], new_v[nm] = _adamw3(w[nm], g_out[nm], m[nm], v[nm], f"adamw_{nm}")
    small_names = ["meta_tokens"] + [nm for nm, _ in SMALL_ITEMS]
    small_rows = [META_ROWS] + [rows for _, rows in SMALL_ITEMS]

    def pack_small(src):
        buf = jnp.concatenate([_rows128(src[nm], rows) for nm, rows in zip(small_names, small_rows)], axis=0)
        return jnp.pad(buf, ((0, SMALL_ADAM_ROWS - buf.shape[0]), (0, 0)))

    d_, m_, v_ = (a[0] for a in _adamw3(pack_small(w)[None], pack_small(g_out)[None], pack_small(m)[None],
                                        pack_small(v)[None], "adamw_small"))
    off = 0
    for nm, rows in zip(small_names, small_rows):
        n = w[nm].size
        for dst, src in ((delta, d_), (new_m, m_), (new_v, v_)):
            dst[nm] = src[off:off + rows].reshape(-1)[:n].reshape(w[nm].shape)
        off += rows

    return (loss, grad_x, *[g_out[n] for n in names], *[delta[n] for n in names],
            *[new_m[n] for n in names], *[new_v[n] for n in names])
```

```python
import math

import numpy as np
import jax
import jax.numpy as jnp
from jax import lax
from jax.experimental import pallas as pl
from jax.experimental.pallas import tpu as pltpu

D = 1024
F = 2816
FT = F // 2
HD = 64
NPAIR = 4
N_META = 16
BLK = 128
PAD = BLK - N_META
EPS = 1e-6
NEG = -1e30
N_BUCKETS = 32
GA, GB, QA, KA, VA, QB, KB, VB, FA, DP = 0, 1024, 2048, 2560, 3072, 3584, 4096, 4224, 4352, 4480
D_IN = 4360
CDT = jnp.bfloat16
F32 = jnp.float32
VMEM_LIMIT = 48 * 1024 * 1024
MESH_ID = pl.DeviceIdType.MESH

ADAM_LR, ADAM_B1, ADAM_B2, ADAM_EPS, ADAM_WD, ADAM_STEP = 0.001, 0.9, 0.999, 1e-08, 0.01, 10

SHARD_ITEMS = (
    ("ffn1_w_in", (1024, 1408), "col"),
    ("ffn1_w_out", (704, 1024), "row"),
    ("w_in", (1024, 1090), "col"),
    ("w_branch_fox", (512, 256), "col"),
    ("w_branch_swa", (512, 256), "col"),
    ("w_out", (256, 1024), "row"),
    ("ffn2_w_in", (1024, 1408), "col"),
    ("ffn2_w_out", (704, 1024), "row"),
)
LAYER_RAW = sum(r * c for _, (r, c), _ in SHARD_ITEMS) // 128
PACK_TILE = 5840
HALF_ROWS = 4 * PACK_TILE
LAYER_ROWS = 2 * HALF_ROWS
SMALL_ROWS = 192
META_ROWS = 32


def _row_tile(t):
    return 384 if t % 384 == 0 else 128


def _dot(a, b):
    return jnp.dot(a, b, preferred_element_type=F32)


def _dot_nt(a, b):
    return lax.dot_general(a, b, (((1,), (1,)), ((), ())), preferred_element_type=F32)


def _dot_hi(a, b):
    return jnp.dot(a, b, preferred_element_type=F32, precision=lax.Precision.HIGHEST)


def _sigmoid(x):
    return 1.0 / (1.0 + jnp.exp(-x))


def _iota(shape, dim):
    return lax.broadcasted_iota(jnp.int32, shape, dim)


def _params(sem=None):
    return pltpu.CompilerParams(dimension_semantics=sem, vmem_limit_bytes=VMEM_LIMIT)


def _sds(shape, dtype):
    return jax.ShapeDtypeStruct(shape, dtype)


def _rms_fwd(h, g, name):
    t = h.shape[0]
    tm = _row_tile(t)

    def body(h_ref, g_ref, a_ref, at_ref):
        x = h_ref[...]
        ms = jnp.mean(x * x, axis=-1, keepdims=True)
        a = x * lax.rsqrt(ms + EPS) * g_ref[...]
        a_ref[...] = a.astype(CDT)
        at_ref[...] = a.T.astype(CDT)

    return pl.pallas_call(
        body, name=name, grid=(t // tm,),
        in_specs=[pl.BlockSpec((tm, D), lambda i: (i, 0)), pl.BlockSpec((1, D), lambda i: (0, 0))],
        out_specs=[pl.BlockSpec((tm, D), lambda i: (i, 0)), pl.BlockSpec((D, tm), lambda i: (0, i))],
        out_shape=[_sds((t, D), CDT), _sds((D, t), CDT)],
        compiler_params=_params(("parallel",)),
    )(h, g)


def _rms_bwd(da, h, g, dres, name):
    t = h.shape[0]
    tm = _row_tile(t)

    def body(da_ref, h_ref, g_ref, dr_ref, dh_ref, dhb_ref, dg_ref):
        i = pl.program_id(0)
        x = h_ref[...]
        da_ = da_ref[...]
        r = lax.rsqrt(jnp.mean(x * x, axis=-1, keepdims=True) + EPS)
        xh = x * r
        day = da_ * g_ref[...]
        dx = r * (day - xh * jnp.mean(day * xh, axis=-1, keepdims=True))
        dh = dr_ref[...] + dx
        dh_ref[...] = dh
        dhb_ref[...] = dh.astype(CDT)

        @pl.when(i == 0)
        def _():
            dg_ref[...] = jnp.zeros(dg_ref.shape, F32)

        dg_ref[0:1, :] += jnp.sum(da_ * xh, axis=0, keepdims=True)

    row = pl.BlockSpec((tm, D), lambda i: (i, 0))
    return pl.pallas_call(
        body, name=name, grid=(t // tm,),
        in_specs=[row, row, pl.BlockSpec((1, D), lambda i: (0, 0)), row],
        out_specs=[row, row, pl.BlockSpec((8, D), lambda i: (0, 0))],
        out_shape=[_sds((t, D), F32), _sds((t, D), CDT), _sds((8, D), F32)],
        compiler_params=_params(("arbitrary",)),
    )(da, h, g, dres)


def _ffn_in(a, w_in, name):
    t = a.shape[0]
    tm = _row_tile(t)
    tn = FT
    nj = F // tn

    def body(a_ref, wg_ref, wu_ref, gu_ref, s_ref, st_ref):
        a_ = a_ref[...]
        g = _dot(a_, wg_ref[...])
        u = _dot(a_, wu_ref[...])
        s = g * _sigmoid(g) * u
        gu_ref[0] = g.astype(CDT)
        gu_ref[1] = u.astype(CDT)
        s_ref[...] = s.astype(CDT)
        st_ref[...] = s.T.astype(CDT)

    return pl.pallas_call(
        body, name=name, grid=(nj, t // tm),
        in_specs=[pl.BlockSpec((tm, D), lambda j, i: (i, 0)),
                  pl.BlockSpec((D, tn), lambda j, i: (0, j)),
                  pl.BlockSpec((D, tn), lambda j, i: (0, j + nj))],
        out_specs=[pl.BlockSpec((2, tm, tn), lambda j, i: (0, i, j)),
                   pl.BlockSpec((tm, tn), lambda j, i: (i, j)),
                   pl.BlockSpec((tn, tm), lambda j, i: (j, i))],
        out_shape=[_sds((2, t, F), CDT), _sds((t, F), CDT), _sds((F, t), CDT)],
        compiler_params=_params(("parallel", "parallel")),
    )(a, w_in, w_in)


def _mm_res(a, b, res, scale, name):
    t, k = a.shape
    n = b.shape[1]
    tm = _row_tile(t)
    tn = 512

    def body(a_ref, b_ref, r_ref, o_ref):
        o_ref[...] = r_ref[...] + scale * _dot(a_ref[...], b_ref[...])

    return pl.pallas_call(
        body, name=name, grid=(t // tm, n // tn),
        in_specs=[pl.BlockSpec((tm, k), lambda i, j: (i, 0)),
                  pl.BlockSpec((k, tn), lambda i, j: (0, j)),
                  pl.BlockSpec((tm, tn), lambda i, j: (i, j))],
        out_specs=pl.BlockSpec((tm, tn), lambda i, j: (i, j)),
        out_shape=_sds((t, n), F32),
        compiler_params=_params(("parallel", "parallel")),
    )(a, b, res)


def _mm(a, b, out_dtype, tm, tn, name, scale=1.0, b_resident=False):
    m, k = a.shape
    order = (lambda j, i: (i, j)) if b_resident else (lambda i, j: (i, j))
    if b.ndim == 3:
        nh = b.shape[2] // tn
        n = 2 * b.shape[2]
        b_spec = pl.BlockSpec((None, k, tn), lambda *g: (order(*g)[1] // nh, 0, order(*g)[1] % nh))
    else:
        n = b.shape[1]
        b_spec = pl.BlockSpec((k, tn), lambda *g: (0, order(*g)[1]))

    def body(a_ref, b_ref, o_ref):
        o_ref[...] = (scale * _dot(a_ref[...], b_ref[...])).astype(out_dtype)

    return pl.pallas_call(
        body, name=name, grid=(n // tn, m // tm) if b_resident else (m // tm, n // tn),
        in_specs=[pl.BlockSpec((tm, k), lambda *g: (order(*g)[0], 0)), b_spec],
        out_specs=pl.BlockSpec((tm, tn), lambda *g: order(*g)),
        out_shape=_sds((m, n), out_dtype),
        compiler_params=_params(("parallel", "parallel")),
    )(a, b)


def _mm_nt(a, b, name, with_t=False):
    m, n = a.shape
    k = b.shape[0]
    tm = _row_tile(m)
    tk = 512

    def body(a_ref, b_ref, o_ref, *t_ref):
        r = _dot_nt(a_ref[...], b_ref[...])
        o_ref[...] = r
        if with_t:
            t_ref[0][...] = r.T.astype(CDT)

    out_specs = [pl.BlockSpec((tm, tk), lambda i, j: (i, j))]
    out_shape = [_sds((m, k), F32)]
    if with_t:
        out_specs.append(pl.BlockSpec((tk, tm), lambda i, j: (j, i)))
        out_shape.append(_sds((k, m), CDT))
    res = pl.pallas_call(
        body, name=name, grid=(m // tm, k // tk),
        in_specs=[pl.BlockSpec((tm, n), lambda i, j: (i, 0)), pl.BlockSpec((tk, n), lambda i, j: (j, 0))],
        out_specs=out_specs, out_shape=out_shape,
        compiler_params=_params(("parallel", "parallel")),
    )(a, b)
    return res if with_t else res[0]


def _ffn_bwd_mid(dhb, w_out, gu, name):
    t = dhb.shape[0]
    tm = _row_tile(t)
    tn = FT

    def body(dh_ref, w_ref, gu_ref, o_ref):
        ds = 0.5 * _dot_nt(dh_ref[...], w_ref[...])
        g = gu_ref[0].astype(F32)
        u = gu_ref[1].astype(F32)
        sg = _sigmoid(g)
        o_ref[0] = (ds * u * (sg * (1.0 + g * (1.0 - sg)))).astype(CDT)
        o_ref[1] = (ds * (g * sg)).astype(CDT)

    return pl.pallas_call(
        body, name=name, grid=(F // tn, t // tm),
        in_specs=[pl.BlockSpec((tm, D), lambda j, i: (i, 0)),
                  pl.BlockSpec((tn, D), lambda j, i: (j, 0)),
                  pl.BlockSpec((2, tm, tn), lambda j, i: (0, i, j))],
        out_specs=pl.BlockSpec((2, tm, tn), lambda j, i: (0, i, j)),
        out_shape=_sds((2, t, F), CDT),
        compiler_params=_params(("parallel", "parallel")),
    )(dhb, w_out, gu)


def _ffn_bwd_in(dgu, w_in, name):
    t = dgu.shape[1]
    tm = _row_tile(t)
    tk = 512

    def body(dg_ref, wg_ref, wu_ref, o_ref):
        o_ref[...] = _dot_nt(dg_ref[0], wg_ref[...]) + _dot_nt(dg_ref[1], wu_ref[...])

    return pl.pallas_call(
        body, name=name, grid=(t // tm, D // tk),
        in_specs=[pl.BlockSpec((2, tm, F), lambda i, j: (0, i, 0)),
                  pl.BlockSpec((tk, F), lambda i, j: (j, 0)),
                  pl.BlockSpec((tk, F), lambda i, j: (j, 1))],
        out_specs=pl.BlockSpec((tm, tk), lambda i, j: (i, j)),
        out_shape=_sds((t, D), F32),
        compiler_params=_params(("parallel", "parallel")),
    )(dgu, w_in, w_in)


def _loss(h, target, name):
    t = h.shape[0]

    def body(h_ref, t_ref, dh_ref, dhb_ref, l_ref):
        i = pl.program_id(0)

        @pl.when(i == 0)
        def _():
            l_ref[...] = jnp.zeros(l_ref.shape, F32)
            dh_ref[...] = jnp.zeros(dh_ref.shape, F32)
            dhb_ref[...] = jnp.zeros(dhb_ref.shape, CDT)

        @pl.when(i > 0)
        def _():
            err = h_ref[...] - t_ref[...]
            l_ref[...] += (0.5 / D) * jnp.sum(err * err)
            d = err * (1.0 / D)
            dh_ref[...] = d
            dhb_ref[...] = d.astype(CDT)

    row = pl.BlockSpec((BLK, D), lambda i: (i, 0))
    return pl.pallas_call(
        body, name=name, grid=(t // BLK,),
        in_specs=[row, pl.BlockSpec((BLK, D), lambda i: (jnp.maximum(i - 1, 0), 0))],
        out_specs=[row, row, pl.BlockSpec((8, 128), lambda i: (0, 0))],
        out_shape=[_sds((t, D), F32), _sds((t, D), CDT), _sds((8, 128), F32)],
        compiler_params=_params(("arbitrary",)),
    )(h, target)


def _block_diag():
    return (_iota((128, 128), 0) // HD == _iota((128, 128), 1) // HD).astype(F32)


def _dup_halves(x, lo):
    sw = pltpu.roll(x, 64, 1)
    return jnp.where(lo, x, sw), jnp.where(lo, sw, x)


def _qknorm_fwd(proj, gfq, gfk, gsq, gsk, fb, name):
    t = proj.shape[0]
    tm = _row_tile(t)

    def body(qa, ka, va, qb, kb, vb, fa, gfq_r, gfk_r, gsq_r, gsk_r, fb_r,
             qf_o, kf_o, vf_o, qs_o, kse_o, vse_o, c_o, ct_o, qft_o, carry):
        i = pl.program_id(0)
        bd = _block_diag()
        lane = _iota((1, 128), 1)
        lo = lane < HD

        def hnorm(x, g):
            ms = _dot_hi(x * x, bd) * (1.0 / HD)
            return x * lax.rsqrt(ms + EPS) * g

        for ch in range(4):
            sl = slice(128 * ch, 128 * (ch + 1))
            qn = hnorm(qa[:, sl], gfq_r[:, sl]) * 0.125
            qf_o[:, sl] = qn.astype(CDT)
            qft_o[sl, :] = qn.T.astype(CDT)
            kf_o[:, sl] = hnorm(ka[:, sl], gfk_r[:, sl]).astype(CDT)
            qs_o[:, sl] = (hnorm(qb[:, sl], gsq_r[:, sl]) * 0.125).astype(CDT)
        vf_o[...] = va[...].astype(CDT)
        k0, k1 = _dup_halves(hnorm(kb[...], gsk_r[...]), lo)
        kse_o[0] = k0.astype(CDT)
        kse_o[1] = k1.astype(CDT)
        v0, v1 = _dup_halves(vb[...], lo)
        vse_o[0] = v0.astype(CDT)
        vse_o[1] = v1.astype(CDT)

        z = fa[...] + fb_r[...]
        lf = jnp.minimum(z, 0.0) - jnp.log(1.0 + jnp.exp(-jnp.abs(z)))
        lf = jnp.where(lane < 8, lf, 0.0)
        ltri = (_iota((tm, tm), 1) <= _iota((tm, tm), 0)).astype(F32)

        @pl.when(i == 0)
        def _():
            carry[...] = jnp.zeros(carry.shape, F32)

        c = _dot_hi(ltri, lf) + carry[0:1, :]
        carry[0:1, :] = c[tm - 1:tm, :]
        c_o[...] = c
        ct_o[...] = c.T[0:8, :]

    def col(width, off):
        return pl.BlockSpec((tm, width), lambda i: (i, off // width))

    def vec(width):
        return pl.BlockSpec((1, width), lambda i: (0, 0))

    return pl.pallas_call(
        body, name=name, grid=(t // tm,),
        in_specs=[col(512, QA), col(512, KA), col(512, VA), col(512, QB), col(128, KB), col(128, VB), col(128, FA),
                  vec(512), vec(512), vec(512), vec(128), vec(128)],
        out_specs=[pl.BlockSpec((tm, 512), lambda i: (i, 0))] * 4
        + [pl.BlockSpec((2, tm, 128), lambda i: (0, i, 0))] * 2
        + [pl.BlockSpec((tm, 128), lambda i: (i, 0)), pl.BlockSpec((8, tm), lambda i: (0, i)),
           pl.BlockSpec((512, tm), lambda i: (0, i))],
        out_shape=[_sds((t, 512), CDT)] * 4 + [_sds((2, t, 128), CDT)] * 2
        + [_sds((t, 128), F32), _sds((8, t), F32), _sds((512, t), CDT)],
        scratch_shapes=[pltpu.VMEM((8, 128), F32)],
        compiler_params=_params(("arbitrary",)),
    )(proj, proj, proj, proj, proj, proj, proj, gfq, gfk, gsq, gsk, fb)


def _qknorm_bwd(proj, dqf, dkf, dvf, dqs, dkse, dvse, dcq, dck, dga, dgb, gfq, gfk, gsq, gsk, fb, name):
    t = proj.shape[0]
    tm = _row_tile(t)
    nt = t // tm

    def body(qa, ka, qb, kb, fa, dqf_r, dkf_r, dvf_r, dqs_r, dkse_r, dvse_r, dcq_r, dck_r, dga_r, dgb_r,
             gfq_r, gfk_r, gsq_r, gsk_r, fb_r, dp_o, dgn_o, carry, acc):
        i = pl.program_id(0)
        bd = _block_diag()
        lane = _iota((1, 128), 1)
        lo = lane < HD

        @pl.when(i == 0)
        def _():
            carry[...] = jnp.zeros(carry.shape, F32)
            acc[...] = jnp.zeros(acc.shape, F32)

        def hnorm_bwd(x, g, dy):
            r = lax.rsqrt(_dot_hi(x * x, bd) * (1.0 / HD) + EPS)
            xh = x * r
            day = dy * g
            dx = r * (day - xh * (_dot_hi(day * xh, bd) * (1.0 / HD)))
            return dx, jnp.sum(dy * xh, axis=0, keepdims=True)

        for ch in range(4):
            sl = slice(128 * ch, 128 * (ch + 1))
            dx, dg = hnorm_bwd(qa[:, sl], gfq_r[:, sl], dqf_r[:, sl] * 0.125)
            dp_o[:, QA + 128 * ch:QA + 128 * (ch + 1)] = dx.astype(CDT)
            acc[0:1, sl] += dg
            dx, dg = hnorm_bwd(ka[:, sl], gfk_r[:, sl], dkf_r[:, sl])
            dp_o[:, KA + 128 * ch:KA + 128 * (ch + 1)] = dx.astype(CDT)
            acc[1:2, sl] += dg
            dx, dg = hnorm_bwd(qb[:, sl], gsq_r[:, sl], dqs_r[:, sl] * 0.125)
            dp_o[:, QB + 128 * ch:QB + 128 * (ch + 1)] = dx.astype(CDT)
            acc[2:3, sl] += dg
        dp_o[:, VA:VA + 512] = dvf_r[...].astype(CDT)
        dp_o[:, GA:GA + D] = dga_r[...]
        dp_o[:, GB:GB + D] = dgb_r[...]

        def fold(x):
            e0 = x[0]
            e1 = x[1]
            return jnp.where(lo, e0 + pltpu.roll(e0, 64, 1), e1 + pltpu.roll(e1, 64, 1))

        dx, dg = hnorm_bwd(kb[...], gsk_r[...], fold(dkse_r))
        dp_o[:, KB:KB + 128] = dx.astype(CDT)
        acc[3:4, 0:128] += dg
        dp_o[:, VB:VB + 128] = fold(dvse_r).astype(CDT)

        rr = _iota((512, 128), 0)
        hh = _iota((512, 128), 1)
        sel = ((rr == (hh >> 1) * 128 + (hh & 1) * HD) & (hh < 8)).astype(F32)
        dcs = _dot_hi(dcq_r[...] - dck_r[...], sel)
        utri = (_iota((tm, tm), 1) >= _iota((tm, tm), 0)).astype(F32)
        dlf = _dot_hi(utri, dcs) + carry[0:1, :]
        carry[0:1, :] = dlf[0:1, :]
        z = fa[...] + fb_r[...]
        dfa = jnp.where(lane < 8, dlf * _sigmoid(-z), 0.0)
        dp_o[:, FA:FA + 128] = dfa.astype(CDT)
        acc[4:5, 0:128] += jnp.sum(dfa, axis=0, keepdims=True)

        @pl.when(i == nt - 1)
        def _():
            foldm = ((_iota((512, 128), 0) & (HD - 1)) == _iota((512, 128), 1)).astype(F32)
            dgn_o[...] = _dot_hi(acc[...], foldm)

    def col(width, off):
        return pl.BlockSpec((tm, width), lambda i: (nt - 1 - i, off // width))

    def rows(width):
        return pl.BlockSpec((tm, width), lambda i: (nt - 1 - i, 0))

    def vec(width):
        return pl.BlockSpec((1, width), lambda i: (0, 0))

    pair = pl.BlockSpec((2, tm, 128), lambda i: (0, nt - 1 - i, 0))
    return pl.pallas_call(
        body, name=name, grid=(nt,),
        in_specs=[col(512, QA), col(512, KA), col(512, QB), col(128, KB), col(128, FA),
                  rows(512), rows(512), rows(512), rows(512), pair, pair, rows(512), rows(512), rows(D), rows(D),
                  vec(512), vec(512), vec(512), vec(128), vec(128)],
        out_specs=[rows(DP), pl.BlockSpec((8, 128), lambda i: (0, 0))],
        out_shape=[_sds((t, DP), CDT), _sds((8, 128), F32)],
        scratch_shapes=[pltpu.VMEM((8, 128), F32), pltpu.VMEM((8, 512), F32)],
        compiler_params=_params(("arbitrary",)),
    )(proj, proj, proj, proj, proj, dqf, dkf, dvf, dqs, dkse, dvse, dcq, dck, dga, dgb, gfq, gfk, gsq, gsk, fb)


def _gate_fwd(ofox, oswa, wbf, wbs, proj, name):
    t = ofox.shape[0]
    tm = _row_tile(t)
    tn = 512

    def body(of_r, os_r, wf_r, ws_r, ga_r, gb_r, y_o, yt_o, pf_o, ps_o, oft_o, ost_o):
        j = pl.program_id(1)
        pf = _dot(of_r[...], wf_r[...])
        ps = _dot(os_r[...], ws_r[...])
        y = _sigmoid(ga_r[...]) * pf + _sigmoid(gb_r[...]) * ps
        y_o[...] = y.astype(CDT)
        yt_o[...] = y.T.astype(CDT)
        pf_o[...] = pf.astype(CDT)
        ps_o[...] = ps.astype(CDT)

        @pl.when(j == 0)
        def _():
            oft_o[...] = of_r[...].astype(F32).T.astype(CDT)
            ost_o[...] = os_r[...].astype(F32).T.astype(CDT)

    tile = pl.BlockSpec((tm, tn), lambda i, j: (i, j))
    return pl.pallas_call(
        body, name=name, grid=(t // tm, D // tn),
        in_specs=[pl.BlockSpec((tm, 512), lambda i, j: (i, 0)), pl.BlockSpec((tm, 512), lambda i, j: (i, 0)),
                  pl.BlockSpec((512, tn), lambda i, j: (0, j)), pl.BlockSpec((512, tn), lambda i, j: (0, j)),
                  pl.BlockSpec((tm, tn), lambda i, j: (i, GA // tn + j)),
                  pl.BlockSpec((tm, tn), lambda i, j: (i, GB // tn + j))],
        out_specs=[tile, pl.BlockSpec((tn, tm), lambda i, j: (j, i)), tile, tile,
                   pl.BlockSpec((512, tm), lambda i, j: (0, i)), pl.BlockSpec((512, tm), lambda i, j: (0, i))],
        out_shape=[_sds((t, D), CDT), _sds((D, t), CDT), _sds((t, D), CDT), _sds((t, D), CDT),
                   _sds((512, t), CDT), _sds((512, t), CDT)],
        compiler_params=_params(("parallel", "arbitrary")),
    )(ofox, oswa, wbf, wbs, proj, proj)


def _gate_bwd(dy, pf, ps, proj, name):
    t = dy.shape[0]
    tm = _row_tile(t)
    tn = 512

    def body(dy_r, pf_r, ps_r, ga_r, gb_r, dpf_o, dps_o, dga_o, dgb_o):
        dy_ = dy_r[...]
        sa = _sigmoid(ga_r[...])
        sb = _sigmoid(gb_r[...])
        dpf_o[...] = (dy_ * sa).astype(CDT)
        dps_o[...] = (dy_ * sb).astype(CDT)
        dga_o[...] = (dy_ * pf_r[...].astype(F32) * (sa * (1.0 - sa))).astype(CDT)
        dgb_o[...] = (dy_ * ps_r[...].astype(F32) * (sb * (1.0 - sb))).astype(CDT)

    tile = pl.BlockSpec((tm, tn), lambda i, j: (i, j))
    return pl.pallas_call(
        body, name=name, grid=(t // tm, D // tn),
        in_specs=[tile, tile, tile,
                  pl.BlockSpec((tm, tn), lambda i, j: (i, GA // tn + j)),
                  pl.BlockSpec((tm, tn), lambda i, j: (i, GB // tn + j))],
        out_specs=[tile] * 4,
        out_shape=[_sds((t, D), CDT)] * 4,
        compiler_params=_params(("parallel", "parallel")),
    )(dy, pf, ps, proj, proj)


def _tri_steps(n, by_key):
    if by_key:
        pairs = [(i, j) for j in range(n) for i in range(j, n)]
    else:
        pairs = [(i, j) for i in range(n) for j in range(i + 1)]
    return (np.array([p[0] for p in pairs], np.int32), np.array([p[1] for p in pairs], np.int32))


def _head_col(blk, lane, h):
    return jnp.sum(jnp.where(lane == h, blk, 0.0), axis=1, keepdims=True)


def _head_row(blk, sub, h):
    return jnp.sum(jnp.where(sub == h, blk, 0.0), axis=0, keepdims=True)


def _ride_specs(ride):
    if ride is None:
        return [], [], [], [], []
    kind, srcs, outs, layer = ride
    return list(srcs), [ANY] * len(srcs), list(outs), [ANY] * len(outs), _dma_sems(3 * len(srcs))


def _ride_start(ride, srcs, dsts, send_sems, recv_sems):
    for cp in _ici_copies(ride[0], srcs, dsts, send_sems, recv_sems, ride[3])[0]:
        cp.start()


def _ride_wait(ride, srcs, dsts, send_sems, recv_sems):
    sends, recvs = _ici_copies(ride[0], srcs, dsts, send_sems, recv_sems, ride[3])
    for cp in recvs:
        cp.wait_recv()
    for cp in sends:
        cp.wait_send()


def _fox_fwd(qf, kf, vf, c, ct, name, ride=None):
    t = qf.shape[0]
    ta = _row_tile(t)
    qi, kj = _tri_steps(t // ta, by_key=False)
    nsteps = len(qi)
    ride_in, ride_in_specs, ride_out, ride_out_specs, ride_sems = _ride_specs(ride)

    def body(qi_r, kj_r, q_r, k_r, v_r, c_r, ct_r, *rest):
        nr = len(ride_in)
        src_r, (o_o, lse_o), dst_o = rest[:nr], rest[nr:nr + 2], rest[nr + 2:2 * nr + 2]
        m_sc, l_sc, acc_sc, cq_sc, *sems = rest[2 * nr + 2:]
        p = pl.program_id(0)
        n = pl.program_id(1)
        i = qi_r[n]
        j = kj_r[n]
        lane = _iota((1, 128), 1)
        lo = lane < HD

        if ride is not None:
            @pl.when((p == 0) & (n == 0))
            def _():
                _ride_start(ride, src_r, dst_o, *sems)

        @pl.when(j == 0)
        def _():
            m_sc[...] = jnp.full(m_sc.shape, NEG, F32)
            l_sc[...] = jnp.zeros(l_sc.shape, F32)
            acc_sc[...] = jnp.zeros(acc_sc.shape, F32)
            for e in (0, 1):
                cq_sc[e] = jnp.broadcast_to(_head_col(c_r[...], lane, 2 * p + e), (ta, 128))

        def step(masked):
            q = q_r[...]
            k = k_r[...]
            vaug = jnp.concatenate([v_r[...], jnp.ones((ta, 128), CDT)], axis=1)
            if masked:
                rows = i * ta + _iota((ta, 1), 0)
                cols = j * ta + _iota((1, ta), 1)
                mask = (cols <= rows) & (cols >= PAD)
            sub = _iota((8, 1), 0)
            alphas, pvs = [], []
            for e in (0, 1):
                sel = lo if e == 0 else jnp.logical_not(lo)
                s = _dot_nt(jnp.where(sel, q, 0), k)
                ck = _head_row(ct_r[...], sub, 2 * p + e)
                cq = cq_sc[e]
                chunks = []
                for ch in range(ta // 128):
                    sl = slice(128 * ch, 128 * (ch + 1))
                    sc = s[:, sl] + cq - ck[:, sl]
                    if masked:
                        sc = jnp.where(mask[:, sl], sc, NEG)
                    chunks.append(sc)
                mx = chunks[0]
                for sc in chunks[1:]:
                    mx = jnp.maximum(mx, sc)
                m_prev = m_sc[e]
                m_new = jnp.maximum(m_prev, jnp.max(mx, axis=1, keepdims=True))
                alpha = jnp.exp(m_prev - m_new)
                pe = jnp.concatenate([jnp.exp(sc - m_new).astype(CDT) for sc in chunks], axis=1)
                pva = _dot(pe, vaug)
                l_sc[e] = alpha * l_sc[e] + pva[:, 128:]
                m_sc[e] = m_new
                alphas.append(alpha)
                pvs.append(pva[:, :128])
            acc_sc[...] = acc_sc[...] * jnp.where(lo, alphas[0], alphas[1]) + jnp.where(lo, pvs[0], pvs[1])

        edge = (j == i) | (j == 0)

        @pl.when(edge)
        def _():
            step(True)

        @pl.when(jnp.logical_not(edge))
        def _():
            step(False)

        @pl.when(j == i)
        def _():
            l = jnp.where(lo, l_sc[0], l_sc[1])
            o_o[...] = (acc_sc[...] / l).astype(CDT)
            lse_o[...] = jnp.where(lo, m_sc[0], m_sc[1]) + jnp.log(l)

        if ride is not None:
            @pl.when((p == NPAIR - 1) & (n == nsteps - 1))
            def _():
                _ride_wait(ride, src_r, dst_o, *sems)

    qblk = pl.BlockSpec((ta, 128), lambda p, n, qi_r, kj_r: (qi_r[n], p))
    kblk = pl.BlockSpec((ta, 128), lambda p, n, qi_r, kj_r: (kj_r[n], p))
    grid_spec = pltpu.PrefetchScalarGridSpec(
        num_scalar_prefetch=2, grid=(NPAIR, nsteps),
        in_specs=[qblk, kblk, kblk,
                  pl.BlockSpec((ta, 128), lambda p, n, qi_r, kj_r: (qi_r[n], 0)),
                  pl.BlockSpec((8, ta), lambda p, n, qi_r, kj_r: (0, kj_r[n]))] + ride_in_specs,
        out_specs=[qblk, qblk] + ride_out_specs,
        scratch_shapes=[pltpu.VMEM((2, ta, 128), F32), pltpu.VMEM((2, ta, 128), F32), pltpu.VMEM((ta, 128), F32),
                        pltpu.VMEM((2, ta, 128), F32)] + ride_sems,
    )
    return pl.pallas_call(
        body, name=name, grid_spec=grid_spec,
        out_shape=[_sds((t, 512), CDT), _sds((t, 512), F32)] + ride_out,
        compiler_params=_params(("arbitrary", "arbitrary")),
    )(jnp.asarray(qi), jnp.asarray(kj), qf, kf, vf, c, ct, *ride_in)


def _fox_bwd(qf, qft, kf, vf, c, ct, o, lse, do, dot, name, ride=None):
    t = qf.shape[0]
    ta = _row_tile(t)
    nq = t // ta
    qi, kj = _tri_steps(nq, by_key=False)
    nsteps = len(qi)
    ride_in, ride_in_specs, ride_out, ride_out_specs, ride_sems = _ride_specs(ride)

    def body(qi_r, kj_r, q_r, qt_r, k_r, v_r, c_r, ct_r, o_r, lse_r, do_r, dot_r, *rest):
        nr = len(ride_in)
        src_r, (dq_o, dcq_o, dk_o, dv_o, dck_o), dst_o = rest[:nr], rest[nr:nr + 5], rest[nr + 5:2 * nr + 5]
        lse_sc, dl_sc, cq_sc, dq_sc, dcq_sc, dkt_sc, dvt_sc, dckt_sc, *sems = rest[2 * nr + 5:]
        p = pl.program_id(0)
        n = pl.program_id(1)
        i = qi_r[n]
        j = kj_r[n]
        lane = _iota((1, 128), 1)
        lo = lane < HD
        top = _iota((128, 1), 0) < HD

        if ride is not None:
            @pl.when((p == 0) & (n == 0))
            def _():
                _ride_start(ride, src_r, dst_o, *sems)

        @pl.when(n == 0)
        def _():
            dkt_sc[...] = jnp.zeros(dkt_sc.shape, F32)
            dvt_sc[...] = jnp.zeros(dvt_sc.shape, F32)
            dckt_sc[...] = jnp.zeros(dckt_sc.shape, F32)

        @pl.when(j == 0)
        def _():
            dq_sc[...] = jnp.zeros(dq_sc.shape, F32)
            dcq_sc[...] = jnp.zeros(dcq_sc.shape, F32)
            dd = do_r[...] * o_r[...].astype(F32)
            lse = lse_r[...]
            for e in (0, 1):
                sel = lo if e == 0 else jnp.logical_not(lo)
                cq_sc[e] = jnp.broadcast_to(_head_col(c_r[...], lane, 2 * p + e), (ta, 128))
                dl_sc[e] = jnp.broadcast_to(jnp.sum(jnp.where(sel, dd, 0.0), axis=1, keepdims=True), (ta, 128))
                lse_sc[e] = jnp.broadcast_to(lse[:, HD * e:HD * e + 1], (ta, 128))

        def step(masked):
            q = q_r[...]
            qt = qt_r[...]
            k = k_r[...]
            v = v_r[...]
            dob = do_r[...].astype(CDT)
            dot_ = dot_r[...]
            ones = jnp.ones((ta, 128), CDT)
            ones16 = jnp.ones((16, ta), CDT)
            if masked:
                rows = i * ta + _iota((ta, 1), 0)
                cols = j * ta + _iota((1, ta), 1)
                mask = (cols <= rows) & (cols >= PAD)
            sub = _iota((8, 1), 0)
            for e in (0, 1):
                sel = lo if e == 0 else jnp.logical_not(lo)
                rsel = top if e == 0 else jnp.logical_not(top)
                s = _dot_nt(jnp.where(sel, q, 0), k)
                dp = _dot_nt(jnp.where(sel, dob, 0), v)
                ck = _head_row(ct_r[...], sub, 2 * p + e)
                cq, lse_e, dl = cq_sc[e], lse_sc[e], dl_sc[e]
                prs, dss = [], []
                for ch in range(ta // 128):
                    sl = slice(128 * ch, 128 * (ch + 1))
                    sc = s[:, sl] + cq - ck[:, sl]
                    if masked:
                        sc = jnp.where(mask[:, sl], sc, NEG)
                    pr = jnp.exp(sc - lse_e)
                    prs.append(pr.astype(CDT))
                    dss.append((pr * (dp[:, sl] - dl)).astype(CDT))
                pb = jnp.concatenate(prs, axis=1)
                dsb = jnp.concatenate(dss, axis=1)
                dvt_sc[j] += _dot(jnp.where(rsel, dot_, 0), pb)
                dkc = _dot(jnp.concatenate([jnp.where(rsel, qt, 0), ones16], axis=0), dsb)
                dkt_sc[j] += dkc[0:128]
                dckt_sc[j, 0:8, :] += jnp.where(sub == e, dkc[128:136], 0.0)
                dqa = _dot(dsb, jnp.concatenate([jnp.where(sel, k, 0), ones], axis=1))
                dq_sc[...] += dqa[:, :128]
                dcq_sc[e] += dqa[:, 128:]

        edge = (j == i) | (j == 0)

        @pl.when(edge)
        def _():
            step(True)

        @pl.when(jnp.logical_not(edge))
        def _():
            step(False)

        @pl.when(j == i)
        def _():
            dq_o[...] = dq_sc[...]
            dcq_o[...] = jnp.where(lo, dcq_sc[0], dcq_sc[1])

        @pl.when(n == nsteps - 1)
        def _():
            spread = (_iota((128, 128), 1) == _iota((128, 128), 0) // HD).astype(F32)
            for jb in range(nq):
                rs = slice(jb * ta, (jb + 1) * ta)
                dk_o[rs, :] = dkt_sc[jb].T
                dv_o[rs, :] = dvt_sc[jb].T
                dck_o[rs, :] = _dot_hi(spread, dckt_sc[jb]).T

        if ride is not None:
            @pl.when((p == NPAIR - 1) & (n == nsteps - 1))
            def _():
                _ride_wait(ride, src_r, dst_o, *sems)

    qblk = pl.BlockSpec((ta, 128), lambda p, n, qi_r, kj_r: (qi_r[n], p))
    qtblk = pl.BlockSpec((128, ta), lambda p, n, qi_r, kj_r: (p, qi_r[n]))
    kblk = pl.BlockSpec((ta, 128), lambda p, n, qi_r, kj_r: (kj_r[n], p))
    whole = pl.BlockSpec((t, 128), lambda p, n, qi_r, kj_r: (0, p))
    grid_spec = pltpu.PrefetchScalarGridSpec(
        num_scalar_prefetch=2, grid=(NPAIR, nsteps),
        in_specs=[qblk, qtblk, kblk, kblk,
                  pl.BlockSpec((ta, 128), lambda p, n, qi_r, kj_r: (qi_r[n], 0)),
                  pl.BlockSpec((8, ta), lambda p, n, qi_r, kj_r: (0, kj_r[n])),
                  qblk, qblk, qblk, qtblk] + ride_in_specs,
        out_specs=[qblk, qblk, whole, whole, whole] + ride_out_specs,
        scratch_shapes=[pltpu.VMEM((2, ta, 128), F32)] * 3 + [pltpu.VMEM((ta, 128), F32), pltpu.VMEM((2, ta, 128), F32)]
        + [pltpu.VMEM((nq, 128, ta), F32)] * 3 + ride_sems,
    )
    return pl.pallas_call(
        body, name=name, grid_spec=grid_spec,
        out_shape=[_sds((t, 512), F32)] * 5 + ride_out,
        compiler_params=_params(("arbitrary", "arbitrary")),
    )(jnp.asarray(qi), jnp.asarray(kj), qf, qft, kf, vf, c, ct, o, lse, do, dot, *ride_in)


def _bucket_table():
    r = np.arange(BLK)[:, None]
    c = np.arange(3 * BLK)[None, :]
    d = np.where(c < BLK, r + BLK - c, r - (c - BLK))
    n = np.maximum(d, 0)
    max_exact = N_BUCKETS // 2
    nf = np.maximum(n, 1).astype(np.float32)
    large = max_exact + (np.log(nf / max_exact) / math.log(BLK / max_exact) * (N_BUCKETS - max_exact)).astype(np.int32)
    large = np.minimum(large, N_BUCKETS - 1)
    b = np.where(n < max_exact, n, large)
    return np.where(c < 2 * BLK, b, N_BUCKETS - 1).astype(np.int32)


def _bias_fwd(table, name):
    bucket = jnp.asarray(_bucket_table())

    def body(tab_r, b_r, o_o):
        h = pl.program_id(0)
        b = b_r[...]
        acc = jnp.zeros(b.shape, F32)
        for k in range(N_BUCKETS):
            acc = jnp.where(b == k, tab_r[k, h], acc)
        o_o[...] = acc

    return pl.pallas_call(
        body, name=name, grid=(8,),
        in_specs=[pl.BlockSpec(memory_space=pltpu.SMEM), pl.BlockSpec((BLK, 3 * BLK), lambda h: (0, 0))],
        out_specs=pl.BlockSpec((None, BLK, 3 * BLK), lambda h: (h, 0, 0)),
        out_shape=_sds((8, BLK, 3 * BLK), F32),
        compiler_params=_params(("parallel",)),
    )(table, bucket)


def _bias_bwd(dbias, name):
    bucket = jnp.asarray(_bucket_table())

    def body(d_r, b_r, o_o):
        h = pl.program_id(0)
        b = b_r[...]
        d = d_r[...]
        lane = _iota((1, 128), 1)
        row = jnp.zeros((1, 128), F32)
        for k in range(N_BUCKETS):
            row = jnp.where(lane == k, jnp.sum(jnp.where(b == k, d, 0.0)), row)
        o_o[pl.ds(h, 1), :] = row

    return pl.pallas_call(
        body, name=name, grid=(8,),
        in_specs=[pl.BlockSpec((None, BLK, 3 * BLK), lambda h: (h, 0, 0)), pl.BlockSpec((BLK, 3 * BLK), lambda h: (0, 0))],
        out_specs=pl.BlockSpec((8, 128), lambda h: (0, 0)),
        out_shape=_sds((8, 128), F32),
        compiler_params=_params(("arbitrary",)),
    )(dbias, bucket)


def _swa_valid(i):
    r = _iota((BLK, 1), 0)
    c = _iota((1, 3 * BLK), 1)
    prev = (c < BLK) & (c > r) & (i >= 1) & ((i - 1) * BLK + c >= PAD)
    cc = c - BLK
    cur = (c >= BLK) & (c < 2 * BLK) & (cc <= r) & (i * BLK + cc >= PAD)
    cm = c - 2 * BLK
    meta = (c >= 2 * BLK) & (cm >= PAD) & (i * BLK + r - cm >= BLK)
    return prev | cur | meta


def _swa_kv_specs():
    def at(f):
        return pl.BlockSpec((None, BLK, 128), lambda p, i: (p // 2, f(i), 0))
    return [at(lambda i: jnp.maximum(i - 1, 0)), at(lambda i: i), at(lambda i: 0)]


def _swa_fwd(qs, kse, vse, bias, sinks, name):
    t = qs.shape[0]

    def body(sink_r, q_r, kp_r, kc_r, km_r, vp_r, vc_r, vm_r, b_r, o_o, lse_o):
        p = pl.program_id(0)
        i = pl.program_id(1)
        lo = _iota((1, 128), 1) < HD
        q = q_r[...]
        k3 = jnp.concatenate([kp_r[...], kc_r[...], km_r[...]], axis=0)
        v3 = jnp.concatenate([vp_r[...], vc_r[...], vm_r[...]], axis=0)
        valid = _swa_valid(i)
        outs, lses = [], []
        for e in (0, 1):
            sel = lo if e == 0 else jnp.logical_not(lo)
            s = _dot_nt(jnp.where(sel, q, 0), k3) + b_r[e]
            s = jnp.where(valid, s, NEG)
            sink = sink_r[2 * p + e]
            mx = jnp.maximum(jnp.max(s, axis=1, keepdims=True), sink)
            pe = jnp.exp(s - mx)
            den = jnp.sum(pe, axis=1, keepdims=True) + jnp.exp(sink - mx)
            outs.append(_dot(pe.astype(CDT), v3) / den)
            lses.append(mx + jnp.log(den))
        o_o[...] = jnp.where(lo, outs[0], outs[1]).astype(CDT)
        lse_o[...] = jnp.where(lo, lses[0], lses[1])

    qblk = pl.BlockSpec((BLK, 128), lambda p, i: (i, p))
    return pl.pallas_call(
        body, name=name, grid=(NPAIR, t // BLK),
        in_specs=[pl.BlockSpec(memory_space=pltpu.SMEM), qblk] + _swa_kv_specs() + _swa_kv_specs()
        + [pl.BlockSpec((2, BLK, 3 * BLK), lambda p, i: (p, 0, 0))],
        out_specs=[qblk, qblk],
        out_shape=[_sds((t, 512), CDT), _sds((t, 512), F32)],
        compiler_params=_params(("parallel", "parallel")),
    )(sinks, qs, kse, kse, kse, vse, vse, vse, bias)


def _swa_bwd(qs, kse, vse, bias, sinks, o, lse, do, name):
    t = qs.shape[0]

    def body(sink_r, q_r, kp_r, kc_r, km_r, vp_r, vc_r, vm_r, b_r, o_r, lse_r, do_r,
             dq_o, dk_o, dv_o, db_o, dsk_o):
        p = pl.program_id(0)
        i = pl.program_id(1)
        lo = _iota((1, 128), 1) < HD

        @pl.when((i == 0) & (p % 2 == 0))
        def _():
            dk_o[...] = jnp.zeros(dk_o.shape, F32)
            dv_o[...] = jnp.zeros(dv_o.shape, F32)

        @pl.when(i == 0)
        def _():
            db_o[...] = jnp.zeros(db_o.shape, F32)
            dsk_o[...] = jnp.zeros(dsk_o.shape, F32)

        q = q_r[...]
        do_ = do_r[...]
        dd = do_ * o_r[...].astype(F32)
        lse = lse_r[...]
        k3 = jnp.concatenate([kp_r[...], kc_r[...], km_r[...]], axis=0)
        v3 = jnp.concatenate([vp_r[...], vc_r[...], vm_r[...]], axis=0)
        valid = _swa_valid(i)
        dq = jnp.zeros((BLK, 128), F32)
        dk3 = jnp.zeros((3 * BLK, 128), F32)
        dv3 = jnp.zeros((3 * BLK, 128), F32)
        dsink = []
        for e in (0, 1):
            sel = lo if e == 0 else jnp.logical_not(lo)
            qe = jnp.where(sel, q, 0)
            doe = jnp.where(sel, do_, 0.0).astype(CDT)
            lse_e = lse[:, HD * e:HD * e + 1]
            s = _dot_nt(qe, k3) + b_r[e]
            s = jnp.where(valid, s, NEG)
            pr = jnp.exp(s - lse_e)
            delta = jnp.sum(jnp.where(sel, dd, 0.0), axis=1, keepdims=True)
            ds = pr * (_dot_nt(doe, v3) - delta)
            db_o[e] += ds
            dsink.append(-jnp.sum(jnp.exp(sink_r[2 * p + e] - lse_e) * delta, axis=0, keepdims=True))
            dq = dq + _dot(ds.astype(CDT), jnp.where(sel, k3, 0))
            dk3 = dk3 + _dot(ds.T.astype(CDT), qe)
            dv3 = dv3 + _dot(pr.T.astype(CDT), doe)
        dq_o[...] = dq
        prev = pl.ds(pl.multiple_of(jnp.maximum(i - 1, 0) * BLK, BLK), BLK)
        cur = pl.ds(pl.multiple_of(i * BLK, BLK), BLK)
        dk_o[prev, :] += dk3[0:BLK]
        dk_o[cur, :] += dk3[BLK:2 * BLK]
        dk_o[0:BLK, :] += dk3[2 * BLK:]
        dv_o[prev, :] += dv3[0:BLK]
        dv_o[cur, :] += dv3[BLK:2 * BLK]
        dv_o[0:BLK, :] += dv3[2 * BLK:]
        dsk_o[0:1, :] += jnp.where(lo, dsink[0], dsink[1])

    qblk = pl.BlockSpec((BLK, 128), lambda p, i: (i, p))
    kvacc = pl.BlockSpec((None, t, 128), lambda p, i: (p // 2, 0, 0))
    bblk = pl.BlockSpec((2, BLK, 3 * BLK), lambda p, i: (p, 0, 0))
    return pl.pallas_call(
        body, name=name, grid=(NPAIR, t // BLK),
        in_specs=[pl.BlockSpec(memory_space=pltpu.SMEM), qblk] + _swa_kv_specs() + _swa_kv_specs()
        + [bblk, qblk, qblk, qblk],
        out_specs=[qblk, kvacc, kvacc, bblk, pl.BlockSpec((None, 8, 128), lambda p, i: (p, 0, 0))],
        out_shape=[_sds((t, 512), F32), _sds((2, t, 128), F32), _sds((2, t, 128), F32),
                   _sds((8, BLK, 3 * BLK), F32), _sds((NPAIR, 8, 128), F32)],
        compiler_params=_params(("arbitrary", "arbitrary")),
    )(sinks, qs, kse, kse, kse, vse, vse, vse, bias, o, lse, do)


def _adamw(w, g, m, v, name):
    r, c = w.shape
    tr = 128 if r % 128 == 0 else r

    def body(w_r, g_r, m_r, v_r, d_o, m_o, v_o):
        g_ = g_r[...]
        m_ = ADAM_B1 * m_r[...] + (1.0 - ADAM_B1) * g_
        v_ = ADAM_B2 * v_r[...] + (1.0 - ADAM_B2) * jnp.square(g_)
        m_hat = m_ / (1.0 - ADAM_B1 ** ADAM_STEP)
        v_hat = v_ / (1.0 - ADAM_B2 ** ADAM_STEP)
        d_o[...] = -ADAM_LR * (m_hat / (jnp.sqrt(v_hat) + ADAM_EPS) + ADAM_WD * w_r[...])
        m_o[...] = m_
        v_o[...] = v_

    blk = pl.BlockSpec((tr, c), lambda i: (i, 0))
    return pl.pallas_call(
        body, name=name, grid=(r // tr,),
        in_specs=[blk] * 4, out_specs=[blk] * 3, out_shape=[_sds((r, c), F32)] * 3,
        compiler_params=_params(("parallel",)),
    )(w, g, m, v)


def _pair_add(own, got, half_idx, name):
    tr = PACK_TILE
    nb = HALF_ROWS // tr
    nl = own.shape[1]

    def body(c_r, a_r, b_r, o_o):
        o_o[...] = (a_r[...].astype(F32) + b_r[...].astype(F32)).astype(CDT)

    grid_spec = pltpu.PrefetchScalarGridSpec(
        num_scalar_prefetch=1, grid=(4, nl, nb),
        in_specs=[pl.BlockSpec((None, None, tr, 128), lambda s, l, i, c_r: (s, l, c_r[0] * nb + i, 0)),
                  pl.BlockSpec((None, None, tr, 128), lambda s, l, i, c_r: (s, l, i, 0))],
        out_specs=pl.BlockSpec((None, None, tr, 128), lambda s, l, i, c_r: (s, l, i, 0)),
    )
    return pl.pallas_call(
        body, name=name, grid_spec=grid_spec, out_shape=_sds((4, nl, HALF_ROWS, 128), CDT),
        compiler_params=_params(("parallel", "parallel", "parallel")),
    )(half_idx, own, got)


def _sum4(ps, got, shard_idx, name):
    tr = PACK_TILE
    nl = ps.shape[1]

    def body(s_r, a_r, b_r, o_o):
        o_o[...] = ((a_r[...].astype(F32) + b_r[0].astype(F32)) + b_r[1].astype(F32)) + b_r[2].astype(F32)

    grid_spec = pltpu.PrefetchScalarGridSpec(
        num_scalar_prefetch=1, grid=(nl, HALF_ROWS // tr),
        in_specs=[pl.BlockSpec((None, None, tr, 128), lambda l, i, s_r: (s_r[0], l, i, 0)),
                  pl.BlockSpec((3, None, tr, 128), lambda l, i, s_r: (0, l, i, 0))],
        out_specs=pl.BlockSpec((None, tr, 128), lambda l, i, s_r: (l, i, 0)),
    )
    return pl.pallas_call(
        body, name=name, grid_spec=grid_spec, out_shape=_sds((nl, HALF_ROWS, 128), F32),
        compiler_params=_params(("parallel", "parallel")),
    )(shard_idx, ps, got)


def _sum8(slots, name):
    def body(a_r, o_o):
        acc = a_r[0]
        for k in range(1, 8):
            acc = acc + a_r[k]
        o_o[...] = acc

    return pl.pallas_call(
        body, name=name, out_shape=_sds((SMALL_ROWS, 128), F32),
        in_specs=[pl.BlockSpec(memory_space=pltpu.VMEM)], out_specs=pl.BlockSpec(memory_space=pltpu.VMEM),
        compiler_params=_params(),
    )(slots)


def _place():
    x, y, c = lax.axis_index("x"), lax.axis_index("y"), lax.axis_index("c")
    chips = [(1 - x, y), (x, 1 - y), (1 - x, 1 - y)]
    return x, y, c, chips


def _remote(src, dst, send_sems, recv_sems, k, to):
    return pltpu.make_async_remote_copy(src_ref=src, dst_ref=dst, send_sem=send_sems.at[k], recv_sem=recv_sems.at[k],
                                        device_id=to, device_id_type=MESH_ID)


ANY = pl.BlockSpec(memory_space=pl.ANY)


def _ici_copies(kind, src_r, dst_r, send_sems, recv_sems):
    x, y, c, chips = _place()
    s = 2 * x + y
    mine = pl.ds(pl.multiple_of(c * HALF_ROWS, 16), HALF_ROWS)
    sends, recvs = [], []
    for j, (cx, cy) in enumerate(chips):
        sj = 2 * cx + cy
        if kind == "gather":
            sends.append(_remote(src_r.at[:, mine], dst_r.at[s, :, mine], send_sems, recv_sems, j, (cx, cy, c)))
            recvs.append(_remote(src_r.at[:, mine], dst_r.at[sj, :, mine], send_sems, recv_sems, j, (cx, cy, c)))
        else:
            sends.append(_remote(src_r.at[sj], dst_r.at[j], send_sems, recv_sems, j, (cx, cy, c)))
            recvs.append(sends[-1])
    return sends, recvs


def _forward_halves(wall, name):
    def body(w_r, w_o, send_sems, recv_sems):
        x, y, c, chips = _place()
        mine = pl.ds(pl.multiple_of(c * HALF_ROWS, 16), HALF_ROWS)
        other = pl.ds(pl.multiple_of((1 - c) * HALF_ROWS, 16), HALF_ROWS)
        sent = []
        for j, (cx, cy) in enumerate(chips):
            sj = 2 * cx + cy
            sent.append(_remote(w_r.at[sj, :, mine], w_o.at[sj, :, mine], send_sems, recv_sems, j, (x, y, 1 - c)))
        for cp in sent:
            cp.start()
        for j, (cx, cy) in enumerate(chips):
            sj = 2 * cx + cy
            _remote(w_r.at[sj, :, other], w_o.at[sj, :, other], send_sems, recv_sems, j, (x, y, 1 - c)).wait_recv()
        for cp in sent:
            cp.wait_send()

    return pl.pallas_call(
        body, name=name, out_shape=_sds(wall.shape, wall.dtype),
        in_specs=[ANY], out_specs=ANY, input_output_aliases={0: 0},
        scratch_shapes=[pltpu.SemaphoreType.DMA((3,)), pltpu.SemaphoreType.DMA((3,))],
    )(wall)


def _gather_weights(wflat, mflat, name):
    nl = wflat.shape[0]

    def body(w_r, m_r, wall_o, mall_o, send_sems, recv_sems):
        x, y, c, chips = _place()
        s = 2 * x + y
        sib = (x, y, 1 - c)
        mine = pl.ds(pl.multiple_of(c * HALF_ROWS, 16), HALF_ROWS)
        other = pl.ds(pl.multiple_of((1 - c) * HALF_ROWS, 16), HALF_ROWS)
        sent = []
        for j, (cx, cy) in enumerate(chips):
            sent.append(_remote(w_r.at[:, mine], wall_o.at[s, :, mine], send_sems, recv_sems, j, (cx, cy, c)))
            sent.append(_remote(m_r, mall_o.at[s], send_sems, recv_sems, 6 + j, (cx, cy, c)))
        for cp in sent:
            cp.start()
        for j, (cx, cy) in enumerate(chips):
            sj = 2 * cx + cy
            _remote(w_r.at[:, mine], wall_o.at[sj, :, mine], send_sems, recv_sems, j, sib).wait_recv()
            fwd = _remote(wall_o.at[sj, :, mine], wall_o.at[sj, :, mine], send_sems, recv_sems, 3 + j, sib)
            fwd.start()
            sent.append(fwd)
        for j, (cx, cy) in enumerate(chips):
            sj = 2 * cx + cy
            _remote(w_r.at[:, other], wall_o.at[sj, :, other], send_sems, recv_sems, 3 + j, sib).wait_recv()
            _remote(m_r, mall_o.at[sj], send_sems, recv_sems, 6 + j, sib).wait_recv()
        for cp in sent:
            cp.wait_send()

    return pl.pallas_call(
        body, name=name,
        out_shape=[_sds((4, nl, LAYER_ROWS, 128), CDT), _sds((4, META_ROWS, 128), F32)],
        in_specs=[ANY, ANY], out_specs=[ANY, ANY],
        scratch_shapes=[pltpu.SemaphoreType.DMA((9,)), pltpu.SemaphoreType.DMA((9,))],
    )(wflat, mflat)


def _swap_only(gfl, name):
    def body(g_r, got_o, send_sem, recv_sem):
        x, y, c, _ = _place()
        theirs = pl.ds(pl.multiple_of((1 - c) * HALF_ROWS, 16), HALF_ROWS)
        cp = _remote(g_r.at[:, :, theirs, :], got_o, send_sem, recv_sem, 0, (x, y, 1 - c))
        cp.start()
        cp.wait()

    return pl.pallas_call(
        body, name=name, out_shape=_sds((4, gfl.shape[1], HALF_ROWS, 128), CDT),
        in_specs=[ANY], out_specs=ANY,
        scratch_shapes=[pltpu.SemaphoreType.DMA((1,)), pltpu.SemaphoreType.DMA((1,))],
    )(gfl)


def _swap_halves(gfl, gsm, name):
    def body(g_r, s_r, got_o, slots_o, send_sems, recv_sems, loc_sem):
        x, y, c, _ = _place()
        me = 4 * x + 2 * y + c
        theirs = pl.ds(pl.multiple_of((1 - c) * HALF_ROWS, 16), HALF_ROWS)
        loc = pltpu.make_async_copy(s_r, slots_o.at[me], loc_sem.at[0])
        loc.start()
        sent = [_remote(g_r.at[:, :, theirs, :], got_o, send_sems, recv_sems, 0, (x, y, 1 - c))]
        for k in range(1, 8):
            px, py, pc = x ^ (k >> 2), y ^ ((k >> 1) & 1), c ^ (k & 1)
            sent.append(_remote(s_r, slots_o.at[me], send_sems, recv_sems, k, (px, py, pc)))
        for cp in sent:
            cp.start()
        _remote(g_r.at[:, :, theirs, :], got_o, send_sems, recv_sems, 0, (x, y, 1 - c)).wait_recv()
        for k in range(1, 8):
            px, py, pc = x ^ (k >> 2), y ^ ((k >> 1) & 1), c ^ (k & 1)
            _remote(s_r, slots_o.at[4 * px + 2 * py + pc], send_sems, recv_sems, k, (px, py, pc)).wait_recv()
        for cp in sent:
            cp.wait_send()
        loc.wait()

    return pl.pallas_call(
        body, name=name,
        out_shape=[_sds((4, gfl.shape[1], HALF_ROWS, 128), CDT), _sds((8, SMALL_ROWS, 128), F32)],
        in_specs=[ANY, ANY], out_specs=[ANY, ANY],
        scratch_shapes=[pltpu.SemaphoreType.DMA((8,)), pltpu.SemaphoreType.DMA((8,)), pltpu.SemaphoreType.DMA((1,))],
    )(gfl, gsm)


def _scatter_shards(ps, name):
    def body(p_r, got_o, send_sems, recv_sems):
        x, y, c, chips = _place()
        sent = [_remote(p_r.at[2 * cx + cy], got_o.at[j], send_sems, recv_sems, j, (cx, cy, c))
                for j, (cx, cy) in enumerate(chips)]
        for cp in sent:
            cp.start()
        for j in range(3):
            _remote(p_r.at[0], got_o.at[j], send_sems, recv_sems, j, (x, y, c)).wait_recv()
        for cp in sent:
            cp.wait_send()

    return pl.pallas_call(
        body, name=name, out_shape=_sds((3, ps.shape[1], HALF_ROWS, 128), CDT),
        in_specs=[ANY], out_specs=ANY,
        scratch_shapes=[pltpu.SemaphoreType.DMA((3,)), pltpu.SemaphoreType.DMA((3,))],
    )(ps)


def _join_halves(red_half, name):
    def body(h_r, got_o, send_sem, recv_sem):
        x, y, c, _ = _place()
        out = _remote(h_r, got_o, send_sem, recv_sem, 0, (x, y, 1 - c))
        out.start()
        out.wait()

    return pl.pallas_call(
        body, name=name, out_shape=_sds(red_half.shape, F32),
        in_specs=[ANY], out_specs=ANY,
        scratch_shapes=[pltpu.SemaphoreType.DMA((1,)), pltpu.SemaphoreType.DMA((1,))],
    )(red_half)


def _pack_shard(local):
    pad = [jnp.zeros((LAYER_ROWS - LAYER_RAW, 128), local[SHARD_ITEMS[0][0]].dtype)]
    return jnp.stack([jnp.concatenate([local[nm][l].reshape(-1, 128) for nm, _, _ in SHARD_ITEMS] + pad)
                      for l in range(2)])


def _unpack_layer(wall):
    ws = {}
    off = 0
    for nm, (r, c), kind in SHARD_ITEMS:
        n = r * c // 128
        piece = wall[:, off:off + n, :].reshape(4, r, c)
        off += n
        if kind == "row":
            ws[nm] = piece.reshape(4 * r, c)
        else:
            ws[nm] = jnp.concatenate([piece[s] for s in range(4)], axis=1)
    return ws


def _mix_cols(w):
    return jnp.concatenate([w[:, 2312:4360], w[:, 0:1536], w[:, 1544:2312], w[:, 1536:1544],
                            jnp.zeros((w.shape[0], DP - D_IN), w.dtype)], axis=1)


def _unmix_cols(w):
    return jnp.concatenate([w[:, QA:QA + 1536], w[:, FA:FA + 8], w[:, QB:QB + 768], w[:, GA:GA + 2048]], axis=1)


def _pack_grads(grads):
    pad = [jnp.zeros((LAYER_ROWS - LAYER_RAW, 128), CDT)]
    shards = []
    for s in range(4):
        parts = []
        for nm, (r, c), kind in SHARD_ITEMS:
            g = grads[nm]
            parts.append((g[s * r:(s + 1) * r] if kind == "row" else g[:, s * c:(s + 1) * c]).reshape(-1, 128))
        shards.append(jnp.concatenate(parts + pad)[None])
    return jnp.stack(shards)


def _unpack_shard(red):
    out = {}
    off = 0
    for nm, (r, c), _ in SHARD_ITEMS:
        n = r * c // 128
        out[nm] = red[:, off:off + n, :].reshape(2, r, c)
        off += n
    return out


def _rows128(a, rows):
    flat = a.reshape(-1)
    return jnp.pad(flat, (0, rows * 128 - flat.shape[0])).reshape(rows, 128)


GRAD_FORM = {"ffn1_w_in": "col", "ffn2_w_in": "col", "w_branch_fox": "col", "w_branch_swa": "col",
             "ffn1_w_out": "3d", "ffn2_w_out": "3d", "w_out": "3d", "w_in": "3d"}
SUM_TILE = {1024: 128, 704: 176, 512: 128, 256: 128}
NT = len(SHARD_ITEMS)


def _half_rows(c, r):
    return pl.ds(pl.multiple_of(c * (r // 2), 16), r // 2)


def _ici_copies(kind, srcs, dsts, send_sems, recv_sems, layer):
    x, y, c, chips = _place()
    s = 2 * x + y
    sends, recvs = [], []
    for t, ((nm, (r, cc), _), src, dst) in enumerate(zip(SHARD_ITEMS, srcs, dsts)):
        for j, (cx, cy) in enumerate(chips):
            sj = 2 * cx + cy
            k = 3 * t + j
            to = (cx, cy, c)
            if kind == "gather":
                hs = _half_rows(c, r)
                sends.append(_remote(src.at[layer, hs], dst.at[s, hs], send_sems, recv_sems, k, to))
                recvs.append(_remote(src.at[layer, hs], dst.at[sj, hs], send_sems, recv_sems, k, to))
            else:
                if GRAD_FORM[nm] == "col":
                    piece = src.at[:, pl.ds(pl.multiple_of(sj * cc, 128), cc)]
                else:
                    piece = src.at[sj]
                sends.append(_remote(piece, dst.at[j], send_sems, recv_sems, k, to))
                recvs.append(sends[-1])
    return sends, recvs


def _slab_shapes():
    return [_sds((4, r, c), CDT) for _, (r, c), _ in SHARD_ITEMS]


def _dma_sems(n):
    return [pltpu.SemaphoreType.DMA((n,)), pltpu.SemaphoreType.DMA((n,))]


def _forward_sends(dsts, send_sems, recv_sems):
    x, y, c, chips = _place()
    sends, recvs = [], []
    for t, ((nm, (r, cc), _), dst) in enumerate(zip(SHARD_ITEMS, dsts)):
        for j, (cx, cy) in enumerate(chips):
            sj = 2 * cx + cy
            hs, ho = _half_rows(c, r), _half_rows(1 - c, r)
            sends.append(_remote(dst.at[sj, hs], dst.at[sj, hs], send_sems, recv_sems, 3 * t + j, (x, y, 1 - c)))
            recvs.append(_remote(dst.at[sj, ho], dst.at[sj, ho], send_sems, recv_sems, 3 * t + j, (x, y, 1 - c)))
    return sends, recvs


def _gather_layer(wb, mflat, layer, name):
    def body(*refs):
        srcs, m_r, dsts, mall_o = refs[:NT], refs[NT], refs[NT + 1:2 * NT + 1], refs[2 * NT + 1]
        send_sems, recv_sems, fsend, frecv, msend, mrecv = refs[2 * NT + 2:]
        x, y, c, chips = _place()
        s = 2 * x + y
        sends, recvs = _ici_copies("gather", srcs, dsts, send_sems, recv_sems, layer)
        metas = [_remote(m_r, mall_o.at[s], msend, mrecv, j, (cx, cy, c)) for j, (cx, cy) in enumerate(chips)]
        for cp in sends + metas:
            cp.start()
        fwds, frecvs = _forward_sends(dsts, fsend, frecv)
        for got, fwd in zip(recvs, fwds):
            got.wait_recv()
            fwd.start()
        for got in frecvs:
            got.wait_recv()
        for j, (cx, cy) in enumerate(chips):
            _remote(m_r, mall_o.at[2 * cx + cy], msend, mrecv, j, (cx, cy, c)).wait_recv()
        for cp in sends + metas + fwds:
            cp.wait_send()

    return pl.pallas_call(
        body, name=name, out_shape=_slab_shapes() + [_sds((4, META_ROWS, 128), F32)],
        in_specs=[ANY] * (NT + 1), out_specs=[ANY] * (NT + 1),
        scratch_shapes=_dma_sems(3 * NT) + _dma_sems(3 * NT) + _dma_sems(3),
    )(*wb, mflat)


def _forward_layer(slabs, name):
    def body(*refs):
        ins, outs, send_sems, recv_sems = refs[:NT], refs[NT:2 * NT], refs[2 * NT], refs[2 * NT + 1]
        sends, recvs = _forward_sends(outs, send_sems, recv_sems)
        for cp in sends:
            cp.start()
        for cp in recvs:
            cp.wait_recv()
        for cp in sends:
            cp.wait_send()

    return pl.pallas_call(
        body, name=name, out_shape=_slab_shapes(), in_specs=[ANY] * NT, out_specs=[ANY] * NT,
        input_output_aliases={t: t for t in range(NT)}, scratch_shapes=_dma_sems(3 * NT),
    )(*slabs)


def _half_shape(nm, r, c):
    return (r // 2, 4 * c) if GRAD_FORM[nm] == "col" else (4, r // 2, c)


def _swap_layer(gs, gsm, name):
    small = gsm is not None

    def body(*refs):
        g_rs = refs[:NT]
        pos = NT
        if small:
            s_r = refs[pos]
            pos += 1
        got_os = refs[pos:pos + NT]
        pos += NT
        if small:
            slots_o = refs[pos]
            pos += 1
        send_sems, recv_sems = refs[pos], refs[pos + 1]
        x, y, c, _ = _place()
        sib = (x, y, 1 - c)
        sent = []
        for t, ((nm, (r, cc), _), g_r, got_o) in enumerate(zip(SHARD_ITEMS, g_rs, got_os)):
            ho = _half_rows(1 - c, r)
            src = g_r.at[ho, :] if GRAD_FORM[nm] == "col" else g_r.at[:, ho, :]
            sent.append(_remote(src, got_o, send_sems, recv_sems, t, sib))
        if small:
            ssend, srecv, loc_sem = refs[pos + 2], refs[pos + 3], refs[pos + 4]
            me = 4 * x + 2 * y + c
            loc = pltpu.make_async_copy(s_r, slots_o.at[me], loc_sem.at[0])
            loc.start()
            peers = [(x ^ (k >> 2), y ^ ((k >> 1) & 1), c ^ (k & 1)) for k in range(1, 8)]
            for k, peer in enumerate(peers):
                sent.append(_remote(s_r, slots_o.at[me], ssend, srecv, k, peer))
        for cp in sent:
            cp.start()
        for cp in sent[:NT]:
            cp.wait_recv()
        if small:
            for k, (px, py, pc) in enumerate(peers):
                _remote(s_r, slots_o.at[4 * px + 2 * py + pc], ssend, srecv, k, (px, py, pc)).wait_recv()
        for cp in sent:
            cp.wait_send()
        if small:
            loc.wait()

    outs = [_sds(_half_shape(nm, r, c), CDT) for nm, (r, c), _ in SHARD_ITEMS]
    ops = list(gs)
    sems = _dma_sems(NT)
    if small:
        outs.append(_sds((8, SMALL_ROWS, 128), F32))
        ops.append(gsm)
        sems = sems + _dma_sems(7) + [pltpu.SemaphoreType.DMA((1,))]
    res = pl.pallas_call(
        body, name=name, out_shape=outs, in_specs=[ANY] * len(ops), out_specs=[ANY] * len(outs), scratch_shapes=sems,
    )(*ops)
    return (res[:NT], res[NT]) if small else (res, None)


def _pair_add_t(own, got, half_idx, nm, r, name):
    tr = SUM_TILE[r]
    nb = (r // 2) // tr
    if GRAD_FORM[nm] == "col":
        blk = (tr, own.shape[1])
        own_spec = pl.BlockSpec(blk, lambda i, c_r: (c_r[0] * nb + i, 0))
        half_spec = pl.BlockSpec(blk, lambda i, c_r: (i, 0))
    else:
        blk = (4, tr, own.shape[2])
        own_spec = pl.BlockSpec(blk, lambda i, c_r: (0, c_r[0] * nb + i, 0))
        half_spec = pl.BlockSpec(blk, lambda i, c_r: (0, i, 0))

    def body(c_r, a_r, b_r, o_o):
        o_o[...] = (a_r[...].astype(F32) + b_r[...].astype(F32)).astype(CDT)

    grid_spec = pltpu.PrefetchScalarGridSpec(num_scalar_prefetch=1, grid=(nb,), in_specs=[own_spec, half_spec],
                                             out_specs=half_spec)
    return pl.pallas_call(body, name=name, grid_spec=grid_spec, out_shape=_sds(got.shape, CDT),
                          compiler_params=_params(("parallel",)))(half_idx, own, got)


def _sum4_t(ps, got3, buf, idx, layer, nm, r, name):
    tr = SUM_TILE[r]
    nb = (r // 2) // tr
    c = got3.shape[2]
    if GRAD_FORM[nm] == "col":
        ps_spec = pl.BlockSpec((tr, c), lambda i, x_r: (i, x_r[0]))
    else:
        ps_spec = pl.BlockSpec((None, tr, c), lambda i, x_r: (x_r[0], i, 0))

    def body(x_r, a_r, b_r, buf_r, o_o):
        o_o[...] = ((a_r[...].astype(F32) + b_r[0].astype(F32)) + b_r[1].astype(F32)) + b_r[2].astype(F32)

    grid_spec = pltpu.PrefetchScalarGridSpec(
        num_scalar_prefetch=1, grid=(nb,),
        in_specs=[ps_spec, pl.BlockSpec((3, tr, c), lambda i, x_r: (0, i, 0)), ANY],
        out_specs=pl.BlockSpec((None, tr, c), lambda i, x_r: (layer, x_r[1] * nb + i, 0)),
    )
    return pl.pallas_call(body, name=name, grid_spec=grid_spec, out_shape=_sds(buf.shape, F32),
                          input_output_aliases={3: 0}, compiler_params=_params(("parallel",)))(idx, ps, got3, buf)


def _scatter_layer(ps, name):
    def body(*refs):
        srcs, dsts, send_sems, recv_sems = refs[:NT], refs[NT:2 * NT], refs[2 * NT], refs[2 * NT + 1]
        sends, recvs = _ici_copies("scatter", srcs, dsts, send_sems, recv_sems, None)
        for cp in sends:
            cp.start()
        for cp in recvs:
            cp.wait_recv()
        for cp in sends:
            cp.wait_send()

    return pl.pallas_call(
        body, name=name, out_shape=_got3_shapes(), in_specs=[ANY] * NT, out_specs=[ANY] * NT,
        scratch_shapes=_dma_sems(3 * NT),
    )(*ps)


def _got3_shapes():
    return [_sds((3, r // 2, c), CDT) for _, (r, c), _ in SHARD_ITEMS]


def _join_layer(bufs, name):
    def body(*refs):
        ins, outs, send_sems, recv_sems = refs[:NT], refs[NT:2 * NT], refs[2 * NT], refs[2 * NT + 1]
        x, y, c, _ = _place()
        sent = []
        for t, ((nm, (r, cc), _), b_o) in enumerate(zip(SHARD_ITEMS, outs)):
            hs = _half_rows(c, r)
            sent.append(_remote(b_o.at[:, hs, :], b_o.at[:, hs, :], send_sems, recv_sems, t, (x, y, 1 - c)))
        for cp in sent:
            cp.start()
        for t, ((nm, (r, cc), _), b_o) in enumerate(zip(SHARD_ITEMS, outs)):
            ho = _half_rows(1 - c, r)
            _remote(b_o.at[:, ho, :], b_o.at[:, ho, :], send_sems, recv_sems, t, (x, y, 1 - c)).wait_recv()
        for cp in sent:
            cp.wait_send()

    return pl.pallas_call(
        body, name=name, out_shape=[_sds(b.shape, F32) for b in bufs], in_specs=[ANY] * NT, out_specs=[ANY] * NT,
        input_output_aliases={t: t for t in range(NT)}, scratch_shapes=_dma_sems(NT),
    )(*bufs)


def _adamw3(w, g, m, v, name):
    nl, r, c = w.shape
    tr = SUM_TILE.get(r, r)

    def body(w_r, g_r, m_r, v_r, d_o, m_o, v_o):
        g_ = g_r[...]
        m_ = ADAM_B1 * m_r[...] + (1.0 - ADAM_B1) * g_
        v_ = ADAM_B2 * v_r[...] + (1.0 - ADAM_B2) * jnp.square(g_)
        m_hat = m_ / (1.0 - ADAM_B1 ** ADAM_STEP)
        v_hat = v_ / (1.0 - ADAM_B2 ** ADAM_STEP)
        d_o[...] = -ADAM_LR * (m_hat / (jnp.sqrt(v_hat) + ADAM_EPS) + ADAM_WD * w_r[...])
        m_o[...] = m_
        v_o[...] = v_

    blk = pl.BlockSpec((None, tr, c), lambda l, i: (l, i, 0))
    return pl.pallas_call(
        body, name=name, grid=(nl, r // tr),
        in_specs=[blk] * 4, out_specs=[blk] * 3, out_shape=[_sds((nl, r, c), F32)] * 3,
        compiler_params=_params(("parallel", "parallel")),
    )(w, g, m, v)


def _full_weights(slabs, wb, layer, shard):
    ws = {}
    for (nm, (r, c), kind), slab in zip(SHARD_ITEMS, slabs):
        slab = lax.dynamic_update_slice(slab, wb[nm][layer][None], (shard, 0, 0))
        ws[nm] = slab.reshape(4 * r, c) if kind == "row" else jnp.concatenate([slab[s] for s in range(4)], axis=1)
    return ws


def _exchange_forms(g):
    out = []
    for nm, (r, c), _ in SHARD_ITEMS:
        a = g[nm]
        if nm == "w_in":
            a = a.reshape(D, 4, c).transpose(1, 0, 2)
        elif GRAD_FORM[nm] == "3d":
            a = a.reshape(4, r, c)
        out.append(a)
    return out


SMALL_ITEMS = (("rel_bias_table", 2), ("ffn1_norm", 16), ("mix_norm", 16), ("ffn2_norm", 16), ("forget_bias", 1),
               ("fox_q_norm", 1), ("fox_k_norm", 1), ("swa_q_norm", 1), ("swa_k_norm", 1), ("swa_sinks", 1))
SMALL_ADAM_ROWS = 96


def _layer_fwd(h, lw, l, ride=None):
    sv = {"h0": h}
    a, sv["a1t"] = _rms_fwd(h, lw["ffn1_norm"], f"rms_fwd_a{l}")
    sv["gu1"], s, sv["s1t"] = _ffn_in(a, lw["ffn1_w_in"], f"ffn_in_a{l}")
    h = _mm_res(s, lw["ffn1_w_out"], h, 0.5, f"ffn_out_a{l}")
    sv["h1"] = h
    a, sv["amt"] = _rms_fwd(h, lw["mix_norm"], f"rms_fwd_m{l}")
    proj = _mm(a, lw["w_mix"], F32, _row_tile(h.shape[0]), 640, f"proj{l}", b_resident=True)
    sv["proj"] = proj
    qf, kf, vf, qs, kse, vse, c, ct, sv["qft"] = _qknorm_fwd(proj, lw["gfq"], lw["gfk"], lw["gsq"], lw["gsk"], lw["fb"],
                                                              f"qknorm_fwd{l}")
    ofox, lse_f, *rode = _fox_fwd(qf, kf, vf, c, ct, f"fox_fwd{l}", ride)
    oswa, lse_s = _swa_fwd(qs, kse, vse, lw["bias"], lw["sinks"], f"swa_fwd{l}")
    sv.update(qf=qf, kf=kf, vf=vf, qs=qs, kse=kse, vse=vse, c=c, ct=ct, ofox=ofox, oswa=oswa, lse_f=lse_f, lse_s=lse_s)
    y, sv["yt"], sv["pf"], sv["ps"], sv["oft"], sv["ost"] = _gate_fwd(ofox, oswa, lw["w_branch_fox"], lw["w_branch_swa"],
                                                                     proj, f"gate_fwd{l}")
    h = _mm_res(y, lw["w_out"], h, 1.0, f"mix_out{l}")
    sv["h2"] = h
    a, sv["a2t"] = _rms_fwd(h, lw["ffn2_norm"], f"rms_fwd_b{l}")
    sv["gu2"], s, sv["s2t"] = _ffn_in(a, lw["ffn2_w_in"], f"ffn_in_b{l}")
    h = _mm_res(s, lw["ffn2_w_out"], h, 0.5, f"ffn_out_b{l}")
    return h, sv, rode


def _ffn_bwd(dh, dhb, h_in, at, gu, st, norm, w_in, w_out, tag):
    dgu = _ffn_bwd_mid(dhb, w_out, gu, f"ffn_bwd_mid_{tag}")
    d_w_out = _mm(st, dhb, CDT, 256, 512, f"dw_ffn_out_{tag}", scale=0.5)
    da = _ffn_bwd_in(dgu, w_in, f"ffn_bwd_in_{tag}")
    d_w_in = _mm(at, dgu, CDT, 512, 256, f"dw_ffn_in_{tag}")
    dh, dhb, dg = _rms_bwd(da, h_in, norm, dh, f"rms_bwd_{tag}")
    return dh, dhb, d_w_out, d_w_in, dg


def _layer_bwd(dh, dhb, sv, lw, l, ride=None):
    g = {}
    dh, dhb, g["ffn2_w_out"], g["ffn2_w_in"], g["ffn2_norm"] = _ffn_bwd(
        dh, dhb, sv["h2"], sv["a2t"], sv["gu2"], sv["s2t"], lw["ffn2_norm"], lw["ffn2_w_in"], lw["ffn2_w_out"], f"b{l}")
    dy = _mm_nt(dhb, lw["w_out"], f"d_y{l}")
    g["w_out"] = _mm(sv["yt"], dhb, CDT, 512, 512, f"dw_out{l}")
    dpf, dps, dga, dgb = _gate_bwd(dy, sv["pf"], sv["ps"], sv["proj"], f"gate_bwd{l}")
    do_f, do_ft = _mm_nt(dpf, lw["w_branch_fox"], f"d_ofox{l}", with_t=True)
    do_s = _mm_nt(dps, lw["w_branch_swa"], f"d_oswa{l}")
    g["w_branch_fox"] = _mm(sv["oft"], dpf, CDT, 512, 512, f"dw_bfox{l}")
    g["w_branch_swa"] = _mm(sv["ost"], dps, CDT, 512, 512, f"dw_bswa{l}")
    dqf, dcq, dkf, dvf, dck, *rode = _fox_bwd(sv["qf"], sv["qft"], sv["kf"], sv["vf"], sv["c"], sv["ct"], sv["ofox"],
                                              sv["lse_f"], do_f, do_ft, f"fox_bwd{l}", ride)
    g["rode"] = rode
    dqs, dkse, dvse, dbias, dsk = _swa_bwd(sv["qs"], sv["kse"], sv["vse"], lw["bias"], lw["sinks"], sv["oswa"],
                                           sv["lse_s"], do_s, f"swa_bwd{l}")
    dproj, dgn = _qknorm_bwd(sv["proj"], dqf, dkf, dvf, dqs, dkse, dvse, dcq, dck, dga, dgb,
                             lw["gfq"], lw["gfk"], lw["gsq"], lw["gsk"], lw["fb"], f"qknorm_bwd{l}")
    dam = _mm_nt(dproj, lw["w_mix"], f"d_am{l}")
    g["w_mix"] = _mm(sv["amt"], dproj, CDT, 512, 640, f"dw_mix{l}")
    dh, dhb, g["mix_norm"] = _rms_bwd(dam, sv["h1"], lw["mix_norm"], dh, f"rms_bwd_m{l}")
    g["dbias"], g["dsk"], g["dgn"] = dbias, dsk, dgn
    dh, dhb, g["ffn1_w_out"], g["ffn1_w_in"], g["ffn1_norm"] = _ffn_bwd(
        dh, dhb, sv["h0"], sv["a1t"], sv["gu1"], sv["s1t"], lw["ffn1_norm"], lw["ffn1_w_in"], lw["ffn1_w_out"], f"a{l}")
    return dh, dhb, g


def kernel(x, meta_tokens, rel_bias_table, ffn1_norm, ffn1_w_in, ffn1_w_out, mix_norm, w_in, forget_bias, fox_q_norm, fox_k_norm, swa_q_norm, swa_k_norm, swa_sinks, w_branch_fox, w_branch_swa, w_out, ffn2_norm, ffn2_w_in, ffn2_w_out, loss_target, m_meta_tokens, m_rel_bias_table, m_ffn1_norm, m_ffn1_w_in, m_ffn1_w_out, m_mix_norm, m_w_in, m_forget_bias, m_fox_q_norm, m_fox_k_norm, m_swa_q_norm, m_swa_k_norm, m_swa_sinks, m_w_branch_fox, m_w_branch_swa, m_w_out, m_ffn2_norm, m_ffn2_w_in, m_ffn2_w_out, v_meta_tokens, v_rel_bias_table, v_ffn1_norm, v_ffn1_w_in, v_ffn1_w_out, v_mix_norm, v_w_in, v_forget_bias, v_fox_q_norm, v_fox_k_norm, v_swa_q_norm, v_swa_k_norm, v_swa_sinks, v_w_branch_fox, v_w_branch_swa, v_w_out, v_ffn2_norm, v_ffn2_w_in, v_ffn2_w_out):
    names = ["meta_tokens", "rel_bias_table", "ffn1_norm", "ffn1_w_in", "ffn1_w_out", "mix_norm", "w_in", "forget_bias",
             "fox_q_norm", "fox_k_norm", "swa_q_norm", "swa_k_norm", "swa_sinks", "w_branch_fox", "w_branch_swa", "w_out",
             "ffn2_norm", "ffn2_w_in", "ffn2_w_out"]
    w = dict(zip(names, [meta_tokens, rel_bias_table, ffn1_norm, ffn1_w_in, ffn1_w_out, mix_norm, w_in, forget_bias,
                         fox_q_norm, fox_k_norm, swa_q_norm, swa_k_norm, swa_sinks, w_branch_fox, w_branch_swa, w_out,
                         ffn2_norm, ffn2_w_in, ffn2_w_out]))
    m = dict(zip(names, [m_meta_tokens, m_rel_bias_table, m_ffn1_norm, m_ffn1_w_in, m_ffn1_w_out, m_mix_norm, m_w_in,
                         m_forget_bias, m_fox_q_norm, m_fox_k_norm, m_swa_q_norm, m_swa_k_norm, m_swa_sinks,
                         m_w_branch_fox, m_w_branch_swa, m_w_out, m_ffn2_norm, m_ffn2_w_in, m_ffn2_w_out]))
    v = dict(zip(names, [v_meta_tokens, v_rel_bias_table, v_ffn1_norm, v_ffn1_w_in, v_ffn1_w_out, v_mix_norm, v_w_in,
                         v_forget_bias, v_fox_q_norm, v_fox_k_norm, v_swa_q_norm, v_swa_k_norm, v_swa_sinks,
                         v_w_branch_fox, v_w_branch_swa, v_w_out, v_ffn2_norm, v_ffn2_w_in, v_ffn2_w_out]))
    xi, yi, ci = lax.axis_index("x"), lax.axis_index("y"), lax.axis_index("c")
    shard = 2 * xi + yi
    seq = x.shape[1]
    t = seq + BLK

    wb = {nm: w[nm].astype(CDT) for nm, _, _ in SHARD_ITEMS}
    wb_list = [wb[nm] for nm, _, _ in SHARD_ITEMS]
    mflat = meta_tokens.reshape(META_ROWS, 128)
    *slabs0, mall = _gather_layer(wb_list, mflat, 0, "gather_weights")
    mall = lax.dynamic_update_slice(mall, mflat[None], (shard, 0, 0))
    meta_full = jnp.concatenate([mall[s].reshape(N_META, 256) for s in range(4)], axis=1)
    bias = _bias_fwd(rel_bias_table, "bias_fwd")

    def layer_weights(slabs, l):
        lw = _full_weights(slabs, wb, l, shard)
        lw["w_mix"] = _mix_cols(lw.pop("w_in"))
        for nm in ("ffn1_norm", "mix_norm", "ffn2_norm"):
            lw[nm] = w[nm][l].reshape(1, D)
        lw["gfq"] = jnp.tile(fox_q_norm[l], 8).reshape(1, 512)
        lw["gfk"] = jnp.tile(fox_k_norm[l], 8).reshape(1, 512)
        lw["gsq"] = jnp.tile(swa_q_norm[l], 8).reshape(1, 512)
        lw["gsk"] = jnp.tile(swa_k_norm[l], 2).reshape(1, 128)
        lw["fb"] = jnp.pad(forget_bias[l], (0, 120)).reshape(1, 128)
        lw["sinks"] = swa_sinks[l]
        lw["bias"] = bias
        return lw

    h = jnp.concatenate([jnp.zeros((PAD, D), F32), meta_full, x[0]], axis=0)
    lws = [layer_weights(slabs0, 0)]
    h, sv0, slabs1 = _layer_fwd(h, lws[0], 0, ("gather", wb_list, _slab_shapes(), 1))
    lws.append(layer_weights(_forward_layer(slabs1, "forward_halves"), 1))
    h, sv1, _ = _layer_fwd(h, lws[1], 1)
    saved = [sv0, sv1]
    dh, dhb, lacc = _loss(h, loss_target[0], "loss")
    loss = lax.psum(lacc[0, 0], ("x", "y", "c"))

    half_idx = ci.reshape(1).astype(jnp.int32)
    place_idx = jnp.stack([shard, ci]).astype(jnp.int32)

    def pair_sums(g, gsm, tag):
        g["w_in"] = _unmix_cols(g.pop("w_mix"))
        forms = _exchange_forms(g)
        got, slots = _swap_layer(forms, gsm, f"swap_halves{tag}")
        return [_pair_add_t(a, b, half_idx, nm, r, f"pair_add{tag}_{nm}")
                for a, b, (nm, (r, c), _) in zip(forms, got, SHARD_ITEMS)], slots

    grads = [None, None]
    dh, dhb, grads[1] = _layer_bwd(dh, dhb, saved[1], lws[1], 1)
    ps1, _ = pair_sums(grads[1], None, 1)
    dh, dhb, grads[0] = _layer_bwd(dh, dhb, saved[0], lws[0], 0, ("scatter", ps1, _got3_shapes(), None))
    grad_x = dh[BLK:].reshape(1, seq, D)
    dtab = _bias_bwd(grads[0]["dbias"] + grads[1]["dbias"], "bias_bwd")

    small = [dh[PAD:BLK].reshape(128, 128), _rows128(dtab[:, :N_BUCKETS].T, 2)]
    for nm in ("ffn1_norm", "mix_norm", "ffn2_norm"):
        small.append(jnp.stack([grads[l][nm][0] for l in range(2)]).reshape(16, 128))
    small.append(_rows128(jnp.stack([grads[l]["dgn"][4, :8] for l in range(2)]), 1))
    for row in range(4):
        small.append(jnp.stack([grads[l]["dgn"][row, :HD] for l in range(2)]).reshape(1, 128))
    dsk = [grads[l]["dsk"][:, 0, :] for l in range(2)]
    small.append(_rows128(jnp.stack([jnp.stack([d[:, 0], d[:, HD]], axis=1).reshape(8) for d in dsk]), 1))
    gsm = jnp.concatenate(small, axis=0)
    gsm = jnp.pad(gsm, ((0, SMALL_ROWS - gsm.shape[0]), (0, 0)))

    ps0, slots = pair_sums(grads[0], gsm, 0)
    got3 = [_scatter_layer(ps0, "scatter_shards"), grads[0]["rode"]]
    bufs = []
    for t, (nm, (r, c), _) in enumerate(SHARD_ITEMS):
        buf = lax.empty((2, r, c), F32)
        for l, ps in ((1, ps1), (0, ps0)):
            buf = _sum4_t(ps[t], got3[l][t], buf, place_idx, l, nm, r, f"sum4_{l}_{nm}")
        bufs.append(buf)
    bufs = _join_layer(bufs, "join_halves")
    gs = _sum8(slots, "sum8")

    g_out = {nm: buf for (nm, _, _), buf in zip(SHARD_ITEMS, bufs)}
    g_out["meta_tokens"] = lax.dynamic_slice(gs[0:128].reshape(N_META, D), (0, shard * 256), (N_META, 256))
    off = 128
    for nm, rows in SMALL_ITEMS:
        n = w[nm].size
        g_out[nm] = gs[off:off + rows].reshape(-1)[:n].reshape(w[nm].shape)
        off += rows

    delta, new_m, new_v = {}, {}, {}
    for nm, _, _ in SHARD_ITEMS:
        delta[nm], new_m[nm], new_v[nm] = _adamw3(w[nm], g_out[nm], m[nm], v[nm], f"adamw_{nm}")
    small_names = ["meta_tokens"] + [nm for nm, _ in SMALL_ITEMS]
    small_rows = [META_ROWS] + [rows for _, rows in SMALL_ITEMS]

    def pack_small(src):
        buf = jnp.concatenate([_rows128(src[nm], rows) for nm, rows in zip(small_names, small_rows)], axis=0)
        return jnp.pad(buf, ((0, SMALL_ADAM_ROWS - buf.shape[0]), (0, 0)))

    d_, m_, v_ = (a[0] for a in _adamw3(pack_small(w)[None], pack_small(g_out)[None], pack_small(m)[None],
                                        pack_small(v)[None], "adamw_small"))
    off = 0
    for nm, rows in zip(small_names, small_rows):
        n = w[nm].size
        for dst, src in ((delta, d_), (new_m, m_), (new_v, v_)):
            dst[nm] = src[off:off + rows].reshape(-1)[:n].reshape(w[nm].shape)
        off += rows

    return (loss, grad_x, *[g_out[n] for n in names], *[delta[n] for n in names],
            *[new_m[n] for n in names], *[new_v[n] for n in names])
```

```python
import math

import numpy as np
import jax
import jax.numpy as jnp
from jax import lax
from jax.experimental import pallas as pl
from jax.experimental.pallas import tpu as pltpu

D = 1024
F = 2816
FT = F // 2
HD = 64
NPAIR = 4
N_META = 16
BLK = 128
PAD = BLK - N_META
EPS = 1e-6
NEG = -1e30
N_BUCKETS = 32
GA, GB, QA, KA, VA, QB, KB, VB, FA, DP = 0, 1024, 2048, 2560, 3072, 3584, 4096, 4224, 4352, 4480
D_IN = 4360
CDT = jnp.bfloat16
F32 = jnp.float32
VMEM_LIMIT = 48 * 1024 * 1024
MESH_ID = pl.DeviceIdType.MESH

ADAM_LR, ADAM_B1, ADAM_B2, ADAM_EPS, ADAM_WD, ADAM_STEP = 0.001, 0.9, 0.999, 1e-08, 0.01, 10

SHARD_ITEMS = (
    ("ffn1_w_in", (1024, 1408), "col"),
    ("ffn1_w_out", (704, 1024), "row"),
    ("w_in", (1024, 1090), "col"),
    ("w_branch_fox", (512, 256), "col"),
    ("w_branch_swa", (512, 256), "col"),
    ("w_out", (256, 1024), "row"),
    ("ffn2_w_in", (1024, 1408), "col"),
    ("ffn2_w_out", (704, 1024), "row"),
)
SMALL_ROWS = 192
META_ROWS = 32


def _row_tile(t):
    return 384 if t % 384 == 0 else 128


def _dot(a, b):
    return jnp.dot(a, b, preferred_element_type=F32)


def _dot_nt(a, b):
    return lax.dot_general(a, b, (((1,), (1,)), ((), ())), preferred_element_type=F32)


def _dot_hi(a, b):
    return jnp.dot(a, b, preferred_element_type=F32, precision=lax.Precision.HIGHEST)


def _sigmoid(x):
    return 1.0 / (1.0 + jnp.exp(-x))


def _iota(shape, dim):
    return lax.broadcasted_iota(jnp.int32, shape, dim)


def _params(sem=None):
    return pltpu.CompilerParams(dimension_semantics=sem, vmem_limit_bytes=VMEM_LIMIT)


def _sds(shape, dtype):
    return jax.ShapeDtypeStruct(shape, dtype)


def _rms_fwd(h, g, name):
    t = h.shape[0]
    tm = _row_tile(t)

    def body(h_ref, g_ref, a_ref, at_ref):
        x = h_ref[...]
        ms = jnp.mean(x * x, axis=-1, keepdims=True)
        a = x * lax.rsqrt(ms + EPS) * g_ref[...]
        a_ref[...] = a.astype(CDT)
        at_ref[...] = a.T.astype(CDT)

    return pl.pallas_call(
        body, name=name, grid=(t // tm,),
        in_specs=[pl.BlockSpec((tm, D), lambda i: (i, 0)), pl.BlockSpec((1, D), lambda i: (0, 0))],
        out_specs=[pl.BlockSpec((tm, D), lambda i: (i, 0)), pl.BlockSpec((D, tm), lambda i: (0, i))],
        out_shape=[_sds((t, D), CDT), _sds((D, t), CDT)],
        compiler_params=_params(("parallel",)),
    )(h, g)


def _rms_bwd(da, h, g, dres, name):
    t = h.shape[0]
    tm = _row_tile(t)

    def body(da_ref, h_ref, g_ref, dr_ref, dh_ref, dhb_ref, dg_ref):
        i = pl.program_id(0)
        x = h_ref[...]
        da_ = da_ref[...]
        r = lax.rsqrt(jnp.mean(x * x, axis=-1, keepdims=True) + EPS)
        xh = x * r
        day = da_ * g_ref[...]
        dx = r * (day - xh * jnp.mean(day * xh, axis=-1, keepdims=True))
        dh = dr_ref[...] + dx
        dh_ref[...] = dh
        dhb_ref[...] = dh.astype(CDT)

        @pl.when(i == 0)
        def _():
            dg_ref[...] = jnp.zeros(dg_ref.shape, F32)

        dg_ref[0:1, :] += jnp.sum(da_ * xh, axis=0, keepdims=True)

    row = pl.BlockSpec((tm, D), lambda i: (i, 0))
    return pl.pallas_call(
        body, name=name, grid=(t // tm,),
        in_specs=[row, row, pl.BlockSpec((1, D), lambda i: (0, 0)), row],
        out_specs=[row, row, pl.BlockSpec((8, D), lambda i: (0, 0))],
        out_shape=[_sds((t, D), F32), _sds((t, D), CDT), _sds((8, D), F32)],
        compiler_params=_params(("arbitrary",)),
    )(da, h, g, dres)


def _ffn_in(a, w_in, name):
    t = a.shape[0]
    tm = _row_tile(t)
    tn = FT
    nj = F // tn

    def body(a_ref, wg_ref, wu_ref, gu_ref, s_ref, st_ref):
        a_ = a_ref[...]
        g = _dot(a_, wg_ref[...])
        u = _dot(a_, wu_ref[...])
        s = g * _sigmoid(g) * u
        gu_ref[0] = g.astype(CDT)
        gu_ref[1] = u.astype(CDT)
        s_ref[...] = s.astype(CDT)
        st_ref[...] = s.T.astype(CDT)

    return pl.pallas_call(
        body, name=name, grid=(nj, t // tm),
        in_specs=[pl.BlockSpec((tm, D), lambda j, i: (i, 0)),
                  pl.BlockSpec((D, tn), lambda j, i: (0, j)),
                  pl.BlockSpec((D, tn), lambda j, i: (0, j + nj))],
        out_specs=[pl.BlockSpec((2, tm, tn), lambda j, i: (0, i, j)),
                   pl.BlockSpec((tm, tn), lambda j, i: (i, j)),
                   pl.BlockSpec((tn, tm), lambda j, i: (j, i))],
        out_shape=[_sds((2, t, F), CDT), _sds((t, F), CDT), _sds((F, t), CDT)],
        compiler_params=_params(("parallel", "parallel")),
    )(a, w_in, w_in)


def _mm_res(a, b, res, scale, name):
    t, k = a.shape
    n = b.shape[1]
    tm = _row_tile(t)
    tn = 512

    def body(a_ref, b_ref, r_ref, o_ref):
        o_ref[...] = r_ref[...] + scale * _dot(a_ref[...], b_ref[...])

    return pl.pallas_call(
        body, name=name, grid=(t // tm, n // tn),
        in_specs=[pl.BlockSpec((tm, k), lambda i, j: (i, 0)),
                  pl.BlockSpec((k, tn), lambda i, j: (0, j)),
                  pl.BlockSpec((tm, tn), lambda i, j: (i, j))],
        out_specs=pl.BlockSpec((tm, tn), lambda i, j: (i, j)),
        out_shape=_sds((t, n), F32),
        compiler_params=_params(("parallel", "parallel")),
    )(a, b, res)


def _mm(a, b, out_dtype, tm, tn, name, scale=1.0, b_resident=False):
    m, k = a.shape
    order = (lambda j, i: (i, j)) if b_resident else (lambda i, j: (i, j))
    if b.ndim == 3:
        nh = b.shape[2] // tn
        n = 2 * b.shape[2]
        b_spec = pl.BlockSpec((None, k, tn), lambda *g: (order(*g)[1] // nh, 0, order(*g)[1] % nh))
    else:
        n = b.shape[1]
        b_spec = pl.BlockSpec((k, tn), lambda *g: (0, order(*g)[1]))

    def body(a_ref, b_ref, o_ref):
        o_ref[...] = (scale * _dot(a_ref[...], b_ref[...])).astype(out_dtype)

    return pl.pallas_call(
        body, name=name, grid=(n // tn, m // tm) if b_resident else (m // tm, n // tn),
        in_specs=[pl.BlockSpec((tm, k), lambda *g: (order(*g)[0], 0)), b_spec],
        out_specs=pl.BlockSpec((tm, tn), lambda *g: order(*g)),
        out_shape=_sds((m, n), out_dtype),
        compiler_params=_params(("parallel", "parallel")),
    )(a, b)


def _mm_nt(a, b, name, with_t=False, tk=512):
    m, n = a.shape
    k = b.shape[0]
    tm = _row_tile(m)

    def body(a_ref, b_ref, o_ref, *t_ref):
        r = _dot_nt(a_ref[...], b_ref[...])
        o_ref[...] = r
        if with_t:
            t_ref[0][...] = r.T.astype(CDT)

    out_specs = [pl.BlockSpec((tm, tk), lambda i, j: (i, j))]
    out_shape = [_sds((m, k), F32)]
    if with_t:
        out_specs.append(pl.BlockSpec((tk, tm), lambda i, j: (j, i)))
        out_shape.append(_sds((k, m), CDT))
    res = pl.pallas_call(
        body, name=name, grid=(m // tm, k // tk),
        in_specs=[pl.BlockSpec((tm, n), lambda i, j: (i, 0)), pl.BlockSpec((tk, n), lambda i, j: (j, 0))],
        out_specs=out_specs, out_shape=out_shape,
        compiler_params=_params(("parallel", "parallel")),
    )(a, b)
    return res if with_t else res[0]


def _ffn_bwd_mid(dhb, w_out, gu, name):
    t = dhb.shape[0]
    tm = _row_tile(t)
    tn = FT

    def body(dh_ref, w_ref, gu_ref, o_ref):
        ds = 0.5 * _dot_nt(dh_ref[...], w_ref[...])
        g = gu_ref[0].astype(F32)
        u = gu_ref[1].astype(F32)
        sg = _sigmoid(g)
        o_ref[0] = (ds * u * (sg * (1.0 + g * (1.0 - sg)))).astype(CDT)
        o_ref[1] = (ds * (g * sg)).astype(CDT)

    return pl.pallas_call(
        body, name=name, grid=(F // tn, t // tm),
        in_specs=[pl.BlockSpec((tm, D), lambda j, i: (i, 0)),
                  pl.BlockSpec((tn, D), lambda j, i: (j, 0)),
                  pl.BlockSpec((2, tm, tn), lambda j, i: (0, i, j))],
        out_specs=pl.BlockSpec((2, tm, tn), lambda j, i: (0, i, j)),
        out_shape=_sds((2, t, F), CDT),
        compiler_params=_params(("parallel", "parallel")),
    )(dhb, w_out, gu)


def _ffn_bwd_in(dgu, w_in, name):
    t = dgu.shape[1]
    tm = _row_tile(t)
    tk = D

    def body(dg_ref, wg_ref, wu_ref, o_ref):
        o_ref[...] = _dot_nt(dg_ref[0], wg_ref[...]) + _dot_nt(dg_ref[1], wu_ref[...])

    return pl.pallas_call(
        body, name=name, grid=(t // tm, D // tk),
        in_specs=[pl.BlockSpec((2, tm, F), lambda i, j: (0, i, 0)),
                  pl.BlockSpec((tk, F), lambda i, j: (j, 0)),
                  pl.BlockSpec((tk, F), lambda i, j: (j, 1))],
        out_specs=pl.BlockSpec((tm, tk), lambda i, j: (i, j)),
        out_shape=_sds((t, D), F32),
        compiler_params=_params(("parallel", "parallel")),
    )(dgu, w_in, w_in)


def _loss(h, target, name):
    t = h.shape[0]

    def body(h_ref, t_ref, dh_ref, dhb_ref, l_ref):
        i = pl.program_id(0)

        @pl.when(i == 0)
        def _():
            l_ref[...] = jnp.zeros(l_ref.shape, F32)
            dh_ref[...] = jnp.zeros(dh_ref.shape, F32)
            dhb_ref[...] = jnp.zeros(dhb_ref.shape, CDT)

        @pl.when(i > 0)
        def _():
            err = h_ref[...] - t_ref[...]
            l_ref[...] += (0.5 / D) * jnp.sum(err * err)
            d = err * (1.0 / D)
            dh_ref[...] = d
            dhb_ref[...] = d.astype(CDT)

    row = pl.BlockSpec((BLK, D), lambda i: (i, 0))
    return pl.pallas_call(
        body, name=name, grid=(t // BLK,),
        in_specs=[row, pl.BlockSpec((BLK, D), lambda i: (jnp.maximum(i - 1, 0), 0))],
        out_specs=[row, row, pl.BlockSpec((8, 128), lambda i: (0, 0))],
        out_shape=[_sds((t, D), F32), _sds((t, D), CDT), _sds((8, 128), F32)],
        compiler_params=_params(("arbitrary",)),
    )(h, target)


def _block_diag():
    return (_iota((128, 128), 0) // HD == _iota((128, 128), 1) // HD).astype(F32)


def _dup_halves(x, lo):
    sw = pltpu.roll(x, 64, 1)
    return jnp.where(lo, x, sw), jnp.where(lo, sw, x)


def _qknorm_fwd(proj, gfq, gfk, gsq, gsk, fb, name):
    t = proj.shape[0]
    tm = _row_tile(t)

    def body(qa, ka, va, qb, kb, vb, fa, gfq_r, gfk_r, gsq_r, gsk_r, fb_r,
             qf_o, kf_o, vf_o, qs_o, kse_o, vse_o, c_o, ct_o, qft_o, carry):
        i = pl.program_id(0)
        bd = _block_diag()
        lane = _iota((1, 128), 1)
        lo = lane < HD

        def hnorm(x, g):
            ms = _dot_hi(x * x, bd) * (1.0 / HD)
            return x * lax.rsqrt(ms + EPS) * g

        for ch in range(4):
            sl = slice(128 * ch, 128 * (ch + 1))
            qn = hnorm(qa[:, sl], gfq_r[:, sl]) * 0.125
            qf_o[:, sl] = qn.astype(CDT)
            qft_o[sl, :] = qn.T.astype(CDT)
            kf_o[:, sl] = hnorm(ka[:, sl], gfk_r[:, sl]).astype(CDT)
            qs_o[:, sl] = (hnorm(qb[:, sl], gsq_r[:, sl]) * 0.125).astype(CDT)
        vf_o[...] = va[...].astype(CDT)
        k0, k1 = _dup_halves(hnorm(kb[...], gsk_r[...]), lo)
        kse_o[0] = k0.astype(CDT)
        kse_o[1] = k1.astype(CDT)
        v0, v1 = _dup_halves(vb[...], lo)
        vse_o[0] = v0.astype(CDT)
        vse_o[1] = v1.astype(CDT)

        z = fa[...] + fb_r[...]
        lf = jnp.minimum(z, 0.0) - jnp.log(1.0 + jnp.exp(-jnp.abs(z)))
        lf = jnp.where(lane < 8, lf, 0.0)
        ltri = (_iota((tm, tm), 1) <= _iota((tm, tm), 0)).astype(F32)

        @pl.when(i == 0)
        def _():
            carry[...] = jnp.zeros(carry.shape, F32)

        c = _dot_hi(ltri, lf) + carry[0:1, :]
        carry[0:1, :] = c[tm - 1:tm, :]
        c_o[...] = c
        ct_o[...] = c.T[0:8, :]

    def col(width, off):
        return pl.BlockSpec((tm, width), lambda i: (i, off // width))

    def vec(width):
        return pl.BlockSpec((1, width), lambda i: (0, 0))

    return pl.pallas_call(
        body, name=name, grid=(t // tm,),
        in_specs=[col(512, QA), col(512, KA), col(512, VA), col(512, QB), col(128, KB), col(128, VB), col(128, FA),
                  vec(512), vec(512), vec(512), vec(128), vec(128)],
        out_specs=[pl.BlockSpec((tm, 512), lambda i: (i, 0))] * 4
        + [pl.BlockSpec((2, tm, 128), lambda i: (0, i, 0))] * 2
        + [pl.BlockSpec((tm, 128), lambda i: (i, 0)), pl.BlockSpec((8, tm), lambda i: (0, i)),
           pl.BlockSpec((512, tm), lambda i: (0, i))],
        out_shape=[_sds((t, 512), CDT)] * 4 + [_sds((2, t, 128), CDT)] * 2
        + [_sds((t, 128), F32), _sds((8, t), F32), _sds((512, t), CDT)],
        scratch_shapes=[pltpu.VMEM((8, 128), F32)],
        compiler_params=_params(("arbitrary",)),
    )(proj, proj, proj, proj, proj, proj, proj, gfq, gfk, gsq, gsk, fb)


def _qknorm_bwd(proj, dqf, dkf, dvf, dqs, dkse, dvse, dcq, dck, dga, dgb, gfq, gfk, gsq, gsk, fb, name):
    t = proj.shape[0]
    tm = _row_tile(t)
    nt = t // tm

    def body(qa, ka, qb, kb, fa, dqf_r, dkf_r, dvf_r, dqs_r, dkse_r, dvse_r, dcq_r, dck_r, dga_r, dgb_r,
             gfq_r, gfk_r, gsq_r, gsk_r, fb_r, dp_o, dgn_o, carry, acc):
        i = pl.program_id(0)
        bd = _block_diag()
        lane = _iota((1, 128), 1)
        lo = lane < HD

        @pl.when(i == 0)
        def _():
            carry[...] = jnp.zeros(carry.shape, F32)
            acc[...] = jnp.zeros(acc.shape, F32)

        def hnorm_bwd(x, g, dy):
            r = lax.rsqrt(_dot_hi(x * x, bd) * (1.0 / HD) + EPS)
            xh = x * r
            day = dy * g
            dx = r * (day - xh * (_dot_hi(day * xh, bd) * (1.0 / HD)))
            return dx, jnp.sum(dy * xh, axis=0, keepdims=True)

        for ch in range(4):
            sl = slice(128 * ch, 128 * (ch + 1))
            dx, dg = hnorm_bwd(qa[:, sl], gfq_r[:, sl], dqf_r[:, sl] * 0.125)
            dp_o[:, QA + 128 * ch:QA + 128 * (ch + 1)] = dx.astype(CDT)
            acc[0:1, sl] += dg
            dx, dg = hnorm_bwd(ka[:, sl], gfk_r[:, sl], dkf_r[:, sl])
            dp_o[:, KA + 128 * ch:KA + 128 * (ch + 1)] = dx.astype(CDT)
            acc[1:2, sl] += dg
            dx, dg = hnorm_bwd(qb[:, sl], gsq_r[:, sl], dqs_r[:, sl] * 0.125)
            dp_o[:, QB + 128 * ch:QB + 128 * (ch + 1)] = dx.astype(CDT)
            acc[2:3, sl] += dg
        dp_o[:, VA:VA + 512] = dvf_r[...].astype(CDT)
        dp_o[:, GA:GA + D] = dga_r[...]
        dp_o[:, GB:GB + D] = dgb_r[...]

        def fold(x):
            e0 = x[0]
            e1 = x[1]
            return jnp.where(lo, e0 + pltpu.roll(e0, 64, 1), e1 + pltpu.roll(e1, 64, 1))

        dx, dg = hnorm_bwd(kb[...], gsk_r[...], fold(dkse_r))
        dp_o[:, KB:KB + 128] = dx.astype(CDT)
        acc[3:4, 0:128] += dg
        dp_o[:, VB:VB + 128] = fold(dvse_r).astype(CDT)

        rr = _iota((512, 128), 0)
        hh = _iota((512, 128), 1)
        sel = ((rr == (hh >> 1) * 128 + (hh & 1) * HD) & (hh < 8)).astype(F32)
        dcs = _dot_hi(dcq_r[...] - dck_r[...], sel)
        utri = (_iota((tm, tm), 1) >= _iota((tm, tm), 0)).astype(F32)
        dlf = _dot_hi(utri, dcs) + carry[0:1, :]
        carry[0:1, :] = dlf[0:1, :]
        z = fa[...] + fb_r[...]
        dfa = jnp.where(lane < 8, dlf * _sigmoid(-z), 0.0)
        dp_o[:, FA:FA + 128] = dfa.astype(CDT)
        acc[4:5, 0:128] += jnp.sum(dfa, axis=0, keepdims=True)

        @pl.when(i == nt - 1)
        def _():
            foldm = ((_iota((512, 128), 0) & (HD - 1)) == _iota((512, 128), 1)).astype(F32)
            dgn_o[...] = _dot_hi(acc[...], foldm)

    def col(width, off):
        return pl.BlockSpec((tm, width), lambda i: (nt - 1 - i, off // width))

    def rows(width):
        return pl.BlockSpec((tm, width), lambda i: (nt - 1 - i, 0))

    def vec(width):
        return pl.BlockSpec((1, width), lambda i: (0, 0))

    pair = pl.BlockSpec((2, tm, 128), lambda i: (0, nt - 1 - i, 0))
    return pl.pallas_call(
        body, name=name, grid=(nt,),
        in_specs=[col(512, QA), col(512, KA), col(512, QB), col(128, KB), col(128, FA),
                  rows(512), rows(512), rows(512), rows(512), pair, pair, rows(512), rows(512), rows(D), rows(D),
                  vec(512), vec(512), vec(512), vec(128), vec(128)],
        out_specs=[rows(DP), pl.BlockSpec((8, 128), lambda i: (0, 0))],
        out_shape=[_sds((t, DP), CDT), _sds((8, 128), F32)],
        scratch_shapes=[pltpu.VMEM((8, 128), F32), pltpu.VMEM((8, 512), F32)],
        compiler_params=_params(("arbitrary",)),
    )(proj, proj, proj, proj, proj, dqf, dkf, dvf, dqs, dkse, dvse, dcq, dck, dga, dgb, gfq, gfk, gsq, gsk, fb)


def _gate_fwd(ofox, oswa, wbf, wbs, proj, name):
    t = ofox.shape[0]
    tm = _row_tile(t)
    tn = 512

    def body(of_r, os_r, wf_r, ws_r, ga_r, gb_r, y_o, yt_o, pf_o, ps_o, oft_o, ost_o):
        j = pl.program_id(1)
        pf = _dot(of_r[...], wf_r[...])
        ps = _dot(os_r[...], ws_r[...])
        y = _sigmoid(ga_r[...]) * pf + _sigmoid(gb_r[...]) * ps
        y_o[...] = y.astype(CDT)
        yt_o[...] = y.T.astype(CDT)
        pf_o[...] = pf.astype(CDT)
        ps_o[...] = ps.astype(CDT)

        @pl.when(j == 0)
        def _():
            oft_o[...] = of_r[...].astype(F32).T.astype(CDT)
            ost_o[...] = os_r[...].astype(F32).T.astype(CDT)

    tile = pl.BlockSpec((tm, tn), lambda i, j: (i, j))
    return pl.pallas_call(
        body, name=name, grid=(t // tm, D // tn),
        in_specs=[pl.BlockSpec((tm, 512), lambda i, j: (i, 0)), pl.BlockSpec((tm, 512), lambda i, j: (i, 0)),
                  pl.BlockSpec((512, tn), lambda i, j: (0, j)), pl.BlockSpec((512, tn), lambda i, j: (0, j)),
                  pl.BlockSpec((tm, tn), lambda i, j: (i, GA // tn + j)),
                  pl.BlockSpec((tm, tn), lambda i, j: (i, GB // tn + j))],
        out_specs=[tile, pl.BlockSpec((tn, tm), lambda i, j: (j, i)), tile, tile,
                   pl.BlockSpec((512, tm), lambda i, j: (0, i)), pl.BlockSpec((512, tm), lambda i, j: (0, i))],
        out_shape=[_sds((t, D), CDT), _sds((D, t), CDT), _sds((t, D), CDT), _sds((t, D), CDT),
                   _sds((512, t), CDT), _sds((512, t), CDT)],
        compiler_params=_params(("parallel", "arbitrary")),
    )(ofox, oswa, wbf, wbs, proj, proj)


def _gate_bwd(dy, pf, ps, proj, name):
    t = dy.shape[0]
    tm = _row_tile(t)
    tn = 512

    def body(dy_r, pf_r, ps_r, ga_r, gb_r, dpf_o, dps_o, dga_o, dgb_o):
        dy_ = dy_r[...]
        sa = _sigmoid(ga_r[...])
        sb = _sigmoid(gb_r[...])
        dpf_o[...] = (dy_ * sa).astype(CDT)
        dps_o[...] = (dy_ * sb).astype(CDT)
        dga_o[...] = (dy_ * pf_r[...].astype(F32) * (sa * (1.0 - sa))).astype(CDT)
        dgb_o[...] = (dy_ * ps_r[...].astype(F32) * (sb * (1.0 - sb))).astype(CDT)

    tile = pl.BlockSpec((tm, tn), lambda i, j: (i, j))
    return pl.pallas_call(
        body, name=name, grid=(t // tm, D // tn),
        in_specs=[tile, tile, tile,
                  pl.BlockSpec((tm, tn), lambda i, j: (i, GA // tn + j)),
                  pl.BlockSpec((tm, tn), lambda i, j: (i, GB // tn + j))],
        out_specs=[tile] * 4,
        out_shape=[_sds((t, D), CDT)] * 4,
        compiler_params=_params(("parallel", "parallel")),
    )(dy, pf, ps, proj, proj)


def _tri_steps(n, by_key):
    if by_key:
        pairs = [(i, j) for j in range(n) for i in range(j, n)]
    else:
        pairs = [(i, j) for i in range(n) for j in range(i + 1)]
    return (np.array([p[0] for p in pairs], np.int32), np.array([p[1] for p in pairs], np.int32))


def _head_col(blk, lane, h):
    return jnp.sum(jnp.where(lane == h, blk, 0.0), axis=1, keepdims=True)


def _head_row(blk, sub, h):
    return jnp.sum(jnp.where(sub == h, blk, 0.0), axis=0, keepdims=True)


def _ride_specs(ride):
    if ride is None:
        return [], [], [], [], []
    kind, srcs, outs, layer = ride
    return list(srcs), [ANY] * len(srcs), list(outs), [ANY] * len(outs), _dma_sems(3 * len(srcs))


def _ride_start(ride, srcs, dsts, send_sems, recv_sems):
    for cp in _ici_copies(ride[0], srcs, dsts, send_sems, recv_sems, ride[3], recv=False)[0]:
        cp.start()


def _ride_wait(ride, srcs, dsts, send_sems, recv_sems):
    sends, recvs = _ici_copies(ride[0], srcs, dsts, send_sems, recv_sems, ride[3])
    for cp in recvs:
        cp.wait_recv()
    for cp in sends:
        cp.wait_send()


def _fox_fwd(qf, kf, vf, c, ct, name, ride=None):
    t = qf.shape[0]
    ta = _row_tile(t)
    qi, kj = _tri_steps(t // ta, by_key=False)
    nsteps = len(qi)
    ride_in, ride_in_specs, ride_out, ride_out_specs, ride_sems = _ride_specs(ride)

    def body(qi_r, kj_r, q_r, k_r, v_r, c_r, ct_r, *rest):
        nr = len(ride_in)
        src_r, (o_o, lse_o), dst_o = rest[:nr], rest[nr:nr + 2], rest[nr + 2:2 * nr + 2]
        m_sc, l_sc, acc_sc, cq_sc, *sems = rest[2 * nr + 2:]
        p = pl.program_id(0)
        n = pl.program_id(1)
        i = qi_r[n]
        j = kj_r[n]
        lane = _iota((1, 128), 1)
        lo = lane < HD

        if ride is not None:
            @pl.when((p == 0) & (n == 0))
            def _():
                _ride_start(ride, src_r, dst_o, *sems)

        @pl.when(j == 0)
        def _():
            m_sc[...] = jnp.full(m_sc.shape, NEG, F32)
            l_sc[...] = jnp.zeros(l_sc.shape, F32)
            acc_sc[...] = jnp.zeros(acc_sc.shape, F32)
            for e in (0, 1):
                cq_sc[e] = jnp.broadcast_to(_head_col(c_r[...], lane, 2 * p + e), (ta, 128))

        def step(masked):
            q = q_r[...]
            k = k_r[...]
            vaug = jnp.concatenate([v_r[...], jnp.ones((ta, 128), CDT)], axis=1)
            if masked:
                rows = i * ta + _iota((ta, 1), 0)
                cols = j * ta + _iota((1, ta), 1)
                mask = (cols <= rows) & (cols >= PAD)
            sub = _iota((8, 1), 0)
            alphas, pvs = [], []
            for e in (0, 1):
                sel = lo if e == 0 else jnp.logical_not(lo)
                s = _dot_nt(jnp.where(sel, q, 0), k)
                ck = _head_row(ct_r[...], sub, 2 * p + e)
                cq = cq_sc[e]
                chunks = []
                for ch in range(ta // 128):
                    sl = slice(128 * ch, 128 * (ch + 1))
                    sc = s[:, sl] + cq - ck[:, sl]
                    if masked:
                        sc = jnp.where(mask[:, sl], sc, NEG)
                    chunks.append(sc)
                mx = chunks[0]
                for sc in chunks[1:]:
                    mx = jnp.maximum(mx, sc)
                m_prev = m_sc[e]
                m_new = jnp.maximum(m_prev, jnp.max(mx, axis=1, keepdims=True))
                alpha = jnp.exp(m_prev - m_new)
                pe = jnp.concatenate([jnp.exp(sc - m_new).astype(CDT) for sc in chunks], axis=1)
                pva = _dot(pe, vaug)
                l_sc[e] = alpha * l_sc[e] + pva[:, 128:]
                m_sc[e] = m_new
                alphas.append(alpha)
                pvs.append(pva[:, :128])
            acc_sc[...] = acc_sc[...] * jnp.where(lo, alphas[0], alphas[1]) + jnp.where(lo, pvs[0], pvs[1])

        edge = (j == i) | (j == 0)

        @pl.when(edge)
        def _():
            step(True)

        @pl.when(jnp.logical_not(edge))
        def _():
            step(False)

        @pl.when(j == i)
        def _():
            l = jnp.where(lo, l_sc[0], l_sc[1])
            o_o[...] = (acc_sc[...] / l).astype(CDT)
            lse_o[...] = jnp.where(lo, m_sc[0], m_sc[1]) + jnp.log(l)

        if ride is not None:
            @pl.when((p == NPAIR - 1) & (n == nsteps - 1))
            def _():
                _ride_wait(ride, src_r, dst_o, *sems)

    qblk = pl.BlockSpec((ta, 128), lambda p, n, qi_r, kj_r: (qi_r[n], p))
    kblk = pl.BlockSpec((ta, 128), lambda p, n, qi_r, kj_r: (kj_r[n], p))
    grid_spec = pltpu.PrefetchScalarGridSpec(
        num_scalar_prefetch=2, grid=(NPAIR, nsteps),
        in_specs=[qblk, kblk, kblk,
                  pl.BlockSpec((ta, 128), lambda p, n, qi_r, kj_r: (qi_r[n], 0)),
                  pl.BlockSpec((8, ta), lambda p, n, qi_r, kj_r: (0, kj_r[n]))] + ride_in_specs,
        out_specs=[qblk, qblk] + ride_out_specs,
        scratch_shapes=[pltpu.VMEM((2, ta, 128), F32), pltpu.VMEM((2, ta, 128), F32), pltpu.VMEM((ta, 128), F32),
                        pltpu.VMEM((2, ta, 128), F32)] + ride_sems,
    )
    return pl.pallas_call(
        body, name=name, grid_spec=grid_spec,
        out_shape=[_sds((t, 512), CDT), _sds((t, 512), F32)] + ride_out,
        compiler_params=_params(("arbitrary", "arbitrary")),
    )(jnp.asarray(qi), jnp.asarray(kj), qf, kf, vf, c, ct, *ride_in)


def _fox_bwd(qf, qft, kf, vf, c, ct, o, lse, do, dot, name, ride=None):
    t = qf.shape[0]
    ta = _row_tile(t)
    nq = t // ta
    qi, kj = _tri_steps(nq, by_key=False)
    nsteps = len(qi)
    ride_in, ride_in_specs, ride_out, ride_out_specs, ride_sems = _ride_specs(ride)

    def body(qi_r, kj_r, q_r, qt_r, k_r, v_r, c_r, ct_r, o_r, lse_r, do_r, dot_r, *rest):
        nr = len(ride_in)
        src_r, (dq_o, dcq_o, dk_o, dv_o, dck_o), dst_o = rest[:nr], rest[nr:nr + 5], rest[nr + 5:2 * nr + 5]
        lse_sc, dl_sc, cq_sc, dq_sc, dcq_sc, dkt_sc, dvt_sc, dckt_sc, *sems = rest[2 * nr + 5:]
        p = pl.program_id(0)
        n = pl.program_id(1)
        i = qi_r[n]
        j = kj_r[n]
        lane = _iota((1, 128), 1)
        lo = lane < HD
        top = _iota((128, 1), 0) < HD

        if ride is not None:
            @pl.when((p == 0) & (n == 0))
            def _():
                _ride_start(ride, src_r, dst_o, *sems)

        @pl.when(n == 0)
        def _():
            dkt_sc[...] = jnp.zeros(dkt_sc.shape, F32)
            dvt_sc[...] = jnp.zeros(dvt_sc.shape, F32)
            dckt_sc[...] = jnp.zeros(dckt_sc.shape, F32)

        @pl.when(j == 0)
        def _():
            dq_sc[...] = jnp.zeros(dq_sc.shape, F32)
            dcq_sc[...] = jnp.zeros(dcq_sc.shape, F32)
            dd = do_r[...] * o_r[...].astype(F32)
            lse = lse_r[...]
            for e in (0, 1):
                sel = lo if e == 0 else jnp.logical_not(lo)
                cq_sc[e] = jnp.broadcast_to(_head_col(c_r[...], lane, 2 * p + e), (ta, 128))
                dl_sc[e] = jnp.broadcast_to(jnp.sum(jnp.where(sel, dd, 0.0), axis=1, keepdims=True), (ta, 128))
                lse_sc[e] = jnp.broadcast_to(lse[:, HD * e:HD * e + 1], (ta, 128))

        def step(masked):
            q = q_r[...]
            qt = qt_r[...]
            k = k_r[...]
            v = v_r[...]
            dob = do_r[...].astype(CDT)
            dot_ = dot_r[...]
            ones = jnp.ones((ta, 128), CDT)
            ones16 = jnp.ones((16, ta), CDT)
            if masked:
                rows = i * ta + _iota((ta, 1), 0)
                cols = j * ta + _iota((1, ta), 1)
                mask = (cols <= rows) & (cols >= PAD)
            sub = _iota((8, 1), 0)
            for e in (0, 1):
                sel = lo if e == 0 else jnp.logical_not(lo)
                rsel = top if e == 0 else jnp.logical_not(top)
                s = _dot_nt(jnp.where(sel, q, 0), k)
                dp = _dot_nt(jnp.where(sel, dob, 0), v)
                ck = _head_row(ct_r[...], sub, 2 * p + e)
                cq, lse_e, dl = cq_sc[e], lse_sc[e], dl_sc[e]
                prs, dss = [], []
                for ch in range(ta // 128):
                    sl = slice(128 * ch, 128 * (ch + 1))
                    sc = s[:, sl] + cq - ck[:, sl]
                    if masked:
                        sc = jnp.where(mask[:, sl], sc, NEG)
                    pr = jnp.exp(sc - lse_e)
                    prs.append(pr.astype(CDT))
                    dss.append((pr * (dp[:, sl] - dl)).astype(CDT))
                pb = jnp.concatenate(prs, axis=1)
                dsb = jnp.concatenate(dss, axis=1)
                dvt_sc[j] += _dot(jnp.where(rsel, dot_, 0), pb)
                dkc = _dot(jnp.concatenate([jnp.where(rsel, qt, 0), ones16], axis=0), dsb)
                dkt_sc[j] += dkc[0:128]
                dckt_sc[j, 0:8, :] += jnp.where(sub == e, dkc[128:136], 0.0)
                dqa = _dot(dsb, jnp.concatenate([jnp.where(sel, k, 0), ones], axis=1))
                dq_sc[...] += dqa[:, :128]
                dcq_sc[e] += dqa[:, 128:]

        edge = (j == i) | (j == 0)

        @pl.when(edge)
        def _():
            step(True)

        @pl.when(jnp.logical_not(edge))
        def _():
            step(False)

        @pl.when(j == i)
        def _():
            dq_o[...] = dq_sc[...]
            dcq_o[...] = jnp.where(lo, dcq_sc[0], dcq_sc[1])

        @pl.when(n == nsteps - 1)
        def _():
            spread = (_iota((128, 128), 1) == _iota((128, 128), 0) // HD).astype(F32)
            for jb in range(nq):
                rs = slice(jb * ta, (jb + 1) * ta)
                dk_o[rs, :] = dkt_sc[jb].T
                dv_o[rs, :] = dvt_sc[jb].T
                dck_o[rs, :] = _dot_hi(spread, dckt_sc[jb]).T

        if ride is not None:
            @pl.when((p == NPAIR - 1) & (n == nsteps - 1))
            def _():
                _ride_wait(ride, src_r, dst_o, *sems)

    qblk = pl.BlockSpec((ta, 128), lambda p, n, qi_r, kj_r: (qi_r[n], p))
    qtblk = pl.BlockSpec((128, ta), lambda p, n, qi_r, kj_r: (p, qi_r[n]))
    kblk = pl.BlockSpec((ta, 128), lambda p, n, qi_r, kj_r: (kj_r[n], p))
    whole = pl.BlockSpec((t, 128), lambda p, n, qi_r, kj_r: (0, p))
    grid_spec = pltpu.PrefetchScalarGridSpec(
        num_scalar_prefetch=2, grid=(NPAIR, nsteps),
        in_specs=[qblk, qtblk, kblk, kblk,
                  pl.BlockSpec((ta, 128), lambda p, n, qi_r, kj_r: (qi_r[n], 0)),
                  pl.BlockSpec((8, ta), lambda p, n, qi_r, kj_r: (0, kj_r[n])),
                  qblk, qblk, qblk, qtblk] + ride_in_specs,
        out_specs=[qblk, qblk, whole, whole, whole] + ride_out_specs,
        scratch_shapes=[pltpu.VMEM((2, ta, 128), F32)] * 3 + [pltpu.VMEM((ta, 128), F32), pltpu.VMEM((2, ta, 128), F32)]
        + [pltpu.VMEM((nq, 128, ta), F32)] * 3 + ride_sems,
    )
    return pl.pallas_call(
        body, name=name, grid_spec=grid_spec,
        out_shape=[_sds((t, 512), F32)] * 5 + ride_out,
        compiler_params=_params(("arbitrary", "arbitrary")),
    )(jnp.asarray(qi), jnp.asarray(kj), qf, qft, kf, vf, c, ct, o, lse, do, dot, *ride_in)


def _bucket_table():
    r = np.arange(BLK)[:, None]
    c = np.arange(3 * BLK)[None, :]
    d = np.where(c < BLK, r + BLK - c, r - (c - BLK))
    n = np.maximum(d, 0)
    max_exact = N_BUCKETS // 2
    nf = np.maximum(n, 1).astype(np.float32)
    large = max_exact + (np.log(nf / max_exact) / math.log(BLK / max_exact) * (N_BUCKETS - max_exact)).astype(np.int32)
    large = np.minimum(large, N_BUCKETS - 1)
    b = np.where(n < max_exact, n, large)
    return np.where(c < 2 * BLK, b, N_BUCKETS - 1).astype(np.int32)


def _bias_fwd(table, name):
    bucket = jnp.asarray(_bucket_table())

    def body(tab_r, b_r, o_o):
        h = pl.program_id(0)
        b = b_r[...]
        acc = jnp.zeros(b.shape, F32)
        for k in range(N_BUCKETS):
            acc = jnp.where(b == k, tab_r[k, h], acc)
        o_o[...] = acc

    return pl.pallas_call(
        body, name=name, grid=(8,),
        in_specs=[pl.BlockSpec(memory_space=pltpu.SMEM), pl.BlockSpec((BLK, 3 * BLK), lambda h: (0, 0))],
        out_specs=pl.BlockSpec((None, BLK, 3 * BLK), lambda h: (h, 0, 0)),
        out_shape=_sds((8, BLK, 3 * BLK), F32),
        compiler_params=_params(("parallel",)),
    )(table, bucket)


def _bias_bwd(dbias, name):
    bucket = jnp.asarray(_bucket_table())

    def body(d_r, b_r, o_o):
        h = pl.program_id(0)
        b = b_r[...]
        d = d_r[...]
        lane = _iota((1, 128), 1)
        row = jnp.zeros((1, 128), F32)
        for k in range(N_BUCKETS):
            row = jnp.where(lane == k, jnp.sum(jnp.where(b == k, d, 0.0)), row)
        o_o[pl.ds(h, 1), :] = row

    return pl.pallas_call(
        body, name=name, grid=(8,),
        in_specs=[pl.BlockSpec((None, BLK, 3 * BLK), lambda h: (h, 0, 0)), pl.BlockSpec((BLK, 3 * BLK), lambda h: (0, 0))],
        out_specs=pl.BlockSpec((8, 128), lambda h: (0, 0)),
        out_shape=_sds((8, 128), F32),
        compiler_params=_params(("arbitrary",)),
    )(dbias, bucket)


def _swa_valid(i):
    r = _iota((BLK, 1), 0)
    c = _iota((1, 3 * BLK), 1)
    prev = (c < BLK) & (c > r) & (i >= 1) & ((i - 1) * BLK + c >= PAD)
    cc = c - BLK
    cur = (c >= BLK) & (c < 2 * BLK) & (cc <= r) & (i * BLK + cc >= PAD)
    cm = c - 2 * BLK
    meta = (c >= 2 * BLK) & (cm >= PAD) & (i * BLK + r - cm >= BLK)
    return prev | cur | meta


def _swa_kv_specs(ta):
    nb = ta // BLK
    return [pl.BlockSpec((None, BLK, 128), lambda p, i: (p // 2, jnp.maximum(i * nb - 1, 0), 0)),
            pl.BlockSpec((None, ta, 128), lambda p, i: (p // 2, i, 0)),
            pl.BlockSpec((None, BLK, 128), lambda p, i: (p // 2, 0, 0))]


def _swa_fwd(qs, kse, vse, bias, sinks, name):
    t = qs.shape[0]
    ta = _row_tile(t)
    nb = ta // BLK

    def body(sink_r, q_r, kp_r, kc_r, km_r, vp_r, vc_r, vm_r, b_r, o_o, lse_o):
        p = pl.program_id(0)
        i = pl.program_id(1)
        lo = _iota((1, 128), 1) < HD
        k4 = jnp.concatenate([kp_r[...], kc_r[...]], axis=0)
        v4 = jnp.concatenate([vp_r[...], vc_r[...]], axis=0)
        for b in range(nb):
            rows = slice(BLK * b, BLK * (b + 1))
            q = q_r[rows, :]
            k3 = jnp.concatenate([k4[BLK * b:BLK * (b + 2)], km_r[...]], axis=0)
            v3 = jnp.concatenate([v4[BLK * b:BLK * (b + 2)], vm_r[...]], axis=0)
            valid = _swa_valid(i * nb + b)
            outs, lses = [], []
            for e in (0, 1):
                sel = lo if e == 0 else jnp.logical_not(lo)
                s = _dot_nt(jnp.where(sel, q, 0), k3) + b_r[e]
                s = jnp.where(valid, s, NEG)
                sink = sink_r[2 * p + e]
                mx = jnp.maximum(jnp.max(s, axis=1, keepdims=True), sink)
                pe = jnp.exp(s - mx)
                den = jnp.sum(pe, axis=1, keepdims=True) + jnp.exp(sink - mx)
                outs.append(_dot(pe.astype(CDT), v3) / den)
                lses.append(mx + jnp.log(den))
            o_o[rows, :] = jnp.where(lo, outs[0], outs[1]).astype(CDT)
            lse_o[rows, :] = jnp.where(lo, lses[0], lses[1])

    qblk = pl.BlockSpec((ta, 128), lambda p, i: (i, p))
    return pl.pallas_call(
        body, name=name, grid=(NPAIR, t // ta),
        in_specs=[pl.BlockSpec(memory_space=pltpu.SMEM), qblk] + _swa_kv_specs(ta) + _swa_kv_specs(ta)
        + [pl.BlockSpec((2, BLK, 3 * BLK), lambda p, i: (p, 0, 0))],
        out_specs=[qblk, qblk],
        out_shape=[_sds((t, 512), CDT), _sds((t, 512), F32)],
        compiler_params=_params(("parallel", "parallel")),
    )(sinks, qs, kse, kse, kse, vse, vse, vse, bias)


def _swa_bwd(qs, kse, vse, bias, sinks, o, lse, do, name):
    t = qs.shape[0]
    ta = _row_tile(t)
    nb = ta // BLK

    def body(sink_r, q_r, kp_r, kc_r, km_r, vp_r, vc_r, vm_r, b_r, o_r, lse_r, do_r,
             dq_o, dk_o, dv_o, db_o, dsk_o):
        p = pl.program_id(0)
        i = pl.program_id(1)
        lo = _iota((1, 128), 1) < HD

        @pl.when((i == 0) & (p % 2 == 0))
        def _():
            dk_o[...] = jnp.zeros(dk_o.shape, F32)
            dv_o[...] = jnp.zeros(dv_o.shape, F32)

        @pl.when(i == 0)
        def _():
            db_o[...] = jnp.zeros(db_o.shape, F32)
            dsk_o[...] = jnp.zeros(dsk_o.shape, F32)

        k4 = jnp.concatenate([kp_r[...], kc_r[...]], axis=0)
        v4 = jnp.concatenate([vp_r[...], vc_r[...]], axis=0)
        for b in range(nb):
            ib = i * nb + b
            rows = slice(BLK * b, BLK * (b + 1))
            q = q_r[rows, :]
            do_ = do_r[rows, :]
            dd = do_ * o_r[rows, :].astype(F32)
            lse = lse_r[rows, :]
            k3 = jnp.concatenate([k4[BLK * b:BLK * (b + 2)], km_r[...]], axis=0)
            v3 = jnp.concatenate([v4[BLK * b:BLK * (b + 2)], vm_r[...]], axis=0)
            valid = _swa_valid(ib)
            dq = jnp.zeros((BLK, 128), F32)
            dk3 = jnp.zeros((3 * BLK, 128), F32)
            dv3 = jnp.zeros((3 * BLK, 128), F32)
            dsink = []
            for e in (0, 1):
                sel = lo if e == 0 else jnp.logical_not(lo)
                qe = jnp.where(sel, q, 0)
                doe = jnp.where(sel, do_, 0.0).astype(CDT)
                lse_e = lse[:, HD * e:HD * e + 1]
                s = _dot_nt(qe, k3) + b_r[e]
                s = jnp.where(valid, s, NEG)
                pr = jnp.exp(s - lse_e)
                delta = jnp.sum(jnp.where(sel, dd, 0.0), axis=1, keepdims=True)
                ds = pr * (_dot_nt(doe, v3) - delta)
                db_o[e] += ds
                dsink.append(-jnp.sum(jnp.exp(sink_r[2 * p + e] - lse_e) * delta, axis=0, keepdims=True))
                dq = dq + _dot(ds.astype(CDT), jnp.where(sel, k3, 0))
                dk3 = dk3 + _dot(ds.T.astype(CDT), qe)
                dv3 = dv3 + _dot(pr.T.astype(CDT), doe)
            dq_o[rows, :] = dq
            prev = pl.ds(pl.multiple_of(jnp.maximum(ib - 1, 0) * BLK, BLK), BLK)
            cur = pl.ds(pl.multiple_of(ib * BLK, BLK), BLK)
            dk_o[prev, :] += dk3[0:BLK]
            dk_o[cur, :] += dk3[BLK:2 * BLK]
            dk_o[0:BLK, :] += dk3[2 * BLK:]
            dv_o[prev, :] += dv3[0:BLK]
            dv_o[cur, :] += dv3[BLK:2 * BLK]
            dv_o[0:BLK, :] += dv3[2 * BLK:]
            dsk_o[0:1, :] += jnp.where(lo, dsink[0], dsink[1])

    qblk = pl.BlockSpec((ta, 128), lambda p, i: (i, p))
    kvacc = pl.BlockSpec((None, t, 128), lambda p, i: (p // 2, 0, 0))
    bblk = pl.BlockSpec((2, BLK, 3 * BLK), lambda p, i: (p, 0, 0))
    return pl.pallas_call(
        body, name=name, grid=(NPAIR, t // ta),
        in_specs=[pl.BlockSpec(memory_space=pltpu.SMEM), qblk] + _swa_kv_specs(ta) + _swa_kv_specs(ta)
        + [bblk, qblk, qblk, qblk],
        out_specs=[qblk, kvacc, kvacc, bblk, pl.BlockSpec((None, 8, 128), lambda p, i: (p, 0, 0))],
        out_shape=[_sds((t, 512), F32), _sds((2, t, 128), F32), _sds((2, t, 128), F32),
                   _sds((8, BLK, 3 * BLK), F32), _sds((NPAIR, 8, 128), F32)],
        compiler_params=_params(("arbitrary", "arbitrary")),
    )(sinks, qs, kse, kse, kse, vse, vse, vse, bias, o, lse, do)


def _sum8(slots, name):
    def body(a_r, o_o):
        acc = a_r[0]
        for k in range(1, 8):
            acc = acc + a_r[k]
        o_o[...] = acc

    return pl.pallas_call(
        body, name=name, out_shape=_sds((SMALL_ROWS, 128), F32),
        in_specs=[pl.BlockSpec(memory_space=pltpu.VMEM)], out_specs=pl.BlockSpec(memory_space=pltpu.VMEM),
        compiler_params=_params(),
    )(slots)


def _place():
    x, y, c = lax.axis_index("x"), lax.axis_index("y"), lax.axis_index("c")
    chips = [(1 - x, y), (x, 1 - y), (1 - x, 1 - y)]
    return x, y, c, chips


def _remote(src, dst, send_sems, recv_sems, k, to):
    return pltpu.make_async_remote_copy(src_ref=src, dst_ref=dst, send_sem=send_sems.at[k], recv_sem=recv_sems.at[k],
                                        device_id=to, device_id_type=MESH_ID)


ANY = pl.BlockSpec(memory_space=pl.ANY)


def _mix_cols(w):
    return jnp.concatenate([w[:, 2312:4360], w[:, 0:1536], w[:, 1544:2312], w[:, 1536:1544],
                            jnp.zeros((w.shape[0], DP - D_IN), w.dtype)], axis=1)


def _unmix_cols(w):
    return jnp.concatenate([w[:, QA:QA + 1536], w[:, FA:FA + 8], w[:, QB:QB + 768], w[:, GA:GA + 2048]], axis=1)


def _rows128(a, rows):
    flat = a.reshape(-1)
    return jnp.pad(flat, (0, rows * 128 - flat.shape[0])).reshape(rows, 128)


GRAD_FORM = {"ffn1_w_in": "col", "ffn2_w_in": "col", "w_branch_fox": "col", "w_branch_swa": "col",
             "ffn1_w_out": "3d", "ffn2_w_out": "3d", "w_out": "3d", "w_in": "3d"}
SUM_TILE = {1024: 128, 704: 176, 512: 128, 256: 128}
NT = len(SHARD_ITEMS)


def _half_rows(c, r):
    return pl.ds(pl.multiple_of(c * (r // 2), 16), r // 2)


def _ici_copies(kind, srcs, dsts, send_sems, recv_sems, layer, recv=True):
    x, y, c, chips = _place()
    s = 2 * x + y
    sends, recvs = [], []
    for t, ((nm, (r, cc), _), src, dst) in enumerate(zip(SHARD_ITEMS, srcs, dsts)):
        for j, (cx, cy) in enumerate(chips):
            sj = 2 * cx + cy
            k = 3 * t + j
            to = (cx, cy, c)
            if kind == "gather":
                hs = _half_rows(c, r)
                sends.append(_remote(src.at[layer, hs], dst.at[s, hs], send_sems, recv_sems, k, to))
                if recv:
                    recvs.append(_remote(src.at[layer, hs], dst.at[sj, hs], send_sems, recv_sems, k, to))
            else:
                if GRAD_FORM[nm] == "col":
                    piece = src.at[:, pl.ds(pl.multiple_of(sj * cc, 128), cc)]
                else:
                    piece = src.at[sj]
                sends.append(_remote(piece, dst.at[j], send_sems, recv_sems, k, to))
                recvs.append(sends[-1])
    return sends, recvs


def _slab_shapes():
    return [_sds((4, r, c), CDT) for _, (r, c), _ in SHARD_ITEMS]


def _dma_sems(n):
    return [pltpu.SemaphoreType.DMA((n,)), pltpu.SemaphoreType.DMA((n,))]


def _forward_sends(dsts, send_sems, recv_sems):
    x, y, c, chips = _place()
    sends, recvs = [], []
    for t, ((nm, (r, cc), _), dst) in enumerate(zip(SHARD_ITEMS, dsts)):
        for j, (cx, cy) in enumerate(chips):
            sj = 2 * cx + cy
            hs, ho = _half_rows(c, r), _half_rows(1 - c, r)
            sends.append(_remote(dst.at[sj, hs], dst.at[sj, hs], send_sems, recv_sems, 3 * t + j, (x, y, 1 - c)))
            recvs.append(_remote(dst.at[sj, ho], dst.at[sj, ho], send_sems, recv_sems, 3 * t + j, (x, y, 1 - c)))
    return sends, recvs


def _gather_layer(wb, mflat, layer, name):
    def body(*refs):
        srcs, m_r, dsts, mall_o = refs[:NT], refs[NT], refs[NT + 1:2 * NT + 1], refs[2 * NT + 1]
        send_sems, recv_sems, fsend, frecv, msend, mrecv = refs[2 * NT + 2:]
        x, y, c, chips = _place()
        s = 2 * x + y
        sends, recvs = _ici_copies("gather", srcs, dsts, send_sems, recv_sems, layer)
        metas = [_remote(m_r, mall_o.at[s], msend, mrecv, j, (cx, cy, c)) for j, (cx, cy) in enumerate(chips)]
        for cp in sends + metas:
            cp.start()
        fwds, frecvs = _forward_sends(dsts, fsend, frecv)
        for got, fwd in zip(recvs, fwds):
            got.wait_recv()
            fwd.start()
        for got in frecvs:
            got.wait_recv()
        for j, (cx, cy) in enumerate(chips):
            _remote(m_r, mall_o.at[2 * cx + cy], msend, mrecv, j, (cx, cy, c)).wait_recv()
        for cp in sends + metas + fwds:
            cp.wait_send()

    return pl.pallas_call(
        body, name=name, out_shape=_slab_shapes() + [_sds((4, META_ROWS, 128), F32)],
        in_specs=[ANY] * (NT + 1), out_specs=[ANY] * (NT + 1),
        scratch_shapes=_dma_sems(3 * NT) + _dma_sems(3 * NT) + _dma_sems(3),
    )(*wb, mflat)


def _forward_layer(slabs, name):
    def body(*refs):
        ins, outs, send_sems, recv_sems = refs[:NT], refs[NT:2 * NT], refs[2 * NT], refs[2 * NT + 1]
        sends, recvs = _forward_sends(outs, send_sems, recv_sems)
        for cp in sends:
            cp.start()
        for cp in recvs:
            cp.wait_recv()
        for cp in sends:
            cp.wait_send()

    return pl.pallas_call(
        body, name=name, out_shape=_slab_shapes(), in_specs=[ANY] * NT, out_specs=[ANY] * NT,
        input_output_aliases={t: t for t in range(NT)}, scratch_shapes=_dma_sems(3 * NT),
    )(*slabs)


def _half_shape(nm, r, c):
    return (r // 2, 4 * c) if GRAD_FORM[nm] == "col" else (4, r // 2, c)


def _swap_layer(gs, gsm, name):
    small = gsm is not None

    def body(*refs):
        g_rs = refs[:NT]
        pos = NT
        if small:
            s_r = refs[pos]
            pos += 1
        got_os = refs[pos:pos + NT]
        pos += NT
        if small:
            slots_o = refs[pos]
            pos += 1
        send_sems, recv_sems = refs[pos], refs[pos + 1]
        x, y, c, _ = _place()
        sib = (x, y, 1 - c)
        sent = []
        for t, ((nm, (r, cc), _), g_r, got_o) in enumerate(zip(SHARD_ITEMS, g_rs, got_os)):
            ho = _half_rows(1 - c, r)
            src = g_r.at[ho, :] if GRAD_FORM[nm] == "col" else g_r.at[:, ho, :]
            sent.append(_remote(src, got_o, send_sems, recv_sems, t, sib))
        if small:
            ssend, srecv, loc_sem = refs[pos + 2], refs[pos + 3], refs[pos + 4]
            me = 4 * x + 2 * y + c
            loc = pltpu.make_async_copy(s_r, slots_o.at[me], loc_sem.at[0])
            loc.start()
            peers = [(x ^ (k >> 2), y ^ ((k >> 1) & 1), c ^ (k & 1)) for k in range(1, 8)]
            for k, peer in enumerate(peers):
                sent.append(_remote(s_r, slots_o.at[me], ssend, srecv, k, peer))
        for cp in sent:
            cp.start()
        for cp in sent[:NT]:
            cp.wait_recv()
        if small:
            for k, (px, py, pc) in enumerate(peers):
                _remote(s_r, slots_o.at[4 * px + 2 * py + pc], ssend, srecv, k, (px, py, pc)).wait_recv()
        for cp in sent:
            cp.wait_send()
        if small:
            loc.wait()

    outs = [_sds(_half_shape(nm, r, c), CDT) for nm, (r, c), _ in SHARD_ITEMS]
    ops = list(gs)
    sems = _dma_sems(NT)
    if small:
        outs.append(_sds((8, SMALL_ROWS, 128), F32))
        ops.append(gsm)
        sems = sems + _dma_sems(7) + [pltpu.SemaphoreType.DMA((1,))]
    res = pl.pallas_call(
        body, name=name, out_shape=outs, in_specs=[ANY] * len(ops), out_specs=[ANY] * len(outs), scratch_shapes=sems,
    )(*ops)
    return (res[:NT], res[NT]) if small else (res, None)


def _pair_add_t(own, got, half_idx, nm, r, name):
    tr = SUM_TILE[r]
    nb = (r // 2) // tr
    if GRAD_FORM[nm] == "col":
        blk = (tr, own.shape[1])
        own_spec = pl.BlockSpec(blk, lambda i, c_r: (c_r[0] * nb + i, 0))
        half_spec = pl.BlockSpec(blk, lambda i, c_r: (i, 0))
    else:
        blk = (4, tr, own.shape[2])
        own_spec = pl.BlockSpec(blk, lambda i, c_r: (0, c_r[0] * nb + i, 0))
        half_spec = pl.BlockSpec(blk, lambda i, c_r: (0, i, 0))

    def body(c_r, a_r, b_r, o_o):
        o_o[...] = (a_r[...].astype(F32) + b_r[...].astype(F32)).astype(CDT)

    grid_spec = pltpu.PrefetchScalarGridSpec(num_scalar_prefetch=1, grid=(nb,), in_specs=[own_spec, half_spec],
                                             out_specs=half_spec)
    return pl.pallas_call(body, name=name, grid_spec=grid_spec, out_shape=_sds(got.shape, CDT),
                          compiler_params=_params(("parallel",)))(half_idx, own, got)


def _sum4_t(ps, got3, buf, idx, layer, nm, r, name):
    tr = SUM_TILE[r]
    nb = (r // 2) // tr
    c = got3.shape[2]
    if GRAD_FORM[nm] == "col":
        ps_spec = pl.BlockSpec((tr, c), lambda i, x_r: (i, x_r[0]))
    else:
        ps_spec = pl.BlockSpec((None, tr, c), lambda i, x_r: (x_r[0], i, 0))

    def body(x_r, a_r, b_r, buf_r, o_o):
        o_o[...] = ((a_r[...].astype(F32) + b_r[0].astype(F32)) + b_r[1].astype(F32)) + b_r[2].astype(F32)

    grid_spec = pltpu.PrefetchScalarGridSpec(
        num_scalar_prefetch=1, grid=(nb,),
        in_specs=[ps_spec, pl.BlockSpec((3, tr, c), lambda i, x_r: (0, i, 0)), ANY],
        out_specs=pl.BlockSpec((None, tr, c), lambda i, x_r: (layer, x_r[1] * nb + i, 0)),
    )
    return pl.pallas_call(body, name=name, grid_spec=grid_spec, out_shape=_sds(buf.shape, F32),
                          input_output_aliases={3: 0}, compiler_params=_params(("parallel",)))(idx, ps, got3, buf)


def _scatter_layer(ps, name):
    def body(*refs):
        srcs, dsts, send_sems, recv_sems = refs[:NT], refs[NT:2 * NT], refs[2 * NT], refs[2 * NT + 1]
        sends, recvs = _ici_copies("scatter", srcs, dsts, send_sems, recv_sems, None)
        for cp in sends:
            cp.start()
        for cp in recvs:
            cp.wait_recv()
        for cp in sends:
            cp.wait_send()

    return pl.pallas_call(
        body, name=name, out_shape=_got3_shapes(), in_specs=[ANY] * NT, out_specs=[ANY] * NT,
        scratch_shapes=_dma_sems(3 * NT),
    )(*ps)


def _got3_shapes():
    return [_sds((3, r // 2, c), CDT) for _, (r, c), _ in SHARD_ITEMS]


def _join_layer(bufs, name):
    def body(*refs):
        ins, outs, send_sems, recv_sems = refs[:NT], refs[NT:2 * NT], refs[2 * NT], refs[2 * NT + 1]
        x, y, c, _ = _place()
        sent = []
        for t, ((nm, (r, cc), _), b_o) in enumerate(zip(SHARD_ITEMS, outs)):
            hs = _half_rows(c, r)
            sent.append(_remote(b_o.at[:, hs, :], b_o.at[:, hs, :], send_sems, recv_sems, t, (x, y, 1 - c)))
        for cp in sent:
            cp.start()
        for t, ((nm, (r, cc), _), b_o) in enumerate(zip(SHARD_ITEMS, outs)):
            ho = _half_rows(1 - c, r)
            _remote(b_o.at[:, ho, :], b_o.at[:, ho, :], send_sems, recv_sems, t, (x, y, 1 - c)).wait_recv()
        for cp in sent:
            cp.wait_send()

    return pl.pallas_call(
        body, name=name, out_shape=[_sds(b.shape, F32) for b in bufs], in_specs=[ANY] * NT, out_specs=[ANY] * NT,
        input_output_aliases={t: t for t in range(NT)}, scratch_shapes=_dma_sems(NT),
    )(*bufs)


def _adamw3(w, g, m, v, name):
    nl, r, c = w.shape
    tr = SUM_TILE.get(r, r)

    def body(w_r, g_r, m_r, v_r, d_o, m_o, v_o):
        g_ = g_r[...]
        m_ = ADAM_B1 * m_r[...] + (1.0 - ADAM_B1) * g_
        v_ = ADAM_B2 * v_r[...] + (1.0 - ADAM_B2) * jnp.square(g_)
        m_hat = m_ / (1.0 - ADAM_B1 ** ADAM_STEP)
        v_hat = v_ / (1.0 - ADAM_B2 ** ADAM_STEP)
        d_o[...] = -ADAM_LR * (m_hat / (jnp.sqrt(v_hat) + ADAM_EPS) + ADAM_WD * w_r[...])
        m_o[...] = m_
        v_o[...] = v_

    blk = pl.BlockSpec((None, tr, c), lambda l, i: (l, i, 0))
    return pl.pallas_call(
        body, name=name, grid=(nl, r // tr),
        in_specs=[blk] * 4, out_specs=[blk] * 3, out_shape=[_sds((nl, r, c), F32)] * 3,
        compiler_params=_params(("parallel", "parallel")),
    )(w, g, m, v)


def _full_weights(slabs, wb, layer, shard):
    ws = {}
    for (nm, (r, c), kind), slab in zip(SHARD_ITEMS, slabs):
        slab = lax.dynamic_update_slice(slab, wb[nm][layer][None], (shard, 0, 0))
        ws[nm] = slab.reshape(4 * r, c) if kind == "row" else jnp.concatenate([slab[s] for s in range(4)], axis=1)
    return ws


def _exchange_forms(g):
    out = []
    for nm, (r, c), _ in SHARD_ITEMS:
        a = g[nm]
        if nm == "w_in":
            a = a.reshape(D, 4, c).transpose(1, 0, 2)
        elif GRAD_FORM[nm] == "3d":
            a = a.reshape(4, r, c)
        out.append(a)
    return out


SMALL_ITEMS = (("rel_bias_table", 2), ("ffn1_norm", 16), ("mix_norm", 16), ("ffn2_norm", 16), ("forget_bias", 1),
               ("fox_q_norm", 1), ("fox_k_norm", 1), ("swa_q_norm", 1), ("swa_k_norm", 1), ("swa_sinks", 1))
SMALL_ADAM_ROWS = 96


def _layer_fwd(h, lw, l, ride=None):
    sv = {"h0": h}
    a, sv["a1t"] = _rms_fwd(h, lw["ffn1_norm"], f"rms_fwd_a{l}")
    sv["gu1"], s, sv["s1t"] = _ffn_in(a, lw["ffn1_w_in"], f"ffn_in_a{l}")
    h = _mm_res(s, lw["ffn1_w_out"], h, 0.5, f"ffn_out_a{l}")
    sv["h1"] = h
    a, sv["amt"] = _rms_fwd(h, lw["mix_norm"], f"rms_fwd_m{l}")
    proj = _mm(a, lw["w_mix"], F32, _row_tile(h.shape[0]), DP, f"proj{l}")
    sv["proj"] = proj
    qf, kf, vf, qs, kse, vse, c, ct, sv["qft"] = _qknorm_fwd(proj, lw["gfq"], lw["gfk"], lw["gsq"], lw["gsk"], lw["fb"],
                                                              f"qknorm_fwd{l}")
    ofox, lse_f, *rode = _fox_fwd(qf, kf, vf, c, ct, f"fox_fwd{l}", ride)
    oswa, lse_s = _swa_fwd(qs, kse, vse, lw["bias"], lw["sinks"], f"swa_fwd{l}")
    sv.update(qf=qf, kf=kf, vf=vf, qs=qs, kse=kse, vse=vse, c=c, ct=ct, ofox=ofox, oswa=oswa, lse_f=lse_f, lse_s=lse_s)
    y, sv["yt"], sv["pf"], sv["ps"], sv["oft"], sv["ost"] = _gate_fwd(ofox, oswa, lw["w_branch_fox"], lw["w_branch_swa"],
                                                                     proj, f"gate_fwd{l}")
    h = _mm_res(y, lw["w_out"], h, 1.0, f"mix_out{l}")
    sv["h2"] = h
    a, sv["a2t"] = _rms_fwd(h, lw["ffn2_norm"], f"rms_fwd_b{l}")
    sv["gu2"], s, sv["s2t"] = _ffn_in(a, lw["ffn2_w_in"], f"ffn_in_b{l}")
    h = _mm_res(s, lw["ffn2_w_out"], h, 0.5, f"ffn_out_b{l}")
    return h, sv, rode


def _ffn_bwd(dh, dhb, h_in, at, gu, st, norm, w_in, w_out, tag):
    dgu = _ffn_bwd_mid(dhb, w_out, gu, f"ffn_bwd_mid_{tag}")
    d_w_out = _mm(st, dhb, CDT, 256, D, f"dw_ffn_out_{tag}", scale=0.5)
    da = _ffn_bwd_in(dgu, w_in, f"ffn_bwd_in_{tag}")
    d_w_in = _mm(at, dgu, CDT, D, 256, f"dw_ffn_in_{tag}")
    dh, dhb, dg = _rms_bwd(da, h_in, norm, dh, f"rms_bwd_{tag}")
    return dh, dhb, d_w_out, d_w_in, dg


def _layer_bwd(dh, dhb, sv, lw, l, ride=None):
    g = {}
    dh, dhb, g["ffn2_w_out"], g["ffn2_w_in"], g["ffn2_norm"] = _ffn_bwd(
        dh, dhb, sv["h2"], sv["a2t"], sv["gu2"], sv["s2t"], lw["ffn2_norm"], lw["ffn2_w_in"], lw["ffn2_w_out"], f"b{l}")
    dy = _mm_nt(dhb, lw["w_out"], f"d_y{l}")
    g["w_out"] = _mm(sv["yt"], dhb, CDT, 512, 512, f"dw_out{l}")
    dpf, dps, dga, dgb = _gate_bwd(dy, sv["pf"], sv["ps"], sv["proj"], f"gate_bwd{l}")
    do_f, do_ft = _mm_nt(dpf, lw["w_branch_fox"], f"d_ofox{l}", with_t=True)
    do_s = _mm_nt(dps, lw["w_branch_swa"], f"d_oswa{l}")
    g["w_branch_fox"] = _mm(sv["oft"], dpf, CDT, 512, 512, f"dw_bfox{l}")
    g["w_branch_swa"] = _mm(sv["ost"], dps, CDT, 512, 512, f"dw_bswa{l}")
    dqf, dcq, dkf, dvf, dck, *rode = _fox_bwd(sv["qf"], sv["qft"], sv["kf"], sv["vf"], sv["c"], sv["ct"], sv["ofox"],
                                              sv["lse_f"], do_f, do_ft, f"fox_bwd{l}", ride)
    g["rode"] = rode
    dqs, dkse, dvse, dbias, dsk = _swa_bwd(sv["qs"], sv["kse"], sv["vse"], lw["bias"], lw["sinks"], sv["oswa"],
                                           sv["lse_s"], do_s, f"swa_bwd{l}")
    dproj, dgn = _qknorm_bwd(sv["proj"], dqf, dkf, dvf, dqs, dkse, dvse, dcq, dck, dga, dgb,
                             lw["gfq"], lw["gfk"], lw["gsq"], lw["gsk"], lw["fb"], f"qknorm_bwd{l}")
    dam = _mm_nt(dproj, lw["w_mix"], f"d_am{l}", tk=D)
    g["w_mix"] = _mm(sv["amt"], dproj, CDT, 512, 640, f"dw_mix{l}")
    dh, dhb, g["mix_norm"] = _rms_bwd(dam, sv["h1"], lw["mix_norm"], dh, f"rms_bwd_m{l}")
    g["dbias"], g["dsk"], g["dgn"] = dbias, dsk, dgn
    dh, dhb, g["ffn1_w_out"], g["ffn1_w_in"], g["ffn1_norm"] = _ffn_bwd(
        dh, dhb, sv["h0"], sv["a1t"], sv["gu1"], sv["s1t"], lw["ffn1_norm"], lw["ffn1_w_in"], lw["ffn1_w_out"], f"a{l}")
    return dh, dhb, g


def kernel(x, meta_tokens, rel_bias_table, ffn1_norm, ffn1_w_in, ffn1_w_out, mix_norm, w_in, forget_bias, fox_q_norm, fox_k_norm, swa_q_norm, swa_k_norm, swa_sinks, w_branch_fox, w_branch_swa, w_out, ffn2_norm, ffn2_w_in, ffn2_w_out, loss_target, m_meta_tokens, m_rel_bias_table, m_ffn1_norm, m_ffn1_w_in, m_ffn1_w_out, m_mix_norm, m_w_in, m_forget_bias, m_fox_q_norm, m_fox_k_norm, m_swa_q_norm, m_swa_k_norm, m_swa_sinks, m_w_branch_fox, m_w_branch_swa, m_w_out, m_ffn2_norm, m_ffn2_w_in, m_ffn2_w_out, v_meta_tokens, v_rel_bias_table, v_ffn1_norm, v_ffn1_w_in, v_ffn1_w_out, v_mix_norm, v_w_in, v_forget_bias, v_fox_q_norm, v_fox_k_norm, v_swa_q_norm, v_swa_k_norm, v_swa_sinks, v_w_branch_fox, v_w_branch_swa, v_w_out, v_ffn2_norm, v_ffn2_w_in, v_ffn2_w_out):
    names = ["meta_tokens", "rel_bias_table", "ffn1_norm", "ffn1_w_in", "ffn1_w_out", "mix_norm", "w_in", "forget_bias",
             "fox_q_norm", "fox_k_norm", "swa_q_norm", "swa_k_norm", "swa_sinks", "w_branch_fox", "w_branch_swa", "w_out",
             "ffn2_norm", "ffn2_w_in", "ffn2_w_out"]
    w = dict(zip(names, [meta_tokens, rel_bias_table, ffn1_norm, ffn1_w_in, ffn1_w_out, mix_norm, w_in, forget_bias,
                         fox_q_norm, fox_k_norm, swa_q_norm, swa_k_norm, swa_sinks, w_branch_fox, w_branch_swa, w_out,
                         ffn2_norm, ffn2_w_in, ffn2_w_out]))
    m = dict(zip(names, [m_meta_tokens, m_rel_bias_table, m_ffn1_norm, m_ffn1_w_in, m_ffn1_w_out, m_mix_norm, m_w_in,
                         m_forget_bias, m_fox_q_norm, m_fox_k_norm, m_swa_q_norm, m_swa_k_norm, m_swa_sinks,
                         m_w_branch_fox, m_w_branch_swa, m_w_out, m_ffn2_norm, m_ffn2_w_in, m_ffn2_w_out]))
    v = dict(zip(names, [v_meta_tokens, v_rel_bias_table, v_ffn1_norm, v_ffn1_w_in, v_ffn1_w_out, v_mix_norm, v_w_in,
                         v_forget_bias, v_fox_q_norm, v_fox_k_norm, v_swa_q_norm, v_swa_k_norm, v_swa_sinks,
                         v_w_branch_fox, v_w_branch_swa, v_w_out, v_ffn2_norm, v_ffn2_w_in, v_ffn2_w_out]))
    xi, yi, ci = lax.axis_index("x"), lax.axis_index("y"), lax.axis_index("c")
    shard = 2 * xi + yi
    seq = x.shape[1]
    t = seq + BLK

    wb = {nm: w[nm].astype(CDT) for nm, _, _ in SHARD_ITEMS}
    wb_list = [wb[nm] for nm, _, _ in SHARD_ITEMS]
    mflat = meta_tokens.reshape(META_ROWS, 128)
    *slabs0, mall = _gather_layer(wb_list, mflat, 0, "gather_weights")
    mall = lax.dynamic_update_slice(mall, mflat[None], (shard, 0, 0))
    meta_full = jnp.concatenate([mall[s].reshape(N_META, 256) for s in range(4)], axis=1)
    bias = _bias_fwd(rel_bias_table, "bias_fwd")

    def layer_weights(slabs, l):
        lw = _full_weights(slabs, wb, l, shard)
        lw["w_mix"] = _mix_cols(lw.pop("w_in"))
        for nm in ("ffn1_norm", "mix_norm", "ffn2_norm"):
            lw[nm] = w[nm][l].reshape(1, D)
        lw["gfq"] = jnp.tile(fox_q_norm[l], 8).reshape(1, 512)
        lw["gfk"] = jnp.tile(fox_k_norm[l], 8).reshape(1, 512)
        lw["gsq"] = jnp.tile(swa_q_norm[l], 8).reshape(1, 512)
        lw["gsk"] = jnp.tile(swa_k_norm[l], 2).reshape(1, 128)
        lw["fb"] = jnp.pad(forget_bias[l], (0, 120)).reshape(1, 128)
        lw["sinks"] = swa_sinks[l]
        lw["bias"] = bias
        return lw

    h = jnp.concatenate([jnp.zeros((PAD, D), F32), meta_full, x[0]], axis=0)
    lws = [layer_weights(slabs0, 0)]
    h, sv0, slabs1 = _layer_fwd(h, lws[0], 0, ("gather", wb_list, _slab_shapes(), 1))
    lws.append(layer_weights(_forward_layer(slabs1, "forward_halves"), 1))
    h, sv1, _ = _layer_fwd(h, lws[1], 1)
    saved = [sv0, sv1]
    dh, dhb, lacc = _loss(h, loss_target[0], "loss")
    loss = lax.psum(lacc[0, 0], ("x", "y", "c"))

    half_idx = ci.reshape(1).astype(jnp.int32)
    place_idx = jnp.stack([shard, ci]).astype(jnp.int32)

    def pair_sums(g, gsm, tag):
        g["w_in"] = _unmix_cols(g.pop("w_mix"))
        forms = _exchange_forms(g)
        got, slots = _swap_layer(forms, gsm, f"swap_halves{tag}")
        return [_pair_add_t(a, b, half_idx, nm, r, f"pair_add{tag}_{nm}")
                for a, b, (nm, (r, c), _) in zip(forms, got, SHARD_ITEMS)], slots

    grads = [None, None]
    dh, dhb, grads[1] = _layer_bwd(dh, dhb, saved[1], lws[1], 1)
    ps1, _ = pair_sums(grads[1], None, 1)
    dh, dhb, grads[0] = _layer_bwd(dh, dhb, saved[0], lws[0], 0, ("scatter", ps1, _got3_shapes(), None))
    grad_x = dh[BLK:].reshape(1, seq, D)
    dtab = _bias_bwd(grads[0]["dbias"] + grads[1]["dbias"], "bias_bwd")

    small = [dh[PAD:BLK].reshape(128, 128), _rows128(dtab[:, :N_BUCKETS].T, 2)]
    for nm in ("ffn1_norm", "mix_norm", "ffn2_norm"):
        small.append(jnp.stack([grads[l][nm][0] for l in range(2)]).reshape(16, 128))
    small.append(_rows128(jnp.stack([grads[l]["dgn"][4, :8] for l in range(2)]), 1))
    for row in range(4):
        small.append(jnp.stack([grads[l]["dgn"][row, :HD] for l in range(2)]).reshape(1, 128))
    dsk = [grads[l]["dsk"][:, 0, :] for l in range(2)]
    small.append(_rows128(jnp.stack([jnp.stack([d[:, 0], d[:, HD]], axis=1).reshape(8) for d in dsk]), 1))
    gsm = jnp.concatenate(small, axis=0)
    gsm = jnp.pad(gsm, ((0, SMALL_ROWS - gsm.shape[0]), (0, 0)))

    ps0, slots = pair_sums(grads[0], gsm, 0)
    got3 = [_scatter_layer(ps0, "scatter_shards"), grads[0]["rode"]]
    bufs = []
    for t, (nm, (r, c), _) in enumerate(SHARD_ITEMS):
        buf = lax.empty((2, r, c), F32)
        for l, ps in ((1, ps1), (0, ps0)):
            buf = _sum4_t(ps[t], got3[l][t], buf, place_idx, l, nm, r, f"sum4_{l}_{nm}")
        bufs.append(buf)
    bufs = _join_layer(bufs, "join_halves")
    gs = _sum8(slots, "sum8")

    g_out = {nm: buf for (nm, _, _), buf in zip(SHARD_ITEMS, bufs)}
    g_out["meta_tokens"] = lax.dynamic_slice(gs[0:128].reshape(N_META, D), (0, shard * 256), (N_META, 256))
    off = 128
    for nm, rows in SMALL_ITEMS:
        n = w[nm].size
        g_out[nm] = gs[off:off + rows].reshape(-1)[:n].reshape(w[nm].shape)
        off += rows

    delta, new_m, new_v = {}, {}, {}
    for nm, _, _ in SHARD_ITEMS:
        delta[nm], new_m[nm], new_v[nm] = _adamw3(w[nm], g_out[nm], m[nm], v[nm], f"adamw_{nm}")
    small_names = ["meta_tokens"] + [nm for nm, _ in SMALL_ITEMS]
    small_rows = [META_ROWS] + [rows for _, rows in SMALL_ITEMS]

    def pack_small(src):
        buf = jnp.concatenate([_rows128(src[nm], rows) for nm, rows in zip(small_names, small_rows)], axis=0)
        return jnp.pad(buf, ((0, SMALL_ADAM_ROWS - buf.shape[0]), (0, 0)))

    d_, m_, v_ = (a[0] for a in _adamw3(pack_small(w)[None], pack_small(g_out)[None], pack_small(m)[None],
                                        pack_small(v)[None], "adamw_small"))
    off = 0
    for nm, rows in zip(small_names, small_rows):
        n = w[nm].size
        for dst, src in ((delta, d_), (new_m, m_), (new_v, v_)):
            dst[nm] = src[off:off + rows].reshape(-1)[:n].reshape(w[nm].shape)
        off += rows

    return (loss, grad_x, *[g_out[n] for n in names], *[delta[n] for n in names],
            *[new_m[n] for n in names], *[new_v[n] for n in names])
```

```python
import math

import numpy as np
import jax
import jax.numpy as jnp
from jax import lax
from jax.experimental import pallas as pl
from jax.experimental.pallas import tpu as pltpu

D = 1024
F = 2816
FT = F // 2
HD = 64
NPAIR = 4
N_META = 16
BLK = 128
PAD = BLK - N_META
EPS = 1e-6
NEG = -1e30
N_BUCKETS = 32
GA, GB, QA, KA, VA, QB, KB, VB, FA, DP = 0, 1024, 2048, 2560, 3072, 3584, 4096, 4224, 4352, 4480
D_IN = 4360
CDT = jnp.bfloat16
F32 = jnp.float32
VMEM_LIMIT = 48 * 1024 * 1024
MESH_ID = pl.DeviceIdType.MESH

ADAM_LR, ADAM_B1, ADAM_B2, ADAM_EPS, ADAM_WD, ADAM_STEP = 0.001, 0.9, 0.999, 1e-08, 0.01, 10

SHARD_ITEMS = (
    ("ffn1_w_in", (1024, 1408), "col"),
    ("ffn1_w_out", (704, 1024), "row"),
    ("w_in", (1024, 1090), "col"),
    ("w_branch_fox", (512, 256), "col"),
    ("w_branch_swa", (512, 256), "col"),
    ("w_out", (256, 1024), "row"),
    ("ffn2_w_in", (1024, 1408), "col"),
    ("ffn2_w_out", (704, 1024), "row"),
)
SMALL_ROWS = 192
META_ROWS = 32


def _row_tile(t):
    return 384 if t % 384 == 0 else 128


def _dot(a, b):
    return jnp.dot(a, b, preferred_element_type=F32)


def _dot_nt(a, b):
    return lax.dot_general(a, b, (((1,), (1,)), ((), ())), preferred_element_type=F32)


def _dot_hi(a, b):
    return jnp.dot(a, b, preferred_element_type=F32, precision=lax.Precision.HIGHEST)


def _sigmoid(x):
    return 1.0 / (1.0 + jnp.exp(-x))


def _iota(shape, dim):
    return lax.broadcasted_iota(jnp.int32, shape, dim)


def _params(sem=None):
    return pltpu.CompilerParams(dimension_semantics=sem, vmem_limit_bytes=VMEM_LIMIT)


def _sds(shape, dtype):
    return jax.ShapeDtypeStruct(shape, dtype)


def _rms_fwd(h, g, name):
    t = h.shape[0]
    tm = _row_tile(t)

    def body(h_ref, g_ref, a_ref, at_ref):
        x = h_ref[...]
        ms = jnp.mean(x * x, axis=-1, keepdims=True)
        a = x * lax.rsqrt(ms + EPS) * g_ref[...]
        a_ref[...] = a.astype(CDT)
        at_ref[...] = a.T.astype(CDT)

    return pl.pallas_call(
        body, name=name, grid=(t // tm,),
        in_specs=[pl.BlockSpec((tm, D), lambda i: (i, 0)), pl.BlockSpec((1, D), lambda i: (0, 0))],
        out_specs=[pl.BlockSpec((tm, D), lambda i: (i, 0)), pl.BlockSpec((D, tm), lambda i: (0, i))],
        out_shape=[_sds((t, D), CDT), _sds((D, t), CDT)],
        compiler_params=_params(("parallel",)),
    )(h, g)


def _ffn_in(a, w_in, name):
    t = a.shape[0]
    tm = _row_tile(t)
    tn = FT
    nj = F // tn

    def body(a_ref, wg_ref, wu_ref, gu_ref, s_ref, st_ref):
        a_ = a_ref[...]
        g = _dot(a_, wg_ref[...])
        u = _dot(a_, wu_ref[...])
        s = g * _sigmoid(g) * u
        gu_ref[0] = g.astype(CDT)
        gu_ref[1] = u.astype(CDT)
        s_ref[...] = s.astype(CDT)
        st_ref[...] = s.T.astype(CDT)

    return pl.pallas_call(
        body, name=name, grid=(nj, t // tm),
        in_specs=[pl.BlockSpec((tm, D), lambda j, i: (i, 0)),
                  pl.BlockSpec((D, tn), lambda j, i: (0, j)),
                  pl.BlockSpec((D, tn), lambda j, i: (0, j + nj))],
        out_specs=[pl.BlockSpec((2, tm, tn), lambda j, i: (0, i, j)),
                   pl.BlockSpec((tm, tn), lambda j, i: (i, j)),
                   pl.BlockSpec((tn, tm), lambda j, i: (j, i))],
        out_shape=[_sds((2, t, F), CDT), _sds((t, F), CDT), _sds((F, t), CDT)],
        compiler_params=_params(("parallel", "parallel")),
    )(a, w_in, w_in)


def _mm_res(a, b, res, scale, name):
    t, k = a.shape
    n = b.shape[1]
    tm = _row_tile(t)
    tn = 512

    def body(a_ref, b_ref, r_ref, o_ref):
        o_ref[...] = r_ref[...] + scale * _dot(a_ref[...], b_ref[...])

    return pl.pallas_call(
        body, name=name, grid=(t // tm, n // tn),
        in_specs=[pl.BlockSpec((tm, k), lambda i, j: (i, 0)),
                  pl.BlockSpec((k, tn), lambda i, j: (0, j)),
                  pl.BlockSpec((tm, tn), lambda i, j: (i, j))],
        out_specs=pl.BlockSpec((tm, tn), lambda i, j: (i, j)),
        out_shape=_sds((t, n), F32),
        compiler_params=_params(("parallel", "parallel")),
    )(a, b, res)


def _mm(a, b, out_dtype, tm, tn, name, scale=1.0, b_resident=False):
    m, k = a.shape
    order = (lambda j, i: (i, j)) if b_resident else (lambda i, j: (i, j))
    if b.ndim == 3:
        nh = b.shape[2] // tn
        n = 2 * b.shape[2]
        b_spec = pl.BlockSpec((None, k, tn), lambda *g: (order(*g)[1] // nh, 0, order(*g)[1] % nh))
    else:
        n = b.shape[1]
        b_spec = pl.BlockSpec((k, tn), lambda *g: (0, order(*g)[1]))

    def body(a_ref, b_ref, o_ref):
        o_ref[...] = (scale * _dot(a_ref[...], b_ref[...])).astype(out_dtype)

    return pl.pallas_call(
        body, name=name, grid=(n // tn, m // tm) if b_resident else (m // tm, n // tn),
        in_specs=[pl.BlockSpec((tm, k), lambda *g: (order(*g)[0], 0)), b_spec],
        out_specs=pl.BlockSpec((tm, tn), lambda *g: order(*g)),
        out_shape=_sds((m, n), out_dtype),
        compiler_params=_params(("parallel", "parallel")),
    )(a, b)


def _mm_nt(a, b, name, with_t=False):
    m, n = a.shape
    k = b.shape[0]
    tm = _row_tile(m)
    tk = 512

    def body(a_ref, b_ref, o_ref, *t_ref):
        r = _dot_nt(a_ref[...], b_ref[...])
        o_ref[...] = r
        if with_t:
            t_ref[0][...] = r.T.astype(CDT)

    out_specs = [pl.BlockSpec((tm, tk), lambda i, j: (i, j))]
    out_shape = [_sds((m, k), F32)]
    if with_t:
        out_specs.append(pl.BlockSpec((tk, tm), lambda i, j: (j, i)))
        out_shape.append(_sds((k, m), CDT))
    res = pl.pallas_call(
        body, name=name, grid=(m // tm, k // tk),
        in_specs=[pl.BlockSpec((tm, n), lambda i, j: (i, 0)), pl.BlockSpec((tk, n), lambda i, j: (j, 0))],
        out_specs=out_specs, out_shape=out_shape,
        compiler_params=_params(("parallel", "parallel")),
    )(a, b)
    return res if with_t else res[0]


def _ffn_bwd_mid(dhb, w_out, gu, name):
    t = dhb.shape[0]
    tm = _row_tile(t)
    tn = FT

    def body(dh_ref, w_ref, gu_ref, o_ref):
        ds = 0.5 * _dot_nt(dh_ref[...], w_ref[...])
        g = gu_ref[0].astype(F32)
        u = gu_ref[1].astype(F32)
        sg = _sigmoid(g)
        o_ref[0] = (ds * u * (sg * (1.0 + g * (1.0 - sg)))).astype(CDT)
        o_ref[1] = (ds * (g * sg)).astype(CDT)

    return pl.pallas_call(
        body, name=name, grid=(F // tn, t // tm),
        in_specs=[pl.BlockSpec((tm, D), lambda j, i: (i, 0)),
                  pl.BlockSpec((tn, D), lambda j, i: (j, 0)),
                  pl.BlockSpec((2, tm, tn), lambda j, i: (0, i, j))],
        out_specs=pl.BlockSpec((2, tm, tn), lambda j, i: (0, i, j)),
        out_shape=_sds((2, t, F), CDT),
        compiler_params=_params(("parallel", "parallel")),
    )(dhb, w_out, gu)


def _rms_bwd_rows(da_, x, g, dres, i, dh_ref, dhb_ref, dg_ref):
    r = lax.rsqrt(jnp.mean(x * x, axis=-1, keepdims=True) + EPS)
    xh = x * r
    day = da_ * g
    dh = dres + r * (day - xh * jnp.mean(day * xh, axis=-1, keepdims=True))
    dh_ref[...] = dh
    dhb_ref[...] = dh.astype(CDT)

    @pl.when(i == 0)
    def _():
        dg_ref[...] = jnp.zeros(dg_ref.shape, F32)

    dg_ref[0:1, :] += jnp.sum(da_ * xh, axis=0, keepdims=True)


def _ffn_bwd_in(dgu, w_in, h, g, dres, name):
    t = dgu.shape[1]
    tm = _row_tile(t)

    def body(dg_ref, wg_ref, wu_ref, h_ref, g_ref, dr_ref, dh_ref, dhb_ref, dgn_ref):
        da_ = _dot_nt(dg_ref[0], wg_ref[...]) + _dot_nt(dg_ref[1], wu_ref[...])
        _rms_bwd_rows(da_, h_ref[...], g_ref[...], dr_ref[...], pl.program_id(0), dh_ref, dhb_ref, dgn_ref)

    row = pl.BlockSpec((tm, D), lambda i: (i, 0))
    return pl.pallas_call(
        body, name=name, grid=(t // tm,),
        in_specs=[pl.BlockSpec((2, tm, F), lambda i: (0, i, 0)),
                  pl.BlockSpec((D, F), lambda i: (0, 0)),
                  pl.BlockSpec((D, F), lambda i: (0, 1)),
                  row, pl.BlockSpec((1, D), lambda i: (0, 0)), row],
        out_specs=[row, row, pl.BlockSpec((8, D), lambda i: (0, 0))],
        out_shape=[_sds((t, D), F32), _sds((t, D), CDT), _sds((8, D), F32)],
        compiler_params=_params(("arbitrary",)),
    )(dgu, w_in, w_in, h, g, dres)


def _mm_nt_rms(a, b, h, g, dres, name):
    t, n = a.shape
    tm = _row_tile(t)

    def body(a_ref, b_ref, h_ref, g_ref, dr_ref, dh_ref, dhb_ref, dgn_ref):
        da_ = _dot_nt(a_ref[...], b_ref[...])
        _rms_bwd_rows(da_, h_ref[...], g_ref[...], dr_ref[...], pl.program_id(0), dh_ref, dhb_ref, dgn_ref)

    row = pl.BlockSpec((tm, D), lambda i: (i, 0))
    return pl.pallas_call(
        body, name=name, grid=(t // tm,),
        in_specs=[pl.BlockSpec((tm, n), lambda i: (i, 0)), pl.BlockSpec((D, n), lambda i: (0, 0)),
                  row, pl.BlockSpec((1, D), lambda i: (0, 0)), row],
        out_specs=[row, row, pl.BlockSpec((8, D), lambda i: (0, 0))],
        out_shape=[_sds((t, D), F32), _sds((t, D), CDT), _sds((8, D), F32)],
        compiler_params=_params(("arbitrary",)),
    )(a, b, h, g, dres)


def _loss(h, target, name):
    t = h.shape[0]

    def body(h_ref, t_ref, dh_ref, dhb_ref, l_ref):
        i = pl.program_id(0)

        @pl.when(i == 0)
        def _():
            l_ref[...] = jnp.zeros(l_ref.shape, F32)
            dh_ref[...] = jnp.zeros(dh_ref.shape, F32)
            dhb_ref[...] = jnp.zeros(dhb_ref.shape, CDT)

        @pl.when(i > 0)
        def _():
            err = h_ref[...] - t_ref[...]
            l_ref[...] += (0.5 / D) * jnp.sum(err * err)
            d = err * (1.0 / D)
            dh_ref[...] = d
            dhb_ref[...] = d.astype(CDT)

    row = pl.BlockSpec((BLK, D), lambda i: (i, 0))
    return pl.pallas_call(
        body, name=name, grid=(t // BLK,),
        in_specs=[row, pl.BlockSpec((BLK, D), lambda i: (jnp.maximum(i - 1, 0), 0))],
        out_specs=[row, row, pl.BlockSpec((8, 128), lambda i: (0, 0))],
        out_shape=[_sds((t, D), F32), _sds((t, D), CDT), _sds((8, 128), F32)],
        compiler_params=_params(("arbitrary",)),
    )(h, target)


def _block_diag():
    return (_iota((128, 128), 0) // HD == _iota((128, 128), 1) // HD).astype(F32)


def _head_sums(v, bd):
    hi = v.astype(CDT)
    rest = (v - hi.astype(F32)).astype(CDT)
    b = bd.astype(CDT)
    return _dot(hi, b) + _dot(rest, b)


def _dup_halves(x, lo):
    sw = pltpu.roll(x, 64, 1)
    return jnp.where(lo, x, sw), jnp.where(lo, sw, x)


def _qknorm_fwd(proj, gfq, gfk, gsq, gsk, fb, name):
    t = proj.shape[0]
    tm = _row_tile(t)

    def body(qa, ka, va, qb, kb, vb, fa, gfq_r, gfk_r, gsq_r, gsk_r, fb_r,
             qf_o, kf_o, vf_o, qs_o, kse_o, vse_o, c_o, ct_o, qft_o, carry):
        i = pl.program_id(0)
        bd = _block_diag()
        lane = _iota((1, 128), 1)
        lo = lane < HD

        def hnorm(x, g):
            ms = _head_sums(x * x, bd) * (1.0 / HD)
            return x * lax.rsqrt(ms + EPS) * g

        for ch in range(4):
            sl = slice(128 * ch, 128 * (ch + 1))
            qn = hnorm(qa[:, sl], gfq_r[:, sl]) * 0.125
            qf_o[:, sl] = qn.astype(CDT)
            qft_o[sl, :] = qn.T.astype(CDT)
            kf_o[:, sl] = hnorm(ka[:, sl], gfk_r[:, sl]).astype(CDT)
            qs_o[:, sl] = (hnorm(qb[:, sl], gsq_r[:, sl]) * 0.125).astype(CDT)
        vf_o[...] = va[...].astype(CDT)
        k0, k1 = _dup_halves(hnorm(kb[...], gsk_r[...]), lo)
        kse_o[0] = k0.astype(CDT)
        kse_o[1] = k1.astype(CDT)
        v0, v1 = _dup_halves(vb[...], lo)
        vse_o[0] = v0.astype(CDT)
        vse_o[1] = v1.astype(CDT)

        z = fa[...] + fb_r[...]
        lf = jnp.minimum(z, 0.0) - jnp.log(1.0 + jnp.exp(-jnp.abs(z)))
        lf = jnp.where(lane < 8, lf, 0.0)
        ltri = (_iota((tm, tm), 1) <= _iota((tm, tm), 0)).astype(F32)

        @pl.when(i == 0)
        def _():
            carry[...] = jnp.zeros(carry.shape, F32)

        c = _dot_hi(ltri, lf) + carry[0:1, :]
        carry[0:1, :] = c[tm - 1:tm, :]
        c_o[...] = c
        ct_o[...] = c.T[0:8, :]

    def col(width, off):
        return pl.BlockSpec((tm, width), lambda i: (i, off // width))

    def vec(width):
        return pl.BlockSpec((1, width), lambda i: (0, 0))

    return pl.pallas_call(
        body, name=name, grid=(t // tm,),
        in_specs=[col(512, QA), col(512, KA), col(512, VA), col(512, QB), col(128, KB), col(128, VB), col(128, FA),
                  vec(512), vec(512), vec(512), vec(128), vec(128)],
        out_specs=[pl.BlockSpec((tm, 512), lambda i: (i, 0))] * 4
        + [pl.BlockSpec((2, tm, 128), lambda i: (0, i, 0))] * 2
        + [pl.BlockSpec((tm, 128), lambda i: (i, 0)), pl.BlockSpec((8, tm), lambda i: (0, i)),
           pl.BlockSpec((512, tm), lambda i: (0, i))],
        out_shape=[_sds((t, 512), CDT)] * 4 + [_sds((2, t, 128), CDT)] * 2
        + [_sds((t, 128), F32), _sds((8, t), F32), _sds((512, t), CDT)],
        scratch_shapes=[pltpu.VMEM((8, 128), F32)],
        compiler_params=_params(("arbitrary",)),
    )(proj, proj, proj, proj, proj, proj, proj, gfq, gfk, gsq, gsk, fb)


def _qknorm_bwd(proj, dqf, dkf, dvf, dqs, dkse, dvse, dcq, dck, dga, dgb, gfq, gfk, gsq, gsk, fb, name):
    t = proj.shape[0]
    tm = _row_tile(t)
    nt = t // tm

    def body(qa, ka, qb, kb, fa, dqf_r, dkf_r, dvf_r, dqs_r, dkse_r, dvse_r, dcq_r, dck_r, dga_r, dgb_r,
             gfq_r, gfk_r, gsq_r, gsk_r, fb_r, dp_o, dgn_o, carry, acc):
        i = pl.program_id(0)
        bd = _block_diag()
        lane = _iota((1, 128), 1)
        lo = lane < HD

        @pl.when(i == 0)
        def _():
            carry[...] = jnp.zeros(carry.shape, F32)
            acc[...] = jnp.zeros(acc.shape, F32)

        def hnorm_bwd(x, g, dy):
            r = lax.rsqrt(_head_sums(x * x, bd) * (1.0 / HD) + EPS)
            xh = x * r
            day = dy * g
            dx = r * (day - xh * (_head_sums(day * xh, bd) * (1.0 / HD)))
            return dx, jnp.sum(dy * xh, axis=0, keepdims=True)

        for ch in range(4):
            sl = slice(128 * ch, 128 * (ch + 1))
            dx, dg = hnorm_bwd(qa[:, sl], gfq_r[:, sl], dqf_r[:, sl] * 0.125)
            dp_o[:, QA + 128 * ch:QA + 128 * (ch + 1)] = dx.astype(CDT)
            acc[0:1, sl] += dg
            dx, dg = hnorm_bwd(ka[:, sl], gfk_r[:, sl], dkf_r[:, sl])
            dp_o[:, KA + 128 * ch:KA + 128 * (ch + 1)] = dx.astype(CDT)
            acc[1:2, sl] += dg
            dx, dg = hnorm_bwd(qb[:, sl], gsq_r[:, sl], dqs_r[:, sl] * 0.125)
            dp_o[:, QB + 128 * ch:QB + 128 * (ch + 1)] = dx.astype(CDT)
            acc[2:3, sl] += dg
        dp_o[:, VA:VA + 512] = dvf_r[...].astype(CDT)
        dp_o[:, GA:GA + D] = dga_r[...]
        dp_o[:, GB:GB + D] = dgb_r[...]

        def fold(x):
            e0 = x[0]
            e1 = x[1]
            return jnp.where(lo, e0 + pltpu.roll(e0, 64, 1), e1 + pltpu.roll(e1, 64, 1))

        dx, dg = hnorm_bwd(kb[...], gsk_r[...], fold(dkse_r))
        dp_o[:, KB:KB + 128] = dx.astype(CDT)
        acc[3:4, 0:128] += dg
        dp_o[:, VB:VB + 128] = fold(dvse_r).astype(CDT)

        rr = _iota((512, 128), 0)
        hh = _iota((512, 128), 1)
        sel = ((rr == (hh >> 1) * 128 + (hh & 1) * HD) & (hh < 8)).astype(F32)
        dcs = _dot_hi(dcq_r[...] - dck_r[...], sel)
        utri = (_iota((tm, tm), 1) >= _iota((tm, tm), 0)).astype(F32)
        dlf = _dot_hi(utri, dcs) + carry[0:1, :]
        carry[0:1, :] = dlf[0:1, :]
        z = fa[...] + fb_r[...]
        dfa = jnp.where(lane < 8, dlf * _sigmoid(-z), 0.0)
        dp_o[:, FA:FA + 128] = dfa.astype(CDT)
        acc[4:5, 0:128] += jnp.sum(dfa, axis=0, keepdims=True)

        @pl.when(i == nt - 1)
        def _():
            foldm = ((_iota((512, 128), 0) & (HD - 1)) == _iota((512, 128), 1)).astype(F32)
            dgn_o[...] = _dot_hi(acc[...], foldm)

    def col(width, off):
        return pl.BlockSpec((tm, width), lambda i: (nt - 1 - i, off // width))

    def rows(width):
        return pl.BlockSpec((tm, width), lambda i: (nt - 1 - i, 0))

    def vec(width):
        return pl.BlockSpec((1, width), lambda i: (0, 0))

    pair = pl.BlockSpec((2, tm, 128), lambda i: (0, nt - 1 - i, 0))
    return pl.pallas_call(
        body, name=name, grid=(nt,),
        in_specs=[col(512, QA), col(512, KA), col(512, QB), col(128, KB), col(128, FA),
                  rows(512), rows(512), rows(512), rows(512), pair, pair, rows(512), rows(512), rows(D), rows(D),
                  vec(512), vec(512), vec(512), vec(128), vec(128)],
        out_specs=[rows(DP), pl.BlockSpec((8, 128), lambda i: (0, 0))],
        out_shape=[_sds((t, DP), CDT), _sds((8, 128), F32)],
        scratch_shapes=[pltpu.VMEM((8, 128), F32), pltpu.VMEM((8, 512), F32)],
        compiler_params=_params(("arbitrary",)),
    )(proj, proj, proj, proj, proj, dqf, dkf, dvf, dqs, dkse, dvse, dcq, dck, dga, dgb, gfq, gfk, gsq, gsk, fb)


def _gate_fwd(ofox, oswa, wbf, wbs, proj, name):
    t = ofox.shape[0]
    tm = _row_tile(t)
    tn = 512

    def body(of_r, os_r, wf_r, ws_r, ga_r, gb_r, y_o, yt_o, pf_o, ps_o, oft_o, ost_o):
        j = pl.program_id(1)
        pf = _dot(of_r[...], wf_r[...])
        ps = _dot(os_r[...], ws_r[...])
        y = _sigmoid(ga_r[...]) * pf + _sigmoid(gb_r[...]) * ps
        y_o[...] = y.astype(CDT)
        yt_o[...] = y.T.astype(CDT)
        pf_o[...] = pf.astype(CDT)
        ps_o[...] = ps.astype(CDT)

        @pl.when(j == 0)
        def _():
            oft_o[...] = of_r[...].astype(F32).T.astype(CDT)
            ost_o[...] = os_r[...].astype(F32).T.astype(CDT)

    tile = pl.BlockSpec((tm, tn), lambda i, j: (i, j))
    return pl.pallas_call(
        body, name=name, grid=(t // tm, D // tn),
        in_specs=[pl.BlockSpec((tm, 512), lambda i, j: (i, 0)), pl.BlockSpec((tm, 512), lambda i, j: (i, 0)),
                  pl.BlockSpec((512, tn), lambda i, j: (0, j)), pl.BlockSpec((512, tn), lambda i, j: (0, j)),
                  pl.BlockSpec((tm, tn), lambda i, j: (i, GA // tn + j)),
                  pl.BlockSpec((tm, tn), lambda i, j: (i, GB // tn + j))],
        out_specs=[tile, pl.BlockSpec((tn, tm), lambda i, j: (j, i)), tile, tile,
                   pl.BlockSpec((512, tm), lambda i, j: (0, i)), pl.BlockSpec((512, tm), lambda i, j: (0, i))],
        out_shape=[_sds((t, D), CDT), _sds((D, t), CDT), _sds((t, D), CDT), _sds((t, D), CDT),
                   _sds((512, t), CDT), _sds((512, t), CDT)],
        compiler_params=_params(("parallel", "arbitrary")),
    )(ofox, oswa, wbf, wbs, proj, proj)


def _gate_bwd(dy, pf, ps, proj, name):
    t = dy.shape[0]
    tm = _row_tile(t)
    tn = 512

    def body(dy_r, pf_r, ps_r, ga_r, gb_r, dpf_o, dps_o, dga_o, dgb_o):
        dy_ = dy_r[...]
        sa = _sigmoid(ga_r[...])
        sb = _sigmoid(gb_r[...])
        dpf_o[...] = (dy_ * sa).astype(CDT)
        dps_o[...] = (dy_ * sb).astype(CDT)
        dga_o[...] = (dy_ * pf_r[...].astype(F32) * (sa * (1.0 - sa))).astype(CDT)
        dgb_o[...] = (dy_ * ps_r[...].astype(F32) * (sb * (1.0 - sb))).astype(CDT)

    tile = pl.BlockSpec((tm, tn), lambda i, j: (i, j))
    return pl.pallas_call(
        body, name=name, grid=(t // tm, D // tn),
        in_specs=[tile, tile, tile,
                  pl.BlockSpec((tm, tn), lambda i, j: (i, GA // tn + j)),
                  pl.BlockSpec((tm, tn), lambda i, j: (i, GB // tn + j))],
        out_specs=[tile] * 4,
        out_shape=[_sds((t, D), CDT)] * 4,
        compiler_params=_params(("parallel", "parallel")),
    )(dy, pf, ps, proj, proj)


def _tri_steps(n, by_key):
    if by_key:
        pairs = [(i, j) for j in range(n) for i in range(j, n)]
    else:
        pairs = [(i, j) for i in range(n) for j in range(i + 1)]
    return (np.array([p[0] for p in pairs], np.int32), np.array([p[1] for p in pairs], np.int32))


def _head_col(blk, lane, h):
    return jnp.sum(jnp.where(lane == h, blk, 0.0), axis=1, keepdims=True)


def _head_row(blk, sub, h):
    return jnp.sum(jnp.where(sub == h, blk, 0.0), axis=0, keepdims=True)


def _ride_specs(ride):
    if ride is None:
        return [], [], [], [], []
    kind, srcs, outs, layer = ride
    return list(srcs), [ANY] * len(srcs), list(outs), [ANY] * len(outs), _dma_sems(3 * len(srcs))


def _ride_start(ride, srcs, dsts, send_sems, recv_sems):
    for cp in _ici_copies(ride[0], srcs, dsts, send_sems, recv_sems, ride[3], recv=False)[0]:
        cp.start()


def _ride_wait(ride, srcs, dsts, send_sems, recv_sems):
    sends, recvs = _ici_copies(ride[0], srcs, dsts, send_sems, recv_sems, ride[3])
    for cp in recvs:
        cp.wait_recv()
    for cp in sends:
        cp.wait_send()


def _fox_fwd(qf, kf, vf, c, ct, name, ride=None):
    t = qf.shape[0]
    ta = _row_tile(t)
    qi, kj = _tri_steps(t // ta, by_key=False)
    nsteps = len(qi)
    ride_in, ride_in_specs, ride_out, ride_out_specs, ride_sems = _ride_specs(ride)

    def body(qi_r, kj_r, q_r, k_r, v_r, c_r, ct_r, *rest):
        nr = len(ride_in)
        src_r, (o_o, lse_o), dst_o = rest[:nr], rest[nr:nr + 2], rest[nr + 2:2 * nr + 2]
        m_sc, l_sc, acc_sc, cq_sc, *sems = rest[2 * nr + 2:]
        p = pl.program_id(0)
        n = pl.program_id(1)
        i = qi_r[n]
        j = kj_r[n]
        lane = _iota((1, 128), 1)
        lo = lane < HD

        if ride is not None:
            @pl.when((p == 0) & (n == 0))
            def _():
                _ride_start(ride, src_r, dst_o, *sems)

        @pl.when(j == 0)
        def _():
            m_sc[...] = jnp.full(m_sc.shape, NEG, F32)
            l_sc[...] = jnp.zeros(l_sc.shape, F32)
            acc_sc[...] = jnp.zeros(acc_sc.shape, F32)
            for e in (0, 1):
                cq_sc[e] = jnp.broadcast_to(_head_col(c_r[...], lane, 2 * p + e), (ta, 128))

        def step(masked):
            q = q_r[...]
            k = k_r[...]
            vaug = jnp.concatenate([v_r[...], jnp.ones((ta, 128), CDT)], axis=1)
            if masked:
                rows = i * ta + _iota((ta, 1), 0)
                cols = j * ta + _iota((1, ta), 1)
                mask = (cols <= rows) & (cols >= PAD)
            sub = _iota((8, 1), 0)
            alphas, pvs = [], []
            for e in (0, 1):
                sel = lo if e == 0 else jnp.logical_not(lo)
                s = _dot_nt(jnp.where(sel, q, 0), k)
                ck = _head_row(ct_r[...], sub, 2 * p + e)
                cq = cq_sc[e]
                chunks = []
                for ch in range(ta // 128):
                    sl = slice(128 * ch, 128 * (ch + 1))
                    sc = s[:, sl] + cq - ck[:, sl]
                    if masked:
                        sc = jnp.where(mask[:, sl], sc, NEG)
                    chunks.append(sc)
                mx = chunks[0]
                for sc in chunks[1:]:
                    mx = jnp.maximum(mx, sc)
                m_prev = m_sc[e]
                m_new = jnp.maximum(m_prev, jnp.max(mx, axis=1, keepdims=True))
                alpha = jnp.exp(m_prev - m_new)
                pe = jnp.concatenate([jnp.exp(sc - m_new).astype(CDT) for sc in chunks], axis=1)
                pva = _dot(pe, vaug)
                l_sc[e] = alpha * l_sc[e] + pva[:, 128:]
                m_sc[e] = m_new
                alphas.append(alpha)
                pvs.append(pva[:, :128])
            acc_sc[...] = acc_sc[...] * jnp.where(lo, alphas[0], alphas[1]) + jnp.where(lo, pvs[0], pvs[1])

        edge = (j == i) | (j == 0)

        @pl.when(edge)
        def _():
            step(True)

        @pl.when(jnp.logical_not(edge))
        def _():
            step(False)

        @pl.when(j == i)
        def _():
            l = jnp.where(lo, l_sc[0], l_sc[1])
            o_o[...] = (acc_sc[...] / l).astype(CDT)
            lse_o[...] = jnp.where(lo, m_sc[0], m_sc[1]) + jnp.log(l)

        if ride is not None:
            @pl.when((p == NPAIR - 1) & (n == nsteps - 1))
            def _():
                _ride_wait(ride, src_r, dst_o, *sems)

    qblk = pl.BlockSpec((ta, 128), lambda p, n, qi_r, kj_r: (qi_r[n], p))
    kblk = pl.BlockSpec((ta, 128), lambda p, n, qi_r, kj_r: (kj_r[n], p))
    grid_spec = pltpu.PrefetchScalarGridSpec(
        num_scalar_prefetch=2, grid=(NPAIR, nsteps),
        in_specs=[qblk, kblk, kblk,
                  pl.BlockSpec((ta, 128), lambda p, n, qi_r, kj_r: (qi_r[n], 0)),
                  pl.BlockSpec((8, ta), lambda p, n, qi_r, kj_r: (0, kj_r[n]))] + ride_in_specs,
        out_specs=[qblk, qblk] + ride_out_specs,
        scratch_shapes=[pltpu.VMEM((2, ta, 128), F32), pltpu.VMEM((2, ta, 128), F32), pltpu.VMEM((ta, 128), F32),
                        pltpu.VMEM((2, ta, 128), F32)] + ride_sems,
    )
    return pl.pallas_call(
        body, name=name, grid_spec=grid_spec,
        out_shape=[_sds((t, 512), CDT), _sds((t, 512), F32)] + ride_out,
        compiler_params=_params(("arbitrary", "arbitrary")),
    )(jnp.asarray(qi), jnp.asarray(kj), qf, kf, vf, c, ct, *ride_in)


def _fox_bwd(qf, qft, kf, vf, c, ct, o, lse, do, dot, name, ride=None):
    t = qf.shape[0]
    ta = _row_tile(t)
    nq = t // ta
    qi, kj = _tri_steps(nq, by_key=False)
    nsteps = len(qi)
    ride_in, ride_in_specs, ride_out, ride_out_specs, ride_sems = _ride_specs(ride)

    def body(qi_r, kj_r, q_r, qt_r, k_r, v_r, c_r, ct_r, o_r, lse_r, do_r, dot_r, *rest):
        nr = len(ride_in)
        src_r, (dq_o, dcq_o, dk_o, dv_o, dck_o), dst_o = rest[:nr], rest[nr:nr + 5], rest[nr + 5:2 * nr + 5]
        lse_sc, dl_sc, cq_sc, dq_sc, dcq_sc, dkt_sc, dvt_sc, dckt_sc, *sems = rest[2 * nr + 5:]
        p = pl.program_id(0)
        n = pl.program_id(1)
        i = qi_r[n]
        j = kj_r[n]
        lane = _iota((1, 128), 1)
        lo = lane < HD
        top = _iota((128, 1), 0) < HD

        if ride is not None:
            @pl.when((p == 0) & (n == 0))
            def _():
                _ride_start(ride, src_r, dst_o, *sems)

        @pl.when(n == 0)
        def _():
            dkt_sc[...] = jnp.zeros(dkt_sc.shape, F32)
            dvt_sc[...] = jnp.zeros(dvt_sc.shape, F32)
            dckt_sc[...] = jnp.zeros(dckt_sc.shape, F32)

        @pl.when(j == 0)
        def _():
            dq_sc[...] = jnp.zeros(dq_sc.shape, F32)
            dcq_sc[...] = jnp.zeros(dcq_sc.shape, F32)
            dd = do_r[...] * o_r[...].astype(F32)
            lse = lse_r[...]
            for e in (0, 1):
                sel = lo if e == 0 else jnp.logical_not(lo)
                cq_sc[e] = jnp.broadcast_to(_head_col(c_r[...], lane, 2 * p + e), (ta, 128))
                dl_sc[e] = jnp.broadcast_to(jnp.sum(jnp.where(sel, dd, 0.0), axis=1, keepdims=True), (ta, 128))
                lse_sc[e] = jnp.broadcast_to(lse[:, HD * e:HD * e + 1], (ta, 128))

        def step(masked):
            q = q_r[...]
            qt = qt_r[...]
            k = k_r[...]
            v = v_r[...]
            dob = do_r[...].astype(CDT)
            dot_ = dot_r[...]
            ones = jnp.ones((ta, 128), CDT)
            ones16 = jnp.ones((16, ta), CDT)
            if masked:
                rows = i * ta + _iota((ta, 1), 0)
                cols = j * ta + _iota((1, ta), 1)
                mask = (cols <= rows) & (cols >= PAD)
            sub = _iota((8, 1), 0)
            for e in (0, 1):
                sel = lo if e == 0 else jnp.logical_not(lo)
                rsel = top if e == 0 else jnp.logical_not(top)
                s = _dot_nt(jnp.where(sel, q, 0), k)
                dp = _dot_nt(jnp.where(sel, dob, 0), v)
                ck = _head_row(ct_r[...], sub, 2 * p + e)
                cq, lse_e, dl = cq_sc[e], lse_sc[e], dl_sc[e]
                prs, dss = [], []
                for ch in range(ta // 128):
                    sl = slice(128 * ch, 128 * (ch + 1))
                    sc = s[:, sl] + cq - ck[:, sl]
                    if masked:
                        sc = jnp.where(mask[:, sl], sc, NEG)
                    pr = jnp.exp(sc - lse_e)
                    prs.append(pr.astype(CDT))
                    dss.append((pr * (dp[:, sl] - dl)).astype(CDT))
                pb = jnp.concatenate(prs, axis=1)
                dsb = jnp.concatenate(dss, axis=1)
                dvt_sc[j] += _dot(jnp.where(rsel, dot_, 0), pb)
                dkc = _dot(jnp.concatenate([jnp.where(rsel, qt, 0), ones16], axis=0), dsb)
                dkt_sc[j] += dkc[0:128]
                dckt_sc[j, 0:8, :] += jnp.where(sub == e, dkc[128:136], 0.0)
                dqa = _dot(dsb, jnp.concatenate([jnp.where(sel, k, 0), ones], axis=1))
                dq_sc[...] += dqa[:, :128]
                dcq_sc[e] += dqa[:, 128:]

        edge = (j == i) | (j == 0)

        @pl.when(edge)
        def _():
            step(True)

        @pl.when(jnp.logical_not(edge))
        def _():
            step(False)

        @pl.when(j == i)
        def _():
            dq_o[...] = dq_sc[...]
            dcq_o[...] = jnp.where(lo, dcq_sc[0], dcq_sc[1])

        @pl.when(n == nsteps - 1)
        def _():
            spread = (_iota((128, 128), 1) == _iota((128, 128), 0) // HD).astype(F32)
            for jb in range(nq):
                rs = slice(jb * ta, (jb + 1) * ta)
                dk_o[rs, :] = dkt_sc[jb].T
                dv_o[rs, :] = dvt_sc[jb].T
                dck_o[rs, :] = _dot_hi(spread, dckt_sc[jb]).T

        if ride is not None:
            @pl.when((p == NPAIR - 1) & (n == nsteps - 1))
            def _():
                _ride_wait(ride, src_r, dst_o, *sems)

    qblk = pl.BlockSpec((ta, 128), lambda p, n, qi_r, kj_r: (qi_r[n], p))
    qtblk = pl.BlockSpec((128, ta), lambda p, n, qi_r, kj_r: (p, qi_r[n]))
    kblk = pl.BlockSpec((ta, 128), lambda p, n, qi_r, kj_r: (kj_r[n], p))
    whole = pl.BlockSpec((t, 128), lambda p, n, qi_r, kj_r: (0, p))
    grid_spec = pltpu.PrefetchScalarGridSpec(
        num_scalar_prefetch=2, grid=(NPAIR, nsteps),
        in_specs=[qblk, qtblk, kblk, kblk,
                  pl.BlockSpec((ta, 128), lambda p, n, qi_r, kj_r: (qi_r[n], 0)),
                  pl.BlockSpec((8, ta), lambda p, n, qi_r, kj_r: (0, kj_r[n])),
                  qblk, qblk, qblk, qtblk] + ride_in_specs,
        out_specs=[qblk, qblk, whole, whole, whole] + ride_out_specs,
        scratch_shapes=[pltpu.VMEM((2, ta, 128), F32)] * 3 + [pltpu.VMEM((ta, 128), F32), pltpu.VMEM((2, ta, 128), F32)]
        + [pltpu.VMEM((nq, 128, ta), F32)] * 3 + ride_sems,
    )
    return pl.pallas_call(
        body, name=name, grid_spec=grid_spec,
        out_shape=[_sds((t, 512), F32)] * 5 + ride_out,
        compiler_params=_params(("arbitrary", "arbitrary")),
    )(jnp.asarray(qi), jnp.asarray(kj), qf, qft, kf, vf, c, ct, o, lse, do, dot, *ride_in)


def _bucket_table():
    r = np.arange(BLK)[:, None]
    c = np.arange(3 * BLK)[None, :]
    d = np.where(c < BLK, r + BLK - c, r - (c - BLK))
    n = np.maximum(d, 0)
    max_exact = N_BUCKETS // 2
    nf = np.maximum(n, 1).astype(np.float32)
    large = max_exact + (np.log(nf / max_exact) / math.log(BLK / max_exact) * (N_BUCKETS - max_exact)).astype(np.int32)
    large = np.minimum(large, N_BUCKETS - 1)
    b = np.where(n < max_exact, n, large)
    return np.where(c < 2 * BLK, b, N_BUCKETS - 1).astype(np.int32)


def _bias_fwd(table, name):
    bucket = jnp.asarray(_bucket_table())

    def body(tab_r, b_r, o_o):
        h = pl.program_id(0)
        b = b_r[...]
        acc = jnp.zeros(b.shape, F32)
        for k in range(N_BUCKETS):
            acc = jnp.where(b == k, tab_r[k, h], acc)
        o_o[...] = acc

    return pl.pallas_call(
        body, name=name, grid=(8,),
        in_specs=[pl.BlockSpec(memory_space=pltpu.SMEM), pl.BlockSpec((BLK, 3 * BLK), lambda h: (0, 0))],
        out_specs=pl.BlockSpec((None, BLK, 3 * BLK), lambda h: (h, 0, 0)),
        out_shape=_sds((8, BLK, 3 * BLK), F32),
        compiler_params=_params(("parallel",)),
    )(table, bucket)


def _bias_bwd(dbias, name):
    bucket = jnp.asarray(_bucket_table())

    def body(d_r, b_r, o_o):
        h = pl.program_id(0)
        b = b_r[...]
        d = d_r[...]
        lane = _iota((1, 128), 1)
        row = jnp.zeros((1, 128), F32)
        for k in range(N_BUCKETS):
            row = jnp.where(lane == k, jnp.sum(jnp.where(b == k, d, 0.0)), row)
        o_o[pl.ds(h, 1), :] = row

    return pl.pallas_call(
        body, name=name, grid=(8,),
        in_specs=[pl.BlockSpec((None, BLK, 3 * BLK), lambda h: (h, 0, 0)), pl.BlockSpec((BLK, 3 * BLK), lambda h: (0, 0))],
        out_specs=pl.BlockSpec((8, 128), lambda h: (0, 0)),
        out_shape=_sds((8, 128), F32),
        compiler_params=_params(("arbitrary",)),
    )(dbias, bucket)


def _swa_valid(i):
    r = _iota((BLK, 1), 0)
    c = _iota((1, 3 * BLK), 1)
    prev = (c < BLK) & (c > r) & (i >= 1) & ((i - 1) * BLK + c >= PAD)
    cc = c - BLK
    cur = (c >= BLK) & (c < 2 * BLK) & (cc <= r) & (i * BLK + cc >= PAD)
    cm = c - 2 * BLK
    meta = (c >= 2 * BLK) & (cm >= PAD) & (i * BLK + r - cm >= BLK)
    return prev | cur | meta


def _swa_kv_specs(ta):
    nb = ta // BLK
    return [pl.BlockSpec((None, BLK, 128), lambda p, i: (p // 2, jnp.maximum(i * nb - 1, 0), 0)),
            pl.BlockSpec((None, ta, 128), lambda p, i: (p // 2, i, 0)),
            pl.BlockSpec((None, BLK, 128), lambda p, i: (p // 2, 0, 0))]


def _swa_fwd(qs, kse, vse, bias, sinks, name):
    t = qs.shape[0]
    ta = _row_tile(t)
    nb = ta // BLK

    def body(sink_r, q_r, kp_r, kc_r, km_r, vp_r, vc_r, vm_r, b_r, o_o, lse_o):
        p = pl.program_id(0)
        i = pl.program_id(1)
        lo = _iota((1, 128), 1) < HD
        k4 = jnp.concatenate([kp_r[...], kc_r[...]], axis=0)
        v4 = jnp.concatenate([vp_r[...], vc_r[...]], axis=0)
        for b in range(nb):
            rows = slice(BLK * b, BLK * (b + 1))
            q = q_r[rows, :]
            k3 = jnp.concatenate([k4[BLK * b:BLK * (b + 2)], km_r[...]], axis=0)
            v3 = jnp.concatenate([v4[BLK * b:BLK * (b + 2)], vm_r[...]], axis=0)
            valid = _swa_valid(i * nb + b)
            outs, lses = [], []
            for e in (0, 1):
                sel = lo if e == 0 else jnp.logical_not(lo)
                s = _dot_nt(jnp.where(sel, q, 0), k3) + b_r[e]
                s = jnp.where(valid, s, NEG)
                sink = sink_r[2 * p + e]
                mx = jnp.maximum(jnp.max(s, axis=1, keepdims=True), sink)
                pe = jnp.exp(s - mx)
                den = jnp.sum(pe, axis=1, keepdims=True) + jnp.exp(sink - mx)
                outs.append(_dot(pe.astype(CDT), v3) / den)
                lses.append(mx + jnp.log(den))
            o_o[rows, :] = jnp.where(lo, outs[0], outs[1]).astype(CDT)
            lse_o[rows, :] = jnp.where(lo, lses[0], lses[1])

    qblk = pl.BlockSpec((ta, 128), lambda p, i: (i, p))
    return pl.pallas_call(
        body, name=name, grid=(NPAIR, t // ta),
        in_specs=[pl.BlockSpec(memory_space=pltpu.SMEM), qblk] + _swa_kv_specs(ta) + _swa_kv_specs(ta)
        + [pl.BlockSpec((2, BLK, 3 * BLK), lambda p, i: (p, 0, 0))],
        out_specs=[qblk, qblk],
        out_shape=[_sds((t, 512), CDT), _sds((t, 512), F32)],
        compiler_params=_params(("parallel", "parallel")),
    )(sinks, qs, kse, kse, kse, vse, vse, vse, bias)


def _swa_bwd(qs, kse, vse, bias, sinks, o, lse, do, name):
    t = qs.shape[0]
    ta = _row_tile(t)
    nb = ta // BLK

    def body(sink_r, q_r, kp_r, kc_r, km_r, vp_r, vc_r, vm_r, b_r, o_r, lse_r, do_r,
             dq_o, dk_o, dv_o, db_o, dsk_o):
        p = pl.program_id(0)
        i = pl.program_id(1)
        lo = _iota((1, 128), 1) < HD

        @pl.when((i == 0) & (p % 2 == 0))
        def _():
            dk_o[...] = jnp.zeros(dk_o.shape, F32)
            dv_o[...] = jnp.zeros(dv_o.shape, F32)

        @pl.when(i == 0)
        def _():
            db_o[...] = jnp.zeros(db_o.shape, F32)
            dsk_o[...] = jnp.zeros(dsk_o.shape, F32)

        k4 = jnp.concatenate([kp_r[...], kc_r[...]], axis=0)
        v4 = jnp.concatenate([vp_r[...], vc_r[...]], axis=0)
        for b in range(nb):
            ib = i * nb + b
            rows = slice(BLK * b, BLK * (b + 1))
            q = q_r[rows, :]
            do_ = do_r[rows, :]
            dd = do_ * o_r[rows, :].astype(F32)
            lse = lse_r[rows, :]
            k3 = jnp.concatenate([k4[BLK * b:BLK * (b + 2)], km_r[...]], axis=0)
            v3 = jnp.concatenate([v4[BLK * b:BLK * (b + 2)], vm_r[...]], axis=0)
            valid = _swa_valid(ib)
            dq = jnp.zeros((BLK, 128), F32)
            dk3 = jnp.zeros((3 * BLK, 128), F32)
            dv3 = jnp.zeros((3 * BLK, 128), F32)
            dsink = []
            for e in (0, 1):
                sel = lo if e == 0 else jnp.logical_not(lo)
                qe = jnp.where(sel, q, 0)
                doe = jnp.where(sel, do_, 0.0).astype(CDT)
                lse_e = lse[:, HD * e:HD * e + 1]
                s = _dot_nt(qe, k3) + b_r[e]
                s = jnp.where(valid, s, NEG)
                pr = jnp.exp(s - lse_e)
                delta = jnp.sum(jnp.where(sel, dd, 0.0), axis=1, keepdims=True)
                ds = pr * (_dot_nt(doe, v3) - delta)
                db_o[e] += ds
                dsink.append(-jnp.sum(jnp.exp(sink_r[2 * p + e] - lse_e) * delta, axis=0, keepdims=True))
                dq = dq + _dot(ds.astype(CDT), jnp.where(sel, k3, 0))
                dk3 = dk3 + _dot(ds.T.astype(CDT), qe)
                dv3 = dv3 + _dot(pr.T.astype(CDT), doe)
            dq_o[rows, :] = dq
            prev = pl.ds(pl.multiple_of(jnp.maximum(ib - 1, 0) * BLK, BLK), BLK)
            cur = pl.ds(pl.multiple_of(ib * BLK, BLK), BLK)
            dk_o[prev, :] += dk3[0:BLK]
            dk_o[cur, :] += dk3[BLK:2 * BLK]
            dk_o[0:BLK, :] += dk3[2 * BLK:]
            dv_o[prev, :] += dv3[0:BLK]
            dv_o[cur, :] += dv3[BLK:2 * BLK]
            dv_o[0:BLK, :] += dv3[2 * BLK:]
            dsk_o[0:1, :] += jnp.where(lo, dsink[0], dsink[1])

    qblk = pl.BlockSpec((ta, 128), lambda p, i: (i, p))
    kvacc = pl.BlockSpec((None, t, 128), lambda p, i: (p // 2, 0, 0))
    bblk = pl.BlockSpec((2, BLK, 3 * BLK), lambda p, i: (p, 0, 0))
    return pl.pallas_call(
        body, name=name, grid=(NPAIR, t // ta),
        in_specs=[pl.BlockSpec(memory_space=pltpu.SMEM), qblk] + _swa_kv_specs(ta) + _swa_kv_specs(ta)
        + [bblk, qblk, qblk, qblk],
        out_specs=[qblk, kvacc, kvacc, bblk, pl.BlockSpec((None, 8, 128), lambda p, i: (p, 0, 0))],
        out_shape=[_sds((t, 512), F32), _sds((2, t, 128), F32), _sds((2, t, 128), F32),
                   _sds((8, BLK, 3 * BLK), F32), _sds((NPAIR, 8, 128), F32)],
        compiler_params=_params(("arbitrary", "arbitrary")),
    )(sinks, qs, kse, kse, kse, vse, vse, vse, bias, o, lse, do)


def _sum8(slots, name):
    def body(a_r, o_o):
        acc = a_r[0]
        for k in range(1, 8):
            acc = acc + a_r[k]
        o_o[...] = acc

    return pl.pallas_call(
        body, name=name, out_shape=_sds((SMALL_ROWS, 128), F32),
        in_specs=[pl.BlockSpec(memory_space=pltpu.VMEM)], out_specs=pl.BlockSpec(memory_space=pltpu.VMEM),
        compiler_params=_params(),
    )(slots)


def _place():
    x, y, c = lax.axis_index("x"), lax.axis_index("y"), lax.axis_index("c")
    chips = [(1 - x, y), (x, 1 - y), (1 - x, 1 - y)]
    return x, y, c, chips


def _remote(src, dst, send_sems, recv_sems, k, to):
    return pltpu.make_async_remote_copy(src_ref=src, dst_ref=dst, send_sem=send_sems.at[k], recv_sem=recv_sems.at[k],
                                        device_id=to, device_id_type=MESH_ID)


ANY = pl.BlockSpec(memory_space=pl.ANY)


def _mix_cols(w):
    return jnp.concatenate([w[:, 2312:4360], w[:, 0:1536], w[:, 1544:2312], w[:, 1536:1544],
                            jnp.zeros((w.shape[0], DP - D_IN), w.dtype)], axis=1)


def _unmix_cols(w):
    return jnp.concatenate([w[:, QA:QA + 1536], w[:, FA:FA + 8], w[:, QB:QB + 768], w[:, GA:GA + 2048]], axis=1)


def _rows128(a, rows):
    flat = a.reshape(-1)
    return jnp.pad(flat, (0, rows * 128 - flat.shape[0])).reshape(rows, 128)


GRAD_FORM = {"ffn1_w_in": "col", "ffn2_w_in": "col", "w_branch_fox": "col", "w_branch_swa": "col",
             "ffn1_w_out": "3d", "ffn2_w_out": "3d", "w_out": "3d", "w_in": "3d"}
SUM_TILE = {1024: 128, 704: 176, 512: 128, 256: 128}
NT = len(SHARD_ITEMS)


def _half_rows(c, r):
    return pl.ds(pl.multiple_of(c * (r // 2), 16), r // 2)


def _ici_copies(kind, srcs, dsts, send_sems, recv_sems, layer, recv=True):
    x, y, c, chips = _place()
    s = 2 * x + y
    sends, recvs = [], []
    for t, ((nm, (r, cc), _), src, dst) in enumerate(zip(SHARD_ITEMS, srcs, dsts)):
        for j, (cx, cy) in enumerate(chips):
            sj = 2 * cx + cy
            k = 3 * t + j
            to = (cx, cy, c)
            if kind == "gather":
                hs = _half_rows(c, r)
                sends.append(_remote(src.at[layer, hs], dst.at[s, hs], send_sems, recv_sems, k, to))
                if recv:
                    recvs.append(_remote(src.at[layer, hs], dst.at[sj, hs], send_sems, recv_sems, k, to))
            else:
                if GRAD_FORM[nm] == "col":
                    piece = src.at[:, pl.ds(pl.multiple_of(sj * cc, 128), cc)]
                else:
                    piece = src.at[sj]
                sends.append(_remote(piece, dst.at[j], send_sems, recv_sems, k, to))
                recvs.append(sends[-1])
    return sends, recvs


def _slab_shapes():
    return [_sds((4, r, c), CDT) for _, (r, c), _ in SHARD_ITEMS]


def _dma_sems(n):
    return [pltpu.SemaphoreType.DMA((n,)), pltpu.SemaphoreType.DMA((n,))]


def _forward_sends(dsts, send_sems, recv_sems):
    x, y, c, chips = _place()
    sends, recvs = [], []
    for t, ((nm, (r, cc), _), dst) in enumerate(zip(SHARD_ITEMS, dsts)):
        for j, (cx, cy) in enumerate(chips):
            sj = 2 * cx + cy
            hs, ho = _half_rows(c, r), _half_rows(1 - c, r)
            sends.append(_remote(dst.at[sj, hs], dst.at[sj, hs], send_sems, recv_sems, 3 * t + j, (x, y, 1 - c)))
            recvs.append(_remote(dst.at[sj, ho], dst.at[sj, ho], send_sems, recv_sems, 3 * t + j, (x, y, 1 - c)))
    return sends, recvs


def _gather_layer(wb, mflat, layer, name):
    def body(*refs):
        srcs, m_r, dsts, mall_o = refs[:NT], refs[NT], refs[NT + 1:2 * NT + 1], refs[2 * NT + 1]
        send_sems, recv_sems, fsend, frecv, msend, mrecv = refs[2 * NT + 2:]
        x, y, c, chips = _place()
        s = 2 * x + y
        sends, recvs = _ici_copies("gather", srcs, dsts, send_sems, recv_sems, layer)
        metas = [_remote(m_r, mall_o.at[s], msend, mrecv, j, (cx, cy, c)) for j, (cx, cy) in enumerate(chips)]
        for cp in sends + metas:
            cp.start()
        fwds, frecvs = _forward_sends(dsts, fsend, frecv)
        for got, fwd in zip(recvs, fwds):
            got.wait_recv()
            fwd.start()
        for got in frecvs:
            got.wait_recv()
        for j, (cx, cy) in enumerate(chips):
            _remote(m_r, mall_o.at[2 * cx + cy], msend, mrecv, j, (cx, cy, c)).wait_recv()
        for cp in sends + metas + fwds:
            cp.wait_send()

    return pl.pallas_call(
        body, name=name, out_shape=_slab_shapes() + [_sds((4, META_ROWS, 128), F32)],
        in_specs=[ANY] * (NT + 1), out_specs=[ANY] * (NT + 1),
        scratch_shapes=_dma_sems(3 * NT) + _dma_sems(3 * NT) + _dma_sems(3),
    )(*wb, mflat)


def _forward_layer(slabs, name):
    def body(*refs):
        ins, outs, send_sems, recv_sems = refs[:NT], refs[NT:2 * NT], refs[2 * NT], refs[2 * NT + 1]
        sends, recvs = _forward_sends(outs, send_sems, recv_sems)
        for cp in sends:
            cp.start()
        for cp in recvs:
            cp.wait_recv()
        for cp in sends:
            cp.wait_send()

    return pl.pallas_call(
        body, name=name, out_shape=_slab_shapes(), in_specs=[ANY] * NT, out_specs=[ANY] * NT,
        input_output_aliases={t: t for t in range(NT)}, scratch_shapes=_dma_sems(3 * NT),
    )(*slabs)


def _half_shape(nm, r, c):
    return (r // 2, 4 * c) if GRAD_FORM[nm] == "col" else (4, r // 2, c)


def _swap_layer(gs, gsm, name):
    small = gsm is not None

    def body(*refs):
        g_rs = refs[:NT]
        pos = NT
        if small:
            s_r = refs[pos]
            pos += 1
        got_os = refs[pos:pos + NT]
        pos += NT
        if small:
            slots_o = refs[pos]
            pos += 1
        send_sems, recv_sems = refs[pos], refs[pos + 1]
        x, y, c, _ = _place()
        sib = (x, y, 1 - c)
        sent = []
        for t, ((nm, (r, cc), _), g_r, got_o) in enumerate(zip(SHARD_ITEMS, g_rs, got_os)):
            ho = _half_rows(1 - c, r)
            src = g_r.at[ho, :] if GRAD_FORM[nm] == "col" else g_r.at[:, ho, :]
            sent.append(_remote(src, got_o, send_sems, recv_sems, t, sib))
        if small:
            ssend, srecv, loc_sem = refs[pos + 2], refs[pos + 3], refs[pos + 4]
            me = 4 * x + 2 * y + c
            loc = pltpu.make_async_copy(s_r, slots_o.at[me], loc_sem.at[0])
            loc.start()
            peers = [(x ^ (k >> 2), y ^ ((k >> 1) & 1), c ^ (k & 1)) for k in range(1, 8)]
            for k, peer in enumerate(peers):
                sent.append(_remote(s_r, slots_o.at[me], ssend, srecv, k, peer))
        for cp in sent:
            cp.start()
        for cp in sent[:NT]:
            cp.wait_recv()
        if small:
            for k, (px, py, pc) in enumerate(peers):
                _remote(s_r, slots_o.at[4 * px + 2 * py + pc], ssend, srecv, k, (px, py, pc)).wait_recv()
        for cp in sent:
            cp.wait_send()
        if small:
            loc.wait()

    outs = [_sds(_half_shape(nm, r, c), CDT) for nm, (r, c), _ in SHARD_ITEMS]
    ops = list(gs)
    sems = _dma_sems(NT)
    if small:
        outs.append(_sds((8, SMALL_ROWS, 128), F32))
        ops.append(gsm)
        sems = sems + _dma_sems(7) + [pltpu.SemaphoreType.DMA((1,))]
    res = pl.pallas_call(
        body, name=name, out_shape=outs, in_specs=[ANY] * len(ops), out_specs=[ANY] * len(outs), scratch_shapes=sems,
    )(*ops)
    return (res[:NT], res[NT]) if small else (res, None)


def _pair_add_t(own, got, half_idx, nm, r, name):
    tr = SUM_TILE[r]
    nb = (r // 2) // tr
    if GRAD_FORM[nm] == "col":
        blk = (tr, own.shape[1])
        own_spec = pl.BlockSpec(blk, lambda i, c_r: (c_r[0] * nb + i, 0))
        half_spec = pl.BlockSpec(blk, lambda i, c_r: (i, 0))
    else:
        blk = (4, tr, own.shape[2])
        own_spec = pl.BlockSpec(blk, lambda i, c_r: (0, c_r[0] * nb + i, 0))
        half_spec = pl.BlockSpec(blk, lambda i, c_r: (0, i, 0))

    def body(c_r, a_r, b_r, o_o):
        o_o[...] = (a_r[...].astype(F32) + b_r[...].astype(F32)).astype(CDT)

    grid_spec = pltpu.PrefetchScalarGridSpec(num_scalar_prefetch=1, grid=(nb,), in_specs=[own_spec, half_spec],
                                             out_specs=half_spec)
    return pl.pallas_call(body, name=name, grid_spec=grid_spec, out_shape=_sds(got.shape, CDT),
                          compiler_params=_params(("parallel",)))(half_idx, own, got)


def _sum4_t(ps, got3, buf, idx, layer, nm, r, name):
    tr = SUM_TILE[r]
    nb = (r // 2) // tr
    c = got3.shape[2]
    if GRAD_FORM[nm] == "col":
        ps_spec = pl.BlockSpec((tr, c), lambda i, x_r: (i, x_r[0]))
    else:
        ps_spec = pl.BlockSpec((None, tr, c), lambda i, x_r: (x_r[0], i, 0))

    def body(x_r, a_r, b_r, buf_r, o_o):
        o_o[...] = ((a_r[...].astype(F32) + b_r[0].astype(F32)) + b_r[1].astype(F32)) + b_r[2].astype(F32)

    grid_spec = pltpu.PrefetchScalarGridSpec(
        num_scalar_prefetch=1, grid=(nb,),
        in_specs=[ps_spec, pl.BlockSpec((3, tr, c), lambda i, x_r: (0, i, 0)), ANY],
        out_specs=pl.BlockSpec((None, tr, c), lambda i, x_r: (layer, x_r[1] * nb + i, 0)),
    )
    return pl.pallas_call(body, name=name, grid_spec=grid_spec, out_shape=_sds(buf.shape, F32),
                          input_output_aliases={3: 0}, compiler_params=_params(("parallel",)))(idx, ps, got3, buf)


def _scatter_layer(ps, name):
    def body(*refs):
        srcs, dsts, send_sems, recv_sems = refs[:NT], refs[NT:2 * NT], refs[2 * NT], refs[2 * NT + 1]
        sends, recvs = _ici_copies("scatter", srcs, dsts, send_sems, recv_sems, None)
        for cp in sends:
            cp.start()
        for cp in recvs:
            cp.wait_recv()
        for cp in sends:
            cp.wait_send()

    return pl.pallas_call(
        body, name=name, out_shape=_got3_shapes(), in_specs=[ANY] * NT, out_specs=[ANY] * NT,
        scratch_shapes=_dma_sems(3 * NT),
    )(*ps)


def _got3_shapes():
    return [_sds((3, r // 2, c), CDT) for _, (r, c), _ in SHARD_ITEMS]


def _join_layer(bufs, name):
    def body(*refs):
        ins, outs, send_sems, recv_sems = refs[:NT], refs[NT:2 * NT], refs[2 * NT], refs[2 * NT + 1]
        x, y, c, _ = _place()
        sent = []
        for t, ((nm, (r, cc), _), b_o) in enumerate(zip(SHARD_ITEMS, outs)):
            hs = _half_rows(c, r)
            sent.append(_remote(b_o.at[:, hs, :], b_o.at[:, hs, :], send_sems, recv_sems, t, (x, y, 1 - c)))
        for cp in sent:
            cp.start()
        for t, ((nm, (r, cc), _), b_o) in enumerate(zip(SHARD_ITEMS, outs)):
            ho = _half_rows(1 - c, r)
            _remote(b_o.at[:, ho, :], b_o.at[:, ho, :], send_sems, recv_sems, t, (x, y, 1 - c)).wait_recv()
        for cp in sent:
            cp.wait_send()

    return pl.pallas_call(
        body, name=name, out_shape=[_sds(b.shape, F32) for b in bufs], in_specs=[ANY] * NT, out_specs=[ANY] * NT,
        input_output_aliases={t: t for t in range(NT)}, scratch_shapes=_dma_sems(NT),
    )(*bufs)


def _adamw3(w, g, m, v, name):
    nl, r, c = w.shape
    tr = SUM_TILE.get(r, r)

    def body(w_r, g_r, m_r, v_r, d_o, m_o, v_o):
        g_ = g_r[...]
        m_ = ADAM_B1 * m_r[...] + (1.0 - ADAM_B1) * g_
        v_ = ADAM_B2 * v_r[...] + (1.0 - ADAM_B2) * jnp.square(g_)
        m_hat = m_ / (1.0 - ADAM_B1 ** ADAM_STEP)
        v_hat = v_ / (1.0 - ADAM_B2 ** ADAM_STEP)
        d_o[...] = -ADAM_LR * (m_hat / (jnp.sqrt(v_hat) + ADAM_EPS) + ADAM_WD * w_r[...])
        m_o[...] = m_
        v_o[...] = v_

    blk = pl.BlockSpec((None, tr, c), lambda l, i: (l, i, 0))
    return pl.pallas_call(
        body, name=name, grid=(nl, r // tr),
        in_specs=[blk] * 4, out_specs=[blk] * 3, out_shape=[_sds((nl, r, c), F32)] * 3,
        compiler_params=_params(("parallel", "parallel")),
    )(w, g, m, v)


def _full_weights(slabs, wb, layer, shard):
    ws = {}
    for (nm, (r, c), kind), slab in zip(SHARD_ITEMS, slabs):
        slab = lax.dynamic_update_slice(slab, wb[nm][layer][None], (shard, 0, 0))
        ws[nm] = slab.reshape(4 * r, c) if kind == "row" else jnp.concatenate([slab[s] for s in range(4)], axis=1)
    return ws


def _exchange_forms(g):
    out = []
    for nm, (r, c), _ in SHARD_ITEMS:
        a = g[nm]
        if nm == "w_in":
            a = a.reshape(D, 4, c).transpose(1, 0, 2)
        elif GRAD_FORM[nm] == "3d":
            a = a.reshape(4, r, c)
        out.append(a)
    return out


SMALL_ITEMS = (("rel_bias_table", 2), ("ffn1_norm", 16), ("mix_norm", 16), ("ffn2_norm", 16), ("forget_bias", 1),
               ("fox_q_norm", 1), ("fox_k_norm", 1), ("swa_q_norm", 1), ("swa_k_norm", 1), ("swa_sinks", 1))
SMALL_ADAM_ROWS = 96


def _layer_fwd(h, lw, l, ride=None):
    sv = {"h0": h}
    a, sv["a1t"] = _rms_fwd(h, lw["ffn1_norm"], f"rms_fwd_a{l}")
    sv["gu1"], s, sv["s1t"] = _ffn_in(a, lw["ffn1_w_in"], f"ffn_in_a{l}")
    h = _mm_res(s, lw["ffn1_w_out"], h, 0.5, f"ffn_out_a{l}")
    sv["h1"] = h
    a, sv["amt"] = _rms_fwd(h, lw["mix_norm"], f"rms_fwd_m{l}")
    proj = _mm(a, lw["w_mix"], F32, _row_tile(h.shape[0]), DP, f"proj{l}")
    sv["proj"] = proj
    qf, kf, vf, qs, kse, vse, c, ct, sv["qft"] = _qknorm_fwd(proj, lw["gfq"], lw["gfk"], lw["gsq"], lw["gsk"], lw["fb"],
                                                              f"qknorm_fwd{l}")
    ofox, lse_f, *rode = _fox_fwd(qf, kf, vf, c, ct, f"fox_fwd{l}", ride)
    oswa, lse_s = _swa_fwd(qs, kse, vse, lw["bias"], lw["sinks"], f"swa_fwd{l}")
    sv.update(qf=qf, kf=kf, vf=vf, qs=qs, kse=kse, vse=vse, c=c, ct=ct, ofox=ofox, oswa=oswa, lse_f=lse_f, lse_s=lse_s)
    y, sv["yt"], sv["pf"], sv["ps"], sv["oft"], sv["ost"] = _gate_fwd(ofox, oswa, lw["w_branch_fox"], lw["w_branch_swa"],
                                                                     proj, f"gate_fwd{l}")
    h = _mm_res(y, lw["w_out"], h, 1.0, f"mix_out{l}")
    sv["h2"] = h
    a, sv["a2t"] = _rms_fwd(h, lw["ffn2_norm"], f"rms_fwd_b{l}")
    sv["gu2"], s, sv["s2t"] = _ffn_in(a, lw["ffn2_w_in"], f"ffn_in_b{l}")
    h = _mm_res(s, lw["ffn2_w_out"], h, 0.5, f"ffn_out_b{l}")
    return h, sv, rode


def _ffn_bwd(dh, dhb, h_in, at, gu, st, norm, w_in, w_out, tag):
    dgu = _ffn_bwd_mid(dhb, w_out, gu, f"ffn_bwd_mid_{tag}")
    d_w_out = _mm(st, dhb, CDT, 256, D, f"dw_ffn_out_{tag}", scale=0.5)
    dh, dhb, dg = _ffn_bwd_in(dgu, w_in, h_in, norm, dh, f"ffn_bwd_in_{tag}")
    d_w_in = _mm(at, dgu, CDT, D, 256, f"dw_ffn_in_{tag}")
    return dh, dhb, d_w_out, d_w_in, dg


def _layer_bwd(dh, dhb, sv, lw, l, ride=None):
    g = {}
    dh, dhb, g["ffn2_w_out"], g["ffn2_w_in"], g["ffn2_norm"] = _ffn_bwd(
        dh, dhb, sv["h2"], sv["a2t"], sv["gu2"], sv["s2t"], lw["ffn2_norm"], lw["ffn2_w_in"], lw["ffn2_w_out"], f"b{l}")
    dy = _mm_nt(dhb, lw["w_out"], f"d_y{l}")
    g["w_out"] = _mm(sv["yt"], dhb, CDT, 512, 512, f"dw_out{l}")
    dpf, dps, dga, dgb = _gate_bwd(dy, sv["pf"], sv["ps"], sv["proj"], f"gate_bwd{l}")
    do_f, do_ft = _mm_nt(dpf, lw["w_branch_fox"], f"d_ofox{l}", with_t=True)
    do_s = _mm_nt(dps, lw["w_branch_swa"], f"d_oswa{l}")
    g["w_branch_fox"] = _mm(sv["oft"], dpf, CDT, 512, 512, f"dw_bfox{l}")
    g["w_branch_swa"] = _mm(sv["ost"], dps, CDT, 512, 512, f"dw_bswa{l}")
    dqf, dcq, dkf, dvf, dck, *rode = _fox_bwd(sv["qf"], sv["qft"], sv["kf"], sv["vf"], sv["c"], sv["ct"], sv["ofox"],
                                              sv["lse_f"], do_f, do_ft, f"fox_bwd{l}", ride)
    g["rode"] = rode
    dqs, dkse, dvse, dbias, dsk = _swa_bwd(sv["qs"], sv["kse"], sv["vse"], lw["bias"], lw["sinks"], sv["oswa"],
                                           sv["lse_s"], do_s, f"swa_bwd{l}")
    dproj, dgn = _qknorm_bwd(sv["proj"], dqf, dkf, dvf, dqs, dkse, dvse, dcq, dck, dga, dgb,
                             lw["gfq"], lw["gfk"], lw["gsq"], lw["gsk"], lw["fb"], f"qknorm_bwd{l}")
    g["w_mix"] = _mm(sv["amt"], dproj, CDT, 512, 640, f"dw_mix{l}")
    dh, dhb, g["mix_norm"] = _mm_nt_rms(dproj, lw["w_mix"], sv["h1"], lw["mix_norm"], dh, f"d_am{l}")
    g["dbias"], g["dsk"], g["dgn"] = dbias, dsk, dgn
    dh, dhb, g["ffn1_w_out"], g["ffn1_w_in"], g["ffn1_norm"] = _ffn_bwd(
        dh, dhb, sv["h0"], sv["a1t"], sv["gu1"], sv["s1t"], lw["ffn1_norm"], lw["ffn1_w_in"], lw["ffn1_w_out"], f"a{l}")
    return dh, dhb, g


def kernel(x, meta_tokens, rel_bias_table, ffn1_norm, ffn1_w_in, ffn1_w_out, mix_norm, w_in, forget_bias, fox_q_norm, fox_k_norm, swa_q_norm, swa_k_norm, swa_sinks, w_branch_fox, w_branch_swa, w_out, ffn2_norm, ffn2_w_in, ffn2_w_out, loss_target, m_meta_tokens, m_rel_bias_table, m_ffn1_norm, m_ffn1_w_in, m_ffn1_w_out, m_mix_norm, m_w_in, m_forget_bias, m_fox_q_norm, m_fox_k_norm, m_swa_q_norm, m_swa_k_norm, m_swa_sinks, m_w_branch_fox, m_w_branch_swa, m_w_out, m_ffn2_norm, m_ffn2_w_in, m_ffn2_w_out, v_meta_tokens, v_rel_bias_table, v_ffn1_norm, v_ffn1_w_in, v_ffn1_w_out, v_mix_norm, v_w_in, v_forget_bias, v_fox_q_norm, v_fox_k_norm, v_swa_q_norm, v_swa_k_norm, v_swa_sinks, v_w_branch_fox, v_w_branch_swa, v_w_out, v_ffn2_norm, v_ffn2_w_in, v_ffn2_w_out):
    names = ["meta_tokens", "rel_bias_table", "ffn1_norm", "ffn1_w_in", "ffn1_w_out", "mix_norm", "w_in", "forget_bias",
             "fox_q_norm", "fox_k_norm", "swa_q_norm", "swa_k_norm", "swa_sinks", "w_branch_fox", "w_branch_swa", "w_out",
             "ffn2_norm", "ffn2_w_in", "ffn2_w_out"]
    w = dict(zip(names, [meta_tokens, rel_bias_table, ffn1_norm, ffn1_w_in, ffn1_w_out, mix_norm, w_in, forget_bias,
                         fox_q_norm, fox_k_norm, swa_q_norm, swa_k_norm, swa_sinks, w_branch_fox, w_branch_swa, w_out,
                         ffn2_norm, ffn2_w_in, ffn2_w_out]))
    m = dict(zip(names, [m_meta_tokens, m_rel_bias_table, m_ffn1_norm, m_ffn1_w_in, m_ffn1_w_out, m_mix_norm, m_w_in,
                         m_forget_bias, m_fox_q_norm, m_fox_k_norm, m_swa_q_norm, m_swa_k_norm, m_swa_sinks,
                         m_w_branch_fox, m_w_branch_swa, m_w_out, m_ffn2_norm, m_ffn2_w_in, m_ffn2_w_out]))
    v = dict(zip(names, [v_meta_tokens, v_rel_bias_table, v_ffn1_norm, v_ffn1_w_in, v_ffn1_w_out, v_mix_norm, v_w_in,
                         v_forget_bias, v_fox_q_norm, v_fox_k_norm, v_swa_q_norm, v_swa_k_norm, v_swa_sinks,
                         v_w_branch_fox, v_w_branch_swa, v_w_out, v_ffn2_norm, v_ffn2_w_in, v_ffn2_w_out]))
    xi, yi, ci = lax.axis_index("x"), lax.axis_index("y"), lax.axis_index("c")
    shard = 2 * xi + yi
    seq = x.shape[1]
    t = seq + BLK

    wb = {nm: w[nm].astype(CDT) for nm, _, _ in SHARD_ITEMS}
    wb_list = [wb[nm] for nm, _, _ in SHARD_ITEMS]
    mflat = meta_tokens.reshape(META_ROWS, 128)
    *slabs0, mall = _gather_layer(wb_list, mflat, 0, "gather_weights")
    mall = lax.dynamic_update_slice(mall, mflat[None], (shard, 0, 0))
    meta_full = jnp.concatenate([mall[s].reshape(N_META, 256) for s in range(4)], axis=1)
    bias = _bias_fwd(rel_bias_table, "bias_fwd")

    def layer_weights(slabs, l):
        lw = _full_weights(slabs, wb, l, shard)
        lw["w_mix"] = _mix_cols(lw.pop("w_in"))
        for nm in ("ffn1_norm", "mix_norm", "ffn2_norm"):
            lw[nm] = w[nm][l].reshape(1, D)
        lw["gfq"] = jnp.tile(fox_q_norm[l], 8).reshape(1, 512)
        lw["gfk"] = jnp.tile(fox_k_norm[l], 8).reshape(1, 512)
        lw["gsq"] = jnp.tile(swa_q_norm[l], 8).reshape(1, 512)
        lw["gsk"] = jnp.tile(swa_k_norm[l], 2).reshape(1, 128)
        lw["fb"] = jnp.pad(forget_bias[l], (0, 120)).reshape(1, 128)
        lw["sinks"] = swa_sinks[l]
        lw["bias"] = bias
        return lw

    h = jnp.concatenate([jnp.zeros((PAD, D), F32), meta_full, x[0]], axis=0)
    lws = [layer_weights(slabs0, 0)]
    h, sv0, slabs1 = _layer_fwd(h, lws[0], 0, ("gather", wb_list, _slab_shapes(), 1))
    lws.append(layer_weights(_forward_layer(slabs1, "forward_halves"), 1))
    h, sv1, _ = _layer_fwd(h, lws[1], 1)
    saved = [sv0, sv1]
    dh, dhb, lacc = _loss(h, loss_target[0], "loss")
    loss = lax.psum(lacc[0, 0], ("x", "y", "c"))

    half_idx = ci.reshape(1).astype(jnp.int32)
    place_idx = jnp.stack([shard, ci]).astype(jnp.int32)

    def pair_sums(g, gsm, tag):
        g["w_in"] = _unmix_cols(g.pop("w_mix"))
        forms = _exchange_forms(g)
        got, slots = _swap_layer(forms, gsm, f"swap_halves{tag}")
        return [_pair_add_t(a, b, half_idx, nm, r, f"pair_add{tag}_{nm}")
                for a, b, (nm, (r, c), _) in zip(forms, got, SHARD_ITEMS)], slots

    grads = [None, None]
    dh, dhb, grads[1] = _layer_bwd(dh, dhb, saved[1], lws[1], 1)
    ps1, _ = pair_sums(grads[1], None, 1)
    dh, dhb, grads[0] = _layer_bwd(dh, dhb, saved[0], lws[0], 0, ("scatter", ps1, _got3_shapes(), None))
    grad_x = dh[BLK:].reshape(1, seq, D)
    dtab = _bias_bwd(grads[0]["dbias"] + grads[1]["dbias"], "bias_bwd")

    small = [dh[PAD:BLK].reshape(128, 128), _rows128(dtab[:, :N_BUCKETS].T, 2)]
    for nm in ("ffn1_norm", "mix_norm", "ffn2_norm"):
        small.append(jnp.stack([grads[l][nm][0] for l in range(2)]).reshape(16, 128))
    small.append(_rows128(jnp.stack([grads[l]["dgn"][4, :8] for l in range(2)]), 1))
    for row in range(4):
        small.append(jnp.stack([grads[l]["dgn"][row, :HD] for l in range(2)]).reshape(1, 128))
    dsk = [grads[l]["dsk"][:, 0, :] for l in range(2)]
    small.append(_rows128(jnp.stack([jnp.stack([d[:, 0], d[:, HD]], axis=1).reshape(8) for d in dsk]), 1))
    gsm = jnp.concatenate(small, axis=0)
    gsm = jnp.pad(gsm, ((0, SMALL_ROWS - gsm.shape[0]), (0, 0)))

    ps0, slots = pair_sums(grads[0], gsm, 0)
    got3 = [_scatter_layer(ps0, "scatter_shards"), grads[0]["rode"]]
    bufs = []
    for t, (nm, (r, c), _) in enumerate(SHARD_ITEMS):
        buf = lax.empty((2, r, c), F32)
        for l, ps in ((1, ps1), (0, ps0)):
            buf = _sum4_t(ps[t], got3[l][t], buf, place_idx, l, nm, r, f"sum4_{l}_{nm}")
        bufs.append(buf)
    bufs = _join_layer(bufs, "join_halves")
    gs = _sum8(slots, "sum8")

    g_out = {nm: buf for (nm, _, _), buf in zip(SHARD_ITEMS, bufs)}
    g_out["meta_tokens"] = lax.dynamic_slice(gs[0:128].reshape(N_META, D), (0, shard * 256), (N_META, 256))
    off = 128
    for nm, rows in SMALL_ITEMS:
        n = w[nm].size
        g_out[nm] = gs[off:off + rows].reshape(-1)[:n].reshape(w[nm].shape)
        off += rows

    delta, new_m, new_v = {}, {}, {}
    for nm, _, _ in SHARD_ITEMS:
        delta[nm], new_m[nm], new_v[nm] = _adamw3(w[nm], g_out[nm], m[nm], v[nm], f"adamw_{nm}")
    small_names = ["meta_tokens"] + [nm for nm, _ in SMALL_ITEMS]
    small_rows = [META_ROWS] + [rows for _, rows in SMALL_ITEMS]

    def pack_small(src):
        buf = jnp.concatenate([_rows128(src[nm], rows) for nm, rows in zip(small_names, small_rows)], axis=0)
        return jnp.pad(buf, ((0, SMALL_ADAM_ROWS - buf.shape[0]), (0, 0)))

    d_, m_, v_ = (a[0] for a in _adamw3(pack_small(w)[None], pack_small(g_out)[None], pack_small(m)[None],
                                        pack_small(v)[None], "adamw_small"))
    off = 0
    for nm, rows in zip(small_names, small_rows):
        n = w[nm].size
        for dst, src in ((delta, d_), (new_m, m_), (new_v, v_)):
            dst[nm] = src[off:off + rows].reshape(-1)[:n].reshape(w[nm].shape)
        off += rows

    return (loss, grad_x, *[g_out[n] for n in names], *[delta[n] for n in names],
            *[new_m[n] for n in names], *[new_v[n] for n in names])
```

```python
import math

import numpy as np
import jax
import jax.numpy as jnp
from jax import lax
from jax.experimental import pallas as pl
from jax.experimental.pallas import tpu as pltpu

D = 1024
F = 2816
FT = F // 2
HD = 64
NPAIR = 4
N_META = 16
BLK = 128
PAD = BLK - N_META
EPS = 1e-6
NEG = -1e30
N_BUCKETS = 32
GA, GB, QA, KA, VA, QB, KB, VB, FA, DP = 0, 1024, 2048, 2560, 3072, 3584, 4096, 4224, 4352, 4480
D_IN = 4360
CDT = jnp.bfloat16
F32 = jnp.float32
VMEM_LIMIT = 48 * 1024 * 1024
MESH_ID = pl.DeviceIdType.MESH

ADAM_LR, ADAM_B1, ADAM_B2, ADAM_EPS, ADAM_WD, ADAM_STEP = 0.001, 0.9, 0.999, 1e-08, 0.01, 10

SHARD_ITEMS = (
    ("ffn1_w_in", (1024, 1408), "col"),
    ("ffn1_w_out", (704, 1024), "row"),
    ("w_in", (1024, 1090), "col"),
    ("w_branch_fox", (512, 256), "col"),
    ("w_branch_swa", (512, 256), "col"),
    ("w_out", (256, 1024), "row"),
    ("ffn2_w_in", (1024, 1408), "col"),
    ("ffn2_w_out", (704, 1024), "row"),
)
SMALL_ROWS = 192
META_ROWS = 32


def _row_tile(t):
    return 384 if t % 384 == 0 else 128


def _dot(a, b):
    return jnp.dot(a, b, preferred_element_type=F32)


def _dot_nt(a, b):
    return lax.dot_general(a, b, (((1,), (1,)), ((), ())), preferred_element_type=F32)


def _dot_hi(a, b):
    return jnp.dot(a, b, preferred_element_type=F32, precision=lax.Precision.HIGHEST)


def _sigmoid(x):
    return 1.0 / (1.0 + jnp.exp(-x))


def _iota(shape, dim):
    return lax.broadcasted_iota(jnp.int32, shape, dim)


def _params(sem=None):
    return pltpu.CompilerParams(dimension_semantics=sem, vmem_limit_bytes=VMEM_LIMIT)


def _sds(shape, dtype):
    return jax.ShapeDtypeStruct(shape, dtype)


def _rms_fwd(h, g, name):
    t = h.shape[0]
    tm = _row_tile(t)

    def body(h_ref, g_ref, a_ref, at_ref):
        x = h_ref[...]
        ms = jnp.mean(x * x, axis=-1, keepdims=True)
        a = x * lax.rsqrt(ms + EPS) * g_ref[...]
        a_ref[...] = a.astype(CDT)
        at_ref[...] = a.T.astype(CDT)

    return pl.pallas_call(
        body, name=name, grid=(t // tm,),
        in_specs=[pl.BlockSpec((tm, D), lambda i: (i, 0)), pl.BlockSpec((1, D), lambda i: (0, 0))],
        out_specs=[pl.BlockSpec((tm, D), lambda i: (i, 0)), pl.BlockSpec((D, tm), lambda i: (0, i))],
        out_shape=[_sds((t, D), CDT), _sds((D, t), CDT)],
        compiler_params=_params(("parallel",)),
    )(h, g)


def _ffn_in(a, w_in, name):
    t = a.shape[0]
    tm = _row_tile(t)
    tn = FT
    nj = F // tn

    def body(a_ref, wg_ref, wu_ref, gu_ref, s_ref, st_ref):
        a_ = a_ref[...]
        g = _dot(a_, wg_ref[...])
        u = _dot(a_, wu_ref[...])
        s = g * _sigmoid(g) * u
        gu_ref[0] = g.astype(CDT)
        gu_ref[1] = u.astype(CDT)
        s_ref[...] = s.astype(CDT)
        st_ref[...] = s.T.astype(CDT)

    return pl.pallas_call(
        body, name=name, grid=(nj, t // tm),
        in_specs=[pl.BlockSpec((tm, D), lambda j, i: (i, 0)),
                  pl.BlockSpec((D, tn), lambda j, i: (0, j)),
                  pl.BlockSpec((D, tn), lambda j, i: (0, j + nj))],
        out_specs=[pl.BlockSpec((2, tm, tn), lambda j, i: (0, i, j)),
                   pl.BlockSpec((tm, tn), lambda j, i: (i, j)),
                   pl.BlockSpec((tn, tm), lambda j, i: (j, i))],
        out_shape=[_sds((2, t, F), CDT), _sds((t, F), CDT), _sds((F, t), CDT)],
        compiler_params=_params(("parallel", "parallel")),
    )(a, w_in, w_in)


def _mm_res(a, b, res, scale, name):
    t, k = a.shape
    n = b.shape[1]
    tm = _row_tile(t)
    tn = 512

    def body(a_ref, b_ref, r_ref, o_ref):
        o_ref[...] = r_ref[...] + scale * _dot(a_ref[...], b_ref[...])

    return pl.pallas_call(
        body, name=name, grid=(t // tm, n // tn),
        in_specs=[pl.BlockSpec((tm, k), lambda i, j: (i, 0)),
                  pl.BlockSpec((k, tn), lambda i, j: (0, j)),
                  pl.BlockSpec((tm, tn), lambda i, j: (i, j))],
        out_specs=pl.BlockSpec((tm, tn), lambda i, j: (i, j)),
        out_shape=_sds((t, n), F32),
        compiler_params=_params(("parallel", "parallel")),
    )(a, b, res)


def _mm(a, b, out_dtype, tm, tn, name, scale=1.0, ride=None):
    m, k = a.shape
    if b.ndim == 3:
        nh = b.shape[2] // tn
        n = 2 * b.shape[2]
        b_spec = pl.BlockSpec((None, k, tn), lambda i, j: (j // nh, 0, j % nh))
    else:
        n = b.shape[1]
        b_spec = pl.BlockSpec((k, tn), lambda i, j: (0, j))
    grid = (m // tm, n // tn)
    ride_in, ride_in_specs, ride_out, ride_out_specs, ride_sems = _ride_specs(ride)

    def body(a_ref, b_ref, o_ref):
        o_ref[...] = (scale * _dot(a_ref[...], b_ref[...])).astype(out_dtype)

    res = pl.pallas_call(
        _riding(body, 2, 1, ride, grid), name=name, grid=grid,
        in_specs=[pl.BlockSpec((tm, k), lambda i, j: (i, 0)), b_spec] + ride_in_specs,
        out_specs=[pl.BlockSpec((tm, tn), lambda i, j: (i, j))] + ride_out_specs,
        out_shape=[_sds((m, n), out_dtype)] + ride_out, scratch_shapes=ride_sems,
        compiler_params=_params(("arbitrary", "arbitrary") if ride else ("parallel", "parallel")),
    )(a, b, *ride_in)
    return (res[0], res[1:]) if ride else res[0]


def _mm_nt(a, b, name, with_t=False):
    m, n = a.shape
    k = b.shape[0]
    tm = _row_tile(m)
    tk = 512

    def body(a_ref, b_ref, o_ref, *t_ref):
        r = _dot_nt(a_ref[...], b_ref[...])
        o_ref[...] = r
        if with_t:
            t_ref[0][...] = r.T.astype(CDT)

    out_specs = [pl.BlockSpec((tm, tk), lambda i, j: (i, j))]
    out_shape = [_sds((m, k), F32)]
    if with_t:
        out_specs.append(pl.BlockSpec((tk, tm), lambda i, j: (j, i)))
        out_shape.append(_sds((k, m), CDT))
    res = pl.pallas_call(
        body, name=name, grid=(m // tm, k // tk),
        in_specs=[pl.BlockSpec((tm, n), lambda i, j: (i, 0)), pl.BlockSpec((tk, n), lambda i, j: (j, 0))],
        out_specs=out_specs, out_shape=out_shape,
        compiler_params=_params(("parallel", "parallel")),
    )(a, b)
    return res if with_t else res[0]


def _ffn_bwd_mid(dhb, w_out, gu, name, ride=None):
    t = dhb.shape[0]
    tm = _row_tile(t)
    tn = FT
    grid = (F // tn, t // tm)
    ride_in, ride_in_specs, ride_out, ride_out_specs, ride_sems = _ride_specs(ride)

    def body(dh_ref, w_ref, gu_ref, o_ref):
        ds = 0.5 * _dot_nt(dh_ref[...], w_ref[...])
        g = gu_ref[0].astype(F32)
        u = gu_ref[1].astype(F32)
        sg = _sigmoid(g)
        o_ref[0] = (ds * u * (sg * (1.0 + g * (1.0 - sg)))).astype(CDT)
        o_ref[1] = (ds * (g * sg)).astype(CDT)

    res = pl.pallas_call(
        _riding(body, 3, 1, ride, grid), name=name, grid=grid,
        in_specs=[pl.BlockSpec((tm, D), lambda j, i: (i, 0)),
                  pl.BlockSpec((tn, D), lambda j, i: (j, 0)),
                  pl.BlockSpec((2, tm, tn), lambda j, i: (0, i, j))] + ride_in_specs,
        out_specs=[pl.BlockSpec((2, tm, tn), lambda j, i: (0, i, j))] + ride_out_specs,
        out_shape=[_sds((2, t, F), CDT)] + ride_out, scratch_shapes=ride_sems,
        compiler_params=_params(("arbitrary", "arbitrary") if ride else ("parallel", "parallel")),
    )(dhb, w_out, gu, *ride_in)
    return (res[0], res[1:]) if ride else res[0]


def _rms_bwd_rows(da_, x, g, dres, i, dh_ref, dhb_ref, dg_ref):
    r = lax.rsqrt(jnp.mean(x * x, axis=-1, keepdims=True) + EPS)
    xh = x * r
    day = da_ * g
    dh = dres + r * (day - xh * jnp.mean(day * xh, axis=-1, keepdims=True))
    dh_ref[...] = dh
    dhb_ref[...] = dh.astype(CDT)

    @pl.when(i == 0)
    def _():
        dg_ref[...] = jnp.zeros(dg_ref.shape, F32)

    dg_ref[0:1, :] += jnp.sum(da_ * xh, axis=0, keepdims=True)


def _ffn_bwd_in(dgu, w_in, h, g, dres, name, ride=None):
    t = dgu.shape[1]
    tm = _row_tile(t)
    grid = (t // tm,)
    ride_in, ride_in_specs, ride_out, ride_out_specs, ride_sems = _ride_specs(ride)

    def body(dg_ref, wg_ref, wu_ref, h_ref, g_ref, dr_ref, dh_ref, dhb_ref, dgn_ref):
        da_ = _dot_nt(dg_ref[0], wg_ref[...]) + _dot_nt(dg_ref[1], wu_ref[...])
        _rms_bwd_rows(da_, h_ref[...], g_ref[...], dr_ref[...], pl.program_id(0), dh_ref, dhb_ref, dgn_ref)

    row = pl.BlockSpec((tm, D), lambda i: (i, 0))
    res = pl.pallas_call(
        _riding(body, 6, 3, ride, grid), name=name, grid=grid,
        in_specs=[pl.BlockSpec((2, tm, F), lambda i: (0, i, 0)),
                  pl.BlockSpec((D, F), lambda i: (0, 0)),
                  pl.BlockSpec((D, F), lambda i: (0, 1)),
                  row, pl.BlockSpec((1, D), lambda i: (0, 0)), row] + ride_in_specs,
        out_specs=[row, row, pl.BlockSpec((8, D), lambda i: (0, 0))] + ride_out_specs,
        out_shape=[_sds((t, D), F32), _sds((t, D), CDT), _sds((8, D), F32)] + ride_out, scratch_shapes=ride_sems,
        compiler_params=_params(("arbitrary",)),
    )(dgu, w_in, w_in, h, g, dres, *ride_in)
    return (*res[:3], res[3:]) if ride else res


def _mm_nt_rms(a, b, h, g, dres, name):
    t, n = a.shape
    tm = _row_tile(t)

    def body(a_ref, b_ref, h_ref, g_ref, dr_ref, dh_ref, dhb_ref, dgn_ref):
        da_ = _dot_nt(a_ref[...], b_ref[...])
        _rms_bwd_rows(da_, h_ref[...], g_ref[...], dr_ref[...], pl.program_id(0), dh_ref, dhb_ref, dgn_ref)

    row = pl.BlockSpec((tm, D), lambda i: (i, 0))
    return pl.pallas_call(
        body, name=name, grid=(t // tm,),
        in_specs=[pl.BlockSpec((tm, n), lambda i: (i, 0)), pl.BlockSpec((D, n), lambda i: (0, 0)),
                  row, pl.BlockSpec((1, D), lambda i: (0, 0)), row],
        out_specs=[row, row, pl.BlockSpec((8, D), lambda i: (0, 0))],
        out_shape=[_sds((t, D), F32), _sds((t, D), CDT), _sds((8, D), F32)],
        compiler_params=_params(("arbitrary",)),
    )(a, b, h, g, dres)


def _loss(h, target, name):
    t = h.shape[0]

    def body(h_ref, t_ref, dh_ref, dhb_ref, l_ref):
        i = pl.program_id(0)

        @pl.when(i == 0)
        def _():
            l_ref[...] = jnp.zeros(l_ref.shape, F32)
            dh_ref[...] = jnp.zeros(dh_ref.shape, F32)
            dhb_ref[...] = jnp.zeros(dhb_ref.shape, CDT)

        @pl.when(i > 0)
        def _():
            err = h_ref[...] - t_ref[...]
            l_ref[...] += (0.5 / D) * jnp.sum(err * err)
            d = err * (1.0 / D)
            dh_ref[...] = d
            dhb_ref[...] = d.astype(CDT)

    row = pl.BlockSpec((BLK, D), lambda i: (i, 0))
    return pl.pallas_call(
        body, name=name, grid=(t // BLK,),
        in_specs=[row, pl.BlockSpec((BLK, D), lambda i: (jnp.maximum(i - 1, 0), 0))],
        out_specs=[row, row, pl.BlockSpec((8, 128), lambda i: (0, 0))],
        out_shape=[_sds((t, D), F32), _sds((t, D), CDT), _sds((8, 128), F32)],
        compiler_params=_params(("arbitrary",)),
    )(h, target)


def _block_diag():
    return (_iota((128, 128), 0) // HD == _iota((128, 128), 1) // HD).astype(F32)


def _head_sums(v, bd):
    hi = v.astype(CDT)
    rest = (v - hi.astype(F32)).astype(CDT)
    b = bd.astype(CDT)
    return _dot(hi, b) + _dot(rest, b)


def _dup_halves(x, lo):
    sw = pltpu.roll(x, 64, 1)
    return jnp.where(lo, x, sw), jnp.where(lo, sw, x)


def _qknorm_fwd(proj, gfq, gfk, gsq, gsk, fb, name):
    t = proj.shape[0]
    tm = _row_tile(t)

    def body(qa, ka, va, qb, kb, vb, fa, gfq_r, gfk_r, gsq_r, gsk_r, fb_r,
             qf_o, kf_o, vf_o, qs_o, kse_o, vse_o, c_o, ct_o, qft_o, carry):
        i = pl.program_id(0)
        bd = _block_diag()
        lane = _iota((1, 128), 1)
        lo = lane < HD

        def hnorm(x, g):
            ms = _head_sums(x * x, bd) * (1.0 / HD)
            return x * lax.rsqrt(ms + EPS) * g

        for ch in range(4):
            sl = slice(128 * ch, 128 * (ch + 1))
            qn = hnorm(qa[:, sl], gfq_r[:, sl]) * 0.125
            qf_o[:, sl] = qn.astype(CDT)
            qft_o[sl, :] = qn.T.astype(CDT)
            kf_o[:, sl] = hnorm(ka[:, sl], gfk_r[:, sl]).astype(CDT)
            qs_o[:, sl] = (hnorm(qb[:, sl], gsq_r[:, sl]) * 0.125).astype(CDT)
        vf_o[...] = va[...].astype(CDT)
        k0, k1 = _dup_halves(hnorm(kb[...], gsk_r[...]), lo)
        kse_o[0] = k0.astype(CDT)
        kse_o[1] = k1.astype(CDT)
        v0, v1 = _dup_halves(vb[...], lo)
        vse_o[0] = v0.astype(CDT)
        vse_o[1] = v1.astype(CDT)

        z = fa[...] + fb_r[...]
        lf = jnp.minimum(z, 0.0) - jnp.log(1.0 + jnp.exp(-jnp.abs(z)))
        lf = jnp.where(lane < 8, lf, 0.0)
        ltri = (_iota((tm, tm), 1) <= _iota((tm, tm), 0)).astype(F32)

        @pl.when(i == 0)
        def _():
            carry[...] = jnp.zeros(carry.shape, F32)

        c = _dot_hi(ltri, lf) + carry[0:1, :]
        carry[0:1, :] = c[tm - 1:tm, :]
        c_o[...] = c
        ct_o[...] = c.T[0:8, :]

    def col(width, off):
        return pl.BlockSpec((tm, width), lambda i: (i, off // width))

    def vec(width):
        return pl.BlockSpec((1, width), lambda i: (0, 0))

    return pl.pallas_call(
        body, name=name, grid=(t // tm,),
        in_specs=[col(512, QA), col(512, KA), col(512, VA), col(512, QB), col(128, KB), col(128, VB), col(128, FA),
                  vec(512), vec(512), vec(512), vec(128), vec(128)],
        out_specs=[pl.BlockSpec((tm, 512), lambda i: (i, 0))] * 4
        + [pl.BlockSpec((2, tm, 128), lambda i: (0, i, 0))] * 2
        + [pl.BlockSpec((tm, 128), lambda i: (i, 0)), pl.BlockSpec((8, tm), lambda i: (0, i)),
           pl.BlockSpec((512, tm), lambda i: (0, i))],
        out_shape=[_sds((t, 512), CDT)] * 4 + [_sds((2, t, 128), CDT)] * 2
        + [_sds((t, 128), F32), _sds((8, t), F32), _sds((512, t), CDT)],
        scratch_shapes=[pltpu.VMEM((8, 128), F32)],
        compiler_params=_params(("arbitrary",)),
    )(proj, proj, proj, proj, proj, proj, proj, gfq, gfk, gsq, gsk, fb)


def _qknorm_bwd(proj, dqf, dkf, dvf, dqs, dkse, dvse, dcq, dck, dga, dgb, gfq, gfk, gsq, gsk, fb, name):
    t = proj.shape[0]
    tm = _row_tile(t)
    nt = t // tm

    def body(qa, ka, qb, kb, fa, dqf_r, dkf_r, dvf_r, dqs_r, dkse_r, dvse_r, dcq_r, dck_r, dga_r, dgb_r,
             gfq_r, gfk_r, gsq_r, gsk_r, fb_r, dp_o, dgn_o, carry, acc):
        i = pl.program_id(0)
        bd = _block_diag()
        lane = _iota((1, 128), 1)
        lo = lane < HD

        @pl.when(i == 0)
        def _():
            carry[...] = jnp.zeros(carry.shape, F32)
            acc[...] = jnp.zeros(acc.shape, F32)

        def hnorm_bwd(x, g, dy):
            r = lax.rsqrt(_head_sums(x * x, bd) * (1.0 / HD) + EPS)
            xh = x * r
            day = dy * g
            dx = r * (day - xh * (_head_sums(day * xh, bd) * (1.0 / HD)))
            return dx, jnp.sum(dy * xh, axis=0, keepdims=True)

        for ch in range(4):
            sl = slice(128 * ch, 128 * (ch + 1))
            dx, dg = hnorm_bwd(qa[:, sl], gfq_r[:, sl], dqf_r[:, sl] * 0.125)
            dp_o[:, QA + 128 * ch:QA + 128 * (ch + 1)] = dx.astype(CDT)
            acc[0:1, sl] += dg
            dx, dg = hnorm_bwd(ka[:, sl], gfk_r[:, sl], dkf_r[:, sl])
            dp_o[:, KA + 128 * ch:KA + 128 * (ch + 1)] = dx.astype(CDT)
            acc[1:2, sl] += dg
            dx, dg = hnorm_bwd(qb[:, sl], gsq_r[:, sl], dqs_r[:, sl] * 0.125)
            dp_o[:, QB + 128 * ch:QB + 128 * (ch + 1)] = dx.astype(CDT)
            acc[2:3, sl] += dg
        dp_o[:, VA:VA + 512] = dvf_r[...].astype(CDT)
        dp_o[:, GA:GA + D] = dga_r[...]
        dp_o[:, GB:GB + D] = dgb_r[...]

        def fold(x):
            e0 = x[0]
            e1 = x[1]
            return jnp.where(lo, e0 + pltpu.roll(e0, 64, 1), e1 + pltpu.roll(e1, 64, 1))

        dx, dg = hnorm_bwd(kb[...], gsk_r[...], fold(dkse_r))
        dp_o[:, KB:KB + 128] = dx.astype(CDT)
        acc[3:4, 0:128] += dg
        dp_o[:, VB:VB + 128] = fold(dvse_r).astype(CDT)

        rr = _iota((512, 128), 0)
        hh = _iota((512, 128), 1)
        sel = ((rr == (hh >> 1) * 128 + (hh & 1) * HD) & (hh < 8)).astype(F32)
        dcs = _dot_hi(dcq_r[...] - dck_r[...], sel)
        utri = (_iota((tm, tm), 1) >= _iota((tm, tm), 0)).astype(F32)
        dlf = _dot_hi(utri, dcs) + carry[0:1, :]
        carry[0:1, :] = dlf[0:1, :]
        z = fa[...] + fb_r[...]
        dfa = jnp.where(lane < 8, dlf * _sigmoid(-z), 0.0)
        dp_o[:, FA:FA + 128] = dfa.astype(CDT)
        acc[4:5, 0:128] += jnp.sum(dfa, axis=0, keepdims=True)

        @pl.when(i == nt - 1)
        def _():
            foldm = ((_iota((512, 128), 0) & (HD - 1)) == _iota((512, 128), 1)).astype(F32)
            dgn_o[...] = _dot_hi(acc[...], foldm)

    def col(width, off):
        return pl.BlockSpec((tm, width), lambda i: (nt - 1 - i, off // width))

    def rows(width):
        return pl.BlockSpec((tm, width), lambda i: (nt - 1 - i, 0))

    def vec(width):
        return pl.BlockSpec((1, width), lambda i: (0, 0))

    pair = pl.BlockSpec((2, tm, 128), lambda i: (0, nt - 1 - i, 0))
    return pl.pallas_call(
        body, name=name, grid=(nt,),
        in_specs=[col(512, QA), col(512, KA), col(512, QB), col(128, KB), col(128, FA),
                  rows(512), rows(512), rows(512), rows(512), pair, pair, rows(512), rows(512), rows(D), rows(D),
                  vec(512), vec(512), vec(512), vec(128), vec(128)],
        out_specs=[rows(DP), pl.BlockSpec((8, 128), lambda i: (0, 0))],
        out_shape=[_sds((t, DP), CDT), _sds((8, 128), F32)],
        scratch_shapes=[pltpu.VMEM((8, 128), F32), pltpu.VMEM((8, 512), F32)],
        compiler_params=_params(("arbitrary",)),
    )(proj, proj, proj, proj, proj, dqf, dkf, dvf, dqs, dkse, dvse, dcq, dck, dga, dgb, gfq, gfk, gsq, gsk, fb)


def _gate_fwd(ofox, oswa, wbf, wbs, proj, name):
    t = ofox.shape[0]
    tm = _row_tile(t)
    tn = 512

    def body(of_r, os_r, wf_r, ws_r, ga_r, gb_r, y_o, yt_o, pf_o, ps_o, oft_o, ost_o):
        j = pl.program_id(1)
        pf = _dot(of_r[...], wf_r[...])
        ps = _dot(os_r[...], ws_r[...])
        y = _sigmoid(ga_r[...]) * pf + _sigmoid(gb_r[...]) * ps
        y_o[...] = y.astype(CDT)
        yt_o[...] = y.T.astype(CDT)
        pf_o[...] = pf.astype(CDT)
        ps_o[...] = ps.astype(CDT)

        @pl.when(j == 0)
        def _():
            oft_o[...] = of_r[...].astype(F32).T.astype(CDT)
            ost_o[...] = os_r[...].astype(F32).T.astype(CDT)

    tile = pl.BlockSpec((tm, tn), lambda i, j: (i, j))
    return pl.pallas_call(
        body, name=name, grid=(t // tm, D // tn),
        in_specs=[pl.BlockSpec((tm, 512), lambda i, j: (i, 0)), pl.BlockSpec((tm, 512), lambda i, j: (i, 0)),
                  pl.BlockSpec((512, tn), lambda i, j: (0, j)), pl.BlockSpec((512, tn), lambda i, j: (0, j)),
                  pl.BlockSpec((tm, tn), lambda i, j: (i, GA // tn + j)),
                  pl.BlockSpec((tm, tn), lambda i, j: (i, GB // tn + j))],
        out_specs=[tile, pl.BlockSpec((tn, tm), lambda i, j: (j, i)), tile, tile,
                   pl.BlockSpec((512, tm), lambda i, j: (0, i)), pl.BlockSpec((512, tm), lambda i, j: (0, i))],
        out_shape=[_sds((t, D), CDT), _sds((D, t), CDT), _sds((t, D), CDT), _sds((t, D), CDT),
                   _sds((512, t), CDT), _sds((512, t), CDT)],
        compiler_params=_params(("parallel", "arbitrary")),
    )(ofox, oswa, wbf, wbs, proj, proj)


def _gate_bwd(dy, pf, ps, proj, name):
    t = dy.shape[0]
    tm = _row_tile(t)
    tn = 512

    def body(dy_r, pf_r, ps_r, ga_r, gb_r, dpf_o, dps_o, dga_o, dgb_o):
        dy_ = dy_r[...]
        sa = _sigmoid(ga_r[...])
        sb = _sigmoid(gb_r[...])
        dpf_o[...] = (dy_ * sa).astype(CDT)
        dps_o[...] = (dy_ * sb).astype(CDT)
        dga_o[...] = (dy_ * pf_r[...].astype(F32) * (sa * (1.0 - sa))).astype(CDT)
        dgb_o[...] = (dy_ * ps_r[...].astype(F32) * (sb * (1.0 - sb))).astype(CDT)

    tile = pl.BlockSpec((tm, tn), lambda i, j: (i, j))
    return pl.pallas_call(
        body, name=name, grid=(t // tm, D // tn),
        in_specs=[tile, tile, tile,
                  pl.BlockSpec((tm, tn), lambda i, j: (i, GA // tn + j)),
                  pl.BlockSpec((tm, tn), lambda i, j: (i, GB // tn + j))],
        out_specs=[tile] * 4,
        out_shape=[_sds((t, D), CDT)] * 4,
        compiler_params=_params(("parallel", "parallel")),
    )(dy, pf, ps, proj, proj)


def _tri_steps(n, by_key):
    if by_key:
        pairs = [(i, j) for j in range(n) for i in range(j, n)]
    else:
        pairs = [(i, j) for i in range(n) for j in range(i + 1)]
    return (np.array([p[0] for p in pairs], np.int32), np.array([p[1] for p in pairs], np.int32))


def _head_col(blk, lane, h):
    return jnp.sum(jnp.where(lane == h, blk, 0.0), axis=1, keepdims=True)


def _head_row(blk, sub, h):
    return jnp.sum(jnp.where(sub == h, blk, 0.0), axis=0, keepdims=True)


def _ride_specs(ride):
    if ride is None:
        return [], [], [], [], []
    kind, srcs, outs, layer, items = ride
    return list(srcs), [ANY] * len(srcs), list(outs), [ANY] * len(outs), _dma_sems(3 * len(srcs))


def _ride_start(ride, srcs, dsts, send_sems, recv_sems):
    for cp in _ici_copies(ride[0], srcs, dsts, send_sems, recv_sems, ride[3], recv=False, items=ride[4])[0]:
        cp.start()


def _ride_wait(ride, srcs, dsts, send_sems, recv_sems):
    sends, recvs = _ici_copies(ride[0], srcs, dsts, send_sems, recv_sems, ride[3], items=ride[4])
    for cp in recvs:
        cp.wait_recv()
    for cp in sends:
        cp.wait_send()


def _riding(body, n_in, n_out, ride, grid):
    if ride is None:
        return body
    nr = len(ride[1])

    def wrapped(*refs):
        ins, srcs = refs[:n_in], refs[n_in:n_in + nr]
        outs, dsts = refs[n_in + nr:n_in + nr + n_out], refs[n_in + nr + n_out:n_in + 2 * nr + n_out]
        scratch, sems = refs[n_in + 2 * nr + n_out:-2], refs[-2:]
        first = pl.program_id(0) == 0
        last = pl.program_id(0) == grid[0] - 1
        for a in range(1, len(grid)):
            first = first & (pl.program_id(a) == 0)
            last = last & (pl.program_id(a) == grid[a] - 1)

        @pl.when(first)
        def _():
            _ride_start(ride, srcs, dsts, *sems)

        body(*ins, *outs, *scratch)

        @pl.when(last)
        def _():
            _ride_wait(ride, srcs, dsts, *sems)

    return wrapped


def _fox_fwd(qf, kf, vf, c, ct, name, ride=None):
    t = qf.shape[0]
    ta = _row_tile(t)
    qi, kj = _tri_steps(t // ta, by_key=False)
    nsteps = len(qi)
    ride_in, ride_in_specs, ride_out, ride_out_specs, ride_sems = _ride_specs(ride)

    def body(qi_r, kj_r, q_r, k_r, v_r, c_r, ct_r, *rest):
        nr = len(ride_in)
        src_r, (o_o, lse_o), dst_o = rest[:nr], rest[nr:nr + 2], rest[nr + 2:2 * nr + 2]
        m_sc, l_sc, acc_sc, cq_sc, *sems = rest[2 * nr + 2:]
        p = pl.program_id(0)
        n = pl.program_id(1)
        i = qi_r[n]
        j = kj_r[n]
        lane = _iota((1, 128), 1)
        lo = lane < HD

        if ride is not None:
            @pl.when((p == 0) & (n == 0))
            def _():
                _ride_start(ride, src_r, dst_o, *sems)

        @pl.when(j == 0)
        def _():
            m_sc[...] = jnp.full(m_sc.shape, NEG, F32)
            l_sc[...] = jnp.zeros(l_sc.shape, F32)
            acc_sc[...] = jnp.zeros(acc_sc.shape, F32)
            for e in (0, 1):
                cq_sc[e] = jnp.broadcast_to(_head_col(c_r[...], lane, 2 * p + e), (ta, 128))

        def step(masked):
            q = q_r[...]
            k = k_r[...]
            vaug = jnp.concatenate([v_r[...], jnp.ones((ta, 128), CDT)], axis=1)
            if masked:
                rows = i * ta + _iota((ta, 1), 0)
                cols = j * ta + _iota((1, ta), 1)
                mask = (cols <= rows) & (cols >= PAD)
            sub = _iota((8, 1), 0)
            alphas, pvs = [], []
            for e in (0, 1):
                sel = lo if e == 0 else jnp.logical_not(lo)
                s = _dot_nt(jnp.where(sel, q, 0), k)
                ck = _head_row(ct_r[...], sub, 2 * p + e)
                cq = cq_sc[e]
                chunks = []
                for ch in range(ta // 128):
                    sl = slice(128 * ch, 128 * (ch + 1))
                    sc = s[:, sl] + cq - ck[:, sl]
                    if masked:
                        sc = jnp.where(mask[:, sl], sc, NEG)
                    chunks.append(sc)
                mx = chunks[0]
                for sc in chunks[1:]:
                    mx = jnp.maximum(mx, sc)
                m_prev = m_sc[e]
                m_new = jnp.maximum(m_prev, jnp.max(mx, axis=1, keepdims=True))
                alpha = jnp.exp(m_prev - m_new)
                pe = jnp.concatenate([jnp.exp(sc - m_new).astype(CDT) for sc in chunks], axis=1)
                pva = _dot(pe, vaug)
                l_sc[e] = alpha * l_sc[e] + pva[:, 128:]
                m_sc[e] = m_new
                alphas.append(alpha)
                pvs.append(pva[:, :128])
            acc_sc[...] = acc_sc[...] * jnp.where(lo, alphas[0], alphas[1]) + jnp.where(lo, pvs[0], pvs[1])

        edge = (j == i) | (j == 0)

        @pl.when(edge)
        def _():
            step(True)

        @pl.when(jnp.logical_not(edge))
        def _():
            step(False)

        @pl.when(j == i)
        def _():
            l = jnp.where(lo, l_sc[0], l_sc[1])
            o_o[...] = (acc_sc[...] / l).astype(CDT)
            lse_o[...] = jnp.where(lo, m_sc[0], m_sc[1]) + jnp.log(l)

        if ride is not None:
            @pl.when((p == NPAIR - 1) & (n == nsteps - 1))
            def _():
                _ride_wait(ride, src_r, dst_o, *sems)

    qblk = pl.BlockSpec((ta, 128), lambda p, n, qi_r, kj_r: (qi_r[n], p))
    kblk = pl.BlockSpec((ta, 128), lambda p, n, qi_r, kj_r: (kj_r[n], p))
    grid_spec = pltpu.PrefetchScalarGridSpec(
        num_scalar_prefetch=2, grid=(NPAIR, nsteps),
        in_specs=[qblk, kblk, kblk,
                  pl.BlockSpec((ta, 128), lambda p, n, qi_r, kj_r: (qi_r[n], 0)),
                  pl.BlockSpec((8, ta), lambda p, n, qi_r, kj_r: (0, kj_r[n]))] + ride_in_specs,
        out_specs=[qblk, qblk] + ride_out_specs,
        scratch_shapes=[pltpu.VMEM((2, ta, 128), F32), pltpu.VMEM((2, ta, 128), F32), pltpu.VMEM((ta, 128), F32),
                        pltpu.VMEM((2, ta, 128), F32)] + ride_sems,
    )
    return pl.pallas_call(
        body, name=name, grid_spec=grid_spec,
        out_shape=[_sds((t, 512), CDT), _sds((t, 512), F32)] + ride_out,
        compiler_params=_params(("arbitrary", "arbitrary")),
    )(jnp.asarray(qi), jnp.asarray(kj), qf, kf, vf, c, ct, *ride_in)


def _fox_bwd(qf, qft, kf, vf, c, ct, o, lse, do, dot, name, ride=None):
    t = qf.shape[0]
    ta = _row_tile(t)
    nq = t // ta
    qi, kj = _tri_steps(nq, by_key=False)
    nsteps = len(qi)
    ride_in, ride_in_specs, ride_out, ride_out_specs, ride_sems = _ride_specs(ride)

    def body(qi_r, kj_r, q_r, qt_r, k_r, v_r, c_r, ct_r, o_r, lse_r, do_r, dot_r, *rest):
        nr = len(ride_in)
        src_r, (dq_o, dcq_o, dk_o, dv_o, dck_o), dst_o = rest[:nr], rest[nr:nr + 5], rest[nr + 5:2 * nr + 5]
        lse_sc, dl_sc, cq_sc, dq_sc, dcq_sc, dkt_sc, dvt_sc, dckt_sc, *sems = rest[2 * nr + 5:]
        p = pl.program_id(0)
        n = pl.program_id(1)
        i = qi_r[n]
        j = kj_r[n]
        lane = _iota((1, 128), 1)
        lo = lane < HD
        top = _iota((128, 1), 0) < HD

        if ride is not None:
            @pl.when((p == 0) & (n == 0))
            def _():
                _ride_start(ride, src_r, dst_o, *sems)

        @pl.when(n == 0)
        def _():
            dkt_sc[...] = jnp.zeros(dkt_sc.shape, F32)
            dvt_sc[...] = jnp.zeros(dvt_sc.shape, F32)
            dckt_sc[...] = jnp.zeros(dckt_sc.shape, F32)

        @pl.when(j == 0)
        def _():
            dq_sc[...] = jnp.zeros(dq_sc.shape, F32)
            dcq_sc[...] = jnp.zeros(dcq_sc.shape, F32)
            dd = do_r[...] * o_r[...].astype(F32)
            lse = lse_r[...]
            for e in (0, 1):
                sel = lo if e == 0 else jnp.logical_not(lo)
                cq_sc[e] = jnp.broadcast_to(_head_col(c_r[...], lane, 2 * p + e), (ta, 128))
                dl_sc[e] = jnp.broadcast_to(jnp.sum(jnp.where(sel, dd, 0.0), axis=1, keepdims=True), (ta, 128))
                lse_sc[e] = jnp.broadcast_to(lse[:, HD * e:HD * e + 1], (ta, 128))

        def step(masked):
            q = q_r[...]
            qt = qt_r[...]
            k = k_r[...]
            v = v_r[...]
            dob = do_r[...].astype(CDT)
            dot_ = dot_r[...]
            ones = jnp.ones((ta, 128), CDT)
            ones16 = jnp.ones((16, ta), CDT)
            if masked:
                rows = i * ta + _iota((ta, 1), 0)
                cols = j * ta + _iota((1, ta), 1)
                mask = (cols <= rows) & (cols >= PAD)
            sub = _iota((8, 1), 0)
            for e in (0, 1):
                sel = lo if e == 0 else jnp.logical_not(lo)
                rsel = top if e == 0 else jnp.logical_not(top)
                s = _dot_nt(jnp.where(sel, q, 0), k)
                dp = _dot_nt(jnp.where(sel, dob, 0), v)
                ck = _head_row(ct_r[...], sub, 2 * p + e)
                cq, lse_e, dl = cq_sc[e], lse_sc[e], dl_sc[e]
                prs, dss = [], []
                for ch in range(ta // 128):
                    sl = slice(128 * ch, 128 * (ch + 1))
                    sc = s[:, sl] + cq - ck[:, sl]
                    if masked:
                        sc = jnp.where(mask[:, sl], sc, NEG)
                    pr = jnp.exp(sc - lse_e)
                    prs.append(pr.astype(CDT))
                    dss.append((pr * (dp[:, sl] - dl)).astype(CDT))
                pb = jnp.concatenate(prs, axis=1)
                dsb = jnp.concatenate(dss, axis=1)
                dvt_sc[j] += _dot(jnp.where(rsel, dot_, 0), pb)
                dkc = _dot(jnp.concatenate([jnp.where(rsel, qt, 0), ones16], axis=0), dsb)
                dkt_sc[j] += dkc[0:128]
                dckt_sc[j, 0:8, :] += jnp.where(sub == e, dkc[128:136], 0.0)
                dqa = _dot(dsb, jnp.concatenate([jnp.where(sel, k, 0), ones], axis=1))
                dq_sc[...] += dqa[:, :128]
                dcq_sc[e] += dqa[:, 128:]

        edge = (j == i) | (j == 0)

        @pl.when(edge)
        def _():
            step(True)

        @pl.when(jnp.logical_not(edge))
        def _():
            step(False)

        @pl.when(j == i)
        def _():
            dq_o[...] = dq_sc[...]
            dcq_o[...] = jnp.where(lo, dcq_sc[0], dcq_sc[1])

        @pl.when(n == nsteps - 1)
        def _():
            spread = (_iota((128, 128), 1) == _iota((128, 128), 0) // HD).astype(F32)
            for jb in range(nq):
                rs = slice(jb * ta, (jb + 1) * ta)
                dk_o[rs, :] = dkt_sc[jb].T
                dv_o[rs, :] = dvt_sc[jb].T
                dck_o[rs, :] = _dot_hi(spread, dckt_sc[jb]).T

        if ride is not None:
            @pl.when((p == NPAIR - 1) & (n == nsteps - 1))
            def _():
                _ride_wait(ride, src_r, dst_o, *sems)

    qblk = pl.BlockSpec((ta, 128), lambda p, n, qi_r, kj_r: (qi_r[n], p))
    qtblk = pl.BlockSpec((128, ta), lambda p, n, qi_r, kj_r: (p, qi_r[n]))
    kblk = pl.BlockSpec((ta, 128), lambda p, n, qi_r, kj_r: (kj_r[n], p))
    whole = pl.BlockSpec((t, 128), lambda p, n, qi_r, kj_r: (0, p))
    grid_spec = pltpu.PrefetchScalarGridSpec(
        num_scalar_prefetch=2, grid=(NPAIR, nsteps),
        in_specs=[qblk, qtblk, kblk, kblk,
                  pl.BlockSpec((ta, 128), lambda p, n, qi_r, kj_r: (qi_r[n], 0)),
                  pl.BlockSpec((8, ta), lambda p, n, qi_r, kj_r: (0, kj_r[n])),
                  qblk, qblk, qblk, qtblk] + ride_in_specs,
        out_specs=[qblk, qblk, whole, whole, whole] + ride_out_specs,
        scratch_shapes=[pltpu.VMEM((2, ta, 128), F32)] * 3 + [pltpu.VMEM((ta, 128), F32), pltpu.VMEM((2, ta, 128), F32)]
        + [pltpu.VMEM((nq, 128, ta), F32)] * 3 + ride_sems,
    )
    return pl.pallas_call(
        body, name=name, grid_spec=grid_spec,
        out_shape=[_sds((t, 512), F32)] * 5 + ride_out,
        compiler_params=_params(("arbitrary", "arbitrary")),
    )(jnp.asarray(qi), jnp.asarray(kj), qf, qft, kf, vf, c, ct, o, lse, do, dot, *ride_in)


def _bucket_table():
    r = np.arange(BLK)[:, None]
    c = np.arange(3 * BLK)[None, :]
    d = np.where(c < BLK, r + BLK - c, r - (c - BLK))
    n = np.maximum(d, 0)
    max_exact = N_BUCKETS // 2
    nf = np.maximum(n, 1).astype(np.float32)
    large = max_exact + (np.log(nf / max_exact) / math.log(BLK / max_exact) * (N_BUCKETS - max_exact)).astype(np.int32)
    large = np.minimum(large, N_BUCKETS - 1)
    b = np.where(n < max_exact, n, large)
    return np.where(c < 2 * BLK, b, N_BUCKETS - 1).astype(np.int32)


def _bias_fwd(table, name):
    bucket = jnp.asarray(_bucket_table())

    def body(tab_r, b_r, o_o):
        h = pl.program_id(0)
        b = b_r[...]
        acc = jnp.zeros(b.shape, F32)
        for k in range(N_BUCKETS):
            acc = jnp.where(b == k, tab_r[k, h], acc)
        o_o[...] = acc

    return pl.pallas_call(
        body, name=name, grid=(8,),
        in_specs=[pl.BlockSpec(memory_space=pltpu.SMEM), pl.BlockSpec((BLK, 3 * BLK), lambda h: (0, 0))],
        out_specs=pl.BlockSpec((None, BLK, 3 * BLK), lambda h: (h, 0, 0)),
        out_shape=_sds((8, BLK, 3 * BLK), F32),
        compiler_params=_params(("parallel",)),
    )(table, bucket)


def _bias_bwd(dbias, name):
    bucket = jnp.asarray(_bucket_table())

    def body(d_r, b_r, o_o):
        h = pl.program_id(0)
        b = b_r[...]
        d = d_r[...]
        lane = _iota((1, 128), 1)
        row = jnp.zeros((1, 128), F32)
        for k in range(N_BUCKETS):
            row = jnp.where(lane == k, jnp.sum(jnp.where(b == k, d, 0.0)), row)
        o_o[pl.ds(h, 1), :] = row

    return pl.pallas_call(
        body, name=name, grid=(8,),
        in_specs=[pl.BlockSpec((None, BLK, 3 * BLK), lambda h: (h, 0, 0)), pl.BlockSpec((BLK, 3 * BLK), lambda h: (0, 0))],
        out_specs=pl.BlockSpec((8, 128), lambda h: (0, 0)),
        out_shape=_sds((8, 128), F32),
        compiler_params=_params(("arbitrary",)),
    )(dbias, bucket)


def _swa_valid(i):
    r = _iota((BLK, 1), 0)
    c = _iota((1, 3 * BLK), 1)
    prev = (c < BLK) & (c > r) & (i >= 1) & ((i - 1) * BLK + c >= PAD)
    cc = c - BLK
    cur = (c >= BLK) & (c < 2 * BLK) & (cc <= r) & (i * BLK + cc >= PAD)
    cm = c - 2 * BLK
    meta = (c >= 2 * BLK) & (cm >= PAD) & (i * BLK + r - cm >= BLK)
    return prev | cur | meta


def _swa_kv_specs(ta):
    nb = ta // BLK
    return [pl.BlockSpec((None, BLK, 128), lambda p, i: (p // 2, jnp.maximum(i * nb - 1, 0), 0)),
            pl.BlockSpec((None, ta, 128), lambda p, i: (p // 2, i, 0)),
            pl.BlockSpec((None, BLK, 128), lambda p, i: (p // 2, 0, 0))]


def _swa_fwd(qs, kse, vse, bias, sinks, name):
    t = qs.shape[0]
    ta = _row_tile(t)
    nb = ta // BLK

    def body(sink_r, q_r, kp_r, kc_r, km_r, vp_r, vc_r, vm_r, b_r, o_o, lse_o):
        p = pl.program_id(0)
        i = pl.program_id(1)
        lo = _iota((1, 128), 1) < HD
        k4 = jnp.concatenate([kp_r[...], kc_r[...]], axis=0)
        v4 = jnp.concatenate([vp_r[...], vc_r[...]], axis=0)
        for b in range(nb):
            rows = slice(BLK * b, BLK * (b + 1))
            q = q_r[rows, :]
            k3 = jnp.concatenate([k4[BLK * b:BLK * (b + 2)], km_r[...]], axis=0)
            v3 = jnp.concatenate([v4[BLK * b:BLK * (b + 2)], vm_r[...]], axis=0)
            valid = _swa_valid(i * nb + b)
            outs, lses = [], []
            for e in (0, 1):
                sel = lo if e == 0 else jnp.logical_not(lo)
                s = _dot_nt(jnp.where(sel, q, 0), k3) + b_r[e]
                s = jnp.where(valid, s, NEG)
                sink = sink_r[2 * p + e]
                mx = jnp.maximum(jnp.max(s, axis=1, keepdims=True), sink)
                pe = jnp.exp(s - mx)
                den = jnp.sum(pe, axis=1, keepdims=True) + jnp.exp(sink - mx)
                outs.append(_dot(pe.astype(CDT), v3) / den)
                lses.append(mx + jnp.log(den))
            o_o[rows, :] = jnp.where(lo, outs[0], outs[1]).astype(CDT)
            lse_o[rows, :] = jnp.where(lo, lses[0], lses[1])

    qblk = pl.BlockSpec((ta, 128), lambda p, i: (i, p))
    return pl.pallas_call(
        body, name=name, grid=(NPAIR, t // ta),
        in_specs=[pl.BlockSpec(memory_space=pltpu.SMEM), qblk] + _swa_kv_specs(ta) + _swa_kv_specs(ta)
        + [pl.BlockSpec((2, BLK, 3 * BLK), lambda p, i: (p, 0, 0))],
        out_specs=[qblk, qblk],
        out_shape=[_sds((t, 512), CDT), _sds((t, 512), F32)],
        compiler_params=_params(("parallel", "parallel")),
    )(sinks, qs, kse, kse, kse, vse, vse, vse, bias)


def _swa_bwd(qs, kse, vse, bias, sinks, o, lse, do, name):
    t = qs.shape[0]
    ta = _row_tile(t)
    nb = ta // BLK

    def body(sink_r, q_r, kp_r, kc_r, km_r, vp_r, vc_r, vm_r, b_r, o_r, lse_r, do_r,
             dq_o, dk_o, dv_o, db_o, dsk_o):
        p = pl.program_id(0)
        i = pl.program_id(1)
        lo = _iota((1, 128), 1) < HD

        @pl.when((i == 0) & (p % 2 == 0))
        def _():
            dk_o[...] = jnp.zeros(dk_o.shape, F32)
            dv_o[...] = jnp.zeros(dv_o.shape, F32)

        @pl.when(i == 0)
        def _():
            db_o[...] = jnp.zeros(db_o.shape, F32)
            dsk_o[...] = jnp.zeros(dsk_o.shape, F32)

        k4 = jnp.concatenate([kp_r[...], kc_r[...]], axis=0)
        v4 = jnp.concatenate([vp_r[...], vc_r[...]], axis=0)
        for b in range(nb):
            ib = i * nb + b
            rows = slice(BLK * b, BLK * (b + 1))
            q = q_r[rows, :]
            do_ = do_r[rows, :]
            dd = do_ * o_r[rows, :].astype(F32)
            lse = lse_r[rows, :]
            k3 = jnp.concatenate([k4[BLK * b:BLK * (b + 2)], km_r[...]], axis=0)
            v3 = jnp.concatenate([v4[BLK * b:BLK * (b + 2)], vm_r[...]], axis=0)
            valid = _swa_valid(ib)
            dq = jnp.zeros((BLK, 128), F32)
            dk3 = jnp.zeros((3 * BLK, 128), F32)
            dv3 = jnp.zeros((3 * BLK, 128), F32)
            dsink = []
            for e in (0, 1):
                sel = lo if e == 0 else jnp.logical_not(lo)
                qe = jnp.where(sel, q, 0)
                doe = jnp.where(sel, do_, 0.0).astype(CDT)
                lse_e = lse[:, HD * e:HD * e + 1]
                s = _dot_nt(qe, k3) + b_r[e]
                s = jnp.where(valid, s, NEG)
                pr = jnp.exp(s - lse_e)
                delta = jnp.sum(jnp.where(sel, dd, 0.0), axis=1, keepdims=True)
                ds = pr * (_dot_nt(doe, v3) - delta)
                db_o[e] += ds
                dsink.append(-jnp.sum(jnp.exp(sink_r[2 * p + e] - lse_e) * delta, axis=0, keepdims=True))
                dq = dq + _dot(ds.astype(CDT), jnp.where(sel, k3, 0))
                dk3 = dk3 + _dot(ds.T.astype(CDT), qe)
                dv3 = dv3 + _dot(pr.T.astype(CDT), doe)
            dq_o[rows, :] = dq
            prev = pl.ds(pl.multiple_of(jnp.maximum(ib - 1, 0) * BLK, BLK), BLK)
            cur = pl.ds(pl.multiple_of(ib * BLK, BLK), BLK)
            dk_o[prev, :] += dk3[0:BLK]
            dk_o[cur, :] += dk3[BLK:2 * BLK]
            dk_o[0:BLK, :] += dk3[2 * BLK:]
            dv_o[prev, :] += dv3[0:BLK]
            dv_o[cur, :] += dv3[BLK:2 * BLK]
            dv_o[0:BLK, :] += dv3[2 * BLK:]
            dsk_o[0:1, :] += jnp.where(lo, dsink[0], dsink[1])

    qblk = pl.BlockSpec((ta, 128), lambda p, i: (i, p))
    kvacc = pl.BlockSpec((None, t, 128), lambda p, i: (p // 2, 0, 0))
    bblk = pl.BlockSpec((2, BLK, 3 * BLK), lambda p, i: (p, 0, 0))
    return pl.pallas_call(
        body, name=name, grid=(NPAIR, t // ta),
        in_specs=[pl.BlockSpec(memory_space=pltpu.SMEM), qblk] + _swa_kv_specs(ta) + _swa_kv_specs(ta)
        + [bblk, qblk, qblk, qblk],
        out_specs=[qblk, kvacc, kvacc, bblk, pl.BlockSpec((None, 8, 128), lambda p, i: (p, 0, 0))],
        out_shape=[_sds((t, 512), F32), _sds((2, t, 128), F32), _sds((2, t, 128), F32),
                   _sds((8, BLK, 3 * BLK), F32), _sds((NPAIR, 8, 128), F32)],
        compiler_params=_params(("arbitrary", "arbitrary")),
    )(sinks, qs, kse, kse, kse, vse, vse, vse, bias, o, lse, do)


def _sum8(slots, name):
    def body(a_r, o_o):
        acc = a_r[0]
        for k in range(1, 8):
            acc = acc + a_r[k]
        o_o[...] = acc

    return pl.pallas_call(
        body, name=name, out_shape=_sds((SMALL_ROWS, 128), F32),
        in_specs=[pl.BlockSpec(memory_space=pltpu.VMEM)], out_specs=pl.BlockSpec(memory_space=pltpu.VMEM),
        compiler_params=_params(),
    )(slots)


def _place():
    x, y, c = lax.axis_index("x"), lax.axis_index("y"), lax.axis_index("c")
    chips = [(1 - x, y), (x, 1 - y), (1 - x, 1 - y)]
    return x, y, c, chips


def _remote(src, dst, send_sems, recv_sems, k, to):
    return pltpu.make_async_remote_copy(src_ref=src, dst_ref=dst, send_sem=send_sems.at[k], recv_sem=recv_sems.at[k],
                                        device_id=to, device_id_type=MESH_ID)


ANY = pl.BlockSpec(memory_space=pl.ANY)


def _mix_cols(w):
    return jnp.concatenate([w[:, 2312:4360], w[:, 0:1536], w[:, 1544:2312], w[:, 1536:1544],
                            jnp.zeros((w.shape[0], DP - D_IN), w.dtype)], axis=1)


def _unmix_cols(w):
    return jnp.concatenate([w[:, QA:QA + 1536], w[:, FA:FA + 8], w[:, QB:QB + 768], w[:, GA:GA + 2048]], axis=1)


def _rows128(a, rows):
    flat = a.reshape(-1)
    return jnp.pad(flat, (0, rows * 128 - flat.shape[0])).reshape(rows, 128)


GRAD_FORM = {"ffn1_w_in": "col", "ffn2_w_in": "col", "w_branch_fox": "col", "w_branch_swa": "col",
             "ffn1_w_out": "3d", "ffn2_w_out": "3d", "w_out": "3d", "w_in": "3d"}
SUM_TILE = {1024: 128, 704: 176, 512: 128, 256: 128}
NT = len(SHARD_ITEMS)
ALL_ITEMS = tuple(range(NT))


def _half_rows(c, r):
    return pl.ds(pl.multiple_of(c * (r // 2), 16), r // 2)


def _ici_copies(kind, srcs, dsts, send_sems, recv_sems, layer, recv=True, items=ALL_ITEMS):
    x, y, c, chips = _place()
    s = 2 * x + y
    sends, recvs = [], []
    for t, (item, src, dst) in enumerate(zip(items, srcs, dsts)):
        nm, (r, cc), _ = SHARD_ITEMS[item]
        for j, (cx, cy) in enumerate(chips):
            sj = 2 * cx + cy
            k = 3 * t + j
            to = (cx, cy, c)
            if kind == "gather":
                hs = _half_rows(c, r)
                sends.append(_remote(src.at[layer, hs], dst.at[s, hs], send_sems, recv_sems, k, to))
                if recv:
                    recvs.append(_remote(src.at[layer, hs], dst.at[sj, hs], send_sems, recv_sems, k, to))
            else:
                if GRAD_FORM[nm] == "col":
                    piece = src.at[:, pl.ds(pl.multiple_of(sj * cc, 128), cc)]
                else:
                    piece = src.at[sj]
                sends.append(_remote(piece, dst.at[j], send_sems, recv_sems, k, to))
                recvs.append(sends[-1])
    return sends, recvs


def _slab_shapes():
    return [_sds((4, r, c), CDT) for _, (r, c), _ in SHARD_ITEMS]


def _dma_sems(n):
    return [pltpu.SemaphoreType.DMA((n,)), pltpu.SemaphoreType.DMA((n,))]


def _forward_sends(dsts, send_sems, recv_sems):
    x, y, c, chips = _place()
    sends, recvs = [], []
    for t, ((nm, (r, cc), _), dst) in enumerate(zip(SHARD_ITEMS, dsts)):
        for j, (cx, cy) in enumerate(chips):
            sj = 2 * cx + cy
            hs, ho = _half_rows(c, r), _half_rows(1 - c, r)
            sends.append(_remote(dst.at[sj, hs], dst.at[sj, hs], send_sems, recv_sems, 3 * t + j, (x, y, 1 - c)))
            recvs.append(_remote(dst.at[sj, ho], dst.at[sj, ho], send_sems, recv_sems, 3 * t + j, (x, y, 1 - c)))
    return sends, recvs


def _gather_layer(wb, mflat, layer, name):
    def body(*refs):
        srcs, m_r, dsts, mall_o = refs[:NT], refs[NT], refs[NT + 1:2 * NT + 1], refs[2 * NT + 1]
        send_sems, recv_sems, fsend, frecv, msend, mrecv = refs[2 * NT + 2:]
        x, y, c, chips = _place()
        s = 2 * x + y
        sends, recvs = _ici_copies("gather", srcs, dsts, send_sems, recv_sems, layer)
        metas = [_remote(m_r, mall_o.at[s], msend, mrecv, j, (cx, cy, c)) for j, (cx, cy) in enumerate(chips)]
        for cp in sends + metas:
            cp.start()
        fwds, frecvs = _forward_sends(dsts, fsend, frecv)
        for got, fwd in zip(recvs, fwds):
            got.wait_recv()
            fwd.start()
        for got in frecvs:
            got.wait_recv()
        for j, (cx, cy) in enumerate(chips):
            _remote(m_r, mall_o.at[2 * cx + cy], msend, mrecv, j, (cx, cy, c)).wait_recv()
        for cp in sends + metas + fwds:
            cp.wait_send()

    return pl.pallas_call(
        body, name=name, out_shape=_slab_shapes() + [_sds((4, META_ROWS, 128), F32)],
        in_specs=[ANY] * (NT + 1), out_specs=[ANY] * (NT + 1),
        scratch_shapes=_dma_sems(3 * NT) + _dma_sems(3 * NT) + _dma_sems(3),
    )(*wb, mflat)


def _forward_layer(slabs, name):
    def body(*refs):
        ins, outs, send_sems, recv_sems = refs[:NT], refs[NT:2 * NT], refs[2 * NT], refs[2 * NT + 1]
        sends, recvs = _forward_sends(outs, send_sems, recv_sems)
        for cp in sends:
            cp.start()
        for cp in recvs:
            cp.wait_recv()
        for cp in sends:
            cp.wait_send()

    return pl.pallas_call(
        body, name=name, out_shape=_slab_shapes(), in_specs=[ANY] * NT, out_specs=[ANY] * NT,
        input_output_aliases={t: t for t in range(NT)}, scratch_shapes=_dma_sems(3 * NT),
    )(*slabs)


def _half_shape(nm, r, c):
    return (r // 2, 4 * c) if GRAD_FORM[nm] == "col" else (4, r // 2, c)


def _swap_layer(gs, gsm, name, items=ALL_ITEMS):
    small = gsm is not None
    nt = len(items)

    def body(*refs):
        g_rs = refs[:nt]
        pos = nt
        if small:
            s_r = refs[pos]
            pos += 1
        got_os = refs[pos:pos + nt]
        pos += nt
        if small:
            slots_o = refs[pos]
            pos += 1
        send_sems, recv_sems = refs[pos], refs[pos + 1]
        x, y, c, _ = _place()
        sib = (x, y, 1 - c)
        sent = []
        for t, (item, g_r, got_o) in enumerate(zip(items, g_rs, got_os)):
            nm, (r, cc), _ = SHARD_ITEMS[item]
            ho = _half_rows(1 - c, r)
            src = g_r.at[ho, :] if GRAD_FORM[nm] == "col" else g_r.at[:, ho, :]
            sent.append(_remote(src, got_o, send_sems, recv_sems, t, sib))
        if small:
            ssend, srecv, loc_sem = refs[pos + 2], refs[pos + 3], refs[pos + 4]
            me = 4 * x + 2 * y + c
            loc = pltpu.make_async_copy(s_r, slots_o.at[me], loc_sem.at[0])
            loc.start()
            peers = [(x ^ (k >> 2), y ^ ((k >> 1) & 1), c ^ (k & 1)) for k in range(1, 8)]
            for k, peer in enumerate(peers):
                sent.append(_remote(s_r, slots_o.at[me], ssend, srecv, k, peer))
        for cp in sent:
            cp.start()
        for cp in sent[:nt]:
            cp.wait_recv()
        if small:
            for k, (px, py, pc) in enumerate(peers):
                _remote(s_r, slots_o.at[4 * px + 2 * py + pc], ssend, srecv, k, (px, py, pc)).wait_recv()
        for cp in sent:
            cp.wait_send()
        if small:
            loc.wait()

    outs = [_sds(_half_shape(*SHARD_ITEMS[item][0:1], *SHARD_ITEMS[item][1]), CDT) for item in items]
    ops = list(gs)
    sems = _dma_sems(nt)
    if small:
        outs.append(_sds((8, SMALL_ROWS, 128), F32))
        ops.append(gsm)
        sems = sems + _dma_sems(7) + [pltpu.SemaphoreType.DMA((1,))]
    res = pl.pallas_call(
        body, name=name, out_shape=outs, in_specs=[ANY] * len(ops), out_specs=[ANY] * len(outs), scratch_shapes=sems,
    )(*ops)
    return (res[:nt], res[nt]) if small else (res, None)


def _pair_add_t(own, got, half_idx, nm, r, name):
    tr = SUM_TILE[r]
    nb = (r // 2) // tr
    if GRAD_FORM[nm] == "col":
        blk = (tr, own.shape[1])
        own_spec = pl.BlockSpec(blk, lambda i, c_r: (c_r[0] * nb + i, 0))
        half_spec = pl.BlockSpec(blk, lambda i, c_r: (i, 0))
    else:
        blk = (4, tr, own.shape[2])
        own_spec = pl.BlockSpec(blk, lambda i, c_r: (0, c_r[0] * nb + i, 0))
        half_spec = pl.BlockSpec(blk, lambda i, c_r: (0, i, 0))

    def body(c_r, a_r, b_r, o_o):
        o_o[...] = (a_r[...].astype(F32) + b_r[...].astype(F32)).astype(CDT)

    grid_spec = pltpu.PrefetchScalarGridSpec(num_scalar_prefetch=1, grid=(nb,), in_specs=[own_spec, half_spec],
                                             out_specs=half_spec)
    return pl.pallas_call(body, name=name, grid_spec=grid_spec, out_shape=_sds(got.shape, CDT),
                          compiler_params=_params(("parallel",)))(half_idx, own, got)


def _sum4_t(ps, got3, buf, idx, layer, nm, r, name):
    tr = SUM_TILE[r]
    nb = (r // 2) // tr
    c = got3.shape[2]
    if GRAD_FORM[nm] == "col":
        ps_spec = pl.BlockSpec((tr, c), lambda i, x_r: (i, x_r[0]))
    else:
        ps_spec = pl.BlockSpec((None, tr, c), lambda i, x_r: (x_r[0], i, 0))

    def body(x_r, a_r, b_r, buf_r, o_o):
        o_o[...] = ((a_r[...].astype(F32) + b_r[0].astype(F32)) + b_r[1].astype(F32)) + b_r[2].astype(F32)

    grid_spec = pltpu.PrefetchScalarGridSpec(
        num_scalar_prefetch=1, grid=(nb,),
        in_specs=[ps_spec, pl.BlockSpec((3, tr, c), lambda i, x_r: (0, i, 0)), ANY],
        out_specs=pl.BlockSpec((None, tr, c), lambda i, x_r: (layer, x_r[1] * nb + i, 0)),
    )
    return pl.pallas_call(body, name=name, grid_spec=grid_spec, out_shape=_sds(buf.shape, F32),
                          input_output_aliases={3: 0}, compiler_params=_params(("parallel",)))(idx, ps, got3, buf)


def _scatter_layer(ps, name, items=ALL_ITEMS):
    nt = len(items)

    def body(*refs):
        srcs, dsts, send_sems, recv_sems = refs[:nt], refs[nt:2 * nt], refs[2 * nt], refs[2 * nt + 1]
        sends, recvs = _ici_copies("scatter", srcs, dsts, send_sems, recv_sems, None, items=items)
        for cp in sends:
            cp.start()
        for cp in recvs:
            cp.wait_recv()
        for cp in sends:
            cp.wait_send()

    return pl.pallas_call(
        body, name=name, out_shape=_got3_shapes(items), in_specs=[ANY] * nt, out_specs=[ANY] * nt,
        scratch_shapes=_dma_sems(3 * nt),
    )(*ps)


def _got3_shapes(items=ALL_ITEMS):
    return [_sds((3, SHARD_ITEMS[t][1][0] // 2, SHARD_ITEMS[t][1][1]), CDT) for t in items]


def _join_layer(bufs, name):
    def body(*refs):
        ins, outs, send_sems, recv_sems = refs[:NT], refs[NT:2 * NT], refs[2 * NT], refs[2 * NT + 1]
        x, y, c, _ = _place()
        sent = []
        for t, ((nm, (r, cc), _), b_o) in enumerate(zip(SHARD_ITEMS, outs)):
            hs = _half_rows(c, r)
            sent.append(_remote(b_o.at[:, hs, :], b_o.at[:, hs, :], send_sems, recv_sems, t, (x, y, 1 - c)))
        for cp in sent:
            cp.start()
        for t, ((nm, (r, cc), _), b_o) in enumerate(zip(SHARD_ITEMS, outs)):
            ho = _half_rows(1 - c, r)
            _remote(b_o.at[:, ho, :], b_o.at[:, ho, :], send_sems, recv_sems, t, (x, y, 1 - c)).wait_recv()
        for cp in sent:
            cp.wait_send()

    return pl.pallas_call(
        body, name=name, out_shape=[_sds(b.shape, F32) for b in bufs], in_specs=[ANY] * NT, out_specs=[ANY] * NT,
        input_output_aliases={t: t for t in range(NT)}, scratch_shapes=_dma_sems(NT),
    )(*bufs)


def _adamw3(w, g, m, v, name):
    nl, r, c = w.shape
    tr = SUM_TILE.get(r, r)

    def body(w_r, g_r, m_r, v_r, d_o, m_o, v_o):
        g_ = g_r[...]
        m_ = ADAM_B1 * m_r[...] + (1.0 - ADAM_B1) * g_
        v_ = ADAM_B2 * v_r[...] + (1.0 - ADAM_B2) * jnp.square(g_)
        m_hat = m_ / (1.0 - ADAM_B1 ** ADAM_STEP)
        v_hat = v_ / (1.0 - ADAM_B2 ** ADAM_STEP)
        d_o[...] = -ADAM_LR * (m_hat / (jnp.sqrt(v_hat) + ADAM_EPS) + ADAM_WD * w_r[...])
        m_o[...] = m_
        v_o[...] = v_

    blk = pl.BlockSpec((None, tr, c), lambda l, i: (l, i, 0))
    return pl.pallas_call(
        body, name=name, grid=(nl, r // tr),
        in_specs=[blk] * 4, out_specs=[blk] * 3, out_shape=[_sds((nl, r, c), F32)] * 3,
        compiler_params=_params(("parallel", "parallel")),
    )(w, g, m, v)


def _full_weights(slabs, wb, layer, shard):
    ws = {}
    for (nm, (r, c), kind), slab in zip(SHARD_ITEMS, slabs):
        slab = lax.dynamic_update_slice(slab, wb[nm][layer][None], (shard, 0, 0))
        ws[nm] = slab.reshape(4 * r, c) if kind == "row" else jnp.concatenate([slab[s] for s in range(4)], axis=1)
    return ws


def _exchange_forms(g, items=ALL_ITEMS):
    out = []
    for t in items:
        nm, (r, c), _ = SHARD_ITEMS[t]
        a = g[nm]
        if nm == "w_in":
            a = a.reshape(D, 4, c).transpose(1, 0, 2)
        elif GRAD_FORM[nm] == "3d":
            a = a.reshape(4, r, c)
        out.append(a)
    return out


SMALL_ITEMS = (("rel_bias_table", 2), ("ffn1_norm", 16), ("mix_norm", 16), ("ffn2_norm", 16), ("forget_bias", 1),
               ("fox_q_norm", 1), ("fox_k_norm", 1), ("swa_q_norm", 1), ("swa_k_norm", 1), ("swa_sinks", 1))
SMALL_ADAM_ROWS = 96


def _layer_fwd(h, lw, l, ride=None):
    sv = {"h0": h}
    a, sv["a1t"] = _rms_fwd(h, lw["ffn1_norm"], f"rms_fwd_a{l}")
    sv["gu1"], s, sv["s1t"] = _ffn_in(a, lw["ffn1_w_in"], f"ffn_in_a{l}")
    h = _mm_res(s, lw["ffn1_w_out"], h, 0.5, f"ffn_out_a{l}")
    sv["h1"] = h
    a, sv["amt"] = _rms_fwd(h, lw["mix_norm"], f"rms_fwd_m{l}")
    proj = _mm(a, lw["w_mix"], F32, _row_tile(h.shape[0]), DP, f"proj{l}")
    sv["proj"] = proj
    qf, kf, vf, qs, kse, vse, c, ct, sv["qft"] = _qknorm_fwd(proj, lw["gfq"], lw["gfk"], lw["gsq"], lw["gsk"], lw["fb"],
                                                              f"qknorm_fwd{l}")
    ofox, lse_f, *rode = _fox_fwd(qf, kf, vf, c, ct, f"fox_fwd{l}", ride)
    oswa, lse_s = _swa_fwd(qs, kse, vse, lw["bias"], lw["sinks"], f"swa_fwd{l}")
    sv.update(qf=qf, kf=kf, vf=vf, qs=qs, kse=kse, vse=vse, c=c, ct=ct, ofox=ofox, oswa=oswa, lse_f=lse_f, lse_s=lse_s)
    y, sv["yt"], sv["pf"], sv["ps"], sv["oft"], sv["ost"] = _gate_fwd(ofox, oswa, lw["w_branch_fox"], lw["w_branch_swa"],
                                                                     proj, f"gate_fwd{l}")
    h = _mm_res(y, lw["w_out"], h, 1.0, f"mix_out{l}")
    sv["h2"] = h
    a, sv["a2t"] = _rms_fwd(h, lw["ffn2_norm"], f"rms_fwd_b{l}")
    sv["gu2"], s, sv["s2t"] = _ffn_in(a, lw["ffn2_w_in"], f"ffn_in_b{l}")
    h = _mm_res(s, lw["ffn2_w_out"], h, 0.5, f"ffn_out_b{l}")
    return h, sv, rode


def _ffn_bwd(dh, dhb, h_in, at, gu, st, norm, w_in, w_out, tag, rides=None):
    r = rides or (None,) * 4
    rode = []

    def split(res, ride):
        if ride is None:
            return res
        rode.extend(res[-1])
        return res[0] if len(res) == 2 else res[:-1]

    dgu = split(_ffn_bwd_mid(dhb, w_out, gu, f"ffn_bwd_mid_{tag}", r[0]), r[0])
    d_w_out = split(_mm(st, dhb, CDT, 256, D, f"dw_ffn_out_{tag}", scale=0.5, ride=r[1]), r[1])
    dh, dhb, dg = split(_ffn_bwd_in(dgu, w_in, h_in, norm, dh, f"ffn_bwd_in_{tag}", r[2]), r[2])
    d_w_in = split(_mm(at, dgu, CDT, D, 256, f"dw_ffn_in_{tag}", ride=r[3]), r[3])
    return dh, dhb, d_w_out, d_w_in, dg, rode


def _layer_bwd(dh, dhb, sv, lw, l, ride=None, before_ffn1=None):
    g = {}
    dh, dhb, g["ffn2_w_out"], g["ffn2_w_in"], g["ffn2_norm"], _ = _ffn_bwd(
        dh, dhb, sv["h2"], sv["a2t"], sv["gu2"], sv["s2t"], lw["ffn2_norm"], lw["ffn2_w_in"], lw["ffn2_w_out"], f"b{l}")
    dy = _mm_nt(dhb, lw["w_out"], f"d_y{l}")
    g["w_out"] = _mm(sv["yt"], dhb, CDT, 512, 512, f"dw_out{l}")
    dpf, dps, dga, dgb = _gate_bwd(dy, sv["pf"], sv["ps"], sv["proj"], f"gate_bwd{l}")
    do_f, do_ft = _mm_nt(dpf, lw["w_branch_fox"], f"d_ofox{l}", with_t=True)
    do_s = _mm_nt(dps, lw["w_branch_swa"], f"d_oswa{l}")
    g["w_branch_fox"] = _mm(sv["oft"], dpf, CDT, 512, 512, f"dw_bfox{l}")
    g["w_branch_swa"] = _mm(sv["ost"], dps, CDT, 512, 512, f"dw_bswa{l}")
    dqf, dcq, dkf, dvf, dck, *rode = _fox_bwd(sv["qf"], sv["qft"], sv["kf"], sv["vf"], sv["c"], sv["ct"], sv["ofox"],
                                              sv["lse_f"], do_f, do_ft, f"fox_bwd{l}", ride)
    g["rode"] = rode
    dqs, dkse, dvse, dbias, dsk = _swa_bwd(sv["qs"], sv["kse"], sv["vse"], lw["bias"], lw["sinks"], sv["oswa"],
                                           sv["lse_s"], do_s, f"swa_bwd{l}")
    dproj, dgn = _qknorm_bwd(sv["proj"], dqf, dkf, dvf, dqs, dkse, dvse, dcq, dck, dga, dgb,
                             lw["gfq"], lw["gfk"], lw["gsq"], lw["gsk"], lw["fb"], f"qknorm_bwd{l}")
    g["w_mix"] = _mm(sv["amt"], dproj, CDT, 512, 640, f"dw_mix{l}")
    dh, dhb, g["mix_norm"] = _mm_nt_rms(dproj, lw["w_mix"], sv["h1"], lw["mix_norm"], dh, f"d_am{l}")
    g["dbias"], g["dsk"], g["dgn"] = dbias, dsk, dgn
    rides = before_ffn1(g) if before_ffn1 else None
    dh, dhb, g["ffn1_w_out"], g["ffn1_w_in"], g["ffn1_norm"], g["rode_ffn1"] = _ffn_bwd(
        dh, dhb, sv["h0"], sv["a1t"], sv["gu1"], sv["s1t"], lw["ffn1_norm"], lw["ffn1_w_in"], lw["ffn1_w_out"], f"a{l}",
        rides)
    return dh, dhb, g


def kernel(x, meta_tokens, rel_bias_table, ffn1_norm, ffn1_w_in, ffn1_w_out, mix_norm, w_in, forget_bias, fox_q_norm, fox_k_norm, swa_q_norm, swa_k_norm, swa_sinks, w_branch_fox, w_branch_swa, w_out, ffn2_norm, ffn2_w_in, ffn2_w_out, loss_target, m_meta_tokens, m_rel_bias_table, m_ffn1_norm, m_ffn1_w_in, m_ffn1_w_out, m_mix_norm, m_w_in, m_forget_bias, m_fox_q_norm, m_fox_k_norm, m_swa_q_norm, m_swa_k_norm, m_swa_sinks, m_w_branch_fox, m_w_branch_swa, m_w_out, m_ffn2_norm, m_ffn2_w_in, m_ffn2_w_out, v_meta_tokens, v_rel_bias_table, v_ffn1_norm, v_ffn1_w_in, v_ffn1_w_out, v_mix_norm, v_w_in, v_forget_bias, v_fox_q_norm, v_fox_k_norm, v_swa_q_norm, v_swa_k_norm, v_swa_sinks, v_w_branch_fox, v_w_branch_swa, v_w_out, v_ffn2_norm, v_ffn2_w_in, v_ffn2_w_out):
    names = ["meta_tokens", "rel_bias_table", "ffn1_norm", "ffn1_w_in", "ffn1_w_out", "mix_norm", "w_in", "forget_bias",
             "fox_q_norm", "fox_k_norm", "swa_q_norm", "swa_k_norm", "swa_sinks", "w_branch_fox", "w_branch_swa", "w_out",
             "ffn2_norm", "ffn2_w_in", "ffn2_w_out"]
    w = dict(zip(names, [meta_tokens, rel_bias_table, ffn1_norm, ffn1_w_in, ffn1_w_out, mix_norm, w_in, forget_bias,
                         fox_q_norm, fox_k_norm, swa_q_norm, swa_k_norm, swa_sinks, w_branch_fox, w_branch_swa, w_out,
                         ffn2_norm, ffn2_w_in, ffn2_w_out]))
    m = dict(zip(names, [m_meta_tokens, m_rel_bias_table, m_ffn1_norm, m_ffn1_w_in, m_ffn1_w_out, m_mix_norm, m_w_in,
                         m_forget_bias, m_fox_q_norm, m_fox_k_norm, m_swa_q_norm, m_swa_k_norm, m_swa_sinks,
                         m_w_branch_fox, m_w_branch_swa, m_w_out, m_ffn2_norm, m_ffn2_w_in, m_ffn2_w_out]))
    v = dict(zip(names, [v_meta_tokens, v_rel_bias_table, v_ffn1_norm, v_ffn1_w_in, v_ffn1_w_out, v_mix_norm, v_w_in,
                         v_forget_bias, v_fox_q_norm, v_fox_k_norm, v_swa_q_norm, v_swa_k_norm, v_swa_sinks,
                         v_w_branch_fox, v_w_branch_swa, v_w_out, v_ffn2_norm, v_ffn2_w_in, v_ffn2_w_out]))
    xi, yi, ci = lax.axis_index("x"), lax.axis_index("y"), lax.axis_index("c")
    shard = 2 * xi + yi
    seq = x.shape[1]
    t = seq + BLK

    wb = {nm: w[nm].astype(CDT) for nm, _, _ in SHARD_ITEMS}
    wb_list = [wb[nm] for nm, _, _ in SHARD_ITEMS]
    mflat = meta_tokens.reshape(META_ROWS, 128)
    *slabs0, mall = _gather_layer(wb_list, mflat, 0, "gather_weights")
    mall = lax.dynamic_update_slice(mall, mflat[None], (shard, 0, 0))
    meta_full = jnp.concatenate([mall[s].reshape(N_META, 256) for s in range(4)], axis=1)
    bias = _bias_fwd(rel_bias_table, "bias_fwd")

    def layer_weights(slabs, l):
        lw = _full_weights(slabs, wb, l, shard)
        lw["w_mix"] = _mix_cols(lw.pop("w_in"))
        for nm in ("ffn1_norm", "mix_norm", "ffn2_norm"):
            lw[nm] = w[nm][l].reshape(1, D)
        lw["gfq"] = jnp.tile(fox_q_norm[l], 8).reshape(1, 512)
        lw["gfk"] = jnp.tile(fox_k_norm[l], 8).reshape(1, 512)
        lw["gsq"] = jnp.tile(swa_q_norm[l], 8).reshape(1, 512)
        lw["gsk"] = jnp.tile(swa_k_norm[l], 2).reshape(1, 128)
        lw["fb"] = jnp.pad(forget_bias[l], (0, 120)).reshape(1, 128)
        lw["sinks"] = swa_sinks[l]
        lw["bias"] = bias
        return lw

    h = jnp.concatenate([jnp.zeros((PAD, D), F32), meta_full, x[0]], axis=0)
    lws = [layer_weights(slabs0, 0)]
    h, sv0, slabs1 = _layer_fwd(h, lws[0], 0, ("gather", wb_list, _slab_shapes(), 1, ALL_ITEMS))
    lws.append(layer_weights(_forward_layer(slabs1, "forward_halves"), 1))
    h, sv1, _ = _layer_fwd(h, lws[1], 1)
    saved = [sv0, sv1]
    dh, dhb, lacc = _loss(h, loss_target[0], "loss")
    loss = lax.psum(lacc[0, 0], ("x", "y", "c"))

    half_idx = ci.reshape(1).astype(jnp.int32)
    place_idx = jnp.stack([shard, ci]).astype(jnp.int32)

    def pair_sums(g, gsm, tag, items=ALL_ITEMS):
        if "w_mix" in g:
            g["w_in"] = _unmix_cols(g.pop("w_mix"))
        forms = _exchange_forms(g, items)
        got, slots = _swap_layer(forms, gsm, f"swap_halves{tag}", items)
        return {t: _pair_add_t(a, b, half_idx, SHARD_ITEMS[t][0], SHARD_ITEMS[t][1][0],
                               f"pair_add{tag}_{SHARD_ITEMS[t][0]}")
                for t, a, b in zip(items, forms, got)}, slots

    def scatter_ride(ps, items):
        return ("scatter", [ps[t] for t in items], _got3_shapes(items), None, items)

    early = (2, 3, 4, 5, 6, 7)
    early_rides = ((6,), (7,), (2, 5), (3, 4))
    ps0 = {}

    def before_ffn1(g):
        ps0.update(pair_sums(g, None, "0e", early)[0])
        return [scatter_ride(ps0, items) for items in early_rides]

    grads = [None, None]
    dh, dhb, grads[1] = _layer_bwd(dh, dhb, saved[1], lws[1], 1)
    ps1, _ = pair_sums(grads[1], None, 1)
    dh, dhb, grads[0] = _layer_bwd(dh, dhb, saved[0], lws[0], 0, scatter_ride(ps1, ALL_ITEMS), before_ffn1)
    grad_x = dh[BLK:].reshape(1, seq, D)
    dtab = _bias_bwd(grads[0]["dbias"] + grads[1]["dbias"], "bias_bwd")

    small = [dh[PAD:BLK].reshape(128, 128), _rows128(dtab[:, :N_BUCKETS].T, 2)]
    for nm in ("ffn1_norm", "mix_norm", "ffn2_norm"):
        small.append(jnp.stack([grads[l][nm][0] for l in range(2)]).reshape(16, 128))
    small.append(_rows128(jnp.stack([grads[l]["dgn"][4, :8] for l in range(2)]), 1))
    for row in range(4):
        small.append(jnp.stack([grads[l]["dgn"][row, :HD] for l in range(2)]).reshape(1, 128))
    dsk = [grads[l]["dsk"][:, 0, :] for l in range(2)]
    small.append(_rows128(jnp.stack([jnp.stack([d[:, 0], d[:, HD]], axis=1).reshape(8) for d in dsk]), 1))
    gsm = jnp.concatenate(small, axis=0)
    gsm = jnp.pad(gsm, ((0, SMALL_ROWS - gsm.shape[0]), (0, 0)))

    late = (0, 1)
    ps_late, slots = pair_sums(grads[0], gsm, "0l", late)
    ps0.update(ps_late)
    got3_0 = dict(zip([t for items in early_rides for t in items], grads[0]["rode_ffn1"]))
    got3_0.update(zip(late, _scatter_layer([ps0[t] for t in late], "scatter_shards", late)))
    got3 = [got3_0, dict(zip(ALL_ITEMS, grads[0]["rode"]))]
    bufs = []
    for t, (nm, (r, c), _) in enumerate(SHARD_ITEMS):
        buf = lax.empty((2, r, c), F32)
        for l, ps in ((1, ps1), (0, ps0)):
            buf = _sum4_t(ps[t], got3[l][t], buf, place_idx, l, nm, r, f"sum4_{l}_{nm}")
        bufs.append(buf)
    bufs = _join_layer(bufs, "join_halves")
    gs = _sum8(slots, "sum8")

    g_out = {nm: buf for (nm, _, _), buf in zip(SHARD_ITEMS, bufs)}
    g_out["meta_tokens"] = lax.dynamic_slice(gs[0:128].reshape(N_META, D), (0, shard * 256), (N_META, 256))
    off = 128
    for nm, rows in SMALL_ITEMS:
        n = w[nm].size
        g_out[nm] = gs[off:off + rows].reshape(-1)[:n].reshape(w[nm].shape)
        off += rows

    delta, new_m, new_v = {}, {}, {}
    for nm, _, _ in SHARD_ITEMS:
        delta[nm], new_m[nm], new_v[nm] = _adamw3(w[nm], g_out[nm], m[nm], v[nm], f"adamw_{nm}")
    small_names = ["meta_tokens"] + [nm for nm, _ in SMALL_ITEMS]
    small_rows = [META_ROWS] + [rows for _, rows in SMALL_ITEMS]

    def pack_small(src):
        buf = jnp.concatenate([_rows128(src[nm], rows) for nm, rows in zip(small_names, small_rows)], axis=0)
        return jnp.pad(buf, ((0, SMALL_ADAM_ROWS - buf.shape[0]), (0, 0)))

    d_, m_, v_ = (a[0] for a in _adamw3(pack_small(w)[None], pack_small(g_out)[None], pack_small(m)[None],
                                        pack_small(v)[None], "adamw_small"))
    off = 0
    for nm, rows in zip(small_names, small_rows):
        n = w[nm].size
        for dst, src in ((delta, d_), (new_m, m_), (new_v, v_)):
            dst[nm] = src[off:off + rows].reshape(-1)[:n].reshape(w[nm].shape)
        off += rows

    return (loss, grad_x, *[g_out[n] for n in names], *[delta[n] for n in names],
            *[new_m[n] for n in names], *[new_v[n] for n in names])
```

```python
import math

import numpy as np
import jax
import jax.numpy as jnp
from jax import lax
from jax.experimental import pallas as pl
from jax.experimental.pallas import tpu as pltpu

D = 1024
F = 2816
FT = F // 2
HD = 64
NPAIR = 4
N_META = 16
BLK = 128
PAD = BLK - N_META
EPS = 1e-6
NEG = -1e30
N_BUCKETS = 32
GA, GB, QA, KA, VA, QB, KB, VB, FA, DP = 0, 1024, 2048, 2560, 3072, 3584, 4096, 4224, 4352, 4480
D_IN = 4360
CDT = jnp.bfloat16
F32 = jnp.float32
VMEM_LIMIT = 48 * 1024 * 1024
MESH_ID = pl.DeviceIdType.MESH

ADAM_LR, ADAM_B1, ADAM_B2, ADAM_EPS, ADAM_WD, ADAM_STEP = 0.001, 0.9, 0.999, 1e-08, 0.01, 10

SHARD_ITEMS = (
    ("ffn1_w_in", (1024, 1408), "col"),
    ("ffn1_w_out", (704, 1024), "row"),
    ("w_in", (1024, 1090), "col"),
    ("w_branch_fox", (512, 256), "col"),
    ("w_branch_swa", (512, 256), "col"),
    ("w_out", (256, 1024), "row"),
    ("ffn2_w_in", (1024, 1408), "col"),
    ("ffn2_w_out", (704, 1024), "row"),
)
SMALL_ROWS = 192
META_ROWS = 32


def _row_tile(t):
    return 384 if t % 384 == 0 else 128


def _dot(a, b):
    return jnp.dot(a, b, preferred_element_type=F32)


def _dot_nt(a, b):
    return lax.dot_general(a, b, (((1,), (1,)), ((), ())), preferred_element_type=F32)


def _dot_hi(a, b):
    return jnp.dot(a, b, preferred_element_type=F32, precision=lax.Precision.HIGHEST)


def _sigmoid(x):
    return 1.0 / (1.0 + jnp.exp(-x))


def _iota(shape, dim):
    return lax.broadcasted_iota(jnp.int32, shape, dim)


def _params(sem=None):
    return pltpu.CompilerParams(dimension_semantics=sem, vmem_limit_bytes=VMEM_LIMIT)


def _sds(shape, dtype):
    return jax.ShapeDtypeStruct(shape, dtype)


def _rms_fwd(h, g, name):
    t = h.shape[0]
    tm = _row_tile(t)

    def body(h_ref, g_ref, a_ref, at_ref):
        x = h_ref[...]
        ms = jnp.mean(x * x, axis=-1, keepdims=True)
        a = x * lax.rsqrt(ms + EPS) * g_ref[...]
        a_ref[...] = a.astype(CDT)
        at_ref[...] = a.T.astype(CDT)

    return pl.pallas_call(
        body, name=name, grid=(t // tm,),
        in_specs=[pl.BlockSpec((tm, D), lambda i: (i, 0)), pl.BlockSpec((1, D), lambda i: (0, 0))],
        out_specs=[pl.BlockSpec((tm, D), lambda i: (i, 0)), pl.BlockSpec((D, tm), lambda i: (0, i))],
        out_shape=[_sds((t, D), CDT), _sds((D, t), CDT)],
        compiler_params=_params(("parallel",)),
    )(h, g)


def _ffn_in(a, w_in, name, ride=None):
    t = a.shape[0]
    tm = _row_tile(t)
    tn = FT
    nj = F // tn
    grid = (nj, t // tm)
    ride_in, ride_in_specs, ride_out, ride_out_specs, ride_sems = _ride_specs(ride)

    def body(a_ref, wg_ref, wu_ref, gu_ref, s_ref, st_ref):
        a_ = a_ref[...]
        g = _dot(a_, wg_ref[...])
        u = _dot(a_, wu_ref[...])
        s = g * _sigmoid(g) * u
        gu_ref[0] = g.astype(CDT)
        gu_ref[1] = u.astype(CDT)
        s_ref[...] = s.astype(CDT)
        st_ref[...] = s.T.astype(CDT)

    res = pl.pallas_call(
        _riding(body, 3, 3, ride, grid), name=name, grid=grid,
        in_specs=[pl.BlockSpec((tm, D), lambda j, i: (i, 0)),
                  pl.BlockSpec((D, tn), lambda j, i: (0, j)),
                  pl.BlockSpec((D, tn), lambda j, i: (0, j + nj))] + ride_in_specs,
        out_specs=[pl.BlockSpec((2, tm, tn), lambda j, i: (0, i, j)),
                   pl.BlockSpec((tm, tn), lambda j, i: (i, j)),
                   pl.BlockSpec((tn, tm), lambda j, i: (j, i))] + ride_out_specs,
        out_shape=[_sds((2, t, F), CDT), _sds((t, F), CDT), _sds((F, t), CDT)] + ride_out, scratch_shapes=ride_sems,
        compiler_params=_params(("arbitrary", "arbitrary") if ride else ("parallel", "parallel")),
    )(a, w_in, w_in, *ride_in)
    return (*res[:3], res[3:]) if ride else res


def _mm_res(a, b, res, scale, name, ride=None):
    t, k = a.shape
    n = b.shape[1]
    tm = _row_tile(t)
    tn = 512
    grid = (t // tm, n // tn)
    ride_in, ride_in_specs, ride_out, ride_out_specs, ride_sems = _ride_specs(ride)

    def body(a_ref, b_ref, r_ref, o_ref):
        o_ref[...] = r_ref[...] + scale * _dot(a_ref[...], b_ref[...])

    out = pl.pallas_call(
        _riding(body, 3, 1, ride, grid), name=name, grid=grid,
        in_specs=[pl.BlockSpec((tm, k), lambda i, j: (i, 0)),
                  pl.BlockSpec((k, tn), lambda i, j: (0, j)),
                  pl.BlockSpec((tm, tn), lambda i, j: (i, j))] + ride_in_specs,
        out_specs=[pl.BlockSpec((tm, tn), lambda i, j: (i, j))] + ride_out_specs,
        out_shape=[_sds((t, n), F32)] + ride_out, scratch_shapes=ride_sems,
        compiler_params=_params(("arbitrary", "arbitrary") if ride else ("parallel", "parallel")),
    )(a, b, res, *ride_in)
    return (out[0], out[1:]) if ride else out[0]


def _mm(a, b, out_dtype, tm, tn, name, scale=1.0, ride=None):
    m, k = a.shape
    if b.ndim == 3:
        nh = b.shape[2] // tn
        n = 2 * b.shape[2]
        b_spec = pl.BlockSpec((None, k, tn), lambda i, j: (j // nh, 0, j % nh))
    else:
        n = b.shape[1]
        b_spec = pl.BlockSpec((k, tn), lambda i, j: (0, j))
    grid = (m // tm, n // tn)
    ride_in, ride_in_specs, ride_out, ride_out_specs, ride_sems = _ride_specs(ride)

    def body(a_ref, b_ref, o_ref):
        o_ref[...] = (scale * _dot(a_ref[...], b_ref[...])).astype(out_dtype)

    res = pl.pallas_call(
        _riding(body, 2, 1, ride, grid), name=name, grid=grid,
        in_specs=[pl.BlockSpec((tm, k), lambda i, j: (i, 0)), b_spec] + ride_in_specs,
        out_specs=[pl.BlockSpec((tm, tn), lambda i, j: (i, j))] + ride_out_specs,
        out_shape=[_sds((m, n), out_dtype)] + ride_out, scratch_shapes=ride_sems,
        compiler_params=_params(("arbitrary", "arbitrary") if ride else ("parallel", "parallel")),
    )(a, b, *ride_in)
    return (res[0], res[1:]) if ride else res[0]


def _mm_nt(a, b, name, with_t=False):
    m, n = a.shape
    k = b.shape[0]
    tm = _row_tile(m)
    tk = 512

    def body(a_ref, b_ref, o_ref, *t_ref):
        r = _dot_nt(a_ref[...], b_ref[...])
        o_ref[...] = r
        if with_t:
            t_ref[0][...] = r.T.astype(CDT)

    out_specs = [pl.BlockSpec((tm, tk), lambda i, j: (i, j))]
    out_shape = [_sds((m, k), F32)]
    if with_t:
        out_specs.append(pl.BlockSpec((tk, tm), lambda i, j: (j, i)))
        out_shape.append(_sds((k, m), CDT))
    res = pl.pallas_call(
        body, name=name, grid=(m // tm, k // tk),
        in_specs=[pl.BlockSpec((tm, n), lambda i, j: (i, 0)), pl.BlockSpec((tk, n), lambda i, j: (j, 0))],
        out_specs=out_specs, out_shape=out_shape,
        compiler_params=_params(("parallel", "parallel")),
    )(a, b)
    return res if with_t else res[0]


def _ffn_bwd_mid(dhb, w_out, gu, name, ride=None):
    t = dhb.shape[0]
    tm = _row_tile(t)
    tn = FT
    grid = (F // tn, t // tm)
    ride_in, ride_in_specs, ride_out, ride_out_specs, ride_sems = _ride_specs(ride)

    def body(dh_ref, w_ref, gu_ref, o_ref):
        ds = 0.5 * _dot_nt(dh_ref[...], w_ref[...])
        g = gu_ref[0].astype(F32)
        u = gu_ref[1].astype(F32)
        sg = _sigmoid(g)
        o_ref[0] = (ds * u * (sg * (1.0 + g * (1.0 - sg)))).astype(CDT)
        o_ref[1] = (ds * (g * sg)).astype(CDT)

    res = pl.pallas_call(
        _riding(body, 3, 1, ride, grid), name=name, grid=grid,
        in_specs=[pl.BlockSpec((tm, D), lambda j, i: (i, 0)),
                  pl.BlockSpec((tn, D), lambda j, i: (j, 0)),
                  pl.BlockSpec((2, tm, tn), lambda j, i: (0, i, j))] + ride_in_specs,
        out_specs=[pl.BlockSpec((2, tm, tn), lambda j, i: (0, i, j))] + ride_out_specs,
        out_shape=[_sds((2, t, F), CDT)] + ride_out, scratch_shapes=ride_sems,
        compiler_params=_params(("arbitrary", "arbitrary") if ride else ("parallel", "parallel")),
    )(dhb, w_out, gu, *ride_in)
    return (res[0], res[1:]) if ride else res[0]


def _rms_bwd_rows(da_, x, g, dres, i, dh_ref, dhb_ref, dg_ref):
    r = lax.rsqrt(jnp.mean(x * x, axis=-1, keepdims=True) + EPS)
    xh = x * r
    day = da_ * g
    dh = dres + r * (day - xh * jnp.mean(day * xh, axis=-1, keepdims=True))
    dh_ref[...] = dh
    dhb_ref[...] = dh.astype(CDT)

    @pl.when(i == 0)
    def _():
        dg_ref[...] = jnp.zeros(dg_ref.shape, F32)

    dg_ref[0:1, :] += jnp.sum(da_ * xh, axis=0, keepdims=True)


def _ffn_bwd_in(dgu, w_in, h, g, dres, name, ride=None):
    t = dgu.shape[1]
    tm = _row_tile(t)
    grid = (t // tm,)
    ride_in, ride_in_specs, ride_out, ride_out_specs, ride_sems = _ride_specs(ride)

    def body(dg_ref, wg_ref, wu_ref, h_ref, g_ref, dr_ref, dh_ref, dhb_ref, dgn_ref):
        da_ = _dot_nt(dg_ref[0], wg_ref[...]) + _dot_nt(dg_ref[1], wu_ref[...])
        _rms_bwd_rows(da_, h_ref[...], g_ref[...], dr_ref[...], pl.program_id(0), dh_ref, dhb_ref, dgn_ref)

    row = pl.BlockSpec((tm, D), lambda i: (i, 0))
    res = pl.pallas_call(
        _riding(body, 6, 3, ride, grid), name=name, grid=grid,
        in_specs=[pl.BlockSpec((2, tm, F), lambda i: (0, i, 0)),
                  pl.BlockSpec((D, F), lambda i: (0, 0)),
                  pl.BlockSpec((D, F), lambda i: (0, 1)),
                  row, pl.BlockSpec((1, D), lambda i: (0, 0)), row] + ride_in_specs,
        out_specs=[row, row, pl.BlockSpec((8, D), lambda i: (0, 0))] + ride_out_specs,
        out_shape=[_sds((t, D), F32), _sds((t, D), CDT), _sds((8, D), F32)] + ride_out, scratch_shapes=ride_sems,
        compiler_params=_params(("arbitrary",)),
    )(dgu, w_in, w_in, h, g, dres, *ride_in)
    return (*res[:3], res[3:]) if ride else res


def _mm_nt_rms(a, b, h, g, dres, name):
    t, n = a.shape
    tm = _row_tile(t)

    def body(a_ref, b_ref, h_ref, g_ref, dr_ref, dh_ref, dhb_ref, dgn_ref):
        da_ = _dot_nt(a_ref[...], b_ref[...])
        _rms_bwd_rows(da_, h_ref[...], g_ref[...], dr_ref[...], pl.program_id(0), dh_ref, dhb_ref, dgn_ref)

    row = pl.BlockSpec((tm, D), lambda i: (i, 0))
    return pl.pallas_call(
        body, name=name, grid=(t // tm,),
        in_specs=[pl.BlockSpec((tm, n), lambda i: (i, 0)), pl.BlockSpec((D, n), lambda i: (0, 0)),
                  row, pl.BlockSpec((1, D), lambda i: (0, 0)), row],
        out_specs=[row, row, pl.BlockSpec((8, D), lambda i: (0, 0))],
        out_shape=[_sds((t, D), F32), _sds((t, D), CDT), _sds((8, D), F32)],
        compiler_params=_params(("arbitrary",)),
    )(a, b, h, g, dres)


def _loss(h, target, name):
    t = h.shape[0]

    def body(h_ref, t_ref, dh_ref, dhb_ref, l_ref):
        i = pl.program_id(0)

        @pl.when(i == 0)
        def _():
            l_ref[...] = jnp.zeros(l_ref.shape, F32)
            dh_ref[...] = jnp.zeros(dh_ref.shape, F32)
            dhb_ref[...] = jnp.zeros(dhb_ref.shape, CDT)

        @pl.when(i > 0)
        def _():
            err = h_ref[...] - t_ref[...]
            l_ref[...] += (0.5 / D) * jnp.sum(err * err)
            d = err * (1.0 / D)
            dh_ref[...] = d
            dhb_ref[...] = d.astype(CDT)

    row = pl.BlockSpec((BLK, D), lambda i: (i, 0))
    return pl.pallas_call(
        body, name=name, grid=(t // BLK,),
        in_specs=[row, pl.BlockSpec((BLK, D), lambda i: (jnp.maximum(i - 1, 0), 0))],
        out_specs=[row, row, pl.BlockSpec((8, 128), lambda i: (0, 0))],
        out_shape=[_sds((t, D), F32), _sds((t, D), CDT), _sds((8, 128), F32)],
        compiler_params=_params(("arbitrary",)),
    )(h, target)


def _block_diag():
    return (_iota((128, 128), 0) // HD == _iota((128, 128), 1) // HD).astype(F32)


def _head_sums(v, bd):
    hi = v.astype(CDT)
    rest = (v - hi.astype(F32)).astype(CDT)
    b = bd.astype(CDT)
    return _dot(hi, b) + _dot(rest, b)


def _dup_halves(x, lo):
    sw = pltpu.roll(x, 64, 1)
    return jnp.where(lo, x, sw), jnp.where(lo, sw, x)


def _qknorm_fwd(proj, gfq, gfk, gsq, gsk, fb, name):
    t = proj.shape[0]
    tm = _row_tile(t)

    def body(qa, ka, va, qb, kb, vb, fa, gfq_r, gfk_r, gsq_r, gsk_r, fb_r,
             qf_o, kf_o, vf_o, qs_o, kse_o, vse_o, c_o, ct_o, qft_o, carry):
        i = pl.program_id(0)
        bd = _block_diag()
        lane = _iota((1, 128), 1)
        lo = lane < HD

        def hnorm(x, g):
            ms = _head_sums(x * x, bd) * (1.0 / HD)
            return x * lax.rsqrt(ms + EPS) * g

        for ch in range(4):
            sl = slice(128 * ch, 128 * (ch + 1))
            qn = hnorm(qa[:, sl], gfq_r[:, sl]) * 0.125
            qf_o[:, sl] = qn.astype(CDT)
            qft_o[sl, :] = qn.T.astype(CDT)
            kf_o[:, sl] = hnorm(ka[:, sl], gfk_r[:, sl]).astype(CDT)
            qs_o[:, sl] = (hnorm(qb[:, sl], gsq_r[:, sl]) * 0.125).astype(CDT)
        vf_o[...] = va[...].astype(CDT)
        k0, k1 = _dup_halves(hnorm(kb[...], gsk_r[...]), lo)
        kse_o[0] = k0.astype(CDT)
        kse_o[1] = k1.astype(CDT)
        v0, v1 = _dup_halves(vb[...], lo)
        vse_o[0] = v0.astype(CDT)
        vse_o[1] = v1.astype(CDT)

        z = fa[...] + fb_r[...]
        lf = jnp.minimum(z, 0.0) - jnp.log(1.0 + jnp.exp(-jnp.abs(z)))
        lf = jnp.where(lane < 8, lf, 0.0)
        ltri = (_iota((tm, tm), 1) <= _iota((tm, tm), 0)).astype(F32)

        @pl.when(i == 0)
        def _():
            carry[...] = jnp.zeros(carry.shape, F32)

        c = _dot_hi(ltri, lf) + carry[0:1, :]
        carry[0:1, :] = c[tm - 1:tm, :]
        c_o[...] = c
        ct_o[...] = c.T[0:8, :]

    def col(width, off):
        return pl.BlockSpec((tm, width), lambda i: (i, off // width))

    def vec(width):
        return pl.BlockSpec((1, width), lambda i: (0, 0))

    return pl.pallas_call(
        body, name=name, grid=(t // tm,),
        in_specs=[col(512, QA), col(512, KA), col(512, VA), col(512, QB), col(128, KB), col(128, VB), col(128, FA),
                  vec(512), vec(512), vec(512), vec(128), vec(128)],
        out_specs=[pl.BlockSpec((tm, 512), lambda i: (i, 0))] * 4
        + [pl.BlockSpec((2, tm, 128), lambda i: (0, i, 0))] * 2
        + [pl.BlockSpec((tm, 128), lambda i: (i, 0)), pl.BlockSpec((8, tm), lambda i: (0, i)),
           pl.BlockSpec((512, tm), lambda i: (0, i))],
        out_shape=[_sds((t, 512), CDT)] * 4 + [_sds((2, t, 128), CDT)] * 2
        + [_sds((t, 128), F32), _sds((8, t), F32), _sds((512, t), CDT)],
        scratch_shapes=[pltpu.VMEM((8, 128), F32)],
        compiler_params=_params(("arbitrary",)),
    )(proj, proj, proj, proj, proj, proj, proj, gfq, gfk, gsq, gsk, fb)


def _qknorm_bwd(proj, dqf, dkf, dvf, dqs, dkse, dvse, dcq, dck, dga, dgb, gfq, gfk, gsq, gsk, fb, name):
    t = proj.shape[0]
    tm = _row_tile(t)
    nt = t // tm

    def body(qa, ka, qb, kb, fa, dqf_r, dkf_r, dvf_r, dqs_r, dkse_r, dvse_r, dcq_r, dck_r, dga_r, dgb_r,
             gfq_r, gfk_r, gsq_r, gsk_r, fb_r, dp_o, dgn_o, carry, acc):
        i = pl.program_id(0)
        bd = _block_diag()
        lane = _iota((1, 128), 1)
        lo = lane < HD

        @pl.when(i == 0)
        def _():
            carry[...] = jnp.zeros(carry.shape, F32)
            acc[...] = jnp.zeros(acc.shape, F32)

        def hnorm_bwd(x, g, dy):
            r = lax.rsqrt(_head_sums(x * x, bd) * (1.0 / HD) + EPS)
            xh = x * r
            day = dy * g
            dx = r * (day - xh * (_head_sums(day * xh, bd) * (1.0 / HD)))
            return dx, jnp.sum(dy * xh, axis=0, keepdims=True)

        for ch in range(4):
            sl = slice(128 * ch, 128 * (ch + 1))
            dx, dg = hnorm_bwd(qa[:, sl], gfq_r[:, sl], dqf_r[:, sl] * 0.125)
            dp_o[:, QA + 128 * ch:QA + 128 * (ch + 1)] = dx.astype(CDT)
            acc[0:1, sl] += dg
            dx, dg = hnorm_bwd(ka[:, sl], gfk_r[:, sl], dkf_r[:, sl])
            dp_o[:, KA + 128 * ch:KA + 128 * (ch + 1)] = dx.astype(CDT)
            acc[1:2, sl] += dg
            dx, dg = hnorm_bwd(qb[:, sl], gsq_r[:, sl], dqs_r[:, sl] * 0.125)
            dp_o[:, QB + 128 * ch:QB + 128 * (ch + 1)] = dx.astype(CDT)
            acc[2:3, sl] += dg
        dp_o[:, VA:VA + 512] = dvf_r[...].astype(CDT)
        dp_o[:, GA:GA + D] = dga_r[...]
        dp_o[:, GB:GB + D] = dgb_r[...]

        def fold(x):
            e0 = x[0]
            e1 = x[1]
            return jnp.where(lo, e0 + pltpu.roll(e0, 64, 1), e1 + pltpu.roll(e1, 64, 1))

        dx, dg = hnorm_bwd(kb[...], gsk_r[...], fold(dkse_r))
        dp_o[:, KB:KB + 128] = dx.astype(CDT)
        acc[3:4, 0:128] += dg
        dp_o[:, VB:VB + 128] = fold(dvse_r).astype(CDT)

        rr = _iota((512, 128), 0)
        hh = _iota((512, 128), 1)
        sel = ((rr == (hh >> 1) * 128 + (hh & 1) * HD) & (hh < 8)).astype(F32)
        dcs = _dot_hi(dcq_r[...] - dck_r[...], sel)
        utri = (_iota((tm, tm), 1) >= _iota((tm, tm), 0)).astype(F32)
        dlf = _dot_hi(utri, dcs) + carry[0:1, :]
        carry[0:1, :] = dlf[0:1, :]
        z = fa[...] + fb_r[...]
        dfa = jnp.where(lane < 8, dlf * _sigmoid(-z), 0.0)
        dp_o[:, FA:FA + 128] = dfa.astype(CDT)
        acc[4:5, 0:128] += jnp.sum(dfa, axis=0, keepdims=True)

        @pl.when(i == nt - 1)
        def _():
            foldm = ((_iota((512, 128), 0) & (HD - 1)) == _iota((512, 128), 1)).astype(F32)
            dgn_o[...] = _dot_hi(acc[...], foldm)

    def col(width, off):
        return pl.BlockSpec((tm, width), lambda i: (nt - 1 - i, off // width))

    def rows(width):
        return pl.BlockSpec((tm, width), lambda i: (nt - 1 - i, 0))

    def vec(width):
        return pl.BlockSpec((1, width), lambda i: (0, 0))

    pair = pl.BlockSpec((2, tm, 128), lambda i: (0, nt - 1 - i, 0))
    return pl.pallas_call(
        body, name=name, grid=(nt,),
        in_specs=[col(512, QA), col(512, KA), col(512, QB), col(128, KB), col(128, FA),
                  rows(512), rows(512), rows(512), rows(512), pair, pair, rows(512), rows(512), rows(D), rows(D),
                  vec(512), vec(512), vec(512), vec(128), vec(128)],
        out_specs=[rows(DP), pl.BlockSpec((8, 128), lambda i: (0, 0))],
        out_shape=[_sds((t, DP), CDT), _sds((8, 128), F32)],
        scratch_shapes=[pltpu.VMEM((8, 128), F32), pltpu.VMEM((8, 512), F32)],
        compiler_params=_params(("arbitrary",)),
    )(proj, proj, proj, proj, proj, dqf, dkf, dvf, dqs, dkse, dvse, dcq, dck, dga, dgb, gfq, gfk, gsq, gsk, fb)


def _gate_fwd(ofox, oswa, wbf, wbs, proj, name):
    t = ofox.shape[0]
    tm = _row_tile(t)
    tn = 512

    def body(of_r, os_r, wf_r, ws_r, ga_r, gb_r, y_o, yt_o, pf_o, ps_o, oft_o, ost_o):
        j = pl.program_id(1)
        pf = _dot(of_r[...], wf_r[...])
        ps = _dot(os_r[...], ws_r[...])
        y = _sigmoid(ga_r[...]) * pf + _sigmoid(gb_r[...]) * ps
        y_o[...] = y.astype(CDT)
        yt_o[...] = y.T.astype(CDT)
        pf_o[...] = pf.astype(CDT)
        ps_o[...] = ps.astype(CDT)

        @pl.when(j == 0)
        def _():
            oft_o[...] = of_r[...].astype(F32).T.astype(CDT)
            ost_o[...] = os_r[...].astype(F32).T.astype(CDT)

    tile = pl.BlockSpec((tm, tn), lambda i, j: (i, j))
    return pl.pallas_call(
        body, name=name, grid=(t // tm, D // tn),
        in_specs=[pl.BlockSpec((tm, 512), lambda i, j: (i, 0)), pl.BlockSpec((tm, 512), lambda i, j: (i, 0)),
                  pl.BlockSpec((512, tn), lambda i, j: (0, j)), pl.BlockSpec((512, tn), lambda i, j: (0, j)),
                  pl.BlockSpec((tm, tn), lambda i, j: (i, GA // tn + j)),
                  pl.BlockSpec((tm, tn), lambda i, j: (i, GB // tn + j))],
        out_specs=[tile, pl.BlockSpec((tn, tm), lambda i, j: (j, i)), tile, tile,
                   pl.BlockSpec((512, tm), lambda i, j: (0, i)), pl.BlockSpec((512, tm), lambda i, j: (0, i))],
        out_shape=[_sds((t, D), CDT), _sds((D, t), CDT), _sds((t, D), CDT), _sds((t, D), CDT),
                   _sds((512, t), CDT), _sds((512, t), CDT)],
        compiler_params=_params(("parallel", "arbitrary")),
    )(ofox, oswa, wbf, wbs, proj, proj)


def _gate_bwd(dy, pf, ps, proj, name):
    t = dy.shape[0]
    tm = _row_tile(t)
    tn = 512

    def body(dy_r, pf_r, ps_r, ga_r, gb_r, dpf_o, dps_o, dga_o, dgb_o):
        dy_ = dy_r[...]
        sa = _sigmoid(ga_r[...])
        sb = _sigmoid(gb_r[...])
        dpf_o[...] = (dy_ * sa).astype(CDT)
        dps_o[...] = (dy_ * sb).astype(CDT)
        dga_o[...] = (dy_ * pf_r[...].astype(F32) * (sa * (1.0 - sa))).astype(CDT)
        dgb_o[...] = (dy_ * ps_r[...].astype(F32) * (sb * (1.0 - sb))).astype(CDT)

    tile = pl.BlockSpec((tm, tn), lambda i, j: (i, j))
    return pl.pallas_call(
        body, name=name, grid=(t // tm, D // tn),
        in_specs=[tile, tile, tile,
                  pl.BlockSpec((tm, tn), lambda i, j: (i, GA // tn + j)),
                  pl.BlockSpec((tm, tn), lambda i, j: (i, GB // tn + j))],
        out_specs=[tile] * 4,
        out_shape=[_sds((t, D), CDT)] * 4,
        compiler_params=_params(("parallel", "parallel")),
    )(dy, pf, ps, proj, proj)


def _tri_steps(n, by_key):
    if by_key:
        pairs = [(i, j) for j in range(n) for i in range(j, n)]
    else:
        pairs = [(i, j) for i in range(n) for j in range(i + 1)]
    return (np.array([p[0] for p in pairs], np.int32), np.array([p[1] for p in pairs], np.int32))


def _head_col(blk, lane, h):
    return jnp.sum(jnp.where(lane == h, blk, 0.0), axis=1, keepdims=True)


def _head_row(blk, sub, h):
    return jnp.sum(jnp.where(sub == h, blk, 0.0), axis=0, keepdims=True)


def _ride_specs(ride):
    if ride is None:
        return [], [], [], [], []
    kind, srcs, outs, layer, items = ride
    return list(srcs), [ANY] * len(srcs), list(outs), [ANY] * len(outs), _dma_sems(3 * len(srcs))


def _ride_start(ride, srcs, dsts, send_sems, recv_sems):
    for cp in _ici_copies(ride[0], srcs, dsts, send_sems, recv_sems, ride[3], recv=False, items=ride[4])[0]:
        cp.start()


def _ride_wait(ride, srcs, dsts, send_sems, recv_sems):
    sends, recvs = _ici_copies(ride[0], srcs, dsts, send_sems, recv_sems, ride[3], items=ride[4])
    for cp in recvs:
        cp.wait_recv()
    for cp in sends:
        cp.wait_send()


def _riding(body, n_in, n_out, ride, grid):
    if ride is None:
        return body
    nr = len(ride[1])

    def wrapped(*refs):
        ins, srcs = refs[:n_in], refs[n_in:n_in + nr]
        outs, dsts = refs[n_in + nr:n_in + nr + n_out], refs[n_in + nr + n_out:n_in + 2 * nr + n_out]
        scratch, sems = refs[n_in + 2 * nr + n_out:-2], refs[-2:]
        first = pl.program_id(0) == 0
        last = pl.program_id(0) == grid[0] - 1
        for a in range(1, len(grid)):
            first = first & (pl.program_id(a) == 0)
            last = last & (pl.program_id(a) == grid[a] - 1)

        @pl.when(first)
        def _():
            _ride_start(ride, srcs, dsts, *sems)

        body(*ins, *outs, *scratch)

        @pl.when(last)
        def _():
            _ride_wait(ride, srcs, dsts, *sems)

    return wrapped


def _fox_fwd(qf, kf, vf, c, ct, name, ride=None):
    t = qf.shape[0]
    ta = _row_tile(t)
    qi, kj = _tri_steps(t // ta, by_key=False)
    nsteps = len(qi)
    ride_in, ride_in_specs, ride_out, ride_out_specs, ride_sems = _ride_specs(ride)

    def body(qi_r, kj_r, q_r, k_r, v_r, c_r, ct_r, *rest):
        nr = len(ride_in)
        src_r, (o_o, lse_o), dst_o = rest[:nr], rest[nr:nr + 2], rest[nr + 2:2 * nr + 2]
        m_sc, l_sc, acc_sc, cq_sc, *sems = rest[2 * nr + 2:]
        p = pl.program_id(0)
        n = pl.program_id(1)
        i = qi_r[n]
        j = kj_r[n]
        lane = _iota((1, 128), 1)
        lo = lane < HD

        if ride is not None:
            @pl.when((p == 0) & (n == 0))
            def _():
                _ride_start(ride, src_r, dst_o, *sems)

        @pl.when(j == 0)
        def _():
            m_sc[...] = jnp.full(m_sc.shape, NEG, F32)
            l_sc[...] = jnp.zeros(l_sc.shape, F32)
            acc_sc[...] = jnp.zeros(acc_sc.shape, F32)
            for e in (0, 1):
                cq_sc[e] = jnp.broadcast_to(_head_col(c_r[...], lane, 2 * p + e), (ta, 128))

        def step(masked):
            q = q_r[...]
            k = k_r[...]
            vaug = jnp.concatenate([v_r[...], jnp.ones((ta, 128), CDT)], axis=1)
            if masked:
                rows = i * ta + _iota((ta, 1), 0)
                cols = j * ta + _iota((1, ta), 1)
                mask = (cols <= rows) & (cols >= PAD)
            sub = _iota((8, 1), 0)
            alphas, pvs = [], []
            for e in (0, 1):
                sel = lo if e == 0 else jnp.logical_not(lo)
                s = _dot_nt(jnp.where(sel, q, 0), k)
                ck = _head_row(ct_r[...], sub, 2 * p + e)
                cq = cq_sc[e]
                chunks = []
                for ch in range(ta // 128):
                    sl = slice(128 * ch, 128 * (ch + 1))
                    sc = s[:, sl] + cq - ck[:, sl]
                    if masked:
                        sc = jnp.where(mask[:, sl], sc, NEG)
                    chunks.append(sc)
                mx = chunks[0]
                for sc in chunks[1:]:
                    mx = jnp.maximum(mx, sc)
                m_prev = m_sc[e]
                m_new = jnp.maximum(m_prev, jnp.max(mx, axis=1, keepdims=True))
                alpha = jnp.exp(m_prev - m_new)
                pe = jnp.concatenate([jnp.exp(sc - m_new).astype(CDT) for sc in chunks], axis=1)
                pva = _dot(pe, vaug)
                l_sc[e] = alpha * l_sc[e] + pva[:, 128:]
                m_sc[e] = m_new
                alphas.append(alpha)
                pvs.append(pva[:, :128])
            acc_sc[...] = acc_sc[...] * jnp.where(lo, alphas[0], alphas[1]) + jnp.where(lo, pvs[0], pvs[1])

        edge = (j == i) | (j == 0)

        @pl.when(edge)
        def _():
            step(True)

        @pl.when(jnp.logical_not(edge))
        def _():
            step(False)

        @pl.when(j == i)
        def _():
            l = jnp.where(lo, l_sc[0], l_sc[1])
            o_o[...] = (acc_sc[...] / l).astype(CDT)
            lse_o[...] = jnp.where(lo, m_sc[0], m_sc[1]) + jnp.log(l)

        if ride is not None:
            @pl.when((p == NPAIR - 1) & (n == nsteps - 1))
            def _():
                _ride_wait(ride, src_r, dst_o, *sems)

    qblk = pl.BlockSpec((ta, 128), lambda p, n, qi_r, kj_r: (qi_r[n], p))
    kblk = pl.BlockSpec((ta, 128), lambda p, n, qi_r, kj_r: (kj_r[n], p))
    grid_spec = pltpu.PrefetchScalarGridSpec(
        num_scalar_prefetch=2, grid=(NPAIR, nsteps),
        in_specs=[qblk, kblk, kblk,
                  pl.BlockSpec((ta, 128), lambda p, n, qi_r, kj_r: (qi_r[n], 0)),
                  pl.BlockSpec((8, ta), lambda p, n, qi_r, kj_r: (0, kj_r[n]))] + ride_in_specs,
        out_specs=[qblk, qblk] + ride_out_specs,
        scratch_shapes=[pltpu.VMEM((2, ta, 128), F32), pltpu.VMEM((2, ta, 128), F32), pltpu.VMEM((ta, 128), F32),
                        pltpu.VMEM((2, ta, 128), F32)] + ride_sems,
    )
    return pl.pallas_call(
        body, name=name, grid_spec=grid_spec,
        out_shape=[_sds((t, 512), CDT), _sds((t, 512), F32)] + ride_out,
        compiler_params=_params(("arbitrary", "arbitrary")),
    )(jnp.asarray(qi), jnp.asarray(kj), qf, kf, vf, c, ct, *ride_in)


def _fox_bwd(qf, qft, kf, vf, c, ct, o, lse, do, dot, name, ride=None):
    t = qf.shape[0]
    ta = _row_tile(t)
    nq = t // ta
    qi, kj = _tri_steps(nq, by_key=False)
    nsteps = len(qi)
    ride_in, ride_in_specs, ride_out, ride_out_specs, ride_sems = _ride_specs(ride)

    def body(qi_r, kj_r, q_r, qt_r, k_r, v_r, c_r, ct_r, o_r, lse_r, do_r, dot_r, *rest):
        nr = len(ride_in)
        src_r, (dq_o, dcq_o, dk_o, dv_o, dck_o), dst_o = rest[:nr], rest[nr:nr + 5], rest[nr + 5:2 * nr + 5]
        lse_sc, dl_sc, cq_sc, dq_sc, dcq_sc, dkt_sc, dvt_sc, dckt_sc, *sems = rest[2 * nr + 5:]
        p = pl.program_id(0)
        n = pl.program_id(1)
        i = qi_r[n]
        j = kj_r[n]
        lane = _iota((1, 128), 1)
        lo = lane < HD
        top = _iota((128, 1), 0) < HD

        if ride is not None:
            @pl.when((p == 0) & (n == 0))
            def _():
                _ride_start(ride, src_r, dst_o, *sems)

        @pl.when(n == 0)
        def _():
            dkt_sc[...] = jnp.zeros(dkt_sc.shape, F32)
            dvt_sc[...] = jnp.zeros(dvt_sc.shape, F32)
            dckt_sc[...] = jnp.zeros(dckt_sc.shape, F32)

        @pl.when(j == 0)
        def _():
            dq_sc[...] = jnp.zeros(dq_sc.shape, F32)
            dcq_sc[...] = jnp.zeros(dcq_sc.shape, F32)
            dd = do_r[...] * o_r[...].astype(F32)
            lse = lse_r[...]
            for e in (0, 1):
                sel = lo if e == 0 else jnp.logical_not(lo)
                cq_sc[e] = jnp.broadcast_to(_head_col(c_r[...], lane, 2 * p + e), (ta, 128))
                dl_sc[e] = jnp.broadcast_to(jnp.sum(jnp.where(sel, dd, 0.0), axis=1, keepdims=True), (ta, 128))
                lse_sc[e] = jnp.broadcast_to(lse[:, HD * e:HD * e + 1], (ta, 128))

        def step(masked):
            q = q_r[...]
            qt = qt_r[...]
            k = k_r[...]
            v = v_r[...]
            dob = do_r[...].astype(CDT)
            dot_ = dot_r[...]
            ones = jnp.ones((ta, 128), CDT)
            ones16 = jnp.ones((16, ta), CDT)
            if masked:
                rows = i * ta + _iota((ta, 1), 0)
                cols = j * ta + _iota((1, ta), 1)
                mask = (cols <= rows) & (cols >= PAD)
            sub = _iota((8, 1), 0)
            for e in (0, 1):
                sel = lo if e == 0 else jnp.logical_not(lo)
                rsel = top if e == 0 else jnp.logical_not(top)
                s = _dot_nt(jnp.where(sel, q, 0), k)
                dp = _dot_nt(jnp.where(sel, dob, 0), v)
                ck = _head_row(ct_r[...], sub, 2 * p + e)
                cq, lse_e, dl = cq_sc[e], lse_sc[e], dl_sc[e]
                prs, dss = [], []
                for ch in range(ta // 128):
                    sl = slice(128 * ch, 128 * (ch + 1))
                    sc = s[:, sl] + cq - ck[:, sl]
                    if masked:
                        sc = jnp.where(mask[:, sl], sc, NEG)
                    pr = jnp.exp(sc - lse_e)
                    prs.append(pr.astype(CDT))
                    dss.append((pr * (dp[:, sl] - dl)).astype(CDT))
                pb = jnp.concatenate(prs, axis=1)
                dsb = jnp.concatenate(dss, axis=1)
                dvt_sc[j] += _dot(jnp.where(rsel, dot_, 0), pb)
                dkc = _dot(jnp.concatenate([jnp.where(rsel, qt, 0), ones16], axis=0), dsb)
                dkt_sc[j] += dkc[0:128]
                dckt_sc[j, 0:8, :] += jnp.where(sub == e, dkc[128:136], 0.0)
                dqa = _dot(dsb, jnp.concatenate([jnp.where(sel, k, 0), ones], axis=1))
                dq_sc[...] += dqa[:, :128]
                dcq_sc[e] += dqa[:, 128:]

        edge = (j == i) | (j == 0)

        @pl.when(edge)
        def _():
            step(True)

        @pl.when(jnp.logical_not(edge))
        def _():
            step(False)

        @pl.when(j == i)
        def _():
            dq_o[...] = dq_sc[...]
            dcq_o[...] = jnp.where(lo, dcq_sc[0], dcq_sc[1])

        @pl.when(n == nsteps - 1)
        def _():
            spread = (_iota((128, 128), 1) == _iota((128, 128), 0) // HD).astype(F32)
            for jb in range(nq):
                rs = slice(jb * ta, (jb + 1) * ta)
                dk_o[rs, :] = dkt_sc[jb].T
                dv_o[rs, :] = dvt_sc[jb].T
                dck_o[rs, :] = _dot_hi(spread, dckt_sc[jb]).T

        if ride is not None:
            @pl.when((p == NPAIR - 1) & (n == nsteps - 1))
            def _():
                _ride_wait(ride, src_r, dst_o, *sems)

    qblk = pl.BlockSpec((ta, 128), lambda p, n, qi_r, kj_r: (qi_r[n], p))
    qtblk = pl.BlockSpec((128, ta), lambda p, n, qi_r, kj_r: (p, qi_r[n]))
    kblk = pl.BlockSpec((ta, 128), lambda p, n, qi_r, kj_r: (kj_r[n], p))
    whole = pl.BlockSpec((t, 128), lambda p, n, qi_r, kj_r: (0, p))
    grid_spec = pltpu.PrefetchScalarGridSpec(
        num_scalar_prefetch=2, grid=(NPAIR, nsteps),
        in_specs=[qblk, qtblk, kblk, kblk,
                  pl.BlockSpec((ta, 128), lambda p, n, qi_r, kj_r: (qi_r[n], 0)),
                  pl.BlockSpec((8, ta), lambda p, n, qi_r, kj_r: (0, kj_r[n])),
                  qblk, qblk, qblk, qtblk] + ride_in_specs,
        out_specs=[qblk, qblk, whole, whole, whole] + ride_out_specs,
        scratch_shapes=[pltpu.VMEM((2, ta, 128), F32)] * 3 + [pltpu.VMEM((ta, 128), F32), pltpu.VMEM((2, ta, 128), F32)]
        + [pltpu.VMEM((nq, 128, ta), F32)] * 3 + ride_sems,
    )
    return pl.pallas_call(
        body, name=name, grid_spec=grid_spec,
        out_shape=[_sds((t, 512), F32)] * 5 + ride_out,
        compiler_params=_params(("arbitrary", "arbitrary")),
    )(jnp.asarray(qi), jnp.asarray(kj), qf, qft, kf, vf, c, ct, o, lse, do, dot, *ride_in)


def _bucket_table():
    r = np.arange(BLK)[:, None]
    c = np.arange(3 * BLK)[None, :]
    d = np.where(c < BLK, r + BLK - c, r - (c - BLK))
    n = np.maximum(d, 0)
    max_exact = N_BUCKETS // 2
    nf = np.maximum(n, 1).astype(np.float32)
    large = max_exact + (np.log(nf / max_exact) / math.log(BLK / max_exact) * (N_BUCKETS - max_exact)).astype(np.int32)
    large = np.minimum(large, N_BUCKETS - 1)
    b = np.where(n < max_exact, n, large)
    return np.where(c < 2 * BLK, b, N_BUCKETS - 1).astype(np.int32)


def _bias_fwd(table, name):
    bucket = jnp.asarray(_bucket_table())

    def body(tab_r, b_r, o_o):
        h = pl.program_id(0)
        b = b_r[...]
        acc = jnp.zeros(b.shape, F32)
        for k in range(N_BUCKETS):
            acc = jnp.where(b == k, tab_r[k, h], acc)
        o_o[...] = acc

    return pl.pallas_call(
        body, name=name, grid=(8,),
        in_specs=[pl.BlockSpec(memory_space=pltpu.SMEM), pl.BlockSpec((BLK, 3 * BLK), lambda h: (0, 0))],
        out_specs=pl.BlockSpec((None, BLK, 3 * BLK), lambda h: (h, 0, 0)),
        out_shape=_sds((8, BLK, 3 * BLK), F32),
        compiler_params=_params(("parallel",)),
    )(table, bucket)


def _bias_bwd(dbias, name):
    bucket = jnp.asarray(_bucket_table())

    def body(d_r, b_r, o_o):
        h = pl.program_id(0)
        b = b_r[...]
        d = d_r[...]
        lane = _iota((1, 128), 1)
        row = jnp.zeros((1, 128), F32)
        for k in range(N_BUCKETS):
            row = jnp.where(lane == k, jnp.sum(jnp.where(b == k, d, 0.0)), row)
        o_o[pl.ds(h, 1), :] = row

    return pl.pallas_call(
        body, name=name, grid=(8,),
        in_specs=[pl.BlockSpec((None, BLK, 3 * BLK), lambda h: (h, 0, 0)), pl.BlockSpec((BLK, 3 * BLK), lambda h: (0, 0))],
        out_specs=pl.BlockSpec((8, 128), lambda h: (0, 0)),
        out_shape=_sds((8, 128), F32),
        compiler_params=_params(("arbitrary",)),
    )(dbias, bucket)


def _swa_valid(i):
    r = _iota((BLK, 1), 0)
    c = _iota((1, 3 * BLK), 1)
    prev = (c < BLK) & (c > r) & (i >= 1) & ((i - 1) * BLK + c >= PAD)
    cc = c - BLK
    cur = (c >= BLK) & (c < 2 * BLK) & (cc <= r) & (i * BLK + cc >= PAD)
    cm = c - 2 * BLK
    meta = (c >= 2 * BLK) & (cm >= PAD) & (i * BLK + r - cm >= BLK)
    return prev | cur | meta


def _swa_kv_specs(ta):
    nb = ta // BLK
    return [pl.BlockSpec((None, BLK, 128), lambda p, i: (p // 2, jnp.maximum(i * nb - 1, 0), 0)),
            pl.BlockSpec((None, ta, 128), lambda p, i: (p // 2, i, 0)),
            pl.BlockSpec((None, BLK, 128), lambda p, i: (p // 2, 0, 0))]


def _swa_fwd(qs, kse, vse, bias, sinks, name, ride=None):
    t = qs.shape[0]
    ta = _row_tile(t)
    nb = ta // BLK
    grid = (NPAIR, t // ta)
    ride_in, ride_in_specs, ride_out, ride_out_specs, ride_sems = _ride_specs(ride)

    def body(sink_r, q_r, kp_r, kc_r, km_r, vp_r, vc_r, vm_r, b_r, o_o, lse_o):
        p = pl.program_id(0)
        i = pl.program_id(1)
        lo = _iota((1, 128), 1) < HD
        k4 = jnp.concatenate([kp_r[...], kc_r[...]], axis=0)
        v4 = jnp.concatenate([vp_r[...], vc_r[...]], axis=0)
        for b in range(nb):
            rows = slice(BLK * b, BLK * (b + 1))
            q = q_r[rows, :]
            k3 = jnp.concatenate([k4[BLK * b:BLK * (b + 2)], km_r[...]], axis=0)
            v3 = jnp.concatenate([v4[BLK * b:BLK * (b + 2)], vm_r[...]], axis=0)
            valid = _swa_valid(i * nb + b)
            outs, lses = [], []
            for e in (0, 1):
                sel = lo if e == 0 else jnp.logical_not(lo)
                s = _dot_nt(jnp.where(sel, q, 0), k3) + b_r[e]
                s = jnp.where(valid, s, NEG)
                sink = sink_r[2 * p + e]
                mx = jnp.maximum(jnp.max(s, axis=1, keepdims=True), sink)
                pe = jnp.exp(s - mx)
                den = jnp.sum(pe, axis=1, keepdims=True) + jnp.exp(sink - mx)
                outs.append(_dot(pe.astype(CDT), v3) / den)
                lses.append(mx + jnp.log(den))
            o_o[rows, :] = jnp.where(lo, outs[0], outs[1]).astype(CDT)
            lse_o[rows, :] = jnp.where(lo, lses[0], lses[1])

    qblk = pl.BlockSpec((ta, 128), lambda p, i: (i, p))
    res = pl.pallas_call(
        _riding(body, 9, 2, ride, grid), name=name, grid=grid,
        in_specs=[pl.BlockSpec(memory_space=pltpu.SMEM), qblk] + _swa_kv_specs(ta) + _swa_kv_specs(ta)
        + [pl.BlockSpec((2, BLK, 3 * BLK), lambda p, i: (p, 0, 0))] + ride_in_specs,
        out_specs=[qblk, qblk] + ride_out_specs,
        out_shape=[_sds((t, 512), CDT), _sds((t, 512), F32)] + ride_out, scratch_shapes=ride_sems,
        compiler_params=_params(("arbitrary", "arbitrary") if ride else ("parallel", "parallel")),
    )(sinks, qs, kse, kse, kse, vse, vse, vse, bias, *ride_in)
    return (res[0], res[1], res[2:]) if ride else res


def _swa_bwd(qs, kse, vse, bias, sinks, o, lse, do, name):
    t = qs.shape[0]
    ta = _row_tile(t)
    nb = ta // BLK

    def body(sink_r, q_r, kp_r, kc_r, km_r, vp_r, vc_r, vm_r, b_r, o_r, lse_r, do_r,
             dq_o, dk_o, dv_o, db_o, dsk_o):
        p = pl.program_id(0)
        i = pl.program_id(1)
        lo = _iota((1, 128), 1) < HD

        @pl.when((i == 0) & (p % 2 == 0))
        def _():
            dk_o[...] = jnp.zeros(dk_o.shape, F32)
            dv_o[...] = jnp.zeros(dv_o.shape, F32)

        @pl.when(i == 0)
        def _():
            db_o[...] = jnp.zeros(db_o.shape, F32)
            dsk_o[...] = jnp.zeros(dsk_o.shape, F32)

        k4 = jnp.concatenate([kp_r[...], kc_r[...]], axis=0)
        v4 = jnp.concatenate([vp_r[...], vc_r[...]], axis=0)
        for b in range(nb):
            ib = i * nb + b
            rows = slice(BLK * b, BLK * (b + 1))
            q = q_r[rows, :]
            do_ = do_r[rows, :]
            dd = do_ * o_r[rows, :].astype(F32)
            lse = lse_r[rows, :]
            k3 = jnp.concatenate([k4[BLK * b:BLK * (b + 2)], km_r[...]], axis=0)
            v3 = jnp.concatenate([v4[BLK * b:BLK * (b + 2)], vm_r[...]], axis=0)
            valid = _swa_valid(ib)
            dq = jnp.zeros((BLK, 128), F32)
            dk3 = jnp.zeros((3 * BLK, 128), F32)
            dv3 = jnp.zeros((3 * BLK, 128), F32)
            dsink = []
            for e in (0, 1):
                sel = lo if e == 0 else jnp.logical_not(lo)
                qe = jnp.where(sel, q, 0)
                doe = jnp.where(sel, do_, 0.0).astype(CDT)
                lse_e = lse[:, HD * e:HD * e + 1]
                s = _dot_nt(qe, k3) + b_r[e]
                s = jnp.where(valid, s, NEG)
                pr = jnp.exp(s - lse_e)
                delta = jnp.sum(jnp.where(sel, dd, 0.0), axis=1, keepdims=True)
                ds = pr * (_dot_nt(doe, v3) - delta)
                db_o[e] += ds
                dsink.append(-jnp.sum(jnp.exp(sink_r[2 * p + e] - lse_e) * delta, axis=0, keepdims=True))
                dq = dq + _dot(ds.astype(CDT), jnp.where(sel, k3, 0))
                dk3 = dk3 + _dot(ds.T.astype(CDT), qe)
                dv3 = dv3 + _dot(pr.T.astype(CDT), doe)
            dq_o[rows, :] = dq
            prev = pl.ds(pl.multiple_of(jnp.maximum(ib - 1, 0) * BLK, BLK), BLK)
            cur = pl.ds(pl.multiple_of(ib * BLK, BLK), BLK)
            dk_o[prev, :] += dk3[0:BLK]
            dk_o[cur, :] += dk3[BLK:2 * BLK]
            dk_o[0:BLK, :] += dk3[2 * BLK:]
            dv_o[prev, :] += dv3[0:BLK]
            dv_o[cur, :] += dv3[BLK:2 * BLK]
            dv_o[0:BLK, :] += dv3[2 * BLK:]
            dsk_o[0:1, :] += jnp.where(lo, dsink[0], dsink[1])

    qblk = pl.BlockSpec((ta, 128), lambda p, i: (i, p))
    kvacc = pl.BlockSpec((None, t, 128), lambda p, i: (p // 2, 0, 0))
    bblk = pl.BlockSpec((2, BLK, 3 * BLK), lambda p, i: (p, 0, 0))
    return pl.pallas_call(
        body, name=name, grid=(NPAIR, t // ta),
        in_specs=[pl.BlockSpec(memory_space=pltpu.SMEM), qblk] + _swa_kv_specs(ta) + _swa_kv_specs(ta)
        + [bblk, qblk, qblk, qblk],
        out_specs=[qblk, kvacc, kvacc, bblk, pl.BlockSpec((None, 8, 128), lambda p, i: (p, 0, 0))],
        out_shape=[_sds((t, 512), F32), _sds((2, t, 128), F32), _sds((2, t, 128), F32),
                   _sds((8, BLK, 3 * BLK), F32), _sds((NPAIR, 8, 128), F32)],
        compiler_params=_params(("arbitrary", "arbitrary")),
    )(sinks, qs, kse, kse, kse, vse, vse, vse, bias, o, lse, do)


def _sum8(slots, name):
    def body(a_r, o_o):
        acc = a_r[0]
        for k in range(1, 8):
            acc = acc + a_r[k]
        o_o[...] = acc

    return pl.pallas_call(
        body, name=name, out_shape=_sds((SMALL_ROWS, 128), F32),
        in_specs=[pl.BlockSpec(memory_space=pltpu.VMEM)], out_specs=pl.BlockSpec(memory_space=pltpu.VMEM),
        compiler_params=_params(),
    )(slots)


def _place():
    x, y, c = lax.axis_index("x"), lax.axis_index("y"), lax.axis_index("c")
    chips = [(1 - x, y), (x, 1 - y), (1 - x, 1 - y)]
    return x, y, c, chips


def _remote(src, dst, send_sems, recv_sems, k, to):
    return pltpu.make_async_remote_copy(src_ref=src, dst_ref=dst, send_sem=send_sems.at[k], recv_sem=recv_sems.at[k],
                                        device_id=to, device_id_type=MESH_ID)


ANY = pl.BlockSpec(memory_space=pl.ANY)


def _mix_cols(w):
    return jnp.concatenate([w[:, 2312:4360], w[:, 0:1536], w[:, 1544:2312], w[:, 1536:1544],
                            jnp.zeros((w.shape[0], DP - D_IN), w.dtype)], axis=1)


def _unmix_cols(w):
    return jnp.concatenate([w[:, QA:QA + 1536], w[:, FA:FA + 8], w[:, QB:QB + 768], w[:, GA:GA + 2048]], axis=1)


def _rows128(a, rows):
    flat = a.reshape(-1)
    return jnp.pad(flat, (0, rows * 128 - flat.shape[0])).reshape(rows, 128)


GRAD_FORM = {"ffn1_w_in": "col", "ffn2_w_in": "col", "w_branch_fox": "col", "w_branch_swa": "col",
             "ffn1_w_out": "3d", "ffn2_w_out": "3d", "w_out": "3d", "w_in": "3d"}
SUM_TILE = {1024: 128, 704: 176, 512: 128, 256: 128}
NT = len(SHARD_ITEMS)
ALL_ITEMS = tuple(range(NT))


def _half_rows(c, r):
    return pl.ds(pl.multiple_of(c * (r // 2), 16), r // 2)


def _ici_copies(kind, srcs, dsts, send_sems, recv_sems, layer, recv=True, items=ALL_ITEMS):
    x, y, c, chips = _place()
    s = 2 * x + y
    sends, recvs = [], []
    for t, (item, src, dst) in enumerate(zip(items, srcs, dsts)):
        nm, (r, cc), _ = SHARD_ITEMS[item]
        for j, (cx, cy) in enumerate(chips):
            sj = 2 * cx + cy
            k = 3 * t + j
            to = (cx, cy, c)
            if kind == "gather":
                hs = _half_rows(c, r)
                sends.append(_remote(src.at[layer, hs], dst.at[s, hs], send_sems, recv_sems, k, to))
                if recv:
                    recvs.append(_remote(src.at[layer, hs], dst.at[sj, hs], send_sems, recv_sems, k, to))
            else:
                if GRAD_FORM[nm] == "col":
                    piece = src.at[:, pl.ds(pl.multiple_of(sj * cc, 128), cc)]
                else:
                    piece = src.at[sj]
                sends.append(_remote(piece, dst.at[j], send_sems, recv_sems, k, to))
                recvs.append(sends[-1])
    return sends, recvs


def _slab_shapes(items=ALL_ITEMS):
    return [_sds((4, *SHARD_ITEMS[t][1]), CDT) for t in items]


def _dma_sems(n):
    return [pltpu.SemaphoreType.DMA((n,)), pltpu.SemaphoreType.DMA((n,))]


def _forward_sends(dsts, send_sems, recv_sems, items=ALL_ITEMS):
    x, y, c, chips = _place()
    sends, recvs = [], []
    for t, (item, dst) in enumerate(zip(items, dsts)):
        r = SHARD_ITEMS[item][1][0]
        for j, (cx, cy) in enumerate(chips):
            sj = 2 * cx + cy
            hs, ho = _half_rows(c, r), _half_rows(1 - c, r)
            sends.append(_remote(dst.at[sj, hs], dst.at[sj, hs], send_sems, recv_sems, 3 * t + j, (x, y, 1 - c)))
            recvs.append(_remote(dst.at[sj, ho], dst.at[sj, ho], send_sems, recv_sems, 3 * t + j, (x, y, 1 - c)))
    return sends, recvs


def _gather_layer(wb, mflat, layer, name, items):
    nt = len(items)

    def body(*refs):
        srcs, m_r, dsts, mall_o = refs[:nt], refs[nt], refs[nt + 1:2 * nt + 1], refs[2 * nt + 1]
        send_sems, recv_sems, fsend, frecv, msend, mrecv = refs[2 * nt + 2:]
        x, y, c, chips = _place()
        s = 2 * x + y
        sends, recvs = _ici_copies("gather", srcs, dsts, send_sems, recv_sems, layer, items=items)
        metas = [_remote(m_r, mall_o.at[s], msend, mrecv, j, (cx, cy, c)) for j, (cx, cy) in enumerate(chips)]
        for cp in sends + metas:
            cp.start()
        fwds, frecvs = _forward_sends(dsts, fsend, frecv, items)
        for got, fwd in zip(recvs, fwds):
            got.wait_recv()
            fwd.start()
        for got in frecvs:
            got.wait_recv()
        for j, (cx, cy) in enumerate(chips):
            _remote(m_r, mall_o.at[2 * cx + cy], msend, mrecv, j, (cx, cy, c)).wait_recv()
        for cp in sends + metas + fwds:
            cp.wait_send()

    return pl.pallas_call(
        body, name=name, out_shape=_slab_shapes(items) + [_sds((4, META_ROWS, 128), F32)],
        in_specs=[ANY] * (nt + 1), out_specs=[ANY] * (nt + 1),
        scratch_shapes=_dma_sems(3 * nt) + _dma_sems(3 * nt) + _dma_sems(3),
    )(*wb, mflat)


def _forward_layer(slabs, name, items=ALL_ITEMS):
    nt = len(items)

    def body(*refs):
        ins, outs, send_sems, recv_sems = refs[:nt], refs[nt:2 * nt], refs[2 * nt], refs[2 * nt + 1]
        sends, recvs = _forward_sends(outs, send_sems, recv_sems, items)
        for cp in sends:
            cp.start()
        for cp in recvs:
            cp.wait_recv()
        for cp in sends:
            cp.wait_send()

    return pl.pallas_call(
        body, name=name, out_shape=_slab_shapes(items), in_specs=[ANY] * nt, out_specs=[ANY] * nt,
        input_output_aliases={t: t for t in range(nt)}, scratch_shapes=_dma_sems(3 * nt),
    )(*slabs)


def _half_shape(nm, r, c):
    return (r // 2, 4 * c) if GRAD_FORM[nm] == "col" else (4, r // 2, c)


def _swap_layer(gs, gsm, name, items=ALL_ITEMS):
    small = gsm is not None
    nt = len(items)

    def body(*refs):
        g_rs = refs[:nt]
        pos = nt
        if small:
            s_r = refs[pos]
            pos += 1
        got_os = refs[pos:pos + nt]
        pos += nt
        if small:
            slots_o = refs[pos]
            pos += 1
        send_sems, recv_sems = refs[pos], refs[pos + 1]
        x, y, c, _ = _place()
        sib = (x, y, 1 - c)
        sent = []
        for t, (item, g_r, got_o) in enumerate(zip(items, g_rs, got_os)):
            nm, (r, cc), _ = SHARD_ITEMS[item]
            ho = _half_rows(1 - c, r)
            src = g_r.at[ho, :] if GRAD_FORM[nm] == "col" else g_r.at[:, ho, :]
            sent.append(_remote(src, got_o, send_sems, recv_sems, t, sib))
        if small:
            ssend, srecv, loc_sem = refs[pos + 2], refs[pos + 3], refs[pos + 4]
            me = 4 * x + 2 * y + c
            loc = pltpu.make_async_copy(s_r, slots_o.at[me], loc_sem.at[0])
            loc.start()
            peers = [(x ^ (k >> 2), y ^ ((k >> 1) & 1), c ^ (k & 1)) for k in range(1, 8)]
            for k, peer in enumerate(peers):
                sent.append(_remote(s_r, slots_o.at[me], ssend, srecv, k, peer))
        for cp in sent:
            cp.start()
        for cp in sent[:nt]:
            cp.wait_recv()
        if small:
            for k, (px, py, pc) in enumerate(peers):
                _remote(s_r, slots_o.at[4 * px + 2 * py + pc], ssend, srecv, k, (px, py, pc)).wait_recv()
        for cp in sent:
            cp.wait_send()
        if small:
            loc.wait()

    outs = [_sds(_half_shape(*SHARD_ITEMS[item][0:1], *SHARD_ITEMS[item][1]), CDT) for item in items]
    ops = list(gs)
    sems = _dma_sems(nt)
    if small:
        outs.append(_sds((8, SMALL_ROWS, 128), F32))
        ops.append(gsm)
        sems = sems + _dma_sems(7) + [pltpu.SemaphoreType.DMA((1,))]
    res = pl.pallas_call(
        body, name=name, out_shape=outs, in_specs=[ANY] * len(ops), out_specs=[ANY] * len(outs), scratch_shapes=sems,
    )(*ops)
    return (res[:nt], res[nt]) if small else (res, None)


def _pair_add_t(own, got, half_idx, nm, r, name):
    tr = SUM_TILE[r]
    nb = (r // 2) // tr
    if GRAD_FORM[nm] == "col":
        blk = (tr, own.shape[1])
        own_spec = pl.BlockSpec(blk, lambda i, c_r: (c_r[0] * nb + i, 0))
        half_spec = pl.BlockSpec(blk, lambda i, c_r: (i, 0))
    else:
        blk = (4, tr, own.shape[2])
        own_spec = pl.BlockSpec(blk, lambda i, c_r: (0, c_r[0] * nb + i, 0))
        half_spec = pl.BlockSpec(blk, lambda i, c_r: (0, i, 0))

    def body(c_r, a_r, b_r, o_o):
        o_o[...] = (a_r[...].astype(F32) + b_r[...].astype(F32)).astype(CDT)

    grid_spec = pltpu.PrefetchScalarGridSpec(num_scalar_prefetch=1, grid=(nb,), in_specs=[own_spec, half_spec],
                                             out_specs=half_spec)
    return pl.pallas_call(body, name=name, grid_spec=grid_spec, out_shape=_sds(got.shape, CDT),
                          compiler_params=_params(("parallel",)))(half_idx, own, got)


def _sum4_t(ps, got3, buf, idx, layer, nm, r, name):
    tr = SUM_TILE[r]
    nb = (r // 2) // tr
    c = got3.shape[2]
    if GRAD_FORM[nm] == "col":
        ps_spec = pl.BlockSpec((tr, c), lambda i, x_r: (i, x_r[0]))
    else:
        ps_spec = pl.BlockSpec((None, tr, c), lambda i, x_r: (x_r[0], i, 0))

    def body(x_r, a_r, b_r, buf_r, o_o):
        o_o[...] = ((a_r[...].astype(F32) + b_r[0].astype(F32)) + b_r[1].astype(F32)) + b_r[2].astype(F32)

    grid_spec = pltpu.PrefetchScalarGridSpec(
        num_scalar_prefetch=1, grid=(nb,),
        in_specs=[ps_spec, pl.BlockSpec((3, tr, c), lambda i, x_r: (0, i, 0)), ANY],
        out_specs=pl.BlockSpec((None, tr, c), lambda i, x_r: (layer, x_r[1] * nb + i, 0)),
    )
    return pl.pallas_call(body, name=name, grid_spec=grid_spec, out_shape=_sds(buf.shape, F32),
                          input_output_aliases={3: 0}, compiler_params=_params(("parallel",)))(idx, ps, got3, buf)


def _scatter_layer(ps, name, items=ALL_ITEMS):
    nt = len(items)

    def body(*refs):
        srcs, dsts, send_sems, recv_sems = refs[:nt], refs[nt:2 * nt], refs[2 * nt], refs[2 * nt + 1]
        sends, recvs = _ici_copies("scatter", srcs, dsts, send_sems, recv_sems, None, items=items)
        for cp in sends:
            cp.start()
        for cp in recvs:
            cp.wait_recv()
        for cp in sends:
            cp.wait_send()

    return pl.pallas_call(
        body, name=name, out_shape=_got3_shapes(items), in_specs=[ANY] * nt, out_specs=[ANY] * nt,
        scratch_shapes=_dma_sems(3 * nt),
    )(*ps)


def _got3_shapes(items=ALL_ITEMS):
    return [_sds((3, SHARD_ITEMS[t][1][0] // 2, SHARD_ITEMS[t][1][1]), CDT) for t in items]


def _join_layer(bufs, name):
    def body(*refs):
        ins, outs, send_sems, recv_sems = refs[:NT], refs[NT:2 * NT], refs[2 * NT], refs[2 * NT + 1]
        x, y, c, _ = _place()
        sent = []
        for t, ((nm, (r, cc), _), b_o) in enumerate(zip(SHARD_ITEMS, outs)):
            hs = _half_rows(c, r)
            sent.append(_remote(b_o.at[:, hs, :], b_o.at[:, hs, :], send_sems, recv_sems, t, (x, y, 1 - c)))
        for cp in sent:
            cp.start()
        for t, ((nm, (r, cc), _), b_o) in enumerate(zip(SHARD_ITEMS, outs)):
            ho = _half_rows(1 - c, r)
            _remote(b_o.at[:, ho, :], b_o.at[:, ho, :], send_sems, recv_sems, t, (x, y, 1 - c)).wait_recv()
        for cp in sent:
            cp.wait_send()

    return pl.pallas_call(
        body, name=name, out_shape=[_sds(b.shape, F32) for b in bufs], in_specs=[ANY] * NT, out_specs=[ANY] * NT,
        input_output_aliases={t: t for t in range(NT)}, scratch_shapes=_dma_sems(NT),
    )(*bufs)


def _adamw3(w, g, m, v, name):
    nl, r, c = w.shape
    tr = SUM_TILE.get(r, r)

    def body(w_r, g_r, m_r, v_r, d_o, m_o, v_o):
        g_ = g_r[...]
        m_ = ADAM_B1 * m_r[...] + (1.0 - ADAM_B1) * g_
        v_ = ADAM_B2 * v_r[...] + (1.0 - ADAM_B2) * jnp.square(g_)
        m_hat = m_ / (1.0 - ADAM_B1 ** ADAM_STEP)
        v_hat = v_ / (1.0 - ADAM_B2 ** ADAM_STEP)
        d_o[...] = -ADAM_LR * (m_hat / (jnp.sqrt(v_hat) + ADAM_EPS) + ADAM_WD * w_r[...])
        m_o[...] = m_
        v_o[...] = v_

    blk = pl.BlockSpec((None, tr, c), lambda l, i: (l, i, 0))
    return pl.pallas_call(
        body, name=name, grid=(nl, r // tr),
        in_specs=[blk] * 4, out_specs=[blk] * 3, out_shape=[_sds((nl, r, c), F32)] * 3,
        compiler_params=_params(("parallel", "parallel")),
    )(w, g, m, v)


def _full_weights(slabs, wb, layer, shard, items=ALL_ITEMS):
    ws = {}
    for t, slab in zip(items, slabs):
        nm, (r, c), kind = SHARD_ITEMS[t]
        slab = lax.dynamic_update_slice(slab, wb[nm][layer][None], (shard, 0, 0))
        ws[nm] = slab.reshape(4 * r, c) if kind == "row" else jnp.concatenate([slab[s] for s in range(4)], axis=1)
    return ws


def _exchange_forms(g, items=ALL_ITEMS):
    out = []
    for t in items:
        nm, (r, c), _ = SHARD_ITEMS[t]
        a = g[nm]
        if nm == "w_in":
            a = a.reshape(D, 4, c).transpose(1, 0, 2)
        elif GRAD_FORM[nm] == "3d":
            a = a.reshape(4, r, c)
        out.append(a)
    return out


SMALL_ITEMS = (("rel_bias_table", 2), ("ffn1_norm", 16), ("mix_norm", 16), ("ffn2_norm", 16), ("forget_bias", 1),
               ("fox_q_norm", 1), ("fox_k_norm", 1), ("swa_q_norm", 1), ("swa_k_norm", 1), ("swa_sinks", 1))
SMALL_ADAM_ROWS = 96


def _layer_fwd(h, lw, l, ride=None, late=None):
    rides = late["rides"] if late else {}

    def run(key, fn, *args):
        r = rides.get(key)
        if r is None:
            return fn(*args)
        out = fn(*args, ride=r)
        late["arrived"](key, out[-1])
        return out[0] if len(out) == 2 else out[:-1]

    sv = {"h0": h}
    a, sv["a1t"] = _rms_fwd(h, lw["ffn1_norm"], f"rms_fwd_a{l}")
    sv["gu1"], s, sv["s1t"] = run("ffn_in_a", _ffn_in, a, lw["ffn1_w_in"], f"ffn_in_a{l}")
    h = run("ffn_out_a", _mm_res, s, lw["ffn1_w_out"], h, 0.5, f"ffn_out_a{l}")
    sv["h1"] = h
    a, sv["amt"] = _rms_fwd(h, lw["mix_norm"], f"rms_fwd_m{l}")
    if late:
        late["need"](lw, "mixer")
    proj = run("proj", _mm, a, lw["w_mix"], F32, _row_tile(h.shape[0]), DP, f"proj{l}")
    sv["proj"] = proj
    qf, kf, vf, qs, kse, vse, c, ct, sv["qft"] = _qknorm_fwd(proj, lw["gfq"], lw["gfk"], lw["gsq"], lw["gsk"], lw["fb"],
                                                              f"qknorm_fwd{l}")
    ofox, lse_f, *rode = _fox_fwd(qf, kf, vf, c, ct, f"fox_fwd{l}", ride)
    oswa, lse_s = run("swa_fwd", _swa_fwd, qs, kse, vse, lw["bias"], lw["sinks"], f"swa_fwd{l}")
    if late:
        late["need"](lw, "gate")
    sv.update(qf=qf, kf=kf, vf=vf, qs=qs, kse=kse, vse=vse, c=c, ct=ct, ofox=ofox, oswa=oswa, lse_f=lse_f, lse_s=lse_s)
    y, sv["yt"], sv["pf"], sv["ps"], sv["oft"], sv["ost"] = _gate_fwd(ofox, oswa, lw["w_branch_fox"], lw["w_branch_swa"],
                                                                     proj, f"gate_fwd{l}")
    h = _mm_res(y, lw["w_out"], h, 1.0, f"mix_out{l}")
    sv["h2"] = h
    a, sv["a2t"] = _rms_fwd(h, lw["ffn2_norm"], f"rms_fwd_b{l}")
    sv["gu2"], s, sv["s2t"] = _ffn_in(a, lw["ffn2_w_in"], f"ffn_in_b{l}")
    h = _mm_res(s, lw["ffn2_w_out"], h, 0.5, f"ffn_out_b{l}")
    return h, sv, rode


def _ffn_bwd(dh, dhb, h_in, at, gu, st, norm, w_in, w_out, tag, rides=None):
    r = rides or (None,) * 4
    rode = []

    def split(res, ride):
        if ride is None:
            return res
        rode.extend(res[-1])
        return res[0] if len(res) == 2 else res[:-1]

    dgu = split(_ffn_bwd_mid(dhb, w_out, gu, f"ffn_bwd_mid_{tag}", r[0]), r[0])
    d_w_out = split(_mm(st, dhb, CDT, 256, D, f"dw_ffn_out_{tag}", scale=0.5, ride=r[1]), r[1])
    dh, dhb, dg = split(_ffn_bwd_in(dgu, w_in, h_in, norm, dh, f"ffn_bwd_in_{tag}", r[2]), r[2])
    d_w_in = split(_mm(at, dgu, CDT, D, 256, f"dw_ffn_in_{tag}", ride=r[3]), r[3])
    return dh, dhb, d_w_out, d_w_in, dg, rode


def _layer_bwd(dh, dhb, sv, lw, l, ride=None, before_ffn1=None):
    g = {}
    dh, dhb, g["ffn2_w_out"], g["ffn2_w_in"], g["ffn2_norm"], _ = _ffn_bwd(
        dh, dhb, sv["h2"], sv["a2t"], sv["gu2"], sv["s2t"], lw["ffn2_norm"], lw["ffn2_w_in"], lw["ffn2_w_out"], f"b{l}")
    dy = _mm_nt(dhb, lw["w_out"], f"d_y{l}")
    g["w_out"] = _mm(sv["yt"], dhb, CDT, 512, 512, f"dw_out{l}")
    dpf, dps, dga, dgb = _gate_bwd(dy, sv["pf"], sv["ps"], sv["proj"], f"gate_bwd{l}")
    do_f, do_ft = _mm_nt(dpf, lw["w_branch_fox"], f"d_ofox{l}", with_t=True)
    do_s = _mm_nt(dps, lw["w_branch_swa"], f"d_oswa{l}")
    g["w_branch_fox"] = _mm(sv["oft"], dpf, CDT, 512, 512, f"dw_bfox{l}")
    g["w_branch_swa"] = _mm(sv["ost"], dps, CDT, 512, 512, f"dw_bswa{l}")
    dqf, dcq, dkf, dvf, dck, *rode = _fox_bwd(sv["qf"], sv["qft"], sv["kf"], sv["vf"], sv["c"], sv["ct"], sv["ofox"],
                                              sv["lse_f"], do_f, do_ft, f"fox_bwd{l}", ride)
    g["rode"] = rode
    dqs, dkse, dvse, dbias, dsk = _swa_bwd(sv["qs"], sv["kse"], sv["vse"], lw["bias"], lw["sinks"], sv["oswa"],
                                           sv["lse_s"], do_s, f"swa_bwd{l}")
    dproj, dgn = _qknorm_bwd(sv["proj"], dqf, dkf, dvf, dqs, dkse, dvse, dcq, dck, dga, dgb,
                             lw["gfq"], lw["gfk"], lw["gsq"], lw["gsk"], lw["fb"], f"qknorm_bwd{l}")
    g["w_mix"] = _mm(sv["amt"], dproj, CDT, 512, 640, f"dw_mix{l}")
    dh, dhb, g["mix_norm"] = _mm_nt_rms(dproj, lw["w_mix"], sv["h1"], lw["mix_norm"], dh, f"d_am{l}")
    g["dbias"], g["dsk"], g["dgn"] = dbias, dsk, dgn
    rides = before_ffn1(g) if before_ffn1 else None
    dh, dhb, g["ffn1_w_out"], g["ffn1_w_in"], g["ffn1_norm"], g["rode_ffn1"] = _ffn_bwd(
        dh, dhb, sv["h0"], sv["a1t"], sv["gu1"], sv["s1t"], lw["ffn1_norm"], lw["ffn1_w_in"], lw["ffn1_w_out"], f"a{l}",
        rides)
    return dh, dhb, g


def kernel(x, meta_tokens, rel_bias_table, ffn1_norm, ffn1_w_in, ffn1_w_out, mix_norm, w_in, forget_bias, fox_q_norm, fox_k_norm, swa_q_norm, swa_k_norm, swa_sinks, w_branch_fox, w_branch_swa, w_out, ffn2_norm, ffn2_w_in, ffn2_w_out, loss_target, m_meta_tokens, m_rel_bias_table, m_ffn1_norm, m_ffn1_w_in, m_ffn1_w_out, m_mix_norm, m_w_in, m_forget_bias, m_fox_q_norm, m_fox_k_norm, m_swa_q_norm, m_swa_k_norm, m_swa_sinks, m_w_branch_fox, m_w_branch_swa, m_w_out, m_ffn2_norm, m_ffn2_w_in, m_ffn2_w_out, v_meta_tokens, v_rel_bias_table, v_ffn1_norm, v_ffn1_w_in, v_ffn1_w_out, v_mix_norm, v_w_in, v_forget_bias, v_fox_q_norm, v_fox_k_norm, v_swa_q_norm, v_swa_k_norm, v_swa_sinks, v_w_branch_fox, v_w_branch_swa, v_w_out, v_ffn2_norm, v_ffn2_w_in, v_ffn2_w_out):
    names = ["meta_tokens", "rel_bias_table", "ffn1_norm", "ffn1_w_in", "ffn1_w_out", "mix_norm", "w_in", "forget_bias",
             "fox_q_norm", "fox_k_norm", "swa_q_norm", "swa_k_norm", "swa_sinks", "w_branch_fox", "w_branch_swa", "w_out",
             "ffn2_norm", "ffn2_w_in", "ffn2_w_out"]
    w = dict(zip(names, [meta_tokens, rel_bias_table, ffn1_norm, ffn1_w_in, ffn1_w_out, mix_norm, w_in, forget_bias,
                         fox_q_norm, fox_k_norm, swa_q_norm, swa_k_norm, swa_sinks, w_branch_fox, w_branch_swa, w_out,
                         ffn2_norm, ffn2_w_in, ffn2_w_out]))
    m = dict(zip(names, [m_meta_tokens, m_rel_bias_table, m_ffn1_norm, m_ffn1_w_in, m_ffn1_w_out, m_mix_norm, m_w_in,
                         m_forget_bias, m_fox_q_norm, m_fox_k_norm, m_swa_q_norm, m_swa_k_norm, m_swa_sinks,
                         m_w_branch_fox, m_w_branch_swa, m_w_out, m_ffn2_norm, m_ffn2_w_in, m_ffn2_w_out]))
    v = dict(zip(names, [v_meta_tokens, v_rel_bias_table, v_ffn1_norm, v_ffn1_w_in, v_ffn1_w_out, v_mix_norm, v_w_in,
                         v_forget_bias, v_fox_q_norm, v_fox_k_norm, v_swa_q_norm, v_swa_k_norm, v_swa_sinks,
                         v_w_branch_fox, v_w_branch_swa, v_w_out, v_ffn2_norm, v_ffn2_w_in, v_ffn2_w_out]))
    xi, yi, ci = lax.axis_index("x"), lax.axis_index("y"), lax.axis_index("c")
    shard = 2 * xi + yi
    seq = x.shape[1]
    t = seq + BLK

    wb = {nm: w[nm].astype(CDT) for nm, _, _ in SHARD_ITEMS}
    wb_list = [wb[nm] for nm, _, _ in SHARD_ITEMS]
    mflat = meta_tokens.reshape(META_ROWS, 128)
    first = (0, 1)
    *slabs_first, mall = _gather_layer([wb_list[t] for t in first], mflat, 0, "gather_weights", first)
    mall = lax.dynamic_update_slice(mall, mflat[None], (shard, 0, 0))
    meta_full = jnp.concatenate([mall[s].reshape(N_META, 256) for s in range(4)], axis=1)
    bias = _bias_fwd(rel_bias_table, "bias_fwd")

    def layer_weights(slabs, l, items=ALL_ITEMS):
        lw = _full_weights(slabs, wb, l, shard, items)
        if "w_in" in lw:
            lw["w_mix"] = _mix_cols(lw.pop("w_in"))
        return lw

    def layer_vectors(l):
        lw = {nm: w[nm][l].reshape(1, D) for nm in ("ffn1_norm", "mix_norm", "ffn2_norm")}
        lw["gfq"] = jnp.tile(fox_q_norm[l], 8).reshape(1, 512)
        lw["gfk"] = jnp.tile(fox_k_norm[l], 8).reshape(1, 512)
        lw["gsq"] = jnp.tile(swa_q_norm[l], 8).reshape(1, 512)
        lw["gsk"] = jnp.tile(swa_k_norm[l], 2).reshape(1, 128)
        lw["fb"] = jnp.pad(forget_bias[l], (0, 120)).reshape(1, 128)
        lw["sinks"] = swa_sinks[l]
        lw["bias"] = bias
        return lw

    def gather_ride(layer, items):
        return ("gather", [wb_list[t] for t in items], _slab_shapes(items), layer, items)

    landed = {}

    def need(lw, stage):
        if stage == "mixer":
            items = (2,)
            slabs = _forward_layer(landed["ffn_in_a"], "forward_halves0m", items)
        else:
            items = (3, 4, 5, 6, 7)
            slabs = _forward_layer(landed["ffn_out_a"] + landed["proj"] + landed["swa_fwd"], "forward_halves0g", items)
        lw.update(layer_weights(slabs, 0, items))

    late = {"rides": {"ffn_in_a": gather_ride(0, (2,)), "ffn_out_a": gather_ride(0, (3, 4, 5)),
                      "proj": gather_ride(0, (6,)), "swa_fwd": gather_ride(0, (7,))},
            "arrived": landed.__setitem__, "need": need}

    h = jnp.concatenate([jnp.zeros((PAD, D), F32), meta_full, x[0]], axis=0)
    lws = [{**layer_vectors(0), **layer_weights(slabs_first, 0, first)}]
    h, sv0, slabs1 = _layer_fwd(h, lws[0], 0, gather_ride(1, ALL_ITEMS), late)
    lws.append({**layer_vectors(1), **layer_weights(_forward_layer(slabs1, "forward_halves"), 1)})
    h, sv1, _ = _layer_fwd(h, lws[1], 1)
    saved = [sv0, sv1]
    dh, dhb, lacc = _loss(h, loss_target[0], "loss")
    loss = lax.psum(lacc[0, 0], ("x", "y", "c"))

    half_idx = ci.reshape(1).astype(jnp.int32)
    place_idx = jnp.stack([shard, ci]).astype(jnp.int32)

    def pair_sums(g, gsm, tag, items=ALL_ITEMS):
        if "w_mix" in g:
            g["w_in"] = _unmix_cols(g.pop("w_mix"))
        forms = _exchange_forms(g, items)
        got, slots = _swap_layer(forms, gsm, f"swap_halves{tag}", items)
        return {t: _pair_add_t(a, b, half_idx, SHARD_ITEMS[t][0], SHARD_ITEMS[t][1][0],
                               f"pair_add{tag}_{SHARD_ITEMS[t][0]}")
                for t, a, b in zip(items, forms, got)}, slots

    def scatter_ride(ps, items):
        return ("scatter", [ps[t] for t in items], _got3_shapes(items), None, items)

    early = (2, 3, 4, 5, 6, 7)
    early_rides = ((6,), (7,), (2, 5), (3, 4))
    ps0 = {}

    def before_ffn1(g):
        ps0.update(pair_sums(g, None, "0e", early)[0])
        return [scatter_ride(ps0, items) for items in early_rides]

    grads = [None, None]
    dh, dhb, grads[1] = _layer_bwd(dh, dhb, saved[1], lws[1], 1)
    ps1, _ = pair_sums(grads[1], None, 1)
    dh, dhb, grads[0] = _layer_bwd(dh, dhb, saved[0], lws[0], 0, scatter_ride(ps1, ALL_ITEMS), before_ffn1)
    grad_x = dh[BLK:].reshape(1, seq, D)
    dtab = _bias_bwd(grads[0]["dbias"] + grads[1]["dbias"], "bias_bwd")

    small = [dh[PAD:BLK].reshape(128, 128), _rows128(dtab[:, :N_BUCKETS].T, 2)]
    for nm in ("ffn1_norm", "mix_norm", "ffn2_norm"):
        small.append(jnp.stack([grads[l][nm][0] for l in range(2)]).reshape(16, 128))
    small.append(_rows128(jnp.stack([grads[l]["dgn"][4, :8] for l in range(2)]), 1))
    for row in range(4):
        small.append(jnp.stack([grads[l]["dgn"][row, :HD] for l in range(2)]).reshape(1, 128))
    dsk = [grads[l]["dsk"][:, 0, :] for l in range(2)]
    small.append(_rows128(jnp.stack([jnp.stack([d[:, 0], d[:, HD]], axis=1).reshape(8) for d in dsk]), 1))
    gsm = jnp.concatenate(small, axis=0)
    gsm = jnp.pad(gsm, ((0, SMALL_ROWS - gsm.shape[0]), (0, 0)))

    late = (0, 1)
    ps_late, slots = pair_sums(grads[0], gsm, "0l", late)
    ps0.update(ps_late)
    got3_0 = dict(zip([t for items in early_rides for t in items], grads[0]["rode_ffn1"]))
    got3_0.update(zip(late, _scatter_layer([ps0[t] for t in late], "scatter_shards", late)))
    got3 = [got3_0, dict(zip(ALL_ITEMS, grads[0]["rode"]))]
    bufs = []
    for t, (nm, (r, c), _) in enumerate(SHARD_ITEMS):
        buf = lax.empty((2, r, c), F32)
        for l, ps in ((1, ps1), (0, ps0)):
            buf = _sum4_t(ps[t], got3[l][t], buf, place_idx, l, nm, r, f"sum4_{l}_{nm}")
        bufs.append(buf)
    bufs = _join_layer(bufs, "join_halves")
    gs = _sum8(slots, "sum8")

    g_out = {nm: buf for (nm, _, _), buf in zip(SHARD_ITEMS, bufs)}
    g_out["meta_tokens"] = lax.dynamic_slice(gs[0:128].reshape(N_META, D), (0, shard * 256), (N_META, 256))
    off = 128
    for nm, rows in SMALL_ITEMS:
        n = w[nm].size
        g_out[nm] = gs[off:off + rows].reshape(-1)[:n].reshape(w[nm].shape)
        off += rows

    delta, new_m, new_v = {}, {}, {}
    for nm, _, _ in SHARD_ITEMS:
        delta[nm], new_m[nm], new_v[nm] = _adamw3(w[nm], g_out[nm], m[nm], v[nm], f"adamw_{nm}")
    small_names = ["meta_tokens"] + [nm for nm, _ in SMALL_ITEMS]
    small_rows = [META_ROWS] + [rows for _, rows in SMALL_ITEMS]

    def pack_small(src):
        buf = jnp.concatenate([_rows128(src[nm], rows) for nm, rows in zip(small_names, small_rows)], axis=0)
        return jnp.pad(buf, ((0, SMALL_ADAM_ROWS - buf.shape[0]), (0, 0)))

    d_, m_, v_ = (a[0] for a in _adamw3(pack_small(w)[None], pack_small(g_out)[None], pack_small(m)[None],
                                        pack_small(v)[None], "adamw_small"))
    off = 0
    for nm, rows in zip(small_names, small_rows):
        n = w[nm].size
        for dst, src in ((delta, d_), (new_m, m_), (new_v, v_)):
            dst[nm] = src[off:off + rows].reshape(-1)[:n].reshape(w[nm].shape)
        off += rows

    return (loss, grad_x, *[g_out[n] for n in names], *[delta[n] for n in names],
            *[new_m[n] for n in names], *[new_v[n] for n in names])
```

```python
import math

import numpy as np
import jax
import jax.numpy as jnp
from jax import lax
from jax.experimental import pallas as pl
from jax.experimental.pallas import tpu as pltpu

D = 1024
F = 2816
FT = F // 2
HD = 64
NPAIR = 4
N_META = 16
BLK = 128
PAD = BLK - N_META
EPS = 1e-6
NEG = -1e30
N_BUCKETS = 32
GA, GB, QA, KA, VA, QB, KB, VB, FA, DP = 0, 1024, 2048, 2560, 3072, 3584, 4096, 4224, 4352, 4480
D_IN = 4360
CDT = jnp.bfloat16
F32 = jnp.float32
VMEM_LIMIT = 48 * 1024 * 1024
MESH_ID = pl.DeviceIdType.MESH

ADAM_LR, ADAM_B1, ADAM_B2, ADAM_EPS, ADAM_WD, ADAM_STEP = 0.001, 0.9, 0.999, 1e-08, 0.01, 10

SHARD_ITEMS = (
    ("ffn1_w_in", (1024, 1408), "col"),
    ("ffn1_w_out", (704, 1024), "row"),
    ("w_in", (1024, 1090), "col"),
    ("w_branch_fox", (512, 256), "col"),
    ("w_branch_swa", (512, 256), "col"),
    ("w_out", (256, 1024), "row"),
    ("ffn2_w_in", (1024, 1408), "col"),
    ("ffn2_w_out", (704, 1024), "row"),
)
SMALL_ROWS = 192
META_ROWS = 32


def _row_tile(t):
    return 384 if t % 384 == 0 else 128


def _dot(a, b):
    return jnp.dot(a, b, preferred_element_type=F32)


def _dot_nt(a, b):
    return lax.dot_general(a, b, (((1,), (1,)), ((), ())), preferred_element_type=F32)


def _dot_hi(a, b):
    return jnp.dot(a, b, preferred_element_type=F32, precision=lax.Precision.HIGHEST)


def _sigmoid(x):
    return 1.0 / (1.0 + jnp.exp(-x))


def _iota(shape, dim):
    return lax.broadcasted_iota(jnp.int32, shape, dim)


def _params(sem=None):
    return pltpu.CompilerParams(dimension_semantics=sem, vmem_limit_bytes=VMEM_LIMIT)


def _sds(shape, dtype):
    return jax.ShapeDtypeStruct(shape, dtype)


def _rms_fwd(h, g, name):
    t = h.shape[0]
    tm = _row_tile(t)

    def body(h_ref, g_ref, a_ref, at_ref):
        x = h_ref[...]
        ms = jnp.mean(x * x, axis=-1, keepdims=True)
        a = x * lax.rsqrt(ms + EPS) * g_ref[...]
        a_ref[...] = a.astype(CDT)
        at_ref[...] = a.T.astype(CDT)

    return pl.pallas_call(
        body, name=name, grid=(t // tm,),
        in_specs=[pl.BlockSpec((tm, D), lambda i: (i, 0)), pl.BlockSpec((1, D), lambda i: (0, 0))],
        out_specs=[pl.BlockSpec((tm, D), lambda i: (i, 0)), pl.BlockSpec((D, tm), lambda i: (0, i))],
        out_shape=[_sds((t, D), CDT), _sds((D, t), CDT)],
        compiler_params=_params(("parallel",)),
    )(h, g)


def _ffn_in(a, w_in, name, ride=None):
    t = a.shape[0]
    tm = _row_tile(t)
    tn = FT
    nj = F // tn
    grid = (nj, t // tm)
    ride_in, ride_in_specs, ride_out, ride_out_specs, ride_sems = _ride_specs(ride)

    def body(a_ref, wg_ref, wu_ref, gu_ref, s_ref, st_ref):
        a_ = a_ref[...]
        g = _dot(a_, wg_ref[...])
        u = _dot(a_, wu_ref[...])
        s = g * _sigmoid(g) * u
        gu_ref[0] = g.astype(CDT)
        gu_ref[1] = u.astype(CDT)
        s_ref[...] = s.astype(CDT)
        st_ref[...] = s.T.astype(CDT)

    res = pl.pallas_call(
        _riding(body, 3, 3, ride, grid), name=name, grid=grid,
        in_specs=[pl.BlockSpec((tm, D), lambda j, i: (i, 0)),
                  pl.BlockSpec((D, tn), lambda j, i: (0, j)),
                  pl.BlockSpec((D, tn), lambda j, i: (0, j + nj))] + ride_in_specs,
        out_specs=[pl.BlockSpec((2, tm, tn), lambda j, i: (0, i, j)),
                   pl.BlockSpec((tm, tn), lambda j, i: (i, j)),
                   pl.BlockSpec((tn, tm), lambda j, i: (j, i))] + ride_out_specs,
        out_shape=[_sds((2, t, F), CDT), _sds((t, F), CDT), _sds((F, t), CDT)] + ride_out, scratch_shapes=ride_sems,
        compiler_params=_params(("arbitrary", "arbitrary") if ride else ("parallel", "parallel")),
    )(a, w_in, w_in, *ride_in)
    return (*res[:3], res[3:]) if ride else res


def _mm_res(a, b, res, scale, name, ride=None):
    t, k = a.shape
    n = b.shape[1]
    tm = _row_tile(t)
    tn = 512
    grid = (t // tm, n // tn)
    ride_in, ride_in_specs, ride_out, ride_out_specs, ride_sems = _ride_specs(ride)

    def body(a_ref, b_ref, r_ref, o_ref):
        o_ref[...] = r_ref[...] + scale * _dot(a_ref[...], b_ref[...])

    out = pl.pallas_call(
        _riding(body, 3, 1, ride, grid), name=name, grid=grid,
        in_specs=[pl.BlockSpec((tm, k), lambda i, j: (i, 0)),
                  pl.BlockSpec((k, tn), lambda i, j: (0, j)),
                  pl.BlockSpec((tm, tn), lambda i, j: (i, j))] + ride_in_specs,
        out_specs=[pl.BlockSpec((tm, tn), lambda i, j: (i, j))] + ride_out_specs,
        out_shape=[_sds((t, n), F32)] + ride_out, scratch_shapes=ride_sems,
        compiler_params=_params(("arbitrary", "arbitrary") if ride else ("parallel", "parallel")),
    )(a, b, res, *ride_in)
    return (out[0], out[1:]) if ride else out[0]


def _mm(a, b, out_dtype, tm, tn, name, scale=1.0, ride=None):
    m, k = a.shape
    if b.ndim == 3:
        nh = b.shape[2] // tn
        n = 2 * b.shape[2]
        b_spec = pl.BlockSpec((None, k, tn), lambda i, j: (j // nh, 0, j % nh))
    else:
        n = b.shape[1]
        b_spec = pl.BlockSpec((k, tn), lambda i, j: (0, j))
    grid = (m // tm, n // tn)
    ride_in, ride_in_specs, ride_out, ride_out_specs, ride_sems = _ride_specs(ride)

    def body(a_ref, b_ref, o_ref):
        o_ref[...] = (scale * _dot(a_ref[...], b_ref[...])).astype(out_dtype)

    res = pl.pallas_call(
        _riding(body, 2, 1, ride, grid), name=name, grid=grid,
        in_specs=[pl.BlockSpec((tm, k), lambda i, j: (i, 0)), b_spec] + ride_in_specs,
        out_specs=[pl.BlockSpec((tm, tn), lambda i, j: (i, j))] + ride_out_specs,
        out_shape=[_sds((m, n), out_dtype)] + ride_out, scratch_shapes=ride_sems,
        compiler_params=_params(("arbitrary", "arbitrary") if ride else ("parallel", "parallel")),
    )(a, b, *ride_in)
    return (res[0], res[1:]) if ride else res[0]


def _mm_nt(a, b, name, with_t=False):
    m, n = a.shape
    k = b.shape[0]
    tm = _row_tile(m)
    tk = 512

    def body(a_ref, b_ref, o_ref, *t_ref):
        r = _dot_nt(a_ref[...], b_ref[...])
        o_ref[...] = r
        if with_t:
            t_ref[0][...] = r.T.astype(CDT)

    out_specs = [pl.BlockSpec((tm, tk), lambda i, j: (i, j))]
    out_shape = [_sds((m, k), F32)]
    if with_t:
        out_specs.append(pl.BlockSpec((tk, tm), lambda i, j: (j, i)))
        out_shape.append(_sds((k, m), CDT))
    res = pl.pallas_call(
        body, name=name, grid=(m // tm, k // tk),
        in_specs=[pl.BlockSpec((tm, n), lambda i, j: (i, 0)), pl.BlockSpec((tk, n), lambda i, j: (j, 0))],
        out_specs=out_specs, out_shape=out_shape,
        compiler_params=_params(("parallel", "parallel")),
    )(a, b)
    return res if with_t else res[0]


def _ffn_bwd_mid(dhb, w_out, gu, name, ride=None):
    t = dhb.shape[0]
    tm = _row_tile(t)
    tn = FT
    grid = (F // tn, t // tm)
    ride_in, ride_in_specs, ride_out, ride_out_specs, ride_sems = _ride_specs(ride)

    def body(dh_ref, w_ref, gu_ref, o_ref):
        ds = 0.5 * _dot_nt(dh_ref[...], w_ref[...])
        g = gu_ref[0].astype(F32)
        u = gu_ref[1].astype(F32)
        sg = _sigmoid(g)
        o_ref[0] = (ds * u * (sg * (1.0 + g * (1.0 - sg)))).astype(CDT)
        o_ref[1] = (ds * (g * sg)).astype(CDT)

    res = pl.pallas_call(
        _riding(body, 3, 1, ride, grid), name=name, grid=grid,
        in_specs=[pl.BlockSpec((tm, D), lambda j, i: (i, 0)),
                  pl.BlockSpec((tn, D), lambda j, i: (j, 0)),
                  pl.BlockSpec((2, tm, tn), lambda j, i: (0, i, j))] + ride_in_specs,
        out_specs=[pl.BlockSpec((2, tm, tn), lambda j, i: (0, i, j))] + ride_out_specs,
        out_shape=[_sds((2, t, F), CDT)] + ride_out, scratch_shapes=ride_sems,
        compiler_params=_params(("arbitrary", "arbitrary") if ride else ("parallel", "parallel")),
    )(dhb, w_out, gu, *ride_in)
    return (res[0], res[1:]) if ride else res[0]


def _rms_bwd_rows(da_, x, g, dres, i, dh_ref, dhb_ref, dg_ref):
    r = lax.rsqrt(jnp.mean(x * x, axis=-1, keepdims=True) + EPS)
    xh = x * r
    day = da_ * g
    dh = dres + r * (day - xh * jnp.mean(day * xh, axis=-1, keepdims=True))
    dh_ref[...] = dh
    dhb_ref[...] = dh.astype(CDT)

    @pl.when(i == 0)
    def _():
        dg_ref[...] = jnp.zeros(dg_ref.shape, F32)

    dg_ref[0:1, :] += jnp.sum(da_ * xh, axis=0, keepdims=True)


def _ffn_bwd_in(dgu, w_in, h, g, dres, name, ride=None):
    t = dgu.shape[1]
    tm = _row_tile(t)
    grid = (t // tm,)
    ride_in, ride_in_specs, ride_out, ride_out_specs, ride_sems = _ride_specs(ride)

    def body(dg_ref, wg_ref, wu_ref, h_ref, g_ref, dr_ref, dh_ref, dhb_ref, dgn_ref):
        da_ = _dot_nt(dg_ref[0], wg_ref[...]) + _dot_nt(dg_ref[1], wu_ref[...])
        _rms_bwd_rows(da_, h_ref[...], g_ref[...], dr_ref[...], pl.program_id(0), dh_ref, dhb_ref, dgn_ref)

    row = pl.BlockSpec((tm, D), lambda i: (i, 0))
    res = pl.pallas_call(
        _riding(body, 6, 3, ride, grid), name=name, grid=grid,
        in_specs=[pl.BlockSpec((2, tm, F), lambda i: (0, i, 0)),
                  pl.BlockSpec((D, F), lambda i: (0, 0)),
                  pl.BlockSpec((D, F), lambda i: (0, 1)),
                  row, pl.BlockSpec((1, D), lambda i: (0, 0)), row] + ride_in_specs,
        out_specs=[row, row, pl.BlockSpec((8, D), lambda i: (0, 0))] + ride_out_specs,
        out_shape=[_sds((t, D), F32), _sds((t, D), CDT), _sds((8, D), F32)] + ride_out, scratch_shapes=ride_sems,
        compiler_params=_params(("arbitrary",)),
    )(dgu, w_in, w_in, h, g, dres, *ride_in)
    return (*res[:3], res[3:]) if ride else res


def _mm_nt_rms(a, b, h, g, dres, name):
    t, n = a.shape
    tm = _row_tile(t)

    def body(a_ref, b_ref, h_ref, g_ref, dr_ref, dh_ref, dhb_ref, dgn_ref):
        da_ = _dot_nt(a_ref[...], b_ref[...])
        _rms_bwd_rows(da_, h_ref[...], g_ref[...], dr_ref[...], pl.program_id(0), dh_ref, dhb_ref, dgn_ref)

    row = pl.BlockSpec((tm, D), lambda i: (i, 0))
    return pl.pallas_call(
        body, name=name, grid=(t // tm,),
        in_specs=[pl.BlockSpec((tm, n), lambda i: (i, 0)), pl.BlockSpec((D, n), lambda i: (0, 0)),
                  row, pl.BlockSpec((1, D), lambda i: (0, 0)), row],
        out_specs=[row, row, pl.BlockSpec((8, D), lambda i: (0, 0))],
        out_shape=[_sds((t, D), F32), _sds((t, D), CDT), _sds((8, D), F32)],
        compiler_params=_params(("arbitrary",)),
    )(a, b, h, g, dres)


def _loss(h, target, name):
    t = h.shape[0]

    def body(h_ref, t_ref, dh_ref, dhb_ref, l_ref):
        i = pl.program_id(0)

        @pl.when(i == 0)
        def _():
            l_ref[...] = jnp.zeros(l_ref.shape, F32)
            dh_ref[...] = jnp.zeros(dh_ref.shape, F32)
            dhb_ref[...] = jnp.zeros(dhb_ref.shape, CDT)

        @pl.when(i > 0)
        def _():
            err = h_ref[...] - t_ref[...]
            l_ref[...] += (0.5 / D) * jnp.sum(err * err)
            d = err * (1.0 / D)
            dh_ref[...] = d
            dhb_ref[...] = d.astype(CDT)

    row = pl.BlockSpec((BLK, D), lambda i: (i, 0))
    return pl.pallas_call(
        body, name=name, grid=(t // BLK,),
        in_specs=[row, pl.BlockSpec((BLK, D), lambda i: (jnp.maximum(i - 1, 0), 0))],
        out_specs=[row, row, pl.BlockSpec((8, 128), lambda i: (0, 0))],
        out_shape=[_sds((t, D), F32), _sds((t, D), CDT), _sds((8, 128), F32)],
        compiler_params=_params(("arbitrary",)),
    )(h, target)


def _block_diag():
    return (_iota((128, 128), 0) // HD == _iota((128, 128), 1) // HD).astype(F32)


def _head_sums(v, bd):
    hi = v.astype(CDT)
    rest = (v - hi.astype(F32)).astype(CDT)
    b = bd.astype(CDT)
    return _dot(hi, b) + _dot(rest, b)


def _dup_halves(x, lo):
    sw = pltpu.roll(x, 64, 1)
    return jnp.where(lo, x, sw), jnp.where(lo, sw, x)


def _qknorm_fwd(proj, gfq, gfk, gsq, gsk, fb, name):
    t = proj.shape[0]
    tm = _row_tile(t)

    def body(qa, ka, va, qb, kb, vb, fa, gfq_r, gfk_r, gsq_r, gsk_r, fb_r,
             qf_o, kf_o, vf_o, qs_o, kse_o, vse_o, c_o, ct_o, qft_o, carry):
        i = pl.program_id(0)
        bd = _block_diag()
        lane = _iota((1, 128), 1)
        lo = lane < HD

        def hnorm(x, g):
            ms = _head_sums(x * x, bd) * (1.0 / HD)
            return x * lax.rsqrt(ms + EPS) * g

        for ch in range(4):
            sl = slice(128 * ch, 128 * (ch + 1))
            qn = hnorm(qa[:, sl], gfq_r[:, sl]) * 0.125
            qf_o[:, sl] = qn.astype(CDT)
            qft_o[sl, :] = qn.T.astype(CDT)
            kf_o[:, sl] = hnorm(ka[:, sl], gfk_r[:, sl]).astype(CDT)
            qs_o[:, sl] = (hnorm(qb[:, sl], gsq_r[:, sl]) * 0.125).astype(CDT)
        vf_o[...] = va[...].astype(CDT)
        k0, k1 = _dup_halves(hnorm(kb[...], gsk_r[...]), lo)
        kse_o[0] = k0.astype(CDT)
        kse_o[1] = k1.astype(CDT)
        v0, v1 = _dup_halves(vb[...], lo)
        vse_o[0] = v0.astype(CDT)
        vse_o[1] = v1.astype(CDT)

        z = fa[...] + fb_r[...]
        lf = jnp.minimum(z, 0.0) - jnp.log(1.0 + jnp.exp(-jnp.abs(z)))
        lf = jnp.where(lane < 8, lf, 0.0)
        ltri = (_iota((tm, tm), 1) <= _iota((tm, tm), 0)).astype(F32)

        @pl.when(i == 0)
        def _():
            carry[...] = jnp.zeros(carry.shape, F32)

        c = _dot_hi(ltri, lf) + carry[0:1, :]
        carry[0:1, :] = c[tm - 1:tm, :]
        c_o[...] = c
        ct_o[...] = c.T[0:8, :]

    def col(width, off):
        return pl.BlockSpec((tm, width), lambda i: (i, off // width))

    def vec(width):
        return pl.BlockSpec((1, width), lambda i: (0, 0))

    return pl.pallas_call(
        body, name=name, grid=(t // tm,),
        in_specs=[col(512, QA), col(512, KA), col(512, VA), col(512, QB), col(128, KB), col(128, VB), col(128, FA),
                  vec(512), vec(512), vec(512), vec(128), vec(128)],
        out_specs=[pl.BlockSpec((tm, 512), lambda i: (i, 0))] * 4
        + [pl.BlockSpec((2, tm, 128), lambda i: (0, i, 0))] * 2
        + [pl.BlockSpec((tm, 128), lambda i: (i, 0)), pl.BlockSpec((8, tm), lambda i: (0, i)),
           pl.BlockSpec((512, tm), lambda i: (0, i))],
        out_shape=[_sds((t, 512), CDT)] * 4 + [_sds((2, t, 128), CDT)] * 2
        + [_sds((t, 128), F32), _sds((8, t), F32), _sds((512, t), CDT)],
        scratch_shapes=[pltpu.VMEM((8, 128), F32)],
        compiler_params=_params(("arbitrary",)),
    )(proj, proj, proj, proj, proj, proj, proj, gfq, gfk, gsq, gsk, fb)


def _qknorm_bwd(proj, dqf, dkf, dvf, dqs, dkse, dvse, dcq, dck, dga, dgb, gfq, gfk, gsq, gsk, fb, name):
    t = proj.shape[0]
    tm = _row_tile(t)
    nt = t // tm

    def body(qa, ka, qb, kb, fa, dqf_r, dkf_r, dvf_r, dqs_r, dkse_r, dvse_r, dcq_r, dck_r, dga_r, dgb_r,
             gfq_r, gfk_r, gsq_r, gsk_r, fb_r, dp_o, dgn_o, carry, acc):
        i = pl.program_id(0)
        bd = _block_diag()
        lane = _iota((1, 128), 1)
        lo = lane < HD

        @pl.when(i == 0)
        def _():
            carry[...] = jnp.zeros(carry.shape, F32)
            acc[...] = jnp.zeros(acc.shape, F32)

        def hnorm_bwd(x, g, dy):
            r = lax.rsqrt(_head_sums(x * x, bd) * (1.0 / HD) + EPS)
            xh = x * r
            day = dy * g
            dx = r * (day - xh * (_head_sums(day * xh, bd) * (1.0 / HD)))
            return dx, jnp.sum(dy * xh, axis=0, keepdims=True)

        for ch in range(4):
            sl = slice(128 * ch, 128 * (ch + 1))
            dx, dg = hnorm_bwd(qa[:, sl], gfq_r[:, sl], dqf_r[:, sl] * 0.125)
            dp_o[:, QA + 128 * ch:QA + 128 * (ch + 1)] = dx.astype(CDT)
            acc[0:1, sl] += dg
            dx, dg = hnorm_bwd(ka[:, sl], gfk_r[:, sl], dkf_r[:, sl])
            dp_o[:, KA + 128 * ch:KA + 128 * (ch + 1)] = dx.astype(CDT)
            acc[1:2, sl] += dg
            dx, dg = hnorm_bwd(qb[:, sl], gsq_r[:, sl], dqs_r[:, sl] * 0.125)
            dp_o[:, QB + 128 * ch:QB + 128 * (ch + 1)] = dx.astype(CDT)
            acc[2:3, sl] += dg
        dp_o[:, VA:VA + 512] = dvf_r[...].astype(CDT)
        dp_o[:, GA:GA + D] = dga_r[...]
        dp_o[:, GB:GB + D] = dgb_r[...]

        def fold(x):
            e0 = x[0]
            e1 = x[1]
            return jnp.where(lo, e0 + pltpu.roll(e0, 64, 1), e1 + pltpu.roll(e1, 64, 1))

        dx, dg = hnorm_bwd(kb[...], gsk_r[...], fold(dkse_r))
        dp_o[:, KB:KB + 128] = dx.astype(CDT)
        acc[3:4, 0:128] += dg
        dp_o[:, VB:VB + 128] = fold(dvse_r).astype(CDT)

        rr = _iota((512, 128), 0)
        hh = _iota((512, 128), 1)
        sel = ((rr == (hh >> 1) * 128 + (hh & 1) * HD) & (hh < 8)).astype(F32)
        dcs = _dot_hi(dcq_r[...] - dck_r[...], sel)
        utri = (_iota((tm, tm), 1) >= _iota((tm, tm), 0)).astype(F32)
        dlf = _dot_hi(utri, dcs) + carry[0:1, :]
        carry[0:1, :] = dlf[0:1, :]
        z = fa[...] + fb_r[...]
        dfa = jnp.where(lane < 8, dlf * _sigmoid(-z), 0.0)
        dp_o[:, FA:FA + 128] = dfa.astype(CDT)
        acc[4:5, 0:128] += jnp.sum(dfa, axis=0, keepdims=True)

        @pl.when(i == nt - 1)
        def _():
            foldm = ((_iota((512, 128), 0) & (HD - 1)) == _iota((512, 128), 1)).astype(F32)
            dgn_o[...] = _dot_hi(acc[...], foldm)

    def col(width, off):
        return pl.BlockSpec((tm, width), lambda i: (nt - 1 - i, off // width))

    def rows(width):
        return pl.BlockSpec((tm, width), lambda i: (nt - 1 - i, 0))

    def vec(width):
        return pl.BlockSpec((1, width), lambda i: (0, 0))

    pair = pl.BlockSpec((2, tm, 128), lambda i: (0, nt - 1 - i, 0))
    return pl.pallas_call(
        body, name=name, grid=(nt,),
        in_specs=[col(512, QA), col(512, KA), col(512, QB), col(128, KB), col(128, FA),
                  rows(512), rows(512), rows(512), rows(512), pair, pair, rows(512), rows(512), rows(D), rows(D),
                  vec(512), vec(512), vec(512), vec(128), vec(128)],
        out_specs=[rows(DP), pl.BlockSpec((8, 128), lambda i: (0, 0))],
        out_shape=[_sds((t, DP), CDT), _sds((8, 128), F32)],
        scratch_shapes=[pltpu.VMEM((8, 128), F32), pltpu.VMEM((8, 512), F32)],
        compiler_params=_params(("arbitrary",)),
    )(proj, proj, proj, proj, proj, dqf, dkf, dvf, dqs, dkse, dvse, dcq, dck, dga, dgb, gfq, gfk, gsq, gsk, fb)


def _gate_fwd(ofox, oswa, wbf, wbs, proj, name):
    t = ofox.shape[0]
    tm = _row_tile(t)
    tn = 512

    def body(of_r, os_r, wf_r, ws_r, ga_r, gb_r, y_o, yt_o, pf_o, ps_o, oft_o, ost_o):
        j = pl.program_id(1)
        pf = _dot(of_r[...], wf_r[...])
        ps = _dot(os_r[...], ws_r[...])
        y = _sigmoid(ga_r[...]) * pf + _sigmoid(gb_r[...]) * ps
        y_o[...] = y.astype(CDT)
        yt_o[...] = y.T.astype(CDT)
        pf_o[...] = pf.astype(CDT)
        ps_o[...] = ps.astype(CDT)

        @pl.when(j == 0)
        def _():
            oft_o[...] = of_r[...].astype(F32).T.astype(CDT)
            ost_o[...] = os_r[...].astype(F32).T.astype(CDT)

    tile = pl.BlockSpec((tm, tn), lambda i, j: (i, j))
    return pl.pallas_call(
        body, name=name, grid=(t // tm, D // tn),
        in_specs=[pl.BlockSpec((tm, 512), lambda i, j: (i, 0)), pl.BlockSpec((tm, 512), lambda i, j: (i, 0)),
                  pl.BlockSpec((512, tn), lambda i, j: (0, j)), pl.BlockSpec((512, tn), lambda i, j: (0, j)),
                  pl.BlockSpec((tm, tn), lambda i, j: (i, GA // tn + j)),
                  pl.BlockSpec((tm, tn), lambda i, j: (i, GB // tn + j))],
        out_specs=[tile, pl.BlockSpec((tn, tm), lambda i, j: (j, i)), tile, tile,
                   pl.BlockSpec((512, tm), lambda i, j: (0, i)), pl.BlockSpec((512, tm), lambda i, j: (0, i))],
        out_shape=[_sds((t, D), CDT), _sds((D, t), CDT), _sds((t, D), CDT), _sds((t, D), CDT),
                   _sds((512, t), CDT), _sds((512, t), CDT)],
        compiler_params=_params(("parallel", "arbitrary")),
    )(ofox, oswa, wbf, wbs, proj, proj)


def _gate_bwd(dy, pf, ps, proj, name):
    t = dy.shape[0]
    tm = _row_tile(t)
    tn = 512

    def body(dy_r, pf_r, ps_r, ga_r, gb_r, dpf_o, dps_o, dga_o, dgb_o):
        dy_ = dy_r[...]
        sa = _sigmoid(ga_r[...])
        sb = _sigmoid(gb_r[...])
        dpf_o[...] = (dy_ * sa).astype(CDT)
        dps_o[...] = (dy_ * sb).astype(CDT)
        dga_o[...] = (dy_ * pf_r[...].astype(F32) * (sa * (1.0 - sa))).astype(CDT)
        dgb_o[...] = (dy_ * ps_r[...].astype(F32) * (sb * (1.0 - sb))).astype(CDT)

    tile = pl.BlockSpec((tm, tn), lambda i, j: (i, j))
    return pl.pallas_call(
        body, name=name, grid=(t // tm, D // tn),
        in_specs=[tile, tile, tile,
                  pl.BlockSpec((tm, tn), lambda i, j: (i, GA // tn + j)),
                  pl.BlockSpec((tm, tn), lambda i, j: (i, GB // tn + j))],
        out_specs=[tile] * 4,
        out_shape=[_sds((t, D), CDT)] * 4,
        compiler_params=_params(("parallel", "parallel")),
    )(dy, pf, ps, proj, proj)


def _tri_steps(n, by_key):
    if by_key:
        pairs = [(i, j) for j in range(n) for i in range(j, n)]
    else:
        pairs = [(i, j) for i in range(n) for j in range(i + 1)]
    return (np.array([p[0] for p in pairs], np.int32), np.array([p[1] for p in pairs], np.int32))


def _head_col(blk, lane, h):
    return jnp.sum(jnp.where(lane == h, blk, 0.0), axis=1, keepdims=True)


def _head_row(blk, sub, h):
    return jnp.sum(jnp.where(sub == h, blk, 0.0), axis=0, keepdims=True)


def _ride_specs(ride):
    if ride is None:
        return [], [], [], [], []
    kind, srcs, outs, layer, items = ride
    return list(srcs), [ANY] * len(srcs), list(outs), [ANY] * len(outs), _dma_sems(3 * len(srcs))


def _ride_start(ride, srcs, dsts, send_sems, recv_sems):
    for cp in _ici_copies(ride[0], srcs, dsts, send_sems, recv_sems, ride[3], recv=False, items=ride[4])[0]:
        cp.start()


def _ride_wait(ride, srcs, dsts, send_sems, recv_sems):
    sends, recvs = _ici_copies(ride[0], srcs, dsts, send_sems, recv_sems, ride[3], items=ride[4])
    for cp in recvs:
        cp.wait_recv()
    for cp in sends:
        cp.wait_send()


def _riding(body, n_in, n_out, ride, grid):
    if ride is None:
        return body
    nr = len(ride[1])

    def wrapped(*refs):
        ins, srcs = refs[:n_in], refs[n_in:n_in + nr]
        outs, dsts = refs[n_in + nr:n_in + nr + n_out], refs[n_in + nr + n_out:n_in + 2 * nr + n_out]
        scratch, sems = refs[n_in + 2 * nr + n_out:-2], refs[-2:]
        first = pl.program_id(0) == 0
        last = pl.program_id(0) == grid[0] - 1
        for a in range(1, len(grid)):
            first = first & (pl.program_id(a) == 0)
            last = last & (pl.program_id(a) == grid[a] - 1)

        @pl.when(first)
        def _():
            _ride_start(ride, srcs, dsts, *sems)

        body(*ins, *outs, *scratch)

        @pl.when(last)
        def _():
            _ride_wait(ride, srcs, dsts, *sems)

    return wrapped


def _fox_fwd(qf, kf, vf, c, ct, name, ride=None):
    t = qf.shape[0]
    ta = _row_tile(t)
    qi, kj = _tri_steps(t // ta, by_key=False)
    nsteps = len(qi)
    ride_in, ride_in_specs, ride_out, ride_out_specs, ride_sems = _ride_specs(ride)

    def body(qi_r, kj_r, q_r, k_r, v_r, c_r, ct_r, *rest):
        nr = len(ride_in)
        src_r, (o_o, lse_o), dst_o = rest[:nr], rest[nr:nr + 2], rest[nr + 2:2 * nr + 2]
        m_sc, l_sc, acc_sc, cq_sc, *sems = rest[2 * nr + 2:]
        p = pl.program_id(0)
        n = pl.program_id(1)
        i = qi_r[n]
        j = kj_r[n]
        lane = _iota((1, 128), 1)
        lo = lane < HD

        if ride is not None:
            @pl.when((p == 0) & (n == 0))
            def _():
                _ride_start(ride, src_r, dst_o, *sems)

        @pl.when(j == 0)
        def _():
            m_sc[...] = jnp.full(m_sc.shape, NEG, F32)
            l_sc[...] = jnp.zeros(l_sc.shape, F32)
            acc_sc[...] = jnp.zeros(acc_sc.shape, F32)
            for e in (0, 1):
                cq_sc[e] = jnp.broadcast_to(_head_col(c_r[...], lane, 2 * p + e), (ta, 128))

        def step(masked):
            q = q_r[...]
            k = k_r[...]
            vaug = jnp.concatenate([v_r[...], jnp.ones((ta, 128), CDT)], axis=1)
            if masked:
                rows = i * ta + _iota((ta, 1), 0)
                cols = j * ta + _iota((1, ta), 1)
                mask = (cols <= rows) & (cols >= PAD)
            sub = _iota((8, 1), 0)
            alphas, pvs = [], []
            for e in (0, 1):
                sel = lo if e == 0 else jnp.logical_not(lo)
                s = _dot_nt(jnp.where(sel, q, 0), k)
                ck = _head_row(ct_r[...], sub, 2 * p + e)
                cq = cq_sc[e]
                chunks = []
                for ch in range(ta // 128):
                    sl = slice(128 * ch, 128 * (ch + 1))
                    sc = s[:, sl] + cq - ck[:, sl]
                    if masked:
                        sc = jnp.where(mask[:, sl], sc, NEG)
                    chunks.append(sc)
                mx = chunks[0]
                for sc in chunks[1:]:
                    mx = jnp.maximum(mx, sc)
                m_prev = m_sc[e]
                m_new = jnp.maximum(m_prev, jnp.max(mx, axis=1, keepdims=True))
                alpha = jnp.exp(m_prev - m_new)
                pe = jnp.concatenate([jnp.exp(sc - m_new).astype(CDT) for sc in chunks], axis=1)
                pva = _dot(pe, vaug)
                l_sc[e] = alpha * l_sc[e] + pva[:, 128:]
                m_sc[e] = m_new
                alphas.append(alpha)
                pvs.append(pva[:, :128])
            acc_sc[...] = acc_sc[...] * jnp.where(lo, alphas[0], alphas[1]) + jnp.where(lo, pvs[0], pvs[1])

        edge = (j == i) | (j == 0)

        @pl.when(edge)
        def _():
            step(True)

        @pl.when(jnp.logical_not(edge))
        def _():
            step(False)

        @pl.when(j == i)
        def _():
            l = jnp.where(lo, l_sc[0], l_sc[1])
            o_o[...] = (acc_sc[...] / l).astype(CDT)
            lse_o[...] = jnp.where(lo, m_sc[0], m_sc[1]) + jnp.log(l)

        if ride is not None:
            @pl.when((p == NPAIR - 1) & (n == nsteps - 1))
            def _():
                _ride_wait(ride, src_r, dst_o, *sems)

    qblk = pl.BlockSpec((ta, 128), lambda p, n, qi_r, kj_r: (qi_r[n], p))
    kblk = pl.BlockSpec((ta, 128), lambda p, n, qi_r, kj_r: (kj_r[n], p))
    grid_spec = pltpu.PrefetchScalarGridSpec(
        num_scalar_prefetch=2, grid=(NPAIR, nsteps),
        in_specs=[qblk, kblk, kblk,
                  pl.BlockSpec((ta, 128), lambda p, n, qi_r, kj_r: (qi_r[n], 0)),
                  pl.BlockSpec((8, ta), lambda p, n, qi_r, kj_r: (0, kj_r[n]))] + ride_in_specs,
        out_specs=[qblk, qblk] + ride_out_specs,
        scratch_shapes=[pltpu.VMEM((2, ta, 128), F32), pltpu.VMEM((2, ta, 128), F32), pltpu.VMEM((ta, 128), F32),
                        pltpu.VMEM((2, ta, 128), F32)] + ride_sems,
    )
    return pl.pallas_call(
        body, name=name, grid_spec=grid_spec,
        out_shape=[_sds((t, 512), CDT), _sds((t, 512), F32)] + ride_out,
        compiler_params=_params(("arbitrary", "arbitrary")),
    )(jnp.asarray(qi), jnp.asarray(kj), qf, kf, vf, c, ct, *ride_in)


def _fox_bwd(qf, qft, kf, vf, c, ct, o, lse, do, dot, name, ride=None):
    t = qf.shape[0]
    ta = _row_tile(t)
    nq = t // ta
    qi, kj = _tri_steps(nq, by_key=False)
    nsteps = len(qi)
    ride_in, ride_in_specs, ride_out, ride_out_specs, ride_sems = _ride_specs(ride)

    def body(qi_r, kj_r, q_r, qt_r, k_r, v_r, c_r, ct_r, o_r, lse_r, do_r, dot_r, *rest):
        nr = len(ride_in)
        src_r, (dq_o, dcq_o, dk_o, dv_o, dck_o), dst_o = rest[:nr], rest[nr:nr + 5], rest[nr + 5:2 * nr + 5]
        lse_sc, dl_sc, cq_sc, dq_sc, dcq_sc, dkt_sc, dvt_sc, dckt_sc, *sems = rest[2 * nr + 5:]
        p = pl.program_id(0)
        n = pl.program_id(1)
        i = qi_r[n]
        j = kj_r[n]
        lane = _iota((1, 128), 1)
        lo = lane < HD
        top = _iota((128, 1), 0) < HD

        if ride is not None:
            @pl.when((p == 0) & (n == 0))
            def _():
                _ride_start(ride, src_r, dst_o, *sems)

        @pl.when(n == 0)
        def _():
            dkt_sc[...] = jnp.zeros(dkt_sc.shape, F32)
            dvt_sc[...] = jnp.zeros(dvt_sc.shape, F32)
            dckt_sc[...] = jnp.zeros(dckt_sc.shape, F32)

        @pl.when(j == 0)
        def _():
            dq_sc[...] = jnp.zeros(dq_sc.shape, F32)
            dcq_sc[...] = jnp.zeros(dcq_sc.shape, F32)
            dd = do_r[...] * o_r[...].astype(F32)
            lse = lse_r[...]
            for e in (0, 1):
                sel = lo if e == 0 else jnp.logical_not(lo)
                cq_sc[e] = jnp.broadcast_to(_head_col(c_r[...], lane, 2 * p + e), (ta, 128))
                dl_sc[e] = jnp.broadcast_to(jnp.sum(jnp.where(sel, dd, 0.0), axis=1, keepdims=True), (ta, 128))
                lse_sc[e] = jnp.broadcast_to(lse[:, HD * e:HD * e + 1], (ta, 128))

        def step(masked):
            q = q_r[...]
            qt = qt_r[...]
            k = k_r[...]
            v = v_r[...]
            dob = do_r[...].astype(CDT)
            dot_ = dot_r[...]
            ones = jnp.ones((ta, 128), CDT)
            ones16 = jnp.ones((16, ta), CDT)
            if masked:
                rows = i * ta + _iota((ta, 1), 0)
                cols = j * ta + _iota((1, ta), 1)
                mask = (cols <= rows) & (cols >= PAD)
            sub = _iota((8, 1), 0)
            for e in (0, 1):
                sel = lo if e == 0 else jnp.logical_not(lo)
                rsel = top if e == 0 else jnp.logical_not(top)
                s = _dot_nt(jnp.where(sel, q, 0), k)
                dp = _dot_nt(jnp.where(sel, dob, 0), v)
                ck = _head_row(ct_r[...], sub, 2 * p + e)
                cq, lse_e, dl = cq_sc[e], lse_sc[e], dl_sc[e]
                prs, dss = [], []
                for ch in range(ta // 128):
                    sl = slice(128 * ch, 128 * (ch + 1))
                    sc = s[:, sl] + cq - ck[:, sl]
                    if masked:
                        sc = jnp.where(mask[:, sl], sc, NEG)
                    pr = jnp.exp(sc - lse_e)
                    prs.append(pr.astype(CDT))
                    dss.append((pr * (dp[:, sl] - dl)).astype(CDT))
                pb = jnp.concatenate(prs, axis=1)
                dsb = jnp.concatenate(dss, axis=1)
                dvt_sc[j] += _dot(jnp.where(rsel, dot_, 0), pb)
                dkc = _dot(jnp.concatenate([jnp.where(rsel, qt, 0), ones16], axis=0), dsb)
                dkt_sc[j] += dkc[0:128]
                dckt_sc[j, 0:8, :] += jnp.where(sub == e, dkc[128:136], 0.0)
                dqa = _dot(dsb, jnp.concatenate([jnp.where(sel, k, 0), ones], axis=1))
                dq_sc[...] += dqa[:, :128]
                dcq_sc[e] += dqa[:, 128:]

        edge = (j == i) | (j == 0)

        @pl.when(edge)
        def _():
            step(True)

        @pl.when(jnp.logical_not(edge))
        def _():
            step(False)

        @pl.when(j == i)
        def _():
            dq_o[...] = dq_sc[...]
            dcq_o[...] = jnp.where(lo, dcq_sc[0], dcq_sc[1])

        @pl.when(n == nsteps - 1)
        def _():
            spread = (_iota((128, 128), 1) == _iota((128, 128), 0) // HD).astype(F32)
            for jb in range(nq):
                rs = slice(jb * ta, (jb + 1) * ta)
                dk_o[rs, :] = dkt_sc[jb].T
                dv_o[rs, :] = dvt_sc[jb].T
                dck_o[rs, :] = _dot_hi(spread, dckt_sc[jb]).T

        if ride is not None:
            @pl.when((p == NPAIR - 1) & (n == nsteps - 1))
            def _():
                _ride_wait(ride, src_r, dst_o, *sems)

    qblk = pl.BlockSpec((ta, 128), lambda p, n, qi_r, kj_r: (qi_r[n], p))
    qtblk = pl.BlockSpec((128, ta), lambda p, n, qi_r, kj_r: (p, qi_r[n]))
    kblk = pl.BlockSpec((ta, 128), lambda p, n, qi_r, kj_r: (kj_r[n], p))
    whole = pl.BlockSpec((t, 128), lambda p, n, qi_r, kj_r: (0, p))
    grid_spec = pltpu.PrefetchScalarGridSpec(
        num_scalar_prefetch=2, grid=(NPAIR, nsteps),
        in_specs=[qblk, qtblk, kblk, kblk,
                  pl.BlockSpec((ta, 128), lambda p, n, qi_r, kj_r: (qi_r[n], 0)),
                  pl.BlockSpec((8, ta), lambda p, n, qi_r, kj_r: (0, kj_r[n])),
                  qblk, qblk, qblk, qtblk] + ride_in_specs,
        out_specs=[qblk, qblk, whole, whole, whole] + ride_out_specs,
        scratch_shapes=[pltpu.VMEM((2, ta, 128), F32)] * 3 + [pltpu.VMEM((ta, 128), F32), pltpu.VMEM((2, ta, 128), F32)]
        + [pltpu.VMEM((nq, 128, ta), F32)] * 3 + ride_sems,
    )
    return pl.pallas_call(
        body, name=name, grid_spec=grid_spec,
        out_shape=[_sds((t, 512), F32)] * 5 + ride_out,
        compiler_params=_params(("arbitrary", "arbitrary")),
    )(jnp.asarray(qi), jnp.asarray(kj), qf, qft, kf, vf, c, ct, o, lse, do, dot, *ride_in)


def _bucket_table():
    r = np.arange(BLK)[:, None]
    c = np.arange(3 * BLK)[None, :]
    d = np.where(c < BLK, r + BLK - c, r - (c - BLK))
    n = np.maximum(d, 0)
    max_exact = N_BUCKETS // 2
    nf = np.maximum(n, 1).astype(np.float32)
    large = max_exact + (np.log(nf / max_exact) / math.log(BLK / max_exact) * (N_BUCKETS - max_exact)).astype(np.int32)
    large = np.minimum(large, N_BUCKETS - 1)
    b = np.where(n < max_exact, n, large)
    return np.where(c < 2 * BLK, b, N_BUCKETS - 1).astype(np.int32)


def _bias_fwd(table, name):
    bucket = jnp.asarray(_bucket_table())

    def body(tab_r, b_r, o_o):
        h = pl.program_id(0)
        b = b_r[...]
        acc = jnp.zeros(b.shape, F32)
        for k in range(N_BUCKETS):
            acc = jnp.where(b == k, tab_r[k, h], acc)
        o_o[...] = acc

    return pl.pallas_call(
        body, name=name, grid=(8,),
        in_specs=[pl.BlockSpec(memory_space=pltpu.SMEM), pl.BlockSpec((BLK, 3 * BLK), lambda h: (0, 0))],
        out_specs=pl.BlockSpec((None, BLK, 3 * BLK), lambda h: (h, 0, 0)),
        out_shape=_sds((8, BLK, 3 * BLK), F32),
        compiler_params=_params(("parallel",)),
    )(table, bucket)


def _bias_bwd(dbias, name):
    bucket = jnp.asarray(_bucket_table())

    def body(d_r, b_r, o_o):
        h = pl.program_id(0)
        b = b_r[...]
        d = d_r[...]
        lane = _iota((1, 128), 1)
        row = jnp.zeros((1, 128), F32)
        for k in range(N_BUCKETS):
            row = jnp.where(lane == k, jnp.sum(jnp.where(b == k, d, 0.0)), row)
        o_o[pl.ds(h, 1), :] = row

    return pl.pallas_call(
        body, name=name, grid=(8,),
        in_specs=[pl.BlockSpec((None, BLK, 3 * BLK), lambda h: (h, 0, 0)), pl.BlockSpec((BLK, 3 * BLK), lambda h: (0, 0))],
        out_specs=pl.BlockSpec((8, 128), lambda h: (0, 0)),
        out_shape=_sds((8, 128), F32),
        compiler_params=_params(("arbitrary",)),
    )(dbias, bucket)


def _swa_valid(i):
    r = _iota((BLK, 1), 0)
    c = _iota((1, 3 * BLK), 1)
    prev = (c < BLK) & (c > r) & (i >= 1) & ((i - 1) * BLK + c >= PAD)
    cc = c - BLK
    cur = (c >= BLK) & (c < 2 * BLK) & (cc <= r) & (i * BLK + cc >= PAD)
    cm = c - 2 * BLK
    meta = (c >= 2 * BLK) & (cm >= PAD) & (i * BLK + r - cm >= BLK)
    return prev | cur | meta


def _swa_kv_specs(ta):
    nb = ta // BLK
    return [pl.BlockSpec((None, BLK, 128), lambda p, i: (p // 2, jnp.maximum(i * nb - 1, 0), 0)),
            pl.BlockSpec((None, ta, 128), lambda p, i: (p // 2, i, 0)),
            pl.BlockSpec((None, BLK, 128), lambda p, i: (p // 2, 0, 0))]


def _swa_fwd(qs, kse, vse, bias, sinks, name, ride=None):
    t = qs.shape[0]
    ta = _row_tile(t)
    nb = ta // BLK
    grid = (NPAIR, t // ta)
    ride_in, ride_in_specs, ride_out, ride_out_specs, ride_sems = _ride_specs(ride)

    def body(sink_r, q_r, kp_r, kc_r, km_r, vp_r, vc_r, vm_r, b_r, o_o, lse_o):
        p = pl.program_id(0)
        i = pl.program_id(1)
        lo = _iota((1, 128), 1) < HD
        k4 = jnp.concatenate([kp_r[...], kc_r[...]], axis=0)
        v4 = jnp.concatenate([vp_r[...], vc_r[...]], axis=0)
        for b in range(nb):
            rows = slice(BLK * b, BLK * (b + 1))
            q = q_r[rows, :]
            k3 = jnp.concatenate([k4[BLK * b:BLK * (b + 2)], km_r[...]], axis=0)
            v3 = jnp.concatenate([v4[BLK * b:BLK * (b + 2)], vm_r[...]], axis=0)
            valid = _swa_valid(i * nb + b)
            outs, lses = [], []
            for e in (0, 1):
                sel = lo if e == 0 else jnp.logical_not(lo)
                s = _dot_nt(jnp.where(sel, q, 0), k3) + b_r[e]
                s = jnp.where(valid, s, NEG)
                sink = sink_r[2 * p + e]
                mx = jnp.maximum(jnp.max(s, axis=1, keepdims=True), sink)
                pe = jnp.exp(s - mx)
                den = jnp.sum(pe, axis=1, keepdims=True) + jnp.exp(sink - mx)
                outs.append(_dot(pe.astype(CDT), v3) / den)
                lses.append(mx + jnp.log(den))
            o_o[rows, :] = jnp.where(lo, outs[0], outs[1]).astype(CDT)
            lse_o[rows, :] = jnp.where(lo, lses[0], lses[1])

    qblk = pl.BlockSpec((ta, 128), lambda p, i: (i, p))
    res = pl.pallas_call(
        _riding(body, 9, 2, ride, grid), name=name, grid=grid,
        in_specs=[pl.BlockSpec(memory_space=pltpu.SMEM), qblk] + _swa_kv_specs(ta) + _swa_kv_specs(ta)
        + [pl.BlockSpec((2, BLK, 3 * BLK), lambda p, i: (p, 0, 0))] + ride_in_specs,
        out_specs=[qblk, qblk] + ride_out_specs,
        out_shape=[_sds((t, 512), CDT), _sds((t, 512), F32)] + ride_out, scratch_shapes=ride_sems,
        compiler_params=_params(("arbitrary", "arbitrary") if ride else ("parallel", "parallel")),
    )(sinks, qs, kse, kse, kse, vse, vse, vse, bias, *ride_in)
    return (res[0], res[1], res[2:]) if ride else res


def _swa_bwd(qs, kse, vse, bias, sinks, o, lse, do, name):
    t = qs.shape[0]
    ta = _row_tile(t)
    nb = ta // BLK

    def body(sink_r, q_r, kp_r, kc_r, km_r, vp_r, vc_r, vm_r, b_r, o_r, lse_r, do_r,
             dq_o, dk_o, dv_o, db_o, dsk_o):
        p = pl.program_id(0)
        i = pl.program_id(1)
        lo = _iota((1, 128), 1) < HD

        @pl.when((i == 0) & (p % 2 == 0))
        def _():
            dk_o[...] = jnp.zeros(dk_o.shape, F32)
            dv_o[...] = jnp.zeros(dv_o.shape, F32)

        @pl.when(i == 0)
        def _():
            db_o[...] = jnp.zeros(db_o.shape, F32)
            dsk_o[...] = jnp.zeros(dsk_o.shape, F32)

        k4 = jnp.concatenate([kp_r[...], kc_r[...]], axis=0)
        v4 = jnp.concatenate([vp_r[...], vc_r[...]], axis=0)
        for b in range(nb):
            ib = i * nb + b
            rows = slice(BLK * b, BLK * (b + 1))
            q = q_r[rows, :]
            do_ = do_r[rows, :]
            dd = do_ * o_r[rows, :].astype(F32)
            lse = lse_r[rows, :]
            k3 = jnp.concatenate([k4[BLK * b:BLK * (b + 2)], km_r[...]], axis=0)
            v3 = jnp.concatenate([v4[BLK * b:BLK * (b + 2)], vm_r[...]], axis=0)
            valid = _swa_valid(ib)
            dq = jnp.zeros((BLK, 128), F32)
            dk3 = jnp.zeros((3 * BLK, 128), F32)
            dv3 = jnp.zeros((3 * BLK, 128), F32)
            dsink = []
            for e in (0, 1):
                sel = lo if e == 0 else jnp.logical_not(lo)
                qe = jnp.where(sel, q, 0)
                doe = jnp.where(sel, do_, 0.0).astype(CDT)
                lse_e = lse[:, HD * e:HD * e + 1]
                s = _dot_nt(qe, k3) + b_r[e]
                s = jnp.where(valid, s, NEG)
                pr = jnp.exp(s - lse_e)
                delta = jnp.sum(jnp.where(sel, dd, 0.0), axis=1, keepdims=True)
                ds = pr * (_dot_nt(doe, v3) - delta)
                db_o[e] += ds
                dsink.append(-jnp.sum(jnp.exp(sink_r[2 * p + e] - lse_e) * delta, axis=0, keepdims=True))
                dq = dq + _dot(ds.astype(CDT), jnp.where(sel, k3, 0))
                dk3 = dk3 + _dot(ds.T.astype(CDT), qe)
                dv3 = dv3 + _dot(pr.T.astype(CDT), doe)
            dq_o[rows, :] = dq
            prev = pl.ds(pl.multiple_of(jnp.maximum(ib - 1, 0) * BLK, BLK), BLK)
            cur = pl.ds(pl.multiple_of(ib * BLK, BLK), BLK)
            dk_o[prev, :] += dk3[0:BLK]
            dk_o[cur, :] += dk3[BLK:2 * BLK]
            dk_o[0:BLK, :] += dk3[2 * BLK:]
            dv_o[prev, :] += dv3[0:BLK]
            dv_o[cur, :] += dv3[BLK:2 * BLK]
            dv_o[0:BLK, :] += dv3[2 * BLK:]
            dsk_o[0:1, :] += jnp.where(lo, dsink[0], dsink[1])

    qblk = pl.BlockSpec((ta, 128), lambda p, i: (i, p))
    kvacc = pl.BlockSpec((None, t, 128), lambda p, i: (p // 2, 0, 0))
    bblk = pl.BlockSpec((2, BLK, 3 * BLK), lambda p, i: (p, 0, 0))
    return pl.pallas_call(
        body, name=name, grid=(NPAIR, t // ta),
        in_specs=[pl.BlockSpec(memory_space=pltpu.SMEM), qblk] + _swa_kv_specs(ta) + _swa_kv_specs(ta)
        + [bblk, qblk, qblk, qblk],
        out_specs=[qblk, kvacc, kvacc, bblk, pl.BlockSpec((None, 8, 128), lambda p, i: (p, 0, 0))],
        out_shape=[_sds((t, 512), F32), _sds((2, t, 128), F32), _sds((2, t, 128), F32),
                   _sds((8, BLK, 3 * BLK), F32), _sds((NPAIR, 8, 128), F32)],
        compiler_params=_params(("arbitrary", "arbitrary")),
    )(sinks, qs, kse, kse, kse, vse, vse, vse, bias, o, lse, do)


def _sum8(slots, name):
    def body(a_r, o_o):
        acc = a_r[0]
        for k in range(1, 8):
            acc = acc + a_r[k]
        o_o[...] = acc

    return pl.pallas_call(
        body, name=name, out_shape=_sds((SMALL_ROWS, 128), F32),
        in_specs=[pl.BlockSpec(memory_space=pltpu.VMEM)], out_specs=pl.BlockSpec(memory_space=pltpu.VMEM),
        compiler_params=_params(),
    )(slots)


def _place():
    x, y, c = lax.axis_index("x"), lax.axis_index("y"), lax.axis_index("c")
    chips = [(1 - x, y), (x, 1 - y), (1 - x, 1 - y)]
    return x, y, c, chips


def _remote(src, dst, send_sems, recv_sems, k, to):
    return pltpu.make_async_remote_copy(src_ref=src, dst_ref=dst, send_sem=send_sems.at[k], recv_sem=recv_sems.at[k],
                                        device_id=to, device_id_type=MESH_ID)


ANY = pl.BlockSpec(memory_space=pl.ANY)


def _mix_cols(w):
    return jnp.concatenate([w[:, 2312:4360], w[:, 0:1536], w[:, 1544:2312], w[:, 1536:1544],
                            jnp.zeros((w.shape[0], DP - D_IN), w.dtype)], axis=1)


def _unmix_cols(w):
    return jnp.concatenate([w[:, QA:QA + 1536], w[:, FA:FA + 8], w[:, QB:QB + 768], w[:, GA:GA + 2048]], axis=1)


def _rows128(a, rows):
    flat = a.reshape(-1)
    return jnp.pad(flat, (0, rows * 128 - flat.shape[0])).reshape(rows, 128)


GRAD_FORM = {"ffn1_w_in": "col", "ffn2_w_in": "col", "w_branch_fox": "col", "w_branch_swa": "col",
             "ffn1_w_out": "3d", "ffn2_w_out": "3d", "w_out": "3d", "w_in": "3d"}
SUM_TILE = {1024: 128, 704: 176, 512: 128, 256: 128}
NT = len(SHARD_ITEMS)
ALL_ITEMS = tuple(range(NT))


def _half_rows(c, r):
    return pl.ds(pl.multiple_of(c * (r // 2), 16), r // 2)


def _ici_copies(kind, srcs, dsts, send_sems, recv_sems, layer, recv=True, items=ALL_ITEMS):
    x, y, c, chips = _place()
    s = 2 * x + y
    sends, recvs = [], []
    for t, (item, src, dst) in enumerate(zip(items, srcs, dsts)):
        nm, (r, cc), _ = SHARD_ITEMS[item]
        for j, (cx, cy) in enumerate(chips):
            sj = 2 * cx + cy
            k = 3 * t + j
            to = (cx, cy, c)
            if kind == "gather":
                hs = _half_rows(c, r)
                sends.append(_remote(src.at[layer, hs], dst.at[s, hs], send_sems, recv_sems, k, to))
                if recv:
                    recvs.append(_remote(src.at[layer, hs], dst.at[sj, hs], send_sems, recv_sems, k, to))
            else:
                if GRAD_FORM[nm] == "col":
                    piece = src.at[:, pl.ds(pl.multiple_of(sj * cc, 128), cc)]
                else:
                    piece = src.at[sj]
                sends.append(_remote(piece, dst.at[j], send_sems, recv_sems, k, to))
                recvs.append(sends[-1])
    return sends, recvs


def _slab_shapes(items=ALL_ITEMS):
    return [_sds((4, *SHARD_ITEMS[t][1]), CDT) for t in items]


def _dma_sems(n):
    return [pltpu.SemaphoreType.DMA((n,)), pltpu.SemaphoreType.DMA((n,))]


def _forward_sends(dsts, send_sems, recv_sems, items=ALL_ITEMS):
    x, y, c, chips = _place()
    sends, recvs = [], []
    for t, (item, dst) in enumerate(zip(items, dsts)):
        r = SHARD_ITEMS[item][1][0]
        for j, (cx, cy) in enumerate(chips):
            sj = 2 * cx + cy
            hs, ho = _half_rows(c, r), _half_rows(1 - c, r)
            sends.append(_remote(dst.at[sj, hs], dst.at[sj, hs], send_sems, recv_sems, 3 * t + j, (x, y, 1 - c)))
            recvs.append(_remote(dst.at[sj, ho], dst.at[sj, ho], send_sems, recv_sems, 3 * t + j, (x, y, 1 - c)))
    return sends, recvs


def _gather_layer(wb, mflat, layer, name, items):
    nt = len(items)

    def body(*refs):
        srcs, m_r, dsts, mall_o = refs[:nt], refs[nt], refs[nt + 1:2 * nt + 1], refs[2 * nt + 1]
        send_sems, recv_sems, fsend, frecv, msend, mrecv = refs[2 * nt + 2:]
        x, y, c, chips = _place()
        s = 2 * x + y
        sends, recvs = _ici_copies("gather", srcs, dsts, send_sems, recv_sems, layer, items=items)
        metas = [_remote(m_r, mall_o.at[s], msend, mrecv, j, (cx, cy, c)) for j, (cx, cy) in enumerate(chips)]
        for cp in sends + metas:
            cp.start()
        fwds, frecvs = _forward_sends(dsts, fsend, frecv, items)
        for got, fwd in zip(recvs, fwds):
            got.wait_recv()
            fwd.start()
        for got in frecvs:
            got.wait_recv()
        for j, (cx, cy) in enumerate(chips):
            _remote(m_r, mall_o.at[2 * cx + cy], msend, mrecv, j, (cx, cy, c)).wait_recv()
        for cp in sends + metas + fwds:
            cp.wait_send()

    return pl.pallas_call(
        body, name=name, out_shape=_slab_shapes(items) + [_sds((4, META_ROWS, 128), F32)],
        in_specs=[ANY] * (nt + 1), out_specs=[ANY] * (nt + 1),
        scratch_shapes=_dma_sems(3 * nt) + _dma_sems(3 * nt) + _dma_sems(3),
    )(*wb, mflat)


def _forward_layer(slabs, name, items=ALL_ITEMS):
    nt = len(items)

    def body(*refs):
        ins, outs, send_sems, recv_sems = refs[:nt], refs[nt:2 * nt], refs[2 * nt], refs[2 * nt + 1]
        sends, recvs = _forward_sends(outs, send_sems, recv_sems, items)
        for cp in sends:
            cp.start()
        for cp in recvs:
            cp.wait_recv()
        for cp in sends:
            cp.wait_send()

    return pl.pallas_call(
        body, name=name, out_shape=_slab_shapes(items), in_specs=[ANY] * nt, out_specs=[ANY] * nt,
        input_output_aliases={t: t for t in range(nt)}, scratch_shapes=_dma_sems(3 * nt),
    )(*slabs)


def _half_shape(nm, r, c):
    return (r // 2, 4 * c) if GRAD_FORM[nm] == "col" else (4, r // 2, c)


def _swap_layer(gs, gsm, name, items=ALL_ITEMS):
    small = gsm is not None
    nt = len(items)

    def body(*refs):
        g_rs = refs[:nt]
        pos = nt
        if small:
            s_r = refs[pos]
            pos += 1
        got_os = refs[pos:pos + nt]
        pos += nt
        if small:
            slots_o = refs[pos]
            pos += 1
        send_sems, recv_sems = refs[pos], refs[pos + 1]
        x, y, c, _ = _place()
        sib = (x, y, 1 - c)
        sent = []
        for t, (item, g_r, got_o) in enumerate(zip(items, g_rs, got_os)):
            nm, (r, cc), _ = SHARD_ITEMS[item]
            ho = _half_rows(1 - c, r)
            src = g_r.at[ho, :] if GRAD_FORM[nm] == "col" else g_r.at[:, ho, :]
            sent.append(_remote(src, got_o, send_sems, recv_sems, t, sib))
        if small:
            ssend, srecv, loc_sem = refs[pos + 2], refs[pos + 3], refs[pos + 4]
            me = 4 * x + 2 * y + c
            loc = pltpu.make_async_copy(s_r, slots_o.at[me], loc_sem.at[0])
            loc.start()
            peers = [(x ^ (k >> 2), y ^ ((k >> 1) & 1), c ^ (k & 1)) for k in range(1, 8)]
            for k, peer in enumerate(peers):
                sent.append(_remote(s_r, slots_o.at[me], ssend, srecv, k, peer))
        for cp in sent:
            cp.start()
        for cp in sent[:nt]:
            cp.wait_recv()
        if small:
            for k, (px, py, pc) in enumerate(peers):
                _remote(s_r, slots_o.at[4 * px + 2 * py + pc], ssend, srecv, k, (px, py, pc)).wait_recv()
        for cp in sent:
            cp.wait_send()
        if small:
            loc.wait()

    outs = [_sds(_half_shape(*SHARD_ITEMS[item][0:1], *SHARD_ITEMS[item][1]), CDT) for item in items]
    ops = list(gs)
    sems = _dma_sems(nt)
    if small:
        outs.append(_sds((8, SMALL_ROWS, 128), F32))
        ops.append(gsm)
        sems = sems + _dma_sems(7) + [pltpu.SemaphoreType.DMA((1,))]
    res = pl.pallas_call(
        body, name=name, out_shape=outs, in_specs=[ANY] * len(ops), out_specs=[ANY] * len(outs), scratch_shapes=sems,
    )(*ops)
    return (res[:nt], res[nt]) if small else (res, None)


def _pair_add_t(own, got, half_idx, nm, r, name):
    tr = SUM_TILE[r]
    nb = (r // 2) // tr
    if GRAD_FORM[nm] == "col":
        blk = (tr, own.shape[1])
        own_spec = pl.BlockSpec(blk, lambda i, c_r: (c_r[0] * nb + i, 0))
        half_spec = pl.BlockSpec(blk, lambda i, c_r: (i, 0))
    else:
        blk = (4, tr, own.shape[2])
        own_spec = pl.BlockSpec(blk, lambda i, c_r: (0, c_r[0] * nb + i, 0))
        half_spec = pl.BlockSpec(blk, lambda i, c_r: (0, i, 0))

    def body(c_r, a_r, b_r, o_o):
        o_o[...] = (a_r[...].astype(F32) + b_r[...].astype(F32)).astype(CDT)

    grid_spec = pltpu.PrefetchScalarGridSpec(num_scalar_prefetch=1, grid=(nb,), in_specs=[own_spec, half_spec],
                                             out_specs=half_spec)
    return pl.pallas_call(body, name=name, grid_spec=grid_spec, out_shape=_sds(got.shape, CDT),
                          compiler_params=_params(("parallel",)))(half_idx, own, got)


def _sum4_t(ps, got3, buf, idx, layer, nm, r, name):
    tr = SUM_TILE[r]
    nb = (r // 2) // tr
    c = got3.shape[2]
    if GRAD_FORM[nm] == "col":
        ps_spec = pl.BlockSpec((tr, c), lambda i, x_r: (i, x_r[0]))
    else:
        ps_spec = pl.BlockSpec((None, tr, c), lambda i, x_r: (x_r[0], i, 0))

    def body(x_r, a_r, b_r, buf_r, o_o):
        o_o[...] = ((a_r[...].astype(F32) + b_r[0].astype(F32)) + b_r[1].astype(F32)) + b_r[2].astype(F32)

    grid_spec = pltpu.PrefetchScalarGridSpec(
        num_scalar_prefetch=1, grid=(nb,),
        in_specs=[ps_spec, pl.BlockSpec((3, tr, c), lambda i, x_r: (0, i, 0)), ANY],
        out_specs=pl.BlockSpec((None, tr, c), lambda i, x_r: (layer, x_r[1] * nb + i, 0)),
    )
    return pl.pallas_call(body, name=name, grid_spec=grid_spec, out_shape=_sds(buf.shape, F32),
                          input_output_aliases={3: 0}, compiler_params=_params(("parallel",)))(idx, ps, got3, buf)


def _scatter_layer(ps, name, items=ALL_ITEMS):
    nt = len(items)

    def body(*refs):
        srcs, dsts, send_sems, recv_sems = refs[:nt], refs[nt:2 * nt], refs[2 * nt], refs[2 * nt + 1]
        sends, recvs = _ici_copies("scatter", srcs, dsts, send_sems, recv_sems, None, items=items)
        for cp in sends:
            cp.start()
        for cp in recvs:
            cp.wait_recv()
        for cp in sends:
            cp.wait_send()

    return pl.pallas_call(
        body, name=name, out_shape=_got3_shapes(items), in_specs=[ANY] * nt, out_specs=[ANY] * nt,
        scratch_shapes=_dma_sems(3 * nt),
    )(*ps)


def _got3_shapes(items=ALL_ITEMS):
    return [_sds((3, SHARD_ITEMS[t][1][0] // 2, SHARD_ITEMS[t][1][1]), CDT) for t in items]


def _join_layer(bufs, name):
    def body(*refs):
        ins, outs, send_sems, recv_sems = refs[:NT], refs[NT:2 * NT], refs[2 * NT], refs[2 * NT + 1]
        x, y, c, _ = _place()
        sent = []
        for t, ((nm, (r, cc), _), b_o) in enumerate(zip(SHARD_ITEMS, outs)):
            hs = _half_rows(c, r)
            sent.append(_remote(b_o.at[:, hs, :], b_o.at[:, hs, :], send_sems, recv_sems, t, (x, y, 1 - c)))
        for cp in sent:
            cp.start()
        for t, ((nm, (r, cc), _), b_o) in enumerate(zip(SHARD_ITEMS, outs)):
            ho = _half_rows(1 - c, r)
            _remote(b_o.at[:, ho, :], b_o.at[:, ho, :], send_sems, recv_sems, t, (x, y, 1 - c)).wait_recv()
        for cp in sent:
            cp.wait_send()

    return pl.pallas_call(
        body, name=name, out_shape=[_sds(b.shape, F32) for b in bufs], in_specs=[ANY] * NT, out_specs=[ANY] * NT,
        input_output_aliases={t: t for t in range(NT)}, scratch_shapes=_dma_sems(NT),
    )(*bufs)


def _adamw3(w, g, m, v, name):
    nl, r, c = w.shape
    tr = SUM_TILE.get(r, r)
    if r % 8:
        blk = pl.BlockSpec((None, r, 256), lambda l, i: (l, 0, i))
        steps = c // 256
    else:
        blk = pl.BlockSpec((None, tr, c), lambda l, i: (l, i, 0))
        steps = r // tr

    def body(w_r, g_r, m_r, v_r, d_o, m_o, v_o):
        g_ = g_r[...]
        m_ = ADAM_B1 * m_r[...] + (1.0 - ADAM_B1) * g_
        v_ = ADAM_B2 * v_r[...] + (1.0 - ADAM_B2) * jnp.square(g_)
        m_hat = m_ / (1.0 - ADAM_B1 ** ADAM_STEP)
        v_hat = v_ / (1.0 - ADAM_B2 ** ADAM_STEP)
        d_o[...] = -ADAM_LR * (m_hat / (jnp.sqrt(v_hat) + ADAM_EPS) + ADAM_WD * w_r[...])
        m_o[...] = m_
        v_o[...] = v_

    return pl.pallas_call(
        body, name=name, grid=(nl, steps),
        in_specs=[blk] * 4, out_specs=[blk] * 3, out_shape=[_sds((nl, r, c), F32)] * 3,
        compiler_params=_params(("parallel", "parallel")),
    )(w, g, m, v)


def _full_weights(slabs, wb, layer, shard, items=ALL_ITEMS):
    ws = {}
    for t, slab in zip(items, slabs):
        nm, (r, c), kind = SHARD_ITEMS[t]
        slab = lax.dynamic_update_slice(slab, wb[nm][layer][None], (shard, 0, 0))
        ws[nm] = slab.reshape(4 * r, c) if kind == "row" else jnp.concatenate([slab[s] for s in range(4)], axis=1)
    return ws


def _exchange_forms(g, items=ALL_ITEMS):
    out = []
    for t in items:
        nm, (r, c), _ = SHARD_ITEMS[t]
        a = g[nm]
        if nm == "w_in":
            a = a.reshape(D, 4, c).transpose(1, 0, 2)
        elif GRAD_FORM[nm] == "3d":
            a = a.reshape(4, r, c)
        out.append(a)
    return out


SMALL_ITEMS = (("rel_bias_table", 2), ("ffn1_norm", 16), ("mix_norm", 16), ("ffn2_norm", 16), ("forget_bias", 1),
               ("fox_q_norm", 1), ("fox_k_norm", 1), ("swa_q_norm", 1), ("swa_k_norm", 1), ("swa_sinks", 1))
SMALL_ADAM_ROWS = 96


def _layer_fwd(h, lw, l, ride=None, late=None):
    rides = late["rides"] if late else {}

    def run(key, fn, *args):
        r = rides.get(key)
        if r is None:
            return fn(*args)
        out = fn(*args, ride=r)
        late["arrived"](key, out[-1])
        return out[0] if len(out) == 2 else out[:-1]

    sv = {"h0": h}
    a, sv["a1t"] = _rms_fwd(h, lw["ffn1_norm"], f"rms_fwd_a{l}")
    sv["gu1"], s, sv["s1t"] = run("ffn_in_a", _ffn_in, a, lw["ffn1_w_in"], f"ffn_in_a{l}")
    h = run("ffn_out_a", _mm_res, s, lw["ffn1_w_out"], h, 0.5, f"ffn_out_a{l}")
    sv["h1"] = h
    a, sv["amt"] = _rms_fwd(h, lw["mix_norm"], f"rms_fwd_m{l}")
    if late:
        late["need"](lw, "mixer")
    proj = run("proj", _mm, a, lw["w_mix"], F32, _row_tile(h.shape[0]), DP, f"proj{l}")
    sv["proj"] = proj
    qf, kf, vf, qs, kse, vse, c, ct, sv["qft"] = _qknorm_fwd(proj, lw["gfq"], lw["gfk"], lw["gsq"], lw["gsk"], lw["fb"],
                                                              f"qknorm_fwd{l}")
    ofox, lse_f, *rode = _fox_fwd(qf, kf, vf, c, ct, f"fox_fwd{l}", ride)
    oswa, lse_s = run("swa_fwd", _swa_fwd, qs, kse, vse, lw["bias"], lw["sinks"], f"swa_fwd{l}")
    if late:
        late["need"](lw, "gate")
    sv.update(qf=qf, kf=kf, vf=vf, qs=qs, kse=kse, vse=vse, c=c, ct=ct, ofox=ofox, oswa=oswa, lse_f=lse_f, lse_s=lse_s)
    y, sv["yt"], sv["pf"], sv["ps"], sv["oft"], sv["ost"] = _gate_fwd(ofox, oswa, lw["w_branch_fox"], lw["w_branch_swa"],
                                                                     proj, f"gate_fwd{l}")
    h = _mm_res(y, lw["w_out"], h, 1.0, f"mix_out{l}")
    sv["h2"] = h
    a, sv["a2t"] = _rms_fwd(h, lw["ffn2_norm"], f"rms_fwd_b{l}")
    sv["gu2"], s, sv["s2t"] = _ffn_in(a, lw["ffn2_w_in"], f"ffn_in_b{l}")
    h = _mm_res(s, lw["ffn2_w_out"], h, 0.5, f"ffn_out_b{l}")
    return h, sv, rode


def _ffn_bwd(dh, dhb, h_in, at, gu, st, norm, w_in, w_out, tag, rides=None):
    r = rides or (None,) * 4
    rode = []

    def split(res, ride):
        if ride is None:
            return res
        rode.extend(res[-1])
        return res[0] if len(res) == 2 else res[:-1]

    dgu = split(_ffn_bwd_mid(dhb, w_out, gu, f"ffn_bwd_mid_{tag}", r[0]), r[0])
    d_w_out = split(_mm(st, dhb, CDT, 256, D, f"dw_ffn_out_{tag}", scale=0.5, ride=r[1]), r[1])
    dh, dhb, dg = split(_ffn_bwd_in(dgu, w_in, h_in, norm, dh, f"ffn_bwd_in_{tag}", r[2]), r[2])
    d_w_in = split(_mm(at, dgu, CDT, D, 256, f"dw_ffn_in_{tag}", ride=r[3]), r[3])
    return dh, dhb, d_w_out, d_w_in, dg, rode


def _layer_bwd(dh, dhb, sv, lw, l, ride=None, before_ffn1=None):
    g = {}
    dh, dhb, g["ffn2_w_out"], g["ffn2_w_in"], g["ffn2_norm"], _ = _ffn_bwd(
        dh, dhb, sv["h2"], sv["a2t"], sv["gu2"], sv["s2t"], lw["ffn2_norm"], lw["ffn2_w_in"], lw["ffn2_w_out"], f"b{l}")
    dy = _mm_nt(dhb, lw["w_out"], f"d_y{l}")
    g["w_out"] = _mm(sv["yt"], dhb, CDT, 512, 512, f"dw_out{l}")
    dpf, dps, dga, dgb = _gate_bwd(dy, sv["pf"], sv["ps"], sv["proj"], f"gate_bwd{l}")
    do_f, do_ft = _mm_nt(dpf, lw["w_branch_fox"], f"d_ofox{l}", with_t=True)
    do_s = _mm_nt(dps, lw["w_branch_swa"], f"d_oswa{l}")
    g["w_branch_fox"] = _mm(sv["oft"], dpf, CDT, 512, 512, f"dw_bfox{l}")
    g["w_branch_swa"] = _mm(sv["ost"], dps, CDT, 512, 512, f"dw_bswa{l}")
    dqf, dcq, dkf, dvf, dck, *rode = _fox_bwd(sv["qf"], sv["qft"], sv["kf"], sv["vf"], sv["c"], sv["ct"], sv["ofox"],
                                              sv["lse_f"], do_f, do_ft, f"fox_bwd{l}", ride)
    g["rode"] = rode
    dqs, dkse, dvse, dbias, dsk = _swa_bwd(sv["qs"], sv["kse"], sv["vse"], lw["bias"], lw["sinks"], sv["oswa"],
                                           sv["lse_s"], do_s, f"swa_bwd{l}")
    dproj, dgn = _qknorm_bwd(sv["proj"], dqf, dkf, dvf, dqs, dkse, dvse, dcq, dck, dga, dgb,
                             lw["gfq"], lw["gfk"], lw["gsq"], lw["gsk"], lw["fb"], f"qknorm_bwd{l}")
    g["w_mix"] = _mm(sv["amt"], dproj, CDT, 512, 640, f"dw_mix{l}")
    dh, dhb, g["mix_norm"] = _mm_nt_rms(dproj, lw["w_mix"], sv["h1"], lw["mix_norm"], dh, f"d_am{l}")
    g["dbias"], g["dsk"], g["dgn"] = dbias, dsk, dgn
    rides = before_ffn1(g) if before_ffn1 else None
    dh, dhb, g["ffn1_w_out"], g["ffn1_w_in"], g["ffn1_norm"], g["rode_ffn1"] = _ffn_bwd(
        dh, dhb, sv["h0"], sv["a1t"], sv["gu1"], sv["s1t"], lw["ffn1_norm"], lw["ffn1_w_in"], lw["ffn1_w_out"], f"a{l}",
        rides)
    return dh, dhb, g


def kernel(x, meta_tokens, rel_bias_table, ffn1_norm, ffn1_w_in, ffn1_w_out, mix_norm, w_in, forget_bias, fox_q_norm, fox_k_norm, swa_q_norm, swa_k_norm, swa_sinks, w_branch_fox, w_branch_swa, w_out, ffn2_norm, ffn2_w_in, ffn2_w_out, loss_target, m_meta_tokens, m_rel_bias_table, m_ffn1_norm, m_ffn1_w_in, m_ffn1_w_out, m_mix_norm, m_w_in, m_forget_bias, m_fox_q_norm, m_fox_k_norm, m_swa_q_norm, m_swa_k_norm, m_swa_sinks, m_w_branch_fox, m_w_branch_swa, m_w_out, m_ffn2_norm, m_ffn2_w_in, m_ffn2_w_out, v_meta_tokens, v_rel_bias_table, v_ffn1_norm, v_ffn1_w_in, v_ffn1_w_out, v_mix_norm, v_w_in, v_forget_bias, v_fox_q_norm, v_fox_k_norm, v_swa_q_norm, v_swa_k_norm, v_swa_sinks, v_w_branch_fox, v_w_branch_swa, v_w_out, v_ffn2_norm, v_ffn2_w_in, v_ffn2_w_out):
    names = ["meta_tokens", "rel_bias_table", "ffn1_norm", "ffn1_w_in", "ffn1_w_out", "mix_norm", "w_in", "forget_bias",
             "fox_q_norm", "fox_k_norm", "swa_q_norm", "swa_k_norm", "swa_sinks", "w_branch_fox", "w_branch_swa", "w_out",
             "ffn2_norm", "ffn2_w_in", "ffn2_w_out"]
    w = dict(zip(names, [meta_tokens, rel_bias_table, ffn1_norm, ffn1_w_in, ffn1_w_out, mix_norm, w_in, forget_bias,
                         fox_q_norm, fox_k_norm, swa_q_norm, swa_k_norm, swa_sinks, w_branch_fox, w_branch_swa, w_out,
                         ffn2_norm, ffn2_w_in, ffn2_w_out]))
    m = dict(zip(names, [m_meta_tokens, m_rel_bias_table, m_ffn1_norm, m_ffn1_w_in, m_ffn1_w_out, m_mix_norm, m_w_in,
                         m_forget_bias, m_fox_q_norm, m_fox_k_norm, m_swa_q_norm, m_swa_k_norm, m_swa_sinks,
                         m_w_branch_fox, m_w_branch_swa, m_w_out, m_ffn2_norm, m_ffn2_w_in, m_ffn2_w_out]))
    v = dict(zip(names, [v_meta_tokens, v_rel_bias_table, v_ffn1_norm, v_ffn1_w_in, v_ffn1_w_out, v_mix_norm, v_w_in,
                         v_forget_bias, v_fox_q_norm, v_fox_k_norm, v_swa_q_norm, v_swa_k_norm, v_swa_sinks,
                         v_w_branch_fox, v_w_branch_swa, v_w_out, v_ffn2_norm, v_ffn2_w_in, v_ffn2_w_out]))
    xi, yi, ci = lax.axis_index("x"), lax.axis_index("y"), lax.axis_index("c")
    shard = 2 * xi + yi
    seq = x.shape[1]
    t = seq + BLK

    wb = {nm: w[nm].astype(CDT) for nm, _, _ in SHARD_ITEMS}
    wb_list = [wb[nm] for nm, _, _ in SHARD_ITEMS]
    mflat = meta_tokens.reshape(META_ROWS, 128)
    first = (0, 1)
    *slabs_first, mall = _gather_layer([wb_list[t] for t in first], mflat, 0, "gather_weights", first)
    mall = lax.dynamic_update_slice(mall, mflat[None], (shard, 0, 0))
    meta_full = jnp.concatenate([mall[s].reshape(N_META, 256) for s in range(4)], axis=1)
    bias = _bias_fwd(rel_bias_table, "bias_fwd")

    def layer_weights(slabs, l, items=ALL_ITEMS):
        lw = _full_weights(slabs, wb, l, shard, items)
        if "w_in" in lw:
            lw["w_mix"] = _mix_cols(lw.pop("w_in"))
        return lw

    def layer_vectors(l):
        lw = {nm: w[nm][l].reshape(1, D) for nm in ("ffn1_norm", "mix_norm", "ffn2_norm")}
        lw["gfq"] = jnp.tile(fox_q_norm[l], 8).reshape(1, 512)
        lw["gfk"] = jnp.tile(fox_k_norm[l], 8).reshape(1, 512)
        lw["gsq"] = jnp.tile(swa_q_norm[l], 8).reshape(1, 512)
        lw["gsk"] = jnp.tile(swa_k_norm[l], 2).reshape(1, 128)
        lw["fb"] = jnp.pad(forget_bias[l], (0, 120)).reshape(1, 128)
        lw["sinks"] = swa_sinks[l]
        lw["bias"] = bias
        return lw

    def gather_ride(layer, items):
        return ("gather", [wb_list[t] for t in items], _slab_shapes(items), layer, items)

    landed = {}

    def need(lw, stage):
        if stage == "mixer":
            items = (2,)
            slabs = _forward_layer(landed["ffn_in_a"], "forward_halves0m", items)
        else:
            items = (3, 4, 5, 6, 7)
            slabs = _forward_layer(landed["ffn_out_a"] + landed["swa_fwd"] + landed["proj"], "forward_halves0g", items)
        lw.update(layer_weights(slabs, 0, items))

    late = {"rides": {"ffn_in_a": gather_ride(0, (2,)), "ffn_out_a": gather_ride(0, (3, 4, 5)),
                      "proj": gather_ride(0, (7,)), "swa_fwd": gather_ride(0, (6,))},
            "arrived": landed.__setitem__, "need": need}

    h = jnp.concatenate([jnp.zeros((PAD, D), F32), meta_full, x[0]], axis=0)
    lws = [{**layer_vectors(0), **layer_weights(slabs_first, 0, first)}]
    h, sv0, slabs1 = _layer_fwd(h, lws[0], 0, gather_ride(1, ALL_ITEMS), late)
    lws.append({**layer_vectors(1), **layer_weights(_forward_layer(slabs1, "forward_halves"), 1)})
    h, sv1, _ = _layer_fwd(h, lws[1], 1)
    saved = [sv0, sv1]
    dh, dhb, lacc = _loss(h, loss_target[0], "loss")
    loss = lax.psum(lacc[0, 0], ("x", "y", "c"))

    half_idx = ci.reshape(1).astype(jnp.int32)
    place_idx = jnp.stack([shard, ci]).astype(jnp.int32)

    def pair_sums(g, gsm, tag, items=ALL_ITEMS):
        if "w_mix" in g:
            g["w_in"] = _unmix_cols(g.pop("w_mix"))
        forms = _exchange_forms(g, items)
        got, slots = _swap_layer(forms, gsm, f"swap_halves{tag}", items)
        return {t: _pair_add_t(a, b, half_idx, SHARD_ITEMS[t][0], SHARD_ITEMS[t][1][0],
                               f"pair_add{tag}_{SHARD_ITEMS[t][0]}")
                for t, a, b in zip(items, forms, got)}, slots

    def scatter_ride(ps, items):
        return ("scatter", [ps[t] for t in items], _got3_shapes(items), None, items)

    early = (2, 3, 4, 5, 6, 7)
    early_rides = ((6,), (7,), (2, 5), (3, 4))
    ps0 = {}

    def before_ffn1(g):
        ps0.update(pair_sums(g, None, "0e", early)[0])
        return [scatter_ride(ps0, items) for items in early_rides]

    grads = [None, None]
    dh, dhb, grads[1] = _layer_bwd(dh, dhb, saved[1], lws[1], 1)
    ps1, _ = pair_sums(grads[1], None, 1)
    dh, dhb, grads[0] = _layer_bwd(dh, dhb, saved[0], lws[0], 0, scatter_ride(ps1, ALL_ITEMS), before_ffn1)
    grad_x = dh[BLK:].reshape(1, seq, D)
    dtab = _bias_bwd(grads[0]["dbias"] + grads[1]["dbias"], "bias_bwd")

    small = [dh[PAD:BLK].reshape(128, 128), _rows128(dtab[:, :N_BUCKETS].T, 2)]
    for nm in ("ffn1_norm", "mix_norm", "ffn2_norm"):
        small.append(jnp.stack([grads[l][nm][0] for l in range(2)]).reshape(16, 128))
    small.append(_rows128(jnp.stack([grads[l]["dgn"][4, :8] for l in range(2)]), 1))
    for row in range(4):
        small.append(jnp.stack([grads[l]["dgn"][row, :HD] for l in range(2)]).reshape(1, 128))
    dsk = [grads[l]["dsk"][:, 0, :] for l in range(2)]
    small.append(_rows128(jnp.stack([jnp.stack([d[:, 0], d[:, HD]], axis=1).reshape(8) for d in dsk]), 1))
    gsm = jnp.concatenate(small, axis=0)
    gsm = jnp.pad(gsm, ((0, SMALL_ROWS - gsm.shape[0]), (0, 0)))

    late = (0, 1)
    ps_late, slots = pair_sums(grads[0], gsm, "0l", late)
    ps0.update(ps_late)
    got3_0 = dict(zip([t for items in early_rides for t in items], grads[0]["rode_ffn1"]))
    got3_0.update(zip(late, _scatter_layer([ps0[t] for t in late], "scatter_shards", late)))
    got3 = [got3_0, dict(zip(ALL_ITEMS, grads[0]["rode"]))]
    bufs = []
    for t, (nm, (r, c), _) in enumerate(SHARD_ITEMS):
        buf = lax.empty((2, r, c), F32)
        for l, ps in ((1, ps1), (0, ps0)):
            buf = _sum4_t(ps[t], got3[l][t], buf, place_idx, l, nm, r, f"sum4_{l}_{nm}")
        bufs.append(buf)
    bufs = _join_layer(bufs, "join_halves")
    gs = _sum8(slots, "sum8")

    g_out = {nm: buf for (nm, _, _), buf in zip(SHARD_ITEMS, bufs)}
    g_out["meta_tokens"] = lax.dynamic_slice(gs[0:128].reshape(N_META, D), (0, shard * 256), (N_META, 256))
    off = 128
    for nm, rows in SMALL_ITEMS:
        n = w[nm].size
        g_out[nm] = gs[off:off + rows].reshape(-1)[:n].reshape(w[nm].shape)
        off += rows

    delta, new_m, new_v = {}, {}, {}
    for nm, _, _ in SHARD_ITEMS:
        if nm == "w_in":
            tr_ = lambda a: jnp.swapaxes(a, 1, 2)
            delta[nm], new_m[nm], new_v[nm] = (tr_(a) for a in _adamw3(tr_(w[nm]), tr_(g_out[nm]), tr_(m[nm]), tr_(v[nm]),
                                                                        f"adamw_{nm}"))
        else:
            delta[nm], new_m[nm], new_v[nm] = _adamw3(w[nm], g_out[nm], m[nm], v[nm], f"adamw_{nm}")
    small_names = ["meta_tokens"] + [nm for nm, _ in SMALL_ITEMS]
    small_rows = [META_ROWS] + [rows for _, rows in SMALL_ITEMS]

    def pack_small(src):
        buf = jnp.concatenate([_rows128(src[nm], rows) for nm, rows in zip(small_names, small_rows)], axis=0)
        return jnp.pad(buf, ((0, SMALL_ADAM_ROWS - buf.shape[0]), (0, 0)))

    d_, m_, v_ = (a[0] for a in _adamw3(pack_small(w)[None], pack_small(g_out)[None], pack_small(m)[None],
                                        pack_small(v)[None], "adamw_small"))
    off = 0
    for nm, rows in zip(small_names, small_rows):
        n = w[nm].size
        for dst, src in ((delta, d_), (new_m, m_), (new_v, v_)):
            dst[nm] = src[off:off + rows].reshape(-1)[:n].reshape(w[nm].shape)
        off += rows

    return (loss, grad_x, *[g_out[n] for n in names], *[delta[n] for n in names],
            *[new_m[n] for n in names], *[new_v[n] for n in names])
```

```python
import math

import numpy as np
import jax
import jax.numpy as jnp
from jax import lax
from jax.experimental import pallas as pl
from jax.experimental.pallas import tpu as pltpu

D = 1024
F = 2816
FT = F // 2
HD = 64
NPAIR = 4
N_META = 16
BLK = 128
PAD = BLK - N_META
EPS = 1e-6
NEG = -1e30
N_BUCKETS = 32
GA, GB, QA, KA, VA, QB, KB, VB, FA, DP = 0, 1024, 2048, 2560, 3072, 3584, 4096, 4224, 4352, 4480
D_IN = 4360
CDT = jnp.bfloat16
F32 = jnp.float32
VMEM_LIMIT = 48 * 1024 * 1024
MESH_ID = pl.DeviceIdType.MESH

ADAM_LR, ADAM_B1, ADAM_B2, ADAM_EPS, ADAM_WD, ADAM_STEP = 0.001, 0.9, 0.999, 1e-08, 0.01, 10

SHARD_ITEMS = (
    ("ffn1_w_in", (1024, 1408), "col"),
    ("ffn1_w_out", (704, 1024), "row"),
    ("w_in", (1024, 1090), "col"),
    ("w_branch_fox", (512, 256), "col"),
    ("w_branch_swa", (512, 256), "col"),
    ("w_out", (256, 1024), "row"),
    ("ffn2_w_in", (1024, 1408), "col"),
    ("ffn2_w_out", (704, 1024), "row"),
)
SMALL_ROWS = 192
META_ROWS = 32


def _row_tile(t):
    return 384 if t % 384 == 0 else 128


def _dot(a, b):
    return jnp.dot(a, b, preferred_element_type=F32)


def _dot_nt(a, b):
    return lax.dot_general(a, b, (((1,), (1,)), ((), ())), preferred_element_type=F32)


def _dot_hi(a, b):
    return jnp.dot(a, b, preferred_element_type=F32, precision=lax.Precision.HIGHEST)


def _sigmoid(x):
    return 0.5 * jnp.tanh(0.5 * x) + 0.5


def _iota(shape, dim):
    return lax.broadcasted_iota(jnp.int32, shape, dim)


def _params(sem=None):
    return pltpu.CompilerParams(dimension_semantics=sem, vmem_limit_bytes=VMEM_LIMIT)


def _sds(shape, dtype):
    return jax.ShapeDtypeStruct(shape, dtype)


def _rms_fwd(h, g, name):
    t = h.shape[0]
    tm = _row_tile(t)

    def body(h_ref, g_ref, a_ref, at_ref):
        x = h_ref[...]
        ms = jnp.mean(x * x, axis=-1, keepdims=True)
        a = x * lax.rsqrt(ms + EPS) * g_ref[...]
        a_ref[...] = a.astype(CDT)
        at_ref[...] = a.T.astype(CDT)

    return pl.pallas_call(
        body, name=name, grid=(t // tm,),
        in_specs=[pl.BlockSpec((tm, D), lambda i: (i, 0)), pl.BlockSpec((1, D), lambda i: (0, 0))],
        out_specs=[pl.BlockSpec((tm, D), lambda i: (i, 0)), pl.BlockSpec((D, tm), lambda i: (0, i))],
        out_shape=[_sds((t, D), CDT), _sds((D, t), CDT)],
        compiler_params=_params(("parallel",)),
    )(h, g)


def _ffn_in(a, w_in, name, ride=None):
    t = a.shape[0]
    tm = _row_tile(t)
    tn = FT
    nj = F // tn
    grid = (nj, t // tm)
    ride_in, ride_in_specs, ride_out, ride_out_specs, ride_sems = _ride_specs(ride)

    def body(a_ref, wg_ref, wu_ref, gu_ref, s_ref, st_ref):
        a_ = a_ref[...]
        g = _dot(a_, wg_ref[...])
        u = _dot(a_, wu_ref[...])
        s = g * _sigmoid(g) * u
        gu_ref[0] = g.astype(CDT)
        gu_ref[1] = u.astype(CDT)
        s_ref[...] = s.astype(CDT)
        st_ref[...] = s.T.astype(CDT)

    res = pl.pallas_call(
        _riding(body, 3, 3, ride, grid), name=name, grid=grid,
        in_specs=[pl.BlockSpec((tm, D), lambda j, i: (i, 0)),
                  pl.BlockSpec((D, tn), lambda j, i: (0, j)),
                  pl.BlockSpec((D, tn), lambda j, i: (0, j + nj))] + ride_in_specs,
        out_specs=[pl.BlockSpec((2, tm, tn), lambda j, i: (0, i, j)),
                   pl.BlockSpec((tm, tn), lambda j, i: (i, j)),
                   pl.BlockSpec((tn, tm), lambda j, i: (j, i))] + ride_out_specs,
        out_shape=[_sds((2, t, F), CDT), _sds((t, F), CDT), _sds((F, t), CDT)] + ride_out, scratch_shapes=ride_sems,
        compiler_params=_params(("arbitrary", "arbitrary") if ride else ("parallel", "parallel")),
    )(a, w_in, w_in, *ride_in)
    return (*res[:3], res[3:]) if ride else res


def _mm_res(a, b, res, scale, name, ride=None):
    t, k = a.shape
    n = b.shape[1]
    tm = _row_tile(t)
    tn = n
    grid = (t // tm, n // tn)
    ride_in, ride_in_specs, ride_out, ride_out_specs, ride_sems = _ride_specs(ride)

    def body(a_ref, b_ref, r_ref, o_ref):
        o_ref[...] = r_ref[...] + scale * _dot(a_ref[...], b_ref[...])

    out = pl.pallas_call(
        _riding(body, 3, 1, ride, grid), name=name, grid=grid,
        in_specs=[pl.BlockSpec((tm, k), lambda i, j: (i, 0)),
                  pl.BlockSpec((k, tn), lambda i, j: (0, j)),
                  pl.BlockSpec((tm, tn), lambda i, j: (i, j))] + ride_in_specs,
        out_specs=[pl.BlockSpec((tm, tn), lambda i, j: (i, j))] + ride_out_specs,
        out_shape=[_sds((t, n), F32)] + ride_out, scratch_shapes=ride_sems,
        compiler_params=_params(("arbitrary", "arbitrary") if ride else ("parallel", "parallel")),
    )(a, b, res, *ride_in)
    return (out[0], out[1:]) if ride else out[0]


def _mm(a, b, out_dtype, tm, tn, name, scale=1.0, ride=None):
    m, k = a.shape
    if b.ndim == 3:
        nh = b.shape[2] // tn
        n = 2 * b.shape[2]
        b_spec = pl.BlockSpec((None, k, tn), lambda i, j: (j // nh, 0, j % nh))
    else:
        n = b.shape[1]
        b_spec = pl.BlockSpec((k, tn), lambda i, j: (0, j))
    grid = (m // tm, n // tn)
    ride_in, ride_in_specs, ride_out, ride_out_specs, ride_sems = _ride_specs(ride)

    def body(a_ref, b_ref, o_ref):
        o_ref[...] = (scale * _dot(a_ref[...], b_ref[...])).astype(out_dtype)

    res = pl.pallas_call(
        _riding(body, 2, 1, ride, grid), name=name, grid=grid,
        in_specs=[pl.BlockSpec((tm, k), lambda i, j: (i, 0)), b_spec] + ride_in_specs,
        out_specs=[pl.BlockSpec((tm, tn), lambda i, j: (i, j))] + ride_out_specs,
        out_shape=[_sds((m, n), out_dtype)] + ride_out, scratch_shapes=ride_sems,
        compiler_params=_params(("arbitrary", "arbitrary") if ride else ("parallel", "parallel")),
    )(a, b, *ride_in)
    return (res[0], res[1:]) if ride else res[0]


def _mm_nt(a, b, name, with_t=False):
    m, n = a.shape
    k = b.shape[0]
    tm = _row_tile(m)
    tk = k

    def body(a_ref, b_ref, o_ref, *t_ref):
        r = _dot_nt(a_ref[...], b_ref[...])
        o_ref[...] = r
        if with_t:
            t_ref[0][...] = r.T.astype(CDT)

    out_specs = [pl.BlockSpec((tm, tk), lambda i, j: (i, j))]
    out_shape = [_sds((m, k), F32)]
    if with_t:
        out_specs.append(pl.BlockSpec((tk, tm), lambda i, j: (j, i)))
        out_shape.append(_sds((k, m), CDT))
    res = pl.pallas_call(
        body, name=name, grid=(m // tm, k // tk),
        in_specs=[pl.BlockSpec((tm, n), lambda i, j: (i, 0)), pl.BlockSpec((tk, n), lambda i, j: (j, 0))],
        out_specs=out_specs, out_shape=out_shape,
        compiler_params=_params(("parallel", "parallel")),
    )(a, b)
    return res if with_t else res[0]


def _ffn_bwd_mid(dhb, w_out, gu, name, ride=None):
    t = dhb.shape[0]
    tm = _row_tile(t)
    tn = FT
    grid = (F // tn, t // tm)
    ride_in, ride_in_specs, ride_out, ride_out_specs, ride_sems = _ride_specs(ride)

    def body(dh_ref, w_ref, gu_ref, o_ref):
        ds = _dot_nt(dh_ref[...] * 0.5, w_ref[...])
        g = gu_ref[0].astype(F32)
        u = gu_ref[1].astype(F32)
        sg = _sigmoid(g)
        o_ref[0] = (ds * u * (sg * (1.0 + g * (1.0 - sg)))).astype(CDT)
        o_ref[1] = (ds * (g * sg)).astype(CDT)

    res = pl.pallas_call(
        _riding(body, 3, 1, ride, grid), name=name, grid=grid,
        in_specs=[pl.BlockSpec((tm, D), lambda j, i: (i, 0)),
                  pl.BlockSpec((tn, D), lambda j, i: (j, 0)),
                  pl.BlockSpec((2, tm, tn), lambda j, i: (0, i, j))] + ride_in_specs,
        out_specs=[pl.BlockSpec((2, tm, tn), lambda j, i: (0, i, j))] + ride_out_specs,
        out_shape=[_sds((2, t, F), CDT)] + ride_out, scratch_shapes=ride_sems,
        compiler_params=_params(("arbitrary", "arbitrary") if ride else ("parallel", "parallel")),
    )(dhb, w_out, gu, *ride_in)
    return (res[0], res[1:]) if ride else res[0]


def _rms_bwd_rows(da_, x, g, dres, i, dh_ref, dhb_ref, dg_ref):
    r = lax.rsqrt(jnp.mean(x * x, axis=-1, keepdims=True) + EPS)
    xh = x * r
    day = da_ * g
    dh = dres + r * (day - xh * jnp.mean(day * xh, axis=-1, keepdims=True))
    dh_ref[...] = dh
    dhb_ref[...] = dh.astype(CDT)

    @pl.when(i == 0)
    def _():
        dg_ref[...] = jnp.zeros(dg_ref.shape, F32)

    dg_ref[0:1, :] += jnp.sum(da_ * xh, axis=0, keepdims=True)


def _ffn_bwd_in(dgu, w_in, h, g, dres, name, ride=None):
    t = dgu.shape[1]
    tm = _row_tile(t)
    grid = (t // tm,)
    ride_in, ride_in_specs, ride_out, ride_out_specs, ride_sems = _ride_specs(ride)

    def body(dg_ref, wg_ref, wu_ref, h_ref, g_ref, dr_ref, dh_ref, dhb_ref, dgn_ref):
        da_ = _dot_nt(dg_ref[0], wg_ref[...]) + _dot_nt(dg_ref[1], wu_ref[...])
        _rms_bwd_rows(da_, h_ref[...], g_ref[...], dr_ref[...], pl.program_id(0), dh_ref, dhb_ref, dgn_ref)

    row = pl.BlockSpec((tm, D), lambda i: (i, 0))
    res = pl.pallas_call(
        _riding(body, 6, 3, ride, grid), name=name, grid=grid,
        in_specs=[pl.BlockSpec((2, tm, F), lambda i: (0, i, 0)),
                  pl.BlockSpec((D, F), lambda i: (0, 0)),
                  pl.BlockSpec((D, F), lambda i: (0, 1)),
                  row, pl.BlockSpec((1, D), lambda i: (0, 0)), row] + ride_in_specs,
        out_specs=[row, row, pl.BlockSpec((8, D), lambda i: (0, 0))] + ride_out_specs,
        out_shape=[_sds((t, D), F32), _sds((t, D), CDT), _sds((8, D), F32)] + ride_out, scratch_shapes=ride_sems,
        compiler_params=_params(("arbitrary",)),
    )(dgu, w_in, w_in, h, g, dres, *ride_in)
    return (*res[:3], res[3:]) if ride else res


def _mm_nt_rms(a, b, h, g, dres, name):
    t, n = a.shape
    tm = _row_tile(t)

    def body(a_ref, b_ref, h_ref, g_ref, dr_ref, dh_ref, dhb_ref, dgn_ref):
        da_ = _dot_nt(a_ref[...], b_ref[...])
        _rms_bwd_rows(da_, h_ref[...], g_ref[...], dr_ref[...], pl.program_id(0), dh_ref, dhb_ref, dgn_ref)

    row = pl.BlockSpec((tm, D), lambda i: (i, 0))
    return pl.pallas_call(
        body, name=name, grid=(t // tm,),
        in_specs=[pl.BlockSpec((tm, n), lambda i: (i, 0)), pl.BlockSpec((D, n), lambda i: (0, 0)),
                  row, pl.BlockSpec((1, D), lambda i: (0, 0)), row],
        out_specs=[row, row, pl.BlockSpec((8, D), lambda i: (0, 0))],
        out_shape=[_sds((t, D), F32), _sds((t, D), CDT), _sds((8, D), F32)],
        compiler_params=_params(("arbitrary",)),
    )(a, b, h, g, dres)


def _loss(h, target, name):
    t = h.shape[0]

    def body(h_ref, t_ref, dh_ref, dhb_ref, l_ref):
        i = pl.program_id(0)

        @pl.when(i == 0)
        def _():
            l_ref[...] = jnp.zeros(l_ref.shape, F32)
            dh_ref[...] = jnp.zeros(dh_ref.shape, F32)
            dhb_ref[...] = jnp.zeros(dhb_ref.shape, CDT)

        @pl.when(i > 0)
        def _():
            err = h_ref[...] - t_ref[...]
            l_ref[...] += (0.5 / D) * jnp.sum(err * err)
            d = err * (1.0 / D)
            dh_ref[...] = d
            dhb_ref[...] = d.astype(CDT)

    row = pl.BlockSpec((BLK, D), lambda i: (i, 0))
    return pl.pallas_call(
        body, name=name, grid=(t // BLK,),
        in_specs=[row, pl.BlockSpec((BLK, D), lambda i: (jnp.maximum(i - 1, 0), 0))],
        out_specs=[row, row, pl.BlockSpec((8, 128), lambda i: (0, 0))],
        out_shape=[_sds((t, D), F32), _sds((t, D), CDT), _sds((8, 128), F32)],
        compiler_params=_params(("arbitrary",)),
    )(h, target)


def _block_diag():
    return (_iota((128, 128), 0) // HD == _iota((128, 128), 1) // HD).astype(F32)


def _head_sums(v, bd):
    hi = v.astype(CDT)
    rest = (v - hi.astype(F32)).astype(CDT)
    b = bd.astype(CDT)
    return _dot(hi, b) + _dot(rest, b)


def _dup_halves(x, lo):
    sw = pltpu.roll(x, 64, 1)
    return jnp.where(lo, x, sw), jnp.where(lo, sw, x)


def _qknorm_fwd(proj, gfq, gfk, gsq, gsk, fb, name):
    t = proj.shape[0]
    tm = _row_tile(t)

    def body(qa, ka, va, qb, kb, vb, fa, gfq_r, gfk_r, gsq_r, gsk_r, fb_r,
             qf_o, kf_o, vf_o, qs_o, kse_o, vse_o, c_o, ct_o, qft_o, carry):
        i = pl.program_id(0)
        bd = _block_diag()
        lane = _iota((1, 128), 1)
        lo = lane < HD

        def hnorm(x, g):
            ms = _head_sums(x * x, bd) * (1.0 / HD)
            return x * lax.rsqrt(ms + EPS) * g

        for ch in range(4):
            sl = slice(128 * ch, 128 * (ch + 1))
            qn = hnorm(qa[:, sl], gfq_r[:, sl]) * 0.125
            qf_o[:, sl] = qn.astype(CDT)
            qft_o[sl, :] = qn.T.astype(CDT)
            kf_o[:, sl] = hnorm(ka[:, sl], gfk_r[:, sl]).astype(CDT)
            qs_o[:, sl] = (hnorm(qb[:, sl], gsq_r[:, sl]) * 0.125).astype(CDT)
        vf_o[...] = va[...].astype(CDT)
        k0, k1 = _dup_halves(hnorm(kb[...], gsk_r[...]), lo)
        kse_o[0] = k0.astype(CDT)
        kse_o[1] = k1.astype(CDT)
        v0, v1 = _dup_halves(vb[...], lo)
        vse_o[0] = v0.astype(CDT)
        vse_o[1] = v1.astype(CDT)

        z = fa[...] + fb_r[...]
        lf = jnp.minimum(z, 0.0) - jnp.log(1.0 + jnp.exp(-jnp.abs(z)))
        lf = jnp.where(lane < 8, lf, 0.0)
        ltri = (_iota((tm, tm), 1) <= _iota((tm, tm), 0)).astype(F32)

        @pl.when(i == 0)
        def _():
            carry[...] = jnp.zeros(carry.shape, F32)

        c = _dot_hi(ltri, lf) + carry[0:1, :]
        carry[0:1, :] = c[tm - 1:tm, :]
        c_o[...] = c
        ct_o[...] = c.T[0:8, :]

    def col(width, off):
        return pl.BlockSpec((tm, width), lambda i: (i, off // width))

    def vec(width):
        return pl.BlockSpec((1, width), lambda i: (0, 0))

    return pl.pallas_call(
        body, name=name, grid=(t // tm,),
        in_specs=[col(512, QA), col(512, KA), col(512, VA), col(512, QB), col(128, KB), col(128, VB), col(128, FA),
                  vec(512), vec(512), vec(512), vec(128), vec(128)],
        out_specs=[pl.BlockSpec((tm, 512), lambda i: (i, 0))] * 4
        + [pl.BlockSpec((2, tm, 128), lambda i: (0, i, 0))] * 2
        + [pl.BlockSpec((tm, 128), lambda i: (i, 0)), pl.BlockSpec((8, tm), lambda i: (0, i)),
           pl.BlockSpec((512, tm), lambda i: (0, i))],
        out_shape=[_sds((t, 512), CDT)] * 4 + [_sds((2, t, 128), CDT)] * 2
        + [_sds((t, 128), F32), _sds((8, t), F32), _sds((512, t), CDT)],
        scratch_shapes=[pltpu.VMEM((8, 128), F32)],
        compiler_params=_params(("arbitrary",)),
    )(proj, proj, proj, proj, proj, proj, proj, gfq, gfk, gsq, gsk, fb)


def _qknorm_bwd(proj, dqf, dkf, dvf, dqs, dkse, dvse, dcq, dck, dga, dgb, gfq, gfk, gsq, gsk, fb, name):
    t = proj.shape[0]
    tm = _row_tile(t)
    nt = t // tm

    def body(qa, ka, qb, kb, fa, dqf_r, dkf_r, dvf_r, dqs_r, dkse_r, dvse_r, dcq_r, dck_r, dga_r, dgb_r,
             gfq_r, gfk_r, gsq_r, gsk_r, fb_r, dp_o, dgn_o, carry, acc):
        i = pl.program_id(0)
        bd = _block_diag()
        lane = _iota((1, 128), 1)
        lo = lane < HD

        @pl.when(i == 0)
        def _():
            carry[...] = jnp.zeros(carry.shape, F32)
            acc[...] = jnp.zeros(acc.shape, F32)

        def hnorm_bwd(x, g, dy):
            r = lax.rsqrt(_head_sums(x * x, bd) * (1.0 / HD) + EPS)
            xh = x * r
            day = dy * g
            dx = r * (day - xh * (_head_sums(day * xh, bd) * (1.0 / HD)))
            return dx, jnp.sum(dy * xh, axis=0, keepdims=True)

        for ch in range(4):
            sl = slice(128 * ch, 128 * (ch + 1))
            dx, dg = hnorm_bwd(qa[:, sl], gfq_r[:, sl], dqf_r[:, sl] * 0.125)
            dp_o[:, QA + 128 * ch:QA + 128 * (ch + 1)] = dx.astype(CDT)
            acc[0:1, sl] += dg
            dx, dg = hnorm_bwd(ka[:, sl], gfk_r[:, sl], dkf_r[:, sl])
            dp_o[:, KA + 128 * ch:KA + 128 * (ch + 1)] = dx.astype(CDT)
            acc[1:2, sl] += dg
            dx, dg = hnorm_bwd(qb[:, sl], gsq_r[:, sl], dqs_r[:, sl] * 0.125)
            dp_o[:, QB + 128 * ch:QB + 128 * (ch + 1)] = dx.astype(CDT)
            acc[2:3, sl] += dg
        dp_o[:, VA:VA + 512] = dvf_r[...].astype(CDT)
        dp_o[:, GA:GA + D] = dga_r[...]
        dp_o[:, GB:GB + D] = dgb_r[...]

        def fold(x):
            e0 = x[0]
            e1 = x[1]
            return jnp.where(lo, e0 + pltpu.roll(e0, 64, 1), e1 + pltpu.roll(e1, 64, 1))

        dx, dg = hnorm_bwd(kb[...], gsk_r[...], fold(dkse_r))
        dp_o[:, KB:KB + 128] = dx.astype(CDT)
        acc[3:4, 0:128] += dg
        dp_o[:, VB:VB + 128] = fold(dvse_r).astype(CDT)

        rr = _iota((512, 128), 0)
        hh = _iota((512, 128), 1)
        sel = ((rr == (hh >> 1) * 128 + (hh & 1) * HD) & (hh < 8)).astype(F32)
        dcs = _dot_hi(dcq_r[...] - dck_r[...], sel)
        utri = (_iota((tm, tm), 1) >= _iota((tm, tm), 0)).astype(F32)
        dlf = _dot_hi(utri, dcs) + carry[0:1, :]
        carry[0:1, :] = dlf[0:1, :]
        z = fa[...] + fb_r[...]
        dfa = jnp.where(lane < 8, dlf * _sigmoid(-z), 0.0)
        dp_o[:, FA:FA + 128] = dfa.astype(CDT)
        acc[4:5, 0:128] += jnp.sum(dfa, axis=0, keepdims=True)

        @pl.when(i == nt - 1)
        def _():
            foldm = ((_iota((512, 128), 0) & (HD - 1)) == _iota((512, 128), 1)).astype(F32)
            dgn_o[...] = _dot_hi(acc[...], foldm)

    def col(width, off):
        return pl.BlockSpec((tm, width), lambda i: (nt - 1 - i, off // width))

    def rows(width):
        return pl.BlockSpec((tm, width), lambda i: (nt - 1 - i, 0))

    def vec(width):
        return pl.BlockSpec((1, width), lambda i: (0, 0))

    pair = pl.BlockSpec((2, tm, 128), lambda i: (0, nt - 1 - i, 0))
    return pl.pallas_call(
        body, name=name, grid=(nt,),
        in_specs=[col(512, QA), col(512, KA), col(512, QB), col(128, KB), col(128, FA),
                  rows(512), rows(512), rows(512), rows(512), pair, pair, rows(512), rows(512), rows(D), rows(D),
                  vec(512), vec(512), vec(512), vec(128), vec(128)],
        out_specs=[rows(DP), pl.BlockSpec((8, 128), lambda i: (0, 0))],
        out_shape=[_sds((t, DP), CDT), _sds((8, 128), F32)],
        scratch_shapes=[pltpu.VMEM((8, 128), F32), pltpu.VMEM((8, 512), F32)],
        compiler_params=_params(("arbitrary",)),
    )(proj, proj, proj, proj, proj, dqf, dkf, dvf, dqs, dkse, dvse, dcq, dck, dga, dgb, gfq, gfk, gsq, gsk, fb)


def _gate_fwd(ofox, oswa, wbf, wbs, proj, name):
    t = ofox.shape[0]
    tm = _row_tile(t)
    tn = 512

    def body(of_r, os_r, wf_r, ws_r, ga_r, gb_r, y_o, yt_o, pf_o, ps_o, oft_o, ost_o):
        j = pl.program_id(1)
        pf = _dot(of_r[...], wf_r[...])
        ps = _dot(os_r[...], ws_r[...])
        y = _sigmoid(ga_r[...]) * pf + _sigmoid(gb_r[...]) * ps
        y_o[...] = y.astype(CDT)
        yt_o[...] = y.T.astype(CDT)
        pf_o[...] = pf.astype(CDT)
        ps_o[...] = ps.astype(CDT)

        @pl.when(j == 0)
        def _():
            oft_o[...] = of_r[...].astype(F32).T.astype(CDT)
            ost_o[...] = os_r[...].astype(F32).T.astype(CDT)

    tile = pl.BlockSpec((tm, tn), lambda i, j: (i, j))
    return pl.pallas_call(
        body, name=name, grid=(t // tm, D // tn),
        in_specs=[pl.BlockSpec((tm, 512), lambda i, j: (i, 0)), pl.BlockSpec((tm, 512), lambda i, j: (i, 0)),
                  pl.BlockSpec((512, tn), lambda i, j: (0, j)), pl.BlockSpec((512, tn), lambda i, j: (0, j)),
                  pl.BlockSpec((tm, tn), lambda i, j: (i, GA // tn + j)),
                  pl.BlockSpec((tm, tn), lambda i, j: (i, GB // tn + j))],
        out_specs=[tile, pl.BlockSpec((tn, tm), lambda i, j: (j, i)), tile, tile,
                   pl.BlockSpec((512, tm), lambda i, j: (0, i)), pl.BlockSpec((512, tm), lambda i, j: (0, i))],
        out_shape=[_sds((t, D), CDT), _sds((D, t), CDT), _sds((t, D), CDT), _sds((t, D), CDT),
                   _sds((512, t), CDT), _sds((512, t), CDT)],
        compiler_params=_params(("parallel", "arbitrary")),
    )(ofox, oswa, wbf, wbs, proj, proj)


def _gate_bwd(dy, pf, ps, proj, name):
    t = dy.shape[0]
    tm = _row_tile(t)
    tn = 512

    def body(dy_r, pf_r, ps_r, ga_r, gb_r, dpf_o, dps_o, dga_o, dgb_o):
        dy_ = dy_r[...]
        sa = _sigmoid(ga_r[...])
        sb = _sigmoid(gb_r[...])
        dpf_o[...] = (dy_ * sa).astype(CDT)
        dps_o[...] = (dy_ * sb).astype(CDT)
        dga_o[...] = (dy_ * pf_r[...].astype(F32) * (sa * (1.0 - sa))).astype(CDT)
        dgb_o[...] = (dy_ * ps_r[...].astype(F32) * (sb * (1.0 - sb))).astype(CDT)

    tile = pl.BlockSpec((tm, tn), lambda i, j: (i, j))
    return pl.pallas_call(
        body, name=name, grid=(t // tm, D // tn),
        in_specs=[tile, tile, tile,
                  pl.BlockSpec((tm, tn), lambda i, j: (i, GA // tn + j)),
                  pl.BlockSpec((tm, tn), lambda i, j: (i, GB // tn + j))],
        out_specs=[tile] * 4,
        out_shape=[_sds((t, D), CDT)] * 4,
        compiler_params=_params(("parallel", "parallel")),
    )(dy, pf, ps, proj, proj)


def _tri_steps(n, by_key):
    if by_key:
        pairs = [(i, j) for j in range(n) for i in range(j, n)]
    else:
        pairs = [(i, j) for i in range(n) for j in range(i + 1)]
    return (np.array([p[0] for p in pairs], np.int32), np.array([p[1] for p in pairs], np.int32))


def _head_col(blk, lane, h):
    return jnp.sum(jnp.where(lane == h, blk, 0.0), axis=1, keepdims=True)


def _head_row(blk, sub, h):
    return jnp.sum(jnp.where(sub == h, blk, 0.0), axis=0, keepdims=True)


def _ride_specs(ride):
    if ride is None:
        return [], [], [], [], []
    kind, srcs, outs, layer, items = ride
    return list(srcs), [ANY] * len(srcs), list(outs), [ANY] * len(outs), _dma_sems(3 * len(srcs))


def _ride_start(ride, srcs, dsts, send_sems, recv_sems):
    for cp in _ici_copies(ride[0], srcs, dsts, send_sems, recv_sems, ride[3], recv=False, items=ride[4])[0]:
        cp.start()


def _ride_wait(ride, srcs, dsts, send_sems, recv_sems):
    sends, recvs = _ici_copies(ride[0], srcs, dsts, send_sems, recv_sems, ride[3], items=ride[4])
    for cp in recvs:
        cp.wait_recv()
    for cp in sends:
        cp.wait_send()


def _riding(body, n_in, n_out, ride, grid):
    if ride is None:
        return body
    nr = len(ride[1])

    def wrapped(*refs):
        ins, srcs = refs[:n_in], refs[n_in:n_in + nr]
        outs, dsts = refs[n_in + nr:n_in + nr + n_out], refs[n_in + nr + n_out:n_in + 2 * nr + n_out]
        scratch, sems = refs[n_in + 2 * nr + n_out:-2], refs[-2:]
        first = pl.program_id(0) == 0
        last = pl.program_id(0) == grid[0] - 1
        for a in range(1, len(grid)):
            first = first & (pl.program_id(a) == 0)
            last = last & (pl.program_id(a) == grid[a] - 1)

        @pl.when(first)
        def _():
            _ride_start(ride, srcs, dsts, *sems)

        body(*ins, *outs, *scratch)

        @pl.when(last)
        def _():
            _ride_wait(ride, srcs, dsts, *sems)

    return wrapped


def _fox_fwd(qf, kf, vf, c, ct, name, ride=None):
    t = qf.shape[0]
    ta = _row_tile(t)
    qi, kj = _tri_steps(t // ta, by_key=False)
    nsteps = len(qi)
    ride_in, ride_in_specs, ride_out, ride_out_specs, ride_sems = _ride_specs(ride)

    def body(qi_r, kj_r, q_r, k_r, v_r, c_r, ct_r, *rest):
        nr = len(ride_in)
        src_r, (o_o, lse_o), dst_o = rest[:nr], rest[nr:nr + 2], rest[nr + 2:2 * nr + 2]
        m_sc, l_sc, acc_sc, cq_sc, *sems = rest[2 * nr + 2:]
        p = pl.program_id(0)
        n = pl.program_id(1)
        i = qi_r[n]
        j = kj_r[n]
        lane = _iota((1, 128), 1)
        lo = lane < HD

        if ride is not None:
            @pl.when((p == 0) & (n == 0))
            def _():
                _ride_start(ride, src_r, dst_o, *sems)

        @pl.when(j == 0)
        def _():
            m_sc[...] = jnp.full(m_sc.shape, NEG, F32)
            l_sc[...] = jnp.zeros(l_sc.shape, F32)
            acc_sc[...] = jnp.zeros(acc_sc.shape, F32)
            for e in (0, 1):
                cq_sc[e] = jnp.broadcast_to(_head_col(c_r[...], lane, 2 * p + e), (ta, 128))

        def step(masked):
            q = q_r[...]
            k = k_r[...]
            vaug = jnp.concatenate([v_r[...], jnp.ones((ta, 128), CDT)], axis=1)
            if masked:
                rows = i * ta + _iota((ta, 1), 0)
                cols = j * ta + _iota((1, ta), 1)
                mask = (cols <= rows) & (cols >= PAD)
            sub = _iota((8, 1), 0)
            alphas, pvs = [], []
            for e in (0, 1):
                sel = lo if e == 0 else jnp.logical_not(lo)
                s = _dot_nt(jnp.where(sel, q, 0), k)
                ck = _head_row(ct_r[...], sub, 2 * p + e)
                cq = cq_sc[e]
                chunks = []
                for ch in range(ta // 128):
                    sl = slice(128 * ch, 128 * (ch + 1))
                    sc = s[:, sl] + cq - ck[:, sl]
                    if masked:
                        sc = jnp.where(mask[:, sl], sc, NEG)
                    chunks.append(sc)
                mx = chunks[0]
                for sc in chunks[1:]:
                    mx = jnp.maximum(mx, sc)
                m_prev = m_sc[e]
                m_new = jnp.maximum(m_prev, jnp.max(mx, axis=1, keepdims=True))
                alpha = jnp.exp(m_prev - m_new)
                pe = jnp.concatenate([jnp.exp(sc - m_new).astype(CDT) for sc in chunks], axis=1)
                pva = _dot(pe, vaug)
                l_sc[e] = alpha * l_sc[e] + pva[:, 128:]
                m_sc[e] = m_new
                alphas.append(alpha)
                pvs.append(pva[:, :128])
            acc_sc[...] = acc_sc[...] * jnp.where(lo, alphas[0], alphas[1]) + jnp.where(lo, pvs[0], pvs[1])

        edge = (j == i) | (j == 0)

        @pl.when(edge)
        def _():
            step(True)

        @pl.when(jnp.logical_not(edge))
        def _():
            step(False)

        @pl.when(j == i)
        def _():
            l = jnp.where(lo, l_sc[0], l_sc[1])
            o_o[...] = (acc_sc[...] / l).astype(CDT)
            lse_o[...] = jnp.where(lo, m_sc[0], m_sc[1]) + jnp.log(l)

        if ride is not None:
            @pl.when((p == NPAIR - 1) & (n == nsteps - 1))
            def _():
                _ride_wait(ride, src_r, dst_o, *sems)

    qblk = pl.BlockSpec((ta, 128), lambda p, n, qi_r, kj_r: (qi_r[n], p))
    kblk = pl.BlockSpec((ta, 128), lambda p, n, qi_r, kj_r: (kj_r[n], p))
    grid_spec = pltpu.PrefetchScalarGridSpec(
        num_scalar_prefetch=2, grid=(NPAIR, nsteps),
        in_specs=[qblk, kblk, kblk,
                  pl.BlockSpec((ta, 128), lambda p, n, qi_r, kj_r: (qi_r[n], 0)),
                  pl.BlockSpec((8, ta), lambda p, n, qi_r, kj_r: (0, kj_r[n]))] + ride_in_specs,
        out_specs=[qblk, qblk] + ride_out_specs,
        scratch_shapes=[pltpu.VMEM((2, ta, 128), F32), pltpu.VMEM((2, ta, 128), F32), pltpu.VMEM((ta, 128), F32),
                        pltpu.VMEM((2, ta, 128), F32)] + ride_sems,
    )
    return pl.pallas_call(
        body, name=name, grid_spec=grid_spec,
        out_shape=[_sds((t, 512), CDT), _sds((t, 512), F32)] + ride_out,
        compiler_params=_params(("arbitrary", "arbitrary")),
    )(jnp.asarray(qi), jnp.asarray(kj), qf, kf, vf, c, ct, *ride_in)


def _fox_bwd(qf, qft, kf, vf, c, ct, o, lse, do, dot, name, ride=None):
    t = qf.shape[0]
    ta = _row_tile(t)
    nq = t // ta
    qi, kj = _tri_steps(nq, by_key=False)
    nsteps = len(qi)
    ride_in, ride_in_specs, ride_out, ride_out_specs, ride_sems = _ride_specs(ride)

    def body(qi_r, kj_r, q_r, qt_r, k_r, v_r, c_r, ct_r, o_r, lse_r, do_r, dot_r, *rest):
        nr = len(ride_in)
        src_r, (dq_o, dcq_o, dk_o, dv_o, dck_o), dst_o = rest[:nr], rest[nr:nr + 5], rest[nr + 5:2 * nr + 5]
        lse_sc, dl_sc, cq_sc, dq_sc, dcq_sc, dkt_sc, dvt_sc, dckt_sc, *sems = rest[2 * nr + 5:]
        p = pl.program_id(0)
        n = pl.program_id(1)
        i = qi_r[n]
        j = kj_r[n]
        lane = _iota((1, 128), 1)
        lo = lane < HD
        top = _iota((128, 1), 0) < HD

        if ride is not None:
            @pl.when((p == 0) & (n == 0))
            def _():
                _ride_start(ride, src_r, dst_o, *sems)

        @pl.when(n == 0)
        def _():
            dkt_sc[...] = jnp.zeros(dkt_sc.shape, F32)
            dvt_sc[...] = jnp.zeros(dvt_sc.shape, F32)
            dckt_sc[...] = jnp.zeros(dckt_sc.shape, F32)

        @pl.when(j == 0)
        def _():
            dq_sc[...] = jnp.zeros(dq_sc.shape, F32)
            dcq_sc[...] = jnp.zeros(dcq_sc.shape, F32)
            dd = do_r[...] * o_r[...].astype(F32)
            lse = lse_r[...]
            for e in (0, 1):
                sel = lo if e == 0 else jnp.logical_not(lo)
                cq_sc[e] = jnp.broadcast_to(_head_col(c_r[...], lane, 2 * p + e), (ta, 128))
                dl_sc[e] = jnp.broadcast_to(jnp.sum(jnp.where(sel, dd, 0.0), axis=1, keepdims=True), (ta, 128))
                lse_sc[e] = jnp.broadcast_to(lse[:, HD * e:HD * e + 1], (ta, 128))

        def step(masked):
            q = q_r[...]
            qt = qt_r[...]
            k = k_r[...]
            v = v_r[...]
            dob = do_r[...].astype(CDT)
            dot_ = dot_r[...]
            ones = jnp.ones((ta, 128), CDT)
            ones16 = jnp.ones((16, ta), CDT)
            if masked:
                rows = i * ta + _iota((ta, 1), 0)
                cols = j * ta + _iota((1, ta), 1)
                mask = (cols <= rows) & (cols >= PAD)
            sub = _iota((8, 1), 0)
            for e in (0, 1):
                sel = lo if e == 0 else jnp.logical_not(lo)
                rsel = top if e == 0 else jnp.logical_not(top)
                s = _dot_nt(jnp.where(sel, q, 0), k)
                dp = _dot_nt(jnp.where(sel, dob, 0), v)
                ck = _head_row(ct_r[...], sub, 2 * p + e)
                cq, lse_e, dl = cq_sc[e], lse_sc[e], dl_sc[e]
                prs, dss = [], []
                for ch in range(ta // 128):
                    sl = slice(128 * ch, 128 * (ch + 1))
                    sc = s[:, sl] + cq - ck[:, sl]
                    if masked:
                        sc = jnp.where(mask[:, sl], sc, NEG)
                    pr = jnp.exp(sc - lse_e)
                    prs.append(pr.astype(CDT))
                    dss.append((pr * (dp[:, sl] - dl)).astype(CDT))
                pb = jnp.concatenate(prs, axis=1)
                dsb = jnp.concatenate(dss, axis=1)
                dvt_sc[j] += _dot(jnp.where(rsel, dot_, 0), pb)
                dkc = _dot(jnp.concatenate([jnp.where(rsel, qt, 0), ones16], axis=0), dsb)
                dkt_sc[j] += dkc[0:128]
                dckt_sc[j, 0:8, :] += jnp.where(sub == e, dkc[128:136], 0.0)
                dqa = _dot(dsb, jnp.concatenate([jnp.where(sel, k, 0), ones], axis=1))
                dq_sc[...] += dqa[:, :128]
                dcq_sc[e] += dqa[:, 128:]

        edge = (j == i) | (j == 0)

        @pl.when(edge)
        def _():
            step(True)

        @pl.when(jnp.logical_not(edge))
        def _():
            step(False)

        @pl.when(j == i)
        def _():
            dq_o[...] = dq_sc[...]
            dcq_o[...] = jnp.where(lo, dcq_sc[0], dcq_sc[1])

        @pl.when(n == nsteps - 1)
        def _():
            spread = (_iota((128, 128), 1) == _iota((128, 128), 0) // HD).astype(F32)
            for jb in range(nq):
                rs = slice(jb * ta, (jb + 1) * ta)
                dk_o[rs, :] = dkt_sc[jb].T
                dv_o[rs, :] = dvt_sc[jb].T
                dck_o[rs, :] = _dot_hi(spread, dckt_sc[jb]).T

        if ride is not None:
            @pl.when((p == NPAIR - 1) & (n == nsteps - 1))
            def _():
                _ride_wait(ride, src_r, dst_o, *sems)

    qblk = pl.BlockSpec((ta, 128), lambda p, n, qi_r, kj_r: (qi_r[n], p))
    qtblk = pl.BlockSpec((128, ta), lambda p, n, qi_r, kj_r: (p, qi_r[n]))
    kblk = pl.BlockSpec((ta, 128), lambda p, n, qi_r, kj_r: (kj_r[n], p))
    whole = pl.BlockSpec((t, 128), lambda p, n, qi_r, kj_r: (0, p))
    grid_spec = pltpu.PrefetchScalarGridSpec(
        num_scalar_prefetch=2, grid=(NPAIR, nsteps),
        in_specs=[qblk, qtblk, kblk, kblk,
                  pl.BlockSpec((ta, 128), lambda p, n, qi_r, kj_r: (qi_r[n], 0)),
                  pl.BlockSpec((8, ta), lambda p, n, qi_r, kj_r: (0, kj_r[n])),
                  qblk, qblk, qblk, qtblk] + ride_in_specs,
        out_specs=[qblk, qblk, whole, whole, whole] + ride_out_specs,
        scratch_shapes=[pltpu.VMEM((2, ta, 128), F32)] * 3 + [pltpu.VMEM((ta, 128), F32), pltpu.VMEM((2, ta, 128), F32)]
        + [pltpu.VMEM((nq, 128, ta), F32)] * 3 + ride_sems,
    )
    return pl.pallas_call(
        body, name=name, grid_spec=grid_spec,
        out_shape=[_sds((t, 512), F32)] * 5 + ride_out,
        compiler_params=_params(("arbitrary", "arbitrary")),
    )(jnp.asarray(qi), jnp.asarray(kj), qf, qft, kf, vf, c, ct, o, lse, do, dot, *ride_in)


def _bucket_table():
    r = np.arange(BLK)[:, None]
    c = np.arange(3 * BLK)[None, :]
    d = np.where(c < BLK, r + BLK - c, r - (c - BLK))
    n = np.maximum(d, 0)
    max_exact = N_BUCKETS // 2
    nf = np.maximum(n, 1).astype(np.float32)
    large = max_exact + (np.log(nf / max_exact) / math.log(BLK / max_exact) * (N_BUCKETS - max_exact)).astype(np.int32)
    large = np.minimum(large, N_BUCKETS - 1)
    b = np.where(n < max_exact, n, large)
    return np.where(c < 2 * BLK, b, N_BUCKETS - 1).astype(np.int32)


def _bias_fwd(table, name):
    bucket = jnp.asarray(_bucket_table())

    def body(tab_r, b_r, o_o):
        h = pl.program_id(0)
        b = b_r[...]
        acc = jnp.zeros(b.shape, F32)
        for k in range(N_BUCKETS):
            acc = jnp.where(b == k, tab_r[k, h], acc)
        o_o[...] = acc

    return pl.pallas_call(
        body, name=name, grid=(8,),
        in_specs=[pl.BlockSpec(memory_space=pltpu.SMEM), pl.BlockSpec((BLK, 3 * BLK), lambda h: (0, 0))],
        out_specs=pl.BlockSpec((None, BLK, 3 * BLK), lambda h: (h, 0, 0)),
        out_shape=_sds((8, BLK, 3 * BLK), F32),
        compiler_params=_params(("parallel",)),
    )(table, bucket)


def _bias_bwd(dbias, name):
    bucket = jnp.asarray(_bucket_table())

    def body(d_r, b_r, o_o):
        h = pl.program_id(0)
        b = b_r[...]
        d = d_r[...]
        lane = _iota((1, 128), 1)
        row = jnp.zeros((1, 128), F32)
        for k in range(N_BUCKETS):
            row = jnp.where(lane == k, jnp.sum(jnp.where(b == k, d, 0.0)), row)
        o_o[pl.ds(h, 1), :] = row

    return pl.pallas_call(
        body, name=name, grid=(8,),
        in_specs=[pl.BlockSpec((None, BLK, 3 * BLK), lambda h: (h, 0, 0)), pl.BlockSpec((BLK, 3 * BLK), lambda h: (0, 0))],
        out_specs=pl.BlockSpec((8, 128), lambda h: (0, 0)),
        out_shape=_sds((8, 128), F32),
        compiler_params=_params(("arbitrary",)),
    )(dbias, bucket)


def _swa_valid(i):
    r = _iota((BLK, 1), 0)
    c = _iota((1, 3 * BLK), 1)
    prev = (c < BLK) & (c > r) & (i >= 1) & ((i - 1) * BLK + c >= PAD)
    cc = c - BLK
    cur = (c >= BLK) & (c < 2 * BLK) & (cc <= r) & (i * BLK + cc >= PAD)
    cm = c - 2 * BLK
    meta = (c >= 2 * BLK) & (cm >= PAD) & (i * BLK + r - cm >= BLK)
    return prev | cur | meta


def _swa_kv_specs(ta):
    nb = ta // BLK
    return [pl.BlockSpec((None, BLK, 128), lambda p, i: (p // 2, jnp.maximum(i * nb - 1, 0), 0)),
            pl.BlockSpec((None, ta, 128), lambda p, i: (p // 2, i, 0)),
            pl.BlockSpec((None, BLK, 128), lambda p, i: (p // 2, 0, 0))]


def _swa_fwd(qs, kse, vse, bias, sinks, name, ride=None):
    t = qs.shape[0]
    ta = _row_tile(t)
    nb = ta // BLK
    grid = (NPAIR, t // ta)
    ride_in, ride_in_specs, ride_out, ride_out_specs, ride_sems = _ride_specs(ride)

    def body(sink_r, q_r, kp_r, kc_r, km_r, vp_r, vc_r, vm_r, b_r, o_o, lse_o):
        p = pl.program_id(0)
        i = pl.program_id(1)
        lo = _iota((1, 128), 1) < HD
        k4 = jnp.concatenate([kp_r[...], kc_r[...]], axis=0)
        v4 = jnp.concatenate([vp_r[...], vc_r[...]], axis=0)
        for b in range(nb):
            rows = slice(BLK * b, BLK * (b + 1))
            q = q_r[rows, :]
            k3 = jnp.concatenate([k4[BLK * b:BLK * (b + 2)], km_r[...]], axis=0)
            v3 = jnp.concatenate([v4[BLK * b:BLK * (b + 2)], vm_r[...]], axis=0)
            valid = _swa_valid(i * nb + b)
            outs, lses = [], []
            for e in (0, 1):
                sel = lo if e == 0 else jnp.logical_not(lo)
                s = _dot_nt(jnp.where(sel, q, 0), k3) + b_r[e]
                s = jnp.where(valid, s, NEG)
                sink = sink_r[2 * p + e]
                mx = jnp.maximum(jnp.max(s, axis=1, keepdims=True), sink)
                pe = jnp.exp(s - mx)
                den = jnp.sum(pe, axis=1, keepdims=True) + jnp.exp(sink - mx)
                outs.append(_dot(pe.astype(CDT), v3) / den)
                lses.append(mx + jnp.log(den))
            o_o[rows, :] = jnp.where(lo, outs[0], outs[1]).astype(CDT)
            lse_o[rows, :] = jnp.where(lo, lses[0], lses[1])

    qblk = pl.BlockSpec((ta, 128), lambda p, i: (i, p))
    res = pl.pallas_call(
        _riding(body, 9, 2, ride, grid), name=name, grid=grid,
        in_specs=[pl.BlockSpec(memory_space=pltpu.SMEM), qblk] + _swa_kv_specs(ta) + _swa_kv_specs(ta)
        + [pl.BlockSpec((2, BLK, 3 * BLK), lambda p, i: (p, 0, 0))] + ride_in_specs,
        out_specs=[qblk, qblk] + ride_out_specs,
        out_shape=[_sds((t, 512), CDT), _sds((t, 512), F32)] + ride_out, scratch_shapes=ride_sems,
        compiler_params=_params(("arbitrary", "arbitrary") if ride else ("parallel", "parallel")),
    )(sinks, qs, kse, kse, kse, vse, vse, vse, bias, *ride_in)
    return (res[0], res[1], res[2:]) if ride else res


def _swa_bwd(qs, kse, vse, bias, sinks, o, lse, do, name):
    t = qs.shape[0]
    ta = _row_tile(t)
    nb = ta // BLK

    def body(sink_r, q_r, kp_r, kc_r, km_r, vp_r, vc_r, vm_r, b_r, o_r, lse_r, do_r,
             dq_o, dk_o, dv_o, db_o, dsk_o):
        p = pl.program_id(0)
        i = pl.program_id(1)
        lo = _iota((1, 128), 1) < HD

        @pl.when((i == 0) & (p % 2 == 0))
        def _():
            dk_o[...] = jnp.zeros(dk_o.shape, F32)
            dv_o[...] = jnp.zeros(dv_o.shape, F32)

        @pl.when(i == 0)
        def _():
            db_o[...] = jnp.zeros(db_o.shape, F32)
            dsk_o[...] = jnp.zeros(dsk_o.shape, F32)

        k4 = jnp.concatenate([kp_r[...], kc_r[...]], axis=0)
        v4 = jnp.concatenate([vp_r[...], vc_r[...]], axis=0)
        for b in range(nb):
            ib = i * nb + b
            rows = slice(BLK * b, BLK * (b + 1))
            q = q_r[rows, :]
            do_ = do_r[rows, :]
            dd = do_ * o_r[rows, :].astype(F32)
            lse = lse_r[rows, :]
            k3 = jnp.concatenate([k4[BLK * b:BLK * (b + 2)], km_r[...]], axis=0)
            v3 = jnp.concatenate([v4[BLK * b:BLK * (b + 2)], vm_r[...]], axis=0)
            valid = _swa_valid(ib)
            dq = jnp.zeros((BLK, 128), F32)
            dk3 = jnp.zeros((3 * BLK, 128), F32)
            dv3 = jnp.zeros((3 * BLK, 128), F32)
            dsink = []
            for e in (0, 1):
                sel = lo if e == 0 else jnp.logical_not(lo)
                qe = jnp.where(sel, q, 0)
                doe = jnp.where(sel, do_, 0.0).astype(CDT)
                lse_e = lse[:, HD * e:HD * e + 1]
                s = _dot_nt(qe, k3) + b_r[e]
                s = jnp.where(valid, s, NEG)
                pr = jnp.exp(s - lse_e)
                delta = jnp.sum(jnp.where(sel, dd, 0.0), axis=1, keepdims=True)
                ds = pr * (_dot_nt(doe, v3) - delta)
                db_o[e] += ds
                dsink.append(-jnp.sum(jnp.exp(sink_r[2 * p + e] - lse_e) * delta, axis=0, keepdims=True))
                dq = dq + _dot(ds.astype(CDT), jnp.where(sel, k3, 0))
                dk3 = dk3 + _dot(ds.T.astype(CDT), qe)
                dv3 = dv3 + _dot(pr.T.astype(CDT), doe)
            dq_o[rows, :] = dq
            prev = pl.ds(pl.multiple_of(jnp.maximum(ib - 1, 0) * BLK, BLK), BLK)
            cur = pl.ds(pl.multiple_of(ib * BLK, BLK), BLK)
            dk_o[prev, :] += dk3[0:BLK]
            dk_o[cur, :] += dk3[BLK:2 * BLK]
            dk_o[0:BLK, :] += dk3[2 * BLK:]
            dv_o[prev, :] += dv3[0:BLK]
            dv_o[cur, :] += dv3[BLK:2 * BLK]
            dv_o[0:BLK, :] += dv3[2 * BLK:]
            dsk_o[0:1, :] += jnp.where(lo, dsink[0], dsink[1])

    qblk = pl.BlockSpec((ta, 128), lambda p, i: (i, p))
    kvacc = pl.BlockSpec((None, t, 128), lambda p, i: (p // 2, 0, 0))
    bblk = pl.BlockSpec((2, BLK, 3 * BLK), lambda p, i: (p, 0, 0))
    return pl.pallas_call(
        body, name=name, grid=(NPAIR, t // ta),
        in_specs=[pl.BlockSpec(memory_space=pltpu.SMEM), qblk] + _swa_kv_specs(ta) + _swa_kv_specs(ta)
        + [bblk, qblk, qblk, qblk],
        out_specs=[qblk, kvacc, kvacc, bblk, pl.BlockSpec((None, 8, 128), lambda p, i: (p, 0, 0))],
        out_shape=[_sds((t, 512), F32), _sds((2, t, 128), F32), _sds((2, t, 128), F32),
                   _sds((8, BLK, 3 * BLK), F32), _sds((NPAIR, 8, 128), F32)],
        compiler_params=_params(("arbitrary", "arbitrary")),
    )(sinks, qs, kse, kse, kse, vse, vse, vse, bias, o, lse, do)


def _sum8(slots, name):
    def body(a_r, o_o):
        acc = a_r[0]
        for k in range(1, 8):
            acc = acc + a_r[k]
        o_o[...] = acc

    return pl.pallas_call(
        body, name=name, out_shape=_sds((SMALL_ROWS, 128), F32),
        in_specs=[pl.BlockSpec(memory_space=pltpu.VMEM)], out_specs=pl.BlockSpec(memory_space=pltpu.VMEM),
        compiler_params=_params(),
    )(slots)


def _place():
    x, y, c = lax.axis_index("x"), lax.axis_index("y"), lax.axis_index("c")
    chips = [(1 - x, y), (x, 1 - y), (1 - x, 1 - y)]
    return x, y, c, chips


def _remote(src, dst, send_sems, recv_sems, k, to):
    return pltpu.make_async_remote_copy(src_ref=src, dst_ref=dst, send_sem=send_sems.at[k], recv_sem=recv_sems.at[k],
                                        device_id=to, device_id_type=MESH_ID)


ANY = pl.BlockSpec(memory_space=pl.ANY)


def _mix_cols(w):
    return jnp.concatenate([w[:, 2312:4360], w[:, 0:1536], w[:, 1544:2312], w[:, 1536:1544],
                            jnp.zeros((w.shape[0], DP - D_IN), w.dtype)], axis=1)


def _unmix_cols(w):
    return jnp.concatenate([w[:, QA:QA + 1536], w[:, FA:FA + 8], w[:, QB:QB + 768], w[:, GA:GA + 2048]], axis=1)


def _rows128(a, rows):
    flat = a.reshape(-1)
    return jnp.pad(flat, (0, rows * 128 - flat.shape[0])).reshape(rows, 128)


GRAD_FORM = {"ffn1_w_in": "col", "ffn2_w_in": "col", "w_branch_fox": "col", "w_branch_swa": "col",
             "ffn1_w_out": "3d", "ffn2_w_out": "3d", "w_out": "3d", "w_in": "3d"}
SUM_TILE = {1024: 128, 704: 176, 512: 128, 256: 128}
NT = len(SHARD_ITEMS)
ALL_ITEMS = tuple(range(NT))


def _half_rows(c, r):
    return pl.ds(pl.multiple_of(c * (r // 2), 16), r // 2)


def _ici_copies(kind, srcs, dsts, send_sems, recv_sems, layer, recv=True, items=ALL_ITEMS):
    x, y, c, chips = _place()
    s = 2 * x + y
    sends, recvs = [], []
    for t, (item, src, dst) in enumerate(zip(items, srcs, dsts)):
        nm, (r, cc), _ = SHARD_ITEMS[item]
        for j, (cx, cy) in enumerate(chips):
            sj = 2 * cx + cy
            k = 3 * t + j
            to = (cx, cy, c)
            if kind == "gather":
                hs = _half_rows(c, r)
                sends.append(_remote(src.at[layer, hs], dst.at[s, hs], send_sems, recv_sems, k, to))
                if recv:
                    recvs.append(_remote(src.at[layer, hs], dst.at[sj, hs], send_sems, recv_sems, k, to))
            else:
                if GRAD_FORM[nm] == "col":
                    piece = src.at[:, pl.ds(pl.multiple_of(sj * cc, 128), cc)]
                else:
                    piece = src.at[sj]
                sends.append(_remote(piece, dst.at[j], send_sems, recv_sems, k, to))
                recvs.append(sends[-1])
    return sends, recvs


def _slab_shapes(items=ALL_ITEMS):
    return [_sds((4, *SHARD_ITEMS[t][1]), CDT) for t in items]


def _dma_sems(n):
    return [pltpu.SemaphoreType.DMA((n,)), pltpu.SemaphoreType.DMA((n,))]


def _forward_sends(dsts, send_sems, recv_sems, items=ALL_ITEMS):
    x, y, c, chips = _place()
    sends, recvs = [], []
    for t, (item, dst) in enumerate(zip(items, dsts)):
        r = SHARD_ITEMS[item][1][0]
        for j, (cx, cy) in enumerate(chips):
            sj = 2 * cx + cy
            hs, ho = _half_rows(c, r), _half_rows(1 - c, r)
            sends.append(_remote(dst.at[sj, hs], dst.at[sj, hs], send_sems, recv_sems, 3 * t + j, (x, y, 1 - c)))
            recvs.append(_remote(dst.at[sj, ho], dst.at[sj, ho], send_sems, recv_sems, 3 * t + j, (x, y, 1 - c)))
    return sends, recvs


def _gather_layer(wb, mflat, layer, name, items):
    nt = len(items)

    def body(*refs):
        srcs, m_r, dsts, mall_o = refs[:nt], refs[nt], refs[nt + 1:2 * nt + 1], refs[2 * nt + 1]
        send_sems, recv_sems, fsend, frecv, msend, mrecv = refs[2 * nt + 2:]
        x, y, c, chips = _place()
        s = 2 * x + y
        sends, recvs = _ici_copies("gather", srcs, dsts, send_sems, recv_sems, layer, items=items)
        metas = [_remote(m_r, mall_o.at[s], msend, mrecv, j, (cx, cy, c)) for j, (cx, cy) in enumerate(chips)]
        for cp in sends + metas:
            cp.start()
        fwds, frecvs = _forward_sends(dsts, fsend, frecv, items)
        for got, fwd in zip(recvs, fwds):
            got.wait_recv()
            fwd.start()
        for got in frecvs:
            got.wait_recv()
        for j, (cx, cy) in enumerate(chips):
            _remote(m_r, mall_o.at[2 * cx + cy], msend, mrecv, j, (cx, cy, c)).wait_recv()
        for cp in sends + metas + fwds:
            cp.wait_send()

    return pl.pallas_call(
        body, name=name, out_shape=_slab_shapes(items) + [_sds((4, META_ROWS, 128), F32)],
        in_specs=[ANY] * (nt + 1), out_specs=[ANY] * (nt + 1),
        scratch_shapes=_dma_sems(3 * nt) + _dma_sems(3 * nt) + _dma_sems(3),
    )(*wb, mflat)


def _forward_layer(slabs, name, items=ALL_ITEMS):
    nt = len(items)

    def body(*refs):
        ins, outs, send_sems, recv_sems = refs[:nt], refs[nt:2 * nt], refs[2 * nt], refs[2 * nt + 1]
        sends, recvs = _forward_sends(outs, send_sems, recv_sems, items)
        for cp in sends:
            cp.start()
        for cp in recvs:
            cp.wait_recv()
        for cp in sends:
            cp.wait_send()

    return pl.pallas_call(
        body, name=name, out_shape=_slab_shapes(items), in_specs=[ANY] * nt, out_specs=[ANY] * nt,
        input_output_aliases={t: t for t in range(nt)}, scratch_shapes=_dma_sems(3 * nt),
    )(*slabs)


def _half_shape(nm, r, c):
    return (r // 2, 4 * c) if GRAD_FORM[nm] == "col" else (4, r // 2, c)


def _swap_layer(gs, gsm, name, items=ALL_ITEMS):
    small = gsm is not None
    nt = len(items)

    def body(*refs):
        g_rs = refs[:nt]
        pos = nt
        if small:
            s_r = refs[pos]
            pos += 1
        got_os = refs[pos:pos + nt]
        pos += nt
        if small:
            slots_o = refs[pos]
            pos += 1
        send_sems, recv_sems = refs[pos], refs[pos + 1]
        x, y, c, _ = _place()
        sib = (x, y, 1 - c)
        sent = []
        for t, (item, g_r, got_o) in enumerate(zip(items, g_rs, got_os)):
            nm, (r, cc), _ = SHARD_ITEMS[item]
            ho = _half_rows(1 - c, r)
            src = g_r.at[ho, :] if GRAD_FORM[nm] == "col" else g_r.at[:, ho, :]
            sent.append(_remote(src, got_o, send_sems, recv_sems, t, sib))
        if small:
            ssend, srecv, loc_sem = refs[pos + 2], refs[pos + 3], refs[pos + 4]
            me = 4 * x + 2 * y + c
            loc = pltpu.make_async_copy(s_r, slots_o.at[me], loc_sem.at[0])
            loc.start()
            peers = [(x ^ (k >> 2), y ^ ((k >> 1) & 1), c ^ (k & 1)) for k in range(1, 8)]
            for k, peer in enumerate(peers):
                sent.append(_remote(s_r, slots_o.at[me], ssend, srecv, k, peer))
        for cp in sent:
            cp.start()
        for cp in sent[:nt]:
            cp.wait_recv()
        if small:
            for k, (px, py, pc) in enumerate(peers):
                _remote(s_r, slots_o.at[4 * px + 2 * py + pc], ssend, srecv, k, (px, py, pc)).wait_recv()
        for cp in sent:
            cp.wait_send()
        if small:
            loc.wait()

    outs = [_sds(_half_shape(*SHARD_ITEMS[item][0:1], *SHARD_ITEMS[item][1]), CDT) for item in items]
    ops = list(gs)
    sems = _dma_sems(nt)
    if small:
        outs.append(_sds((8, SMALL_ROWS, 128), F32))
        ops.append(gsm)
        sems = sems + _dma_sems(7) + [pltpu.SemaphoreType.DMA((1,))]
    res = pl.pallas_call(
        body, name=name, out_shape=outs, in_specs=[ANY] * len(ops), out_specs=[ANY] * len(outs), scratch_shapes=sems,
    )(*ops)
    return (res[:nt], res[nt]) if small else (res, None)


def _pair_add_t(own, got, half_idx, nm, r, name):
    tr = SUM_TILE[r]
    nb = (r // 2) // tr
    if GRAD_FORM[nm] == "col":
        blk = (tr, own.shape[1])
        own_spec = pl.BlockSpec(blk, lambda i, c_r: (c_r[0] * nb + i, 0))
        half_spec = pl.BlockSpec(blk, lambda i, c_r: (i, 0))
    else:
        blk = (4, tr, own.shape[2])
        own_spec = pl.BlockSpec(blk, lambda i, c_r: (0, c_r[0] * nb + i, 0))
        half_spec = pl.BlockSpec(blk, lambda i, c_r: (0, i, 0))

    def body(c_r, a_r, b_r, o_o):
        o_o[...] = (a_r[...].astype(F32) + b_r[...].astype(F32)).astype(CDT)

    grid_spec = pltpu.PrefetchScalarGridSpec(num_scalar_prefetch=1, grid=(nb,), in_specs=[own_spec, half_spec],
                                             out_specs=half_spec)
    return pl.pallas_call(body, name=name, grid_spec=grid_spec, out_shape=_sds(got.shape, CDT),
                          compiler_params=_params(("parallel",)))(half_idx, own, got)


def _sum4_t(ps, got3, buf, idx, layer, nm, r, name):
    tr = SUM_TILE[r]
    nb = (r // 2) // tr
    c = got3.shape[2]
    if GRAD_FORM[nm] == "col":
        ps_spec = pl.BlockSpec((tr, c), lambda i, x_r: (i, x_r[0]))
    else:
        ps_spec = pl.BlockSpec((None, tr, c), lambda i, x_r: (x_r[0], i, 0))

    def body(x_r, a_r, b_r, buf_r, o_o):
        o_o[...] = ((a_r[...].astype(F32) + b_r[0].astype(F32)) + b_r[1].astype(F32)) + b_r[2].astype(F32)

    grid_spec = pltpu.PrefetchScalarGridSpec(
        num_scalar_prefetch=1, grid=(nb,),
        in_specs=[ps_spec, pl.BlockSpec((3, tr, c), lambda i, x_r: (0, i, 0)), ANY],
        out_specs=pl.BlockSpec((None, tr, c), lambda i, x_r: (layer, x_r[1] * nb + i, 0)),
    )
    return pl.pallas_call(body, name=name, grid_spec=grid_spec, out_shape=_sds(buf.shape, F32),
                          input_output_aliases={3: 0}, compiler_params=_params(("parallel",)))(idx, ps, got3, buf)


def _scatter_layer(ps, name, items=ALL_ITEMS):
    nt = len(items)

    def body(*refs):
        srcs, dsts, send_sems, recv_sems = refs[:nt], refs[nt:2 * nt], refs[2 * nt], refs[2 * nt + 1]
        sends, recvs = _ici_copies("scatter", srcs, dsts, send_sems, recv_sems, None, items=items)
        for cp in sends:
            cp.start()
        for cp in recvs:
            cp.wait_recv()
        for cp in sends:
            cp.wait_send()

    return pl.pallas_call(
        body, name=name, out_shape=_got3_shapes(items), in_specs=[ANY] * nt, out_specs=[ANY] * nt,
        scratch_shapes=_dma_sems(3 * nt),
    )(*ps)


def _got3_shapes(items=ALL_ITEMS):
    return [_sds((3, SHARD_ITEMS[t][1][0] // 2, SHARD_ITEMS[t][1][1]), CDT) for t in items]


def _join_layer(bufs, name):
    def body(*refs):
        ins, outs, send_sems, recv_sems = refs[:NT], refs[NT:2 * NT], refs[2 * NT], refs[2 * NT + 1]
        x, y, c, _ = _place()
        sent = []
        for t, ((nm, (r, cc), _), b_o) in enumerate(zip(SHARD_ITEMS, outs)):
            hs = _half_rows(c, r)
            sent.append(_remote(b_o.at[:, hs, :], b_o.at[:, hs, :], send_sems, recv_sems, t, (x, y, 1 - c)))
        for cp in sent:
            cp.start()
        for t, ((nm, (r, cc), _), b_o) in enumerate(zip(SHARD_ITEMS, outs)):
            ho = _half_rows(1 - c, r)
            _remote(b_o.at[:, ho, :], b_o.at[:, ho, :], send_sems, recv_sems, t, (x, y, 1 - c)).wait_recv()
        for cp in sent:
            cp.wait_send()

    return pl.pallas_call(
        body, name=name, out_shape=[_sds(b.shape, F32) for b in bufs], in_specs=[ANY] * NT, out_specs=[ANY] * NT,
        input_output_aliases={t: t for t in range(NT)}, scratch_shapes=_dma_sems(NT),
    )(*bufs)


def _adamw3(w, g, m, v, name):
    nl, r, c = w.shape
    tr = SUM_TILE.get(r, r)
    if r % 8:
        blk = pl.BlockSpec((None, r, 256), lambda l, i: (l, 0, i))
        steps = c // 256
    else:
        blk = pl.BlockSpec((None, tr, c), lambda l, i: (l, i, 0))
        steps = r // tr

    def body(w_r, g_r, m_r, v_r, d_o, m_o, v_o):
        g_ = g_r[...]
        m_ = ADAM_B1 * m_r[...] + (1.0 - ADAM_B1) * g_
        v_ = ADAM_B2 * v_r[...] + (1.0 - ADAM_B2) * jnp.square(g_)
        m_hat = m_ / (1.0 - ADAM_B1 ** ADAM_STEP)
        v_hat = v_ / (1.0 - ADAM_B2 ** ADAM_STEP)
        d_o[...] = -ADAM_LR * (m_hat / (jnp.sqrt(v_hat) + ADAM_EPS) + ADAM_WD * w_r[...])
        m_o[...] = m_
        v_o[...] = v_

    return pl.pallas_call(
        body, name=name, grid=(nl, steps),
        in_specs=[blk] * 4, out_specs=[blk] * 3, out_shape=[_sds((nl, r, c), F32)] * 3,
        compiler_params=_params(("parallel", "parallel")),
    )(w, g, m, v)


def _full_weights(slabs, wb, layer, shard, items=ALL_ITEMS):
    ws = {}
    for t, slab in zip(items, slabs):
        nm, (r, c), kind = SHARD_ITEMS[t]
        slab = lax.dynamic_update_slice(slab, wb[nm][layer][None], (shard, 0, 0))
        ws[nm] = slab.reshape(4 * r, c) if kind == "row" else jnp.concatenate([slab[s] for s in range(4)], axis=1)
    return ws


def _exchange_forms(g, items=ALL_ITEMS):
    out = []
    for t in items:
        nm, (r, c), _ = SHARD_ITEMS[t]
        a = g[nm]
        if nm == "w_in":
            a = a.reshape(D, 4, c).transpose(1, 0, 2)
        elif GRAD_FORM[nm] == "3d":
            a = a.reshape(4, r, c)
        out.append(a)
    return out


SMALL_ITEMS = (("rel_bias_table", 2), ("ffn1_norm", 16), ("mix_norm", 16), ("ffn2_norm", 16), ("forget_bias", 1),
               ("fox_q_norm", 1), ("fox_k_norm", 1), ("swa_q_norm", 1), ("swa_k_norm", 1), ("swa_sinks", 1))
SMALL_ADAM_ROWS = 96


def _layer_fwd(h, lw, l, ride=None, late=None):
    rides = late["rides"] if late else {}

    def run(key, fn, *args):
        r = rides.get(key)
        if r is None:
            return fn(*args)
        out = fn(*args, ride=r)
        late["arrived"](key, out[-1])
        return out[0] if len(out) == 2 else out[:-1]

    sv = {"h0": h}
    a, sv["a1t"] = _rms_fwd(h, lw["ffn1_norm"], f"rms_fwd_a{l}")
    sv["gu1"], s, sv["s1t"] = run("ffn_in_a", _ffn_in, a, lw["ffn1_w_in"], f"ffn_in_a{l}")
    h = run("ffn_out_a", _mm_res, s, lw["ffn1_w_out"], h, 0.5, f"ffn_out_a{l}")
    sv["h1"] = h
    a, sv["amt"] = _rms_fwd(h, lw["mix_norm"], f"rms_fwd_m{l}")
    if late:
        late["need"](lw, "mixer")
    proj = run("proj", _mm, a, lw["w_mix"], F32, _row_tile(h.shape[0]), DP, f"proj{l}")
    sv["proj"] = proj
    qf, kf, vf, qs, kse, vse, c, ct, sv["qft"] = _qknorm_fwd(proj, lw["gfq"], lw["gfk"], lw["gsq"], lw["gsk"], lw["fb"],
                                                              f"qknorm_fwd{l}")
    ofox, lse_f, *rode = _fox_fwd(qf, kf, vf, c, ct, f"fox_fwd{l}", ride)
    oswa, lse_s = run("swa_fwd", _swa_fwd, qs, kse, vse, lw["bias"], lw["sinks"], f"swa_fwd{l}")
    if late:
        late["need"](lw, "gate")
    sv.update(qf=qf, kf=kf, vf=vf, qs=qs, kse=kse, vse=vse, c=c, ct=ct, ofox=ofox, oswa=oswa, lse_f=lse_f, lse_s=lse_s)
    y, sv["yt"], sv["pf"], sv["ps"], sv["oft"], sv["ost"] = _gate_fwd(ofox, oswa, lw["w_branch_fox"], lw["w_branch_swa"],
                                                                     proj, f"gate_fwd{l}")
    h = _mm_res(y, lw["w_out"], h, 1.0, f"mix_out{l}")
    sv["h2"] = h
    a, sv["a2t"] = _rms_fwd(h, lw["ffn2_norm"], f"rms_fwd_b{l}")
    sv["gu2"], s, sv["s2t"] = _ffn_in(a, lw["ffn2_w_in"], f"ffn_in_b{l}")
    h = _mm_res(s, lw["ffn2_w_out"], h, 0.5, f"ffn_out_b{l}")
    return h, sv, rode


def _ffn_bwd(dh, dhb, h_in, at, gu, st, norm, w_in, w_out, tag, rides=None):
    r = rides or (None,) * 4
    rode = []

    def split(res, ride):
        if ride is None:
            return res
        rode.extend(res[-1])
        return res[0] if len(res) == 2 else res[:-1]

    dgu = split(_ffn_bwd_mid(dhb, w_out, gu, f"ffn_bwd_mid_{tag}", r[0]), r[0])
    d_w_out = split(_mm(st, dhb, CDT, 256, D, f"dw_ffn_out_{tag}", scale=0.5, ride=r[1]), r[1])
    dh, dhb, dg = split(_ffn_bwd_in(dgu, w_in, h_in, norm, dh, f"ffn_bwd_in_{tag}", r[2]), r[2])
    d_w_in = split(_mm(at, dgu, CDT, D, 256, f"dw_ffn_in_{tag}", ride=r[3]), r[3])
    return dh, dhb, d_w_out, d_w_in, dg, rode


def _layer_bwd(dh, dhb, sv, lw, l, ride=None, before_ffn1=None):
    g = {}
    dh, dhb, g["ffn2_w_out"], g["ffn2_w_in"], g["ffn2_norm"], _ = _ffn_bwd(
        dh, dhb, sv["h2"], sv["a2t"], sv["gu2"], sv["s2t"], lw["ffn2_norm"], lw["ffn2_w_in"], lw["ffn2_w_out"], f"b{l}")
    dy = _mm_nt(dhb, lw["w_out"], f"d_y{l}")
    g["w_out"] = _mm(sv["yt"], dhb, CDT, 512, 512, f"dw_out{l}")
    dpf, dps, dga, dgb = _gate_bwd(dy, sv["pf"], sv["ps"], sv["proj"], f"gate_bwd{l}")
    do_f, do_ft = _mm_nt(dpf, lw["w_branch_fox"], f"d_ofox{l}", with_t=True)
    do_s = _mm_nt(dps, lw["w_branch_swa"], f"d_oswa{l}")
    g["w_branch_fox"] = _mm(sv["oft"], dpf, CDT, 512, 512, f"dw_bfox{l}")
    g["w_branch_swa"] = _mm(sv["ost"], dps, CDT, 512, 512, f"dw_bswa{l}")
    dqf, dcq, dkf, dvf, dck, *rode = _fox_bwd(sv["qf"], sv["qft"], sv["kf"], sv["vf"], sv["c"], sv["ct"], sv["ofox"],
                                              sv["lse_f"], do_f, do_ft, f"fox_bwd{l}", ride)
    g["rode"] = rode
    dqs, dkse, dvse, dbias, dsk = _swa_bwd(sv["qs"], sv["kse"], sv["vse"], lw["bias"], lw["sinks"], sv["oswa"],
                                           sv["lse_s"], do_s, f"swa_bwd{l}")
    dproj, dgn = _qknorm_bwd(sv["proj"], dqf, dkf, dvf, dqs, dkse, dvse, dcq, dck, dga, dgb,
                             lw["gfq"], lw["gfk"], lw["gsq"], lw["gsk"], lw["fb"], f"qknorm_bwd{l}")
    g["w_mix"] = _mm(sv["amt"], dproj, CDT, 512, 640, f"dw_mix{l}")
    dh, dhb, g["mix_norm"] = _mm_nt_rms(dproj, lw["w_mix"], sv["h1"], lw["mix_norm"], dh, f"d_am{l}")
    g["dbias"], g["dsk"], g["dgn"] = dbias, dsk, dgn
    rides = before_ffn1(g) if before_ffn1 else None
    dh, dhb, g["ffn1_w_out"], g["ffn1_w_in"], g["ffn1_norm"], g["rode_ffn1"] = _ffn_bwd(
        dh, dhb, sv["h0"], sv["a1t"], sv["gu1"], sv["s1t"], lw["ffn1_norm"], lw["ffn1_w_in"], lw["ffn1_w_out"], f"a{l}",
        rides)
    return dh, dhb, g


def kernel(x, meta_tokens, rel_bias_table, ffn1_norm, ffn1_w_in, ffn1_w_out, mix_norm, w_in, forget_bias, fox_q_norm, fox_k_norm, swa_q_norm, swa_k_norm, swa_sinks, w_branch_fox, w_branch_swa, w_out, ffn2_norm, ffn2_w_in, ffn2_w_out, loss_target, m_meta_tokens, m_rel_bias_table, m_ffn1_norm, m_ffn1_w_in, m_ffn1_w_out, m_mix_norm, m_w_in, m_forget_bias, m_fox_q_norm, m_fox_k_norm, m_swa_q_norm, m_swa_k_norm, m_swa_sinks, m_w_branch_fox, m_w_branch_swa, m_w_out, m_ffn2_norm, m_ffn2_w_in, m_ffn2_w_out, v_meta_tokens, v_rel_bias_table, v_ffn1_norm, v_ffn1_w_in, v_ffn1_w_out, v_mix_norm, v_w_in, v_forget_bias, v_fox_q_norm, v_fox_k_norm, v_swa_q_norm, v_swa_k_norm, v_swa_sinks, v_w_branch_fox, v_w_branch_swa, v_w_out, v_ffn2_norm, v_ffn2_w_in, v_ffn2_w_out):
    names = ["meta_tokens", "rel_bias_table", "ffn1_norm", "ffn1_w_in", "ffn1_w_out", "mix_norm", "w_in", "forget_bias",
             "fox_q_norm", "fox_k_norm", "swa_q_norm", "swa_k_norm", "swa_sinks", "w_branch_fox", "w_branch_swa", "w_out",
             "ffn2_norm", "ffn2_w_in", "ffn2_w_out"]
    w = dict(zip(names, [meta_tokens, rel_bias_table, ffn1_norm, ffn1_w_in, ffn1_w_out, mix_norm, w_in, forget_bias,
                         fox_q_norm, fox_k_norm, swa_q_norm, swa_k_norm, swa_sinks, w_branch_fox, w_branch_swa, w_out,
                         ffn2_norm, ffn2_w_in, ffn2_w_out]))
    m = dict(zip(names, [m_meta_tokens, m_rel_bias_table, m_ffn1_norm, m_ffn1_w_in, m_ffn1_w_out, m_mix_norm, m_w_in,
                         m_forget_bias, m_fox_q_norm, m_fox_k_norm, m_swa_q_norm, m_swa_k_norm, m_swa_sinks,
                         m_w_branch_fox, m_w_branch_swa, m_w_out, m_ffn2_norm, m_ffn2_w_in, m_ffn2_w_out]))
    v = dict(zip(names, [v_meta_tokens, v_rel_bias_table, v_ffn1_norm, v_ffn1_w_in, v_ffn1_w_out, v_mix_norm, v_w_in,
                         v_forget_bias, v_fox_q_norm, v_fox_k_norm, v_swa_q_norm, v_swa_k_norm, v_swa_sinks,
                         v_w_branch_fox, v_w_branch_swa, v_w_out, v_ffn2_norm, v_ffn2_w_in, v_ffn2_w_out]))
    xi, yi, ci = lax.axis_index("x"), lax.axis_index("y"), lax.axis_index("c")
    shard = 2 * xi + yi
    seq = x.shape[1]
    t = seq + BLK

    wb = {nm: w[nm].astype(CDT) for nm, _, _ in SHARD_ITEMS}
    wb_list = [wb[nm] for nm, _, _ in SHARD_ITEMS]
    mflat = meta_tokens.reshape(META_ROWS, 128)
    first = (0, 1)
    *slabs_first, mall = _gather_layer([wb_list[t] for t in first], mflat, 0, "gather_weights", first)
    mall = lax.dynamic_update_slice(mall, mflat[None], (shard, 0, 0))
    meta_full = jnp.concatenate([mall[s].reshape(N_META, 256) for s in range(4)], axis=1)
    bias = _bias_fwd(rel_bias_table, "bias_fwd")

    def layer_weights(slabs, l, items=ALL_ITEMS):
        lw = _full_weights(slabs, wb, l, shard, items)
        if "w_in" in lw:
            lw["w_mix"] = _mix_cols(lw.pop("w_in"))
        return lw

    def layer_vectors(l):
        lw = {nm: w[nm][l].reshape(1, D) for nm in ("ffn1_norm", "mix_norm", "ffn2_norm")}
        lw["gfq"] = jnp.tile(fox_q_norm[l], 8).reshape(1, 512)
        lw["gfk"] = jnp.tile(fox_k_norm[l], 8).reshape(1, 512)
        lw["gsq"] = jnp.tile(swa_q_norm[l], 8).reshape(1, 512)
        lw["gsk"] = jnp.tile(swa_k_norm[l], 2).reshape(1, 128)
        lw["fb"] = jnp.pad(forget_bias[l], (0, 120)).reshape(1, 128)
        lw["sinks"] = swa_sinks[l]
        lw["bias"] = bias
        return lw

    def gather_ride(layer, items):
        return ("gather", [wb_list[t] for t in items], _slab_shapes(items), layer, items)

    landed = {}

    def need(lw, stage):
        if stage == "mixer":
            items = (2,)
            slabs = _forward_layer(landed["ffn_in_a"], "forward_halves0m", items)
        else:
            items = (3, 4, 5, 6, 7)
            slabs = _forward_layer(landed["ffn_out_a"] + landed["swa_fwd"] + landed["proj"], "forward_halves0g", items)
        lw.update(layer_weights(slabs, 0, items))

    late = {"rides": {"ffn_in_a": gather_ride(0, (2,)), "ffn_out_a": gather_ride(0, (3, 4, 5)),
                      "proj": gather_ride(0, (7,)), "swa_fwd": gather_ride(0, (6,))},
            "arrived": landed.__setitem__, "need": need}

    h = jnp.concatenate([jnp.zeros((PAD, D), F32), meta_full, x[0]], axis=0)
    lws = [{**layer_vectors(0), **layer_weights(slabs_first, 0, first)}]
    h, sv0, slabs1 = _layer_fwd(h, lws[0], 0, gather_ride(1, ALL_ITEMS), late)
    lws.append({**layer_vectors(1), **layer_weights(_forward_layer(slabs1, "forward_halves"), 1)})
    h, sv1, _ = _layer_fwd(h, lws[1], 1)
    saved = [sv0, sv1]
    dh, dhb, lacc = _loss(h, loss_target[0], "loss")
    loss = lax.psum(lacc[0, 0], ("x", "y", "c"))

    half_idx = ci.reshape(1).astype(jnp.int32)
    place_idx = jnp.stack([shard, ci]).astype(jnp.int32)

    def pair_sums(g, gsm, tag, items=ALL_ITEMS):
        if "w_mix" in g:
            g["w_in"] = _unmix_cols(g.pop("w_mix"))
        forms = _exchange_forms(g, items)
        got, slots = _swap_layer(forms, gsm, f"swap_halves{tag}", items)
        return {t: _pair_add_t(a, b, half_idx, SHARD_ITEMS[t][0], SHARD_ITEMS[t][1][0],
                               f"pair_add{tag}_{SHARD_ITEMS[t][0]}")
                for t, a, b in zip(items, forms, got)}, slots

    def scatter_ride(ps, items):
        return ("scatter", [ps[t] for t in items], _got3_shapes(items), None, items)

    early = (2, 3, 4, 5, 6, 7)
    early_rides = ((6,), (7,), (2, 5), (3, 4))
    ps0 = {}

    def before_ffn1(g):
        ps0.update(pair_sums(g, None, "0e", early)[0])
        return [scatter_ride(ps0, items) for items in early_rides]

    grads = [None, None]
    dh, dhb, grads[1] = _layer_bwd(dh, dhb, saved[1], lws[1], 1)
    ps1, _ = pair_sums(grads[1], None, 1)
    dh, dhb, grads[0] = _layer_bwd(dh, dhb, saved[0], lws[0], 0, scatter_ride(ps1, ALL_ITEMS), before_ffn1)
    grad_x = dh[BLK:].reshape(1, seq, D)
    dtab = _bias_bwd(grads[0]["dbias"] + grads[1]["dbias"], "bias_bwd")

    small = [dh[PAD:BLK].reshape(128, 128), _rows128(dtab[:, :N_BUCKETS].T, 2)]
    for nm in ("ffn1_norm", "mix_norm", "ffn2_norm"):
        small.append(jnp.stack([grads[l][nm][0] for l in range(2)]).reshape(16, 128))
    small.append(_rows128(jnp.stack([grads[l]["dgn"][4, :8] for l in range(2)]), 1))
    for row in range(4):
        small.append(jnp.stack([grads[l]["dgn"][row, :HD] for l in range(2)]).reshape(1, 128))
    dsk = [grads[l]["dsk"][:, 0, :] for l in range(2)]
    small.append(_rows128(jnp.stack([jnp.stack([d[:, 0], d[:, HD]], axis=1).reshape(8) for d in dsk]), 1))
    gsm = jnp.concatenate(small, axis=0)
    gsm = jnp.pad(gsm, ((0, SMALL_ROWS - gsm.shape[0]), (0, 0)))

    late = (0, 1)
    ps_late, slots = pair_sums(grads[0], gsm, "0l", late)
    ps0.update(ps_late)
    got3_0 = dict(zip([t for items in early_rides for t in items], grads[0]["rode_ffn1"]))
    got3_0.update(zip(late, _scatter_layer([ps0[t] for t in late], "scatter_shards", late)))
    got3 = [got3_0, dict(zip(ALL_ITEMS, grads[0]["rode"]))]
    bufs = []
    for t, (nm, (r, c), _) in enumerate(SHARD_ITEMS):
        buf = lax.empty((2, r, c), F32)
        for l, ps in ((1, ps1), (0, ps0)):
            buf = _sum4_t(ps[t], got3[l][t], buf, place_idx, l, nm, r, f"sum4_{l}_{nm}")
        bufs.append(buf)
    bufs = _join_layer(bufs, "join_halves")
    gs = _sum8(slots, "sum8")

    g_out = {nm: buf for (nm, _, _), buf in zip(SHARD_ITEMS, bufs)}
    g_out["meta_tokens"] = lax.dynamic_slice(gs[0:128].reshape(N_META, D), (0, shard * 256), (N_META, 256))
    off = 128
    for nm, rows in SMALL_ITEMS:
        n = w[nm].size
        g_out[nm] = gs[off:off + rows].reshape(-1)[:n].reshape(w[nm].shape)
        off += rows

    delta, new_m, new_v = {}, {}, {}
    for nm, _, _ in SHARD_ITEMS:
        if nm == "w_in":
            tr_ = lambda a: jnp.swapaxes(a, 1, 2)
            delta[nm], new_m[nm], new_v[nm] = (tr_(a) for a in _adamw3(tr_(w[nm]), tr_(g_out[nm]), tr_(m[nm]), tr_(v[nm]),
                                                                        f"adamw_{nm}"))
        else:
            delta[nm], new_m[nm], new_v[nm] = _adamw3(w[nm], g_out[nm], m[nm], v[nm], f"adamw_{nm}")
    small_names = ["meta_tokens"] + [nm for nm, _ in SMALL_ITEMS]
    small_rows = [META_ROWS] + [rows for _, rows in SMALL_ITEMS]

    def pack_small(src):
        buf = jnp.concatenate([_rows128(src[nm], rows) for nm, rows in zip(small_names, small_rows)], axis=0)
        return jnp.pad(buf, ((0, SMALL_ADAM_ROWS - buf.shape[0]), (0, 0)))

    d_, m_, v_ = (a[0] for a in _adamw3(pack_small(w)[None], pack_small(g_out)[None], pack_small(m)[None],
                                        pack_small(v)[None], "adamw_small"))
    off = 0
    for nm, rows in zip(small_names, small_rows):
        n = w[nm].size
        for dst, src in ((delta, d_), (new_m, m_), (new_v, v_)):
            dst[nm] = src[off:off + rows].reshape(-1)[:n].reshape(w[nm].shape)
        off += rows

    return (loss, grad_x, *[g_out[n] for n in names], *[delta[n] for n in names],
            *[new_m[n] for n in names], *[new_v[n] for n in names])
```

```python
import math

import numpy as np
import jax
import jax.numpy as jnp
from jax import lax
from jax.experimental import pallas as pl
from jax.experimental.pallas import tpu as pltpu

D = 1024
F = 2816
FT = F // 2
HD = 64
NPAIR = 4
N_META = 16
BLK = 128
PAD = BLK - N_META
EPS = 1e-6
NEG = -1e30
N_BUCKETS = 32
GA, GB, QA, KA, VA, QB, KB, VB, FA, DP = 0, 1024, 2048, 2560, 3072, 3584, 4096, 4224, 4352, 4480
D_IN = 4360
CDT = jnp.bfloat16
F32 = jnp.float32
VMEM_LIMIT = 48 * 1024 * 1024
MESH_ID = pl.DeviceIdType.MESH

ADAM_LR, ADAM_B1, ADAM_B2, ADAM_EPS, ADAM_WD, ADAM_STEP = 0.001, 0.9, 0.999, 1e-08, 0.01, 10

SHARD_ITEMS = (
    ("ffn1_w_in", (1024, 1408), "col"),
    ("ffn1_w_out", (704, 1024), "row"),
    ("w_in", (1024, 1090), "col"),
    ("w_branch_fox", (512, 256), "col"),
    ("w_branch_swa", (512, 256), "col"),
    ("w_out", (256, 1024), "row"),
    ("ffn2_w_in", (1024, 1408), "col"),
    ("ffn2_w_out", (704, 1024), "row"),
)
SMALL_ROWS = 192
META_ROWS = 32


def _row_tile(t):
    return 384 if t % 384 == 0 else 128


def _dot(a, b):
    return jnp.dot(a, b, preferred_element_type=F32)


def _dot_nt(a, b):
    return lax.dot_general(a, b, (((1,), (1,)), ((), ())), preferred_element_type=F32)


def _dot_hi(a, b):
    return jnp.dot(a, b, preferred_element_type=F32, precision=lax.Precision.HIGHEST)


def _sigmoid(x):
    return 0.5 * jnp.tanh(0.5 * x) + 0.5


def _iota(shape, dim):
    return lax.broadcasted_iota(jnp.int32, shape, dim)


def _params(sem=None):
    return pltpu.CompilerParams(dimension_semantics=sem, vmem_limit_bytes=VMEM_LIMIT)


def _sds(shape, dtype):
    return jax.ShapeDtypeStruct(shape, dtype)


def _rms_fwd(h, g, name):
    t = h.shape[0]
    tm = _row_tile(t)

    def body(h_ref, g_ref, a_ref, at_ref):
        x = h_ref[...]
        ms = jnp.mean(x * x, axis=-1, keepdims=True)
        a = x * lax.rsqrt(ms + EPS) * g_ref[...]
        a_ref[...] = a.astype(CDT)
        at_ref[...] = a.T.astype(CDT)

    return pl.pallas_call(
        body, name=name, grid=(t // tm,),
        in_specs=[pl.BlockSpec((tm, D), lambda i: (i, 0)), pl.BlockSpec((1, D), lambda i: (0, 0))],
        out_specs=[pl.BlockSpec((tm, D), lambda i: (i, 0)), pl.BlockSpec((D, tm), lambda i: (0, i))],
        out_shape=[_sds((t, D), CDT), _sds((D, t), CDT)],
        compiler_params=_params(("parallel",)),
    )(h, g)


def _ffn_in(a, w_in, name, ride=None):
    t = a.shape[0]
    tm = _row_tile(t)
    tn = FT
    nj = F // tn
    grid = (nj, t // tm)
    ride_in, ride_in_specs, ride_out, ride_out_specs, ride_sems = _ride_specs(ride)

    def body(a_ref, wg_ref, wu_ref, gu_ref, s_ref, st_ref):
        a_ = a_ref[...]
        g = _dot(a_, wg_ref[...])
        u = _dot(a_, wu_ref[...])
        s = g * _sigmoid(g) * u
        gu_ref[0] = g.astype(CDT)
        gu_ref[1] = u.astype(CDT)
        s_ref[...] = s.astype(CDT)
        st_ref[...] = s.T.astype(CDT)

    res = pl.pallas_call(
        _riding(body, 3, 3, ride, grid), name=name, grid=grid,
        in_specs=[pl.BlockSpec((tm, D), lambda j, i: (i, 0)),
                  pl.BlockSpec((D, tn), lambda j, i: (0, j)),
                  pl.BlockSpec((D, tn), lambda j, i: (0, j + nj))] + ride_in_specs,
        out_specs=[pl.BlockSpec((2, tm, tn), lambda j, i: (0, i, j)),
                   pl.BlockSpec((tm, tn), lambda j, i: (i, j)),
                   pl.BlockSpec((tn, tm), lambda j, i: (j, i))] + ride_out_specs,
        out_shape=[_sds((2, t, F), CDT), _sds((t, F), CDT), _sds((F, t), CDT)] + ride_out, scratch_shapes=ride_sems,
        compiler_params=_params(("arbitrary", "arbitrary") if ride else ("parallel", "parallel")),
    )(a, w_in, w_in, *ride_in)
    return (*res[:3], res[3:]) if ride else res


def _mm_res(a, b, res, scale, name, ride=None):
    t, k = a.shape
    n = b.shape[1]
    tm = _row_tile(t)
    tn = n
    grid = (t // tm, n // tn)
    ride_in, ride_in_specs, ride_out, ride_out_specs, ride_sems = _ride_specs(ride)

    def body(a_ref, b_ref, r_ref, o_ref):
        o_ref[...] = r_ref[...] + scale * _dot(a_ref[...], b_ref[...])

    out = pl.pallas_call(
        _riding(body, 3, 1, ride, grid), name=name, grid=grid,
        in_specs=[pl.BlockSpec((tm, k), lambda i, j: (i, 0)),
                  pl.BlockSpec((k, tn), lambda i, j: (0, j)),
                  pl.BlockSpec((tm, tn), lambda i, j: (i, j))] + ride_in_specs,
        out_specs=[pl.BlockSpec((tm, tn), lambda i, j: (i, j))] + ride_out_specs,
        out_shape=[_sds((t, n), F32)] + ride_out, scratch_shapes=ride_sems,
        compiler_params=_params(("arbitrary", "arbitrary") if ride else ("parallel", "parallel")),
    )(a, b, res, *ride_in)
    return (out[0], out[1:]) if ride else out[0]


def _mm(a, b, out_dtype, tm, tn, name, scale=1.0, ride=None):
    m, k = a.shape
    if b.ndim == 3:
        nh = b.shape[2] // tn
        n = 2 * b.shape[2]
        b_spec = pl.BlockSpec((None, k, tn), lambda i, j: (j // nh, 0, j % nh))
    else:
        n = b.shape[1]
        b_spec = pl.BlockSpec((k, tn), lambda i, j: (0, j))
    grid = (m // tm, n // tn)
    ride_in, ride_in_specs, ride_out, ride_out_specs, ride_sems = _ride_specs(ride)

    def body(a_ref, b_ref, o_ref):
        o_ref[...] = (scale * _dot(a_ref[...], b_ref[...])).astype(out_dtype)

    res = pl.pallas_call(
        _riding(body, 2, 1, ride, grid), name=name, grid=grid,
        in_specs=[pl.BlockSpec((tm, k), lambda i, j: (i, 0)), b_spec] + ride_in_specs,
        out_specs=[pl.BlockSpec((tm, tn), lambda i, j: (i, j))] + ride_out_specs,
        out_shape=[_sds((m, n), out_dtype)] + ride_out, scratch_shapes=ride_sems,
        compiler_params=_params(("arbitrary", "arbitrary") if ride else ("parallel", "parallel")),
    )(a, b, *ride_in)
    return (res[0], res[1:]) if ride else res[0]


def _mm_nt(a, b, name, with_t=False):
    m, n = a.shape
    k = b.shape[0]
    tm = _row_tile(m)
    tk = k

    def body(a_ref, b_ref, o_ref, *t_ref):
        r = _dot_nt(a_ref[...], b_ref[...])
        o_ref[...] = r
        if with_t:
            t_ref[0][...] = r.T.astype(CDT)

    out_specs = [pl.BlockSpec((tm, tk), lambda i, j: (i, j))]
    out_shape = [_sds((m, k), F32)]
    if with_t:
        out_specs.append(pl.BlockSpec((tk, tm), lambda i, j: (j, i)))
        out_shape.append(_sds((k, m), CDT))
    res = pl.pallas_call(
        body, name=name, grid=(m // tm, k // tk),
        in_specs=[pl.BlockSpec((tm, n), lambda i, j: (i, 0)), pl.BlockSpec((tk, n), lambda i, j: (j, 0))],
        out_specs=out_specs, out_shape=out_shape,
        compiler_params=_params(("parallel", "parallel")),
    )(a, b)
    return res if with_t else res[0]


def _ffn_bwd_mid(dhb, w_out, gu, name, ride=None):
    t = dhb.shape[0]
    tm = _row_tile(t)
    tn = FT
    grid = (F // tn, t // tm)
    ride_in, ride_in_specs, ride_out, ride_out_specs, ride_sems = _ride_specs(ride)

    def body(dh_ref, w_ref, gu_ref, o_ref):
        ds = _dot_nt(dh_ref[...] * 0.5, w_ref[...])
        g = gu_ref[0].astype(F32)
        u = gu_ref[1].astype(F32)
        sg = _sigmoid(g)
        o_ref[0] = (ds * u * (sg * (1.0 + g * (1.0 - sg)))).astype(CDT)
        o_ref[1] = (ds * (g * sg)).astype(CDT)

    res = pl.pallas_call(
        _riding(body, 3, 1, ride, grid), name=name, grid=grid,
        in_specs=[pl.BlockSpec((tm, D), lambda j, i: (i, 0)),
                  pl.BlockSpec((tn, D), lambda j, i: (j, 0)),
                  pl.BlockSpec((2, tm, tn), lambda j, i: (0, i, j))] + ride_in_specs,
        out_specs=[pl.BlockSpec((2, tm, tn), lambda j, i: (0, i, j))] + ride_out_specs,
        out_shape=[_sds((2, t, F), CDT)] + ride_out, scratch_shapes=ride_sems,
        compiler_params=_params(("arbitrary", "arbitrary") if ride else ("parallel", "parallel")),
    )(dhb, w_out, gu, *ride_in)
    return (res[0], res[1:]) if ride else res[0]


def _rms_bwd_rows(da_, x, g, dres, i, dh_ref, dhb_ref, dg_ref):
    r = lax.rsqrt(jnp.mean(x * x, axis=-1, keepdims=True) + EPS)
    xh = x * r
    day = da_ * g
    dh = dres + r * (day - xh * jnp.mean(day * xh, axis=-1, keepdims=True))
    dh_ref[...] = dh
    dhb_ref[...] = dh.astype(CDT)

    @pl.when(i == 0)
    def _():
        dg_ref[...] = jnp.zeros(dg_ref.shape, F32)

    dg_ref[0:1, :] += jnp.sum(da_ * xh, axis=0, keepdims=True)


def _ffn_bwd_in(dgu, w_in, h, g, dres, name, ride=None):
    t = dgu.shape[1]
    tm = _row_tile(t)
    grid = (t // tm,)
    ride_in, ride_in_specs, ride_out, ride_out_specs, ride_sems = _ride_specs(ride)

    def body(dg_ref, wg_ref, wu_ref, h_ref, g_ref, dr_ref, dh_ref, dhb_ref, dgn_ref):
        da_ = _dot_nt(dg_ref[0], wg_ref[...]) + _dot_nt(dg_ref[1], wu_ref[...])
        _rms_bwd_rows(da_, h_ref[...], g_ref[...], dr_ref[...], pl.program_id(0), dh_ref, dhb_ref, dgn_ref)

    row = pl.BlockSpec((tm, D), lambda i: (i, 0))
    res = pl.pallas_call(
        _riding(body, 6, 3, ride, grid), name=name, grid=grid,
        in_specs=[pl.BlockSpec((2, tm, F), lambda i: (0, i, 0)),
                  pl.BlockSpec((D, F), lambda i: (0, 0)),
                  pl.BlockSpec((D, F), lambda i: (0, 1)),
                  row, pl.BlockSpec((1, D), lambda i: (0, 0)), row] + ride_in_specs,
        out_specs=[row, row, pl.BlockSpec((8, D), lambda i: (0, 0))] + ride_out_specs,
        out_shape=[_sds((t, D), F32), _sds((t, D), CDT), _sds((8, D), F32)] + ride_out, scratch_shapes=ride_sems,
        compiler_params=_params(("arbitrary",)),
    )(dgu, w_in, w_in, h, g, dres, *ride_in)
    return (*res[:3], res[3:]) if ride else res


def _mm_nt_rms(a, b, h, g, dres, name):
    t, n = a.shape
    tm = _row_tile(t)

    def body(a_ref, b_ref, h_ref, g_ref, dr_ref, dh_ref, dhb_ref, dgn_ref):
        da_ = _dot_nt(a_ref[...], b_ref[...])
        _rms_bwd_rows(da_, h_ref[...], g_ref[...], dr_ref[...], pl.program_id(0), dh_ref, dhb_ref, dgn_ref)

    row = pl.BlockSpec((tm, D), lambda i: (i, 0))
    return pl.pallas_call(
        body, name=name, grid=(t // tm,),
        in_specs=[pl.BlockSpec((tm, n), lambda i: (i, 0)), pl.BlockSpec((D, n), lambda i: (0, 0)),
                  row, pl.BlockSpec((1, D), lambda i: (0, 0)), row],
        out_specs=[row, row, pl.BlockSpec((8, D), lambda i: (0, 0))],
        out_shape=[_sds((t, D), F32), _sds((t, D), CDT), _sds((8, D), F32)],
        compiler_params=_params(("arbitrary",)),
    )(a, b, h, g, dres)


def _loss(h, target, name):
    t = h.shape[0]

    def body(h_ref, t_ref, dh_ref, dhb_ref, l_ref):
        i = pl.program_id(0)

        @pl.when(i == 0)
        def _():
            l_ref[...] = jnp.zeros(l_ref.shape, F32)
            dh_ref[...] = jnp.zeros(dh_ref.shape, F32)
            dhb_ref[...] = jnp.zeros(dhb_ref.shape, CDT)

        @pl.when(i > 0)
        def _():
            err = h_ref[...] - t_ref[...]
            l_ref[...] += (0.5 / D) * jnp.sum(err * err)
            d = err * (1.0 / D)
            dh_ref[...] = d
            dhb_ref[...] = d.astype(CDT)

    row = pl.BlockSpec((BLK, D), lambda i: (i, 0))
    return pl.pallas_call(
        body, name=name, grid=(t // BLK,),
        in_specs=[row, pl.BlockSpec((BLK, D), lambda i: (jnp.maximum(i - 1, 0), 0))],
        out_specs=[row, row, pl.BlockSpec((8, 128), lambda i: (0, 0))],
        out_shape=[_sds((t, D), F32), _sds((t, D), CDT), _sds((8, 128), F32)],
        compiler_params=_params(("arbitrary",)),
    )(h, target)


def _block_diag():
    return (_iota((128, 128), 0) // HD == _iota((128, 128), 1) // HD).astype(F32)


def _head_sums(v, bd):
    hi = v.astype(CDT)
    rest = (v - hi.astype(F32)).astype(CDT)
    b = bd.astype(CDT)
    return _dot(hi, b) + _dot(rest, b)


def _dup_halves(x, lo):
    sw = pltpu.roll(x, 64, 1)
    return jnp.where(lo, x, sw), jnp.where(lo, sw, x)


def _qknorm_fwd(proj, gfq, gfk, gsq, gsk, fb, name):
    t = proj.shape[0]
    tm = _row_tile(t)

    def body(qa, ka, va, qb, kb, vb, fa, gfq_r, gfk_r, gsq_r, gsk_r, fb_r,
             qf_o, kf_o, vf_o, qs_o, kse_o, vse_o, c_o, ct_o, qft_o, carry):
        i = pl.program_id(0)
        bd = _block_diag()
        lane = _iota((1, 128), 1)
        lo = lane < HD

        def hnorm(x, g):
            ms = _head_sums(x * x, bd) * (1.0 / HD)
            return x * lax.rsqrt(ms + EPS) * g

        for ch in range(4):
            sl = slice(128 * ch, 128 * (ch + 1))
            qn = hnorm(qa[:, sl], gfq_r[:, sl]) * 0.125
            qf_o[:, sl] = qn.astype(CDT)
            qft_o[sl, :] = qn.T.astype(CDT)
            kf_o[:, sl] = hnorm(ka[:, sl], gfk_r[:, sl]).astype(CDT)
            qs_o[:, sl] = (hnorm(qb[:, sl], gsq_r[:, sl]) * 0.125).astype(CDT)
        vf_o[...] = va[...].astype(CDT)
        k0, k1 = _dup_halves(hnorm(kb[...], gsk_r[...]), lo)
        kse_o[0] = k0.astype(CDT)
        kse_o[1] = k1.astype(CDT)
        v0, v1 = _dup_halves(vb[...], lo)
        vse_o[0] = v0.astype(CDT)
        vse_o[1] = v1.astype(CDT)

        z = fa[...] + fb_r[...]
        lf = jnp.minimum(z, 0.0) - jnp.log(1.0 + jnp.exp(-jnp.abs(z)))
        lf = jnp.where(lane < 8, lf, 0.0)
        ltri = (_iota((tm, tm), 1) <= _iota((tm, tm), 0)).astype(F32)

        @pl.when(i == 0)
        def _():
            carry[...] = jnp.zeros(carry.shape, F32)

        c = _dot_hi(ltri, lf) + carry[0:1, :]
        carry[0:1, :] = c[tm - 1:tm, :]
        c_o[...] = c
        ct_o[...] = c.T[0:8, :]

    def col(width, off):
        return pl.BlockSpec((tm, width), lambda i: (i, off // width))

    def vec(width):
        return pl.BlockSpec((1, width), lambda i: (0, 0))

    return pl.pallas_call(
        body, name=name, grid=(t // tm,),
        in_specs=[col(512, QA), col(512, KA), col(512, VA), col(512, QB), col(128, KB), col(128, VB), col(128, FA),
                  vec(512), vec(512), vec(512), vec(128), vec(128)],
        out_specs=[pl.BlockSpec((tm, 512), lambda i: (i, 0))] * 4
        + [pl.BlockSpec((2, tm, 128), lambda i: (0, i, 0))] * 2
        + [pl.BlockSpec((tm, 128), lambda i: (i, 0)), pl.BlockSpec((8, tm), lambda i: (0, i)),
           pl.BlockSpec((512, tm), lambda i: (0, i))],
        out_shape=[_sds((t, 512), CDT)] * 4 + [_sds((2, t, 128), CDT)] * 2
        + [_sds((t, 128), F32), _sds((8, t), F32), _sds((512, t), CDT)],
        scratch_shapes=[pltpu.VMEM((8, 128), F32)],
        compiler_params=_params(("arbitrary",)),
    )(proj, proj, proj, proj, proj, proj, proj, gfq, gfk, gsq, gsk, fb)


def _qknorm_bwd(proj, dqf, dkf, dvf, dqs, dkse, dvse, dcq, dck, dga, dgb, gfq, gfk, gsq, gsk, fb, name):
    t = proj.shape[0]
    tm = _row_tile(t)
    nt = t // tm

    def body(qa, ka, qb, kb, fa, dqf_r, dkf_r, dvf_r, dqs_r, dkse_r, dvse_r, dcq_r, dck_r, dga_r, dgb_r,
             gfq_r, gfk_r, gsq_r, gsk_r, fb_r, dp_o, dgn_o, carry, acc):
        i = pl.program_id(0)
        bd = _block_diag()
        lane = _iota((1, 128), 1)
        lo = lane < HD

        @pl.when(i == 0)
        def _():
            carry[...] = jnp.zeros(carry.shape, F32)
            acc[...] = jnp.zeros(acc.shape, F32)

        def hnorm_bwd(x, g, dy):
            r = lax.rsqrt(_head_sums(x * x, bd) * (1.0 / HD) + EPS)
            xh = x * r
            day = dy * g
            dx = r * (day - xh * (_head_sums(day * xh, bd) * (1.0 / HD)))
            return dx, jnp.sum(dy * xh, axis=0, keepdims=True)

        for ch in range(4):
            sl = slice(128 * ch, 128 * (ch + 1))
            dx, dg = hnorm_bwd(qa[:, sl], gfq_r[:, sl], dqf_r[:, sl] * 0.125)
            dp_o[:, QA + 128 * ch:QA + 128 * (ch + 1)] = dx.astype(CDT)
            acc[0:1, sl] += dg
            dx, dg = hnorm_bwd(ka[:, sl], gfk_r[:, sl], dkf_r[:, sl])
            dp_o[:, KA + 128 * ch:KA + 128 * (ch + 1)] = dx.astype(CDT)
            acc[1:2, sl] += dg
            dx, dg = hnorm_bwd(qb[:, sl], gsq_r[:, sl], dqs_r[:, sl] * 0.125)
            dp_o[:, QB + 128 * ch:QB + 128 * (ch + 1)] = dx.astype(CDT)
            acc[2:3, sl] += dg
        dp_o[:, VA:VA + 512] = dvf_r[...].astype(CDT)
        dp_o[:, GA:GA + D] = dga_r[...]
        dp_o[:, GB:GB + D] = dgb_r[...]

        def fold(x):
            e0 = x[0]
            e1 = x[1]
            return jnp.where(lo, e0 + pltpu.roll(e0, 64, 1), e1 + pltpu.roll(e1, 64, 1))

        dx, dg = hnorm_bwd(kb[...], gsk_r[...], fold(dkse_r))
        dp_o[:, KB:KB + 128] = dx.astype(CDT)
        acc[3:4, 0:128] += dg
        dp_o[:, VB:VB + 128] = fold(dvse_r).astype(CDT)

        rr = _iota((512, 128), 0)
        hh = _iota((512, 128), 1)
        sel = ((rr == (hh >> 1) * 128 + (hh & 1) * HD) & (hh < 8)).astype(F32)
        dcs = _dot_hi(dcq_r[...] - dck_r[...], sel)
        utri = (_iota((tm, tm), 1) >= _iota((tm, tm), 0)).astype(F32)
        dlf = _dot_hi(utri, dcs) + carry[0:1, :]
        carry[0:1, :] = dlf[0:1, :]
        z = fa[...] + fb_r[...]
        dfa = jnp.where(lane < 8, dlf * _sigmoid(-z), 0.0)
        dp_o[:, FA:FA + 128] = dfa.astype(CDT)
        acc[4:5, 0:128] += jnp.sum(dfa, axis=0, keepdims=True)

        @pl.when(i == nt - 1)
        def _():
            foldm = ((_iota((512, 128), 0) & (HD - 1)) == _iota((512, 128), 1)).astype(F32)
            dgn_o[...] = _dot_hi(acc[...], foldm)

    def col(width, off):
        return pl.BlockSpec((tm, width), lambda i: (nt - 1 - i, off // width))

    def rows(width):
        return pl.BlockSpec((tm, width), lambda i: (nt - 1 - i, 0))

    def vec(width):
        return pl.BlockSpec((1, width), lambda i: (0, 0))

    pair = pl.BlockSpec((2, tm, 128), lambda i: (0, nt - 1 - i, 0))
    return pl.pallas_call(
        body, name=name, grid=(nt,),
        in_specs=[col(512, QA), col(512, KA), col(512, QB), col(128, KB), col(128, FA),
                  rows(512), rows(512), rows(512), rows(512), pair, pair, rows(512), rows(512), rows(D), rows(D),
                  vec(512), vec(512), vec(512), vec(128), vec(128)],
        out_specs=[rows(DP), pl.BlockSpec((8, 128), lambda i: (0, 0))],
        out_shape=[_sds((t, DP), CDT), _sds((8, 128), F32)],
        scratch_shapes=[pltpu.VMEM((8, 128), F32), pltpu.VMEM((8, 512), F32)],
        compiler_params=_params(("arbitrary",)),
    )(proj, proj, proj, proj, proj, dqf, dkf, dvf, dqs, dkse, dvse, dcq, dck, dga, dgb, gfq, gfk, gsq, gsk, fb)


def _gate_out_fwd(ofox, oswa, wbf, wbs, proj, w_out, h, name):
    t = ofox.shape[0]
    tm = _row_tile(t)

    def body(of_r, os_r, wf_r, ws_r, ga_r, gb_r, wo_r, h_r, ho_o, yt_o, pf_o, ps_o, oft_o, ost_o):
        pf = _dot(of_r[...], wf_r[...])
        ps = _dot(os_r[...], ws_r[...])
        y = _sigmoid(ga_r[...]) * pf + _sigmoid(gb_r[...]) * ps
        ho_o[...] = h_r[...] + _dot(y.astype(CDT), wo_r[...])
        yt_o[...] = y.T.astype(CDT)
        pf_o[...] = pf.astype(CDT)
        ps_o[...] = ps.astype(CDT)
        oft_o[...] = of_r[...].astype(F32).T.astype(CDT)
        ost_o[...] = os_r[...].astype(F32).T.astype(CDT)

    row = pl.BlockSpec((tm, D), lambda i: (i, 0))
    half = pl.BlockSpec((tm, 512), lambda i: (i, 0))
    whole = lambda r: pl.BlockSpec((r, D), lambda i: (0, 0))
    tcol = lambda r: pl.BlockSpec((r, tm), lambda i: (0, i))
    return pl.pallas_call(
        body, name=name, grid=(t // tm,),
        in_specs=[half, half, whole(512), whole(512),
                  pl.BlockSpec((tm, D), lambda i: (i, GA // D)), pl.BlockSpec((tm, D), lambda i: (i, GB // D)),
                  whole(D), row],
        out_specs=[row, tcol(D), row, row, tcol(512), tcol(512)],
        out_shape=[_sds((t, D), F32), _sds((D, t), CDT), _sds((t, D), CDT), _sds((t, D), CDT),
                   _sds((512, t), CDT), _sds((512, t), CDT)],
        compiler_params=_params(("parallel",)),
    )(ofox, oswa, wbf, wbs, proj, proj, w_out, h)


def _gate_out_bwd(dhb, w_out, pf, ps, proj, name):
    t = dhb.shape[0]
    tm = _row_tile(t)

    def body(dh_r, wo_r, pf_r, ps_r, ga_r, gb_r, dpf_o, dps_o, dga_o, dgb_o):
        dy_ = _dot_nt(dh_r[...], wo_r[...])
        sa = _sigmoid(ga_r[...])
        sb = _sigmoid(gb_r[...])
        dpf_o[...] = (dy_ * sa).astype(CDT)
        dps_o[...] = (dy_ * sb).astype(CDT)
        dga_o[...] = (dy_ * pf_r[...].astype(F32) * (sa * (1.0 - sa))).astype(CDT)
        dgb_o[...] = (dy_ * ps_r[...].astype(F32) * (sb * (1.0 - sb))).astype(CDT)

    row = pl.BlockSpec((tm, D), lambda i: (i, 0))
    return pl.pallas_call(
        body, name=name, grid=(t // tm,),
        in_specs=[row, pl.BlockSpec((D, D), lambda i: (0, 0)), row, row,
                  pl.BlockSpec((tm, D), lambda i: (i, GA // D)), pl.BlockSpec((tm, D), lambda i: (i, GB // D))],
        out_specs=[row] * 4,
        out_shape=[_sds((t, D), CDT)] * 4,
        compiler_params=_params(("parallel",)),
    )(dhb, w_out, pf, ps, proj, proj)


def _tri_steps(n, by_key):
    if by_key:
        pairs = [(i, j) for j in range(n) for i in range(j, n)]
    else:
        pairs = [(i, j) for i in range(n) for j in range(i + 1)]
    return (np.array([p[0] for p in pairs], np.int32), np.array([p[1] for p in pairs], np.int32))


def _head_col(blk, lane, h):
    return jnp.sum(jnp.where(lane == h, blk, 0.0), axis=1, keepdims=True)


def _head_row(blk, sub, h):
    return jnp.sum(jnp.where(sub == h, blk, 0.0), axis=0, keepdims=True)


def _ride_specs(ride):
    if ride is None:
        return [], [], [], [], []
    kind, srcs, outs, layer, items = ride
    return list(srcs), [ANY] * len(srcs), list(outs), [ANY] * len(outs), _dma_sems(3 * len(srcs))


def _ride_start(ride, srcs, dsts, send_sems, recv_sems):
    for cp in _ici_copies(ride[0], srcs, dsts, send_sems, recv_sems, ride[3], recv=False, items=ride[4])[0]:
        cp.start()


def _ride_wait(ride, srcs, dsts, send_sems, recv_sems):
    sends, recvs = _ici_copies(ride[0], srcs, dsts, send_sems, recv_sems, ride[3], items=ride[4])
    for cp in recvs:
        cp.wait_recv()
    for cp in sends:
        cp.wait_send()


def _riding(body, n_in, n_out, ride, grid):
    if ride is None:
        return body
    nr = len(ride[1])

    def wrapped(*refs):
        ins, srcs = refs[:n_in], refs[n_in:n_in + nr]
        outs, dsts = refs[n_in + nr:n_in + nr + n_out], refs[n_in + nr + n_out:n_in + 2 * nr + n_out]
        scratch, sems = refs[n_in + 2 * nr + n_out:-2], refs[-2:]
        first = pl.program_id(0) == 0
        last = pl.program_id(0) == grid[0] - 1
        for a in range(1, len(grid)):
            first = first & (pl.program_id(a) == 0)
            last = last & (pl.program_id(a) == grid[a] - 1)

        @pl.when(first)
        def _():
            _ride_start(ride, srcs, dsts, *sems)

        body(*ins, *outs, *scratch)

        @pl.when(last)
        def _():
            _ride_wait(ride, srcs, dsts, *sems)

    return wrapped


def _fox_fwd(qf, kf, vf, c, ct, name, ride=None):
    t = qf.shape[0]
    ta = _row_tile(t)
    qi, kj = _tri_steps(t // ta, by_key=False)
    nsteps = len(qi)
    ride_in, ride_in_specs, ride_out, ride_out_specs, ride_sems = _ride_specs(ride)

    def body(qi_r, kj_r, q_r, k_r, v_r, c_r, ct_r, *rest):
        nr = len(ride_in)
        src_r, (o_o, lse_o), dst_o = rest[:nr], rest[nr:nr + 2], rest[nr + 2:2 * nr + 2]
        m_sc, l_sc, acc_sc, cq_sc, *sems = rest[2 * nr + 2:]
        p = pl.program_id(0)
        n = pl.program_id(1)
        i = qi_r[n]
        j = kj_r[n]
        lane = _iota((1, 128), 1)
        lo = lane < HD

        if ride is not None:
            @pl.when((p == 0) & (n == 0))
            def _():
                _ride_start(ride, src_r, dst_o, *sems)

        @pl.when(j == 0)
        def _():
            m_sc[...] = jnp.full(m_sc.shape, NEG, F32)
            l_sc[...] = jnp.zeros(l_sc.shape, F32)
            acc_sc[...] = jnp.zeros(acc_sc.shape, F32)
            for e in (0, 1):
                cq_sc[e] = jnp.broadcast_to(_head_col(c_r[...], lane, 2 * p + e), (ta, 128))

        def step(masked):
            q = q_r[...]
            k = k_r[...]
            vaug = jnp.concatenate([v_r[...], jnp.ones((ta, 128), CDT)], axis=1)
            if masked:
                rows = i * ta + _iota((ta, 1), 0)
                cols = j * ta + _iota((1, ta), 1)
                mask = (cols <= rows) & (cols >= PAD)
            sub = _iota((8, 1), 0)
            alphas, pvs = [], []
            for e in (0, 1):
                sel = lo if e == 0 else jnp.logical_not(lo)
                s = _dot_nt(jnp.where(sel, q, 0), k)
                ck = _head_row(ct_r[...], sub, 2 * p + e)
                cq = cq_sc[e]
                chunks = []
                for ch in range(ta // 128):
                    sl = slice(128 * ch, 128 * (ch + 1))
                    sc = s[:, sl] + cq - ck[:, sl]
                    if masked:
                        sc = jnp.where(mask[:, sl], sc, NEG)
                    chunks.append(sc)
                mx = chunks[0]
                for sc in chunks[1:]:
                    mx = jnp.maximum(mx, sc)
                m_prev = m_sc[e]
                m_new = jnp.maximum(m_prev, jnp.max(mx, axis=1, keepdims=True))
                alpha = jnp.exp(m_prev - m_new)
                pe = jnp.concatenate([jnp.exp(sc - m_new).astype(CDT) for sc in chunks], axis=1)
                pva = _dot(pe, vaug)
                l_sc[e] = alpha * l_sc[e] + pva[:, 128:]
                m_sc[e] = m_new
                alphas.append(alpha)
                pvs.append(pva[:, :128])
            acc_sc[...] = acc_sc[...] * jnp.where(lo, alphas[0], alphas[1]) + jnp.where(lo, pvs[0], pvs[1])

        edge = (j == i) | (j == 0)

        @pl.when(edge)
        def _():
            step(True)

        @pl.when(jnp.logical_not(edge))
        def _():
            step(False)

        @pl.when(j == i)
        def _():
            l = jnp.where(lo, l_sc[0], l_sc[1])
            o_o[...] = (acc_sc[...] / l).astype(CDT)
            lse_o[...] = jnp.where(lo, m_sc[0], m_sc[1]) + jnp.log(l)

        if ride is not None:
            @pl.when((p == NPAIR - 1) & (n == nsteps - 1))
            def _():
                _ride_wait(ride, src_r, dst_o, *sems)

    qblk = pl.BlockSpec((ta, 128), lambda p, n, qi_r, kj_r: (qi_r[n], p))
    kblk = pl.BlockSpec((ta, 128), lambda p, n, qi_r, kj_r: (kj_r[n], p))
    grid_spec = pltpu.PrefetchScalarGridSpec(
        num_scalar_prefetch=2, grid=(NPAIR, nsteps),
        in_specs=[qblk, kblk, kblk,
                  pl.BlockSpec((ta, 128), lambda p, n, qi_r, kj_r: (qi_r[n], 0)),
                  pl.BlockSpec((8, ta), lambda p, n, qi_r, kj_r: (0, kj_r[n]))] + ride_in_specs,
        out_specs=[qblk, qblk] + ride_out_specs,
        scratch_shapes=[pltpu.VMEM((2, ta, 128), F32), pltpu.VMEM((2, ta, 128), F32), pltpu.VMEM((ta, 128), F32),
                        pltpu.VMEM((2, ta, 128), F32)] + ride_sems,
    )
    return pl.pallas_call(
        body, name=name, grid_spec=grid_spec,
        out_shape=[_sds((t, 512), CDT), _sds((t, 512), F32)] + ride_out,
        compiler_params=_params(("arbitrary", "arbitrary")),
    )(jnp.asarray(qi), jnp.asarray(kj), qf, kf, vf, c, ct, *ride_in)


def _fox_bwd(qf, qft, kf, vf, c, ct, o, lse, do, dot, name, ride=None):
    t = qf.shape[0]
    ta = _row_tile(t)
    nq = t // ta
    qi, kj = _tri_steps(nq, by_key=False)
    nsteps = len(qi)
    ride_in, ride_in_specs, ride_out, ride_out_specs, ride_sems = _ride_specs(ride)

    def body(qi_r, kj_r, q_r, qt_r, k_r, v_r, c_r, ct_r, o_r, lse_r, do_r, dot_r, *rest):
        nr = len(ride_in)
        src_r, (dq_o, dcq_o, dk_o, dv_o, dck_o), dst_o = rest[:nr], rest[nr:nr + 5], rest[nr + 5:2 * nr + 5]
        lse_sc, dl_sc, cq_sc, dq_sc, dcq_sc, dkt_sc, dvt_sc, dckt_sc, *sems = rest[2 * nr + 5:]
        p = pl.program_id(0)
        n = pl.program_id(1)
        i = qi_r[n]
        j = kj_r[n]
        lane = _iota((1, 128), 1)
        lo = lane < HD
        top = _iota((128, 1), 0) < HD

        if ride is not None:
            @pl.when((p == 0) & (n == 0))
            def _():
                _ride_start(ride, src_r, dst_o, *sems)

        @pl.when(n == 0)
        def _():
            dkt_sc[...] = jnp.zeros(dkt_sc.shape, F32)
            dvt_sc[...] = jnp.zeros(dvt_sc.shape, F32)
            dckt_sc[...] = jnp.zeros(dckt_sc.shape, F32)

        @pl.when(j == 0)
        def _():
            dq_sc[...] = jnp.zeros(dq_sc.shape, F32)
            dcq_sc[...] = jnp.zeros(dcq_sc.shape, F32)
            dd = do_r[...] * o_r[...].astype(F32)
            lse = lse_r[...]
            for e in (0, 1):
                sel = lo if e == 0 else jnp.logical_not(lo)
                cq_sc[e] = jnp.broadcast_to(_head_col(c_r[...], lane, 2 * p + e), (ta, 128))
                dl_sc[e] = jnp.broadcast_to(jnp.sum(jnp.where(sel, dd, 0.0), axis=1, keepdims=True), (ta, 128))
                lse_sc[e] = jnp.broadcast_to(lse[:, HD * e:HD * e + 1], (ta, 128))

        def step(masked):
            q = q_r[...]
            qt = qt_r[...]
            k = k_r[...]
            v = v_r[...]
            dob = do_r[...].astype(CDT)
            dot_ = dot_r[...]
            ones = jnp.ones((ta, 128), CDT)
            ones16 = jnp.ones((16, ta), CDT)
            if masked:
                rows = i * ta + _iota((ta, 1), 0)
                cols = j * ta + _iota((1, ta), 1)
                mask = (cols <= rows) & (cols >= PAD)
            sub = _iota((8, 1), 0)
            for e in (0, 1):
                sel = lo if e == 0 else jnp.logical_not(lo)
                rsel = top if e == 0 else jnp.logical_not(top)
                s = _dot_nt(jnp.where(sel, q, 0), k)
                dp = _dot_nt(jnp.where(sel, dob, 0), v)
                ck = _head_row(ct_r[...], sub, 2 * p + e)
                cq, lse_e, dl = cq_sc[e], lse_sc[e], dl_sc[e]
                prs, dss = [], []
                for ch in range(ta // 128):
                    sl = slice(128 * ch, 128 * (ch + 1))
                    sc = s[:, sl] + cq - ck[:, sl]
                    if masked:
                        sc = jnp.where(mask[:, sl], sc, NEG)
                    pr = jnp.exp(sc - lse_e)
                    prs.append(pr.astype(CDT))
                    dss.append((pr * (dp[:, sl] - dl)).astype(CDT))
                pb = jnp.concatenate(prs, axis=1)
                dsb = jnp.concatenate(dss, axis=1)
                dvt_sc[j] += _dot(jnp.where(rsel, dot_, 0), pb)
                dkc = _dot(jnp.concatenate([jnp.where(rsel, qt, 0), ones16], axis=0), dsb)
                dkt_sc[j] += dkc[0:128]
                dckt_sc[j, 0:8, :] += jnp.where(sub == e, dkc[128:136], 0.0)
                dqa = _dot(dsb, jnp.concatenate([jnp.where(sel, k, 0), ones], axis=1))
                dq_sc[...] += dqa[:, :128]
                dcq_sc[e] += dqa[:, 128:]

        edge = (j == i) | (j == 0)

        @pl.when(edge)
        def _():
            step(True)

        @pl.when(jnp.logical_not(edge))
        def _():
            step(False)

        @pl.when(j == i)
        def _():
            dq_o[...] = dq_sc[...]
            dcq_o[...] = jnp.where(lo, dcq_sc[0], dcq_sc[1])

        @pl.when(n == nsteps - 1)
        def _():
            spread = (_iota((128, 128), 1) == _iota((128, 128), 0) // HD).astype(F32)
            for jb in range(nq):
                rs = slice(jb * ta, (jb + 1) * ta)
                dk_o[rs, :] = dkt_sc[jb].T
                dv_o[rs, :] = dvt_sc[jb].T
                dck_o[rs, :] = _dot_hi(spread, dckt_sc[jb]).T

        if ride is not None:
            @pl.when((p == NPAIR - 1) & (n == nsteps - 1))
            def _():
                _ride_wait(ride, src_r, dst_o, *sems)

    qblk = pl.BlockSpec((ta, 128), lambda p, n, qi_r, kj_r: (qi_r[n], p))
    qtblk = pl.BlockSpec((128, ta), lambda p, n, qi_r, kj_r: (p, qi_r[n]))
    kblk = pl.BlockSpec((ta, 128), lambda p, n, qi_r, kj_r: (kj_r[n], p))
    whole = pl.BlockSpec((t, 128), lambda p, n, qi_r, kj_r: (0, p))
    grid_spec = pltpu.PrefetchScalarGridSpec(
        num_scalar_prefetch=2, grid=(NPAIR, nsteps),
        in_specs=[qblk, qtblk, kblk, kblk,
                  pl.BlockSpec((ta, 128), lambda p, n, qi_r, kj_r: (qi_r[n], 0)),
                  pl.BlockSpec((8, ta), lambda p, n, qi_r, kj_r: (0, kj_r[n])),
                  qblk, qblk, qblk, qtblk] + ride_in_specs,
        out_specs=[qblk, qblk, whole, whole, whole] + ride_out_specs,
        scratch_shapes=[pltpu.VMEM((2, ta, 128), F32)] * 3 + [pltpu.VMEM((ta, 128), F32), pltpu.VMEM((2, ta, 128), F32)]
        + [pltpu.VMEM((nq, 128, ta), F32)] * 3 + ride_sems,
    )
    return pl.pallas_call(
        body, name=name, grid_spec=grid_spec,
        out_shape=[_sds((t, 512), F32)] * 5 + ride_out,
        compiler_params=_params(("arbitrary", "arbitrary")),
    )(jnp.asarray(qi), jnp.asarray(kj), qf, qft, kf, vf, c, ct, o, lse, do, dot, *ride_in)


def _bucket_table():
    r = np.arange(BLK)[:, None]
    c = np.arange(3 * BLK)[None, :]
    d = np.where(c < BLK, r + BLK - c, r - (c - BLK))
    n = np.maximum(d, 0)
    max_exact = N_BUCKETS // 2
    nf = np.maximum(n, 1).astype(np.float32)
    large = max_exact + (np.log(nf / max_exact) / math.log(BLK / max_exact) * (N_BUCKETS - max_exact)).astype(np.int32)
    large = np.minimum(large, N_BUCKETS - 1)
    b = np.where(n < max_exact, n, large)
    return np.where(c < 2 * BLK, b, N_BUCKETS - 1).astype(np.int32)


def _bias_fwd(table, name):
    bucket = jnp.asarray(_bucket_table())

    def body(tab_r, b_r, o_o):
        h = pl.program_id(0)
        b = b_r[...]
        acc = jnp.zeros(b.shape, F32)
        for k in range(N_BUCKETS):
            acc = jnp.where(b == k, tab_r[k, h], acc)
        o_o[...] = acc

    return pl.pallas_call(
        body, name=name, grid=(8,),
        in_specs=[pl.BlockSpec(memory_space=pltpu.SMEM), pl.BlockSpec((BLK, 3 * BLK), lambda h: (0, 0))],
        out_specs=pl.BlockSpec((None, BLK, 3 * BLK), lambda h: (h, 0, 0)),
        out_shape=_sds((8, BLK, 3 * BLK), F32),
        compiler_params=_params(("parallel",)),
    )(table, bucket)


def _bias_bwd(dbias, name):
    bucket = jnp.asarray(_bucket_table())

    def body(d_r, b_r, o_o):
        h = pl.program_id(0)
        b = b_r[...]
        d = d_r[...]
        lane = _iota((1, 128), 1)
        row = jnp.zeros((1, 128), F32)
        for k in range(N_BUCKETS):
            row = jnp.where(lane == k, jnp.sum(jnp.where(b == k, d, 0.0)), row)
        o_o[pl.ds(h, 1), :] = row

    return pl.pallas_call(
        body, name=name, grid=(8,),
        in_specs=[pl.BlockSpec((None, BLK, 3 * BLK), lambda h: (h, 0, 0)), pl.BlockSpec((BLK, 3 * BLK), lambda h: (0, 0))],
        out_specs=pl.BlockSpec((8, 128), lambda h: (0, 0)),
        out_shape=_sds((8, 128), F32),
        compiler_params=_params(("arbitrary",)),
    )(dbias, bucket)


def _swa_valid(i):
    r = _iota((BLK, 1), 0)
    c = _iota((1, 3 * BLK), 1)
    prev = (c < BLK) & (c > r) & (i >= 1) & ((i - 1) * BLK + c >= PAD)
    cc = c - BLK
    cur = (c >= BLK) & (c < 2 * BLK) & (cc <= r) & (i * BLK + cc >= PAD)
    cm = c - 2 * BLK
    meta = (c >= 2 * BLK) & (cm >= PAD) & (i * BLK + r - cm >= BLK)
    return prev | cur | meta


def _swa_kv_specs(ta):
    nb = ta // BLK
    return [pl.BlockSpec((None, BLK, 128), lambda p, i: (p // 2, jnp.maximum(i * nb - 1, 0), 0)),
            pl.BlockSpec((None, ta, 128), lambda p, i: (p // 2, i, 0)),
            pl.BlockSpec((None, BLK, 128), lambda p, i: (p // 2, 0, 0))]


def _swa_fwd(qs, kse, vse, bias, sinks, name, ride=None):
    t = qs.shape[0]
    ta = _row_tile(t)
    nb = ta // BLK
    grid = (NPAIR, t // ta)
    ride_in, ride_in_specs, ride_out, ride_out_specs, ride_sems = _ride_specs(ride)

    def body(sink_r, q_r, kp_r, kc_r, km_r, vp_r, vc_r, vm_r, b_r, o_o, lse_o):
        p = pl.program_id(0)
        i = pl.program_id(1)
        lo = _iota((1, 128), 1) < HD
        k4 = jnp.concatenate([kp_r[...], kc_r[...]], axis=0)
        v4 = jnp.concatenate([vp_r[...], vc_r[...]], axis=0)
        for b in range(nb):
            rows = slice(BLK * b, BLK * (b + 1))
            q = q_r[rows, :]
            k3 = jnp.concatenate([k4[BLK * b:BLK * (b + 2)], km_r[...]], axis=0)
            v3 = jnp.concatenate([v4[BLK * b:BLK * (b + 2)], vm_r[...]], axis=0)
            valid = _swa_valid(i * nb + b)
            outs, lses = [], []
            for e in (0, 1):
                sel = lo if e == 0 else jnp.logical_not(lo)
                s = _dot_nt(jnp.where(sel, q, 0), k3) + b_r[e]
                s = jnp.where(valid, s, NEG)
                sink = sink_r[2 * p + e]
                mx = jnp.maximum(jnp.max(s, axis=1, keepdims=True), sink)
                pe = jnp.exp(s - mx)
                den = jnp.sum(pe, axis=1, keepdims=True) + jnp.exp(sink - mx)
                outs.append(_dot(pe.astype(CDT), v3) / den)
                lses.append(mx + jnp.log(den))
            o_o[rows, :] = jnp.where(lo, outs[0], outs[1]).astype(CDT)
            lse_o[rows, :] = jnp.where(lo, lses[0], lses[1])

    qblk = pl.BlockSpec((ta, 128), lambda p, i: (i, p))
    res = pl.pallas_call(
        _riding(body, 9, 2, ride, grid), name=name, grid=grid,
        in_specs=[pl.BlockSpec(memory_space=pltpu.SMEM), qblk] + _swa_kv_specs(ta) + _swa_kv_specs(ta)
        + [pl.BlockSpec((2, BLK, 3 * BLK), lambda p, i: (p, 0, 0))] + ride_in_specs,
        out_specs=[qblk, qblk] + ride_out_specs,
        out_shape=[_sds((t, 512), CDT), _sds((t, 512), F32)] + ride_out, scratch_shapes=ride_sems,
        compiler_params=_params(("arbitrary", "arbitrary") if ride else ("parallel", "parallel")),
    )(sinks, qs, kse, kse, kse, vse, vse, vse, bias, *ride_in)
    return (res[0], res[1], res[2:]) if ride else res


def _swa_bwd(qs, kse, vse, bias, sinks, o, lse, do, name):
    t = qs.shape[0]
    ta = _row_tile(t)
    nb = ta // BLK

    def body(sink_r, q_r, kp_r, kc_r, km_r, vp_r, vc_r, vm_r, b_r, o_r, lse_r, do_r,
             dq_o, dk_o, dv_o, db_o, dsk_o):
        p = pl.program_id(0)
        i = pl.program_id(1)
        lo = _iota((1, 128), 1) < HD

        @pl.when((i == 0) & (p % 2 == 0))
        def _():
            dk_o[...] = jnp.zeros(dk_o.shape, F32)
            dv_o[...] = jnp.zeros(dv_o.shape, F32)

        @pl.when(i == 0)
        def _():
            db_o[...] = jnp.zeros(db_o.shape, F32)
            dsk_o[...] = jnp.zeros(dsk_o.shape, F32)

        k4 = jnp.concatenate([kp_r[...], kc_r[...]], axis=0)
        v4 = jnp.concatenate([vp_r[...], vc_r[...]], axis=0)
        for b in range(nb):
            ib = i * nb + b
            rows = slice(BLK * b, BLK * (b + 1))
            q = q_r[rows, :]
            do_ = do_r[rows, :]
            dd = do_ * o_r[rows, :].astype(F32)
            lse = lse_r[rows, :]
            k3 = jnp.concatenate([k4[BLK * b:BLK * (b + 2)], km_r[...]], axis=0)
            v3 = jnp.concatenate([v4[BLK * b:BLK * (b + 2)], vm_r[...]], axis=0)
            valid = _swa_valid(ib)
            dq = jnp.zeros((BLK, 128), F32)
            dk3 = jnp.zeros((3 * BLK, 128), F32)
            dv3 = jnp.zeros((3 * BLK, 128), F32)
            dsink = []
            for e in (0, 1):
                sel = lo if e == 0 else jnp.logical_not(lo)
                qe = jnp.where(sel, q, 0)
                doe = jnp.where(sel, do_, 0.0).astype(CDT)
                lse_e = lse[:, HD * e:HD * e + 1]
                s = _dot_nt(qe, k3) + b_r[e]
                s = jnp.where(valid, s, NEG)
                pr = jnp.exp(s - lse_e)
                delta = jnp.sum(jnp.where(sel, dd, 0.0), axis=1, keepdims=True)
                ds = pr * (_dot_nt(doe, v3) - delta)
                db_o[e] += ds
                dsink.append(-jnp.sum(jnp.exp(sink_r[2 * p + e] - lse_e) * delta, axis=0, keepdims=True))
                dq = dq + _dot(ds.astype(CDT), jnp.where(sel, k3, 0))
                dk3 = dk3 + _dot(ds.T.astype(CDT), qe)
                dv3 = dv3 + _dot(pr.T.astype(CDT), doe)
            dq_o[rows, :] = dq
            prev = pl.ds(pl.multiple_of(jnp.maximum(ib - 1, 0) * BLK, BLK), BLK)
            cur = pl.ds(pl.multiple_of(ib * BLK, BLK), BLK)
            dk_o[prev, :] += dk3[0:BLK]
            dk_o[cur, :] += dk3[BLK:2 * BLK]
            dk_o[0:BLK, :] += dk3[2 * BLK:]
            dv_o[prev, :] += dv3[0:BLK]
            dv_o[cur, :] += dv3[BLK:2 * BLK]
            dv_o[0:BLK, :] += dv3[2 * BLK:]
            dsk_o[0:1, :] += jnp.where(lo, dsink[0], dsink[1])

    qblk = pl.BlockSpec((ta, 128), lambda p, i: (i, p))
    kvacc = pl.BlockSpec((None, t, 128), lambda p, i: (p // 2, 0, 0))
    bblk = pl.BlockSpec((2, BLK, 3 * BLK), lambda p, i: (p, 0, 0))
    return pl.pallas_call(
        body, name=name, grid=(NPAIR, t // ta),
        in_specs=[pl.BlockSpec(memory_space=pltpu.SMEM), qblk] + _swa_kv_specs(ta) + _swa_kv_specs(ta)
        + [bblk, qblk, qblk, qblk],
        out_specs=[qblk, kvacc, kvacc, bblk, pl.BlockSpec((None, 8, 128), lambda p, i: (p, 0, 0))],
        out_shape=[_sds((t, 512), F32), _sds((2, t, 128), F32), _sds((2, t, 128), F32),
                   _sds((8, BLK, 3 * BLK), F32), _sds((NPAIR, 8, 128), F32)],
        compiler_params=_params(("arbitrary", "arbitrary")),
    )(sinks, qs, kse, kse, kse, vse, vse, vse, bias, o, lse, do)


def _sum8(slots, name):
    def body(a_r, o_o):
        acc = a_r[0]
        for k in range(1, 8):
            acc = acc + a_r[k]
        o_o[...] = acc

    return pl.pallas_call(
        body, name=name, out_shape=_sds((SMALL_ROWS, 128), F32),
        in_specs=[pl.BlockSpec(memory_space=pltpu.VMEM)], out_specs=pl.BlockSpec(memory_space=pltpu.VMEM),
        compiler_params=_params(),
    )(slots)


def _place():
    x, y, c = lax.axis_index("x"), lax.axis_index("y"), lax.axis_index("c")
    chips = [(1 - x, y), (x, 1 - y), (1 - x, 1 - y)]
    return x, y, c, chips


def _remote(src, dst, send_sems, recv_sems, k, to):
    return pltpu.make_async_remote_copy(src_ref=src, dst_ref=dst, send_sem=send_sems.at[k], recv_sem=recv_sems.at[k],
                                        device_id=to, device_id_type=MESH_ID)


ANY = pl.BlockSpec(memory_space=pl.ANY)


def _mix_cols(w):
    return jnp.concatenate([w[:, 2312:4360], w[:, 0:1536], w[:, 1544:2312], w[:, 1536:1544],
                            jnp.zeros((w.shape[0], DP - D_IN), w.dtype)], axis=1)


def _unmix_cols(w):
    return jnp.concatenate([w[:, QA:QA + 1536], w[:, FA:FA + 8], w[:, QB:QB + 768], w[:, GA:GA + 2048]], axis=1)


def _rows128(a, rows):
    flat = a.reshape(-1)
    return jnp.pad(flat, (0, rows * 128 - flat.shape[0])).reshape(rows, 128)


GRAD_FORM = {"ffn1_w_in": "col", "ffn2_w_in": "col", "w_branch_fox": "col", "w_branch_swa": "col",
             "ffn1_w_out": "3d", "ffn2_w_out": "3d", "w_out": "3d", "w_in": "3d"}
SUM_TILE = {1024: 128, 704: 176, 512: 128, 256: 128}
NT = len(SHARD_ITEMS)
ALL_ITEMS = tuple(range(NT))


def _half_rows(c, r):
    return pl.ds(pl.multiple_of(c * (r // 2), 16), r // 2)


def _ici_copies(kind, srcs, dsts, send_sems, recv_sems, layer, recv=True, items=ALL_ITEMS):
    x, y, c, chips = _place()
    s = 2 * x + y
    sends, recvs = [], []
    for t, (item, src, dst) in enumerate(zip(items, srcs, dsts)):
        nm, (r, cc), _ = SHARD_ITEMS[item]
        for j, (cx, cy) in enumerate(chips):
            sj = 2 * cx + cy
            k = 3 * t + j
            to = (cx, cy, c)
            if kind == "gather":
                hs = _half_rows(c, r)
                sends.append(_remote(src.at[layer, hs], dst.at[s, hs], send_sems, recv_sems, k, to))
                if recv:
                    recvs.append(_remote(src.at[layer, hs], dst.at[sj, hs], send_sems, recv_sems, k, to))
            else:
                if GRAD_FORM[nm] == "col":
                    piece = src.at[:, pl.ds(pl.multiple_of(sj * cc, 128), cc)]
                else:
                    piece = src.at[sj]
                sends.append(_remote(piece, dst.at[j], send_sems, recv_sems, k, to))
                recvs.append(sends[-1])
    return sends, recvs


def _slab_shapes(items=ALL_ITEMS):
    return [_sds((4, *SHARD_ITEMS[t][1]), CDT) for t in items]


def _dma_sems(n):
    return [pltpu.SemaphoreType.DMA((n,)), pltpu.SemaphoreType.DMA((n,))]


def _forward_sends(dsts, send_sems, recv_sems, items=ALL_ITEMS):
    x, y, c, chips = _place()
    sends, recvs = [], []
    for t, (item, dst) in enumerate(zip(items, dsts)):
        r = SHARD_ITEMS[item][1][0]
        for j, (cx, cy) in enumerate(chips):
            sj = 2 * cx + cy
            hs, ho = _half_rows(c, r), _half_rows(1 - c, r)
            sends.append(_remote(dst.at[sj, hs], dst.at[sj, hs], send_sems, recv_sems, 3 * t + j, (x, y, 1 - c)))
            recvs.append(_remote(dst.at[sj, ho], dst.at[sj, ho], send_sems, recv_sems, 3 * t + j, (x, y, 1 - c)))
    return sends, recvs


def _gather_layer(wb, mflat, layer, name, items):
    nt = len(items)

    def body(*refs):
        srcs, m_r, dsts, mall_o = refs[:nt], refs[nt], refs[nt + 1:2 * nt + 1], refs[2 * nt + 1]
        send_sems, recv_sems, fsend, frecv, msend, mrecv = refs[2 * nt + 2:]
        x, y, c, chips = _place()
        s = 2 * x + y
        sends, recvs = _ici_copies("gather", srcs, dsts, send_sems, recv_sems, layer, items=items)
        metas = [_remote(m_r, mall_o.at[s], msend, mrecv, j, (cx, cy, c)) for j, (cx, cy) in enumerate(chips)]
        for cp in sends + metas:
            cp.start()
        fwds, frecvs = _forward_sends(dsts, fsend, frecv, items)
        for got, fwd in zip(recvs, fwds):
            got.wait_recv()
            fwd.start()
        for got in frecvs:
            got.wait_recv()
        for j, (cx, cy) in enumerate(chips):
            _remote(m_r, mall_o.at[2 * cx + cy], msend, mrecv, j, (cx, cy, c)).wait_recv()
        for cp in sends + metas + fwds:
            cp.wait_send()

    return pl.pallas_call(
        body, name=name, out_shape=_slab_shapes(items) + [_sds((4, META_ROWS, 128), F32)],
        in_specs=[ANY] * (nt + 1), out_specs=[ANY] * (nt + 1),
        scratch_shapes=_dma_sems(3 * nt) + _dma_sems(3 * nt) + _dma_sems(3),
    )(*wb, mflat)


def _forward_layer(slabs, name, items=ALL_ITEMS):
    nt = len(items)

    def body(*refs):
        ins, outs, send_sems, recv_sems = refs[:nt], refs[nt:2 * nt], refs[2 * nt], refs[2 * nt + 1]
        sends, recvs = _forward_sends(outs, send_sems, recv_sems, items)
        for cp in sends:
            cp.start()
        for cp in recvs:
            cp.wait_recv()
        for cp in sends:
            cp.wait_send()

    return pl.pallas_call(
        body, name=name, out_shape=_slab_shapes(items), in_specs=[ANY] * nt, out_specs=[ANY] * nt,
        input_output_aliases={t: t for t in range(nt)}, scratch_shapes=_dma_sems(3 * nt),
    )(*slabs)


def _half_shape(nm, r, c):
    return (r // 2, 4 * c) if GRAD_FORM[nm] == "col" else (4, r // 2, c)


def _swap_layer(gs, gsm, name, items=ALL_ITEMS):
    small = gsm is not None
    nt = len(items)

    def body(*refs):
        g_rs = refs[:nt]
        pos = nt
        if small:
            s_r = refs[pos]
            pos += 1
        got_os = refs[pos:pos + nt]
        pos += nt
        if small:
            slots_o = refs[pos]
            pos += 1
        send_sems, recv_sems = refs[pos], refs[pos + 1]
        x, y, c, _ = _place()
        sib = (x, y, 1 - c)
        sent = []
        for t, (item, g_r, got_o) in enumerate(zip(items, g_rs, got_os)):
            nm, (r, cc), _ = SHARD_ITEMS[item]
            ho = _half_rows(1 - c, r)
            src = g_r.at[ho, :] if GRAD_FORM[nm] == "col" else g_r.at[:, ho, :]
            sent.append(_remote(src, got_o, send_sems, recv_sems, t, sib))
        if small:
            ssend, srecv, loc_sem = refs[pos + 2], refs[pos + 3], refs[pos + 4]
            me = 4 * x + 2 * y + c
            loc = pltpu.make_async_copy(s_r, slots_o.at[me], loc_sem.at[0])
            loc.start()
            peers = [(x ^ (k >> 2), y ^ ((k >> 1) & 1), c ^ (k & 1)) for k in range(1, 8)]
            for k, peer in enumerate(peers):
                sent.append(_remote(s_r, slots_o.at[me], ssend, srecv, k, peer))
        for cp in sent:
            cp.start()
        for cp in sent[:nt]:
            cp.wait_recv()
        if small:
            for k, (px, py, pc) in enumerate(peers):
                _remote(s_r, slots_o.at[4 * px + 2 * py + pc], ssend, srecv, k, (px, py, pc)).wait_recv()
        for cp in sent:
            cp.wait_send()
        if small:
            loc.wait()

    outs = [_sds(_half_shape(*SHARD_ITEMS[item][0:1], *SHARD_ITEMS[item][1]), CDT) for item in items]
    ops = list(gs)
    sems = _dma_sems(nt)
    if small:
        outs.append(_sds((8, SMALL_ROWS, 128), F32))
        ops.append(gsm)
        sems = sems + _dma_sems(7) + [pltpu.SemaphoreType.DMA((1,))]
    res = pl.pallas_call(
        body, name=name, out_shape=outs, in_specs=[ANY] * len(ops), out_specs=[ANY] * len(outs), scratch_shapes=sems,
    )(*ops)
    return (res[:nt], res[nt]) if small else (res, None)


def _pair_add_t(own, got, half_idx, nm, r, name):
    tr = SUM_TILE[r]
    nb = (r // 2) // tr
    if GRAD_FORM[nm] == "col":
        blk = (tr, own.shape[1])
        own_spec = pl.BlockSpec(blk, lambda i, c_r: (c_r[0] * nb + i, 0))
        half_spec = pl.BlockSpec(blk, lambda i, c_r: (i, 0))
    else:
        blk = (4, tr, own.shape[2])
        own_spec = pl.BlockSpec(blk, lambda i, c_r: (0, c_r[0] * nb + i, 0))
        half_spec = pl.BlockSpec(blk, lambda i, c_r: (0, i, 0))

    def body(c_r, a_r, b_r, o_o):
        o_o[...] = (a_r[...].astype(F32) + b_r[...].astype(F32)).astype(CDT)

    grid_spec = pltpu.PrefetchScalarGridSpec(num_scalar_prefetch=1, grid=(nb,), in_specs=[own_spec, half_spec],
                                             out_specs=half_spec)
    return pl.pallas_call(body, name=name, grid_spec=grid_spec, out_shape=_sds(got.shape, CDT),
                          compiler_params=_params(("parallel",)))(half_idx, own, got)


def _sum4_t(ps, got3, buf, idx, layer, nm, r, name):
    tr = SUM_TILE[r]
    nb = (r // 2) // tr
    c = got3.shape[2]
    if GRAD_FORM[nm] == "col":
        ps_spec = pl.BlockSpec((tr, c), lambda i, x_r: (i, x_r[0]))
    else:
        ps_spec = pl.BlockSpec((None, tr, c), lambda i, x_r: (x_r[0], i, 0))

    def body(x_r, a_r, b_r, buf_r, o_o):
        o_o[...] = ((a_r[...].astype(F32) + b_r[0].astype(F32)) + b_r[1].astype(F32)) + b_r[2].astype(F32)

    grid_spec = pltpu.PrefetchScalarGridSpec(
        num_scalar_prefetch=1, grid=(nb,),
        in_specs=[ps_spec, pl.BlockSpec((3, tr, c), lambda i, x_r: (0, i, 0)), ANY],
        out_specs=pl.BlockSpec((None, tr, c), lambda i, x_r: (layer, x_r[1] * nb + i, 0)),
    )
    return pl.pallas_call(body, name=name, grid_spec=grid_spec, out_shape=_sds(buf.shape, F32),
                          input_output_aliases={3: 0}, compiler_params=_params(("parallel",)))(idx, ps, got3, buf)


def _scatter_layer(ps, name, items=ALL_ITEMS):
    nt = len(items)

    def body(*refs):
        srcs, dsts, send_sems, recv_sems = refs[:nt], refs[nt:2 * nt], refs[2 * nt], refs[2 * nt + 1]
        sends, recvs = _ici_copies("scatter", srcs, dsts, send_sems, recv_sems, None, items=items)
        for cp in sends:
            cp.start()
        for cp in recvs:
            cp.wait_recv()
        for cp in sends:
            cp.wait_send()

    return pl.pallas_call(
        body, name=name, out_shape=_got3_shapes(items), in_specs=[ANY] * nt, out_specs=[ANY] * nt,
        scratch_shapes=_dma_sems(3 * nt),
    )(*ps)


def _got3_shapes(items=ALL_ITEMS):
    return [_sds((3, SHARD_ITEMS[t][1][0] // 2, SHARD_ITEMS[t][1][1]), CDT) for t in items]


def _join_layer(bufs, name):
    def body(*refs):
        ins, outs, send_sems, recv_sems = refs[:NT], refs[NT:2 * NT], refs[2 * NT], refs[2 * NT + 1]
        x, y, c, _ = _place()
        sent = []
        for t, ((nm, (r, cc), _), b_o) in enumerate(zip(SHARD_ITEMS, outs)):
            hs = _half_rows(c, r)
            sent.append(_remote(b_o.at[:, hs, :], b_o.at[:, hs, :], send_sems, recv_sems, t, (x, y, 1 - c)))
        for cp in sent:
            cp.start()
        for t, ((nm, (r, cc), _), b_o) in enumerate(zip(SHARD_ITEMS, outs)):
            ho = _half_rows(1 - c, r)
            _remote(b_o.at[:, ho, :], b_o.at[:, ho, :], send_sems, recv_sems, t, (x, y, 1 - c)).wait_recv()
        for cp in sent:
            cp.wait_send()

    return pl.pallas_call(
        body, name=name, out_shape=[_sds(b.shape, F32) for b in bufs], in_specs=[ANY] * NT, out_specs=[ANY] * NT,
        input_output_aliases={t: t for t in range(NT)}, scratch_shapes=_dma_sems(NT),
    )(*bufs)


def _adamw3(w, g, m, v, name):
    nl, r, c = w.shape
    tr = SUM_TILE.get(r, r)
    if r % 8:
        blk = pl.BlockSpec((None, r, 256), lambda l, i: (l, 0, i))
        steps = c // 256
    else:
        blk = pl.BlockSpec((None, tr, c), lambda l, i: (l, i, 0))
        steps = r // tr

    def body(w_r, g_r, m_r, v_r, d_o, m_o, v_o):
        g_ = g_r[...]
        m_ = ADAM_B1 * m_r[...] + (1.0 - ADAM_B1) * g_
        v_ = ADAM_B2 * v_r[...] + (1.0 - ADAM_B2) * jnp.square(g_)
        m_hat = m_ / (1.0 - ADAM_B1 ** ADAM_STEP)
        v_hat = v_ / (1.0 - ADAM_B2 ** ADAM_STEP)
        d_o[...] = -ADAM_LR * (m_hat / (jnp.sqrt(v_hat) + ADAM_EPS) + ADAM_WD * w_r[...])
        m_o[...] = m_
        v_o[...] = v_

    return pl.pallas_call(
        body, name=name, grid=(nl, steps),
        in_specs=[blk] * 4, out_specs=[blk] * 3, out_shape=[_sds((nl, r, c), F32)] * 3,
        compiler_params=_params(("parallel", "parallel")),
    )(w, g, m, v)


def _full_weights(slabs, wb, layer, shard, items=ALL_ITEMS):
    ws = {}
    for t, slab in zip(items, slabs):
        nm, (r, c), kind = SHARD_ITEMS[t]
        slab = lax.dynamic_update_slice(slab, wb[nm][layer][None], (shard, 0, 0))
        ws[nm] = slab.reshape(4 * r, c) if kind == "row" else jnp.concatenate([slab[s] for s in range(4)], axis=1)
    return ws


def _exchange_forms(g, items=ALL_ITEMS):
    out = []
    for t in items:
        nm, (r, c), _ = SHARD_ITEMS[t]
        a = g[nm]
        if nm == "w_in":
            a = a.reshape(D, 4, c).transpose(1, 0, 2)
        elif GRAD_FORM[nm] == "3d":
            a = a.reshape(4, r, c)
        out.append(a)
    return out


SMALL_ITEMS = (("rel_bias_table", 2), ("ffn1_norm", 16), ("mix_norm", 16), ("ffn2_norm", 16), ("forget_bias", 1),
               ("fox_q_norm", 1), ("fox_k_norm", 1), ("swa_q_norm", 1), ("swa_k_norm", 1), ("swa_sinks", 1))
SMALL_ADAM_ROWS = 96


def _layer_fwd(h, lw, l, ride=None, late=None):
    rides = late["rides"] if late else {}

    def run(key, fn, *args):
        r = rides.get(key)
        if r is None:
            return fn(*args)
        out = fn(*args, ride=r)
        late["arrived"](key, out[-1])
        return out[0] if len(out) == 2 else out[:-1]

    sv = {"h0": h}
    a, sv["a1t"] = _rms_fwd(h, lw["ffn1_norm"], f"rms_fwd_a{l}")
    sv["gu1"], s, sv["s1t"] = run("ffn_in_a", _ffn_in, a, lw["ffn1_w_in"], f"ffn_in_a{l}")
    h = run("ffn_out_a", _mm_res, s, lw["ffn1_w_out"], h, 0.5, f"ffn_out_a{l}")
    sv["h1"] = h
    a, sv["amt"] = _rms_fwd(h, lw["mix_norm"], f"rms_fwd_m{l}")
    if late:
        late["need"](lw, "mixer")
    proj = run("proj", _mm, a, lw["w_mix"], F32, _row_tile(h.shape[0]), DP, f"proj{l}")
    sv["proj"] = proj
    qf, kf, vf, qs, kse, vse, c, ct, sv["qft"] = _qknorm_fwd(proj, lw["gfq"], lw["gfk"], lw["gsq"], lw["gsk"], lw["fb"],
                                                              f"qknorm_fwd{l}")
    ofox, lse_f, *rode = _fox_fwd(qf, kf, vf, c, ct, f"fox_fwd{l}", ride)
    oswa, lse_s = run("swa_fwd", _swa_fwd, qs, kse, vse, lw["bias"], lw["sinks"], f"swa_fwd{l}")
    if late:
        late["need"](lw, "gate")
    sv.update(qf=qf, kf=kf, vf=vf, qs=qs, kse=kse, vse=vse, c=c, ct=ct, ofox=ofox, oswa=oswa, lse_f=lse_f, lse_s=lse_s)
    h, sv["yt"], sv["pf"], sv["ps"], sv["oft"], sv["ost"] = _gate_out_fwd(
        ofox, oswa, lw["w_branch_fox"], lw["w_branch_swa"], proj, lw["w_out"], h, f"gate_out_fwd{l}")
    sv["h2"] = h
    a, sv["a2t"] = _rms_fwd(h, lw["ffn2_norm"], f"rms_fwd_b{l}")
    sv["gu2"], s, sv["s2t"] = _ffn_in(a, lw["ffn2_w_in"], f"ffn_in_b{l}")
    h = _mm_res(s, lw["ffn2_w_out"], h, 0.5, f"ffn_out_b{l}")
    return h, sv, rode


def _ffn_bwd(dh, dhb, h_in, at, gu, st, norm, w_in, w_out, tag, rides=None):
    r = rides or (None,) * 4
    rode = []

    def split(res, ride):
        if ride is None:
            return res
        rode.extend(res[-1])
        return res[0] if len(res) == 2 else res[:-1]

    dgu = split(_ffn_bwd_mid(dhb, w_out, gu, f"ffn_bwd_mid_{tag}", r[0]), r[0])
    d_w_out = split(_mm(st, dhb, CDT, 256, D, f"dw_ffn_out_{tag}", scale=0.5, ride=r[1]), r[1])
    dh, dhb, dg = split(_ffn_bwd_in(dgu, w_in, h_in, norm, dh, f"ffn_bwd_in_{tag}", r[2]), r[2])
    d_w_in = split(_mm(at, dgu, CDT, D, 256, f"dw_ffn_in_{tag}", ride=r[3]), r[3])
    return dh, dhb, d_w_out, d_w_in, dg, rode


def _layer_bwd(dh, dhb, sv, lw, l, ride=None, before_ffn1=None):
    g = {}
    dh, dhb, g["ffn2_w_out"], g["ffn2_w_in"], g["ffn2_norm"], _ = _ffn_bwd(
        dh, dhb, sv["h2"], sv["a2t"], sv["gu2"], sv["s2t"], lw["ffn2_norm"], lw["ffn2_w_in"], lw["ffn2_w_out"], f"b{l}")
    g["w_out"] = _mm(sv["yt"], dhb, CDT, 512, 512, f"dw_out{l}")
    dpf, dps, dga, dgb = _gate_out_bwd(dhb, lw["w_out"], sv["pf"], sv["ps"], sv["proj"], f"gate_out_bwd{l}")
    do_f, do_ft = _mm_nt(dpf, lw["w_branch_fox"], f"d_ofox{l}", with_t=True)
    do_s = _mm_nt(dps, lw["w_branch_swa"], f"d_oswa{l}")
    g["w_branch_fox"] = _mm(sv["oft"], dpf, CDT, 512, 512, f"dw_bfox{l}")
    g["w_branch_swa"] = _mm(sv["ost"], dps, CDT, 512, 512, f"dw_bswa{l}")
    dqf, dcq, dkf, dvf, dck, *rode = _fox_bwd(sv["qf"], sv["qft"], sv["kf"], sv["vf"], sv["c"], sv["ct"], sv["ofox"],
                                              sv["lse_f"], do_f, do_ft, f"fox_bwd{l}", ride)
    g["rode"] = rode
    dqs, dkse, dvse, dbias, dsk = _swa_bwd(sv["qs"], sv["kse"], sv["vse"], lw["bias"], lw["sinks"], sv["oswa"],
                                           sv["lse_s"], do_s, f"swa_bwd{l}")
    dproj, dgn = _qknorm_bwd(sv["proj"], dqf, dkf, dvf, dqs, dkse, dvse, dcq, dck, dga, dgb,
                             lw["gfq"], lw["gfk"], lw["gsq"], lw["gsk"], lw["fb"], f"qknorm_bwd{l}")
    g["w_mix"] = _mm(sv["amt"], dproj, CDT, 512, 640, f"dw_mix{l}")
    dh, dhb, g["mix_norm"] = _mm_nt_rms(dproj, lw["w_mix"], sv["h1"], lw["mix_norm"], dh, f"d_am{l}")
    g["dbias"], g["dsk"], g["dgn"] = dbias, dsk, dgn
    rides = before_ffn1(g) if before_ffn1 else None
    dh, dhb, g["ffn1_w_out"], g["ffn1_w_in"], g["ffn1_norm"], g["rode_ffn1"] = _ffn_bwd(
        dh, dhb, sv["h0"], sv["a1t"], sv["gu1"], sv["s1t"], lw["ffn1_norm"], lw["ffn1_w_in"], lw["ffn1_w_out"], f"a{l}",
        rides)
    return dh, dhb, g


def kernel(x, meta_tokens, rel_bias_table, ffn1_norm, ffn1_w_in, ffn1_w_out, mix_norm, w_in, forget_bias, fox_q_norm, fox_k_norm, swa_q_norm, swa_k_norm, swa_sinks, w_branch_fox, w_branch_swa, w_out, ffn2_norm, ffn2_w_in, ffn2_w_out, loss_target, m_meta_tokens, m_rel_bias_table, m_ffn1_norm, m_ffn1_w_in, m_ffn1_w_out, m_mix_norm, m_w_in, m_forget_bias, m_fox_q_norm, m_fox_k_norm, m_swa_q_norm, m_swa_k_norm, m_swa_sinks, m_w_branch_fox, m_w_branch_swa, m_w_out, m_ffn2_norm, m_ffn2_w_in, m_ffn2_w_out, v_meta_tokens, v_rel_bias_table, v_ffn1_norm, v_ffn1_w_in, v_ffn1_w_out, v_mix_norm, v_w_in, v_forget_bias, v_fox_q_norm, v_fox_k_norm, v_swa_q_norm, v_swa_k_norm, v_swa_sinks, v_w_branch_fox, v_w_branch_swa, v_w_out, v_ffn2_norm, v_ffn2_w_in, v_ffn2_w_out):
    names = ["meta_tokens", "rel_bias_table", "ffn1_norm", "ffn1_w_in", "ffn1_w_out", "mix_norm", "w_in", "forget_bias",
             "fox_q_norm", "fox_k_norm", "swa_q_norm", "swa_k_norm", "swa_sinks", "w_branch_fox", "w_branch_swa", "w_out",
             "ffn2_norm", "ffn2_w_in", "ffn2_w_out"]
    w = dict(zip(names, [meta_tokens, rel_bias_table, ffn1_norm, ffn1_w_in, ffn1_w_out, mix_norm, w_in, forget_bias,
                         fox_q_norm, fox_k_norm, swa_q_norm, swa_k_norm, swa_sinks, w_branch_fox, w_branch_swa, w_out,
                         ffn2_norm, ffn2_w_in, ffn2_w_out]))
    m = dict(zip(names, [m_meta_tokens, m_rel_bias_table, m_ffn1_norm, m_ffn1_w_in, m_ffn1_w_out, m_mix_norm, m_w_in,
                         m_forget_bias, m_fox_q_norm, m_fox_k_norm, m_swa_q_norm, m_swa_k_norm, m_swa_sinks,
                         m_w_branch_fox, m_w_branch_swa, m_w_out, m_ffn2_norm, m_ffn2_w_in, m_ffn2_w_out]))
    v = dict(zip(names, [v_meta_tokens, v_rel_bias_table, v_ffn1_norm, v_ffn1_w_in, v_ffn1_w_out, v_mix_norm, v_w_in,
                         v_forget_bias, v_fox_q_norm, v_fox_k_norm, v_swa_q_norm, v_swa_k_norm, v_swa_sinks,
                         v_w_branch_fox, v_w_branch_swa, v_w_out, v_ffn2_norm, v_ffn2_w_in, v_ffn2_w_out]))
    xi, yi, ci = lax.axis_index("x"), lax.axis_index("y"), lax.axis_index("c")
    shard = 2 * xi + yi
    seq = x.shape[1]
    t = seq + BLK

    wb = {nm: w[nm].astype(CDT) for nm, _, _ in SHARD_ITEMS}
    wb_list = [wb[nm] for nm, _, _ in SHARD_ITEMS]
    mflat = meta_tokens.reshape(META_ROWS, 128)
    first = (0, 1)
    *slabs_first, mall = _gather_layer([wb_list[t] for t in first], mflat, 0, "gather_weights", first)
    mall = lax.dynamic_update_slice(mall, mflat[None], (shard, 0, 0))
    meta_full = jnp.concatenate([mall[s].reshape(N_META, 256) for s in range(4)], axis=1)
    bias = _bias_fwd(rel_bias_table, "bias_fwd")

    def layer_weights(slabs, l, items=ALL_ITEMS):
        lw = _full_weights(slabs, wb, l, shard, items)
        if "w_in" in lw:
            lw["w_mix"] = _mix_cols(lw.pop("w_in"))
        return lw

    def layer_vectors(l):
        lw = {nm: w[nm][l].reshape(1, D) for nm in ("ffn1_norm", "mix_norm", "ffn2_norm")}
        lw["gfq"] = jnp.tile(fox_q_norm[l], 8).reshape(1, 512)
        lw["gfk"] = jnp.tile(fox_k_norm[l], 8).reshape(1, 512)
        lw["gsq"] = jnp.tile(swa_q_norm[l], 8).reshape(1, 512)
        lw["gsk"] = jnp.tile(swa_k_norm[l], 2).reshape(1, 128)
        lw["fb"] = jnp.pad(forget_bias[l], (0, 120)).reshape(1, 128)
        lw["sinks"] = swa_sinks[l]
        lw["bias"] = bias
        return lw

    def gather_ride(layer, items):
        return ("gather", [wb_list[t] for t in items], _slab_shapes(items), layer, items)

    landed = {}

    def need(lw, stage):
        if stage == "mixer":
            items = (2,)
            slabs = _forward_layer(landed["ffn_in_a"], "forward_halves0m", items)
        else:
            items = (3, 4, 5, 6, 7)
            slabs = _forward_layer(landed["ffn_out_a"] + landed["swa_fwd"] + landed["proj"], "forward_halves0g", items)
        lw.update(layer_weights(slabs, 0, items))

    late = {"rides": {"ffn_in_a": gather_ride(0, (2,)), "ffn_out_a": gather_ride(0, (3, 4, 5)),
                      "proj": gather_ride(0, (7,)), "swa_fwd": gather_ride(0, (6,))},
            "arrived": landed.__setitem__, "need": need}

    h = jnp.concatenate([jnp.zeros((PAD, D), F32), meta_full, x[0]], axis=0)
    lws = [{**layer_vectors(0), **layer_weights(slabs_first, 0, first)}]
    h, sv0, slabs1 = _layer_fwd(h, lws[0], 0, gather_ride(1, ALL_ITEMS), late)
    lws.append({**layer_vectors(1), **layer_weights(_forward_layer(slabs1, "forward_halves"), 1)})
    h, sv1, _ = _layer_fwd(h, lws[1], 1)
    saved = [sv0, sv1]
    dh, dhb, lacc = _loss(h, loss_target[0], "loss")
    loss = lax.psum(lacc[0, 0], ("x", "y", "c"))

    half_idx = ci.reshape(1).astype(jnp.int32)
    place_idx = jnp.stack([shard, ci]).astype(jnp.int32)

    def pair_sums(g, gsm, tag, items=ALL_ITEMS):
        if "w_mix" in g:
            g["w_in"] = _unmix_cols(g.pop("w_mix"))
        forms = _exchange_forms(g, items)
        got, slots = _swap_layer(forms, gsm, f"swap_halves{tag}", items)
        return {t: _pair_add_t(a, b, half_idx, SHARD_ITEMS[t][0], SHARD_ITEMS[t][1][0],
                               f"pair_add{tag}_{SHARD_ITEMS[t][0]}")
                for t, a, b in zip(items, forms, got)}, slots

    def scatter_ride(ps, items):
        return ("scatter", [ps[t] for t in items], _got3_shapes(items), None, items)

    early = (2, 3, 4, 5, 6, 7)
    early_rides = ((6,), (7,), (2, 5), (3, 4))
    ps0 = {}

    def before_ffn1(g):
        ps0.update(pair_sums(g, None, "0e", early)[0])
        return [scatter_ride(ps0, items) for items in early_rides]

    grads = [None, None]
    dh, dhb, grads[1] = _layer_bwd(dh, dhb, saved[1], lws[1], 1)
    ps1, _ = pair_sums(grads[1], None, 1)
    dh, dhb, grads[0] = _layer_bwd(dh, dhb, saved[0], lws[0], 0, scatter_ride(ps1, ALL_ITEMS), before_ffn1)
    grad_x = dh[BLK:].reshape(1, seq, D)
    dtab = _bias_bwd(grads[0]["dbias"] + grads[1]["dbias"], "bias_bwd")

    small = [dh[PAD:BLK].reshape(128, 128), _rows128(dtab[:, :N_BUCKETS].T, 2)]
    for nm in ("ffn1_norm", "mix_norm", "ffn2_norm"):
        small.append(jnp.stack([grads[l][nm][0] for l in range(2)]).reshape(16, 128))
    small.append(_rows128(jnp.stack([grads[l]["dgn"][4, :8] for l in range(2)]), 1))
    for row in range(4):
        small.append(jnp.stack([grads[l]["dgn"][row, :HD] for l in range(2)]).reshape(1, 128))
    dsk = [grads[l]["dsk"][:, 0, :] for l in range(2)]
    small.append(_rows128(jnp.stack([jnp.stack([d[:, 0], d[:, HD]], axis=1).reshape(8) for d in dsk]), 1))
    gsm = jnp.concatenate(small, axis=0)
    gsm = jnp.pad(gsm, ((0, SMALL_ROWS - gsm.shape[0]), (0, 0)))

    late = (0, 1)
    ps_late, slots = pair_sums(grads[0], gsm, "0l", late)
    ps0.update(ps_late)
    got3_0 = dict(zip([t for items in early_rides for t in items], grads[0]["rode_ffn1"]))
    got3_0.update(zip(late, _scatter_layer([ps0[t] for t in late], "scatter_shards", late)))
    got3 = [got3_0, dict(zip(ALL_ITEMS, grads[0]["rode"]))]
    bufs = []
    for t, (nm, (r, c), _) in enumerate(SHARD_ITEMS):
        buf = lax.empty((2, r, c), F32)
        for l, ps in ((1, ps1), (0, ps0)):
            buf = _sum4_t(ps[t], got3[l][t], buf, place_idx, l, nm, r, f"sum4_{l}_{nm}")
        bufs.append(buf)
    bufs = _join_layer(bufs, "join_halves")
    gs = _sum8(slots, "sum8")

    g_out = {nm: buf for (nm, _, _), buf in zip(SHARD_ITEMS, bufs)}
    g_out["meta_tokens"] = lax.dynamic_slice(gs[0:128].reshape(N_META, D), (0, shard * 256), (N_META, 256))
    off = 128
    for nm, rows in SMALL_ITEMS:
        n = w[nm].size
        g_out[nm] = gs[off:off + rows].reshape(-1)[:n].reshape(w[nm].shape)
        off += rows

    delta, new_m, new_v = {}, {}, {}
    for nm, _, _ in SHARD_ITEMS:
        if nm == "w_in":
            tr_ = lambda a: jnp.swapaxes(a, 1, 2)
            delta[nm], new_m[nm], new_v[nm] = (tr_(a) for a in _adamw3(tr_(w[nm]), tr_(g_out[nm]), tr_(m[nm]), tr_(v[nm]),
                                                                        f"adamw_{nm}"))
        else:
            delta[nm], new_m[nm], new_v[nm] = _adamw3(w[nm], g_out[nm], m[nm], v[nm], f"adamw_{nm}")
    small_names = ["meta_tokens"] + [nm for nm, _ in SMALL_ITEMS]
    small_rows = [META_ROWS] + [rows for _, rows in SMALL_ITEMS]

    def pack_small(src):
        buf = jnp.concatenate([_rows128(src[nm], rows) for nm, rows in zip(small_names, small_rows)], axis=0)
        return jnp.pad(buf, ((0, SMALL_ADAM_ROWS - buf.shape[0]), (0, 0)))

    d_, m_, v_ = (a[0] for a in _adamw3(pack_small(w)[None], pack_small(g_out)[None], pack_small(m)[None],
                                        pack_small(v)[None], "adamw_small"))
    off = 0
    for nm, rows in zip(small_names, small_rows):
        n = w[nm].size
        for dst, src in ((delta, d_), (new_m, m_), (new_v, v_)):
            dst[nm] = src[off:off + rows].reshape(-1)[:n].reshape(w[nm].shape)
        off += rows

    return (loss, grad_x, *[g_out[n] for n in names], *[delta[n] for n in names],
            *[new_m[n] for n in names], *[new_v[n] for n in names])
```

```python
import math

import numpy as np
import jax
import jax.numpy as jnp
from jax import lax
from jax.experimental import pallas as pl
from jax.experimental.pallas import tpu as pltpu

D = 1024
F = 2816
FT = F // 2
HD = 64
NPAIR = 4
N_META = 16
BLK = 128
PAD = BLK - N_META
EPS = 1e-6
NEG = -1e30
N_BUCKETS = 32
GA, GB, QA, KA, VA, QB, KB, VB, FA, DP = 0, 1024, 2048, 2560, 3072, 3584, 4096, 4224, 4352, 4480
D_IN = 4360
CDT = jnp.bfloat16
F32 = jnp.float32
VMEM_LIMIT = 48 * 1024 * 1024
MESH_ID = pl.DeviceIdType.MESH

ADAM_LR, ADAM_B1, ADAM_B2, ADAM_EPS, ADAM_WD, ADAM_STEP = 0.001, 0.9, 0.999, 1e-08, 0.01, 10

SHARD_ITEMS = (
    ("ffn1_w_in", (1024, 1408), "col"),
    ("ffn1_w_out", (704, 1024), "row"),
    ("w_in", (1024, 1090), "col"),
    ("w_branch_fox", (512, 256), "col"),
    ("w_branch_swa", (512, 256), "col"),
    ("w_out", (256, 1024), "row"),
    ("ffn2_w_in", (1024, 1408), "col"),
    ("ffn2_w_out", (704, 1024), "row"),
)
SMALL_ROWS = 192
META_ROWS = 32


def _row_tile(t):
    return 384 if t % 384 == 0 else 128


def _dot(a, b):
    return jnp.dot(a, b, preferred_element_type=F32)


def _dot_nt(a, b):
    return lax.dot_general(a, b, (((1,), (1,)), ((), ())), preferred_element_type=F32)


def _dot_hi(a, b):
    return jnp.dot(a, b, preferred_element_type=F32, precision=lax.Precision.HIGHEST)


def _sigmoid(x):
    return 0.5 * jnp.tanh(0.5 * x) + 0.5


def _iota(shape, dim):
    return lax.broadcasted_iota(jnp.int32, shape, dim)


def _params(sem=None):
    return pltpu.CompilerParams(dimension_semantics=sem, vmem_limit_bytes=VMEM_LIMIT)


def _sds(shape, dtype):
    return jax.ShapeDtypeStruct(shape, dtype)


def _rms_fwd(h, g, name):
    t = h.shape[0]
    tm = _row_tile(t)

    def body(h_ref, g_ref, a_ref, at_ref):
        x = h_ref[...]
        ms = jnp.mean(x * x, axis=-1, keepdims=True)
        a = x * lax.rsqrt(ms + EPS) * g_ref[...]
        a_ref[...] = a.astype(CDT)
        at_ref[...] = a.T.astype(CDT)

    return pl.pallas_call(
        body, name=name, grid=(t // tm,),
        in_specs=[pl.BlockSpec((tm, D), lambda i: (i, 0)), pl.BlockSpec((1, D), lambda i: (0, 0))],
        out_specs=[pl.BlockSpec((tm, D), lambda i: (i, 0)), pl.BlockSpec((D, tm), lambda i: (0, i))],
        out_shape=[_sds((t, D), CDT), _sds((D, t), CDT)],
        compiler_params=_params(("parallel",)),
    )(h, g)


def _ffn_in(a, w_in, name, ride=None):
    t = a.shape[0]
    tm = _row_tile(t)
    tn = FT
    nj = F // tn
    grid = (nj, t // tm)
    ride_in, ride_in_specs, ride_out, ride_out_specs, ride_sems = _ride_specs(ride)

    def body(a_ref, wg_ref, wu_ref, gu_ref, s_ref, st_ref):
        a_ = a_ref[...]
        g = _dot(a_, wg_ref[...])
        u = _dot(a_, wu_ref[...])
        s = g * _sigmoid(g) * u
        gu_ref[0] = g.astype(CDT)
        gu_ref[1] = u.astype(CDT)
        s_ref[...] = s.astype(CDT)
        st_ref[...] = s.T.astype(CDT)

    res = pl.pallas_call(
        _riding(body, 3, 3, ride, grid), name=name, grid=grid,
        in_specs=[pl.BlockSpec((tm, D), lambda j, i: (i, 0)),
                  pl.BlockSpec((D, tn), lambda j, i: (0, j)),
                  pl.BlockSpec((D, tn), lambda j, i: (0, j + nj))] + ride_in_specs,
        out_specs=[pl.BlockSpec((2, tm, tn), lambda j, i: (0, i, j)),
                   pl.BlockSpec((tm, tn), lambda j, i: (i, j)),
                   pl.BlockSpec((tn, tm), lambda j, i: (j, i))] + ride_out_specs,
        out_shape=[_sds((2, t, F), CDT), _sds((t, F), CDT), _sds((F, t), CDT)] + ride_out, scratch_shapes=ride_sems,
        compiler_params=_params(("arbitrary", "arbitrary") if ride else ("parallel", "parallel")),
    )(a, w_in, w_in, *ride_in)
    return (*res[:3], res[3:]) if ride else res


def _mm_res(a, b, res, scale, name, ride=None):
    t, k = a.shape
    n = b.shape[1]
    tm = _row_tile(t)
    tn = n
    grid = (t // tm, n // tn)
    ride_in, ride_in_specs, ride_out, ride_out_specs, ride_sems = _ride_specs(ride)

    def body(a_ref, b_ref, r_ref, o_ref):
        o_ref[...] = r_ref[...] + scale * _dot(a_ref[...], b_ref[...])

    out = pl.pallas_call(
        _riding(body, 3, 1, ride, grid), name=name, grid=grid,
        in_specs=[pl.BlockSpec((tm, k), lambda i, j: (i, 0)),
                  pl.BlockSpec((k, tn), lambda i, j: (0, j)),
                  pl.BlockSpec((tm, tn), lambda i, j: (i, j))] + ride_in_specs,
        out_specs=[pl.BlockSpec((tm, tn), lambda i, j: (i, j))] + ride_out_specs,
        out_shape=[_sds((t, n), F32)] + ride_out, scratch_shapes=ride_sems,
        compiler_params=_params(("arbitrary", "arbitrary") if ride else ("parallel", "parallel")),
    )(a, b, res, *ride_in)
    return (out[0], out[1:]) if ride else out[0]


def _mm(a, b, out_dtype, tm, tn, name, scale=1.0, ride=None):
    m, k = a.shape
    if b.ndim == 3:
        nh = b.shape[2] // tn
        n = 2 * b.shape[2]
        b_spec = pl.BlockSpec((None, k, tn), lambda i, j: (j // nh, 0, j % nh))
    else:
        n = b.shape[1]
        b_spec = pl.BlockSpec((k, tn), lambda i, j: (0, j))
    grid = (m // tm, n // tn)
    ride_in, ride_in_specs, ride_out, ride_out_specs, ride_sems = _ride_specs(ride)

    def body(a_ref, b_ref, o_ref):
        o_ref[...] = (scale * _dot(a_ref[...], b_ref[...])).astype(out_dtype)

    res = pl.pallas_call(
        _riding(body, 2, 1, ride, grid), name=name, grid=grid,
        in_specs=[pl.BlockSpec((tm, k), lambda i, j: (i, 0)), b_spec] + ride_in_specs,
        out_specs=[pl.BlockSpec((tm, tn), lambda i, j: (i, j))] + ride_out_specs,
        out_shape=[_sds((m, n), out_dtype)] + ride_out, scratch_shapes=ride_sems,
        compiler_params=_params(("arbitrary", "arbitrary") if ride else ("parallel", "parallel")),
    )(a, b, *ride_in)
    return (res[0], res[1:]) if ride else res[0]


def _mm_nt(a, b, name, with_t=False):
    m, n = a.shape
    k = b.shape[0]
    tm = _row_tile(m)
    tk = k

    def body(a_ref, b_ref, o_ref, *t_ref):
        r = _dot_nt(a_ref[...], b_ref[...])
        o_ref[...] = r
        if with_t:
            t_ref[0][...] = r.T.astype(CDT)

    out_specs = [pl.BlockSpec((tm, tk), lambda i, j: (i, j))]
    out_shape = [_sds((m, k), F32)]
    if with_t:
        out_specs.append(pl.BlockSpec((tk, tm), lambda i, j: (j, i)))
        out_shape.append(_sds((k, m), CDT))
    res = pl.pallas_call(
        body, name=name, grid=(m // tm, k // tk),
        in_specs=[pl.BlockSpec((tm, n), lambda i, j: (i, 0)), pl.BlockSpec((tk, n), lambda i, j: (j, 0))],
        out_specs=out_specs, out_shape=out_shape,
        compiler_params=_params(("parallel", "parallel")),
    )(a, b)
    return res if with_t else res[0]


def _ffn_bwd_mid(dhb, w_out, gu, name, ride=None):
    t = dhb.shape[0]
    tm = _row_tile(t)
    tn = FT
    grid = (F // tn, t // tm)
    ride_in, ride_in_specs, ride_out, ride_out_specs, ride_sems = _ride_specs(ride)

    def body(dh_ref, w_ref, gu_ref, o_ref):
        ds = _dot_nt(dh_ref[...] * 0.5, w_ref[...])
        g = gu_ref[0].astype(F32)
        u = gu_ref[1].astype(F32)
        sg = _sigmoid(g)
        o_ref[0] = (ds * u * (sg * (1.0 + g * (1.0 - sg)))).astype(CDT)
        o_ref[1] = (ds * (g * sg)).astype(CDT)

    res = pl.pallas_call(
        _riding(body, 3, 1, ride, grid), name=name, grid=grid,
        in_specs=[pl.BlockSpec((tm, D), lambda j, i: (i, 0)),
                  pl.BlockSpec((tn, D), lambda j, i: (j, 0)),
                  pl.BlockSpec((2, tm, tn), lambda j, i: (0, i, j))] + ride_in_specs,
        out_specs=[pl.BlockSpec((2, tm, tn), lambda j, i: (0, i, j))] + ride_out_specs,
        out_shape=[_sds((2, t, F), CDT)] + ride_out, scratch_shapes=ride_sems,
        compiler_params=_params(("arbitrary", "arbitrary") if ride else ("parallel", "parallel")),
    )(dhb, w_out, gu, *ride_in)
    return (res[0], res[1:]) if ride else res[0]


def _rms_bwd_rows(da_, x, g, dres, i, dh_ref, dhb_ref, dg_ref):
    r = lax.rsqrt(jnp.mean(x * x, axis=-1, keepdims=True) + EPS)
    xh = x * r
    day = da_ * g
    dh = dres + r * (day - xh * jnp.mean(day * xh, axis=-1, keepdims=True))
    dh_ref[...] = dh
    dhb_ref[...] = dh.astype(CDT)

    @pl.when(i == 0)
    def _():
        dg_ref[...] = jnp.zeros(dg_ref.shape, F32)

    dg_ref[0:1, :] += jnp.sum(da_ * xh, axis=0, keepdims=True)


def _ffn_bwd_in(dgu, w_in, h, g, dres, name, ride=None):
    t = dgu.shape[1]
    tm = _row_tile(t)
    grid = (t // tm,)
    ride_in, ride_in_specs, ride_out, ride_out_specs, ride_sems = _ride_specs(ride)

    def body(dg_ref, wg_ref, wu_ref, h_ref, g_ref, dr_ref, dh_ref, dhb_ref, dgn_ref):
        da_ = _dot_nt(dg_ref[0], wg_ref[...]) + _dot_nt(dg_ref[1], wu_ref[...])
        _rms_bwd_rows(da_, h_ref[...], g_ref[...], dr_ref[...], pl.program_id(0), dh_ref, dhb_ref, dgn_ref)

    row = pl.BlockSpec((tm, D), lambda i: (i, 0))
    res = pl.pallas_call(
        _riding(body, 6, 3, ride, grid), name=name, grid=grid,
        in_specs=[pl.BlockSpec((2, tm, F), lambda i: (0, i, 0)),
                  pl.BlockSpec((D, F), lambda i: (0, 0)),
                  pl.BlockSpec((D, F), lambda i: (0, 1)),
                  row, pl.BlockSpec((1, D), lambda i: (0, 0)), row] + ride_in_specs,
        out_specs=[row, row, pl.BlockSpec((8, D), lambda i: (0, 0))] + ride_out_specs,
        out_shape=[_sds((t, D), F32), _sds((t, D), CDT), _sds((8, D), F32)] + ride_out, scratch_shapes=ride_sems,
        compiler_params=_params(("arbitrary",)),
    )(dgu, w_in, w_in, h, g, dres, *ride_in)
    return (*res[:3], res[3:]) if ride else res


def _mm_nt_rms(a, b, h, g, dres, name):
    t, n = a.shape
    tm = _row_tile(t)

    def body(a_ref, b_ref, h_ref, g_ref, dr_ref, dh_ref, dhb_ref, dgn_ref):
        da_ = _dot_nt(a_ref[...], b_ref[...])
        _rms_bwd_rows(da_, h_ref[...], g_ref[...], dr_ref[...], pl.program_id(0), dh_ref, dhb_ref, dgn_ref)

    row = pl.BlockSpec((tm, D), lambda i: (i, 0))
    return pl.pallas_call(
        body, name=name, grid=(t // tm,),
        in_specs=[pl.BlockSpec((tm, n), lambda i: (i, 0)), pl.BlockSpec((D, n), lambda i: (0, 0)),
                  row, pl.BlockSpec((1, D), lambda i: (0, 0)), row],
        out_specs=[row, row, pl.BlockSpec((8, D), lambda i: (0, 0))],
        out_shape=[_sds((t, D), F32), _sds((t, D), CDT), _sds((8, D), F32)],
        compiler_params=_params(("arbitrary",)),
    )(a, b, h, g, dres)


def _loss(h, target, name):
    t = h.shape[0]

    def body(h_ref, t_ref, dh_ref, dhb_ref, l_ref):
        i = pl.program_id(0)

        @pl.when(i == 0)
        def _():
            l_ref[...] = jnp.zeros(l_ref.shape, F32)
            dh_ref[...] = jnp.zeros(dh_ref.shape, F32)
            dhb_ref[...] = jnp.zeros(dhb_ref.shape, CDT)

        @pl.when(i > 0)
        def _():
            err = h_ref[...] - t_ref[...]
            l_ref[...] += (0.5 / D) * jnp.sum(err * err)
            d = err * (1.0 / D)
            dh_ref[...] = d
            dhb_ref[...] = d.astype(CDT)

    row = pl.BlockSpec((BLK, D), lambda i: (i, 0))
    return pl.pallas_call(
        body, name=name, grid=(t // BLK,),
        in_specs=[row, pl.BlockSpec((BLK, D), lambda i: (jnp.maximum(i - 1, 0), 0))],
        out_specs=[row, row, pl.BlockSpec((8, 128), lambda i: (0, 0))],
        out_shape=[_sds((t, D), F32), _sds((t, D), CDT), _sds((8, 128), F32)],
        compiler_params=_params(("arbitrary",)),
    )(h, target)


def _block_diag():
    return (_iota((128, 128), 0) // HD == _iota((128, 128), 1) // HD).astype(F32)


def _head_sums(v, bd):
    hi = v.astype(CDT)
    rest = (v - hi.astype(F32)).astype(CDT)
    b = bd.astype(CDT)
    return _dot(hi, b) + _dot(rest, b)


def _dup_halves(x, lo):
    sw = pltpu.roll(x, 64, 1)
    return jnp.where(lo, x, sw), jnp.where(lo, sw, x)


def _qknorm_fwd(proj, gfq, gfk, gsq, gsk, fb, name):
    t = proj.shape[0]
    tm = _row_tile(t)

    def body(qa, ka, va, qb, kb, vb, fa, gfq_r, gfk_r, gsq_r, gsk_r, fb_r,
             qf_o, kf_o, vf_o, qs_o, kse_o, vse_o, c_o, ct_o, qft_o, carry):
        i = pl.program_id(0)
        bd = _block_diag()
        lane = _iota((1, 128), 1)
        lo = lane < HD

        def hnorm(x, g):
            ms = _head_sums(x * x, bd) * (1.0 / HD)
            return x * lax.rsqrt(ms + EPS) * g

        for ch in range(4):
            sl = slice(128 * ch, 128 * (ch + 1))
            qn = hnorm(qa[:, sl], gfq_r[:, sl]) * 0.125
            qf_o[:, sl] = qn.astype(CDT)
            qft_o[sl, :] = qn.T.astype(CDT)
            kf_o[:, sl] = hnorm(ka[:, sl], gfk_r[:, sl]).astype(CDT)
            qs_o[:, sl] = (hnorm(qb[:, sl], gsq_r[:, sl]) * 0.125).astype(CDT)
        vf_o[...] = va[...].astype(CDT)
        k0, k1 = _dup_halves(hnorm(kb[...], gsk_r[...]), lo)
        kse_o[0] = k0.astype(CDT)
        kse_o[1] = k1.astype(CDT)
        v0, v1 = _dup_halves(vb[...], lo)
        vse_o[0] = v0.astype(CDT)
        vse_o[1] = v1.astype(CDT)

        z = fa[...] + fb_r[...]
        lf = jnp.minimum(z, 0.0) - jnp.log(1.0 + jnp.exp(-jnp.abs(z)))
        lf = jnp.where(lane < 8, lf, 0.0)
        ltri = (_iota((tm, tm), 1) <= _iota((tm, tm), 0)).astype(F32)

        @pl.when(i == 0)
        def _():
            carry[...] = jnp.zeros(carry.shape, F32)

        c = _dot_hi(ltri, lf) + carry[0:1, :]
        carry[0:1, :] = c[tm - 1:tm, :]
        c_o[...] = c
        ct_o[...] = c.T[0:8, :]

    def col(width, off):
        return pl.BlockSpec((tm, width), lambda i: (i, off // width))

    def vec(width):
        return pl.BlockSpec((1, width), lambda i: (0, 0))

    return pl.pallas_call(
        body, name=name, grid=(t // tm,),
        in_specs=[col(512, QA), col(512, KA), col(512, VA), col(512, QB), col(128, KB), col(128, VB), col(128, FA),
                  vec(512), vec(512), vec(512), vec(128), vec(128)],
        out_specs=[pl.BlockSpec((tm, 512), lambda i: (i, 0))] * 4
        + [pl.BlockSpec((2, tm, 128), lambda i: (0, i, 0))] * 2
        + [pl.BlockSpec((tm, 128), lambda i: (i, 0)), pl.BlockSpec((8, tm), lambda i: (0, i)),
           pl.BlockSpec((512, tm), lambda i: (0, i))],
        out_shape=[_sds((t, 512), CDT)] * 4 + [_sds((2, t, 128), CDT)] * 2
        + [_sds((t, 128), F32), _sds((8, t), F32), _sds((512, t), CDT)],
        scratch_shapes=[pltpu.VMEM((8, 128), F32)],
        compiler_params=_params(("arbitrary",)),
    )(proj, proj, proj, proj, proj, proj, proj, gfq, gfk, gsq, gsk, fb)


def _qknorm_bwd(proj, dqf, dkf, dvf, dqs, dkse, dvse, dcq, dck, dga, dgb, gfq, gfk, gsq, gsk, fb, name):
    t = proj.shape[0]
    tm = _row_tile(t)
    nt = t // tm

    def body(qa, ka, qb, kb, fa, dqf_r, dkf_r, dvf_r, dqs_r, dkse_r, dvse_r, dcq_r, dck_r, dga_r, dgb_r,
             gfq_r, gfk_r, gsq_r, gsk_r, fb_r, dp_o, dgn_o, carry, acc):
        i = pl.program_id(0)
        bd = _block_diag()
        lane = _iota((1, 128), 1)
        lo = lane < HD

        @pl.when(i == 0)
        def _():
            carry[...] = jnp.zeros(carry.shape, F32)
            acc[...] = jnp.zeros(acc.shape, F32)

        def hnorm_bwd(x, g, dy):
            r = lax.rsqrt(_head_sums(x * x, bd) * (1.0 / HD) + EPS)
            xh = x * r
            day = dy * g
            dx = r * (day - xh * (_head_sums(day * xh, bd) * (1.0 / HD)))
            return dx, jnp.sum(dy * xh, axis=0, keepdims=True)

        for ch in range(4):
            sl = slice(128 * ch, 128 * (ch + 1))
            dx, dg = hnorm_bwd(qa[:, sl], gfq_r[:, sl], dqf_r[:, sl] * 0.125)
            dp_o[:, QA + 128 * ch:QA + 128 * (ch + 1)] = dx.astype(CDT)
            acc[0:1, sl] += dg
            dx, dg = hnorm_bwd(ka[:, sl], gfk_r[:, sl], dkf_r[:, sl])
            dp_o[:, KA + 128 * ch:KA + 128 * (ch + 1)] = dx.astype(CDT)
            acc[1:2, sl] += dg
            dx, dg = hnorm_bwd(qb[:, sl], gsq_r[:, sl], dqs_r[:, sl] * 0.125)
            dp_o[:, QB + 128 * ch:QB + 128 * (ch + 1)] = dx.astype(CDT)
            acc[2:3, sl] += dg
        dp_o[:, VA:VA + 512] = dvf_r[...].astype(CDT)
        dp_o[:, GA:GA + D] = dga_r[...]
        dp_o[:, GB:GB + D] = dgb_r[...]

        def fold(x):
            e0 = x[0]
            e1 = x[1]
            return jnp.where(lo, e0 + pltpu.roll(e0, 64, 1), e1 + pltpu.roll(e1, 64, 1))

        dx, dg = hnorm_bwd(kb[...], gsk_r[...], fold(dkse_r))
        dp_o[:, KB:KB + 128] = dx.astype(CDT)
        acc[3:4, 0:128] += dg
        dp_o[:, VB:VB + 128] = fold(dvse_r).astype(CDT)

        rr = _iota((512, 128), 0)
        hh = _iota((512, 128), 1)
        sel = ((rr == (hh >> 1) * 128 + (hh & 1) * HD) & (hh < 8)).astype(F32)
        dcs = _dot_hi(dcq_r[...] - dck_r[...], sel)
        utri = (_iota((tm, tm), 1) >= _iota((tm, tm), 0)).astype(F32)
        dlf = _dot_hi(utri, dcs) + carry[0:1, :]
        carry[0:1, :] = dlf[0:1, :]
        z = fa[...] + fb_r[...]
        dfa = jnp.where(lane < 8, dlf * _sigmoid(-z), 0.0)
        dp_o[:, FA:FA + 128] = dfa.astype(CDT)
        acc[4:5, 0:128] += jnp.sum(dfa, axis=0, keepdims=True)

        @pl.when(i == nt - 1)
        def _():
            foldm = ((_iota((512, 128), 0) & (HD - 1)) == _iota((512, 128), 1)).astype(F32)
            dgn_o[...] = _dot_hi(acc[...], foldm)

    def col(width, off):
        return pl.BlockSpec((tm, width), lambda i: (nt - 1 - i, off // width))

    def rows(width):
        return pl.BlockSpec((tm, width), lambda i: (nt - 1 - i, 0))

    def vec(width):
        return pl.BlockSpec((1, width), lambda i: (0, 0))

    pair = pl.BlockSpec((2, tm, 128), lambda i: (0, nt - 1 - i, 0))
    return pl.pallas_call(
        body, name=name, grid=(nt,),
        in_specs=[col(512, QA), col(512, KA), col(512, QB), col(128, KB), col(128, FA),
                  rows(512), rows(512), rows(512), rows(512), pair, pair, rows(512), rows(512), rows(D), rows(D),
                  vec(512), vec(512), vec(512), vec(128), vec(128)],
        out_specs=[rows(DP), pl.BlockSpec((8, 128), lambda i: (0, 0))],
        out_shape=[_sds((t, DP), CDT), _sds((8, 128), F32)],
        scratch_shapes=[pltpu.VMEM((8, 128), F32), pltpu.VMEM((8, 512), F32)],
        compiler_params=_params(("arbitrary",)),
    )(proj, proj, proj, proj, proj, dqf, dkf, dvf, dqs, dkse, dvse, dcq, dck, dga, dgb, gfq, gfk, gsq, gsk, fb)


def _gate_out_fwd(ofox, oswa, wbf, wbs, proj, w_out, h, name):
    t = ofox.shape[0]
    tm = _row_tile(t)

    def body(of_r, os_r, wf_r, ws_r, ga_r, gb_r, wo_r, h_r, ho_o, yt_o, pf_o, ps_o, oft_o, ost_o):
        pf = _dot(of_r[...], wf_r[...])
        ps = _dot(os_r[...], ws_r[...])
        y = _sigmoid(ga_r[...]) * pf + _sigmoid(gb_r[...]) * ps
        ho_o[...] = h_r[...] + _dot(y.astype(CDT), wo_r[...])
        yt_o[...] = y.T.astype(CDT)
        pf_o[...] = pf.astype(CDT)
        ps_o[...] = ps.astype(CDT)
        oft_o[...] = of_r[...].astype(F32).T.astype(CDT)
        ost_o[...] = os_r[...].astype(F32).T.astype(CDT)

    row = pl.BlockSpec((tm, D), lambda i: (i, 0))
    half = pl.BlockSpec((tm, 512), lambda i: (i, 0))
    whole = lambda r: pl.BlockSpec((r, D), lambda i: (0, 0))
    tcol = lambda r: pl.BlockSpec((r, tm), lambda i: (0, i))
    return pl.pallas_call(
        body, name=name, grid=(t // tm,),
        in_specs=[half, half, whole(512), whole(512),
                  pl.BlockSpec((tm, D), lambda i: (i, GA // D)), pl.BlockSpec((tm, D), lambda i: (i, GB // D)),
                  whole(D), row],
        out_specs=[row, tcol(D), row, row, tcol(512), tcol(512)],
        out_shape=[_sds((t, D), F32), _sds((D, t), CDT), _sds((t, D), CDT), _sds((t, D), CDT),
                   _sds((512, t), CDT), _sds((512, t), CDT)],
        compiler_params=_params(("parallel",)),
    )(ofox, oswa, wbf, wbs, proj, proj, w_out, h)


def _gate_out_bwd(dhb, w_out, pf, ps, proj, name):
    t = dhb.shape[0]
    tm = _row_tile(t)

    def body(dh_r, wo_r, pf_r, ps_r, ga_r, gb_r, dpf_o, dps_o, dga_o, dgb_o):
        dy_ = _dot_nt(dh_r[...], wo_r[...])
        sa = _sigmoid(ga_r[...])
        sb = _sigmoid(gb_r[...])
        dpf_o[...] = (dy_ * sa).astype(CDT)
        dps_o[...] = (dy_ * sb).astype(CDT)
        dga_o[...] = (dy_ * pf_r[...].astype(F32) * (sa * (1.0 - sa))).astype(CDT)
        dgb_o[...] = (dy_ * ps_r[...].astype(F32) * (sb * (1.0 - sb))).astype(CDT)

    row = pl.BlockSpec((tm, D), lambda i: (i, 0))
    return pl.pallas_call(
        body, name=name, grid=(t // tm,),
        in_specs=[row, pl.BlockSpec((D, D), lambda i: (0, 0)), row, row,
                  pl.BlockSpec((tm, D), lambda i: (i, GA // D)), pl.BlockSpec((tm, D), lambda i: (i, GB // D))],
        out_specs=[row] * 4,
        out_shape=[_sds((t, D), CDT)] * 4,
        compiler_params=_params(("parallel",)),
    )(dhb, w_out, pf, ps, proj, proj)


def _tri_steps(n, by_key):
    if by_key:
        pairs = [(i, j) for j in range(n) for i in range(j, n)]
    else:
        pairs = [(i, j) for i in range(n) for j in range(i + 1)]
    return (np.array([p[0] for p in pairs], np.int32), np.array([p[1] for p in pairs], np.int32))


def _head_col(blk, lane, h):
    return jnp.sum(jnp.where(lane == h, blk, 0.0), axis=1, keepdims=True)


def _head_row(blk, sub, h):
    return jnp.sum(jnp.where(sub == h, blk, 0.0), axis=0, keepdims=True)


def _ride_specs(ride):
    if ride is None:
        return [], [], [], [], []
    kind, srcs, outs, layer, items = ride
    return list(srcs), [ANY] * len(srcs), list(outs), [ANY] * len(outs), _dma_sems(3 * len(srcs))


def _ride_start(ride, srcs, dsts, send_sems, recv_sems):
    for cp in _ici_copies(ride[0], srcs, dsts, send_sems, recv_sems, ride[3], recv=False, items=ride[4])[0]:
        cp.start()


def _ride_wait(ride, srcs, dsts, send_sems, recv_sems):
    sends, recvs = _ici_copies(ride[0], srcs, dsts, send_sems, recv_sems, ride[3], items=ride[4])
    for cp in recvs:
        cp.wait_recv()
    for cp in sends:
        cp.wait_send()


def _riding(body, n_in, n_out, ride, grid):
    if ride is None:
        return body
    nr = len(ride[1])

    def wrapped(*refs):
        ins, srcs = refs[:n_in], refs[n_in:n_in + nr]
        outs, dsts = refs[n_in + nr:n_in + nr + n_out], refs[n_in + nr + n_out:n_in + 2 * nr + n_out]
        scratch, sems = refs[n_in + 2 * nr + n_out:-2], refs[-2:]
        first = pl.program_id(0) == 0
        last = pl.program_id(0) == grid[0] - 1
        for a in range(1, len(grid)):
            first = first & (pl.program_id(a) == 0)
            last = last & (pl.program_id(a) == grid[a] - 1)

        @pl.when(first)
        def _():
            _ride_start(ride, srcs, dsts, *sems)

        body(*ins, *outs, *scratch)

        @pl.when(last)
        def _():
            _ride_wait(ride, srcs, dsts, *sems)

    return wrapped


def _fox_fwd(qf, kf, vf, c, ct, name, ride=None):
    t = qf.shape[0]
    ta = _row_tile(t)
    qi, kj = _tri_steps(t // ta, by_key=False)
    nsteps = len(qi)
    ride_in, ride_in_specs, ride_out, ride_out_specs, ride_sems = _ride_specs(ride)

    def body(qi_r, kj_r, q_r, k_r, v_r, c_r, ct_r, *rest):
        nr = len(ride_in)
        src_r, (o_o, lse_o), dst_o = rest[:nr], rest[nr:nr + 2], rest[nr + 2:2 * nr + 2]
        m_sc, l_sc, acc_sc, cq_sc, *sems = rest[2 * nr + 2:]
        p = pl.program_id(0)
        n = pl.program_id(1)
        i = qi_r[n]
        j = kj_r[n]
        lane = _iota((1, 128), 1)
        lo = lane < HD

        if ride is not None:
            @pl.when((p == 0) & (n == 0))
            def _():
                _ride_start(ride, src_r, dst_o, *sems)

        @pl.when(j == 0)
        def _():
            m_sc[...] = jnp.full(m_sc.shape, NEG, F32)
            l_sc[...] = jnp.zeros(l_sc.shape, F32)
            acc_sc[...] = jnp.zeros(acc_sc.shape, F32)
            for e in (0, 1):
                cq_sc[e] = jnp.broadcast_to(_head_col(c_r[...], lane, 2 * p + e), (ta, 128))

        def step(masked):
            q = q_r[...]
            k = k_r[...]
            vaug = jnp.concatenate([v_r[...], jnp.ones((ta, 128), CDT)], axis=1)
            if masked:
                rows = i * ta + _iota((ta, 1), 0)
                cols = j * ta + _iota((1, ta), 1)
                mask = (cols <= rows) & (cols >= PAD)
            sub = _iota((8, 1), 0)
            alphas, pvs = [], []
            for e in (0, 1):
                sel = lo if e == 0 else jnp.logical_not(lo)
                s = _dot_nt(jnp.where(sel, q, 0), k)
                ck = _head_row(ct_r[...], sub, 2 * p + e)
                cq = cq_sc[e]
                chunks = []
                for ch in range(ta // 128):
                    sl = slice(128 * ch, 128 * (ch + 1))
                    sc = s[:, sl] + cq - ck[:, sl]
                    if masked:
                        sc = jnp.where(mask[:, sl], sc, NEG)
                    chunks.append(sc)
                mx = chunks[0]
                for sc in chunks[1:]:
                    mx = jnp.maximum(mx, sc)
                m_prev = m_sc[e]
                m_new = jnp.maximum(m_prev, jnp.max(mx, axis=1, keepdims=True))
                alpha = jnp.exp(m_prev - m_new)
                pe = jnp.concatenate([jnp.exp(sc - m_new).astype(CDT) for sc in chunks], axis=1)
                pva = _dot(pe, vaug)
                l_sc[e] = alpha * l_sc[e] + pva[:, 128:]
                m_sc[e] = m_new
                alphas.append(alpha)
                pvs.append(pva[:, :128])
            acc_sc[...] = acc_sc[...] * jnp.where(lo, alphas[0], alphas[1]) + jnp.where(lo, pvs[0], pvs[1])

        edge = (j == i) | (j == 0)

        @pl.when(edge)
        def _():
            step(True)

        @pl.when(jnp.logical_not(edge))
        def _():
            step(False)

        @pl.when(j == i)
        def _():
            l = jnp.where(lo, l_sc[0], l_sc[1])
            o_o[...] = (acc_sc[...] / l).astype(CDT)
            lse_o[...] = jnp.where(lo, m_sc[0], m_sc[1]) + jnp.log(l)

        if ride is not None:
            @pl.when((p == NPAIR - 1) & (n == nsteps - 1))
            def _():
                _ride_wait(ride, src_r, dst_o, *sems)

    qblk = pl.BlockSpec((ta, 128), lambda p, n, qi_r, kj_r: (qi_r[n], p))
    kblk = pl.BlockSpec((ta, 128), lambda p, n, qi_r, kj_r: (kj_r[n], p))
    grid_spec = pltpu.PrefetchScalarGridSpec(
        num_scalar_prefetch=2, grid=(NPAIR, nsteps),
        in_specs=[qblk, kblk, kblk,
                  pl.BlockSpec((ta, 128), lambda p, n, qi_r, kj_r: (qi_r[n], 0)),
                  pl.BlockSpec((8, ta), lambda p, n, qi_r, kj_r: (0, kj_r[n]))] + ride_in_specs,
        out_specs=[qblk, qblk] + ride_out_specs,
        scratch_shapes=[pltpu.VMEM((2, ta, 128), F32), pltpu.VMEM((2, ta, 128), F32), pltpu.VMEM((ta, 128), F32),
                        pltpu.VMEM((2, ta, 128), F32)] + ride_sems,
    )
    return pl.pallas_call(
        body, name=name, grid_spec=grid_spec,
        out_shape=[_sds((t, 512), CDT), _sds((t, 512), F32)] + ride_out,
        compiler_params=_params(("arbitrary", "arbitrary")),
    )(jnp.asarray(qi), jnp.asarray(kj), qf, kf, vf, c, ct, *ride_in)


def _fox_bwd(qf, qft, kf, vf, c, ct, o, lse, do, dot, name, ride=None):
    t = qf.shape[0]
    ta = _row_tile(t)
    nq = t // ta
    qi, kj = _tri_steps(nq, by_key=False)
    nsteps = len(qi)
    ride_in, ride_in_specs, ride_out, ride_out_specs, ride_sems = _ride_specs(ride)

    def body(qi_r, kj_r, q_r, qt_r, k_r, v_r, c_r, ct_r, o_r, lse_r, do_r, dot_r, *rest):
        nr = len(ride_in)
        src_r, (dq_o, dcq_o, dk_o, dv_o, dck_o), dst_o = rest[:nr], rest[nr:nr + 5], rest[nr + 5:2 * nr + 5]
        lse_sc, dl_sc, cq_sc, dq_sc, dcq_sc, dkt_sc, dvt_sc, dckt_sc, *sems = rest[2 * nr + 5:]
        p = pl.program_id(0)
        n = pl.program_id(1)
        i = qi_r[n]
        j = kj_r[n]
        lane = _iota((1, 128), 1)
        lo = lane < HD
        top = _iota((128, 1), 0) < HD

        if ride is not None:
            @pl.when((p == 0) & (n == 0))
            def _():
                _ride_start(ride, src_r, dst_o, *sems)

        @pl.when(n == 0)
        def _():
            dkt_sc[...] = jnp.zeros(dkt_sc.shape, F32)
            dvt_sc[...] = jnp.zeros(dvt_sc.shape, F32)
            dckt_sc[...] = jnp.zeros(dckt_sc.shape, F32)

        @pl.when(j == 0)
        def _():
            dq_sc[...] = jnp.zeros(dq_sc.shape, F32)
            dcq_sc[...] = jnp.zeros(dcq_sc.shape, F32)
            dd = do_r[...] * o_r[...].astype(F32)
            lse = lse_r[...]
            for e in (0, 1):
                sel = lo if e == 0 else jnp.logical_not(lo)
                cq_sc[e] = jnp.broadcast_to(_head_col(c_r[...], lane, 2 * p + e), (ta, 128))
                dl_sc[e] = jnp.broadcast_to(jnp.sum(jnp.where(sel, dd, 0.0), axis=1, keepdims=True), (ta, 128))
                lse_sc[e] = jnp.broadcast_to(lse[:, HD * e:HD * e + 1], (ta, 128))

        def step(masked):
            q = q_r[...]
            qt = qt_r[...]
            k = k_r[...]
            v = v_r[...]
            dob = do_r[...].astype(CDT)
            dot_ = dot_r[...]
            ones = jnp.ones((ta, 128), CDT)
            ones16 = jnp.ones((16, ta), CDT)
            if masked:
                rows = i * ta + _iota((ta, 1), 0)
                cols = j * ta + _iota((1, ta), 1)
                mask = (cols <= rows) & (cols >= PAD)
            sub = _iota((8, 1), 0)
            for e in (0, 1):
                sel = lo if e == 0 else jnp.logical_not(lo)
                rsel = top if e == 0 else jnp.logical_not(top)
                s = _dot_nt(jnp.where(sel, q, 0), k)
                dp = _dot_nt(jnp.where(sel, dob, 0), v)
                ck = _head_row(ct_r[...], sub, 2 * p + e)
                cq, lse_e, dl = cq_sc[e], lse_sc[e], dl_sc[e]
                prs, dss = [], []
                for ch in range(ta // 128):
                    sl = slice(128 * ch, 128 * (ch + 1))
                    sc = s[:, sl] + cq - ck[:, sl]
                    if masked:
                        sc = jnp.where(mask[:, sl], sc, NEG)
                    pr = jnp.exp(sc - lse_e)
                    prs.append(pr.astype(CDT))
                    dss.append((pr * (dp[:, sl] - dl)).astype(CDT))
                pb = jnp.concatenate(prs, axis=1)
                dsb = jnp.concatenate(dss, axis=1)
                dvt_sc[j] += _dot(jnp.where(rsel, dot_, 0), pb)
                dkc = _dot(jnp.concatenate([jnp.where(rsel, qt, 0), ones16], axis=0), dsb)
                dkt_sc[j] += dkc[0:128]
                dckt_sc[j, 0:8, :] += jnp.where(sub == e, dkc[128:136], 0.0)
                dqa = _dot(dsb, jnp.concatenate([jnp.where(sel, k, 0), ones], axis=1))
                dq_sc[...] += dqa[:, :128]
                dcq_sc[e] += dqa[:, 128:]

        edge = (j == i) | (j == 0)

        @pl.when(edge)
        def _():
            step(True)

        @pl.when(jnp.logical_not(edge))
        def _():
            step(False)

        @pl.when(j == i)
        def _():
            dq_o[...] = dq_sc[...]
            dcq_o[...] = jnp.where(lo, dcq_sc[0], dcq_sc[1])

        @pl.when(n == nsteps - 1)
        def _():
            spread = (_iota((128, 128), 1) == _iota((128, 128), 0) // HD).astype(F32)
            for jb in range(nq):
                rs = slice(jb * ta, (jb + 1) * ta)
                dk_o[rs, :] = dkt_sc[jb].T
                dv_o[rs, :] = dvt_sc[jb].T
                dck_o[rs, :] = _dot_hi(spread, dckt_sc[jb]).T

        if ride is not None:
            @pl.when((p == NPAIR - 1) & (n == nsteps - 1))
            def _():
                _ride_wait(ride, src_r, dst_o, *sems)

    qblk = pl.BlockSpec((ta, 128), lambda p, n, qi_r, kj_r: (qi_r[n], p))
    qtblk = pl.BlockSpec((128, ta), lambda p, n, qi_r, kj_r: (p, qi_r[n]))
    kblk = pl.BlockSpec((ta, 128), lambda p, n, qi_r, kj_r: (kj_r[n], p))
    whole = pl.BlockSpec((t, 128), lambda p, n, qi_r, kj_r: (0, p))
    grid_spec = pltpu.PrefetchScalarGridSpec(
        num_scalar_prefetch=2, grid=(NPAIR, nsteps),
        in_specs=[qblk, qtblk, kblk, kblk,
                  pl.BlockSpec((ta, 128), lambda p, n, qi_r, kj_r: (qi_r[n], 0)),
                  pl.BlockSpec((8, ta), lambda p, n, qi_r, kj_r: (0, kj_r[n])),
                  qblk, qblk, qblk, qtblk] + ride_in_specs,
        out_specs=[qblk, qblk, whole, whole, whole] + ride_out_specs,
        scratch_shapes=[pltpu.VMEM((2, ta, 128), F32)] * 3 + [pltpu.VMEM((ta, 128), F32), pltpu.VMEM((2, ta, 128), F32)]
        + [pltpu.VMEM((nq, 128, ta), F32)] * 3 + ride_sems,
    )
    return pl.pallas_call(
        body, name=name, grid_spec=grid_spec,
        out_shape=[_sds((t, 512), F32)] * 5 + ride_out,
        compiler_params=_params(("arbitrary", "arbitrary")),
    )(jnp.asarray(qi), jnp.asarray(kj), qf, qft, kf, vf, c, ct, o, lse, do, dot, *ride_in)


def _bucket_table():
    r = np.arange(BLK)[:, None]
    c = np.arange(3 * BLK)[None, :]
    d = np.where(c < BLK, r + BLK - c, r - (c - BLK))
    n = np.maximum(d, 0)
    max_exact = N_BUCKETS // 2
    nf = np.maximum(n, 1).astype(np.float32)
    large = max_exact + (np.log(nf / max_exact) / math.log(BLK / max_exact) * (N_BUCKETS - max_exact)).astype(np.int32)
    large = np.minimum(large, N_BUCKETS - 1)
    b = np.where(n < max_exact, n, large)
    return np.where(c < 2 * BLK, b, N_BUCKETS - 1).astype(np.int32)


def _bias_fwd(table, name):
    bucket = jnp.asarray(_bucket_table())

    def body(tab_r, b_r, o_o):
        h = pl.program_id(0)
        b = b_r[...]
        acc = jnp.zeros(b.shape, F32)
        for k in range(N_BUCKETS):
            acc = jnp.where(b == k, tab_r[k, h], acc)
        o_o[...] = acc

    return pl.pallas_call(
        body, name=name, grid=(8,),
        in_specs=[pl.BlockSpec(memory_space=pltpu.SMEM), pl.BlockSpec((BLK, 3 * BLK), lambda h: (0, 0))],
        out_specs=pl.BlockSpec((None, BLK, 3 * BLK), lambda h: (h, 0, 0)),
        out_shape=_sds((8, BLK, 3 * BLK), F32),
        compiler_params=_params(("parallel",)),
    )(table, bucket)


def _bias_bwd(dbias, name):
    bucket = jnp.asarray(_bucket_table())

    def body(d_r, b_r, o_o):
        h = pl.program_id(0)
        b = b_r[...]
        d = d_r[...]
        lane = _iota((1, 128), 1)
        row = jnp.zeros((1, 128), F32)
        for k in range(N_BUCKETS):
            row = jnp.where(lane == k, jnp.sum(jnp.where(b == k, d, 0.0)), row)
        o_o[pl.ds(h, 1), :] = row

    return pl.pallas_call(
        body, name=name, grid=(8,),
        in_specs=[pl.BlockSpec((None, BLK, 3 * BLK), lambda h: (h, 0, 0)), pl.BlockSpec((BLK, 3 * BLK), lambda h: (0, 0))],
        out_specs=pl.BlockSpec((8, 128), lambda h: (0, 0)),
        out_shape=_sds((8, 128), F32),
        compiler_params=_params(("arbitrary",)),
    )(dbias, bucket)


def _swa_valid(i):
    r = _iota((BLK, 1), 0)
    c = _iota((1, 3 * BLK), 1)
    prev = (c < BLK) & (c > r) & (i >= 1) & ((i - 1) * BLK + c >= PAD)
    cc = c - BLK
    cur = (c >= BLK) & (c < 2 * BLK) & (cc <= r) & (i * BLK + cc >= PAD)
    cm = c - 2 * BLK
    meta = (c >= 2 * BLK) & (cm >= PAD) & (i * BLK + r - cm >= BLK)
    return prev | cur | meta


def _swa_kv_specs(ta):
    nb = ta // BLK
    return [pl.BlockSpec((None, BLK, 128), lambda p, i: (p // 2, jnp.maximum(i * nb - 1, 0), 0)),
            pl.BlockSpec((None, ta, 128), lambda p, i: (p // 2, i, 0)),
            pl.BlockSpec((None, BLK, 128), lambda p, i: (p // 2, 0, 0))]


def _swa_fwd(qs, kse, vse, bias, sinks, name, ride=None):
    t = qs.shape[0]
    ta = _row_tile(t)
    nb = ta // BLK
    grid = (NPAIR, t // ta)
    ride_in, ride_in_specs, ride_out, ride_out_specs, ride_sems = _ride_specs(ride)

    def body(sink_r, q_r, kp_r, kc_r, km_r, vp_r, vc_r, vm_r, b_r, o_o, lse_o):
        p = pl.program_id(0)
        i = pl.program_id(1)
        lo = _iota((1, 128), 1) < HD
        k4 = jnp.concatenate([kp_r[...], kc_r[...]], axis=0)
        v4 = jnp.concatenate([vp_r[...], vc_r[...]], axis=0)
        work = [(b, e) for b in range(nb) for e in (0, 1)]
        sinks = [sink_r[2 * p + e] for e in (0, 1)]
        v3 = [jnp.concatenate([v4[BLK * b:BLK * (b + 2)], vm_r[...]], axis=0) for b in range(nb)]
        s = {}
        for b in range(nb):
            q = q_r[BLK * b:BLK * (b + 1), :]
            k3 = jnp.concatenate([k4[BLK * b:BLK * (b + 2)], km_r[...]], axis=0)
            valid = _swa_valid(i * nb + b)
            for e in (0, 1):
                sel = lo if e == 0 else jnp.logical_not(lo)
                s[b, e] = jnp.where(valid, _dot_nt(jnp.where(sel, q, 0), k3) + b_r[e], NEG)
        mx = {w: jnp.maximum(jnp.max(s[w], axis=1, keepdims=True), sinks[w[1]]) for w in work}
        pe = {w: jnp.exp(s[w] - mx[w]) for w in work}
        den = {w: jnp.sum(pe[w], axis=1, keepdims=True) + jnp.exp(sinks[w[1]] - mx[w]) for w in work}
        out = {w: _dot(pe[w].astype(CDT), v3[w[0]]) / den[w] for w in work}
        for b in range(nb):
            rows = slice(BLK * b, BLK * (b + 1))
            o_o[rows, :] = jnp.where(lo, out[b, 0], out[b, 1]).astype(CDT)
            lse_o[rows, :] = jnp.where(lo, mx[b, 0] + jnp.log(den[b, 0]), mx[b, 1] + jnp.log(den[b, 1]))

    qblk = pl.BlockSpec((ta, 128), lambda p, i: (i, p))
    res = pl.pallas_call(
        _riding(body, 9, 2, ride, grid), name=name, grid=grid,
        in_specs=[pl.BlockSpec(memory_space=pltpu.SMEM), qblk] + _swa_kv_specs(ta) + _swa_kv_specs(ta)
        + [pl.BlockSpec((2, BLK, 3 * BLK), lambda p, i: (p, 0, 0))] + ride_in_specs,
        out_specs=[qblk, qblk] + ride_out_specs,
        out_shape=[_sds((t, 512), CDT), _sds((t, 512), F32)] + ride_out, scratch_shapes=ride_sems,
        compiler_params=_params(("arbitrary", "arbitrary") if ride else ("parallel", "parallel")),
    )(sinks, qs, kse, kse, kse, vse, vse, vse, bias, *ride_in)
    return (res[0], res[1], res[2:]) if ride else res


def _swa_bwd(qs, kse, vse, bias, sinks, o, lse, do, name):
    t = qs.shape[0]
    ta = _row_tile(t)
    nb = ta // BLK

    def body(sink_r, q_r, kp_r, kc_r, km_r, vp_r, vc_r, vm_r, b_r, o_r, lse_r, do_r,
             dq_o, dk_o, dv_o, db_o, dsk_o):
        p = pl.program_id(0)
        i = pl.program_id(1)
        lo = _iota((1, 128), 1) < HD

        @pl.when((i == 0) & (p % 2 == 0))
        def _():
            dk_o[...] = jnp.zeros(dk_o.shape, F32)
            dv_o[...] = jnp.zeros(dv_o.shape, F32)

        @pl.when(i == 0)
        def _():
            db_o[...] = jnp.zeros(db_o.shape, F32)
            dsk_o[...] = jnp.zeros(dsk_o.shape, F32)

        k4 = jnp.concatenate([kp_r[...], kc_r[...]], axis=0)
        v4 = jnp.concatenate([vp_r[...], vc_r[...]], axis=0)
        work = [(b, e) for b in range(nb) for e in (0, 1)]
        sel = [lo, jnp.logical_not(lo)]
        k3 = [jnp.concatenate([k4[BLK * b:BLK * (b + 2)], km_r[...]], axis=0) for b in range(nb)]
        v3 = [jnp.concatenate([v4[BLK * b:BLK * (b + 2)], vm_r[...]], axis=0) for b in range(nb)]
        q = [q_r[BLK * b:BLK * (b + 1), :] for b in range(nb)]
        do_ = [do_r[BLK * b:BLK * (b + 1), :] for b in range(nb)]
        lse = [lse_r[BLK * b:BLK * (b + 1), :] for b in range(nb)]
        dd = [do_[b] * o_r[BLK * b:BLK * (b + 1), :].astype(F32) for b in range(nb)]
        valid = [_swa_valid(i * nb + b) for b in range(nb)]
        qe = {(b, e): jnp.where(sel[e], q[b], 0) for b, e in work}
        doe = {(b, e): jnp.where(sel[e], do_[b], 0.0).astype(CDT) for b, e in work}
        lse_e = {(b, e): lse[b][:, HD * e:HD * e + 1] for b, e in work}
        delta = {(b, e): jnp.sum(jnp.where(sel[e], dd[b], 0.0), axis=1, keepdims=True) for b, e in work}
        s = {(b, e): jnp.where(valid[b], _dot_nt(qe[b, e], k3[b]) + b_r[e], NEG) for b, e in work}
        dp = {(b, e): _dot_nt(doe[b, e], v3[b]) for b, e in work}
        pr = {w: jnp.exp(s[w] - lse_e[w]) for w in work}
        ds = {w: pr[w] * (dp[w] - delta[w]) for w in work}
        dqs = {(b, e): _dot(ds[b, e].astype(CDT), jnp.where(sel[e], k3[b], 0)) for b, e in work}
        dk3 = {(b, e): _dot(ds[b, e].T.astype(CDT), qe[b, e]) for b, e in work}
        dv3 = {(b, e): _dot(pr[b, e].T.astype(CDT), doe[b, e]) for b, e in work}
        for e in (0, 1):
            tot = ds[0, e]
            for b in range(1, nb):
                tot = tot + ds[b, e]
            db_o[e] += tot
        dsink = [sum(-jnp.sum(jnp.exp(sink_r[2 * p + e] - lse_e[b, e]) * delta[b, e], axis=0, keepdims=True)
                     for b in range(nb)) for e in (0, 1)]
        dsk_o[0:1, :] += jnp.where(lo, dsink[0], dsink[1])
        for b in range(nb):
            ib = i * nb + b
            dq_o[BLK * b:BLK * (b + 1), :] = dqs[b, 0] + dqs[b, 1]
            dk = dk3[b, 0] + dk3[b, 1]
            dv = dv3[b, 0] + dv3[b, 1]
            prev = pl.ds(pl.multiple_of(jnp.maximum(ib - 1, 0) * BLK, BLK), BLK)
            cur = pl.ds(pl.multiple_of(ib * BLK, BLK), BLK)
            dk_o[prev, :] += dk[0:BLK]
            dk_o[cur, :] += dk[BLK:2 * BLK]
            dk_o[0:BLK, :] += dk[2 * BLK:]
            dv_o[prev, :] += dv[0:BLK]
            dv_o[cur, :] += dv[BLK:2 * BLK]
            dv_o[0:BLK, :] += dv[2 * BLK:]

    qblk = pl.BlockSpec((ta, 128), lambda p, i: (i, p))
    kvacc = pl.BlockSpec((None, t, 128), lambda p, i: (p // 2, 0, 0))
    bblk = pl.BlockSpec((2, BLK, 3 * BLK), lambda p, i: (p, 0, 0))
    return pl.pallas_call(
        body, name=name, grid=(NPAIR, t // ta),
        in_specs=[pl.BlockSpec(memory_space=pltpu.SMEM), qblk] + _swa_kv_specs(ta) + _swa_kv_specs(ta)
        + [bblk, qblk, qblk, qblk],
        out_specs=[qblk, kvacc, kvacc, bblk, pl.BlockSpec((None, 8, 128), lambda p, i: (p, 0, 0))],
        out_shape=[_sds((t, 512), F32), _sds((2, t, 128), F32), _sds((2, t, 128), F32),
                   _sds((8, BLK, 3 * BLK), F32), _sds((NPAIR, 8, 128), F32)],
        compiler_params=_params(("arbitrary", "arbitrary")),
    )(sinks, qs, kse, kse, kse, vse, vse, vse, bias, o, lse, do)


def _sum8(slots, name):
    def body(a_r, o_o):
        acc = a_r[0]
        for k in range(1, 8):
            acc = acc + a_r[k]
        o_o[...] = acc

    return pl.pallas_call(
        body, name=name, out_shape=_sds((SMALL_ROWS, 128), F32),
        in_specs=[pl.BlockSpec(memory_space=pltpu.VMEM)], out_specs=pl.BlockSpec(memory_space=pltpu.VMEM),
        compiler_params=_params(),
    )(slots)


def _place():
    x, y, c = lax.axis_index("x"), lax.axis_index("y"), lax.axis_index("c")
    chips = [(1 - x, y), (x, 1 - y), (1 - x, 1 - y)]
    return x, y, c, chips


def _remote(src, dst, send_sems, recv_sems, k, to):
    return pltpu.make_async_remote_copy(src_ref=src, dst_ref=dst, send_sem=send_sems.at[k], recv_sem=recv_sems.at[k],
                                        device_id=to, device_id_type=MESH_ID)


ANY = pl.BlockSpec(memory_space=pl.ANY)


def _mix_cols(w):
    return jnp.concatenate([w[:, 2312:4360], w[:, 0:1536], w[:, 1544:2312], w[:, 1536:1544],
                            jnp.zeros((w.shape[0], DP - D_IN), w.dtype)], axis=1)


def _unmix_cols(w):
    return jnp.concatenate([w[:, QA:QA + 1536], w[:, FA:FA + 8], w[:, QB:QB + 768], w[:, GA:GA + 2048]], axis=1)


def _rows128(a, rows):
    flat = a.reshape(-1)
    return jnp.pad(flat, (0, rows * 128 - flat.shape[0])).reshape(rows, 128)


GRAD_FORM = {"ffn1_w_in": "col", "ffn2_w_in": "col", "w_branch_fox": "col", "w_branch_swa": "col",
             "ffn1_w_out": "3d", "ffn2_w_out": "3d", "w_out": "3d", "w_in": "3d"}
SUM_TILE = {1024: 128, 704: 176, 512: 128, 256: 128}
NT = len(SHARD_ITEMS)
ALL_ITEMS = tuple(range(NT))


def _half_rows(c, r):
    return pl.ds(pl.multiple_of(c * (r // 2), 16), r // 2)


def _ici_copies(kind, srcs, dsts, send_sems, recv_sems, layer, recv=True, items=ALL_ITEMS):
    x, y, c, chips = _place()
    s = 2 * x + y
    sends, recvs = [], []
    for t, (item, src, dst) in enumerate(zip(items, srcs, dsts)):
        nm, (r, cc), _ = SHARD_ITEMS[item]
        for j, (cx, cy) in enumerate(chips):
            sj = 2 * cx + cy
            k = 3 * t + j
            to = (cx, cy, c)
            if kind == "gather":
                hs = _half_rows(c, r)
                sends.append(_remote(src.at[layer, hs], dst.at[s, hs], send_sems, recv_sems, k, to))
                if recv:
                    recvs.append(_remote(src.at[layer, hs], dst.at[sj, hs], send_sems, recv_sems, k, to))
            else:
                if GRAD_FORM[nm] == "col":
                    piece = src.at[:, pl.ds(pl.multiple_of(sj * cc, 128), cc)]
                else:
                    piece = src.at[sj]
                sends.append(_remote(piece, dst.at[j], send_sems, recv_sems, k, to))
                recvs.append(sends[-1])
    return sends, recvs


def _slab_shapes(items=ALL_ITEMS):
    return [_sds((4, *SHARD_ITEMS[t][1]), CDT) for t in items]


def _dma_sems(n):
    return [pltpu.SemaphoreType.DMA((n,)), pltpu.SemaphoreType.DMA((n,))]


def _forward_sends(dsts, send_sems, recv_sems, items=ALL_ITEMS):
    x, y, c, chips = _place()
    sends, recvs = [], []
    for t, (item, dst) in enumerate(zip(items, dsts)):
        r = SHARD_ITEMS[item][1][0]
        for j, (cx, cy) in enumerate(chips):
            sj = 2 * cx + cy
            hs, ho = _half_rows(c, r), _half_rows(1 - c, r)
            sends.append(_remote(dst.at[sj, hs], dst.at[sj, hs], send_sems, recv_sems, 3 * t + j, (x, y, 1 - c)))
            recvs.append(_remote(dst.at[sj, ho], dst.at[sj, ho], send_sems, recv_sems, 3 * t + j, (x, y, 1 - c)))
    return sends, recvs


def _gather_layer(wb, mflat, layer, name, items):
    nt = len(items)

    def body(*refs):
        srcs, m_r, dsts, mall_o = refs[:nt], refs[nt], refs[nt + 1:2 * nt + 1], refs[2 * nt + 1]
        send_sems, recv_sems, fsend, frecv, msend, mrecv = refs[2 * nt + 2:]
        x, y, c, chips = _place()
        s = 2 * x + y
        sends, recvs = _ici_copies("gather", srcs, dsts, send_sems, recv_sems, layer, items=items)
        metas = [_remote(m_r, mall_o.at[s], msend, mrecv, j, (cx, cy, c)) for j, (cx, cy) in enumerate(chips)]
        for cp in sends + metas:
            cp.start()
        fwds, frecvs = _forward_sends(dsts, fsend, frecv, items)
        for got, fwd in zip(recvs, fwds):
            got.wait_recv()
            fwd.start()
        for got in frecvs:
            got.wait_recv()
        for j, (cx, cy) in enumerate(chips):
            _remote(m_r, mall_o.at[2 * cx + cy], msend, mrecv, j, (cx, cy, c)).wait_recv()
        for cp in sends + metas + fwds:
            cp.wait_send()

    return pl.pallas_call(
        body, name=name, out_shape=_slab_shapes(items) + [_sds((4, META_ROWS, 128), F32)],
        in_specs=[ANY] * (nt + 1), out_specs=[ANY] * (nt + 1),
        scratch_shapes=_dma_sems(3 * nt) + _dma_sems(3 * nt) + _dma_sems(3),
    )(*wb, mflat)


def _forward_layer(slabs, name, items=ALL_ITEMS):
    nt = len(items)

    def body(*refs):
        ins, outs, send_sems, recv_sems = refs[:nt], refs[nt:2 * nt], refs[2 * nt], refs[2 * nt + 1]
        sends, recvs = _forward_sends(outs, send_sems, recv_sems, items)
        for cp in sends:
            cp.start()
        for cp in recvs:
            cp.wait_recv()
        for cp in sends:
            cp.wait_send()

    return pl.pallas_call(
        body, name=name, out_shape=_slab_shapes(items), in_specs=[ANY] * nt, out_specs=[ANY] * nt,
        input_output_aliases={t: t for t in range(nt)}, scratch_shapes=_dma_sems(3 * nt),
    )(*slabs)


def _half_shape(nm, r, c):
    return (r // 2, 4 * c) if GRAD_FORM[nm] == "col" else (4, r // 2, c)


def _swap_layer(gs, gsm, name, items=ALL_ITEMS):
    small = gsm is not None
    nt = len(items)

    def body(*refs):
        g_rs = refs[:nt]
        pos = nt
        if small:
            s_r = refs[pos]
            pos += 1
        got_os = refs[pos:pos + nt]
        pos += nt
        if small:
            slots_o = refs[pos]
            pos += 1
        send_sems, recv_sems = refs[pos], refs[pos + 1]
        x, y, c, _ = _place()
        sib = (x, y, 1 - c)
        sent = []
        for t, (item, g_r, got_o) in enumerate(zip(items, g_rs, got_os)):
            nm, (r, cc), _ = SHARD_ITEMS[item]
            ho = _half_rows(1 - c, r)
            src = g_r.at[ho, :] if GRAD_FORM[nm] == "col" else g_r.at[:, ho, :]
            sent.append(_remote(src, got_o, send_sems, recv_sems, t, sib))
        if small:
            ssend, srecv, loc_sem = refs[pos + 2], refs[pos + 3], refs[pos + 4]
            me = 4 * x + 2 * y + c
            loc = pltpu.make_async_copy(s_r, slots_o.at[me], loc_sem.at[0])
            loc.start()
            peers = [(x ^ (k >> 2), y ^ ((k >> 1) & 1), c ^ (k & 1)) for k in range(1, 8)]
            for k, peer in enumerate(peers):
                sent.append(_remote(s_r, slots_o.at[me], ssend, srecv, k, peer))
        for cp in sent:
            cp.start()
        for cp in sent[:nt]:
            cp.wait_recv()
        if small:
            for k, (px, py, pc) in enumerate(peers):
                _remote(s_r, slots_o.at[4 * px + 2 * py + pc], ssend, srecv, k, (px, py, pc)).wait_recv()
        for cp in sent:
            cp.wait_send()
        if small:
            loc.wait()

    outs = [_sds(_half_shape(*SHARD_ITEMS[item][0:1], *SHARD_ITEMS[item][1]), CDT) for item in items]
    ops = list(gs)
    sems = _dma_sems(nt)
    if small:
        outs.append(_sds((8, SMALL_ROWS, 128), F32))
        ops.append(gsm)
        sems = sems + _dma_sems(7) + [pltpu.SemaphoreType.DMA((1,))]
    res = pl.pallas_call(
        body, name=name, out_shape=outs, in_specs=[ANY] * len(ops), out_specs=[ANY] * len(outs), scratch_shapes=sems,
    )(*ops)
    return (res[:nt], res[nt]) if small else (res, None)


def _pair_add_t(own, got, half_idx, nm, r, name):
    tr = SUM_TILE[r]
    nb = (r // 2) // tr
    if GRAD_FORM[nm] == "col":
        blk = (tr, own.shape[1])
        own_spec = pl.BlockSpec(blk, lambda i, c_r: (c_r[0] * nb + i, 0))
        half_spec = pl.BlockSpec(blk, lambda i, c_r: (i, 0))
    else:
        blk = (4, tr, own.shape[2])
        own_spec = pl.BlockSpec(blk, lambda i, c_r: (0, c_r[0] * nb + i, 0))
        half_spec = pl.BlockSpec(blk, lambda i, c_r: (0, i, 0))

    def body(c_r, a_r, b_r, o_o):
        o_o[...] = (a_r[...].astype(F32) + b_r[...].astype(F32)).astype(CDT)

    grid_spec = pltpu.PrefetchScalarGridSpec(num_scalar_prefetch=1, grid=(nb,), in_specs=[own_spec, half_spec],
                                             out_specs=half_spec)
    return pl.pallas_call(body, name=name, grid_spec=grid_spec, out_shape=_sds(got.shape, CDT),
                          compiler_params=_params(("parallel",)))(half_idx, own, got)


def _sum4_t(ps, got3, buf, idx, layer, nm, r, name):
    tr = SUM_TILE[r]
    nb = (r // 2) // tr
    c = got3.shape[2]
    if GRAD_FORM[nm] == "col":
        ps_spec = pl.BlockSpec((tr, c), lambda i, x_r: (i, x_r[0]))
    else:
        ps_spec = pl.BlockSpec((None, tr, c), lambda i, x_r: (x_r[0], i, 0))

    def body(x_r, a_r, b_r, buf_r, o_o):
        o_o[...] = ((a_r[...].astype(F32) + b_r[0].astype(F32)) + b_r[1].astype(F32)) + b_r[2].astype(F32)

    grid_spec = pltpu.PrefetchScalarGridSpec(
        num_scalar_prefetch=1, grid=(nb,),
        in_specs=[ps_spec, pl.BlockSpec((3, tr, c), lambda i, x_r: (0, i, 0)), ANY],
        out_specs=pl.BlockSpec((None, tr, c), lambda i, x_r: (layer, x_r[1] * nb + i, 0)),
    )
    return pl.pallas_call(body, name=name, grid_spec=grid_spec, out_shape=_sds(buf.shape, F32),
                          input_output_aliases={3: 0}, compiler_params=_params(("parallel",)))(idx, ps, got3, buf)


def _scatter_layer(ps, name, items=ALL_ITEMS):
    nt = len(items)

    def body(*refs):
        srcs, dsts, send_sems, recv_sems = refs[:nt], refs[nt:2 * nt], refs[2 * nt], refs[2 * nt + 1]
        sends, recvs = _ici_copies("scatter", srcs, dsts, send_sems, recv_sems, None, items=items)
        for cp in sends:
            cp.start()
        for cp in recvs:
            cp.wait_recv()
        for cp in sends:
            cp.wait_send()

    return pl.pallas_call(
        body, name=name, out_shape=_got3_shapes(items), in_specs=[ANY] * nt, out_specs=[ANY] * nt,
        scratch_shapes=_dma_sems(3 * nt),
    )(*ps)


def _got3_shapes(items=ALL_ITEMS):
    return [_sds((3, SHARD_ITEMS[t][1][0] // 2, SHARD_ITEMS[t][1][1]), CDT) for t in items]


def _join_layer(bufs, name):
    def body(*refs):
        ins, outs, send_sems, recv_sems = refs[:NT], refs[NT:2 * NT], refs[2 * NT], refs[2 * NT + 1]
        x, y, c, _ = _place()
        sent = []
        for t, ((nm, (r, cc), _), b_o) in enumerate(zip(SHARD_ITEMS, outs)):
            hs = _half_rows(c, r)
            sent.append(_remote(b_o.at[:, hs, :], b_o.at[:, hs, :], send_sems, recv_sems, t, (x, y, 1 - c)))
        for cp in sent:
            cp.start()
        for t, ((nm, (r, cc), _), b_o) in enumerate(zip(SHARD_ITEMS, outs)):
            ho = _half_rows(1 - c, r)
            _remote(b_o.at[:, ho, :], b_o.at[:, ho, :], send_sems, recv_sems, t, (x, y, 1 - c)).wait_recv()
        for cp in sent:
            cp.wait_send()

    return pl.pallas_call(
        body, name=name, out_shape=[_sds(b.shape, F32) for b in bufs], in_specs=[ANY] * NT, out_specs=[ANY] * NT,
        input_output_aliases={t: t for t in range(NT)}, scratch_shapes=_dma_sems(NT),
    )(*bufs)


def _adamw3(w, g, m, v, name):
    nl, r, c = w.shape
    tr = SUM_TILE.get(r, r)
    if r % 8:
        blk = pl.BlockSpec((None, r, 256), lambda l, i: (l, 0, i))
        steps = c // 256
    else:
        blk = pl.BlockSpec((None, tr, c), lambda l, i: (l, i, 0))
        steps = r // tr

    def body(w_r, g_r, m_r, v_r, d_o, m_o, v_o):
        g_ = g_r[...]
        m_ = ADAM_B1 * m_r[...] + (1.0 - ADAM_B1) * g_
        v_ = ADAM_B2 * v_r[...] + (1.0 - ADAM_B2) * jnp.square(g_)
        m_hat = m_ / (1.0 - ADAM_B1 ** ADAM_STEP)
        v_hat = v_ / (1.0 - ADAM_B2 ** ADAM_STEP)
        d_o[...] = -ADAM_LR * (m_hat / (jnp.sqrt(v_hat) + ADAM_EPS) + ADAM_WD * w_r[...])
        m_o[...] = m_
        v_o[...] = v_

    return pl.pallas_call(
        body, name=name, grid=(nl, steps),
        in_specs=[blk] * 4, out_specs=[blk] * 3, out_shape=[_sds((nl, r, c), F32)] * 3,
        compiler_params=_params(("parallel", "parallel")),
    )(w, g, m, v)


def _full_weights(slabs, wb, layer, shard, items=ALL_ITEMS):
    ws = {}
    for t, slab in zip(items, slabs):
        nm, (r, c), kind = SHARD_ITEMS[t]
        slab = lax.dynamic_update_slice(slab, wb[nm][layer][None], (shard, 0, 0))
        ws[nm] = slab.reshape(4 * r, c) if kind == "row" else jnp.concatenate([slab[s] for s in range(4)], axis=1)
    return ws


def _exchange_forms(g, items=ALL_ITEMS):
    out = []
    for t in items:
        nm, (r, c), _ = SHARD_ITEMS[t]
        a = g[nm]
        if nm == "w_in":
            a = a.reshape(D, 4, c).transpose(1, 0, 2)
        elif GRAD_FORM[nm] == "3d":
            a = a.reshape(4, r, c)
        out.append(a)
    return out


SMALL_ITEMS = (("rel_bias_table", 2), ("ffn1_norm", 16), ("mix_norm", 16), ("ffn2_norm", 16), ("forget_bias", 1),
               ("fox_q_norm", 1), ("fox_k_norm", 1), ("swa_q_norm", 1), ("swa_k_norm", 1), ("swa_sinks", 1))
SMALL_ADAM_ROWS = 96


def _layer_fwd(h, lw, l, ride=None, late=None):
    rides = late["rides"] if late else {}

    def run(key, fn, *args):
        r = rides.get(key)
        if r is None:
            return fn(*args)
        out = fn(*args, ride=r)
        late["arrived"](key, out[-1])
        return out[0] if len(out) == 2 else out[:-1]

    sv = {"h0": h}
    a, sv["a1t"] = _rms_fwd(h, lw["ffn1_norm"], f"rms_fwd_a{l}")
    sv["gu1"], s, sv["s1t"] = run("ffn_in_a", _ffn_in, a, lw["ffn1_w_in"], f"ffn_in_a{l}")
    h = run("ffn_out_a", _mm_res, s, lw["ffn1_w_out"], h, 0.5, f"ffn_out_a{l}")
    sv["h1"] = h
    a, sv["amt"] = _rms_fwd(h, lw["mix_norm"], f"rms_fwd_m{l}")
    if late:
        late["need"](lw, "mixer")
    proj = run("proj", _mm, a, lw["w_mix"], F32, _row_tile(h.shape[0]), DP, f"proj{l}")
    sv["proj"] = proj
    qf, kf, vf, qs, kse, vse, c, ct, sv["qft"] = _qknorm_fwd(proj, lw["gfq"], lw["gfk"], lw["gsq"], lw["gsk"], lw["fb"],
                                                              f"qknorm_fwd{l}")
    ofox, lse_f, *rode = _fox_fwd(qf, kf, vf, c, ct, f"fox_fwd{l}", ride)
    oswa, lse_s = run("swa_fwd", _swa_fwd, qs, kse, vse, lw["bias"], lw["sinks"], f"swa_fwd{l}")
    if late:
        late["need"](lw, "gate")
    sv.update(qf=qf, kf=kf, vf=vf, qs=qs, kse=kse, vse=vse, c=c, ct=ct, ofox=ofox, oswa=oswa, lse_f=lse_f, lse_s=lse_s)
    h, sv["yt"], sv["pf"], sv["ps"], sv["oft"], sv["ost"] = _gate_out_fwd(
        ofox, oswa, lw["w_branch_fox"], lw["w_branch_swa"], proj, lw["w_out"], h, f"gate_out_fwd{l}")
    sv["h2"] = h
    a, sv["a2t"] = _rms_fwd(h, lw["ffn2_norm"], f"rms_fwd_b{l}")
    sv["gu2"], s, sv["s2t"] = _ffn_in(a, lw["ffn2_w_in"], f"ffn_in_b{l}")
    h = _mm_res(s, lw["ffn2_w_out"], h, 0.5, f"ffn_out_b{l}")
    return h, sv, rode


def _ffn_bwd(dh, dhb, h_in, at, gu, st, norm, w_in, w_out, tag, rides=None):
    r = rides or (None,) * 4
    rode = []

    def split(res, ride):
        if ride is None:
            return res
        rode.extend(res[-1])
        return res[0] if len(res) == 2 else res[:-1]

    dgu = split(_ffn_bwd_mid(dhb, w_out, gu, f"ffn_bwd_mid_{tag}", r[0]), r[0])
    d_w_out = split(_mm(st, dhb, CDT, 256, D, f"dw_ffn_out_{tag}", scale=0.5, ride=r[1]), r[1])
    dh, dhb, dg = split(_ffn_bwd_in(dgu, w_in, h_in, norm, dh, f"ffn_bwd_in_{tag}", r[2]), r[2])
    d_w_in = split(_mm(at, dgu, CDT, D, 256, f"dw_ffn_in_{tag}", ride=r[3]), r[3])
    return dh, dhb, d_w_out, d_w_in, dg, rode


def _layer_bwd(dh, dhb, sv, lw, l, ride=None, before_ffn1=None):
    g = {}
    dh, dhb, g["ffn2_w_out"], g["ffn2_w_in"], g["ffn2_norm"], _ = _ffn_bwd(
        dh, dhb, sv["h2"], sv["a2t"], sv["gu2"], sv["s2t"], lw["ffn2_norm"], lw["ffn2_w_in"], lw["ffn2_w_out"], f"b{l}")
    g["w_out"] = _mm(sv["yt"], dhb, CDT, 512, 512, f"dw_out{l}")
    dpf, dps, dga, dgb = _gate_out_bwd(dhb, lw["w_out"], sv["pf"], sv["ps"], sv["proj"], f"gate_out_bwd{l}")
    do_f, do_ft = _mm_nt(dpf, lw["w_branch_fox"], f"d_ofox{l}", with_t=True)
    do_s = _mm_nt(dps, lw["w_branch_swa"], f"d_oswa{l}")
    g["w_branch_fox"] = _mm(sv["oft"], dpf, CDT, 512, 512, f"dw_bfox{l}")
    g["w_branch_swa"] = _mm(sv["ost"], dps, CDT, 512, 512, f"dw_bswa{l}")
    dqf, dcq, dkf, dvf, dck, *rode = _fox_bwd(sv["qf"], sv["qft"], sv["kf"], sv["vf"], sv["c"], sv["ct"], sv["ofox"],
                                              sv["lse_f"], do_f, do_ft, f"fox_bwd{l}", ride)
    g["rode"] = rode
    dqs, dkse, dvse, dbias, dsk = _swa_bwd(sv["qs"], sv["kse"], sv["vse"], lw["bias"], lw["sinks"], sv["oswa"],
                                           sv["lse_s"], do_s, f"swa_bwd{l}")
    dproj, dgn = _qknorm_bwd(sv["proj"], dqf, dkf, dvf, dqs, dkse, dvse, dcq, dck, dga, dgb,
                             lw["gfq"], lw["gfk"], lw["gsq"], lw["gsk"], lw["fb"], f"qknorm_bwd{l}")
    g["w_mix"] = _mm(sv["amt"], dproj, CDT, 512, 640, f"dw_mix{l}")
    dh, dhb, g["mix_norm"] = _mm_nt_rms(dproj, lw["w_mix"], sv["h1"], lw["mix_norm"], dh, f"d_am{l}")
    g["dbias"], g["dsk"], g["dgn"] = dbias, dsk, dgn
    rides = before_ffn1(g) if before_ffn1 else None
    dh, dhb, g["ffn1_w_out"], g["ffn1_w_in"], g["ffn1_norm"], g["rode_ffn1"] = _ffn_bwd(
        dh, dhb, sv["h0"], sv["a1t"], sv["gu1"], sv["s1t"], lw["ffn1_norm"], lw["ffn1_w_in"], lw["ffn1_w_out"], f"a{l}",
        rides)
    return dh, dhb, g


def kernel(x, meta_tokens, rel_bias_table, ffn1_norm, ffn1_w_in, ffn1_w_out, mix_norm, w_in, forget_bias, fox_q_norm, fox_k_norm, swa_q_norm, swa_k_norm, swa_sinks, w_branch_fox, w_branch_swa, w_out, ffn2_norm, ffn2_w_in, ffn2_w_out, loss_target, m_meta_tokens, m_rel_bias_table, m_ffn1_norm, m_ffn1_w_in, m_ffn1_w_out, m_mix_norm, m_w_in, m_forget_bias, m_fox_q_norm, m_fox_k_norm, m_swa_q_norm, m_swa_k_norm, m_swa_sinks, m_w_branch_fox, m_w_branch_swa, m_w_out, m_ffn2_norm, m_ffn2_w_in, m_ffn2_w_out, v_meta_tokens, v_rel_bias_table, v_ffn1_norm, v_ffn1_w_in, v_ffn1_w_out, v_mix_norm, v_w_in, v_forget_bias, v_fox_q_norm, v_fox_k_norm, v_swa_q_norm, v_swa_k_norm, v_swa_sinks, v_w_branch_fox, v_w_branch_swa, v_w_out, v_ffn2_norm, v_ffn2_w_in, v_ffn2_w_out):
    names = ["meta_tokens", "rel_bias_table", "ffn1_norm", "ffn1_w_in", "ffn1_w_out", "mix_norm", "w_in", "forget_bias",
             "fox_q_norm", "fox_k_norm", "swa_q_norm", "swa_k_norm", "swa_sinks", "w_branch_fox", "w_branch_swa", "w_out",
             "ffn2_norm", "ffn2_w_in", "ffn2_w_out"]
    w = dict(zip(names, [meta_tokens, rel_bias_table, ffn1_norm, ffn1_w_in, ffn1_w_out, mix_norm, w_in, forget_bias,
                         fox_q_norm, fox_k_norm, swa_q_norm, swa_k_norm, swa_sinks, w_branch_fox, w_branch_swa, w_out,
                         ffn2_norm, ffn2_w_in, ffn2_w_out]))
    m = dict(zip(names, [m_meta_tokens, m_rel_bias_table, m_ffn1_norm, m_ffn1_w_in, m_ffn1_w_out, m_mix_norm, m_w_in,
                         m_forget_bias, m_fox_q_norm, m_fox_k_norm, m_swa_q_norm, m_swa_k_norm, m_swa_sinks,
                         m_w_branch_fox, m_w_branch_swa, m_w_out, m_ffn2_norm, m_ffn2_w_in, m_ffn2_w_out]))
    v = dict(zip(names, [v_meta_tokens, v_rel_bias_table, v_ffn1_norm, v_ffn1_w_in, v_ffn1_w_out, v_mix_norm, v_w_in,
                         v_forget_bias, v_fox_q_norm, v_fox_k_norm, v_swa_q_norm, v_swa_k_norm, v_swa_sinks,
                         v_w_branch_fox, v_w_branch_swa, v_w_out, v_ffn2_norm, v_ffn2_w_in, v_ffn2_w_out]))
    xi, yi, ci = lax.axis_index("x"), lax.axis_index("y"), lax.axis_index("c")
    shard = 2 * xi + yi
    seq = x.shape[1]
    t = seq + BLK

    wb = {nm: w[nm].astype(CDT) for nm, _, _ in SHARD_ITEMS}
    wb_list = [wb[nm] for nm, _, _ in SHARD_ITEMS]
    mflat = meta_tokens.reshape(META_ROWS, 128)
    first = (0, 1)
    *slabs_first, mall = _gather_layer([wb_list[t] for t in first], mflat, 0, "gather_weights", first)
    mall = lax.dynamic_update_slice(mall, mflat[None], (shard, 0, 0))
    meta_full = jnp.concatenate([mall[s].reshape(N_META, 256) for s in range(4)], axis=1)
    bias = _bias_fwd(rel_bias_table, "bias_fwd")

    def layer_weights(slabs, l, items=ALL_ITEMS):
        lw = _full_weights(slabs, wb, l, shard, items)
        if "w_in" in lw:
            lw["w_mix"] = _mix_cols(lw.pop("w_in"))
        return lw

    def layer_vectors(l):
        lw = {nm: w[nm][l].reshape(1, D) for nm in ("ffn1_norm", "mix_norm", "ffn2_norm")}
        lw["gfq"] = jnp.tile(fox_q_norm[l], 8).reshape(1, 512)
        lw["gfk"] = jnp.tile(fox_k_norm[l], 8).reshape(1, 512)
        lw["gsq"] = jnp.tile(swa_q_norm[l], 8).reshape(1, 512)
        lw["gsk"] = jnp.tile(swa_k_norm[l], 2).reshape(1, 128)
        lw["fb"] = jnp.pad(forget_bias[l], (0, 120)).reshape(1, 128)
        lw["sinks"] = swa_sinks[l]
        lw["bias"] = bias
        return lw

    def gather_ride(layer, items):
        return ("gather", [wb_list[t] for t in items], _slab_shapes(items), layer, items)

    landed = {}

    def need(lw, stage):
        if stage == "mixer":
            items = (2,)
            slabs = _forward_layer(landed["ffn_in_a"], "forward_halves0m", items)
        else:
            items = (3, 4, 5, 6, 7)
            slabs = _forward_layer(landed["ffn_out_a"] + landed["swa_fwd"] + landed["proj"], "forward_halves0g", items)
        lw.update(layer_weights(slabs, 0, items))

    late = {"rides": {"ffn_in_a": gather_ride(0, (2,)), "ffn_out_a": gather_ride(0, (3, 4, 5)),
                      "proj": gather_ride(0, (7,)), "swa_fwd": gather_ride(0, (6,))},
            "arrived": landed.__setitem__, "need": need}

    h = jnp.concatenate([jnp.zeros((PAD, D), F32), meta_full, x[0]], axis=0)
    lws = [{**layer_vectors(0), **layer_weights(slabs_first, 0, first)}]
    h, sv0, slabs1 = _layer_fwd(h, lws[0], 0, gather_ride(1, ALL_ITEMS), late)
    lws.append({**layer_vectors(1), **layer_weights(_forward_layer(slabs1, "forward_halves"), 1)})
    h, sv1, _ = _layer_fwd(h, lws[1], 1)
    saved = [sv0, sv1]
    dh, dhb, lacc = _loss(h, loss_target[0], "loss")
    loss = lax.psum(lacc[0, 0], ("x", "y", "c"))

    half_idx = ci.reshape(1).astype(jnp.int32)
    place_idx = jnp.stack([shard, ci]).astype(jnp.int32)

    def pair_sums(g, gsm, tag, items=ALL_ITEMS):
        if "w_mix" in g:
            g["w_in"] = _unmix_cols(g.pop("w_mix"))
        forms = _exchange_forms(g, items)
        got, slots = _swap_layer(forms, gsm, f"swap_halves{tag}", items)
        return {t: _pair_add_t(a, b, half_idx, SHARD_ITEMS[t][0], SHARD_ITEMS[t][1][0],
                               f"pair_add{tag}_{SHARD_ITEMS[t][0]}")
                for t, a, b in zip(items, forms, got)}, slots

    def scatter_ride(ps, items):
        return ("scatter", [ps[t] for t in items], _got3_shapes(items), None, items)

    early = (2, 3, 4, 5, 6, 7)
    early_rides = ((6,), (7,), (2, 5), (3, 4))
    ps0 = {}

    def before_ffn1(g):
        ps0.update(pair_sums(g, None, "0e", early)[0])
        return [scatter_ride(ps0, items) for items in early_rides]

    grads = [None, None]
    dh, dhb, grads[1] = _layer_bwd(dh, dhb, saved[1], lws[1], 1)
    ps1, _ = pair_sums(grads[1], None, 1)
    dh, dhb, grads[0] = _layer_bwd(dh, dhb, saved[0], lws[0], 0, scatter_ride(ps1, ALL_ITEMS), before_ffn1)
    grad_x = dh[BLK:].reshape(1, seq, D)
    dtab = _bias_bwd(grads[0]["dbias"] + grads[1]["dbias"], "bias_bwd")

    small = [dh[PAD:BLK].reshape(128, 128), _rows128(dtab[:, :N_BUCKETS].T, 2)]
    for nm in ("ffn1_norm", "mix_norm", "ffn2_norm"):
        small.append(jnp.stack([grads[l][nm][0] for l in range(2)]).reshape(16, 128))
    small.append(_rows128(jnp.stack([grads[l]["dgn"][4, :8] for l in range(2)]), 1))
    for row in range(4):
        small.append(jnp.stack([grads[l]["dgn"][row, :HD] for l in range(2)]).reshape(1, 128))
    dsk = [grads[l]["dsk"][:, 0, :] for l in range(2)]
    small.append(_rows128(jnp.stack([jnp.stack([d[:, 0], d[:, HD]], axis=1).reshape(8) for d in dsk]), 1))
    gsm = jnp.concatenate(small, axis=0)
    gsm = jnp.pad(gsm, ((0, SMALL_ROWS - gsm.shape[0]), (0, 0)))

    late = (0, 1)
    ps_late, slots = pair_sums(grads[0], gsm, "0l", late)
    ps0.update(ps_late)
    got3_0 = dict(zip([t for items in early_rides for t in items], grads[0]["rode_ffn1"]))
    got3_0.update(zip(late, _scatter_layer([ps0[t] for t in late], "scatter_shards", late)))
    got3 = [got3_0, dict(zip(ALL_ITEMS, grads[0]["rode"]))]
    bufs = []
    for t, (nm, (r, c), _) in enumerate(SHARD_ITEMS):
        buf = lax.empty((2, r, c), F32)
        for l, ps in ((1, ps1), (0, ps0)):
            buf = _sum4_t(ps[t], got3[l][t], buf, place_idx, l, nm, r, f"sum4_{l}_{nm}")
        bufs.append(buf)
    bufs = _join_layer(bufs, "join_halves")
    gs = _sum8(slots, "sum8")

    g_out = {nm: buf for (nm, _, _), buf in zip(SHARD_ITEMS, bufs)}
    g_out["meta_tokens"] = lax.dynamic_slice(gs[0:128].reshape(N_META, D), (0, shard * 256), (N_META, 256))
    off = 128
    for nm, rows in SMALL_ITEMS:
        n = w[nm].size
        g_out[nm] = gs[off:off + rows].reshape(-1)[:n].reshape(w[nm].shape)
        off += rows

    delta, new_m, new_v = {}, {}, {}
    for nm, _, _ in SHARD_ITEMS:
        if nm == "w_in":
            tr_ = lambda a: jnp.swapaxes(a, 1, 2)
            delta[nm], new_m[nm], new_v[nm] = (tr_(a) for a in _adamw3(tr_(w[nm]), tr_(g_out[nm]), tr_(m[nm]), tr_(v[nm]),
                                                                        f"adamw_{nm}"))
        else:
            delta[nm], new_m[nm], new_v[nm] = _adamw3(w[nm], g_out[nm], m[nm], v[nm], f"adamw_{nm}")
    small_names = ["meta_tokens"] + [nm for nm, _ in SMALL_ITEMS]
    small_rows = [META_ROWS] + [rows for _, rows in SMALL_ITEMS]

    def pack_small(src):
        buf = jnp.concatenate([_rows128(src[nm], rows) for nm, rows in zip(small_names, small_rows)], axis=0)
        return jnp.pad(buf, ((0, SMALL_ADAM_ROWS - buf.shape[0]), (0, 0)))

    d_, m_, v_ = (a[0] for a in _adamw3(pack_small(w)[None], pack_small(g_out)[None], pack_small(m)[None],
                                        pack_small(v)[None], "adamw_small"))
    off = 0
    for nm, rows in zip(small_names, small_rows):
        n = w[nm].size
        for dst, src in ((delta, d_), (new_m, m_), (new_v, v_)):
            dst[nm] = src[off:off + rows].reshape(-1)[:n].reshape(w[nm].shape)
        off += rows

    return (loss, grad_x, *[g_out[n] for n in names], *[delta[n] for n in names],
            *[new_m[n] for n in names], *[new_v[n] for n in names])
```

```python
import math

import numpy as np
import jax
import jax.numpy as jnp
from jax import lax
from jax.experimental import pallas as pl
from jax.experimental.pallas import tpu as pltpu

D = 1024
F = 2816
FT = F // 2
HD = 64
NPAIR = 4
N_META = 16
BLK = 128
PAD = BLK - N_META
EPS = 1e-6
NEG = -1e30
N_BUCKETS = 32
GA, GB, QA, KA, VA, QB, KB, VB, FA, DP = 0, 1024, 2048, 2560, 3072, 3584, 4096, 4224, 4352, 4480
D_IN = 4360
CDT = jnp.bfloat16
F32 = jnp.float32
VMEM_LIMIT = 48 * 1024 * 1024
MESH_ID = pl.DeviceIdType.MESH

ADAM_LR, ADAM_B1, ADAM_B2, ADAM_EPS, ADAM_WD, ADAM_STEP = 0.001, 0.9, 0.999, 1e-08, 0.01, 10

SHARD_ITEMS = (
    ("ffn1_w_in", (1024, 1408), "col"),
    ("ffn1_w_out", (704, 1024), "row"),
    ("w_in", (1024, 1090), "col"),
    ("w_branch_fox", (512, 256), "col"),
    ("w_branch_swa", (512, 256), "col"),
    ("w_out", (256, 1024), "row"),
    ("ffn2_w_in", (1024, 1408), "col"),
    ("ffn2_w_out", (704, 1024), "row"),
)
SMALL_ROWS = 192
META_ROWS = 32


def _row_tile(t):
    return 384 if t % 384 == 0 else 128


def _dot(a, b):
    return jnp.dot(a, b, preferred_element_type=F32)


def _dot_nt(a, b):
    return lax.dot_general(a, b, (((1,), (1,)), ((), ())), preferred_element_type=F32)


def _dot_hi(a, b):
    return jnp.dot(a, b, preferred_element_type=F32, precision=lax.Precision.HIGHEST)


def _sigmoid(x):
    return 0.5 * jnp.tanh(0.5 * x) + 0.5


def _iota(shape, dim):
    return lax.broadcasted_iota(jnp.int32, shape, dim)


def _params(sem=None):
    return pltpu.CompilerParams(dimension_semantics=sem, vmem_limit_bytes=VMEM_LIMIT)


def _sds(shape, dtype):
    return jax.ShapeDtypeStruct(shape, dtype)


def _rms_fwd(h, g, name):
    t = h.shape[0]
    tm = _row_tile(t)

    def body(h_ref, g_ref, a_ref, at_ref):
        x = h_ref[...]
        ms = jnp.mean(x * x, axis=-1, keepdims=True)
        a = x * lax.rsqrt(ms + EPS) * g_ref[...]
        a_ref[...] = a.astype(CDT)
        at_ref[...] = a.T.astype(CDT)

    return pl.pallas_call(
        body, name=name, grid=(t // tm,),
        in_specs=[pl.BlockSpec((tm, D), lambda i: (i, 0)), pl.BlockSpec((1, D), lambda i: (0, 0))],
        out_specs=[pl.BlockSpec((tm, D), lambda i: (i, 0)), pl.BlockSpec((D, tm), lambda i: (0, i))],
        out_shape=[_sds((t, D), CDT), _sds((D, t), CDT)],
        compiler_params=_params(("parallel",)),
    )(h, g)


def _ffn_in(a, w_in, name, ride=None):
    t = a.shape[0]
    tm = _row_tile(t)
    tn = FT
    nj = F // tn
    grid = (nj, t // tm)
    ride_in, ride_in_specs, ride_out, ride_out_specs, ride_sems = _ride_specs(ride)

    def body(a_ref, wg_ref, wu_ref, gu_ref, s_ref, st_ref):
        a_ = a_ref[...]
        g = _dot(a_, wg_ref[...])
        u = _dot(a_, wu_ref[...])
        s = g * _sigmoid(g) * u
        gu_ref[0] = g.astype(CDT)
        gu_ref[1] = u.astype(CDT)
        s_ref[...] = s.astype(CDT)
        st_ref[...] = s.T.astype(CDT)

    res = pl.pallas_call(
        _riding(body, 3, 3, ride, grid), name=name, grid=grid,
        in_specs=[pl.BlockSpec((tm, D), lambda j, i: (i, 0)),
                  pl.BlockSpec((D, tn), lambda j, i: (0, j)),
                  pl.BlockSpec((D, tn), lambda j, i: (0, j + nj))] + ride_in_specs,
        out_specs=[pl.BlockSpec((2, tm, tn), lambda j, i: (0, i, j)),
                   pl.BlockSpec((tm, tn), lambda j, i: (i, j)),
                   pl.BlockSpec((tn, tm), lambda j, i: (j, i))] + ride_out_specs,
        out_shape=[_sds((2, t, F), CDT), _sds((t, F), CDT), _sds((F, t), CDT)] + ride_out, scratch_shapes=ride_sems,
        compiler_params=_params(("arbitrary", "arbitrary") if ride else ("parallel", "parallel")),
    )(a, w_in, w_in, *ride_in)
    return (*res[:3], res[3:]) if ride else res


def _mm_res(a, b, res, scale, name, ride=None):
    t, k = a.shape
    n = b.shape[1]
    tm = _row_tile(t)
    tn = n
    grid = (t // tm, n // tn)
    ride_in, ride_in_specs, ride_out, ride_out_specs, ride_sems = _ride_specs(ride)

    def body(a_ref, b_ref, r_ref, o_ref):
        o_ref[...] = r_ref[...] + scale * _dot(a_ref[...], b_ref[...])

    out = pl.pallas_call(
        _riding(body, 3, 1, ride, grid), name=name, grid=grid,
        in_specs=[pl.BlockSpec((tm, k), lambda i, j: (i, 0)),
                  pl.BlockSpec((k, tn), lambda i, j: (0, j)),
                  pl.BlockSpec((tm, tn), lambda i, j: (i, j))] + ride_in_specs,
        out_specs=[pl.BlockSpec((tm, tn), lambda i, j: (i, j))] + ride_out_specs,
        out_shape=[_sds((t, n), F32)] + ride_out, scratch_shapes=ride_sems,
        compiler_params=_params(("arbitrary", "arbitrary") if ride else ("parallel", "parallel")),
    )(a, b, res, *ride_in)
    return (out[0], out[1:]) if ride else out[0]


def _mm(a, b, out_dtype, tm, tn, name, scale=1.0, ride=None):
    m, k = a.shape
    if b.ndim == 3:
        nh = b.shape[2] // tn
        n = 2 * b.shape[2]
        b_spec = pl.BlockSpec((None, k, tn), lambda i, j: (j // nh, 0, j % nh))
    else:
        n = b.shape[1]
        b_spec = pl.BlockSpec((k, tn), lambda i, j: (0, j))
    grid = (m // tm, n // tn)
    ride_in, ride_in_specs, ride_out, ride_out_specs, ride_sems = _ride_specs(ride)

    def body(a_ref, b_ref, o_ref):
        o_ref[...] = (scale * _dot(a_ref[...], b_ref[...])).astype(out_dtype)

    res = pl.pallas_call(
        _riding(body, 2, 1, ride, grid), name=name, grid=grid,
        in_specs=[pl.BlockSpec((tm, k), lambda i, j: (i, 0)), b_spec] + ride_in_specs,
        out_specs=[pl.BlockSpec((tm, tn), lambda i, j: (i, j))] + ride_out_specs,
        out_shape=[_sds((m, n), out_dtype)] + ride_out, scratch_shapes=ride_sems,
        compiler_params=_params(("arbitrary", "arbitrary") if ride else ("parallel", "parallel")),
    )(a, b, *ride_in)
    return (res[0], res[1:]) if ride else res[0]


def _mm_nt(a, b, name, with_t=False):
    m, n = a.shape
    k = b.shape[0]
    tm = _row_tile(m)
    tk = k

    def body(a_ref, b_ref, o_ref, *t_ref):
        r = _dot_nt(a_ref[...], b_ref[...])
        o_ref[...] = r
        if with_t:
            t_ref[0][...] = r.T.astype(CDT)

    out_specs = [pl.BlockSpec((tm, tk), lambda i, j: (i, j))]
    out_shape = [_sds((m, k), F32)]
    if with_t:
        out_specs.append(pl.BlockSpec((tk, tm), lambda i, j: (j, i)))
        out_shape.append(_sds((k, m), CDT))
    res = pl.pallas_call(
        body, name=name, grid=(m // tm, k // tk),
        in_specs=[pl.BlockSpec((tm, n), lambda i, j: (i, 0)), pl.BlockSpec((tk, n), lambda i, j: (j, 0))],
        out_specs=out_specs, out_shape=out_shape,
        compiler_params=_params(("parallel", "parallel")),
    )(a, b)
    return res if with_t else res[0]


def _ffn_bwd_mid(dhb, w_out, gu, name, ride=None):
    t = dhb.shape[0]
    tm = _row_tile(t)
    tn = FT
    grid = (F // tn, t // tm)
    ride_in, ride_in_specs, ride_out, ride_out_specs, ride_sems = _ride_specs(ride)

    def body(dh_ref, w_ref, gu_ref, o_ref):
        ds = _dot_nt(dh_ref[...] * 0.5, w_ref[...])
        g = gu_ref[0].astype(F32)
        u = gu_ref[1].astype(F32)
        sg = _sigmoid(g)
        o_ref[0] = (ds * u * (sg * (1.0 + g * (1.0 - sg)))).astype(CDT)
        o_ref[1] = (ds * (g * sg)).astype(CDT)

    res = pl.pallas_call(
        _riding(body, 3, 1, ride, grid), name=name, grid=grid,
        in_specs=[pl.BlockSpec((tm, D), lambda j, i: (i, 0)),
                  pl.BlockSpec((tn, D), lambda j, i: (j, 0)),
                  pl.BlockSpec((2, tm, tn), lambda j, i: (0, i, j))] + ride_in_specs,
        out_specs=[pl.BlockSpec((2, tm, tn), lambda j, i: (0, i, j))] + ride_out_specs,
        out_shape=[_sds((2, t, F), CDT)] + ride_out, scratch_shapes=ride_sems,
        compiler_params=_params(("arbitrary", "arbitrary") if ride else ("parallel", "parallel")),
    )(dhb, w_out, gu, *ride_in)
    return (res[0], res[1:]) if ride else res[0]


def _rms_bwd_rows(da_, x, g, dres, i, dh_ref, dhb_ref, dg_ref):
    r = lax.rsqrt(jnp.mean(x * x, axis=-1, keepdims=True) + EPS)
    xh = x * r
    day = da_ * g
    dh = dres + r * (day - xh * jnp.mean(day * xh, axis=-1, keepdims=True))
    dh_ref[...] = dh
    dhb_ref[...] = dh.astype(CDT)

    @pl.when(i == 0)
    def _():
        dg_ref[...] = jnp.zeros(dg_ref.shape, F32)

    dg_ref[0:1, :] += jnp.sum(da_ * xh, axis=0, keepdims=True)


def _ffn_bwd_in(dgu, w_in, h, g, dres, name, ride=None):
    t = dgu.shape[1]
    tm = _row_tile(t)
    grid = (t // tm,)
    ride_in, ride_in_specs, ride_out, ride_out_specs, ride_sems = _ride_specs(ride)

    def body(dg_ref, wg_ref, wu_ref, h_ref, g_ref, dr_ref, dh_ref, dhb_ref, dgn_ref):
        da_ = _dot_nt(dg_ref[0], wg_ref[...]) + _dot_nt(dg_ref[1], wu_ref[...])
        _rms_bwd_rows(da_, h_ref[...], g_ref[...], dr_ref[...], pl.program_id(0), dh_ref, dhb_ref, dgn_ref)

    row = pl.BlockSpec((tm, D), lambda i: (i, 0))
    res = pl.pallas_call(
        _riding(body, 6, 3, ride, grid), name=name, grid=grid,
        in_specs=[pl.BlockSpec((2, tm, F), lambda i: (0, i, 0)),
                  pl.BlockSpec((D, F), lambda i: (0, 0)),
                  pl.BlockSpec((D, F), lambda i: (0, 1)),
                  row, pl.BlockSpec((1, D), lambda i: (0, 0)), row] + ride_in_specs,
        out_specs=[row, row, pl.BlockSpec((8, D), lambda i: (0, 0))] + ride_out_specs,
        out_shape=[_sds((t, D), F32), _sds((t, D), CDT), _sds((8, D), F32)] + ride_out, scratch_shapes=ride_sems,
        compiler_params=_params(("arbitrary",)),
    )(dgu, w_in, w_in, h, g, dres, *ride_in)
    return (*res[:3], res[3:]) if ride else res


def _mm_nt_rms(a, b, h, g, dres, name):
    t, n = a.shape
    tm = _row_tile(t)

    def body(a_ref, b_ref, h_ref, g_ref, dr_ref, dh_ref, dhb_ref, dgn_ref):
        da_ = _dot_nt(a_ref[...], b_ref[...])
        _rms_bwd_rows(da_, h_ref[...], g_ref[...], dr_ref[...], pl.program_id(0), dh_ref, dhb_ref, dgn_ref)

    row = pl.BlockSpec((tm, D), lambda i: (i, 0))
    return pl.pallas_call(
        body, name=name, grid=(t // tm,),
        in_specs=[pl.BlockSpec((tm, n), lambda i: (i, 0)), pl.BlockSpec((D, n), lambda i: (0, 0)),
                  row, pl.BlockSpec((1, D), lambda i: (0, 0)), row],
        out_specs=[row, row, pl.BlockSpec((8, D), lambda i: (0, 0))],
        out_shape=[_sds((t, D), F32), _sds((t, D), CDT), _sds((8, D), F32)],
        compiler_params=_params(("arbitrary",)),
    )(a, b, h, g, dres)


def _loss(h, target, name):
    t = h.shape[0]

    def body(h_ref, t_ref, dh_ref, dhb_ref, l_ref):
        i = pl.program_id(0)

        @pl.when(i == 0)
        def _():
            l_ref[...] = jnp.zeros(l_ref.shape, F32)
            dh_ref[...] = jnp.zeros(dh_ref.shape, F32)
            dhb_ref[...] = jnp.zeros(dhb_ref.shape, CDT)

        @pl.when(i > 0)
        def _():
            err = h_ref[...] - t_ref[...]
            l_ref[...] += (0.5 / D) * jnp.sum(err * err)
            d = err * (1.0 / D)
            dh_ref[...] = d
            dhb_ref[...] = d.astype(CDT)

    row = pl.BlockSpec((BLK, D), lambda i: (i, 0))
    return pl.pallas_call(
        body, name=name, grid=(t // BLK,),
        in_specs=[row, pl.BlockSpec((BLK, D), lambda i: (jnp.maximum(i - 1, 0), 0))],
        out_specs=[row, row, pl.BlockSpec((8, 128), lambda i: (0, 0))],
        out_shape=[_sds((t, D), F32), _sds((t, D), CDT), _sds((8, 128), F32)],
        compiler_params=_params(("arbitrary",)),
    )(h, target)


def _block_diag():
    return (_iota((128, 128), 0) // HD == _iota((128, 128), 1) // HD).astype(F32)


def _head_sums(v, bd):
    hi = v.astype(CDT)
    rest = (v - hi.astype(F32)).astype(CDT)
    b = bd.astype(CDT)
    return _dot(hi, b) + _dot(rest, b)


def _dup_halves(x, lo):
    sw = pltpu.roll(x, 64, 1)
    return jnp.where(lo, x, sw), jnp.where(lo, sw, x)


def _qknorm_fwd(proj, gfq, gfk, gsq, gsk, fb, name):
    t = proj.shape[0]
    tm = _row_tile(t)

    def body(qa, ka, va, qb, kb, vb, fa, gfq_r, gfk_r, gsq_r, gsk_r, fb_r,
             qf_o, kf_o, vf_o, qs_o, kse_o, vse_o, c_o, ct_o, qft_o, carry):
        i = pl.program_id(0)
        bd = _block_diag()
        lane = _iota((1, 128), 1)
        lo = lane < HD

        def hnorm(x, g):
            ms = _head_sums(x * x, bd) * (1.0 / HD)
            return x * lax.rsqrt(ms + EPS) * g

        for ch in range(4):
            sl = slice(128 * ch, 128 * (ch + 1))
            qn = hnorm(qa[:, sl], gfq_r[:, sl]) * 0.125
            qf_o[:, sl] = qn.astype(CDT)
            qft_o[sl, :] = qn.T.astype(CDT)
            kf_o[:, sl] = hnorm(ka[:, sl], gfk_r[:, sl]).astype(CDT)
            qs_o[:, sl] = (hnorm(qb[:, sl], gsq_r[:, sl]) * 0.125).astype(CDT)
        vf_o[...] = va[...].astype(CDT)
        k0, k1 = _dup_halves(hnorm(kb[...], gsk_r[...]), lo)
        kse_o[0] = k0.astype(CDT)
        kse_o[1] = k1.astype(CDT)
        v0, v1 = _dup_halves(vb[...], lo)
        vse_o[0] = v0.astype(CDT)
        vse_o[1] = v1.astype(CDT)

        z = fa[...] + fb_r[...]
        lf = jnp.minimum(z, 0.0) - jnp.log(1.0 + jnp.exp(-jnp.abs(z)))
        lf = jnp.where(lane < 8, lf, 0.0)
        ltri = (_iota((tm, tm), 1) <= _iota((tm, tm), 0)).astype(F32)

        @pl.when(i == 0)
        def _():
            carry[...] = jnp.zeros(carry.shape, F32)

        c = _dot_hi(ltri, lf) + carry[0:1, :]
        carry[0:1, :] = c[tm - 1:tm, :]
        c_o[...] = c
        ct_o[...] = c.T[0:8, :]

    def col(width, off):
        return pl.BlockSpec((tm, width), lambda i: (i, off // width))

    def vec(width):
        return pl.BlockSpec((1, width), lambda i: (0, 0))

    return pl.pallas_call(
        body, name=name, grid=(t // tm,),
        in_specs=[col(512, QA), col(512, KA), col(512, VA), col(512, QB), col(128, KB), col(128, VB), col(128, FA),
                  vec(512), vec(512), vec(512), vec(128), vec(128)],
        out_specs=[pl.BlockSpec((tm, 512), lambda i: (i, 0))] * 4
        + [pl.BlockSpec((2, tm, 128), lambda i: (0, i, 0))] * 2
        + [pl.BlockSpec((tm, 128), lambda i: (i, 0)), pl.BlockSpec((8, tm), lambda i: (0, i)),
           pl.BlockSpec((512, tm), lambda i: (0, i))],
        out_shape=[_sds((t, 512), CDT)] * 4 + [_sds((2, t, 128), CDT)] * 2
        + [_sds((t, 128), F32), _sds((8, t), F32), _sds((512, t), CDT)],
        scratch_shapes=[pltpu.VMEM((8, 128), F32)],
        compiler_params=_params(("arbitrary",)),
    )(proj, proj, proj, proj, proj, proj, proj, gfq, gfk, gsq, gsk, fb)


def _qknorm_bwd(proj, dqf, dkf, dvf, dqs, dkse, dvse, dcq, dck, dga, dgb, gfq, gfk, gsq, gsk, fb, name):
    t = proj.shape[0]
    tm = _row_tile(t)
    nt = t // tm

    def body(qa, ka, qb, kb, fa, dqf_r, dkf_r, dvf_r, dqs_r, dkse_r, dvse_r, dcq_r, dck_r, dga_r, dgb_r,
             gfq_r, gfk_r, gsq_r, gsk_r, fb_r, dp_o, dgn_o, carry, acc):
        i = pl.program_id(0)
        bd = _block_diag()
        lane = _iota((1, 128), 1)
        lo = lane < HD

        @pl.when(i == 0)
        def _():
            carry[...] = jnp.zeros(carry.shape, F32)
            acc[...] = jnp.zeros(acc.shape, F32)

        def hnorm_bwd(x, g, dy):
            r = lax.rsqrt(_head_sums(x * x, bd) * (1.0 / HD) + EPS)
            xh = x * r
            day = dy * g
            dx = r * (day - xh * (_head_sums(day * xh, bd) * (1.0 / HD)))
            return dx, jnp.sum(dy * xh, axis=0, keepdims=True)

        for ch in range(4):
            sl = slice(128 * ch, 128 * (ch + 1))
            dx, dg = hnorm_bwd(qa[:, sl], gfq_r[:, sl], dqf_r[:, sl] * 0.125)
            dp_o[:, QA + 128 * ch:QA + 128 * (ch + 1)] = dx.astype(CDT)
            acc[0:1, sl] += dg
            dx, dg = hnorm_bwd(ka[:, sl], gfk_r[:, sl], dkf_r[:, sl])
            dp_o[:, KA + 128 * ch:KA + 128 * (ch + 1)] = dx.astype(CDT)
            acc[1:2, sl] += dg
            dx, dg = hnorm_bwd(qb[:, sl], gsq_r[:, sl], dqs_r[:, sl] * 0.125)
            dp_o[:, QB + 128 * ch:QB + 128 * (ch + 1)] = dx.astype(CDT)
            acc[2:3, sl] += dg
        dp_o[:, VA:VA + 512] = dvf_r[...].astype(CDT)
        dp_o[:, GA:GA + D] = dga_r[...]
        dp_o[:, GB:GB + D] = dgb_r[...]

        def fold(x):
            e0 = x[0]
            e1 = x[1]
            return jnp.where(lo, e0 + pltpu.roll(e0, 64, 1), e1 + pltpu.roll(e1, 64, 1))

        dx, dg = hnorm_bwd(kb[...], gsk_r[...], fold(dkse_r))
        dp_o[:, KB:KB + 128] = dx.astype(CDT)
        acc[3:4, 0:128] += dg
        dp_o[:, VB:VB + 128] = fold(dvse_r).astype(CDT)

        rr = _iota((512, 128), 0)
        hh = _iota((512, 128), 1)
        sel = ((rr == (hh >> 1) * 128 + (hh & 1) * HD) & (hh < 8)).astype(F32)
        dcs = _dot_hi(dcq_r[...] - dck_r[...], sel)
        utri = (_iota((tm, tm), 1) >= _iota((tm, tm), 0)).astype(F32)
        dlf = _dot_hi(utri, dcs) + carry[0:1, :]
        carry[0:1, :] = dlf[0:1, :]
        z = fa[...] + fb_r[...]
        dfa = jnp.where(lane < 8, dlf * _sigmoid(-z), 0.0)
        dp_o[:, FA:FA + 128] = dfa.astype(CDT)
        acc[4:5, 0:128] += jnp.sum(dfa, axis=0, keepdims=True)

        @pl.when(i == nt - 1)
        def _():
            foldm = ((_iota((512, 128), 0) & (HD - 1)) == _iota((512, 128), 1)).astype(F32)
            dgn_o[...] = _dot_hi(acc[...], foldm)

    def col(width, off):
        return pl.BlockSpec((tm, width), lambda i: (nt - 1 - i, off // width))

    def rows(width):
        return pl.BlockSpec((tm, width), lambda i: (nt - 1 - i, 0))

    def vec(width):
        return pl.BlockSpec((1, width), lambda i: (0, 0))

    pair = pl.BlockSpec((2, tm, 128), lambda i: (0, nt - 1 - i, 0))
    return pl.pallas_call(
        body, name=name, grid=(nt,),
        in_specs=[col(512, QA), col(512, KA), col(512, QB), col(128, KB), col(128, FA),
                  rows(512), rows(512), rows(512), rows(512), pair, pair, rows(512), rows(512), rows(D), rows(D),
                  vec(512), vec(512), vec(512), vec(128), vec(128)],
        out_specs=[rows(DP), pl.BlockSpec((8, 128), lambda i: (0, 0))],
        out_shape=[_sds((t, DP), CDT), _sds((8, 128), F32)],
        scratch_shapes=[pltpu.VMEM((8, 128), F32), pltpu.VMEM((8, 512), F32)],
        compiler_params=_params(("arbitrary",)),
    )(proj, proj, proj, proj, proj, dqf, dkf, dvf, dqs, dkse, dvse, dcq, dck, dga, dgb, gfq, gfk, gsq, gsk, fb)


def _gate_out_fwd(ofox, oswa, wbf, wbs, proj, w_out, h, name):
    t = ofox.shape[0]
    tm = _row_tile(t)

    def body(of_r, os_r, wf_r, ws_r, ga_r, gb_r, wo_r, h_r, ho_o, yt_o, pf_o, ps_o, oft_o, ost_o):
        pf = _dot(of_r[...], wf_r[...])
        ps = _dot(os_r[...], ws_r[...])
        y = _sigmoid(ga_r[...]) * pf + _sigmoid(gb_r[...]) * ps
        ho_o[...] = h_r[...] + _dot(y.astype(CDT), wo_r[...])
        yt_o[...] = y.T.astype(CDT)
        pf_o[...] = pf.astype(CDT)
        ps_o[...] = ps.astype(CDT)
        oft_o[...] = of_r[...].astype(F32).T.astype(CDT)
        ost_o[...] = os_r[...].astype(F32).T.astype(CDT)

    row = pl.BlockSpec((tm, D), lambda i: (i, 0))
    half = pl.BlockSpec((tm, 512), lambda i: (i, 0))
    whole = lambda r: pl.BlockSpec((r, D), lambda i: (0, 0))
    tcol = lambda r: pl.BlockSpec((r, tm), lambda i: (0, i))
    return pl.pallas_call(
        body, name=name, grid=(t // tm,),
        in_specs=[half, half, whole(512), whole(512),
                  pl.BlockSpec((tm, D), lambda i: (i, GA // D)), pl.BlockSpec((tm, D), lambda i: (i, GB // D)),
                  whole(D), row],
        out_specs=[row, tcol(D), row, row, tcol(512), tcol(512)],
        out_shape=[_sds((t, D), F32), _sds((D, t), CDT), _sds((t, D), CDT), _sds((t, D), CDT),
                   _sds((512, t), CDT), _sds((512, t), CDT)],
        compiler_params=_params(("parallel",)),
    )(ofox, oswa, wbf, wbs, proj, proj, w_out, h)


def _gate_out_bwd(dhb, w_out, pf, ps, proj, name):
    t = dhb.shape[0]
    tm = _row_tile(t)

    def body(dh_r, wo_r, pf_r, ps_r, ga_r, gb_r, dpf_o, dps_o, dga_o, dgb_o):
        dy_ = _dot_nt(dh_r[...], wo_r[...])
        sa = _sigmoid(ga_r[...])
        sb = _sigmoid(gb_r[...])
        dpf_o[...] = (dy_ * sa).astype(CDT)
        dps_o[...] = (dy_ * sb).astype(CDT)
        dga_o[...] = (dy_ * pf_r[...].astype(F32) * (sa * (1.0 - sa))).astype(CDT)
        dgb_o[...] = (dy_ * ps_r[...].astype(F32) * (sb * (1.0 - sb))).astype(CDT)

    row = pl.BlockSpec((tm, D), lambda i: (i, 0))
    return pl.pallas_call(
        body, name=name, grid=(t // tm,),
        in_specs=[row, pl.BlockSpec((D, D), lambda i: (0, 0)), row, row,
                  pl.BlockSpec((tm, D), lambda i: (i, GA // D)), pl.BlockSpec((tm, D), lambda i: (i, GB // D))],
        out_specs=[row] * 4,
        out_shape=[_sds((t, D), CDT)] * 4,
        compiler_params=_params(("parallel",)),
    )(dhb, w_out, pf, ps, proj, proj)


def _tri_steps(n, by_key):
    if by_key:
        pairs = [(i, j) for j in range(n) for i in range(j, n)]
    else:
        pairs = [(i, j) for i in range(n) for j in range(i + 1)]
    return (np.array([p[0] for p in pairs], np.int32), np.array([p[1] for p in pairs], np.int32))


def _head_col(blk, lane, h):
    return jnp.sum(jnp.where(lane == h, blk, 0.0), axis=1, keepdims=True)


def _head_row(blk, sub, h):
    return jnp.sum(jnp.where(sub == h, blk, 0.0), axis=0, keepdims=True)


def _ride_specs(ride):
    if ride is None:
        return [], [], [], [], []
    kind, srcs, outs, layer, items = ride
    return list(srcs), [ANY] * len(srcs), list(outs), [ANY] * len(outs), _dma_sems(3 * len(srcs))


def _ride_start(ride, srcs, dsts, send_sems, recv_sems):
    for cp in _ici_copies(ride[0], srcs, dsts, send_sems, recv_sems, ride[3], recv=False, items=ride[4])[0]:
        cp.start()


def _ride_wait(ride, srcs, dsts, send_sems, recv_sems):
    sends, recvs = _ici_copies(ride[0], srcs, dsts, send_sems, recv_sems, ride[3], items=ride[4])
    for cp in recvs:
        cp.wait_recv()
    for cp in sends:
        cp.wait_send()


def _riding(body, n_in, n_out, ride, grid):
    if ride is None:
        return body
    nr = len(ride[1])

    def wrapped(*refs):
        ins, srcs = refs[:n_in], refs[n_in:n_in + nr]
        outs, dsts = refs[n_in + nr:n_in + nr + n_out], refs[n_in + nr + n_out:n_in + 2 * nr + n_out]
        scratch, sems = refs[n_in + 2 * nr + n_out:-2], refs[-2:]
        first = pl.program_id(0) == 0
        last = pl.program_id(0) == grid[0] - 1
        for a in range(1, len(grid)):
            first = first & (pl.program_id(a) == 0)
            last = last & (pl.program_id(a) == grid[a] - 1)

        @pl.when(first)
        def _():
            _ride_start(ride, srcs, dsts, *sems)

        body(*ins, *outs, *scratch)

        @pl.when(last)
        def _():
            _ride_wait(ride, srcs, dsts, *sems)

    return wrapped


def _fox_fwd(qf, kf, vf, c, ct, name, ride=None):
    t = qf.shape[0]
    ta = _row_tile(t)
    qi, kj = _tri_steps(t // ta, by_key=False)
    nsteps = len(qi)
    ride_in, ride_in_specs, ride_out, ride_out_specs, ride_sems = _ride_specs(ride)

    def body(qi_r, kj_r, q_r, k_r, v_r, c_r, ct_r, *rest):
        nr = len(ride_in)
        src_r, (o_o, lse_o), dst_o = rest[:nr], rest[nr:nr + 2], rest[nr + 2:2 * nr + 2]
        m_sc, l_sc, acc_sc, cq_sc, *sems = rest[2 * nr + 2:]
        p = pl.program_id(0)
        n = pl.program_id(1)
        i = qi_r[n]
        j = kj_r[n]
        lane = _iota((1, 128), 1)
        lo = lane < HD

        if ride is not None:
            @pl.when((p == 0) & (n == 0))
            def _():
                _ride_start(ride, src_r, dst_o, *sems)

        @pl.when(j == 0)
        def _():
            m_sc[...] = jnp.full(m_sc.shape, NEG, F32)
            l_sc[...] = jnp.zeros(l_sc.shape, F32)
            acc_sc[...] = jnp.zeros(acc_sc.shape, F32)
            for e in (0, 1):
                cq_sc[e] = jnp.broadcast_to(_head_col(c_r[...], lane, 2 * p + e), (ta, 128))

        def step(masked):
            q = q_r[...]
            k = k_r[...]
            vaug = jnp.concatenate([v_r[...], jnp.ones((ta, 128), CDT)], axis=1)
            if masked:
                rows = i * ta + _iota((ta, 1), 0)
                cols = j * ta + _iota((1, ta), 1)
                mask = (cols <= rows) & (cols >= PAD)
            sub = _iota((8, 1), 0)
            alphas, pvs = [], []
            for e in (0, 1):
                sel = lo if e == 0 else jnp.logical_not(lo)
                s = _dot_nt(jnp.where(sel, q, 0), k)
                ck = _head_row(ct_r[...], sub, 2 * p + e)
                cq = cq_sc[e]
                chunks = []
                for ch in range(ta // 128):
                    sl = slice(128 * ch, 128 * (ch + 1))
                    sc = s[:, sl] + cq - ck[:, sl]
                    if masked:
                        sc = jnp.where(mask[:, sl], sc, NEG)
                    chunks.append(sc)
                mx = chunks[0]
                for sc in chunks[1:]:
                    mx = jnp.maximum(mx, sc)
                m_prev = m_sc[e]
                m_new = jnp.maximum(m_prev, jnp.max(mx, axis=1, keepdims=True))
                alpha = jnp.exp(m_prev - m_new)
                pe = jnp.concatenate([jnp.exp(sc - m_new).astype(CDT) for sc in chunks], axis=1)
                pva = _dot(pe, vaug)
                l_sc[e] = alpha * l_sc[e] + pva[:, 128:]
                m_sc[e] = m_new
                alphas.append(alpha)
                pvs.append(pva[:, :128])
            acc_sc[...] = acc_sc[...] * jnp.where(lo, alphas[0], alphas[1]) + jnp.where(lo, pvs[0], pvs[1])

        edge = (j == i) | (j == 0)

        @pl.when(edge)
        def _():
            step(True)

        @pl.when(jnp.logical_not(edge))
        def _():
            step(False)

        @pl.when(j == i)
        def _():
            l = jnp.where(lo, l_sc[0], l_sc[1])
            o_o[...] = (acc_sc[...] / l).astype(CDT)
            lse_o[...] = jnp.where(lo, m_sc[0], m_sc[1]) + jnp.log(l)

        if ride is not None:
            @pl.when((p == NPAIR - 1) & (n == nsteps - 1))
            def _():
                _ride_wait(ride, src_r, dst_o, *sems)

    qblk = pl.BlockSpec((ta, 128), lambda p, n, qi_r, kj_r: (qi_r[n], p))
    kblk = pl.BlockSpec((ta, 128), lambda p, n, qi_r, kj_r: (kj_r[n], p))
    grid_spec = pltpu.PrefetchScalarGridSpec(
        num_scalar_prefetch=2, grid=(NPAIR, nsteps),
        in_specs=[qblk, kblk, kblk,
                  pl.BlockSpec((ta, 128), lambda p, n, qi_r, kj_r: (qi_r[n], 0)),
                  pl.BlockSpec((8, ta), lambda p, n, qi_r, kj_r: (0, kj_r[n]))] + ride_in_specs,
        out_specs=[qblk, qblk] + ride_out_specs,
        scratch_shapes=[pltpu.VMEM((2, ta, 128), F32), pltpu.VMEM((2, ta, 128), F32), pltpu.VMEM((ta, 128), F32),
                        pltpu.VMEM((2, ta, 128), F32)] + ride_sems,
    )
    return pl.pallas_call(
        body, name=name, grid_spec=grid_spec,
        out_shape=[_sds((t, 512), CDT), _sds((t, 512), F32)] + ride_out,
        compiler_params=_params(("arbitrary", "arbitrary")),
    )(jnp.asarray(qi), jnp.asarray(kj), qf, kf, vf, c, ct, *ride_in)


def _fox_bwd(qf, qft, kf, vf, c, ct, o, lse, do, dot, name, ride=None):
    t = qf.shape[0]
    ta = _row_tile(t)
    nq = t // ta
    qi, kj = _tri_steps(nq, by_key=False)
    nsteps = len(qi)
    ride_in, ride_in_specs, ride_out, ride_out_specs, ride_sems = _ride_specs(ride)

    def body(qi_r, kj_r, q_r, qt_r, k_r, v_r, c_r, ct_r, o_r, lse_r, do_r, dot_r, *rest):
        nr = len(ride_in)
        src_r, (dq_o, dcq_o, dk_o, dv_o, dck_o), dst_o = rest[:nr], rest[nr:nr + 5], rest[nr + 5:2 * nr + 5]
        lse_sc, dl_sc, cq_sc, dq_sc, dcq_sc, dkt_sc, dvt_sc, dckt_sc, *sems = rest[2 * nr + 5:]
        p = pl.program_id(0)
        n = pl.program_id(1)
        i = qi_r[n]
        j = kj_r[n]
        lane = _iota((1, 128), 1)
        lo = lane < HD
        top = _iota((128, 1), 0) < HD

        if ride is not None:
            @pl.when((p == 0) & (n == 0))
            def _():
                _ride_start(ride, src_r, dst_o, *sems)

        @pl.when(n == 0)
        def _():
            dkt_sc[...] = jnp.zeros(dkt_sc.shape, F32)
            dvt_sc[...] = jnp.zeros(dvt_sc.shape, F32)
            dckt_sc[...] = jnp.zeros(dckt_sc.shape, F32)

        @pl.when(j == 0)
        def _():
            dq_sc[...] = jnp.zeros(dq_sc.shape, F32)
            dcq_sc[...] = jnp.zeros(dcq_sc.shape, F32)
            dd = do_r[...] * o_r[...].astype(F32)
            lse = lse_r[...]
            for e in (0, 1):
                sel = lo if e == 0 else jnp.logical_not(lo)
                cq_sc[e] = jnp.broadcast_to(_head_col(c_r[...], lane, 2 * p + e), (ta, 128))
                dl_sc[e] = jnp.broadcast_to(jnp.sum(jnp.where(sel, dd, 0.0), axis=1, keepdims=True), (ta, 128))
                lse_sc[e] = jnp.broadcast_to(lse[:, HD * e:HD * e + 1], (ta, 128))

        def step(masked):
            q = q_r[...]
            qt = qt_r[...]
            k = k_r[...]
            v = v_r[...]
            dob = do_r[...].astype(CDT)
            dot_ = dot_r[...]
            ones = jnp.ones((ta, 128), CDT)
            ones16 = jnp.ones((16, ta), CDT)
            if masked:
                rows = i * ta + _iota((ta, 1), 0)
                cols = j * ta + _iota((1, ta), 1)
                mask = (cols <= rows) & (cols >= PAD)
            sub = _iota((8, 1), 0)
            for e in (0, 1):
                sel = lo if e == 0 else jnp.logical_not(lo)
                rsel = top if e == 0 else jnp.logical_not(top)
                s = _dot_nt(jnp.where(sel, q, 0), k)
                dp = _dot_nt(jnp.where(sel, dob, 0), v)
                ck = _head_row(ct_r[...], sub, 2 * p + e)
                cq, lse_e, dl = cq_sc[e], lse_sc[e], dl_sc[e]
                prs, dss = [], []
                for ch in range(ta // 128):
                    sl = slice(128 * ch, 128 * (ch + 1))
                    sc = s[:, sl] + cq - ck[:, sl]
                    if masked:
                        sc = jnp.where(mask[:, sl], sc, NEG)
                    pr = jnp.exp(sc - lse_e)
                    prs.append(pr.astype(CDT))
                    dss.append((pr * (dp[:, sl] - dl)).astype(CDT))
                pb = jnp.concatenate(prs, axis=1)
                dsb = jnp.concatenate(dss, axis=1)
                dvt_sc[j] += _dot(jnp.where(rsel, dot_, 0), pb)
                dkc = _dot(jnp.concatenate([jnp.where(rsel, qt, 0), ones16], axis=0), dsb)
                dkt_sc[j] += dkc[0:128]
                dckt_sc[j, 0:8, :] += jnp.where(sub == e, dkc[128:136], 0.0)
                dqa = _dot(dsb, jnp.concatenate([jnp.where(sel, k, 0), ones], axis=1))
                dq_sc[...] += dqa[:, :128]
                dcq_sc[e] += dqa[:, 128:]

        edge = (j == i) | (j == 0)

        @pl.when(edge)
        def _():
            step(True)

        @pl.when(jnp.logical_not(edge))
        def _():
            step(False)

        @pl.when(j == i)
        def _():
            dq_o[...] = dq_sc[...]
            dcq_o[...] = jnp.where(lo, dcq_sc[0], dcq_sc[1])

        @pl.when(n == nsteps - 1)
        def _():
            spread = (_iota((128, 128), 1) == _iota((128, 128), 0) // HD).astype(F32)
            for jb in range(nq):
                rs = slice(jb * ta, (jb + 1) * ta)
                dk_o[rs, :] = dkt_sc[jb].T
                dv_o[rs, :] = dvt_sc[jb].T
                dck_o[rs, :] = _dot_hi(spread, dckt_sc[jb]).T

        if ride is not None:
            @pl.when((p == NPAIR - 1) & (n == nsteps - 1))
            def _():
                _ride_wait(ride, src_r, dst_o, *sems)

    qblk = pl.BlockSpec((ta, 128), lambda p, n, qi_r, kj_r: (qi_r[n], p))
    qtblk = pl.BlockSpec((128, ta), lambda p, n, qi_r, kj_r: (p, qi_r[n]))
    kblk = pl.BlockSpec((ta, 128), lambda p, n, qi_r, kj_r: (kj_r[n], p))
    whole = pl.BlockSpec((t, 128), lambda p, n, qi_r, kj_r: (0, p))
    grid_spec = pltpu.PrefetchScalarGridSpec(
        num_scalar_prefetch=2, grid=(NPAIR, nsteps),
        in_specs=[qblk, qtblk, kblk, kblk,
                  pl.BlockSpec((ta, 128), lambda p, n, qi_r, kj_r: (qi_r[n], 0)),
                  pl.BlockSpec((8, ta), lambda p, n, qi_r, kj_r: (0, kj_r[n])),
                  qblk, qblk, qblk, qtblk] + ride_in_specs,
        out_specs=[qblk, qblk, whole, whole, whole] + ride_out_specs,
        scratch_shapes=[pltpu.VMEM((2, ta, 128), F32)] * 3 + [pltpu.VMEM((ta, 128), F32), pltpu.VMEM((2, ta, 128), F32)]
        + [pltpu.VMEM((nq, 128, ta), F32)] * 3 + ride_sems,
    )
    return pl.pallas_call(
        body, name=name, grid_spec=grid_spec,
        out_shape=[_sds((t, 512), F32)] * 5 + ride_out,
        compiler_params=_params(("arbitrary", "arbitrary")),
    )(jnp.asarray(qi), jnp.asarray(kj), qf, qft, kf, vf, c, ct, o, lse, do, dot, *ride_in)


def _bucket_table():
    r = np.arange(BLK)[:, None]
    c = np.arange(3 * BLK)[None, :]
    d = np.where(c < BLK, r + BLK - c, r - (c - BLK))
    n = np.maximum(d, 0)
    max_exact = N_BUCKETS // 2
    nf = np.maximum(n, 1).astype(np.float32)
    large = max_exact + (np.log(nf / max_exact) / math.log(BLK / max_exact) * (N_BUCKETS - max_exact)).astype(np.int32)
    large = np.minimum(large, N_BUCKETS - 1)
    b = np.where(n < max_exact, n, large)
    return np.where(c < 2 * BLK, b, N_BUCKETS - 1).astype(np.int32)


def _bias_fwd(table, name):
    bucket = jnp.asarray(_bucket_table())

    def body(tab_r, b_r, o_o):
        h = pl.program_id(0)
        b = b_r[...]
        acc = jnp.zeros(b.shape, F32)
        for k in range(N_BUCKETS):
            acc = jnp.where(b == k, tab_r[k, h], acc)
        o_o[...] = acc

    return pl.pallas_call(
        body, name=name, grid=(8,),
        in_specs=[pl.BlockSpec(memory_space=pltpu.SMEM), pl.BlockSpec((BLK, 3 * BLK), lambda h: (0, 0))],
        out_specs=pl.BlockSpec((None, BLK, 3 * BLK), lambda h: (h, 0, 0)),
        out_shape=_sds((8, BLK, 3 * BLK), F32),
        compiler_params=_params(("parallel",)),
    )(table, bucket)


def _bias_bwd(dbias, name):
    bucket = jnp.asarray(_bucket_table())

    def body(d_r, b_r, o_o):
        h = pl.program_id(0)
        b = b_r[...]
        d = d_r[...]
        lane = _iota((1, 128), 1)
        row = jnp.zeros((1, 128), F32)
        for k in range(N_BUCKETS):
            row = jnp.where(lane == k, jnp.sum(jnp.where(b == k, d, 0.0)), row)
        o_o[pl.ds(h, 1), :] = row

    return pl.pallas_call(
        body, name=name, grid=(8,),
        in_specs=[pl.BlockSpec((None, BLK, 3 * BLK), lambda h: (h, 0, 0)), pl.BlockSpec((BLK, 3 * BLK), lambda h: (0, 0))],
        out_specs=pl.BlockSpec((8, 128), lambda h: (0, 0)),
        out_shape=_sds((8, 128), F32),
        compiler_params=_params(("arbitrary",)),
    )(dbias, bucket)


def _swa_valid(i):
    r = _iota((BLK, 1), 0)
    c = _iota((1, 3 * BLK), 1)
    prev = (c < BLK) & (c > r) & (i >= 1) & ((i - 1) * BLK + c >= PAD)
    cc = c - BLK
    cur = (c >= BLK) & (c < 2 * BLK) & (cc <= r) & (i * BLK + cc >= PAD)
    cm = c - 2 * BLK
    meta = (c >= 2 * BLK) & (cm >= PAD) & (i * BLK + r - cm >= BLK)
    return prev | cur | meta


def _swa_kv_specs(ta):
    nb = ta // BLK
    return [pl.BlockSpec((None, BLK, 128), lambda p, i: (p // 2, jnp.maximum(i * nb - 1, 0), 0)),
            pl.BlockSpec((None, ta, 128), lambda p, i: (p // 2, i, 0)),
            pl.BlockSpec((None, BLK, 128), lambda p, i: (p // 2, 0, 0))]


def _swa_fwd(qs, kse, vse, bias, sinks, name, ride=None):
    t = qs.shape[0]
    ta = _row_tile(t)
    nb = ta // BLK
    grid = (NPAIR, t // ta)
    ride_in, ride_in_specs, ride_out, ride_out_specs, ride_sems = _ride_specs(ride)

    def body(sink_r, q_r, kp_r, kc_r, km_r, vp_r, vc_r, vm_r, b_r, o_o, lse_o):
        p = pl.program_id(0)
        i = pl.program_id(1)
        lo = _iota((1, 128), 1) < HD
        k4 = jnp.concatenate([kp_r[...], kc_r[...]], axis=0)
        v4 = jnp.concatenate([vp_r[...], vc_r[...]], axis=0)
        work = [(b, e) for b in range(nb) for e in (0, 1)]
        sinks = [sink_r[2 * p + e] for e in (0, 1)]
        v3 = [jnp.concatenate([v4[BLK * b:BLK * (b + 2)], vm_r[...]], axis=0) for b in range(nb)]
        s = {}
        for b in range(nb):
            q = q_r[BLK * b:BLK * (b + 1), :]
            k3 = jnp.concatenate([k4[BLK * b:BLK * (b + 2)], km_r[...]], axis=0)
            valid = _swa_valid(i * nb + b)
            for e in (0, 1):
                sel = lo if e == 0 else jnp.logical_not(lo)
                s[b, e] = jnp.where(valid, _dot_nt(jnp.where(sel, q, 0), k3) + b_r[e], NEG)
        mx = {w: jnp.maximum(jnp.max(s[w], axis=1, keepdims=True), sinks[w[1]]) for w in work}
        pe = {w: jnp.exp(s[w] - mx[w]) for w in work}
        den = {w: jnp.sum(pe[w], axis=1, keepdims=True) + jnp.exp(sinks[w[1]] - mx[w]) for w in work}
        out = {w: _dot(pe[w].astype(CDT), v3[w[0]]) / den[w] for w in work}
        for b in range(nb):
            rows = slice(BLK * b, BLK * (b + 1))
            o_o[rows, :] = jnp.where(lo, out[b, 0], out[b, 1]).astype(CDT)
            lse_o[rows, :] = jnp.where(lo, mx[b, 0] + jnp.log(den[b, 0]), mx[b, 1] + jnp.log(den[b, 1]))

    qblk = pl.BlockSpec((ta, 128), lambda p, i: (i, p))
    res = pl.pallas_call(
        _riding(body, 9, 2, ride, grid), name=name, grid=grid,
        in_specs=[pl.BlockSpec(memory_space=pltpu.SMEM), qblk] + _swa_kv_specs(ta) + _swa_kv_specs(ta)
        + [pl.BlockSpec((2, BLK, 3 * BLK), lambda p, i: (p, 0, 0))] + ride_in_specs,
        out_specs=[qblk, qblk] + ride_out_specs,
        out_shape=[_sds((t, 512), CDT), _sds((t, 512), F32)] + ride_out, scratch_shapes=ride_sems,
        compiler_params=_params(("arbitrary", "arbitrary") if ride else ("parallel", "parallel")),
    )(sinks, qs, kse, kse, kse, vse, vse, vse, bias, *ride_in)
    return (res[0], res[1], res[2:]) if ride else res


def _swa_bwd(qs, kse, vse, bias, sinks, o, lse, do, name):
    t = qs.shape[0]
    ta = _row_tile(t)
    nb = ta // BLK

    def body(sink_r, q_r, kp_r, kc_r, km_r, vp_r, vc_r, vm_r, b_r, o_r, lse_r, do_r,
             dq_o, dk_o, dv_o, db_o, dsk_o):
        p = pl.program_id(0)
        i = pl.program_id(1)
        lo = _iota((1, 128), 1) < HD

        @pl.when((i == 0) & (p % 2 == 0))
        def _():
            dk_o[...] = jnp.zeros(dk_o.shape, F32)
            dv_o[...] = jnp.zeros(dv_o.shape, F32)

        @pl.when(i == 0)
        def _():
            db_o[...] = jnp.zeros(db_o.shape, F32)
            dsk_o[...] = jnp.zeros(dsk_o.shape, F32)

        k4 = jnp.concatenate([kp_r[...], kc_r[...]], axis=0)
        v4 = jnp.concatenate([vp_r[...], vc_r[...]], axis=0)
        work = [(b, e) for b in range(nb) for e in (0, 1)]
        sel = [lo, jnp.logical_not(lo)]
        k3 = [jnp.concatenate([k4[BLK * b:BLK * (b + 2)], km_r[...]], axis=0) for b in range(nb)]
        v3 = [jnp.concatenate([v4[BLK * b:BLK * (b + 2)], vm_r[...]], axis=0) for b in range(nb)]
        q = [q_r[BLK * b:BLK * (b + 1), :] for b in range(nb)]
        do_ = [do_r[BLK * b:BLK * (b + 1), :] for b in range(nb)]
        lse = [lse_r[BLK * b:BLK * (b + 1), :] for b in range(nb)]
        dd = [do_[b] * o_r[BLK * b:BLK * (b + 1), :].astype(F32) for b in range(nb)]
        valid = [_swa_valid(i * nb + b) for b in range(nb)]
        qe = {(b, e): jnp.where(sel[e], q[b], 0) for b, e in work}
        doe = {(b, e): jnp.where(sel[e], do_[b], 0.0).astype(CDT) for b, e in work}
        lse_e = {(b, e): lse[b][:, HD * e:HD * e + 1] for b, e in work}
        delta = {(b, e): jnp.sum(jnp.where(sel[e], dd[b], 0.0), axis=1, keepdims=True) for b, e in work}
        s = {(b, e): jnp.where(valid[b], _dot_nt(qe[b, e], k3[b]) + b_r[e], NEG) for b, e in work}
        dp = {(b, e): _dot_nt(doe[b, e], v3[b]) for b, e in work}
        pr = {w: jnp.exp(s[w] - lse_e[w]) for w in work}
        ds = {w: pr[w] * (dp[w] - delta[w]) for w in work}
        dqs = {(b, e): _dot(ds[b, e].astype(CDT), jnp.where(sel[e], k3[b], 0)) for b, e in work}
        both = lambda x, b: jnp.concatenate([x[b, 0], x[b, 1]], axis=0)
        dk3 = [_dot(both(ds, b).T.astype(CDT), both(qe, b)) for b in range(nb)]
        dv3 = [_dot(both(pr, b).T.astype(CDT), both(doe, b)) for b in range(nb)]
        for e in (0, 1):
            tot = ds[0, e]
            for b in range(1, nb):
                tot = tot + ds[b, e]
            db_o[e] += tot
        dsink = [sum(-jnp.sum(jnp.exp(sink_r[2 * p + e] - lse_e[b, e]) * delta[b, e], axis=0, keepdims=True)
                     for b in range(nb)) for e in (0, 1)]
        dsk_o[0:1, :] += jnp.where(lo, dsink[0], dsink[1])
        for b in range(nb):
            ib = i * nb + b
            dq_o[BLK * b:BLK * (b + 1), :] = dqs[b, 0] + dqs[b, 1]
            dk = dk3[b]
            dv = dv3[b]
            prev = pl.ds(pl.multiple_of(jnp.maximum(ib - 1, 0) * BLK, BLK), BLK)
            cur = pl.ds(pl.multiple_of(ib * BLK, BLK), BLK)
            dk_o[prev, :] += dk[0:BLK]
            dk_o[cur, :] += dk[BLK:2 * BLK]
            dk_o[0:BLK, :] += dk[2 * BLK:]
            dv_o[prev, :] += dv[0:BLK]
            dv_o[cur, :] += dv[BLK:2 * BLK]
            dv_o[0:BLK, :] += dv[2 * BLK:]

    qblk = pl.BlockSpec((ta, 128), lambda p, i: (i, p))
    kvacc = pl.BlockSpec((None, t, 128), lambda p, i: (p // 2, 0, 0))
    bblk = pl.BlockSpec((2, BLK, 3 * BLK), lambda p, i: (p, 0, 0))
    return pl.pallas_call(
        body, name=name, grid=(NPAIR, t // ta),
        in_specs=[pl.BlockSpec(memory_space=pltpu.SMEM), qblk] + _swa_kv_specs(ta) + _swa_kv_specs(ta)
        + [bblk, qblk, qblk, qblk],
        out_specs=[qblk, kvacc, kvacc, bblk, pl.BlockSpec((None, 8, 128), lambda p, i: (p, 0, 0))],
        out_shape=[_sds((t, 512), F32), _sds((2, t, 128), F32), _sds((2, t, 128), F32),
                   _sds((8, BLK, 3 * BLK), F32), _sds((NPAIR, 8, 128), F32)],
        compiler_params=_params(("arbitrary", "arbitrary")),
    )(sinks, qs, kse, kse, kse, vse, vse, vse, bias, o, lse, do)


def _sum8(slots, name):
    def body(a_r, o_o):
        acc = a_r[0]
        for k in range(1, 8):
            acc = acc + a_r[k]
        o_o[...] = acc

    return pl.pallas_call(
        body, name=name, out_shape=_sds((SMALL_ROWS, 128), F32),
        in_specs=[pl.BlockSpec(memory_space=pltpu.VMEM)], out_specs=pl.BlockSpec(memory_space=pltpu.VMEM),
        compiler_params=_params(),
    )(slots)


def _place():
    x, y, c = lax.axis_index("x"), lax.axis_index("y"), lax.axis_index("c")
    chips = [(1 - x, y), (x, 1 - y), (1 - x, 1 - y)]
    return x, y, c, chips


def _remote(src, dst, send_sems, recv_sems, k, to):
    return pltpu.make_async_remote_copy(src_ref=src, dst_ref=dst, send_sem=send_sems.at[k], recv_sem=recv_sems.at[k],
                                        device_id=to, device_id_type=MESH_ID)


ANY = pl.BlockSpec(memory_space=pl.ANY)


def _mix_cols(w):
    return jnp.concatenate([w[:, 2312:4360], w[:, 0:1536], w[:, 1544:2312], w[:, 1536:1544],
                            jnp.zeros((w.shape[0], DP - D_IN), w.dtype)], axis=1)


def _unmix_cols(w):
    return jnp.concatenate([w[:, QA:QA + 1536], w[:, FA:FA + 8], w[:, QB:QB + 768], w[:, GA:GA + 2048]], axis=1)


def _rows128(a, rows):
    flat = a.reshape(-1)
    return jnp.pad(flat, (0, rows * 128 - flat.shape[0])).reshape(rows, 128)


GRAD_FORM = {"ffn1_w_in": "col", "ffn2_w_in": "col", "w_branch_fox": "col", "w_branch_swa": "col",
             "ffn1_w_out": "3d", "ffn2_w_out": "3d", "w_out": "3d", "w_in": "3d"}
SUM_TILE = {1024: 128, 704: 176, 512: 128, 256: 128}
NT = len(SHARD_ITEMS)
ALL_ITEMS = tuple(range(NT))


def _half_rows(c, r):
    return pl.ds(pl.multiple_of(c * (r // 2), 16), r // 2)


def _ici_copies(kind, srcs, dsts, send_sems, recv_sems, layer, recv=True, items=ALL_ITEMS):
    x, y, c, chips = _place()
    s = 2 * x + y
    sends, recvs = [], []
    for t, (item, src, dst) in enumerate(zip(items, srcs, dsts)):
        nm, (r, cc), _ = SHARD_ITEMS[item]
        for j, (cx, cy) in enumerate(chips):
            sj = 2 * cx + cy
            k = 3 * t + j
            to = (cx, cy, c)
            if kind == "gather":
                hs = _half_rows(c, r)
                sends.append(_remote(src.at[layer, hs], dst.at[s, hs], send_sems, recv_sems, k, to))
                if recv:
                    recvs.append(_remote(src.at[layer, hs], dst.at[sj, hs], send_sems, recv_sems, k, to))
            else:
                if GRAD_FORM[nm] == "col":
                    piece = src.at[:, pl.ds(pl.multiple_of(sj * cc, 128), cc)]
                else:
                    piece = src.at[sj]
                sends.append(_remote(piece, dst.at[j], send_sems, recv_sems, k, to))
                recvs.append(sends[-1])
    return sends, recvs


def _slab_shapes(items=ALL_ITEMS):
    return [_sds((4, *SHARD_ITEMS[t][1]), CDT) for t in items]


def _dma_sems(n):
    return [pltpu.SemaphoreType.DMA((n,)), pltpu.SemaphoreType.DMA((n,))]


def _forward_sends(dsts, send_sems, recv_sems, items=ALL_ITEMS):
    x, y, c, chips = _place()
    sends, recvs = [], []
    for t, (item, dst) in enumerate(zip(items, dsts)):
        r = SHARD_ITEMS[item][1][0]
        for j, (cx, cy) in enumerate(chips):
            sj = 2 * cx + cy
            hs, ho = _half_rows(c, r), _half_rows(1 - c, r)
            sends.append(_remote(dst.at[sj, hs], dst.at[sj, hs], send_sems, recv_sems, 3 * t + j, (x, y, 1 - c)))
            recvs.append(_remote(dst.at[sj, ho], dst.at[sj, ho], send_sems, recv_sems, 3 * t + j, (x, y, 1 - c)))
    return sends, recvs


def _gather_layer(wb, mflat, layer, name, items):
    nt = len(items)

    def body(*refs):
        srcs, m_r, dsts, mall_o = refs[:nt], refs[nt], refs[nt + 1:2 * nt + 1], refs[2 * nt + 1]
        send_sems, recv_sems, fsend, frecv, msend, mrecv = refs[2 * nt + 2:]
        x, y, c, chips = _place()
        s = 2 * x + y
        sends, recvs = _ici_copies("gather", srcs, dsts, send_sems, recv_sems, layer, items=items)
        metas = [_remote(m_r, mall_o.at[s], msend, mrecv, j, (cx, cy, c)) for j, (cx, cy) in enumerate(chips)]
        for cp in sends + metas:
            cp.start()
        fwds, frecvs = _forward_sends(dsts, fsend, frecv, items)
        for got, fwd in zip(recvs, fwds):
            got.wait_recv()
            fwd.start()
        for got in frecvs:
            got.wait_recv()
        for j, (cx, cy) in enumerate(chips):
            _remote(m_r, mall_o.at[2 * cx + cy], msend, mrecv, j, (cx, cy, c)).wait_recv()
        for cp in sends + metas + fwds:
            cp.wait_send()

    return pl.pallas_call(
        body, name=name, out_shape=_slab_shapes(items) + [_sds((4, META_ROWS, 128), F32)],
        in_specs=[ANY] * (nt + 1), out_specs=[ANY] * (nt + 1),
        scratch_shapes=_dma_sems(3 * nt) + _dma_sems(3 * nt) + _dma_sems(3),
    )(*wb, mflat)


def _forward_layer(slabs, name, items=ALL_ITEMS):
    nt = len(items)

    def body(*refs):
        ins, outs, send_sems, recv_sems = refs[:nt], refs[nt:2 * nt], refs[2 * nt], refs[2 * nt + 1]
        sends, recvs = _forward_sends(outs, send_sems, recv_sems, items)
        for cp in sends:
            cp.start()
        for cp in recvs:
            cp.wait_recv()
        for cp in sends:
            cp.wait_send()

    return pl.pallas_call(
        body, name=name, out_shape=_slab_shapes(items), in_specs=[ANY] * nt, out_specs=[ANY] * nt,
        input_output_aliases={t: t for t in range(nt)}, scratch_shapes=_dma_sems(3 * nt),
    )(*slabs)


def _half_shape(nm, r, c):
    return (r // 2, 4 * c) if GRAD_FORM[nm] == "col" else (4, r // 2, c)


def _swap_layer(gs, gsm, name, items=ALL_ITEMS):
    small = gsm is not None
    nt = len(items)

    def body(*refs):
        g_rs = refs[:nt]
        pos = nt
        if small:
            s_r = refs[pos]
            pos += 1
        got_os = refs[pos:pos + nt]
        pos += nt
        if small:
            slots_o = refs[pos]
            pos += 1
        send_sems, recv_sems = refs[pos], refs[pos + 1]
        x, y, c, _ = _place()
        sib = (x, y, 1 - c)
        sent = []
        for t, (item, g_r, got_o) in enumerate(zip(items, g_rs, got_os)):
            nm, (r, cc), _ = SHARD_ITEMS[item]
            ho = _half_rows(1 - c, r)
            src = g_r.at[ho, :] if GRAD_FORM[nm] == "col" else g_r.at[:, ho, :]
            sent.append(_remote(src, got_o, send_sems, recv_sems, t, sib))
        if small:
            ssend, srecv, loc_sem = refs[pos + 2], refs[pos + 3], refs[pos + 4]
            me = 4 * x + 2 * y + c
            loc = pltpu.make_async_copy(s_r, slots_o.at[me], loc_sem.at[0])
            loc.start()
            peers = [(x ^ (k >> 2), y ^ ((k >> 1) & 1), c ^ (k & 1)) for k in range(1, 8)]
            for k, peer in enumerate(peers):
                sent.append(_remote(s_r, slots_o.at[me], ssend, srecv, k, peer))
        for cp in sent:
            cp.start()
        for cp in sent[:nt]:
            cp.wait_recv()
        if small:
            for k, (px, py, pc) in enumerate(peers):
                _remote(s_r, slots_o.at[4 * px + 2 * py + pc], ssend, srecv, k, (px, py, pc)).wait_recv()
        for cp in sent:
            cp.wait_send()
        if small:
            loc.wait()

    outs = [_sds(_half_shape(*SHARD_ITEMS[item][0:1], *SHARD_ITEMS[item][1]), CDT) for item in items]
    ops = list(gs)
    sems = _dma_sems(nt)
    if small:
        outs.append(_sds((8, SMALL_ROWS, 128), F32))
        ops.append(gsm)
        sems = sems + _dma_sems(7) + [pltpu.SemaphoreType.DMA((1,))]
    res = pl.pallas_call(
        body, name=name, out_shape=outs, in_specs=[ANY] * len(ops), out_specs=[ANY] * len(outs), scratch_shapes=sems,
    )(*ops)
    return (res[:nt], res[nt]) if small else (res, None)


def _pair_add_t(own, got, half_idx, nm, r, name):
    tr = SUM_TILE[r]
    nb = (r // 2) // tr
    if GRAD_FORM[nm] == "col":
        blk = (tr, own.shape[1])
        own_spec = pl.BlockSpec(blk, lambda i, c_r: (c_r[0] * nb + i, 0))
        half_spec = pl.BlockSpec(blk, lambda i, c_r: (i, 0))
    else:
        blk = (4, tr, own.shape[2])
        own_spec = pl.BlockSpec(blk, lambda i, c_r: (0, c_r[0] * nb + i, 0))
        half_spec = pl.BlockSpec(blk, lambda i, c_r: (0, i, 0))

    def body(c_r, a_r, b_r, o_o):
        o_o[...] = (a_r[...].astype(F32) + b_r[...].astype(F32)).astype(CDT)

    grid_spec = pltpu.PrefetchScalarGridSpec(num_scalar_prefetch=1, grid=(nb,), in_specs=[own_spec, half_spec],
                                             out_specs=half_spec)
    return pl.pallas_call(body, name=name, grid_spec=grid_spec, out_shape=_sds(got.shape, CDT),
                          compiler_params=_params(("parallel",)))(half_idx, own, got)


def _sum4_t(ps, got3, buf, idx, layer, nm, r, name):
    tr = SUM_TILE[r]
    nb = (r // 2) // tr
    c = got3.shape[2]
    if GRAD_FORM[nm] == "col":
        ps_spec = pl.BlockSpec((tr, c), lambda i, x_r: (i, x_r[0]))
    else:
        ps_spec = pl.BlockSpec((None, tr, c), lambda i, x_r: (x_r[0], i, 0))

    def body(x_r, a_r, b_r, buf_r, o_o):
        o_o[...] = ((a_r[...].astype(F32) + b_r[0].astype(F32)) + b_r[1].astype(F32)) + b_r[2].astype(F32)

    grid_spec = pltpu.PrefetchScalarGridSpec(
        num_scalar_prefetch=1, grid=(nb,),
        in_specs=[ps_spec, pl.BlockSpec((3, tr, c), lambda i, x_r: (0, i, 0)), ANY],
        out_specs=pl.BlockSpec((None, tr, c), lambda i, x_r: (layer, x_r[1] * nb + i, 0)),
    )
    return pl.pallas_call(body, name=name, grid_spec=grid_spec, out_shape=_sds(buf.shape, F32),
                          input_output_aliases={3: 0}, compiler_params=_params(("parallel",)))(idx, ps, got3, buf)


def _scatter_layer(ps, name, items=ALL_ITEMS):
    nt = len(items)

    def body(*refs):
        srcs, dsts, send_sems, recv_sems = refs[:nt], refs[nt:2 * nt], refs[2 * nt], refs[2 * nt + 1]
        sends, recvs = _ici_copies("scatter", srcs, dsts, send_sems, recv_sems, None, items=items)
        for cp in sends:
            cp.start()
        for cp in recvs:
            cp.wait_recv()
        for cp in sends:
            cp.wait_send()

    return pl.pallas_call(
        body, name=name, out_shape=_got3_shapes(items), in_specs=[ANY] * nt, out_specs=[ANY] * nt,
        scratch_shapes=_dma_sems(3 * nt),
    )(*ps)


def _got3_shapes(items=ALL_ITEMS):
    return [_sds((3, SHARD_ITEMS[t][1][0] // 2, SHARD_ITEMS[t][1][1]), CDT) for t in items]


def _join_layer(bufs, name):
    def body(*refs):
        ins, outs, send_sems, recv_sems = refs[:NT], refs[NT:2 * NT], refs[2 * NT], refs[2 * NT + 1]
        x, y, c, _ = _place()
        sent = []
        for t, ((nm, (r, cc), _), b_o) in enumerate(zip(SHARD_ITEMS, outs)):
            hs = _half_rows(c, r)
            sent.append(_remote(b_o.at[:, hs, :], b_o.at[:, hs, :], send_sems, recv_sems, t, (x, y, 1 - c)))
        for cp in sent:
            cp.start()
        for t, ((nm, (r, cc), _), b_o) in enumerate(zip(SHARD_ITEMS, outs)):
            ho = _half_rows(1 - c, r)
            _remote(b_o.at[:, ho, :], b_o.at[:, ho, :], send_sems, recv_sems, t, (x, y, 1 - c)).wait_recv()
        for cp in sent:
            cp.wait_send()

    return pl.pallas_call(
        body, name=name, out_shape=[_sds(b.shape, F32) for b in bufs], in_specs=[ANY] * NT, out_specs=[ANY] * NT,
        input_output_aliases={t: t for t in range(NT)}, scratch_shapes=_dma_sems(NT),
    )(*bufs)


def _adamw3(w, g, m, v, name):
    nl, r, c = w.shape
    tr = SUM_TILE.get(r, r)
    if r % 8:
        blk = pl.BlockSpec((None, r, 256), lambda l, i: (l, 0, i))
        steps = c // 256
    else:
        blk = pl.BlockSpec((None, tr, c), lambda l, i: (l, i, 0))
        steps = r // tr

    def body(w_r, g_r, m_r, v_r, d_o, m_o, v_o):
        g_ = g_r[...]
        m_ = ADAM_B1 * m_r[...] + (1.0 - ADAM_B1) * g_
        v_ = ADAM_B2 * v_r[...] + (1.0 - ADAM_B2) * jnp.square(g_)
        m_hat = m_ / (1.0 - ADAM_B1 ** ADAM_STEP)
        v_hat = v_ / (1.0 - ADAM_B2 ** ADAM_STEP)
        d_o[...] = -ADAM_LR * (m_hat / (jnp.sqrt(v_hat) + ADAM_EPS) + ADAM_WD * w_r[...])
        m_o[...] = m_
        v_o[...] = v_

    return pl.pallas_call(
        body, name=name, grid=(nl, steps),
        in_specs=[blk] * 4, out_specs=[blk] * 3, out_shape=[_sds((nl, r, c), F32)] * 3,
        compiler_params=_params(("parallel", "parallel")),
    )(w, g, m, v)


def _full_weights(slabs, wb, layer, shard, items=ALL_ITEMS):
    ws = {}
    for t, slab in zip(items, slabs):
        nm, (r, c), kind = SHARD_ITEMS[t]
        slab = lax.dynamic_update_slice(slab, wb[nm][layer][None], (shard, 0, 0))
        ws[nm] = slab.reshape(4 * r, c) if kind == "row" else jnp.concatenate([slab[s] for s in range(4)], axis=1)
    return ws


def _exchange_forms(g, items=ALL_ITEMS):
    out = []
    for t in items:
        nm, (r, c), _ = SHARD_ITEMS[t]
        a = g[nm]
        if nm == "w_in":
            a = a.reshape(D, 4, c).transpose(1, 0, 2)
        elif GRAD_FORM[nm] == "3d":
            a = a.reshape(4, r, c)
        out.append(a)
    return out


SMALL_ITEMS = (("rel_bias_table", 2), ("ffn1_norm", 16), ("mix_norm", 16), ("ffn2_norm", 16), ("forget_bias", 1),
               ("fox_q_norm", 1), ("fox_k_norm", 1), ("swa_q_norm", 1), ("swa_k_norm", 1), ("swa_sinks", 1))
SMALL_ADAM_ROWS = 96


def _layer_fwd(h, lw, l, ride=None, late=None):
    rides = late["rides"] if late else {}

    def run(key, fn, *args):
        r = rides.get(key)
        if r is None:
            return fn(*args)
        out = fn(*args, ride=r)
        late["arrived"](key, out[-1])
        return out[0] if len(out) == 2 else out[:-1]

    sv = {"h0": h}
    a, sv["a1t"] = _rms_fwd(h, lw["ffn1_norm"], f"rms_fwd_a{l}")
    sv["gu1"], s, sv["s1t"] = run("ffn_in_a", _ffn_in, a, lw["ffn1_w_in"], f"ffn_in_a{l}")
    h = run("ffn_out_a", _mm_res, s, lw["ffn1_w_out"], h, 0.5, f"ffn_out_a{l}")
    sv["h1"] = h
    a, sv["amt"] = _rms_fwd(h, lw["mix_norm"], f"rms_fwd_m{l}")
    if late:
        late["need"](lw, "mixer")
    proj = run("proj", _mm, a, lw["w_mix"], F32, _row_tile(h.shape[0]), DP, f"proj{l}")
    sv["proj"] = proj
    qf, kf, vf, qs, kse, vse, c, ct, sv["qft"] = _qknorm_fwd(proj, lw["gfq"], lw["gfk"], lw["gsq"], lw["gsk"], lw["fb"],
                                                              f"qknorm_fwd{l}")
    ofox, lse_f, *rode = _fox_fwd(qf, kf, vf, c, ct, f"fox_fwd{l}", ride)
    oswa, lse_s = run("swa_fwd", _swa_fwd, qs, kse, vse, lw["bias"], lw["sinks"], f"swa_fwd{l}")
    if late:
        late["need"](lw, "gate")
    sv.update(qf=qf, kf=kf, vf=vf, qs=qs, kse=kse, vse=vse, c=c, ct=ct, ofox=ofox, oswa=oswa, lse_f=lse_f, lse_s=lse_s)
    h, sv["yt"], sv["pf"], sv["ps"], sv["oft"], sv["ost"] = _gate_out_fwd(
        ofox, oswa, lw["w_branch_fox"], lw["w_branch_swa"], proj, lw["w_out"], h, f"gate_out_fwd{l}")
    sv["h2"] = h
    a, sv["a2t"] = _rms_fwd(h, lw["ffn2_norm"], f"rms_fwd_b{l}")
    sv["gu2"], s, sv["s2t"] = _ffn_in(a, lw["ffn2_w_in"], f"ffn_in_b{l}")
    h = _mm_res(s, lw["ffn2_w_out"], h, 0.5, f"ffn_out_b{l}")
    return h, sv, rode


def _ffn_bwd(dh, dhb, h_in, at, gu, st, norm, w_in, w_out, tag, rides=None):
    r = rides or (None,) * 4
    rode = []

    def split(res, ride):
        if ride is None:
            return res
        rode.extend(res[-1])
        return res[0] if len(res) == 2 else res[:-1]

    dgu = split(_ffn_bwd_mid(dhb, w_out, gu, f"ffn_bwd_mid_{tag}", r[0]), r[0])
    d_w_out = split(_mm(st, dhb, CDT, 256, D, f"dw_ffn_out_{tag}", scale=0.5, ride=r[1]), r[1])
    dh, dhb, dg = split(_ffn_bwd_in(dgu, w_in, h_in, norm, dh, f"ffn_bwd_in_{tag}", r[2]), r[2])
    d_w_in = split(_mm(at, dgu, CDT, D, 256, f"dw_ffn_in_{tag}", ride=r[3]), r[3])
    return dh, dhb, d_w_out, d_w_in, dg, rode


def _layer_bwd(dh, dhb, sv, lw, l, ride=None, before_ffn1=None):
    g = {}
    dh, dhb, g["ffn2_w_out"], g["ffn2_w_in"], g["ffn2_norm"], _ = _ffn_bwd(
        dh, dhb, sv["h2"], sv["a2t"], sv["gu2"], sv["s2t"], lw["ffn2_norm"], lw["ffn2_w_in"], lw["ffn2_w_out"], f"b{l}")
    g["w_out"] = _mm(sv["yt"], dhb, CDT, 512, 512, f"dw_out{l}")
    dpf, dps, dga, dgb = _gate_out_bwd(dhb, lw["w_out"], sv["pf"], sv["ps"], sv["proj"], f"gate_out_bwd{l}")
    do_f, do_ft = _mm_nt(dpf, lw["w_branch_fox"], f"d_ofox{l}", with_t=True)
    do_s = _mm_nt(dps, lw["w_branch_swa"], f"d_oswa{l}")
    g["w_branch_fox"] = _mm(sv["oft"], dpf, CDT, 512, 512, f"dw_bfox{l}")
    g["w_branch_swa"] = _mm(sv["ost"], dps, CDT, 512, 512, f"dw_bswa{l}")
    dqf, dcq, dkf, dvf, dck, *rode = _fox_bwd(sv["qf"], sv["qft"], sv["kf"], sv["vf"], sv["c"], sv["ct"], sv["ofox"],
                                              sv["lse_f"], do_f, do_ft, f"fox_bwd{l}", ride)
    g["rode"] = rode
    dqs, dkse, dvse, dbias, dsk = _swa_bwd(sv["qs"], sv["kse"], sv["vse"], lw["bias"], lw["sinks"], sv["oswa"],
                                           sv["lse_s"], do_s, f"swa_bwd{l}")
    dproj, dgn = _qknorm_bwd(sv["proj"], dqf, dkf, dvf, dqs, dkse, dvse, dcq, dck, dga, dgb,
                             lw["gfq"], lw["gfk"], lw["gsq"], lw["gsk"], lw["fb"], f"qknorm_bwd{l}")
    g["w_mix"] = _mm(sv["amt"], dproj, CDT, D, 640, f"dw_mix{l}")
    dh, dhb, g["mix_norm"] = _mm_nt_rms(dproj, lw["w_mix"], sv["h1"], lw["mix_norm"], dh, f"d_am{l}")
    g["dbias"], g["dsk"], g["dgn"] = dbias, dsk, dgn
    rides = before_ffn1(g) if before_ffn1 else None
    dh, dhb, g["ffn1_w_out"], g["ffn1_w_in"], g["ffn1_norm"], g["rode_ffn1"] = _ffn_bwd(
        dh, dhb, sv["h0"], sv["a1t"], sv["gu1"], sv["s1t"], lw["ffn1_norm"], lw["ffn1_w_in"], lw["ffn1_w_out"], f"a{l}",
        rides)
    return dh, dhb, g


def kernel(x, meta_tokens, rel_bias_table, ffn1_norm, ffn1_w_in, ffn1_w_out, mix_norm, w_in, forget_bias, fox_q_norm, fox_k_norm, swa_q_norm, swa_k_norm, swa_sinks, w_branch_fox, w_branch_swa, w_out, ffn2_norm, ffn2_w_in, ffn2_w_out, loss_target, m_meta_tokens, m_rel_bias_table, m_ffn1_norm, m_ffn1_w_in, m_ffn1_w_out, m_mix_norm, m_w_in, m_forget_bias, m_fox_q_norm, m_fox_k_norm, m_swa_q_norm, m_swa_k_norm, m_swa_sinks, m_w_branch_fox, m_w_branch_swa, m_w_out, m_ffn2_norm, m_ffn2_w_in, m_ffn2_w_out, v_meta_tokens, v_rel_bias_table, v_ffn1_norm, v_ffn1_w_in, v_ffn1_w_out, v_mix_norm, v_w_in, v_forget_bias, v_fox_q_norm, v_fox_k_norm, v_swa_q_norm, v_swa_k_norm, v_swa_sinks, v_w_branch_fox, v_w_branch_swa, v_w_out, v_ffn2_norm, v_ffn2_w_in, v_ffn2_w_out):
    names = ["meta_tokens", "rel_bias_table", "ffn1_norm", "ffn1_w_in", "ffn1_w_out", "mix_norm", "w_in", "forget_bias",
             "fox_q_norm", "fox_k_norm", "swa_q_norm", "swa_k_norm", "swa_sinks", "w_branch_fox", "w_branch_swa", "w_out",
             "ffn2_norm", "ffn2_w_in", "ffn2_w_out"]
    w = dict(zip(names, [meta_tokens, rel_bias_table, ffn1_norm, ffn1_w_in, ffn1_w_out, mix_norm, w_in, forget_bias,
                         fox_q_norm, fox_k_norm, swa_q_norm, swa_k_norm, swa_sinks, w_branch_fox, w_branch_swa, w_out,
                         ffn2_norm, ffn2_w_in, ffn2_w_out]))
    m = dict(zip(names, [m_meta_tokens, m_rel_bias_table, m_ffn1_norm, m_ffn1_w_in, m_ffn1_w_out, m_mix_norm, m_w_in,
                         m_forget_bias, m_fox_q_norm, m_fox_k_norm, m_swa_q_norm, m_swa_k_norm, m_swa_sinks,
                         m_w_branch_fox, m_w_branch_swa, m_w_out, m_ffn2_norm, m_ffn2_w_in, m_ffn2_w_out]))
    v = dict(zip(names, [v_meta_tokens, v_rel_bias_table, v_ffn1_norm, v_ffn1_w_in, v_ffn1_w_out, v_mix_norm, v_w_in,
                         v_forget_bias, v_fox_q_norm, v_fox_k_norm, v_swa_q_norm, v_swa_k_norm, v_swa_sinks,
                         v_w_branch_fox, v_w_branch_swa, v_w_out, v_ffn2_norm, v_ffn2_w_in, v_ffn2_w_out]))
    xi, yi, ci = lax.axis_index("x"), lax.axis_index("y"), lax.axis_index("c")
    shard = 2 * xi + yi
    seq = x.shape[1]
    t = seq + BLK

    wb = {nm: w[nm].astype(CDT) for nm, _, _ in SHARD_ITEMS}
    wb_list = [wb[nm] for nm, _, _ in SHARD_ITEMS]
    mflat = meta_tokens.reshape(META_ROWS, 128)
    first = (0, 1)
    *slabs_first, mall = _gather_layer([wb_list[t] for t in first], mflat, 0, "gather_weights", first)
    mall = lax.dynamic_update_slice(mall, mflat[None], (shard, 0, 0))
    meta_full = jnp.concatenate([mall[s].reshape(N_META, 256) for s in range(4)], axis=1)
    bias = _bias_fwd(rel_bias_table, "bias_fwd")

    def layer_weights(slabs, l, items=ALL_ITEMS):
        lw = _full_weights(slabs, wb, l, shard, items)
        if "w_in" in lw:
            lw["w_mix"] = _mix_cols(lw.pop("w_in"))
        return lw

    def layer_vectors(l):
        lw = {nm: w[nm][l].reshape(1, D) for nm in ("ffn1_norm", "mix_norm", "ffn2_norm")}
        lw["gfq"] = jnp.tile(fox_q_norm[l], 8).reshape(1, 512)
        lw["gfk"] = jnp.tile(fox_k_norm[l], 8).reshape(1, 512)
        lw["gsq"] = jnp.tile(swa_q_norm[l], 8).reshape(1, 512)
        lw["gsk"] = jnp.tile(swa_k_norm[l], 2).reshape(1, 128)
        lw["fb"] = jnp.pad(forget_bias[l], (0, 120)).reshape(1, 128)
        lw["sinks"] = swa_sinks[l]
        lw["bias"] = bias
        return lw

    def gather_ride(layer, items):
        return ("gather", [wb_list[t] for t in items], _slab_shapes(items), layer, items)

    landed = {}

    def need(lw, stage):
        if stage == "mixer":
            items = (2,)
            slabs = _forward_layer(landed["ffn_in_a"], "forward_halves0m", items)
        else:
            items = (3, 4, 5, 6, 7)
            slabs = _forward_layer(landed["ffn_out_a"] + landed["proj"] + landed["swa_fwd"], "forward_halves0g", items)
        lw.update(layer_weights(slabs, 0, items))

    late = {"rides": {"ffn_in_a": gather_ride(0, (2,)), "ffn_out_a": gather_ride(0, (3, 4, 5)),
                      "proj": gather_ride(0, (6,)), "swa_fwd": gather_ride(0, (7,))},
            "arrived": landed.__setitem__, "need": need}

    h = jnp.concatenate([jnp.zeros((PAD, D), F32), meta_full, x[0]], axis=0)
    lws = [{**layer_vectors(0), **layer_weights(slabs_first, 0, first)}]
    h, sv0, slabs1 = _layer_fwd(h, lws[0], 0, gather_ride(1, ALL_ITEMS), late)
    lws.append({**layer_vectors(1), **layer_weights(_forward_layer(slabs1, "forward_halves"), 1)})
    h, sv1, _ = _layer_fwd(h, lws[1], 1)
    saved = [sv0, sv1]
    dh, dhb, lacc = _loss(h, loss_target[0], "loss")
    loss = lax.psum(lacc[0, 0], ("x", "y", "c"))

    half_idx = ci.reshape(1).astype(jnp.int32)
    place_idx = jnp.stack([shard, ci]).astype(jnp.int32)

    def pair_sums(g, gsm, tag, items=ALL_ITEMS):
        if "w_mix" in g:
            g["w_in"] = _unmix_cols(g.pop("w_mix"))
        forms = _exchange_forms(g, items)
        got, slots = _swap_layer(forms, gsm, f"swap_halves{tag}", items)
        return {t: _pair_add_t(a, b, half_idx, SHARD_ITEMS[t][0], SHARD_ITEMS[t][1][0],
                               f"pair_add{tag}_{SHARD_ITEMS[t][0]}")
                for t, a, b in zip(items, forms, got)}, slots

    def scatter_ride(ps, items):
        return ("scatter", [ps[t] for t in items], _got3_shapes(items), None, items)

    early = (2, 3, 4, 5, 6, 7)
    early_rides = ((6,), (7,), (2, 5), (3, 4))
    ps0 = {}

    def before_ffn1(g):
        ps0.update(pair_sums(g, None, "0e", early)[0])
        return [scatter_ride(ps0, items) for items in early_rides]

    grads = [None, None]
    dh, dhb, grads[1] = _layer_bwd(dh, dhb, saved[1], lws[1], 1)
    ps1, _ = pair_sums(grads[1], None, 1)
    dh, dhb, grads[0] = _layer_bwd(dh, dhb, saved[0], lws[0], 0, scatter_ride(ps1, ALL_ITEMS), before_ffn1)
    grad_x = dh[BLK:].reshape(1, seq, D)
    dtab = _bias_bwd(grads[0]["dbias"] + grads[1]["dbias"], "bias_bwd")

    small = [dh[PAD:BLK].reshape(128, 128), _rows128(dtab[:, :N_BUCKETS].T, 2)]
    for nm in ("ffn1_norm", "mix_norm", "ffn2_norm"):
        small.append(jnp.stack([grads[l][nm][0] for l in range(2)]).reshape(16, 128))
    small.append(_rows128(jnp.stack([grads[l]["dgn"][4, :8] for l in range(2)]), 1))
    for row in range(4):
        small.append(jnp.stack([grads[l]["dgn"][row, :HD] for l in range(2)]).reshape(1, 128))
    dsk = [grads[l]["dsk"][:, 0, :] for l in range(2)]
    small.append(_rows128(jnp.stack([jnp.stack([d[:, 0], d[:, HD]], axis=1).reshape(8) for d in dsk]), 1))
    gsm = jnp.concatenate(small, axis=0)
    gsm = jnp.pad(gsm, ((0, SMALL_ROWS - gsm.shape[0]), (0, 0)))

    late = (0, 1)
    ps_late, slots = pair_sums(grads[0], gsm, "0l", late)
    ps0.update(ps_late)
    got3_0 = dict(zip([t for items in early_rides for t in items], grads[0]["rode_ffn1"]))
    got3_0.update(zip(late, _scatter_layer([ps0[t] for t in late], "scatter_shards", late)))
    got3 = [got3_0, dict(zip(ALL_ITEMS, grads[0]["rode"]))]
    bufs = []
    for t, (nm, (r, c), _) in enumerate(SHARD_ITEMS):
        buf = lax.empty((2, r, c), F32)
        for l, ps in ((1, ps1), (0, ps0)):
            buf = _sum4_t(ps[t], got3[l][t], buf, place_idx, l, nm, r, f"sum4_{l}_{nm}")
        bufs.append(buf)
    bufs = _join_layer(bufs, "join_halves")
    gs = _sum8(slots, "sum8")

    g_out = {nm: buf for (nm, _, _), buf in zip(SHARD_ITEMS, bufs)}
    g_out["meta_tokens"] = lax.dynamic_slice(gs[0:128].reshape(N_META, D), (0, shard * 256), (N_META, 256))
    off = 128
    for nm, rows in SMALL_ITEMS:
        n = w[nm].size
        g_out[nm] = gs[off:off + rows].reshape(-1)[:n].reshape(w[nm].shape)
        off += rows

    delta, new_m, new_v = {}, {}, {}
    for nm, _, _ in SHARD_ITEMS:
        if nm == "w_in":
            tr_ = lambda a: jnp.swapaxes(a, 1, 2)
            delta[nm], new_m[nm], new_v[nm] = (tr_(a) for a in _adamw3(tr_(w[nm]), tr_(g_out[nm]), tr_(m[nm]), tr_(v[nm]),
                                                                        f"adamw_{nm}"))
        else:
            delta[nm], new_m[nm], new_v[nm] = _adamw3(w[nm], g_out[nm], m[nm], v[nm], f"adamw_{nm}")
    small_names = ["meta_tokens"] + [nm for nm, _ in SMALL_ITEMS]
    small_rows = [META_ROWS] + [rows for _, rows in SMALL_ITEMS]

    def pack_small(src):
        buf = jnp.concatenate([_rows128(src[nm], rows) for nm, rows in zip(small_names, small_rows)], axis=0)
        return jnp.pad(buf, ((0, SMALL_ADAM_ROWS - buf.shape[0]), (0, 0)))

    d_, m_, v_ = (a[0] for a in _adamw3(pack_small(w)[None], pack_small(g_out)[None], pack_small(m)[None],
                                        pack_small(v)[None], "adamw_small"))
    off = 0
    for nm, rows in zip(small_names, small_rows):
        n = w[nm].size
        for dst, src in ((delta, d_), (new_m, m_), (new_v, v_)):
            dst[nm] = src[off:off + rows].reshape(-1)[:n].reshape(w[nm].shape)
        off += rows

    return (loss, grad_x, *[g_out[n] for n in names], *[delta[n] for n in names],
            *[new_m[n] for n in names], *[new_v[n] for n in names])
```

```python
import math

import numpy as np
import jax
import jax.numpy as jnp
from jax import lax
from jax.experimental import pallas as pl
from jax.experimental.pallas import tpu as pltpu

D = 1024
F = 2816
FT = F // 2
HD = 64
NPAIR = 4
N_META = 16
BLK = 128
PAD = BLK - N_META
EPS = 1e-6
NEG = -1e30
N_BUCKETS = 32
GA, GB, QA, KA, VA, QB, KB, VB, FA, DP = 0, 1024, 2048, 2560, 3072, 3584, 4096, 4224, 4352, 4480
D_IN = 4360
CDT = jnp.bfloat16
F32 = jnp.float32
VMEM_LIMIT = 48 * 1024 * 1024
MESH_ID = pl.DeviceIdType.MESH

ADAM_LR, ADAM_B1, ADAM_B2, ADAM_EPS, ADAM_WD, ADAM_STEP = 0.001, 0.9, 0.999, 1e-08, 0.01, 10

SHARD_ITEMS = (
    ("ffn1_w_in", (1024, 1408), "col"),
    ("ffn1_w_out", (704, 1024), "row"),
    ("w_in", (1024, 1090), "col"),
    ("w_branch_fox", (512, 256), "col"),
    ("w_branch_swa", (512, 256), "col"),
    ("w_out", (256, 1024), "row"),
    ("ffn2_w_in", (1024, 1408), "col"),
    ("ffn2_w_out", (704, 1024), "row"),
)
SMALL_ROWS = 192
META_ROWS = 32


def _row_tile(t):
    return 384 if t % 384 == 0 else 128


def _dot(a, b):
    return jnp.dot(a, b, preferred_element_type=F32)


def _dot_nt(a, b):
    return lax.dot_general(a, b, (((1,), (1,)), ((), ())), preferred_element_type=F32)


def _dot_hi(a, b):
    return jnp.dot(a, b, preferred_element_type=F32, precision=lax.Precision.HIGHEST)


def _sigmoid(x):
    return 0.5 * jnp.tanh(0.5 * x) + 0.5


def _iota(shape, dim):
    return lax.broadcasted_iota(jnp.int32, shape, dim)


def _params(sem=None):
    return pltpu.CompilerParams(dimension_semantics=sem, vmem_limit_bytes=VMEM_LIMIT)


def _sds(shape, dtype):
    return jax.ShapeDtypeStruct(shape, dtype)


def _rms_fwd(h, g, name):
    t = h.shape[0]
    tm = _row_tile(t)

    def body(h_ref, g_ref, a_ref, at_ref):
        x = h_ref[...]
        ms = jnp.mean(x * x, axis=-1, keepdims=True)
        a = x * lax.rsqrt(ms + EPS) * g_ref[...]
        a_ref[...] = a.astype(CDT)
        at_ref[...] = a.T.astype(CDT)

    return pl.pallas_call(
        body, name=name, grid=(t // tm,),
        in_specs=[pl.BlockSpec((tm, D), lambda i: (i, 0)), pl.BlockSpec((1, D), lambda i: (0, 0))],
        out_specs=[pl.BlockSpec((tm, D), lambda i: (i, 0)), pl.BlockSpec((D, tm), lambda i: (0, i))],
        out_shape=[_sds((t, D), CDT), _sds((D, t), CDT)],
        compiler_params=_params(("parallel",)),
    )(h, g)


def _ffn_in(a, w_in, name, ride=None):
    t = a.shape[0]
    tm = _row_tile(t)
    tn = FT
    nj = F // tn
    grid = (nj, t // tm)
    ride_in, ride_in_specs, ride_out, ride_out_specs, ride_sems = _ride_specs(ride)

    def body(a_ref, wg_ref, wu_ref, gu_ref, s_ref, st_ref):
        a_ = a_ref[...]
        g = _dot(a_, wg_ref[...])
        u = _dot(a_, wu_ref[...])
        s = g * _sigmoid(g) * u
        gu_ref[0] = g.astype(CDT)
        gu_ref[1] = u.astype(CDT)
        s_ref[...] = s.astype(CDT)
        st_ref[...] = s.T.astype(CDT)

    res = pl.pallas_call(
        _riding(body, 3, 3, ride, grid), name=name, grid=grid,
        in_specs=[pl.BlockSpec((tm, D), lambda j, i: (i, 0)),
                  pl.BlockSpec((D, tn), lambda j, i: (0, j)),
                  pl.BlockSpec((D, tn), lambda j, i: (0, j + nj))] + ride_in_specs,
        out_specs=[pl.BlockSpec((2, tm, tn), lambda j, i: (0, i, j)),
                   pl.BlockSpec((tm, tn), lambda j, i: (i, j)),
                   pl.BlockSpec((tn, tm), lambda j, i: (j, i))] + ride_out_specs,
        out_shape=[_sds((2, t, F), CDT), _sds((t, F), CDT), _sds((F, t), CDT)] + ride_out, scratch_shapes=ride_sems,
        compiler_params=_params(("arbitrary", "arbitrary") if ride else ("parallel", "parallel")),
    )(a, w_in, w_in, *ride_in)
    return (*res[:3], res[3:]) if ride else res


def _mm_res(a, b, res, scale, name, ride=None):
    t, k = a.shape
    n = b.shape[1]
    tm = _row_tile(t)
    tn = n
    grid = (t // tm, n // tn)
    ride_in, ride_in_specs, ride_out, ride_out_specs, ride_sems = _ride_specs(ride)

    def body(a_ref, b_ref, r_ref, o_ref):
        o_ref[...] = r_ref[...] + scale * _dot(a_ref[...], b_ref[...])

    out = pl.pallas_call(
        _riding(body, 3, 1, ride, grid), name=name, grid=grid,
        in_specs=[pl.BlockSpec((tm, k), lambda i, j: (i, 0)),
                  pl.BlockSpec((k, tn), lambda i, j: (0, j)),
                  pl.BlockSpec((tm, tn), lambda i, j: (i, j))] + ride_in_specs,
        out_specs=[pl.BlockSpec((tm, tn), lambda i, j: (i, j))] + ride_out_specs,
        out_shape=[_sds((t, n), F32)] + ride_out, scratch_shapes=ride_sems,
        compiler_params=_params(("arbitrary", "arbitrary") if ride else ("parallel", "parallel")),
    )(a, b, res, *ride_in)
    return (out[0], out[1:]) if ride else out[0]


def _mm(a, b, out_dtype, tm, tn, name, scale=1.0, ride=None):
    m, k = a.shape
    if b.ndim == 3:
        nh = b.shape[2] // tn
        n = 2 * b.shape[2]
        b_spec = pl.BlockSpec((None, k, tn), lambda i, j: (j // nh, 0, j % nh))
    else:
        n = b.shape[1]
        b_spec = pl.BlockSpec((k, tn), lambda i, j: (0, j))
    grid = (m // tm, n // tn)
    ride_in, ride_in_specs, ride_out, ride_out_specs, ride_sems = _ride_specs(ride)

    def body(a_ref, b_ref, o_ref):
        o_ref[...] = (scale * _dot(a_ref[...], b_ref[...])).astype(out_dtype)

    res = pl.pallas_call(
        _riding(body, 2, 1, ride, grid), name=name, grid=grid,
        in_specs=[pl.BlockSpec((tm, k), lambda i, j: (i, 0)), b_spec] + ride_in_specs,
        out_specs=[pl.BlockSpec((tm, tn), lambda i, j: (i, j))] + ride_out_specs,
        out_shape=[_sds((m, n), out_dtype)] + ride_out, scratch_shapes=ride_sems,
        compiler_params=_params(("arbitrary", "arbitrary") if ride else ("parallel", "parallel")),
    )(a, b, *ride_in)
    return (res[0], res[1:]) if ride else res[0]


def _mm_nt(a, b, name, with_t=False):
    m, n = a.shape
    k = b.shape[0]
    tm = _row_tile(m)
    tk = k

    def body(a_ref, b_ref, o_ref, *t_ref):
        r = _dot_nt(a_ref[...], b_ref[...])
        o_ref[...] = r
        if with_t:
            t_ref[0][...] = r.T.astype(CDT)

    out_specs = [pl.BlockSpec((tm, tk), lambda i, j: (i, j))]
    out_shape = [_sds((m, k), F32)]
    if with_t:
        out_specs.append(pl.BlockSpec((tk, tm), lambda i, j: (j, i)))
        out_shape.append(_sds((k, m), CDT))
    res = pl.pallas_call(
        body, name=name, grid=(m // tm, k // tk),
        in_specs=[pl.BlockSpec((tm, n), lambda i, j: (i, 0)), pl.BlockSpec((tk, n), lambda i, j: (j, 0))],
        out_specs=out_specs, out_shape=out_shape,
        compiler_params=_params(("parallel", "parallel")),
    )(a, b)
    return res if with_t else res[0]


def _ffn_bwd_mid(dhb, w_out, gu, name, ride=None):
    t = dhb.shape[0]
    tm = _row_tile(t)
    tn = FT
    grid = (F // tn, t // tm)
    ride_in, ride_in_specs, ride_out, ride_out_specs, ride_sems = _ride_specs(ride)

    def body(dh_ref, w_ref, gu_ref, o_ref):
        ds = _dot_nt(dh_ref[...] * 0.5, w_ref[...])
        g = gu_ref[0].astype(F32)
        u = gu_ref[1].astype(F32)
        sg = _sigmoid(g)
        o_ref[0] = (ds * u * (sg * (1.0 + g * (1.0 - sg)))).astype(CDT)
        o_ref[1] = (ds * (g * sg)).astype(CDT)

    res = pl.pallas_call(
        _riding(body, 3, 1, ride, grid), name=name, grid=grid,
        in_specs=[pl.BlockSpec((tm, D), lambda j, i: (i, 0)),
                  pl.BlockSpec((tn, D), lambda j, i: (j, 0)),
                  pl.BlockSpec((2, tm, tn), lambda j, i: (0, i, j))] + ride_in_specs,
        out_specs=[pl.BlockSpec((2, tm, tn), lambda j, i: (0, i, j))] + ride_out_specs,
        out_shape=[_sds((2, t, F), CDT)] + ride_out, scratch_shapes=ride_sems,
        compiler_params=_params(("arbitrary", "arbitrary") if ride else ("parallel", "parallel")),
    )(dhb, w_out, gu, *ride_in)
    return (res[0], res[1:]) if ride else res[0]


def _rms_bwd_rows(da_, x, g, dres, i, dh_ref, dhb_ref, dg_ref):
    r = lax.rsqrt(jnp.mean(x * x, axis=-1, keepdims=True) + EPS)
    xh = x * r
    day = da_ * g
    dh = dres + r * (day - xh * jnp.mean(day * xh, axis=-1, keepdims=True))
    dh_ref[...] = dh
    dhb_ref[...] = dh.astype(CDT)

    @pl.when(i == 0)
    def _():
        dg_ref[...] = jnp.zeros(dg_ref.shape, F32)

    dg_ref[0:1, :] += jnp.sum(da_ * xh, axis=0, keepdims=True)


def _ffn_bwd_in(dgu, w_in, h, g, dres, name, ride=None):
    t = dgu.shape[1]
    tm = _row_tile(t)
    grid = (t // tm,)
    ride_in, ride_in_specs, ride_out, ride_out_specs, ride_sems = _ride_specs(ride)

    def body(dg_ref, wg_ref, wu_ref, h_ref, g_ref, dr_ref, dh_ref, dhb_ref, dgn_ref):
        da_ = _dot_nt(dg_ref[0], wg_ref[...]) + _dot_nt(dg_ref[1], wu_ref[...])
        _rms_bwd_rows(da_, h_ref[...], g_ref[...], dr_ref[...], pl.program_id(0), dh_ref, dhb_ref, dgn_ref)

    row = pl.BlockSpec((tm, D), lambda i: (i, 0))
    res = pl.pallas_call(
        _riding(body, 6, 3, ride, grid), name=name, grid=grid,
        in_specs=[pl.BlockSpec((2, tm, F), lambda i: (0, i, 0)),
                  pl.BlockSpec((D, F), lambda i: (0, 0)),
                  pl.BlockSpec((D, F), lambda i: (0, 1)),
                  row, pl.BlockSpec((1, D), lambda i: (0, 0)), row] + ride_in_specs,
        out_specs=[row, row, pl.BlockSpec((8, D), lambda i: (0, 0))] + ride_out_specs,
        out_shape=[_sds((t, D), F32), _sds((t, D), CDT), _sds((8, D), F32)] + ride_out, scratch_shapes=ride_sems,
        compiler_params=_params(("arbitrary",)),
    )(dgu, w_in, w_in, h, g, dres, *ride_in)
    return (*res[:3], res[3:]) if ride else res


def _mm_nt_rms(a, b, h, g, dres, name):
    t, n = a.shape
    tm = _row_tile(t)

    def body(a_ref, b_ref, h_ref, g_ref, dr_ref, dh_ref, dhb_ref, dgn_ref):
        da_ = _dot_nt(a_ref[...], b_ref[...])
        _rms_bwd_rows(da_, h_ref[...], g_ref[...], dr_ref[...], pl.program_id(0), dh_ref, dhb_ref, dgn_ref)

    row = pl.BlockSpec((tm, D), lambda i: (i, 0))
    return pl.pallas_call(
        body, name=name, grid=(t // tm,),
        in_specs=[pl.BlockSpec((tm, n), lambda i: (i, 0)), pl.BlockSpec((D, n), lambda i: (0, 0)),
                  row, pl.BlockSpec((1, D), lambda i: (0, 0)), row],
        out_specs=[row, row, pl.BlockSpec((8, D), lambda i: (0, 0))],
        out_shape=[_sds((t, D), F32), _sds((t, D), CDT), _sds((8, D), F32)],
        compiler_params=_params(("arbitrary",)),
    )(a, b, h, g, dres)


def _loss(h, target, name):
    t = h.shape[0]

    def body(h_ref, t_ref, dh_ref, dhb_ref, l_ref):
        i = pl.program_id(0)

        @pl.when(i == 0)
        def _():
            l_ref[...] = jnp.zeros(l_ref.shape, F32)
            dh_ref[...] = jnp.zeros(dh_ref.shape, F32)
            dhb_ref[...] = jnp.zeros(dhb_ref.shape, CDT)

        @pl.when(i > 0)
        def _():
            err = h_ref[...] - t_ref[...]
            l_ref[...] += (0.5 / D) * jnp.sum(err * err)
            d = err * (1.0 / D)
            dh_ref[...] = d
            dhb_ref[...] = d.astype(CDT)

    row = pl.BlockSpec((BLK, D), lambda i: (i, 0))
    return pl.pallas_call(
        body, name=name, grid=(t // BLK,),
        in_specs=[row, pl.BlockSpec((BLK, D), lambda i: (jnp.maximum(i - 1, 0), 0))],
        out_specs=[row, row, pl.BlockSpec((8, 128), lambda i: (0, 0))],
        out_shape=[_sds((t, D), F32), _sds((t, D), CDT), _sds((8, 128), F32)],
        compiler_params=_params(("arbitrary",)),
    )(h, target)


def _block_diag():
    return (_iota((128, 128), 0) // HD == _iota((128, 128), 1) // HD).astype(F32)


def _head_sums(v, bd):
    hi = v.astype(CDT)
    rest = (v - hi.astype(F32)).astype(CDT)
    b = bd.astype(CDT)
    return _dot(hi, b) + _dot(rest, b)


def _dup_halves(x, lo):
    sw = pltpu.roll(x, 64, 1)
    return jnp.where(lo, x, sw), jnp.where(lo, sw, x)


def _qknorm_fwd(proj, gfq, gfk, gsq, gsk, fb, name):
    t = proj.shape[0]
    tm = _row_tile(t)

    def body(qa, ka, va, qb, kb, vb, fa, gfq_r, gfk_r, gsq_r, gsk_r, fb_r,
             qf_o, kf_o, vf_o, qs_o, kse_o, vse_o, c_o, ct_o, qft_o, carry):
        i = pl.program_id(0)
        bd = _block_diag()
        lane = _iota((1, 128), 1)
        lo = lane < HD

        def hnorm(x, g):
            ms = _head_sums(x * x, bd) * (1.0 / HD)
            return x * lax.rsqrt(ms + EPS) * g

        for ch in range(4):
            sl = slice(128 * ch, 128 * (ch + 1))
            qn = hnorm(qa[:, sl], gfq_r[:, sl]) * 0.125
            qf_o[:, sl] = qn.astype(CDT)
            qft_o[sl, :] = qn.T.astype(CDT)
            kf_o[:, sl] = hnorm(ka[:, sl], gfk_r[:, sl]).astype(CDT)
            qs_o[:, sl] = (hnorm(qb[:, sl], gsq_r[:, sl]) * 0.125).astype(CDT)
        vf_o[...] = va[...].astype(CDT)
        k0, k1 = _dup_halves(hnorm(kb[...], gsk_r[...]), lo)
        kse_o[0] = k0.astype(CDT)
        kse_o[1] = k1.astype(CDT)
        v0, v1 = _dup_halves(vb[...], lo)
        vse_o[0] = v0.astype(CDT)
        vse_o[1] = v1.astype(CDT)

        z = fa[...] + fb_r[...]
        lf = jnp.minimum(z, 0.0) - jnp.log(1.0 + jnp.exp(-jnp.abs(z)))
        lf = jnp.where(lane < 8, lf, 0.0)
        ltri = (_iota((tm, tm), 1) <= _iota((tm, tm), 0)).astype(F32)

        @pl.when(i == 0)
        def _():
            carry[...] = jnp.zeros(carry.shape, F32)

        c = _dot_hi(ltri, lf) + carry[0:1, :]
        carry[0:1, :] = c[tm - 1:tm, :]
        c_o[...] = c
        ct_o[...] = c.T[0:8, :]

    def col(width, off):
        return pl.BlockSpec((tm, width), lambda i: (i, off // width))

    def vec(width):
        return pl.BlockSpec((1, width), lambda i: (0, 0))

    return pl.pallas_call(
        body, name=name, grid=(t // tm,),
        in_specs=[col(512, QA), col(512, KA), col(512, VA), col(512, QB), col(128, KB), col(128, VB), col(128, FA),
                  vec(512), vec(512), vec(512), vec(128), vec(128)],
        out_specs=[pl.BlockSpec((tm, 512), lambda i: (i, 0))] * 4
        + [pl.BlockSpec((2, tm, 128), lambda i: (0, i, 0))] * 2
        + [pl.BlockSpec((tm, 128), lambda i: (i, 0)), pl.BlockSpec((8, tm), lambda i: (0, i)),
           pl.BlockSpec((512, tm), lambda i: (0, i))],
        out_shape=[_sds((t, 512), CDT)] * 4 + [_sds((2, t, 128), CDT)] * 2
        + [_sds((t, 128), F32), _sds((8, t), F32), _sds((512, t), CDT)],
        scratch_shapes=[pltpu.VMEM((8, 128), F32)],
        compiler_params=_params(("arbitrary",)),
    )(proj, proj, proj, proj, proj, proj, proj, gfq, gfk, gsq, gsk, fb)


def _qknorm_bwd(proj, dqf, dkf, dvf, dqs, dkse, dvse, dcq, dck, dga, dgb, gfq, gfk, gsq, gsk, fb, name):
    t = proj.shape[0]
    tm = _row_tile(t)
    nt = t // tm

    def body(qa, ka, qb, kb, fa, dqf_r, dkf_r, dvf_r, dqs_r, dkse_r, dvse_r, dcq_r, dck_r, dga_r, dgb_r,
             gfq_r, gfk_r, gsq_r, gsk_r, fb_r, dp_o, dgn_o, carry, acc):
        i = pl.program_id(0)
        bd = _block_diag()
        lane = _iota((1, 128), 1)
        lo = lane < HD

        @pl.when(i == 0)
        def _():
            carry[...] = jnp.zeros(carry.shape, F32)
            acc[...] = jnp.zeros(acc.shape, F32)

        def hnorm_bwd(x, g, dy):
            r = lax.rsqrt(_head_sums(x * x, bd) * (1.0 / HD) + EPS)
            xh = x * r
            day = dy * g
            dx = r * (day - xh * (_head_sums(day * xh, bd) * (1.0 / HD)))
            return dx, jnp.sum(dy * xh, axis=0, keepdims=True)

        for ch in range(4):
            sl = slice(128 * ch, 128 * (ch + 1))
            dx, dg = hnorm_bwd(qa[:, sl], gfq_r[:, sl], dqf_r[:, sl] * 0.125)
            dp_o[:, QA + 128 * ch:QA + 128 * (ch + 1)] = dx.astype(CDT)
            acc[0:1, sl] += dg
            dx, dg = hnorm_bwd(ka[:, sl], gfk_r[:, sl], dkf_r[:, sl])
            dp_o[:, KA + 128 * ch:KA + 128 * (ch + 1)] = dx.astype(CDT)
            acc[1:2, sl] += dg
            dx, dg = hnorm_bwd(qb[:, sl], gsq_r[:, sl], dqs_r[:, sl] * 0.125)
            dp_o[:, QB + 128 * ch:QB + 128 * (ch + 1)] = dx.astype(CDT)
            acc[2:3, sl] += dg
        dp_o[:, VA:VA + 512] = dvf_r[...].astype(CDT)
        dp_o[:, GA:GA + D] = dga_r[...]
        dp_o[:, GB:GB + D] = dgb_r[...]

        def fold(x):
            e0 = x[0]
            e1 = x[1]
            return jnp.where(lo, e0 + pltpu.roll(e0, 64, 1), e1 + pltpu.roll(e1, 64, 1))

        dx, dg = hnorm_bwd(kb[...], gsk_r[...], fold(dkse_r))
        dp_o[:, KB:KB + 128] = dx.astype(CDT)
        acc[3:4, 0:128] += dg
        dp_o[:, VB:VB + 128] = fold(dvse_r).astype(CDT)

        rr = _iota((512, 128), 0)
        hh = _iota((512, 128), 1)
        sel = ((rr == (hh >> 1) * 128 + (hh & 1) * HD) & (hh < 8)).astype(F32)
        dcs = _dot_hi(dcq_r[...] - dck_r[...], sel)
        utri = (_iota((tm, tm), 1) >= _iota((tm, tm), 0)).astype(F32)
        dlf = _dot_hi(utri, dcs) + carry[0:1, :]
        carry[0:1, :] = dlf[0:1, :]
        z = fa[...] + fb_r[...]
        dfa = jnp.where(lane < 8, dlf * _sigmoid(-z), 0.0)
        dp_o[:, FA:FA + 128] = dfa.astype(CDT)
        acc[4:5, 0:128] += jnp.sum(dfa, axis=0, keepdims=True)

        @pl.when(i == nt - 1)
        def _():
            foldm = ((_iota((512, 128), 0) & (HD - 1)) == _iota((512, 128), 1)).astype(F32)
            dgn_o[...] = _dot_hi(acc[...], foldm)

    def col(width, off):
        return pl.BlockSpec((tm, width), lambda i: (nt - 1 - i, off // width))

    def rows(width):
        return pl.BlockSpec((tm, width), lambda i: (nt - 1 - i, 0))

    def vec(width):
        return pl.BlockSpec((1, width), lambda i: (0, 0))

    pair = pl.BlockSpec((2, tm, 128), lambda i: (0, nt - 1 - i, 0))
    return pl.pallas_call(
        body, name=name, grid=(nt,),
        in_specs=[col(512, QA), col(512, KA), col(512, QB), col(128, KB), col(128, FA),
                  rows(512), rows(512), rows(512), rows(512), pair, pair, rows(512), rows(512), rows(D), rows(D),
                  vec(512), vec(512), vec(512), vec(128), vec(128)],
        out_specs=[rows(DP), pl.BlockSpec((8, 128), lambda i: (0, 0))],
        out_shape=[_sds((t, DP), CDT), _sds((8, 128), F32)],
        scratch_shapes=[pltpu.VMEM((8, 128), F32), pltpu.VMEM((8, 512), F32)],
        compiler_params=_params(("arbitrary",)),
    )(proj, proj, proj, proj, proj, dqf, dkf, dvf, dqs, dkse, dvse, dcq, dck, dga, dgb, gfq, gfk, gsq, gsk, fb)


def _gate_out_fwd(ofox, oswa, wbf, wbs, proj, w_out, h, name):
    t = ofox.shape[0]
    tm = _row_tile(t)

    def body(of_r, os_r, wf_r, ws_r, ga_r, gb_r, wo_r, h_r, ho_o, yt_o, pf_o, ps_o, oft_o, ost_o):
        pf = _dot(of_r[...], wf_r[...])
        ps = _dot(os_r[...], ws_r[...])
        y = _sigmoid(ga_r[...]) * pf + _sigmoid(gb_r[...]) * ps
        ho_o[...] = h_r[...] + _dot(y.astype(CDT), wo_r[...])
        yt_o[...] = y.T.astype(CDT)
        pf_o[...] = pf.astype(CDT)
        ps_o[...] = ps.astype(CDT)
        oft_o[...] = of_r[...].astype(F32).T.astype(CDT)
        ost_o[...] = os_r[...].astype(F32).T.astype(CDT)

    row = pl.BlockSpec((tm, D), lambda i: (i, 0))
    half = pl.BlockSpec((tm, 512), lambda i: (i, 0))
    whole = lambda r: pl.BlockSpec((r, D), lambda i: (0, 0))
    tcol = lambda r: pl.BlockSpec((r, tm), lambda i: (0, i))
    return pl.pallas_call(
        body, name=name, grid=(t // tm,),
        in_specs=[half, half, whole(512), whole(512),
                  pl.BlockSpec((tm, D), lambda i: (i, GA // D)), pl.BlockSpec((tm, D), lambda i: (i, GB // D)),
                  whole(D), row],
        out_specs=[row, tcol(D), row, row, tcol(512), tcol(512)],
        out_shape=[_sds((t, D), F32), _sds((D, t), CDT), _sds((t, D), CDT), _sds((t, D), CDT),
                   _sds((512, t), CDT), _sds((512, t), CDT)],
        compiler_params=_params(("parallel",)),
    )(ofox, oswa, wbf, wbs, proj, proj, w_out, h)


def _gate_out_bwd(dhb, w_out, pf, ps, proj, name):
    t = dhb.shape[0]
    tm = _row_tile(t)

    def body(dh_r, wo_r, pf_r, ps_r, ga_r, gb_r, dpf_o, dps_o, dga_o, dgb_o):
        dy_ = _dot_nt(dh_r[...], wo_r[...])
        sa = _sigmoid(ga_r[...])
        sb = _sigmoid(gb_r[...])
        dpf_o[...] = (dy_ * sa).astype(CDT)
        dps_o[...] = (dy_ * sb).astype(CDT)
        dga_o[...] = (dy_ * pf_r[...].astype(F32) * (sa * (1.0 - sa))).astype(CDT)
        dgb_o[...] = (dy_ * ps_r[...].astype(F32) * (sb * (1.0 - sb))).astype(CDT)

    row = pl.BlockSpec((tm, D), lambda i: (i, 0))
    return pl.pallas_call(
        body, name=name, grid=(t // tm,),
        in_specs=[row, pl.BlockSpec((D, D), lambda i: (0, 0)), row, row,
                  pl.BlockSpec((tm, D), lambda i: (i, GA // D)), pl.BlockSpec((tm, D), lambda i: (i, GB // D))],
        out_specs=[row] * 4,
        out_shape=[_sds((t, D), CDT)] * 4,
        compiler_params=_params(("parallel",)),
    )(dhb, w_out, pf, ps, proj, proj)


def _tri_steps(n, by_key):
    if by_key:
        pairs = [(i, j) for j in range(n) for i in range(j, n)]
    else:
        pairs = [(i, j) for i in range(n) for j in range(i + 1)]
    return (np.array([p[0] for p in pairs], np.int32), np.array([p[1] for p in pairs], np.int32))


def _head_col(blk, lane, h):
    return jnp.sum(jnp.where(lane == h, blk, 0.0), axis=1, keepdims=True)


def _head_row(blk, sub, h):
    return jnp.sum(jnp.where(sub == h, blk, 0.0), axis=0, keepdims=True)


def _ride_specs(ride):
    if ride is None:
        return [], [], [], [], []
    kind, srcs, outs, layer, items = ride
    return list(srcs), [ANY] * len(srcs), list(outs), [ANY] * len(outs), _dma_sems(3 * len(srcs))


def _ride_start(ride, srcs, dsts, send_sems, recv_sems):
    for cp in _ici_copies(ride[0], srcs, dsts, send_sems, recv_sems, ride[3], recv=False, items=ride[4])[0]:
        cp.start()


def _ride_wait(ride, srcs, dsts, send_sems, recv_sems):
    sends, recvs = _ici_copies(ride[0], srcs, dsts, send_sems, recv_sems, ride[3], items=ride[4])
    for cp in recvs:
        cp.wait_recv()
    for cp in sends:
        cp.wait_send()


def _riding(body, n_in, n_out, ride, grid):
    if ride is None:
        return body
    nr = len(ride[1])

    def wrapped(*refs):
        ins, srcs = refs[:n_in], refs[n_in:n_in + nr]
        outs, dsts = refs[n_in + nr:n_in + nr + n_out], refs[n_in + nr + n_out:n_in + 2 * nr + n_out]
        scratch, sems = refs[n_in + 2 * nr + n_out:-2], refs[-2:]
        first = pl.program_id(0) == 0
        last = pl.program_id(0) == grid[0] - 1
        for a in range(1, len(grid)):
            first = first & (pl.program_id(a) == 0)
            last = last & (pl.program_id(a) == grid[a] - 1)

        @pl.when(first)
        def _():
            _ride_start(ride, srcs, dsts, *sems)

        body(*ins, *outs, *scratch)

        @pl.when(last)
        def _():
            _ride_wait(ride, srcs, dsts, *sems)

    return wrapped


def _fox_fwd(qf, kf, vf, c, ct, name, ride=None):
    t = qf.shape[0]
    ta = _row_tile(t)
    qi, kj = _tri_steps(t // ta, by_key=False)
    nsteps = len(qi)
    ride_in, ride_in_specs, ride_out, ride_out_specs, ride_sems = _ride_specs(ride)

    def body(qi_r, kj_r, q_r, k_r, v_r, c_r, ct_r, *rest):
        nr = len(ride_in)
        src_r, (o_o, lse_o), dst_o = rest[:nr], rest[nr:nr + 2], rest[nr + 2:2 * nr + 2]
        m_sc, l_sc, acc_sc, cq_sc, *sems = rest[2 * nr + 2:]
        p = pl.program_id(0)
        n = pl.program_id(1)
        i = qi_r[n]
        j = kj_r[n]
        lane = _iota((1, 128), 1)
        lo = lane < HD

        if ride is not None:
            @pl.when((p == 0) & (n == 0))
            def _():
                _ride_start(ride, src_r, dst_o, *sems)

        @pl.when(j == 0)
        def _():
            m_sc[...] = jnp.full(m_sc.shape, NEG, F32)
            l_sc[...] = jnp.zeros(l_sc.shape, F32)
            acc_sc[...] = jnp.zeros(acc_sc.shape, F32)
            for e in (0, 1):
                cq_sc[e] = jnp.broadcast_to(_head_col(c_r[...], lane, 2 * p + e), (ta, 128))

        def step(masked):
            q = q_r[...]
            k = k_r[...]
            vaug = jnp.concatenate([v_r[...], jnp.ones((ta, 128), CDT)], axis=1)
            if masked:
                rows = i * ta + _iota((ta, 1), 0)
                cols = j * ta + _iota((1, ta), 1)
                mask = (cols <= rows) & (cols >= PAD)
            sub = _iota((8, 1), 0)
            heads = (0, 1)
            sels = [lo, jnp.logical_not(lo)]
            s = [_dot_nt(jnp.where(sels[e], q, 0), k) for e in heads]
            ck = [_head_row(ct_r[...], sub, 2 * p + e) for e in heads]
            chunks = []
            for e in heads:
                cq = cq_sc[e]
                row = []
                for ch in range(ta // 128):
                    sl = slice(128 * ch, 128 * (ch + 1))
                    sc = s[e][:, sl] + cq - ck[e][:, sl]
                    if masked:
                        sc = jnp.where(mask[:, sl], sc, NEG)
                    row.append(sc)
                chunks.append(row)
            m_new, alphas = [], []
            for e in heads:
                mx = chunks[e][0]
                for sc in chunks[e][1:]:
                    mx = jnp.maximum(mx, sc)
                m_prev = m_sc[e]
                m_new.append(jnp.maximum(m_prev, jnp.max(mx, axis=1, keepdims=True)))
                alphas.append(jnp.exp(m_prev - m_new[e]))
            pe = [jnp.concatenate([jnp.exp(sc - m_new[e]).astype(CDT) for sc in chunks[e]], axis=1) for e in heads]
            pva = [_dot(pe[e], vaug) for e in heads]
            for e in heads:
                l_sc[e] = alphas[e] * l_sc[e] + pva[e][:, 128:]
                m_sc[e] = m_new[e]
            acc_sc[...] = (acc_sc[...] * jnp.where(lo, alphas[0], alphas[1])
                           + jnp.where(lo, pva[0][:, :128], pva[1][:, :128]))

        edge = (j == i) | (j == 0)

        @pl.when(edge)
        def _():
            step(True)

        @pl.when(jnp.logical_not(edge))
        def _():
            step(False)

        @pl.when(j == i)
        def _():
            l = jnp.where(lo, l_sc[0], l_sc[1])
            o_o[...] = (acc_sc[...] / l).astype(CDT)
            lse_o[...] = jnp.where(lo, m_sc[0], m_sc[1]) + jnp.log(l)

        if ride is not None:
            @pl.when((p == NPAIR - 1) & (n == nsteps - 1))
            def _():
                _ride_wait(ride, src_r, dst_o, *sems)

    qblk = pl.BlockSpec((ta, 128), lambda p, n, qi_r, kj_r: (qi_r[n], p))
    kblk = pl.BlockSpec((ta, 128), lambda p, n, qi_r, kj_r: (kj_r[n], p))
    grid_spec = pltpu.PrefetchScalarGridSpec(
        num_scalar_prefetch=2, grid=(NPAIR, nsteps),
        in_specs=[qblk, kblk, kblk,
                  pl.BlockSpec((ta, 128), lambda p, n, qi_r, kj_r: (qi_r[n], 0)),
                  pl.BlockSpec((8, ta), lambda p, n, qi_r, kj_r: (0, kj_r[n]))] + ride_in_specs,
        out_specs=[qblk, qblk] + ride_out_specs,
        scratch_shapes=[pltpu.VMEM((2, ta, 128), F32), pltpu.VMEM((2, ta, 128), F32), pltpu.VMEM((ta, 128), F32),
                        pltpu.VMEM((2, ta, 128), F32)] + ride_sems,
    )
    return pl.pallas_call(
        body, name=name, grid_spec=grid_spec,
        out_shape=[_sds((t, 512), CDT), _sds((t, 512), F32)] + ride_out,
        compiler_params=_params(("arbitrary", "arbitrary")),
    )(jnp.asarray(qi), jnp.asarray(kj), qf, kf, vf, c, ct, *ride_in)


def _fox_bwd(qf, qft, kf, vf, c, ct, o, lse, do, dot, name, ride=None):
    t = qf.shape[0]
    ta = _row_tile(t)
    nq = t // ta
    qi, kj = _tri_steps(nq, by_key=False)
    nsteps = len(qi)
    ride_in, ride_in_specs, ride_out, ride_out_specs, ride_sems = _ride_specs(ride)

    def body(qi_r, kj_r, q_r, qt_r, k_r, v_r, c_r, ct_r, o_r, lse_r, do_r, dot_r, *rest):
        nr = len(ride_in)
        src_r, (dq_o, dcq_o, dk_o, dv_o, dck_o), dst_o = rest[:nr], rest[nr:nr + 5], rest[nr + 5:2 * nr + 5]
        lse_sc, dl_sc, cq_sc, dq_sc, dcq_sc, dkt_sc, dvt_sc, dckt_sc, *sems = rest[2 * nr + 5:]
        p = pl.program_id(0)
        n = pl.program_id(1)
        i = qi_r[n]
        j = kj_r[n]
        lane = _iota((1, 128), 1)
        lo = lane < HD
        top = _iota((128, 1), 0) < HD

        if ride is not None:
            @pl.when((p == 0) & (n == 0))
            def _():
                _ride_start(ride, src_r, dst_o, *sems)

        @pl.when(n == 0)
        def _():
            dkt_sc[...] = jnp.zeros(dkt_sc.shape, F32)
            dvt_sc[...] = jnp.zeros(dvt_sc.shape, F32)
            dckt_sc[...] = jnp.zeros(dckt_sc.shape, F32)

        @pl.when(j == 0)
        def _():
            dq_sc[...] = jnp.zeros(dq_sc.shape, F32)
            dcq_sc[...] = jnp.zeros(dcq_sc.shape, F32)
            dd = do_r[...] * o_r[...].astype(F32)
            lse = lse_r[...]
            for e in (0, 1):
                sel = lo if e == 0 else jnp.logical_not(lo)
                cq_sc[e] = jnp.broadcast_to(_head_col(c_r[...], lane, 2 * p + e), (ta, 128))
                dl_sc[e] = jnp.broadcast_to(jnp.sum(jnp.where(sel, dd, 0.0), axis=1, keepdims=True), (ta, 128))
                lse_sc[e] = jnp.broadcast_to(lse[:, HD * e:HD * e + 1], (ta, 128))

        def step(masked):
            q = q_r[...]
            qt = qt_r[...]
            k = k_r[...]
            v = v_r[...]
            dob = do_r[...].astype(CDT)
            dot_ = dot_r[...]
            ones = jnp.ones((ta, 128), CDT)
            ones16 = jnp.ones((16, ta), CDT)
            if masked:
                rows = i * ta + _iota((ta, 1), 0)
                cols = j * ta + _iota((1, ta), 1)
                mask = (cols <= rows) & (cols >= PAD)
            sub = _iota((8, 1), 0)
            for e in (0, 1):
                sel = lo if e == 0 else jnp.logical_not(lo)
                rsel = top if e == 0 else jnp.logical_not(top)
                s = _dot_nt(jnp.where(sel, q, 0), k)
                dp = _dot_nt(jnp.where(sel, dob, 0), v)
                ck = _head_row(ct_r[...], sub, 2 * p + e)
                cq, lse_e, dl = cq_sc[e], lse_sc[e], dl_sc[e]
                prs, dss = [], []
                for ch in range(ta // 128):
                    sl = slice(128 * ch, 128 * (ch + 1))
                    sc = s[:, sl] + cq - ck[:, sl]
                    if masked:
                        sc = jnp.where(mask[:, sl], sc, NEG)
                    pr = jnp.exp(sc - lse_e)
                    prs.append(pr.astype(CDT))
                    dss.append((pr * (dp[:, sl] - dl)).astype(CDT))
                pb = jnp.concatenate(prs, axis=1)
                dsb = jnp.concatenate(dss, axis=1)
                dvt_sc[j] += _dot(jnp.where(rsel, dot_, 0), pb)
                dkc = _dot(jnp.concatenate([jnp.where(rsel, qt, 0), ones16], axis=0), dsb)
                dkt_sc[j] += dkc[0:128]
                dckt_sc[j, 0:8, :] += jnp.where(sub == e, dkc[128:136], 0.0)
                dqa = _dot(dsb, jnp.concatenate([jnp.where(sel, k, 0), ones], axis=1))
                dq_sc[...] += dqa[:, :128]
                dcq_sc[e] += dqa[:, 128:]

        edge = (j == i) | (j == 0)

        @pl.when(edge)
        def _():
            step(True)

        @pl.when(jnp.logical_not(edge))
        def _():
            step(False)

        @pl.when(j == i)
        def _():
            dq_o[...] = dq_sc[...]
            dcq_o[...] = jnp.where(lo, dcq_sc[0], dcq_sc[1])

        @pl.when(n == nsteps - 1)
        def _():
            spread = (_iota((128, 128), 1) == _iota((128, 128), 0) // HD).astype(F32)
            for jb in range(nq):
                rs = slice(jb * ta, (jb + 1) * ta)
                dk_o[rs, :] = dkt_sc[jb].T
                dv_o[rs, :] = dvt_sc[jb].T
                dck_o[rs, :] = _dot_hi(spread, dckt_sc[jb]).T

        if ride is not None:
            @pl.when((p == NPAIR - 1) & (n == nsteps - 1))
            def _():
                _ride_wait(ride, src_r, dst_o, *sems)

    qblk = pl.BlockSpec((ta, 128), lambda p, n, qi_r, kj_r: (qi_r[n], p))
    qtblk = pl.BlockSpec((128, ta), lambda p, n, qi_r, kj_r: (p, qi_r[n]))
    kblk = pl.BlockSpec((ta, 128), lambda p, n, qi_r, kj_r: (kj_r[n], p))
    whole = pl.BlockSpec((t, 128), lambda p, n, qi_r, kj_r: (0, p))
    grid_spec = pltpu.PrefetchScalarGridSpec(
        num_scalar_prefetch=2, grid=(NPAIR, nsteps),
        in_specs=[qblk, qtblk, kblk, kblk,
                  pl.BlockSpec((ta, 128), lambda p, n, qi_r, kj_r: (qi_r[n], 0)),
                  pl.BlockSpec((8, ta), lambda p, n, qi_r, kj_r: (0, kj_r[n])),
                  qblk, qblk, qblk, qtblk] + ride_in_specs,
        out_specs=[qblk, qblk, whole, whole, whole] + ride_out_specs,
        scratch_shapes=[pltpu.VMEM((2, ta, 128), F32)] * 3 + [pltpu.VMEM((ta, 128), F32), pltpu.VMEM((2, ta, 128), F32)]
        + [pltpu.VMEM((nq, 128, ta), F32)] * 3 + ride_sems,
    )
    return pl.pallas_call(
        body, name=name, grid_spec=grid_spec,
        out_shape=[_sds((t, 512), F32)] * 5 + ride_out,
        compiler_params=_params(("arbitrary", "arbitrary")),
    )(jnp.asarray(qi), jnp.asarray(kj), qf, qft, kf, vf, c, ct, o, lse, do, dot, *ride_in)


def _bucket_table():
    r = np.arange(BLK)[:, None]
    c = np.arange(3 * BLK)[None, :]
    d = np.where(c < BLK, r + BLK - c, r - (c - BLK))
    n = np.maximum(d, 0)
    max_exact = N_BUCKETS // 2
    nf = np.maximum(n, 1).astype(np.float32)
    large = max_exact + (np.log(nf / max_exact) / math.log(BLK / max_exact) * (N_BUCKETS - max_exact)).astype(np.int32)
    large = np.minimum(large, N_BUCKETS - 1)
    b = np.where(n < max_exact, n, large)
    return np.where(c < 2 * BLK, b, N_BUCKETS - 1).astype(np.int32)


def _bias_fwd(table, name):
    bucket = jnp.asarray(_bucket_table())

    def body(tab_r, b_r, o_o):
        h = pl.program_id(0)
        b = b_r[...]
        acc = jnp.zeros(b.shape, F32)
        for k in range(N_BUCKETS):
            acc = jnp.where(b == k, tab_r[k, h], acc)
        o_o[...] = acc

    return pl.pallas_call(
        body, name=name, grid=(8,),
        in_specs=[pl.BlockSpec(memory_space=pltpu.SMEM), pl.BlockSpec((BLK, 3 * BLK), lambda h: (0, 0))],
        out_specs=pl.BlockSpec((None, BLK, 3 * BLK), lambda h: (h, 0, 0)),
        out_shape=_sds((8, BLK, 3 * BLK), F32),
        compiler_params=_params(("parallel",)),
    )(table, bucket)


def _bias_bwd(dbias, name):
    bucket = jnp.asarray(_bucket_table())

    def body(d_r, b_r, o_o):
        h = pl.program_id(0)
        b = b_r[...]
        d = d_r[...]
        lane = _iota((1, 128), 1)
        row = jnp.zeros((1, 128), F32)
        for k in range(N_BUCKETS):
            row = jnp.where(lane == k, jnp.sum(jnp.where(b == k, d, 0.0)), row)
        o_o[pl.ds(h, 1), :] = row

    return pl.pallas_call(
        body, name=name, grid=(8,),
        in_specs=[pl.BlockSpec((None, BLK, 3 * BLK), lambda h: (h, 0, 0)), pl.BlockSpec((BLK, 3 * BLK), lambda h: (0, 0))],
        out_specs=pl.BlockSpec((8, 128), lambda h: (0, 0)),
        out_shape=_sds((8, 128), F32),
        compiler_params=_params(("arbitrary",)),
    )(dbias, bucket)


def _swa_valid(i):
    r = _iota((BLK, 1), 0)
    c = _iota((1, 3 * BLK), 1)
    prev = (c < BLK) & (c > r) & (i >= 1) & ((i - 1) * BLK + c >= PAD)
    cc = c - BLK
    cur = (c >= BLK) & (c < 2 * BLK) & (cc <= r) & (i * BLK + cc >= PAD)
    cm = c - 2 * BLK
    meta = (c >= 2 * BLK) & (cm >= PAD) & (i * BLK + r - cm >= BLK)
    return prev | cur | meta


def _swa_kv_specs(ta):
    nb = ta // BLK
    return [pl.BlockSpec((None, BLK, 128), lambda p, i: (p // 2, jnp.maximum(i * nb - 1, 0), 0)),
            pl.BlockSpec((None, ta, 128), lambda p, i: (p // 2, i, 0)),
            pl.BlockSpec((None, BLK, 128), lambda p, i: (p // 2, 0, 0))]


def _swa_fwd(qs, kse, vse, bias, sinks, name, ride=None):
    t = qs.shape[0]
    ta = _row_tile(t)
    nb = ta // BLK
    grid = (NPAIR, t // ta)
    ride_in, ride_in_specs, ride_out, ride_out_specs, ride_sems = _ride_specs(ride)

    def body(sink_r, q_r, kp_r, kc_r, km_r, vp_r, vc_r, vm_r, b_r, o_o, lse_o):
        p = pl.program_id(0)
        i = pl.program_id(1)
        lo = _iota((1, 128), 1) < HD
        k4 = jnp.concatenate([kp_r[...], kc_r[...]], axis=0)
        v4 = jnp.concatenate([vp_r[...], vc_r[...]], axis=0)
        work = [(b, e) for b in range(nb) for e in (0, 1)]
        sinks = [sink_r[2 * p + e] for e in (0, 1)]
        v3 = [jnp.concatenate([v4[BLK * b:BLK * (b + 2)], vm_r[...]], axis=0) for b in range(nb)]
        s = {}
        for b in range(nb):
            q = q_r[BLK * b:BLK * (b + 1), :]
            k3 = jnp.concatenate([k4[BLK * b:BLK * (b + 2)], km_r[...]], axis=0)
            valid = _swa_valid(i * nb + b)
            for e in (0, 1):
                sel = lo if e == 0 else jnp.logical_not(lo)
                s[b, e] = jnp.where(valid, _dot_nt(jnp.where(sel, q, 0), k3) + b_r[e], NEG)
        mx = {w: jnp.maximum(jnp.max(s[w], axis=1, keepdims=True), sinks[w[1]]) for w in work}
        pe = {w: jnp.exp(s[w] - mx[w]) for w in work}
        den = {w: jnp.sum(pe[w], axis=1, keepdims=True) + jnp.exp(sinks[w[1]] - mx[w]) for w in work}
        out = {w: _dot(pe[w].astype(CDT), v3[w[0]]) / den[w] for w in work}
        for b in range(nb):
            rows = slice(BLK * b, BLK * (b + 1))
            o_o[rows, :] = jnp.where(lo, out[b, 0], out[b, 1]).astype(CDT)
            lse_o[rows, :] = jnp.where(lo, mx[b, 0] + jnp.log(den[b, 0]), mx[b, 1] + jnp.log(den[b, 1]))

    qblk = pl.BlockSpec((ta, 128), lambda p, i: (i, p))
    res = pl.pallas_call(
        _riding(body, 9, 2, ride, grid), name=name, grid=grid,
        in_specs=[pl.BlockSpec(memory_space=pltpu.SMEM), qblk] + _swa_kv_specs(ta) + _swa_kv_specs(ta)
        + [pl.BlockSpec((2, BLK, 3 * BLK), lambda p, i: (p, 0, 0))] + ride_in_specs,
        out_specs=[qblk, qblk] + ride_out_specs,
        out_shape=[_sds((t, 512), CDT), _sds((t, 512), F32)] + ride_out, scratch_shapes=ride_sems,
        compiler_params=_params(("arbitrary", "arbitrary") if ride else ("parallel", "parallel")),
    )(sinks, qs, kse, kse, kse, vse, vse, vse, bias, *ride_in)
    return (res[0], res[1], res[2:]) if ride else res


def _swa_bwd(qs, kse, vse, bias, sinks, o, lse, do, name):
    t = qs.shape[0]
    ta = _row_tile(t)
    nb = ta // BLK

    def body(sink_r, q_r, kp_r, kc_r, km_r, vp_r, vc_r, vm_r, b_r, o_r, lse_r, do_r,
             dq_o, dk_o, dv_o, db_o, dsk_o):
        p = pl.program_id(0)
        i = pl.program_id(1)
        lo = _iota((1, 128), 1) < HD

        @pl.when((i == 0) & (p % 2 == 0))
        def _():
            dk_o[...] = jnp.zeros(dk_o.shape, F32)
            dv_o[...] = jnp.zeros(dv_o.shape, F32)

        @pl.when(i == 0)
        def _():
            db_o[...] = jnp.zeros(db_o.shape, F32)
            dsk_o[...] = jnp.zeros(dsk_o.shape, F32)

        k4 = jnp.concatenate([kp_r[...], kc_r[...]], axis=0)
        v4 = jnp.concatenate([vp_r[...], vc_r[...]], axis=0)
        work = [(b, e) for b in range(nb) for e in (0, 1)]
        sel = [lo, jnp.logical_not(lo)]
        k3 = [jnp.concatenate([k4[BLK * b:BLK * (b + 2)], km_r[...]], axis=0) for b in range(nb)]
        v3 = [jnp.concatenate([v4[BLK * b:BLK * (b + 2)], vm_r[...]], axis=0) for b in range(nb)]
        q = [q_r[BLK * b:BLK * (b + 1), :] for b in range(nb)]
        do_ = [do_r[BLK * b:BLK * (b + 1), :] for b in range(nb)]
        lse = [lse_r[BLK * b:BLK * (b + 1), :] for b in range(nb)]
        dd = [do_[b] * o_r[BLK * b:BLK * (b + 1), :].astype(F32) for b in range(nb)]
        valid = [_swa_valid(i * nb + b) for b in range(nb)]
        qe = {(b, e): jnp.where(sel[e], q[b], 0) for b, e in work}
        doe = {(b, e): jnp.where(sel[e], do_[b], 0.0).astype(CDT) for b, e in work}
        lse_e = {(b, e): lse[b][:, HD * e:HD * e + 1] for b, e in work}
        delta = {(b, e): jnp.sum(jnp.where(sel[e], dd[b], 0.0), axis=1, keepdims=True) for b, e in work}
        s = {(b, e): jnp.where(valid[b], _dot_nt(qe[b, e], k3[b]) + b_r[e], NEG) for b, e in work}
        dp = {(b, e): _dot_nt(doe[b, e], v3[b]) for b, e in work}
        pr = {w: jnp.exp(s[w] - lse_e[w]) for w in work}
        ds = {w: pr[w] * (dp[w] - delta[w]) for w in work}
        dqs = {(b, e): _dot(ds[b, e].astype(CDT), jnp.where(sel[e], k3[b], 0)) for b, e in work}
        both = lambda x, b: jnp.concatenate([x[b, 0], x[b, 1]], axis=0)
        dk3 = [_dot(both(ds, b).T.astype(CDT), both(qe, b)) for b in range(nb)]
        dv3 = [_dot(both(pr, b).T.astype(CDT), both(doe, b)) for b in range(nb)]
        for e in (0, 1):
            tot = ds[0, e]
            for b in range(1, nb):
                tot = tot + ds[b, e]
            db_o[e] += tot
        dsink = [sum(-jnp.sum(jnp.exp(sink_r[2 * p + e] - lse_e[b, e]) * delta[b, e], axis=0, keepdims=True)
                     for b in range(nb)) for e in (0, 1)]
        dsk_o[0:1, :] += jnp.where(lo, dsink[0], dsink[1])
        for b in range(nb):
            ib = i * nb + b
            dq_o[BLK * b:BLK * (b + 1), :] = dqs[b, 0] + dqs[b, 1]
            dk = dk3[b]
            dv = dv3[b]
            prev = pl.ds(pl.multiple_of(jnp.maximum(ib - 1, 0) * BLK, BLK), BLK)
            cur = pl.ds(pl.multiple_of(ib * BLK, BLK), BLK)
            dk_o[prev, :] += dk[0:BLK]
            dk_o[cur, :] += dk[BLK:2 * BLK]
            dk_o[0:BLK, :] += dk[2 * BLK:]
            dv_o[prev, :] += dv[0:BLK]
            dv_o[cur, :] += dv[BLK:2 * BLK]
            dv_o[0:BLK, :] += dv[2 * BLK:]

    qblk = pl.BlockSpec((ta, 128), lambda p, i: (i, p))
    kvacc = pl.BlockSpec((None, t, 128), lambda p, i: (p // 2, 0, 0))
    bblk = pl.BlockSpec((2, BLK, 3 * BLK), lambda p, i: (p, 0, 0))
    return pl.pallas_call(
        body, name=name, grid=(NPAIR, t // ta),
        in_specs=[pl.BlockSpec(memory_space=pltpu.SMEM), qblk] + _swa_kv_specs(ta) + _swa_kv_specs(ta)
        + [bblk, qblk, qblk, qblk],
        out_specs=[qblk, kvacc, kvacc, bblk, pl.BlockSpec((None, 8, 128), lambda p, i: (p, 0, 0))],
        out_shape=[_sds((t, 512), F32), _sds((2, t, 128), F32), _sds((2, t, 128), F32),
                   _sds((8, BLK, 3 * BLK), F32), _sds((NPAIR, 8, 128), F32)],
        compiler_params=_params(("arbitrary", "arbitrary")),
    )(sinks, qs, kse, kse, kse, vse, vse, vse, bias, o, lse, do)


def _sum8(slots, name):
    def body(a_r, o_o):
        acc = a_r[0]
        for k in range(1, 8):
            acc = acc + a_r[k]
        o_o[...] = acc

    return pl.pallas_call(
        body, name=name, out_shape=_sds((SMALL_ROWS, 128), F32),
        in_specs=[pl.BlockSpec(memory_space=pltpu.VMEM)], out_specs=pl.BlockSpec(memory_space=pltpu.VMEM),
        compiler_params=_params(),
    )(slots)


def _place():
    x, y, c = lax.axis_index("x"), lax.axis_index("y"), lax.axis_index("c")
    chips = [(1 - x, y), (x, 1 - y), (1 - x, 1 - y)]
    return x, y, c, chips


def _remote(src, dst, send_sems, recv_sems, k, to):
    return pltpu.make_async_remote_copy(src_ref=src, dst_ref=dst, send_sem=send_sems.at[k], recv_sem=recv_sems.at[k],
                                        device_id=to, device_id_type=MESH_ID)


ANY = pl.BlockSpec(memory_space=pl.ANY)


def _mix_cols(w):
    return jnp.concatenate([w[:, 2312:4360], w[:, 0:1536], w[:, 1544:2312], w[:, 1536:1544],
                            jnp.zeros((w.shape[0], DP - D_IN), w.dtype)], axis=1)


def _unmix_cols(w):
    return jnp.concatenate([w[:, QA:QA + 1536], w[:, FA:FA + 8], w[:, QB:QB + 768], w[:, GA:GA + 2048]], axis=1)


def _rows128(a, rows):
    flat = a.reshape(-1)
    return jnp.pad(flat, (0, rows * 128 - flat.shape[0])).reshape(rows, 128)


GRAD_FORM = {"ffn1_w_in": "col", "ffn2_w_in": "col", "w_branch_fox": "col", "w_branch_swa": "col",
             "ffn1_w_out": "3d", "ffn2_w_out": "3d", "w_out": "3d", "w_in": "3d"}
SUM_TILE = {1024: 128, 704: 176, 512: 128, 256: 128}
NT = len(SHARD_ITEMS)
ALL_ITEMS = tuple(range(NT))


def _half_rows(c, r):
    return pl.ds(pl.multiple_of(c * (r // 2), 16), r // 2)


def _ici_copies(kind, srcs, dsts, send_sems, recv_sems, layer, recv=True, items=ALL_ITEMS):
    x, y, c, chips = _place()
    s = 2 * x + y
    sends, recvs = [], []
    for t, (item, src, dst) in enumerate(zip(items, srcs, dsts)):
        nm, (r, cc), _ = SHARD_ITEMS[item]
        for j, (cx, cy) in enumerate(chips):
            sj = 2 * cx + cy
            k = 3 * t + j
            to = (cx, cy, c)
            if kind == "gather":
                hs = _half_rows(c, r)
                sends.append(_remote(src.at[layer, hs], dst.at[s, hs], send_sems, recv_sems, k, to))
                if recv:
                    recvs.append(_remote(src.at[layer, hs], dst.at[sj, hs], send_sems, recv_sems, k, to))
            else:
                if GRAD_FORM[nm] == "col":
                    piece = src.at[:, pl.ds(pl.multiple_of(sj * cc, 128), cc)]
                else:
                    piece = src.at[sj]
                sends.append(_remote(piece, dst.at[j], send_sems, recv_sems, k, to))
                recvs.append(sends[-1])
    return sends, recvs


def _slab_shapes(items=ALL_ITEMS):
    return [_sds((4, *SHARD_ITEMS[t][1]), CDT) for t in items]


def _dma_sems(n):
    return [pltpu.SemaphoreType.DMA((n,)), pltpu.SemaphoreType.DMA((n,))]


def _forward_sends(dsts, send_sems, recv_sems, items=ALL_ITEMS):
    x, y, c, chips = _place()
    sends, recvs = [], []
    for t, (item, dst) in enumerate(zip(items, dsts)):
        r = SHARD_ITEMS[item][1][0]
        for j, (cx, cy) in enumerate(chips):
            sj = 2 * cx + cy
            hs, ho = _half_rows(c, r), _half_rows(1 - c, r)
            sends.append(_remote(dst.at[sj, hs], dst.at[sj, hs], send_sems, recv_sems, 3 * t + j, (x, y, 1 - c)))
            recvs.append(_remote(dst.at[sj, ho], dst.at[sj, ho], send_sems, recv_sems, 3 * t + j, (x, y, 1 - c)))
    return sends, recvs


def _gather_layer(wb, mflat, layer, name, items):
    nt = len(items)

    def body(*refs):
        srcs, m_r, dsts, mall_o = refs[:nt], refs[nt], refs[nt + 1:2 * nt + 1], refs[2 * nt + 1]
        send_sems, recv_sems, fsend, frecv, msend, mrecv = refs[2 * nt + 2:]
        x, y, c, chips = _place()
        s = 2 * x + y
        sends, recvs = _ici_copies("gather", srcs, dsts, send_sems, recv_sems, layer, items=items)
        metas = [_remote(m_r, mall_o.at[s], msend, mrecv, j, (cx, cy, c)) for j, (cx, cy) in enumerate(chips)]
        for cp in sends + metas:
            cp.start()
        fwds, frecvs = _forward_sends(dsts, fsend, frecv, items)
        for got, fwd in zip(recvs, fwds):
            got.wait_recv()
            fwd.start()
        for got in frecvs:
            got.wait_recv()
        for j, (cx, cy) in enumerate(chips):
            _remote(m_r, mall_o.at[2 * cx + cy], msend, mrecv, j, (cx, cy, c)).wait_recv()
        for cp in sends + metas + fwds:
            cp.wait_send()

    return pl.pallas_call(
        body, name=name, out_shape=_slab_shapes(items) + [_sds((4, META_ROWS, 128), F32)],
        in_specs=[ANY] * (nt + 1), out_specs=[ANY] * (nt + 1),
        scratch_shapes=_dma_sems(3 * nt) + _dma_sems(3 * nt) + _dma_sems(3),
    )(*wb, mflat)


def _forward_layer(slabs, name, items=ALL_ITEMS):
    nt = len(items)

    def body(*refs):
        ins, outs, send_sems, recv_sems = refs[:nt], refs[nt:2 * nt], refs[2 * nt], refs[2 * nt + 1]
        sends, recvs = _forward_sends(outs, send_sems, recv_sems, items)
        for cp in sends:
            cp.start()
        for cp in recvs:
            cp.wait_recv()
        for cp in sends:
            cp.wait_send()

    return pl.pallas_call(
        body, name=name, out_shape=_slab_shapes(items), in_specs=[ANY] * nt, out_specs=[ANY] * nt,
        input_output_aliases={t: t for t in range(nt)}, scratch_shapes=_dma_sems(3 * nt),
    )(*slabs)


def _half_shape(nm, r, c):
    return (r // 2, 4 * c) if GRAD_FORM[nm] == "col" else (4, r // 2, c)


def _swap_layer(gs, gsm, name, items=ALL_ITEMS):
    small = gsm is not None
    nt = len(items)

    def body(*refs):
        g_rs = refs[:nt]
        pos = nt
        if small:
            s_r = refs[pos]
            pos += 1
        got_os = refs[pos:pos + nt]
        pos += nt
        if small:
            slots_o = refs[pos]
            pos += 1
        send_sems, recv_sems = refs[pos], refs[pos + 1]
        x, y, c, _ = _place()
        sib = (x, y, 1 - c)
        sent = []
        for t, (item, g_r, got_o) in enumerate(zip(items, g_rs, got_os)):
            nm, (r, cc), _ = SHARD_ITEMS[item]
            ho = _half_rows(1 - c, r)
            src = g_r.at[ho, :] if GRAD_FORM[nm] == "col" else g_r.at[:, ho, :]
            sent.append(_remote(src, got_o, send_sems, recv_sems, t, sib))
        if small:
            ssend, srecv, loc_sem = refs[pos + 2], refs[pos + 3], refs[pos + 4]
            me = 4 * x + 2 * y + c
            loc = pltpu.make_async_copy(s_r, slots_o.at[me], loc_sem.at[0])
            loc.start()
            peers = [(x ^ (k >> 2), y ^ ((k >> 1) & 1), c ^ (k & 1)) for k in range(1, 8)]
            for k, peer in enumerate(peers):
                sent.append(_remote(s_r, slots_o.at[me], ssend, srecv, k, peer))
        for cp in sent:
            cp.start()
        for cp in sent[:nt]:
            cp.wait_recv()
        if small:
            for k, (px, py, pc) in enumerate(peers):
                _remote(s_r, slots_o.at[4 * px + 2 * py + pc], ssend, srecv, k, (px, py, pc)).wait_recv()
        for cp in sent:
            cp.wait_send()
        if small:
            loc.wait()

    outs = [_sds(_half_shape(*SHARD_ITEMS[item][0:1], *SHARD_ITEMS[item][1]), CDT) for item in items]
    ops = list(gs)
    sems = _dma_sems(nt)
    if small:
        outs.append(_sds((8, SMALL_ROWS, 128), F32))
        ops.append(gsm)
        sems = sems + _dma_sems(7) + [pltpu.SemaphoreType.DMA((1,))]
    res = pl.pallas_call(
        body, name=name, out_shape=outs, in_specs=[ANY] * len(ops), out_specs=[ANY] * len(outs), scratch_shapes=sems,
    )(*ops)
    return (res[:nt], res[nt]) if small else (res, None)


def _pair_add_t(own, got, half_idx, nm, r, name):
    tr = SUM_TILE[r]
    nb = (r // 2) // tr
    if GRAD_FORM[nm] == "col":
        blk = (tr, own.shape[1])
        own_spec = pl.BlockSpec(blk, lambda i, c_r: (c_r[0] * nb + i, 0))
        half_spec = pl.BlockSpec(blk, lambda i, c_r: (i, 0))
    else:
        blk = (4, tr, own.shape[2])
        own_spec = pl.BlockSpec(blk, lambda i, c_r: (0, c_r[0] * nb + i, 0))
        half_spec = pl.BlockSpec(blk, lambda i, c_r: (0, i, 0))

    def body(c_r, a_r, b_r, o_o):
        o_o[...] = (a_r[...].astype(F32) + b_r[...].astype(F32)).astype(CDT)

    grid_spec = pltpu.PrefetchScalarGridSpec(num_scalar_prefetch=1, grid=(nb,), in_specs=[own_spec, half_spec],
                                             out_specs=half_spec)
    return pl.pallas_call(body, name=name, grid_spec=grid_spec, out_shape=_sds(got.shape, CDT),
                          compiler_params=_params(("parallel",)))(half_idx, own, got)


def _sum4_t(ps, got3, buf, idx, layer, nm, r, name):
    tr = SUM_TILE[r]
    nb = (r // 2) // tr
    c = got3.shape[2]
    if GRAD_FORM[nm] == "col":
        ps_spec = pl.BlockSpec((tr, c), lambda i, x_r: (i, x_r[0]))
    else:
        ps_spec = pl.BlockSpec((None, tr, c), lambda i, x_r: (x_r[0], i, 0))

    def body(x_r, a_r, b_r, buf_r, o_o):
        o_o[...] = ((a_r[...].astype(F32) + b_r[0].astype(F32)) + b_r[1].astype(F32)) + b_r[2].astype(F32)

    grid_spec = pltpu.PrefetchScalarGridSpec(
        num_scalar_prefetch=1, grid=(nb,),
        in_specs=[ps_spec, pl.BlockSpec((3, tr, c), lambda i, x_r: (0, i, 0)), ANY],
        out_specs=pl.BlockSpec((None, tr, c), lambda i, x_r: (layer, x_r[1] * nb + i, 0)),
    )
    return pl.pallas_call(body, name=name, grid_spec=grid_spec, out_shape=_sds(buf.shape, F32),
                          input_output_aliases={3: 0}, compiler_params=_params(("parallel",)))(idx, ps, got3, buf)


def _scatter_layer(ps, name, items=ALL_ITEMS):
    nt = len(items)

    def body(*refs):
        srcs, dsts, send_sems, recv_sems = refs[:nt], refs[nt:2 * nt], refs[2 * nt], refs[2 * nt + 1]
        sends, recvs = _ici_copies("scatter", srcs, dsts, send_sems, recv_sems, None, items=items)
        for cp in sends:
            cp.start()
        for cp in recvs:
            cp.wait_recv()
        for cp in sends:
            cp.wait_send()

    return pl.pallas_call(
        body, name=name, out_shape=_got3_shapes(items), in_specs=[ANY] * nt, out_specs=[ANY] * nt,
        scratch_shapes=_dma_sems(3 * nt),
    )(*ps)


def _got3_shapes(items=ALL_ITEMS):
    return [_sds((3, SHARD_ITEMS[t][1][0] // 2, SHARD_ITEMS[t][1][1]), CDT) for t in items]


def _join_layer(bufs, name):
    def body(*refs):
        ins, outs, send_sems, recv_sems = refs[:NT], refs[NT:2 * NT], refs[2 * NT], refs[2 * NT + 1]
        x, y, c, _ = _place()
        sent = []
        for t, ((nm, (r, cc), _), b_o) in enumerate(zip(SHARD_ITEMS, outs)):
            hs = _half_rows(c, r)
            sent.append(_remote(b_o.at[:, hs, :], b_o.at[:, hs, :], send_sems, recv_sems, t, (x, y, 1 - c)))
        for cp in sent:
            cp.start()
        for t, ((nm, (r, cc), _), b_o) in enumerate(zip(SHARD_ITEMS, outs)):
            ho = _half_rows(1 - c, r)
            _remote(b_o.at[:, ho, :], b_o.at[:, ho, :], send_sems, recv_sems, t, (x, y, 1 - c)).wait_recv()
        for cp in sent:
            cp.wait_send()

    return pl.pallas_call(
        body, name=name, out_shape=[_sds(b.shape, F32) for b in bufs], in_specs=[ANY] * NT, out_specs=[ANY] * NT,
        input_output_aliases={t: t for t in range(NT)}, scratch_shapes=_dma_sems(NT),
    )(*bufs)


def _adamw3(w, g, m, v, name):
    nl, r, c = w.shape
    tr = SUM_TILE.get(r, r)
    if r % 8:
        blk = pl.BlockSpec((None, r, 256), lambda l, i: (l, 0, i))
        steps = c // 256
    else:
        blk = pl.BlockSpec((None, tr, c), lambda l, i: (l, i, 0))
        steps = r // tr

    def body(w_r, g_r, m_r, v_r, d_o, m_o, v_o):
        g_ = g_r[...]
        m_ = ADAM_B1 * m_r[...] + (1.0 - ADAM_B1) * g_
        v_ = ADAM_B2 * v_r[...] + (1.0 - ADAM_B2) * jnp.square(g_)
        m_hat = m_ / (1.0 - ADAM_B1 ** ADAM_STEP)
        v_hat = v_ / (1.0 - ADAM_B2 ** ADAM_STEP)
        d_o[...] = -ADAM_LR * (m_hat / (jnp.sqrt(v_hat) + ADAM_EPS) + ADAM_WD * w_r[...])
        m_o[...] = m_
        v_o[...] = v_

    return pl.pallas_call(
        body, name=name, grid=(nl, steps),
        in_specs=[blk] * 4, out_specs=[blk] * 3, out_shape=[_sds((nl, r, c), F32)] * 3,
        compiler_params=_params(("parallel", "parallel")),
    )(w, g, m, v)


def _full_weights(slabs, wb, layer, shard, items=ALL_ITEMS):
    ws = {}
    for t, slab in zip(items, slabs):
        nm, (r, c), kind = SHARD_ITEMS[t]
        slab = lax.dynamic_update_slice(slab, wb[nm][layer][None], (shard, 0, 0))
        ws[nm] = slab.reshape(4 * r, c) if kind == "row" else jnp.concatenate([slab[s] for s in range(4)], axis=1)
    return ws


def _exchange_forms(g, items=ALL_ITEMS):
    out = []
    for t in items:
        nm, (r, c), _ = SHARD_ITEMS[t]
        a = g[nm]
        if nm == "w_in":
            a = a.reshape(D, 4, c).transpose(1, 0, 2)
        elif GRAD_FORM[nm] == "3d":
            a = a.reshape(4, r, c)
        out.append(a)
    return out


SMALL_ITEMS = (("rel_bias_table", 2), ("ffn1_norm", 16), ("mix_norm", 16), ("ffn2_norm", 16), ("forget_bias", 1),
               ("fox_q_norm", 1), ("fox_k_norm", 1), ("swa_q_norm", 1), ("swa_k_norm", 1), ("swa_sinks", 1))
SMALL_ADAM_ROWS = 96


def _layer_fwd(h, lw, l, ride=None, late=None):
    rides = late["rides"] if late else {}

    def run(key, fn, *args):
        r = rides.get(key)
        if r is None:
            return fn(*args)
        out = fn(*args, ride=r)
        late["arrived"](key, out[-1])
        return out[0] if len(out) == 2 else out[:-1]

    sv = {"h0": h}
    a, sv["a1t"] = _rms_fwd(h, lw["ffn1_norm"], f"rms_fwd_a{l}")
    sv["gu1"], s, sv["s1t"] = run("ffn_in_a", _ffn_in, a, lw["ffn1_w_in"], f"ffn_in_a{l}")
    h = run("ffn_out_a", _mm_res, s, lw["ffn1_w_out"], h, 0.5, f"ffn_out_a{l}")
    sv["h1"] = h
    a, sv["amt"] = _rms_fwd(h, lw["mix_norm"], f"rms_fwd_m{l}")
    if late:
        late["need"](lw, "mixer")
    proj = run("proj", _mm, a, lw["w_mix"], F32, _row_tile(h.shape[0]), DP, f"proj{l}")
    sv["proj"] = proj
    qf, kf, vf, qs, kse, vse, c, ct, sv["qft"] = _qknorm_fwd(proj, lw["gfq"], lw["gfk"], lw["gsq"], lw["gsk"], lw["fb"],
                                                              f"qknorm_fwd{l}")
    ofox, lse_f, *rode = _fox_fwd(qf, kf, vf, c, ct, f"fox_fwd{l}", ride)
    oswa, lse_s = run("swa_fwd", _swa_fwd, qs, kse, vse, lw["bias"], lw["sinks"], f"swa_fwd{l}")
    if late:
        late["need"](lw, "gate")
    sv.update(qf=qf, kf=kf, vf=vf, qs=qs, kse=kse, vse=vse, c=c, ct=ct, ofox=ofox, oswa=oswa, lse_f=lse_f, lse_s=lse_s)
    h, sv["yt"], sv["pf"], sv["ps"], sv["oft"], sv["ost"] = _gate_out_fwd(
        ofox, oswa, lw["w_branch_fox"], lw["w_branch_swa"], proj, lw["w_out"], h, f"gate_out_fwd{l}")
    sv["h2"] = h
    a, sv["a2t"] = _rms_fwd(h, lw["ffn2_norm"], f"rms_fwd_b{l}")
    sv["gu2"], s, sv["s2t"] = _ffn_in(a, lw["ffn2_w_in"], f"ffn_in_b{l}")
    h = _mm_res(s, lw["ffn2_w_out"], h, 0.5, f"ffn_out_b{l}")
    return h, sv, rode


def _ffn_bwd(dh, dhb, h_in, at, gu, st, norm, w_in, w_out, tag, rides=None):
    r = rides or (None,) * 4
    rode = []

    def split(res, ride):
        if ride is None:
            return res
        rode.extend(res[-1])
        return res[0] if len(res) == 2 else res[:-1]

    dgu = split(_ffn_bwd_mid(dhb, w_out, gu, f"ffn_bwd_mid_{tag}", r[0]), r[0])
    d_w_out = split(_mm(st, dhb, CDT, 256, D, f"dw_ffn_out_{tag}", scale=0.5, ride=r[1]), r[1])
    dh, dhb, dg = split(_ffn_bwd_in(dgu, w_in, h_in, norm, dh, f"ffn_bwd_in_{tag}", r[2]), r[2])
    d_w_in = split(_mm(at, dgu, CDT, D, 256, f"dw_ffn_in_{tag}", ride=r[3]), r[3])
    return dh, dhb, d_w_out, d_w_in, dg, rode


def _layer_bwd(dh, dhb, sv, lw, l, ride=None, before_ffn1=None):
    g = {}
    dh, dhb, g["ffn2_w_out"], g["ffn2_w_in"], g["ffn2_norm"], _ = _ffn_bwd(
        dh, dhb, sv["h2"], sv["a2t"], sv["gu2"], sv["s2t"], lw["ffn2_norm"], lw["ffn2_w_in"], lw["ffn2_w_out"], f"b{l}")
    g["w_out"] = _mm(sv["yt"], dhb, CDT, 512, 512, f"dw_out{l}")
    dpf, dps, dga, dgb = _gate_out_bwd(dhb, lw["w_out"], sv["pf"], sv["ps"], sv["proj"], f"gate_out_bwd{l}")
    do_f, do_ft = _mm_nt(dpf, lw["w_branch_fox"], f"d_ofox{l}", with_t=True)
    do_s = _mm_nt(dps, lw["w_branch_swa"], f"d_oswa{l}")
    g["w_branch_fox"] = _mm(sv["oft"], dpf, CDT, 512, 512, f"dw_bfox{l}")
    g["w_branch_swa"] = _mm(sv["ost"], dps, CDT, 512, 512, f"dw_bswa{l}")
    dqf, dcq, dkf, dvf, dck, *rode = _fox_bwd(sv["qf"], sv["qft"], sv["kf"], sv["vf"], sv["c"], sv["ct"], sv["ofox"],
                                              sv["lse_f"], do_f, do_ft, f"fox_bwd{l}", ride)
    g["rode"] = rode
    dqs, dkse, dvse, dbias, dsk = _swa_bwd(sv["qs"], sv["kse"], sv["vse"], lw["bias"], lw["sinks"], sv["oswa"],
                                           sv["lse_s"], do_s, f"swa_bwd{l}")
    dproj, dgn = _qknorm_bwd(sv["proj"], dqf, dkf, dvf, dqs, dkse, dvse, dcq, dck, dga, dgb,
                             lw["gfq"], lw["gfk"], lw["gsq"], lw["gsk"], lw["fb"], f"qknorm_bwd{l}")
    g["w_mix"] = _mm(sv["amt"], dproj, CDT, D, 640, f"dw_mix{l}")
    dh, dhb, g["mix_norm"] = _mm_nt_rms(dproj, lw["w_mix"], sv["h1"], lw["mix_norm"], dh, f"d_am{l}")
    g["dbias"], g["dsk"], g["dgn"] = dbias, dsk, dgn
    rides = before_ffn1(g) if before_ffn1 else None
    dh, dhb, g["ffn1_w_out"], g["ffn1_w_in"], g["ffn1_norm"], g["rode_ffn1"] = _ffn_bwd(
        dh, dhb, sv["h0"], sv["a1t"], sv["gu1"], sv["s1t"], lw["ffn1_norm"], lw["ffn1_w_in"], lw["ffn1_w_out"], f"a{l}",
        rides)
    return dh, dhb, g


def kernel(x, meta_tokens, rel_bias_table, ffn1_norm, ffn1_w_in, ffn1_w_out, mix_norm, w_in, forget_bias, fox_q_norm, fox_k_norm, swa_q_norm, swa_k_norm, swa_sinks, w_branch_fox, w_branch_swa, w_out, ffn2_norm, ffn2_w_in, ffn2_w_out, loss_target, m_meta_tokens, m_rel_bias_table, m_ffn1_norm, m_ffn1_w_in, m_ffn1_w_out, m_mix_norm, m_w_in, m_forget_bias, m_fox_q_norm, m_fox_k_norm, m_swa_q_norm, m_swa_k_norm, m_swa_sinks, m_w_branch_fox, m_w_branch_swa, m_w_out, m_ffn2_norm, m_ffn2_w_in, m_ffn2_w_out, v_meta_tokens, v_rel_bias_table, v_ffn1_norm, v_ffn1_w_in, v_ffn1_w_out, v_mix_norm, v_w_in, v_forget_bias, v_fox_q_norm, v_fox_k_norm, v_swa_q_norm, v_swa_k_norm, v_swa_sinks, v_w_branch_fox, v_w_branch_swa, v_w_out, v_ffn2_norm, v_ffn2_w_in, v_ffn2_w_out):
    names = ["meta_tokens", "rel_bias_table", "ffn1_norm", "ffn1_w_in", "ffn1_w_out", "mix_norm", "w_in", "forget_bias",
             "fox_q_norm", "fox_k_norm", "swa_q_norm", "swa_k_norm", "swa_sinks", "w_branch_fox", "w_branch_swa", "w_out",
             "ffn2_norm", "ffn2_w_in", "ffn2_w_out"]
    w = dict(zip(names, [meta_tokens, rel_bias_table, ffn1_norm, ffn1_w_in, ffn1_w_out, mix_norm, w_in, forget_bias,
                         fox_q_norm, fox_k_norm, swa_q_norm, swa_k_norm, swa_sinks, w_branch_fox, w_branch_swa, w_out,
                         ffn2_norm, ffn2_w_in, ffn2_w_out]))
    m = dict(zip(names, [m_meta_tokens, m_rel_bias_table, m_ffn1_norm, m_ffn1_w_in, m_ffn1_w_out, m_mix_norm, m_w_in,
                         m_forget_bias, m_fox_q_norm, m_fox_k_norm, m_swa_q_norm, m_swa_k_norm, m_swa_sinks,
                         m_w_branch_fox, m_w_branch_swa, m_w_out, m_ffn2_norm, m_ffn2_w_in, m_ffn2_w_out]))
    v = dict(zip(names, [v_meta_tokens, v_rel_bias_table, v_ffn1_norm, v_ffn1_w_in, v_ffn1_w_out, v_mix_norm, v_w_in,
                         v_forget_bias, v_fox_q_norm, v_fox_k_norm, v_swa_q_norm, v_swa_k_norm, v_swa_sinks,
                         v_w_branch_fox, v_w_branch_swa, v_w_out, v_ffn2_norm, v_ffn2_w_in, v_ffn2_w_out]))
    xi, yi, ci = lax.axis_index("x"), lax.axis_index("y"), lax.axis_index("c")
    shard = 2 * xi + yi
    seq = x.shape[1]
    t = seq + BLK

    wb = {nm: w[nm].astype(CDT) for nm, _, _ in SHARD_ITEMS}
    wb_list = [wb[nm] for nm, _, _ in SHARD_ITEMS]
    mflat = meta_tokens.reshape(META_ROWS, 128)
    first = (0, 1)
    *slabs_first, mall = _gather_layer([wb_list[t] for t in first], mflat, 0, "gather_weights", first)
    mall = lax.dynamic_update_slice(mall, mflat[None], (shard, 0, 0))
    meta_full = jnp.concatenate([mall[s].reshape(N_META, 256) for s in range(4)], axis=1)
    bias = _bias_fwd(rel_bias_table, "bias_fwd")

    def layer_weights(slabs, l, items=ALL_ITEMS):
        lw = _full_weights(slabs, wb, l, shard, items)
        if "w_in" in lw:
            lw["w_mix"] = _mix_cols(lw.pop("w_in"))
        return lw

    def layer_vectors(l):
        lw = {nm: w[nm][l].reshape(1, D) for nm in ("ffn1_norm", "mix_norm", "ffn2_norm")}
        lw["gfq"] = jnp.tile(fox_q_norm[l], 8).reshape(1, 512)
        lw["gfk"] = jnp.tile(fox_k_norm[l], 8).reshape(1, 512)
        lw["gsq"] = jnp.tile(swa_q_norm[l], 8).reshape(1, 512)
        lw["gsk"] = jnp.tile(swa_k_norm[l], 2).reshape(1, 128)
        lw["fb"] = jnp.pad(forget_bias[l], (0, 120)).reshape(1, 128)
        lw["sinks"] = swa_sinks[l]
        lw["bias"] = bias
        return lw

    def gather_ride(layer, items):
        return ("gather", [wb_list[t] for t in items], _slab_shapes(items), layer, items)

    landed = {}

    def need(lw, stage):
        if stage == "mixer":
            items = (2,)
            slabs = _forward_layer(landed["ffn_in_a"], "forward_halves0m", items)
        else:
            items = (3, 4, 5, 6, 7)
            slabs = _forward_layer(landed["ffn_out_a"] + landed["proj"] + landed["swa_fwd"], "forward_halves0g", items)
        lw.update(layer_weights(slabs, 0, items))

    late = {"rides": {"ffn_in_a": gather_ride(0, (2,)), "ffn_out_a": gather_ride(0, (3, 4, 5)),
                      "proj": gather_ride(0, (6,)), "swa_fwd": gather_ride(0, (7,))},
            "arrived": landed.__setitem__, "need": need}

    h = jnp.concatenate([jnp.zeros((PAD, D), F32), meta_full, x[0]], axis=0)
    lws = [{**layer_vectors(0), **layer_weights(slabs_first, 0, first)}]
    h, sv0, slabs1 = _layer_fwd(h, lws[0], 0, gather_ride(1, ALL_ITEMS), late)
    lws.append({**layer_vectors(1), **layer_weights(_forward_layer(slabs1, "forward_halves"), 1)})
    h, sv1, _ = _layer_fwd(h, lws[1], 1)
    saved = [sv0, sv1]
    dh, dhb, lacc = _loss(h, loss_target[0], "loss")
    loss = lax.psum(lacc[0, 0], ("x", "y", "c"))

    half_idx = ci.reshape(1).astype(jnp.int32)
    place_idx = jnp.stack([shard, ci]).astype(jnp.int32)

    def pair_sums(g, gsm, tag, items=ALL_ITEMS):
        if "w_mix" in g:
            g["w_in"] = _unmix_cols(g.pop("w_mix"))
        forms = _exchange_forms(g, items)
        got, slots = _swap_layer(forms, gsm, f"swap_halves{tag}", items)
        return {t: _pair_add_t(a, b, half_idx, SHARD_ITEMS[t][0], SHARD_ITEMS[t][1][0],
                               f"pair_add{tag}_{SHARD_ITEMS[t][0]}")
                for t, a, b in zip(items, forms, got)}, slots

    def scatter_ride(ps, items):
        return ("scatter", [ps[t] for t in items], _got3_shapes(items), None, items)

    early = (2, 3, 4, 5, 6, 7)
    early_rides = ((6,), (7,), (2, 5), (3, 4))
    ps0 = {}

    def before_ffn1(g):
        ps0.update(pair_sums(g, None, "0e", early)[0])
        return [scatter_ride(ps0, items) for items in early_rides]

    grads = [None, None]
    dh, dhb, grads[1] = _layer_bwd(dh, dhb, saved[1], lws[1], 1)
    ps1, _ = pair_sums(grads[1], None, 1)
    dh, dhb, grads[0] = _layer_bwd(dh, dhb, saved[0], lws[0], 0, scatter_ride(ps1, ALL_ITEMS), before_ffn1)
    grad_x = dh[BLK:].reshape(1, seq, D)
    dtab = _bias_bwd(grads[0]["dbias"] + grads[1]["dbias"], "bias_bwd")

    small = [dh[PAD:BLK].reshape(128, 128), _rows128(dtab[:, :N_BUCKETS].T, 2)]
    for nm in ("ffn1_norm", "mix_norm", "ffn2_norm"):
        small.append(jnp.stack([grads[l][nm][0] for l in range(2)]).reshape(16, 128))
    small.append(_rows128(jnp.stack([grads[l]["dgn"][4, :8] for l in range(2)]), 1))
    for row in range(4):
        small.append(jnp.stack([grads[l]["dgn"][row, :HD] for l in range(2)]).reshape(1, 128))
    dsk = [grads[l]["dsk"][:, 0, :] for l in range(2)]
    small.append(_rows128(jnp.stack([jnp.stack([d[:, 0], d[:, HD]], axis=1).reshape(8) for d in dsk]), 1))
    gsm = jnp.concatenate(small, axis=0)
    gsm = jnp.pad(gsm, ((0, SMALL_ROWS - gsm.shape[0]), (0, 0)))

    late = (0, 1)
    ps_late, slots = pair_sums(grads[0], gsm, "0l", late)
    ps0.update(ps_late)
    got3_0 = dict(zip([t for items in early_rides for t in items], grads[0]["rode_ffn1"]))
    got3_0.update(zip(late, _scatter_layer([ps0[t] for t in late], "scatter_shards", late)))
    got3 = [got3_0, dict(zip(ALL_ITEMS, grads[0]["rode"]))]
    bufs = []
    for t, (nm, (r, c), _) in enumerate(SHARD_ITEMS):
        buf = lax.empty((2, r, c), F32)
        for l, ps in ((1, ps1), (0, ps0)):
            buf = _sum4_t(ps[t], got3[l][t], buf, place_idx, l, nm, r, f"sum4_{l}_{nm}")
        bufs.append(buf)
    bufs = _join_layer(bufs, "join_halves")
    gs = _sum8(slots, "sum8")

    g_out = {nm: buf for (nm, _, _), buf in zip(SHARD_ITEMS, bufs)}
    g_out["meta_tokens"] = lax.dynamic_slice(gs[0:128].reshape(N_META, D), (0, shard * 256), (N_META, 256))
    off = 128
    for nm, rows in SMALL_ITEMS:
        n = w[nm].size
        g_out[nm] = gs[off:off + rows].reshape(-1)[:n].reshape(w[nm].shape)
        off += rows

    delta, new_m, new_v = {}, {}, {}
    for nm, _, _ in SHARD_ITEMS:
        if nm == "w_in":
            tr_ = lambda a: jnp.swapaxes(a, 1, 2)
            delta[nm], new_m[nm], new_v[nm] = (tr_(a) for a in _adamw3(tr_(w[nm]), tr_(g_out[nm]), tr_(m[nm]), tr_(v[nm]),
                                                                        f"adamw_{nm}"))
        else:
            delta[nm], new_m[nm], new_v[nm] = _adamw3(w[nm], g_out[nm], m[nm], v[nm], f"adamw_{nm}")
    small_names = ["meta_tokens"] + [nm for nm, _ in SMALL_ITEMS]
    small_rows = [META_ROWS] + [rows for _, rows in SMALL_ITEMS]

    def pack_small(src):
        buf = jnp.concatenate([_rows128(src[nm], rows) for nm, rows in zip(small_names, small_rows)], axis=0)
        return jnp.pad(buf, ((0, SMALL_ADAM_ROWS - buf.shape[0]), (0, 0)))

    d_, m_, v_ = (a[0] for a in _adamw3(pack_small(w)[None], pack_small(g_out)[None], pack_small(m)[None],
                                        pack_small(v)[None], "adamw_small"))
    off = 0
    for nm, rows in zip(small_names, small_rows):
        n = w[nm].size
        for dst, src in ((delta, d_), (new_m, m_), (new_v, v_)):
            dst[nm] = src[off:off + rows].reshape(-1)[:n].reshape(w[nm].shape)
        off += rows

    return (loss, grad_x, *[g_out[n] for n in names], *[delta[n] for n in names],
            *[new_m[n] for n in names], *[new_v[n] for n in names])
```

```python
import math

import numpy as np
import jax
import jax.numpy as jnp
from jax import lax
from jax.experimental import pallas as pl
from jax.experimental.pallas import tpu as pltpu

D = 1024
F = 2816
FT = F // 2
HD = 64
NPAIR = 4
N_META = 16
BLK = 128
PAD = BLK - N_META
EPS = 1e-6
NEG = -1e30
N_BUCKETS = 32
GA, GB, QA, KA, VA, QB, KB, VB, FA, DP = 0, 1024, 2048, 2560, 3072, 3584, 4096, 4224, 4352, 4480
D_IN = 4360
CDT = jnp.bfloat16
F32 = jnp.float32
VMEM_LIMIT = 48 * 1024 * 1024
MESH_ID = pl.DeviceIdType.MESH

ADAM_LR, ADAM_B1, ADAM_B2, ADAM_EPS, ADAM_WD, ADAM_STEP = 0.001, 0.9, 0.999, 1e-08, 0.01, 10

SHARD_ITEMS = (
    ("ffn1_w_in", (1024, 1408), "col"),
    ("ffn1_w_out", (704, 1024), "row"),
    ("w_in", (1024, 1090), "col"),
    ("w_branch_fox", (512, 256), "col"),
    ("w_branch_swa", (512, 256), "col"),
    ("w_out", (256, 1024), "row"),
    ("ffn2_w_in", (1024, 1408), "col"),
    ("ffn2_w_out", (704, 1024), "row"),
)
SMALL_ROWS = 192
META_ROWS = 32


def _row_tile(t):
    return 384 if t % 384 == 0 else 128


def _dot(a, b):
    return jnp.dot(a, b, preferred_element_type=F32)


def _dot_nt(a, b):
    return lax.dot_general(a, b, (((1,), (1,)), ((), ())), preferred_element_type=F32)


def _dot_hi(a, b):
    return jnp.dot(a, b, preferred_element_type=F32, precision=lax.Precision.HIGHEST)


def _sigmoid(x):
    return 0.5 * jnp.tanh(0.5 * x) + 0.5


def _iota(shape, dim):
    return lax.broadcasted_iota(jnp.int32, shape, dim)


def _params(sem=None):
    return pltpu.CompilerParams(dimension_semantics=sem, vmem_limit_bytes=VMEM_LIMIT)


def _sds(shape, dtype):
    return jax.ShapeDtypeStruct(shape, dtype)


def _rms_fwd(h, g, name):
    t = h.shape[0]
    tm = _row_tile(t)

    def body(h_ref, g_ref, a_ref, at_ref):
        x = h_ref[...]
        ms = jnp.mean(x * x, axis=-1, keepdims=True)
        a = x * lax.rsqrt(ms + EPS) * g_ref[...]
        a_ref[...] = a.astype(CDT)
        at_ref[...] = a.T.astype(CDT)

    return pl.pallas_call(
        body, name=name, grid=(t // tm,),
        in_specs=[pl.BlockSpec((tm, D), lambda i: (i, 0)), pl.BlockSpec((1, D), lambda i: (0, 0))],
        out_specs=[pl.BlockSpec((tm, D), lambda i: (i, 0)), pl.BlockSpec((D, tm), lambda i: (0, i))],
        out_shape=[_sds((t, D), CDT), _sds((D, t), CDT)],
        compiler_params=_params(("parallel",)),
    )(h, g)


def _ffn_in(a, w_in, name, ride=None):
    t = a.shape[0]
    tm = _row_tile(t)
    tn = FT
    nj = F // tn
    grid = (nj, t // tm)
    ride_in, ride_in_specs, ride_out, ride_out_specs, ride_sems = _ride_specs(ride)

    def body(a_ref, wg_ref, wu_ref, gu_ref, s_ref, st_ref):
        a_ = a_ref[...]
        g = _dot(a_, wg_ref[...])
        u = _dot(a_, wu_ref[...])
        s = g * _sigmoid(g) * u
        gu_ref[0] = g.astype(CDT)
        gu_ref[1] = u.astype(CDT)
        s_ref[...] = s.astype(CDT)
        st_ref[...] = s.T.astype(CDT)

    res = pl.pallas_call(
        _riding(body, 3, 3, ride, grid), name=name, grid=grid,
        in_specs=[pl.BlockSpec((tm, D), lambda j, i: (i, 0)),
                  pl.BlockSpec((D, tn), lambda j, i: (0, j)),
                  pl.BlockSpec((D, tn), lambda j, i: (0, j + nj))] + ride_in_specs,
        out_specs=[pl.BlockSpec((2, tm, tn), lambda j, i: (0, i, j)),
                   pl.BlockSpec((tm, tn), lambda j, i: (i, j)),
                   pl.BlockSpec((tn, tm), lambda j, i: (j, i))] + ride_out_specs,
        out_shape=[_sds((2, t, F), CDT), _sds((t, F), CDT), _sds((F, t), CDT)] + ride_out, scratch_shapes=ride_sems,
        compiler_params=_params(("arbitrary", "arbitrary") if ride else ("parallel", "parallel")),
    )(a, w_in, w_in, *ride_in)
    return (*res[:3], res[3:]) if ride else res


def _mm_res(a, b, res, scale, name, ride=None):
    t, k = a.shape
    n = b.shape[1]
    tm = _row_tile(t)
    tn = n
    grid = (t // tm, n // tn)
    ride_in, ride_in_specs, ride_out, ride_out_specs, ride_sems = _ride_specs(ride)

    def body(a_ref, b_ref, r_ref, o_ref):
        o_ref[...] = r_ref[...] + scale * _dot(a_ref[...], b_ref[...])

    out = pl.pallas_call(
        _riding(body, 3, 1, ride, grid), name=name, grid=grid,
        in_specs=[pl.BlockSpec((tm, k), lambda i, j: (i, 0)),
                  pl.BlockSpec((k, tn), lambda i, j: (0, j)),
                  pl.BlockSpec((tm, tn), lambda i, j: (i, j))] + ride_in_specs,
        out_specs=[pl.BlockSpec((tm, tn), lambda i, j: (i, j))] + ride_out_specs,
        out_shape=[_sds((t, n), F32)] + ride_out, scratch_shapes=ride_sems,
        compiler_params=_params(("arbitrary", "arbitrary") if ride else ("parallel", "parallel")),
    )(a, b, res, *ride_in)
    return (out[0], out[1:]) if ride else out[0]


def _mm(a, b, out_dtype, tm, tn, name, scale=1.0, ride=None):
    m, k = a.shape
    if b.ndim == 3:
        nh = b.shape[2] // tn
        n = 2 * b.shape[2]
        b_spec = pl.BlockSpec((None, k, tn), lambda i, j: (j // nh, 0, j % nh))
    else:
        n = b.shape[1]
        b_spec = pl.BlockSpec((k, tn), lambda i, j: (0, j))
    grid = (m // tm, n // tn)
    ride_in, ride_in_specs, ride_out, ride_out_specs, ride_sems = _ride_specs(ride)

    def body(a_ref, b_ref, o_ref):
        o_ref[...] = (scale * _dot(a_ref[...], b_ref[...])).astype(out_dtype)

    res = pl.pallas_call(
        _riding(body, 2, 1, ride, grid), name=name, grid=grid,
        in_specs=[pl.BlockSpec((tm, k), lambda i, j: (i, 0)), b_spec] + ride_in_specs,
        out_specs=[pl.BlockSpec((tm, tn), lambda i, j: (i, j))] + ride_out_specs,
        out_shape=[_sds((m, n), out_dtype)] + ride_out, scratch_shapes=ride_sems,
        compiler_params=_params(("arbitrary", "arbitrary") if ride else ("parallel", "parallel")),
    )(a, b, *ride_in)
    return (res[0], res[1:]) if ride else res[0]


def _mm_nt(a, b, name, with_t=False):
    m, n = a.shape
    k = b.shape[0]
    tm = _row_tile(m)
    tk = k

    def body(a_ref, b_ref, o_ref, *t_ref):
        r = _dot_nt(a_ref[...], b_ref[...])
        o_ref[...] = r
        if with_t:
            t_ref[0][...] = r.T.astype(CDT)

    out_specs = [pl.BlockSpec((tm, tk), lambda i, j: (i, j))]
    out_shape = [_sds((m, k), F32)]
    if with_t:
        out_specs.append(pl.BlockSpec((tk, tm), lambda i, j: (j, i)))
        out_shape.append(_sds((k, m), CDT))
    res = pl.pallas_call(
        body, name=name, grid=(m // tm, k // tk),
        in_specs=[pl.BlockSpec((tm, n), lambda i, j: (i, 0)), pl.BlockSpec((tk, n), lambda i, j: (j, 0))],
        out_specs=out_specs, out_shape=out_shape,
        compiler_params=_params(("parallel", "parallel")),
    )(a, b)
    return res if with_t else res[0]


def _ffn_bwd_mid(dhb, w_out, gu, name, ride=None):
    t = dhb.shape[0]
    tm = _row_tile(t)
    tn = FT
    grid = (F // tn, t // tm)
    ride_in, ride_in_specs, ride_out, ride_out_specs, ride_sems = _ride_specs(ride)

    def body(dh_ref, w_ref, gu_ref, o_ref):
        ds = _dot_nt(dh_ref[...] * 0.5, w_ref[...])
        g = gu_ref[0].astype(F32)
        u = gu_ref[1].astype(F32)
        sg = _sigmoid(g)
        o_ref[0] = (ds * u * (sg * (1.0 + g * (1.0 - sg)))).astype(CDT)
        o_ref[1] = (ds * (g * sg)).astype(CDT)

    res = pl.pallas_call(
        _riding(body, 3, 1, ride, grid), name=name, grid=grid,
        in_specs=[pl.BlockSpec((tm, D), lambda j, i: (i, 0)),
                  pl.BlockSpec((tn, D), lambda j, i: (j, 0)),
                  pl.BlockSpec((2, tm, tn), lambda j, i: (0, i, j))] + ride_in_specs,
        out_specs=[pl.BlockSpec((2, tm, tn), lambda j, i: (0, i, j))] + ride_out_specs,
        out_shape=[_sds((2, t, F), CDT)] + ride_out, scratch_shapes=ride_sems,
        compiler_params=_params(("arbitrary", "arbitrary") if ride else ("parallel", "parallel")),
    )(dhb, w_out, gu, *ride_in)
    return (res[0], res[1:]) if ride else res[0]


def _rms_bwd_rows(da_, x, g, dres, i, dh_ref, dhb_ref, dg_ref):
    r = lax.rsqrt(jnp.mean(x * x, axis=-1, keepdims=True) + EPS)
    xh = x * r
    day = da_ * g
    dh = dres + r * (day - xh * jnp.mean(day * xh, axis=-1, keepdims=True))
    dh_ref[...] = dh
    dhb_ref[...] = dh.astype(CDT)

    @pl.when(i == 0)
    def _():
        dg_ref[...] = jnp.zeros(dg_ref.shape, F32)

    dg_ref[0:1, :] += jnp.sum(da_ * xh, axis=0, keepdims=True)


def _ffn_bwd_in(dgu, w_in, h, g, dres, name, ride=None):
    t = dgu.shape[1]
    tm = _row_tile(t)
    grid = (t // tm,)
    ride_in, ride_in_specs, ride_out, ride_out_specs, ride_sems = _ride_specs(ride)

    def body(dg_ref, wg_ref, wu_ref, h_ref, g_ref, dr_ref, dh_ref, dhb_ref, dgn_ref):
        da_ = _dot_nt(dg_ref[0], wg_ref[...]) + _dot_nt(dg_ref[1], wu_ref[...])
        _rms_bwd_rows(da_, h_ref[...], g_ref[...], dr_ref[...], pl.program_id(0), dh_ref, dhb_ref, dgn_ref)

    row = pl.BlockSpec((tm, D), lambda i: (i, 0))
    res = pl.pallas_call(
        _riding(body, 6, 3, ride, grid), name=name, grid=grid,
        in_specs=[pl.BlockSpec((2, tm, F), lambda i: (0, i, 0)),
                  pl.BlockSpec((D, F), lambda i: (0, 0)),
                  pl.BlockSpec((D, F), lambda i: (0, 1)),
                  row, pl.BlockSpec((1, D), lambda i: (0, 0)), row] + ride_in_specs,
        out_specs=[row, row, pl.BlockSpec((8, D), lambda i: (0, 0))] + ride_out_specs,
        out_shape=[_sds((t, D), F32), _sds((t, D), CDT), _sds((8, D), F32)] + ride_out, scratch_shapes=ride_sems,
        compiler_params=_params(("arbitrary",)),
    )(dgu, w_in, w_in, h, g, dres, *ride_in)
    return (*res[:3], res[3:]) if ride else res


def _mm_nt_rms(a, b, h, g, dres, name):
    t, n = a.shape
    tm = _row_tile(t)

    def body(a_ref, b_ref, h_ref, g_ref, dr_ref, dh_ref, dhb_ref, dgn_ref):
        da_ = _dot_nt(a_ref[...], b_ref[...])
        _rms_bwd_rows(da_, h_ref[...], g_ref[...], dr_ref[...], pl.program_id(0), dh_ref, dhb_ref, dgn_ref)

    row = pl.BlockSpec((tm, D), lambda i: (i, 0))
    return pl.pallas_call(
        body, name=name, grid=(t // tm,),
        in_specs=[pl.BlockSpec((tm, n), lambda i: (i, 0)), pl.BlockSpec((D, n), lambda i: (0, 0)),
                  row, pl.BlockSpec((1, D), lambda i: (0, 0)), row],
        out_specs=[row, row, pl.BlockSpec((8, D), lambda i: (0, 0))],
        out_shape=[_sds((t, D), F32), _sds((t, D), CDT), _sds((8, D), F32)],
        compiler_params=_params(("arbitrary",)),
    )(a, b, h, g, dres)


def _loss(h, target, name):
    t = h.shape[0]

    def body(h_ref, t_ref, dh_ref, dhb_ref, l_ref):
        i = pl.program_id(0)

        @pl.when(i == 0)
        def _():
            l_ref[...] = jnp.zeros(l_ref.shape, F32)
            dh_ref[...] = jnp.zeros(dh_ref.shape, F32)
            dhb_ref[...] = jnp.zeros(dhb_ref.shape, CDT)

        @pl.when(i > 0)
        def _():
            err = h_ref[...] - t_ref[...]
            l_ref[...] += (0.5 / D) * jnp.sum(err * err)
            d = err * (1.0 / D)
            dh_ref[...] = d
            dhb_ref[...] = d.astype(CDT)

    row = pl.BlockSpec((BLK, D), lambda i: (i, 0))
    return pl.pallas_call(
        body, name=name, grid=(t // BLK,),
        in_specs=[row, pl.BlockSpec((BLK, D), lambda i: (jnp.maximum(i - 1, 0), 0))],
        out_specs=[row, row, pl.BlockSpec((8, 128), lambda i: (0, 0))],
        out_shape=[_sds((t, D), F32), _sds((t, D), CDT), _sds((8, 128), F32)],
        compiler_params=_params(("arbitrary",)),
    )(h, target)


def _block_diag():
    return (_iota((128, 128), 0) // HD == _iota((128, 128), 1) // HD).astype(F32)


def _head_sums(v, bd):
    hi = v.astype(CDT)
    rest = (v - hi.astype(F32)).astype(CDT)
    b = bd.astype(CDT)
    return _dot(hi, b) + _dot(rest, b)


def _dup_halves(x, lo):
    sw = pltpu.roll(x, 64, 1)
    return jnp.where(lo, x, sw), jnp.where(lo, sw, x)


def _qknorm_fwd(proj, gfq, gfk, gsq, gsk, fb, name):
    t = proj.shape[0]
    tm = _row_tile(t)

    def body(qa, ka, va, qb, kb, vb, fa, gfq_r, gfk_r, gsq_r, gsk_r, fb_r,
             qf_o, kf_o, vf_o, qs_o, kse_o, vse_o, c_o, ct_o, qft_o, carry):
        i = pl.program_id(0)
        bd = _block_diag()
        lane = _iota((1, 128), 1)
        lo = lane < HD

        def hnorm(x, g):
            ms = _head_sums(x * x, bd) * (1.0 / HD)
            return x * lax.rsqrt(ms + EPS) * g

        for ch in range(4):
            sl = slice(128 * ch, 128 * (ch + 1))
            qn = hnorm(qa[:, sl], gfq_r[:, sl]) * 0.125
            qf_o[:, sl] = qn.astype(CDT)
            qft_o[sl, :] = qn.T.astype(CDT)
            kf_o[:, sl] = hnorm(ka[:, sl], gfk_r[:, sl]).astype(CDT)
            qs_o[:, sl] = (hnorm(qb[:, sl], gsq_r[:, sl]) * 0.125).astype(CDT)
        vf_o[...] = va[...].astype(CDT)
        k0, k1 = _dup_halves(hnorm(kb[...], gsk_r[...]), lo)
        kse_o[0] = k0.astype(CDT)
        kse_o[1] = k1.astype(CDT)
        v0, v1 = _dup_halves(vb[...], lo)
        vse_o[0] = v0.astype(CDT)
        vse_o[1] = v1.astype(CDT)

        z = fa[...] + fb_r[...]
        lf = jnp.minimum(z, 0.0) - jnp.log(1.0 + jnp.exp(-jnp.abs(z)))
        lf = jnp.where(lane < 8, lf, 0.0)
        ltri = (_iota((tm, tm), 1) <= _iota((tm, tm), 0)).astype(F32)

        @pl.when(i == 0)
        def _():
            carry[...] = jnp.zeros(carry.shape, F32)

        c = _dot_hi(ltri, lf) + carry[0:1, :]
        carry[0:1, :] = c[tm - 1:tm, :]
        c_o[...] = c
        ct_o[...] = c.T[0:8, :]

    def col(width, off):
        return pl.BlockSpec((tm, width), lambda i: (i, off // width))

    def vec(width):
        return pl.BlockSpec((1, width), lambda i: (0, 0))

    return pl.pallas_call(
        body, name=name, grid=(t // tm,),
        in_specs=[col(512, QA), col(512, KA), col(512, VA), col(512, QB), col(128, KB), col(128, VB), col(128, FA),
                  vec(512), vec(512), vec(512), vec(128), vec(128)],
        out_specs=[pl.BlockSpec((tm, 512), lambda i: (i, 0))] * 4
        + [pl.BlockSpec((2, tm, 128), lambda i: (0, i, 0))] * 2
        + [pl.BlockSpec((tm, 128), lambda i: (i, 0)), pl.BlockSpec((8, tm), lambda i: (0, i)),
           pl.BlockSpec((512, tm), lambda i: (0, i))],
        out_shape=[_sds((t, 512), CDT)] * 4 + [_sds((2, t, 128), CDT)] * 2
        + [_sds((t, 128), F32), _sds((8, t), F32), _sds((512, t), CDT)],
        scratch_shapes=[pltpu.VMEM((8, 128), F32)],
        compiler_params=_params(("arbitrary",)),
    )(proj, proj, proj, proj, proj, proj, proj, gfq, gfk, gsq, gsk, fb)


def _qknorm_bwd(proj, dqf, dkf, dvf, dqs, dkse, dvse, dcq, dck, dga, dgb, gfq, gfk, gsq, gsk, fb, name):
    t = proj.shape[0]
    tm = _row_tile(t)
    nt = t // tm

    def body(qa, ka, qb, kb, fa, dqf_r, dkf_r, dvf_r, dqs_r, dkse_r, dvse_r, dcq_r, dck_r, dga_r, dgb_r,
             gfq_r, gfk_r, gsq_r, gsk_r, fb_r, dp_o, dgn_o, carry, acc):
        i = pl.program_id(0)
        bd = _block_diag()
        lane = _iota((1, 128), 1)
        lo = lane < HD

        @pl.when(i == 0)
        def _():
            carry[...] = jnp.zeros(carry.shape, F32)
            acc[...] = jnp.zeros(acc.shape, F32)

        def hnorm_bwd(x, g, dy):
            r = lax.rsqrt(_head_sums(x * x, bd) * (1.0 / HD) + EPS)
            xh = x * r
            day = dy * g
            dx = r * (day - xh * (_head_sums(day * xh, bd) * (1.0 / HD)))
            return dx, jnp.sum(dy * xh, axis=0, keepdims=True)

        for ch in range(4):
            sl = slice(128 * ch, 128 * (ch + 1))
            dx, dg = hnorm_bwd(qa[:, sl], gfq_r[:, sl], dqf_r[:, sl] * 0.125)
            dp_o[:, QA + 128 * ch:QA + 128 * (ch + 1)] = dx.astype(CDT)
            acc[0:1, sl] += dg
            dx, dg = hnorm_bwd(ka[:, sl], gfk_r[:, sl], dkf_r[:, sl])
            dp_o[:, KA + 128 * ch:KA + 128 * (ch + 1)] = dx.astype(CDT)
            acc[1:2, sl] += dg
            dx, dg = hnorm_bwd(qb[:, sl], gsq_r[:, sl], dqs_r[:, sl] * 0.125)
            dp_o[:, QB + 128 * ch:QB + 128 * (ch + 1)] = dx.astype(CDT)
            acc[2:3, sl] += dg
        dp_o[:, VA:VA + 512] = dvf_r[...].astype(CDT)
        dp_o[:, GA:GA + D] = dga_r[...]
        dp_o[:, GB:GB + D] = dgb_r[...]

        def fold(x):
            e0 = x[0]
            e1 = x[1]
            return jnp.where(lo, e0 + pltpu.roll(e0, 64, 1), e1 + pltpu.roll(e1, 64, 1))

        dx, dg = hnorm_bwd(kb[...], gsk_r[...], fold(dkse_r))
        dp_o[:, KB:KB + 128] = dx.astype(CDT)
        acc[3:4, 0:128] += dg
        dp_o[:, VB:VB + 128] = fold(dvse_r).astype(CDT)

        rr = _iota((512, 128), 0)
        hh = _iota((512, 128), 1)
        sel = ((rr == (hh >> 1) * 128 + (hh & 1) * HD) & (hh < 8)).astype(F32)
        dcs = _dot_hi(dcq_r[...] - dck_r[...], sel)
        utri = (_iota((tm, tm), 1) >= _iota((tm, tm), 0)).astype(F32)
        dlf = _dot_hi(utri, dcs) + carry[0:1, :]
        carry[0:1, :] = dlf[0:1, :]
        z = fa[...] + fb_r[...]
        dfa = jnp.where(lane < 8, dlf * _sigmoid(-z), 0.0)
        dp_o[:, FA:FA + 128] = dfa.astype(CDT)
        acc[4:5, 0:128] += jnp.sum(dfa, axis=0, keepdims=True)

        @pl.when(i == nt - 1)
        def _():
            foldm = ((_iota((512, 128), 0) & (HD - 1)) == _iota((512, 128), 1)).astype(F32)
            dgn_o[...] = _dot_hi(acc[...], foldm)

    def col(width, off):
        return pl.BlockSpec((tm, width), lambda i: (nt - 1 - i, off // width))

    def rows(width):
        return pl.BlockSpec((tm, width), lambda i: (nt - 1 - i, 0))

    def vec(width):
        return pl.BlockSpec((1, width), lambda i: (0, 0))

    pair = pl.BlockSpec((2, tm, 128), lambda i: (0, nt - 1 - i, 0))
    return pl.pallas_call(
        body, name=name, grid=(nt,),
        in_specs=[col(512, QA), col(512, KA), col(512, QB), col(128, KB), col(128, FA),
                  rows(512), rows(512), rows(512), rows(512), pair, pair, rows(512), rows(512), rows(D), rows(D),
                  vec(512), vec(512), vec(512), vec(128), vec(128)],
        out_specs=[rows(DP), pl.BlockSpec((8, 128), lambda i: (0, 0))],
        out_shape=[_sds((t, DP), CDT), _sds((8, 128), F32)],
        scratch_shapes=[pltpu.VMEM((8, 128), F32), pltpu.VMEM((8, 512), F32)],
        compiler_params=_params(("arbitrary",)),
    )(proj, proj, proj, proj, proj, dqf, dkf, dvf, dqs, dkse, dvse, dcq, dck, dga, dgb, gfq, gfk, gsq, gsk, fb)


def _gate_out_fwd(ofox, oswa, wbf, wbs, proj, w_out, h, name):
    t = ofox.shape[0]
    tm = _row_tile(t)

    def body(of_r, os_r, wf_r, ws_r, ga_r, gb_r, wo_r, h_r, ho_o, yt_o, pf_o, ps_o, oft_o, ost_o):
        pf = _dot(of_r[...], wf_r[...])
        ps = _dot(os_r[...], ws_r[...])
        y = _sigmoid(ga_r[...]) * pf + _sigmoid(gb_r[...]) * ps
        ho_o[...] = h_r[...] + _dot(y.astype(CDT), wo_r[...])
        yt_o[...] = y.T.astype(CDT)
        pf_o[...] = pf.astype(CDT)
        ps_o[...] = ps.astype(CDT)
        oft_o[...] = of_r[...].astype(F32).T.astype(CDT)
        ost_o[...] = os_r[...].astype(F32).T.astype(CDT)

    row = pl.BlockSpec((tm, D), lambda i: (i, 0))
    half = pl.BlockSpec((tm, 512), lambda i: (i, 0))
    whole = lambda r: pl.BlockSpec((r, D), lambda i: (0, 0))
    tcol = lambda r: pl.BlockSpec((r, tm), lambda i: (0, i))
    return pl.pallas_call(
        body, name=name, grid=(t // tm,),
        in_specs=[half, half, whole(512), whole(512),
                  pl.BlockSpec((tm, D), lambda i: (i, GA // D)), pl.BlockSpec((tm, D), lambda i: (i, GB // D)),
                  whole(D), row],
        out_specs=[row, tcol(D), row, row, tcol(512), tcol(512)],
        out_shape=[_sds((t, D), F32), _sds((D, t), CDT), _sds((t, D), CDT), _sds((t, D), CDT),
                   _sds((512, t), CDT), _sds((512, t), CDT)],
        compiler_params=_params(("parallel",)),
    )(ofox, oswa, wbf, wbs, proj, proj, w_out, h)


def _gate_out_bwd(dhb, w_out, pf, ps, proj, name):
    t = dhb.shape[0]
    tm = _row_tile(t)

    def body(dh_r, wo_r, pf_r, ps_r, ga_r, gb_r, dpf_o, dps_o, dga_o, dgb_o):
        dy_ = _dot_nt(dh_r[...], wo_r[...])
        sa = _sigmoid(ga_r[...])
        sb = _sigmoid(gb_r[...])
        dpf_o[...] = (dy_ * sa).astype(CDT)
        dps_o[...] = (dy_ * sb).astype(CDT)
        dga_o[...] = (dy_ * pf_r[...].astype(F32) * (sa * (1.0 - sa))).astype(CDT)
        dgb_o[...] = (dy_ * ps_r[...].astype(F32) * (sb * (1.0 - sb))).astype(CDT)

    row = pl.BlockSpec((tm, D), lambda i: (i, 0))
    return pl.pallas_call(
        body, name=name, grid=(t // tm,),
        in_specs=[row, pl.BlockSpec((D, D), lambda i: (0, 0)), row, row,
                  pl.BlockSpec((tm, D), lambda i: (i, GA // D)), pl.BlockSpec((tm, D), lambda i: (i, GB // D))],
        out_specs=[row] * 4,
        out_shape=[_sds((t, D), CDT)] * 4,
        compiler_params=_params(("parallel",)),
    )(dhb, w_out, pf, ps, proj, proj)


def _tri_steps(n, by_key):
    if by_key:
        pairs = [(i, j) for j in range(n) for i in range(j, n)]
    else:
        pairs = [(i, j) for i in range(n) for j in range(i + 1)]
    return (np.array([p[0] for p in pairs], np.int32), np.array([p[1] for p in pairs], np.int32))


def _head_col(blk, lane, h):
    return jnp.sum(jnp.where(lane == h, blk, 0.0), axis=1, keepdims=True)


def _head_row(blk, sub, h):
    return jnp.sum(jnp.where(sub == h, blk, 0.0), axis=0, keepdims=True)


def _ride_specs(ride):
    if ride is None:
        return [], [], [], [], []
    kind, srcs, outs, layer, items = ride
    return list(srcs), [ANY] * len(srcs), list(outs), [ANY] * len(outs), _dma_sems(3 * len(srcs))


def _ride_start(ride, srcs, dsts, send_sems, recv_sems):
    for cp in _ici_copies(ride[0], srcs, dsts, send_sems, recv_sems, ride[3], recv=False, items=ride[4])[0]:
        cp.start()


def _ride_wait(ride, srcs, dsts, send_sems, recv_sems):
    sends, recvs = _ici_copies(ride[0], srcs, dsts, send_sems, recv_sems, ride[3], items=ride[4])
    for cp in recvs:
        cp.wait_recv()
    for cp in sends:
        cp.wait_send()


def _riding(body, n_in, n_out, ride, grid):
    if ride is None:
        return body
    nr = len(ride[1])

    def wrapped(*refs):
        ins, srcs = refs[:n_in], refs[n_in:n_in + nr]
        outs, dsts = refs[n_in + nr:n_in + nr + n_out], refs[n_in + nr + n_out:n_in + 2 * nr + n_out]
        scratch, sems = refs[n_in + 2 * nr + n_out:-2], refs[-2:]
        first = pl.program_id(0) == 0
        last = pl.program_id(0) == grid[0] - 1
        for a in range(1, len(grid)):
            first = first & (pl.program_id(a) == 0)
            last = last & (pl.program_id(a) == grid[a] - 1)

        @pl.when(first)
        def _():
            _ride_start(ride, srcs, dsts, *sems)

        body(*ins, *outs, *scratch)

        @pl.when(last)
        def _():
            _ride_wait(ride, srcs, dsts, *sems)

    return wrapped


def _fox_fwd(qf, kf, vf, c, ct, name, ride=None):
    t = qf.shape[0]
    ta = _row_tile(t)
    qi, kj = _tri_steps(t // ta, by_key=False)
    nsteps = len(qi)
    ride_in, ride_in_specs, ride_out, ride_out_specs, ride_sems = _ride_specs(ride)

    def body(qi_r, kj_r, q_r, k_r, v_r, c_r, ct_r, *rest):
        nr = len(ride_in)
        src_r, (o_o, lse_o), dst_o = rest[:nr], rest[nr:nr + 2], rest[nr + 2:2 * nr + 2]
        m_sc, l_sc, acc_sc, cq_sc, *sems = rest[2 * nr + 2:]
        p = pl.program_id(0)
        n = pl.program_id(1)
        i = qi_r[n]
        j = kj_r[n]
        lane = _iota((1, 128), 1)
        lo = lane < HD

        if ride is not None:
            @pl.when((p == 0) & (n == 0))
            def _():
                _ride_start(ride, src_r, dst_o, *sems)

        @pl.when(j == 0)
        def _():
            m_sc[...] = jnp.full(m_sc.shape, NEG, F32)
            l_sc[...] = jnp.zeros(l_sc.shape, F32)
            acc_sc[...] = jnp.zeros(acc_sc.shape, F32)
            for e in (0, 1):
                cq_sc[e] = jnp.broadcast_to(_head_col(c_r[...], lane, 2 * p + e), (ta, 128))

        def step(masked):
            q = q_r[...]
            k = k_r[...]
            vaug = jnp.concatenate([v_r[...], jnp.ones((ta, 128), CDT)], axis=1)
            if masked:
                rows = i * ta + _iota((ta, 1), 0)
                cols = j * ta + _iota((1, ta), 1)
                mask = (cols <= rows) & (cols >= PAD)
            sub = _iota((8, 1), 0)
            heads = (0, 1)
            sels = [lo, jnp.logical_not(lo)]
            s = [_dot_nt(jnp.where(sels[e], q, 0), k) for e in heads]
            ck = [_head_row(ct_r[...], sub, 2 * p + e) for e in heads]
            chunks = []
            for e in heads:
                cq = cq_sc[e]
                row = []
                for ch in range(ta // 128):
                    sl = slice(128 * ch, 128 * (ch + 1))
                    sc = s[e][:, sl] + cq - ck[e][:, sl]
                    if masked:
                        sc = jnp.where(mask[:, sl], sc, NEG)
                    row.append(sc)
                chunks.append(row)
            m_new, alphas = [], []
            for e in heads:
                mx = chunks[e][0]
                for sc in chunks[e][1:]:
                    mx = jnp.maximum(mx, sc)
                m_prev = m_sc[e]
                m_new.append(jnp.maximum(m_prev, jnp.max(mx, axis=1, keepdims=True)))
                alphas.append(jnp.exp(m_prev - m_new[e]))
            pe = [jnp.concatenate([jnp.exp(sc - m_new[e]).astype(CDT) for sc in chunks[e]], axis=1) for e in heads]
            pva = [_dot(pe[e], vaug) for e in heads]
            for e in heads:
                l_sc[e] = alphas[e] * l_sc[e] + pva[e][:, 128:]
                m_sc[e] = m_new[e]
            acc_sc[...] = (acc_sc[...] * jnp.where(lo, alphas[0], alphas[1])
                           + jnp.where(lo, pva[0][:, :128], pva[1][:, :128]))

        edge = (j == i) | (j == 0)

        @pl.when(edge)
        def _():
            step(True)

        @pl.when(jnp.logical_not(edge))
        def _():
            step(False)

        @pl.when(j == i)
        def _():
            l = jnp.where(lo, l_sc[0], l_sc[1])
            o_o[...] = (acc_sc[...] / l).astype(CDT)
            lse_o[...] = jnp.where(lo, m_sc[0], m_sc[1]) + jnp.log(l)

        if ride is not None:
            @pl.when((p == NPAIR - 1) & (n == nsteps - 1))
            def _():
                _ride_wait(ride, src_r, dst_o, *sems)

    qblk = pl.BlockSpec((ta, 128), lambda p, n, qi_r, kj_r: (qi_r[n], p))
    kblk = pl.BlockSpec((ta, 128), lambda p, n, qi_r, kj_r: (kj_r[n], p))
    grid_spec = pltpu.PrefetchScalarGridSpec(
        num_scalar_prefetch=2, grid=(NPAIR, nsteps),
        in_specs=[qblk, kblk, kblk,
                  pl.BlockSpec((ta, 128), lambda p, n, qi_r, kj_r: (qi_r[n], 0)),
                  pl.BlockSpec((8, ta), lambda p, n, qi_r, kj_r: (0, kj_r[n]))] + ride_in_specs,
        out_specs=[qblk, qblk] + ride_out_specs,
        scratch_shapes=[pltpu.VMEM((2, ta, 128), F32), pltpu.VMEM((2, ta, 128), F32), pltpu.VMEM((ta, 128), F32),
                        pltpu.VMEM((2, ta, 128), F32)] + ride_sems,
    )
    return pl.pallas_call(
        body, name=name, grid_spec=grid_spec,
        out_shape=[_sds((t, 512), CDT), _sds((t, 512), F32)] + ride_out,
        compiler_params=_params(("arbitrary", "arbitrary")),
    )(jnp.asarray(qi), jnp.asarray(kj), qf, kf, vf, c, ct, *ride_in)


def _fox_bwd(qf, qft, kf, vf, c, ct, o, lse, do, dot, name, ride=None):
    t = qf.shape[0]
    ta = _row_tile(t)
    nq = t // ta
    qi, kj = _tri_steps(nq, by_key=False)
    nsteps = len(qi)
    ride_in, ride_in_specs, ride_out, ride_out_specs, ride_sems = _ride_specs(ride)

    def body(qi_r, kj_r, q_r, qt_r, k_r, v_r, c_r, ct_r, o_r, lse_r, do_r, dot_r, *rest):
        nr = len(ride_in)
        src_r, (dq_o, dcq_o, dk_o, dv_o, dck_o), dst_o = rest[:nr], rest[nr:nr + 5], rest[nr + 5:2 * nr + 5]
        lse_sc, dl_sc, cq_sc, dq_sc, dcq_sc, dkt_sc, dvt_sc, dckt_sc, *sems = rest[2 * nr + 5:]
        p = pl.program_id(0)
        n = pl.program_id(1)
        i = qi_r[n]
        j = kj_r[n]
        lane = _iota((1, 128), 1)
        lo = lane < HD
        top = _iota((128, 1), 0) < HD

        if ride is not None:
            @pl.when((p == 0) & (n == 0))
            def _():
                _ride_start(ride, src_r, dst_o, *sems)

        @pl.when(n == 0)
        def _():
            dkt_sc[...] = jnp.zeros(dkt_sc.shape, F32)
            dvt_sc[...] = jnp.zeros(dvt_sc.shape, F32)
            dckt_sc[...] = jnp.zeros(dckt_sc.shape, F32)

        @pl.when(j == 0)
        def _():
            dq_sc[...] = jnp.zeros(dq_sc.shape, F32)
            dcq_sc[...] = jnp.zeros(dcq_sc.shape, F32)
            dd = do_r[...] * o_r[...].astype(F32)
            lse = lse_r[...]
            for e in (0, 1):
                sel = lo if e == 0 else jnp.logical_not(lo)
                cq_sc[e] = jnp.broadcast_to(_head_col(c_r[...], lane, 2 * p + e), (ta, 128))
                dl_sc[e] = jnp.broadcast_to(jnp.sum(jnp.where(sel, dd, 0.0), axis=1, keepdims=True), (ta, 128))
                lse_sc[e] = jnp.broadcast_to(lse[:, HD * e:HD * e + 1], (ta, 128))

        def step(masked):
            q = q_r[...]
            qt = qt_r[...]
            k = k_r[...]
            v = v_r[...]
            dob = do_r[...].astype(CDT)
            dot_ = dot_r[...]
            ones = jnp.ones((ta, 128), CDT)
            ones16 = jnp.ones((16, ta), CDT)
            if masked:
                rows = i * ta + _iota((ta, 1), 0)
                cols = j * ta + _iota((1, ta), 1)
                mask = (cols <= rows) & (cols >= PAD)
            sub = _iota((8, 1), 0)
            heads = (0, 1)
            sels = [lo, jnp.logical_not(lo)]
            rsels = [top, jnp.logical_not(top)]
            s = [_dot_nt(jnp.where(sels[e], q, 0), k) for e in heads]
            dp = [_dot_nt(jnp.where(sels[e], dob, 0), v) for e in heads]
            ck = [_head_row(ct_r[...], sub, 2 * p + e) for e in heads]
            pb, dsb = [], []
            for e in heads:
                cq, lse_e, dl = cq_sc[e], lse_sc[e], dl_sc[e]
                prs, dss = [], []
                for ch in range(ta // 128):
                    sl = slice(128 * ch, 128 * (ch + 1))
                    sc = s[e][:, sl] + cq - ck[e][:, sl]
                    if masked:
                        sc = jnp.where(mask[:, sl], sc, NEG)
                    pr = jnp.exp(sc - lse_e)
                    prs.append(pr.astype(CDT))
                    dss.append((pr * (dp[e][:, sl] - dl)).astype(CDT))
                pb.append(jnp.concatenate(prs, axis=1))
                dsb.append(jnp.concatenate(dss, axis=1))
            dvt = [_dot(jnp.where(rsels[e], dot_, 0), pb[e]) for e in heads]
            dkc = [_dot(jnp.concatenate([jnp.where(rsels[e], qt, 0), ones16], axis=0), dsb[e]) for e in heads]
            dqa = [_dot(dsb[e], jnp.concatenate([jnp.where(sels[e], k, 0), ones], axis=1)) for e in heads]
            dvt_sc[j] += dvt[0] + dvt[1]
            dkt_sc[j] += dkc[0][0:128] + dkc[1][0:128]
            dckt_sc[j, 0:8, :] += jnp.where(sub == 0, dkc[0][128:136], jnp.where(sub == 1, dkc[1][128:136], 0.0))
            dq_sc[...] += dqa[0][:, :128] + dqa[1][:, :128]
            for e in heads:
                dcq_sc[e] += dqa[e][:, 128:]

        edge = (j == i) | (j == 0)

        @pl.when(edge)
        def _():
            step(True)

        @pl.when(jnp.logical_not(edge))
        def _():
            step(False)

        @pl.when(j == i)
        def _():
            dq_o[...] = dq_sc[...]
            dcq_o[...] = jnp.where(lo, dcq_sc[0], dcq_sc[1])

        @pl.when(n == nsteps - 1)
        def _():
            spread = (_iota((128, 128), 1) == _iota((128, 128), 0) // HD).astype(F32)
            for jb in range(nq):
                rs = slice(jb * ta, (jb + 1) * ta)
                dk_o[rs, :] = dkt_sc[jb].T
                dv_o[rs, :] = dvt_sc[jb].T
                dck_o[rs, :] = _dot_hi(spread, dckt_sc[jb]).T

        if ride is not None:
            @pl.when((p == NPAIR - 1) & (n == nsteps - 1))
            def _():
                _ride_wait(ride, src_r, dst_o, *sems)

    qblk = pl.BlockSpec((ta, 128), lambda p, n, qi_r, kj_r: (qi_r[n], p))
    qtblk = pl.BlockSpec((128, ta), lambda p, n, qi_r, kj_r: (p, qi_r[n]))
    kblk = pl.BlockSpec((ta, 128), lambda p, n, qi_r, kj_r: (kj_r[n], p))
    whole = pl.BlockSpec((t, 128), lambda p, n, qi_r, kj_r: (0, p))
    grid_spec = pltpu.PrefetchScalarGridSpec(
        num_scalar_prefetch=2, grid=(NPAIR, nsteps),
        in_specs=[qblk, qtblk, kblk, kblk,
                  pl.BlockSpec((ta, 128), lambda p, n, qi_r, kj_r: (qi_r[n], 0)),
                  pl.BlockSpec((8, ta), lambda p, n, qi_r, kj_r: (0, kj_r[n])),
                  qblk, qblk, qblk, qtblk] + ride_in_specs,
        out_specs=[qblk, qblk, whole, whole, whole] + ride_out_specs,
        scratch_shapes=[pltpu.VMEM((2, ta, 128), F32)] * 3 + [pltpu.VMEM((ta, 128), F32), pltpu.VMEM((2, ta, 128), F32)]
        + [pltpu.VMEM((nq, 128, ta), F32)] * 3 + ride_sems,
    )
    return pl.pallas_call(
        body, name=name, grid_spec=grid_spec,
        out_shape=[_sds((t, 512), F32)] * 5 + ride_out,
        compiler_params=_params(("arbitrary", "arbitrary")),
    )(jnp.asarray(qi), jnp.asarray(kj), qf, qft, kf, vf, c, ct, o, lse, do, dot, *ride_in)


def _bucket_table():
    r = np.arange(BLK)[:, None]
    c = np.arange(3 * BLK)[None, :]
    d = np.where(c < BLK, r + BLK - c, r - (c - BLK))
    n = np.maximum(d, 0)
    max_exact = N_BUCKETS // 2
    nf = np.maximum(n, 1).astype(np.float32)
    large = max_exact + (np.log(nf / max_exact) / math.log(BLK / max_exact) * (N_BUCKETS - max_exact)).astype(np.int32)
    large = np.minimum(large, N_BUCKETS - 1)
    b = np.where(n < max_exact, n, large)
    return np.where(c < 2 * BLK, b, N_BUCKETS - 1).astype(np.int32)


def _bias_fwd(table, name):
    bucket = jnp.asarray(_bucket_table())

    def body(tab_r, b_r, o_o):
        h = pl.program_id(0)
        b = b_r[...]
        acc = jnp.zeros(b.shape, F32)
        for k in range(N_BUCKETS):
            acc = jnp.where(b == k, tab_r[k, h], acc)
        o_o[...] = acc

    return pl.pallas_call(
        body, name=name, grid=(8,),
        in_specs=[pl.BlockSpec(memory_space=pltpu.SMEM), pl.BlockSpec((BLK, 3 * BLK), lambda h: (0, 0))],
        out_specs=pl.BlockSpec((None, BLK, 3 * BLK), lambda h: (h, 0, 0)),
        out_shape=_sds((8, BLK, 3 * BLK), F32),
        compiler_params=_params(("parallel",)),
    )(table, bucket)


def _bias_bwd(dbias, name):
    bucket = jnp.asarray(_bucket_table())

    def body(d_r, b_r, o_o):
        h = pl.program_id(0)
        b = b_r[...]
        d = d_r[...]
        lane = _iota((1, 128), 1)
        row = jnp.zeros((1, 128), F32)
        for k in range(N_BUCKETS):
            row = jnp.where(lane == k, jnp.sum(jnp.where(b == k, d, 0.0)), row)
        o_o[pl.ds(h, 1), :] = row

    return pl.pallas_call(
        body, name=name, grid=(8,),
        in_specs=[pl.BlockSpec((None, BLK, 3 * BLK), lambda h: (h, 0, 0)), pl.BlockSpec((BLK, 3 * BLK), lambda h: (0, 0))],
        out_specs=pl.BlockSpec((8, 128), lambda h: (0, 0)),
        out_shape=_sds((8, 128), F32),
        compiler_params=_params(("arbitrary",)),
    )(dbias, bucket)


def _swa_valid(i):
    r = _iota((BLK, 1), 0)
    c = _iota((1, 3 * BLK), 1)
    prev = (c < BLK) & (c > r) & (i >= 1) & ((i - 1) * BLK + c >= PAD)
    cc = c - BLK
    cur = (c >= BLK) & (c < 2 * BLK) & (cc <= r) & (i * BLK + cc >= PAD)
    cm = c - 2 * BLK
    meta = (c >= 2 * BLK) & (cm >= PAD) & (i * BLK + r - cm >= BLK)
    return prev | cur | meta


def _swa_kv_specs(ta):
    nb = ta // BLK
    return [pl.BlockSpec((None, BLK, 128), lambda p, i: (p // 2, jnp.maximum(i * nb - 1, 0), 0)),
            pl.BlockSpec((None, ta, 128), lambda p, i: (p // 2, i, 0)),
            pl.BlockSpec((None, BLK, 128), lambda p, i: (p // 2, 0, 0))]


def _swa_fwd(qs, kse, vse, bias, sinks, name, ride=None):
    t = qs.shape[0]
    ta = _row_tile(t)
    nb = ta // BLK
    grid = (NPAIR, t // ta)
    ride_in, ride_in_specs, ride_out, ride_out_specs, ride_sems = _ride_specs(ride)

    def body(sink_r, q_r, kp_r, kc_r, km_r, vp_r, vc_r, vm_r, b_r, o_o, lse_o):
        p = pl.program_id(0)
        i = pl.program_id(1)
        lo = _iota((1, 128), 1) < HD
        k4 = jnp.concatenate([kp_r[...], kc_r[...]], axis=0)
        v4 = jnp.concatenate([vp_r[...], vc_r[...]], axis=0)
        work = [(b, e) for b in range(nb) for e in (0, 1)]
        sinks = [sink_r[2 * p + e] for e in (0, 1)]
        v3 = [jnp.concatenate([v4[BLK * b:BLK * (b + 2)], vm_r[...]], axis=0) for b in range(nb)]
        s = {}
        for b in range(nb):
            q = q_r[BLK * b:BLK * (b + 1), :]
            k3 = jnp.concatenate([k4[BLK * b:BLK * (b + 2)], km_r[...]], axis=0)
            valid = _swa_valid(i * nb + b)
            for e in (0, 1):
                sel = lo if e == 0 else jnp.logical_not(lo)
                s[b, e] = jnp.where(valid, _dot_nt(jnp.where(sel, q, 0), k3) + b_r[e], NEG)
        mx = {w: jnp.maximum(jnp.max(s[w], axis=1, keepdims=True), sinks[w[1]]) for w in work}
        pe = {w: jnp.exp(s[w] - mx[w]) for w in work}
        den = {w: jnp.sum(pe[w], axis=1, keepdims=True) + jnp.exp(sinks[w[1]] - mx[w]) for w in work}
        out = {w: _dot(pe[w].astype(CDT), v3[w[0]]) / den[w] for w in work}
        for b in range(nb):
            rows = slice(BLK * b, BLK * (b + 1))
            o_o[rows, :] = jnp.where(lo, out[b, 0], out[b, 1]).astype(CDT)
            lse_o[rows, :] = jnp.where(lo, mx[b, 0] + jnp.log(den[b, 0]), mx[b, 1] + jnp.log(den[b, 1]))

    qblk = pl.BlockSpec((ta, 128), lambda p, i: (i, p))
    res = pl.pallas_call(
        _riding(body, 9, 2, ride, grid), name=name, grid=grid,
        in_specs=[pl.BlockSpec(memory_space=pltpu.SMEM), qblk] + _swa_kv_specs(ta) + _swa_kv_specs(ta)
        + [pl.BlockSpec((2, BLK, 3 * BLK), lambda p, i: (p, 0, 0))] + ride_in_specs,
        out_specs=[qblk, qblk] + ride_out_specs,
        out_shape=[_sds((t, 512), CDT), _sds((t, 512), F32)] + ride_out, scratch_shapes=ride_sems,
        compiler_params=_params(("arbitrary", "arbitrary") if ride else ("parallel", "parallel")),
    )(sinks, qs, kse, kse, kse, vse, vse, vse, bias, *ride_in)
    return (res[0], res[1], res[2:]) if ride else res


def _swa_bwd(qs, kse, vse, bias, sinks, o, lse, do, name):
    t = qs.shape[0]
    ta = _row_tile(t)
    nb = ta // BLK

    def body(sink_r, q_r, kp_r, kc_r, km_r, vp_r, vc_r, vm_r, b_r, o_r, lse_r, do_r,
             dq_o, dk_o, dv_o, db_o, dsk_o):
        p = pl.program_id(0)
        i = pl.program_id(1)
        lo = _iota((1, 128), 1) < HD

        @pl.when((i == 0) & (p % 2 == 0))
        def _():
            dk_o[...] = jnp.zeros(dk_o.shape, F32)
            dv_o[...] = jnp.zeros(dv_o.shape, F32)

        @pl.when(i == 0)
        def _():
            db_o[...] = jnp.zeros(db_o.shape, F32)
            dsk_o[...] = jnp.zeros(dsk_o.shape, F32)

        k4 = jnp.concatenate([kp_r[...], kc_r[...]], axis=0)
        v4 = jnp.concatenate([vp_r[...], vc_r[...]], axis=0)
        work = [(b, e) for b in range(nb) for e in (0, 1)]
        sel = [lo, jnp.logical_not(lo)]
        k3 = [jnp.concatenate([k4[BLK * b:BLK * (b + 2)], km_r[...]], axis=0) for b in range(nb)]
        v3 = [jnp.concatenate([v4[BLK * b:BLK * (b + 2)], vm_r[...]], axis=0) for b in range(nb)]
        q = [q_r[BLK * b:BLK * (b + 1), :] for b in range(nb)]
        do_ = [do_r[BLK * b:BLK * (b + 1), :] for b in range(nb)]
        lse = [lse_r[BLK * b:BLK * (b + 1), :] for b in range(nb)]
        dd = [do_[b] * o_r[BLK * b:BLK * (b + 1), :].astype(F32) for b in range(nb)]
        valid = [_swa_valid(i * nb + b) for b in range(nb)]
        qe = {(b, e): jnp.where(sel[e], q[b], 0) for b, e in work}
        doe = {(b, e): jnp.where(sel[e], do_[b], 0.0).astype(CDT) for b, e in work}
        lse_e = {(b, e): lse[b][:, HD * e:HD * e + 1] for b, e in work}
        delta = {(b, e): jnp.sum(jnp.where(sel[e], dd[b], 0.0), axis=1, keepdims=True) for b, e in work}
        s = {(b, e): jnp.where(valid[b], _dot_nt(qe[b, e], k3[b]) + b_r[e], NEG) for b, e in work}
        dp = {(b, e): _dot_nt(doe[b, e], v3[b]) for b, e in work}
        pr = {w: jnp.exp(s[w] - lse_e[w]) for w in work}
        ds = {w: pr[w] * (dp[w] - delta[w]) for w in work}
        dqs = {(b, e): _dot(ds[b, e].astype(CDT), jnp.where(sel[e], k3[b], 0)) for b, e in work}
        both = lambda x, b: jnp.concatenate([x[b, 0], x[b, 1]], axis=0)
        dk3 = [_dot(both(ds, b).T.astype(CDT), both(qe, b)) for b in range(nb)]
        dv3 = [_dot(both(pr, b).T.astype(CDT), both(doe, b)) for b in range(nb)]
        for e in (0, 1):
            tot = ds[0, e]
            for b in range(1, nb):
                tot = tot + ds[b, e]
            db_o[e] += tot
        dsink = [sum(-jnp.sum(jnp.exp(sink_r[2 * p + e] - lse_e[b, e]) * delta[b, e], axis=0, keepdims=True)
                     for b in range(nb)) for e in (0, 1)]
        dsk_o[0:1, :] += jnp.where(lo, dsink[0], dsink[1])
        for b in range(nb):
            ib = i * nb + b
            dq_o[BLK * b:BLK * (b + 1), :] = dqs[b, 0] + dqs[b, 1]
            dk = dk3[b]
            dv = dv3[b]
            prev = pl.ds(pl.multiple_of(jnp.maximum(ib - 1, 0) * BLK, BLK), BLK)
            cur = pl.ds(pl.multiple_of(ib * BLK, BLK), BLK)
            dk_o[prev, :] += dk[0:BLK]
            dk_o[cur, :] += dk[BLK:2 * BLK]
            dk_o[0:BLK, :] += dk[2 * BLK:]
            dv_o[prev, :] += dv[0:BLK]
            dv_o[cur, :] += dv[BLK:2 * BLK]
            dv_o[0:BLK, :] += dv[2 * BLK:]

    qblk = pl.BlockSpec((ta, 128), lambda p, i: (i, p))
    kvacc = pl.BlockSpec((None, t, 128), lambda p, i: (p // 2, 0, 0))
    bblk = pl.BlockSpec((2, BLK, 3 * BLK), lambda p, i: (p, 0, 0))
    return pl.pallas_call(
        body, name=name, grid=(NPAIR, t // ta),
        in_specs=[pl.BlockSpec(memory_space=pltpu.SMEM), qblk] + _swa_kv_specs(ta) + _swa_kv_specs(ta)
        + [bblk, qblk, qblk, qblk],
        out_specs=[qblk, kvacc, kvacc, bblk, pl.BlockSpec((None, 8, 128), lambda p, i: (p, 0, 0))],
        out_shape=[_sds((t, 512), F32), _sds((2, t, 128), F32), _sds((2, t, 128), F32),
                   _sds((8, BLK, 3 * BLK), F32), _sds((NPAIR, 8, 128), F32)],
        compiler_params=_params(("arbitrary", "arbitrary")),
    )(sinks, qs, kse, kse, kse, vse, vse, vse, bias, o, lse, do)


def _sum8(slots, name):
    def body(a_r, o_o):
        acc = a_r[0]
        for k in range(1, 8):
            acc = acc + a_r[k]
        o_o[...] = acc

    return pl.pallas_call(
        body, name=name, out_shape=_sds((SMALL_ROWS, 128), F32),
        in_specs=[pl.BlockSpec(memory_space=pltpu.VMEM)], out_specs=pl.BlockSpec(memory_space=pltpu.VMEM),
        compiler_params=_params(),
    )(slots)


def _place():
    x, y, c = lax.axis_index("x"), lax.axis_index("y"), lax.axis_index("c")
    chips = [(1 - x, y), (x, 1 - y), (1 - x, 1 - y)]
    return x, y, c, chips


def _remote(src, dst, send_sems, recv_sems, k, to):
    return pltpu.make_async_remote_copy(src_ref=src, dst_ref=dst, send_sem=send_sems.at[k], recv_sem=recv_sems.at[k],
                                        device_id=to, device_id_type=MESH_ID)


ANY = pl.BlockSpec(memory_space=pl.ANY)


def _mix_cols(w):
    return jnp.concatenate([w[:, 2312:4360], w[:, 0:1536], w[:, 1544:2312], w[:, 1536:1544],
                            jnp.zeros((w.shape[0], DP - D_IN), w.dtype)], axis=1)


def _unmix_cols(w):
    return jnp.concatenate([w[:, QA:QA + 1536], w[:, FA:FA + 8], w[:, QB:QB + 768], w[:, GA:GA + 2048]], axis=1)


def _rows128(a, rows):
    flat = a.reshape(-1)
    return jnp.pad(flat, (0, rows * 128 - flat.shape[0])).reshape(rows, 128)


GRAD_FORM = {"ffn1_w_in": "col", "ffn2_w_in": "col", "w_branch_fox": "col", "w_branch_swa": "col",
             "ffn1_w_out": "3d", "ffn2_w_out": "3d", "w_out": "3d", "w_in": "3d"}
SUM_TILE = {1024: 128, 704: 176, 512: 128, 256: 128}
NT = len(SHARD_ITEMS)
ALL_ITEMS = tuple(range(NT))


def _half_rows(c, r):
    return pl.ds(pl.multiple_of(c * (r // 2), 16), r // 2)


def _ici_copies(kind, srcs, dsts, send_sems, recv_sems, layer, recv=True, items=ALL_ITEMS):
    x, y, c, chips = _place()
    s = 2 * x + y
    sends, recvs = [], []
    for t, (item, src, dst) in enumerate(zip(items, srcs, dsts)):
        nm, (r, cc), _ = SHARD_ITEMS[item]
        for j, (cx, cy) in enumerate(chips):
            sj = 2 * cx + cy
            k = 3 * t + j
            to = (cx, cy, c)
            if kind == "gather":
                hs = _half_rows(c, r)
                sends.append(_remote(src.at[layer, hs], dst.at[s, hs], send_sems, recv_sems, k, to))
                if recv:
                    recvs.append(_remote(src.at[layer, hs], dst.at[sj, hs], send_sems, recv_sems, k, to))
            else:
                if GRAD_FORM[nm] == "col":
                    piece = src.at[:, pl.ds(pl.multiple_of(sj * cc, 128), cc)]
                else:
                    piece = src.at[sj]
                sends.append(_remote(piece, dst.at[j], send_sems, recv_sems, k, to))
                recvs.append(sends[-1])
    return sends, recvs


def _slab_shapes(items=ALL_ITEMS):
    return [_sds((4, *SHARD_ITEMS[t][1]), CDT) for t in items]


def _dma_sems(n):
    return [pltpu.SemaphoreType.DMA((n,)), pltpu.SemaphoreType.DMA((n,))]


def _forward_sends(dsts, send_sems, recv_sems, items=ALL_ITEMS):
    x, y, c, chips = _place()
    sends, recvs = [], []
    for t, (item, dst) in enumerate(zip(items, dsts)):
        r = SHARD_ITEMS[item][1][0]
        for j, (cx, cy) in enumerate(chips):
            sj = 2 * cx + cy
            hs, ho = _half_rows(c, r), _half_rows(1 - c, r)
            sends.append(_remote(dst.at[sj, hs], dst.at[sj, hs], send_sems, recv_sems, 3 * t + j, (x, y, 1 - c)))
            recvs.append(_remote(dst.at[sj, ho], dst.at[sj, ho], send_sems, recv_sems, 3 * t + j, (x, y, 1 - c)))
    return sends, recvs


def _gather_layer(wb, mflat, layer, name, items):
    nt = len(items)

    def body(*refs):
        srcs, m_r, dsts, mall_o = refs[:nt], refs[nt], refs[nt + 1:2 * nt + 1], refs[2 * nt + 1]
        send_sems, recv_sems, fsend, frecv, msend, mrecv = refs[2 * nt + 2:]
        x, y, c, chips = _place()
        s = 2 * x + y
        sends, recvs = _ici_copies("gather", srcs, dsts, send_sems, recv_sems, layer, items=items)
        metas = [_remote(m_r, mall_o.at[s], msend, mrecv, j, (cx, cy, c)) for j, (cx, cy) in enumerate(chips)]
        for cp in sends + metas:
            cp.start()
        fwds, frecvs = _forward_sends(dsts, fsend, frecv, items)
        for got, fwd in zip(recvs, fwds):
            got.wait_recv()
            fwd.start()
        for got in frecvs:
            got.wait_recv()
        for j, (cx, cy) in enumerate(chips):
            _remote(m_r, mall_o.at[2 * cx + cy], msend, mrecv, j, (cx, cy, c)).wait_recv()
        for cp in sends + metas + fwds:
            cp.wait_send()

    return pl.pallas_call(
        body, name=name, out_shape=_slab_shapes(items) + [_sds((4, META_ROWS, 128), F32)],
        in_specs=[ANY] * (nt + 1), out_specs=[ANY] * (nt + 1),
        scratch_shapes=_dma_sems(3 * nt) + _dma_sems(3 * nt) + _dma_sems(3),
    )(*wb, mflat)


def _forward_layer(slabs, name, items=ALL_ITEMS):
    nt = len(items)

    def body(*refs):
        ins, outs, send_sems, recv_sems = refs[:nt], refs[nt:2 * nt], refs[2 * nt], refs[2 * nt + 1]
        sends, recvs = _forward_sends(outs, send_sems, recv_sems, items)
        for cp in sends:
            cp.start()
        for cp in recvs:
            cp.wait_recv()
        for cp in sends:
            cp.wait_send()

    return pl.pallas_call(
        body, name=name, out_shape=_slab_shapes(items), in_specs=[ANY] * nt, out_specs=[ANY] * nt,
        input_output_aliases={t: t for t in range(nt)}, scratch_shapes=_dma_sems(3 * nt),
    )(*slabs)


def _half_shape(nm, r, c):
    return (r // 2, 4 * c) if GRAD_FORM[nm] == "col" else (4, r // 2, c)


def _swap_layer(gs, gsm, name, items=ALL_ITEMS):
    small = gsm is not None
    nt = len(items)

    def body(*refs):
        g_rs = refs[:nt]
        pos = nt
        if small:
            s_r = refs[pos]
            pos += 1
        got_os = refs[pos:pos + nt]
        pos += nt
        if small:
            slots_o = refs[pos]
            pos += 1
        send_sems, recv_sems = refs[pos], refs[pos + 1]
        x, y, c, _ = _place()
        sib = (x, y, 1 - c)
        sent = []
        for t, (item, g_r, got_o) in enumerate(zip(items, g_rs, got_os)):
            nm, (r, cc), _ = SHARD_ITEMS[item]
            ho = _half_rows(1 - c, r)
            src = g_r.at[ho, :] if GRAD_FORM[nm] == "col" else g_r.at[:, ho, :]
            sent.append(_remote(src, got_o, send_sems, recv_sems, t, sib))
        if small:
            ssend, srecv, loc_sem = refs[pos + 2], refs[pos + 3], refs[pos + 4]
            me = 4 * x + 2 * y + c
            loc = pltpu.make_async_copy(s_r, slots_o.at[me], loc_sem.at[0])
            loc.start()
            peers = [(x ^ (k >> 2), y ^ ((k >> 1) & 1), c ^ (k & 1)) for k in range(1, 8)]
            for k, peer in enumerate(peers):
                sent.append(_remote(s_r, slots_o.at[me], ssend, srecv, k, peer))
        for cp in sent:
            cp.start()
        for cp in sent[:nt]:
            cp.wait_recv()
        if small:
            for k, (px, py, pc) in enumerate(peers):
                _remote(s_r, slots_o.at[4 * px + 2 * py + pc], ssend, srecv, k, (px, py, pc)).wait_recv()
        for cp in sent:
            cp.wait_send()
        if small:
            loc.wait()

    outs = [_sds(_half_shape(*SHARD_ITEMS[item][0:1], *SHARD_ITEMS[item][1]), CDT) for item in items]
    ops = list(gs)
    sems = _dma_sems(nt)
    if small:
        outs.append(_sds((8, SMALL_ROWS, 128), F32))
        ops.append(gsm)
        sems = sems + _dma_sems(7) + [pltpu.SemaphoreType.DMA((1,))]
    res = pl.pallas_call(
        body, name=name, out_shape=outs, in_specs=[ANY] * len(ops), out_specs=[ANY] * len(outs), scratch_shapes=sems,
    )(*ops)
    return (res[:nt], res[nt]) if small else (res, None)


def _pair_add_t(own, got, half_idx, nm, r, name):
    tr = SUM_TILE[r]
    nb = (r // 2) // tr
    if GRAD_FORM[nm] == "col":
        blk = (tr, own.shape[1])
        own_spec = pl.BlockSpec(blk, lambda i, c_r: (c_r[0] * nb + i, 0))
        half_spec = pl.BlockSpec(blk, lambda i, c_r: (i, 0))
    else:
        blk = (4, tr, own.shape[2])
        own_spec = pl.BlockSpec(blk, lambda i, c_r: (0, c_r[0] * nb + i, 0))
        half_spec = pl.BlockSpec(blk, lambda i, c_r: (0, i, 0))

    def body(c_r, a_r, b_r, o_o):
        o_o[...] = (a_r[...].astype(F32) + b_r[...].astype(F32)).astype(CDT)

    grid_spec = pltpu.PrefetchScalarGridSpec(num_scalar_prefetch=1, grid=(nb,), in_specs=[own_spec, half_spec],
                                             out_specs=half_spec)
    return pl.pallas_call(body, name=name, grid_spec=grid_spec, out_shape=_sds(got.shape, CDT),
                          compiler_params=_params(("parallel",)))(half_idx, own, got)


def _sum4_t(ps, got3, buf, idx, layer, nm, r, name):
    tr = SUM_TILE[r]
    nb = (r // 2) // tr
    c = got3.shape[2]
    if GRAD_FORM[nm] == "col":
        ps_spec = pl.BlockSpec((tr, c), lambda i, x_r: (i, x_r[0]))
    else:
        ps_spec = pl.BlockSpec((None, tr, c), lambda i, x_r: (x_r[0], i, 0))

    def body(x_r, a_r, b_r, buf_r, o_o):
        o_o[...] = ((a_r[...].astype(F32) + b_r[0].astype(F32)) + b_r[1].astype(F32)) + b_r[2].astype(F32)

    grid_spec = pltpu.PrefetchScalarGridSpec(
        num_scalar_prefetch=1, grid=(nb,),
        in_specs=[ps_spec, pl.BlockSpec((3, tr, c), lambda i, x_r: (0, i, 0)), ANY],
        out_specs=pl.BlockSpec((None, tr, c), lambda i, x_r: (layer, x_r[1] * nb + i, 0)),
    )
    return pl.pallas_call(body, name=name, grid_spec=grid_spec, out_shape=_sds(buf.shape, F32),
                          input_output_aliases={3: 0}, compiler_params=_params(("parallel",)))(idx, ps, got3, buf)


def _scatter_layer(ps, name, items=ALL_ITEMS):
    nt = len(items)

    def body(*refs):
        srcs, dsts, send_sems, recv_sems = refs[:nt], refs[nt:2 * nt], refs[2 * nt], refs[2 * nt + 1]
        sends, recvs = _ici_copies("scatter", srcs, dsts, send_sems, recv_sems, None, items=items)
        for cp in sends:
            cp.start()
        for cp in recvs:
            cp.wait_recv()
        for cp in sends:
            cp.wait_send()

    return pl.pallas_call(
        body, name=name, out_shape=_got3_shapes(items), in_specs=[ANY] * nt, out_specs=[ANY] * nt,
        scratch_shapes=_dma_sems(3 * nt),
    )(*ps)


def _got3_shapes(items=ALL_ITEMS):
    return [_sds((3, SHARD_ITEMS[t][1][0] // 2, SHARD_ITEMS[t][1][1]), CDT) for t in items]


def _join_layer(bufs, name):
    def body(*refs):
        ins, outs, send_sems, recv_sems = refs[:NT], refs[NT:2 * NT], refs[2 * NT], refs[2 * NT + 1]
        x, y, c, _ = _place()
        sent = []
        for t, ((nm, (r, cc), _), b_o) in enumerate(zip(SHARD_ITEMS, outs)):
            hs = _half_rows(c, r)
            sent.append(_remote(b_o.at[:, hs, :], b_o.at[:, hs, :], send_sems, recv_sems, t, (x, y, 1 - c)))
        for cp in sent:
            cp.start()
        for t, ((nm, (r, cc), _), b_o) in enumerate(zip(SHARD_ITEMS, outs)):
            ho = _half_rows(1 - c, r)
            _remote(b_o.at[:, ho, :], b_o.at[:, ho, :], send_sems, recv_sems, t, (x, y, 1 - c)).wait_recv()
        for cp in sent:
            cp.wait_send()

    return pl.pallas_call(
        body, name=name, out_shape=[_sds(b.shape, F32) for b in bufs], in_specs=[ANY] * NT, out_specs=[ANY] * NT,
        input_output_aliases={t: t for t in range(NT)}, scratch_shapes=_dma_sems(NT),
    )(*bufs)


def _adamw3(w, g, m, v, name):
    nl, r, c = w.shape
    tr = SUM_TILE.get(r, r)
    if r % 8:
        blk = pl.BlockSpec((None, r, 256), lambda l, i: (l, 0, i))
        steps = c // 256
    else:
        blk = pl.BlockSpec((None, tr, c), lambda l, i: (l, i, 0))
        steps = r // tr

    def body(w_r, g_r, m_r, v_r, d_o, m_o, v_o):
        g_ = g_r[...]
        m_ = ADAM_B1 * m_r[...] + (1.0 - ADAM_B1) * g_
        v_ = ADAM_B2 * v_r[...] + (1.0 - ADAM_B2) * jnp.square(g_)
        m_hat = m_ / (1.0 - ADAM_B1 ** ADAM_STEP)
        v_hat = v_ / (1.0 - ADAM_B2 ** ADAM_STEP)
        d_o[...] = -ADAM_LR * (m_hat / (jnp.sqrt(v_hat) + ADAM_EPS) + ADAM_WD * w_r[...])
        m_o[...] = m_
        v_o[...] = v_

    return pl.pallas_call(
        body, name=name, grid=(nl, steps),
        in_specs=[blk] * 4, out_specs=[blk] * 3, out_shape=[_sds((nl, r, c), F32)] * 3,
        compiler_params=_params(("parallel", "parallel")),
    )(w, g, m, v)


def _full_weights(slabs, wb, layer, shard, items=ALL_ITEMS):
    ws = {}
    for t, slab in zip(items, slabs):
        nm, (r, c), kind = SHARD_ITEMS[t]
        slab = lax.dynamic_update_slice(slab, wb[nm][layer][None], (shard, 0, 0))
        ws[nm] = slab.reshape(4 * r, c) if kind == "row" else jnp.concatenate([slab[s] for s in range(4)], axis=1)
    return ws


def _exchange_forms(g, items=ALL_ITEMS):
    out = []
    for t in items:
        nm, (r, c), _ = SHARD_ITEMS[t]
        a = g[nm]
        if nm == "w_in":
            a = a.reshape(D, 4, c).transpose(1, 0, 2)
        elif GRAD_FORM[nm] == "3d":
            a = a.reshape(4, r, c)
        out.append(a)
    return out


SMALL_ITEMS = (("rel_bias_table", 2), ("ffn1_norm", 16), ("mix_norm", 16), ("ffn2_norm", 16), ("forget_bias", 1),
               ("fox_q_norm", 1), ("fox_k_norm", 1), ("swa_q_norm", 1), ("swa_k_norm", 1), ("swa_sinks", 1))
SMALL_ADAM_ROWS = 96


def _layer_fwd(h, lw, l, ride=None, late=None):
    rides = late["rides"] if late else {}

    def run(key, fn, *args):
        r = rides.get(key)
        if r is None:
            return fn(*args)
        out = fn(*args, ride=r)
        late["arrived"](key, out[-1])
        return out[0] if len(out) == 2 else out[:-1]

    sv = {"h0": h}
    a, sv["a1t"] = _rms_fwd(h, lw["ffn1_norm"], f"rms_fwd_a{l}")
    sv["gu1"], s, sv["s1t"] = run("ffn_in_a", _ffn_in, a, lw["ffn1_w_in"], f"ffn_in_a{l}")
    h = run("ffn_out_a", _mm_res, s, lw["ffn1_w_out"], h, 0.5, f"ffn_out_a{l}")
    sv["h1"] = h
    a, sv["amt"] = _rms_fwd(h, lw["mix_norm"], f"rms_fwd_m{l}")
    if late:
        late["need"](lw, "mixer")
    proj = run("proj", _mm, a, lw["w_mix"], F32, _row_tile(h.shape[0]), DP, f"proj{l}")
    sv["proj"] = proj
    qf, kf, vf, qs, kse, vse, c, ct, sv["qft"] = _qknorm_fwd(proj, lw["gfq"], lw["gfk"], lw["gsq"], lw["gsk"], lw["fb"],
                                                              f"qknorm_fwd{l}")
    ofox, lse_f, *rode = _fox_fwd(qf, kf, vf, c, ct, f"fox_fwd{l}", ride)
    oswa, lse_s = run("swa_fwd", _swa_fwd, qs, kse, vse, lw["bias"], lw["sinks"], f"swa_fwd{l}")
    if late:
        late["need"](lw, "gate")
    sv.update(qf=qf, kf=kf, vf=vf, qs=qs, kse=kse, vse=vse, c=c, ct=ct, ofox=ofox, oswa=oswa, lse_f=lse_f, lse_s=lse_s)
    h, sv["yt"], sv["pf"], sv["ps"], sv["oft"], sv["ost"] = _gate_out_fwd(
        ofox, oswa, lw["w_branch_fox"], lw["w_branch_swa"], proj, lw["w_out"], h, f"gate_out_fwd{l}")
    sv["h2"] = h
    a, sv["a2t"] = _rms_fwd(h, lw["ffn2_norm"], f"rms_fwd_b{l}")
    sv["gu2"], s, sv["s2t"] = _ffn_in(a, lw["ffn2_w_in"], f"ffn_in_b{l}")
    h = _mm_res(s, lw["ffn2_w_out"], h, 0.5, f"ffn_out_b{l}")
    return h, sv, rode


def _ffn_bwd(dh, dhb, h_in, at, gu, st, norm, w_in, w_out, tag, rides=None):
    r = rides or (None,) * 4
    rode = []

    def split(res, ride):
        if ride is None:
            return res
        rode.extend(res[-1])
        return res[0] if len(res) == 2 else res[:-1]

    dgu = split(_ffn_bwd_mid(dhb, w_out, gu, f"ffn_bwd_mid_{tag}", r[0]), r[0])
    d_w_out = split(_mm(st, dhb, CDT, 256, D, f"dw_ffn_out_{tag}", scale=0.5, ride=r[1]), r[1])
    dh, dhb, dg = split(_ffn_bwd_in(dgu, w_in, h_in, norm, dh, f"ffn_bwd_in_{tag}", r[2]), r[2])
    d_w_in = split(_mm(at, dgu, CDT, D, 256, f"dw_ffn_in_{tag}", ride=r[3]), r[3])
    return dh, dhb, d_w_out, d_w_in, dg, rode


def _layer_bwd(dh, dhb, sv, lw, l, ride=None, before_ffn1=None):
    g = {}
    dh, dhb, g["ffn2_w_out"], g["ffn2_w_in"], g["ffn2_norm"], _ = _ffn_bwd(
        dh, dhb, sv["h2"], sv["a2t"], sv["gu2"], sv["s2t"], lw["ffn2_norm"], lw["ffn2_w_in"], lw["ffn2_w_out"], f"b{l}")
    g["w_out"] = _mm(sv["yt"], dhb, CDT, 512, 512, f"dw_out{l}")
    dpf, dps, dga, dgb = _gate_out_bwd(dhb, lw["w_out"], sv["pf"], sv["ps"], sv["proj"], f"gate_out_bwd{l}")
    do_f, do_ft = _mm_nt(dpf, lw["w_branch_fox"], f"d_ofox{l}", with_t=True)
    do_s = _mm_nt(dps, lw["w_branch_swa"], f"d_oswa{l}")
    g["w_branch_fox"] = _mm(sv["oft"], dpf, CDT, 512, 512, f"dw_bfox{l}")
    g["w_branch_swa"] = _mm(sv["ost"], dps, CDT, 512, 512, f"dw_bswa{l}")
    dqf, dcq, dkf, dvf, dck, *rode = _fox_bwd(sv["qf"], sv["qft"], sv["kf"], sv["vf"], sv["c"], sv["ct"], sv["ofox"],
                                              sv["lse_f"], do_f, do_ft, f"fox_bwd{l}", ride)
    g["rode"] = rode
    dqs, dkse, dvse, dbias, dsk = _swa_bwd(sv["qs"], sv["kse"], sv["vse"], lw["bias"], lw["sinks"], sv["oswa"],
                                           sv["lse_s"], do_s, f"swa_bwd{l}")
    dproj, dgn = _qknorm_bwd(sv["proj"], dqf, dkf, dvf, dqs, dkse, dvse, dcq, dck, dga, dgb,
                             lw["gfq"], lw["gfk"], lw["gsq"], lw["gsk"], lw["fb"], f"qknorm_bwd{l}")
    g["w_mix"] = _mm(sv["amt"], dproj, CDT, D, 640, f"dw_mix{l}")
    dh, dhb, g["mix_norm"] = _mm_nt_rms(dproj, lw["w_mix"], sv["h1"], lw["mix_norm"], dh, f"d_am{l}")
    g["dbias"], g["dsk"], g["dgn"] = dbias, dsk, dgn
    rides = before_ffn1(g) if before_ffn1 else None
    dh, dhb, g["ffn1_w_out"], g["ffn1_w_in"], g["ffn1_norm"], g["rode_ffn1"] = _ffn_bwd(
        dh, dhb, sv["h0"], sv["a1t"], sv["gu1"], sv["s1t"], lw["ffn1_norm"], lw["ffn1_w_in"], lw["ffn1_w_out"], f"a{l}",
        rides)
    return dh, dhb, g


def kernel(x, meta_tokens, rel_bias_table, ffn1_norm, ffn1_w_in, ffn1_w_out, mix_norm, w_in, forget_bias, fox_q_norm, fox_k_norm, swa_q_norm, swa_k_norm, swa_sinks, w_branch_fox, w_branch_swa, w_out, ffn2_norm, ffn2_w_in, ffn2_w_out, loss_target, m_meta_tokens, m_rel_bias_table, m_ffn1_norm, m_ffn1_w_in, m_ffn1_w_out, m_mix_norm, m_w_in, m_forget_bias, m_fox_q_norm, m_fox_k_norm, m_swa_q_norm, m_swa_k_norm, m_swa_sinks, m_w_branch_fox, m_w_branch_swa, m_w_out, m_ffn2_norm, m_ffn2_w_in, m_ffn2_w_out, v_meta_tokens, v_rel_bias_table, v_ffn1_norm, v_ffn1_w_in, v_ffn1_w_out, v_mix_norm, v_w_in, v_forget_bias, v_fox_q_norm, v_fox_k_norm, v_swa_q_norm, v_swa_k_norm, v_swa_sinks, v_w_branch_fox, v_w_branch_swa, v_w_out, v_ffn2_norm, v_ffn2_w_in, v_ffn2_w_out):
    names = ["meta_tokens", "rel_bias_table", "ffn1_norm", "ffn1_w_in", "ffn1_w_out", "mix_norm", "w_in", "forget_bias",
             "fox_q_norm", "fox_k_norm", "swa_q_norm", "swa_k_norm", "swa_sinks", "w_branch_fox", "w_branch_swa", "w_out",
             "ffn2_norm", "ffn2_w_in", "ffn2_w_out"]
    w = dict(zip(names, [meta_tokens, rel_bias_table, ffn1_norm, ffn1_w_in, ffn1_w_out, mix_norm, w_in, forget_bias,
                         fox_q_norm, fox_k_norm, swa_q_norm, swa_k_norm, swa_sinks, w_branch_fox, w_branch_swa, w_out,
                         ffn2_norm, ffn2_w_in, ffn2_w_out]))
    m = dict(zip(names, [m_meta_tokens, m_rel_bias_table, m_ffn1_norm, m_ffn1_w_in, m_ffn1_w_out, m_mix_norm, m_w_in,
                         m_forget_bias, m_fox_q_norm, m_fox_k_norm, m_swa_q_norm, m_swa_k_norm, m_swa_sinks,
                         m_w_branch_fox, m_w_branch_swa, m_w_out, m_ffn2_norm, m_ffn2_w_in, m_ffn2_w_out]))
    v = dict(zip(names, [v_meta_tokens, v_rel_bias_table, v_ffn1_norm, v_ffn1_w_in, v_ffn1_w_out, v_mix_norm, v_w_in,
                         v_forget_bias, v_fox_q_norm, v_fox_k_norm, v_swa_q_norm, v_swa_k_norm, v_swa_sinks,
                         v_w_branch_fox, v_w_branch_swa, v_w_out, v_ffn2_norm, v_ffn2_w_in, v_ffn2_w_out]))
    xi, yi, ci = lax.axis_index("x"), lax.axis_index("y"), lax.axis_index("c")
    shard = 2 * xi + yi
    seq = x.shape[1]
    t = seq + BLK

    wb = {nm: w[nm].astype(CDT) for nm, _, _ in SHARD_ITEMS}
    wb_list = [wb[nm] for nm, _, _ in SHARD_ITEMS]
    mflat = meta_tokens.reshape(META_ROWS, 128)
    first = (0, 1)
    *slabs_first, mall = _gather_layer([wb_list[t] for t in first], mflat, 0, "gather_weights", first)
    mall = lax.dynamic_update_slice(mall, mflat[None], (shard, 0, 0))
    meta_full = jnp.concatenate([mall[s].reshape(N_META, 256) for s in range(4)], axis=1)
    bias = _bias_fwd(rel_bias_table, "bias_fwd")

    def layer_weights(slabs, l, items=ALL_ITEMS):
        lw = _full_weights(slabs, wb, l, shard, items)
        if "w_in" in lw:
            lw["w_mix"] = _mix_cols(lw.pop("w_in"))
        return lw

    def layer_vectors(l):
        lw = {nm: w[nm][l].reshape(1, D) for nm in ("ffn1_norm", "mix_norm", "ffn2_norm")}
        lw["gfq"] = jnp.tile(fox_q_norm[l], 8).reshape(1, 512)
        lw["gfk"] = jnp.tile(fox_k_norm[l], 8).reshape(1, 512)
        lw["gsq"] = jnp.tile(swa_q_norm[l], 8).reshape(1, 512)
        lw["gsk"] = jnp.tile(swa_k_norm[l], 2).reshape(1, 128)
        lw["fb"] = jnp.pad(forget_bias[l], (0, 120)).reshape(1, 128)
        lw["sinks"] = swa_sinks[l]
        lw["bias"] = bias
        return lw

    def gather_ride(layer, items):
        return ("gather", [wb_list[t] for t in items], _slab_shapes(items), layer, items)

    landed = {}

    def need(lw, stage):
        if stage == "mixer":
            items = (2,)
            slabs = _forward_layer(landed["ffn_in_a"], "forward_halves0m", items)
        else:
            items = (3, 4, 5, 6, 7)
            slabs = _forward_layer(landed["ffn_out_a"] + landed["proj"] + landed["swa_fwd"], "forward_halves0g", items)
        lw.update(layer_weights(slabs, 0, items))

    late = {"rides": {"ffn_in_a": gather_ride(0, (2,)), "ffn_out_a": gather_ride(0, (3, 4, 5)),
                      "proj": gather_ride(0, (6,)), "swa_fwd": gather_ride(0, (7,))},
            "arrived": landed.__setitem__, "need": need}

    h = jnp.concatenate([jnp.zeros((PAD, D), F32), meta_full, x[0]], axis=0)
    lws = [{**layer_vectors(0), **layer_weights(slabs_first, 0, first)}]
    h, sv0, slabs1 = _layer_fwd(h, lws[0], 0, gather_ride(1, ALL_ITEMS), late)
    lws.append({**layer_vectors(1), **layer_weights(_forward_layer(slabs1, "forward_halves"), 1)})
    h, sv1, _ = _layer_fwd(h, lws[1], 1)
    saved = [sv0, sv1]
    dh, dhb, lacc = _loss(h, loss_target[0], "loss")
    loss = lax.psum(lacc[0, 0], ("x", "y", "c"))

    half_idx = ci.reshape(1).astype(jnp.int32)
    place_idx = jnp.stack([shard, ci]).astype(jnp.int32)

    def pair_sums(g, gsm, tag, items=ALL_ITEMS):
        if "w_mix" in g:
            g["w_in"] = _unmix_cols(g.pop("w_mix"))
        forms = _exchange_forms(g, items)
        got, slots = _swap_layer(forms, gsm, f"swap_halves{tag}", items)
        return {t: _pair_add_t(a, b, half_idx, SHARD_ITEMS[t][0], SHARD_ITEMS[t][1][0],
                               f"pair_add{tag}_{SHARD_ITEMS[t][0]}")
                for t, a, b in zip(items, forms, got)}, slots

    def scatter_ride(ps, items):
        return ("scatter", [ps[t] for t in items], _got3_shapes(items), None, items)

    early = (2, 3, 4, 5, 6, 7)
    early_rides = ((6,), (7,), (2, 5), (3, 4))
    ps0 = {}

    def before_ffn1(g):
        ps0.update(pair_sums(g, None, "0e", early)[0])
        return [scatter_ride(ps0, items) for items in early_rides]

    grads = [None, None]
    dh, dhb, grads[1] = _layer_bwd(dh, dhb, saved[1], lws[1], 1)
    ps1, _ = pair_sums(grads[1], None, 1)
    dh, dhb, grads[0] = _layer_bwd(dh, dhb, saved[0], lws[0], 0, scatter_ride(ps1, ALL_ITEMS), before_ffn1)
    grad_x = dh[BLK:].reshape(1, seq, D)
    dtab = _bias_bwd(grads[0]["dbias"] + grads[1]["dbias"], "bias_bwd")

    small = [dh[PAD:BLK].reshape(128, 128), _rows128(dtab[:, :N_BUCKETS].T, 2)]
    for nm in ("ffn1_norm", "mix_norm", "ffn2_norm"):
        small.append(jnp.stack([grads[l][nm][0] for l in range(2)]).reshape(16, 128))
    small.append(_rows128(jnp.stack([grads[l]["dgn"][4, :8] for l in range(2)]), 1))
    for row in range(4):
        small.append(jnp.stack([grads[l]["dgn"][row, :HD] for l in range(2)]).reshape(1, 128))
    dsk = [grads[l]["dsk"][:, 0, :] for l in range(2)]
    small.append(_rows128(jnp.stack([jnp.stack([d[:, 0], d[:, HD]], axis=1).reshape(8) for d in dsk]), 1))
    gsm = jnp.concatenate(small, axis=0)
    gsm = jnp.pad(gsm, ((0, SMALL_ROWS - gsm.shape[0]), (0, 0)))

    late = (0, 1)
    ps_late, slots = pair_sums(grads[0], gsm, "0l", late)
    ps0.update(ps_late)
    got3_0 = dict(zip([t for items in early_rides for t in items], grads[0]["rode_ffn1"]))
    got3_0.update(zip(late, _scatter_layer([ps0[t] for t in late], "scatter_shards", late)))
    got3 = [got3_0, dict(zip(ALL_ITEMS, grads[0]["rode"]))]
    bufs = []
    for t, (nm, (r, c), _) in enumerate(SHARD_ITEMS):
        buf = lax.empty((2, r, c), F32)
        for l, ps in ((1, ps1), (0, ps0)):
            buf = _sum4_t(ps[t], got3[l][t], buf, place_idx, l, nm, r, f"sum4_{l}_{nm}")
        bufs.append(buf)
    bufs = _join_layer(bufs, "join_halves")
    gs = _sum8(slots, "sum8")

    g_out = {nm: buf for (nm, _, _), buf in zip(SHARD_ITEMS, bufs)}
    g_out["meta_tokens"] = lax.dynamic_slice(gs[0:128].reshape(N_META, D), (0, shard * 256), (N_META, 256))
    off = 128
    for nm, rows in SMALL_ITEMS:
        n = w[nm].size
        g_out[nm] = gs[off:off + rows].reshape(-1)[:n].reshape(w[nm].shape)
        off += rows

    delta, new_m, new_v = {}, {}, {}
    for nm, _, _ in SHARD_ITEMS:
        if nm == "w_in":
            tr_ = lambda a: jnp.swapaxes(a, 1, 2)
            delta[nm], new_m[nm], new_v[nm] = (tr_(a) for a in _adamw3(tr_(w[nm]), tr_(g_out[nm]), tr_(m[nm]), tr_(v[nm]),
                                                                        f"adamw_{nm}"))
        else:
            delta[nm], new_m[nm], new_v[nm] = _adamw3(w[nm], g_out[nm], m[nm], v[nm], f"adamw_{nm}")
    small_names = ["meta_tokens"] + [nm for nm, _ in SMALL_ITEMS]
    small_rows = [META_ROWS] + [rows for _, rows in SMALL_ITEMS]

    def pack_small(src):
        buf = jnp.concatenate([_rows128(src[nm], rows) for nm, rows in zip(small_names, small_rows)], axis=0)
        return jnp.pad(buf, ((0, SMALL_ADAM_ROWS - buf.shape[0]), (0, 0)))

    d_, m_, v_ = (a[0] for a in _adamw3(pack_small(w)[None], pack_small(g_out)[None], pack_small(m)[None],
                                        pack_small(v)[None], "adamw_small"))
    off = 0
    for nm, rows in zip(small_names, small_rows):
        n = w[nm].size
        for dst, src in ((delta, d_), (new_m, m_), (new_v, v_)):
            dst[nm] = src[off:off + rows].reshape(-1)[:n].reshape(w[nm].shape)
        off += rows

    return (loss, grad_x, *[g_out[n] for n in names], *[delta[n] for n in names],
            *[new_m[n] for n in names], *[new_v[n] for n in names])
```

```python
import math

import numpy as np
import jax
import jax.numpy as jnp
from jax import lax
from jax.experimental import pallas as pl
from jax.experimental.pallas import tpu as pltpu

D = 1024
F = 2816
FT = F // 2
HD = 64
NPAIR = 4
N_META = 16
BLK = 128
PAD = BLK - N_META
EPS = 1e-6
NEG = -1e30
N_BUCKETS = 32
GA, GB, QA, KA, VA, QB, KB, VB, FA, DP = 0, 1024, 2048, 2560, 3072, 3584, 4096, 4224, 4352, 4480
D_IN = 4360
CDT = jnp.bfloat16
F32 = jnp.float32
VMEM_LIMIT = 48 * 1024 * 1024
MESH_ID = pl.DeviceIdType.MESH

ADAM_LR, ADAM_B1, ADAM_B2, ADAM_EPS, ADAM_WD, ADAM_STEP = 0.001, 0.9, 0.999, 1e-08, 0.01, 10

SHARD_ITEMS = (
    ("ffn1_w_in", (1024, 1408), "col"),
    ("ffn1_w_out", (704, 1024), "row"),
    ("w_in", (1024, 1090), "col"),
    ("w_branch_fox", (512, 256), "col"),
    ("w_branch_swa", (512, 256), "col"),
    ("w_out", (256, 1024), "row"),
    ("ffn2_w_in", (1024, 1408), "col"),
    ("ffn2_w_out", (704, 1024), "row"),
)
SMALL_ROWS = 192
META_ROWS = 32


def _row_tile(t):
    return 384 if t % 384 == 0 else 128


def _dot(a, b):
    return jnp.dot(a, b, preferred_element_type=F32)


def _dot_nt(a, b):
    return lax.dot_general(a, b, (((1,), (1,)), ((), ())), preferred_element_type=F32)


def _dot_hi(a, b):
    return jnp.dot(a, b, preferred_element_type=F32, precision=lax.Precision.HIGHEST)


def _sigmoid(x):
    return 0.5 * jnp.tanh(0.5 * x) + 0.5


def _iota(shape, dim):
    return lax.broadcasted_iota(jnp.int32, shape, dim)


def _params(sem=None):
    return pltpu.CompilerParams(dimension_semantics=sem, vmem_limit_bytes=VMEM_LIMIT)


def _sds(shape, dtype):
    return jax.ShapeDtypeStruct(shape, dtype)


def _rms_fwd(h, g, name):
    t = h.shape[0]
    tm = _row_tile(t)

    def body(h_ref, g_ref, a_ref, at_ref):
        x = h_ref[...]
        ms = jnp.mean(x * x, axis=-1, keepdims=True)
        a = x * lax.rsqrt(ms + EPS) * g_ref[...]
        a_ref[...] = a.astype(CDT)
        at_ref[...] = a.T.astype(CDT)

    return pl.pallas_call(
        body, name=name, grid=(t // tm,),
        in_specs=[pl.BlockSpec((tm, D), lambda i: (i, 0)), pl.BlockSpec((1, D), lambda i: (0, 0))],
        out_specs=[pl.BlockSpec((tm, D), lambda i: (i, 0)), pl.BlockSpec((D, tm), lambda i: (0, i))],
        out_shape=[_sds((t, D), CDT), _sds((D, t), CDT)],
        compiler_params=_params(("parallel",)),
    )(h, g)


def _ffn_in(a, w_in, name, ride=None):
    t = a.shape[0]
    tm = _row_tile(t)
    tn = FT
    nj = F // tn
    grid = (nj, t // tm)
    ride_in, ride_in_specs, ride_out, ride_out_specs, ride_sems = _ride_specs(ride)

    def body(a_ref, wg_ref, wu_ref, gu_ref, s_ref, st_ref):
        a_ = a_ref[...]
        g = _dot(a_, wg_ref[...])
        u = _dot(a_, wu_ref[...])
        s = g * _sigmoid(g) * u
        gu_ref[0] = g.astype(CDT)
        gu_ref[1] = u.astype(CDT)
        s_ref[...] = s.astype(CDT)
        st_ref[...] = s.T.astype(CDT)

    res = pl.pallas_call(
        _riding(body, 3, 3, ride, grid), name=name, grid=grid,
        in_specs=[pl.BlockSpec((tm, D), lambda j, i: (i, 0)),
                  pl.BlockSpec((D, tn), lambda j, i: (0, j)),
                  pl.BlockSpec((D, tn), lambda j, i: (0, j + nj))] + ride_in_specs,
        out_specs=[pl.BlockSpec((2, tm, tn), lambda j, i: (0, i, j)),
                   pl.BlockSpec((tm, tn), lambda j, i: (i, j)),
                   pl.BlockSpec((tn, tm), lambda j, i: (j, i))] + ride_out_specs,
        out_shape=[_sds((2, t, F), CDT), _sds((t, F), CDT), _sds((F, t), CDT)] + ride_out, scratch_shapes=ride_sems,
        compiler_params=_params(("arbitrary", "arbitrary") if ride else ("parallel", "parallel")),
    )(a, w_in, w_in, *ride_in)
    return (*res[:3], res[3:]) if ride else res


def _mm_res(a, b, res, scale, name, ride=None):
    t, k = a.shape
    n = b.shape[1]
    tm = _row_tile(t)
    tn = n
    grid = (t // tm, n // tn)
    ride_in, ride_in_specs, ride_out, ride_out_specs, ride_sems = _ride_specs(ride)

    def body(a_ref, b_ref, r_ref, o_ref):
        o_ref[...] = r_ref[...] + scale * _dot(a_ref[...], b_ref[...])

    out = pl.pallas_call(
        _riding(body, 3, 1, ride, grid), name=name, grid=grid,
        in_specs=[pl.BlockSpec((tm, k), lambda i, j: (i, 0)),
                  pl.BlockSpec((k, tn), lambda i, j: (0, j)),
                  pl.BlockSpec((tm, tn), lambda i, j: (i, j))] + ride_in_specs,
        out_specs=[pl.BlockSpec((tm, tn), lambda i, j: (i, j))] + ride_out_specs,
        out_shape=[_sds((t, n), F32)] + ride_out, scratch_shapes=ride_sems,
        compiler_params=_params(("arbitrary", "arbitrary") if ride else ("parallel", "parallel")),
    )(a, b, res, *ride_in)
    return (out[0], out[1:]) if ride else out[0]


def _mm(a, b, out_dtype, tm, tn, name, scale=1.0, ride=None):
    m, k = a.shape
    if b.ndim == 3:
        nh = b.shape[2] // tn
        n = 2 * b.shape[2]
        b_spec = pl.BlockSpec((None, k, tn), lambda i, j: (j // nh, 0, j % nh))
    else:
        n = b.shape[1]
        b_spec = pl.BlockSpec((k, tn), lambda i, j: (0, j))
    grid = (m // tm, n // tn)
    ride_in, ride_in_specs, ride_out, ride_out_specs, ride_sems = _ride_specs(ride)

    def body(a_ref, b_ref, o_ref):
        o_ref[...] = (scale * _dot(a_ref[...], b_ref[...])).astype(out_dtype)

    res = pl.pallas_call(
        _riding(body, 2, 1, ride, grid), name=name, grid=grid,
        in_specs=[pl.BlockSpec((tm, k), lambda i, j: (i, 0)), b_spec] + ride_in_specs,
        out_specs=[pl.BlockSpec((tm, tn), lambda i, j: (i, j))] + ride_out_specs,
        out_shape=[_sds((m, n), out_dtype)] + ride_out, scratch_shapes=ride_sems,
        compiler_params=_params(("arbitrary", "arbitrary") if ride else ("parallel", "parallel")),
    )(a, b, *ride_in)
    return (res[0], res[1:]) if ride else res[0]


def _mm_nt(a, b, name, with_t=False):
    m, n = a.shape
    k = b.shape[0]
    tm = _row_tile(m)
    tk = k

    def body(a_ref, b_ref, o_ref, *t_ref):
        r = _dot_nt(a_ref[...], b_ref[...])
        o_ref[...] = r
        if with_t:
            t_ref[0][...] = r.T.astype(CDT)

    out_specs = [pl.BlockSpec((tm, tk), lambda i, j: (i, j))]
    out_shape = [_sds((m, k), F32)]
    if with_t:
        out_specs.append(pl.BlockSpec((tk, tm), lambda i, j: (j, i)))
        out_shape.append(_sds((k, m), CDT))
    res = pl.pallas_call(
        body, name=name, grid=(m // tm, k // tk),
        in_specs=[pl.BlockSpec((tm, n), lambda i, j: (i, 0)), pl.BlockSpec((tk, n), lambda i, j: (j, 0))],
        out_specs=out_specs, out_shape=out_shape,
        compiler_params=_params(("parallel", "parallel")),
    )(a, b)
    return res if with_t else res[0]


def _ffn_bwd_mid(dhb, w_out, gu, name, ride=None):
    t = dhb.shape[0]
    tm = _row_tile(t)
    tn = FT
    grid = (F // tn, t // tm)
    ride_in, ride_in_specs, ride_out, ride_out_specs, ride_sems = _ride_specs(ride)

    def body(dh_ref, w_ref, gu_ref, o_ref):
        ds = _dot_nt(dh_ref[...] * 0.5, w_ref[...])
        g = gu_ref[0].astype(F32)
        u = gu_ref[1].astype(F32)
        sg = _sigmoid(g)
        o_ref[0] = (ds * u * (sg * (1.0 + g * (1.0 - sg)))).astype(CDT)
        o_ref[1] = (ds * (g * sg)).astype(CDT)

    res = pl.pallas_call(
        _riding(body, 3, 1, ride, grid), name=name, grid=grid,
        in_specs=[pl.BlockSpec((tm, D), lambda j, i: (i, 0)),
                  pl.BlockSpec((tn, D), lambda j, i: (j, 0)),
                  pl.BlockSpec((2, tm, tn), lambda j, i: (0, i, j))] + ride_in_specs,
        out_specs=[pl.BlockSpec((2, tm, tn), lambda j, i: (0, i, j))] + ride_out_specs,
        out_shape=[_sds((2, t, F), CDT)] + ride_out, scratch_shapes=ride_sems,
        compiler_params=_params(("arbitrary", "arbitrary") if ride else ("parallel", "parallel")),
    )(dhb, w_out, gu, *ride_in)
    return (res[0], res[1:]) if ride else res[0]


def _rms_bwd_rows(da_, x, g, dres, i, dh_ref, dhb_ref, dg_ref):
    r = lax.rsqrt(jnp.mean(x * x, axis=-1, keepdims=True) + EPS)
    xh = x * r
    day = da_ * g
    dh = dres + r * (day - xh * jnp.mean(day * xh, axis=-1, keepdims=True))
    dh_ref[...] = dh
    dhb_ref[...] = dh.astype(CDT)

    @pl.when(i == 0)
    def _():
        dg_ref[...] = jnp.zeros(dg_ref.shape, F32)

    dg_ref[0:1, :] += jnp.sum(da_ * xh, axis=0, keepdims=True)


def _ffn_bwd_in(dgu, w_in, h, g, dres, name, ride=None):
    t = dgu.shape[1]
    tm = _row_tile(t)
    grid = (t // tm,)
    ride_in, ride_in_specs, ride_out, ride_out_specs, ride_sems = _ride_specs(ride)

    def body(dg_ref, wg_ref, wu_ref, h_ref, g_ref, dr_ref, dh_ref, dhb_ref, dgn_ref):
        da_ = _dot_nt(dg_ref[0], wg_ref[...]) + _dot_nt(dg_ref[1], wu_ref[...])
        _rms_bwd_rows(da_, h_ref[...], g_ref[...], dr_ref[...], pl.program_id(0), dh_ref, dhb_ref, dgn_ref)

    row = pl.BlockSpec((tm, D), lambda i: (i, 0))
    res = pl.pallas_call(
        _riding(body, 6, 3, ride, grid), name=name, grid=grid,
        in_specs=[pl.BlockSpec((2, tm, F), lambda i: (0, i, 0)),
                  pl.BlockSpec((D, F), lambda i: (0, 0)),
                  pl.BlockSpec((D, F), lambda i: (0, 1)),
                  row, pl.BlockSpec((1, D), lambda i: (0, 0)), row] + ride_in_specs,
        out_specs=[row, row, pl.BlockSpec((8, D), lambda i: (0, 0))] + ride_out_specs,
        out_shape=[_sds((t, D), F32), _sds((t, D), CDT), _sds((8, D), F32)] + ride_out, scratch_shapes=ride_sems,
        compiler_params=_params(("arbitrary",)),
    )(dgu, w_in, w_in, h, g, dres, *ride_in)
    return (*res[:3], res[3:]) if ride else res


def _mm_nt_rms(a, b, h, g, dres, name):
    t, n = a.shape
    tm = _row_tile(t)

    def body(a_ref, b_ref, h_ref, g_ref, dr_ref, dh_ref, dhb_ref, dgn_ref):
        da_ = _dot_nt(a_ref[...], b_ref[...])
        _rms_bwd_rows(da_, h_ref[...], g_ref[...], dr_ref[...], pl.program_id(0), dh_ref, dhb_ref, dgn_ref)

    row = pl.BlockSpec((tm, D), lambda i: (i, 0))
    return pl.pallas_call(
        body, name=name, grid=(t // tm,),
        in_specs=[pl.BlockSpec((tm, n), lambda i: (i, 0)), pl.BlockSpec((D, n), lambda i: (0, 0)),
                  row, pl.BlockSpec((1, D), lambda i: (0, 0)), row],
        out_specs=[row, row, pl.BlockSpec((8, D), lambda i: (0, 0))],
        out_shape=[_sds((t, D), F32), _sds((t, D), CDT), _sds((8, D), F32)],
        compiler_params=_params(("arbitrary",)),
    )(a, b, h, g, dres)


def _loss(h, target, name):
    t = h.shape[0]
    ta = _row_tile(t)
    nb = ta // BLK

    def body(h_ref, *refs):
        t_refs, (dh_ref, dhb_ref, l_ref) = refs[:nb], refs[nb:]
        i = pl.program_id(0)

        @pl.when(i == 0)
        def _():
            l_ref[...] = jnp.zeros(l_ref.shape, F32)

        tot = 0.0
        for b in range(nb):
            rows = slice(BLK * b, BLK * (b + 1))
            err = jnp.where(i * nb + b > 0, h_ref[rows, :] - t_refs[b][...], 0.0)
            tot = tot + jnp.sum(err * err)
            d = err * (1.0 / D)
            dh_ref[rows, :] = d
            dhb_ref[rows, :] = d.astype(CDT)
        l_ref[...] += (0.5 / D) * tot

    row = pl.BlockSpec((ta, D), lambda i: (i, 0))
    tspecs = [pl.BlockSpec((BLK, D), lambda i, b=b: (jnp.maximum(i * nb + b - 1, 0), 0)) for b in range(nb)]
    return pl.pallas_call(
        body, name=name, grid=(t // ta,),
        in_specs=[row] + tspecs,
        out_specs=[row, row, pl.BlockSpec((8, 128), lambda i: (0, 0))],
        out_shape=[_sds((t, D), F32), _sds((t, D), CDT), _sds((8, 128), F32)],
        compiler_params=_params(("arbitrary",)),
    )(h, *([target] * nb))


def _block_diag():
    return (_iota((128, 128), 0) // HD == _iota((128, 128), 1) // HD).astype(F32)


def _head_sums(v, bd):
    hi = v.astype(CDT)
    rest = (v - hi.astype(F32)).astype(CDT)
    b = bd.astype(CDT)
    return _dot(hi, b) + _dot(rest, b)


def _dup_halves(x, lo):
    sw = pltpu.roll(x, 64, 1)
    return jnp.where(lo, x, sw), jnp.where(lo, sw, x)


def _qknorm_fwd(proj, gfq, gfk, gsq, gsk, fb, name):
    t = proj.shape[0]
    tm = _row_tile(t)

    def body(qa, ka, va, qb, kb, vb, fa, gfq_r, gfk_r, gsq_r, gsk_r, fb_r,
             qf_o, kf_o, vf_o, qs_o, kse_o, vse_o, c_o, ct_o, qft_o, carry):
        i = pl.program_id(0)
        bd = _block_diag()
        lane = _iota((1, 128), 1)
        lo = lane < HD

        def hnorm(x, g):
            ms = _head_sums(x * x, bd) * (1.0 / HD)
            return x * lax.rsqrt(ms + EPS) * g

        for ch in range(4):
            sl = slice(128 * ch, 128 * (ch + 1))
            qn = hnorm(qa[:, sl], gfq_r[:, sl]) * 0.125
            qf_o[:, sl] = qn.astype(CDT)
            qft_o[sl, :] = qn.T.astype(CDT)
            kf_o[:, sl] = hnorm(ka[:, sl], gfk_r[:, sl]).astype(CDT)
            qs_o[:, sl] = (hnorm(qb[:, sl], gsq_r[:, sl]) * 0.125).astype(CDT)
        vf_o[...] = va[...].astype(CDT)
        k0, k1 = _dup_halves(hnorm(kb[...], gsk_r[...]), lo)
        kse_o[0] = k0.astype(CDT)
        kse_o[1] = k1.astype(CDT)
        v0, v1 = _dup_halves(vb[...], lo)
        vse_o[0] = v0.astype(CDT)
        vse_o[1] = v1.astype(CDT)

        z = fa[...] + fb_r[...]
        lf = jnp.minimum(z, 0.0) - jnp.log(1.0 + jnp.exp(-jnp.abs(z)))
        lf = jnp.where(lane < 8, lf, 0.0)
        ltri = (_iota((tm, tm), 1) <= _iota((tm, tm), 0)).astype(F32)

        @pl.when(i == 0)
        def _():
            carry[...] = jnp.zeros(carry.shape, F32)

        c = _dot_hi(ltri, lf) + carry[0:1, :]
        carry[0:1, :] = c[tm - 1:tm, :]
        c_o[...] = c
        ct_o[...] = c.T[0:8, :]

    def col(width, off):
        return pl.BlockSpec((tm, width), lambda i: (i, off // width))

    def vec(width):
        return pl.BlockSpec((1, width), lambda i: (0, 0))

    return pl.pallas_call(
        body, name=name, grid=(t // tm,),
        in_specs=[col(512, QA), col(512, KA), col(512, VA), col(512, QB), col(128, KB), col(128, VB), col(128, FA),
                  vec(512), vec(512), vec(512), vec(128), vec(128)],
        out_specs=[pl.BlockSpec((tm, 512), lambda i: (i, 0))] * 4
        + [pl.BlockSpec((2, tm, 128), lambda i: (0, i, 0))] * 2
        + [pl.BlockSpec((tm, 128), lambda i: (i, 0)), pl.BlockSpec((8, tm), lambda i: (0, i)),
           pl.BlockSpec((512, tm), lambda i: (0, i))],
        out_shape=[_sds((t, 512), CDT)] * 4 + [_sds((2, t, 128), CDT)] * 2
        + [_sds((t, 128), F32), _sds((8, t), F32), _sds((512, t), CDT)],
        scratch_shapes=[pltpu.VMEM((8, 128), F32)],
        compiler_params=_params(("arbitrary",)),
    )(proj, proj, proj, proj, proj, proj, proj, gfq, gfk, gsq, gsk, fb)


def _qknorm_bwd(proj, dqf, dkf, dvf, dqs, dkse, dvse, dcq, dck, dga, dgb, gfq, gfk, gsq, gsk, fb, name):
    t = proj.shape[0]
    tm = _row_tile(t)
    nt = t // tm

    def body(qa, ka, qb, kb, fa, dqf_r, dkf_r, dvf_r, dqs_r, dkse_r, dvse_r, dcq_r, dck_r, dga_r, dgb_r,
             gfq_r, gfk_r, gsq_r, gsk_r, fb_r, dp_o, dgn_o, carry, acc):
        i = pl.program_id(0)
        bd = _block_diag()
        lane = _iota((1, 128), 1)
        lo = lane < HD

        @pl.when(i == 0)
        def _():
            carry[...] = jnp.zeros(carry.shape, F32)
            acc[...] = jnp.zeros(acc.shape, F32)

        def hnorm_bwd(x, g, dy):
            r = lax.rsqrt(_head_sums(x * x, bd) * (1.0 / HD) + EPS)
            xh = x * r
            day = dy * g
            dx = r * (day - xh * (_head_sums(day * xh, bd) * (1.0 / HD)))
            return dx, jnp.sum(dy * xh, axis=0, keepdims=True)

        for ch in range(4):
            sl = slice(128 * ch, 128 * (ch + 1))
            dx, dg = hnorm_bwd(qa[:, sl], gfq_r[:, sl], dqf_r[:, sl] * 0.125)
            dp_o[:, QA + 128 * ch:QA + 128 * (ch + 1)] = dx.astype(CDT)
            acc[0:1, sl] += dg
            dx, dg = hnorm_bwd(ka[:, sl], gfk_r[:, sl], dkf_r[:, sl])
            dp_o[:, KA + 128 * ch:KA + 128 * (ch + 1)] = dx.astype(CDT)
            acc[1:2, sl] += dg
            dx, dg = hnorm_bwd(qb[:, sl], gsq_r[:, sl], dqs_r[:, sl] * 0.125)
            dp_o[:, QB + 128 * ch:QB + 128 * (ch + 1)] = dx.astype(CDT)
            acc[2:3, sl] += dg
        dp_o[:, VA:VA + 512] = dvf_r[...].astype(CDT)
        dp_o[:, GA:GA + D] = dga_r[...]
        dp_o[:, GB:GB + D] = dgb_r[...]

        def fold(x):
            e0 = x[0]
            e1 = x[1]
            return jnp.where(lo, e0 + pltpu.roll(e0, 64, 1), e1 + pltpu.roll(e1, 64, 1))

        dx, dg = hnorm_bwd(kb[...], gsk_r[...], fold(dkse_r))
        dp_o[:, KB:KB + 128] = dx.astype(CDT)
        acc[3:4, 0:128] += dg
        dp_o[:, VB:VB + 128] = fold(dvse_r).astype(CDT)

        rr = _iota((512, 128), 0)
        hh = _iota((512, 128), 1)
        sel = ((rr == (hh >> 1) * 128 + (hh & 1) * HD) & (hh < 8)).astype(F32)
        dcs = _dot_hi(dcq_r[...] - dck_r[...], sel)
        utri = (_iota((tm, tm), 1) >= _iota((tm, tm), 0)).astype(F32)
        dlf = _dot_hi(utri, dcs) + carry[0:1, :]
        carry[0:1, :] = dlf[0:1, :]
        z = fa[...] + fb_r[...]
        dfa = jnp.where(lane < 8, dlf * _sigmoid(-z), 0.0)
        dp_o[:, FA:FA + 128] = dfa.astype(CDT)
        acc[4:5, 0:128] += jnp.sum(dfa, axis=0, keepdims=True)

        @pl.when(i == nt - 1)
        def _():
            foldm = ((_iota((512, 128), 0) & (HD - 1)) == _iota((512, 128), 1)).astype(F32)
            dgn_o[...] = _dot_hi(acc[...], foldm)

    def col(width, off):
        return pl.BlockSpec((tm, width), lambda i: (nt - 1 - i, off // width))

    def rows(width):
        return pl.BlockSpec((tm, width), lambda i: (nt - 1 - i, 0))

    def vec(width):
        return pl.BlockSpec((1, width), lambda i: (0, 0))

    pair = pl.BlockSpec((2, tm, 128), lambda i: (0, nt - 1 - i, 0))
    return pl.pallas_call(
        body, name=name, grid=(nt,),
        in_specs=[col(512, QA), col(512, KA), col(512, QB), col(128, KB), col(128, FA),
                  rows(512), rows(512), rows(512), rows(512), pair, pair, rows(512), rows(512), rows(D), rows(D),
                  vec(512), vec(512), vec(512), vec(128), vec(128)],
        out_specs=[rows(DP), pl.BlockSpec((8, 128), lambda i: (0, 0))],
        out_shape=[_sds((t, DP), CDT), _sds((8, 128), F32)],
        scratch_shapes=[pltpu.VMEM((8, 128), F32), pltpu.VMEM((8, 512), F32)],
        compiler_params=_params(("arbitrary",)),
    )(proj, proj, proj, proj, proj, dqf, dkf, dvf, dqs, dkse, dvse, dcq, dck, dga, dgb, gfq, gfk, gsq, gsk, fb)


def _gate_out_fwd(ofox, oswa, wbf, wbs, proj, w_out, h, name):
    t = ofox.shape[0]
    tm = _row_tile(t)

    def body(of_r, os_r, wf_r, ws_r, ga_r, gb_r, wo_r, h_r, ho_o, yt_o, pf_o, ps_o, oft_o, ost_o):
        pf = _dot(of_r[...], wf_r[...])
        ps = _dot(os_r[...], ws_r[...])
        y = _sigmoid(ga_r[...]) * pf + _sigmoid(gb_r[...]) * ps
        ho_o[...] = h_r[...] + _dot(y.astype(CDT), wo_r[...])
        yt_o[...] = y.T.astype(CDT)
        pf_o[...] = pf.astype(CDT)
        ps_o[...] = ps.astype(CDT)
        oft_o[...] = of_r[...].astype(F32).T.astype(CDT)
        ost_o[...] = os_r[...].astype(F32).T.astype(CDT)

    row = pl.BlockSpec((tm, D), lambda i: (i, 0))
    half = pl.BlockSpec((tm, 512), lambda i: (i, 0))
    whole = lambda r: pl.BlockSpec((r, D), lambda i: (0, 0))
    tcol = lambda r: pl.BlockSpec((r, tm), lambda i: (0, i))
    return pl.pallas_call(
        body, name=name, grid=(t // tm,),
        in_specs=[half, half, whole(512), whole(512),
                  pl.BlockSpec((tm, D), lambda i: (i, GA // D)), pl.BlockSpec((tm, D), lambda i: (i, GB // D)),
                  whole(D), row],
        out_specs=[row, tcol(D), row, row, tcol(512), tcol(512)],
        out_shape=[_sds((t, D), F32), _sds((D, t), CDT), _sds((t, D), CDT), _sds((t, D), CDT),
                   _sds((512, t), CDT), _sds((512, t), CDT)],
        compiler_params=_params(("parallel",)),
    )(ofox, oswa, wbf, wbs, proj, proj, w_out, h)


def _gate_out_bwd(dhb, w_out, pf, ps, proj, name):
    t = dhb.shape[0]
    tm = _row_tile(t)

    def body(dh_r, wo_r, pf_r, ps_r, ga_r, gb_r, dpf_o, dps_o, dga_o, dgb_o):
        dy_ = _dot_nt(dh_r[...], wo_r[...])
        sa = _sigmoid(ga_r[...])
        sb = _sigmoid(gb_r[...])
        dpf_o[...] = (dy_ * sa).astype(CDT)
        dps_o[...] = (dy_ * sb).astype(CDT)
        dga_o[...] = (dy_ * pf_r[...].astype(F32) * (sa * (1.0 - sa))).astype(CDT)
        dgb_o[...] = (dy_ * ps_r[...].astype(F32) * (sb * (1.0 - sb))).astype(CDT)

    row = pl.BlockSpec((tm, D), lambda i: (i, 0))
    return pl.pallas_call(
        body, name=name, grid=(t // tm,),
        in_specs=[row, pl.BlockSpec((D, D), lambda i: (0, 0)), row, row,
                  pl.BlockSpec((tm, D), lambda i: (i, GA // D)), pl.BlockSpec((tm, D), lambda i: (i, GB // D))],
        out_specs=[row] * 4,
        out_shape=[_sds((t, D), CDT)] * 4,
        compiler_params=_params(("parallel",)),
    )(dhb, w_out, pf, ps, proj, proj)


def _tri_steps(n, by_key):
    if by_key:
        pairs = [(i, j) for j in range(n) for i in range(j, n)]
    else:
        pairs = [(i, j) for i in range(n) for j in range(i + 1)]
    return (np.array([p[0] for p in pairs], np.int32), np.array([p[1] for p in pairs], np.int32))


def _head_col(blk, lane, h):
    return jnp.sum(jnp.where(lane == h, blk, 0.0), axis=1, keepdims=True)


def _head_row(blk, sub, h):
    return jnp.sum(jnp.where(sub == h, blk, 0.0), axis=0, keepdims=True)


def _ride_specs(ride):
    if ride is None:
        return [], [], [], [], []
    kind, srcs, outs, layer, items = ride
    return list(srcs), [ANY] * len(srcs), list(outs), [ANY] * len(outs), _dma_sems(3 * len(srcs))


def _ride_start(ride, srcs, dsts, send_sems, recv_sems):
    for cp in _ici_copies(ride[0], srcs, dsts, send_sems, recv_sems, ride[3], recv=False, items=ride[4])[0]:
        cp.start()


def _ride_wait(ride, srcs, dsts, send_sems, recv_sems):
    sends, recvs = _ici_copies(ride[0], srcs, dsts, send_sems, recv_sems, ride[3], items=ride[4])
    for cp in recvs:
        cp.wait_recv()
    for cp in sends:
        cp.wait_send()


def _riding(body, n_in, n_out, ride, grid):
    if ride is None:
        return body
    nr = len(ride[1])

    def wrapped(*refs):
        ins, srcs = refs[:n_in], refs[n_in:n_in + nr]
        outs, dsts = refs[n_in + nr:n_in + nr + n_out], refs[n_in + nr + n_out:n_in + 2 * nr + n_out]
        scratch, sems = refs[n_in + 2 * nr + n_out:-2], refs[-2:]
        first = pl.program_id(0) == 0
        last = pl.program_id(0) == grid[0] - 1
        for a in range(1, len(grid)):
            first = first & (pl.program_id(a) == 0)
            last = last & (pl.program_id(a) == grid[a] - 1)

        @pl.when(first)
        def _():
            _ride_start(ride, srcs, dsts, *sems)

        body(*ins, *outs, *scratch)

        @pl.when(last)
        def _():
            _ride_wait(ride, srcs, dsts, *sems)

    return wrapped


def _fox_fwd(qf, kf, vf, c, ct, name, ride=None):
    t = qf.shape[0]
    ta = _row_tile(t)
    qi, kj = _tri_steps(t // ta, by_key=False)
    nsteps = len(qi)
    ride_in, ride_in_specs, ride_out, ride_out_specs, ride_sems = _ride_specs(ride)

    def body(qi_r, kj_r, q_r, k_r, v_r, c_r, ct_r, *rest):
        nr = len(ride_in)
        src_r, (o_o, lse_o), dst_o = rest[:nr], rest[nr:nr + 2], rest[nr + 2:2 * nr + 2]
        m_sc, l_sc, acc_sc, cq_sc, *sems = rest[2 * nr + 2:]
        p = pl.program_id(0)
        n = pl.program_id(1)
        i = qi_r[n]
        j = kj_r[n]
        lane = _iota((1, 128), 1)
        lo = lane < HD

        if ride is not None:
            @pl.when((p == 0) & (n == 0))
            def _():
                _ride_start(ride, src_r, dst_o, *sems)

        @pl.when(j == 0)
        def _():
            m_sc[...] = jnp.full(m_sc.shape, NEG, F32)
            l_sc[...] = jnp.zeros(l_sc.shape, F32)
            acc_sc[...] = jnp.zeros(acc_sc.shape, F32)
            for e in (0, 1):
                cq_sc[e] = jnp.broadcast_to(_head_col(c_r[...], lane, 2 * p + e), (ta, 128))

        def step(masked):
            q = q_r[...]
            k = k_r[...]
            vaug = jnp.concatenate([v_r[...], jnp.ones((ta, 128), CDT)], axis=1)
            if masked:
                rows = i * ta + _iota((ta, 1), 0)
                cols = j * ta + _iota((1, ta), 1)
                mask = (cols <= rows) & (cols >= PAD)
            sub = _iota((8, 1), 0)
            heads = (0, 1)
            sels = [lo, jnp.logical_not(lo)]
            s = [_dot_nt(jnp.where(sels[e], q, 0), k) for e in heads]
            ck = [_head_row(ct_r[...], sub, 2 * p + e) for e in heads]
            chunks = []
            for e in heads:
                cq = cq_sc[e]
                row = []
                for ch in range(ta // 128):
                    sl = slice(128 * ch, 128 * (ch + 1))
                    sc = s[e][:, sl] + cq - ck[e][:, sl]
                    if masked:
                        sc = jnp.where(mask[:, sl], sc, NEG)
                    row.append(sc)
                chunks.append(row)
            m_new, alphas = [], []
            for e in heads:
                mx = chunks[e][0]
                for sc in chunks[e][1:]:
                    mx = jnp.maximum(mx, sc)
                m_prev = m_sc[e]
                m_new.append(jnp.maximum(m_prev, jnp.max(mx, axis=1, keepdims=True)))
                alphas.append(jnp.exp(m_prev - m_new[e]))
            pe = [jnp.concatenate([jnp.exp(sc - m_new[e]).astype(CDT) for sc in chunks[e]], axis=1) for e in heads]
            pva = [_dot(pe[e], vaug) for e in heads]
            for e in heads:
                l_sc[e] = alphas[e] * l_sc[e] + pva[e][:, 128:]
                m_sc[e] = m_new[e]
            acc_sc[...] = (acc_sc[...] * jnp.where(lo, alphas[0], alphas[1])
                           + jnp.where(lo, pva[0][:, :128], pva[1][:, :128]))

        edge = (j == i) | (j == 0)

        @pl.when(edge)
        def _():
            step(True)

        @pl.when(jnp.logical_not(edge))
        def _():
            step(False)

        @pl.when(j == i)
        def _():
            l = jnp.where(lo, l_sc[0], l_sc[1])
            o_o[...] = (acc_sc[...] / l).astype(CDT)
            lse_o[...] = jnp.where(lo, m_sc[0], m_sc[1]) + jnp.log(l)

        if ride is not None:
            @pl.when((p == NPAIR - 1) & (n == nsteps - 1))
            def _():
                _ride_wait(ride, src_r, dst_o, *sems)

    qblk = pl.BlockSpec((ta, 128), lambda p, n, qi_r, kj_r: (qi_r[n], p))
    kblk = pl.BlockSpec((ta, 128), lambda p, n, qi_r, kj_r: (kj_r[n], p))
    grid_spec = pltpu.PrefetchScalarGridSpec(
        num_scalar_prefetch=2, grid=(NPAIR, nsteps),
        in_specs=[qblk, kblk, kblk,
                  pl.BlockSpec((ta, 128), lambda p, n, qi_r, kj_r: (qi_r[n], 0)),
                  pl.BlockSpec((8, ta), lambda p, n, qi_r, kj_r: (0, kj_r[n]))] + ride_in_specs,
        out_specs=[qblk, qblk] + ride_out_specs,
        scratch_shapes=[pltpu.VMEM((2, ta, 128), F32), pltpu.VMEM((2, ta, 128), F32), pltpu.VMEM((ta, 128), F32),
                        pltpu.VMEM((2, ta, 128), F32)] + ride_sems,
    )
    return pl.pallas_call(
        body, name=name, grid_spec=grid_spec,
        out_shape=[_sds((t, 512), CDT), _sds((t, 512), F32)] + ride_out,
        compiler_params=_params(("arbitrary", "arbitrary")),
    )(jnp.asarray(qi), jnp.asarray(kj), qf, kf, vf, c, ct, *ride_in)


def _fox_bwd(qf, qft, kf, vf, c, ct, o, lse, do, dot, name, ride=None):
    t = qf.shape[0]
    ta = _row_tile(t)
    nq = t // ta
    qi, kj = _tri_steps(nq, by_key=False)
    nsteps = len(qi)
    ride_in, ride_in_specs, ride_out, ride_out_specs, ride_sems = _ride_specs(ride)

    def body(qi_r, kj_r, q_r, qt_r, k_r, v_r, c_r, ct_r, o_r, lse_r, do_r, dot_r, *rest):
        nr = len(ride_in)
        src_r, (dq_o, dcq_o, dk_o, dv_o, dck_o), dst_o = rest[:nr], rest[nr:nr + 5], rest[nr + 5:2 * nr + 5]
        lse_sc, dl_sc, cq_sc, dq_sc, dcq_sc, dkt_sc, dvt_sc, dckt_sc, *sems = rest[2 * nr + 5:]
        p = pl.program_id(0)
        n = pl.program_id(1)
        i = qi_r[n]
        j = kj_r[n]
        lane = _iota((1, 128), 1)
        lo = lane < HD
        top = _iota((128, 1), 0) < HD

        if ride is not None:
            @pl.when((p == 0) & (n == 0))
            def _():
                _ride_start(ride, src_r, dst_o, *sems)

        @pl.when(n == 0)
        def _():
            dkt_sc[...] = jnp.zeros(dkt_sc.shape, F32)
            dvt_sc[...] = jnp.zeros(dvt_sc.shape, F32)
            dckt_sc[...] = jnp.zeros(dckt_sc.shape, F32)

        @pl.when(j == 0)
        def _():
            dq_sc[...] = jnp.zeros(dq_sc.shape, F32)
            dcq_sc[...] = jnp.zeros(dcq_sc.shape, F32)
            dd = do_r[...] * o_r[...].astype(F32)
            lse = lse_r[...]
            for e in (0, 1):
                sel = lo if e == 0 else jnp.logical_not(lo)
                cq_sc[e] = jnp.broadcast_to(_head_col(c_r[...], lane, 2 * p + e), (ta, 128))
                dl_sc[e] = jnp.broadcast_to(jnp.sum(jnp.where(sel, dd, 0.0), axis=1, keepdims=True), (ta, 128))
                lse_sc[e] = jnp.broadcast_to(lse[:, HD * e:HD * e + 1], (ta, 128))

        def step(masked):
            q = q_r[...]
            qt = qt_r[...]
            k = k_r[...]
            v = v_r[...]
            dob = do_r[...].astype(CDT)
            dot_ = dot_r[...]
            ones = jnp.ones((ta, 128), CDT)
            ones16 = jnp.ones((16, ta), CDT)
            if masked:
                rows = i * ta + _iota((ta, 1), 0)
                cols = j * ta + _iota((1, ta), 1)
                mask = (cols <= rows) & (cols >= PAD)
            sub = _iota((8, 1), 0)
            heads = (0, 1)
            sels = [lo, jnp.logical_not(lo)]
            rsels = [top, jnp.logical_not(top)]
            s = [_dot_nt(jnp.where(sels[e], q, 0), k) for e in heads]
            dp = [_dot_nt(jnp.where(sels[e], dob, 0), v) for e in heads]
            ck = [_head_row(ct_r[...], sub, 2 * p + e) for e in heads]
            pb, dsb = [], []
            for e in heads:
                cq, lse_e, dl = cq_sc[e], lse_sc[e], dl_sc[e]
                prs, dss = [], []
                for ch in range(ta // 128):
                    sl = slice(128 * ch, 128 * (ch + 1))
                    sc = s[e][:, sl] + cq - ck[e][:, sl]
                    if masked:
                        sc = jnp.where(mask[:, sl], sc, NEG)
                    pr = jnp.exp(sc - lse_e)
                    prs.append(pr.astype(CDT))
                    dss.append((pr * (dp[e][:, sl] - dl)).astype(CDT))
                pb.append(jnp.concatenate(prs, axis=1))
                dsb.append(jnp.concatenate(dss, axis=1))
            dvt = [_dot(jnp.where(rsels[e], dot_, 0), pb[e]) for e in heads]
            dkc = [_dot(jnp.concatenate([jnp.where(rsels[e], qt, 0), ones16], axis=0), dsb[e]) for e in heads]
            dqa = [_dot(dsb[e], jnp.concatenate([jnp.where(sels[e], k, 0), ones], axis=1)) for e in heads]
            dvt_sc[j] += dvt[0] + dvt[1]
            dkt_sc[j] += dkc[0][0:128] + dkc[1][0:128]
            dckt_sc[j, 0:8, :] += jnp.where(sub == 0, dkc[0][128:136], jnp.where(sub == 1, dkc[1][128:136], 0.0))
            dq_sc[...] += dqa[0][:, :128] + dqa[1][:, :128]
            for e in heads:
                dcq_sc[e] += dqa[e][:, 128:]

        edge = (j == i) | (j == 0)

        @pl.when(edge)
        def _():
            step(True)

        @pl.when(jnp.logical_not(edge))
        def _():
            step(False)

        @pl.when(j == i)
        def _():
            dq_o[...] = dq_sc[...]
            dcq_o[...] = jnp.where(lo, dcq_sc[0], dcq_sc[1])

        @pl.when(n == nsteps - 1)
        def _():
            spread = (_iota((128, 128), 1) == _iota((128, 128), 0) // HD).astype(F32)
            for jb in range(nq):
                rs = slice(jb * ta, (jb + 1) * ta)
                dk_o[rs, :] = dkt_sc[jb].T
                dv_o[rs, :] = dvt_sc[jb].T
                dck_o[rs, :] = _dot_hi(spread, dckt_sc[jb]).T

        if ride is not None:
            @pl.when((p == NPAIR - 1) & (n == nsteps - 1))
            def _():
                _ride_wait(ride, src_r, dst_o, *sems)

    qblk = pl.BlockSpec((ta, 128), lambda p, n, qi_r, kj_r: (qi_r[n], p))
    qtblk = pl.BlockSpec((128, ta), lambda p, n, qi_r, kj_r: (p, qi_r[n]))
    kblk = pl.BlockSpec((ta, 128), lambda p, n, qi_r, kj_r: (kj_r[n], p))
    whole = pl.BlockSpec((t, 128), lambda p, n, qi_r, kj_r: (0, p))
    grid_spec = pltpu.PrefetchScalarGridSpec(
        num_scalar_prefetch=2, grid=(NPAIR, nsteps),
        in_specs=[qblk, qtblk, kblk, kblk,
                  pl.BlockSpec((ta, 128), lambda p, n, qi_r, kj_r: (qi_r[n], 0)),
                  pl.BlockSpec((8, ta), lambda p, n, qi_r, kj_r: (0, kj_r[n])),
                  qblk, qblk, qblk, qtblk] + ride_in_specs,
        out_specs=[qblk, qblk, whole, whole, whole] + ride_out_specs,
        scratch_shapes=[pltpu.VMEM((2, ta, 128), F32)] * 3 + [pltpu.VMEM((ta, 128), F32), pltpu.VMEM((2, ta, 128), F32)]
        + [pltpu.VMEM((nq, 128, ta), F32)] * 3 + ride_sems,
    )
    return pl.pallas_call(
        body, name=name, grid_spec=grid_spec,
        out_shape=[_sds((t, 512), F32)] * 5 + ride_out,
        compiler_params=_params(("arbitrary", "arbitrary")),
    )(jnp.asarray(qi), jnp.asarray(kj), qf, qft, kf, vf, c, ct, o, lse, do, dot, *ride_in)


def _bucket_table():
    r = np.arange(BLK)[:, None]
    c = np.arange(3 * BLK)[None, :]
    d = np.where(c < BLK, r + BLK - c, r - (c - BLK))
    n = np.maximum(d, 0)
    max_exact = N_BUCKETS // 2
    nf = np.maximum(n, 1).astype(np.float32)
    large = max_exact + (np.log(nf / max_exact) / math.log(BLK / max_exact) * (N_BUCKETS - max_exact)).astype(np.int32)
    large = np.minimum(large, N_BUCKETS - 1)
    b = np.where(n < max_exact, n, large)
    return np.where(c < 2 * BLK, b, N_BUCKETS - 1).astype(np.int32)


def _bias_fwd(table, name):
    bucket = jnp.asarray(_bucket_table())

    def body(tab_r, b_r, o_o):
        h = pl.program_id(0)
        b = b_r[...]
        acc = jnp.zeros(b.shape, F32)
        for k in range(N_BUCKETS):
            acc = jnp.where(b == k, tab_r[k, h], acc)
        o_o[...] = acc

    return pl.pallas_call(
        body, name=name, grid=(8,),
        in_specs=[pl.BlockSpec(memory_space=pltpu.SMEM), pl.BlockSpec((BLK, 3 * BLK), lambda h: (0, 0))],
        out_specs=pl.BlockSpec((None, BLK, 3 * BLK), lambda h: (h, 0, 0)),
        out_shape=_sds((8, BLK, 3 * BLK), F32),
        compiler_params=_params(("parallel",)),
    )(table, bucket)


def _bias_bwd(dbias, name):
    bucket = jnp.asarray(_bucket_table())

    def body(d_r, b_r, o_o):
        h = pl.program_id(0)
        b = b_r[...]
        d = d_r[...]
        lane = _iota((1, 128), 1)
        row = jnp.zeros((1, 128), F32)
        for k in range(N_BUCKETS):
            row = jnp.where(lane == k, jnp.sum(jnp.where(b == k, d, 0.0)), row)
        o_o[pl.ds(h, 1), :] = row

    return pl.pallas_call(
        body, name=name, grid=(8,),
        in_specs=[pl.BlockSpec((None, BLK, 3 * BLK), lambda h: (h, 0, 0)), pl.BlockSpec((BLK, 3 * BLK), lambda h: (0, 0))],
        out_specs=pl.BlockSpec((8, 128), lambda h: (0, 0)),
        out_shape=_sds((8, 128), F32),
        compiler_params=_params(("arbitrary",)),
    )(dbias, bucket)


def _swa_valid(i):
    r = _iota((BLK, 1), 0)
    c = _iota((1, 3 * BLK), 1)
    prev = (c < BLK) & (c > r) & (i >= 1) & ((i - 1) * BLK + c >= PAD)
    cc = c - BLK
    cur = (c >= BLK) & (c < 2 * BLK) & (cc <= r) & (i * BLK + cc >= PAD)
    cm = c - 2 * BLK
    meta = (c >= 2 * BLK) & (cm >= PAD) & (i * BLK + r - cm >= BLK)
    return prev | cur | meta


def _swa_kv_specs(ta):
    nb = ta // BLK
    return [pl.BlockSpec((None, BLK, 128), lambda p, i: (p // 2, jnp.maximum(i * nb - 1, 0), 0)),
            pl.BlockSpec((None, ta, 128), lambda p, i: (p // 2, i, 0)),
            pl.BlockSpec((None, BLK, 128), lambda p, i: (p // 2, 0, 0))]


def _swa_fwd(qs, kse, vse, bias, sinks, name, ride=None):
    t = qs.shape[0]
    ta = _row_tile(t)
    nb = ta // BLK
    grid = (NPAIR, t // ta)
    ride_in, ride_in_specs, ride_out, ride_out_specs, ride_sems = _ride_specs(ride)

    def body(sink_r, q_r, kp_r, kc_r, km_r, vp_r, vc_r, vm_r, b_r, o_o, lse_o):
        p = pl.program_id(0)
        i = pl.program_id(1)
        lo = _iota((1, 128), 1) < HD
        k4 = jnp.concatenate([kp_r[...], kc_r[...]], axis=0)
        v4 = jnp.concatenate([vp_r[...], vc_r[...]], axis=0)
        work = [(b, e) for b in range(nb) for e in (0, 1)]
        sinks = [sink_r[2 * p + e] for e in (0, 1)]
        v3 = [jnp.concatenate([v4[BLK * b:BLK * (b + 2)], vm_r[...]], axis=0) for b in range(nb)]
        s = {}
        for b in range(nb):
            q = q_r[BLK * b:BLK * (b + 1), :]
            k3 = jnp.concatenate([k4[BLK * b:BLK * (b + 2)], km_r[...]], axis=0)
            valid = _swa_valid(i * nb + b)
            for e in (0, 1):
                sel = lo if e == 0 else jnp.logical_not(lo)
                s[b, e] = jnp.where(valid, _dot_nt(jnp.where(sel, q, 0), k3) + b_r[e], NEG)
        mx = {w: jnp.maximum(jnp.max(s[w], axis=1, keepdims=True), sinks[w[1]]) for w in work}
        pe = {w: jnp.exp(s[w] - mx[w]) for w in work}
        den = {w: jnp.sum(pe[w], axis=1, keepdims=True) + jnp.exp(sinks[w[1]] - mx[w]) for w in work}
        out = {w: _dot(pe[w].astype(CDT), v3[w[0]]) / den[w] for w in work}
        for b in range(nb):
            rows = slice(BLK * b, BLK * (b + 1))
            o_o[rows, :] = jnp.where(lo, out[b, 0], out[b, 1]).astype(CDT)
            lse_o[rows, :] = jnp.where(lo, mx[b, 0] + jnp.log(den[b, 0]), mx[b, 1] + jnp.log(den[b, 1]))

    qblk = pl.BlockSpec((ta, 128), lambda p, i: (i, p))
    res = pl.pallas_call(
        _riding(body, 9, 2, ride, grid), name=name, grid=grid,
        in_specs=[pl.BlockSpec(memory_space=pltpu.SMEM), qblk] + _swa_kv_specs(ta) + _swa_kv_specs(ta)
        + [pl.BlockSpec((2, BLK, 3 * BLK), lambda p, i: (p, 0, 0))] + ride_in_specs,
        out_specs=[qblk, qblk] + ride_out_specs,
        out_shape=[_sds((t, 512), CDT), _sds((t, 512), F32)] + ride_out, scratch_shapes=ride_sems,
        compiler_params=_params(("arbitrary", "arbitrary") if ride else ("parallel", "parallel")),
    )(sinks, qs, kse, kse, kse, vse, vse, vse, bias, *ride_in)
    return (res[0], res[1], res[2:]) if ride else res


def _swa_bwd(qs, kse, vse, bias, sinks, o, lse, do, name):
    t = qs.shape[0]
    ta = _row_tile(t)
    nb = ta // BLK

    def body(sink_r, q_r, kp_r, kc_r, km_r, vp_r, vc_r, vm_r, b_r, o_r, lse_r, do_r,
             dq_o, dk_o, dv_o, db_o, dsk_o):
        p = pl.program_id(0)
        i = pl.program_id(1)
        lo = _iota((1, 128), 1) < HD

        @pl.when((i == 0) & (p % 2 == 0))
        def _():
            dk_o[...] = jnp.zeros(dk_o.shape, F32)
            dv_o[...] = jnp.zeros(dv_o.shape, F32)

        @pl.when(i == 0)
        def _():
            db_o[...] = jnp.zeros(db_o.shape, F32)
            dsk_o[...] = jnp.zeros(dsk_o.shape, F32)

        k4 = jnp.concatenate([kp_r[...], kc_r[...]], axis=0)
        v4 = jnp.concatenate([vp_r[...], vc_r[...]], axis=0)
        work = [(b, e) for b in range(nb) for e in (0, 1)]
        sel = [lo, jnp.logical_not(lo)]
        k3 = [jnp.concatenate([k4[BLK * b:BLK * (b + 2)], km_r[...]], axis=0) for b in range(nb)]
        v3 = [jnp.concatenate([v4[BLK * b:BLK * (b + 2)], vm_r[...]], axis=0) for b in range(nb)]
        q = [q_r[BLK * b:BLK * (b + 1), :] for b in range(nb)]
        do_ = [do_r[BLK * b:BLK * (b + 1), :] for b in range(nb)]
        lse = [lse_r[BLK * b:BLK * (b + 1), :] for b in range(nb)]
        dd = [do_[b] * o_r[BLK * b:BLK * (b + 1), :].astype(F32) for b in range(nb)]
        valid = [_swa_valid(i * nb + b) for b in range(nb)]
        qe = {(b, e): jnp.where(sel[e], q[b], 0) for b, e in work}
        doe = {(b, e): jnp.where(sel[e], do_[b], 0.0).astype(CDT) for b, e in work}
        lse_e = {(b, e): lse[b][:, HD * e:HD * e + 1] for b, e in work}
        delta = {(b, e): jnp.sum(jnp.where(sel[e], dd[b], 0.0), axis=1, keepdims=True) for b, e in work}
        s = {(b, e): jnp.where(valid[b], _dot_nt(qe[b, e], k3[b]) + b_r[e], NEG) for b, e in work}
        dp = {(b, e): _dot_nt(doe[b, e], v3[b]) for b, e in work}
        pr = {w: jnp.exp(s[w] - lse_e[w]) for w in work}
        ds = {w: pr[w] * (dp[w] - delta[w]) for w in work}
        dqs = {(b, e): _dot(ds[b, e].astype(CDT), jnp.where(sel[e], k3[b], 0)) for b, e in work}
        both = lambda x, b: jnp.concatenate([x[b, 0], x[b, 1]], axis=0)
        dk3 = [_dot(both(ds, b).T.astype(CDT), both(qe, b)) for b in range(nb)]
        dv3 = [_dot(both(pr, b).T.astype(CDT), both(doe, b)) for b in range(nb)]
        for e in (0, 1):
            tot = ds[0, e]
            for b in range(1, nb):
                tot = tot + ds[b, e]
            db_o[e] += tot
        dsink = [sum(-jnp.sum(jnp.exp(sink_r[2 * p + e] - lse_e[b, e]) * delta[b, e], axis=0, keepdims=True)
                     for b in range(nb)) for e in (0, 1)]
        dsk_o[0:1, :] += jnp.where(lo, dsink[0], dsink[1])
        for b in range(nb):
            ib = i * nb + b
            dq_o[BLK * b:BLK * (b + 1), :] = dqs[b, 0] + dqs[b, 1]
            dk = dk3[b]
            dv = dv3[b]
            prev = pl.ds(pl.multiple_of(jnp.maximum(ib - 1, 0) * BLK, BLK), BLK)
            cur = pl.ds(pl.multiple_of(ib * BLK, BLK), BLK)
            dk_o[prev, :] += dk[0:BLK]
            dk_o[cur, :] += dk[BLK:2 * BLK]
            dk_o[0:BLK, :] += dk[2 * BLK:]
            dv_o[prev, :] += dv[0:BLK]
            dv_o[cur, :] += dv[BLK:2 * BLK]
            dv_o[0:BLK, :] += dv[2 * BLK:]

    qblk = pl.BlockSpec((ta, 128), lambda p, i: (i, p))
    kvacc = pl.BlockSpec((None, t, 128), lambda p, i: (p // 2, 0, 0))
    bblk = pl.BlockSpec((2, BLK, 3 * BLK), lambda p, i: (p, 0, 0))
    return pl.pallas_call(
        body, name=name, grid=(NPAIR, t // ta),
        in_specs=[pl.BlockSpec(memory_space=pltpu.SMEM), qblk] + _swa_kv_specs(ta) + _swa_kv_specs(ta)
        + [bblk, qblk, qblk, qblk],
        out_specs=[qblk, kvacc, kvacc, bblk, pl.BlockSpec((None, 8, 128), lambda p, i: (p, 0, 0))],
        out_shape=[_sds((t, 512), F32), _sds((2, t, 128), F32), _sds((2, t, 128), F32),
                   _sds((8, BLK, 3 * BLK), F32), _sds((NPAIR, 8, 128), F32)],
        compiler_params=_params(("arbitrary", "arbitrary")),
    )(sinks, qs, kse, kse, kse, vse, vse, vse, bias, o, lse, do)


def _sum8(slots, name):
    def body(a_r, o_o):
        acc = a_r[0]
        for k in range(1, 8):
            acc = acc + a_r[k]
        o_o[...] = acc

    return pl.pallas_call(
        body, name=name, out_shape=_sds((SMALL_ROWS, 128), F32),
        in_specs=[pl.BlockSpec(memory_space=pltpu.VMEM)], out_specs=pl.BlockSpec(memory_space=pltpu.VMEM),
        compiler_params=_params(),
    )(slots)


def _place():
    x, y, c = lax.axis_index("x"), lax.axis_index("y"), lax.axis_index("c")
    chips = [(1 - x, y), (x, 1 - y), (1 - x, 1 - y)]
    return x, y, c, chips


def _remote(src, dst, send_sems, recv_sems, k, to):
    return pltpu.make_async_remote_copy(src_ref=src, dst_ref=dst, send_sem=send_sems.at[k], recv_sem=recv_sems.at[k],
                                        device_id=to, device_id_type=MESH_ID)


ANY = pl.BlockSpec(memory_space=pl.ANY)


def _mix_cols(w):
    return jnp.concatenate([w[:, 2312:4360], w[:, 0:1536], w[:, 1544:2312], w[:, 1536:1544],
                            jnp.zeros((w.shape[0], DP - D_IN), w.dtype)], axis=1)


def _unmix_cols(w):
    return jnp.concatenate([w[:, QA:QA + 1536], w[:, FA:FA + 8], w[:, QB:QB + 768], w[:, GA:GA + 2048]], axis=1)


def _rows128(a, rows):
    flat = a.reshape(-1)
    return jnp.pad(flat, (0, rows * 128 - flat.shape[0])).reshape(rows, 128)


GRAD_FORM = {"ffn1_w_in": "col", "ffn2_w_in": "col", "w_branch_fox": "col", "w_branch_swa": "col",
             "ffn1_w_out": "3d", "ffn2_w_out": "3d", "w_out": "3d", "w_in": "3d"}
SUM_TILE = {1024: 128, 704: 176, 512: 128, 256: 128}
NT = len(SHARD_ITEMS)
ALL_ITEMS = tuple(range(NT))


def _half_rows(c, r):
    return pl.ds(pl.multiple_of(c * (r // 2), 16), r // 2)


def _ici_copies(kind, srcs, dsts, send_sems, recv_sems, layer, recv=True, items=ALL_ITEMS):
    x, y, c, chips = _place()
    s = 2 * x + y
    sends, recvs = [], []
    for t, (item, src, dst) in enumerate(zip(items, srcs, dsts)):
        nm, (r, cc), _ = SHARD_ITEMS[item]
        for j, (cx, cy) in enumerate(chips):
            sj = 2 * cx + cy
            k = 3 * t + j
            to = (cx, cy, c)
            if kind == "gather":
                hs = _half_rows(c, r)
                sends.append(_remote(src.at[layer, hs], dst.at[s, hs], send_sems, recv_sems, k, to))
                if recv:
                    recvs.append(_remote(src.at[layer, hs], dst.at[sj, hs], send_sems, recv_sems, k, to))
            else:
                if GRAD_FORM[nm] == "col":
                    piece = src.at[:, pl.ds(pl.multiple_of(sj * cc, 128), cc)]
                else:
                    piece = src.at[sj]
                sends.append(_remote(piece, dst.at[j], send_sems, recv_sems, k, to))
                recvs.append(sends[-1])
    return sends, recvs


def _slab_shapes(items=ALL_ITEMS):
    return [_sds((4, *SHARD_ITEMS[t][1]), CDT) for t in items]


def _dma_sems(n):
    return [pltpu.SemaphoreType.DMA((n,)), pltpu.SemaphoreType.DMA((n,))]


def _forward_sends(dsts, send_sems, recv_sems, items=ALL_ITEMS):
    x, y, c, chips = _place()
    sends, recvs = [], []
    for t, (item, dst) in enumerate(zip(items, dsts)):
        r = SHARD_ITEMS[item][1][0]
        for j, (cx, cy) in enumerate(chips):
            sj = 2 * cx + cy
            hs, ho = _half_rows(c, r), _half_rows(1 - c, r)
            sends.append(_remote(dst.at[sj, hs], dst.at[sj, hs], send_sems, recv_sems, 3 * t + j, (x, y, 1 - c)))
            recvs.append(_remote(dst.at[sj, ho], dst.at[sj, ho], send_sems, recv_sems, 3 * t + j, (x, y, 1 - c)))
    return sends, recvs


def _gather_layer(wb, mflat, layer, name, items):
    nt = len(items)

    def body(*refs):
        srcs, m_r, dsts, mall_o = refs[:nt], refs[nt], refs[nt + 1:2 * nt + 1], refs[2 * nt + 1]
        send_sems, recv_sems, fsend, frecv, msend, mrecv = refs[2 * nt + 2:]
        x, y, c, chips = _place()
        s = 2 * x + y
        sends, recvs = _ici_copies("gather", srcs, dsts, send_sems, recv_sems, layer, items=items)
        metas = [_remote(m_r, mall_o.at[s], msend, mrecv, j, (cx, cy, c)) for j, (cx, cy) in enumerate(chips)]
        for cp in sends + metas:
            cp.start()
        fwds, frecvs = _forward_sends(dsts, fsend, frecv, items)
        for got, fwd in zip(recvs, fwds):
            got.wait_recv()
            fwd.start()
        for got in frecvs:
            got.wait_recv()
        for j, (cx, cy) in enumerate(chips):
            _remote(m_r, mall_o.at[2 * cx + cy], msend, mrecv, j, (cx, cy, c)).wait_recv()
        for cp in sends + metas + fwds:
            cp.wait_send()

    return pl.pallas_call(
        body, name=name, out_shape=_slab_shapes(items) + [_sds((4, META_ROWS, 128), F32)],
        in_specs=[ANY] * (nt + 1), out_specs=[ANY] * (nt + 1),
        scratch_shapes=_dma_sems(3 * nt) + _dma_sems(3 * nt) + _dma_sems(3),
    )(*wb, mflat)


def _forward_layer(slabs, name, items=ALL_ITEMS):
    nt = len(items)

    def body(*refs):
        ins, outs, send_sems, recv_sems = refs[:nt], refs[nt:2 * nt], refs[2 * nt], refs[2 * nt + 1]
        sends, recvs = _forward_sends(outs, send_sems, recv_sems, items)
        for cp in sends:
            cp.start()
        for cp in recvs:
            cp.wait_recv()
        for cp in sends:
            cp.wait_send()

    return pl.pallas_call(
        body, name=name, out_shape=_slab_shapes(items), in_specs=[ANY] * nt, out_specs=[ANY] * nt,
        input_output_aliases={t: t for t in range(nt)}, scratch_shapes=_dma_sems(3 * nt),
    )(*slabs)


def _half_shape(nm, r, c):
    return (r // 2, 4 * c) if GRAD_FORM[nm] == "col" else (4, r // 2, c)


def _swap_layer(gs, gsm, name, items=ALL_ITEMS):
    small = gsm is not None
    nt = len(items)

    def body(*refs):
        g_rs = refs[:nt]
        pos = nt
        if small:
            s_r = refs[pos]
            pos += 1
        got_os = refs[pos:pos + nt]
        pos += nt
        if small:
            slots_o = refs[pos]
            pos += 1
        send_sems, recv_sems = refs[pos], refs[pos + 1]
        x, y, c, _ = _place()
        sib = (x, y, 1 - c)
        sent = []
        for t, (item, g_r, got_o) in enumerate(zip(items, g_rs, got_os)):
            nm, (r, cc), _ = SHARD_ITEMS[item]
            ho = _half_rows(1 - c, r)
            src = g_r.at[ho, :] if GRAD_FORM[nm] == "col" else g_r.at[:, ho, :]
            sent.append(_remote(src, got_o, send_sems, recv_sems, t, sib))
        if small:
            ssend, srecv, loc_sem = refs[pos + 2], refs[pos + 3], refs[pos + 4]
            me = 4 * x + 2 * y + c
            loc = pltpu.make_async_copy(s_r, slots_o.at[me], loc_sem.at[0])
            loc.start()
            peers = [(x ^ (k >> 2), y ^ ((k >> 1) & 1), c ^ (k & 1)) for k in range(1, 8)]
            for k, peer in enumerate(peers):
                sent.append(_remote(s_r, slots_o.at[me], ssend, srecv, k, peer))
        for cp in sent:
            cp.start()
        for cp in sent[:nt]:
            cp.wait_recv()
        if small:
            for k, (px, py, pc) in enumerate(peers):
                _remote(s_r, slots_o.at[4 * px + 2 * py + pc], ssend, srecv, k, (px, py, pc)).wait_recv()
        for cp in sent:
            cp.wait_send()
        if small:
            loc.wait()

    outs = [_sds(_half_shape(*SHARD_ITEMS[item][0:1], *SHARD_ITEMS[item][1]), CDT) for item in items]
    ops = list(gs)
    sems = _dma_sems(nt)
    if small:
        outs.append(_sds((8, SMALL_ROWS, 128), F32))
        ops.append(gsm)
        sems = sems + _dma_sems(7) + [pltpu.SemaphoreType.DMA((1,))]
    res = pl.pallas_call(
        body, name=name, out_shape=outs, in_specs=[ANY] * len(ops), out_specs=[ANY] * len(outs), scratch_shapes=sems,
    )(*ops)
    return (res[:nt], res[nt]) if small else (res, None)


def _pair_add_t(own, got, half_idx, nm, r, name):
    tr = SUM_TILE[r]
    nb = (r // 2) // tr
    if GRAD_FORM[nm] == "col":
        blk = (tr, own.shape[1])
        own_spec = pl.BlockSpec(blk, lambda i, c_r: (c_r[0] * nb + i, 0))
        half_spec = pl.BlockSpec(blk, lambda i, c_r: (i, 0))
    else:
        blk = (4, tr, own.shape[2])
        own_spec = pl.BlockSpec(blk, lambda i, c_r: (0, c_r[0] * nb + i, 0))
        half_spec = pl.BlockSpec(blk, lambda i, c_r: (0, i, 0))

    def body(c_r, a_r, b_r, o_o):
        o_o[...] = (a_r[...].astype(F32) + b_r[...].astype(F32)).astype(CDT)

    grid_spec = pltpu.PrefetchScalarGridSpec(num_scalar_prefetch=1, grid=(nb,), in_specs=[own_spec, half_spec],
                                             out_specs=half_spec)
    return pl.pallas_call(body, name=name, grid_spec=grid_spec, out_shape=_sds(got.shape, CDT),
                          compiler_params=_params(("parallel",)))(half_idx, own, got)


def _sum4_t(ps, got3, buf, idx, layer, nm, r, name):
    tr = SUM_TILE[r]
    nb = (r // 2) // tr
    c = got3.shape[2]
    if GRAD_FORM[nm] == "col":
        ps_spec = pl.BlockSpec((tr, c), lambda i, x_r: (i, x_r[0]))
    else:
        ps_spec = pl.BlockSpec((None, tr, c), lambda i, x_r: (x_r[0], i, 0))

    def body(x_r, a_r, b_r, buf_r, o_o):
        o_o[...] = ((a_r[...].astype(F32) + b_r[0].astype(F32)) + b_r[1].astype(F32)) + b_r[2].astype(F32)

    grid_spec = pltpu.PrefetchScalarGridSpec(
        num_scalar_prefetch=1, grid=(nb,),
        in_specs=[ps_spec, pl.BlockSpec((3, tr, c), lambda i, x_r: (0, i, 0)), ANY],
        out_specs=pl.BlockSpec((None, tr, c), lambda i, x_r: (layer, x_r[1] * nb + i, 0)),
    )
    return pl.pallas_call(body, name=name, grid_spec=grid_spec, out_shape=_sds(buf.shape, F32),
                          input_output_aliases={3: 0}, compiler_params=_params(("parallel",)))(idx, ps, got3, buf)


def _scatter_layer(ps, name, items=ALL_ITEMS):
    nt = len(items)

    def body(*refs):
        srcs, dsts, send_sems, recv_sems = refs[:nt], refs[nt:2 * nt], refs[2 * nt], refs[2 * nt + 1]
        sends, recvs = _ici_copies("scatter", srcs, dsts, send_sems, recv_sems, None, items=items)
        for cp in sends:
            cp.start()
        for cp in recvs:
            cp.wait_recv()
        for cp in sends:
            cp.wait_send()

    return pl.pallas_call(
        body, name=name, out_shape=_got3_shapes(items), in_specs=[ANY] * nt, out_specs=[ANY] * nt,
        scratch_shapes=_dma_sems(3 * nt),
    )(*ps)


def _got3_shapes(items=ALL_ITEMS):
    return [_sds((3, SHARD_ITEMS[t][1][0] // 2, SHARD_ITEMS[t][1][1]), CDT) for t in items]


def _join_layer(bufs, name):
    def body(*refs):
        ins, outs, send_sems, recv_sems = refs[:NT], refs[NT:2 * NT], refs[2 * NT], refs[2 * NT + 1]
        x, y, c, _ = _place()
        sent = []
        for t, ((nm, (r, cc), _), b_o) in enumerate(zip(SHARD_ITEMS, outs)):
            hs = _half_rows(c, r)
            sent.append(_remote(b_o.at[:, hs, :], b_o.at[:, hs, :], send_sems, recv_sems, t, (x, y, 1 - c)))
        for cp in sent:
            cp.start()
        for t, ((nm, (r, cc), _), b_o) in enumerate(zip(SHARD_ITEMS, outs)):
            ho = _half_rows(1 - c, r)
            _remote(b_o.at[:, ho, :], b_o.at[:, ho, :], send_sems, recv_sems, t, (x, y, 1 - c)).wait_recv()
        for cp in sent:
            cp.wait_send()

    return pl.pallas_call(
        body, name=name, out_shape=[_sds(b.shape, F32) for b in bufs], in_specs=[ANY] * NT, out_specs=[ANY] * NT,
        input_output_aliases={t: t for t in range(NT)}, scratch_shapes=_dma_sems(NT),
    )(*bufs)


def _adamw3(w, g, m, v, name):
    nl, r, c = w.shape
    tr = SUM_TILE.get(r, r)
    if r % 8:
        blk = pl.BlockSpec((None, r, 256), lambda l, i: (l, 0, i))
        steps = c // 256
    else:
        blk = pl.BlockSpec((None, tr, c), lambda l, i: (l, i, 0))
        steps = r // tr

    def body(w_r, g_r, m_r, v_r, d_o, m_o, v_o):
        g_ = g_r[...]
        m_ = ADAM_B1 * m_r[...] + (1.0 - ADAM_B1) * g_
        v_ = ADAM_B2 * v_r[...] + (1.0 - ADAM_B2) * jnp.square(g_)
        m_hat = m_ / (1.0 - ADAM_B1 ** ADAM_STEP)
        v_hat = v_ / (1.0 - ADAM_B2 ** ADAM_STEP)
        d_o[...] = -ADAM_LR * (m_hat / (jnp.sqrt(v_hat) + ADAM_EPS) + ADAM_WD * w_r[...])
        m_o[...] = m_
        v_o[...] = v_

    return pl.pallas_call(
        body, name=name, grid=(nl, steps),
        in_specs=[blk] * 4, out_specs=[blk] * 3, out_shape=[_sds((nl, r, c), F32)] * 3,
        compiler_params=_params(("parallel", "parallel")),
    )(w, g, m, v)


def _full_weights(slabs, wb, layer, shard, items=ALL_ITEMS):
    ws = {}
    for t, slab in zip(items, slabs):
        nm, (r, c), kind = SHARD_ITEMS[t]
        slab = lax.dynamic_update_slice(slab, wb[nm][layer][None], (shard, 0, 0))
        ws[nm] = slab.reshape(4 * r, c) if kind == "row" else jnp.concatenate([slab[s] for s in range(4)], axis=1)
    return ws


def _exchange_forms(g, items=ALL_ITEMS):
    out = []
    for t in items:
        nm, (r, c), _ = SHARD_ITEMS[t]
        a = g[nm]
        if nm == "w_in":
            a = a.reshape(D, 4, c).transpose(1, 0, 2)
        elif GRAD_FORM[nm] == "3d":
            a = a.reshape(4, r, c)
        out.append(a)
    return out


SMALL_ITEMS = (("rel_bias_table", 2), ("ffn1_norm", 16), ("mix_norm", 16), ("ffn2_norm", 16), ("forget_bias", 1),
               ("fox_q_norm", 1), ("fox_k_norm", 1), ("swa_q_norm", 1), ("swa_k_norm", 1), ("swa_sinks", 1))
SMALL_ADAM_ROWS = 96


def _layer_fwd(h, lw, l, ride=None, late=None):
    rides = late["rides"] if late else {}

    def run(key, fn, *args):
        r = rides.get(key)
        if r is None:
            return fn(*args)
        out = fn(*args, ride=r)
        late["arrived"](key, out[-1])
        return out[0] if len(out) == 2 else out[:-1]

    sv = {"h0": h}
    a, sv["a1t"] = _rms_fwd(h, lw["ffn1_norm"], f"rms_fwd_a{l}")
    sv["gu1"], s, sv["s1t"] = run("ffn_in_a", _ffn_in, a, lw["ffn1_w_in"], f"ffn_in_a{l}")
    h = run("ffn_out_a", _mm_res, s, lw["ffn1_w_out"], h, 0.5, f"ffn_out_a{l}")
    sv["h1"] = h
    a, sv["amt"] = _rms_fwd(h, lw["mix_norm"], f"rms_fwd_m{l}")
    if late:
        late["need"](lw, "mixer")
    proj = run("proj", _mm, a, lw["w_mix"], F32, _row_tile(h.shape[0]), DP, f"proj{l}")
    sv["proj"] = proj
    qf, kf, vf, qs, kse, vse, c, ct, sv["qft"] = _qknorm_fwd(proj, lw["gfq"], lw["gfk"], lw["gsq"], lw["gsk"], lw["fb"],
                                                              f"qknorm_fwd{l}")
    ofox, lse_f, *rode = _fox_fwd(qf, kf, vf, c, ct, f"fox_fwd{l}", ride)
    oswa, lse_s = run("swa_fwd", _swa_fwd, qs, kse, vse, lw["bias"], lw["sinks"], f"swa_fwd{l}")
    if late:
        late["need"](lw, "gate")
    sv.update(qf=qf, kf=kf, vf=vf, qs=qs, kse=kse, vse=vse, c=c, ct=ct, ofox=ofox, oswa=oswa, lse_f=lse_f, lse_s=lse_s)
    h, sv["yt"], sv["pf"], sv["ps"], sv["oft"], sv["ost"] = _gate_out_fwd(
        ofox, oswa, lw["w_branch_fox"], lw["w_branch_swa"], proj, lw["w_out"], h, f"gate_out_fwd{l}")
    sv["h2"] = h
    a, sv["a2t"] = _rms_fwd(h, lw["ffn2_norm"], f"rms_fwd_b{l}")
    sv["gu2"], s, sv["s2t"] = _ffn_in(a, lw["ffn2_w_in"], f"ffn_in_b{l}")
    h = _mm_res(s, lw["ffn2_w_out"], h, 0.5, f"ffn_out_b{l}")
    return h, sv, rode


def _ffn_bwd(dh, dhb, h_in, at, gu, st, norm, w_in, w_out, tag, rides=None):
    r = rides or (None,) * 4
    rode = []

    def split(res, ride):
        if ride is None:
            return res
        rode.extend(res[-1])
        return res[0] if len(res) == 2 else res[:-1]

    dgu = split(_ffn_bwd_mid(dhb, w_out, gu, f"ffn_bwd_mid_{tag}", r[0]), r[0])
    d_w_out = split(_mm(st, dhb, CDT, 256, D, f"dw_ffn_out_{tag}", scale=0.5, ride=r[1]), r[1])
    dh, dhb, dg = split(_ffn_bwd_in(dgu, w_in, h_in, norm, dh, f"ffn_bwd_in_{tag}", r[2]), r[2])
    d_w_in = split(_mm(at, dgu, CDT, D, 256, f"dw_ffn_in_{tag}", ride=r[3]), r[3])
    return dh, dhb, d_w_out, d_w_in, dg, rode


def _layer_bwd(dh, dhb, sv, lw, l, ride=None, before_ffn1=None):
    g = {}
    dh, dhb, g["ffn2_w_out"], g["ffn2_w_in"], g["ffn2_norm"], _ = _ffn_bwd(
        dh, dhb, sv["h2"], sv["a2t"], sv["gu2"], sv["s2t"], lw["ffn2_norm"], lw["ffn2_w_in"], lw["ffn2_w_out"], f"b{l}")
    g["w_out"] = _mm(sv["yt"], dhb, CDT, 512, 512, f"dw_out{l}")
    dpf, dps, dga, dgb = _gate_out_bwd(dhb, lw["w_out"], sv["pf"], sv["ps"], sv["proj"], f"gate_out_bwd{l}")
    do_f, do_ft = _mm_nt(dpf, lw["w_branch_fox"], f"d_ofox{l}", with_t=True)
    do_s = _mm_nt(dps, lw["w_branch_swa"], f"d_oswa{l}")
    g["w_branch_fox"] = _mm(sv["oft"], dpf, CDT, 512, 512, f"dw_bfox{l}")
    g["w_branch_swa"] = _mm(sv["ost"], dps, CDT, 512, 512, f"dw_bswa{l}")
    dqf, dcq, dkf, dvf, dck, *rode = _fox_bwd(sv["qf"], sv["qft"], sv["kf"], sv["vf"], sv["c"], sv["ct"], sv["ofox"],
                                              sv["lse_f"], do_f, do_ft, f"fox_bwd{l}", ride)
    g["rode"] = rode
    dqs, dkse, dvse, dbias, dsk = _swa_bwd(sv["qs"], sv["kse"], sv["vse"], lw["bias"], lw["sinks"], sv["oswa"],
                                           sv["lse_s"], do_s, f"swa_bwd{l}")
    dproj, dgn = _qknorm_bwd(sv["proj"], dqf, dkf, dvf, dqs, dkse, dvse, dcq, dck, dga, dgb,
                             lw["gfq"], lw["gfk"], lw["gsq"], lw["gsk"], lw["fb"], f"qknorm_bwd{l}")
    g["w_mix"] = _mm(sv["amt"], dproj, CDT, D, 640, f"dw_mix{l}")
    dh, dhb, g["mix_norm"] = _mm_nt_rms(dproj, lw["w_mix"], sv["h1"], lw["mix_norm"], dh, f"d_am{l}")
    g["dbias"], g["dsk"], g["dgn"] = dbias, dsk, dgn
    rides = before_ffn1(g) if before_ffn1 else None
    dh, dhb, g["ffn1_w_out"], g["ffn1_w_in"], g["ffn1_norm"], g["rode_ffn1"] = _ffn_bwd(
        dh, dhb, sv["h0"], sv["a1t"], sv["gu1"], sv["s1t"], lw["ffn1_norm"], lw["ffn1_w_in"], lw["ffn1_w_out"], f"a{l}",
        rides)
    return dh, dhb, g


def kernel(x, meta_tokens, rel_bias_table, ffn1_norm, ffn1_w_in, ffn1_w_out, mix_norm, w_in, forget_bias, fox_q_norm, fox_k_norm, swa_q_norm, swa_k_norm, swa_sinks, w_branch_fox, w_branch_swa, w_out, ffn2_norm, ffn2_w_in, ffn2_w_out, loss_target, m_meta_tokens, m_rel_bias_table, m_ffn1_norm, m_ffn1_w_in, m_ffn1_w_out, m_mix_norm, m_w_in, m_forget_bias, m_fox_q_norm, m_fox_k_norm, m_swa_q_norm, m_swa_k_norm, m_swa_sinks, m_w_branch_fox, m_w_branch_swa, m_w_out, m_ffn2_norm, m_ffn2_w_in, m_ffn2_w_out, v_meta_tokens, v_rel_bias_table, v_ffn1_norm, v_ffn1_w_in, v_ffn1_w_out, v_mix_norm, v_w_in, v_forget_bias, v_fox_q_norm, v_fox_k_norm, v_swa_q_norm, v_swa_k_norm, v_swa_sinks, v_w_branch_fox, v_w_branch_swa, v_w_out, v_ffn2_norm, v_ffn2_w_in, v_ffn2_w_out):
    names = ["meta_tokens", "rel_bias_table", "ffn1_norm", "ffn1_w_in", "ffn1_w_out", "mix_norm", "w_in", "forget_bias",
             "fox_q_norm", "fox_k_norm", "swa_q_norm", "swa_k_norm", "swa_sinks", "w_branch_fox", "w_branch_swa", "w_out",
             "ffn2_norm", "ffn2_w_in", "ffn2_w_out"]
    w = dict(zip(names, [meta_tokens, rel_bias_table, ffn1_norm, ffn1_w_in, ffn1_w_out, mix_norm, w_in, forget_bias,
                         fox_q_norm, fox_k_norm, swa_q_norm, swa_k_norm, swa_sinks, w_branch_fox, w_branch_swa, w_out,
                         ffn2_norm, ffn2_w_in, ffn2_w_out]))
    m = dict(zip(names, [m_meta_tokens, m_rel_bias_table, m_ffn1_norm, m_ffn1_w_in, m_ffn1_w_out, m_mix_norm, m_w_in,
                         m_forget_bias, m_fox_q_norm, m_fox_k_norm, m_swa_q_norm, m_swa_k_norm, m_swa_sinks,
                         m_w_branch_fox, m_w_branch_swa, m_w_out, m_ffn2_norm, m_ffn2_w_in, m_ffn2_w_out]))
    v = dict(zip(names, [v_meta_tokens, v_rel_bias_table, v_ffn1_norm, v_ffn1_w_in, v_ffn1_w_out, v_mix_norm, v_w_in,
                         v_forget_bias, v_fox_q_norm, v_fox_k_norm, v_swa_q_norm, v_swa_k_norm, v_swa_sinks,
                         v_w_branch_fox, v_w_branch_swa, v_w_out, v_ffn2_norm, v_ffn2_w_in, v_ffn2_w_out]))
    xi, yi, ci = lax.axis_index("x"), lax.axis_index("y"), lax.axis_index("c")
    shard = 2 * xi + yi
    seq = x.shape[1]
    t = seq + BLK

    wb = {nm: w[nm].astype(CDT) for nm, _, _ in SHARD_ITEMS}
    wb_list = [wb[nm] for nm, _, _ in SHARD_ITEMS]
    mflat = meta_tokens.reshape(META_ROWS, 128)
    first = (0, 1)
    *slabs_first, mall = _gather_layer([wb_list[t] for t in first], mflat, 0, "gather_weights", first)
    mall = lax.dynamic_update_slice(mall, mflat[None], (shard, 0, 0))
    meta_full = jnp.concatenate([mall[s].reshape(N_META, 256) for s in range(4)], axis=1)
    bias = _bias_fwd(rel_bias_table, "bias_fwd")

    def layer_weights(slabs, l, items=ALL_ITEMS):
        lw = _full_weights(slabs, wb, l, shard, items)
        if "w_in" in lw:
            lw["w_mix"] = _mix_cols(lw.pop("w_in"))
        return lw

    def layer_vectors(l):
        lw = {nm: w[nm][l].reshape(1, D) for nm in ("ffn1_norm", "mix_norm", "ffn2_norm")}
        lw["gfq"] = jnp.tile(fox_q_norm[l], 8).reshape(1, 512)
        lw["gfk"] = jnp.tile(fox_k_norm[l], 8).reshape(1, 512)
        lw["gsq"] = jnp.tile(swa_q_norm[l], 8).reshape(1, 512)
        lw["gsk"] = jnp.tile(swa_k_norm[l], 2).reshape(1, 128)
        lw["fb"] = jnp.pad(forget_bias[l], (0, 120)).reshape(1, 128)
        lw["sinks"] = swa_sinks[l]
        lw["bias"] = bias
        return lw

    def gather_ride(layer, items):
        return ("gather", [wb_list[t] for t in items], _slab_shapes(items), layer, items)

    landed = {}

    def need(lw, stage):
        if stage == "mixer":
            items = (2,)
            slabs = _forward_layer(landed["ffn_in_a"], "forward_halves0m", items)
        else:
            items = (3, 4, 5, 6, 7)
            slabs = _forward_layer(landed["ffn_out_a"] + landed["proj"] + landed["swa_fwd"], "forward_halves0g", items)
        lw.update(layer_weights(slabs, 0, items))

    late = {"rides": {"ffn_in_a": gather_ride(0, (2,)), "ffn_out_a": gather_ride(0, (3, 4, 5)),
                      "proj": gather_ride(0, (6,)), "swa_fwd": gather_ride(0, (7,))},
            "arrived": landed.__setitem__, "need": need}

    h = jnp.concatenate([jnp.zeros((PAD, D), F32), meta_full, x[0]], axis=0)
    lws = [{**layer_vectors(0), **layer_weights(slabs_first, 0, first)}]
    h, sv0, slabs1 = _layer_fwd(h, lws[0], 0, gather_ride(1, ALL_ITEMS), late)
    lws.append({**layer_vectors(1), **layer_weights(_forward_layer(slabs1, "forward_halves"), 1)})
    h, sv1, _ = _layer_fwd(h, lws[1], 1)
    saved = [sv0, sv1]
    dh, dhb, lacc = _loss(h, loss_target[0], "loss")
    loss = lax.psum(lacc[0, 0], ("x", "y", "c"))

    half_idx = ci.reshape(1).astype(jnp.int32)
    place_idx = jnp.stack([shard, ci]).astype(jnp.int32)

    def pair_sums(g, gsm, tag, items=ALL_ITEMS):
        if "w_mix" in g:
            g["w_in"] = _unmix_cols(g.pop("w_mix"))
        forms = _exchange_forms(g, items)
        got, slots = _swap_layer(forms, gsm, f"swap_halves{tag}", items)
        return {t: _pair_add_t(a, b, half_idx, SHARD_ITEMS[t][0], SHARD_ITEMS[t][1][0],
                               f"pair_add{tag}_{SHARD_ITEMS[t][0]}")
                for t, a, b in zip(items, forms, got)}, slots

    def scatter_ride(ps, items):
        return ("scatter", [ps[t] for t in items], _got3_shapes(items), None, items)

    early = (2, 3, 4, 5, 6, 7)
    early_rides = ((6,), (7,), (2, 5), (3, 4))
    ps0 = {}

    def before_ffn1(g):
        ps0.update(pair_sums(g, None, "0e", early)[0])
        return [scatter_ride(ps0, items) for items in early_rides]

    grads = [None, None]
    dh, dhb, grads[1] = _layer_bwd(dh, dhb, saved[1], lws[1], 1)
    ps1, _ = pair_sums(grads[1], None, 1)
    dh, dhb, grads[0] = _layer_bwd(dh, dhb, saved[0], lws[0], 0, scatter_ride(ps1, ALL_ITEMS), before_ffn1)
    grad_x = dh[BLK:].reshape(1, seq, D)
    dtab = _bias_bwd(grads[0]["dbias"] + grads[1]["dbias"], "bias_bwd")

    small = [dh[PAD:BLK].reshape(128, 128), _rows128(dtab[:, :N_BUCKETS].T, 2)]
    for nm in ("ffn1_norm", "mix_norm", "ffn2_norm"):
        small.append(jnp.stack([grads[l][nm][0] for l in range(2)]).reshape(16, 128))
    small.append(_rows128(jnp.stack([grads[l]["dgn"][4, :8] for l in range(2)]), 1))
    for row in range(4):
        small.append(jnp.stack([grads[l]["dgn"][row, :HD] for l in range(2)]).reshape(1, 128))
    dsk = [grads[l]["dsk"][:, 0, :] for l in range(2)]
    small.append(_rows128(jnp.stack([jnp.stack([d[:, 0], d[:, HD]], axis=1).reshape(8) for d in dsk]), 1))
    gsm = jnp.concatenate(small, axis=0)
    gsm = jnp.pad(gsm, ((0, SMALL_ROWS - gsm.shape[0]), (0, 0)))

    late = (0, 1)
    ps_late, slots = pair_sums(grads[0], gsm, "0l", late)
    ps0.update(ps_late)
    got3_0 = dict(zip([t for items in early_rides for t in items], grads[0]["rode_ffn1"]))
    got3_0.update(zip(late, _scatter_layer([ps0[t] for t in late], "scatter_shards", late)))
    got3 = [got3_0, dict(zip(ALL_ITEMS, grads[0]["rode"]))]
    bufs = []
    for t, (nm, (r, c), _) in enumerate(SHARD_ITEMS):
        buf = lax.empty((2, r, c), F32)
        for l, ps in ((1, ps1), (0, ps0)):
            buf = _sum4_t(ps[t], got3[l][t], buf, place_idx, l, nm, r, f"sum4_{l}_{nm}")
        bufs.append(buf)
    bufs = _join_layer(bufs, "join_halves")
    gs = _sum8(slots, "sum8")

    g_out = {nm: buf for (nm, _, _), buf in zip(SHARD_ITEMS, bufs)}
    g_out["meta_tokens"] = lax.dynamic_slice(gs[0:128].reshape(N_META, D), (0, shard * 256), (N_META, 256))
    off = 128
    for nm, rows in SMALL_ITEMS:
        n = w[nm].size
        g_out[nm] = gs[off:off + rows].reshape(-1)[:n].reshape(w[nm].shape)
        off += rows

    delta, new_m, new_v = {}, {}, {}
    for nm, _, _ in SHARD_ITEMS:
        if nm == "w_in":
            tr_ = lambda a: jnp.swapaxes(a, 1, 2)
            delta[nm], new_m[nm], new_v[nm] = (tr_(a) for a in _adamw3(tr_(w[nm]), tr_(g_out[nm]), tr_(m[nm]), tr_(v[nm]),
                                                                        f"adamw_{nm}"))
        else:
            delta[nm], new_m[nm], new_v[nm] = _adamw3(w[nm], g_out[nm], m[nm], v[nm], f"adamw_{nm}")
    small_names = ["meta_tokens"] + [nm for nm, _ in SMALL_ITEMS]
    small_rows = [META_ROWS] + [rows for _, rows in SMALL_ITEMS]

    def pack_small(src):
        buf = jnp.concatenate([_rows128(src[nm], rows) for nm, rows in zip(small_names, small_rows)], axis=0)
        return jnp.pad(buf, ((0, SMALL_ADAM_ROWS - buf.shape[0]), (0, 0)))

    d_, m_, v_ = (a[0] for a in _adamw3(pack_small(w)[None], pack_small(g_out)[None], pack_small(m)[None],
                                        pack_small(v)[None], "adamw_small"))
    off = 0
    for nm, rows in zip(small_names, small_rows):
        n = w[nm].size
        for dst, src in ((delta, d_), (new_m, m_), (new_v, v_)):
            dst[nm] = src[off:off + rows].reshape(-1)[:n].reshape(w[nm].shape)
        off += rows

    return (loss, grad_x, *[g_out[n] for n in names], *[delta[n] for n in names],
            *[new_m[n] for n in names], *[new_v[n] for n in names])
```

```python
import math

import numpy as np
import jax
import jax.numpy as jnp
from jax import lax
from jax.experimental import pallas as pl
from jax.experimental.pallas import tpu as pltpu

D = 1024
F = 2816
FT = F // 2
HD = 64
NPAIR = 4
N_META = 16
BLK = 128
PAD = BLK - N_META
EPS = 1e-6
NEG = -1e30
N_BUCKETS = 32
GA, GB, QA, KA, VA, QB, KB, VB, FA, DP = 0, 1024, 2048, 2560, 3072, 3584, 4096, 4224, 4352, 4480
D_IN = 4360
CDT = jnp.bfloat16
F32 = jnp.float32
VMEM_LIMIT = 48 * 1024 * 1024
MESH_ID = pl.DeviceIdType.MESH

ADAM_LR, ADAM_B1, ADAM_B2, ADAM_EPS, ADAM_WD, ADAM_STEP = 0.001, 0.9, 0.999, 1e-08, 0.01, 10

SHARD_ITEMS = (
    ("ffn1_w_in", (1024, 1408), "col"),
    ("ffn1_w_out", (704, 1024), "row"),
    ("w_in", (1024, 1090), "col"),
    ("w_branch_fox", (512, 256), "col"),
    ("w_branch_swa", (512, 256), "col"),
    ("w_out", (256, 1024), "row"),
    ("ffn2_w_in", (1024, 1408), "col"),
    ("ffn2_w_out", (704, 1024), "row"),
)
SMALL_ROWS = 192
META_ROWS = 32


def _row_tile(t):
    return 384 if t % 384 == 0 else 128


def _dot(a, b):
    return jnp.dot(a, b, preferred_element_type=F32)


def _dot_nt(a, b):
    return lax.dot_general(a, b, (((1,), (1,)), ((), ())), preferred_element_type=F32)


def _dot_hi(a, b):
    return jnp.dot(a, b, preferred_element_type=F32, precision=lax.Precision.HIGHEST)


def _sigmoid(x):
    return 0.5 * jnp.tanh(0.5 * x) + 0.5


def _iota(shape, dim):
    return lax.broadcasted_iota(jnp.int32, shape, dim)


def _params(sem=None):
    return pltpu.CompilerParams(dimension_semantics=sem, vmem_limit_bytes=VMEM_LIMIT)


def _sds(shape, dtype):
    return jax.ShapeDtypeStruct(shape, dtype)


def _rms_fwd(h, g, name):
    t = h.shape[0]
    tm = _row_tile(t)

    def body(h_ref, g_ref, a_ref, at_ref):
        x = h_ref[...]
        ms = jnp.mean(x * x, axis=-1, keepdims=True)
        a = x * lax.rsqrt(ms + EPS) * g_ref[...]
        a_ref[...] = a.astype(CDT)
        at_ref[...] = a.T.astype(CDT)

    return pl.pallas_call(
        body, name=name, grid=(t // tm,),
        in_specs=[pl.BlockSpec((tm, D), lambda i: (i, 0)), pl.BlockSpec((1, D), lambda i: (0, 0))],
        out_specs=[pl.BlockSpec((tm, D), lambda i: (i, 0)), pl.BlockSpec((D, tm), lambda i: (0, i))],
        out_shape=[_sds((t, D), CDT), _sds((D, t), CDT)],
        compiler_params=_params(("parallel",)),
    )(h, g)


def _ffn_in(a, w_in, name, ride=None):
    t = a.shape[0]
    tm = _row_tile(t)
    tn = FT
    nj = F // tn
    grid = (nj, t // tm)
    ride_in, ride_in_specs, ride_out, ride_out_specs, ride_sems = _ride_specs(ride)

    def body(a_ref, wg_ref, wu_ref, gu_ref, s_ref, st_ref):
        a_ = a_ref[...]
        g = _dot(a_, wg_ref[...])
        u = _dot(a_, wu_ref[...])
        s = g * _sigmoid(g) * u
        gu_ref[0] = g.astype(CDT)
        gu_ref[1] = u.astype(CDT)
        s_ref[...] = s.astype(CDT)
        st_ref[...] = s.T.astype(CDT)

    res = pl.pallas_call(
        _riding(body, 3, 3, ride, grid), name=name, grid=grid,
        in_specs=[pl.BlockSpec((tm, D), lambda j, i: (i, 0)),
                  pl.BlockSpec((D, tn), lambda j, i: (0, j)),
                  pl.BlockSpec((D, tn), lambda j, i: (0, j + nj))] + ride_in_specs,
        out_specs=[pl.BlockSpec((2, tm, tn), lambda j, i: (0, i, j)),
                   pl.BlockSpec((tm, tn), lambda j, i: (i, j)),
                   pl.BlockSpec((tn, tm), lambda j, i: (j, i))] + ride_out_specs,
        out_shape=[_sds((2, t, F), CDT), _sds((t, F), CDT), _sds((F, t), CDT)] + ride_out, scratch_shapes=ride_sems,
        compiler_params=_params(("arbitrary", "arbitrary") if ride else ("parallel", "parallel")),
    )(a, w_in, w_in, *ride_in)
    return (*res[:3], res[3:]) if ride else res


def _mm_res(a, b, res, scale, name, ride=None):
    t, k = a.shape
    n = b.shape[1]
    tm = _row_tile(t)
    tn = n
    grid = (t // tm, n // tn)
    ride_in, ride_in_specs, ride_out, ride_out_specs, ride_sems = _ride_specs(ride)

    def body(a_ref, b_ref, r_ref, o_ref):
        o_ref[...] = r_ref[...] + scale * _dot(a_ref[...], b_ref[...])

    out = pl.pallas_call(
        _riding(body, 3, 1, ride, grid), name=name, grid=grid,
        in_specs=[pl.BlockSpec((tm, k), lambda i, j: (i, 0)),
                  pl.BlockSpec((k, tn), lambda i, j: (0, j)),
                  pl.BlockSpec((tm, tn), lambda i, j: (i, j))] + ride_in_specs,
        out_specs=[pl.BlockSpec((tm, tn), lambda i, j: (i, j))] + ride_out_specs,
        out_shape=[_sds((t, n), F32)] + ride_out, scratch_shapes=ride_sems,
        compiler_params=_params(("arbitrary", "arbitrary") if ride else ("parallel", "parallel")),
    )(a, b, res, *ride_in)
    return (out[0], out[1:]) if ride else out[0]


def _mm(a, b, out_dtype, tm, tn, name, scale=1.0, ride=None):
    m, k = a.shape
    if b.ndim == 3:
        nh = b.shape[2] // tn
        n = 2 * b.shape[2]
        b_spec = pl.BlockSpec((None, k, tn), lambda i, j: (j // nh, 0, j % nh))
    else:
        n = b.shape[1]
        b_spec = pl.BlockSpec((k, tn), lambda i, j: (0, j))
    grid = (m // tm, n // tn)
    ride_in, ride_in_specs, ride_out, ride_out_specs, ride_sems = _ride_specs(ride)

    def body(a_ref, b_ref, o_ref):
        o_ref[...] = (scale * _dot(a_ref[...], b_ref[...])).astype(out_dtype)

    res = pl.pallas_call(
        _riding(body, 2, 1, ride, grid), name=name, grid=grid,
        in_specs=[pl.BlockSpec((tm, k), lambda i, j: (i, 0)), b_spec] + ride_in_specs,
        out_specs=[pl.BlockSpec((tm, tn), lambda i, j: (i, j))] + ride_out_specs,
        out_shape=[_sds((m, n), out_dtype)] + ride_out, scratch_shapes=ride_sems,
        compiler_params=_params(("arbitrary", "arbitrary") if ride else ("parallel", "parallel")),
    )(a, b, *ride_in)
    return (res[0], res[1:]) if ride else res[0]


def _ffn_bwd_mid(dhb, w_out, gu, name, ride=None):
    t = dhb.shape[0]
    tm = _row_tile(t)
    tn = FT
    grid = (F // tn, t // tm)
    ride_in, ride_in_specs, ride_out, ride_out_specs, ride_sems = _ride_specs(ride)

    def body(dh_ref, w_ref, gu_ref, o_ref):
        ds = _dot_nt(dh_ref[...] * 0.5, w_ref[...])
        g = gu_ref[0].astype(F32)
        u = gu_ref[1].astype(F32)
        sg = _sigmoid(g)
        o_ref[0] = (ds * u * (sg * (1.0 + g * (1.0 - sg)))).astype(CDT)
        o_ref[1] = (ds * (g * sg)).astype(CDT)

    res = pl.pallas_call(
        _riding(body, 3, 1, ride, grid), name=name, grid=grid,
        in_specs=[pl.BlockSpec((tm, D), lambda j, i: (i, 0)),
                  pl.BlockSpec((tn, D), lambda j, i: (j, 0)),
                  pl.BlockSpec((2, tm, tn), lambda j, i: (0, i, j))] + ride_in_specs,
        out_specs=[pl.BlockSpec((2, tm, tn), lambda j, i: (0, i, j))] + ride_out_specs,
        out_shape=[_sds((2, t, F), CDT)] + ride_out, scratch_shapes=ride_sems,
        compiler_params=_params(("arbitrary", "arbitrary") if ride else ("parallel", "parallel")),
    )(dhb, w_out, gu, *ride_in)
    return (res[0], res[1:]) if ride else res[0]


def _rms_bwd_rows(da_, x, g, dres, i, dh_ref, dhb_ref, dg_ref):
    r = lax.rsqrt(jnp.mean(x * x, axis=-1, keepdims=True) + EPS)
    xh = x * r
    day = da_ * g
    dh = dres + r * (day - xh * jnp.mean(day * xh, axis=-1, keepdims=True))
    dh_ref[...] = dh
    dhb_ref[...] = dh.astype(CDT)

    @pl.when(i == 0)
    def _():
        dg_ref[...] = jnp.zeros(dg_ref.shape, F32)

    dg_ref[0:1, :] += jnp.sum(da_ * xh, axis=0, keepdims=True)


def _ffn_bwd_in(dgu, w_in, h, g, dres, name, ride=None):
    t = dgu.shape[1]
    tm = _row_tile(t)
    grid = (t // tm,)
    ride_in, ride_in_specs, ride_out, ride_out_specs, ride_sems = _ride_specs(ride)

    def body(dg_ref, wg_ref, wu_ref, h_ref, g_ref, dr_ref, dh_ref, dhb_ref, dgn_ref):
        da_ = _dot_nt(dg_ref[0], wg_ref[...]) + _dot_nt(dg_ref[1], wu_ref[...])
        _rms_bwd_rows(da_, h_ref[...], g_ref[...], dr_ref[...], pl.program_id(0), dh_ref, dhb_ref, dgn_ref)

    row = pl.BlockSpec((tm, D), lambda i: (i, 0))
    res = pl.pallas_call(
        _riding(body, 6, 3, ride, grid), name=name, grid=grid,
        in_specs=[pl.BlockSpec((2, tm, F), lambda i: (0, i, 0)),
                  pl.BlockSpec((D, F), lambda i: (0, 0)),
                  pl.BlockSpec((D, F), lambda i: (0, 1)),
                  row, pl.BlockSpec((1, D), lambda i: (0, 0)), row] + ride_in_specs,
        out_specs=[row, row, pl.BlockSpec((8, D), lambda i: (0, 0))] + ride_out_specs,
        out_shape=[_sds((t, D), F32), _sds((t, D), CDT), _sds((8, D), F32)] + ride_out, scratch_shapes=ride_sems,
        compiler_params=_params(("arbitrary",)),
    )(dgu, w_in, w_in, h, g, dres, *ride_in)
    return (*res[:3], res[3:]) if ride else res


def _mm_nt_rms(a, b, h, g, dres, name):
    t, n = a.shape
    tm = _row_tile(t)

    def body(a_ref, b_ref, h_ref, g_ref, dr_ref, dh_ref, dhb_ref, dgn_ref):
        da_ = _dot_nt(a_ref[...], b_ref[...])
        _rms_bwd_rows(da_, h_ref[...], g_ref[...], dr_ref[...], pl.program_id(0), dh_ref, dhb_ref, dgn_ref)

    row = pl.BlockSpec((tm, D), lambda i: (i, 0))
    return pl.pallas_call(
        body, name=name, grid=(t // tm,),
        in_specs=[pl.BlockSpec((tm, n), lambda i: (i, 0)), pl.BlockSpec((D, n), lambda i: (0, 0)),
                  row, pl.BlockSpec((1, D), lambda i: (0, 0)), row],
        out_specs=[row, row, pl.BlockSpec((8, D), lambda i: (0, 0))],
        out_shape=[_sds((t, D), F32), _sds((t, D), CDT), _sds((8, D), F32)],
        compiler_params=_params(("arbitrary",)),
    )(a, b, h, g, dres)


def _loss(h, target, name):
    t = h.shape[0]
    ta = _row_tile(t)
    nb = ta // BLK

    def body(h_ref, *refs):
        t_refs, (dh_ref, dhb_ref, l_ref) = refs[:nb], refs[nb:]
        i = pl.program_id(0)

        @pl.when(i == 0)
        def _():
            l_ref[...] = jnp.zeros(l_ref.shape, F32)

        tot = 0.0
        for b in range(nb):
            rows = slice(BLK * b, BLK * (b + 1))
            err = jnp.where(i * nb + b > 0, h_ref[rows, :] - t_refs[b][...], 0.0)
            tot = tot + jnp.sum(err * err)
            d = err * (1.0 / D)
            dh_ref[rows, :] = d
            dhb_ref[rows, :] = d.astype(CDT)
        l_ref[...] += (0.5 / D) * tot

    row = pl.BlockSpec((ta, D), lambda i: (i, 0))
    tspecs = [pl.BlockSpec((BLK, D), lambda i, b=b: (jnp.maximum(i * nb + b - 1, 0), 0)) for b in range(nb)]
    return pl.pallas_call(
        body, name=name, grid=(t // ta,),
        in_specs=[row] + tspecs,
        out_specs=[row, row, pl.BlockSpec((8, 128), lambda i: (0, 0))],
        out_shape=[_sds((t, D), F32), _sds((t, D), CDT), _sds((8, 128), F32)],
        compiler_params=_params(("arbitrary",)),
    )(h, *([target] * nb))


def _block_diag():
    return (_iota((128, 128), 0) // HD == _iota((128, 128), 1) // HD).astype(F32)


def _head_sums(v, bd):
    hi = v.astype(CDT)
    rest = (v - hi.astype(F32)).astype(CDT)
    b = bd.astype(CDT)
    return _dot(hi, b) + _dot(rest, b)


def _dup_halves(x, lo):
    sw = pltpu.roll(x, 64, 1)
    return jnp.where(lo, x, sw), jnp.where(lo, sw, x)


def _qknorm_fwd(proj, gfq, gfk, gsq, gsk, fb, name):
    t = proj.shape[0]
    tm = _row_tile(t)

    def body(qa, ka, va, qb, kb, vb, fa, gfq_r, gfk_r, gsq_r, gsk_r, fb_r,
             qf_o, kf_o, vf_o, qs_o, kse_o, vse_o, c_o, ct_o, qft_o, carry):
        i = pl.program_id(0)
        bd = _block_diag()
        lane = _iota((1, 128), 1)
        lo = lane < HD

        def hnorm(x, g):
            ms = _head_sums(x * x, bd) * (1.0 / HD)
            return x * lax.rsqrt(ms + EPS) * g

        for ch in range(4):
            sl = slice(128 * ch, 128 * (ch + 1))
            qn = hnorm(qa[:, sl], gfq_r[:, sl]) * 0.125
            qf_o[:, sl] = qn.astype(CDT)
            qft_o[sl, :] = qn.T.astype(CDT)
            kf_o[:, sl] = hnorm(ka[:, sl], gfk_r[:, sl]).astype(CDT)
            qs_o[:, sl] = (hnorm(qb[:, sl], gsq_r[:, sl]) * 0.125).astype(CDT)
        vf_o[...] = va[...].astype(CDT)
        k0, k1 = _dup_halves(hnorm(kb[...], gsk_r[...]), lo)
        kse_o[0] = k0.astype(CDT)
        kse_o[1] = k1.astype(CDT)
        v0, v1 = _dup_halves(vb[...], lo)
        vse_o[0] = v0.astype(CDT)
        vse_o[1] = v1.astype(CDT)

        z = fa[...] + fb_r[...]
        lf = jnp.minimum(z, 0.0) - jnp.log(1.0 + jnp.exp(-jnp.abs(z)))
        lf = jnp.where(lane < 8, lf, 0.0)
        ltri = (_iota((tm, tm), 1) <= _iota((tm, tm), 0)).astype(F32)

        @pl.when(i == 0)
        def _():
            carry[...] = jnp.zeros(carry.shape, F32)

        c = _dot_hi(ltri, lf) + carry[0:1, :]
        carry[0:1, :] = c[tm - 1:tm, :]
        c_o[...] = c
        ct_o[...] = c.T[0:8, :]

    def col(width, off):
        return pl.BlockSpec((tm, width), lambda i: (i, off // width))

    def vec(width):
        return pl.BlockSpec((1, width), lambda i: (0, 0))

    return pl.pallas_call(
        body, name=name, grid=(t // tm,),
        in_specs=[col(512, QA), col(512, KA), col(512, VA), col(512, QB), col(128, KB), col(128, VB), col(128, FA),
                  vec(512), vec(512), vec(512), vec(128), vec(128)],
        out_specs=[pl.BlockSpec((tm, 512), lambda i: (i, 0))] * 4
        + [pl.BlockSpec((2, tm, 128), lambda i: (0, i, 0))] * 2
        + [pl.BlockSpec((tm, 128), lambda i: (i, 0)), pl.BlockSpec((8, tm), lambda i: (0, i)),
           pl.BlockSpec((512, tm), lambda i: (0, i))],
        out_shape=[_sds((t, 512), CDT)] * 4 + [_sds((2, t, 128), CDT)] * 2
        + [_sds((t, 128), F32), _sds((8, t), F32), _sds((512, t), CDT)],
        scratch_shapes=[pltpu.VMEM((8, 128), F32)],
        compiler_params=_params(("arbitrary",)),
    )(proj, proj, proj, proj, proj, proj, proj, gfq, gfk, gsq, gsk, fb)


def _qknorm_bwd(proj, dqf, dkf, dvf, dqs, dkse, dvse, dcq, dck, dga, dgb, gfq, gfk, gsq, gsk, fb, name):
    t = proj.shape[0]
    tm = _row_tile(t)
    nt = t // tm

    def body(qa, ka, qb, kb, fa, dqf_r, dkf_r, dvf_r, dqs_r, dkse_r, dvse_r, dcq_r, dck_r, dga_r, dgb_r,
             gfq_r, gfk_r, gsq_r, gsk_r, fb_r, dp_o, dgn_o, carry, acc):
        i = pl.program_id(0)
        bd = _block_diag()
        lane = _iota((1, 128), 1)
        lo = lane < HD

        @pl.when(i == 0)
        def _():
            carry[...] = jnp.zeros(carry.shape, F32)
            acc[...] = jnp.zeros(acc.shape, F32)

        def hnorm_bwd(x, g, dy):
            r = lax.rsqrt(_head_sums(x * x, bd) * (1.0 / HD) + EPS)
            xh = x * r
            day = dy * g
            dx = r * (day - xh * (_head_sums(day * xh, bd) * (1.0 / HD)))
            return dx, jnp.sum(dy * xh, axis=0, keepdims=True)

        for ch in range(4):
            sl = slice(128 * ch, 128 * (ch + 1))
            dx, dg = hnorm_bwd(qa[:, sl], gfq_r[:, sl], dqf_r[:, sl] * 0.125)
            dp_o[:, QA + 128 * ch:QA + 128 * (ch + 1)] = dx.astype(CDT)
            acc[0:1, sl] += dg
            dx, dg = hnorm_bwd(ka[:, sl], gfk_r[:, sl], dkf_r[:, sl])
            dp_o[:, KA + 128 * ch:KA + 128 * (ch + 1)] = dx.astype(CDT)
            acc[1:2, sl] += dg
            dx, dg = hnorm_bwd(qb[:, sl], gsq_r[:, sl], dqs_r[:, sl] * 0.125)
            dp_o[:, QB + 128 * ch:QB + 128 * (ch + 1)] = dx.astype(CDT)
            acc[2:3, sl] += dg
        dp_o[:, VA:VA + 512] = dvf_r[...].astype(CDT)
        dp_o[:, GA:GA + D] = dga_r[...]
        dp_o[:, GB:GB + D] = dgb_r[...]

        def fold(x):
            e0 = x[0]
            e1 = x[1]
            return jnp.where(lo, e0 + pltpu.roll(e0, 64, 1), e1 + pltpu.roll(e1, 64, 1))

        dx, dg = hnorm_bwd(kb[...], gsk_r[...], fold(dkse_r))
        dp_o[:, KB:KB + 128] = dx.astype(CDT)
        acc[3:4, 0:128] += dg
        dp_o[:, VB:VB + 128] = fold(dvse_r).astype(CDT)

        rr = _iota((512, 128), 0)
        hh = _iota((512, 128), 1)
        sel = ((rr == (hh >> 1) * 128 + (hh & 1) * HD) & (hh < 8)).astype(F32)
        dcs = _dot_hi(dcq_r[...] - dck_r[...], sel)
        utri = (_iota((tm, tm), 1) >= _iota((tm, tm), 0)).astype(F32)
        dlf = _dot_hi(utri, dcs) + carry[0:1, :]
        carry[0:1, :] = dlf[0:1, :]
        z = fa[...] + fb_r[...]
        dfa = jnp.where(lane < 8, dlf * _sigmoid(-z), 0.0)
        dp_o[:, FA:FA + 128] = dfa.astype(CDT)
        acc[4:5, 0:128] += jnp.sum(dfa, axis=0, keepdims=True)

        @pl.when(i == nt - 1)
        def _():
            foldm = ((_iota((512, 128), 0) & (HD - 1)) == _iota((512, 128), 1)).astype(F32)
            dgn_o[...] = _dot_hi(acc[...], foldm)

    def col(width, off):
        return pl.BlockSpec((tm, width), lambda i: (nt - 1 - i, off // width))

    def rows(width):
        return pl.BlockSpec((tm, width), lambda i: (nt - 1 - i, 0))

    def vec(width):
        return pl.BlockSpec((1, width), lambda i: (0, 0))

    pair = pl.BlockSpec((2, tm, 128), lambda i: (0, nt - 1 - i, 0))
    return pl.pallas_call(
        body, name=name, grid=(nt,),
        in_specs=[col(512, QA), col(512, KA), col(512, QB), col(128, KB), col(128, FA),
                  rows(512), rows(512), rows(512), rows(512), pair, pair, rows(512), rows(512), rows(D), rows(D),
                  vec(512), vec(512), vec(512), vec(128), vec(128)],
        out_specs=[rows(DP), pl.BlockSpec((8, 128), lambda i: (0, 0))],
        out_shape=[_sds((t, DP), CDT), _sds((8, 128), F32)],
        scratch_shapes=[pltpu.VMEM((8, 128), F32), pltpu.VMEM((8, 512), F32)],
        compiler_params=_params(("arbitrary",)),
    )(proj, proj, proj, proj, proj, dqf, dkf, dvf, dqs, dkse, dvse, dcq, dck, dga, dgb, gfq, gfk, gsq, gsk, fb)


def _gate_out_fwd(ofox, oswa, wbf, wbs, proj, w_out, h, name):
    t = ofox.shape[0]
    tm = _row_tile(t)

    def body(of_r, os_r, wf_r, ws_r, ga_r, gb_r, wo_r, h_r, ho_o, yt_o, pf_o, ps_o, oft_o, ost_o):
        pf = _dot(of_r[...], wf_r[...])
        ps = _dot(os_r[...], ws_r[...])
        y = _sigmoid(ga_r[...]) * pf + _sigmoid(gb_r[...]) * ps
        ho_o[...] = h_r[...] + _dot(y.astype(CDT), wo_r[...])
        yt_o[...] = y.T.astype(CDT)
        pf_o[...] = pf.astype(CDT)
        ps_o[...] = ps.astype(CDT)
        oft_o[...] = of_r[...].astype(F32).T.astype(CDT)
        ost_o[...] = os_r[...].astype(F32).T.astype(CDT)

    row = pl.BlockSpec((tm, D), lambda i: (i, 0))
    half = pl.BlockSpec((tm, 512), lambda i: (i, 0))
    whole = lambda r: pl.BlockSpec((r, D), lambda i: (0, 0))
    tcol = lambda r: pl.BlockSpec((r, tm), lambda i: (0, i))
    return pl.pallas_call(
        body, name=name, grid=(t // tm,),
        in_specs=[half, half, whole(512), whole(512),
                  pl.BlockSpec((tm, D), lambda i: (i, GA // D)), pl.BlockSpec((tm, D), lambda i: (i, GB // D)),
                  whole(D), row],
        out_specs=[row, tcol(D), row, row, tcol(512), tcol(512)],
        out_shape=[_sds((t, D), F32), _sds((D, t), CDT), _sds((t, D), CDT), _sds((t, D), CDT),
                   _sds((512, t), CDT), _sds((512, t), CDT)],
        compiler_params=_params(("parallel",)),
    )(ofox, oswa, wbf, wbs, proj, proj, w_out, h)


def _gate_out_bwd(dhb, w_out, pf, ps, proj, wbf, wbs, name):
    t = dhb.shape[0]
    tm = _row_tile(t)

    def body(dh_r, wo_r, pf_r, ps_r, ga_r, gb_r, wf_r, ws_r, dpf_o, dps_o, dga_o, dgb_o, dof_o, doft_o, dos_o):
        dy_ = _dot_nt(dh_r[...], wo_r[...])
        sa = _sigmoid(ga_r[...])
        sb = _sigmoid(gb_r[...])
        dpf = (dy_ * sa).astype(CDT)
        dps = (dy_ * sb).astype(CDT)
        dpf_o[...] = dpf
        dps_o[...] = dps
        dga_o[...] = (dy_ * pf_r[...].astype(F32) * (sa * (1.0 - sa))).astype(CDT)
        dgb_o[...] = (dy_ * ps_r[...].astype(F32) * (sb * (1.0 - sb))).astype(CDT)
        dof = _dot_nt(dpf, wf_r[...])
        dof_o[...] = dof
        doft_o[...] = dof.T.astype(CDT)
        dos_o[...] = _dot_nt(dps, ws_r[...])

    row = pl.BlockSpec((tm, D), lambda i: (i, 0))
    half = pl.BlockSpec((tm, 512), lambda i: (i, 0))
    whole = lambda r: pl.BlockSpec((r, D), lambda i: (0, 0))
    return pl.pallas_call(
        body, name=name, grid=(t // tm,),
        in_specs=[row, whole(D), row, row,
                  pl.BlockSpec((tm, D), lambda i: (i, GA // D)), pl.BlockSpec((tm, D), lambda i: (i, GB // D)),
                  whole(512), whole(512)],
        out_specs=[row] * 4 + [half, pl.BlockSpec((512, tm), lambda i: (0, i)), half],
        out_shape=[_sds((t, D), CDT)] * 4 + [_sds((t, 512), F32), _sds((512, t), CDT), _sds((t, 512), F32)],
        compiler_params=_params(("parallel",)),
    )(dhb, w_out, pf, ps, proj, proj, wbf, wbs)


def _tri_steps(n, by_key):
    if by_key:
        pairs = [(i, j) for j in range(n) for i in range(j, n)]
    else:
        pairs = [(i, j) for i in range(n) for j in range(i + 1)]
    return (np.array([p[0] for p in pairs], np.int32), np.array([p[1] for p in pairs], np.int32))


def _head_col(blk, lane, h):
    return jnp.sum(jnp.where(lane == h, blk, 0.0), axis=1, keepdims=True)


def _head_row(blk, sub, h):
    return jnp.sum(jnp.where(sub == h, blk, 0.0), axis=0, keepdims=True)


def _ride_specs(ride):
    if ride is None:
        return [], [], [], [], []
    kind, srcs, outs, layer, items = ride
    return list(srcs), [ANY] * len(srcs), list(outs), [ANY] * len(outs), _dma_sems(3 * len(srcs))


def _ride_start(ride, srcs, dsts, send_sems, recv_sems):
    for cp in _ici_copies(ride[0], srcs, dsts, send_sems, recv_sems, ride[3], recv=False, items=ride[4])[0]:
        cp.start()


def _ride_wait(ride, srcs, dsts, send_sems, recv_sems):
    sends, recvs = _ici_copies(ride[0], srcs, dsts, send_sems, recv_sems, ride[3], items=ride[4])
    for cp in recvs:
        cp.wait_recv()
    for cp in sends:
        cp.wait_send()


def _riding(body, n_in, n_out, ride, grid):
    if ride is None:
        return body
    nr = len(ride[1])

    def wrapped(*refs):
        ins, srcs = refs[:n_in], refs[n_in:n_in + nr]
        outs, dsts = refs[n_in + nr:n_in + nr + n_out], refs[n_in + nr + n_out:n_in + 2 * nr + n_out]
        scratch, sems = refs[n_in + 2 * nr + n_out:-2], refs[-2:]
        first = pl.program_id(0) == 0
        last = pl.program_id(0) == grid[0] - 1
        for a in range(1, len(grid)):
            first = first & (pl.program_id(a) == 0)
            last = last & (pl.program_id(a) == grid[a] - 1)

        @pl.when(first)
        def _():
            _ride_start(ride, srcs, dsts, *sems)

        body(*ins, *outs, *scratch)

        @pl.when(last)
        def _():
            _ride_wait(ride, srcs, dsts, *sems)

    return wrapped


def _fox_fwd(qf, kf, vf, c, ct, name, ride=None):
    t = qf.shape[0]
    ta = _row_tile(t)
    qi, kj = _tri_steps(t // ta, by_key=False)
    nsteps = len(qi)
    ride_in, ride_in_specs, ride_out, ride_out_specs, ride_sems = _ride_specs(ride)

    def body(qi_r, kj_r, q_r, k_r, v_r, c_r, ct_r, *rest):
        nr = len(ride_in)
        src_r, (o_o, lse_o), dst_o = rest[:nr], rest[nr:nr + 2], rest[nr + 2:2 * nr + 2]
        m_sc, l_sc, acc_sc, cq_sc, *sems = rest[2 * nr + 2:]
        p = pl.program_id(0)
        n = pl.program_id(1)
        i = qi_r[n]
        j = kj_r[n]
        lane = _iota((1, 128), 1)
        lo = lane < HD

        if ride is not None:
            @pl.when((p == 0) & (n == 0))
            def _():
                _ride_start(ride, src_r, dst_o, *sems)

        @pl.when(j == 0)
        def _():
            m_sc[...] = jnp.full(m_sc.shape, NEG, F32)
            l_sc[...] = jnp.zeros(l_sc.shape, F32)
            acc_sc[...] = jnp.zeros(acc_sc.shape, F32)
            for e in (0, 1):
                cq_sc[e] = jnp.broadcast_to(_head_col(c_r[...], lane, 2 * p + e), (ta, 128))

        def step(masked):
            q = q_r[...]
            k = k_r[...]
            vaug = jnp.concatenate([v_r[...], jnp.ones((ta, 128), CDT)], axis=1)
            if masked:
                rows = i * ta + _iota((ta, 1), 0)
                cols = j * ta + _iota((1, ta), 1)
                mask = (cols <= rows) & (cols >= PAD)
            sub = _iota((8, 1), 0)
            heads = (0, 1)
            sels = [lo, jnp.logical_not(lo)]
            s = [_dot_nt(jnp.where(sels[e], q, 0), k) for e in heads]
            ck = [_head_row(ct_r[...], sub, 2 * p + e) for e in heads]
            chunks = []
            for e in heads:
                cq = cq_sc[e]
                row = []
                for ch in range(ta // 128):
                    sl = slice(128 * ch, 128 * (ch + 1))
                    sc = s[e][:, sl] + cq - ck[e][:, sl]
                    if masked:
                        sc = jnp.where(mask[:, sl], sc, NEG)
                    row.append(sc)
                chunks.append(row)
            m_new, alphas = [], []
            for e in heads:
                mx = chunks[e][0]
                for sc in chunks[e][1:]:
                    mx = jnp.maximum(mx, sc)
                m_prev = m_sc[e]
                m_new.append(jnp.maximum(m_prev, jnp.max(mx, axis=1, keepdims=True)))
                alphas.append(jnp.exp(m_prev - m_new[e]))
            pe = [jnp.concatenate([jnp.exp(sc - m_new[e]).astype(CDT) for sc in chunks[e]], axis=1) for e in heads]
            pva = [_dot(pe[e], vaug) for e in heads]
            for e in heads:
                l_sc[e] = alphas[e] * l_sc[e] + pva[e][:, 128:]
                m_sc[e] = m_new[e]
            acc_sc[...] = (acc_sc[...] * jnp.where(lo, alphas[0], alphas[1])
                           + jnp.where(lo, pva[0][:, :128], pva[1][:, :128]))

        edge = (j == i) | (j == 0)

        @pl.when(edge)
        def _():
            step(True)

        @pl.when(jnp.logical_not(edge))
        def _():
            step(False)

        @pl.when(j == i)
        def _():
            l = jnp.where(lo, l_sc[0], l_sc[1])
            o_o[...] = (acc_sc[...] / l).astype(CDT)
            lse_o[...] = jnp.where(lo, m_sc[0], m_sc[1]) + jnp.log(l)

        if ride is not None:
            @pl.when((p == NPAIR - 1) & (n == nsteps - 1))
            def _():
                _ride_wait(ride, src_r, dst_o, *sems)

    qblk = pl.BlockSpec((ta, 128), lambda p, n, qi_r, kj_r: (qi_r[n], p))
    kblk = pl.BlockSpec((ta, 128), lambda p, n, qi_r, kj_r: (kj_r[n], p))
    grid_spec = pltpu.PrefetchScalarGridSpec(
        num_scalar_prefetch=2, grid=(NPAIR, nsteps),
        in_specs=[qblk, kblk, kblk,
                  pl.BlockSpec((ta, 128), lambda p, n, qi_r, kj_r: (qi_r[n], 0)),
                  pl.BlockSpec((8, ta), lambda p, n, qi_r, kj_r: (0, kj_r[n]))] + ride_in_specs,
        out_specs=[qblk, qblk] + ride_out_specs,
        scratch_shapes=[pltpu.VMEM((2, ta, 128), F32), pltpu.VMEM((2, ta, 128), F32), pltpu.VMEM((ta, 128), F32),
                        pltpu.VMEM((2, ta, 128), F32)] + ride_sems,
    )
    return pl.pallas_call(
        body, name=name, grid_spec=grid_spec,
        out_shape=[_sds((t, 512), CDT), _sds((t, 512), F32)] + ride_out,
        compiler_params=_params(("arbitrary", "arbitrary")),
    )(jnp.asarray(qi), jnp.asarray(kj), qf, kf, vf, c, ct, *ride_in)


def _fox_bwd(qf, qft, kf, vf, c, ct, o, lse, do, dot, name, ride=None):
    t = qf.shape[0]
    ta = _row_tile(t)
    nq = t // ta
    qi, kj = _tri_steps(nq, by_key=False)
    nsteps = len(qi)
    ride_in, ride_in_specs, ride_out, ride_out_specs, ride_sems = _ride_specs(ride)

    def body(qi_r, kj_r, q_r, qt_r, k_r, v_r, c_r, ct_r, o_r, lse_r, do_r, dot_r, *rest):
        nr = len(ride_in)
        src_r, (dq_o, dcq_o, dk_o, dv_o, dck_o), dst_o = rest[:nr], rest[nr:nr + 5], rest[nr + 5:2 * nr + 5]
        lse_sc, dl_sc, cq_sc, dq_sc, dcq_sc, dkt_sc, dvt_sc, dckt_sc, *sems = rest[2 * nr + 5:]
        p = pl.program_id(0)
        n = pl.program_id(1)
        i = qi_r[n]
        j = kj_r[n]
        lane = _iota((1, 128), 1)
        lo = lane < HD
        top = _iota((128, 1), 0) < HD

        if ride is not None:
            @pl.when((p == 0) & (n == 0))
            def _():
                _ride_start(ride, src_r, dst_o, *sems)

        @pl.when(n == 0)
        def _():
            dkt_sc[...] = jnp.zeros(dkt_sc.shape, F32)
            dvt_sc[...] = jnp.zeros(dvt_sc.shape, F32)
            dckt_sc[...] = jnp.zeros(dckt_sc.shape, F32)

        @pl.when(j == 0)
        def _():
            dq_sc[...] = jnp.zeros(dq_sc.shape, F32)
            dcq_sc[...] = jnp.zeros(dcq_sc.shape, F32)
            dd = do_r[...] * o_r[...].astype(F32)
            lse = lse_r[...]
            for e in (0, 1):
                sel = lo if e == 0 else jnp.logical_not(lo)
                cq_sc[e] = jnp.broadcast_to(_head_col(c_r[...], lane, 2 * p + e), (ta, 128))
                dl_sc[e] = jnp.broadcast_to(jnp.sum(jnp.where(sel, dd, 0.0), axis=1, keepdims=True), (ta, 128))
                lse_sc[e] = jnp.broadcast_to(lse[:, HD * e:HD * e + 1], (ta, 128))

        def step(masked):
            q = q_r[...]
            qt = qt_r[...]
            k = k_r[...]
            v = v_r[...]
            dob = do_r[...].astype(CDT)
            dot_ = dot_r[...]
            ones = jnp.ones((ta, 128), CDT)
            ones16 = jnp.ones((16, ta), CDT)
            if masked:
                rows = i * ta + _iota((ta, 1), 0)
                cols = j * ta + _iota((1, ta), 1)
                mask = (cols <= rows) & (cols >= PAD)
            sub = _iota((8, 1), 0)
            heads = (0, 1)
            sels = [lo, jnp.logical_not(lo)]
            rsels = [top, jnp.logical_not(top)]
            s = [_dot_nt(jnp.where(sels[e], q, 0), k) for e in heads]
            dp = [_dot_nt(jnp.where(sels[e], dob, 0), v) for e in heads]
            ck = [_head_row(ct_r[...], sub, 2 * p + e) for e in heads]
            pb, dsb = [], []
            for e in heads:
                cq, lse_e, dl = cq_sc[e], lse_sc[e], dl_sc[e]
                prs, dss = [], []
                for ch in range(ta // 128):
                    sl = slice(128 * ch, 128 * (ch + 1))
                    sc = s[e][:, sl] + cq - ck[e][:, sl]
                    if masked:
                        sc = jnp.where(mask[:, sl], sc, NEG)
                    pr = jnp.exp(sc - lse_e)
                    prs.append(pr.astype(CDT))
                    dss.append((pr * (dp[e][:, sl] - dl)).astype(CDT))
                pb.append(jnp.concatenate(prs, axis=1))
                dsb.append(jnp.concatenate(dss, axis=1))
            dvt = [_dot(jnp.where(rsels[e], dot_, 0), pb[e]) for e in heads]
            dkc = [_dot(jnp.concatenate([jnp.where(rsels[e], qt, 0), ones16], axis=0), dsb[e]) for e in heads]
            dqa = [_dot(dsb[e], jnp.concatenate([jnp.where(sels[e], k, 0), ones], axis=1)) for e in heads]
            dvt_sc[j] += dvt[0] + dvt[1]
            dkt_sc[j] += dkc[0][0:128] + dkc[1][0:128]
            dckt_sc[j, 0:8, :] += jnp.where(sub == 0, dkc[0][128:136], jnp.where(sub == 1, dkc[1][128:136], 0.0))
            dq_sc[...] += dqa[0][:, :128] + dqa[1][:, :128]
            for e in heads:
                dcq_sc[e] += dqa[e][:, 128:]

        edge = (j == i) | (j == 0)

        @pl.when(edge)
        def _():
            step(True)

        @pl.when(jnp.logical_not(edge))
        def _():
            step(False)

        @pl.when(j == i)
        def _():
            dq_o[...] = dq_sc[...]
            dcq_o[...] = jnp.where(lo, dcq_sc[0], dcq_sc[1])

        @pl.when(n == nsteps - 1)
        def _():
            spread = (_iota((128, 128), 1) == _iota((128, 128), 0) // HD).astype(F32)
            for jb in range(nq):
                rs = slice(jb * ta, (jb + 1) * ta)
                dk_o[rs, :] = dkt_sc[jb].T
                dv_o[rs, :] = dvt_sc[jb].T
                dck_o[rs, :] = _dot_hi(spread, dckt_sc[jb]).T

        if ride is not None:
            @pl.when((p == NPAIR - 1) & (n == nsteps - 1))
            def _():
                _ride_wait(ride, src_r, dst_o, *sems)

    qblk = pl.BlockSpec((ta, 128), lambda p, n, qi_r, kj_r: (qi_r[n], p))
    qtblk = pl.BlockSpec((128, ta), lambda p, n, qi_r, kj_r: (p, qi_r[n]))
    kblk = pl.BlockSpec((ta, 128), lambda p, n, qi_r, kj_r: (kj_r[n], p))
    whole = pl.BlockSpec((t, 128), lambda p, n, qi_r, kj_r: (0, p))
    grid_spec = pltpu.PrefetchScalarGridSpec(
        num_scalar_prefetch=2, grid=(NPAIR, nsteps),
        in_specs=[qblk, qtblk, kblk, kblk,
                  pl.BlockSpec((ta, 128), lambda p, n, qi_r, kj_r: (qi_r[n], 0)),
                  pl.BlockSpec((8, ta), lambda p, n, qi_r, kj_r: (0, kj_r[n])),
                  qblk, qblk, qblk, qtblk] + ride_in_specs,
        out_specs=[qblk, qblk, whole, whole, whole] + ride_out_specs,
        scratch_shapes=[pltpu.VMEM((2, ta, 128), F32)] * 3 + [pltpu.VMEM((ta, 128), F32), pltpu.VMEM((2, ta, 128), F32)]
        + [pltpu.VMEM((nq, 128, ta), F32)] * 3 + ride_sems,
    )
    return pl.pallas_call(
        body, name=name, grid_spec=grid_spec,
        out_shape=[_sds((t, 512), F32)] * 5 + ride_out,
        compiler_params=_params(("arbitrary", "arbitrary")),
    )(jnp.asarray(qi), jnp.asarray(kj), qf, qft, kf, vf, c, ct, o, lse, do, dot, *ride_in)


def _bucket_table():
    r = np.arange(BLK)[:, None]
    c = np.arange(3 * BLK)[None, :]
    d = np.where(c < BLK, r + BLK - c, r - (c - BLK))
    n = np.maximum(d, 0)
    max_exact = N_BUCKETS // 2
    nf = np.maximum(n, 1).astype(np.float32)
    large = max_exact + (np.log(nf / max_exact) / math.log(BLK / max_exact) * (N_BUCKETS - max_exact)).astype(np.int32)
    large = np.minimum(large, N_BUCKETS - 1)
    b = np.where(n < max_exact, n, large)
    return np.where(c < 2 * BLK, b, N_BUCKETS - 1).astype(np.int32)


def _bias_fwd(table, name):
    bucket = jnp.asarray(_bucket_table())

    def body(tab_r, b_r, o_o):
        h = pl.program_id(0)
        b = b_r[...]
        acc = jnp.zeros(b.shape, F32)
        for k in range(N_BUCKETS):
            acc = jnp.where(b == k, tab_r[k, h], acc)
        o_o[...] = acc

    return pl.pallas_call(
        body, name=name, grid=(8,),
        in_specs=[pl.BlockSpec(memory_space=pltpu.SMEM), pl.BlockSpec((BLK, 3 * BLK), lambda h: (0, 0))],
        out_specs=pl.BlockSpec((None, BLK, 3 * BLK), lambda h: (h, 0, 0)),
        out_shape=_sds((8, BLK, 3 * BLK), F32),
        compiler_params=_params(("parallel",)),
    )(table, bucket)


def _bias_bwd(dbias, name):
    bucket = jnp.asarray(_bucket_table())

    def body(d_r, b_r, o_o):
        h = pl.program_id(0)
        b = b_r[...]
        d = d_r[...]
        lane = _iota((1, 128), 1)
        row = jnp.zeros((1, 128), F32)
        for k in range(N_BUCKETS):
            row = jnp.where(lane == k, jnp.sum(jnp.where(b == k, d, 0.0)), row)
        o_o[pl.ds(h, 1), :] = row

    return pl.pallas_call(
        body, name=name, grid=(8,),
        in_specs=[pl.BlockSpec((None, BLK, 3 * BLK), lambda h: (h, 0, 0)), pl.BlockSpec((BLK, 3 * BLK), lambda h: (0, 0))],
        out_specs=pl.BlockSpec((8, 128), lambda h: (0, 0)),
        out_shape=_sds((8, 128), F32),
        compiler_params=_params(("arbitrary",)),
    )(dbias, bucket)


def _swa_valid(i):
    r = _iota((BLK, 1), 0)
    c = _iota((1, 3 * BLK), 1)
    prev = (c < BLK) & (c > r) & (i >= 1) & ((i - 1) * BLK + c >= PAD)
    cc = c - BLK
    cur = (c >= BLK) & (c < 2 * BLK) & (cc <= r) & (i * BLK + cc >= PAD)
    cm = c - 2 * BLK
    meta = (c >= 2 * BLK) & (cm >= PAD) & (i * BLK + r - cm >= BLK)
    return prev | cur | meta


def _swa_kv_specs(ta):
    nb = ta // BLK
    return [pl.BlockSpec((None, BLK, 128), lambda p, i: (p // 2, jnp.maximum(i * nb - 1, 0), 0)),
            pl.BlockSpec((None, ta, 128), lambda p, i: (p // 2, i, 0)),
            pl.BlockSpec((None, BLK, 128), lambda p, i: (p // 2, 0, 0))]


def _swa_fwd(qs, kse, vse, bias, sinks, name, ride=None):
    t = qs.shape[0]
    ta = _row_tile(t)
    nb = ta // BLK
    grid = (NPAIR, t // ta)
    ride_in, ride_in_specs, ride_out, ride_out_specs, ride_sems = _ride_specs(ride)

    def body(sink_r, q_r, kp_r, kc_r, km_r, vp_r, vc_r, vm_r, b_r, o_o, lse_o):
        p = pl.program_id(0)
        i = pl.program_id(1)
        lo = _iota((1, 128), 1) < HD
        k4 = jnp.concatenate([kp_r[...], kc_r[...]], axis=0)
        v4 = jnp.concatenate([vp_r[...], vc_r[...]], axis=0)
        work = [(b, e) for b in range(nb) for e in (0, 1)]
        sinks = [sink_r[2 * p + e] for e in (0, 1)]
        v3 = [jnp.concatenate([v4[BLK * b:BLK * (b + 2)], vm_r[...]], axis=0) for b in range(nb)]
        s = {}
        for b in range(nb):
            q = q_r[BLK * b:BLK * (b + 1), :]
            k3 = jnp.concatenate([k4[BLK * b:BLK * (b + 2)], km_r[...]], axis=0)
            valid = _swa_valid(i * nb + b)
            for e in (0, 1):
                sel = lo if e == 0 else jnp.logical_not(lo)
                s[b, e] = jnp.where(valid, _dot_nt(jnp.where(sel, q, 0), k3) + b_r[e], NEG)
        mx = {w: jnp.maximum(jnp.max(s[w], axis=1, keepdims=True), sinks[w[1]]) for w in work}
        pe = {w: jnp.exp(s[w] - mx[w]) for w in work}
        den = {w: jnp.sum(pe[w], axis=1, keepdims=True) + jnp.exp(sinks[w[1]] - mx[w]) for w in work}
        out = {w: _dot(pe[w].astype(CDT), v3[w[0]]) / den[w] for w in work}
        for b in range(nb):
            rows = slice(BLK * b, BLK * (b + 1))
            o_o[rows, :] = jnp.where(lo, out[b, 0], out[b, 1]).astype(CDT)
            lse_o[rows, :] = jnp.where(lo, mx[b, 0] + jnp.log(den[b, 0]), mx[b, 1] + jnp.log(den[b, 1]))

    qblk = pl.BlockSpec((ta, 128), lambda p, i: (i, p))
    res = pl.pallas_call(
        _riding(body, 9, 2, ride, grid), name=name, grid=grid,
        in_specs=[pl.BlockSpec(memory_space=pltpu.SMEM), qblk] + _swa_kv_specs(ta) + _swa_kv_specs(ta)
        + [pl.BlockSpec((2, BLK, 3 * BLK), lambda p, i: (p, 0, 0))] + ride_in_specs,
        out_specs=[qblk, qblk] + ride_out_specs,
        out_shape=[_sds((t, 512), CDT), _sds((t, 512), F32)] + ride_out, scratch_shapes=ride_sems,
        compiler_params=_params(("arbitrary", "arbitrary") if ride else ("parallel", "parallel")),
    )(sinks, qs, kse, kse, kse, vse, vse, vse, bias, *ride_in)
    return (res[0], res[1], res[2:]) if ride else res


def _swa_bwd(qs, kse, vse, bias, sinks, o, lse, do, name):
    t = qs.shape[0]
    ta = _row_tile(t)
    nb = ta // BLK

    def body(sink_r, q_r, kp_r, kc_r, km_r, vp_r, vc_r, vm_r, b_r, o_r, lse_r, do_r,
             dq_o, dk_o, dv_o, db_o, dsk_o):
        p = pl.program_id(0)
        i = pl.program_id(1)
        lo = _iota((1, 128), 1) < HD

        @pl.when((i == 0) & (p % 2 == 0))
        def _():
            dk_o[...] = jnp.zeros(dk_o.shape, F32)
            dv_o[...] = jnp.zeros(dv_o.shape, F32)

        @pl.when(i == 0)
        def _():
            db_o[...] = jnp.zeros(db_o.shape, F32)
            dsk_o[...] = jnp.zeros(dsk_o.shape, F32)

        k4 = jnp.concatenate([kp_r[...], kc_r[...]], axis=0)
        v4 = jnp.concatenate([vp_r[...], vc_r[...]], axis=0)
        work = [(b, e) for b in range(nb) for e in (0, 1)]
        sel = [lo, jnp.logical_not(lo)]
        k3 = [jnp.concatenate([k4[BLK * b:BLK * (b + 2)], km_r[...]], axis=0) for b in range(nb)]
        v3 = [jnp.concatenate([v4[BLK * b:BLK * (b + 2)], vm_r[...]], axis=0) for b in range(nb)]
        q = [q_r[BLK * b:BLK * (b + 1), :] for b in range(nb)]
        do_ = [do_r[BLK * b:BLK * (b + 1), :] for b in range(nb)]
        lse = [lse_r[BLK * b:BLK * (b + 1), :] for b in range(nb)]
        dd = [do_[b] * o_r[BLK * b:BLK * (b + 1), :].astype(F32) for b in range(nb)]
        valid = [_swa_valid(i * nb + b) for b in range(nb)]
        qe = {(b, e): jnp.where(sel[e], q[b], 0) for b, e in work}
        doe = {(b, e): jnp.where(sel[e], do_[b], 0.0).astype(CDT) for b, e in work}
        lse_e = {(b, e): lse[b][:, HD * e:HD * e + 1] for b, e in work}
        delta = {(b, e): jnp.sum(jnp.where(sel[e], dd[b], 0.0), axis=1, keepdims=True) for b, e in work}
        s = {(b, e): jnp.where(valid[b], _dot_nt(qe[b, e], k3[b]) + b_r[e], NEG) for b, e in work}
        dp = {(b, e): _dot_nt(doe[b, e], v3[b]) for b, e in work}
        pr = {w: jnp.exp(s[w] - lse_e[w]) for w in work}
        ds = {w: pr[w] * (dp[w] - delta[w]) for w in work}
        dqs = {(b, e): _dot(ds[b, e].astype(CDT), jnp.where(sel[e], k3[b], 0)) for b, e in work}
        both = lambda x, b: jnp.concatenate([x[b, 0], x[b, 1]], axis=0)
        dk3 = [_dot(both(ds, b).T.astype(CDT), both(qe, b)) for b in range(nb)]
        dv3 = [_dot(both(pr, b).T.astype(CDT), both(doe, b)) for b in range(nb)]
        for e in (0, 1):
            tot = ds[0, e]
            for b in range(1, nb):
                tot = tot + ds[b, e]
            db_o[e] += tot
        dsink = [sum(-jnp.sum(jnp.exp(sink_r[2 * p + e] - lse_e[b, e]) * delta[b, e], axis=0, keepdims=True)
                     for b in range(nb)) for e in (0, 1)]
        dsk_o[0:1, :] += jnp.where(lo, dsink[0], dsink[1])
        for b in range(nb):
            ib = i * nb + b
            dq_o[BLK * b:BLK * (b + 1), :] = dqs[b, 0] + dqs[b, 1]
            dk = dk3[b]
            dv = dv3[b]
            prev = pl.ds(pl.multiple_of(jnp.maximum(ib - 1, 0) * BLK, BLK), BLK)
            cur = pl.ds(pl.multiple_of(ib * BLK, BLK), BLK)
            dk_o[prev, :] += dk[0:BLK]
            dk_o[cur, :] += dk[BLK:2 * BLK]
            dk_o[0:BLK, :] += dk[2 * BLK:]
            dv_o[prev, :] += dv[0:BLK]
            dv_o[cur, :] += dv[BLK:2 * BLK]
            dv_o[0:BLK, :] += dv[2 * BLK:]

    qblk = pl.BlockSpec((ta, 128), lambda p, i: (i, p))
    kvacc = pl.BlockSpec((None, t, 128), lambda p, i: (p // 2, 0, 0))
    bblk = pl.BlockSpec((2, BLK, 3 * BLK), lambda p, i: (p, 0, 0))
    return pl.pallas_call(
        body, name=name, grid=(NPAIR, t // ta),
        in_specs=[pl.BlockSpec(memory_space=pltpu.SMEM), qblk] + _swa_kv_specs(ta) + _swa_kv_specs(ta)
        + [bblk, qblk, qblk, qblk],
        out_specs=[qblk, kvacc, kvacc, bblk, pl.BlockSpec((None, 8, 128), lambda p, i: (p, 0, 0))],
        out_shape=[_sds((t, 512), F32), _sds((2, t, 128), F32), _sds((2, t, 128), F32),
                   _sds((8, BLK, 3 * BLK), F32), _sds((NPAIR, 8, 128), F32)],
        compiler_params=_params(("arbitrary", "arbitrary")),
    )(sinks, qs, kse, kse, kse, vse, vse, vse, bias, o, lse, do)


def _sum8(slots, name):
    def body(a_r, o_o):
        acc = a_r[0]
        for k in range(1, 8):
            acc = acc + a_r[k]
        o_o[...] = acc

    return pl.pallas_call(
        body, name=name, out_shape=_sds((SMALL_ROWS, 128), F32),
        in_specs=[pl.BlockSpec(memory_space=pltpu.VMEM)], out_specs=pl.BlockSpec(memory_space=pltpu.VMEM),
        compiler_params=_params(),
    )(slots)


def _place():
    x, y, c = lax.axis_index("x"), lax.axis_index("y"), lax.axis_index("c")
    chips = [(1 - x, y), (x, 1 - y), (1 - x, 1 - y)]
    return x, y, c, chips


def _remote(src, dst, send_sems, recv_sems, k, to):
    return pltpu.make_async_remote_copy(src_ref=src, dst_ref=dst, send_sem=send_sems.at[k], recv_sem=recv_sems.at[k],
                                        device_id=to, device_id_type=MESH_ID)


ANY = pl.BlockSpec(memory_space=pl.ANY)


def _mix_cols(w):
    return jnp.concatenate([w[:, 2312:4360], w[:, 0:1536], w[:, 1544:2312], w[:, 1536:1544],
                            jnp.zeros((w.shape[0], DP - D_IN), w.dtype)], axis=1)


def _unmix_cols(w):
    return jnp.concatenate([w[:, QA:QA + 1536], w[:, FA:FA + 8], w[:, QB:QB + 768], w[:, GA:GA + 2048]], axis=1)


def _rows128(a, rows):
    flat = a.reshape(-1)
    return jnp.pad(flat, (0, rows * 128 - flat.shape[0])).reshape(rows, 128)


GRAD_FORM = {"ffn1_w_in": "col", "ffn2_w_in": "col", "w_branch_fox": "col", "w_branch_swa": "col",
             "ffn1_w_out": "3d", "ffn2_w_out": "3d", "w_out": "3d", "w_in": "3d"}
SUM_TILE = {1024: 128, 704: 176, 512: 128, 256: 128}
NT = len(SHARD_ITEMS)
ALL_ITEMS = tuple(range(NT))


def _half_rows(c, r):
    return pl.ds(pl.multiple_of(c * (r // 2), 16), r // 2)


def _ici_copies(kind, srcs, dsts, send_sems, recv_sems, layer, recv=True, items=ALL_ITEMS):
    x, y, c, chips = _place()
    s = 2 * x + y
    sends, recvs = [], []
    for t, (item, src, dst) in enumerate(zip(items, srcs, dsts)):
        nm, (r, cc), _ = SHARD_ITEMS[item]
        for j, (cx, cy) in enumerate(chips):
            sj = 2 * cx + cy
            k = 3 * t + j
            to = (cx, cy, c)
            if kind == "gather":
                hs = _half_rows(c, r)
                sends.append(_remote(src.at[layer, hs], dst.at[s, hs], send_sems, recv_sems, k, to))
                if recv:
                    recvs.append(_remote(src.at[layer, hs], dst.at[sj, hs], send_sems, recv_sems, k, to))
            else:
                if GRAD_FORM[nm] == "col":
                    piece = src.at[:, pl.ds(pl.multiple_of(sj * cc, 128), cc)]
                else:
                    piece = src.at[sj]
                sends.append(_remote(piece, dst.at[j], send_sems, recv_sems, k, to))
                recvs.append(sends[-1])
    return sends, recvs


def _slab_shapes(items=ALL_ITEMS):
    return [_sds((4, *SHARD_ITEMS[t][1]), CDT) for t in items]


def _dma_sems(n):
    return [pltpu.SemaphoreType.DMA((n,)), pltpu.SemaphoreType.DMA((n,))]


def _forward_sends(dsts, send_sems, recv_sems, items=ALL_ITEMS):
    x, y, c, chips = _place()
    sends, recvs = [], []
    for t, (item, dst) in enumerate(zip(items, dsts)):
        r = SHARD_ITEMS[item][1][0]
        for j, (cx, cy) in enumerate(chips):
            sj = 2 * cx + cy
            hs, ho = _half_rows(c, r), _half_rows(1 - c, r)
            sends.append(_remote(dst.at[sj, hs], dst.at[sj, hs], send_sems, recv_sems, 3 * t + j, (x, y, 1 - c)))
            recvs.append(_remote(dst.at[sj, ho], dst.at[sj, ho], send_sems, recv_sems, 3 * t + j, (x, y, 1 - c)))
    return sends, recvs


def _gather_layer(wb, mflat, layer, name, items):
    nt = len(items)

    def body(*refs):
        srcs, m_r, dsts, mall_o = refs[:nt], refs[nt], refs[nt + 1:2 * nt + 1], refs[2 * nt + 1]
        send_sems, recv_sems, fsend, frecv, msend, mrecv = refs[2 * nt + 2:]
        x, y, c, chips = _place()
        s = 2 * x + y
        sends, recvs = _ici_copies("gather", srcs, dsts, send_sems, recv_sems, layer, items=items)
        metas = [_remote(m_r, mall_o.at[s], msend, mrecv, j, (cx, cy, c)) for j, (cx, cy) in enumerate(chips)]
        for cp in sends + metas:
            cp.start()
        fwds, frecvs = _forward_sends(dsts, fsend, frecv, items)
        for got, fwd in zip(recvs, fwds):
            got.wait_recv()
            fwd.start()
        for got in frecvs:
            got.wait_recv()
        for j, (cx, cy) in enumerate(chips):
            _remote(m_r, mall_o.at[2 * cx + cy], msend, mrecv, j, (cx, cy, c)).wait_recv()
        for cp in sends + metas + fwds:
            cp.wait_send()

    return pl.pallas_call(
        body, name=name, out_shape=_slab_shapes(items) + [_sds((4, META_ROWS, 128), F32)],
        in_specs=[ANY] * (nt + 1), out_specs=[ANY] * (nt + 1),
        scratch_shapes=_dma_sems(3 * nt) + _dma_sems(3 * nt) + _dma_sems(3),
    )(*wb, mflat)


def _forward_layer(slabs, name, items=ALL_ITEMS):
    nt = len(items)

    def body(*refs):
        ins, outs, send_sems, recv_sems = refs[:nt], refs[nt:2 * nt], refs[2 * nt], refs[2 * nt + 1]
        sends, recvs = _forward_sends(outs, send_sems, recv_sems, items)
        for cp in sends:
            cp.start()
        for cp in recvs:
            cp.wait_recv()
        for cp in sends:
            cp.wait_send()

    return pl.pallas_call(
        body, name=name, out_shape=_slab_shapes(items), in_specs=[ANY] * nt, out_specs=[ANY] * nt,
        input_output_aliases={t: t for t in range(nt)}, scratch_shapes=_dma_sems(3 * nt),
    )(*slabs)


def _half_shape(nm, r, c):
    return (r // 2, 4 * c) if GRAD_FORM[nm] == "col" else (4, r // 2, c)


def _swap_layer(gs, gsm, name, items=ALL_ITEMS):
    small = gsm is not None
    nt = len(items)

    def body(*refs):
        g_rs = refs[:nt]
        pos = nt
        if small:
            s_r = refs[pos]
            pos += 1
        got_os = refs[pos:pos + nt]
        pos += nt
        if small:
            slots_o = refs[pos]
            pos += 1
        send_sems, recv_sems = refs[pos], refs[pos + 1]
        x, y, c, _ = _place()
        sib = (x, y, 1 - c)
        sent = []
        for t, (item, g_r, got_o) in enumerate(zip(items, g_rs, got_os)):
            nm, (r, cc), _ = SHARD_ITEMS[item]
            ho = _half_rows(1 - c, r)
            src = g_r.at[ho, :] if GRAD_FORM[nm] == "col" else g_r.at[:, ho, :]
            sent.append(_remote(src, got_o, send_sems, recv_sems, t, sib))
        if small:
            ssend, srecv, loc_sem = refs[pos + 2], refs[pos + 3], refs[pos + 4]
            me = 4 * x + 2 * y + c
            loc = pltpu.make_async_copy(s_r, slots_o.at[me], loc_sem.at[0])
            loc.start()
            peers = [(x ^ (k >> 2), y ^ ((k >> 1) & 1), c ^ (k & 1)) for k in range(1, 8)]
            for k, peer in enumerate(peers):
                sent.append(_remote(s_r, slots_o.at[me], ssend, srecv, k, peer))
        for cp in sent:
            cp.start()
        for cp in sent[:nt]:
            cp.wait_recv()
        if small:
            for k, (px, py, pc) in enumerate(peers):
                _remote(s_r, slots_o.at[4 * px + 2 * py + pc], ssend, srecv, k, (px, py, pc)).wait_recv()
        for cp in sent:
            cp.wait_send()
        if small:
            loc.wait()

    outs = [_sds(_half_shape(*SHARD_ITEMS[item][0:1], *SHARD_ITEMS[item][1]), CDT) for item in items]
    ops = list(gs)
    sems = _dma_sems(nt)
    if small:
        outs.append(_sds((8, SMALL_ROWS, 128), F32))
        ops.append(gsm)
        sems = sems + _dma_sems(7) + [pltpu.SemaphoreType.DMA((1,))]
    res = pl.pallas_call(
        body, name=name, out_shape=outs, in_specs=[ANY] * len(ops), out_specs=[ANY] * len(outs), scratch_shapes=sems,
    )(*ops)
    return (res[:nt], res[nt]) if small else (res, None)


def _pair_add_t(own, got, half_idx, nm, r, name):
    tr = SUM_TILE[r]
    nb = (r // 2) // tr
    if GRAD_FORM[nm] == "col":
        blk = (tr, own.shape[1])
        own_spec = pl.BlockSpec(blk, lambda i, c_r: (c_r[0] * nb + i, 0))
        half_spec = pl.BlockSpec(blk, lambda i, c_r: (i, 0))
    else:
        blk = (4, tr, own.shape[2])
        own_spec = pl.BlockSpec(blk, lambda i, c_r: (0, c_r[0] * nb + i, 0))
        half_spec = pl.BlockSpec(blk, lambda i, c_r: (0, i, 0))

    def body(c_r, a_r, b_r, o_o):
        o_o[...] = (a_r[...].astype(F32) + b_r[...].astype(F32)).astype(CDT)

    grid_spec = pltpu.PrefetchScalarGridSpec(num_scalar_prefetch=1, grid=(nb,), in_specs=[own_spec, half_spec],
                                             out_specs=half_spec)
    return pl.pallas_call(body, name=name, grid_spec=grid_spec, out_shape=_sds(got.shape, CDT),
                          compiler_params=_params(("parallel",)))(half_idx, own, got)


def _sum4_t(ps, got3, buf, idx, layer, nm, r, name):
    tr = SUM_TILE[r]
    nb = (r // 2) // tr
    c = got3.shape[2]
    if GRAD_FORM[nm] == "col":
        ps_spec = pl.BlockSpec((tr, c), lambda i, x_r: (i, x_r[0]))
    else:
        ps_spec = pl.BlockSpec((None, tr, c), lambda i, x_r: (x_r[0], i, 0))

    def body(x_r, a_r, b_r, buf_r, o_o):
        o_o[...] = ((a_r[...].astype(F32) + b_r[0].astype(F32)) + b_r[1].astype(F32)) + b_r[2].astype(F32)

    grid_spec = pltpu.PrefetchScalarGridSpec(
        num_scalar_prefetch=1, grid=(nb,),
        in_specs=[ps_spec, pl.BlockSpec((3, tr, c), lambda i, x_r: (0, i, 0)), ANY],
        out_specs=pl.BlockSpec((None, tr, c), lambda i, x_r: (layer, x_r[1] * nb + i, 0)),
    )
    return pl.pallas_call(body, name=name, grid_spec=grid_spec, out_shape=_sds(buf.shape, F32),
                          input_output_aliases={3: 0}, compiler_params=_params(("parallel",)))(idx, ps, got3, buf)


def _scatter_layer(ps, name, items=ALL_ITEMS):
    nt = len(items)

    def body(*refs):
        srcs, dsts, send_sems, recv_sems = refs[:nt], refs[nt:2 * nt], refs[2 * nt], refs[2 * nt + 1]
        sends, recvs = _ici_copies("scatter", srcs, dsts, send_sems, recv_sems, None, items=items)
        for cp in sends:
            cp.start()
        for cp in recvs:
            cp.wait_recv()
        for cp in sends:
            cp.wait_send()

    return pl.pallas_call(
        body, name=name, out_shape=_got3_shapes(items), in_specs=[ANY] * nt, out_specs=[ANY] * nt,
        scratch_shapes=_dma_sems(3 * nt),
    )(*ps)


def _got3_shapes(items=ALL_ITEMS):
    return [_sds((3, SHARD_ITEMS[t][1][0] // 2, SHARD_ITEMS[t][1][1]), CDT) for t in items]


def _join_layer(bufs, name):
    def body(*refs):
        ins, outs, send_sems, recv_sems = refs[:NT], refs[NT:2 * NT], refs[2 * NT], refs[2 * NT + 1]
        x, y, c, _ = _place()
        sent = []
        for t, ((nm, (r, cc), _), b_o) in enumerate(zip(SHARD_ITEMS, outs)):
            hs = _half_rows(c, r)
            sent.append(_remote(b_o.at[:, hs, :], b_o.at[:, hs, :], send_sems, recv_sems, t, (x, y, 1 - c)))
        for cp in sent:
            cp.start()
        for t, ((nm, (r, cc), _), b_o) in enumerate(zip(SHARD_ITEMS, outs)):
            ho = _half_rows(1 - c, r)
            _remote(b_o.at[:, ho, :], b_o.at[:, ho, :], send_sems, recv_sems, t, (x, y, 1 - c)).wait_recv()
        for cp in sent:
            cp.wait_send()

    return pl.pallas_call(
        body, name=name, out_shape=[_sds(b.shape, F32) for b in bufs], in_specs=[ANY] * NT, out_specs=[ANY] * NT,
        input_output_aliases={t: t for t in range(NT)}, scratch_shapes=_dma_sems(NT),
    )(*bufs)


def _adamw3(w, g, m, v, name):
    nl, r, c = w.shape
    tr = SUM_TILE.get(r, r)
    if r % 8:
        blk = pl.BlockSpec((None, r, 256), lambda l, i: (l, 0, i))
        steps = c // 256
    else:
        blk = pl.BlockSpec((None, tr, c), lambda l, i: (l, i, 0))
        steps = r // tr

    def body(w_r, g_r, m_r, v_r, d_o, m_o, v_o):
        g_ = g_r[...]
        m_ = ADAM_B1 * m_r[...] + (1.0 - ADAM_B1) * g_
        v_ = ADAM_B2 * v_r[...] + (1.0 - ADAM_B2) * jnp.square(g_)
        m_hat = m_ / (1.0 - ADAM_B1 ** ADAM_STEP)
        v_hat = v_ / (1.0 - ADAM_B2 ** ADAM_STEP)
        d_o[...] = -ADAM_LR * (m_hat / (jnp.sqrt(v_hat) + ADAM_EPS) + ADAM_WD * w_r[...])
        m_o[...] = m_
        v_o[...] = v_

    return pl.pallas_call(
        body, name=name, grid=(nl, steps),
        in_specs=[blk] * 4, out_specs=[blk] * 3, out_shape=[_sds((nl, r, c), F32)] * 3,
        compiler_params=_params(("parallel", "parallel")),
    )(w, g, m, v)


def _full_weights(slabs, wb, layer, shard, items=ALL_ITEMS):
    ws = {}
    for t, slab in zip(items, slabs):
        nm, (r, c), kind = SHARD_ITEMS[t]
        slab = lax.dynamic_update_slice(slab, wb[nm][layer][None], (shard, 0, 0))
        ws[nm] = slab.reshape(4 * r, c) if kind == "row" else jnp.concatenate([slab[s] for s in range(4)], axis=1)
    return ws


def _exchange_forms(g, items=ALL_ITEMS):
    out = []
    for t in items:
        nm, (r, c), _ = SHARD_ITEMS[t]
        a = g[nm]
        if nm == "w_in":
            a = a.reshape(D, 4, c).transpose(1, 0, 2)
        elif GRAD_FORM[nm] == "3d":
            a = a.reshape(4, r, c)
        out.append(a)
    return out


SMALL_ITEMS = (("rel_bias_table", 2), ("ffn1_norm", 16), ("mix_norm", 16), ("ffn2_norm", 16), ("forget_bias", 1),
               ("fox_q_norm", 1), ("fox_k_norm", 1), ("swa_q_norm", 1), ("swa_k_norm", 1), ("swa_sinks", 1))
SMALL_ADAM_ROWS = 96


def _layer_fwd(h, lw, l, ride=None, late=None):
    rides = late["rides"] if late else {}

    def run(key, fn, *args):
        r = rides.get(key)
        if r is None:
            return fn(*args)
        out = fn(*args, ride=r)
        late["arrived"](key, out[-1])
        return out[0] if len(out) == 2 else out[:-1]

    sv = {"h0": h}
    a, sv["a1t"] = _rms_fwd(h, lw["ffn1_norm"], f"rms_fwd_a{l}")
    sv["gu1"], s, sv["s1t"] = run("ffn_in_a", _ffn_in, a, lw["ffn1_w_in"], f"ffn_in_a{l}")
    h = run("ffn_out_a", _mm_res, s, lw["ffn1_w_out"], h, 0.5, f"ffn_out_a{l}")
    sv["h1"] = h
    a, sv["amt"] = _rms_fwd(h, lw["mix_norm"], f"rms_fwd_m{l}")
    if late:
        late["need"](lw, "mixer")
    proj = run("proj", _mm, a, lw["w_mix"], F32, _row_tile(h.shape[0]), DP, f"proj{l}")
    sv["proj"] = proj
    qf, kf, vf, qs, kse, vse, c, ct, sv["qft"] = _qknorm_fwd(proj, lw["gfq"], lw["gfk"], lw["gsq"], lw["gsk"], lw["fb"],
                                                              f"qknorm_fwd{l}")
    ofox, lse_f, *rode = _fox_fwd(qf, kf, vf, c, ct, f"fox_fwd{l}", ride)
    oswa, lse_s = run("swa_fwd", _swa_fwd, qs, kse, vse, lw["bias"], lw["sinks"], f"swa_fwd{l}")
    if late:
        late["need"](lw, "gate")
    sv.update(qf=qf, kf=kf, vf=vf, qs=qs, kse=kse, vse=vse, c=c, ct=ct, ofox=ofox, oswa=oswa, lse_f=lse_f, lse_s=lse_s)
    h, sv["yt"], sv["pf"], sv["ps"], sv["oft"], sv["ost"] = _gate_out_fwd(
        ofox, oswa, lw["w_branch_fox"], lw["w_branch_swa"], proj, lw["w_out"], h, f"gate_out_fwd{l}")
    sv["h2"] = h
    a, sv["a2t"] = _rms_fwd(h, lw["ffn2_norm"], f"rms_fwd_b{l}")
    sv["gu2"], s, sv["s2t"] = _ffn_in(a, lw["ffn2_w_in"], f"ffn_in_b{l}")
    h = _mm_res(s, lw["ffn2_w_out"], h, 0.5, f"ffn_out_b{l}")
    return h, sv, rode


def _ffn_bwd(dh, dhb, h_in, at, gu, st, norm, w_in, w_out, tag, rides=None):
    r = rides or (None,) * 4
    rode = []

    def split(res, ride):
        if ride is None:
            return res
        rode.extend(res[-1])
        return res[0] if len(res) == 2 else res[:-1]

    dgu = split(_ffn_bwd_mid(dhb, w_out, gu, f"ffn_bwd_mid_{tag}", r[0]), r[0])
    d_w_out = split(_mm(st, dhb, CDT, 256, D, f"dw_ffn_out_{tag}", scale=0.5, ride=r[1]), r[1])
    dh, dhb, dg = split(_ffn_bwd_in(dgu, w_in, h_in, norm, dh, f"ffn_bwd_in_{tag}", r[2]), r[2])
    d_w_in = split(_mm(at, dgu, CDT, D, 256, f"dw_ffn_in_{tag}", ride=r[3]), r[3])
    return dh, dhb, d_w_out, d_w_in, dg, rode


def _layer_bwd(dh, dhb, sv, lw, l, ride=None, before_ffn1=None):
    g = {}
    dh, dhb, g["ffn2_w_out"], g["ffn2_w_in"], g["ffn2_norm"], _ = _ffn_bwd(
        dh, dhb, sv["h2"], sv["a2t"], sv["gu2"], sv["s2t"], lw["ffn2_norm"], lw["ffn2_w_in"], lw["ffn2_w_out"], f"b{l}")
    g["w_out"] = _mm(sv["yt"], dhb, CDT, 512, 512, f"dw_out{l}")
    dpf, dps, dga, dgb, do_f, do_ft, do_s = _gate_out_bwd(dhb, lw["w_out"], sv["pf"], sv["ps"], sv["proj"],
                                                           lw["w_branch_fox"], lw["w_branch_swa"], f"gate_out_bwd{l}")
    g["w_branch_fox"] = _mm(sv["oft"], dpf, CDT, 512, 512, f"dw_bfox{l}")
    g["w_branch_swa"] = _mm(sv["ost"], dps, CDT, 512, 512, f"dw_bswa{l}")
    dqf, dcq, dkf, dvf, dck, *rode = _fox_bwd(sv["qf"], sv["qft"], sv["kf"], sv["vf"], sv["c"], sv["ct"], sv["ofox"],
                                              sv["lse_f"], do_f, do_ft, f"fox_bwd{l}", ride)
    g["rode"] = rode
    dqs, dkse, dvse, dbias, dsk = _swa_bwd(sv["qs"], sv["kse"], sv["vse"], lw["bias"], lw["sinks"], sv["oswa"],
                                           sv["lse_s"], do_s, f"swa_bwd{l}")
    dproj, dgn = _qknorm_bwd(sv["proj"], dqf, dkf, dvf, dqs, dkse, dvse, dcq, dck, dga, dgb,
                             lw["gfq"], lw["gfk"], lw["gsq"], lw["gsk"], lw["fb"], f"qknorm_bwd{l}")
    g["w_mix"] = _mm(sv["amt"], dproj, CDT, D, 640, f"dw_mix{l}")
    dh, dhb, g["mix_norm"] = _mm_nt_rms(dproj, lw["w_mix"], sv["h1"], lw["mix_norm"], dh, f"d_am{l}")
    g["dbias"], g["dsk"], g["dgn"] = dbias, dsk, dgn
    rides = before_ffn1(g) if before_ffn1 else None
    dh, dhb, g["ffn1_w_out"], g["ffn1_w_in"], g["ffn1_norm"], g["rode_ffn1"] = _ffn_bwd(
        dh, dhb, sv["h0"], sv["a1t"], sv["gu1"], sv["s1t"], lw["ffn1_norm"], lw["ffn1_w_in"], lw["ffn1_w_out"], f"a{l}",
        rides)
    return dh, dhb, g


def kernel(x, meta_tokens, rel_bias_table, ffn1_norm, ffn1_w_in, ffn1_w_out, mix_norm, w_in, forget_bias, fox_q_norm, fox_k_norm, swa_q_norm, swa_k_norm, swa_sinks, w_branch_fox, w_branch_swa, w_out, ffn2_norm, ffn2_w_in, ffn2_w_out, loss_target, m_meta_tokens, m_rel_bias_table, m_ffn1_norm, m_ffn1_w_in, m_ffn1_w_out, m_mix_norm, m_w_in, m_forget_bias, m_fox_q_norm, m_fox_k_norm, m_swa_q_norm, m_swa_k_norm, m_swa_sinks, m_w_branch_fox, m_w_branch_swa, m_w_out, m_ffn2_norm, m_ffn2_w_in, m_ffn2_w_out, v_meta_tokens, v_rel_bias_table, v_ffn1_norm, v_ffn1_w_in, v_ffn1_w_out, v_mix_norm, v_w_in, v_forget_bias, v_fox_q_norm, v_fox_k_norm, v_swa_q_norm, v_swa_k_norm, v_swa_sinks, v_w_branch_fox, v_w_branch_swa, v_w_out, v_ffn2_norm, v_ffn2_w_in, v_ffn2_w_out):
    names = ["meta_tokens", "rel_bias_table", "ffn1_norm", "ffn1_w_in", "ffn1_w_out", "mix_norm", "w_in", "forget_bias",
             "fox_q_norm", "fox_k_norm", "swa_q_norm", "swa_k_norm", "swa_sinks", "w_branch_fox", "w_branch_swa", "w_out",
             "ffn2_norm", "ffn2_w_in", "ffn2_w_out"]
    w = dict(zip(names, [meta_tokens, rel_bias_table, ffn1_norm, ffn1_w_in, ffn1_w_out, mix_norm, w_in, forget_bias,
                         fox_q_norm, fox_k_norm, swa_q_norm, swa_k_norm, swa_sinks, w_branch_fox, w_branch_swa, w_out,
                         ffn2_norm, ffn2_w_in, ffn2_w_out]))
    m = dict(zip(names, [m_meta_tokens, m_rel_bias_table, m_ffn1_norm, m_ffn1_w_in, m_ffn1_w_out, m_mix_norm, m_w_in,
                         m_forget_bias, m_fox_q_norm, m_fox_k_norm, m_swa_q_norm, m_swa_k_norm, m_swa_sinks,
                         m_w_branch_fox, m_w_branch_swa, m_w_out, m_ffn2_norm, m_ffn2_w_in, m_ffn2_w_out]))
    v = dict(zip(names, [v_meta_tokens, v_rel_bias_table, v_ffn1_norm, v_ffn1_w_in, v_ffn1_w_out, v_mix_norm, v_w_in,
                         v_forget_bias, v_fox_q_norm, v_fox_k_norm, v_swa_q_norm, v_swa_k_norm, v_swa_sinks,
                         v_w_branch_fox, v_w_branch_swa, v_w_out, v_ffn2_norm, v_ffn2_w_in, v_ffn2_w_out]))
    xi, yi, ci = lax.axis_index("x"), lax.axis_index("y"), lax.axis_index("c")
    shard = 2 * xi + yi
    seq = x.shape[1]
    t = seq + BLK

    wb = {nm: w[nm].astype(CDT) for nm, _, _ in SHARD_ITEMS}
    wb_list = [wb[nm] for nm, _, _ in SHARD_ITEMS]
    mflat = meta_tokens.reshape(META_ROWS, 128)
    first = (0, 1)
    *slabs_first, mall = _gather_layer([wb_list[t] for t in first], mflat, 0, "gather_weights", first)
    mall = lax.dynamic_update_slice(mall, mflat[None], (shard, 0, 0))
    meta_full = jnp.concatenate([mall[s].reshape(N_META, 256) for s in range(4)], axis=1)
    bias = _bias_fwd(rel_bias_table, "bias_fwd")

    def layer_weights(slabs, l, items=ALL_ITEMS):
        lw = _full_weights(slabs, wb, l, shard, items)
        if "w_in" in lw:
            lw["w_mix"] = _mix_cols(lw.pop("w_in"))
        return lw

    def layer_vectors(l):
        lw = {nm: w[nm][l].reshape(1, D) for nm in ("ffn1_norm", "mix_norm", "ffn2_norm")}
        lw["gfq"] = jnp.tile(fox_q_norm[l], 8).reshape(1, 512)
        lw["gfk"] = jnp.tile(fox_k_norm[l], 8).reshape(1, 512)
        lw["gsq"] = jnp.tile(swa_q_norm[l], 8).reshape(1, 512)
        lw["gsk"] = jnp.tile(swa_k_norm[l], 2).reshape(1, 128)
        lw["fb"] = jnp.pad(forget_bias[l], (0, 120)).reshape(1, 128)
        lw["sinks"] = swa_sinks[l]
        lw["bias"] = bias
        return lw

    def gather_ride(layer, items):
        return ("gather", [wb_list[t] for t in items], _slab_shapes(items), layer, items)

    landed = {}

    def need(lw, stage):
        if stage == "mixer":
            items = (2,)
            slabs = _forward_layer(landed["ffn_in_a"], "forward_halves0m", items)
        else:
            items = (3, 4, 5, 6, 7)
            slabs = _forward_layer(landed["ffn_out_a"] + landed["proj"] + landed["swa_fwd"], "forward_halves0g", items)
        lw.update(layer_weights(slabs, 0, items))

    late = {"rides": {"ffn_in_a": gather_ride(0, (2,)), "ffn_out_a": gather_ride(0, (3, 4, 5)),
                      "proj": gather_ride(0, (6,)), "swa_fwd": gather_ride(0, (7,))},
            "arrived": landed.__setitem__, "need": need}

    h = jnp.concatenate([jnp.zeros((PAD, D), F32), meta_full, x[0]], axis=0)
    lws = [{**layer_vectors(0), **layer_weights(slabs_first, 0, first)}]
    h, sv0, slabs1 = _layer_fwd(h, lws[0], 0, gather_ride(1, ALL_ITEMS), late)
    lws.append({**layer_vectors(1), **layer_weights(_forward_layer(slabs1, "forward_halves"), 1)})
    h, sv1, _ = _layer_fwd(h, lws[1], 1)
    saved = [sv0, sv1]
    dh, dhb, lacc = _loss(h, loss_target[0], "loss")
    loss = lax.psum(lacc[0, 0], ("x", "y", "c"))

    half_idx = ci.reshape(1).astype(jnp.int32)
    place_idx = jnp.stack([shard, ci]).astype(jnp.int32)

    def pair_sums(g, gsm, tag, items=ALL_ITEMS):
        if "w_mix" in g:
            g["w_in"] = _unmix_cols(g.pop("w_mix"))
        forms = _exchange_forms(g, items)
        got, slots = _swap_layer(forms, gsm, f"swap_halves{tag}", items)
        return {t: _pair_add_t(a, b, half_idx, SHARD_ITEMS[t][0], SHARD_ITEMS[t][1][0],
                               f"pair_add{tag}_{SHARD_ITEMS[t][0]}")
                for t, a, b in zip(items, forms, got)}, slots

    def scatter_ride(ps, items):
        return ("scatter", [ps[t] for t in items], _got3_shapes(items), None, items)

    early = (2, 3, 4, 5, 6, 7)
    early_rides = ((6,), (7,), (2, 5), (3, 4))
    ps0 = {}

    def before_ffn1(g):
        ps0.update(pair_sums(g, None, "0e", early)[0])
        return [scatter_ride(ps0, items) for items in early_rides]

    grads = [None, None]
    dh, dhb, grads[1] = _layer_bwd(dh, dhb, saved[1], lws[1], 1)
    ps1, _ = pair_sums(grads[1], None, 1)
    dh, dhb, grads[0] = _layer_bwd(dh, dhb, saved[0], lws[0], 0, scatter_ride(ps1, ALL_ITEMS), before_ffn1)
    grad_x = dh[BLK:].reshape(1, seq, D)
    dtab = _bias_bwd(grads[0]["dbias"] + grads[1]["dbias"], "bias_bwd")

    small = [dh[PAD:BLK].reshape(128, 128), _rows128(dtab[:, :N_BUCKETS].T, 2)]
    for nm in ("ffn1_norm", "mix_norm", "ffn2_norm"):
        small.append(jnp.stack([grads[l][nm][0] for l in range(2)]).reshape(16, 128))
    small.append(_rows128(jnp.stack([grads[l]["dgn"][4, :8] for l in range(2)]), 1))
    for row in range(4):
        small.append(jnp.stack([grads[l]["dgn"][row, :HD] for l in range(2)]).reshape(1, 128))
    dsk = [grads[l]["dsk"][:, 0, :] for l in range(2)]
    small.append(_rows128(jnp.stack([jnp.stack([d[:, 0], d[:, HD]], axis=1).reshape(8) for d in dsk]), 1))
    gsm = jnp.concatenate(small, axis=0)
    gsm = jnp.pad(gsm, ((0, SMALL_ROWS - gsm.shape[0]), (0, 0)))

    late = (0, 1)
    ps_late, slots = pair_sums(grads[0], gsm, "0l", late)
    ps0.update(ps_late)
    got3_0 = dict(zip([t for items in early_rides for t in items], grads[0]["rode_ffn1"]))
    got3_0.update(zip(late, _scatter_layer([ps0[t] for t in late], "scatter_shards", late)))
    got3 = [got3_0, dict(zip(ALL_ITEMS, grads[0]["rode"]))]
    bufs = []
    for t, (nm, (r, c), _) in enumerate(SHARD_ITEMS):
        buf = lax.empty((2, r, c), F32)
        for l, ps in ((1, ps1), (0, ps0)):
            buf = _sum4_t(ps[t], got3[l][t], buf, place_idx, l, nm, r, f"sum4_{l}_{nm}")
        bufs.append(buf)
    bufs = _join_layer(bufs, "join_halves")
    gs = _sum8(slots, "sum8")

    g_out = {nm: buf for (nm, _, _), buf in zip(SHARD_ITEMS, bufs)}
    g_out["meta_tokens"] = lax.dynamic_slice(gs[0:128].reshape(N_META, D), (0, shard * 256), (N_META, 256))
    off = 128
    for nm, rows in SMALL_ITEMS:
        n = w[nm].size
        g_out[nm] = gs[off:off + rows].reshape(-1)[:n].reshape(w[nm].shape)
        off += rows

    delta, new_m, new_v = {}, {}, {}
    for nm, _, _ in SHARD_ITEMS:
        if nm == "w_in":
            tr_ = lambda a: jnp.swapaxes(a, 1, 2)
            delta[nm], new_m[nm], new_v[nm] = (tr_(a) for a in _adamw3(tr_(w[nm]), tr_(g_out[nm]), tr_(m[nm]), tr_(v[nm]),
                                                                        f"adamw_{nm}"))
        else:
            delta[nm], new_m[nm], new_v[nm] = _adamw3(w[nm], g_out[nm], m[nm], v[nm], f"adamw_{nm}")
    small_names = ["meta_tokens"] + [nm for nm, _ in SMALL_ITEMS]
    small_rows = [META_ROWS] + [rows for _, rows in SMALL_ITEMS]

    def pack_small(src):
        buf = jnp.concatenate([_rows128(src[nm], rows) for nm, rows in zip(small_names, small_rows)], axis=0)
        return jnp.pad(buf, ((0, SMALL_ADAM_ROWS - buf.shape[0]), (0, 0)))

    d_, m_, v_ = (a[0] for a in _adamw3(pack_small(w)[None], pack_small(g_out)[None], pack_small(m)[None],
                                        pack_small(v)[None], "adamw_small"))
    off = 0
    for nm, rows in zip(small_names, small_rows):
        n = w[nm].size
        for dst, src in ((delta, d_), (new_m, m_), (new_v, v_)):
            dst[nm] = src[off:off + rows].reshape(-1)[:n].reshape(w[nm].shape)
        off += rows

    return (loss, grad_x, *[g_out[n] for n in names], *[delta[n] for n in names],
            *[new_m[n] for n in names], *[new_v[n] for n in names])
```

```python
import math

import numpy as np
import jax
import jax.numpy as jnp
from jax import lax
from jax.experimental import pallas as pl
from jax.experimental.pallas import tpu as pltpu

D = 1024
F = 2816
FT = F // 2
HD = 64
NPAIR = 4
N_META = 16
BLK = 128
PAD = BLK - N_META
EPS = 1e-6
NEG = -1e30
N_BUCKETS = 32
GA, GB, QA, KA, VA, QB, KB, VB, FA, DP = 0, 1024, 2048, 2560, 3072, 3584, 4096, 4224, 4352, 4480
D_IN = 4360
CDT = jnp.bfloat16
F32 = jnp.float32
VMEM_LIMIT = 48 * 1024 * 1024
MESH_ID = pl.DeviceIdType.MESH

ADAM_LR, ADAM_B1, ADAM_B2, ADAM_EPS, ADAM_WD, ADAM_STEP = 0.001, 0.9, 0.999, 1e-08, 0.01, 10

SHARD_ITEMS = (
    ("ffn1_w_in", (1024, 1408), "col"),
    ("ffn1_w_out", (704, 1024), "row"),
    ("w_in", (1024, 1090), "col"),
    ("w_branch_fox", (512, 256), "col"),
    ("w_branch_swa", (512, 256), "col"),
    ("w_out", (256, 1024), "row"),
    ("ffn2_w_in", (1024, 1408), "col"),
    ("ffn2_w_out", (704, 1024), "row"),
)
SMALL_ROWS = 192
META_ROWS = 32


def _row_tile(t):
    return 384 if t % 384 == 0 else 128


def _dot(a, b):
    return jnp.dot(a, b, preferred_element_type=F32)


def _dot_nt(a, b):
    return lax.dot_general(a, b, (((1,), (1,)), ((), ())), preferred_element_type=F32)


def _dot_hi(a, b):
    return jnp.dot(a, b, preferred_element_type=F32, precision=lax.Precision.HIGHEST)


def _sigmoid(x):
    return 0.5 * jnp.tanh(0.5 * x) + 0.5


def _iota(shape, dim):
    return lax.broadcasted_iota(jnp.int32, shape, dim)


def _params(sem=None):
    return pltpu.CompilerParams(dimension_semantics=sem, vmem_limit_bytes=VMEM_LIMIT)


def _sds(shape, dtype):
    return jax.ShapeDtypeStruct(shape, dtype)


def _rms_fwd(h, g, name):
    t = h.shape[0]
    tm = _row_tile(t)

    def body(h_ref, g_ref, a_ref, at_ref):
        x = h_ref[...]
        ms = jnp.mean(x * x, axis=-1, keepdims=True)
        a = x * lax.rsqrt(ms + EPS) * g_ref[...]
        a_ref[...] = a.astype(CDT)
        at_ref[...] = a.T.astype(CDT)

    return pl.pallas_call(
        body, name=name, grid=(t // tm,),
        in_specs=[pl.BlockSpec((tm, D), lambda i: (i, 0)), pl.BlockSpec((1, D), lambda i: (0, 0))],
        out_specs=[pl.BlockSpec((tm, D), lambda i: (i, 0)), pl.BlockSpec((D, tm), lambda i: (0, i))],
        out_shape=[_sds((t, D), CDT), _sds((D, t), CDT)],
        compiler_params=_params(("parallel",)),
    )(h, g)


def _ffn_in(a, w_in, name, ride=None):
    t = a.shape[0]
    tm = _row_tile(t)
    tn = FT
    nj = F // tn
    grid = (nj, t // tm)
    ride_in, ride_in_specs, ride_out, ride_out_specs, ride_sems = _ride_specs(ride)

    def body(a_ref, wg_ref, wu_ref, gu_ref, s_ref, st_ref):
        a_ = a_ref[...]
        g = _dot(a_, wg_ref[...])
        u = _dot(a_, wu_ref[...])
        s = g * _sigmoid(g) * u
        gu_ref[0] = g.astype(CDT)
        gu_ref[1] = u.astype(CDT)
        s_ref[...] = s.astype(CDT)
        st_ref[...] = s.T.astype(CDT)

    res = pl.pallas_call(
        _riding(body, 3, 3, ride, grid), name=name, grid=grid,
        in_specs=[pl.BlockSpec((tm, D), lambda j, i: (i, 0)),
                  pl.BlockSpec((D, tn), lambda j, i: (0, j)),
                  pl.BlockSpec((D, tn), lambda j, i: (0, j + nj))] + ride_in_specs,
        out_specs=[pl.BlockSpec((2, tm, tn), lambda j, i: (0, i, j)),
                   pl.BlockSpec((tm, tn), lambda j, i: (i, j)),
                   pl.BlockSpec((tn, tm), lambda j, i: (j, i))] + ride_out_specs,
        out_shape=[_sds((2, t, F), CDT), _sds((t, F), CDT), _sds((F, t), CDT)] + ride_out, scratch_shapes=ride_sems,
        compiler_params=_params(("arbitrary", "arbitrary") if ride else ("parallel", "parallel")),
    )(a, w_in, w_in, *ride_in)
    return (*res[:3], res[3:]) if ride else res


def _mm_res(a, b, res, scale, name, ride=None):
    t, k = a.shape
    n = b.shape[1]
    tm = _row_tile(t)
    tn = n
    grid = (t // tm, n // tn)
    ride_in, ride_in_specs, ride_out, ride_out_specs, ride_sems = _ride_specs(ride)

    def body(a_ref, b_ref, r_ref, o_ref):
        o_ref[...] = r_ref[...] + scale * _dot(a_ref[...], b_ref[...])

    out = pl.pallas_call(
        _riding(body, 3, 1, ride, grid), name=name, grid=grid,
        in_specs=[pl.BlockSpec((tm, k), lambda i, j: (i, 0)),
                  pl.BlockSpec((k, tn), lambda i, j: (0, j)),
                  pl.BlockSpec((tm, tn), lambda i, j: (i, j))] + ride_in_specs,
        out_specs=[pl.BlockSpec((tm, tn), lambda i, j: (i, j))] + ride_out_specs,
        out_shape=[_sds((t, n), F32)] + ride_out, scratch_shapes=ride_sems,
        compiler_params=_params(("arbitrary", "arbitrary") if ride else ("parallel", "parallel")),
    )(a, b, res, *ride_in)
    return (out[0], out[1:]) if ride else out[0]


def _mm(a, b, out_dtype, tm, tn, name, scale=1.0, ride=None):
    m, k = a.shape
    if b.ndim == 3:
        nh = b.shape[2] // tn
        n = 2 * b.shape[2]
        b_spec = pl.BlockSpec((None, k, tn), lambda i, j: (j // nh, 0, j % nh))
    else:
        n = b.shape[1]
        b_spec = pl.BlockSpec((k, tn), lambda i, j: (0, j))
    grid = (m // tm, n // tn)
    ride_in, ride_in_specs, ride_out, ride_out_specs, ride_sems = _ride_specs(ride)

    def body(a_ref, b_ref, o_ref):
        o_ref[...] = (scale * _dot(a_ref[...], b_ref[...])).astype(out_dtype)

    res = pl.pallas_call(
        _riding(body, 2, 1, ride, grid), name=name, grid=grid,
        in_specs=[pl.BlockSpec((tm, k), lambda i, j: (i, 0)), b_spec] + ride_in_specs,
        out_specs=[pl.BlockSpec((tm, tn), lambda i, j: (i, j))] + ride_out_specs,
        out_shape=[_sds((m, n), out_dtype)] + ride_out, scratch_shapes=ride_sems,
        compiler_params=_params(("arbitrary", "arbitrary") if ride else ("parallel", "parallel")),
    )(a, b, *ride_in)
    return (res[0], res[1:]) if ride else res[0]


def _ffn_bwd_mid(dhb, w_out, gu, name, ride=None):
    t = dhb.shape[0]
    tm = _row_tile(t)
    tn = FT
    grid = (F // tn, t // tm)
    ride_in, ride_in_specs, ride_out, ride_out_specs, ride_sems = _ride_specs(ride)

    def body(dh_ref, w_ref, gu_ref, o_ref):
        ds = _dot_nt(dh_ref[...] * 0.5, w_ref[...])
        g = gu_ref[0].astype(F32)
        u = gu_ref[1].astype(F32)
        sg = _sigmoid(g)
        o_ref[0] = (ds * u * (sg * (1.0 + g * (1.0 - sg)))).astype(CDT)
        o_ref[1] = (ds * (g * sg)).astype(CDT)

    res = pl.pallas_call(
        _riding(body, 3, 1, ride, grid), name=name, grid=grid,
        in_specs=[pl.BlockSpec((tm, D), lambda j, i: (i, 0)),
                  pl.BlockSpec((tn, D), lambda j, i: (j, 0)),
                  pl.BlockSpec((2, tm, tn), lambda j, i: (0, i, j))] + ride_in_specs,
        out_specs=[pl.BlockSpec((2, tm, tn), lambda j, i: (0, i, j))] + ride_out_specs,
        out_shape=[_sds((2, t, F), CDT)] + ride_out, scratch_shapes=ride_sems,
        compiler_params=_params(("arbitrary", "arbitrary") if ride else ("parallel", "parallel")),
    )(dhb, w_out, gu, *ride_in)
    return (res[0], res[1:]) if ride else res[0]


def _rms_bwd_rows(da_, x, g, dres, i, dh_ref, dhb_ref, dg_ref):
    r = lax.rsqrt(jnp.mean(x * x, axis=-1, keepdims=True) + EPS)
    xh = x * r
    day = da_ * g
    dh = dres + r * (day - xh * jnp.mean(day * xh, axis=-1, keepdims=True))
    dh_ref[...] = dh
    dhb_ref[...] = dh.astype(CDT)

    @pl.when(i == 0)
    def _():
        dg_ref[...] = jnp.zeros(dg_ref.shape, F32)

    dg_ref[0:1, :] += jnp.sum(da_ * xh, axis=0, keepdims=True)


def _ffn_bwd_in(dgu, w_in, h, g, dres, name, ride=None):
    t = dgu.shape[1]
    tm = _row_tile(t)
    grid = (t // tm,)
    ride_in, ride_in_specs, ride_out, ride_out_specs, ride_sems = _ride_specs(ride)

    def body(dg_ref, wg_ref, wu_ref, h_ref, g_ref, dr_ref, dh_ref, dhb_ref, dgn_ref):
        da_ = _dot_nt(dg_ref[0], wg_ref[...]) + _dot_nt(dg_ref[1], wu_ref[...])
        _rms_bwd_rows(da_, h_ref[...], g_ref[...], dr_ref[...], pl.program_id(0), dh_ref, dhb_ref, dgn_ref)

    row = pl.BlockSpec((tm, D), lambda i: (i, 0))
    res = pl.pallas_call(
        _riding(body, 6, 3, ride, grid), name=name, grid=grid,
        in_specs=[pl.BlockSpec((2, tm, F), lambda i: (0, i, 0)),
                  pl.BlockSpec((D, F), lambda i: (0, 0)),
                  pl.BlockSpec((D, F), lambda i: (0, 1)),
                  row, pl.BlockSpec((1, D), lambda i: (0, 0)), row] + ride_in_specs,
        out_specs=[row, row, pl.BlockSpec((8, D), lambda i: (0, 0))] + ride_out_specs,
        out_shape=[_sds((t, D), F32), _sds((t, D), CDT), _sds((8, D), F32)] + ride_out, scratch_shapes=ride_sems,
        compiler_params=_params(("arbitrary",)),
    )(dgu, w_in, w_in, h, g, dres, *ride_in)
    return (*res[:3], res[3:]) if ride else res


def _mm_nt_rms(a, b, h, g, dres, name):
    t, n = a.shape
    tm = _row_tile(t)

    def body(a_ref, b_ref, h_ref, g_ref, dr_ref, dh_ref, dhb_ref, dgn_ref):
        da_ = _dot_nt(a_ref[...], b_ref[...])
        _rms_bwd_rows(da_, h_ref[...], g_ref[...], dr_ref[...], pl.program_id(0), dh_ref, dhb_ref, dgn_ref)

    row = pl.BlockSpec((tm, D), lambda i: (i, 0))
    return pl.pallas_call(
        body, name=name, grid=(t // tm,),
        in_specs=[pl.BlockSpec((tm, n), lambda i: (i, 0)), pl.BlockSpec((D, n), lambda i: (0, 0)),
                  row, pl.BlockSpec((1, D), lambda i: (0, 0)), row],
        out_specs=[row, row, pl.BlockSpec((8, D), lambda i: (0, 0))],
        out_shape=[_sds((t, D), F32), _sds((t, D), CDT), _sds((8, D), F32)],
        compiler_params=_params(("arbitrary",)),
    )(a, b, h, g, dres)


def _loss(h, target, name):
    t = h.shape[0]
    ta = _row_tile(t)
    nb = ta // BLK

    def body(h_ref, *refs):
        t_refs, (dh_ref, dhb_ref, l_ref) = refs[:nb], refs[nb:]
        i = pl.program_id(0)

        @pl.when(i == 0)
        def _():
            l_ref[...] = jnp.zeros(l_ref.shape, F32)

        tot = 0.0
        for b in range(nb):
            rows = slice(BLK * b, BLK * (b + 1))
            err = jnp.where(i * nb + b > 0, h_ref[rows, :] - t_refs[b][...], 0.0)
            tot = tot + jnp.sum(err * err)
            d = err * (1.0 / D)
            dh_ref[rows, :] = d
            dhb_ref[rows, :] = d.astype(CDT)
        l_ref[...] += (0.5 / D) * tot

    row = pl.BlockSpec((ta, D), lambda i: (i, 0))
    tspecs = [pl.BlockSpec((BLK, D), lambda i, b=b: (jnp.maximum(i * nb + b - 1, 0), 0)) for b in range(nb)]
    return pl.pallas_call(
        body, name=name, grid=(t // ta,),
        in_specs=[row] + tspecs,
        out_specs=[row, row, pl.BlockSpec((8, 128), lambda i: (0, 0))],
        out_shape=[_sds((t, D), F32), _sds((t, D), CDT), _sds((8, 128), F32)],
        compiler_params=_params(("arbitrary",)),
    )(h, *([target] * nb))


def _block_diag():
    return (_iota((128, 128), 0) // HD == _iota((128, 128), 1) // HD).astype(F32)


def _head_sums(v, bd):
    hi = v.astype(CDT)
    rest = (v - hi.astype(F32)).astype(CDT)
    b = bd.astype(CDT)
    return _dot(hi, b) + _dot(rest, b)


def _dup_halves(x, lo):
    sw = pltpu.roll(x, 64, 1)
    return jnp.where(lo, x, sw), jnp.where(lo, sw, x)


def _qknorm_fwd(proj, gfq, gfk, gsq, gsk, fb, name):
    t = proj.shape[0]
    tm = _row_tile(t)

    def body(qa, ka, va, qb, kb, vb, fa, gfq_r, gfk_r, gsq_r, gsk_r, fb_r,
             qf_o, kf_o, vf_o, qs_o, kse_o, vse_o, c_o, ct_o, qft_o, carry):
        i = pl.program_id(0)
        bd = _block_diag()
        lane = _iota((1, 128), 1)
        lo = lane < HD

        def hnorm(x, g):
            ms = _head_sums(x * x, bd) * (1.0 / HD)
            return x * lax.rsqrt(ms + EPS) * g

        for ch in range(4):
            sl = slice(128 * ch, 128 * (ch + 1))
            qn = hnorm(qa[:, sl], gfq_r[:, sl]) * 0.125
            qf_o[:, sl] = qn.astype(CDT)
            qft_o[sl, :] = qn.T.astype(CDT)
            kf_o[:, sl] = hnorm(ka[:, sl], gfk_r[:, sl]).astype(CDT)
            qs_o[:, sl] = (hnorm(qb[:, sl], gsq_r[:, sl]) * 0.125).astype(CDT)
        vf_o[...] = va[...].astype(CDT)
        k0, k1 = _dup_halves(hnorm(kb[...], gsk_r[...]), lo)
        kse_o[0] = k0.astype(CDT)
        kse_o[1] = k1.astype(CDT)
        v0, v1 = _dup_halves(vb[...], lo)
        vse_o[0] = v0.astype(CDT)
        vse_o[1] = v1.astype(CDT)

        z = fa[...] + fb_r[...]
        lf = jnp.minimum(z, 0.0) - jnp.log(1.0 + jnp.exp(-jnp.abs(z)))
        lf = jnp.where(lane < 8, lf, 0.0)
        ltri = (_iota((tm, tm), 1) <= _iota((tm, tm), 0)).astype(F32)

        @pl.when(i == 0)
        def _():
            carry[...] = jnp.zeros(carry.shape, F32)

        c = _dot_hi(ltri, lf) + carry[0:1, :]
        carry[0:1, :] = c[tm - 1:tm, :]
        c_o[...] = c
        ct_o[...] = c.T[0:8, :]

    def col(width, off):
        return pl.BlockSpec((tm, width), lambda i: (i, off // width))

    def vec(width):
        return pl.BlockSpec((1, width), lambda i: (0, 0))

    return pl.pallas_call(
        body, name=name, grid=(t // tm,),
        in_specs=[col(512, QA), col(512, KA), col(512, VA), col(512, QB), col(128, KB), col(128, VB), col(128, FA),
                  vec(512), vec(512), vec(512), vec(128), vec(128)],
        out_specs=[pl.BlockSpec((tm, 512), lambda i: (i, 0))] * 4
        + [pl.BlockSpec((2, tm, 128), lambda i: (0, i, 0))] * 2
        + [pl.BlockSpec((tm, 128), lambda i: (i, 0)), pl.BlockSpec((8, tm), lambda i: (0, i)),
           pl.BlockSpec((512, tm), lambda i: (0, i))],
        out_shape=[_sds((t, 512), CDT)] * 4 + [_sds((2, t, 128), CDT)] * 2
        + [_sds((t, 128), F32), _sds((8, t), F32), _sds((512, t), CDT)],
        scratch_shapes=[pltpu.VMEM((8, 128), F32)],
        compiler_params=_params(("arbitrary",)),
    )(proj, proj, proj, proj, proj, proj, proj, gfq, gfk, gsq, gsk, fb)


def _qknorm_bwd(proj, dqf, dkf, dvf, dqs, dkse, dvse, dcq, dck, dga, dgb, gfq, gfk, gsq, gsk, fb, name):
    t = proj.shape[0]
    tm = _row_tile(t)
    nt = t // tm

    def body(qa, ka, qb, kb, fa, dqf_r, dkf_r, dvf_r, dqs_r, dkse_r, dvse_r, dcq_r, dck_r, dga_r, dgb_r,
             gfq_r, gfk_r, gsq_r, gsk_r, fb_r, dp_o, dgn_o, carry, acc):
        i = pl.program_id(0)
        bd = _block_diag()
        lane = _iota((1, 128), 1)
        lo = lane < HD

        @pl.when(i == 0)
        def _():
            carry[...] = jnp.zeros(carry.shape, F32)
            acc[...] = jnp.zeros(acc.shape, F32)

        def hnorm_bwd(x, g, dy):
            r = lax.rsqrt(_head_sums(x * x, bd) * (1.0 / HD) + EPS)
            xh = x * r
            day = dy * g
            dx = r * (day - xh * (_head_sums(day * xh, bd) * (1.0 / HD)))
            return dx, jnp.sum(dy * xh, axis=0, keepdims=True)

        for ch in range(4):
            sl = slice(128 * ch, 128 * (ch + 1))
            dx, dg = hnorm_bwd(qa[:, sl], gfq_r[:, sl], dqf_r[:, sl] * 0.125)
            dp_o[:, QA + 128 * ch:QA + 128 * (ch + 1)] = dx.astype(CDT)
            acc[0:1, sl] += dg
            dx, dg = hnorm_bwd(ka[:, sl], gfk_r[:, sl], dkf_r[:, sl])
            dp_o[:, KA + 128 * ch:KA + 128 * (ch + 1)] = dx.astype(CDT)
            acc[1:2, sl] += dg
            dx, dg = hnorm_bwd(qb[:, sl], gsq_r[:, sl], dqs_r[:, sl] * 0.125)
            dp_o[:, QB + 128 * ch:QB + 128 * (ch + 1)] = dx.astype(CDT)
            acc[2:3, sl] += dg
        dp_o[:, VA:VA + 512] = dvf_r[...].astype(CDT)
        dp_o[:, GA:GA + D] = dga_r[...]
        dp_o[:, GB:GB + D] = dgb_r[...]

        def fold(x):
            e0 = x[0]
            e1 = x[1]
            return jnp.where(lo, e0 + pltpu.roll(e0, 64, 1), e1 + pltpu.roll(e1, 64, 1))

        dx, dg = hnorm_bwd(kb[...], gsk_r[...], fold(dkse_r))
        dp_o[:, KB:KB + 128] = dx.astype(CDT)
        acc[3:4, 0:128] += dg
        dp_o[:, VB:VB + 128] = fold(dvse_r).astype(CDT)

        rr = _iota((512, 128), 0)
        hh = _iota((512, 128), 1)
        sel = ((rr == (hh >> 1) * 128 + (hh & 1) * HD) & (hh < 8)).astype(F32)
        dcs = _dot_hi(dcq_r[...] - dck_r[...], sel)
        utri = (_iota((tm, tm), 1) >= _iota((tm, tm), 0)).astype(F32)
        dlf = _dot_hi(utri, dcs) + carry[0:1, :]
        carry[0:1, :] = dlf[0:1, :]
        z = fa[...] + fb_r[...]
        dfa = jnp.where(lane < 8, dlf * _sigmoid(-z), 0.0)
        dp_o[:, FA:FA + 128] = dfa.astype(CDT)
        acc[4:5, 0:128] += jnp.sum(dfa, axis=0, keepdims=True)

        @pl.when(i == nt - 1)
        def _():
            foldm = ((_iota((512, 128), 0) & (HD - 1)) == _iota((512, 128), 1)).astype(F32)
            dgn_o[...] = _dot_hi(acc[...], foldm)

    def col(width, off):
        return pl.BlockSpec((tm, width), lambda i: (nt - 1 - i, off // width))

    def rows(width):
        return pl.BlockSpec((tm, width), lambda i: (nt - 1 - i, 0))

    def vec(width):
        return pl.BlockSpec((1, width), lambda i: (0, 0))

    pair = pl.BlockSpec((2, tm, 128), lambda i: (0, nt - 1 - i, 0))
    return pl.pallas_call(
        body, name=name, grid=(nt,),
        in_specs=[col(512, QA), col(512, KA), col(512, QB), col(128, KB), col(128, FA),
                  rows(512), rows(512), rows(512), rows(512), pair, pair, rows(512), rows(512), rows(D), rows(D),
                  vec(512), vec(512), vec(512), vec(128), vec(128)],
        out_specs=[rows(DP), pl.BlockSpec((8, 128), lambda i: (0, 0))],
        out_shape=[_sds((t, DP), CDT), _sds((8, 128), F32)],
        scratch_shapes=[pltpu.VMEM((8, 128), F32), pltpu.VMEM((8, 512), F32)],
        compiler_params=_params(("arbitrary",)),
    )(proj, proj, proj, proj, proj, dqf, dkf, dvf, dqs, dkse, dvse, dcq, dck, dga, dgb, gfq, gfk, gsq, gsk, fb)


def _gate_out_fwd(ofox, oswa, wbf, wbs, proj, w_out, h, name):
    t = ofox.shape[0]
    tm = _row_tile(t)

    def body(of_r, os_r, wf_r, ws_r, ga_r, gb_r, wo_r, h_r, ho_o, yt_o, pf_o, ps_o, oft_o, ost_o):
        pf = _dot(of_r[...], wf_r[...])
        ps = _dot(os_r[...], ws_r[...])
        y = _sigmoid(ga_r[...]) * pf + _sigmoid(gb_r[...]) * ps
        ho_o[...] = h_r[...] + _dot(y.astype(CDT), wo_r[...])
        yt_o[...] = y.T.astype(CDT)
        pf_o[...] = pf.astype(CDT)
        ps_o[...] = ps.astype(CDT)
        oft_o[...] = of_r[...].astype(F32).T.astype(CDT)
        ost_o[...] = os_r[...].astype(F32).T.astype(CDT)

    row = pl.BlockSpec((tm, D), lambda i: (i, 0))
    half = pl.BlockSpec((tm, 512), lambda i: (i, 0))
    whole = lambda r: pl.BlockSpec((r, D), lambda i: (0, 0))
    tcol = lambda r: pl.BlockSpec((r, tm), lambda i: (0, i))
    return pl.pallas_call(
        body, name=name, grid=(t // tm,),
        in_specs=[half, half, whole(512), whole(512),
                  pl.BlockSpec((tm, D), lambda i: (i, GA // D)), pl.BlockSpec((tm, D), lambda i: (i, GB // D)),
                  whole(D), row],
        out_specs=[row, tcol(D), row, row, tcol(512), tcol(512)],
        out_shape=[_sds((t, D), F32), _sds((D, t), CDT), _sds((t, D), CDT), _sds((t, D), CDT),
                   _sds((512, t), CDT), _sds((512, t), CDT)],
        compiler_params=_params(("parallel",)),
    )(ofox, oswa, wbf, wbs, proj, proj, w_out, h)


def _gate_out_bwd(dhb, w_out, pf, ps, proj, wbf, wbs, name):
    t = dhb.shape[0]
    tm = _row_tile(t)

    def body(dh_r, wo_r, pf_r, ps_r, ga_r, gb_r, wf_r, ws_r, dpf_o, dps_o, dga_o, dgb_o, dof_o, doft_o, dos_o):
        dy_ = _dot_nt(dh_r[...], wo_r[...])
        sa = _sigmoid(ga_r[...])
        sb = _sigmoid(gb_r[...])
        dpf = (dy_ * sa).astype(CDT)
        dps = (dy_ * sb).astype(CDT)
        dpf_o[...] = dpf
        dps_o[...] = dps
        dga_o[...] = (dy_ * pf_r[...].astype(F32) * (sa * (1.0 - sa))).astype(CDT)
        dgb_o[...] = (dy_ * ps_r[...].astype(F32) * (sb * (1.0 - sb))).astype(CDT)
        dof = _dot_nt(dpf, wf_r[...])
        dof_o[...] = dof
        doft_o[...] = dof.T.astype(CDT)
        dos_o[...] = _dot_nt(dps, ws_r[...])

    row = pl.BlockSpec((tm, D), lambda i: (i, 0))
    half = pl.BlockSpec((tm, 512), lambda i: (i, 0))
    whole = lambda r: pl.BlockSpec((r, D), lambda i: (0, 0))
    return pl.pallas_call(
        body, name=name, grid=(t // tm,),
        in_specs=[row, whole(D), row, row,
                  pl.BlockSpec((tm, D), lambda i: (i, GA // D)), pl.BlockSpec((tm, D), lambda i: (i, GB // D)),
                  whole(512), whole(512)],
        out_specs=[row] * 4 + [half, pl.BlockSpec((512, tm), lambda i: (0, i)), half],
        out_shape=[_sds((t, D), CDT)] * 4 + [_sds((t, 512), F32), _sds((512, t), CDT), _sds((t, 512), F32)],
        compiler_params=_params(("parallel",)),
    )(dhb, w_out, pf, ps, proj, proj, wbf, wbs)


def _tri_steps(n, by_key):
    if by_key:
        pairs = [(i, j) for j in range(n) for i in range(j, n)]
    else:
        pairs = [(i, j) for i in range(n) for j in range(i + 1)]
    return (np.array([p[0] for p in pairs], np.int32), np.array([p[1] for p in pairs], np.int32))


def _head_col(blk, lane, h):
    return jnp.sum(jnp.where(lane == h, blk, 0.0), axis=1, keepdims=True)


def _head_row(blk, sub, h):
    return jnp.sum(jnp.where(sub == h, blk, 0.0), axis=0, keepdims=True)


def _ride_specs(ride):
    if ride is None:
        return [], [], [], [], []
    kind, srcs, outs, layer, items = ride
    return list(srcs), [ANY] * len(srcs), list(outs), [ANY] * len(outs), _dma_sems(3 * len(srcs))


def _ride_start(ride, srcs, dsts, send_sems, recv_sems):
    for cp in _ici_copies(ride[0], srcs, dsts, send_sems, recv_sems, ride[3], recv=False, items=ride[4])[0]:
        cp.start()


def _ride_wait(ride, srcs, dsts, send_sems, recv_sems):
    sends, recvs = _ici_copies(ride[0], srcs, dsts, send_sems, recv_sems, ride[3], items=ride[4])
    for cp in recvs:
        cp.wait_recv()
    for cp in sends:
        cp.wait_send()


def _riding(body, n_in, n_out, ride, grid):
    if ride is None:
        return body
    nr = len(ride[1])

    def wrapped(*refs):
        ins, srcs = refs[:n_in], refs[n_in:n_in + nr]
        outs, dsts = refs[n_in + nr:n_in + nr + n_out], refs[n_in + nr + n_out:n_in + 2 * nr + n_out]
        scratch, sems = refs[n_in + 2 * nr + n_out:-2], refs[-2:]
        first = pl.program_id(0) == 0
        last = pl.program_id(0) == grid[0] - 1
        for a in range(1, len(grid)):
            first = first & (pl.program_id(a) == 0)
            last = last & (pl.program_id(a) == grid[a] - 1)

        @pl.when(first)
        def _():
            _ride_start(ride, srcs, dsts, *sems)

        body(*ins, *outs, *scratch)

        @pl.when(last)
        def _():
            _ride_wait(ride, srcs, dsts, *sems)

    return wrapped


def _fox_fwd(qf, kf, vf, c, ct, name, ride=None):
    t = qf.shape[0]
    ta = _row_tile(t)
    qi, kj = _tri_steps(t // ta, by_key=False)
    nsteps = len(qi)
    ride_in, ride_in_specs, ride_out, ride_out_specs, ride_sems = _ride_specs(ride)

    def body(qi_r, kj_r, q_r, k_r, v_r, c_r, ct_r, *rest):
        nr = len(ride_in)
        src_r, (o_o, lse_o), dst_o = rest[:nr], rest[nr:nr + 2], rest[nr + 2:2 * nr + 2]
        m_sc, l_sc, acc_sc, cq_sc, *sems = rest[2 * nr + 2:]
        p = pl.program_id(0)
        n = pl.program_id(1)
        i = qi_r[n]
        j = kj_r[n]
        lane = _iota((1, 128), 1)
        lo = lane < HD

        if ride is not None:
            @pl.when((p == 0) & (n == 0))
            def _():
                _ride_start(ride, src_r, dst_o, *sems)

        @pl.when(j == 0)
        def _():
            m_sc[...] = jnp.full(m_sc.shape, NEG, F32)
            l_sc[...] = jnp.zeros(l_sc.shape, F32)
            acc_sc[...] = jnp.zeros(acc_sc.shape, F32)
            for e in (0, 1):
                cq_sc[e] = jnp.broadcast_to(_head_col(c_r[...], lane, 2 * p + e), (ta, 128))

        def step(masked):
            q = q_r[...]
            k = k_r[...]
            vaug = jnp.concatenate([v_r[...], jnp.ones((ta, 128), CDT)], axis=1)
            if masked:
                rows = i * ta + _iota((ta, 1), 0)
                cols = j * ta + _iota((1, ta), 1)
                mask = (cols <= rows) & (cols >= PAD)
            sub = _iota((8, 1), 0)
            heads = (0, 1)
            sels = [lo, jnp.logical_not(lo)]
            s = [_dot_nt(jnp.where(sels[e], q, 0), k) for e in heads]
            ck = [_head_row(ct_r[...], sub, 2 * p + e) for e in heads]
            chunks = []
            for e in heads:
                cq = cq_sc[e]
                row = []
                for ch in range(ta // 128):
                    sl = slice(128 * ch, 128 * (ch + 1))
                    sc = s[e][:, sl] + cq - ck[e][:, sl]
                    if masked:
                        sc = jnp.where(mask[:, sl], sc, NEG)
                    row.append(sc)
                chunks.append(row)
            m_new, alphas = [], []
            for e in heads:
                mx = chunks[e][0]
                for sc in chunks[e][1:]:
                    mx = jnp.maximum(mx, sc)
                m_prev = m_sc[e]
                m_new.append(jnp.maximum(m_prev, jnp.max(mx, axis=1, keepdims=True)))
                alphas.append(jnp.exp(m_prev - m_new[e]))
            pe = [jnp.concatenate([jnp.exp(sc - m_new[e]).astype(CDT) for sc in chunks[e]], axis=1) for e in heads]
            pva = [_dot(pe[e], vaug) for e in heads]
            for e in heads:
                l_sc[e] = alphas[e] * l_sc[e] + pva[e][:, 128:]
                m_sc[e] = m_new[e]
            acc_sc[...] = (acc_sc[...] * jnp.where(lo, alphas[0], alphas[1])
                           + jnp.where(lo, pva[0][:, :128], pva[1][:, :128]))

        edge = (j == i) | (j == 0)

        @pl.when(edge)
        def _():
            step(True)

        @pl.when(jnp.logical_not(edge))
        def _():
            step(False)

        @pl.when(j == i)
        def _():
            l = jnp.where(lo, l_sc[0], l_sc[1])
            o_o[...] = (acc_sc[...] / l).astype(CDT)
            lse_o[...] = jnp.where(lo, m_sc[0], m_sc[1]) + jnp.log(l)

        if ride is not None:
            @pl.when((p == NPAIR - 1) & (n == nsteps - 1))
            def _():
                _ride_wait(ride, src_r, dst_o, *sems)

    qblk = pl.BlockSpec((ta, 128), lambda p, n, qi_r, kj_r: (qi_r[n], p))
    kblk = pl.BlockSpec((ta, 128), lambda p, n, qi_r, kj_r: (kj_r[n], p))
    grid_spec = pltpu.PrefetchScalarGridSpec(
        num_scalar_prefetch=2, grid=(NPAIR, nsteps),
        in_specs=[qblk, kblk, kblk,
                  pl.BlockSpec((ta, 128), lambda p, n, qi_r, kj_r: (qi_r[n], 0)),
                  pl.BlockSpec((8, ta), lambda p, n, qi_r, kj_r: (0, kj_r[n]))] + ride_in_specs,
        out_specs=[qblk, qblk] + ride_out_specs,
        scratch_shapes=[pltpu.VMEM((2, ta, 128), F32), pltpu.VMEM((2, ta, 128), F32), pltpu.VMEM((ta, 128), F32),
                        pltpu.VMEM((2, ta, 128), F32)] + ride_sems,
    )
    return pl.pallas_call(
        body, name=name, grid_spec=grid_spec,
        out_shape=[_sds((t, 512), CDT), _sds((t, 512), F32)] + ride_out,
        compiler_params=_params(("arbitrary", "arbitrary")),
    )(jnp.asarray(qi), jnp.asarray(kj), qf, kf, vf, c, ct, *ride_in)


def _fox_bwd(qf, qft, kf, vf, c, ct, o, lse, do, dot, name, ride=None):
    t = qf.shape[0]
    ta = _row_tile(t)
    nq = t // ta
    qi, kj = _tri_steps(nq, by_key=False)
    nsteps = len(qi)
    ride_in, ride_in_specs, ride_out, ride_out_specs, ride_sems = _ride_specs(ride)

    def body(qi_r, kj_r, q_r, qt_r, k_r, v_r, c_r, ct_r, o_r, lse_r, do_r, dot_r, *rest):
        nr = len(ride_in)
        src_r, (dq_o, dcq_o, dk_o, dv_o, dck_o), dst_o = rest[:nr], rest[nr:nr + 5], rest[nr + 5:2 * nr + 5]
        lse_sc, dl_sc, cq_sc, dq_sc, dcq_sc, dkt_sc, dvt_sc, dckt_sc, *sems = rest[2 * nr + 5:]
        p = pl.program_id(0)
        n = pl.program_id(1)
        i = qi_r[n]
        j = kj_r[n]
        lane = _iota((1, 128), 1)
        lo = lane < HD
        top = _iota((128, 1), 0) < HD

        if ride is not None:
            @pl.when((p == 0) & (n == 0))
            def _():
                _ride_start(ride, src_r, dst_o, *sems)

        @pl.when(n == 0)
        def _():
            dkt_sc[...] = jnp.zeros(dkt_sc.shape, F32)
            dvt_sc[...] = jnp.zeros(dvt_sc.shape, F32)
            dckt_sc[...] = jnp.zeros(dckt_sc.shape, F32)

        @pl.when(j == 0)
        def _():
            dq_sc[...] = jnp.zeros(dq_sc.shape, F32)
            dcq_sc[...] = jnp.zeros(dcq_sc.shape, F32)
            dd = do_r[...] * o_r[...].astype(F32)
            lse = lse_r[...]
            for e in (0, 1):
                sel = lo if e == 0 else jnp.logical_not(lo)
                cq_sc[e] = jnp.broadcast_to(_head_col(c_r[...], lane, 2 * p + e), (ta, 128))
                dl_sc[e] = jnp.broadcast_to(jnp.sum(jnp.where(sel, dd, 0.0), axis=1, keepdims=True), (ta, 128))
                lse_sc[e] = jnp.broadcast_to(lse[:, HD * e:HD * e + 1], (ta, 128))

        def step(masked):
            q = q_r[...]
            qt = qt_r[...]
            k = k_r[...]
            v = v_r[...]
            dob = do_r[...].astype(CDT)
            dot_ = dot_r[...]
            ones = jnp.ones((ta, 128), CDT)
            ones16 = jnp.ones((16, ta), CDT)
            if masked:
                rows = i * ta + _iota((ta, 1), 0)
                cols = j * ta + _iota((1, ta), 1)
                mask = (cols <= rows) & (cols >= PAD)
            sub = _iota((8, 1), 0)
            heads = (0, 1)
            sels = [lo, jnp.logical_not(lo)]
            rsels = [top, jnp.logical_not(top)]
            s = [_dot_nt(jnp.where(sels[e], q, 0), k) for e in heads]
            dp = [_dot_nt(jnp.where(sels[e], dob, 0), v) for e in heads]
            ck = [_head_row(ct_r[...], sub, 2 * p + e) for e in heads]
            pb, dsb = [], []
            for e in heads:
                cq, lse_e, dl = cq_sc[e], lse_sc[e], dl_sc[e]
                prs, dss = [], []
                for ch in range(ta // 128):
                    sl = slice(128 * ch, 128 * (ch + 1))
                    sc = s[e][:, sl] + cq - ck[e][:, sl]
                    if masked:
                        sc = jnp.where(mask[:, sl], sc, NEG)
                    pr = jnp.exp(sc - lse_e)
                    prs.append(pr.astype(CDT))
                    dss.append((pr * (dp[e][:, sl] - dl)).astype(CDT))
                pb.append(jnp.concatenate(prs, axis=1))
                dsb.append(jnp.concatenate(dss, axis=1))
            dvt = [_dot(jnp.where(rsels[e], dot_, 0), pb[e]) for e in heads]
            dkc = [_dot(jnp.concatenate([jnp.where(rsels[e], qt, 0), ones16], axis=0), dsb[e]) for e in heads]
            dqa = [_dot(dsb[e], jnp.concatenate([jnp.where(sels[e], k, 0), ones], axis=1)) for e in heads]
            dvt_sc[j] += dvt[0] + dvt[1]
            dkt_sc[j] += dkc[0][0:128] + dkc[1][0:128]
            dckt_sc[j, 0:8, :] += jnp.where(sub == 0, dkc[0][128:136], jnp.where(sub == 1, dkc[1][128:136], 0.0))
            dq_sc[...] += dqa[0][:, :128] + dqa[1][:, :128]
            for e in heads:
                dcq_sc[e] += dqa[e][:, 128:]

        edge = (j == i) | (j == 0)

        @pl.when(edge)
        def _():
            step(True)

        @pl.when(jnp.logical_not(edge))
        def _():
            step(False)

        @pl.when(j == i)
        def _():
            dq_o[...] = dq_sc[...]
            dcq_o[...] = jnp.where(lo, dcq_sc[0], dcq_sc[1])

        @pl.when(n == nsteps - 1)
        def _():
            spread = (_iota((128, 128), 1) == _iota((128, 128), 0) // HD).astype(F32)
            for jb in range(nq):
                rs = slice(jb * ta, (jb + 1) * ta)
                dk_o[rs, :] = dkt_sc[jb].T
                dv_o[rs, :] = dvt_sc[jb].T
                dck_o[rs, :] = _dot_hi(spread, dckt_sc[jb]).T

        if ride is not None:
            @pl.when((p == NPAIR - 1) & (n == nsteps - 1))
            def _():
                _ride_wait(ride, src_r, dst_o, *sems)

    qblk = pl.BlockSpec((ta, 128), lambda p, n, qi_r, kj_r: (qi_r[n], p))
    qtblk = pl.BlockSpec((128, ta), lambda p, n, qi_r, kj_r: (p, qi_r[n]))
    kblk = pl.BlockSpec((ta, 128), lambda p, n, qi_r, kj_r: (kj_r[n], p))
    whole = pl.BlockSpec((t, 128), lambda p, n, qi_r, kj_r: (0, p))
    grid_spec = pltpu.PrefetchScalarGridSpec(
        num_scalar_prefetch=2, grid=(NPAIR, nsteps),
        in_specs=[qblk, qtblk, kblk, kblk,
                  pl.BlockSpec((ta, 128), lambda p, n, qi_r, kj_r: (qi_r[n], 0)),
                  pl.BlockSpec((8, ta), lambda p, n, qi_r, kj_r: (0, kj_r[n])),
                  qblk, qblk, qblk, qtblk] + ride_in_specs,
        out_specs=[qblk, qblk, whole, whole, whole] + ride_out_specs,
        scratch_shapes=[pltpu.VMEM((2, ta, 128), F32)] * 3 + [pltpu.VMEM((ta, 128), F32), pltpu.VMEM((2, ta, 128), F32)]
        + [pltpu.VMEM((nq, 128, ta), F32)] * 3 + ride_sems,
    )
    return pl.pallas_call(
        body, name=name, grid_spec=grid_spec,
        out_shape=[_sds((t, 512), F32)] * 5 + ride_out,
        compiler_params=_params(("arbitrary", "arbitrary")),
    )(jnp.asarray(qi), jnp.asarray(kj), qf, qft, kf, vf, c, ct, o, lse, do, dot, *ride_in)


def _bucket_table():
    r = np.arange(BLK)[:, None]
    c = np.arange(3 * BLK)[None, :]
    d = np.where(c < BLK, r + BLK - c, r - (c - BLK))
    n = np.maximum(d, 0)
    max_exact = N_BUCKETS // 2
    nf = np.maximum(n, 1).astype(np.float32)
    large = max_exact + (np.log(nf / max_exact) / math.log(BLK / max_exact) * (N_BUCKETS - max_exact)).astype(np.int32)
    large = np.minimum(large, N_BUCKETS - 1)
    b = np.where(n < max_exact, n, large)
    return np.where(c < 2 * BLK, b, N_BUCKETS - 1).astype(np.int32)


def _bias_fwd(table, name):
    bucket = jnp.asarray(_bucket_table())

    def body(tab_r, b_r, o_o):
        h = pl.program_id(0)
        b = b_r[...]
        acc = jnp.zeros(b.shape, F32)
        for k in range(N_BUCKETS):
            acc = jnp.where(b == k, tab_r[k, h], acc)
        o_o[...] = acc

    return pl.pallas_call(
        body, name=name, grid=(8,),
        in_specs=[pl.BlockSpec(memory_space=pltpu.SMEM), pl.BlockSpec((BLK, 3 * BLK), lambda h: (0, 0))],
        out_specs=pl.BlockSpec((None, BLK, 3 * BLK), lambda h: (h, 0, 0)),
        out_shape=_sds((8, BLK, 3 * BLK), F32),
        compiler_params=_params(("parallel",)),
    )(table, bucket)


def _bias_bwd(dbias, name):
    bucket = jnp.asarray(_bucket_table())

    def body(d_r, b_r, o_o):
        h = pl.program_id(0)
        b = b_r[...]
        d = d_r[...]
        lane = _iota((1, 128), 1)
        row = jnp.zeros((1, 128), F32)
        for k in range(N_BUCKETS):
            row = jnp.where(lane == k, jnp.sum(jnp.where(b == k, d, 0.0)), row)
        o_o[pl.ds(h, 1), :] = row

    return pl.pallas_call(
        body, name=name, grid=(8,),
        in_specs=[pl.BlockSpec((None, BLK, 3 * BLK), lambda h: (h, 0, 0)), pl.BlockSpec((BLK, 3 * BLK), lambda h: (0, 0))],
        out_specs=pl.BlockSpec((8, 128), lambda h: (0, 0)),
        out_shape=_sds((8, 128), F32),
        compiler_params=_params(("arbitrary",)),
    )(dbias, bucket)


def _swa_valid(i):
    r = _iota((BLK, 1), 0)
    c = _iota((1, 3 * BLK), 1)
    prev = (c < BLK) & (c > r) & (i >= 1) & ((i - 1) * BLK + c >= PAD)
    cc = c - BLK
    cur = (c >= BLK) & (c < 2 * BLK) & (cc <= r) & (i * BLK + cc >= PAD)
    cm = c - 2 * BLK
    meta = (c >= 2 * BLK) & (cm >= PAD) & (i * BLK + r - cm >= BLK)
    return prev | cur | meta


def _swa_kv_specs(ta):
    nb = ta // BLK
    return [pl.BlockSpec((None, BLK, 128), lambda p, i: (p // 2, jnp.maximum(i * nb - 1, 0), 0)),
            pl.BlockSpec((None, ta, 128), lambda p, i: (p // 2, i, 0)),
            pl.BlockSpec((None, BLK, 128), lambda p, i: (p // 2, 0, 0))]


def _swa_fwd(qs, kse, vse, bias, sinks, name, ride=None):
    t = qs.shape[0]
    ta = _row_tile(t)
    nb = ta // BLK
    grid = (NPAIR, t // ta)
    ride_in, ride_in_specs, ride_out, ride_out_specs, ride_sems = _ride_specs(ride)

    def body(sink_r, q_r, kp_r, kc_r, km_r, vp_r, vc_r, vm_r, b_r, o_o, lse_o):
        p = pl.program_id(0)
        i = pl.program_id(1)
        lo = _iota((1, 128), 1) < HD
        k4 = jnp.concatenate([kp_r[...], kc_r[...]], axis=0)
        v4 = jnp.concatenate([vp_r[...], vc_r[...]], axis=0)
        work = [(b, e) for b in range(nb) for e in (0, 1)]
        sinks = [sink_r[2 * p + e] for e in (0, 1)]
        v3 = [jnp.concatenate([v4[BLK * b:BLK * (b + 2)], vm_r[...]], axis=0) for b in range(nb)]
        s = {}
        for b in range(nb):
            q = q_r[BLK * b:BLK * (b + 1), :]
            k3 = jnp.concatenate([k4[BLK * b:BLK * (b + 2)], km_r[...]], axis=0)
            valid = _swa_valid(i * nb + b)
            for e in (0, 1):
                sel = lo if e == 0 else jnp.logical_not(lo)
                s[b, e] = jnp.where(valid, _dot_nt(jnp.where(sel, q, 0), k3) + b_r[e], NEG)
        mx = {w: jnp.maximum(jnp.max(s[w], axis=1, keepdims=True), sinks[w[1]]) for w in work}
        pe = {w: jnp.exp(s[w] - mx[w]) for w in work}
        den = {w: jnp.sum(pe[w], axis=1, keepdims=True) + jnp.exp(sinks[w[1]] - mx[w]) for w in work}
        out = {w: _dot(pe[w].astype(CDT), v3[w[0]]) / den[w] for w in work}
        for b in range(nb):
            rows = slice(BLK * b, BLK * (b + 1))
            o_o[rows, :] = jnp.where(lo, out[b, 0], out[b, 1]).astype(CDT)
            lse_o[rows, :] = jnp.where(lo, mx[b, 0] + jnp.log(den[b, 0]), mx[b, 1] + jnp.log(den[b, 1]))

    qblk = pl.BlockSpec((ta, 128), lambda p, i: (i, p))
    res = pl.pallas_call(
        _riding(body, 9, 2, ride, grid), name=name, grid=grid,
        in_specs=[pl.BlockSpec(memory_space=pltpu.SMEM), qblk] + _swa_kv_specs(ta) + _swa_kv_specs(ta)
        + [pl.BlockSpec((2, BLK, 3 * BLK), lambda p, i: (p, 0, 0))] + ride_in_specs,
        out_specs=[qblk, qblk] + ride_out_specs,
        out_shape=[_sds((t, 512), CDT), _sds((t, 512), F32)] + ride_out, scratch_shapes=ride_sems,
        compiler_params=_params(("arbitrary", "arbitrary") if ride else ("parallel", "parallel")),
    )(sinks, qs, kse, kse, kse, vse, vse, vse, bias, *ride_in)
    return (res[0], res[1], res[2:]) if ride else res


def _swa_bwd(qs, kse, vse, bias, sinks, o, lse, do, name):
    t = qs.shape[0]
    ta = _row_tile(t)
    nb = ta // BLK

    def body(sink_r, q_r, kp_r, kc_r, km_r, vp_r, vc_r, vm_r, b_r, o_r, lse_r, do_r,
             dq_o, dk_o, dv_o, db_o, dsk_o):
        p = pl.program_id(0)
        i = pl.program_id(1)
        lo = _iota((1, 128), 1) < HD

        @pl.when((i == 0) & (p % 2 == 0))
        def _():
            dk_o[...] = jnp.zeros(dk_o.shape, F32)
            dv_o[...] = jnp.zeros(dv_o.shape, F32)

        @pl.when(i == 0)
        def _():
            db_o[...] = jnp.zeros(db_o.shape, F32)
            dsk_o[...] = jnp.zeros(dsk_o.shape, F32)

        k4 = jnp.concatenate([kp_r[...], kc_r[...]], axis=0)
        v4 = jnp.concatenate([vp_r[...], vc_r[...]], axis=0)
        work = [(b, e) for b in range(nb) for e in (0, 1)]
        sel = [lo, jnp.logical_not(lo)]
        k3 = [jnp.concatenate([k4[BLK * b:BLK * (b + 2)], km_r[...]], axis=0) for b in range(nb)]
        v3 = [jnp.concatenate([v4[BLK * b:BLK * (b + 2)], vm_r[...]], axis=0) for b in range(nb)]
        q = [q_r[BLK * b:BLK * (b + 1), :] for b in range(nb)]
        do_ = [do_r[BLK * b:BLK * (b + 1), :] for b in range(nb)]
        lse = [lse_r[BLK * b:BLK * (b + 1), :] for b in range(nb)]
        dd = [do_[b] * o_r[BLK * b:BLK * (b + 1), :].astype(F32) for b in range(nb)]
        valid = [_swa_valid(i * nb + b) for b in range(nb)]
        qe = {(b, e): jnp.where(sel[e], q[b], 0) for b, e in work}
        doe = {(b, e): jnp.where(sel[e], do_[b], 0.0).astype(CDT) for b, e in work}
        lse_e = {(b, e): lse[b][:, HD * e:HD * e + 1] for b, e in work}
        delta = {(b, e): jnp.sum(jnp.where(sel[e], dd[b], 0.0), axis=1, keepdims=True) for b, e in work}
        s = {(b, e): jnp.where(valid[b], _dot_nt(qe[b, e], k3[b]) + b_r[e], NEG) for b, e in work}
        dp = {(b, e): _dot_nt(doe[b, e], v3[b]) for b, e in work}
        pr = {w: jnp.exp(s[w] - lse_e[w]) for w in work}
        ds = {w: pr[w] * (dp[w] - delta[w]) for w in work}
        dqs = {(b, e): _dot(ds[b, e].astype(CDT), jnp.where(sel[e], k3[b], 0)) for b, e in work}
        both = lambda x, b: jnp.concatenate([x[b, 0], x[b, 1]], axis=0)
        dk3 = [_dot(both(ds, b).T.astype(CDT), both(qe, b)) for b in range(nb)]
        dv3 = [_dot(both(pr, b).T.astype(CDT), both(doe, b)) for b in range(nb)]
        for e in (0, 1):
            tot = ds[0, e]
            for b in range(1, nb):
                tot = tot + ds[b, e]
            db_o[e] += tot
        dsink = [sum(-jnp.sum(jnp.exp(sink_r[2 * p + e] - lse_e[b, e]) * delta[b, e], axis=0, keepdims=True)
                     for b in range(nb)) for e in (0, 1)]
        dsk_o[0:1, :] += jnp.where(lo, dsink[0], dsink[1])
        for b in range(nb):
            ib = i * nb + b
            dq_o[BLK * b:BLK * (b + 1), :] = dqs[b, 0] + dqs[b, 1]
            dk = dk3[b]
            dv = dv3[b]
            prev = pl.ds(pl.multiple_of(jnp.maximum(ib - 1, 0) * BLK, BLK), BLK)
            cur = pl.ds(pl.multiple_of(ib * BLK, BLK), BLK)
            dk_o[prev, :] += dk[0:BLK]
            dk_o[cur, :] += dk[BLK:2 * BLK]
            dk_o[0:BLK, :] += dk[2 * BLK:]
            dv_o[prev, :] += dv[0:BLK]
            dv_o[cur, :] += dv[BLK:2 * BLK]
            dv_o[0:BLK, :] += dv[2 * BLK:]

    qblk = pl.BlockSpec((ta, 128), lambda p, i: (i, p))
    kvacc = pl.BlockSpec((None, t, 128), lambda p, i: (p // 2, 0, 0))
    bblk = pl.BlockSpec((2, BLK, 3 * BLK), lambda p, i: (p, 0, 0))
    return pl.pallas_call(
        body, name=name, grid=(NPAIR, t // ta),
        in_specs=[pl.BlockSpec(memory_space=pltpu.SMEM), qblk] + _swa_kv_specs(ta) + _swa_kv_specs(ta)
        + [bblk, qblk, qblk, qblk],
        out_specs=[qblk, kvacc, kvacc, bblk, pl.BlockSpec((None, 8, 128), lambda p, i: (p, 0, 0))],
        out_shape=[_sds((t, 512), F32), _sds((2, t, 128), F32), _sds((2, t, 128), F32),
                   _sds((8, BLK, 3 * BLK), F32), _sds((NPAIR, 8, 128), F32)],
        compiler_params=_params(("arbitrary", "arbitrary")),
    )(sinks, qs, kse, kse, kse, vse, vse, vse, bias, o, lse, do)


def _sum8(slots, name):
    def body(a_r, o_o):
        acc = a_r[0]
        for k in range(1, 8):
            acc = acc + a_r[k]
        o_o[...] = acc

    return pl.pallas_call(
        body, name=name, out_shape=_sds((SMALL_ROWS, 128), F32),
        in_specs=[pl.BlockSpec(memory_space=pltpu.VMEM)], out_specs=pl.BlockSpec(memory_space=pltpu.VMEM),
        compiler_params=_params(),
    )(slots)


def _place():
    x, y, c = lax.axis_index("x"), lax.axis_index("y"), lax.axis_index("c")
    chips = [(1 - x, y), (x, 1 - y), (1 - x, 1 - y)]
    return x, y, c, chips


def _remote(src, dst, send_sems, recv_sems, k, to):
    return pltpu.make_async_remote_copy(src_ref=src, dst_ref=dst, send_sem=send_sems.at[k], recv_sem=recv_sems.at[k],
                                        device_id=to, device_id_type=MESH_ID)


ANY = pl.BlockSpec(memory_space=pl.ANY)


def _mix_cols(w):
    return jnp.concatenate([w[:, 2312:4360], w[:, 0:1536], w[:, 1544:2312], w[:, 1536:1544],
                            jnp.zeros((w.shape[0], DP - D_IN), w.dtype)], axis=1)


def _unmix_cols(w):
    return jnp.concatenate([w[:, QA:QA + 1536], w[:, FA:FA + 8], w[:, QB:QB + 768], w[:, GA:GA + 2048]], axis=1)


def _rows128(a, rows):
    flat = a.reshape(-1)
    return jnp.pad(flat, (0, rows * 128 - flat.shape[0])).reshape(rows, 128)


GRAD_FORM = {"ffn1_w_in": "col", "ffn2_w_in": "col", "w_branch_fox": "col", "w_branch_swa": "col",
             "ffn1_w_out": "3d", "ffn2_w_out": "3d", "w_out": "3d", "w_in": "3d"}
SUM_TILE = {1024: 256, 704: 176, 512: 256, 256: 128}
NT = len(SHARD_ITEMS)
ALL_ITEMS = tuple(range(NT))


def _half_rows(c, r):
    return pl.ds(pl.multiple_of(c * (r // 2), 16), r // 2)


def _ici_copies(kind, srcs, dsts, send_sems, recv_sems, layer, recv=True, items=ALL_ITEMS):
    x, y, c, chips = _place()
    s = 2 * x + y
    sends, recvs = [], []
    for t, (item, src, dst) in enumerate(zip(items, srcs, dsts)):
        nm, (r, cc), _ = SHARD_ITEMS[item]
        for j, (cx, cy) in enumerate(chips):
            sj = 2 * cx + cy
            k = 3 * t + j
            to = (cx, cy, c)
            if kind == "gather":
                hs = _half_rows(c, r)
                sends.append(_remote(src.at[layer, hs], dst.at[s, hs], send_sems, recv_sems, k, to))
                if recv:
                    recvs.append(_remote(src.at[layer, hs], dst.at[sj, hs], send_sems, recv_sems, k, to))
            else:
                if GRAD_FORM[nm] == "col":
                    piece = src.at[:, pl.ds(pl.multiple_of(sj * cc, 128), cc)]
                else:
                    piece = src.at[sj]
                sends.append(_remote(piece, dst.at[j], send_sems, recv_sems, k, to))
                recvs.append(sends[-1])
    return sends, recvs


def _slab_shapes(items=ALL_ITEMS):
    return [_sds((4, *SHARD_ITEMS[t][1]), CDT) for t in items]


def _dma_sems(n):
    return [pltpu.SemaphoreType.DMA((n,)), pltpu.SemaphoreType.DMA((n,))]


def _forward_sends(dsts, send_sems, recv_sems, items=ALL_ITEMS):
    x, y, c, chips = _place()
    sends, recvs = [], []
    for t, (item, dst) in enumerate(zip(items, dsts)):
        r = SHARD_ITEMS[item][1][0]
        for j, (cx, cy) in enumerate(chips):
            sj = 2 * cx + cy
            hs, ho = _half_rows(c, r), _half_rows(1 - c, r)
            sends.append(_remote(dst.at[sj, hs], dst.at[sj, hs], send_sems, recv_sems, 3 * t + j, (x, y, 1 - c)))
            recvs.append(_remote(dst.at[sj, ho], dst.at[sj, ho], send_sems, recv_sems, 3 * t + j, (x, y, 1 - c)))
    return sends, recvs


def _gather_layer(wb, mflat, layer, name, items):
    nt = len(items)

    def body(*refs):
        srcs, m_r, dsts, mall_o = refs[:nt], refs[nt], refs[nt + 1:2 * nt + 1], refs[2 * nt + 1]
        send_sems, recv_sems, fsend, frecv, msend, mrecv = refs[2 * nt + 2:]
        x, y, c, chips = _place()
        s = 2 * x + y
        sends, recvs = _ici_copies("gather", srcs, dsts, send_sems, recv_sems, layer, items=items)
        metas = [_remote(m_r, mall_o.at[s], msend, mrecv, j, (cx, cy, c)) for j, (cx, cy) in enumerate(chips)]
        for cp in sends + metas:
            cp.start()
        fwds, frecvs = _forward_sends(dsts, fsend, frecv, items)
        for got, fwd in zip(recvs, fwds):
            got.wait_recv()
            fwd.start()
        for got in frecvs:
            got.wait_recv()
        for j, (cx, cy) in enumerate(chips):
            _remote(m_r, mall_o.at[2 * cx + cy], msend, mrecv, j, (cx, cy, c)).wait_recv()
        for cp in sends + metas + fwds:
            cp.wait_send()

    return pl.pallas_call(
        body, name=name, out_shape=_slab_shapes(items) + [_sds((4, META_ROWS, 128), F32)],
        in_specs=[ANY] * (nt + 1), out_specs=[ANY] * (nt + 1),
        scratch_shapes=_dma_sems(3 * nt) + _dma_sems(3 * nt) + _dma_sems(3),
    )(*wb, mflat)


def _forward_layer(slabs, name, items=ALL_ITEMS):
    nt = len(items)

    def body(*refs):
        ins, outs, send_sems, recv_sems = refs[:nt], refs[nt:2 * nt], refs[2 * nt], refs[2 * nt + 1]
        sends, recvs = _forward_sends(outs, send_sems, recv_sems, items)
        for cp in sends:
            cp.start()
        for cp in recvs:
            cp.wait_recv()
        for cp in sends:
            cp.wait_send()

    return pl.pallas_call(
        body, name=name, out_shape=_slab_shapes(items), in_specs=[ANY] * nt, out_specs=[ANY] * nt,
        input_output_aliases={t: t for t in range(nt)}, scratch_shapes=_dma_sems(3 * nt),
    )(*slabs)


def _half_shape(nm, r, c):
    return (r // 2, 4 * c) if GRAD_FORM[nm] == "col" else (4, r // 2, c)


def _swap_layer(gs, gsm, name, items=ALL_ITEMS):
    small = gsm is not None
    nt = len(items)

    def body(*refs):
        g_rs = refs[:nt]
        pos = nt
        if small:
            s_r = refs[pos]
            pos += 1
        got_os = refs[pos:pos + nt]
        pos += nt
        if small:
            slots_o = refs[pos]
            pos += 1
        send_sems, recv_sems = refs[pos], refs[pos + 1]
        x, y, c, _ = _place()
        sib = (x, y, 1 - c)
        sent = []
        for t, (item, g_r, got_o) in enumerate(zip(items, g_rs, got_os)):
            nm, (r, cc), _ = SHARD_ITEMS[item]
            ho = _half_rows(1 - c, r)
            src = g_r.at[ho, :] if GRAD_FORM[nm] == "col" else g_r.at[:, ho, :]
            sent.append(_remote(src, got_o, send_sems, recv_sems, t, sib))
        if small:
            ssend, srecv, loc_sem = refs[pos + 2], refs[pos + 3], refs[pos + 4]
            me = 4 * x + 2 * y + c
            loc = pltpu.make_async_copy(s_r, slots_o.at[me], loc_sem.at[0])
            loc.start()
            peers = [(x ^ (k >> 2), y ^ ((k >> 1) & 1), c ^ (k & 1)) for k in range(1, 8)]
            for k, peer in enumerate(peers):
                sent.append(_remote(s_r, slots_o.at[me], ssend, srecv, k, peer))
        for cp in sent:
            cp.start()
        for cp in sent[:nt]:
            cp.wait_recv()
        if small:
            for k, (px, py, pc) in enumerate(peers):
                _remote(s_r, slots_o.at[4 * px + 2 * py + pc], ssend, srecv, k, (px, py, pc)).wait_recv()
        for cp in sent:
            cp.wait_send()
        if small:
            loc.wait()

    outs = [_sds(_half_shape(*SHARD_ITEMS[item][0:1], *SHARD_ITEMS[item][1]), CDT) for item in items]
    ops = list(gs)
    sems = _dma_sems(nt)
    if small:
        outs.append(_sds((8, SMALL_ROWS, 128), F32))
        ops.append(gsm)
        sems = sems + _dma_sems(7) + [pltpu.SemaphoreType.DMA((1,))]
    res = pl.pallas_call(
        body, name=name, out_shape=outs, in_specs=[ANY] * len(ops), out_specs=[ANY] * len(outs), scratch_shapes=sems,
    )(*ops)
    return (res[:nt], res[nt]) if small else (res, None)


def _pair_add_t(own, got, half_idx, nm, r, name):
    tr = SUM_TILE[r]
    nb = (r // 2) // tr
    if GRAD_FORM[nm] == "col":
        blk = (tr, own.shape[1])
        own_spec = pl.BlockSpec(blk, lambda i, c_r: (c_r[0] * nb + i, 0))
        half_spec = pl.BlockSpec(blk, lambda i, c_r: (i, 0))
    else:
        blk = (4, tr, own.shape[2])
        own_spec = pl.BlockSpec(blk, lambda i, c_r: (0, c_r[0] * nb + i, 0))
        half_spec = pl.BlockSpec(blk, lambda i, c_r: (0, i, 0))

    def body(c_r, a_r, b_r, o_o):
        o_o[...] = (a_r[...].astype(F32) + b_r[...].astype(F32)).astype(CDT)

    grid_spec = pltpu.PrefetchScalarGridSpec(num_scalar_prefetch=1, grid=(nb,), in_specs=[own_spec, half_spec],
                                             out_specs=half_spec)
    return pl.pallas_call(body, name=name, grid_spec=grid_spec, out_shape=_sds(got.shape, CDT),
                          compiler_params=_params(("parallel",)))(half_idx, own, got)


def _sum4_t(ps, got3, buf, idx, layer, nm, r, name):
    tr = SUM_TILE[r]
    nb = (r // 2) // tr
    c = got3.shape[2]
    if GRAD_FORM[nm] == "col":
        ps_spec = pl.BlockSpec((tr, c), lambda i, x_r: (i, x_r[0]))
    else:
        ps_spec = pl.BlockSpec((None, tr, c), lambda i, x_r: (x_r[0], i, 0))

    def body(x_r, a_r, b_r, buf_r, o_o):
        o_o[...] = ((a_r[...].astype(F32) + b_r[0].astype(F32)) + b_r[1].astype(F32)) + b_r[2].astype(F32)

    grid_spec = pltpu.PrefetchScalarGridSpec(
        num_scalar_prefetch=1, grid=(nb,),
        in_specs=[ps_spec, pl.BlockSpec((3, tr, c), lambda i, x_r: (0, i, 0)), ANY],
        out_specs=pl.BlockSpec((None, tr, c), lambda i, x_r: (layer, x_r[1] * nb + i, 0)),
    )
    return pl.pallas_call(body, name=name, grid_spec=grid_spec, out_shape=_sds(buf.shape, F32),
                          input_output_aliases={3: 0}, compiler_params=_params(("parallel",)))(idx, ps, got3, buf)


def _scatter_layer(ps, name, items=ALL_ITEMS):
    nt = len(items)

    def body(*refs):
        srcs, dsts, send_sems, recv_sems = refs[:nt], refs[nt:2 * nt], refs[2 * nt], refs[2 * nt + 1]
        sends, recvs = _ici_copies("scatter", srcs, dsts, send_sems, recv_sems, None, items=items)
        for cp in sends:
            cp.start()
        for cp in recvs:
            cp.wait_recv()
        for cp in sends:
            cp.wait_send()

    return pl.pallas_call(
        body, name=name, out_shape=_got3_shapes(items), in_specs=[ANY] * nt, out_specs=[ANY] * nt,
        scratch_shapes=_dma_sems(3 * nt),
    )(*ps)


def _got3_shapes(items=ALL_ITEMS):
    return [_sds((3, SHARD_ITEMS[t][1][0] // 2, SHARD_ITEMS[t][1][1]), CDT) for t in items]


def _join_layer(bufs, name):
    def body(*refs):
        ins, outs, send_sems, recv_sems = refs[:NT], refs[NT:2 * NT], refs[2 * NT], refs[2 * NT + 1]
        x, y, c, _ = _place()
        sent = []
        for t, ((nm, (r, cc), _), b_o) in enumerate(zip(SHARD_ITEMS, outs)):
            hs = _half_rows(c, r)
            sent.append(_remote(b_o.at[:, hs, :], b_o.at[:, hs, :], send_sems, recv_sems, t, (x, y, 1 - c)))
        for cp in sent:
            cp.start()
        for t, ((nm, (r, cc), _), b_o) in enumerate(zip(SHARD_ITEMS, outs)):
            ho = _half_rows(1 - c, r)
            _remote(b_o.at[:, ho, :], b_o.at[:, ho, :], send_sems, recv_sems, t, (x, y, 1 - c)).wait_recv()
        for cp in sent:
            cp.wait_send()

    return pl.pallas_call(
        body, name=name, out_shape=[_sds(b.shape, F32) for b in bufs], in_specs=[ANY] * NT, out_specs=[ANY] * NT,
        input_output_aliases={t: t for t in range(NT)}, scratch_shapes=_dma_sems(NT),
    )(*bufs)


def _adamw3(w, g, m, v, name):
    nl, r, c = w.shape
    tr = SUM_TILE.get(r, r)
    if r % 8:
        blk = pl.BlockSpec((None, r, 256), lambda l, i: (l, 0, i))
        steps = c // 256
    else:
        blk = pl.BlockSpec((None, tr, c), lambda l, i: (l, i, 0))
        steps = r // tr

    def body(w_r, g_r, m_r, v_r, d_o, m_o, v_o):
        g_ = g_r[...]
        m_ = ADAM_B1 * m_r[...] + (1.0 - ADAM_B1) * g_
        v_ = ADAM_B2 * v_r[...] + (1.0 - ADAM_B2) * jnp.square(g_)
        m_hat = m_ / (1.0 - ADAM_B1 ** ADAM_STEP)
        v_hat = v_ / (1.0 - ADAM_B2 ** ADAM_STEP)
        d_o[...] = -ADAM_LR * (m_hat / (jnp.sqrt(v_hat) + ADAM_EPS) + ADAM_WD * w_r[...])
        m_o[...] = m_
        v_o[...] = v_

    return pl.pallas_call(
        body, name=name, grid=(nl, steps),
        in_specs=[blk] * 4, out_specs=[blk] * 3, out_shape=[_sds((nl, r, c), F32)] * 3,
        compiler_params=_params(("parallel", "parallel")),
    )(w, g, m, v)


def _full_weights(slabs, wb, layer, shard, items=ALL_ITEMS):
    ws = {}
    for t, slab in zip(items, slabs):
        nm, (r, c), kind = SHARD_ITEMS[t]
        slab = lax.dynamic_update_slice(slab, wb[nm][layer][None], (shard, 0, 0))
        ws[nm] = slab.reshape(4 * r, c) if kind == "row" else jnp.concatenate([slab[s] for s in range(4)], axis=1)
    return ws


def _exchange_forms(g, items=ALL_ITEMS):
    out = []
    for t in items:
        nm, (r, c), _ = SHARD_ITEMS[t]
        a = g[nm]
        if nm == "w_in":
            a = a.reshape(D, 4, c).transpose(1, 0, 2)
        elif GRAD_FORM[nm] == "3d":
            a = a.reshape(4, r, c)
        out.append(a)
    return out


SMALL_ITEMS = (("rel_bias_table", 2), ("ffn1_norm", 16), ("mix_norm", 16), ("ffn2_norm", 16), ("forget_bias", 1),
               ("fox_q_norm", 1), ("fox_k_norm", 1), ("swa_q_norm", 1), ("swa_k_norm", 1), ("swa_sinks", 1))
SMALL_ADAM_ROWS = 96


def _layer_fwd(h, lw, l, ride=None, late=None):
    rides = late["rides"] if late else {}

    def run(key, fn, *args):
        r = rides.get(key)
        if r is None:
            return fn(*args)
        out = fn(*args, ride=r)
        late["arrived"](key, out[-1])
        return out[0] if len(out) == 2 else out[:-1]

    sv = {"h0": h}
    a, sv["a1t"] = _rms_fwd(h, lw["ffn1_norm"], f"rms_fwd_a{l}")
    sv["gu1"], s, sv["s1t"] = run("ffn_in_a", _ffn_in, a, lw["ffn1_w_in"], f"ffn_in_a{l}")
    h = run("ffn_out_a", _mm_res, s, lw["ffn1_w_out"], h, 0.5, f"ffn_out_a{l}")
    sv["h1"] = h
    a, sv["amt"] = _rms_fwd(h, lw["mix_norm"], f"rms_fwd_m{l}")
    if late:
        late["need"](lw, "mixer")
    proj = run("proj", _mm, a, lw["w_mix"], F32, _row_tile(h.shape[0]), DP, f"proj{l}")
    sv["proj"] = proj
    qf, kf, vf, qs, kse, vse, c, ct, sv["qft"] = _qknorm_fwd(proj, lw["gfq"], lw["gfk"], lw["gsq"], lw["gsk"], lw["fb"],
                                                              f"qknorm_fwd{l}")
    ofox, lse_f, *rode = _fox_fwd(qf, kf, vf, c, ct, f"fox_fwd{l}", ride)
    oswa, lse_s = run("swa_fwd", _swa_fwd, qs, kse, vse, lw["bias"], lw["sinks"], f"swa_fwd{l}")
    if late:
        late["need"](lw, "gate")
    sv.update(qf=qf, kf=kf, vf=vf, qs=qs, kse=kse, vse=vse, c=c, ct=ct, ofox=ofox, oswa=oswa, lse_f=lse_f, lse_s=lse_s)
    h, sv["yt"], sv["pf"], sv["ps"], sv["oft"], sv["ost"] = _gate_out_fwd(
        ofox, oswa, lw["w_branch_fox"], lw["w_branch_swa"], proj, lw["w_out"], h, f"gate_out_fwd{l}")
    sv["h2"] = h
    a, sv["a2t"] = _rms_fwd(h, lw["ffn2_norm"], f"rms_fwd_b{l}")
    sv["gu2"], s, sv["s2t"] = _ffn_in(a, lw["ffn2_w_in"], f"ffn_in_b{l}")
    h = _mm_res(s, lw["ffn2_w_out"], h, 0.5, f"ffn_out_b{l}")
    return h, sv, rode


def _ffn_bwd(dh, dhb, h_in, at, gu, st, norm, w_in, w_out, tag, rides=None):
    r = rides or (None,) * 4
    rode = []

    def split(res, ride):
        if ride is None:
            return res
        rode.extend(res[-1])
        return res[0] if len(res) == 2 else res[:-1]

    dgu = split(_ffn_bwd_mid(dhb, w_out, gu, f"ffn_bwd_mid_{tag}", r[0]), r[0])
    d_w_out = split(_mm(st, dhb, CDT, 256, D, f"dw_ffn_out_{tag}", scale=0.5, ride=r[1]), r[1])
    dh, dhb, dg = split(_ffn_bwd_in(dgu, w_in, h_in, norm, dh, f"ffn_bwd_in_{tag}", r[2]), r[2])
    d_w_in = split(_mm(at, dgu, CDT, D, 256, f"dw_ffn_in_{tag}", ride=r[3]), r[3])
    return dh, dhb, d_w_out, d_w_in, dg, rode


def _layer_bwd(dh, dhb, sv, lw, l, ride=None, before_ffn1=None):
    g = {}
    dh, dhb, g["ffn2_w_out"], g["ffn2_w_in"], g["ffn2_norm"], _ = _ffn_bwd(
        dh, dhb, sv["h2"], sv["a2t"], sv["gu2"], sv["s2t"], lw["ffn2_norm"], lw["ffn2_w_in"], lw["ffn2_w_out"], f"b{l}")
    g["w_out"] = _mm(sv["yt"], dhb, CDT, 512, 512, f"dw_out{l}")
    dpf, dps, dga, dgb, do_f, do_ft, do_s = _gate_out_bwd(dhb, lw["w_out"], sv["pf"], sv["ps"], sv["proj"],
                                                           lw["w_branch_fox"], lw["w_branch_swa"], f"gate_out_bwd{l}")
    g["w_branch_fox"] = _mm(sv["oft"], dpf, CDT, 512, 512, f"dw_bfox{l}")
    g["w_branch_swa"] = _mm(sv["ost"], dps, CDT, 512, 512, f"dw_bswa{l}")
    dqf, dcq, dkf, dvf, dck, *rode = _fox_bwd(sv["qf"], sv["qft"], sv["kf"], sv["vf"], sv["c"], sv["ct"], sv["ofox"],
                                              sv["lse_f"], do_f, do_ft, f"fox_bwd{l}", ride)
    g["rode"] = rode
    dqs, dkse, dvse, dbias, dsk = _swa_bwd(sv["qs"], sv["kse"], sv["vse"], lw["bias"], lw["sinks"], sv["oswa"],
                                           sv["lse_s"], do_s, f"swa_bwd{l}")
    dproj, dgn = _qknorm_bwd(sv["proj"], dqf, dkf, dvf, dqs, dkse, dvse, dcq, dck, dga, dgb,
                             lw["gfq"], lw["gfk"], lw["gsq"], lw["gsk"], lw["fb"], f"qknorm_bwd{l}")
    g["w_mix"] = _mm(sv["amt"], dproj, CDT, D, 640, f"dw_mix{l}")
    dh, dhb, g["mix_norm"] = _mm_nt_rms(dproj, lw["w_mix"], sv["h1"], lw["mix_norm"], dh, f"d_am{l}")
    g["dbias"], g["dsk"], g["dgn"] = dbias, dsk, dgn
    rides = before_ffn1(g) if before_ffn1 else None
    dh, dhb, g["ffn1_w_out"], g["ffn1_w_in"], g["ffn1_norm"], g["rode_ffn1"] = _ffn_bwd(
        dh, dhb, sv["h0"], sv["a1t"], sv["gu1"], sv["s1t"], lw["ffn1_norm"], lw["ffn1_w_in"], lw["ffn1_w_out"], f"a{l}",
        rides)
    return dh, dhb, g


def kernel(x, meta_tokens, rel_bias_table, ffn1_norm, ffn1_w_in, ffn1_w_out, mix_norm, w_in, forget_bias, fox_q_norm, fox_k_norm, swa_q_norm, swa_k_norm, swa_sinks, w_branch_fox, w_branch_swa, w_out, ffn2_norm, ffn2_w_in, ffn2_w_out, loss_target, m_meta_tokens, m_rel_bias_table, m_ffn1_norm, m_ffn1_w_in, m_ffn1_w_out, m_mix_norm, m_w_in, m_forget_bias, m_fox_q_norm, m_fox_k_norm, m_swa_q_norm, m_swa_k_norm, m_swa_sinks, m_w_branch_fox, m_w_branch_swa, m_w_out, m_ffn2_norm, m_ffn2_w_in, m_ffn2_w_out, v_meta_tokens, v_rel_bias_table, v_ffn1_norm, v_ffn1_w_in, v_ffn1_w_out, v_mix_norm, v_w_in, v_forget_bias, v_fox_q_norm, v_fox_k_norm, v_swa_q_norm, v_swa_k_norm, v_swa_sinks, v_w_branch_fox, v_w_branch_swa, v_w_out, v_ffn2_norm, v_ffn2_w_in, v_ffn2_w_out):
    names = ["meta_tokens", "rel_bias_table", "ffn1_norm", "ffn1_w_in", "ffn1_w_out", "mix_norm", "w_in", "forget_bias",
             "fox_q_norm", "fox_k_norm", "swa_q_norm", "swa_k_norm", "swa_sinks", "w_branch_fox", "w_branch_swa", "w_out",
             "ffn2_norm", "ffn2_w_in", "ffn2_w_out"]
    w = dict(zip(names, [meta_tokens, rel_bias_table, ffn1_norm, ffn1_w_in, ffn1_w_out, mix_norm, w_in, forget_bias,
                         fox_q_norm, fox_k_norm, swa_q_norm, swa_k_norm, swa_sinks, w_branch_fox, w_branch_swa, w_out,
                         ffn2_norm, ffn2_w_in, ffn2_w_out]))
    m = dict(zip(names, [m_meta_tokens, m_rel_bias_table, m_ffn1_norm, m_ffn1_w_in, m_ffn1_w_out, m_mix_norm, m_w_in,
                         m_forget_bias, m_fox_q_norm, m_fox_k_norm, m_swa_q_norm, m_swa_k_norm, m_swa_sinks,
                         m_w_branch_fox, m_w_branch_swa, m_w_out, m_ffn2_norm, m_ffn2_w_in, m_ffn2_w_out]))
    v = dict(zip(names, [v_meta_tokens, v_rel_bias_table, v_ffn1_norm, v_ffn1_w_in, v_ffn1_w_out, v_mix_norm, v_w_in,
                         v_forget_bias, v_fox_q_norm, v_fox_k_norm, v_swa_q_norm, v_swa_k_norm, v_swa_sinks,
                         v_w_branch_fox, v_w_branch_swa, v_w_out, v_ffn2_norm, v_ffn2_w_in, v_ffn2_w_out]))
    xi, yi, ci = lax.axis_index("x"), lax.axis_index("y"), lax.axis_index("c")
    shard = 2 * xi + yi
    seq = x.shape[1]
    t = seq + BLK

    wb = {nm: w[nm].astype(CDT) for nm, _, _ in SHARD_ITEMS}
    wb_list = [wb[nm] for nm, _, _ in SHARD_ITEMS]
    mflat = meta_tokens.reshape(META_ROWS, 128)
    first = (0, 1)
    *slabs_first, mall = _gather_layer([wb_list[t] for t in first], mflat, 0, "gather_weights", first)
    mall = lax.dynamic_update_slice(mall, mflat[None], (shard, 0, 0))
    meta_full = jnp.concatenate([mall[s].reshape(N_META, 256) for s in range(4)], axis=1)
    bias = _bias_fwd(rel_bias_table, "bias_fwd")

    def layer_weights(slabs, l, items=ALL_ITEMS):
        lw = _full_weights(slabs, wb, l, shard, items)
        if "w_in" in lw:
            lw["w_mix"] = _mix_cols(lw.pop("w_in"))
        return lw

    def layer_vectors(l):
        lw = {nm: w[nm][l].reshape(1, D) for nm in ("ffn1_norm", "mix_norm", "ffn2_norm")}
        lw["gfq"] = jnp.tile(fox_q_norm[l], 8).reshape(1, 512)
        lw["gfk"] = jnp.tile(fox_k_norm[l], 8).reshape(1, 512)
        lw["gsq"] = jnp.tile(swa_q_norm[l], 8).reshape(1, 512)
        lw["gsk"] = jnp.tile(swa_k_norm[l], 2).reshape(1, 128)
        lw["fb"] = jnp.pad(forget_bias[l], (0, 120)).reshape(1, 128)
        lw["sinks"] = swa_sinks[l]
        lw["bias"] = bias
        return lw

    def gather_ride(layer, items):
        return ("gather", [wb_list[t] for t in items], _slab_shapes(items), layer, items)

    landed = {}

    def need(lw, stage):
        if stage == "mixer":
            items = (2,)
            slabs = _forward_layer(landed["ffn_in_a"], "forward_halves0m", items)
        else:
            items = (3, 4, 5, 6, 7)
            slabs = _forward_layer(landed["ffn_out_a"] + landed["proj"] + landed["swa_fwd"], "forward_halves0g", items)
        lw.update(layer_weights(slabs, 0, items))

    late = {"rides": {"ffn_in_a": gather_ride(0, (2,)), "ffn_out_a": gather_ride(0, (3, 4, 5)),
                      "proj": gather_ride(0, (6,)), "swa_fwd": gather_ride(0, (7,))},
            "arrived": landed.__setitem__, "need": need}

    h = jnp.concatenate([jnp.zeros((PAD, D), F32), meta_full, x[0]], axis=0)
    lws = [{**layer_vectors(0), **layer_weights(slabs_first, 0, first)}]
    h, sv0, slabs1 = _layer_fwd(h, lws[0], 0, gather_ride(1, ALL_ITEMS), late)
    lws.append({**layer_vectors(1), **layer_weights(_forward_layer(slabs1, "forward_halves"), 1)})
    h, sv1, _ = _layer_fwd(h, lws[1], 1)
    saved = [sv0, sv1]
    dh, dhb, lacc = _loss(h, loss_target[0], "loss")
    loss = lax.psum(lacc[0, 0], ("x", "y", "c"))

    half_idx = ci.reshape(1).astype(jnp.int32)
    place_idx = jnp.stack([shard, ci]).astype(jnp.int32)

    def pair_sums(g, gsm, tag, items=ALL_ITEMS):
        if "w_mix" in g:
            g["w_in"] = _unmix_cols(g.pop("w_mix"))
        forms = _exchange_forms(g, items)
        got, slots = _swap_layer(forms, gsm, f"swap_halves{tag}", items)
        return {t: _pair_add_t(a, b, half_idx, SHARD_ITEMS[t][0], SHARD_ITEMS[t][1][0],
                               f"pair_add{tag}_{SHARD_ITEMS[t][0]}")
                for t, a, b in zip(items, forms, got)}, slots

    def scatter_ride(ps, items):
        return ("scatter", [ps[t] for t in items], _got3_shapes(items), None, items)

    early = (2, 3, 4, 5, 6, 7)
    early_rides = ((6,), (7,), (2, 5), (3, 4))
    ps0 = {}

    def before_ffn1(g):
        ps0.update(pair_sums(g, None, "0e", early)[0])
        return [scatter_ride(ps0, items) for items in early_rides]

    grads = [None, None]
    dh, dhb, grads[1] = _layer_bwd(dh, dhb, saved[1], lws[1], 1)
    ps1, _ = pair_sums(grads[1], None, 1)
    dh, dhb, grads[0] = _layer_bwd(dh, dhb, saved[0], lws[0], 0, scatter_ride(ps1, ALL_ITEMS), before_ffn1)
    grad_x = dh[BLK:].reshape(1, seq, D)
    dtab = _bias_bwd(grads[0]["dbias"] + grads[1]["dbias"], "bias_bwd")

    small = [dh[PAD:BLK].reshape(128, 128), _rows128(dtab[:, :N_BUCKETS].T, 2)]
    for nm in ("ffn1_norm", "mix_norm", "ffn2_norm"):
        small.append(jnp.stack([grads[l][nm][0] for l in range(2)]).reshape(16, 128))
    small.append(_rows128(jnp.stack([grads[l]["dgn"][4, :8] for l in range(2)]), 1))
    for row in range(4):
        small.append(jnp.stack([grads[l]["dgn"][row, :HD] for l in range(2)]).reshape(1, 128))
    dsk = [grads[l]["dsk"][:, 0, :] for l in range(2)]
    small.append(_rows128(jnp.stack([jnp.stack([d[:, 0], d[:, HD]], axis=1).reshape(8) for d in dsk]), 1))
    gsm = jnp.concatenate(small, axis=0)
    gsm = jnp.pad(gsm, ((0, SMALL_ROWS - gsm.shape[0]), (0, 0)))

    late = (0, 1)
    ps_late, slots = pair_sums(grads[0], gsm, "0l", late)
    ps0.update(ps_late)
    got3_0 = dict(zip([t for items in early_rides for t in items], grads[0]["rode_ffn1"]))
    got3_0.update(zip(late, _scatter_layer([ps0[t] for t in late], "scatter_shards", late)))
    got3 = [got3_0, dict(zip(ALL_ITEMS, grads[0]["rode"]))]
    bufs = []
    for t, (nm, (r, c), _) in enumerate(SHARD_ITEMS):
        buf = lax.empty((2, r, c), F32)
        for l, ps in ((1, ps1), (0, ps0)):
            buf = _sum4_t(ps[t], got3[l][t], buf, place_idx, l, nm, r, f"sum4_{l}_{nm}")
        bufs.append(buf)
    bufs = _join_layer(bufs, "join_halves")
    gs = _sum8(slots, "sum8")

    g_out = {nm: buf for (nm, _, _), buf in zip(SHARD_ITEMS, bufs)}
    g_out["meta_tokens"] = lax.dynamic_slice(gs[0:128].reshape(N_META, D), (0, shard * 256), (N_META, 256))
    off = 128
    for nm, rows in SMALL_ITEMS:
        n = w[nm].size
        g_out[nm] = gs[off:off + rows].reshape(-1)[:n].reshape(w[nm].shape)
        off += rows

    delta, new_m, new_v = {}, {}, {}
    for nm, _, _ in SHARD_ITEMS:
        if nm == "w_in":
            tr_ = lambda a: jnp.swapaxes(a, 1, 2)
            delta[nm], new_m[nm], new_v[nm] = (tr_(a) for a in _adamw3(tr_(w[nm]), tr_(g_out[nm]), tr_(m[nm]), tr_(v[nm]),
                                                                        f"adamw_{nm}"))
        else:
            delta[nm], new_m[nm], new_v[nm] = _adamw3(w[nm], g_out[nm], m[nm], v[nm], f"adamw_{nm}")
    small_names = ["meta_tokens"] + [nm for nm, _ in SMALL_ITEMS]
    small_rows = [META_ROWS] + [rows for _, rows in SMALL_ITEMS]

    def pack_small(src):
        buf = jnp.concatenate([_rows128(src[nm], rows) for nm, rows in zip(small_names, small_rows)], axis=0)
        return jnp.pad(buf, ((0, SMALL_ADAM_ROWS - buf.shape[0]), (0, 0)))

    d_, m_, v_ = (a[0] for a in _adamw3(pack_small(w)[None], pack_small(g_out)[None], pack_small(m)[None],
                                        pack_small(v)[None], "adamw_small"))
    off = 0
    for nm, rows in zip(small_names, small_rows):
        n = w[nm].size
        for dst, src in ((delta, d_), (new_m, m_), (new_v, v_)):
            dst[nm] = src[off:off + rows].reshape(-1)[:n].reshape(w[nm].shape)
        off += rows

    return (loss, grad_x, *[g_out[n] for n in names], *[delta[n] for n in names],
            *[new_m[n] for n in names], *[new_v[n] for n in names])
```

```python
import math

import numpy as np
import jax
import jax.numpy as jnp
from jax import lax
from jax.experimental import pallas as pl
from jax.experimental.pallas import tpu as pltpu

D = 1024
F = 2816
FT = F // 2
HD = 64
NPAIR = 4
N_META = 16
BLK = 128
PAD = BLK - N_META
EPS = 1e-6
NEG = -1e30
N_BUCKETS = 32
GA, GB, QA, KA, VA, QB, KB, VB, FA, DP = 0, 1024, 2048, 2560, 3072, 3584, 4096, 4224, 4352, 4480
D_IN = 4360
CDT = jnp.bfloat16
F32 = jnp.float32
VMEM_LIMIT = 48 * 1024 * 1024
MESH_ID = pl.DeviceIdType.MESH

ADAM_LR, ADAM_B1, ADAM_B2, ADAM_EPS, ADAM_WD, ADAM_STEP = 0.001, 0.9, 0.999, 1e-08, 0.01, 10

SHARD_ITEMS = (
    ("ffn1_w_in", (1024, 1408), "col"),
    ("ffn1_w_out", (704, 1024), "row"),
    ("w_in", (1024, 1090), "col"),
    ("w_branch_fox", (512, 256), "col"),
    ("w_branch_swa", (512, 256), "col"),
    ("w_out", (256, 1024), "row"),
    ("ffn2_w_in", (1024, 1408), "col"),
    ("ffn2_w_out", (704, 1024), "row"),
)
SMALL_ROWS = 192
META_ROWS = 32


def _row_tile(t):
    return 384 if t % 384 == 0 else 128


def _dot(a, b):
    return jnp.dot(a, b, preferred_element_type=F32)


def _dot_nt(a, b):
    return lax.dot_general(a, b, (((1,), (1,)), ((), ())), preferred_element_type=F32)


def _dot_hi(a, b):
    return jnp.dot(a, b, preferred_element_type=F32, precision=lax.Precision.HIGHEST)


def _sigmoid(x):
    return 0.5 * jnp.tanh(0.5 * x) + 0.5


def _iota(shape, dim):
    return lax.broadcasted_iota(jnp.int32, shape, dim)


def _params(sem=None):
    return pltpu.CompilerParams(dimension_semantics=sem, vmem_limit_bytes=VMEM_LIMIT)


def _sds(shape, dtype):
    return jax.ShapeDtypeStruct(shape, dtype)


def _rms_fwd(h, g, name):
    t = h.shape[0]
    tm = _row_tile(t)

    def body(h_ref, g_ref, a_ref, at_ref):
        x = h_ref[...]
        ms = jnp.mean(x * x, axis=-1, keepdims=True)
        a = x * lax.rsqrt(ms + EPS) * g_ref[...]
        a_ref[...] = a.astype(CDT)
        at_ref[...] = a.T.astype(CDT)

    return pl.pallas_call(
        body, name=name, grid=(t // tm,),
        in_specs=[pl.BlockSpec((tm, D), lambda i: (i, 0)), pl.BlockSpec((1, D), lambda i: (0, 0))],
        out_specs=[pl.BlockSpec((tm, D), lambda i: (i, 0)), pl.BlockSpec((D, tm), lambda i: (0, i))],
        out_shape=[_sds((t, D), CDT), _sds((D, t), CDT)],
        compiler_params=_params(("parallel",)),
    )(h, g)


def _ffn_in(a, w_in, name, ride=None):
    t = a.shape[0]
    tm = _row_tile(t)
    tn = FT
    nj = F // tn
    grid = (nj, t // tm)
    ride_in, ride_in_specs, ride_out, ride_out_specs, ride_sems = _ride_specs(ride)

    def body(a_ref, wg_ref, wu_ref, gu_ref, s_ref, st_ref):
        a_ = a_ref[...]
        g = _dot(a_, wg_ref[...])
        u = _dot(a_, wu_ref[...])
        s = g * _sigmoid(g) * u
        gu_ref[0] = g.astype(CDT)
        gu_ref[1] = u.astype(CDT)
        s_ref[...] = s.astype(CDT)
        st_ref[...] = s.T.astype(CDT)

    res = pl.pallas_call(
        _riding(body, 3, 3, ride, grid), name=name, grid=grid,
        in_specs=[pl.BlockSpec((tm, D), lambda j, i: (i, 0)),
                  pl.BlockSpec((D, tn), lambda j, i: (0, j)),
                  pl.BlockSpec((D, tn), lambda j, i: (0, j + nj))] + ride_in_specs,
        out_specs=[pl.BlockSpec((2, tm, tn), lambda j, i: (0, i, j)),
                   pl.BlockSpec((tm, tn), lambda j, i: (i, j)),
                   pl.BlockSpec((tn, tm), lambda j, i: (j, i))] + ride_out_specs,
        out_shape=[_sds((2, t, F), CDT), _sds((t, F), CDT), _sds((F, t), CDT)] + ride_out, scratch_shapes=ride_sems,
        compiler_params=_params(("arbitrary", "arbitrary") if ride else ("parallel", "parallel")),
    )(a, w_in, w_in, *ride_in)
    return (*res[:3], res[3:]) if ride else res


def _mm_res(a, b, res, scale, name, ride=None):
    t, k = a.shape
    n = b.shape[1]
    tm = _row_tile(t)
    tn = n
    grid = (t // tm, n // tn)
    ride_in, ride_in_specs, ride_out, ride_out_specs, ride_sems = _ride_specs(ride)

    def body(a_ref, b_ref, r_ref, o_ref):
        o_ref[...] = r_ref[...] + scale * _dot(a_ref[...], b_ref[...])

    out = pl.pallas_call(
        _riding(body, 3, 1, ride, grid), name=name, grid=grid,
        in_specs=[pl.BlockSpec((tm, k), lambda i, j: (i, 0)),
                  pl.BlockSpec((k, tn), lambda i, j: (0, j)),
                  pl.BlockSpec((tm, tn), lambda i, j: (i, j))] + ride_in_specs,
        out_specs=[pl.BlockSpec((tm, tn), lambda i, j: (i, j))] + ride_out_specs,
        out_shape=[_sds((t, n), F32)] + ride_out, scratch_shapes=ride_sems,
        compiler_params=_params(("arbitrary", "arbitrary") if ride else ("parallel", "parallel")),
    )(a, b, res, *ride_in)
    return (out[0], out[1:]) if ride else out[0]


def _mm(a, b, out_dtype, tm, tn, name, scale=1.0, ride=None):
    m, k = a.shape
    if b.ndim == 3:
        nh = b.shape[2] // tn
        n = 2 * b.shape[2]
        b_spec = pl.BlockSpec((None, k, tn), lambda i, j: (j // nh, 0, j % nh))
    else:
        n = b.shape[1]
        b_spec = pl.BlockSpec((k, tn), lambda i, j: (0, j))
    grid = (m // tm, n // tn)
    ride_in, ride_in_specs, ride_out, ride_out_specs, ride_sems = _ride_specs(ride)

    def body(a_ref, b_ref, o_ref):
        o_ref[...] = (scale * _dot(a_ref[...], b_ref[...])).astype(out_dtype)

    res = pl.pallas_call(
        _riding(body, 2, 1, ride, grid), name=name, grid=grid,
        in_specs=[pl.BlockSpec((tm, k), lambda i, j: (i, 0)), b_spec] + ride_in_specs,
        out_specs=[pl.BlockSpec((tm, tn), lambda i, j: (i, j))] + ride_out_specs,
        out_shape=[_sds((m, n), out_dtype)] + ride_out, scratch_shapes=ride_sems,
        compiler_params=_params(("arbitrary", "arbitrary") if ride else ("parallel", "parallel")),
    )(a, b, *ride_in)
    return (res[0], res[1:]) if ride else res[0]


def _ffn_bwd_mid(dhb, w_out, gu, name, ride=None):
    t = dhb.shape[0]
    tm = _row_tile(t)
    tn = FT
    grid = (F // tn, t // tm)
    ride_in, ride_in_specs, ride_out, ride_out_specs, ride_sems = _ride_specs(ride)

    def body(dh_ref, w_ref, gu_ref, o_ref):
        ds = _dot_nt(dh_ref[...] * 0.5, w_ref[...])
        g = gu_ref[0].astype(F32)
        u = gu_ref[1].astype(F32)
        sg = _sigmoid(g)
        o_ref[0] = (ds * u * (sg * (1.0 + g * (1.0 - sg)))).astype(CDT)
        o_ref[1] = (ds * (g * sg)).astype(CDT)

    res = pl.pallas_call(
        _riding(body, 3, 1, ride, grid), name=name, grid=grid,
        in_specs=[pl.BlockSpec((tm, D), lambda j, i: (i, 0)),
                  pl.BlockSpec((tn, D), lambda j, i: (j, 0)),
                  pl.BlockSpec((2, tm, tn), lambda j, i: (0, i, j))] + ride_in_specs,
        out_specs=[pl.BlockSpec((2, tm, tn), lambda j, i: (0, i, j))] + ride_out_specs,
        out_shape=[_sds((2, t, F), CDT)] + ride_out, scratch_shapes=ride_sems,
        compiler_params=_params(("arbitrary", "arbitrary") if ride else ("parallel", "parallel")),
    )(dhb, w_out, gu, *ride_in)
    return (res[0], res[1:]) if ride else res[0]


def _rms_bwd_rows(da_, x, g, dres, i, dh_ref, dhb_ref, dg_ref):
    r = lax.rsqrt(jnp.mean(x * x, axis=-1, keepdims=True) + EPS)
    xh = x * r
    day = da_ * g
    dh = dres + r * (day - xh * jnp.mean(day * xh, axis=-1, keepdims=True))
    dh_ref[...] = dh
    dhb_ref[...] = dh.astype(CDT)

    @pl.when(i == 0)
    def _():
        dg_ref[...] = jnp.zeros(dg_ref.shape, F32)

    dg_ref[0:1, :] += jnp.sum(da_ * xh, axis=0, keepdims=True)


def _ffn_bwd_in(dgu, w_in, h, g, dres, name, ride=None):
    t = dgu.shape[1]
    tm = _row_tile(t)
    grid = (t // tm,)
    ride_in, ride_in_specs, ride_out, ride_out_specs, ride_sems = _ride_specs(ride)

    def body(dg_ref, wg_ref, wu_ref, h_ref, g_ref, dr_ref, dh_ref, dhb_ref, dgn_ref):
        da_ = _dot_nt(dg_ref[0], wg_ref[...]) + _dot_nt(dg_ref[1], wu_ref[...])
        _rms_bwd_rows(da_, h_ref[...], g_ref[...], dr_ref[...], pl.program_id(0), dh_ref, dhb_ref, dgn_ref)

    row = pl.BlockSpec((tm, D), lambda i: (i, 0))
    res = pl.pallas_call(
        _riding(body, 6, 3, ride, grid), name=name, grid=grid,
        in_specs=[pl.BlockSpec((2, tm, F), lambda i: (0, i, 0)),
                  pl.BlockSpec((D, F), lambda i: (0, 0)),
                  pl.BlockSpec((D, F), lambda i: (0, 1)),
                  row, pl.BlockSpec((1, D), lambda i: (0, 0)), row] + ride_in_specs,
        out_specs=[row, row, pl.BlockSpec((8, D), lambda i: (0, 0))] + ride_out_specs,
        out_shape=[_sds((t, D), F32), _sds((t, D), CDT), _sds((8, D), F32)] + ride_out, scratch_shapes=ride_sems,
        compiler_params=_params(("arbitrary",)),
    )(dgu, w_in, w_in, h, g, dres, *ride_in)
    return (*res[:3], res[3:]) if ride else res


def _mm_nt_rms(a, b, h, g, dres, name):
    t, n = a.shape
    tm = _row_tile(t)

    def body(a_ref, b_ref, h_ref, g_ref, dr_ref, dh_ref, dhb_ref, dgn_ref):
        da_ = _dot_nt(a_ref[...], b_ref[...])
        _rms_bwd_rows(da_, h_ref[...], g_ref[...], dr_ref[...], pl.program_id(0), dh_ref, dhb_ref, dgn_ref)

    row = pl.BlockSpec((tm, D), lambda i: (i, 0))
    return pl.pallas_call(
        body, name=name, grid=(t // tm,),
        in_specs=[pl.BlockSpec((tm, n), lambda i: (i, 0)), pl.BlockSpec((D, n), lambda i: (0, 0)),
                  row, pl.BlockSpec((1, D), lambda i: (0, 0)), row],
        out_specs=[row, row, pl.BlockSpec((8, D), lambda i: (0, 0))],
        out_shape=[_sds((t, D), F32), _sds((t, D), CDT), _sds((8, D), F32)],
        compiler_params=_params(("arbitrary",)),
    )(a, b, h, g, dres)


def _loss(h, target, name):
    t = h.shape[0]
    ta = _row_tile(t)
    nb = ta // BLK

    def body(h_ref, *refs):
        t_refs, (dh_ref, dhb_ref, l_ref) = refs[:nb], refs[nb:]
        i = pl.program_id(0)

        @pl.when(i == 0)
        def _():
            l_ref[...] = jnp.zeros(l_ref.shape, F32)

        tot = 0.0
        for b in range(nb):
            rows = slice(BLK * b, BLK * (b + 1))
            err = jnp.where(i * nb + b > 0, h_ref[rows, :] - t_refs[b][...], 0.0)
            tot = tot + jnp.sum(err * err)
            d = err * (1.0 / D)
            dh_ref[rows, :] = d
            dhb_ref[rows, :] = d.astype(CDT)
        l_ref[...] += (0.5 / D) * tot

    row = pl.BlockSpec((ta, D), lambda i: (i, 0))
    tspecs = [pl.BlockSpec((BLK, D), lambda i, b=b: (jnp.maximum(i * nb + b - 1, 0), 0)) for b in range(nb)]
    return pl.pallas_call(
        body, name=name, grid=(t // ta,),
        in_specs=[row] + tspecs,
        out_specs=[row, row, pl.BlockSpec((8, 128), lambda i: (0, 0))],
        out_shape=[_sds((t, D), F32), _sds((t, D), CDT), _sds((8, 128), F32)],
        compiler_params=_params(("arbitrary",)),
    )(h, *([target] * nb))


def _block_diag():
    return (_iota((128, 128), 0) // HD == _iota((128, 128), 1) // HD).astype(F32)


def _head_sums(v, bd):
    hi = v.astype(CDT)
    rest = (v - hi.astype(F32)).astype(CDT)
    b = bd.astype(CDT)
    return _dot(hi, b) + _dot(rest, b)


def _dup_halves(x, lo):
    sw = pltpu.roll(x, 64, 1)
    return jnp.where(lo, x, sw), jnp.where(lo, sw, x)


def _qknorm_fwd(proj, gfq, gfk, gsq, gsk, fb, name):
    t = proj.shape[0]
    tm = _row_tile(t)

    def body(qa, ka, va, qb, kb, vb, fa, gfq_r, gfk_r, gsq_r, gsk_r, fb_r,
             qf_o, kf_o, vf_o, qs_o, kse_o, vse_o, c_o, ct_o, qft_o, carry):
        i = pl.program_id(0)
        bd = _block_diag()
        lane = _iota((1, 128), 1)
        lo = lane < HD

        def hnorm(x, g):
            ms = _head_sums(x * x, bd) * (1.0 / HD)
            return x * lax.rsqrt(ms + EPS) * g

        for ch in range(4):
            sl = slice(128 * ch, 128 * (ch + 1))
            qn = hnorm(qa[:, sl], gfq_r[:, sl]) * 0.125
            qf_o[:, sl] = qn.astype(CDT)
            qft_o[sl, :] = qn.T.astype(CDT)
            kf_o[:, sl] = hnorm(ka[:, sl], gfk_r[:, sl]).astype(CDT)
            qs_o[:, sl] = (hnorm(qb[:, sl], gsq_r[:, sl]) * 0.125).astype(CDT)
        vf_o[...] = va[...].astype(CDT)
        k0, k1 = _dup_halves(hnorm(kb[...], gsk_r[...]), lo)
        kse_o[0] = k0.astype(CDT)
        kse_o[1] = k1.astype(CDT)
        v0, v1 = _dup_halves(vb[...], lo)
        vse_o[0] = v0.astype(CDT)
        vse_o[1] = v1.astype(CDT)

        z = fa[...] + fb_r[...]
        lf = jnp.minimum(z, 0.0) - jnp.log(1.0 + jnp.exp(-jnp.abs(z)))
        lf = jnp.where(lane < 8, lf, 0.0)
        ltri = (_iota((tm, tm), 1) <= _iota((tm, tm), 0)).astype(F32)

        @pl.when(i == 0)
        def _():
            carry[...] = jnp.zeros(carry.shape, F32)

        c = _dot_hi(ltri, lf) + carry[0:1, :]
        carry[0:1, :] = c[tm - 1:tm, :]
        c_o[...] = c
        ct_o[...] = c.T[0:8, :]

    def col(width, off):
        return pl.BlockSpec((tm, width), lambda i: (i, off // width))

    def vec(width):
        return pl.BlockSpec((1, width), lambda i: (0, 0))

    return pl.pallas_call(
        body, name=name, grid=(t // tm,),
        in_specs=[col(512, QA), col(512, KA), col(512, VA), col(512, QB), col(128, KB), col(128, VB), col(128, FA),
                  vec(512), vec(512), vec(512), vec(128), vec(128)],
        out_specs=[pl.BlockSpec((tm, 512), lambda i: (i, 0))] * 4
        + [pl.BlockSpec((2, tm, 128), lambda i: (0, i, 0))] * 2
        + [pl.BlockSpec((tm, 128), lambda i: (i, 0)), pl.BlockSpec((8, tm), lambda i: (0, i)),
           pl.BlockSpec((512, tm), lambda i: (0, i))],
        out_shape=[_sds((t, 512), CDT)] * 4 + [_sds((2, t, 128), CDT)] * 2
        + [_sds((t, 128), F32), _sds((8, t), F32), _sds((512, t), CDT)],
        scratch_shapes=[pltpu.VMEM((8, 128), F32)],
        compiler_params=_params(("arbitrary",)),
    )(proj, proj, proj, proj, proj, proj, proj, gfq, gfk, gsq, gsk, fb)


def _qknorm_bwd(proj, dqf, dkf, dvf, dqs, dkse, dvse, dcq, dck, dga, dgb, gfq, gfk, gsq, gsk, fb, name):
    t = proj.shape[0]
    tm = _row_tile(t)
    nt = t // tm

    def body(qa, ka, qb, kb, fa, dqf_r, dkf_r, dvf_r, dqs_r, dkse_r, dvse_r, dcq_r, dck_r, dga_r, dgb_r,
             gfq_r, gfk_r, gsq_r, gsk_r, fb_r, dp_o, dgn_o, carry, acc):
        i = pl.program_id(0)
        bd = _block_diag()
        lane = _iota((1, 128), 1)
        lo = lane < HD

        @pl.when(i == 0)
        def _():
            carry[...] = jnp.zeros(carry.shape, F32)
            acc[...] = jnp.zeros(acc.shape, F32)

        def hnorm_bwd(x, g, dy):
            r = lax.rsqrt(_head_sums(x * x, bd) * (1.0 / HD) + EPS)
            xh = x * r
            day = dy * g
            dx = r * (day - xh * (_head_sums(day * xh, bd) * (1.0 / HD)))
            return dx, jnp.sum(dy * xh, axis=0, keepdims=True)

        for ch in range(4):
            sl = slice(128 * ch, 128 * (ch + 1))
            dx, dg = hnorm_bwd(qa[:, sl], gfq_r[:, sl], dqf_r[:, sl] * 0.125)
            dp_o[:, QA + 128 * ch:QA + 128 * (ch + 1)] = dx.astype(CDT)
            acc[0:1, sl] += dg
            dx, dg = hnorm_bwd(ka[:, sl], gfk_r[:, sl], dkf_r[:, sl])
            dp_o[:, KA + 128 * ch:KA + 128 * (ch + 1)] = dx.astype(CDT)
            acc[1:2, sl] += dg
            dx, dg = hnorm_bwd(qb[:, sl], gsq_r[:, sl], dqs_r[:, sl] * 0.125)
            dp_o[:, QB + 128 * ch:QB + 128 * (ch + 1)] = dx.astype(CDT)
            acc[2:3, sl] += dg
        dp_o[:, VA:VA + 512] = dvf_r[...].astype(CDT)
        dp_o[:, GA:GA + D] = dga_r[...]
        dp_o[:, GB:GB + D] = dgb_r[...]

        def fold(x):
            e0 = x[0]
            e1 = x[1]
            return jnp.where(lo, e0 + pltpu.roll(e0, 64, 1), e1 + pltpu.roll(e1, 64, 1))

        dx, dg = hnorm_bwd(kb[...], gsk_r[...], fold(dkse_r))
        dp_o[:, KB:KB + 128] = dx.astype(CDT)
        acc[3:4, 0:128] += dg
        dp_o[:, VB:VB + 128] = fold(dvse_r).astype(CDT)

        rr = _iota((512, 128), 0)
        hh = _iota((512, 128), 1)
        sel = ((rr == (hh >> 1) * 128 + (hh & 1) * HD) & (hh < 8)).astype(F32)
        dcs = _dot_hi(dcq_r[...] - dck_r[...], sel)
        utri = (_iota((tm, tm), 1) >= _iota((tm, tm), 0)).astype(F32)
        dlf = _dot_hi(utri, dcs) + carry[0:1, :]
        carry[0:1, :] = dlf[0:1, :]
        z = fa[...] + fb_r[...]
        dfa = jnp.where(lane < 8, dlf * _sigmoid(-z), 0.0)
        dp_o[:, FA:FA + 128] = dfa.astype(CDT)
        acc[4:5, 0:128] += jnp.sum(dfa, axis=0, keepdims=True)

        @pl.when(i == nt - 1)
        def _():
            foldm = ((_iota((512, 128), 0) & (HD - 1)) == _iota((512, 128), 1)).astype(F32)
            dgn_o[...] = _dot_hi(acc[...], foldm)

    def col(width, off):
        return pl.BlockSpec((tm, width), lambda i: (nt - 1 - i, off // width))

    def rows(width):
        return pl.BlockSpec((tm, width), lambda i: (nt - 1 - i, 0))

    def vec(width):
        return pl.BlockSpec((1, width), lambda i: (0, 0))

    pair = pl.BlockSpec((2, tm, 128), lambda i: (0, nt - 1 - i, 0))
    return pl.pallas_call(
        body, name=name, grid=(nt,),
        in_specs=[col(512, QA), col(512, KA), col(512, QB), col(128, KB), col(128, FA),
                  rows(512), rows(512), rows(512), rows(512), pair, pair, rows(512), rows(512), rows(D), rows(D),
                  vec(512), vec(512), vec(512), vec(128), vec(128)],
        out_specs=[rows(DP), pl.BlockSpec((8, 128), lambda i: (0, 0))],
        out_shape=[_sds((t, DP), CDT), _sds((8, 128), F32)],
        scratch_shapes=[pltpu.VMEM((8, 128), F32), pltpu.VMEM((8, 512), F32)],
        compiler_params=_params(("arbitrary",)),
    )(proj, proj, proj, proj, proj, dqf, dkf, dvf, dqs, dkse, dvse, dcq, dck, dga, dgb, gfq, gfk, gsq, gsk, fb)


def _gate_out_fwd(ofox, oswa, wbf, wbs, proj, w_out, h, name):
    t = ofox.shape[0]
    tm = _row_tile(t)

    def body(of_r, os_r, wf_r, ws_r, ga_r, gb_r, wo_r, h_r, ho_o, yt_o, pf_o, ps_o, oft_o, ost_o):
        pf = _dot(of_r[...], wf_r[...])
        ps = _dot(os_r[...], ws_r[...])
        y = _sigmoid(ga_r[...]) * pf + _sigmoid(gb_r[...]) * ps
        ho_o[...] = h_r[...] + _dot(y.astype(CDT), wo_r[...])
        yt_o[...] = y.T.astype(CDT)
        pf_o[...] = pf.astype(CDT)
        ps_o[...] = ps.astype(CDT)
        oft_o[...] = of_r[...].astype(F32).T.astype(CDT)
        ost_o[...] = os_r[...].astype(F32).T.astype(CDT)

    row = pl.BlockSpec((tm, D), lambda i: (i, 0))
    half = pl.BlockSpec((tm, 512), lambda i: (i, 0))
    whole = lambda r: pl.BlockSpec((r, D), lambda i: (0, 0))
    tcol = lambda r: pl.BlockSpec((r, tm), lambda i: (0, i))
    return pl.pallas_call(
        body, name=name, grid=(t // tm,),
        in_specs=[half, half, whole(512), whole(512),
                  pl.BlockSpec((tm, D), lambda i: (i, GA // D)), pl.BlockSpec((tm, D), lambda i: (i, GB // D)),
                  whole(D), row],
        out_specs=[row, tcol(D), row, row, tcol(512), tcol(512)],
        out_shape=[_sds((t, D), F32), _sds((D, t), CDT), _sds((t, D), CDT), _sds((t, D), CDT),
                   _sds((512, t), CDT), _sds((512, t), CDT)],
        compiler_params=_params(("parallel",)),
    )(ofox, oswa, wbf, wbs, proj, proj, w_out, h)


def _gate_out_bwd(dhb, w_out, pf, ps, proj, wbf, wbs, name):
    t = dhb.shape[0]
    tm = _row_tile(t)

    def body(dh_r, wo_r, pf_r, ps_r, ga_r, gb_r, wf_r, ws_r, dpf_o, dps_o, dga_o, dgb_o, dof_o, doft_o, dos_o):
        dy_ = _dot_nt(dh_r[...], wo_r[...])
        sa = _sigmoid(ga_r[...])
        sb = _sigmoid(gb_r[...])
        dpf = (dy_ * sa).astype(CDT)
        dps = (dy_ * sb).astype(CDT)
        dpf_o[...] = dpf
        dps_o[...] = dps
        dga_o[...] = (dy_ * pf_r[...].astype(F32) * (sa * (1.0 - sa))).astype(CDT)
        dgb_o[...] = (dy_ * ps_r[...].astype(F32) * (sb * (1.0 - sb))).astype(CDT)
        dof = _dot_nt(dpf, wf_r[...])
        dof_o[...] = dof
        doft_o[...] = dof.T.astype(CDT)
        dos_o[...] = _dot_nt(dps, ws_r[...])

    row = pl.BlockSpec((tm, D), lambda i: (i, 0))
    half = pl.BlockSpec((tm, 512), lambda i: (i, 0))
    whole = lambda r: pl.BlockSpec((r, D), lambda i: (0, 0))
    return pl.pallas_call(
        body, name=name, grid=(t // tm,),
        in_specs=[row, whole(D), row, row,
                  pl.BlockSpec((tm, D), lambda i: (i, GA // D)), pl.BlockSpec((tm, D), lambda i: (i, GB // D)),
                  whole(512), whole(512)],
        out_specs=[row] * 4 + [half, pl.BlockSpec((512, tm), lambda i: (0, i)), half],
        out_shape=[_sds((t, D), CDT)] * 4 + [_sds((t, 512), F32), _sds((512, t), CDT), _sds((t, 512), F32)],
        compiler_params=_params(("parallel",)),
    )(dhb, w_out, pf, ps, proj, proj, wbf, wbs)


def _tri_steps(n, by_key):
    if by_key:
        pairs = [(i, j) for j in range(n) for i in range(j, n)]
    else:
        pairs = [(i, j) for i in range(n) for j in range(i + 1)]
    return (np.array([p[0] for p in pairs], np.int32), np.array([p[1] for p in pairs], np.int32))


def _head_col(blk, lane, h):
    return jnp.sum(jnp.where(lane == h, blk, 0.0), axis=1, keepdims=True)


def _head_row(blk, sub, h):
    return jnp.sum(jnp.where(sub == h, blk, 0.0), axis=0, keepdims=True)


def _ride_specs(ride):
    if ride is None:
        return [], [], [], [], []
    kind, srcs, outs, layer, items = ride
    return list(srcs), [ANY] * len(srcs), list(outs), [ANY] * len(outs), _dma_sems(3 * len(srcs))


def _ride_start(ride, srcs, dsts, send_sems, recv_sems):
    for cp in _ici_copies(ride[0], srcs, dsts, send_sems, recv_sems, ride[3], recv=False, items=ride[4])[0]:
        cp.start()


def _ride_wait(ride, srcs, dsts, send_sems, recv_sems):
    sends, recvs = _ici_copies(ride[0], srcs, dsts, send_sems, recv_sems, ride[3], items=ride[4])
    for cp in recvs:
        cp.wait_recv()
    for cp in sends:
        cp.wait_send()


def _riding(body, n_in, n_out, ride, grid):
    if ride is None:
        return body
    nr = len(ride[1])

    def wrapped(*refs):
        ins, srcs = refs[:n_in], refs[n_in:n_in + nr]
        outs, dsts = refs[n_in + nr:n_in + nr + n_out], refs[n_in + nr + n_out:n_in + 2 * nr + n_out]
        scratch, sems = refs[n_in + 2 * nr + n_out:-2], refs[-2:]
        first = pl.program_id(0) == 0
        last = pl.program_id(0) == grid[0] - 1
        for a in range(1, len(grid)):
            first = first & (pl.program_id(a) == 0)
            last = last & (pl.program_id(a) == grid[a] - 1)

        @pl.when(first)
        def _():
            _ride_start(ride, srcs, dsts, *sems)

        body(*ins, *outs, *scratch)

        @pl.when(last)
        def _():
            _ride_wait(ride, srcs, dsts, *sems)

    return wrapped


def _fox_fwd(qf, kf, vf, c, ct, name, ride=None):
    t = qf.shape[0]
    ta = _row_tile(t)
    qi, kj = _tri_steps(t // ta, by_key=False)
    nsteps = len(qi)
    ride_in, ride_in_specs, ride_out, ride_out_specs, ride_sems = _ride_specs(ride)

    def body(qi_r, kj_r, q_r, k_r, v_r, c_r, ct_r, *rest):
        nr = len(ride_in)
        src_r, (o_o, lse_o), dst_o = rest[:nr], rest[nr:nr + 2], rest[nr + 2:2 * nr + 2]
        m_sc, l_sc, acc_sc, cq_sc, *sems = rest[2 * nr + 2:]
        p = pl.program_id(0)
        n = pl.program_id(1)
        i = qi_r[n]
        j = kj_r[n]
        lane = _iota((1, 128), 1)
        lo = lane < HD

        if ride is not None:
            @pl.when((p == 0) & (n == 0))
            def _():
                _ride_start(ride, src_r, dst_o, *sems)

        @pl.when(j == 0)
        def _():
            m_sc[...] = jnp.full(m_sc.shape, NEG, F32)
            l_sc[...] = jnp.zeros(l_sc.shape, F32)
            acc_sc[...] = jnp.zeros(acc_sc.shape, F32)
            for e in (0, 1):
                cq_sc[e] = jnp.broadcast_to(_head_col(c_r[...], lane, 2 * p + e), (ta, 128))

        def step(masked):
            q = q_r[...]
            k = k_r[...]
            vaug = jnp.concatenate([v_r[...], jnp.ones((ta, 128), CDT)], axis=1)
            if masked:
                rows = i * ta + _iota((ta, 1), 0)
                cols = j * ta + _iota((1, ta), 1)
                mask = (cols <= rows) & (cols >= PAD)
            sub = _iota((8, 1), 0)
            heads = (0, 1)
            sels = [lo, jnp.logical_not(lo)]
            s = [_dot_nt(jnp.where(sels[e], q, 0), k) for e in heads]
            ck = [_head_row(ct_r[...], sub, 2 * p + e) for e in heads]
            chunks = []
            for e in heads:
                cq = cq_sc[e]
                row = []
                for ch in range(ta // 128):
                    sl = slice(128 * ch, 128 * (ch + 1))
                    sc = s[e][:, sl] + cq - ck[e][:, sl]
                    if masked:
                        sc = jnp.where(mask[:, sl], sc, NEG)
                    row.append(sc)
                chunks.append(row)
            m_new, alphas = [], []
            for e in heads:
                mx = chunks[e][0]
                for sc in chunks[e][1:]:
                    mx = jnp.maximum(mx, sc)
                m_prev = m_sc[e]
                m_new.append(jnp.maximum(m_prev, jnp.max(mx, axis=1, keepdims=True)))
                alphas.append(jnp.exp(m_prev - m_new[e]))
            pe = [jnp.concatenate([jnp.exp(sc - m_new[e]).astype(CDT) for sc in chunks[e]], axis=1) for e in heads]
            pva = [_dot(pe[e], vaug) for e in heads]
            for e in heads:
                l_sc[e] = alphas[e] * l_sc[e] + pva[e][:, 128:]
                m_sc[e] = m_new[e]
            acc_sc[...] = (acc_sc[...] * jnp.where(lo, alphas[0], alphas[1])
                           + jnp.where(lo, pva[0][:, :128], pva[1][:, :128]))

        edge = (j == i) | (j == 0)

        @pl.when(edge)
        def _():
            step(True)

        @pl.when(jnp.logical_not(edge))
        def _():
            step(False)

        @pl.when(j == i)
        def _():
            l = jnp.where(lo, l_sc[0], l_sc[1])
            o_o[...] = (acc_sc[...] / l).astype(CDT)
            lse_o[...] = jnp.where(lo, m_sc[0], m_sc[1]) + jnp.log(l)

        if ride is not None:
            @pl.when((p == NPAIR - 1) & (n == nsteps - 1))
            def _():
                _ride_wait(ride, src_r, dst_o, *sems)

    qblk = pl.BlockSpec((ta, 128), lambda p, n, qi_r, kj_r: (qi_r[n], p))
    kblk = pl.BlockSpec((ta, 128), lambda p, n, qi_r, kj_r: (kj_r[n], p))
    grid_spec = pltpu.PrefetchScalarGridSpec(
        num_scalar_prefetch=2, grid=(NPAIR, nsteps),
        in_specs=[qblk, kblk, kblk,
                  pl.BlockSpec((ta, 128), lambda p, n, qi_r, kj_r: (qi_r[n], 0)),
                  pl.BlockSpec((8, ta), lambda p, n, qi_r, kj_r: (0, kj_r[n]))] + ride_in_specs,
        out_specs=[qblk, qblk] + ride_out_specs,
        scratch_shapes=[pltpu.VMEM((2, ta, 128), F32), pltpu.VMEM((2, ta, 128), F32), pltpu.VMEM((ta, 128), F32),
                        pltpu.VMEM((2, ta, 128), F32)] + ride_sems,
    )
    return pl.pallas_call(
        body, name=name, grid_spec=grid_spec,
        out_shape=[_sds((t, 512), CDT), _sds((t, 512), F32)] + ride_out,
        compiler_params=_params(("arbitrary", "arbitrary")),
    )(jnp.asarray(qi), jnp.asarray(kj), qf, kf, vf, c, ct, *ride_in)


def _fox_bwd(qf, qft, kf, vf, c, ct, o, lse, do, dot, name, ride=None):
    t = qf.shape[0]
    ta = _row_tile(t)
    nq = t // ta
    qi, kj = _tri_steps(nq, by_key=False)
    nsteps = len(qi)
    ride_in, ride_in_specs, ride_out, ride_out_specs, ride_sems = _ride_specs(ride)

    def body(qi_r, kj_r, q_r, qt_r, k_r, v_r, c_r, ct_r, o_r, lse_r, do_r, dot_r, *rest):
        nr = len(ride_in)
        src_r, (dq_o, dcq_o, dk_o, dv_o, dck_o), dst_o = rest[:nr], rest[nr:nr + 5], rest[nr + 5:2 * nr + 5]
        lse_sc, dl_sc, cq_sc, dq_sc, dcq_sc, dkt_sc, dvt_sc, dckt_sc, *sems = rest[2 * nr + 5:]
        p = pl.program_id(0)
        n = pl.program_id(1)
        i = qi_r[n]
        j = kj_r[n]
        lane = _iota((1, 128), 1)
        lo = lane < HD
        top = _iota((128, 1), 0) < HD

        if ride is not None:
            @pl.when((p == 0) & (n == 0))
            def _():
                _ride_start(ride, src_r, dst_o, *sems)

        @pl.when(n == 0)
        def _():
            dkt_sc[...] = jnp.zeros(dkt_sc.shape, F32)
            dvt_sc[...] = jnp.zeros(dvt_sc.shape, F32)
            dckt_sc[...] = jnp.zeros(dckt_sc.shape, F32)

        @pl.when(j == 0)
        def _():
            dq_sc[...] = jnp.zeros(dq_sc.shape, F32)
            dcq_sc[...] = jnp.zeros(dcq_sc.shape, F32)
            dd = do_r[...] * o_r[...].astype(F32)
            lse = lse_r[...]
            for e in (0, 1):
                sel = lo if e == 0 else jnp.logical_not(lo)
                cq_sc[e] = jnp.broadcast_to(_head_col(c_r[...], lane, 2 * p + e), (ta, 128))
                dl_sc[e] = jnp.broadcast_to(jnp.sum(jnp.where(sel, dd, 0.0), axis=1, keepdims=True), (ta, 128))
                lse_sc[e] = jnp.broadcast_to(lse[:, HD * e:HD * e + 1], (ta, 128))

        def step(masked):
            q = q_r[...]
            qt = qt_r[...]
            k = k_r[...]
            v = v_r[...]
            dob = do_r[...].astype(CDT)
            dot_ = dot_r[...]
            ones = jnp.ones((ta, 128), CDT)
            ones16 = jnp.ones((16, ta), CDT)
            if masked:
                rows = i * ta + _iota((ta, 1), 0)
                cols = j * ta + _iota((1, ta), 1)
                mask = (cols <= rows) & (cols >= PAD)
            sub = _iota((8, 1), 0)
            heads = (0, 1)
            sels = [lo, jnp.logical_not(lo)]
            rsels = [top, jnp.logical_not(top)]
            s = [_dot_nt(jnp.where(sels[e], q, 0), k) for e in heads]
            dp = [_dot_nt(jnp.where(sels[e], dob, 0), v) for e in heads]
            ck = [_head_row(ct_r[...], sub, 2 * p + e) for e in heads]
            pb, dsb = [], []
            for e in heads:
                cq, lse_e, dl = cq_sc[e], lse_sc[e], dl_sc[e]
                prs, dss = [], []
                for ch in range(ta // 128):
                    sl = slice(128 * ch, 128 * (ch + 1))
                    sc = s[e][:, sl] + cq - ck[e][:, sl]
                    if masked:
                        sc = jnp.where(mask[:, sl], sc, NEG)
                    pr = jnp.exp(sc - lse_e)
                    prs.append(pr.astype(CDT))
                    dss.append((pr * (dp[e][:, sl] - dl)).astype(CDT))
                pb.append(jnp.concatenate(prs, axis=1))
                dsb.append(jnp.concatenate(dss, axis=1))
            dvt = [_dot(jnp.where(rsels[e], dot_, 0), pb[e]) for e in heads]
            dkc = [_dot(jnp.concatenate([jnp.where(rsels[e], qt, 0), ones16], axis=0), dsb[e]) for e in heads]
            dqa = [_dot(dsb[e], jnp.concatenate([jnp.where(sels[e], k, 0), ones], axis=1)) for e in heads]
            dvt_sc[j] += dvt[0] + dvt[1]
            dkt_sc[j] += dkc[0][0:128] + dkc[1][0:128]
            dckt_sc[j, 0:8, :] += jnp.where(sub == 0, dkc[0][128:136], jnp.where(sub == 1, dkc[1][128:136], 0.0))
            dq_sc[...] += dqa[0][:, :128] + dqa[1][:, :128]
            for e in heads:
                dcq_sc[e] += dqa[e][:, 128:]

        edge = (j == i) | (j == 0)

        @pl.when(edge)
        def _():
            step(True)

        @pl.when(jnp.logical_not(edge))
        def _():
            step(False)

        @pl.when(j == i)
        def _():
            dq_o[...] = dq_sc[...]
            dcq_o[...] = jnp.where(lo, dcq_sc[0], dcq_sc[1])

        @pl.when(n == nsteps - 1)
        def _():
            spread = (_iota((128, 128), 1) == _iota((128, 128), 0) // HD).astype(F32)
            for jb in range(nq):
                rs = slice(jb * ta, (jb + 1) * ta)
                dk_o[rs, :] = dkt_sc[jb].T
                dv_o[rs, :] = dvt_sc[jb].T
                dck_o[rs, :] = _dot_hi(spread, dckt_sc[jb]).T

        if ride is not None:
            @pl.when((p == NPAIR - 1) & (n == nsteps - 1))
            def _():
                _ride_wait(ride, src_r, dst_o, *sems)

    qblk = pl.BlockSpec((ta, 128), lambda p, n, qi_r, kj_r: (qi_r[n], p))
    qtblk = pl.BlockSpec((128, ta), lambda p, n, qi_r, kj_r: (p, qi_r[n]))
    kblk = pl.BlockSpec((ta, 128), lambda p, n, qi_r, kj_r: (kj_r[n], p))
    whole = pl.BlockSpec((t, 128), lambda p, n, qi_r, kj_r: (0, p))
    grid_spec = pltpu.PrefetchScalarGridSpec(
        num_scalar_prefetch=2, grid=(NPAIR, nsteps),
        in_specs=[qblk, qtblk, kblk, kblk,
                  pl.BlockSpec((ta, 128), lambda p, n, qi_r, kj_r: (qi_r[n], 0)),
                  pl.BlockSpec((8, ta), lambda p, n, qi_r, kj_r: (0, kj_r[n])),
                  qblk, qblk, qblk, qtblk] + ride_in_specs,
        out_specs=[qblk, qblk, whole, whole, whole] + ride_out_specs,
        scratch_shapes=[pltpu.VMEM((2, ta, 128), F32)] * 3 + [pltpu.VMEM((ta, 128), F32), pltpu.VMEM((2, ta, 128), F32)]
        + [pltpu.VMEM((nq, 128, ta), F32)] * 3 + ride_sems,
    )
    return pl.pallas_call(
        body, name=name, grid_spec=grid_spec,
        out_shape=[_sds((t, 512), F32)] * 5 + ride_out,
        compiler_params=_params(("arbitrary", "arbitrary")),
    )(jnp.asarray(qi), jnp.asarray(kj), qf, qft, kf, vf, c, ct, o, lse, do, dot, *ride_in)


def _bucket_table():
    r = np.arange(BLK)[:, None]
    c = np.arange(3 * BLK)[None, :]
    d = np.where(c < BLK, r + BLK - c, r - (c - BLK))
    n = np.maximum(d, 0)
    max_exact = N_BUCKETS // 2
    nf = np.maximum(n, 1).astype(np.float32)
    large = max_exact + (np.log(nf / max_exact) / math.log(BLK / max_exact) * (N_BUCKETS - max_exact)).astype(np.int32)
    large = np.minimum(large, N_BUCKETS - 1)
    b = np.where(n < max_exact, n, large)
    return np.where(c < 2 * BLK, b, N_BUCKETS - 1).astype(np.int32)


def _bias_fwd(table, name):
    bucket = jnp.asarray(_bucket_table())

    def body(tab_r, b_r, o_o):
        h = pl.program_id(0)
        b = b_r[...]
        acc = jnp.zeros(b.shape, F32)
        for k in range(N_BUCKETS):
            acc = jnp.where(b == k, tab_r[k, h], acc)
        o_o[...] = acc

    return pl.pallas_call(
        body, name=name, grid=(8,),
        in_specs=[pl.BlockSpec(memory_space=pltpu.SMEM), pl.BlockSpec((BLK, 3 * BLK), lambda h: (0, 0))],
        out_specs=pl.BlockSpec((None, BLK, 3 * BLK), lambda h: (h, 0, 0)),
        out_shape=_sds((8, BLK, 3 * BLK), F32),
        compiler_params=_params(("parallel",)),
    )(table, bucket)


def _bias_bwd(dbias, name):
    bucket = jnp.asarray(_bucket_table())

    def body(d_r, b_r, o_o):
        h = pl.program_id(0)
        b = b_r[...]
        d = d_r[...]
        lane = _iota((1, 128), 1)
        row = jnp.zeros((1, 128), F32)
        for k in range(N_BUCKETS):
            row = jnp.where(lane == k, jnp.sum(jnp.where(b == k, d, 0.0)), row)
        o_o[pl.ds(h, 1), :] = row

    return pl.pallas_call(
        body, name=name, grid=(8,),
        in_specs=[pl.BlockSpec((None, BLK, 3 * BLK), lambda h: (h, 0, 0)), pl.BlockSpec((BLK, 3 * BLK), lambda h: (0, 0))],
        out_specs=pl.BlockSpec((8, 128), lambda h: (0, 0)),
        out_shape=_sds((8, 128), F32),
        compiler_params=_params(("arbitrary",)),
    )(dbias, bucket)


def _swa_valid(i):
    r = _iota((BLK, 1), 0)
    c = _iota((1, 3 * BLK), 1)
    prev = (c < BLK) & (c > r) & (i >= 1) & ((i - 1) * BLK + c >= PAD)
    cc = c - BLK
    cur = (c >= BLK) & (c < 2 * BLK) & (cc <= r) & (i * BLK + cc >= PAD)
    cm = c - 2 * BLK
    meta = (c >= 2 * BLK) & (cm >= PAD) & (i * BLK + r - cm >= BLK)
    return prev | cur | meta


def _swa_kv_specs(ta):
    nb = ta // BLK
    return [pl.BlockSpec((None, BLK, 128), lambda p, i: (p // 2, jnp.maximum(i * nb - 1, 0), 0)),
            pl.BlockSpec((None, ta, 128), lambda p, i: (p // 2, i, 0)),
            pl.BlockSpec((None, BLK, 128), lambda p, i: (p // 2, 0, 0))]


def _swa_fwd(qs, kse, vse, bias, sinks, name, ride=None):
    t = qs.shape[0]
    ta = _row_tile(t)
    nb = ta // BLK
    grid = (NPAIR, t // ta)
    ride_in, ride_in_specs, ride_out, ride_out_specs, ride_sems = _ride_specs(ride)

    def body(sink_r, q_r, kp_r, kc_r, km_r, vp_r, vc_r, vm_r, b_r, o_o, lse_o):
        p = pl.program_id(0)
        i = pl.program_id(1)
        lo = _iota((1, 128), 1) < HD
        k4 = jnp.concatenate([kp_r[...], kc_r[...]], axis=0)
        v4 = jnp.concatenate([vp_r[...], vc_r[...]], axis=0)
        work = [(b, e) for b in range(nb) for e in (0, 1)]
        sinks = [sink_r[2 * p + e] for e in (0, 1)]
        v3 = [jnp.concatenate([v4[BLK * b:BLK * (b + 2)], vm_r[...]], axis=0) for b in range(nb)]
        s = {}
        for b in range(nb):
            q = q_r[BLK * b:BLK * (b + 1), :]
            k3 = jnp.concatenate([k4[BLK * b:BLK * (b + 2)], km_r[...]], axis=0)
            valid = _swa_valid(i * nb + b)
            for e in (0, 1):
                sel = lo if e == 0 else jnp.logical_not(lo)
                s[b, e] = jnp.where(valid, _dot_nt(jnp.where(sel, q, 0), k3) + b_r[e], NEG)
        mx = {w: jnp.maximum(jnp.max(s[w], axis=1, keepdims=True), sinks[w[1]]) for w in work}
        pe = {w: jnp.exp(s[w] - mx[w]) for w in work}
        den = {w: jnp.sum(pe[w], axis=1, keepdims=True) + jnp.exp(sinks[w[1]] - mx[w]) for w in work}
        out = {w: _dot(pe[w].astype(CDT), v3[w[0]]) / den[w] for w in work}
        for b in range(nb):
            rows = slice(BLK * b, BLK * (b + 1))
            o_o[rows, :] = jnp.where(lo, out[b, 0], out[b, 1]).astype(CDT)
            lse_o[rows, :] = jnp.where(lo, mx[b, 0] + jnp.log(den[b, 0]), mx[b, 1] + jnp.log(den[b, 1]))

    qblk = pl.BlockSpec((ta, 128), lambda p, i: (i, p))
    res = pl.pallas_call(
        _riding(body, 9, 2, ride, grid), name=name, grid=grid,
        in_specs=[pl.BlockSpec(memory_space=pltpu.SMEM), qblk] + _swa_kv_specs(ta) + _swa_kv_specs(ta)
        + [pl.BlockSpec((2, BLK, 3 * BLK), lambda p, i: (p, 0, 0))] + ride_in_specs,
        out_specs=[qblk, qblk] + ride_out_specs,
        out_shape=[_sds((t, 512), CDT), _sds((t, 512), F32)] + ride_out, scratch_shapes=ride_sems,
        compiler_params=_params(("arbitrary", "arbitrary") if ride else ("parallel", "parallel")),
    )(sinks, qs, kse, kse, kse, vse, vse, vse, bias, *ride_in)
    return (res[0], res[1], res[2:]) if ride else res


def _swa_bwd(qs, kse, vse, bias, sinks, o, lse, do, name):
    t = qs.shape[0]
    ta = _row_tile(t)
    nb = ta // BLK

    def body(sink_r, q_r, kp_r, kc_r, km_r, vp_r, vc_r, vm_r, b_r, o_r, lse_r, do_r,
             dq_o, dk_o, dv_o, db_o, dsk_o):
        p = pl.program_id(0)
        i = pl.program_id(1)
        lo = _iota((1, 128), 1) < HD

        @pl.when((i == 0) & (p % 2 == 0))
        def _():
            dk_o[...] = jnp.zeros(dk_o.shape, F32)
            dv_o[...] = jnp.zeros(dv_o.shape, F32)

        @pl.when(i == 0)
        def _():
            db_o[...] = jnp.zeros(db_o.shape, F32)
            dsk_o[...] = jnp.zeros(dsk_o.shape, F32)

        k4 = jnp.concatenate([kp_r[...], kc_r[...]], axis=0)
        v4 = jnp.concatenate([vp_r[...], vc_r[...]], axis=0)
        work = [(b, e) for b in range(nb) for e in (0, 1)]
        sel = [lo, jnp.logical_not(lo)]
        k3 = [jnp.concatenate([k4[BLK * b:BLK * (b + 2)], km_r[...]], axis=0) for b in range(nb)]
        v3 = [jnp.concatenate([v4[BLK * b:BLK * (b + 2)], vm_r[...]], axis=0) for b in range(nb)]
        q = [q_r[BLK * b:BLK * (b + 1), :] for b in range(nb)]
        do_ = [do_r[BLK * b:BLK * (b + 1), :] for b in range(nb)]
        lse = [lse_r[BLK * b:BLK * (b + 1), :] for b in range(nb)]
        dd = [do_[b] * o_r[BLK * b:BLK * (b + 1), :].astype(F32) for b in range(nb)]
        valid = [_swa_valid(i * nb + b) for b in range(nb)]
        qe = {(b, e): jnp.where(sel[e], q[b], 0) for b, e in work}
        doe = {(b, e): jnp.where(sel[e], do_[b], 0.0).astype(CDT) for b, e in work}
        lse_e = {(b, e): lse[b][:, HD * e:HD * e + 1] for b, e in work}
        delta = {(b, e): jnp.sum(jnp.where(sel[e], dd[b], 0.0), axis=1, keepdims=True) for b, e in work}
        s = {(b, e): jnp.where(valid[b], _dot_nt(qe[b, e], k3[b]) + b_r[e], NEG) for b, e in work}
        dp = {(b, e): _dot_nt(doe[b, e], v3[b]) for b, e in work}
        pr = {w: jnp.exp(s[w] - lse_e[w]) for w in work}
        ds = {w: pr[w] * (dp[w] - delta[w]) for w in work}
        dqs = {(b, e): _dot(ds[b, e].astype(CDT), jnp.where(sel[e], k3[b], 0)) for b, e in work}
        both = lambda x, b: jnp.concatenate([x[b, 0], x[b, 1]], axis=0)
        dk3 = [_dot(both(ds, b).T.astype(CDT), both(qe, b)) for b in range(nb)]
        dv3 = [_dot(both(pr, b).T.astype(CDT), both(doe, b)) for b in range(nb)]
        for e in (0, 1):
            tot = ds[0, e]
            for b in range(1, nb):
                tot = tot + ds[b, e]
            db_o[e] += tot
        dsink = [sum(-jnp.sum(jnp.exp(sink_r[2 * p + e] - lse_e[b, e]) * delta[b, e], axis=0, keepdims=True)
                     for b in range(nb)) for e in (0, 1)]
        dsk_o[0:1, :] += jnp.where(lo, dsink[0], dsink[1])
        for b in range(nb):
            ib = i * nb + b
            dq_o[BLK * b:BLK * (b + 1), :] = dqs[b, 0] + dqs[b, 1]
            dk = dk3[b]
            dv = dv3[b]
            prev = pl.ds(pl.multiple_of(jnp.maximum(ib - 1, 0) * BLK, BLK), BLK)
            cur = pl.ds(pl.multiple_of(ib * BLK, BLK), BLK)
            dk_o[prev, :] += dk[0:BLK]
            dk_o[cur, :] += dk[BLK:2 * BLK]
            dk_o[0:BLK, :] += dk[2 * BLK:]
            dv_o[prev, :] += dv[0:BLK]
            dv_o[cur, :] += dv[BLK:2 * BLK]
            dv_o[0:BLK, :] += dv[2 * BLK:]

    qblk = pl.BlockSpec((ta, 128), lambda p, i: (i, p))
    kvacc = pl.BlockSpec((None, t, 128), lambda p, i: (p // 2, 0, 0))
    bblk = pl.BlockSpec((2, BLK, 3 * BLK), lambda p, i: (p, 0, 0))
    return pl.pallas_call(
        body, name=name, grid=(NPAIR, t // ta),
        in_specs=[pl.BlockSpec(memory_space=pltpu.SMEM), qblk] + _swa_kv_specs(ta) + _swa_kv_specs(ta)
        + [bblk, qblk, qblk, qblk],
        out_specs=[qblk, kvacc, kvacc, bblk, pl.BlockSpec((None, 8, 128), lambda p, i: (p, 0, 0))],
        out_shape=[_sds((t, 512), F32), _sds((2, t, 128), F32), _sds((2, t, 128), F32),
                   _sds((8, BLK, 3 * BLK), F32), _sds((NPAIR, 8, 128), F32)],
        compiler_params=_params(("arbitrary", "arbitrary")),
    )(sinks, qs, kse, kse, kse, vse, vse, vse, bias, o, lse, do)


def _sum8(slots, name):
    def body(a_r, o_o):
        acc = a_r[0]
        for k in range(1, 8):
            acc = acc + a_r[k]
        o_o[...] = acc

    return pl.pallas_call(
        body, name=name, out_shape=_sds((SMALL_ROWS, 128), F32),
        in_specs=[pl.BlockSpec(memory_space=pltpu.VMEM)], out_specs=pl.BlockSpec(memory_space=pltpu.VMEM),
        compiler_params=_params(),
    )(slots)


def _place():
    x, y, c = lax.axis_index("x"), lax.axis_index("y"), lax.axis_index("c")
    chips = [(1 - x, y), (x, 1 - y), (1 - x, 1 - y)]
    return x, y, c, chips


def _remote(src, dst, send_sems, recv_sems, k, to):
    return pltpu.make_async_remote_copy(src_ref=src, dst_ref=dst, send_sem=send_sems.at[k], recv_sem=recv_sems.at[k],
                                        device_id=to, device_id_type=MESH_ID)


ANY = pl.BlockSpec(memory_space=pl.ANY)


def _mix_cols(w):
    return jnp.concatenate([w[:, 2312:4360], w[:, 0:1536], w[:, 1544:2312], w[:, 1536:1544],
                            jnp.zeros((w.shape[0], DP - D_IN), w.dtype)], axis=1)


def _unmix_cols(w):
    return jnp.concatenate([w[:, QA:QA + 1536], w[:, FA:FA + 8], w[:, QB:QB + 768], w[:, GA:GA + 2048]], axis=1)


def _rows128(a, rows):
    flat = a.reshape(-1)
    return jnp.pad(flat, (0, rows * 128 - flat.shape[0])).reshape(rows, 128)


GRAD_FORM = {"ffn1_w_in": "col", "ffn2_w_in": "col", "w_branch_fox": "col", "w_branch_swa": "col",
             "ffn1_w_out": "3d", "ffn2_w_out": "3d", "w_out": "3d", "w_in": "3d"}
SUM_TILE = {1024: 512, 704: 352, 512: 256, 256: 128}
NT = len(SHARD_ITEMS)
ALL_ITEMS = tuple(range(NT))


def _half_rows(c, r):
    return pl.ds(pl.multiple_of(c * (r // 2), 16), r // 2)


def _ici_copies(kind, srcs, dsts, send_sems, recv_sems, layer, recv=True, items=ALL_ITEMS):
    x, y, c, chips = _place()
    s = 2 * x + y
    sends, recvs = [], []
    for t, (item, src, dst) in enumerate(zip(items, srcs, dsts)):
        nm, (r, cc), _ = SHARD_ITEMS[item]
        for j, (cx, cy) in enumerate(chips):
            sj = 2 * cx + cy
            k = 3 * t + j
            to = (cx, cy, c)
            if kind == "gather":
                hs = _half_rows(c, r)
                sends.append(_remote(src.at[layer, hs], dst.at[s, hs], send_sems, recv_sems, k, to))
                if recv:
                    recvs.append(_remote(src.at[layer, hs], dst.at[sj, hs], send_sems, recv_sems, k, to))
            else:
                if GRAD_FORM[nm] == "col":
                    piece = src.at[:, pl.ds(pl.multiple_of(sj * cc, 128), cc)]
                else:
                    piece = src.at[sj]
                sends.append(_remote(piece, dst.at[j], send_sems, recv_sems, k, to))
                recvs.append(sends[-1])
    return sends, recvs


def _slab_shapes(items=ALL_ITEMS):
    return [_sds((4, *SHARD_ITEMS[t][1]), CDT) for t in items]


def _dma_sems(n):
    return [pltpu.SemaphoreType.DMA((n,)), pltpu.SemaphoreType.DMA((n,))]


def _forward_sends(dsts, send_sems, recv_sems, items=ALL_ITEMS):
    x, y, c, chips = _place()
    sends, recvs = [], []
    for t, (item, dst) in enumerate(zip(items, dsts)):
        r = SHARD_ITEMS[item][1][0]
        for j, (cx, cy) in enumerate(chips):
            sj = 2 * cx + cy
            hs, ho = _half_rows(c, r), _half_rows(1 - c, r)
            sends.append(_remote(dst.at[sj, hs], dst.at[sj, hs], send_sems, recv_sems, 3 * t + j, (x, y, 1 - c)))
            recvs.append(_remote(dst.at[sj, ho], dst.at[sj, ho], send_sems, recv_sems, 3 * t + j, (x, y, 1 - c)))
    return sends, recvs


def _gather_layer(wb, mflat, layer, name, items):
    nt = len(items)

    def body(*refs):
        srcs, m_r, dsts, mall_o = refs[:nt], refs[nt], refs[nt + 1:2 * nt + 1], refs[2 * nt + 1]
        send_sems, recv_sems, fsend, frecv, msend, mrecv = refs[2 * nt + 2:]
        x, y, c, chips = _place()
        s = 2 * x + y
        sends, recvs = _ici_copies("gather", srcs, dsts, send_sems, recv_sems, layer, items=items)
        metas = [_remote(m_r, mall_o.at[s], msend, mrecv, j, (cx, cy, c)) for j, (cx, cy) in enumerate(chips)]
        for cp in sends + metas:
            cp.start()
        fwds, frecvs = _forward_sends(dsts, fsend, frecv, items)
        for got, fwd in zip(recvs, fwds):
            got.wait_recv()
            fwd.start()
        for got in frecvs:
            got.wait_recv()
        for j, (cx, cy) in enumerate(chips):
            _remote(m_r, mall_o.at[2 * cx + cy], msend, mrecv, j, (cx, cy, c)).wait_recv()
        for cp in sends + metas + fwds:
            cp.wait_send()

    return pl.pallas_call(
        body, name=name, out_shape=_slab_shapes(items) + [_sds((4, META_ROWS, 128), F32)],
        in_specs=[ANY] * (nt + 1), out_specs=[ANY] * (nt + 1),
        scratch_shapes=_dma_sems(3 * nt) + _dma_sems(3 * nt) + _dma_sems(3),
    )(*wb, mflat)


def _forward_layer(slabs, name, items=ALL_ITEMS):
    nt = len(items)

    def body(*refs):
        ins, outs, send_sems, recv_sems = refs[:nt], refs[nt:2 * nt], refs[2 * nt], refs[2 * nt + 1]
        sends, recvs = _forward_sends(outs, send_sems, recv_sems, items)
        for cp in sends:
            cp.start()
        for cp in recvs:
            cp.wait_recv()
        for cp in sends:
            cp.wait_send()

    return pl.pallas_call(
        body, name=name, out_shape=_slab_shapes(items), in_specs=[ANY] * nt, out_specs=[ANY] * nt,
        input_output_aliases={t: t for t in range(nt)}, scratch_shapes=_dma_sems(3 * nt),
    )(*slabs)


def _half_shape(nm, r, c):
    return (r // 2, 4 * c) if GRAD_FORM[nm] == "col" else (4, r // 2, c)


def _swap_layer(gs, gsm, name, items=ALL_ITEMS):
    small = gsm is not None
    nt = len(items)

    def body(*refs):
        g_rs = refs[:nt]
        pos = nt
        if small:
            s_r = refs[pos]
            pos += 1
        got_os = refs[pos:pos + nt]
        pos += nt
        if small:
            slots_o = refs[pos]
            pos += 1
        send_sems, recv_sems = refs[pos], refs[pos + 1]
        x, y, c, _ = _place()
        sib = (x, y, 1 - c)
        sent = []
        for t, (item, g_r, got_o) in enumerate(zip(items, g_rs, got_os)):
            nm, (r, cc), _ = SHARD_ITEMS[item]
            ho = _half_rows(1 - c, r)
            src = g_r.at[ho, :] if GRAD_FORM[nm] == "col" else g_r.at[:, ho, :]
            sent.append(_remote(src, got_o, send_sems, recv_sems, t, sib))
        if small:
            ssend, srecv, loc_sem = refs[pos + 2], refs[pos + 3], refs[pos + 4]
            me = 4 * x + 2 * y + c
            loc = pltpu.make_async_copy(s_r, slots_o.at[me], loc_sem.at[0])
            loc.start()
            peers = [(x ^ (k >> 2), y ^ ((k >> 1) & 1), c ^ (k & 1)) for k in range(1, 8)]
            for k, peer in enumerate(peers):
                sent.append(_remote(s_r, slots_o.at[me], ssend, srecv, k, peer))
        for cp in sent:
            cp.start()
        for cp in sent[:nt]:
            cp.wait_recv()
        if small:
            for k, (px, py, pc) in enumerate(peers):
                _remote(s_r, slots_o.at[4 * px + 2 * py + pc], ssend, srecv, k, (px, py, pc)).wait_recv()
        for cp in sent:
            cp.wait_send()
        if small:
            loc.wait()

    outs = [_sds(_half_shape(*SHARD_ITEMS[item][0:1], *SHARD_ITEMS[item][1]), CDT) for item in items]
    ops = list(gs)
    sems = _dma_sems(nt)
    if small:
        outs.append(_sds((8, SMALL_ROWS, 128), F32))
        ops.append(gsm)
        sems = sems + _dma_sems(7) + [pltpu.SemaphoreType.DMA((1,))]
    res = pl.pallas_call(
        body, name=name, out_shape=outs, in_specs=[ANY] * len(ops), out_specs=[ANY] * len(outs), scratch_shapes=sems,
    )(*ops)
    return (res[:nt], res[nt]) if small else (res, None)


def _pair_add_t(own, got, half_idx, nm, r, name):
    tr = SUM_TILE[r]
    nb = (r // 2) // tr
    if GRAD_FORM[nm] == "col":
        blk = (tr, own.shape[1])
        own_spec = pl.BlockSpec(blk, lambda i, c_r: (c_r[0] * nb + i, 0))
        half_spec = pl.BlockSpec(blk, lambda i, c_r: (i, 0))
    else:
        blk = (4, tr, own.shape[2])
        own_spec = pl.BlockSpec(blk, lambda i, c_r: (0, c_r[0] * nb + i, 0))
        half_spec = pl.BlockSpec(blk, lambda i, c_r: (0, i, 0))

    def body(c_r, a_r, b_r, o_o):
        o_o[...] = (a_r[...].astype(F32) + b_r[...].astype(F32)).astype(CDT)

    grid_spec = pltpu.PrefetchScalarGridSpec(num_scalar_prefetch=1, grid=(nb,), in_specs=[own_spec, half_spec],
                                             out_specs=half_spec)
    return pl.pallas_call(body, name=name, grid_spec=grid_spec, out_shape=_sds(got.shape, CDT),
                          compiler_params=_params(("parallel",)))(half_idx, own, got)


def _sum4_t(ps, got3, buf, idx, layer, nm, r, name):
    tr = SUM_TILE[r]
    nb = (r // 2) // tr
    c = got3.shape[2]
    if GRAD_FORM[nm] == "col":
        ps_spec = pl.BlockSpec((tr, c), lambda i, x_r: (i, x_r[0]))
    else:
        ps_spec = pl.BlockSpec((None, tr, c), lambda i, x_r: (x_r[0], i, 0))

    def body(x_r, a_r, b_r, buf_r, o_o):
        o_o[...] = ((a_r[...].astype(F32) + b_r[0].astype(F32)) + b_r[1].astype(F32)) + b_r[2].astype(F32)

    grid_spec = pltpu.PrefetchScalarGridSpec(
        num_scalar_prefetch=1, grid=(nb,),
        in_specs=[ps_spec, pl.BlockSpec((3, tr, c), lambda i, x_r: (0, i, 0)), ANY],
        out_specs=pl.BlockSpec((None, tr, c), lambda i, x_r: (layer, x_r[1] * nb + i, 0)),
    )
    return pl.pallas_call(body, name=name, grid_spec=grid_spec, out_shape=_sds(buf.shape, F32),
                          input_output_aliases={3: 0}, compiler_params=_params(("parallel",)))(idx, ps, got3, buf)


def _scatter_layer(ps, name, items=ALL_ITEMS):
    nt = len(items)

    def body(*refs):
        srcs, dsts, send_sems, recv_sems = refs[:nt], refs[nt:2 * nt], refs[2 * nt], refs[2 * nt + 1]
        sends, recvs = _ici_copies("scatter", srcs, dsts, send_sems, recv_sems, None, items=items)
        for cp in sends:
            cp.start()
        for cp in recvs:
            cp.wait_recv()
        for cp in sends:
            cp.wait_send()

    return pl.pallas_call(
        body, name=name, out_shape=_got3_shapes(items), in_specs=[ANY] * nt, out_specs=[ANY] * nt,
        scratch_shapes=_dma_sems(3 * nt),
    )(*ps)


def _got3_shapes(items=ALL_ITEMS):
    return [_sds((3, SHARD_ITEMS[t][1][0] // 2, SHARD_ITEMS[t][1][1]), CDT) for t in items]


def _join_layer(bufs, name):
    def body(*refs):
        ins, outs, send_sems, recv_sems = refs[:NT], refs[NT:2 * NT], refs[2 * NT], refs[2 * NT + 1]
        x, y, c, _ = _place()
        sent = []
        for t, ((nm, (r, cc), _), b_o) in enumerate(zip(SHARD_ITEMS, outs)):
            hs = _half_rows(c, r)
            sent.append(_remote(b_o.at[:, hs, :], b_o.at[:, hs, :], send_sems, recv_sems, t, (x, y, 1 - c)))
        for cp in sent:
            cp.start()
        for t, ((nm, (r, cc), _), b_o) in enumerate(zip(SHARD_ITEMS, outs)):
            ho = _half_rows(1 - c, r)
            _remote(b_o.at[:, ho, :], b_o.at[:, ho, :], send_sems, recv_sems, t, (x, y, 1 - c)).wait_recv()
        for cp in sent:
            cp.wait_send()

    return pl.pallas_call(
        body, name=name, out_shape=[_sds(b.shape, F32) for b in bufs], in_specs=[ANY] * NT, out_specs=[ANY] * NT,
        input_output_aliases={t: t for t in range(NT)}, scratch_shapes=_dma_sems(NT),
    )(*bufs)


def _adamw3(w, g, m, v, name):
    nl, r, c = w.shape
    tr = SUM_TILE.get(r, r)
    if r % 8:
        blk = pl.BlockSpec((None, r, 256), lambda l, i: (l, 0, i))
        steps = c // 256
    else:
        blk = pl.BlockSpec((None, tr, c), lambda l, i: (l, i, 0))
        steps = r // tr

    def body(w_r, g_r, m_r, v_r, d_o, m_o, v_o):
        g_ = g_r[...]
        m_ = ADAM_B1 * m_r[...] + (1.0 - ADAM_B1) * g_
        v_ = ADAM_B2 * v_r[...] + (1.0 - ADAM_B2) * jnp.square(g_)
        m_hat = m_ / (1.0 - ADAM_B1 ** ADAM_STEP)
        v_hat = v_ / (1.0 - ADAM_B2 ** ADAM_STEP)
        d_o[...] = -ADAM_LR * (m_hat / (jnp.sqrt(v_hat) + ADAM_EPS) + ADAM_WD * w_r[...])
        m_o[...] = m_
        v_o[...] = v_

    return pl.pallas_call(
        body, name=name, grid=(nl, steps),
        in_specs=[blk] * 4, out_specs=[blk] * 3, out_shape=[_sds((nl, r, c), F32)] * 3,
        compiler_params=_params(("parallel", "parallel")),
    )(w, g, m, v)


def _full_weights(slabs, wb, layer, shard, items=ALL_ITEMS):
    ws = {}
    for t, slab in zip(items, slabs):
        nm, (r, c), kind = SHARD_ITEMS[t]
        slab = lax.dynamic_update_slice(slab, wb[nm][layer][None], (shard, 0, 0))
        ws[nm] = slab.reshape(4 * r, c) if kind == "row" else jnp.concatenate([slab[s] for s in range(4)], axis=1)
    return ws


def _exchange_forms(g, items=ALL_ITEMS):
    out = []
    for t in items:
        nm, (r, c), _ = SHARD_ITEMS[t]
        a = g[nm]
        if nm == "w_in":
            a = a.reshape(D, 4, c).transpose(1, 0, 2)
        elif GRAD_FORM[nm] == "3d":
            a = a.reshape(4, r, c)
        out.append(a)
    return out


SMALL_ITEMS = (("rel_bias_table", 2), ("ffn1_norm", 16), ("mix_norm", 16), ("ffn2_norm", 16), ("forget_bias", 1),
               ("fox_q_norm", 1), ("fox_k_norm", 1), ("swa_q_norm", 1), ("swa_k_norm", 1), ("swa_sinks", 1))
SMALL_ADAM_ROWS = 96


def _layer_fwd(h, lw, l, ride=None, late=None):
    rides = late["rides"] if late else {}

    def run(key, fn, *args):
        r = rides.get(key)
        if r is None:
            return fn(*args)
        out = fn(*args, ride=r)
        late["arrived"](key, out[-1])
        return out[0] if len(out) == 2 else out[:-1]

    sv = {"h0": h}
    a, sv["a1t"] = _rms_fwd(h, lw["ffn1_norm"], f"rms_fwd_a{l}")
    sv["gu1"], s, sv["s1t"] = run("ffn_in_a", _ffn_in, a, lw["ffn1_w_in"], f"ffn_in_a{l}")
    h = run("ffn_out_a", _mm_res, s, lw["ffn1_w_out"], h, 0.5, f"ffn_out_a{l}")
    sv["h1"] = h
    a, sv["amt"] = _rms_fwd(h, lw["mix_norm"], f"rms_fwd_m{l}")
    if late:
        late["need"](lw, "mixer")
    proj = run("proj", _mm, a, lw["w_mix"], F32, _row_tile(h.shape[0]), DP, f"proj{l}")
    sv["proj"] = proj
    qf, kf, vf, qs, kse, vse, c, ct, sv["qft"] = _qknorm_fwd(proj, lw["gfq"], lw["gfk"], lw["gsq"], lw["gsk"], lw["fb"],
                                                              f"qknorm_fwd{l}")
    ofox, lse_f, *rode = _fox_fwd(qf, kf, vf, c, ct, f"fox_fwd{l}", ride)
    oswa, lse_s = run("swa_fwd", _swa_fwd, qs, kse, vse, lw["bias"], lw["sinks"], f"swa_fwd{l}")
    if late:
        late["need"](lw, "gate")
    sv.update(qf=qf, kf=kf, vf=vf, qs=qs, kse=kse, vse=vse, c=c, ct=ct, ofox=ofox, oswa=oswa, lse_f=lse_f, lse_s=lse_s)
    h, sv["yt"], sv["pf"], sv["ps"], sv["oft"], sv["ost"] = _gate_out_fwd(
        ofox, oswa, lw["w_branch_fox"], lw["w_branch_swa"], proj, lw["w_out"], h, f"gate_out_fwd{l}")
    sv["h2"] = h
    a, sv["a2t"] = _rms_fwd(h, lw["ffn2_norm"], f"rms_fwd_b{l}")
    sv["gu2"], s, sv["s2t"] = _ffn_in(a, lw["ffn2_w_in"], f"ffn_in_b{l}")
    h = _mm_res(s, lw["ffn2_w_out"], h, 0.5, f"ffn_out_b{l}")
    return h, sv, rode


def _ffn_bwd(dh, dhb, h_in, at, gu, st, norm, w_in, w_out, tag, rides=None):
    r = rides or (None,) * 4
    rode = []

    def split(res, ride):
        if ride is None:
            return res
        rode.extend(res[-1])
        return res[0] if len(res) == 2 else res[:-1]

    dgu = split(_ffn_bwd_mid(dhb, w_out, gu, f"ffn_bwd_mid_{tag}", r[0]), r[0])
    d_w_out = split(_mm(st, dhb, CDT, 256, D, f"dw_ffn_out_{tag}", scale=0.5, ride=r[1]), r[1])
    dh, dhb, dg = split(_ffn_bwd_in(dgu, w_in, h_in, norm, dh, f"ffn_bwd_in_{tag}", r[2]), r[2])
    d_w_in = split(_mm(at, dgu, CDT, D, 256, f"dw_ffn_in_{tag}", ride=r[3]), r[3])
    return dh, dhb, d_w_out, d_w_in, dg, rode


def _layer_bwd(dh, dhb, sv, lw, l, ride=None, before_ffn1=None):
    g = {}
    dh, dhb, g["ffn2_w_out"], g["ffn2_w_in"], g["ffn2_norm"], _ = _ffn_bwd(
        dh, dhb, sv["h2"], sv["a2t"], sv["gu2"], sv["s2t"], lw["ffn2_norm"], lw["ffn2_w_in"], lw["ffn2_w_out"], f"b{l}")
    g["w_out"] = _mm(sv["yt"], dhb, CDT, 512, 512, f"dw_out{l}")
    dpf, dps, dga, dgb, do_f, do_ft, do_s = _gate_out_bwd(dhb, lw["w_out"], sv["pf"], sv["ps"], sv["proj"],
                                                           lw["w_branch_fox"], lw["w_branch_swa"], f"gate_out_bwd{l}")
    g["w_branch_fox"] = _mm(sv["oft"], dpf, CDT, 512, 512, f"dw_bfox{l}")
    g["w_branch_swa"] = _mm(sv["ost"], dps, CDT, 512, 512, f"dw_bswa{l}")
    dqf, dcq, dkf, dvf, dck, *rode = _fox_bwd(sv["qf"], sv["qft"], sv["kf"], sv["vf"], sv["c"], sv["ct"], sv["ofox"],
                                              sv["lse_f"], do_f, do_ft, f"fox_bwd{l}", ride)
    g["rode"] = rode
    dqs, dkse, dvse, dbias, dsk = _swa_bwd(sv["qs"], sv["kse"], sv["vse"], lw["bias"], lw["sinks"], sv["oswa"],
                                           sv["lse_s"], do_s, f"swa_bwd{l}")
    dproj, dgn = _qknorm_bwd(sv["proj"], dqf, dkf, dvf, dqs, dkse, dvse, dcq, dck, dga, dgb,
                             lw["gfq"], lw["gfk"], lw["gsq"], lw["gsk"], lw["fb"], f"qknorm_bwd{l}")
    g["w_mix"] = _mm(sv["amt"], dproj, CDT, D, 640, f"dw_mix{l}")
    dh, dhb, g["mix_norm"] = _mm_nt_rms(dproj, lw["w_mix"], sv["h1"], lw["mix_norm"], dh, f"d_am{l}")
    g["dbias"], g["dsk"], g["dgn"] = dbias, dsk, dgn
    rides = before_ffn1(g) if before_ffn1 else None
    dh, dhb, g["ffn1_w_out"], g["ffn1_w_in"], g["ffn1_norm"], g["rode_ffn1"] = _ffn_bwd(
        dh, dhb, sv["h0"], sv["a1t"], sv["gu1"], sv["s1t"], lw["ffn1_norm"], lw["ffn1_w_in"], lw["ffn1_w_out"], f"a{l}",
        rides)
    return dh, dhb, g


def kernel(x, meta_tokens, rel_bias_table, ffn1_norm, ffn1_w_in, ffn1_w_out, mix_norm, w_in, forget_bias, fox_q_norm, fox_k_norm, swa_q_norm, swa_k_norm, swa_sinks, w_branch_fox, w_branch_swa, w_out, ffn2_norm, ffn2_w_in, ffn2_w_out, loss_target, m_meta_tokens, m_rel_bias_table, m_ffn1_norm, m_ffn1_w_in, m_ffn1_w_out, m_mix_norm, m_w_in, m_forget_bias, m_fox_q_norm, m_fox_k_norm, m_swa_q_norm, m_swa_k_norm, m_swa_sinks, m_w_branch_fox, m_w_branch_swa, m_w_out, m_ffn2_norm, m_ffn2_w_in, m_ffn2_w_out, v_meta_tokens, v_rel_bias_table, v_ffn1_norm, v_ffn1_w_in, v_ffn1_w_out, v_mix_norm, v_w_in, v_forget_bias, v_fox_q_norm, v_fox_k_norm, v_swa_q_norm, v_swa_k_norm, v_swa_sinks, v_w_branch_fox, v_w_branch_swa, v_w_out, v_ffn2_norm, v_ffn2_w_in, v_ffn2_w_out):
    names = ["meta_tokens", "rel_bias_table", "ffn1_norm", "ffn1_w_in", "ffn1_w_out", "mix_norm", "w_in", "forget_bias",
             "fox_q_norm", "fox_k_norm", "swa_q_norm", "swa_k_norm", "swa_sinks", "w_branch_fox", "w_branch_swa", "w_out",
             "ffn2_norm", "ffn2_w_in", "ffn2_w_out"]
    w = dict(zip(names, [meta_tokens, rel_bias_table, ffn1_norm, ffn1_w_in, ffn1_w_out, mix_norm, w_in, forget_bias,
                         fox_q_norm, fox_k_norm, swa_q_norm, swa_k_norm, swa_sinks, w_branch_fox, w_branch_swa, w_out,
                         ffn2_norm, ffn2_w_in, ffn2_w_out]))
    m = dict(zip(names, [m_meta_tokens, m_rel_bias_table, m_ffn1_norm, m_ffn1_w_in, m_ffn1_w_out, m_mix_norm, m_w_in,
                         m_forget_bias, m_fox_q_norm, m_fox_k_norm, m_swa_q_norm, m_swa_k_norm, m_swa_sinks,
                         m_w_branch_fox, m_w_branch_swa, m_w_out, m_ffn2_norm, m_ffn2_w_in, m_ffn2_w_out]))
    v = dict(zip(names, [v_meta_tokens, v_rel_bias_table, v_ffn1_norm, v_ffn1_w_in, v_ffn1_w_out, v_mix_norm, v_w_in,
                         v_forget_bias, v_fox_q_norm, v_fox_k_norm, v_swa_q_norm, v_swa_k_norm, v_swa_sinks,
                         v_w_branch_fox, v_w_branch_swa, v_w_out, v_ffn2_norm, v_ffn2_w_in, v_ffn2_w_out]))
    xi, yi, ci = lax.axis_index("x"), lax.axis_index("y"), lax.axis_index("c")
    shard = 2 * xi + yi
    seq = x.shape[1]
    t = seq + BLK

    wb = {nm: w[nm].astype(CDT) for nm, _, _ in SHARD_ITEMS}
    wb_list = [wb[nm] for nm, _, _ in SHARD_ITEMS]
    mflat = meta_tokens.reshape(META_ROWS, 128)
    first = (0, 1)
    *slabs_first, mall = _gather_layer([wb_list[t] for t in first], mflat, 0, "gather_weights", first)
    mall = lax.dynamic_update_slice(mall, mflat[None], (shard, 0, 0))
    meta_full = jnp.concatenate([mall[s].reshape(N_META, 256) for s in range(4)], axis=1)
    bias = _bias_fwd(rel_bias_table, "bias_fwd")

    def layer_weights(slabs, l, items=ALL_ITEMS):
        lw = _full_weights(slabs, wb, l, shard, items)
        if "w_in" in lw:
            lw["w_mix"] = _mix_cols(lw.pop("w_in"))
        return lw

    def layer_vectors(l):
        lw = {nm: w[nm][l].reshape(1, D) for nm in ("ffn1_norm", "mix_norm", "ffn2_norm")}
        lw["gfq"] = jnp.tile(fox_q_norm[l], 8).reshape(1, 512)
        lw["gfk"] = jnp.tile(fox_k_norm[l], 8).reshape(1, 512)
        lw["gsq"] = jnp.tile(swa_q_norm[l], 8).reshape(1, 512)
        lw["gsk"] = jnp.tile(swa_k_norm[l], 2).reshape(1, 128)
        lw["fb"] = jnp.pad(forget_bias[l], (0, 120)).reshape(1, 128)
        lw["sinks"] = swa_sinks[l]
        lw["bias"] = bias
        return lw

    def gather_ride(layer, items):
        return ("gather", [wb_list[t] for t in items], _slab_shapes(items), layer, items)

    landed = {}

    def need(lw, stage):
        if stage == "mixer":
            items = (2,)
            slabs = _forward_layer(landed["ffn_in_a"], "forward_halves0m", items)
        else:
            items = (3, 4, 5, 6, 7)
            slabs = _forward_layer(landed["ffn_out_a"] + landed["proj"] + landed["swa_fwd"], "forward_halves0g", items)
        lw.update(layer_weights(slabs, 0, items))

    late = {"rides": {"ffn_in_a": gather_ride(0, (2,)), "ffn_out_a": gather_ride(0, (3, 4, 5)),
                      "proj": gather_ride(0, (6,)), "swa_fwd": gather_ride(0, (7,))},
            "arrived": landed.__setitem__, "need": need}

    h = jnp.concatenate([jnp.zeros((PAD, D), F32), meta_full, x[0]], axis=0)
    lws = [{**layer_vectors(0), **layer_weights(slabs_first, 0, first)}]
    h, sv0, slabs1 = _layer_fwd(h, lws[0], 0, gather_ride(1, ALL_ITEMS), late)
    lws.append({**layer_vectors(1), **layer_weights(_forward_layer(slabs1, "forward_halves"), 1)})
    h, sv1, _ = _layer_fwd(h, lws[1], 1)
    saved = [sv0, sv1]
    dh, dhb, lacc = _loss(h, loss_target[0], "loss")
    loss = lax.psum(lacc[0, 0], ("x", "y", "c"))

    half_idx = ci.reshape(1).astype(jnp.int32)
    place_idx = jnp.stack([shard, ci]).astype(jnp.int32)

    def pair_sums(g, gsm, tag, items=ALL_ITEMS):
        if "w_mix" in g:
            g["w_in"] = _unmix_cols(g.pop("w_mix"))
        forms = _exchange_forms(g, items)
        got, slots = _swap_layer(forms, gsm, f"swap_halves{tag}", items)
        return {t: _pair_add_t(a, b, half_idx, SHARD_ITEMS[t][0], SHARD_ITEMS[t][1][0],
                               f"pair_add{tag}_{SHARD_ITEMS[t][0]}")
                for t, a, b in zip(items, forms, got)}, slots

    def scatter_ride(ps, items):
        return ("scatter", [ps[t] for t in items], _got3_shapes(items), None, items)

    early = (2, 3, 4, 5, 6, 7)
    early_rides = ((6,), (7,), (2, 5), (3, 4))
    ps0 = {}

    def before_ffn1(g):
        ps0.update(pair_sums(g, None, "0e", early)[0])
        return [scatter_ride(ps0, items) for items in early_rides]

    grads = [None, None]
    dh, dhb, grads[1] = _layer_bwd(dh, dhb, saved[1], lws[1], 1)
    ps1, _ = pair_sums(grads[1], None, 1)
    dh, dhb, grads[0] = _layer_bwd(dh, dhb, saved[0], lws[0], 0, scatter_ride(ps1, ALL_ITEMS), before_ffn1)
    grad_x = dh[BLK:].reshape(1, seq, D)
    dtab = _bias_bwd(grads[0]["dbias"] + grads[1]["dbias"], "bias_bwd")

    small = [dh[PAD:BLK].reshape(128, 128), _rows128(dtab[:, :N_BUCKETS].T, 2)]
    for nm in ("ffn1_norm", "mix_norm", "ffn2_norm"):
        small.append(jnp.stack([grads[l][nm][0] for l in range(2)]).reshape(16, 128))
    small.append(_rows128(jnp.stack([grads[l]["dgn"][4, :8] for l in range(2)]), 1))
    for row in range(4):
        small.append(jnp.stack([grads[l]["dgn"][row, :HD] for l in range(2)]).reshape(1, 128))
    dsk = [grads[l]["dsk"][:, 0, :] for l in range(2)]
    small.append(_rows128(jnp.stack([jnp.stack([d[:, 0], d[:, HD]], axis=1).reshape(8) for d in dsk]), 1))
    gsm = jnp.concatenate(small, axis=0)
    gsm = jnp.pad(gsm, ((0, SMALL_ROWS - gsm.shape[0]), (0, 0)))

    late = (0, 1)
    ps_late, slots = pair_sums(grads[0], gsm, "0l", late)
    ps0.update(ps_late)
    got3_0 = dict(zip([t for items in early_rides for t in items], grads[0]["rode_ffn1"]))
    got3_0.update(zip(late, _scatter_layer([ps0[t] for t in late], "scatter_shards", late)))
    got3 = [got3_0, dict(zip(ALL_ITEMS, grads[0]["rode"]))]
    bufs = []
    for t, (nm, (r, c), _) in enumerate(SHARD_ITEMS):
        buf = lax.empty((2, r, c), F32)
        for l, ps in ((1, ps1), (0, ps0)):
            buf = _sum4_t(ps[t], got3[l][t], buf, place_idx, l, nm, r, f"sum4_{l}_{nm}")
        bufs.append(buf)
    bufs = _join_layer(bufs, "join_halves")
    gs = _sum8(slots, "sum8")

    g_out = {nm: buf for (nm, _, _), buf in zip(SHARD_ITEMS, bufs)}
    g_out["meta_tokens"] = lax.dynamic_slice(gs[0:128].reshape(N_META, D), (0, shard * 256), (N_META, 256))
    off = 128
    for nm, rows in SMALL_ITEMS:
        n = w[nm].size
        g_out[nm] = gs[off:off + rows].reshape(-1)[:n].reshape(w[nm].shape)
        off += rows

    delta, new_m, new_v = {}, {}, {}
    for nm, _, _ in SHARD_ITEMS:
        if nm == "w_in":
            tr_ = lambda a: jnp.swapaxes(a, 1, 2)
            delta[nm], new_m[nm], new_v[nm] = (tr_(a) for a in _adamw3(tr_(w[nm]), tr_(g_out[nm]), tr_(m[nm]), tr_(v[nm]),
                                                                        f"adamw_{nm}"))
        else:
            delta[nm], new_m[nm], new_v[nm] = _adamw3(w[nm], g_out[nm], m[nm], v[nm], f"adamw_{nm}")
    small_names = ["meta_tokens"] + [nm for nm, _ in SMALL_ITEMS]
    small_rows = [META_ROWS] + [rows for _, rows in SMALL_ITEMS]

    def pack_small(src):
        buf = jnp.concatenate([_rows128(src[nm], rows) for nm, rows in zip(small_names, small_rows)], axis=0)
        return jnp.pad(buf, ((0, SMALL_ADAM_ROWS - buf.shape[0]), (0, 0)))

    d_, m_, v_ = (a[0] for a in _adamw3(pack_small(w)[None], pack_small(g_out)[None], pack_small(m)[None],
                                        pack_small(v)[None], "adamw_small"))
    off = 0
    for nm, rows in zip(small_names, small_rows):
        n = w[nm].size
        for dst, src in ((delta, d_), (new_m, m_), (new_v, v_)):
            dst[nm] = src[off:off + rows].reshape(-1)[:n].reshape(w[nm].shape)
        off += rows

    return (loss, grad_x, *[g_out[n] for n in names], *[delta[n] for n in names],
            *[new_m[n] for n in names], *[new_v[n] for n in names])
```
